```python
import jax, jax.numpy as jnp
from jax import lax
import numpy as np

D_MODEL = 2048
BATCH = 8
SEQ = 2048
DEPTH = 1

HEAD_DIM = 128
N_HEADS_FOX = 8
N_HEADS_DIL = 8
DIL_PATTERNS = ((128, 1), (512, 4), (2048, 16))
Q_BLOCK = 128
D_FF = 5632
CONV_WIDTH = 3
EPS = 1e-6
NEG_INF = -1e30
W_FOX = N_HEADS_FOX * HEAD_DIM
W_DIL = N_HEADS_DIL * HEAD_DIM
IN_SPLITS = (W_FOX, W_FOX, W_FOX, N_HEADS_FOX, W_DIL, W_DIL, W_DIL, D_MODEL, D_MODEL)
IN_COLS = sum(IN_SPLITS)

kernel_name = "hybrid_fox_dilated_convffn"


def rms_norm(x, g):
    xf = x.astype(jnp.float32)
    y = xf * lax.rsqrt(jnp.mean(xf * xf, axis=-1, keepdims=True) + EPS)
    return (y * g.astype(jnp.float32)).astype(x.dtype)


def alibi_slopes(n_heads):
    return jnp.asarray(2.0 ** (-8.0 * np.arange(1, n_heads + 1) / n_heads), dtype=jnp.float32)


def split_heads(a, n_heads):
    b, s, _ = a.shape
    return a.reshape(b, s, n_heads, HEAD_DIM).transpose(0, 2, 1, 3)


def merge_heads(a):
    b, h, s, d = a.shape
    return a.transpose(0, 2, 1, 3).reshape(b, s, h * d)


def fox_attention(q, k, v, log_f):
    b, h, s, d = q.shape
    nb = s // Q_BLOCK
    scale = 1.0 / np.sqrt(d)
    c = jnp.cumsum(log_f, axis=-1)
    qb = q.reshape(b, h, nb, Q_BLOCK, d).transpose(2, 0, 1, 3, 4)
    cb = c.reshape(b, h, nb, Q_BLOCK).transpose(2, 0, 1, 3)
    kpos = jnp.arange(s)

    def block(args):
        n, qn, cn = args
        sc = jnp.einsum('bhqd,bhkd->bhqk', qn, k) * scale
        sc = sc + cn[..., :, None] - c[:, :, None, :]
        qpos = n * Q_BLOCK + jnp.arange(Q_BLOCK)
        sc = jnp.where(kpos[None, :] <= qpos[:, None], sc, NEG_INF)
        p = jax.nn.softmax(sc, axis=-1)
        return jnp.einsum('bhqk,bhkd->bhqd', p, v)

    o = lax.map(block, (jnp.arange(nb), qb, cb))
    return o.transpose(1, 2, 0, 3, 4).reshape(b, h, s, d)


def dilated_pattern(q, k, v, slopes, window, dil):
    b, h, s, d = q.shape
    steps = window // dil
    L = s // dil
    nb = -(-L // steps)
    Lp = nb * steps
    scale = 1.0 / np.sqrt(d)

    def to_sub(a):
        return a.reshape(b, h, L, dil, d).transpose(0, 1, 3, 2, 4)

    qs = jnp.pad(to_sub(q), ((0, 0), (0, 0), (0, 0), (0, Lp - L), (0, 0)))
    qb = qs.reshape(b, h, dil, nb, steps, d)

    def band(a):
        ap = jnp.pad(to_sub(a), ((0, 0), (0, 0), (0, 0), (steps, Lp - L), (0, 0)))
        prev = ap[:, :, :, :Lp].reshape(b, h, dil, nb, steps, d)
        cur = ap[:, :, :, steps:].reshape(b, h, dil, nb, steps, d)
        return jnp.concatenate([prev, cur], axis=4)

    kb, vb = band(k), band(v)
    sc = jnp.einsum('bhrnqd,bhrnkd->bhrnqk', qb, kb) * scale
    i = jnp.arange(steps)[:, None]
    j = jnp.arange(2 * steps)[None, :]
    dist = i + steps - j
    ksub = jnp.arange(nb)[:, None, None] * steps + j[None] - steps
    valid = (dist >= 0) & (dist <= steps) & (ksub >= 0)
    penalty = slopes[None, :, None, None, None, None] * (dil * dist).astype(jnp.float32)
    sc = jnp.where(valid, sc - penalty, NEG_INF)
    m = jnp.max(sc, axis=-1, keepdims=True)
    e = jnp.exp(sc - m)
    den = jnp.sum(e, axis=-1)
    o = jnp.einsum('bhrnqk,bhrnkd->bhrnqd', e, vb) / den[..., None]
    lse = m[..., 0] + jnp.log(den)

    def from_sub(a, tail):
        a = a.reshape((b, h, dil, Lp) + tail)[:, :, :, :L]
        perm = (0, 1, 3, 2) + tuple(range(4, 4 + len(tail)))
        return a.transpose(perm).reshape((b, h, s) + tail)

    return from_sub(o, (d,)), from_sub(lse, ())


def dilated_attention(q, k, v, slopes):
    outs, lses = [], []
    for window, dil in DIL_PATTERNS:
        o, l = dilated_pattern(q, k, v, slopes, window, dil)
        outs.append(o)
        lses.append(l)
    o = jnp.stack(outs, axis=0)
    w = jax.nn.softmax(jnp.stack(lses, axis=0), axis=0)
    return jnp.sum(w[..., None] * o, axis=0)


def qk_norm(a, g):
    af = a.astype(jnp.float32)
    af = af * lax.rsqrt(jnp.mean(af * af, axis=-1, keepdims=True) + EPS)
    return af * g.astype(jnp.float32)[None, :, None, :]


def _fwd_setup_inputs(seed: int = 0) -> dict:
    key = jax.random.key(seed)
    ks = jax.random.split(key, 20)
    f32 = jnp.float32

    def nrm(k, shape, scale):
        return jax.random.normal(k, shape, f32) * scale

    return {
        "x": jax.random.normal(ks[0], (BATCH, SEQ, D_MODEL), f32),
        "g_attn": 1.0 + nrm(ks[1], (DEPTH, D_MODEL), 0.02),
        "w_in": nrm(ks[2], (DEPTH, D_MODEL, IN_COLS), D_MODEL ** -0.5),
        "b_forget": 3.0 + nrm(ks[3], (DEPTH, N_HEADS_FOX), 0.5),
        "g_q_fox": 1.0 + nrm(ks[4], (DEPTH, N_HEADS_FOX, HEAD_DIM), 0.02),
        "g_k_fox": 1.0 + nrm(ks[5], (DEPTH, N_HEADS_FOX, HEAD_DIM), 0.02),
        "g_q_dil": 1.0 + nrm(ks[6], (DEPTH, N_HEADS_DIL, HEAD_DIM), 0.02),
        "g_k_dil": 1.0 + nrm(ks[7], (DEPTH, N_HEADS_DIL, HEAD_DIM), 0.02),
        "w_br_fox": nrm(ks[8], (DEPTH, W_FOX, D_MODEL), W_FOX ** -0.5),
        "w_br_dil": nrm(ks[9], (DEPTH, W_DIL, D_MODEL), W_DIL ** -0.5),
        "w_out": nrm(ks[10], (DEPTH, D_MODEL, D_MODEL), D_MODEL ** -0.5),
        "g_ffn": 1.0 + nrm(ks[11], (DEPTH, D_MODEL), 0.02),
        "w_up": nrm(ks[12], (DEPTH, D_MODEL, 2 * D_FF), D_MODEL ** -0.5),
        "w_conv": nrm(ks[13], (DEPTH, CONV_WIDTH, 2 * D_FF), CONV_WIDTH ** -0.5),
        "b_conv": nrm(ks[14], (DEPTH, 2 * D_FF), 0.02),
        "w_down": nrm(ks[15], (DEPTH, D_FF, D_MODEL), D_FF ** -0.5),
    }


def _fwd_reference(x, g_attn, w_in, b_forget, g_q_fox, g_k_fox, g_q_dil, g_k_dil,
              w_br_fox, w_br_dil, w_out, g_ffn, w_up, w_conv, b_conv, w_down):
    b, s, _ = x.shape
    offs = np.cumsum(IN_SPLITS)[:-1].tolist()
    slopes = alibi_slopes(N_HEADS_DIL)
    for l in range(DEPTH):
        h = rms_norm(x, g_attn[l])
        proj = h @ w_in[l]
        qa, ka, va, fa, qb, kb, vb, ga, gb = jnp.split(proj, offs, axis=-1)
        q_a = qk_norm(split_heads(qa, N_HEADS_FOX), g_q_fox[l])
        k_a = qk_norm(split_heads(ka, N_HEADS_FOX), g_k_fox[l])
        v_a = split_heads(va, N_HEADS_FOX).astype(jnp.float32)
        log_f = jax.nn.log_sigmoid((fa + b_forget[l]).astype(jnp.float32)).transpose(0, 2, 1)
        o_a = merge_heads(fox_attention(q_a, k_a, v_a, log_f)).astype(x.dtype)
        q_b = qk_norm(split_heads(qb, N_HEADS_DIL), g_q_dil[l])
        k_b = qk_norm(split_heads(kb, N_HEADS_DIL), g_k_dil[l])
        v_b = split_heads(vb, N_HEADS_DIL).astype(jnp.float32)
        o_b = merge_heads(dilated_attention(q_b, k_b, v_b, slopes)).astype(x.dtype)
        merged = jax.nn.sigmoid(ga) * (o_a @ w_br_fox[l]) + jax.nn.sigmoid(gb) * (o_b @ w_br_dil[l])
        x = x + merged @ w_out[l]
        h = rms_norm(x, g_ffn[l])
        u = h @ w_up[l]
        up = jnp.pad(u, ((0, 0), (CONV_WIDTH - 1, 0), (0, 0)))
        wc = w_conv[l]
        conv = sum(wc[t] * up[:, t:t + s] for t in range(CONV_WIDTH)) + b_conv[l]
        gate, val = jnp.split(conv, 2, axis=-1)
        x = x + (jax.nn.silu(gate) * val) @ w_down[l]
    return x


import jax as _jax
import jax.numpy as _jnp

TWIN_FORMAT = 'train_step'
FWD_PARAMS = ['x', 'g_attn', 'w_in', 'b_forget', 'g_q_fox', 'g_k_fox', 'g_q_dil', 'g_k_dil', 'w_br_fox', 'w_br_dil', 'w_out', 'g_ffn', 'w_up', 'w_conv', 'b_conv', 'w_down']
TWIN_WEIGHTS = ['g_attn', 'w_in', 'b_forget', 'g_q_fox', 'g_k_fox', 'g_q_dil', 'g_k_dil', 'w_br_fox', 'w_br_dil', 'w_out', 'g_ffn', 'w_up', 'w_conv', 'b_conv', 'w_down']
TWIN_DIFF_INPUT = 'x'
TWIN_INPUTS = ['x', 'g_attn', 'w_in', 'b_forget', 'g_q_fox', 'g_k_fox', 'g_q_dil', 'g_k_dil', 'w_br_fox', 'w_br_dil', 'w_out', 'g_ffn', 'w_up', 'w_conv', 'b_conv', 'w_down', 'loss_target', 'm_g_attn', 'm_w_in', 'm_b_forget', 'm_g_q_fox', 'm_g_k_fox', 'm_g_q_dil', 'm_g_k_dil', 'm_w_br_fox', 'm_w_br_dil', 'm_w_out', 'm_g_ffn', 'm_w_up', 'm_w_conv', 'm_b_conv', 'm_w_down', 'v_g_attn', 'v_w_in', 'v_b_forget', 'v_g_q_fox', 'v_g_k_fox', 'v_g_q_dil', 'v_g_k_dil', 'v_w_br_fox', 'v_w_br_dil', 'v_w_out', 'v_g_ffn', 'v_w_up', 'v_w_conv', 'v_b_conv', 'v_w_down']
TWIN_OUTPUTS = ['loss', 'grad_x', 'grad_g_attn', 'grad_w_in', 'grad_b_forget', 'grad_g_q_fox', 'grad_g_k_fox', 'grad_g_q_dil', 'grad_g_k_dil', 'grad_w_br_fox', 'grad_w_br_dil', 'grad_w_out', 'grad_g_ffn', 'grad_w_up', 'grad_w_conv', 'grad_b_conv', 'grad_w_down', 'delta_g_attn', 'delta_w_in', 'delta_b_forget', 'delta_g_q_fox', 'delta_g_k_fox', 'delta_g_q_dil', 'delta_g_k_dil', 'delta_w_br_fox', 'delta_w_br_dil', 'delta_w_out', 'delta_g_ffn', 'delta_w_up', 'delta_w_conv', 'delta_b_conv', 'delta_w_down', 'new_m_g_attn', 'new_m_w_in', 'new_m_b_forget', 'new_m_g_q_fox', 'new_m_g_k_fox', 'new_m_g_q_dil', 'new_m_g_k_dil', 'new_m_w_br_fox', 'new_m_w_br_dil', 'new_m_w_out', 'new_m_g_ffn', 'new_m_w_up', 'new_m_w_conv', 'new_m_b_conv', 'new_m_w_down', 'new_v_g_attn', 'new_v_w_in', 'new_v_b_forget', 'new_v_g_q_fox', 'new_v_g_k_fox', 'new_v_g_q_dil', 'new_v_g_k_dil', 'new_v_w_br_fox', 'new_v_w_br_dil', 'new_v_w_out', 'new_v_g_ffn', 'new_v_w_up', 'new_v_w_conv', 'new_v_b_conv', 'new_v_w_down']
TWIN_LEAF_KINDS = {'loss': 'loss', 'grad_x': 'grad_x', 'grad_g_attn': 'grad_w', 'grad_w_in': 'grad_w', 'grad_b_forget': 'grad_w', 'grad_g_q_fox': 'grad_w', 'grad_g_k_fox': 'grad_w', 'grad_g_q_dil': 'grad_w', 'grad_g_k_dil': 'grad_w', 'grad_w_br_fox': 'grad_w', 'grad_w_br_dil': 'grad_w', 'grad_w_out': 'grad_w', 'grad_g_ffn': 'grad_w', 'grad_w_up': 'grad_w', 'grad_w_conv': 'grad_w', 'grad_b_conv': 'grad_w', 'grad_w_down': 'grad_w', 'delta_g_attn': 'delta_w', 'delta_w_in': 'delta_w', 'delta_b_forget': 'delta_w', 'delta_g_q_fox': 'delta_w', 'delta_g_k_fox': 'delta_w', 'delta_g_q_dil': 'delta_w', 'delta_g_k_dil': 'delta_w', 'delta_w_br_fox': 'delta_w', 'delta_w_br_dil': 'delta_w', 'delta_w_out': 'delta_w', 'delta_g_ffn': 'delta_w', 'delta_w_up': 'delta_w', 'delta_w_conv': 'delta_w', 'delta_b_conv': 'delta_w', 'delta_w_down': 'delta_w', 'new_m_g_attn': 'new_m', 'new_m_w_in': 'new_m', 'new_m_b_forget': 'new_m', 'new_m_g_q_fox': 'new_m', 'new_m_g_k_fox': 'new_m', 'new_m_g_q_dil': 'new_m', 'new_m_g_k_dil': 'new_m', 'new_m_w_br_fox': 'new_m', 'new_m_w_br_dil': 'new_m', 'new_m_w_out': 'new_m', 'new_m_g_ffn': 'new_m', 'new_m_w_up': 'new_m', 'new_m_w_conv': 'new_m', 'new_m_b_conv': 'new_m', 'new_m_w_down': 'new_m', 'new_v_g_attn': 'new_v', 'new_v_w_in': 'new_v', 'new_v_b_forget': 'new_v', 'new_v_g_q_fox': 'new_v', 'new_v_g_k_fox': 'new_v', 'new_v_g_q_dil': 'new_v', 'new_v_g_k_dil': 'new_v', 'new_v_w_br_fox': 'new_v', 'new_v_w_br_dil': 'new_v', 'new_v_w_out': 'new_v', 'new_v_g_ffn': 'new_v', 'new_v_w_up': 'new_v', 'new_v_w_conv': 'new_v', 'new_v_b_conv': 'new_v', 'new_v_w_down': 'new_v'}


def _forward(args):
    return _fwd_reference(*[args[k] for k in FWD_PARAMS])


def _output_shape():
    out = _jax.eval_shape(lambda: _forward(_fwd_setup_inputs(0)))
    return out.shape, out.dtype

N_MICROBATCH = 1
ADAM_LR = 0.001
ADAM_B1 = 0.9
ADAM_B2 = 0.999
ADAM_EPS = 1e-08
ADAM_WD = 0.01
ADAM_STEP = 10
PER_EXAMPLE_BATCH_AXIS = {'x': 0, 'loss_target': 0}
SHARED_INPUTS = []
_WEIGHT_DTYPES = {'g_attn': _jnp.float32, 'w_in': _jnp.float32, 'b_forget': _jnp.float32, 'g_q_fox': _jnp.float32, 'g_k_fox': _jnp.float32, 'g_q_dil': _jnp.float32, 'g_k_dil': _jnp.float32, 'w_br_fox': _jnp.float32, 'w_br_dil': _jnp.float32, 'w_out': _jnp.float32, 'g_ffn': _jnp.float32, 'w_up': _jnp.float32, 'w_conv': _jnp.float32, 'b_conv': _jnp.float32, 'w_down': _jnp.float32}
MOMENT_SCALE = {'g_attn': 5.741174e-01, 'w_in': 3.530927e-02, 'b_forget': 1.855637e+01, 'g_q_fox': 2.051314e-01, 'g_k_fox': 2.047869e-01, 'g_q_dil': 1.370556e-01, 'g_k_dil': 1.370206e-01, 'w_br_fox': 3.091248e-02, 'w_br_dil': 3.935101e-02, 'w_out': 4.799377e-02, 'g_ffn': 6.607611e+00, 'w_up': 6.600994e-02, 'w_conv': 9.083889e-01, 'b_conv': 8.041151e-01, 'w_down': 8.012336e-02}


def _to_microbatches(a, axis):
    t = _jnp.moveaxis(a, axis, 0)
    t = t.reshape((N_MICROBATCH, t.shape[0] // N_MICROBATCH) + t.shape[1:])
    return _jnp.moveaxis(t, 1, axis + 1)


def setup_inputs(seed: int = 0) -> dict:
    inp = _fwd_setup_inputs(seed)
    key = _jax.random.fold_in(_jax.random.key(seed), 7919)
    shape, _ = _output_shape()
    out = dict(inp)
    out["loss_target"] = _jax.random.normal(_jax.random.fold_in(key, 0), shape, _jnp.float32)
    for i, name in enumerate(TWIN_WEIGHTS):
        w = inp[name].astype(_jnp.float32)
        if MOMENT_SCALE is None:
            s = _jnp.sqrt(_jnp.mean(_jnp.square(w)) + 1e-30)
        else:
            s = MOMENT_SCALE[name]
        km, kv = _jax.random.split(_jax.random.fold_in(key, i + 1))
        out[name] = w
        out["m_" + name] = s * _jax.random.normal(km, w.shape, _jnp.float32)
        out["v_" + name] = (s * s) * _jax.random.uniform(kv, w.shape, _jnp.float32, 0.5, 1.5)
    if N_MICROBATCH > 1:
        for name, axis in PER_EXAMPLE_BATCH_AXIS.items():
            out[name] = _to_microbatches(out[name], axis)
    return {'x': out['x'], 'g_attn': out['g_attn'], 'w_in': out['w_in'], 'b_forget': out['b_forget'], 'g_q_fox': out['g_q_fox'], 'g_k_fox': out['g_k_fox'], 'g_q_dil': out['g_q_dil'], 'g_k_dil': out['g_k_dil'], 'w_br_fox': out['w_br_fox'], 'w_br_dil': out['w_br_dil'], 'w_out': out['w_out'], 'g_ffn': out['g_ffn'], 'w_up': out['w_up'], 'w_conv': out['w_conv'], 'b_conv': out['b_conv'], 'w_down': out['w_down'], 'loss_target': out['loss_target'], 'm_g_attn': out['m_g_attn'], 'm_w_in': out['m_w_in'], 'm_b_forget': out['m_b_forget'], 'm_g_q_fox': out['m_g_q_fox'], 'm_g_k_fox': out['m_g_k_fox'], 'm_g_q_dil': out['m_g_q_dil'], 'm_g_k_dil': out['m_g_k_dil'], 'm_w_br_fox': out['m_w_br_fox'], 'm_w_br_dil': out['m_w_br_dil'], 'm_w_out': out['m_w_out'], 'm_g_ffn': out['m_g_ffn'], 'm_w_up': out['m_w_up'], 'm_w_conv': out['m_w_conv'], 'm_b_conv': out['m_b_conv'], 'm_w_down': out['m_w_down'], 'v_g_attn': out['v_g_attn'], 'v_w_in': out['v_w_in'], 'v_b_forget': out['v_b_forget'], 'v_g_q_fox': out['v_g_q_fox'], 'v_g_k_fox': out['v_g_k_fox'], 'v_g_q_dil': out['v_g_q_dil'], 'v_g_k_dil': out['v_g_k_dil'], 'v_w_br_fox': out['v_w_br_fox'], 'v_w_br_dil': out['v_w_br_dil'], 'v_w_out': out['v_w_out'], 'v_g_ffn': out['v_g_ffn'], 'v_w_up': out['v_w_up'], 'v_w_conv': out['v_w_conv'], 'v_b_conv': out['v_b_conv'], 'v_w_down': out['v_w_down']}


def _loss(weights, diff, rest, loss_target):
    with _jax.named_scope("forward"):
        args = {**rest, TWIN_DIFF_INPUT: diff, **{k: w.astype(_WEIGHT_DTYPES[k]) for k, w in weights.items()}}
        y = _forward(args)
    with _jax.named_scope("loss_head"):
        err = _jnp.square(y.astype(_jnp.float32) - loss_target)
        return 0.5 * _jnp.sum(_jnp.mean(err, axis=-1)) if err.ndim else 0.5 * err


def _adamw(w, g, m, v):
    m = ADAM_B1 * m + (1.0 - ADAM_B1) * g
    v = ADAM_B2 * v + (1.0 - ADAM_B2) * _jnp.square(g)
    m_hat = m / (1.0 - ADAM_B1 ** ADAM_STEP)
    v_hat = v / (1.0 - ADAM_B2 ** ADAM_STEP)
    delta = -ADAM_LR * (m_hat / (_jnp.sqrt(v_hat) + ADAM_EPS) + ADAM_WD * w)
    return delta, m, v


def reference(x, g_attn, w_in, b_forget, g_q_fox, g_k_fox, g_q_dil, g_k_dil, w_br_fox, w_br_dil, w_out, g_ffn, w_up, w_conv, b_conv, w_down, loss_target, m_g_attn, m_w_in, m_b_forget, m_g_q_fox, m_g_k_fox, m_g_q_dil, m_g_k_dil, m_w_br_fox, m_w_br_dil, m_w_out, m_g_ffn, m_w_up, m_w_conv, m_b_conv, m_w_down, v_g_attn, v_w_in, v_b_forget, v_g_q_fox, v_g_k_fox, v_g_q_dil, v_g_k_dil, v_w_br_fox, v_w_br_dil, v_w_out, v_g_ffn, v_w_up, v_w_conv, v_b_conv, v_w_down):
    given = dict(x=x, g_attn=g_attn, w_in=w_in, b_forget=b_forget, g_q_fox=g_q_fox, g_k_fox=g_k_fox, g_q_dil=g_q_dil, g_k_dil=g_k_dil, w_br_fox=w_br_fox, w_br_dil=w_br_dil, w_out=w_out, g_ffn=g_ffn, w_up=w_up, w_conv=w_conv, b_conv=b_conv, w_down=w_down, loss_target=loss_target, m_g_attn=m_g_attn, m_w_in=m_w_in, m_b_forget=m_b_forget, m_g_q_fox=m_g_q_fox, m_g_k_fox=m_g_k_fox, m_g_q_dil=m_g_q_dil, m_g_k_dil=m_g_k_dil, m_w_br_fox=m_w_br_fox, m_w_br_dil=m_w_br_dil, m_w_out=m_w_out, m_g_ffn=m_g_ffn, m_w_up=m_w_up, m_w_conv=m_w_conv, m_b_conv=m_b_conv, m_w_down=m_w_down, v_g_attn=v_g_attn, v_w_in=v_w_in, v_b_forget=v_b_forget, v_g_q_fox=v_g_q_fox, v_g_k_fox=v_g_k_fox, v_g_q_dil=v_g_q_dil, v_g_k_dil=v_g_k_dil, v_w_br_fox=v_w_br_fox, v_w_br_dil=v_w_br_dil, v_w_out=v_w_out, v_g_ffn=v_g_ffn, v_w_up=v_w_up, v_w_conv=v_w_conv, v_b_conv=v_b_conv, v_w_down=v_w_down)
    weights = {n: given[n] for n in TWIN_WEIGHTS}
    shared = {n: given[n] for n in SHARED_INPUTS}
    per_example = {n: given[n] for n in ['x']}
    grad_fn = _jax.value_and_grad(_loss, argnums=(0, 1))

    def one_microbatch(ex, loss_target):
        ex = dict(ex)
        diff = ex.pop(TWIN_DIFF_INPUT)
        return grad_fn(weights, diff, {**shared, **ex}, loss_target)

    if N_MICROBATCH == 1:
        loss, (grad_w, grad_x) = one_microbatch(per_example, given["loss_target"])
    else:
        def body(carry, xs):
            loss_sum, grad_sum = carry
            l_k, (gw_k, gx_k) = one_microbatch(xs[0], xs[1])
            with _jax.named_scope("update"):
                return (loss_sum + l_k, _jax.tree.map(_jnp.add, grad_sum, gw_k)), gx_k

        init = (_jnp.zeros((), _jnp.float32), _jax.tree.map(_jnp.zeros_like, weights))
        (loss, grad_w), grad_x = _jax.lax.scan(body, init, (per_example, given["loss_target"]))
    with _jax.named_scope("update"):
        delta_w, new_m, new_v = {}, {}, {}
        for n in TWIN_WEIGHTS:
            delta_w[n], new_m[n], new_v[n] = _adamw(weights[n], grad_w[n], given["m_" + n], given["v_" + n])
    return (loss, grad_x, *[grad_w[n] for n in TWIN_WEIGHTS], *[delta_w[n] for n in TWIN_WEIGHTS],
            *[new_m[n] for n in TWIN_WEIGHTS], *[new_v[n] for n in TWIN_WEIGHTS])
```

```python
import functools
import math

import jax
import jax.numpy as jnp
import numpy as np
from jax import lax
from jax.experimental import pallas as pl
from jax.experimental.pallas import tpu as pltpu

F32 = jnp.float32
BF16 = jnp.bfloat16
HEAD_DIM = 128
N_HEADS = 8
EPS = 1e-6
NEG = -1e30
N_CHIPS = 4
N_DEV = 8
LANES = 128
VMEM_LIMIT_BYTES = 56 * 1024 * 1024
DIL_PATTERNS = ((128, 1), (512, 4), (2048, 16))
ADAM_LR, ADAM_B1, ADAM_B2, ADAM_EPS, ADAM_WD, ADAM_STEP = 0.001, 0.9, 0.999, 1e-08, 0.01, 10
MESH = pl.DeviceIdType.MESH


def _params(*sem):
    return pltpu.CompilerParams(dimension_semantics=sem, vmem_limit_bytes=VMEM_LIMIT_BYTES)


def _round_up(n, m):
    return -(-n // m) * m


def _pick(dim, prefs):
    for p in prefs:
        if dim % p == 0:
            return p
    raise ValueError(f"no tile for {dim} in {prefs}")


def _logical_shape(arr, kind):
    if kind is None:
        return arr.shape
    s, r, c = arr.shape
    return (r, s * c) if kind == "col" else (s * r, c)


def _spec(shape, kind, br, bc, fi, fj):
    if kind is None:
        return pl.BlockSpec((br, bc), lambda *g: (fi(*g), fj(*g)))
    _, r, c = shape
    if kind == "col":
        nb = c // bc
        assert nb * bc == c, (shape, bc)
        return pl.BlockSpec((None, br, bc), lambda *g: (fj(*g) // nb, fi(*g), fj(*g) % nb))
    nb = r // br
    assert nb * br == r, (shape, br)
    return pl.BlockSpec((None, br, bc), lambda *g: (fi(*g) // nb, fi(*g) % nb, fj(*g)))


def _mm(a, b, *, mode, tm, tn, tk, name, a_kind=None, b_kind=None, out_kind=None,
        out_dtype=F32, res=None):
    la, lb = _logical_shape(a, a_kind), _logical_shape(b, b_kind)
    if mode == "nn":
        (m, k), (k2, n) = la, lb
    elif mode == "nt":
        (m, k), (n, k2) = la, lb
    else:
        (k, m), (k2, n) = la, lb
    assert k == k2, (name, la, lb)
    assert m % tm == 0 and n % tn == 0 and k % tk == 0, (name, m, n, k, tm, tn, tk)
    nk = k // tk
    im = lambda i, j, l: i
    jn = lambda i, j, l: j
    lk = lambda i, j, l: l
    if mode == "tn":
        a_spec = _spec(a.shape, a_kind, tk, tm, lk, im)
        dims = (((0,), (0,)), ((), ()))
    else:
        a_spec = _spec(a.shape, a_kind, tm, tk, im, lk)
        dims = (((1,), (1,)), ((), ())) if mode == "nt" else (((1,), (0,)), ((), ()))
    if mode == "nt":
        b_spec = _spec(b.shape, b_kind, tn, tk, jn, lk)
    else:
        b_spec = _spec(b.shape, b_kind, tk, tn, lk, jn)
    if out_kind is None:
        oshape = (m, n)
    elif out_kind == "col":
        oshape = (N_CHIPS, m, n // N_CHIPS)
    else:
        oshape = (N_CHIPS, m // N_CHIPS, n)
    o_spec = _spec(oshape, out_kind, tm, tn, im, jn)
    in_specs = [a_spec, b_spec]
    args = [a, b]
    if res is not None:
        in_specs.append(pl.BlockSpec((tm, tn), lambda i, j, l: (i, j)))
        args.append(res)

    def body(*refs):
        a_ref, b_ref = refs[0], refs[1]
        res_ref = refs[2] if res is not None else None
        o_ref, acc_ref = refs[-2], refs[-1]
        step = pl.program_id(2)

        @pl.when(step == 0)
        def _():
            acc_ref[...] = jnp.zeros_like(acc_ref)

        acc_ref[...] += lax.dot_general(a_ref[...], b_ref[...], dims, preferred_element_type=F32)

        @pl.when(step == nk - 1)
        def _():
            out = acc_ref[...]
            if res_ref is not None:
                out = out + res_ref[...]
            o_ref[...] = out.astype(o_ref.dtype)

    return pl.pallas_call(
        body, name=name, grid=(m // tm, n // tn, nk),
        in_specs=in_specs, out_specs=o_spec,
        out_shape=jax.ShapeDtypeStruct(oshape, out_dtype),
        scratch_shapes=[pltpu.VMEM((tm, tn), F32)],
        compiler_params=_params("parallel", "parallel", "arbitrary"),
    )(*args)


def _norm_fwd(x, g, *, group, name, tm=256):
    s, w = x.shape
    ng = w // group

    def body(x_ref, g_ref, o_ref):
        for i in range(ng):
            cols = slice(i * group, (i + 1) * group)
            xv = x_ref[:, cols]
            r = lax.rsqrt(jnp.mean(xv * xv, axis=-1, keepdims=True) + EPS)
            o_ref[:, cols] = ((xv * r) * g_ref[:, cols]).astype(o_ref.dtype)

    return pl.pallas_call(
        body, name=name, grid=(s // tm,),
        in_specs=[pl.BlockSpec((tm, w), lambda i: (i, 0)), pl.BlockSpec((1, w), lambda i: (0, 0))],
        out_specs=pl.BlockSpec((tm, w), lambda i: (i, 0)),
        out_shape=jax.ShapeDtypeStruct((s, w), BF16),
        compiler_params=_params("parallel"),
    )(x, g)


def _norm_bwd(dy, x, g, *, group, name, res=None, out_dtypes=(BF16,), tm=256):
    s, w = x.shape
    ng = w // group
    n_in = 4 if res is not None else 3

    def body(*refs):
        dy_ref, x_ref, g_ref = refs[:3]
        res_ref = refs[3] if res is not None else None
        outs = refs[n_in:]
        dx_refs, dg_ref = outs[:-1], outs[-1]

        @pl.when(pl.program_id(0) == 0)
        def _():
            dg_ref[...] = jnp.zeros_like(dg_ref)

        for i in range(ng):
            cols = slice(i * group, (i + 1) * group)
            xv = x_ref[:, cols]
            dyv = dy_ref[:, cols].astype(F32)
            r = lax.rsqrt(jnp.mean(xv * xv, axis=-1, keepdims=True) + EPS)
            xr = xv * r
            dg_ref[:, cols] += jnp.sum(dyv * xr, axis=0, keepdims=True)
            gdy = dyv * g_ref[:, cols]
            dx = r * (gdy - xr * jnp.mean(gdy * xr, axis=-1, keepdims=True))
            if res_ref is not None:
                dx = dx + res_ref[:, cols]
            for dx_ref in dx_refs:
                dx_ref[:, cols] = dx.astype(dx_ref.dtype)

    row = pl.BlockSpec((tm, w), lambda i: (i, 0))
    vec = pl.BlockSpec((1, w), lambda i: (0, 0))
    in_specs = [row, row, vec] + ([row] if res is not None else [])
    args = [dy, x, g] + ([res] if res is not None else [])
    out_specs = [row] * len(out_dtypes) + [vec]
    out_shape = [jax.ShapeDtypeStruct((s, w), dt) for dt in out_dtypes] + [jax.ShapeDtypeStruct((1, w), F32)]
    return pl.pallas_call(
        body, name=name, grid=(s // tm,), in_specs=in_specs, out_specs=out_specs,
        out_shape=out_shape, compiler_params=_params("arbitrary"),
    )(*args)


def _split3(v):
    p1 = v.astype(BF16)
    r1 = v - p1.astype(F32)
    p2 = r1.astype(BF16)
    p3 = (r1 - p2.astype(F32)).astype(BF16)
    return p1, p2, p3


def _tri_sum(v, reverse, tcol=512):
    h, s = v.shape
    tcol = min(tcol, s)
    parts = _split3(v)
    outs = []
    for j in range(s // tcol):
        src = lax.broadcasted_iota(jnp.int32, (s, tcol), 0)
        dst = lax.broadcasted_iota(jnp.int32, (s, tcol), 1) + j * tcol
        keep = (src >= dst) if reverse else (src <= dst)
        tri = jnp.where(keep, 1.0, 0.0).astype(BF16)
        acc = jnp.zeros((h, tcol), F32)
        for p in parts:
            acc = acc + jnp.dot(p, tri, preferred_element_type=F32)
        outs.append(acc)
    return outs


def _forget_fwd(fa_t, b):
    h, s = fa_t.shape
    tcol = min(512, s)

    def body(f_ref, b_ref, c_ref):
        z = f_ref[...] + b_ref[...]
        logf = jnp.minimum(z, 0.0) - jnp.log(1.0 + jnp.exp(-jnp.abs(z)))
        for j, blk in enumerate(_tri_sum(logf, reverse=False, tcol=tcol)):
            c_ref[:, j * tcol:(j + 1) * tcol] = blk

    return pl.pallas_call(
        body, name="forget_fwd", out_shape=jax.ShapeDtypeStruct((h, s), F32),
        compiler_params=_params(),
    )(fa_t, b)


def _forget_bwd(dacol, fa_t, b):
    h, s = fa_t.shape
    tcol = min(512, s)

    def body(d_ref, f_ref, b_ref, dfa_ref, db_ref):
        z = f_ref[...] + b_ref[...]
        dc = -d_ref[...]
        total = jnp.zeros((h, 1), F32)
        for j, blk in enumerate(_tri_sum(dc, reverse=True, tcol=tcol)):
            cols = slice(j * tcol, (j + 1) * tcol)
            dfa = blk * (1.0 - jax.nn.sigmoid(z[:, cols]))
            dfa_ref[:, cols] = dfa
            total = total + jnp.sum(dfa, axis=-1, keepdims=True)
        db_ref[...] = total

    return pl.pallas_call(
        body, name="forget_bwd",
        out_shape=[jax.ShapeDtypeStruct((h, s), F32), jax.ShapeDtypeStruct((h, 1), F32)],
        compiler_params=_params(),
    )(dacol, fa_t, b)


def _logits(q, k, arow, acol, q0, k0, dilated):
    tq, tk = q.shape[0], k.shape[0]
    s = lax.dot_general(q, k, (((1,), (1,)), ((), ())), preferred_element_type=F32)
    s = s * (1.0 / math.sqrt(HEAD_DIM)) + arow - acol
    dist = (q0 + lax.broadcasted_iota(jnp.int32, (tq, tk), 0)) - (k0 + lax.broadcasted_iota(jnp.int32, (tq, tk), 1))
    valid = dist >= 0
    if dilated:
        mult = jnp.zeros((tq, tk), jnp.int32)
        for window, dil in DIL_PATTERNS:
            mult = mult + ((dist <= window) & ((dist & (dil - 1)) == 0)).astype(jnp.int32)
        s = s + jnp.where(mult == 3, math.log(3.0), jnp.where(mult == 2, math.log(2.0), 0.0))
        valid = valid & (mult > 0)
    return jnp.where(valid, s, NEG)


def _attn_fwd(q, k, v, arow, acol, *, dilated, name, tq=256, tk=256):
    s, w = q.shape
    nh = w // HEAD_DIM
    assert tq == tk
    nq, nk = s // tq, s // tk

    def body(q_ref, k_ref, v_ref, ar_ref, ac_ref, o_ref, of_ref, lse_ref, m_ref, l_ref, acc_ref):
        qi, ki = pl.program_id(1), pl.program_id(2)

        @pl.when(ki == 0)
        def _():
            m_ref[...] = jnp.full_like(m_ref, NEG)
            l_ref[...] = jnp.zeros_like(l_ref)
            acc_ref[...] = jnp.zeros_like(acc_ref)

        @pl.when(ki <= qi)
        def _():
            sc = _logits(q_ref[...], k_ref[...], ar_ref[...], ac_ref[...], qi * tq, ki * tk, dilated)
            m_new = jnp.maximum(m_ref[...], jnp.max(sc, axis=-1, keepdims=True))
            alpha = jnp.exp(m_ref[...] - m_new)
            p = jnp.exp(sc - m_new)
            l_ref[...] = alpha * l_ref[...] + jnp.sum(p, axis=-1, keepdims=True)
            p_hi = p.astype(BF16)
            p_lo = (p - p_hi.astype(F32)).astype(BF16)
            vv = v_ref[...]
            acc_ref[...] = (alpha * acc_ref[...] + jnp.dot(p_hi, vv, preferred_element_type=F32)
                            + jnp.dot(p_lo, vv, preferred_element_type=F32))
            m_ref[...] = m_new

        @pl.when(ki == nk - 1)
        def _():
            out = acc_ref[...] / l_ref[...]
            o_ref[...] = out.astype(o_ref.dtype)
            of_ref[...] = out
            lse_ref[...] = m_ref[...] + jnp.log(l_ref[...])

    kv = pl.BlockSpec((tk, HEAD_DIM), lambda h, i, j: (jnp.minimum(j, i), h))
    return pl.pallas_call(
        body, name=name, grid=(nh, nq, nk),
        in_specs=[pl.BlockSpec((tq, HEAD_DIM), lambda h, i, j: (i, h)), kv, kv,
                  pl.BlockSpec((None, tq, 1), lambda h, i, j: (h, i, 0)),
                  pl.BlockSpec((None, 1, tk), lambda h, i, j: (h, 0, jnp.minimum(j, i)))],
        out_specs=[pl.BlockSpec((tq, HEAD_DIM), lambda h, i, j: (i, h)),
                   pl.BlockSpec((tq, HEAD_DIM), lambda h, i, j: (i, h)),
                   pl.BlockSpec((None, tq, 1), lambda h, i, j: (h, i, 0))],
        out_shape=[jax.ShapeDtypeStruct((s, w), BF16), jax.ShapeDtypeStruct((s, w), F32),
                   jax.ShapeDtypeStruct((nh, s, 1), F32)],
        scratch_shapes=[pltpu.VMEM((tq, 1), F32), pltpu.VMEM((tq, 1), F32), pltpu.VMEM((tq, HEAD_DIM), F32)],
        compiler_params=_params("parallel", "parallel", "arbitrary"),
    )(q, k, v, arow, acol)


def _attn_bwd(q, k, v, o, do, lse, arow, acol, *, dilated, name, tq=256, tk=256):
    s, w = q.shape
    nh = w // HEAD_DIM
    assert tq == tk
    nq, nk = s // tq, s // tk
    scale = 1.0 / math.sqrt(HEAD_DIM)

    def body(q_ref, k_ref, v_ref, o_ref, do_ref, lse_ref, ar_ref, ac_ref,
             dq_ref, dk_ref, dv_ref, dac_ref, dk_acc, dv_acc, dac_acc):
        ki, qi = pl.program_id(1), pl.program_id(2)

        @pl.when((ki == 0) & (qi == 0))
        def _():
            dq_ref[...] = jnp.zeros_like(dq_ref)

        @pl.when(qi == 0)
        def _():
            dk_acc[...] = jnp.zeros_like(dk_acc)
            dv_acc[...] = jnp.zeros_like(dv_acc)
            dac_acc[...] = jnp.zeros_like(dac_acc)

        @pl.when(qi >= ki)
        def _():
            qv, kvv, dov = q_ref[...], k_ref[...], do_ref[...]
            sc = _logits(qv, kvv, ar_ref[...], ac_ref[...], qi * tq, ki * tk, dilated)
            p = jnp.exp(sc - lse_ref[...])
            dp = lax.dot_general(dov, v_ref[...], (((1,), (1,)), ((), ())), preferred_element_type=F32)
            delta = jnp.sum(dov.astype(F32) * o_ref[...].astype(F32), axis=-1, keepdims=True)
            ds = p * (dp - delta)
            dsb = ds.astype(BF16)
            dv_acc[...] += lax.dot_general(p.astype(BF16), dov, (((0,), (0,)), ((), ())), preferred_element_type=F32)
            dk_acc[...] += lax.dot_general(dsb, qv, (((0,), (0,)), ((), ())), preferred_element_type=F32)
            rows = pl.ds(pl.multiple_of(qi * tq, tq), tq)
            dq_ref[rows, :] += jnp.dot(dsb, kvv, preferred_element_type=F32) * scale
            dac_acc[...] += jnp.sum(ds, axis=0, keepdims=True)

        @pl.when(qi == nq - 1)
        def _():
            dk_ref[...] = dk_acc[...] * scale
            dv_ref[...] = dv_acc[...]
            dac_ref[...] = dac_acc[...]

    qs = pl.BlockSpec((tq, HEAD_DIM), lambda h, j, i: (jnp.maximum(i, j), h))
    ks = pl.BlockSpec((tk, HEAD_DIM), lambda h, j, i: (j, h))
    rowv = pl.BlockSpec((None, tq, 1), lambda h, j, i: (h, jnp.maximum(i, j), 0))
    colv = pl.BlockSpec((None, 1, tk), lambda h, j, i: (h, 0, j))
    return pl.pallas_call(
        body, name=name, grid=(nh, nk, nq),
        in_specs=[qs, ks, ks, qs, qs, rowv, rowv, colv],
        out_specs=[pl.BlockSpec((s, HEAD_DIM), lambda h, j, i: (0, h)), ks, ks, colv],
        out_shape=[jax.ShapeDtypeStruct((s, w), F32), jax.ShapeDtypeStruct((s, w), F32),
                   jax.ShapeDtypeStruct((s, w), F32), jax.ShapeDtypeStruct((nh, 1, s), F32)],
        scratch_shapes=[pltpu.VMEM((tk, HEAD_DIM), F32), pltpu.VMEM((tk, HEAD_DIM), F32), pltpu.VMEM((1, tk), F32)],
        compiler_params=_params("arbitrary", "arbitrary", "arbitrary"),
    )(q, k, v, o, do, lse, arow, acol)


def _gate_fwd(ga, gb, pa, pb, tm=256):
    s, d = ga.shape

    def body(ga_ref, gb_ref, pa_ref, pb_ref, o_ref):
        o_ref[...] = (jax.nn.sigmoid(ga_ref[...]) * pa_ref[...]
                      + jax.nn.sigmoid(gb_ref[...]) * pb_ref[...]).astype(o_ref.dtype)

    row = pl.BlockSpec((tm, d), lambda i: (i, 0))
    return pl.pallas_call(
        body, name="gate_fwd", grid=(s // tm,), in_specs=[row] * 4, out_specs=row,
        out_shape=jax.ShapeDtypeStruct((s, d), BF16), compiler_params=_params("parallel"),
    )(ga, gb, pa, pb)


def _gate_bwd(dm, ga, gb, pa, pb, tm=256):
    s, d = ga.shape

    def body(dm_ref, ga_ref, gb_ref, pa_ref, pb_ref, dpa_ref, dpb_ref, dga_ref, dgb_ref):
        dmv = dm_ref[...]
        for g_ref, p_ref, dp_ref, dg_ref in ((ga_ref, pa_ref, dpa_ref, dga_ref), (gb_ref, pb_ref, dpb_ref, dgb_ref)):
            sg = jax.nn.sigmoid(g_ref[...])
            dp_ref[...] = (dmv * sg).astype(BF16)
            dg_ref[...] = (dmv * p_ref[...] * (sg * (1.0 - sg))).astype(BF16)

    row = pl.BlockSpec((tm, d), lambda i: (i, 0))
    return pl.pallas_call(
        body, name="gate_bwd", grid=(s // tm,), in_specs=[row] * 5, out_specs=[row] * 4,
        out_shape=[jax.ShapeDtypeStruct((s, d), BF16)] * 4, compiler_params=_params("parallel"),
    )(dm, ga, gb, pa, pb)


def _shift_down(u, k):
    row = lax.broadcasted_iota(jnp.int32, u.shape, 0)
    return jnp.where(row >= k, pltpu.roll(u, k, 0), 0.0)


def _shift_up(u, k):
    n = u.shape[0]
    row = lax.broadcasted_iota(jnp.int32, u.shape, 0)
    return jnp.where(row < n - k, pltpu.roll(u, n - k, 0), 0.0)


def _conv3(u, wc, b):
    return wc[0:1, :] * _shift_down(u, 2) + wc[1:2, :] * _shift_down(u, 1) + wc[2:3, :] * u + b


def _conv_glu_fwd(u, wc, b, tn=256):
    s, f2 = u.shape
    f = f2 // 2
    nb = f // tn

    def body(ug_ref, uv_ref, wg_ref, wv_ref, bg_ref, bv_ref, o_ref):
        cg = _conv3(ug_ref[...], wg_ref[...], bg_ref[...])
        cv = _conv3(uv_ref[...], wv_ref[...], bv_ref[...])
        o_ref[...] = (cg * jax.nn.sigmoid(cg) * cv).astype(o_ref.dtype)

    def cols(rows, off):
        return pl.BlockSpec((rows, tn), lambda j: (0, j + off))

    return pl.pallas_call(
        body, name="conv_glu_fwd", grid=(nb,),
        in_specs=[cols(s, 0), cols(s, nb), cols(3, 0), cols(3, nb), cols(1, 0), cols(1, nb)],
        out_specs=cols(s, 0), out_shape=jax.ShapeDtypeStruct((s, f), BF16),
        compiler_params=_params("parallel"),
    )(u, u, wc, wc, b, b)


def _conv_glu_bwd(u, da, wc, b, tn=256):
    s, f2 = u.shape
    f = f2 // 2
    nb = f // tn

    def body(ug_ref, uv_ref, da_ref, wg_ref, wv_ref, bg_ref, bv_ref, dug_ref, duv_ref, sg_ref, sv_ref):
        ug, uv, wg, wv = ug_ref[...], uv_ref[...], wg_ref[...], wv_ref[...]
        cg = _conv3(ug, wg, bg_ref[...])
        cv = _conv3(uv, wv, bv_ref[...])
        sig = jax.nn.sigmoid(cg)
        dav = da_ref[...]
        dcv = dav * (cg * sig)
        dcg = dav * cv * (sig * (1.0 + cg * (1.0 - sig)))
        for dc, uu, w, du_ref, st_ref in ((dcg, ug, wg, dug_ref, sg_ref), (dcv, uv, wv, duv_ref, sv_ref)):
            du = w[2:3, :] * dc + w[1:2, :] * _shift_up(dc, 1) + w[0:1, :] * _shift_up(dc, 2)
            du_ref[...] = du.astype(BF16)
            st_ref[...] = jnp.zeros_like(st_ref)
            st_ref[0:1, :] = jnp.sum(dc * _shift_down(uu, 2), axis=0, keepdims=True)
            st_ref[1:2, :] = jnp.sum(dc * _shift_down(uu, 1), axis=0, keepdims=True)
            st_ref[2:3, :] = jnp.sum(dc * uu, axis=0, keepdims=True)
            st_ref[3:4, :] = jnp.sum(dc, axis=0, keepdims=True)

    def cols(rows, off):
        return pl.BlockSpec((rows, tn), lambda j: (0, j + off))

    return pl.pallas_call(
        body, name="conv_glu_bwd", grid=(nb,),
        in_specs=[cols(s, 0), cols(s, nb), cols(s, 0), cols(3, 0), cols(3, nb), cols(1, 0), cols(1, nb)],
        out_specs=[cols(s, 0), cols(s, 0), cols(8, 0), cols(8, 0)],
        out_shape=[jax.ShapeDtypeStruct((s, f), BF16), jax.ShapeDtypeStruct((s, f), BF16),
                   jax.ShapeDtypeStruct((8, f), F32), jax.ShapeDtypeStruct((8, f), F32)],
        compiler_params=_params("parallel"),
    )(u, u, da, wc, wc, b, b)


def _loss_head(y, target, tm=256):
    s, d = y.shape

    def body(y_ref, t_ref, dyf_ref, dyb_ref, l_ref):
        @pl.when(pl.program_id(0) == 0)
        def _():
            l_ref[...] = jnp.zeros_like(l_ref)

        err = y_ref[...] - t_ref[...]
        dy = err * (1.0 / d)
        dyf_ref[...] = dy
        dyb_ref[...] = dy.astype(BF16)
        l_ref[...] += 0.5 * jnp.sum(jnp.sum(err * err, axis=-1, keepdims=True) * (1.0 / d), axis=0, keepdims=True)

    row = pl.BlockSpec((tm, d), lambda i: (i, 0))
    return pl.pallas_call(
        body, name="loss_head", grid=(s // tm,), in_specs=[row, row],
        out_specs=[row, row, pl.BlockSpec((8, LANES), lambda i: (0, 0))],
        out_shape=[jax.ShapeDtypeStruct((s, d), F32), jax.ShapeDtypeStruct((s, d), BF16),
                   jax.ShapeDtypeStruct((8, LANES), F32)],
        compiler_params=_params("arbitrary"),
    )(y, target)


ROW_TILES = (256, 128, 64, 32, 16, 8)
BLOCK_BYTES = 1 << 20


def _add_halves(g, r1, c_idx):
    ns, r, c = g.shape
    rh = r // 2
    tr = _pick(rh, ROW_TILES)
    g4 = g.reshape(ns, 2, rh, c)

    def body(c_ref, g_ref, r_ref, o_ref):
        o_ref[...] = g_ref[...] + r_ref[...]

    return pl.pallas_call(
        body, name="add_halves",
        grid_spec=pltpu.PrefetchScalarGridSpec(
            num_scalar_prefetch=1, grid=(ns, rh // tr),
            in_specs=[pl.BlockSpec((None, None, tr, c), lambda s, i, cr: (s, cr[0], i, 0)),
                      pl.BlockSpec((None, tr, c), lambda s, i, cr: (s, i, 0))],
            out_specs=pl.BlockSpec((None, tr, c), lambda s, i, cr: (s, i, 0))),
        out_shape=jax.ShapeDtypeStruct((ns, rh, c), F32),
        compiler_params=_params("parallel", "parallel"),
    )(c_idx, g4, r1)


def _sum_chips(sums, recv, j_idx):
    _, rh, c = sums.shape
    tr = _pick(rh, ROW_TILES)

    def body(j_ref, s_ref, t0_ref, t1_ref, t2_ref, o_ref):
        o_ref[...] = ((s_ref[...] + t0_ref[...]) + t1_ref[...]) + t2_ref[...]

    def peer(k):
        return pl.BlockSpec((None, tr, c), lambda i, jr: (k, i, 0))

    return pl.pallas_call(
        body, name="sum_chips",
        grid_spec=pltpu.PrefetchScalarGridSpec(
            num_scalar_prefetch=1, grid=(rh // tr,),
            in_specs=[pl.BlockSpec((None, tr, c), lambda i, jr: (jr[0], i, 0)), peer(0), peer(1), peer(2)],
            out_specs=pl.BlockSpec((tr, c), lambda i, jr: (i, 0))),
        out_shape=jax.ShapeDtypeStruct((rh, c), F32),
        compiler_params=_params("parallel"),
    )(j_idx, sums, recv, recv, recv)


def _sum_devices(packs):
    n, r, c = packs.shape

    def body(p_ref, o_ref):
        acc = p_ref[0]
        for d in range(1, n):
            acc = acc + p_ref[d]
        o_ref[...] = acc

    return pl.pallas_call(
        body, name="sum_devices", out_shape=jax.ShapeDtypeStruct((r, c), F32), compiler_params=_params(),
    )(packs)


def _adamw_update(wv, gv, mv, vv):
    c1 = 1.0 - ADAM_B1 ** ADAM_STEP
    c2 = 1.0 - ADAM_B2 ** ADAM_STEP
    mn = ADAM_B1 * mv + (1.0 - ADAM_B1) * gv
    vn = ADAM_B2 * vv + (1.0 - ADAM_B2) * (gv * gv)
    m_hat = mn / c1
    v_hat = vn / c2
    return -ADAM_LR * (m_hat / (jnp.sqrt(v_hat) + ADAM_EPS) + ADAM_WD * wv), mn, vn


def _adamw(w, g, m, v, name):
    r, c = w.shape
    tr = _pick(r, ROW_TILES) if r >= 8 else r

    def body(w_ref, g_ref, m_ref, v_ref, d_ref, mo_ref, vo_ref):
        d_ref[...], mo_ref[...], vo_ref[...] = _adamw_update(w_ref[...], g_ref[...], m_ref[...], v_ref[...])

    blk = pl.BlockSpec((tr, c), lambda i: (i, 0))
    return pl.pallas_call(
        body, name=name, grid=(r // tr,), in_specs=[blk] * 4, out_specs=[blk] * 3,
        out_shape=[jax.ShapeDtypeStruct((r, c), F32)] * 3, compiler_params=_params("parallel"),
    )(w, g, m, v)


def _adamw_halves(w, mine, theirs, c_idx, m, v, name):
    r, c = w.shape
    rh = r // 2
    tr = _pick(rh, [t for t in ROW_TILES if t * c * 4 <= BLOCK_BYTES])
    nb = rh // tr

    def body(c_ref, w_ref, a_ref, b_ref, m_ref, v_ref, g_ref, d_ref, mo_ref, vo_ref):
        gv = jnp.where(pl.program_id(0) // nb == c_ref[0], a_ref[...], b_ref[...])
        g_ref[...] = gv
        d_ref[...], mo_ref[...], vo_ref[...] = _adamw_update(w_ref[...], gv, m_ref[...], v_ref[...])

    blk = pl.BlockSpec((tr, c), lambda i, cr: (i, 0))
    mine_spec = pl.BlockSpec((tr, c), lambda i, cr: (jnp.clip(i - cr[0] * nb, 0, nb - 1), 0))
    theirs_spec = pl.BlockSpec((tr, c), lambda i, cr: (jnp.clip(i - (1 - cr[0]) * nb, 0, nb - 1), 0))
    return pl.pallas_call(
        body, name=name,
        grid_spec=pltpu.PrefetchScalarGridSpec(
            num_scalar_prefetch=1, grid=(r // tr,),
            in_specs=[blk, mine_spec, theirs_spec, blk, blk], out_specs=[blk] * 4),
        out_shape=[jax.ShapeDtypeStruct((r, c), F32)] * 4, compiler_params=_params("arbitrary"),
    )(c_idx, w, mine, theirs, m, v)


ANY = pl.BlockSpec(memory_space=pl.ANY)


def _place():
    x, y, c = lax.axis_index("x"), lax.axis_index("y"), lax.axis_index("c")
    chips = [(1 - x, y), (x, 1 - y), (1 - x, 1 - y)]
    return x, y, c, chips


def _remote(src, dst, send_sem, recv_sem, to):
    return pltpu.make_async_remote_copy(src_ref=src, dst_ref=dst, send_sem=send_sem, recv_sem=recv_sem,
                                        device_id=to, device_id_type=MESH)


def _gather_weights(shards, wconv):
    nw = len(shards)

    def body(*refs):
        srcs, wc_ref = refs[:nw], refs[nw]
        outs, gc_ref = refs[nw + 1:2 * nw + 1], refs[2 * nw + 1]
        s1, r1, s2, r2 = refs[2 * nw + 2:]
        x, y, c, chips = _place()
        j = 2 * x + y
        sib = (x, y, 1 - c)

        def half(i, who):
            rh = srcs[i].shape[0] // 2
            return pl.ds(who * rh, rh)

        first = []
        for i in range(nw):
            for k, chip in enumerate(chips):
                first.append(_remote(srcs[i].at[half(i, c)], outs[i].at[j, half(i, c)], s1.at[i, k], r1.at[i, k], (*chip, c)))
        for k, chip in enumerate(chips):
            first.append(_remote(wc_ref, gc_ref.at[j], s1.at[nw, k], r1.at[nw, k], (*chip, c)))
        for cp in first:
            cp.start()
        passed = []
        for i in range(nw):
            for k, (cx, cy) in enumerate(chips):
                jk = 2 * cx + cy
                landed = outs[i].at[jk, half(i, c)]
                _remote(landed, landed, s1.at[i, k], r1.at[i, k], sib).wait_recv()
                cp = _remote(landed, landed, s2.at[i, k], r2.at[i, k], sib)
                cp.start()
                passed.append(cp)
        for k, (cx, cy) in enumerate(chips):
            landed = gc_ref.at[2 * cx + cy]
            _remote(landed, landed, s1.at[nw, k], r1.at[nw, k], sib).wait_recv()
        for i in range(nw):
            for k, (cx, cy) in enumerate(chips):
                landed = outs[i].at[2 * cx + cy, half(i, 1 - c)]
                _remote(landed, landed, s2.at[i, k], r2.at[i, k], sib).wait_recv()
        for cp in first + passed:
            cp.wait_send()

    out_shape = [jax.ShapeDtypeStruct((N_CHIPS,) + s.shape, s.dtype) for s in shards]
    out_shape.append(jax.ShapeDtypeStruct((N_CHIPS,) + wconv.shape, wconv.dtype))
    slabs = pl.pallas_call(
        body, name="gather_weights", in_specs=[ANY] * (nw + 1), out_specs=[ANY] * (nw + 1), out_shape=out_shape,
        scratch_shapes=[pltpu.SemaphoreType.DMA((nw + 1, 3)), pltpu.SemaphoreType.DMA((nw + 1, 3)),
                        pltpu.SemaphoreType.DMA((nw, 3)), pltpu.SemaphoreType.DMA((nw, 3))],
    )(*shards, wconv)
    chip = 2 * lax.axis_index("x") + lax.axis_index("y")
    return [lax.dynamic_update_slice(g, own[None], (chip, 0, 0)) for g, own in zip(slabs, [*shards, wconv])]


def _swap_halves(grads):
    nw = len(grads)

    def body(*refs):
        srcs, outs = refs[:nw], refs[nw:2 * nw]
        ssem, rsem = refs[2 * nw:]
        x, y, c, _ = _place()
        sib = (x, y, 1 - c)
        cps = []
        for i in range(nw):
            rh = srcs[i].shape[1] // 2
            cps.append(_remote(srcs[i].at[:, pl.ds((1 - c) * rh, rh)], outs[i], ssem.at[i], rsem.at[i], sib))
        for cp in cps:
            cp.start()
        for cp in cps:
            cp.wait()

    return pl.pallas_call(
        body, name="swap_halves", in_specs=[ANY] * nw, out_specs=[ANY] * nw,
        out_shape=[jax.ShapeDtypeStruct((g.shape[0], g.shape[1] // 2, g.shape[2]), g.dtype) for g in grads],
        scratch_shapes=[pltpu.SemaphoreType.DMA((nw,)), pltpu.SemaphoreType.DMA((nw,))],
    )(*grads)


def _scatter_chips(sums):
    nw = len(sums)

    def body(*refs):
        srcs, outs = refs[:nw], refs[nw:2 * nw]
        ssem, rsem = refs[2 * nw:]
        x, y, c, chips = _place()
        cps = []
        for i in range(nw):
            for k, (cx, cy) in enumerate(chips):
                cps.append(_remote(srcs[i].at[2 * cx + cy], outs[i].at[k], ssem.at[i, k], rsem.at[i, k], (cx, cy, c)))
        for cp in cps:
            cp.start()
        for cp in cps:
            cp.wait()

    return pl.pallas_call(
        body, name="scatter_chips", in_specs=[ANY] * nw, out_specs=[ANY] * nw,
        out_shape=[jax.ShapeDtypeStruct((3,) + s.shape[1:], s.dtype) for s in sums],
        scratch_shapes=[pltpu.SemaphoreType.DMA((nw, 3)), pltpu.SemaphoreType.DMA((nw, 3))],
    )(*sums)


def _join_halves(halves):
    nw = len(halves)

    def body(*refs):
        srcs, outs = refs[:nw], refs[nw:2 * nw]
        ssem, rsem = refs[2 * nw:]
        x, y, c, _ = _place()
        cps = [_remote(srcs[i], outs[i], ssem.at[i], rsem.at[i], (x, y, 1 - c)) for i in range(nw)]
        for cp in cps:
            cp.start()
        for cp in cps:
            cp.wait()

    return pl.pallas_call(
        body, name="join_halves", in_specs=[ANY] * nw, out_specs=[ANY] * nw,
        out_shape=[jax.ShapeDtypeStruct(h.shape, h.dtype) for h in halves],
        scratch_shapes=[pltpu.SemaphoreType.DMA((nw,)), pltpu.SemaphoreType.DMA((nw,))],
    )(*halves)


def _gather_packs(pack):
    def body(p_ref, o_ref, lsem, ssem, rsem):
        x, y, c, _ = _place()
        me = 4 * x + 2 * y + c
        local = pltpu.make_async_copy(p_ref, o_ref.at[me], lsem)
        local.start()
        cps = []
        for k in range(1, N_DEV):
            fx, fy, fc = (k >> 2) & 1, (k >> 1) & 1, k & 1
            to = (x ^ fx, y ^ fy, c ^ fc)
            cps.append(_remote(p_ref, o_ref.at[me], ssem.at[k - 1], rsem.at[k - 1], to))
        for cp in cps:
            cp.start()
        for k in range(1, N_DEV):
            fx, fy, fc = (k >> 2) & 1, (k >> 1) & 1, k & 1
            src = o_ref.at[4 * (x ^ fx) + 2 * (y ^ fy) + (c ^ fc)]
            _remote(src, src, ssem.at[k - 1], rsem.at[k - 1], (x, y, c)).wait_recv()
        for cp in cps:
            cp.wait_send()
        local.wait()

    return pl.pallas_call(
        body, name="gather_packs", in_specs=[ANY], out_specs=ANY,
        out_shape=jax.ShapeDtypeStruct((N_DEV,) + pack.shape, pack.dtype),
        scratch_shapes=[pltpu.SemaphoreType.DMA, pltpu.SemaphoreType.DMA((N_DEV - 1,)), pltpu.SemaphoreType.DMA((N_DEV - 1,))],
    )(pack)


LANE_TILES = (512, 896, 1408, 704, 384, 256, 128)


def _layer_grads(x, target, small, wg):
    s, d = x.shape
    w_att = N_HEADS * HEAD_DIM
    in_splits = (w_att, w_att, w_att, N_HEADS, w_att, w_att, w_att, d, d)
    in_cols = sum(in_splits)
    cs = in_cols // N_CHIPS
    cp = wg["in"].shape[2]
    tm = min(s, 1024)
    t_in = _pick(cp, LANE_TILES)
    t_d = _pick(d, LANE_TILES)
    t_dq = _pick(d // N_CHIPS, LANE_TILES)
    t_w = _pick(w_att, LANE_TILES)
    t_up = _pick(wg["up"].shape[2], LANE_TILES)
    f = wg["down"].shape[1] * N_CHIPS
    t_f = _pick(f, LANE_TILES)
    t_fq = _pick(f // N_CHIPS, LANE_TILES)
    offs = np.cumsum(in_splits)[:-1].tolist()

    h1 = _norm_fwd(x, small["g_attn"], group=d, name="rms1_fwd")
    proj_p = _mm(h1, wg["in"], mode="nn", b_kind="col", tm=tm, tn=t_in, tk=t_d, name="mm_in")
    proj = proj_p.reshape(s, N_CHIPS, cp)[:, :, :cs].reshape(s, in_cols)
    qa, ka, va, fa, qb, kb, vb, ga, gb = jnp.split(proj, offs, axis=-1)
    gains = {n: small[n].reshape(1, w_att) for n in ("g_q_fox", "g_k_fox", "g_q_dil", "g_k_dil")}
    qa_n = _norm_fwd(qa, gains["g_q_fox"], group=HEAD_DIM, name="qnorm_fox")
    ka_n = _norm_fwd(ka, gains["g_k_fox"], group=HEAD_DIM, name="knorm_fox")
    qb_n = _norm_fwd(qb, gains["g_q_dil"], group=HEAD_DIM, name="qnorm_dil")
    kb_n = _norm_fwd(kb, gains["g_k_dil"], group=HEAD_DIM, name="knorm_dil")
    va_b, vb_b = va.astype(BF16), vb.astype(BF16)
    fa_t = fa.T
    b_f = small["b_forget"].reshape(N_HEADS, 1)
    c_f = _forget_fwd(fa_t, b_f)
    slopes = jnp.asarray(2.0 ** (-8.0 * np.arange(1, N_HEADS + 1) / N_HEADS), dtype=F32)
    a_d = -(slopes[:, None] * jnp.arange(s, dtype=F32)[None, :])
    rows_f, cols_f = c_f[:, :, None], c_f[:, None, :]
    rows_d, cols_d = a_d[:, :, None], a_d[:, None, :]
    o_a, o_a32, lse_a = _attn_fwd(qa_n, ka_n, va_b, rows_f, cols_f, dilated=False, name="attn_fox_fwd")
    o_b, o_b32, lse_b = _attn_fwd(qb_n, kb_n, vb_b, rows_d, cols_d, dilated=True, name="attn_dil_fwd")
    pa = _mm(o_a, wg["brf"], mode="nn", b_kind="col", tm=tm, tn=t_dq, tk=t_w, name="mm_brf")
    pb = _mm(o_b, wg["brd"], mode="nn", b_kind="col", tm=tm, tn=t_dq, tk=t_w, name="mm_brd")
    merged = _gate_fwd(ga, gb, pa, pb)
    x1 = _mm(merged, wg["out"], mode="nn", b_kind="row", res=x, tm=tm, tn=t_d, tk=t_dq, name="mm_out")
    h2 = _norm_fwd(x1, small["g_ffn"], group=d, name="rms2_fwd")
    u = _mm(h2, wg["up"], mode="nn", b_kind="col", tm=tm, tn=t_up, tk=t_d, name="mm_up")
    act = _conv_glu_fwd(u, wg["conv"], wg["bconv"])
    y = _mm(act, wg["down"], mode="nn", b_kind="row", res=x1, tm=tm, tn=t_d, tk=t_fq, name="mm_down")
    dy_f, dy_b, loss_blk = _loss_head(y, target)

    grads = {}
    d_act = _mm(dy_b, wg["down"], mode="nt", b_kind="row", tm=tm, tn=t_fq, tk=t_d, name="mm_down_dx")
    grads["down"] = _mm(act, dy_b, mode="tn", out_kind="row", tm=t_fq, tn=t_d, tk=tm, name="mm_down_dw")
    du_g, du_v, st_g, st_v = _conv_glu_bwd(u, d_act, wg["conv"], wg["bconv"])
    du = jnp.concatenate([du_g, du_v], axis=1)
    grads["up"] = _mm(h2, du, mode="tn", out_kind="col", tm=t_d, tn=t_up, tk=tm, name="mm_up_dw")
    dh2 = _mm(du, wg["up"], mode="nt", b_kind="col", tm=tm, tn=t_d, tk=t_up, name="mm_up_dx")
    dx1_b, dx1_f, dg_ffn = _norm_bwd(dh2, x1, small["g_ffn"], group=d, res=dy_f, out_dtypes=(BF16, F32), name="rms2_bwd")
    d_merged = _mm(dx1_b, wg["out"], mode="nt", b_kind="row", tm=tm, tn=t_dq, tk=t_d, name="mm_out_dx")
    grads["out"] = _mm(merged, dx1_b, mode="tn", out_kind="row", tm=t_dq, tn=t_d, tk=tm, name="mm_out_dw")
    dpa, dpb, dga, dgb = _gate_bwd(d_merged, ga, gb, pa, pb)
    do_a = _mm(dpa, wg["brf"], mode="nt", b_kind="col", out_dtype=BF16, tm=tm, tn=t_w, tk=t_dq, name="mm_brf_dx")
    do_b = _mm(dpb, wg["brd"], mode="nt", b_kind="col", out_dtype=BF16, tm=tm, tn=t_w, tk=t_dq, name="mm_brd_dx")
    grads["brf"] = _mm(o_a, dpa, mode="tn", out_kind="col", tm=t_w, tn=t_dq, tk=tm, name="mm_brf_dw")
    grads["brd"] = _mm(o_b, dpb, mode="tn", out_kind="col", tm=t_w, tn=t_dq, tk=tm, name="mm_brd_dw")
    dqa_n, dka_n, dva, dac_a = _attn_bwd(qa_n, ka_n, va_b, o_a32, do_a, lse_a, rows_f, cols_f, dilated=False, name="attn_fox_bwd")
    dqb_n, dkb_n, dvb, _ = _attn_bwd(qb_n, kb_n, vb_b, o_b32, do_b, lse_b, rows_d, cols_d, dilated=True, name="attn_dil_bwd")
    dqa, dg_qf = _norm_bwd(dqa_n, qa, gains["g_q_fox"], group=HEAD_DIM, name="qnorm_fox_bwd")
    dka, dg_kf = _norm_bwd(dka_n, ka, gains["g_k_fox"], group=HEAD_DIM, name="knorm_fox_bwd")
    dqb, dg_qd = _norm_bwd(dqb_n, qb, gains["g_q_dil"], group=HEAD_DIM, name="qnorm_dil_bwd")
    dkb, dg_kd = _norm_bwd(dkb_n, kb, gains["g_k_dil"], group=HEAD_DIM, name="knorm_dil_bwd")
    dfa_t, db_f = _forget_bwd(dac_a[:, 0, :], fa_t, b_f)
    dproj = jnp.concatenate([dqa, dka, dva.astype(BF16), dfa_t.T.astype(BF16), dqb, dkb, dvb.astype(BF16), dga, dgb], axis=1)
    dproj_p = jnp.pad(dproj.reshape(s, N_CHIPS, cs), ((0, 0), (0, 0), (0, cp - cs))).reshape(s, N_CHIPS * cp)
    grads["in"] = _mm(h1, dproj_p, mode="tn", out_kind="col", tm=t_d, tn=t_in, tk=tm, name="mm_in_dw")
    dh1 = _mm(dproj_p, wg["in"], mode="nt", b_kind="col", tm=tm, tn=t_d, tk=t_in, name="mm_in_dx")
    grad_x, dg_attn = _norm_bwd(dh1, x, small["g_attn"], group=d, res=dx1_f, out_dtypes=(F32,), name="rms1_bwd")

    small_grads = {
        "g_attn": dg_attn, "b_forget": db_f.reshape(1, N_HEADS),
        "g_q_fox": dg_qf, "g_k_fox": dg_kf, "g_q_dil": dg_qd, "g_k_dil": dg_kd, "g_ffn": dg_ffn,
        "w_conv": jnp.concatenate([st_g[0:3], st_v[0:3]], axis=1),
        "b_conv": jnp.concatenate([st_g[3:4], st_v[3:4]], axis=1),
        "loss": loss_blk[0:1, 0:1],
    }
    return grads, small_grads, grad_x


SMALL_ORDER = ("g_attn", "b_forget", "g_q_fox", "g_k_fox", "g_q_dil", "g_k_dil", "g_ffn", "w_conv", "b_conv", "loss")
WEIGHT_ORDER = ("g_attn", "w_in", "b_forget", "g_q_fox", "g_k_fox", "g_q_dil", "g_k_dil", "w_br_fox", "w_br_dil",
                "w_out", "g_ffn", "w_up", "w_conv", "b_conv", "w_down")
BIG = {"w_in": "in", "w_br_fox": "brf", "w_br_dil": "brd", "w_out": "out", "w_up": "up", "w_down": "down"}


def kernel(x, g_attn, w_in, b_forget, g_q_fox, g_k_fox, g_q_dil, g_k_dil, w_br_fox, w_br_dil, w_out, g_ffn, w_up, w_conv, b_conv, w_down, loss_target, m_g_attn, m_w_in, m_b_forget, m_g_q_fox, m_g_k_fox, m_g_q_dil, m_g_k_dil, m_w_br_fox, m_w_br_dil, m_w_out, m_g_ffn, m_w_up, m_w_conv, m_b_conv, m_w_down, v_g_attn, v_w_in, v_b_forget, v_g_q_fox, v_g_k_fox, v_g_q_dil, v_g_k_dil, v_w_br_fox, v_w_br_dil, v_w_out, v_g_ffn, v_w_up, v_w_conv, v_b_conv, v_w_down):
    w = dict(g_attn=g_attn, w_in=w_in, b_forget=b_forget, g_q_fox=g_q_fox, g_k_fox=g_k_fox, g_q_dil=g_q_dil,
             g_k_dil=g_k_dil, w_br_fox=w_br_fox, w_br_dil=w_br_dil, w_out=w_out, g_ffn=g_ffn, w_up=w_up,
             w_conv=w_conv, b_conv=b_conv, w_down=w_down)
    m = dict(g_attn=m_g_attn, w_in=m_w_in, b_forget=m_b_forget, g_q_fox=m_g_q_fox, g_k_fox=m_g_k_fox,
             g_q_dil=m_g_q_dil, g_k_dil=m_g_k_dil, w_br_fox=m_w_br_fox, w_br_dil=m_w_br_dil, w_out=m_w_out,
             g_ffn=m_g_ffn, w_up=m_w_up, w_conv=m_w_conv, b_conv=m_b_conv, w_down=m_w_down)
    v = dict(g_attn=v_g_attn, w_in=v_w_in, b_forget=v_b_forget, g_q_fox=v_g_q_fox, g_k_fox=v_g_k_fox,
             g_q_dil=v_g_q_dil, g_k_dil=v_g_k_dil, w_br_fox=v_w_br_fox, w_br_dil=v_w_br_dil, w_out=v_w_out,
             g_ffn=v_g_ffn, w_up=v_w_up, w_conv=v_w_conv, b_conv=v_b_conv, w_down=v_w_down)
    xi, yi, ci = lax.axis_index("x"), lax.axis_index("y"), lax.axis_index("c")
    chip = (2 * xi + yi).astype(jnp.int32)
    c_idx = ci.astype(jnp.int32).reshape(1)
    j_idx = chip.reshape(1)

    cs = w_in.shape[2]
    cp = _round_up(cs, LANES)
    shards = {
        "in": jnp.pad(w_in[0].astype(BF16), ((0, 0), (0, cp - cs))),
        "brf": w_br_fox[0].astype(BF16), "brd": w_br_dil[0].astype(BF16), "out": w_out[0].astype(BF16),
        "up": w_up[0].astype(BF16), "down": w_down[0].astype(BF16),
    }
    names = tuple(shards)
    conv_pad = jnp.pad(w_conv[0], ((0, 8 - w_conv.shape[1]), (0, 0)))
    gathered = _gather_weights([shards[n] for n in names], conv_pad)
    wg = dict(zip(names, gathered[:-1]))
    conv_all = gathered[-1]
    wg["conv"] = jnp.transpose(conv_all[:, :w_conv.shape[1], :], (1, 0, 2)).reshape(w_conv.shape[1], -1)
    wg["bconv"] = b_conv
    small = {n: w[n] for n in ("g_attn", "b_forget", "g_q_fox", "g_k_fox", "g_q_dil", "g_k_dil", "g_ffn")}
    small = {n: (a[0] if a.ndim == 3 else a) for n, a in small.items()}

    grads, small_grads, grad_x = _layer_grads(x[0], loss_target[0], small, wg)

    glist = [grads[n] for n in names]
    from_sibling = _swap_halves(glist)
    chip_sums = [_add_halves(g, r, c_idx) for g, r in zip(glist, from_sibling)]
    from_chips = _scatter_chips(chip_sums)
    halves = [_sum_chips(sm, rc, j_idx) for sm, rc in zip(chip_sums, from_chips)]
    mine = dict(zip(names, halves))
    theirs = dict(zip(names, _join_halves(halves)))
    mine["in"], theirs["in"] = mine["in"][:, :cs], theirs["in"][:, :cs]

    flat = jnp.concatenate([small_grads[n].reshape(-1) for n in SMALL_ORDER])
    rows = _round_up(flat.shape[0], 8 * LANES) // LANES
    pack = jnp.pad(flat, (0, rows * LANES - flat.shape[0])).reshape(rows, LANES)
    total = _sum_devices(_gather_packs(pack)).reshape(-1)
    red, at = {}, 0
    for n in SMALL_ORDER:
        size = small_grads[n].size
        red[n] = total[at:at + size].reshape(small_grads[n].shape)
        at += size
    loss = red["loss"].reshape(())
    c2 = w_conv.shape[2]
    red["w_conv"] = lax.dynamic_slice_in_dim(red["w_conv"], chip * c2, c2, axis=1)

    g_out, d_out, m_out, v_out = {}, {}, {}, {}
    for n in WEIGHT_ORDER:
        shape = w[n].shape
        r2 = (shape[-2], shape[-1]) if n not in ("g_attn", "b_forget", "g_ffn", "b_conv") else (1, shape[-1])
        if n in BIG:
            g2, dl, mn, vn = _adamw_halves(w[n].reshape(r2), mine[BIG[n]], theirs[BIG[n]], c_idx,
                                           m[n].reshape(r2), v[n].reshape(r2), name="adamw_" + n)
        else:
            g2 = red[n].reshape(r2)
            dl, mn, vn = _adamw(w[n].reshape(r2), g2, m[n].reshape(r2), v[n].reshape(r2), name="adamw_" + n)
        g_out[n], d_out[n], m_out[n], v_out[n] = (a.reshape(shape) for a in (g2, dl, mn, vn))

    return (loss, grad_x[None], *[g_out[n] for n in WEIGHT_ORDER], *[d_out[n] for n in WEIGHT_ORDER],
            *[m_out[n] for n in WEIGHT_ORDER], *[v_out[n] for n in WEIGHT_ORDER])
```

```python
import functools
import math

import jax
import jax.numpy as jnp
import numpy as np
from jax import lax
from jax.experimental import pallas as pl
from jax.experimental.pallas import tpu as pltpu

F32 = jnp.float32
BF16 = jnp.bfloat16
HEAD_DIM = 128
N_HEADS = 8
EPS = 1e-6
NEG = -1e30
N_CHIPS = 4
N_DEV = 8
LANES = 128
VMEM_LIMIT_BYTES = 56 * 1024 * 1024
DIL_PATTERNS = ((128, 1), (512, 4), (2048, 16))
ATTN_TILE = 512
ADAM_LR, ADAM_B1, ADAM_B2, ADAM_EPS, ADAM_WD, ADAM_STEP = 0.001, 0.9, 0.999, 1e-08, 0.01, 10
MESH = pl.DeviceIdType.MESH


def _params(*sem):
    return pltpu.CompilerParams(dimension_semantics=sem, vmem_limit_bytes=VMEM_LIMIT_BYTES)


def _round_up(n, m):
    return -(-n // m) * m


def _pick(dim, prefs):
    for p in prefs:
        if dim % p == 0:
            return p
    raise ValueError(f"no tile for {dim} in {prefs}")


def _logical_shape(arr, kind):
    if kind is None:
        return arr.shape
    s, r, c = arr.shape
    return (r, s * c) if kind == "col" else (s * r, c)


def _spec(shape, kind, br, bc, fi, fj):
    if kind is None:
        return pl.BlockSpec((br, bc), lambda *g: (fi(*g), fj(*g)))
    _, r, c = shape
    if kind == "col":
        nb = c // bc
        assert nb * bc == c, (shape, bc)
        return pl.BlockSpec((None, br, bc), lambda *g: (fj(*g) // nb, fi(*g), fj(*g) % nb))
    nb = r // br
    assert nb * br == r, (shape, br)
    return pl.BlockSpec((None, br, bc), lambda *g: (fi(*g) // nb, fi(*g) % nb, fj(*g)))


def _mm(a, b, *, mode, tm, tn, tk, name, a_kind=None, b_kind=None, out_kind=None,
        out_dtype=F32, res=None):
    la, lb = _logical_shape(a, a_kind), _logical_shape(b, b_kind)
    if mode == "nn":
        (m, k), (k2, n) = la, lb
    elif mode == "nt":
        (m, k), (n, k2) = la, lb
    else:
        (k, m), (k2, n) = la, lb
    assert k == k2, (name, la, lb)
    assert m % tm == 0 and n % tn == 0 and k % tk == 0, (name, m, n, k, tm, tn, tk)
    nk = k // tk
    im = lambda i, j, l: i
    jn = lambda i, j, l: j
    lk = lambda i, j, l: l
    if mode == "tn":
        a_spec = _spec(a.shape, a_kind, tk, tm, lk, im)
        dims = (((0,), (0,)), ((), ()))
    else:
        a_spec = _spec(a.shape, a_kind, tm, tk, im, lk)
        dims = (((1,), (1,)), ((), ())) if mode == "nt" else (((1,), (0,)), ((), ()))
    if mode == "nt":
        b_spec = _spec(b.shape, b_kind, tn, tk, jn, lk)
    else:
        b_spec = _spec(b.shape, b_kind, tk, tn, lk, jn)
    if out_kind is None:
        oshape = (m, n)
    elif out_kind == "col":
        oshape = (N_CHIPS, m, n // N_CHIPS)
    else:
        oshape = (N_CHIPS, m // N_CHIPS, n)
    o_spec = _spec(oshape, out_kind, tm, tn, im, jn)
    in_specs = [a_spec, b_spec]
    args = [a, b]
    if res is not None:
        in_specs.append(pl.BlockSpec((tm, tn), lambda i, j, l: (i, j)))
        args.append(res)

    def body(*refs):
        a_ref, b_ref = refs[0], refs[1]
        res_ref = refs[2] if res is not None else None
        o_ref, acc_ref = refs[-2], refs[-1]
        step = pl.program_id(2)

        @pl.when(step == 0)
        def _():
            acc_ref[...] = jnp.zeros_like(acc_ref)

        acc_ref[...] += lax.dot_general(a_ref[...], b_ref[...], dims, preferred_element_type=F32)

        @pl.when(step == nk - 1)
        def _():
            out = acc_ref[...]
            if res_ref is not None:
                out = out + res_ref[...]
            o_ref[...] = out.astype(o_ref.dtype)

    return pl.pallas_call(
        body, name=name, grid=(m // tm, n // tn, nk),
        in_specs=in_specs, out_specs=o_spec,
        out_shape=jax.ShapeDtypeStruct(oshape, out_dtype),
        scratch_shapes=[pltpu.VMEM((tm, tn), F32)],
        compiler_params=_params("parallel", "parallel", "arbitrary"),
    )(*args)


def _norm_fwd(x, g, *, group, name, tm=256):
    s, w = x.shape
    ng = w // group

    def body(x_ref, g_ref, o_ref):
        for i in range(ng):
            cols = slice(i * group, (i + 1) * group)
            xv = x_ref[:, cols]
            r = lax.rsqrt(jnp.mean(xv * xv, axis=-1, keepdims=True) + EPS)
            o_ref[:, cols] = ((xv * r) * g_ref[:, cols]).astype(o_ref.dtype)

    return pl.pallas_call(
        body, name=name, grid=(s // tm,),
        in_specs=[pl.BlockSpec((tm, w), lambda i: (i, 0)), pl.BlockSpec((1, w), lambda i: (0, 0))],
        out_specs=pl.BlockSpec((tm, w), lambda i: (i, 0)),
        out_shape=jax.ShapeDtypeStruct((s, w), BF16),
        compiler_params=_params("parallel"),
    )(x, g)


def _norm_bwd(dy, x, g, *, group, name, res=None, out_dtypes=(BF16,), tm=256):
    s, w = x.shape
    ng = w // group
    n_in = 4 if res is not None else 3

    def body(*refs):
        dy_ref, x_ref, g_ref = refs[:3]
        res_ref = refs[3] if res is not None else None
        outs = refs[n_in:]
        dx_refs, dg_ref = outs[:-1], outs[-1]

        @pl.when(pl.program_id(0) == 0)
        def _():
            dg_ref[...] = jnp.zeros_like(dg_ref)

        for i in range(ng):
            cols = slice(i * group, (i + 1) * group)
            xv = x_ref[:, cols]
            dyv = dy_ref[:, cols].astype(F32)
            r = lax.rsqrt(jnp.mean(xv * xv, axis=-1, keepdims=True) + EPS)
            xr = xv * r
            dg_ref[:, cols] += jnp.sum(dyv * xr, axis=0, keepdims=True)
            gdy = dyv * g_ref[:, cols]
            dx = r * (gdy - xr * jnp.mean(gdy * xr, axis=-1, keepdims=True))
            if res_ref is not None:
                dx = dx + res_ref[:, cols]
            for dx_ref in dx_refs:
                dx_ref[:, cols] = dx.astype(dx_ref.dtype)

    row = pl.BlockSpec((tm, w), lambda i: (i, 0))
    vec = pl.BlockSpec((1, w), lambda i: (0, 0))
    in_specs = [row, row, vec] + ([row] if res is not None else [])
    args = [dy, x, g] + ([res] if res is not None else [])
    out_specs = [row] * len(out_dtypes) + [vec]
    out_shape = [jax.ShapeDtypeStruct((s, w), dt) for dt in out_dtypes] + [jax.ShapeDtypeStruct((1, w), F32)]
    return pl.pallas_call(
        body, name=name, grid=(s // tm,), in_specs=in_specs, out_specs=out_specs,
        out_shape=out_shape, compiler_params=_params("arbitrary"),
    )(*args)


def _split3(v):
    p1 = v.astype(BF16)
    r1 = v - p1.astype(F32)
    p2 = r1.astype(BF16)
    p3 = (r1 - p2.astype(F32)).astype(BF16)
    return p1, p2, p3


def _tri_sum(v, reverse, tcol=512):
    h, s = v.shape
    tcol = min(tcol, s)
    parts = _split3(v)
    outs = []
    for j in range(s // tcol):
        src = lax.broadcasted_iota(jnp.int32, (s, tcol), 0)
        dst = lax.broadcasted_iota(jnp.int32, (s, tcol), 1) + j * tcol
        keep = (src >= dst) if reverse else (src <= dst)
        tri = jnp.where(keep, 1.0, 0.0).astype(BF16)
        acc = jnp.zeros((h, tcol), F32)
        for p in parts:
            acc = acc + jnp.dot(p, tri, preferred_element_type=F32)
        outs.append(acc)
    return outs


def _forget_fwd(fa_t, b):
    h, s = fa_t.shape
    tcol = min(512, s)

    def body(f_ref, b_ref, c_ref):
        z = f_ref[...] + b_ref[...]
        logf = jnp.minimum(z, 0.0) - jnp.log(1.0 + jnp.exp(-jnp.abs(z)))
        for j, blk in enumerate(_tri_sum(logf, reverse=False, tcol=tcol)):
            c_ref[:, j * tcol:(j + 1) * tcol] = blk

    return pl.pallas_call(
        body, name="forget_fwd", out_shape=jax.ShapeDtypeStruct((h, s), F32),
        compiler_params=_params(),
    )(fa_t, b)


def _forget_bwd(dacol, fa_t, b):
    h, s = fa_t.shape
    tcol = min(512, s)

    def body(d_ref, f_ref, b_ref, dfa_ref, db_ref):
        z = f_ref[...] + b_ref[...]
        dc = -d_ref[...]
        total = jnp.zeros((h, 1), F32)
        for j, blk in enumerate(_tri_sum(dc, reverse=True, tcol=tcol)):
            cols = slice(j * tcol, (j + 1) * tcol)
            dfa = blk * (1.0 - jax.nn.sigmoid(z[:, cols]))
            dfa_ref[:, cols] = dfa
            total = total + jnp.sum(dfa, axis=-1, keepdims=True)
        db_ref[...] = total

    return pl.pallas_call(
        body, name="forget_bwd",
        out_shape=[jax.ShapeDtypeStruct((h, s), F32), jax.ShapeDtypeStruct((h, 1), F32)],
        compiler_params=_params(),
    )(dacol, fa_t, b)


def _logits(q, k, arow, acol, q0, k0, dilated):
    tq, tk = q.shape[0], k.shape[0]
    s = lax.dot_general(q, k, (((1,), (1,)), ((), ())), preferred_element_type=F32)
    s = s * (1.0 / math.sqrt(HEAD_DIM)) + arow - acol
    dist = (q0 + lax.broadcasted_iota(jnp.int32, (tq, tk), 0)) - (k0 + lax.broadcasted_iota(jnp.int32, (tq, tk), 1))
    valid = dist >= 0
    if dilated:
        mult = jnp.zeros((tq, tk), jnp.int32)
        for window, dil in DIL_PATTERNS:
            mult = mult + ((dist <= window) & ((dist & (dil - 1)) == 0)).astype(jnp.int32)
        s = s + jnp.where(mult == 3, math.log(3.0), jnp.where(mult == 2, math.log(2.0), 0.0))
        valid = valid & (mult > 0)
    return jnp.where(valid, s, NEG)


def _attn_fwd(q, k, v, arow, acol, *, dilated, name, tq=ATTN_TILE, tk=ATTN_TILE):
    s, w = q.shape
    nh = w // HEAD_DIM
    assert tq == tk
    tq = tk = min(tq, s)
    nq, nk = s // tq, s // tk

    def body(q_ref, k_ref, v_ref, ar_ref, ac_ref, o_ref, of_ref, lse_ref, m_ref, l_ref, acc_ref):
        qi, ki = pl.program_id(1), pl.program_id(2)

        @pl.when(ki == 0)
        def _():
            m_ref[...] = jnp.full_like(m_ref, NEG)
            l_ref[...] = jnp.zeros_like(l_ref)
            acc_ref[...] = jnp.zeros_like(acc_ref)

        @pl.when(ki <= qi)
        def _():
            sc = _logits(q_ref[...], k_ref[...], ar_ref[...], ac_ref[...], qi * tq, ki * tk, dilated)
            m_new = jnp.maximum(m_ref[...], jnp.max(sc, axis=-1, keepdims=True))
            alpha = jnp.exp(m_ref[...] - m_new)
            p = jnp.exp(sc - m_new)
            l_ref[...] = alpha * l_ref[...] + jnp.sum(p, axis=-1, keepdims=True)
            p_hi = p.astype(BF16)
            p_lo = (p - p_hi.astype(F32)).astype(BF16)
            vv = v_ref[...]
            acc_ref[...] = (alpha * acc_ref[...] + jnp.dot(p_hi, vv, preferred_element_type=F32)
                            + jnp.dot(p_lo, vv, preferred_element_type=F32))
            m_ref[...] = m_new

        @pl.when(ki == nk - 1)
        def _():
            out = acc_ref[...] / l_ref[...]
            o_ref[...] = out.astype(o_ref.dtype)
            of_ref[...] = out
            lse_ref[...] = m_ref[...] + jnp.log(l_ref[...])

    kv = pl.BlockSpec((tk, HEAD_DIM), lambda h, i, j: (jnp.minimum(j, i), h))
    return pl.pallas_call(
        body, name=name, grid=(nh, nq, nk),
        in_specs=[pl.BlockSpec((tq, HEAD_DIM), lambda h, i, j: (i, h)), kv, kv,
                  pl.BlockSpec((None, tq, 1), lambda h, i, j: (h, i, 0)),
                  pl.BlockSpec((None, 1, tk), lambda h, i, j: (h, 0, jnp.minimum(j, i)))],
        out_specs=[pl.BlockSpec((tq, HEAD_DIM), lambda h, i, j: (i, h)),
                   pl.BlockSpec((tq, HEAD_DIM), lambda h, i, j: (i, h)),
                   pl.BlockSpec((None, tq, 1), lambda h, i, j: (h, i, 0))],
        out_shape=[jax.ShapeDtypeStruct((s, w), BF16), jax.ShapeDtypeStruct((s, w), F32),
                   jax.ShapeDtypeStruct((nh, s, 1), F32)],
        scratch_shapes=[pltpu.VMEM((tq, 1), F32), pltpu.VMEM((tq, 1), F32), pltpu.VMEM((tq, HEAD_DIM), F32)],
        compiler_params=_params("parallel", "parallel", "arbitrary"),
    )(q, k, v, arow, acol)


def _attn_bwd(q, k, v, o, do, lse, arow, acol, *, dilated, name, tq=ATTN_TILE, tk=ATTN_TILE):
    s, w = q.shape
    nh = w // HEAD_DIM
    assert tq == tk
    tq = tk = min(tq, s)
    nq, nk = s // tq, s // tk
    scale = 1.0 / math.sqrt(HEAD_DIM)

    def body(q_ref, k_ref, v_ref, o_ref, do_ref, lse_ref, ar_ref, ac_ref,
             dq_ref, dk_ref, dv_ref, dac_ref, dk_acc, dv_acc, dac_acc):
        ki, qi = pl.program_id(1), pl.program_id(2)

        @pl.when((ki == 0) & (qi == 0))
        def _():
            dq_ref[...] = jnp.zeros_like(dq_ref)

        @pl.when(qi == 0)
        def _():
            dk_acc[...] = jnp.zeros_like(dk_acc)
            dv_acc[...] = jnp.zeros_like(dv_acc)
            dac_acc[...] = jnp.zeros_like(dac_acc)

        @pl.when(qi >= ki)
        def _():
            qv, kvv, dov = q_ref[...], k_ref[...], do_ref[...]
            sc = _logits(qv, kvv, ar_ref[...], ac_ref[...], qi * tq, ki * tk, dilated)
            p = jnp.exp(sc - lse_ref[...])
            dp = lax.dot_general(dov, v_ref[...], (((1,), (1,)), ((), ())), preferred_element_type=F32)
            delta = jnp.sum(dov.astype(F32) * o_ref[...].astype(F32), axis=-1, keepdims=True)
            ds = p * (dp - delta)
            dsb = ds.astype(BF16)
            dv_acc[...] += lax.dot_general(p.astype(BF16), dov, (((0,), (0,)), ((), ())), preferred_element_type=F32)
            dk_acc[...] += lax.dot_general(dsb, qv, (((0,), (0,)), ((), ())), preferred_element_type=F32)
            rows = pl.ds(pl.multiple_of(qi * tq, tq), tq)
            dq_ref[rows, :] += jnp.dot(dsb, kvv, preferred_element_type=F32) * scale
            dac_acc[...] += jnp.sum(ds, axis=0, keepdims=True)

        @pl.when(qi == nq - 1)
        def _():
            dk_ref[...] = dk_acc[...] * scale
            dv_ref[...] = dv_acc[...]
            dac_ref[...] = dac_acc[...]

    qs = pl.BlockSpec((tq, HEAD_DIM), lambda h, j, i: (jnp.maximum(i, j), h))
    ks = pl.BlockSpec((tk, HEAD_DIM), lambda h, j, i: (j, h))
    rowv = pl.BlockSpec((None, tq, 1), lambda h, j, i: (h, jnp.maximum(i, j), 0))
    colv = pl.BlockSpec((None, 1, tk), lambda h, j, i: (h, 0, j))
    return pl.pallas_call(
        body, name=name, grid=(nh, nk, nq),
        in_specs=[qs, ks, ks, qs, qs, rowv, rowv, colv],
        out_specs=[pl.BlockSpec((s, HEAD_DIM), lambda h, j, i: (0, h)), ks, ks, colv],
        out_shape=[jax.ShapeDtypeStruct((s, w), F32), jax.ShapeDtypeStruct((s, w), F32),
                   jax.ShapeDtypeStruct((s, w), F32), jax.ShapeDtypeStruct((nh, 1, s), F32)],
        scratch_shapes=[pltpu.VMEM((tk, HEAD_DIM), F32), pltpu.VMEM((tk, HEAD_DIM), F32), pltpu.VMEM((1, tk), F32)],
        compiler_params=_params("arbitrary", "arbitrary", "arbitrary"),
    )(q, k, v, o, do, lse, arow, acol)


def _gate_fwd(ga, gb, pa, pb, tm=256):
    s, d = ga.shape

    def body(ga_ref, gb_ref, pa_ref, pb_ref, o_ref):
        o_ref[...] = (jax.nn.sigmoid(ga_ref[...]) * pa_ref[...]
                      + jax.nn.sigmoid(gb_ref[...]) * pb_ref[...]).astype(o_ref.dtype)

    row = pl.BlockSpec((tm, d), lambda i: (i, 0))
    return pl.pallas_call(
        body, name="gate_fwd", grid=(s // tm,), in_specs=[row] * 4, out_specs=row,
        out_shape=jax.ShapeDtypeStruct((s, d), BF16), compiler_params=_params("parallel"),
    )(ga, gb, pa, pb)


def _gate_bwd(dm, ga, gb, pa, pb, tm=256):
    s, d = ga.shape

    def body(dm_ref, ga_ref, gb_ref, pa_ref, pb_ref, dpa_ref, dpb_ref, dga_ref, dgb_ref):
        dmv = dm_ref[...]
        for g_ref, p_ref, dp_ref, dg_ref in ((ga_ref, pa_ref, dpa_ref, dga_ref), (gb_ref, pb_ref, dpb_ref, dgb_ref)):
            sg = jax.nn.sigmoid(g_ref[...])
            dp_ref[...] = (dmv * sg).astype(BF16)
            dg_ref[...] = (dmv * p_ref[...] * (sg * (1.0 - sg))).astype(BF16)

    row = pl.BlockSpec((tm, d), lambda i: (i, 0))
    return pl.pallas_call(
        body, name="gate_bwd", grid=(s // tm,), in_specs=[row] * 5, out_specs=[row] * 4,
        out_shape=[jax.ShapeDtypeStruct((s, d), BF16)] * 4, compiler_params=_params("parallel"),
    )(dm, ga, gb, pa, pb)


def _shift_down(u, k):
    row = lax.broadcasted_iota(jnp.int32, u.shape, 0)
    return jnp.where(row >= k, pltpu.roll(u, k, 0), 0.0)


def _shift_up(u, k):
    n = u.shape[0]
    row = lax.broadcasted_iota(jnp.int32, u.shape, 0)
    return jnp.where(row < n - k, pltpu.roll(u, n - k, 0), 0.0)


def _conv3(u, wc, b):
    return wc[0:1, :] * _shift_down(u, 2) + wc[1:2, :] * _shift_down(u, 1) + wc[2:3, :] * u + b


def _conv_glu_fwd(u, wc, b, tn=256):
    s, f2 = u.shape
    f = f2 // 2
    nb = f // tn

    def body(ug_ref, uv_ref, wg_ref, wv_ref, bg_ref, bv_ref, o_ref):
        cg = _conv3(ug_ref[...], wg_ref[...], bg_ref[...])
        cv = _conv3(uv_ref[...], wv_ref[...], bv_ref[...])
        o_ref[...] = (cg * jax.nn.sigmoid(cg) * cv).astype(o_ref.dtype)

    def cols(rows, off):
        return pl.BlockSpec((rows, tn), lambda j: (0, j + off))

    return pl.pallas_call(
        body, name="conv_glu_fwd", grid=(nb,),
        in_specs=[cols(s, 0), cols(s, nb), cols(3, 0), cols(3, nb), cols(1, 0), cols(1, nb)],
        out_specs=cols(s, 0), out_shape=jax.ShapeDtypeStruct((s, f), BF16),
        compiler_params=_params("parallel"),
    )(u, u, wc, wc, b, b)


def _conv_glu_bwd(u, da, wc, b, tn=256):
    s, f2 = u.shape
    f = f2 // 2
    nb = f // tn

    def body(ug_ref, uv_ref, da_ref, wg_ref, wv_ref, bg_ref, bv_ref, dug_ref, duv_ref, sg_ref, sv_ref):
        ug, uv, wg, wv = ug_ref[...], uv_ref[...], wg_ref[...], wv_ref[...]
        cg = _conv3(ug, wg, bg_ref[...])
        cv = _conv3(uv, wv, bv_ref[...])
        sig = jax.nn.sigmoid(cg)
        dav = da_ref[...]
        dcv = dav * (cg * sig)
        dcg = dav * cv * (sig * (1.0 + cg * (1.0 - sig)))
        for dc, uu, w, du_ref, st_ref in ((dcg, ug, wg, dug_ref, sg_ref), (dcv, uv, wv, duv_ref, sv_ref)):
            du = w[2:3, :] * dc + w[1:2, :] * _shift_up(dc, 1) + w[0:1, :] * _shift_up(dc, 2)
            du_ref[...] = du.astype(BF16)
            st_ref[...] = jnp.zeros_like(st_ref)
            st_ref[0:1, :] = jnp.sum(dc * _shift_down(uu, 2), axis=0, keepdims=True)
            st_ref[1:2, :] = jnp.sum(dc * _shift_down(uu, 1), axis=0, keepdims=True)
            st_ref[2:3, :] = jnp.sum(dc * uu, axis=0, keepdims=True)
            st_ref[3:4, :] = jnp.sum(dc, axis=0, keepdims=True)

    def cols(rows, off):
        return pl.BlockSpec((rows, tn), lambda j: (0, j + off))

    return pl.pallas_call(
        body, name="conv_glu_bwd", grid=(nb,),
        in_specs=[cols(s, 0), cols(s, nb), cols(s, 0), cols(3, 0), cols(3, nb), cols(1, 0), cols(1, nb)],
        out_specs=[cols(s, 0), cols(s, 0), cols(8, 0), cols(8, 0)],
        out_shape=[jax.ShapeDtypeStruct((s, f), BF16), jax.ShapeDtypeStruct((s, f), BF16),
                   jax.ShapeDtypeStruct((8, f), F32), jax.ShapeDtypeStruct((8, f), F32)],
        compiler_params=_params("parallel"),
    )(u, u, da, wc, wc, b, b)


def _loss_head(y, target, tm=256):
    s, d = y.shape

    def body(y_ref, t_ref, dyf_ref, dyb_ref, l_ref):
        @pl.when(pl.program_id(0) == 0)
        def _():
            l_ref[...] = jnp.zeros_like(l_ref)

        err = y_ref[...] - t_ref[...]
        dy = err * (1.0 / d)
        dyf_ref[...] = dy
        dyb_ref[...] = dy.astype(BF16)
        l_ref[...] += 0.5 * jnp.sum(jnp.sum(err * err, axis=-1, keepdims=True) * (1.0 / d), axis=0, keepdims=True)

    row = pl.BlockSpec((tm, d), lambda i: (i, 0))
    return pl.pallas_call(
        body, name="loss_head", grid=(s // tm,), in_specs=[row, row],
        out_specs=[row, row, pl.BlockSpec((8, LANES), lambda i: (0, 0))],
        out_shape=[jax.ShapeDtypeStruct((s, d), F32), jax.ShapeDtypeStruct((s, d), BF16),
                   jax.ShapeDtypeStruct((8, LANES), F32)],
        compiler_params=_params("arbitrary"),
    )(y, target)


ROW_TILES = (256, 128, 64, 32, 16, 8)
BLOCK_BYTES = 1 << 20


def _add_halves(g, r1, place):
    ns, r, c = g.shape
    rh = r // 2
    tr = _pick(rh, ROW_TILES)
    g4 = g.reshape(ns, 2, rh, c)

    def body(p_ref, g_ref, r_ref, o_ref):
        o_ref[...] = (g_ref[...].astype(F32) + r_ref[...].astype(F32)).astype(o_ref.dtype)

    return pl.pallas_call(
        body, name="add_halves",
        grid_spec=pltpu.PrefetchScalarGridSpec(
            num_scalar_prefetch=1, grid=(ns, rh // tr),
            in_specs=[pl.BlockSpec((None, None, tr, c), lambda s, i, pr: (s, pr[1], i, 0)),
                      pl.BlockSpec((None, tr, c), lambda s, i, pr: (s, i, 0))],
            out_specs=pl.BlockSpec((None, tr, c), lambda s, i, pr: (s, i, 0))),
        out_shape=jax.ShapeDtypeStruct((ns, rh, c), BF16),
        compiler_params=_params("parallel", "parallel"),
    )(place, g4, r1)


def _sum_chips(g, r1, recv, place):
    ns, r, c = g.shape
    rh = r // 2
    tr = _pick(rh, ROW_TILES)
    g4 = g.reshape(ns, 2, rh, c)

    def body(p_ref, g_ref, r_ref, t0_ref, t1_ref, t2_ref, o_ref):
        own = g_ref[...].astype(F32) + r_ref[...].astype(F32)
        o_ref[...] = ((own + t0_ref[...].astype(F32)) + t1_ref[...].astype(F32)) + t2_ref[...].astype(F32)

    def peer(k):
        return pl.BlockSpec((None, tr, c), lambda i, pr: (k, i, 0))

    return pl.pallas_call(
        body, name="sum_chips",
        grid_spec=pltpu.PrefetchScalarGridSpec(
            num_scalar_prefetch=1, grid=(rh // tr,),
            in_specs=[pl.BlockSpec((None, None, tr, c), lambda i, pr: (pr[0], pr[1], i, 0)),
                      pl.BlockSpec((None, tr, c), lambda i, pr: (pr[0], i, 0)), peer(0), peer(1), peer(2)],
            out_specs=pl.BlockSpec((tr, c), lambda i, pr: (i, 0))),
        out_shape=jax.ShapeDtypeStruct((rh, c), F32),
        compiler_params=_params("parallel"),
    )(place, g4, r1, recv, recv, recv)


def _sum_devices(packs):
    n, r, c = packs.shape

    def body(p_ref, o_ref):
        acc = p_ref[0]
        for d in range(1, n):
            acc = acc + p_ref[d]
        o_ref[...] = acc

    return pl.pallas_call(
        body, name="sum_devices", out_shape=jax.ShapeDtypeStruct((r, c), F32), compiler_params=_params(),
    )(packs)


def _adamw_update(wv, gv, mv, vv):
    c1 = 1.0 - ADAM_B1 ** ADAM_STEP
    c2 = 1.0 - ADAM_B2 ** ADAM_STEP
    mn = ADAM_B1 * mv + (1.0 - ADAM_B1) * gv
    vn = ADAM_B2 * vv + (1.0 - ADAM_B2) * (gv * gv)
    m_hat = mn / c1
    v_hat = vn / c2
    return -ADAM_LR * (m_hat / (jnp.sqrt(v_hat) + ADAM_EPS) + ADAM_WD * wv), mn, vn


def _adamw(w, g, m, v, name):
    r, c = w.shape
    tr = _pick(r, ROW_TILES) if r >= 8 else r

    def body(w_ref, g_ref, m_ref, v_ref, d_ref, mo_ref, vo_ref):
        d_ref[...], mo_ref[...], vo_ref[...] = _adamw_update(w_ref[...], g_ref[...], m_ref[...], v_ref[...])

    blk = pl.BlockSpec((tr, c), lambda i: (i, 0))
    return pl.pallas_call(
        body, name=name, grid=(r // tr,), in_specs=[blk] * 4, out_specs=[blk] * 3,
        out_shape=[jax.ShapeDtypeStruct((r, c), F32)] * 3, compiler_params=_params("parallel"),
    )(w, g, m, v)


def _adamw_halves(w, mine, theirs, c_idx, m, v, name):
    r, c = w.shape
    rh = r // 2
    tr = _pick(rh, [t for t in ROW_TILES if t * c * 4 <= BLOCK_BYTES])
    nb = rh // tr

    def body(c_ref, w_ref, a_ref, b_ref, m_ref, v_ref, g_ref, d_ref, mo_ref, vo_ref):
        gv = jnp.where(pl.program_id(0) // nb == c_ref[0], a_ref[...], b_ref[...])
        g_ref[...] = gv
        d_ref[...], mo_ref[...], vo_ref[...] = _adamw_update(w_ref[...], gv, m_ref[...], v_ref[...])

    blk = pl.BlockSpec((tr, c), lambda i, cr: (i, 0))
    mine_spec = pl.BlockSpec((tr, c), lambda i, cr: (jnp.clip(i - cr[0] * nb, 0, nb - 1), 0))
    theirs_spec = pl.BlockSpec((tr, c), lambda i, cr: (jnp.clip(i - (1 - cr[0]) * nb, 0, nb - 1), 0))
    return pl.pallas_call(
        body, name=name,
        grid_spec=pltpu.PrefetchScalarGridSpec(
            num_scalar_prefetch=1, grid=(r // tr,),
            in_specs=[blk, mine_spec, theirs_spec, blk, blk], out_specs=[blk] * 4),
        out_shape=[jax.ShapeDtypeStruct((r, c), F32)] * 4, compiler_params=_params("arbitrary"),
    )(c_idx, w, mine, theirs, m, v)


ANY = pl.BlockSpec(memory_space=pl.ANY)


def _place():
    x, y, c = lax.axis_index("x"), lax.axis_index("y"), lax.axis_index("c")
    chips = [(1 - x, y), (x, 1 - y), (1 - x, 1 - y)]
    return x, y, c, chips


def _remote(src, dst, send_sem, recv_sem, to):
    return pltpu.make_async_remote_copy(src_ref=src, dst_ref=dst, send_sem=send_sem, recv_sem=recv_sem,
                                        device_id=to, device_id_type=MESH)


def _gather_weights(shards, wconv):
    nw = len(shards)

    def body(*refs):
        srcs, wc_ref = refs[:nw], refs[nw]
        outs, gc_ref = refs[nw + 1:2 * nw + 1], refs[2 * nw + 1]
        s1, r1, s2, r2 = refs[2 * nw + 2:]
        x, y, c, chips = _place()
        j = 2 * x + y
        sib = (x, y, 1 - c)

        def half(i, who):
            rh = srcs[i].shape[0] // 2
            return pl.ds(who * rh, rh)

        first = []
        for i in range(nw):
            for k, chip in enumerate(chips):
                first.append(_remote(srcs[i].at[half(i, c)], outs[i].at[j, half(i, c)], s1.at[i, k], r1.at[i, k], (*chip, c)))
        for k, chip in enumerate(chips):
            first.append(_remote(wc_ref, gc_ref.at[j], s1.at[nw, k], r1.at[nw, k], (*chip, c)))
        for cp in first:
            cp.start()
        passed = []
        for i in range(nw):
            for k, (cx, cy) in enumerate(chips):
                jk = 2 * cx + cy
                landed = outs[i].at[jk, half(i, c)]
                _remote(landed, landed, s1.at[i, k], r1.at[i, k], sib).wait_recv()
                cp = _remote(landed, landed, s2.at[i, k], r2.at[i, k], sib)
                cp.start()
                passed.append(cp)
        for k, (cx, cy) in enumerate(chips):
            landed = gc_ref.at[2 * cx + cy]
            _remote(landed, landed, s1.at[nw, k], r1.at[nw, k], sib).wait_recv()
        for i in range(nw):
            for k, (cx, cy) in enumerate(chips):
                landed = outs[i].at[2 * cx + cy, half(i, 1 - c)]
                _remote(landed, landed, s2.at[i, k], r2.at[i, k], sib).wait_recv()
        for cp in first + passed:
            cp.wait_send()

    out_shape = [jax.ShapeDtypeStruct((N_CHIPS,) + s.shape, s.dtype) for s in shards]
    out_shape.append(jax.ShapeDtypeStruct((N_CHIPS,) + wconv.shape, wconv.dtype))
    slabs = pl.pallas_call(
        body, name="gather_weights", in_specs=[ANY] * (nw + 1), out_specs=[ANY] * (nw + 1), out_shape=out_shape,
        scratch_shapes=[pltpu.SemaphoreType.DMA((nw + 1, 3)), pltpu.SemaphoreType.DMA((nw + 1, 3)),
                        pltpu.SemaphoreType.DMA((nw, 3)), pltpu.SemaphoreType.DMA((nw, 3))],
    )(*shards, wconv)
    chip = 2 * lax.axis_index("x") + lax.axis_index("y")
    return [lax.dynamic_update_slice(g, own[None], (chip, 0, 0)) for g, own in zip(slabs, [*shards, wconv])]


def _swap_halves(grads):
    nw = len(grads)

    def body(*refs):
        srcs, outs = refs[:nw], refs[nw:2 * nw]
        ssem, rsem = refs[2 * nw:]
        x, y, c, _ = _place()
        sib = (x, y, 1 - c)
        cps = []
        for i in range(nw):
            rh = srcs[i].shape[1] // 2
            cps.append(_remote(srcs[i].at[:, pl.ds((1 - c) * rh, rh)], outs[i], ssem.at[i], rsem.at[i], sib))
        for cp in cps:
            cp.start()
        for cp in cps:
            cp.wait()

    return pl.pallas_call(
        body, name="swap_halves", in_specs=[ANY] * nw, out_specs=[ANY] * nw,
        out_shape=[jax.ShapeDtypeStruct((g.shape[0], g.shape[1] // 2, g.shape[2]), g.dtype) for g in grads],
        scratch_shapes=[pltpu.SemaphoreType.DMA((nw,)), pltpu.SemaphoreType.DMA((nw,))],
    )(*grads)


def _scatter_chips(sums):
    nw = len(sums)

    def body(*refs):
        srcs, outs = refs[:nw], refs[nw:2 * nw]
        ssem, rsem = refs[2 * nw:]
        x, y, c, chips = _place()
        cps = []
        for i in range(nw):
            for k, (cx, cy) in enumerate(chips):
                cps.append(_remote(srcs[i].at[2 * cx + cy], outs[i].at[k], ssem.at[i, k], rsem.at[i, k], (cx, cy, c)))
        for cp in cps:
            cp.start()
        for cp in cps:
            cp.wait()

    return pl.pallas_call(
        body, name="scatter_chips", in_specs=[ANY] * nw, out_specs=[ANY] * nw,
        out_shape=[jax.ShapeDtypeStruct((3,) + s.shape[1:], s.dtype) for s in sums],
        scratch_shapes=[pltpu.SemaphoreType.DMA((nw, 3)), pltpu.SemaphoreType.DMA((nw, 3))],
    )(*sums)


def _join_halves(halves):
    nw = len(halves)

    def body(*refs):
        srcs, outs = refs[:nw], refs[nw:2 * nw]
        ssem, rsem = refs[2 * nw:]
        x, y, c, _ = _place()
        cps = [_remote(srcs[i], outs[i], ssem.at[i], rsem.at[i], (x, y, 1 - c)) for i in range(nw)]
        for cp in cps:
            cp.start()
        for cp in cps:
            cp.wait()

    return pl.pallas_call(
        body, name="join_halves", in_specs=[ANY] * nw, out_specs=[ANY] * nw,
        out_shape=[jax.ShapeDtypeStruct(h.shape, h.dtype) for h in halves],
        scratch_shapes=[pltpu.SemaphoreType.DMA((nw,)), pltpu.SemaphoreType.DMA((nw,))],
    )(*halves)


def _gather_packs(pack):
    def body(p_ref, o_ref, lsem, ssem, rsem):
        x, y, c, _ = _place()
        me = 4 * x + 2 * y + c
        local = pltpu.make_async_copy(p_ref, o_ref.at[me], lsem)
        local.start()
        cps = []
        for k in range(1, N_DEV):
            fx, fy, fc = (k >> 2) & 1, (k >> 1) & 1, k & 1
            to = (x ^ fx, y ^ fy, c ^ fc)
            cps.append(_remote(p_ref, o_ref.at[me], ssem.at[k - 1], rsem.at[k - 1], to))
        for cp in cps:
            cp.start()
        for k in range(1, N_DEV):
            fx, fy, fc = (k >> 2) & 1, (k >> 1) & 1, k & 1
            src = o_ref.at[4 * (x ^ fx) + 2 * (y ^ fy) + (c ^ fc)]
            _remote(src, src, ssem.at[k - 1], rsem.at[k - 1], (x, y, c)).wait_recv()
        for cp in cps:
            cp.wait_send()
        local.wait()

    return pl.pallas_call(
        body, name="gather_packs", in_specs=[ANY], out_specs=ANY,
        out_shape=jax.ShapeDtypeStruct((N_DEV,) + pack.shape, pack.dtype),
        scratch_shapes=[pltpu.SemaphoreType.DMA, pltpu.SemaphoreType.DMA((N_DEV - 1,)), pltpu.SemaphoreType.DMA((N_DEV - 1,))],
    )(pack)


LANE_TILES = (512, 896, 1408, 704, 384, 256, 128)


def _layer_grads(x, target, small, wg):
    s, d = x.shape
    w_att = N_HEADS * HEAD_DIM
    in_splits = (w_att, w_att, w_att, N_HEADS, w_att, w_att, w_att, d, d)
    in_cols = sum(in_splits)
    cs = in_cols // N_CHIPS
    cp = wg["in"].shape[2]
    tm = min(s, 1024)
    t_in = _pick(cp, LANE_TILES)
    t_d = _pick(d, LANE_TILES)
    t_dq = _pick(d // N_CHIPS, LANE_TILES)
    t_w = _pick(w_att, LANE_TILES)
    t_up = _pick(wg["up"].shape[2], LANE_TILES)
    f = wg["down"].shape[1] * N_CHIPS
    t_f = _pick(f, LANE_TILES)
    t_fq = _pick(f // N_CHIPS, LANE_TILES)
    offs = np.cumsum(in_splits)[:-1].tolist()

    h1 = _norm_fwd(x, small["g_attn"], group=d, name="rms1_fwd")
    proj_p = _mm(h1, wg["in"], mode="nn", b_kind="col", tm=tm, tn=t_in, tk=t_d, name="mm_in")
    proj = proj_p.reshape(s, N_CHIPS, cp)[:, :, :cs].reshape(s, in_cols)
    qa, ka, va, fa, qb, kb, vb, ga, gb = jnp.split(proj, offs, axis=-1)
    gains = {n: small[n].reshape(1, w_att) for n in ("g_q_fox", "g_k_fox", "g_q_dil", "g_k_dil")}
    qa_n = _norm_fwd(qa, gains["g_q_fox"], group=HEAD_DIM, name="qnorm_fox")
    ka_n = _norm_fwd(ka, gains["g_k_fox"], group=HEAD_DIM, name="knorm_fox")
    qb_n = _norm_fwd(qb, gains["g_q_dil"], group=HEAD_DIM, name="qnorm_dil")
    kb_n = _norm_fwd(kb, gains["g_k_dil"], group=HEAD_DIM, name="knorm_dil")
    va_b, vb_b = va.astype(BF16), vb.astype(BF16)
    fa_t = fa.T
    b_f = small["b_forget"].reshape(N_HEADS, 1)
    c_f = _forget_fwd(fa_t, b_f)
    slopes = jnp.asarray(2.0 ** (-8.0 * np.arange(1, N_HEADS + 1) / N_HEADS), dtype=F32)
    a_d = -(slopes[:, None] * jnp.arange(s, dtype=F32)[None, :])
    rows_f, cols_f = c_f[:, :, None], c_f[:, None, :]
    rows_d, cols_d = a_d[:, :, None], a_d[:, None, :]
    o_a, o_a32, lse_a = _attn_fwd(qa_n, ka_n, va_b, rows_f, cols_f, dilated=False, name="attn_fox_fwd")
    o_b, o_b32, lse_b = _attn_fwd(qb_n, kb_n, vb_b, rows_d, cols_d, dilated=True, name="attn_dil_fwd")
    pa = _mm(o_a, wg["brf"], mode="nn", b_kind="col", tm=tm, tn=t_dq, tk=t_w, name="mm_brf")
    pb = _mm(o_b, wg["brd"], mode="nn", b_kind="col", tm=tm, tn=t_dq, tk=t_w, name="mm_brd")
    merged = _gate_fwd(ga, gb, pa, pb)
    x1 = _mm(merged, wg["out"], mode="nn", b_kind="row", res=x, tm=tm, tn=t_d, tk=t_dq, name="mm_out")
    h2 = _norm_fwd(x1, small["g_ffn"], group=d, name="rms2_fwd")
    u = _mm(h2, wg["up"], mode="nn", b_kind="col", tm=tm, tn=t_up, tk=t_d, name="mm_up")
    act = _conv_glu_fwd(u, wg["conv"], wg["bconv"])
    y = _mm(act, wg["down"], mode="nn", b_kind="row", res=x1, tm=tm, tn=t_d, tk=t_fq, name="mm_down")
    dy_f, dy_b, loss_blk = _loss_head(y, target)

    grads = {}
    d_act = _mm(dy_b, wg["down"], mode="nt", b_kind="row", tm=tm, tn=t_fq, tk=t_d, name="mm_down_dx")
    grads["down"] = _mm(act, dy_b, mode="tn", out_dtype=BF16, out_kind="row", tm=t_fq, tn=t_d, tk=tm, name="mm_down_dw")
    du_g, du_v, st_g, st_v = _conv_glu_bwd(u, d_act, wg["conv"], wg["bconv"])
    du = jnp.concatenate([du_g, du_v], axis=1)
    grads["up"] = _mm(h2, du, mode="tn", out_dtype=BF16, out_kind="col", tm=t_d, tn=t_up, tk=tm, name="mm_up_dw")
    dh2 = _mm(du, wg["up"], mode="nt", b_kind="col", tm=tm, tn=t_d, tk=t_up, name="mm_up_dx")
    dx1_b, dx1_f, dg_ffn = _norm_bwd(dh2, x1, small["g_ffn"], group=d, res=dy_f, out_dtypes=(BF16, F32), name="rms2_bwd")
    d_merged = _mm(dx1_b, wg["out"], mode="nt", b_kind="row", tm=tm, tn=t_dq, tk=t_d, name="mm_out_dx")
    grads["out"] = _mm(merged, dx1_b, mode="tn", out_dtype=BF16, out_kind="row", tm=t_dq, tn=t_d, tk=tm, name="mm_out_dw")
    dpa, dpb, dga, dgb = _gate_bwd(d_merged, ga, gb, pa, pb)
    do_a = _mm(dpa, wg["brf"], mode="nt", b_kind="col", out_dtype=BF16, tm=tm, tn=t_w, tk=t_dq, name="mm_brf_dx")
    do_b = _mm(dpb, wg["brd"], mode="nt", b_kind="col", out_dtype=BF16, tm=tm, tn=t_w, tk=t_dq, name="mm_brd_dx")
    grads["brf"] = _mm(o_a, dpa, mode="tn", out_dtype=BF16, out_kind="col", tm=t_w, tn=t_dq, tk=tm, name="mm_brf_dw")
    grads["brd"] = _mm(o_b, dpb, mode="tn", out_dtype=BF16, out_kind="col", tm=t_w, tn=t_dq, tk=tm, name="mm_brd_dw")
    dqa_n, dka_n, dva, dac_a = _attn_bwd(qa_n, ka_n, va_b, o_a32, do_a, lse_a, rows_f, cols_f, dilated=False, name="attn_fox_bwd")
    dqb_n, dkb_n, dvb, _ = _attn_bwd(qb_n, kb_n, vb_b, o_b32, do_b, lse_b, rows_d, cols_d, dilated=True, name="attn_dil_bwd")
    dqa, dg_qf = _norm_bwd(dqa_n, qa, gains["g_q_fox"], group=HEAD_DIM, name="qnorm_fox_bwd")
    dka, dg_kf = _norm_bwd(dka_n, ka, gains["g_k_fox"], group=HEAD_DIM, name="knorm_fox_bwd")
    dqb, dg_qd = _norm_bwd(dqb_n, qb, gains["g_q_dil"], group=HEAD_DIM, name="qnorm_dil_bwd")
    dkb, dg_kd = _norm_bwd(dkb_n, kb, gains["g_k_dil"], group=HEAD_DIM, name="knorm_dil_bwd")
    dfa_t, db_f = _forget_bwd(dac_a[:, 0, :], fa_t, b_f)
    dproj = jnp.concatenate([dqa, dka, dva.astype(BF16), dfa_t.T.astype(BF16), dqb, dkb, dvb.astype(BF16), dga, dgb], axis=1)
    dproj_p = jnp.pad(dproj.reshape(s, N_CHIPS, cs), ((0, 0), (0, 0), (0, cp - cs))).reshape(s, N_CHIPS * cp)
    grads["in"] = _mm(h1, dproj_p, mode="tn", out_dtype=BF16, out_kind="col", tm=t_d, tn=t_in, tk=tm, name="mm_in_dw")
    dh1 = _mm(dproj_p, wg["in"], mode="nt", b_kind="col", tm=tm, tn=t_d, tk=t_in, name="mm_in_dx")
    grad_x, dg_attn = _norm_bwd(dh1, x, small["g_attn"], group=d, res=dx1_f, out_dtypes=(F32,), name="rms1_bwd")

    small_grads = {
        "g_attn": dg_attn, "b_forget": db_f.reshape(1, N_HEADS),
        "g_q_fox": dg_qf, "g_k_fox": dg_kf, "g_q_dil": dg_qd, "g_k_dil": dg_kd, "g_ffn": dg_ffn,
        "w_conv": jnp.concatenate([st_g[0:3], st_v[0:3]], axis=1),
        "b_conv": jnp.concatenate([st_g[3:4], st_v[3:4]], axis=1),
        "loss": loss_blk[0:1, 0:1],
    }
    return grads, small_grads, grad_x


SMALL_ORDER = ("g_attn", "b_forget", "g_q_fox", "g_k_fox", "g_q_dil", "g_k_dil", "g_ffn", "w_conv", "b_conv", "loss")
WEIGHT_ORDER = ("g_attn", "w_in", "b_forget", "g_q_fox", "g_k_fox", "g_q_dil", "g_k_dil", "w_br_fox", "w_br_dil",
                "w_out", "g_ffn", "w_up", "w_conv", "b_conv", "w_down")
BIG = {"w_in": "in", "w_br_fox": "brf", "w_br_dil": "brd", "w_out": "out", "w_up": "up", "w_down": "down"}


def kernel(x, g_attn, w_in, b_forget, g_q_fox, g_k_fox, g_q_dil, g_k_dil, w_br_fox, w_br_dil, w_out, g_ffn, w_up, w_conv, b_conv, w_down, loss_target, m_g_attn, m_w_in, m_b_forget, m_g_q_fox, m_g_k_fox, m_g_q_dil, m_g_k_dil, m_w_br_fox, m_w_br_dil, m_w_out, m_g_ffn, m_w_up, m_w_conv, m_b_conv, m_w_down, v_g_attn, v_w_in, v_b_forget, v_g_q_fox, v_g_k_fox, v_g_q_dil, v_g_k_dil, v_w_br_fox, v_w_br_dil, v_w_out, v_g_ffn, v_w_up, v_w_conv, v_b_conv, v_w_down):
    w = dict(g_attn=g_attn, w_in=w_in, b_forget=b_forget, g_q_fox=g_q_fox, g_k_fox=g_k_fox, g_q_dil=g_q_dil,
             g_k_dil=g_k_dil, w_br_fox=w_br_fox, w_br_dil=w_br_dil, w_out=w_out, g_ffn=g_ffn, w_up=w_up,
             w_conv=w_conv, b_conv=b_conv, w_down=w_down)
    m = dict(g_attn=m_g_attn, w_in=m_w_in, b_forget=m_b_forget, g_q_fox=m_g_q_fox, g_k_fox=m_g_k_fox,
             g_q_dil=m_g_q_dil, g_k_dil=m_g_k_dil, w_br_fox=m_w_br_fox, w_br_dil=m_w_br_dil, w_out=m_w_out,
             g_ffn=m_g_ffn, w_up=m_w_up, w_conv=m_w_conv, b_conv=m_b_conv, w_down=m_w_down)
    v = dict(g_attn=v_g_attn, w_in=v_w_in, b_forget=v_b_forget, g_q_fox=v_g_q_fox, g_k_fox=v_g_k_fox,
             g_q_dil=v_g_q_dil, g_k_dil=v_g_k_dil, w_br_fox=v_w_br_fox, w_br_dil=v_w_br_dil, w_out=v_w_out,
             g_ffn=v_g_ffn, w_up=v_w_up, w_conv=v_w_conv, b_conv=v_b_conv, w_down=v_w_down)
    xi, yi, ci = lax.axis_index("x"), lax.axis_index("y"), lax.axis_index("c")
    chip = (2 * xi + yi).astype(jnp.int32)
    c_idx = ci.astype(jnp.int32).reshape(1)
    j_idx = chip.reshape(1)

    cs = w_in.shape[2]
    cp = _round_up(cs, LANES)
    shards = {
        "in": jnp.pad(w_in[0].astype(BF16), ((0, 0), (0, cp - cs))),
        "brf": w_br_fox[0].astype(BF16), "brd": w_br_dil[0].astype(BF16), "out": w_out[0].astype(BF16),
        "up": w_up[0].astype(BF16), "down": w_down[0].astype(BF16),
    }
    names = tuple(shards)
    conv_pad = jnp.pad(w_conv[0], ((0, 8 - w_conv.shape[1]), (0, 0)))
    gathered = _gather_weights([shards[n] for n in names], conv_pad)
    wg = dict(zip(names, gathered[:-1]))
    conv_all = gathered[-1]
    wg["conv"] = jnp.transpose(conv_all[:, :w_conv.shape[1], :], (1, 0, 2)).reshape(w_conv.shape[1], -1)
    wg["bconv"] = b_conv
    small = {n: w[n] for n in ("g_attn", "b_forget", "g_q_fox", "g_k_fox", "g_q_dil", "g_k_dil", "g_ffn")}
    small = {n: (a[0] if a.ndim == 3 else a) for n, a in small.items()}

    grads, small_grads, grad_x = _layer_grads(x[0], loss_target[0], small, wg)

    glist = [grads[n] for n in names]
    place = jnp.stack([chip, ci.astype(jnp.int32)])
    from_sibling = _swap_halves(glist)
    chip_sums = [_add_halves(g, r, place) for g, r in zip(glist, from_sibling)]
    from_chips = _scatter_chips(chip_sums)
    halves = [_sum_chips(g, r, rc, place) for g, r, rc in zip(glist, from_sibling, from_chips)]
    mine = dict(zip(names, halves))
    theirs = dict(zip(names, _join_halves(halves)))
    mine["in"], theirs["in"] = mine["in"][:, :cs], theirs["in"][:, :cs]

    flat = jnp.concatenate([small_grads[n].reshape(-1) for n in SMALL_ORDER])
    rows = _round_up(flat.shape[0], 8 * LANES) // LANES
    pack = jnp.pad(flat, (0, rows * LANES - flat.shape[0])).reshape(rows, LANES)
    total = _sum_devices(_gather_packs(pack)).reshape(-1)
    red, at = {}, 0
    for n in SMALL_ORDER:
        size = small_grads[n].size
        red[n] = total[at:at + size].reshape(small_grads[n].shape)
        at += size
    loss = red["loss"].reshape(())
    c2 = w_conv.shape[2]
    red["w_conv"] = lax.dynamic_slice_in_dim(red["w_conv"], chip * c2, c2, axis=1)

    g_out, d_out, m_out, v_out = {}, {}, {}, {}
    for n in WEIGHT_ORDER:
        shape = w[n].shape
        r2 = (shape[-2], shape[-1]) if n not in ("g_attn", "b_forget", "g_ffn", "b_conv") else (1, shape[-1])
        if n in BIG:
            g2, dl, mn, vn = _adamw_halves(w[n].reshape(r2), mine[BIG[n]], theirs[BIG[n]], c_idx,
                                           m[n].reshape(r2), v[n].reshape(r2), name="adamw_" + n)
        else:
            g2 = red[n].reshape(r2)
            dl, mn, vn = _adamw(w[n].reshape(r2), g2, m[n].reshape(r2), v[n].reshape(r2), name="adamw_" + n)
        g_out[n], d_out[n], m_out[n], v_out[n] = (a.reshape(shape) for a in (g2, dl, mn, vn))

    return (loss, grad_x[None], *[g_out[n] for n in WEIGHT_ORDER], *[d_out[n] for n in WEIGHT_ORDER],
            *[m_out[n] for n in WEIGHT_ORDER], *[v_out[n] for n in WEIGHT_ORDER])
```

```python
import functools
import math

import jax
import jax.numpy as jnp
import numpy as np
from jax import lax
from jax.experimental import pallas as pl
from jax.experimental.pallas import tpu as pltpu

F32 = jnp.float32
BF16 = jnp.bfloat16
HEAD_DIM = 128
N_HEADS = 8
EPS = 1e-6
NEG = -1e30
N_CHIPS = 4
N_DEV = 8
LANES = 128
VMEM_LIMIT_BYTES = 56 * 1024 * 1024
DIL_PATTERNS = ((128, 1), (512, 4), (2048, 16))
ATTN_TILE = 512
ADAM_LR, ADAM_B1, ADAM_B2, ADAM_EPS, ADAM_WD, ADAM_STEP = 0.001, 0.9, 0.999, 1e-08, 0.01, 10
MESH = pl.DeviceIdType.MESH


def _params(*sem):
    return pltpu.CompilerParams(dimension_semantics=sem, vmem_limit_bytes=VMEM_LIMIT_BYTES)


def _round_up(n, m):
    return -(-n // m) * m


def _pick(dim, prefs):
    for p in prefs:
        if dim % p == 0:
            return p
    raise ValueError(f"no tile for {dim} in {prefs}")


def _logical_shape(arr, kind):
    if kind is None:
        return arr.shape
    s, r, c = arr.shape
    return (r, s * c) if kind == "col" else (s * r, c)


def _spec(shape, kind, br, bc, fi, fj):
    if kind is None:
        return pl.BlockSpec((br, bc), lambda *g: (fi(*g), fj(*g)))
    _, r, c = shape
    if kind == "col":
        nb = c // bc
        assert nb * bc == c, (shape, bc)
        return pl.BlockSpec((None, br, bc), lambda *g: (fj(*g) // nb, fi(*g), fj(*g) % nb))
    nb = r // br
    assert nb * br == r, (shape, br)
    return pl.BlockSpec((None, br, bc), lambda *g: (fi(*g) // nb, fi(*g) % nb, fj(*g)))


def _mm(a, b, *, mode, tm, tn, tk, name, a_kind=None, b_kind=None, out_kind=None,
        out_dtype=F32, res=None):
    la, lb = _logical_shape(a, a_kind), _logical_shape(b, b_kind)
    if mode == "nn":
        (m, k), (k2, n) = la, lb
    elif mode == "nt":
        (m, k), (n, k2) = la, lb
    else:
        (k, m), (k2, n) = la, lb
    assert k == k2, (name, la, lb)
    assert m % tm == 0 and n % tn == 0 and k % tk == 0, (name, m, n, k, tm, tn, tk)
    nk = k // tk
    im = lambda i, j, l: i
    jn = lambda i, j, l: j
    lk = lambda i, j, l: l
    if mode == "tn":
        a_spec = _spec(a.shape, a_kind, tk, tm, lk, im)
        dims = (((0,), (0,)), ((), ()))
    else:
        a_spec = _spec(a.shape, a_kind, tm, tk, im, lk)
        dims = (((1,), (1,)), ((), ())) if mode == "nt" else (((1,), (0,)), ((), ()))
    if mode == "nt":
        b_spec = _spec(b.shape, b_kind, tn, tk, jn, lk)
    else:
        b_spec = _spec(b.shape, b_kind, tk, tn, lk, jn)
    if out_kind is None:
        oshape = (m, n)
    elif out_kind == "col":
        oshape = (N_CHIPS, m, n // N_CHIPS)
    else:
        oshape = (N_CHIPS, m // N_CHIPS, n)
    o_spec = _spec(oshape, out_kind, tm, tn, im, jn)
    in_specs = [a_spec, b_spec]
    args = [a, b]
    if res is not None:
        in_specs.append(pl.BlockSpec((tm, tn), lambda i, j, l: (i, j)))
        args.append(res)

    def body(*refs):
        a_ref, b_ref = refs[0], refs[1]
        res_ref = refs[2] if res is not None else None
        o_ref, acc_ref = refs[-2], refs[-1]
        step = pl.program_id(2)

        @pl.when(step == 0)
        def _():
            acc_ref[...] = jnp.zeros_like(acc_ref)

        acc_ref[...] += lax.dot_general(a_ref[...], b_ref[...], dims, preferred_element_type=F32)

        @pl.when(step == nk - 1)
        def _():
            out = acc_ref[...]
            if res_ref is not None:
                out = out + res_ref[...]
            o_ref[...] = out.astype(o_ref.dtype)

    return pl.pallas_call(
        body, name=name, grid=(m // tm, n // tn, nk),
        in_specs=in_specs, out_specs=o_spec,
        out_shape=jax.ShapeDtypeStruct(oshape, out_dtype),
        scratch_shapes=[pltpu.VMEM((tm, tn), F32)],
        compiler_params=_params("parallel", "parallel", "arbitrary"),
    )(*args)


def _norm_fwd(x, g, *, group, name, tm=256):
    s, w = x.shape
    ng = w // group

    def body(x_ref, g_ref, o_ref):
        for i in range(ng):
            cols = slice(i * group, (i + 1) * group)
            xv = x_ref[:, cols]
            r = lax.rsqrt(jnp.mean(xv * xv, axis=-1, keepdims=True) + EPS)
            o_ref[:, cols] = ((xv * r) * g_ref[:, cols]).astype(o_ref.dtype)

    return pl.pallas_call(
        body, name=name, grid=(s // tm,),
        in_specs=[pl.BlockSpec((tm, w), lambda i: (i, 0)), pl.BlockSpec((1, w), lambda i: (0, 0))],
        out_specs=pl.BlockSpec((tm, w), lambda i: (i, 0)),
        out_shape=jax.ShapeDtypeStruct((s, w), BF16),
        compiler_params=_params("parallel"),
    )(x, g)


def _norm_bwd(dy, x, g, *, group, name, res=None, out_dtypes=(BF16,), tm=256):
    s, w = x.shape
    ng = w // group
    n_in = 4 if res is not None else 3

    def body(*refs):
        dy_ref, x_ref, g_ref = refs[:3]
        res_ref = refs[3] if res is not None else None
        outs = refs[n_in:]
        dx_refs, dg_ref = outs[:-1], outs[-1]

        @pl.when(pl.program_id(0) == 0)
        def _():
            dg_ref[...] = jnp.zeros_like(dg_ref)

        for i in range(ng):
            cols = slice(i * group, (i + 1) * group)
            xv = x_ref[:, cols]
            dyv = dy_ref[:, cols].astype(F32)
            r = lax.rsqrt(jnp.mean(xv * xv, axis=-1, keepdims=True) + EPS)
            xr = xv * r
            dg_ref[:, cols] += jnp.sum(dyv * xr, axis=0, keepdims=True)
            gdy = dyv * g_ref[:, cols]
            dx = r * (gdy - xr * jnp.mean(gdy * xr, axis=-1, keepdims=True))
            if res_ref is not None:
                dx = dx + res_ref[:, cols]
            for dx_ref in dx_refs:
                dx_ref[:, cols] = dx.astype(dx_ref.dtype)

    row = pl.BlockSpec((tm, w), lambda i: (i, 0))
    vec = pl.BlockSpec((1, w), lambda i: (0, 0))
    in_specs = [row, row, vec] + ([row] if res is not None else [])
    args = [dy, x, g] + ([res] if res is not None else [])
    out_specs = [row] * len(out_dtypes) + [vec]
    out_shape = [jax.ShapeDtypeStruct((s, w), dt) for dt in out_dtypes] + [jax.ShapeDtypeStruct((1, w), F32)]
    return pl.pallas_call(
        body, name=name, grid=(s // tm,), in_specs=in_specs, out_specs=out_specs,
        out_shape=out_shape, compiler_params=_params("arbitrary"),
    )(*args)


def _split3(v):
    p1 = v.astype(BF16)
    r1 = v - p1.astype(F32)
    p2 = r1.astype(BF16)
    p3 = (r1 - p2.astype(F32)).astype(BF16)
    return p1, p2, p3


def _tri_sum(v, reverse, tcol=512):
    h, s = v.shape
    tcol = min(tcol, s)
    parts = _split3(v)
    outs = []
    for j in range(s // tcol):
        src = lax.broadcasted_iota(jnp.int32, (s, tcol), 0)
        dst = lax.broadcasted_iota(jnp.int32, (s, tcol), 1) + j * tcol
        keep = (src >= dst) if reverse else (src <= dst)
        tri = jnp.where(keep, 1.0, 0.0).astype(BF16)
        acc = jnp.zeros((h, tcol), F32)
        for p in parts:
            acc = acc + jnp.dot(p, tri, preferred_element_type=F32)
        outs.append(acc)
    return outs


def _forget_fwd(fa_t, b):
    h, s = fa_t.shape
    tcol = min(512, s)

    def body(f_ref, b_ref, c_ref):
        z = f_ref[...] + b_ref[...]
        logf = jnp.minimum(z, 0.0) - jnp.log(1.0 + jnp.exp(-jnp.abs(z)))
        for j, blk in enumerate(_tri_sum(logf, reverse=False, tcol=tcol)):
            c_ref[:, j * tcol:(j + 1) * tcol] = blk

    return pl.pallas_call(
        body, name="forget_fwd", out_shape=jax.ShapeDtypeStruct((h, s), F32),
        compiler_params=_params(),
    )(fa_t, b)


def _forget_bwd(dacol, fa_t, b):
    h, s = fa_t.shape
    tcol = min(512, s)

    def body(d_ref, f_ref, b_ref, dfa_ref, db_ref):
        z = f_ref[...] + b_ref[...]
        dc = -d_ref[...]
        total = jnp.zeros((h, 1), F32)
        for j, blk in enumerate(_tri_sum(dc, reverse=True, tcol=tcol)):
            cols = slice(j * tcol, (j + 1) * tcol)
            dfa = blk * (1.0 - jax.nn.sigmoid(z[:, cols]))
            dfa_ref[:, cols] = dfa
            total = total + jnp.sum(dfa, axis=-1, keepdims=True)
        db_ref[...] = total

    return pl.pallas_call(
        body, name="forget_bwd",
        out_shape=[jax.ShapeDtypeStruct((h, s), F32), jax.ShapeDtypeStruct((h, 1), F32)],
        compiler_params=_params(),
    )(dacol, fa_t, b)


def _logits(q, k, arow, acol, q0, k0, dilated):
    tq, tk = q.shape[0], k.shape[0]
    s = lax.dot_general(q, k, (((1,), (1,)), ((), ())), preferred_element_type=F32)
    s = s * (1.0 / math.sqrt(HEAD_DIM)) + arow - acol
    dist = (q0 + lax.broadcasted_iota(jnp.int32, (tq, tk), 0)) - (k0 + lax.broadcasted_iota(jnp.int32, (tq, tk), 1))
    valid = dist >= 0
    if dilated:
        mult = jnp.zeros((tq, tk), jnp.int32)
        for window, dil in DIL_PATTERNS:
            mult = mult + ((dist <= window) & ((dist & (dil - 1)) == 0)).astype(jnp.int32)
        s = s + jnp.where(mult == 3, math.log(3.0), jnp.where(mult == 2, math.log(2.0), 0.0))
        valid = valid & (mult > 0)
    return jnp.where(valid, s, NEG)


def _attn_fwd(q, k, v, arow, acol, *, dilated, name, tq=ATTN_TILE, tk=ATTN_TILE):
    s, w = q.shape
    nh = w // HEAD_DIM
    assert tq == tk
    tq = tk = min(tq, s)
    nq, nk = s // tq, s // tk

    def body(q_ref, k_ref, v_ref, ar_ref, ac_ref, o_ref, of_ref, lse_ref, m_ref, l_ref, acc_ref):
        qi, ki = pl.program_id(1), pl.program_id(2)

        @pl.when(ki == 0)
        def _():
            m_ref[...] = jnp.full_like(m_ref, NEG)
            l_ref[...] = jnp.zeros_like(l_ref)
            acc_ref[...] = jnp.zeros_like(acc_ref)

        @pl.when(ki <= qi)
        def _():
            sc = _logits(q_ref[...], k_ref[...], ar_ref[...], ac_ref[...], qi * tq, ki * tk, dilated)
            m_new = jnp.maximum(m_ref[...], jnp.max(sc, axis=-1, keepdims=True))
            alpha = jnp.exp(m_ref[...] - m_new)
            p = jnp.exp(sc - m_new)
            l_ref[...] = alpha * l_ref[...] + jnp.sum(p, axis=-1, keepdims=True)
            p_hi = p.astype(BF16)
            p_lo = (p - p_hi.astype(F32)).astype(BF16)
            vv = v_ref[...]
            acc_ref[...] = (alpha * acc_ref[...] + jnp.dot(p_hi, vv, preferred_element_type=F32)
                            + jnp.dot(p_lo, vv, preferred_element_type=F32))
            m_ref[...] = m_new

        @pl.when(ki == nk - 1)
        def _():
            out = acc_ref[...] / l_ref[...]
            o_ref[...] = out.astype(o_ref.dtype)
            of_ref[...] = out
            lse_ref[...] = m_ref[...] + jnp.log(l_ref[...])

    kv = pl.BlockSpec((tk, HEAD_DIM), lambda h, i, j: (jnp.minimum(j, i), h))
    return pl.pallas_call(
        body, name=name, grid=(nh, nq, nk),
        in_specs=[pl.BlockSpec((tq, HEAD_DIM), lambda h, i, j: (i, h)), kv, kv,
                  pl.BlockSpec((None, tq, 1), lambda h, i, j: (h, i, 0)),
                  pl.BlockSpec((None, 1, tk), lambda h, i, j: (h, 0, jnp.minimum(j, i)))],
        out_specs=[pl.BlockSpec((tq, HEAD_DIM), lambda h, i, j: (i, h)),
                   pl.BlockSpec((tq, HEAD_DIM), lambda h, i, j: (i, h)),
                   pl.BlockSpec((None, tq, 1), lambda h, i, j: (h, i, 0))],
        out_shape=[jax.ShapeDtypeStruct((s, w), BF16), jax.ShapeDtypeStruct((s, w), F32),
                   jax.ShapeDtypeStruct((nh, s, 1), F32)],
        scratch_shapes=[pltpu.VMEM((tq, 1), F32), pltpu.VMEM((tq, 1), F32), pltpu.VMEM((tq, HEAD_DIM), F32)],
        compiler_params=_params("parallel", "parallel", "arbitrary"),
    )(q, k, v, arow, acol)


def _attn_bwd(q, k, v, o, do, lse, arow, acol, *, dilated, name, tq=ATTN_TILE, tk=ATTN_TILE):
    s, w = q.shape
    nh = w // HEAD_DIM
    assert tq == tk
    tq = tk = min(tq, s)
    nq, nk = s // tq, s // tk
    scale = 1.0 / math.sqrt(HEAD_DIM)

    def body(q_ref, k_ref, v_ref, o_ref, do_ref, lse_ref, ar_ref, ac_ref,
             dq_ref, dk_ref, dv_ref, dac_ref, dk_acc, dv_acc, dac_acc):
        ki, qi = pl.program_id(1), pl.program_id(2)

        @pl.when((ki == 0) & (qi == 0))
        def _():
            dq_ref[...] = jnp.zeros_like(dq_ref)

        @pl.when(qi == 0)
        def _():
            dk_acc[...] = jnp.zeros_like(dk_acc)
            dv_acc[...] = jnp.zeros_like(dv_acc)
            dac_acc[...] = jnp.zeros_like(dac_acc)

        @pl.when(qi >= ki)
        def _():
            qv, kvv, dov = q_ref[...], k_ref[...], do_ref[...]
            sc = _logits(qv, kvv, ar_ref[...], ac_ref[...], qi * tq, ki * tk, dilated)
            p = jnp.exp(sc - lse_ref[...])
            dp = lax.dot_general(dov, v_ref[...], (((1,), (1,)), ((), ())), preferred_element_type=F32)
            delta = jnp.sum(dov.astype(F32) * o_ref[...].astype(F32), axis=-1, keepdims=True)
            ds = p * (dp - delta)
            dsb = ds.astype(BF16)
            dv_acc[...] += lax.dot_general(p.astype(BF16), dov, (((0,), (0,)), ((), ())), preferred_element_type=F32)
            dk_acc[...] += lax.dot_general(dsb, qv, (((0,), (0,)), ((), ())), preferred_element_type=F32)
            rows = pl.ds(pl.multiple_of(qi * tq, tq), tq)
            dq_ref[rows, :] += jnp.dot(dsb, kvv, preferred_element_type=F32) * scale
            dac_acc[...] += jnp.sum(ds, axis=0, keepdims=True)

        @pl.when(qi == nq - 1)
        def _():
            dk_ref[...] = dk_acc[...] * scale
            dv_ref[...] = dv_acc[...]
            dac_ref[...] = dac_acc[...]

    qs = pl.BlockSpec((tq, HEAD_DIM), lambda h, j, i: (jnp.maximum(i, j), h))
    ks = pl.BlockSpec((tk, HEAD_DIM), lambda h, j, i: (j, h))
    rowv = pl.BlockSpec((None, tq, 1), lambda h, j, i: (h, jnp.maximum(i, j), 0))
    colv = pl.BlockSpec((None, 1, tk), lambda h, j, i: (h, 0, j))
    return pl.pallas_call(
        body, name=name, grid=(nh, nk, nq),
        in_specs=[qs, ks, ks, qs, qs, rowv, rowv, colv],
        out_specs=[pl.BlockSpec((s, HEAD_DIM), lambda h, j, i: (0, h)), ks, ks, colv],
        out_shape=[jax.ShapeDtypeStruct((s, w), F32), jax.ShapeDtypeStruct((s, w), F32),
                   jax.ShapeDtypeStruct((s, w), F32), jax.ShapeDtypeStruct((nh, 1, s), F32)],
        scratch_shapes=[pltpu.VMEM((tk, HEAD_DIM), F32), pltpu.VMEM((tk, HEAD_DIM), F32), pltpu.VMEM((1, tk), F32)],
        compiler_params=_params("arbitrary", "arbitrary", "arbitrary"),
    )(q, k, v, o, do, lse, arow, acol)


def _gate_fwd(ga, gb, pa, pb, tm=256):
    s, d = ga.shape

    def body(ga_ref, gb_ref, pa_ref, pb_ref, o_ref):
        o_ref[...] = (jax.nn.sigmoid(ga_ref[...]) * pa_ref[...]
                      + jax.nn.sigmoid(gb_ref[...]) * pb_ref[...]).astype(o_ref.dtype)

    row = pl.BlockSpec((tm, d), lambda i: (i, 0))
    return pl.pallas_call(
        body, name="gate_fwd", grid=(s // tm,), in_specs=[row] * 4, out_specs=row,
        out_shape=jax.ShapeDtypeStruct((s, d), BF16), compiler_params=_params("parallel"),
    )(ga, gb, pa, pb)


def _gate_bwd(dm, ga, gb, pa, pb, tm=256):
    s, d = ga.shape

    def body(dm_ref, ga_ref, gb_ref, pa_ref, pb_ref, dpa_ref, dpb_ref, dga_ref, dgb_ref):
        dmv = dm_ref[...]
        for g_ref, p_ref, dp_ref, dg_ref in ((ga_ref, pa_ref, dpa_ref, dga_ref), (gb_ref, pb_ref, dpb_ref, dgb_ref)):
            sg = jax.nn.sigmoid(g_ref[...])
            dp_ref[...] = (dmv * sg).astype(BF16)
            dg_ref[...] = (dmv * p_ref[...] * (sg * (1.0 - sg))).astype(BF16)

    row = pl.BlockSpec((tm, d), lambda i: (i, 0))
    return pl.pallas_call(
        body, name="gate_bwd", grid=(s // tm,), in_specs=[row] * 5, out_specs=[row] * 4,
        out_shape=[jax.ShapeDtypeStruct((s, d), BF16)] * 4, compiler_params=_params("parallel"),
    )(dm, ga, gb, pa, pb)


def _shift_down(u, k):
    row = lax.broadcasted_iota(jnp.int32, u.shape, 0)
    return jnp.where(row >= k, pltpu.roll(u, k, 0), 0.0)


def _shift_up(u, k):
    n = u.shape[0]
    row = lax.broadcasted_iota(jnp.int32, u.shape, 0)
    return jnp.where(row < n - k, pltpu.roll(u, n - k, 0), 0.0)


def _conv3(u, wc, b):
    return wc[0:1, :] * _shift_down(u, 2) + wc[1:2, :] * _shift_down(u, 1) + wc[2:3, :] * u + b


def _conv_glu_fwd(u, wc, b, tn=256):
    s, f2 = u.shape
    f = f2 // 2
    nb = f // tn

    def body(ug_ref, uv_ref, wg_ref, wv_ref, bg_ref, bv_ref, o_ref):
        cg = _conv3(ug_ref[...], wg_ref[...], bg_ref[...])
        cv = _conv3(uv_ref[...], wv_ref[...], bv_ref[...])
        o_ref[...] = (cg * jax.nn.sigmoid(cg) * cv).astype(o_ref.dtype)

    def cols(rows, off):
        return pl.BlockSpec((rows, tn), lambda j: (0, j + off))

    return pl.pallas_call(
        body, name="conv_glu_fwd", grid=(nb,),
        in_specs=[cols(s, 0), cols(s, nb), cols(3, 0), cols(3, nb), cols(1, 0), cols(1, nb)],
        out_specs=cols(s, 0), out_shape=jax.ShapeDtypeStruct((s, f), BF16),
        compiler_params=_params("parallel"),
    )(u, u, wc, wc, b, b)


def _conv_glu_bwd(u, da, wc, b, tn=256):
    s, f2 = u.shape
    f = f2 // 2
    nb = f // tn

    def body(ug_ref, uv_ref, da_ref, wg_ref, wv_ref, bg_ref, bv_ref, dug_ref, duv_ref, sg_ref, sv_ref):
        ug, uv, wg, wv = ug_ref[...], uv_ref[...], wg_ref[...], wv_ref[...]
        cg = _conv3(ug, wg, bg_ref[...])
        cv = _conv3(uv, wv, bv_ref[...])
        sig = jax.nn.sigmoid(cg)
        dav = da_ref[...]
        dcv = dav * (cg * sig)
        dcg = dav * cv * (sig * (1.0 + cg * (1.0 - sig)))
        for dc, uu, w, du_ref, st_ref in ((dcg, ug, wg, dug_ref, sg_ref), (dcv, uv, wv, duv_ref, sv_ref)):
            du = w[2:3, :] * dc + w[1:2, :] * _shift_up(dc, 1) + w[0:1, :] * _shift_up(dc, 2)
            du_ref[...] = du.astype(BF16)
            st_ref[...] = jnp.zeros_like(st_ref)
            st_ref[0:1, :] = jnp.sum(dc * _shift_down(uu, 2), axis=0, keepdims=True)
            st_ref[1:2, :] = jnp.sum(dc * _shift_down(uu, 1), axis=0, keepdims=True)
            st_ref[2:3, :] = jnp.sum(dc * uu, axis=0, keepdims=True)
            st_ref[3:4, :] = jnp.sum(dc, axis=0, keepdims=True)

    def cols(rows, off):
        return pl.BlockSpec((rows, tn), lambda j: (0, j + off))

    return pl.pallas_call(
        body, name="conv_glu_bwd", grid=(nb,),
        in_specs=[cols(s, 0), cols(s, nb), cols(s, 0), cols(3, 0), cols(3, nb), cols(1, 0), cols(1, nb)],
        out_specs=[cols(s, 0), cols(s, 0), cols(8, 0), cols(8, 0)],
        out_shape=[jax.ShapeDtypeStruct((s, f), BF16), jax.ShapeDtypeStruct((s, f), BF16),
                   jax.ShapeDtypeStruct((8, f), F32), jax.ShapeDtypeStruct((8, f), F32)],
        compiler_params=_params("parallel"),
    )(u, u, da, wc, wc, b, b)


def _loss_head(y, target, tm=256):
    s, d = y.shape

    def body(y_ref, t_ref, dyf_ref, dyb_ref, l_ref):
        @pl.when(pl.program_id(0) == 0)
        def _():
            l_ref[...] = jnp.zeros_like(l_ref)

        err = y_ref[...] - t_ref[...]
        dy = err * (1.0 / d)
        dyf_ref[...] = dy
        dyb_ref[...] = dy.astype(BF16)
        l_ref[...] += 0.5 * jnp.sum(jnp.sum(err * err, axis=-1, keepdims=True) * (1.0 / d), axis=0, keepdims=True)

    row = pl.BlockSpec((tm, d), lambda i: (i, 0))
    return pl.pallas_call(
        body, name="loss_head", grid=(s // tm,), in_specs=[row, row],
        out_specs=[row, row, pl.BlockSpec((8, LANES), lambda i: (0, 0))],
        out_shape=[jax.ShapeDtypeStruct((s, d), F32), jax.ShapeDtypeStruct((s, d), BF16),
                   jax.ShapeDtypeStruct((8, LANES), F32)],
        compiler_params=_params("arbitrary"),
    )(y, target)


ROW_TILES = (256, 128, 64, 32, 16, 8)
BLOCK_BYTES = 1 << 20


def _add_halves(g, r1, place):
    ns, r, c = g.shape
    rh = r // 2
    tr = _pick(rh, ROW_TILES)
    g4 = g.reshape(ns, 2, rh, c)

    def body(p_ref, g_ref, r_ref, o_ref):
        o_ref[...] = (g_ref[...].astype(F32) + r_ref[...].astype(F32)).astype(o_ref.dtype)

    return pl.pallas_call(
        body, name="add_halves",
        grid_spec=pltpu.PrefetchScalarGridSpec(
            num_scalar_prefetch=1, grid=(ns, rh // tr),
            in_specs=[pl.BlockSpec((None, None, tr, c), lambda s, i, pr: (s, pr[1], i, 0)),
                      pl.BlockSpec((None, tr, c), lambda s, i, pr: (s, i, 0))],
            out_specs=pl.BlockSpec((None, tr, c), lambda s, i, pr: (s, i, 0))),
        out_shape=jax.ShapeDtypeStruct((ns, rh, c), BF16),
        compiler_params=_params("parallel", "parallel"),
    )(place, g4, r1)


def _sum_chips(g, r1, recv, place):
    ns, r, c = g.shape
    rh = r // 2
    tr = _pick(rh, ROW_TILES)
    g4 = g.reshape(ns, 2, rh, c)

    def body(p_ref, g_ref, r_ref, t0_ref, t1_ref, t2_ref, o_ref):
        own = g_ref[...].astype(F32) + r_ref[...].astype(F32)
        o_ref[...] = ((own + t0_ref[...].astype(F32)) + t1_ref[...].astype(F32)) + t2_ref[...].astype(F32)

    def peer(k):
        return pl.BlockSpec((None, tr, c), lambda i, pr: (k, i, 0))

    return pl.pallas_call(
        body, name="sum_chips",
        grid_spec=pltpu.PrefetchScalarGridSpec(
            num_scalar_prefetch=1, grid=(rh // tr,),
            in_specs=[pl.BlockSpec((None, None, tr, c), lambda i, pr: (pr[0], pr[1], i, 0)),
                      pl.BlockSpec((None, tr, c), lambda i, pr: (pr[0], i, 0)), peer(0), peer(1), peer(2)],
            out_specs=pl.BlockSpec((tr, c), lambda i, pr: (i, 0))),
        out_shape=jax.ShapeDtypeStruct((rh, c), F32),
        compiler_params=_params("parallel"),
    )(place, g4, r1, recv, recv, recv)


def _sum_devices(packs):
    n, r, c = packs.shape

    def body(p_ref, o_ref):
        acc = p_ref[0]
        for d in range(1, n):
            acc = acc + p_ref[d]
        o_ref[...] = acc

    return pl.pallas_call(
        body, name="sum_devices", out_shape=jax.ShapeDtypeStruct((r, c), F32), compiler_params=_params(),
    )(packs)


def _adamw_update(wv, gv, mv, vv):
    c1 = 1.0 - ADAM_B1 ** ADAM_STEP
    c2 = 1.0 - ADAM_B2 ** ADAM_STEP
    mn = ADAM_B1 * mv + (1.0 - ADAM_B1) * gv
    vn = ADAM_B2 * vv + (1.0 - ADAM_B2) * (gv * gv)
    m_hat = mn / c1
    v_hat = vn / c2
    return -ADAM_LR * (m_hat / (jnp.sqrt(v_hat) + ADAM_EPS) + ADAM_WD * wv), mn, vn


def _adamw(w, g, m, v, name):
    r, c = w.shape
    tr = _pick(r, ROW_TILES) if r >= 8 else r

    def body(w_ref, g_ref, m_ref, v_ref, d_ref, mo_ref, vo_ref):
        d_ref[...], mo_ref[...], vo_ref[...] = _adamw_update(w_ref[...], g_ref[...], m_ref[...], v_ref[...])

    blk = pl.BlockSpec((tr, c), lambda i: (i, 0))
    return pl.pallas_call(
        body, name=name, grid=(r // tr,), in_specs=[blk] * 4, out_specs=[blk] * 3,
        out_shape=[jax.ShapeDtypeStruct((r, c), F32)] * 3, compiler_params=_params("parallel"),
    )(w, g, m, v)


def _adamw_halves(w, mine, theirs, c_idx, m, v, name):
    r, c = w.shape
    rh = r // 2
    tr = _pick(rh, [t for t in ROW_TILES if t * c * 4 <= BLOCK_BYTES])
    nb = rh // tr

    def body(c_ref, w_ref, a_ref, b_ref, m_ref, v_ref, g_ref, d_ref, mo_ref, vo_ref):
        gv = jnp.where(pl.program_id(0) // nb == c_ref[0], a_ref[...], b_ref[...])
        g_ref[...] = gv
        d_ref[...], mo_ref[...], vo_ref[...] = _adamw_update(w_ref[...], gv, m_ref[...], v_ref[...])

    blk = pl.BlockSpec((tr, c), lambda i, cr: (i, 0))
    mine_spec = pl.BlockSpec((tr, c), lambda i, cr: (jnp.clip(i - cr[0] * nb, 0, nb - 1), 0))
    theirs_spec = pl.BlockSpec((tr, c), lambda i, cr: (jnp.clip(i - (1 - cr[0]) * nb, 0, nb - 1), 0))
    return pl.pallas_call(
        body, name=name,
        grid_spec=pltpu.PrefetchScalarGridSpec(
            num_scalar_prefetch=1, grid=(r // tr,),
            in_specs=[blk, mine_spec, theirs_spec, blk, blk], out_specs=[blk] * 4),
        out_shape=[jax.ShapeDtypeStruct((r, c), F32)] * 4, compiler_params=_params("arbitrary"),
    )(c_idx, w, mine, theirs, m, v)


ANY = pl.BlockSpec(memory_space=pl.ANY)


def _place():
    x, y, c = lax.axis_index("x"), lax.axis_index("y"), lax.axis_index("c")
    chips = [(1 - x, y), (x, 1 - y), (1 - x, 1 - y)]
    return x, y, c, chips


def _remote(src, dst, send_sem, recv_sem, to):
    return pltpu.make_async_remote_copy(src_ref=src, dst_ref=dst, send_sem=send_sem, recv_sem=recv_sem,
                                        device_id=to, device_id_type=MESH)


HBM = pl.BlockSpec(memory_space=pltpu.HBM)
SEM = pl.BlockSpec(memory_space=pltpu.SEMAPHORE)
EFFECT = pltpu.SideEffectType.DATAFLOW_SIDE_EFFECTING


def _in_hbm(a):
    return pltpu.with_memory_space_constraint(a, pltpu.HBM)


def _half(ref_rows, who):
    return pl.ds(who * (ref_rows // 2), ref_rows // 2)


def _gather_start(groups):
    items = [it for g in groups for it in g]
    n = len(items)
    sizes = [len(g) for g in groups]

    def body(*refs):
        srcs, lands = refs[:n], refs[n:2 * n]
        sems = refs[2 * n:2 * n + 2 * len(groups)]
        token = refs[-1]
        x, y, c, chips = _place()
        j = 2 * x + y
        at = 0
        for gi, g in enumerate(groups):
            send, recv = sems[2 * gi], sems[2 * gi + 1]
            for i, (shard, split) in enumerate(g):
                src, land = srcs[at], lands[at]
                at += 1
                rows = _half(shard.shape[0], c) if split else slice(None)
                for k, chip in enumerate(chips):
                    _remote(src.at[rows], land.at[j, rows], send.at[3 * i + k], recv.at[3 * i + k], (*chip, c)).start()
        token[...] = jnp.zeros_like(token)

    sem_shapes = []
    for sz in sizes:
        sem_shapes += [pltpu.SemaphoreType.DMA((3 * sz,)), pltpu.SemaphoreType.DMA((3 * sz,))]
    out_shape = (sem_shapes + [pltpu.HBM(sh.shape, sh.dtype) for sh, _ in items]
                 + [pltpu.HBM((N_CHIPS,) + sh.shape, sh.dtype) for sh, _ in items]
                 + [jax.ShapeDtypeStruct((8, LANES), F32)])
    ns = len(sem_shapes)
    outs = pl.pallas_call(
        body, name="gather_start", in_specs=[HBM] * (2 * n),
        out_specs=[SEM] * ns + [HBM] * (2 * n) + [pl.BlockSpec(memory_space=pltpu.VMEM)],
        out_shape=out_shape, input_output_aliases={i: ns + i for i in range(2 * n)},
        compiler_params=pltpu.CompilerParams(has_side_effects=EFFECT),
    )(*[_in_hbm(sh) for sh, _ in items], *[_in_hbm(lax.empty((N_CHIPS,) + sh.shape, sh.dtype)) for sh, _ in items])
    sems, shards, lands, token = outs[:ns], outs[ns:ns + n], outs[ns + n:ns + 2 * n], outs[-1]
    res, at = [], 0
    for gi, sz in enumerate(sizes):
        res.append((shards[at:at + sz], lands[at:at + sz], sems[2 * gi], sems[2 * gi + 1]))
        at += sz
    return res, token


def _gather_pass(group, started, after, name):
    shards, lands, send, recv = started
    n = len(group)
    split_ix = [i for i, (_, split) in enumerate(group) if split]

    def body(*refs):
        lnds, send1, recv1 = refs[n:2 * n], refs[2 * n], refs[2 * n + 1]
        outs = refs[2 * n + 3:]
        send2, recv2, token = outs[2 * n], outs[2 * n + 1], outs[2 * n + 2]
        x, y, c, chips = _place()
        sib = (x, y, 1 - c)
        for i, (shard, split) in enumerate(group):
            rows = _half(shard.shape[0], c) if split else slice(None)
            for k, (cx, cy) in enumerate(chips):
                landed = lnds[i].at[2 * cx + cy, rows]
                cp = _remote(landed, landed, send1.at[3 * i + k], recv1.at[3 * i + k], sib)
                cp.wait_send()
                cp.wait_recv()
        for i2, i in enumerate(split_ix):
            rows = _half(group[i][0].shape[0], c)
            for k, (cx, cy) in enumerate(chips):
                landed = lnds[i].at[2 * cx + cy, rows]
                _remote(landed, landed, send2.at[3 * i2 + k], recv2.at[3 * i2 + k], sib).start()
        token[...] = jnp.zeros_like(token)

    n2 = len(split_ix)
    out_shape = ([pltpu.HBM(a.shape, a.dtype) for a in (*shards, *lands)]
                 + [pltpu.SemaphoreType.DMA((3 * n2,)), pltpu.SemaphoreType.DMA((3 * n2,)), jax.ShapeDtypeStruct((8, LANES), F32)])
    outs = pl.pallas_call(
        body, name=name, in_specs=[HBM] * (2 * n) + [SEM, SEM, ANY],
        out_specs=[HBM] * (2 * n) + [SEM, SEM, pl.BlockSpec(memory_space=pltpu.VMEM)],
        out_shape=out_shape, input_output_aliases={i: i for i in range(2 * n)},
        compiler_params=pltpu.CompilerParams(has_side_effects=EFFECT),
    )(*shards, *lands, send, recv, after)
    return outs[:n], (outs[n:2 * n], outs[2 * n], outs[2 * n + 1]), outs[2 * n + 2]


def _gather_wait(group, passed, after, name):
    lands, send2, recv2 = passed
    n = len(group)
    split_ix = [i for i, (_, split) in enumerate(group) if split]

    def body(*refs):
        lnds, s2, r2 = refs[:n], refs[n], refs[n + 1]
        x, y, c, chips = _place()
        sib = (x, y, 1 - c)
        for i2, i in enumerate(split_ix):
            rows = _half(group[i][0].shape[0], 1 - c)
            for k, (cx, cy) in enumerate(chips):
                landed = lnds[i].at[2 * cx + cy, rows]
                cp = _remote(landed, landed, s2.at[3 * i2 + k], r2.at[3 * i2 + k], sib)
                cp.wait_send()
                cp.wait_recv()

    return pl.pallas_call(
        body, name=name, in_specs=[HBM] * n + [SEM, SEM, ANY], out_specs=[HBM] * n,
        out_shape=[pltpu.HBM(a.shape, a.dtype) for a in lands], input_output_aliases={i: i for i in range(n)},
        compiler_params=pltpu.CompilerParams(has_side_effects=EFFECT),
    )(*lands, send2, recv2, after)


def _own_slab(land, shard):
    chip = 2 * lax.axis_index("x") + lax.axis_index("y")
    return lax.dynamic_update_slice(land, shard[None], (chip, 0, 0))


def _swap_halves(grads):
    nw = len(grads)

    def body(*refs):
        srcs, outs = refs[:nw], refs[nw:2 * nw]
        ssem, rsem = refs[2 * nw:]
        x, y, c, _ = _place()
        sib = (x, y, 1 - c)
        cps = []
        for i in range(nw):
            rh = srcs[i].shape[1] // 2
            cps.append(_remote(srcs[i].at[:, pl.ds((1 - c) * rh, rh)], outs[i], ssem.at[i], rsem.at[i], sib))
        for cp in cps:
            cp.start()
        for cp in cps:
            cp.wait()

    return pl.pallas_call(
        body, name="swap_halves", in_specs=[ANY] * nw, out_specs=[ANY] * nw,
        out_shape=[jax.ShapeDtypeStruct((g.shape[0], g.shape[1] // 2, g.shape[2]), g.dtype) for g in grads],
        scratch_shapes=[pltpu.SemaphoreType.DMA((nw,)), pltpu.SemaphoreType.DMA((nw,))],
    )(*grads)


def _scatter_chips(sums):
    nw = len(sums)

    def body(*refs):
        srcs, outs = refs[:nw], refs[nw:2 * nw]
        ssem, rsem = refs[2 * nw:]
        x, y, c, chips = _place()
        cps = []
        for i in range(nw):
            for k, (cx, cy) in enumerate(chips):
                cps.append(_remote(srcs[i].at[2 * cx + cy], outs[i].at[k], ssem.at[i, k], rsem.at[i, k], (cx, cy, c)))
        for cp in cps:
            cp.start()
        for cp in cps:
            cp.wait()

    return pl.pallas_call(
        body, name="scatter_chips", in_specs=[ANY] * nw, out_specs=[ANY] * nw,
        out_shape=[jax.ShapeDtypeStruct((3,) + s.shape[1:], s.dtype) for s in sums],
        scratch_shapes=[pltpu.SemaphoreType.DMA((nw, 3)), pltpu.SemaphoreType.DMA((nw, 3))],
    )(*sums)


def _join_halves(halves):
    nw = len(halves)

    def body(*refs):
        srcs, outs = refs[:nw], refs[nw:2 * nw]
        ssem, rsem = refs[2 * nw:]
        x, y, c, _ = _place()
        cps = [_remote(srcs[i], outs[i], ssem.at[i], rsem.at[i], (x, y, 1 - c)) for i in range(nw)]
        for cp in cps:
            cp.start()
        for cp in cps:
            cp.wait()

    return pl.pallas_call(
        body, name="join_halves", in_specs=[ANY] * nw, out_specs=[ANY] * nw,
        out_shape=[jax.ShapeDtypeStruct(h.shape, h.dtype) for h in halves],
        scratch_shapes=[pltpu.SemaphoreType.DMA((nw,)), pltpu.SemaphoreType.DMA((nw,))],
    )(*halves)


def _gather_packs(pack):
    def body(p_ref, o_ref, lsem, ssem, rsem):
        x, y, c, _ = _place()
        me = 4 * x + 2 * y + c
        local = pltpu.make_async_copy(p_ref, o_ref.at[me], lsem)
        local.start()
        cps = []
        for k in range(1, N_DEV):
            fx, fy, fc = (k >> 2) & 1, (k >> 1) & 1, k & 1
            to = (x ^ fx, y ^ fy, c ^ fc)
            cps.append(_remote(p_ref, o_ref.at[me], ssem.at[k - 1], rsem.at[k - 1], to))
        for cp in cps:
            cp.start()
        for k in range(1, N_DEV):
            fx, fy, fc = (k >> 2) & 1, (k >> 1) & 1, k & 1
            src = o_ref.at[4 * (x ^ fx) + 2 * (y ^ fy) + (c ^ fc)]
            _remote(src, src, ssem.at[k - 1], rsem.at[k - 1], (x, y, c)).wait_recv()
        for cp in cps:
            cp.wait_send()
        local.wait()

    return pl.pallas_call(
        body, name="gather_packs", in_specs=[ANY], out_specs=ANY,
        out_shape=jax.ShapeDtypeStruct((N_DEV,) + pack.shape, pack.dtype),
        scratch_shapes=[pltpu.SemaphoreType.DMA, pltpu.SemaphoreType.DMA((N_DEV - 1,)), pltpu.SemaphoreType.DMA((N_DEV - 1,))],
    )(pack)


LANE_TILES = (512, 896, 1408, 704, 384, 256, 128)


def _layer_grads(x, target, small, wg, rest_pass, rest_wait):
    s, d = x.shape
    f = wg["conv"].shape[1] // 2
    w_att = N_HEADS * HEAD_DIM
    in_splits = (w_att, w_att, w_att, N_HEADS, w_att, w_att, w_att, d, d)
    in_cols = sum(in_splits)
    cs = in_cols // N_CHIPS
    cp = wg["in"].shape[2]
    tm = min(s, 1024)
    t_in = _pick(cp, LANE_TILES)
    t_d = _pick(d, LANE_TILES)
    t_dq = _pick(d // N_CHIPS, LANE_TILES)
    t_w = _pick(w_att, LANE_TILES)
    t_up = _pick(2 * f // N_CHIPS, LANE_TILES)
    t_fq = _pick(f // N_CHIPS, LANE_TILES)
    offs = np.cumsum(in_splits)[:-1].tolist()

    h1 = _norm_fwd(x, small["g_attn"], group=d, name="rms1_fwd")
    proj_p = _mm(h1, wg["in"], mode="nn", b_kind="col", tm=tm, tn=t_in, tk=t_d, name="mm_in")
    proj = proj_p.reshape(s, N_CHIPS, cp)[:, :, :cs].reshape(s, in_cols)
    qa, ka, va, fa, qb, kb, vb, ga, gb = jnp.split(proj, offs, axis=-1)
    gains = {n: small[n].reshape(1, w_att) for n in ("g_q_fox", "g_k_fox", "g_q_dil", "g_k_dil")}
    qa_n = _norm_fwd(qa, gains["g_q_fox"], group=HEAD_DIM, name="qnorm_fox")
    ka_n = _norm_fwd(ka, gains["g_k_fox"], group=HEAD_DIM, name="knorm_fox")
    qb_n = _norm_fwd(qb, gains["g_q_dil"], group=HEAD_DIM, name="qnorm_dil")
    kb_n = _norm_fwd(kb, gains["g_k_dil"], group=HEAD_DIM, name="knorm_dil")
    va_b, vb_b = va.astype(BF16), vb.astype(BF16)
    fa_t = fa.T
    b_f = small["b_forget"].reshape(N_HEADS, 1)
    c_f = _forget_fwd(fa_t, b_f)
    slopes = jnp.asarray(2.0 ** (-8.0 * np.arange(1, N_HEADS + 1) / N_HEADS), dtype=F32)
    a_d = -(slopes[:, None] * jnp.arange(s, dtype=F32)[None, :])
    rows_f, cols_f = c_f[:, :, None], c_f[:, None, :]
    rows_d, cols_d = a_d[:, :, None], a_d[:, None, :]
    o_a, o_a32, lse_a = _attn_fwd(qa_n, ka_n, va_b, rows_f, cols_f, dilated=False, name="attn_fox_fwd")
    token = rest_pass(o_a)
    rows_d = rows_d + token[0, 0]
    o_b, o_b32, lse_b = _attn_fwd(qb_n, kb_n, vb_b, rows_d, cols_d, dilated=True, name="attn_dil_fwd")
    wg = dict(wg, **rest_wait(o_b))
    pa = _mm(o_a, wg["brf"], mode="nn", b_kind="col", tm=tm, tn=t_dq, tk=t_w, name="mm_brf")
    pb = _mm(o_b, wg["brd"], mode="nn", b_kind="col", tm=tm, tn=t_dq, tk=t_w, name="mm_brd")
    merged = _gate_fwd(ga, gb, pa, pb)
    x1 = _mm(merged, wg["out"], mode="nn", b_kind="row", res=x, tm=tm, tn=t_d, tk=t_dq, name="mm_out")
    h2 = _norm_fwd(x1, small["g_ffn"], group=d, name="rms2_fwd")
    u = _mm(h2, wg["up"], mode="nn", b_kind="col", tm=tm, tn=t_up, tk=t_d, name="mm_up")
    act = _conv_glu_fwd(u, wg["conv"], wg["bconv"])
    y = _mm(act, wg["down"], mode="nn", b_kind="row", res=x1, tm=tm, tn=t_d, tk=t_fq, name="mm_down")
    dy_f, dy_b, loss_blk = _loss_head(y, target)

    grads = {}
    d_act = _mm(dy_b, wg["down"], mode="nt", b_kind="row", tm=tm, tn=t_fq, tk=t_d, name="mm_down_dx")
    grads["down"] = _mm(act, dy_b, mode="tn", out_dtype=BF16, out_kind="row", tm=t_fq, tn=t_d, tk=tm, name="mm_down_dw")
    du_g, du_v, st_g, st_v = _conv_glu_bwd(u, d_act, wg["conv"], wg["bconv"])
    du = jnp.concatenate([du_g, du_v], axis=1)
    grads["up"] = _mm(h2, du, mode="tn", out_dtype=BF16, out_kind="col", tm=t_d, tn=t_up, tk=tm, name="mm_up_dw")
    dh2 = _mm(du, wg["up"], mode="nt", b_kind="col", tm=tm, tn=t_d, tk=t_up, name="mm_up_dx")
    dx1_b, dx1_f, dg_ffn = _norm_bwd(dh2, x1, small["g_ffn"], group=d, res=dy_f, out_dtypes=(BF16, F32), name="rms2_bwd")
    d_merged = _mm(dx1_b, wg["out"], mode="nt", b_kind="row", tm=tm, tn=t_dq, tk=t_d, name="mm_out_dx")
    grads["out"] = _mm(merged, dx1_b, mode="tn", out_dtype=BF16, out_kind="row", tm=t_dq, tn=t_d, tk=tm, name="mm_out_dw")
    dpa, dpb, dga, dgb = _gate_bwd(d_merged, ga, gb, pa, pb)
    do_a = _mm(dpa, wg["brf"], mode="nt", b_kind="col", out_dtype=BF16, tm=tm, tn=t_w, tk=t_dq, name="mm_brf_dx")
    do_b = _mm(dpb, wg["brd"], mode="nt", b_kind="col", out_dtype=BF16, tm=tm, tn=t_w, tk=t_dq, name="mm_brd_dx")
    grads["brf"] = _mm(o_a, dpa, mode="tn", out_dtype=BF16, out_kind="col", tm=t_w, tn=t_dq, tk=tm, name="mm_brf_dw")
    grads["brd"] = _mm(o_b, dpb, mode="tn", out_dtype=BF16, out_kind="col", tm=t_w, tn=t_dq, tk=tm, name="mm_brd_dw")
    dqa_n, dka_n, dva, dac_a = _attn_bwd(qa_n, ka_n, va_b, o_a32, do_a, lse_a, rows_f, cols_f, dilated=False, name="attn_fox_bwd")
    dqb_n, dkb_n, dvb, _ = _attn_bwd(qb_n, kb_n, vb_b, o_b32, do_b, lse_b, rows_d, cols_d, dilated=True, name="attn_dil_bwd")
    dqa, dg_qf = _norm_bwd(dqa_n, qa, gains["g_q_fox"], group=HEAD_DIM, name="qnorm_fox_bwd")
    dka, dg_kf = _norm_bwd(dka_n, ka, gains["g_k_fox"], group=HEAD_DIM, name="knorm_fox_bwd")
    dqb, dg_qd = _norm_bwd(dqb_n, qb, gains["g_q_dil"], group=HEAD_DIM, name="qnorm_dil_bwd")
    dkb, dg_kd = _norm_bwd(dkb_n, kb, gains["g_k_dil"], group=HEAD_DIM, name="knorm_dil_bwd")
    dfa_t, db_f = _forget_bwd(dac_a[:, 0, :], fa_t, b_f)
    dproj = jnp.concatenate([dqa, dka, dva.astype(BF16), dfa_t.T.astype(BF16), dqb, dkb, dvb.astype(BF16), dga, dgb], axis=1)
    dproj_p = jnp.pad(dproj.reshape(s, N_CHIPS, cs), ((0, 0), (0, 0), (0, cp - cs))).reshape(s, N_CHIPS * cp)
    grads["in"] = _mm(h1, dproj_p, mode="tn", out_dtype=BF16, out_kind="col", tm=t_d, tn=t_in, tk=tm, name="mm_in_dw")
    dh1 = _mm(dproj_p, wg["in"], mode="nt", b_kind="col", tm=tm, tn=t_d, tk=t_in, name="mm_in_dx")
    grad_x, dg_attn = _norm_bwd(dh1, x, small["g_attn"], group=d, res=dx1_f, out_dtypes=(F32,), name="rms1_bwd")

    small_grads = {
        "g_attn": dg_attn, "b_forget": db_f.reshape(1, N_HEADS),
        "g_q_fox": dg_qf, "g_k_fox": dg_kf, "g_q_dil": dg_qd, "g_k_dil": dg_kd, "g_ffn": dg_ffn,
        "w_conv": jnp.concatenate([st_g[0:3], st_v[0:3]], axis=1),
        "b_conv": jnp.concatenate([st_g[3:4], st_v[3:4]], axis=1),
        "loss": loss_blk[0:1, 0:1],
    }
    return grads, small_grads, grad_x


SMALL_ORDER = ("g_attn", "b_forget", "g_q_fox", "g_k_fox", "g_q_dil", "g_k_dil", "g_ffn", "w_conv", "b_conv", "loss")
WEIGHT_ORDER = ("g_attn", "w_in", "b_forget", "g_q_fox", "g_k_fox", "g_q_dil", "g_k_dil", "w_br_fox", "w_br_dil",
                "w_out", "g_ffn", "w_up", "w_conv", "b_conv", "w_down")
BIG = {"w_in": "in", "w_br_fox": "brf", "w_br_dil": "brd", "w_out": "out", "w_up": "up", "w_down": "down"}


def kernel(x, g_attn, w_in, b_forget, g_q_fox, g_k_fox, g_q_dil, g_k_dil, w_br_fox, w_br_dil, w_out, g_ffn, w_up, w_conv, b_conv, w_down, loss_target, m_g_attn, m_w_in, m_b_forget, m_g_q_fox, m_g_k_fox, m_g_q_dil, m_g_k_dil, m_w_br_fox, m_w_br_dil, m_w_out, m_g_ffn, m_w_up, m_w_conv, m_b_conv, m_w_down, v_g_attn, v_w_in, v_b_forget, v_g_q_fox, v_g_k_fox, v_g_q_dil, v_g_k_dil, v_w_br_fox, v_w_br_dil, v_w_out, v_g_ffn, v_w_up, v_w_conv, v_b_conv, v_w_down):
    w = dict(g_attn=g_attn, w_in=w_in, b_forget=b_forget, g_q_fox=g_q_fox, g_k_fox=g_k_fox, g_q_dil=g_q_dil,
             g_k_dil=g_k_dil, w_br_fox=w_br_fox, w_br_dil=w_br_dil, w_out=w_out, g_ffn=g_ffn, w_up=w_up,
             w_conv=w_conv, b_conv=b_conv, w_down=w_down)
    m = dict(g_attn=m_g_attn, w_in=m_w_in, b_forget=m_b_forget, g_q_fox=m_g_q_fox, g_k_fox=m_g_k_fox,
             g_q_dil=m_g_q_dil, g_k_dil=m_g_k_dil, w_br_fox=m_w_br_fox, w_br_dil=m_w_br_dil, w_out=m_w_out,
             g_ffn=m_g_ffn, w_up=m_w_up, w_conv=m_w_conv, b_conv=m_b_conv, w_down=m_w_down)
    v = dict(g_attn=v_g_attn, w_in=v_w_in, b_forget=v_b_forget, g_q_fox=v_g_q_fox, g_k_fox=v_g_k_fox,
             g_q_dil=v_g_q_dil, g_k_dil=v_g_k_dil, w_br_fox=v_w_br_fox, w_br_dil=v_w_br_dil, w_out=v_w_out,
             g_ffn=v_g_ffn, w_up=v_w_up, w_conv=v_w_conv, b_conv=v_b_conv, w_down=v_w_down)
    xi, yi, ci = lax.axis_index("x"), lax.axis_index("y"), lax.axis_index("c")
    chip = (2 * xi + yi).astype(jnp.int32)
    c_idx = ci.astype(jnp.int32).reshape(1)
    j_idx = chip.reshape(1)

    cs = w_in.shape[2]
    cp = _round_up(cs, LANES)
    shards = {
        "in": jnp.pad(w_in[0].astype(BF16), ((0, 0), (0, cp - cs))),
        "brf": w_br_fox[0].astype(BF16), "brd": w_br_dil[0].astype(BF16), "out": w_out[0].astype(BF16),
        "up": w_up[0].astype(BF16), "down": w_down[0].astype(BF16),
    }
    names = tuple(shards)
    conv_pad = jnp.pad(w_conv[0], ((0, 8 - w_conv.shape[1]), (0, 0)))
    first = [(shards["in"], True), (conv_pad, False)]
    rest_names = names[1:]
    rest = [(shards[n], True) for n in rest_names]
    (started_first, started_rest), token = _gather_start([first, rest])
    own_first, passed_first, token = _gather_pass(first, started_first, token, "gather_pass_in")
    land_in, land_conv = _gather_wait(first, passed_first, token, "gather_wait_in")
    conv_all = _own_slab(land_conv, own_first[1])
    wg = {"in": _own_slab(land_in, own_first[0]), "bconv": b_conv,
          "conv": jnp.transpose(conv_all[:, :w_conv.shape[1], :], (1, 0, 2)).reshape(w_conv.shape[1], -1)}
    small = {n: w[n] for n in ("g_attn", "b_forget", "g_q_fox", "g_k_fox", "g_q_dil", "g_k_dil", "g_ffn")}
    small = {n: (a[0] if a.ndim == 3 else a) for n, a in small.items()}
    in_flight = {}

    def rest_pass(after):
        in_flight["own"], in_flight["passed"], tok = _gather_pass(rest, started_rest, after, "gather_pass_rest")
        return tok

    def rest_wait(after):
        lands = _gather_wait(rest, in_flight["passed"], after, "gather_wait_rest")
        return {n: _own_slab(land, own) for n, land, own in zip(rest_names, lands, in_flight["own"])}

    grads, small_grads, grad_x = _layer_grads(x[0], loss_target[0], small, wg, rest_pass, rest_wait)

    glist = [grads[n] for n in names]
    place = jnp.stack([chip, ci.astype(jnp.int32)])
    from_sibling = _swap_halves(glist)
    chip_sums = [_add_halves(g, r, place) for g, r in zip(glist, from_sibling)]
    from_chips = _scatter_chips(chip_sums)
    halves = [_sum_chips(g, r, rc, place) for g, r, rc in zip(glist, from_sibling, from_chips)]
    mine = dict(zip(names, halves))
    theirs = dict(zip(names, _join_halves(halves)))
    mine["in"], theirs["in"] = mine["in"][:, :cs], theirs["in"][:, :cs]

    flat = jnp.concatenate([small_grads[n].reshape(-1) for n in SMALL_ORDER])
    rows = _round_up(flat.shape[0], 8 * LANES) // LANES
    pack = jnp.pad(flat, (0, rows * LANES - flat.shape[0])).reshape(rows, LANES)
    total = _sum_devices(_gather_packs(pack)).reshape(-1)
    red, at = {}, 0
    for n in SMALL_ORDER:
        size = small_grads[n].size
        red[n] = total[at:at + size].reshape(small_grads[n].shape)
        at += size
    loss = red["loss"].reshape(())
    c2 = w_conv.shape[2]
    red["w_conv"] = lax.dynamic_slice_in_dim(red["w_conv"], chip * c2, c2, axis=1)

    g_out, d_out, m_out, v_out = {}, {}, {}, {}
    for n in WEIGHT_ORDER:
        shape = w[n].shape
        r2 = (shape[-2], shape[-1]) if n not in ("g_attn", "b_forget", "g_ffn", "b_conv") else (1, shape[-1])
        if n in BIG:
            g2, dl, mn, vn = _adamw_halves(w[n].reshape(r2), mine[BIG[n]], theirs[BIG[n]], c_idx,
                                           m[n].reshape(r2), v[n].reshape(r2), name="adamw_" + n)
        else:
            g2 = red[n].reshape(r2)
            dl, mn, vn = _adamw(w[n].reshape(r2), g2, m[n].reshape(r2), v[n].reshape(r2), name="adamw_" + n)
        g_out[n], d_out[n], m_out[n], v_out[n] = (a.reshape(shape) for a in (g2, dl, mn, vn))

    return (loss, grad_x[None], *[g_out[n] for n in WEIGHT_ORDER], *[d_out[n] for n in WEIGHT_ORDER],
            *[m_out[n] for n in WEIGHT_ORDER], *[v_out[n] for n in WEIGHT_ORDER])
```

```python
import functools
import math

import jax
import jax.numpy as jnp
import numpy as np
from jax import lax
from jax.experimental import pallas as pl
from jax.experimental.pallas import tpu as pltpu

F32 = jnp.float32
BF16 = jnp.bfloat16
HEAD_DIM = 128
N_HEADS = 8
EPS = 1e-6
NEG = -1e30
N_CHIPS = 4
N_DEV = 8
LANES = 128
VMEM_LIMIT_BYTES = 56 * 1024 * 1024
DIL_PATTERNS = ((128, 1), (512, 4), (2048, 16))
ATTN_TILE = 512
ADAM_LR, ADAM_B1, ADAM_B2, ADAM_EPS, ADAM_WD, ADAM_STEP = 0.001, 0.9, 0.999, 1e-08, 0.01, 10
MESH = pl.DeviceIdType.MESH


def _params(*sem):
    return pltpu.CompilerParams(dimension_semantics=sem, vmem_limit_bytes=VMEM_LIMIT_BYTES)


def _round_up(n, m):
    return -(-n // m) * m


def _pick(dim, prefs):
    for p in prefs:
        if dim % p == 0:
            return p
    raise ValueError(f"no tile for {dim} in {prefs}")


def _logical_shape(arr, kind):
    if kind is None:
        return arr.shape
    s, r, c = arr.shape
    return (r, s * c) if kind == "col" else (s * r, c)


def _spec(shape, kind, br, bc, fi, fj):
    if kind is None:
        return pl.BlockSpec((br, bc), lambda *g: (fi(*g), fj(*g)))
    _, r, c = shape
    if kind == "col":
        nb = c // bc
        assert nb * bc == c, (shape, bc)
        return pl.BlockSpec((None, br, bc), lambda *g: (fj(*g) // nb, fi(*g), fj(*g) % nb))
    nb = r // br
    assert nb * br == r, (shape, br)
    return pl.BlockSpec((None, br, bc), lambda *g: (fi(*g) // nb, fi(*g) % nb, fj(*g)))


def _mm(a, b, *, mode, tm, tn, tk, name, a_kind=None, b_kind=None, out_kind=None,
        out_dtype=F32, res=None, deps=()):
    la, lb = _logical_shape(a, a_kind), _logical_shape(b, b_kind)
    if mode == "nn":
        (m, k), (k2, n) = la, lb
    elif mode == "nt":
        (m, k), (n, k2) = la, lb
    else:
        (k, m), (k2, n) = la, lb
    assert k == k2, (name, la, lb)
    assert m % tm == 0 and n % tn == 0 and k % tk == 0, (name, m, n, k, tm, tn, tk)
    nk = k // tk
    im = lambda i, j, l: i
    jn = lambda i, j, l: j
    lk = lambda i, j, l: l
    if mode == "tn":
        a_spec = _spec(a.shape, a_kind, tk, tm, lk, im)
        dims = (((0,), (0,)), ((), ()))
    else:
        a_spec = _spec(a.shape, a_kind, tm, tk, im, lk)
        dims = (((1,), (1,)), ((), ())) if mode == "nt" else (((1,), (0,)), ((), ()))
    if mode == "nt":
        b_spec = _spec(b.shape, b_kind, tn, tk, jn, lk)
    else:
        b_spec = _spec(b.shape, b_kind, tk, tn, lk, jn)
    if out_kind is None:
        oshape = (m, n)
    elif out_kind == "col":
        oshape = (N_CHIPS, m, n // N_CHIPS)
    else:
        oshape = (N_CHIPS, m // N_CHIPS, n)
    o_spec = _spec(oshape, out_kind, tm, tn, im, jn)
    in_specs = [a_spec, b_spec]
    args = [a, b]
    if res is not None:
        in_specs.append(pl.BlockSpec((tm, tn), lambda i, j, l: (i, j)))
        args.append(res)
    in_specs += [pl.BlockSpec(memory_space=pl.ANY)] * len(deps)
    args += list(deps)

    def body(*refs):
        a_ref, b_ref = refs[0], refs[1]
        res_ref = refs[2] if res is not None else None
        o_ref, acc_ref = refs[-2], refs[-1]
        step = pl.program_id(2)

        @pl.when(step == 0)
        def _():
            acc_ref[...] = jnp.zeros_like(acc_ref)

        acc_ref[...] += lax.dot_general(a_ref[...], b_ref[...], dims, preferred_element_type=F32)

        @pl.when(step == nk - 1)
        def _():
            out = acc_ref[...]
            if res_ref is not None:
                out = out + res_ref[...]
            o_ref[...] = out.astype(o_ref.dtype)

    return pl.pallas_call(
        body, name=name, grid=(m // tm, n // tn, nk),
        in_specs=in_specs, out_specs=o_spec,
        out_shape=jax.ShapeDtypeStruct(oshape, out_dtype),
        scratch_shapes=[pltpu.VMEM((tm, tn), F32)],
        compiler_params=_params("parallel", "parallel", "arbitrary"),
    )(*args)


def _norm_fwd(x, g, *, group, name, tm=256):
    s, w = x.shape
    ng = w // group

    def body(x_ref, g_ref, o_ref):
        for i in range(ng):
            cols = slice(i * group, (i + 1) * group)
            xv = x_ref[:, cols]
            r = lax.rsqrt(jnp.mean(xv * xv, axis=-1, keepdims=True) + EPS)
            o_ref[:, cols] = ((xv * r) * g_ref[:, cols]).astype(o_ref.dtype)

    return pl.pallas_call(
        body, name=name, grid=(s // tm,),
        in_specs=[pl.BlockSpec((tm, w), lambda i: (i, 0)), pl.BlockSpec((1, w), lambda i: (0, 0))],
        out_specs=pl.BlockSpec((tm, w), lambda i: (i, 0)),
        out_shape=jax.ShapeDtypeStruct((s, w), BF16),
        compiler_params=_params("parallel"),
    )(x, g)


def _norm_bwd(dy, x, g, *, group, name, res=None, out_dtypes=(BF16,), tm=256):
    s, w = x.shape
    ng = w // group
    n_in = 4 if res is not None else 3

    def body(*refs):
        dy_ref, x_ref, g_ref = refs[:3]
        res_ref = refs[3] if res is not None else None
        outs = refs[n_in:]
        dx_refs, dg_ref = outs[:-1], outs[-1]

        @pl.when(pl.program_id(0) == 0)
        def _():
            dg_ref[...] = jnp.zeros_like(dg_ref)

        for i in range(ng):
            cols = slice(i * group, (i + 1) * group)
            xv = x_ref[:, cols]
            dyv = dy_ref[:, cols].astype(F32)
            r = lax.rsqrt(jnp.mean(xv * xv, axis=-1, keepdims=True) + EPS)
            xr = xv * r
            dg_ref[:, cols] += jnp.sum(dyv * xr, axis=0, keepdims=True)
            gdy = dyv * g_ref[:, cols]
            dx = r * (gdy - xr * jnp.mean(gdy * xr, axis=-1, keepdims=True))
            if res_ref is not None:
                dx = dx + res_ref[:, cols]
            for dx_ref in dx_refs:
                dx_ref[:, cols] = dx.astype(dx_ref.dtype)

    row = pl.BlockSpec((tm, w), lambda i: (i, 0))
    vec = pl.BlockSpec((1, w), lambda i: (0, 0))
    in_specs = [row, row, vec] + ([row] if res is not None else [])
    args = [dy, x, g] + ([res] if res is not None else [])
    out_specs = [row] * len(out_dtypes) + [vec]
    out_shape = [jax.ShapeDtypeStruct((s, w), dt) for dt in out_dtypes] + [jax.ShapeDtypeStruct((1, w), F32)]
    return pl.pallas_call(
        body, name=name, grid=(s // tm,), in_specs=in_specs, out_specs=out_specs,
        out_shape=out_shape, compiler_params=_params("arbitrary"),
    )(*args)


def _split3(v):
    p1 = v.astype(BF16)
    r1 = v - p1.astype(F32)
    p2 = r1.astype(BF16)
    p3 = (r1 - p2.astype(F32)).astype(BF16)
    return p1, p2, p3


def _tri_sum(v, reverse, tcol=512):
    h, s = v.shape
    tcol = min(tcol, s)
    parts = _split3(v)
    outs = []
    for j in range(s // tcol):
        src = lax.broadcasted_iota(jnp.int32, (s, tcol), 0)
        dst = lax.broadcasted_iota(jnp.int32, (s, tcol), 1) + j * tcol
        keep = (src >= dst) if reverse else (src <= dst)
        tri = jnp.where(keep, 1.0, 0.0).astype(BF16)
        acc = jnp.zeros((h, tcol), F32)
        for p in parts:
            acc = acc + jnp.dot(p, tri, preferred_element_type=F32)
        outs.append(acc)
    return outs


def _forget_fwd(fa_t, b):
    h, s = fa_t.shape
    tcol = min(512, s)

    def body(f_ref, b_ref, c_ref):
        z = f_ref[...] + b_ref[...]
        logf = jnp.minimum(z, 0.0) - jnp.log(1.0 + jnp.exp(-jnp.abs(z)))
        for j, blk in enumerate(_tri_sum(logf, reverse=False, tcol=tcol)):
            c_ref[:, j * tcol:(j + 1) * tcol] = blk

    return pl.pallas_call(
        body, name="forget_fwd", out_shape=jax.ShapeDtypeStruct((h, s), F32),
        compiler_params=_params(),
    )(fa_t, b)


def _forget_bwd(dacol, fa_t, b):
    h, s = fa_t.shape
    tcol = min(512, s)

    def body(d_ref, f_ref, b_ref, dfa_ref, db_ref):
        z = f_ref[...] + b_ref[...]
        dc = -d_ref[...]
        total = jnp.zeros((h, 1), F32)
        for j, blk in enumerate(_tri_sum(dc, reverse=True, tcol=tcol)):
            cols = slice(j * tcol, (j + 1) * tcol)
            dfa = blk * (1.0 - jax.nn.sigmoid(z[:, cols]))
            dfa_ref[:, cols] = dfa
            total = total + jnp.sum(dfa, axis=-1, keepdims=True)
        db_ref[...] = total

    return pl.pallas_call(
        body, name="forget_bwd",
        out_shape=[jax.ShapeDtypeStruct((h, s), F32), jax.ShapeDtypeStruct((h, 1), F32)],
        compiler_params=_params(),
    )(dacol, fa_t, b)


def _logits(q, k, arow, acol, q0, k0, dilated):
    tq, tk = q.shape[0], k.shape[0]
    s = lax.dot_general(q, k, (((1,), (1,)), ((), ())), preferred_element_type=F32)
    s = s * (1.0 / math.sqrt(HEAD_DIM)) + arow - acol
    dist = (q0 + lax.broadcasted_iota(jnp.int32, (tq, tk), 0)) - (k0 + lax.broadcasted_iota(jnp.int32, (tq, tk), 1))
    valid = dist >= 0
    if dilated:
        mult = jnp.zeros((tq, tk), jnp.int32)
        for window, dil in DIL_PATTERNS:
            mult = mult + ((dist <= window) & ((dist & (dil - 1)) == 0)).astype(jnp.int32)
        s = s + jnp.where(mult == 3, math.log(3.0), jnp.where(mult == 2, math.log(2.0), 0.0))
        valid = valid & (mult > 0)
    return jnp.where(valid, s, NEG)


def _attn_fwd(q, k, v, arow, acol, *, dilated, name, tq=ATTN_TILE, tk=ATTN_TILE):
    s, w = q.shape
    nh = w // HEAD_DIM
    assert tq == tk
    tq = tk = min(tq, s)
    nq, nk = s // tq, s // tk

    def body(q_ref, k_ref, v_ref, ar_ref, ac_ref, o_ref, of_ref, lse_ref, m_ref, l_ref, acc_ref):
        qi, ki = pl.program_id(1), pl.program_id(2)

        @pl.when(ki == 0)
        def _():
            m_ref[...] = jnp.full_like(m_ref, NEG)
            l_ref[...] = jnp.zeros_like(l_ref)
            acc_ref[...] = jnp.zeros_like(acc_ref)

        @pl.when(ki <= qi)
        def _():
            sc = _logits(q_ref[...], k_ref[...], ar_ref[...], ac_ref[...], qi * tq, ki * tk, dilated)
            m_new = jnp.maximum(m_ref[...], jnp.max(sc, axis=-1, keepdims=True))
            alpha = jnp.exp(m_ref[...] - m_new)
            p = jnp.exp(sc - m_new)
            l_ref[...] = alpha * l_ref[...] + jnp.sum(p, axis=-1, keepdims=True)
            p_hi = p.astype(BF16)
            p_lo = (p - p_hi.astype(F32)).astype(BF16)
            vv = v_ref[...]
            acc_ref[...] = (alpha * acc_ref[...] + jnp.dot(p_hi, vv, preferred_element_type=F32)
                            + jnp.dot(p_lo, vv, preferred_element_type=F32))
            m_ref[...] = m_new

        @pl.when(ki == nk - 1)
        def _():
            out = acc_ref[...] / l_ref[...]
            o_ref[...] = out.astype(o_ref.dtype)
            of_ref[...] = out
            lse_ref[...] = m_ref[...] + jnp.log(l_ref[...])

    kv = pl.BlockSpec((tk, HEAD_DIM), lambda h, i, j: (jnp.minimum(j, i), h))
    return pl.pallas_call(
        body, name=name, grid=(nh, nq, nk),
        in_specs=[pl.BlockSpec((tq, HEAD_DIM), lambda h, i, j: (i, h)), kv, kv,
                  pl.BlockSpec((None, tq, 1), lambda h, i, j: (h, i, 0)),
                  pl.BlockSpec((None, 1, tk), lambda h, i, j: (h, 0, jnp.minimum(j, i)))],
        out_specs=[pl.BlockSpec((tq, HEAD_DIM), lambda h, i, j: (i, h)),
                   pl.BlockSpec((tq, HEAD_DIM), lambda h, i, j: (i, h)),
                   pl.BlockSpec((None, tq, 1), lambda h, i, j: (h, i, 0))],
        out_shape=[jax.ShapeDtypeStruct((s, w), BF16), jax.ShapeDtypeStruct((s, w), F32),
                   jax.ShapeDtypeStruct((nh, s, 1), F32)],
        scratch_shapes=[pltpu.VMEM((tq, 1), F32), pltpu.VMEM((tq, 1), F32), pltpu.VMEM((tq, HEAD_DIM), F32)],
        compiler_params=_params("parallel", "parallel", "arbitrary"),
    )(q, k, v, arow, acol)


def _attn_bwd(q, k, v, o, do, lse, arow, acol, *, dilated, name, tq=ATTN_TILE, tk=ATTN_TILE):
    s, w = q.shape
    nh = w // HEAD_DIM
    assert tq == tk
    tq = tk = min(tq, s)
    nq, nk = s // tq, s // tk
    scale = 1.0 / math.sqrt(HEAD_DIM)

    def body(q_ref, k_ref, v_ref, o_ref, do_ref, lse_ref, ar_ref, ac_ref,
             dq_ref, dk_ref, dv_ref, dac_ref, dk_acc, dv_acc, dac_acc):
        ki, qi = pl.program_id(1), pl.program_id(2)

        @pl.when((ki == 0) & (qi == 0))
        def _():
            dq_ref[...] = jnp.zeros_like(dq_ref)

        @pl.when(qi == 0)
        def _():
            dk_acc[...] = jnp.zeros_like(dk_acc)
            dv_acc[...] = jnp.zeros_like(dv_acc)
            dac_acc[...] = jnp.zeros_like(dac_acc)

        @pl.when(qi >= ki)
        def _():
            qv, kvv, dov = q_ref[...], k_ref[...], do_ref[...]
            sc = _logits(qv, kvv, ar_ref[...], ac_ref[...], qi * tq, ki * tk, dilated)
            p = jnp.exp(sc - lse_ref[...])
            dp = lax.dot_general(dov, v_ref[...], (((1,), (1,)), ((), ())), preferred_element_type=F32)
            delta = jnp.sum(dov.astype(F32) * o_ref[...].astype(F32), axis=-1, keepdims=True)
            ds = p * (dp - delta)
            dsb = ds.astype(BF16)
            dv_acc[...] += lax.dot_general(p.astype(BF16), dov, (((0,), (0,)), ((), ())), preferred_element_type=F32)
            dk_acc[...] += lax.dot_general(dsb, qv, (((0,), (0,)), ((), ())), preferred_element_type=F32)
            rows = pl.ds(pl.multiple_of(qi * tq, tq), tq)
            dq_ref[rows, :] += jnp.dot(dsb, kvv, preferred_element_type=F32) * scale
            dac_acc[...] += jnp.sum(ds, axis=0, keepdims=True)

        @pl.when(qi == nq - 1)
        def _():
            dk_ref[...] = dk_acc[...] * scale
            dv_ref[...] = dv_acc[...]
            dac_ref[...] = dac_acc[...]

    qs = pl.BlockSpec((tq, HEAD_DIM), lambda h, j, i: (jnp.maximum(i, j), h))
    ks = pl.BlockSpec((tk, HEAD_DIM), lambda h, j, i: (j, h))
    rowv = pl.BlockSpec((None, tq, 1), lambda h, j, i: (h, jnp.maximum(i, j), 0))
    colv = pl.BlockSpec((None, 1, tk), lambda h, j, i: (h, 0, j))
    return pl.pallas_call(
        body, name=name, grid=(nh, nk, nq),
        in_specs=[qs, ks, ks, qs, qs, rowv, rowv, colv],
        out_specs=[pl.BlockSpec((s, HEAD_DIM), lambda h, j, i: (0, h)), ks, ks, colv],
        out_shape=[jax.ShapeDtypeStruct((s, w), F32), jax.ShapeDtypeStruct((s, w), F32),
                   jax.ShapeDtypeStruct((s, w), F32), jax.ShapeDtypeStruct((nh, 1, s), F32)],
        scratch_shapes=[pltpu.VMEM((tk, HEAD_DIM), F32), pltpu.VMEM((tk, HEAD_DIM), F32), pltpu.VMEM((1, tk), F32)],
        compiler_params=_params("arbitrary", "arbitrary", "arbitrary"),
    )(q, k, v, o, do, lse, arow, acol)


def _gate_fwd(ga, gb, pa, pb, tm=256):
    s, d = ga.shape

    def body(ga_ref, gb_ref, pa_ref, pb_ref, o_ref):
        o_ref[...] = (jax.nn.sigmoid(ga_ref[...]) * pa_ref[...]
                      + jax.nn.sigmoid(gb_ref[...]) * pb_ref[...]).astype(o_ref.dtype)

    row = pl.BlockSpec((tm, d), lambda i: (i, 0))
    return pl.pallas_call(
        body, name="gate_fwd", grid=(s // tm,), in_specs=[row] * 4, out_specs=row,
        out_shape=jax.ShapeDtypeStruct((s, d), BF16), compiler_params=_params("parallel"),
    )(ga, gb, pa, pb)


def _gate_bwd(dm, ga, gb, pa, pb, tm=256):
    s, d = ga.shape

    def body(dm_ref, ga_ref, gb_ref, pa_ref, pb_ref, dpa_ref, dpb_ref, dga_ref, dgb_ref):
        dmv = dm_ref[...]
        for g_ref, p_ref, dp_ref, dg_ref in ((ga_ref, pa_ref, dpa_ref, dga_ref), (gb_ref, pb_ref, dpb_ref, dgb_ref)):
            sg = jax.nn.sigmoid(g_ref[...])
            dp_ref[...] = (dmv * sg).astype(BF16)
            dg_ref[...] = (dmv * p_ref[...] * (sg * (1.0 - sg))).astype(BF16)

    row = pl.BlockSpec((tm, d), lambda i: (i, 0))
    return pl.pallas_call(
        body, name="gate_bwd", grid=(s // tm,), in_specs=[row] * 5, out_specs=[row] * 4,
        out_shape=[jax.ShapeDtypeStruct((s, d), BF16)] * 4, compiler_params=_params("parallel"),
    )(dm, ga, gb, pa, pb)


def _shift_down(u, k):
    row = lax.broadcasted_iota(jnp.int32, u.shape, 0)
    return jnp.where(row >= k, pltpu.roll(u, k, 0), 0.0)


def _shift_up(u, k):
    n = u.shape[0]
    row = lax.broadcasted_iota(jnp.int32, u.shape, 0)
    return jnp.where(row < n - k, pltpu.roll(u, n - k, 0), 0.0)


def _conv3(u, wc, b):
    return wc[0:1, :] * _shift_down(u, 2) + wc[1:2, :] * _shift_down(u, 1) + wc[2:3, :] * u + b


def _conv_glu_fwd(u, wc, b, tn=256):
    s, f2 = u.shape
    f = f2 // 2
    nb = f // tn

    def body(ug_ref, uv_ref, wg_ref, wv_ref, bg_ref, bv_ref, o_ref):
        cg = _conv3(ug_ref[...], wg_ref[...], bg_ref[...])
        cv = _conv3(uv_ref[...], wv_ref[...], bv_ref[...])
        o_ref[...] = (cg * jax.nn.sigmoid(cg) * cv).astype(o_ref.dtype)

    def cols(rows, off):
        return pl.BlockSpec((rows, tn), lambda j: (0, j + off))

    return pl.pallas_call(
        body, name="conv_glu_fwd", grid=(nb,),
        in_specs=[cols(s, 0), cols(s, nb), cols(3, 0), cols(3, nb), cols(1, 0), cols(1, nb)],
        out_specs=cols(s, 0), out_shape=jax.ShapeDtypeStruct((s, f), BF16),
        compiler_params=_params("parallel"),
    )(u, u, wc, wc, b, b)


def _conv_glu_bwd(u, da, wc, b, tn=256):
    s, f2 = u.shape
    f = f2 // 2
    nb = f // tn

    def body(ug_ref, uv_ref, da_ref, wg_ref, wv_ref, bg_ref, bv_ref, dug_ref, duv_ref, sg_ref, sv_ref):
        ug, uv, wg, wv = ug_ref[...], uv_ref[...], wg_ref[...], wv_ref[...]
        cg = _conv3(ug, wg, bg_ref[...])
        cv = _conv3(uv, wv, bv_ref[...])
        sig = jax.nn.sigmoid(cg)
        dav = da_ref[...]
        dcv = dav * (cg * sig)
        dcg = dav * cv * (sig * (1.0 + cg * (1.0 - sig)))
        for dc, uu, w, du_ref, st_ref in ((dcg, ug, wg, dug_ref, sg_ref), (dcv, uv, wv, duv_ref, sv_ref)):
            du = w[2:3, :] * dc + w[1:2, :] * _shift_up(dc, 1) + w[0:1, :] * _shift_up(dc, 2)
            du_ref[...] = du.astype(BF16)
            st_ref[...] = jnp.zeros_like(st_ref)
            st_ref[0:1, :] = jnp.sum(dc * _shift_down(uu, 2), axis=0, keepdims=True)
            st_ref[1:2, :] = jnp.sum(dc * _shift_down(uu, 1), axis=0, keepdims=True)
            st_ref[2:3, :] = jnp.sum(dc * uu, axis=0, keepdims=True)
            st_ref[3:4, :] = jnp.sum(dc, axis=0, keepdims=True)

    def cols(rows, off):
        return pl.BlockSpec((rows, tn), lambda j: (0, j + off))

    return pl.pallas_call(
        body, name="conv_glu_bwd", grid=(nb,),
        in_specs=[cols(s, 0), cols(s, nb), cols(s, 0), cols(3, 0), cols(3, nb), cols(1, 0), cols(1, nb)],
        out_specs=[cols(s, 0), cols(s, 0), cols(8, 0), cols(8, 0)],
        out_shape=[jax.ShapeDtypeStruct((s, f), BF16), jax.ShapeDtypeStruct((s, f), BF16),
                   jax.ShapeDtypeStruct((8, f), F32), jax.ShapeDtypeStruct((8, f), F32)],
        compiler_params=_params("parallel"),
    )(u, u, da, wc, wc, b, b)


def _loss_head(y, target, tm=256):
    s, d = y.shape

    def body(y_ref, t_ref, dyf_ref, dyb_ref, l_ref):
        @pl.when(pl.program_id(0) == 0)
        def _():
            l_ref[...] = jnp.zeros_like(l_ref)

        err = y_ref[...] - t_ref[...]
        dy = err * (1.0 / d)
        dyf_ref[...] = dy
        dyb_ref[...] = dy.astype(BF16)
        l_ref[...] += 0.5 * jnp.sum(jnp.sum(err * err, axis=-1, keepdims=True) * (1.0 / d), axis=0, keepdims=True)

    row = pl.BlockSpec((tm, d), lambda i: (i, 0))
    return pl.pallas_call(
        body, name="loss_head", grid=(s // tm,), in_specs=[row, row],
        out_specs=[row, row, pl.BlockSpec((8, LANES), lambda i: (0, 0))],
        out_shape=[jax.ShapeDtypeStruct((s, d), F32), jax.ShapeDtypeStruct((s, d), BF16),
                   jax.ShapeDtypeStruct((8, LANES), F32)],
        compiler_params=_params("arbitrary"),
    )(y, target)


ROW_TILES = (256, 128, 64, 32, 16, 8)
BLOCK_BYTES = 1 << 20


def _add_halves(g, r1, place):
    ns, r, c = g.shape
    rh = r // 2
    tr = _pick(rh, ROW_TILES)
    g4 = g.reshape(ns, 2, rh, c)

    def body(p_ref, g_ref, r_ref, o_ref):
        o_ref[...] = (g_ref[...].astype(F32) + r_ref[...].astype(F32)).astype(o_ref.dtype)

    return pl.pallas_call(
        body, name="add_halves",
        grid_spec=pltpu.PrefetchScalarGridSpec(
            num_scalar_prefetch=1, grid=(ns, rh // tr),
            in_specs=[pl.BlockSpec((None, None, tr, c), lambda s, i, pr: (s, pr[1], i, 0)),
                      pl.BlockSpec((None, tr, c), lambda s, i, pr: (s, i, 0))],
            out_specs=pl.BlockSpec((None, tr, c), lambda s, i, pr: (s, i, 0))),
        out_shape=jax.ShapeDtypeStruct((ns, rh, c), BF16),
        compiler_params=_params("parallel", "parallel"),
    )(place, g4, r1)


def _sum_chips(g, r1, recv, place):
    ns, r, c = g.shape
    rh = r // 2
    tr = _pick(rh, ROW_TILES)
    g4 = g.reshape(ns, 2, rh, c)

    def body(p_ref, g_ref, r_ref, t0_ref, t1_ref, t2_ref, o_ref):
        own = g_ref[...].astype(F32) + r_ref[...].astype(F32)
        o_ref[...] = ((own + t0_ref[...].astype(F32)) + t1_ref[...].astype(F32)) + t2_ref[...].astype(F32)

    def peer(k):
        return pl.BlockSpec((None, tr, c), lambda i, pr: (k, i, 0))

    return pl.pallas_call(
        body, name="sum_chips",
        grid_spec=pltpu.PrefetchScalarGridSpec(
            num_scalar_prefetch=1, grid=(rh // tr,),
            in_specs=[pl.BlockSpec((None, None, tr, c), lambda i, pr: (pr[0], pr[1], i, 0)),
                      pl.BlockSpec((None, tr, c), lambda i, pr: (pr[0], i, 0)), peer(0), peer(1), peer(2)],
            out_specs=pl.BlockSpec((tr, c), lambda i, pr: (i, 0))),
        out_shape=jax.ShapeDtypeStruct((rh, c), F32),
        compiler_params=_params("parallel"),
    )(place, g4, r1, recv, recv, recv)


def _sum_devices(packs):
    n, r, c = packs.shape

    def body(p_ref, o_ref):
        acc = p_ref[0]
        for d in range(1, n):
            acc = acc + p_ref[d]
        o_ref[...] = acc

    return pl.pallas_call(
        body, name="sum_devices", out_shape=jax.ShapeDtypeStruct((r, c), F32), compiler_params=_params(),
    )(packs)


def _adamw_update(wv, gv, mv, vv):
    c1 = 1.0 - ADAM_B1 ** ADAM_STEP
    c2 = 1.0 - ADAM_B2 ** ADAM_STEP
    mn = ADAM_B1 * mv + (1.0 - ADAM_B1) * gv
    vn = ADAM_B2 * vv + (1.0 - ADAM_B2) * (gv * gv)
    m_hat = mn / c1
    v_hat = vn / c2
    return -ADAM_LR * (m_hat / (jnp.sqrt(v_hat) + ADAM_EPS) + ADAM_WD * wv), mn, vn


def _adamw(w, g, m, v, name):
    r, c = w.shape
    tr = _pick(r, ROW_TILES) if r >= 8 else r

    def body(w_ref, g_ref, m_ref, v_ref, d_ref, mo_ref, vo_ref):
        d_ref[...], mo_ref[...], vo_ref[...] = _adamw_update(w_ref[...], g_ref[...], m_ref[...], v_ref[...])

    blk = pl.BlockSpec((tr, c), lambda i: (i, 0))
    return pl.pallas_call(
        body, name=name, grid=(r // tr,), in_specs=[blk] * 4, out_specs=[blk] * 3,
        out_shape=[jax.ShapeDtypeStruct((r, c), F32)] * 3, compiler_params=_params("parallel"),
    )(w, g, m, v)


def _adamw_halves(w, mine, theirs, c_idx, m, v, name):
    r, c = w.shape
    rh = r // 2
    tr = _pick(rh, [t for t in ROW_TILES if t * c * 4 <= BLOCK_BYTES])
    nb = rh // tr

    def body(c_ref, w_ref, a_ref, b_ref, m_ref, v_ref, g_ref, d_ref, mo_ref, vo_ref):
        gv = jnp.where(pl.program_id(0) // nb == c_ref[0], a_ref[...], b_ref[...])
        g_ref[...] = gv
        d_ref[...], mo_ref[...], vo_ref[...] = _adamw_update(w_ref[...], gv, m_ref[...], v_ref[...])

    blk = pl.BlockSpec((tr, c), lambda i, cr: (i, 0))
    mine_spec = pl.BlockSpec((tr, c), lambda i, cr: (jnp.clip(i - cr[0] * nb, 0, nb - 1), 0))
    theirs_spec = pl.BlockSpec((tr, c), lambda i, cr: (jnp.clip(i - (1 - cr[0]) * nb, 0, nb - 1), 0))
    return pl.pallas_call(
        body, name=name,
        grid_spec=pltpu.PrefetchScalarGridSpec(
            num_scalar_prefetch=1, grid=(r // tr,),
            in_specs=[blk, mine_spec, theirs_spec, blk, blk], out_specs=[blk] * 4),
        out_shape=[jax.ShapeDtypeStruct((r, c), F32)] * 4, compiler_params=_params("arbitrary"),
    )(c_idx, w, mine, theirs, m, v)


ANY = pl.BlockSpec(memory_space=pl.ANY)


def _place():
    x, y, c = lax.axis_index("x"), lax.axis_index("y"), lax.axis_index("c")
    chips = [(1 - x, y), (x, 1 - y), (1 - x, 1 - y)]
    return x, y, c, chips


def _remote(src, dst, send_sem, recv_sem, to):
    return pltpu.make_async_remote_copy(src_ref=src, dst_ref=dst, send_sem=send_sem, recv_sem=recv_sem,
                                        device_id=to, device_id_type=MESH)


HBM = pl.BlockSpec(memory_space=pltpu.HBM)
SEM = pl.BlockSpec(memory_space=pltpu.SEMAPHORE)
EFFECT = pltpu.SideEffectType.DATAFLOW_SIDE_EFFECTING


def _in_hbm(a):
    return pltpu.with_memory_space_constraint(a, pltpu.HBM)


def _half(ref_rows, who):
    return pl.ds(who * (ref_rows // 2), ref_rows // 2)


def _gather_start(groups):
    items = [it for g in groups for it in g]
    n = len(items)
    sizes = [len(g) for g in groups]

    def body(*refs):
        srcs, lands = refs[:n], refs[n:2 * n]
        sems = refs[2 * n:2 * n + 2 * len(groups)]
        token = refs[-1]
        x, y, c, chips = _place()
        j = 2 * x + y
        at = 0
        for gi, g in enumerate(groups):
            send, recv = sems[2 * gi], sems[2 * gi + 1]
            for i, (shard, split) in enumerate(g):
                src, land = srcs[at], lands[at]
                at += 1
                rows = _half(shard.shape[0], c) if split else slice(None)
                for k, chip in enumerate(chips):
                    _remote(src.at[rows], land.at[j, rows], send.at[3 * i + k], recv.at[3 * i + k], (*chip, c)).start()
        token[...] = jnp.zeros_like(token)

    sem_shapes = []
    for sz in sizes:
        sem_shapes += [pltpu.SemaphoreType.DMA((3 * sz,)), pltpu.SemaphoreType.DMA((3 * sz,))]
    out_shape = (sem_shapes + [pltpu.HBM(sh.shape, sh.dtype) for sh, _ in items]
                 + [pltpu.HBM((N_CHIPS,) + sh.shape, sh.dtype) for sh, _ in items]
                 + [jax.ShapeDtypeStruct((8, LANES), F32)])
    ns = len(sem_shapes)
    outs = pl.pallas_call(
        body, name="gather_start", in_specs=[HBM] * (2 * n),
        out_specs=[SEM] * ns + [HBM] * (2 * n) + [pl.BlockSpec(memory_space=pltpu.VMEM)],
        out_shape=out_shape, input_output_aliases={i: ns + i for i in range(2 * n)},
        compiler_params=pltpu.CompilerParams(has_side_effects=EFFECT),
    )(*[_in_hbm(sh) for sh, _ in items], *[_in_hbm(lax.empty((N_CHIPS,) + sh.shape, sh.dtype)) for sh, _ in items])
    sems, shards, lands, token = outs[:ns], outs[ns:ns + n], outs[ns + n:ns + 2 * n], outs[-1]
    res, at = [], 0
    for gi, sz in enumerate(sizes):
        res.append((shards[at:at + sz], lands[at:at + sz], sems[2 * gi], sems[2 * gi + 1]))
        at += sz
    return res, token


def _gather_pass(group, started, after, name):
    shards, lands, send, recv = started
    n = len(group)
    split_ix = [i for i, (_, split) in enumerate(group) if split]

    def body(*refs):
        lnds, send1, recv1 = refs[n:2 * n], refs[2 * n], refs[2 * n + 1]
        outs = refs[2 * n + 3:]
        send2, recv2, token = outs[2 * n], outs[2 * n + 1], outs[2 * n + 2]
        x, y, c, chips = _place()
        sib = (x, y, 1 - c)
        for i, (shard, split) in enumerate(group):
            rows = _half(shard.shape[0], c) if split else slice(None)
            for k, (cx, cy) in enumerate(chips):
                landed = lnds[i].at[2 * cx + cy, rows]
                cp = _remote(landed, landed, send1.at[3 * i + k], recv1.at[3 * i + k], sib)
                cp.wait_send()
                cp.wait_recv()
        for i2, i in enumerate(split_ix):
            rows = _half(group[i][0].shape[0], c)
            for k, (cx, cy) in enumerate(chips):
                landed = lnds[i].at[2 * cx + cy, rows]
                _remote(landed, landed, send2.at[3 * i2 + k], recv2.at[3 * i2 + k], sib).start()
        token[...] = jnp.zeros_like(token)

    n2 = len(split_ix)
    out_shape = ([pltpu.HBM(a.shape, a.dtype) for a in (*shards, *lands)]
                 + [pltpu.SemaphoreType.DMA((3 * n2,)), pltpu.SemaphoreType.DMA((3 * n2,)), jax.ShapeDtypeStruct((8, LANES), F32)])
    outs = pl.pallas_call(
        body, name=name, in_specs=[HBM] * (2 * n) + [SEM, SEM, ANY],
        out_specs=[HBM] * (2 * n) + [SEM, SEM, pl.BlockSpec(memory_space=pltpu.VMEM)],
        out_shape=out_shape, input_output_aliases={i: i for i in range(2 * n)},
        compiler_params=pltpu.CompilerParams(has_side_effects=EFFECT),
    )(*shards, *lands, send, recv, after)
    return outs[:n], (outs[n:2 * n], outs[2 * n], outs[2 * n + 1]), outs[2 * n + 2]


def _gather_wait(group, passed, after, name):
    lands, send2, recv2 = passed
    n = len(group)
    split_ix = [i for i, (_, split) in enumerate(group) if split]

    def body(*refs):
        lnds, s2, r2 = refs[:n], refs[n], refs[n + 1]
        x, y, c, chips = _place()
        sib = (x, y, 1 - c)
        for i2, i in enumerate(split_ix):
            rows = _half(group[i][0].shape[0], 1 - c)
            for k, (cx, cy) in enumerate(chips):
                landed = lnds[i].at[2 * cx + cy, rows]
                cp = _remote(landed, landed, s2.at[3 * i2 + k], r2.at[3 * i2 + k], sib)
                cp.wait_send()
                cp.wait_recv()

    return pl.pallas_call(
        body, name=name, in_specs=[HBM] * n + [SEM, SEM, ANY], out_specs=[HBM] * n,
        out_shape=[pltpu.HBM(a.shape, a.dtype) for a in lands], input_output_aliases={i: i for i in range(n)},
        compiler_params=pltpu.CompilerParams(has_side_effects=EFFECT),
    )(*lands, send2, recv2, after)


def _own_slab(land, shard):
    chip = 2 * lax.axis_index("x") + lax.axis_index("y")
    return lax.dynamic_update_slice(land, shard[None], (chip, 0, 0))


def _xfer_start(name, srcs, land_shapes, n_copies, copies, after):
    n = len(srcs)

    def body(*refs):
        src_refs, land_refs = refs[:n], refs[n:2 * n]
        send, recv, token = refs[2 * n + 1], refs[2 * n + 2], refs[-1]
        for cp in copies(src_refs, land_refs, send, recv):
            cp.start()
        token[...] = jnp.zeros_like(token)

    lands = [_in_hbm(lax.empty(shape, dtype)) for shape, dtype in land_shapes]
    out_shape = ([pltpu.SemaphoreType.DMA((n_copies,)), pltpu.SemaphoreType.DMA((n_copies,))]
                 + [pltpu.HBM(a.shape, a.dtype) for a in (*srcs, *lands)] + [jax.ShapeDtypeStruct((8, LANES), F32)])
    outs = pl.pallas_call(
        body, name=name, in_specs=[HBM] * (2 * n) + [ANY],
        out_specs=[SEM, SEM] + [HBM] * (2 * n) + [pl.BlockSpec(memory_space=pltpu.VMEM)],
        out_shape=out_shape, input_output_aliases={i: 2 + i for i in range(2 * n)},
        compiler_params=pltpu.CompilerParams(has_side_effects=EFFECT),
    )(*[_in_hbm(a) for a in srcs], *lands, after)
    return (outs[2:2 + n], outs[2 + n:2 + 2 * n], outs[0], outs[1]), outs[-1]


def _xfer_wait(name, started, copies, after):
    srcs, lands, send, recv = started
    n = len(srcs)

    def body(*refs):
        src_refs, land_refs, s_ref, r_ref = refs[:n], refs[n:2 * n], refs[2 * n], refs[2 * n + 1]
        for cp in copies(src_refs, land_refs, s_ref, r_ref):
            cp.wait_send()
            cp.wait_recv()

    outs = pl.pallas_call(
        body, name=name, in_specs=[HBM] * (2 * n) + [SEM, SEM, ANY], out_specs=[HBM] * (2 * n),
        out_shape=[pltpu.HBM(a.shape, a.dtype) for a in (*srcs, *lands)],
        input_output_aliases={i: i for i in range(2 * n)},
        compiler_params=pltpu.CompilerParams(has_side_effects=EFFECT),
    )(*srcs, *lands, send, recv, after)
    return outs[:n], outs[n:]


def _swap_copies(srcs, lands, send, recv):
    x, y, c, _ = _place()
    return [_remote(src.at[:, _half(src.shape[1], 1 - c)], land, send.at[i], recv.at[i], (x, y, 1 - c))
            for i, (src, land) in enumerate(zip(srcs, lands))]


def _scatter_copies(srcs, lands, send, recv):
    x, y, c, chips = _place()
    return [_remote(src.at[2 * cx + cy], land.at[k], send.at[3 * i + k], recv.at[3 * i + k], (cx, cy, c))
            for i, (src, land) in enumerate(zip(srcs, lands)) for k, (cx, cy) in enumerate(chips)]


def _join_copies(srcs, lands, send, recv):
    x, y, c, _ = _place()
    return [_remote(src, land, send.at[i], recv.at[i], (x, y, 1 - c)) for i, (src, land) in enumerate(zip(srcs, lands))]


class _Reducer:
    def __init__(self, place):
        self.place = place
        self.state = {}

    def swap(self, key, grads, after):
        shapes = [((g.shape[0], g.shape[1] // 2, g.shape[2]), g.dtype) for g in grads]
        self.state[key], token = _xfer_start("swap_start_" + key, grads, shapes, len(grads), _swap_copies, after)
        return token

    def to_chips(self, key, after):
        grads, from_sibling = _xfer_wait("swap_wait_" + key, self.state[key], _swap_copies, after)
        sums = [_add_halves(g, r, self.place) for g, r in zip(grads, from_sibling)]
        shapes = [((3,) + s.shape[1:], s.dtype) for s in sums]
        started, token = _xfer_start("scatter_start_" + key, sums, shapes, 3 * len(sums), _scatter_copies, sums[-1])
        self.state[key] = (grads, from_sibling, started)
        return token

    def to_core(self, key, after):
        grads, from_sibling, started = self.state[key]
        _, from_chips = _xfer_wait("scatter_wait_" + key, started, _scatter_copies, after)
        halves = [_sum_chips(g, r, rc, self.place) for g, r, rc in zip(grads, from_sibling, from_chips)]
        shapes = [(h.shape, h.dtype) for h in halves]
        self.state[key], token = _xfer_start("join_start_" + key, halves, shapes, len(halves), _join_copies, halves[-1])
        return token

    def finish(self, key, after):
        return _xfer_wait("join_wait_" + key, self.state.pop(key), _join_copies, after)


def _gather_packs(pack):
    def body(p_ref, o_ref, lsem, ssem, rsem):
        x, y, c, _ = _place()
        me = 4 * x + 2 * y + c
        local = pltpu.make_async_copy(p_ref, o_ref.at[me], lsem)
        local.start()
        cps = []
        for k in range(1, N_DEV):
            fx, fy, fc = (k >> 2) & 1, (k >> 1) & 1, k & 1
            to = (x ^ fx, y ^ fy, c ^ fc)
            cps.append(_remote(p_ref, o_ref.at[me], ssem.at[k - 1], rsem.at[k - 1], to))
        for cp in cps:
            cp.start()
        for k in range(1, N_DEV):
            fx, fy, fc = (k >> 2) & 1, (k >> 1) & 1, k & 1
            src = o_ref.at[4 * (x ^ fx) + 2 * (y ^ fy) + (c ^ fc)]
            _remote(src, src, ssem.at[k - 1], rsem.at[k - 1], (x, y, c)).wait_recv()
        for cp in cps:
            cp.wait_send()
        local.wait()

    return pl.pallas_call(
        body, name="gather_packs", in_specs=[ANY], out_specs=ANY,
        out_shape=jax.ShapeDtypeStruct((N_DEV,) + pack.shape, pack.dtype),
        scratch_shapes=[pltpu.SemaphoreType.DMA, pltpu.SemaphoreType.DMA((N_DEV - 1,)), pltpu.SemaphoreType.DMA((N_DEV - 1,))],
    )(pack)


LANE_TILES = (512, 896, 1408, 704, 384, 256, 128)


def _layer_grads(x, target, small, wg, rest_pass, rest_wait, red):
    s, d = x.shape
    f = wg["conv"].shape[1] // 2
    w_att = N_HEADS * HEAD_DIM
    in_splits = (w_att, w_att, w_att, N_HEADS, w_att, w_att, w_att, d, d)
    in_cols = sum(in_splits)
    cs = in_cols // N_CHIPS
    cp = wg["in"].shape[2]
    tm = min(s, 1024)
    t_in = _pick(cp, LANE_TILES)
    t_d = _pick(d, LANE_TILES)
    t_dq = _pick(d // N_CHIPS, LANE_TILES)
    t_w = _pick(w_att, LANE_TILES)
    t_up = _pick(2 * f // N_CHIPS, LANE_TILES)
    t_fq = _pick(f // N_CHIPS, LANE_TILES)
    offs = np.cumsum(in_splits)[:-1].tolist()

    h1 = _norm_fwd(x, small["g_attn"], group=d, name="rms1_fwd")
    proj_p = _mm(h1, wg["in"], mode="nn", b_kind="col", tm=tm, tn=t_in, tk=t_d, name="mm_in")
    proj = proj_p.reshape(s, N_CHIPS, cp)[:, :, :cs].reshape(s, in_cols)
    qa, ka, va, fa, qb, kb, vb, ga, gb = jnp.split(proj, offs, axis=-1)
    gains = {n: small[n].reshape(1, w_att) for n in ("g_q_fox", "g_k_fox", "g_q_dil", "g_k_dil")}
    qa_n = _norm_fwd(qa, gains["g_q_fox"], group=HEAD_DIM, name="qnorm_fox")
    ka_n = _norm_fwd(ka, gains["g_k_fox"], group=HEAD_DIM, name="knorm_fox")
    qb_n = _norm_fwd(qb, gains["g_q_dil"], group=HEAD_DIM, name="qnorm_dil")
    kb_n = _norm_fwd(kb, gains["g_k_dil"], group=HEAD_DIM, name="knorm_dil")
    va_b, vb_b = va.astype(BF16), vb.astype(BF16)
    fa_t = fa.T
    b_f = small["b_forget"].reshape(N_HEADS, 1)
    c_f = _forget_fwd(fa_t, b_f)
    slopes = jnp.asarray(2.0 ** (-8.0 * np.arange(1, N_HEADS + 1) / N_HEADS), dtype=F32)
    a_d = -(slopes[:, None] * jnp.arange(s, dtype=F32)[None, :])
    rows_f, cols_f = c_f[:, :, None], c_f[:, None, :]
    rows_d, cols_d = a_d[:, :, None], a_d[:, None, :]
    o_a, o_a32, lse_a = _attn_fwd(qa_n, ka_n, va_b, rows_f, cols_f, dilated=False, name="attn_fox_fwd")
    token = rest_pass(o_a)
    rows_d = rows_d + token[0, 0]
    o_b, o_b32, lse_b = _attn_fwd(qb_n, kb_n, vb_b, rows_d, cols_d, dilated=True, name="attn_dil_fwd")
    wg = dict(wg, **rest_wait(o_b))
    pa = _mm(o_a, wg["brf"], mode="nn", b_kind="col", tm=tm, tn=t_dq, tk=t_w, name="mm_brf")
    pb = _mm(o_b, wg["brd"], mode="nn", b_kind="col", tm=tm, tn=t_dq, tk=t_w, name="mm_brd")
    merged = _gate_fwd(ga, gb, pa, pb)
    x1 = _mm(merged, wg["out"], mode="nn", b_kind="row", res=x, tm=tm, tn=t_d, tk=t_dq, name="mm_out")
    h2 = _norm_fwd(x1, small["g_ffn"], group=d, name="rms2_fwd")
    u = _mm(h2, wg["up"], mode="nn", b_kind="col", tm=tm, tn=t_up, tk=t_d, name="mm_up")
    act = _conv_glu_fwd(u, wg["conv"], wg["bconv"])
    y = _mm(act, wg["down"], mode="nn", b_kind="row", res=x1, tm=tm, tn=t_d, tk=t_fq, name="mm_down")
    dy_f, dy_b, loss_blk = _loss_head(y, target)

    d_act = _mm(dy_b, wg["down"], mode="nt", b_kind="row", tm=tm, tn=t_fq, tk=t_d, name="mm_down_dx")
    g_down = _mm(act, dy_b, mode="tn", out_dtype=BF16, out_kind="row", tm=t_fq, tn=t_d, tk=tm, name="mm_down_dw")
    tok = red.swap("down", [g_down], g_down)
    du_g, du_v, st_g, st_v = _conv_glu_bwd(u, d_act, wg["conv"] + tok[0, 0], wg["bconv"])
    tok = red.to_chips("down", du_g)
    du = jnp.concatenate([du_g, du_v], axis=1)
    g_up = _mm(h2, du, mode="tn", out_dtype=BF16, out_kind="col", tm=t_d, tn=t_up, tk=tm, name="mm_up_dw", deps=(tok,))
    tok = red.to_core("down", g_up)
    tok2 = red.swap("up", [g_up], g_up)
    dh2 = _mm(du, wg["up"], mode="nt", b_kind="col", tm=tm, tn=t_d, tk=t_up, name="mm_up_dx", deps=(tok, tok2))
    tok = red.to_chips("up", dh2)
    dx1_b, dx1_f, dg_ffn = _norm_bwd(dh2, x1, small["g_ffn"], group=d, res=dy_f, out_dtypes=(BF16, F32), name="rms2_bwd")
    d_merged = _mm(dx1_b, wg["out"], mode="nt", b_kind="row", tm=tm, tn=t_dq, tk=t_d, name="mm_out_dx", deps=(tok,))
    g_out = _mm(merged, dx1_b, mode="tn", out_dtype=BF16, out_kind="row", tm=t_dq, tn=t_d, tk=tm, name="mm_out_dw")
    dpa, dpb, dga, dgb = _gate_bwd(d_merged, ga, gb, pa, pb)
    do_a = _mm(dpa, wg["brf"], mode="nt", b_kind="col", out_dtype=BF16, tm=tm, tn=t_w, tk=t_dq, name="mm_brf_dx")
    do_b = _mm(dpb, wg["brd"], mode="nt", b_kind="col", out_dtype=BF16, tm=tm, tn=t_w, tk=t_dq, name="mm_brd_dx")
    g_brf = _mm(o_a, dpa, mode="tn", out_dtype=BF16, out_kind="col", tm=t_w, tn=t_dq, tk=tm, name="mm_brf_dw")
    g_brd = _mm(o_b, dpb, mode="tn", out_dtype=BF16, out_kind="col", tm=t_w, tn=t_dq, tk=tm, name="mm_brd_dw")
    tok = red.swap("mix", [g_out, g_brf, g_brd], g_brd)
    dqa_n, dka_n, dva, dac_a = _attn_bwd(qa_n, ka_n, va_b, o_a32, do_a, lse_a, rows_f + tok[0, 0], cols_f, dilated=False, name="attn_fox_bwd")
    tok = red.to_core("up", dqa_n)
    tok2 = red.to_chips("mix", dqa_n)
    dqb_n, dkb_n, dvb, _ = _attn_bwd(qb_n, kb_n, vb_b, o_b32, do_b, lse_b, rows_d + (tok[0, 0] + tok2[0, 0]), cols_d, dilated=True, name="attn_dil_bwd")
    tok = red.to_core("mix", dqb_n)
    dqa, dg_qf = _norm_bwd(dqa_n, qa, gains["g_q_fox"], group=HEAD_DIM, name="qnorm_fox_bwd")
    dka, dg_kf = _norm_bwd(dka_n, ka, gains["g_k_fox"], group=HEAD_DIM, name="knorm_fox_bwd")
    dqb, dg_qd = _norm_bwd(dqb_n, qb, gains["g_q_dil"], group=HEAD_DIM, name="qnorm_dil_bwd")
    dkb, dg_kd = _norm_bwd(dkb_n, kb, gains["g_k_dil"], group=HEAD_DIM, name="knorm_dil_bwd")
    dfa_t, db_f = _forget_bwd(dac_a[:, 0, :], fa_t, b_f)
    dproj = jnp.concatenate([dqa, dka, dva.astype(BF16), dfa_t.T.astype(BF16), dqb, dkb, dvb.astype(BF16), dga, dgb], axis=1)
    dproj_p = jnp.pad(dproj.reshape(s, N_CHIPS, cs), ((0, 0), (0, 0), (0, cp - cs))).reshape(s, N_CHIPS * cp)
    g_in = _mm(h1, dproj_p, mode="tn", out_dtype=BF16, out_kind="col", tm=t_d, tn=t_in, tk=tm, name="mm_in_dw", deps=(tok,))
    tok = red.swap("in", [g_in], g_in)
    dh1 = _mm(dproj_p, wg["in"], mode="nt", b_kind="col", tm=tm, tn=t_d, tk=t_in, name="mm_in_dx", deps=(tok,))
    tok = red.to_chips("in", dh1)
    grad_x, dg_attn = _norm_bwd(dh1, x, small["g_attn"], group=d, res=dx1_f, out_dtypes=(F32,), name="rms1_bwd")

    small_grads = {
        "g_attn": dg_attn, "b_forget": db_f.reshape(1, N_HEADS),
        "g_q_fox": dg_qf, "g_k_fox": dg_kf, "g_q_dil": dg_qd, "g_k_dil": dg_kd, "g_ffn": dg_ffn,
        "w_conv": jnp.concatenate([st_g[0:3], st_v[0:3]], axis=1),
        "b_conv": jnp.concatenate([st_g[3:4], st_v[3:4]], axis=1),
        "loss": loss_blk[0:1, 0:1],
    }
    return small_grads, grad_x


SMALL_ORDER = ("g_attn", "b_forget", "g_q_fox", "g_k_fox", "g_q_dil", "g_k_dil", "g_ffn", "w_conv", "b_conv", "loss")
WEIGHT_ORDER = ("g_attn", "w_in", "b_forget", "g_q_fox", "g_k_fox", "g_q_dil", "g_k_dil", "w_br_fox", "w_br_dil",
                "w_out", "g_ffn", "w_up", "w_conv", "b_conv", "w_down")
BIG = {"w_in": "in", "w_br_fox": "brf", "w_br_dil": "brd", "w_out": "out", "w_up": "up", "w_down": "down"}


def kernel(x, g_attn, w_in, b_forget, g_q_fox, g_k_fox, g_q_dil, g_k_dil, w_br_fox, w_br_dil, w_out, g_ffn, w_up, w_conv, b_conv, w_down, loss_target, m_g_attn, m_w_in, m_b_forget, m_g_q_fox, m_g_k_fox, m_g_q_dil, m_g_k_dil, m_w_br_fox, m_w_br_dil, m_w_out, m_g_ffn, m_w_up, m_w_conv, m_b_conv, m_w_down, v_g_attn, v_w_in, v_b_forget, v_g_q_fox, v_g_k_fox, v_g_q_dil, v_g_k_dil, v_w_br_fox, v_w_br_dil, v_w_out, v_g_ffn, v_w_up, v_w_conv, v_b_conv, v_w_down):
    w = dict(g_attn=g_attn, w_in=w_in, b_forget=b_forget, g_q_fox=g_q_fox, g_k_fox=g_k_fox, g_q_dil=g_q_dil,
             g_k_dil=g_k_dil, w_br_fox=w_br_fox, w_br_dil=w_br_dil, w_out=w_out, g_ffn=g_ffn, w_up=w_up,
             w_conv=w_conv, b_conv=b_conv, w_down=w_down)
    m = dict(g_attn=m_g_attn, w_in=m_w_in, b_forget=m_b_forget, g_q_fox=m_g_q_fox, g_k_fox=m_g_k_fox,
             g_q_dil=m_g_q_dil, g_k_dil=m_g_k_dil, w_br_fox=m_w_br_fox, w_br_dil=m_w_br_dil, w_out=m_w_out,
             g_ffn=m_g_ffn, w_up=m_w_up, w_conv=m_w_conv, b_conv=m_b_conv, w_down=m_w_down)
    v = dict(g_attn=v_g_attn, w_in=v_w_in, b_forget=v_b_forget, g_q_fox=v_g_q_fox, g_k_fox=v_g_k_fox,
             g_q_dil=v_g_q_dil, g_k_dil=v_g_k_dil, w_br_fox=v_w_br_fox, w_br_dil=v_w_br_dil, w_out=v_w_out,
             g_ffn=v_g_ffn, w_up=v_w_up, w_conv=v_w_conv, b_conv=v_b_conv, w_down=v_w_down)
    xi, yi, ci = lax.axis_index("x"), lax.axis_index("y"), lax.axis_index("c")
    chip = (2 * xi + yi).astype(jnp.int32)
    c_idx = ci.astype(jnp.int32).reshape(1)
    j_idx = chip.reshape(1)

    cs = w_in.shape[2]
    cp = _round_up(cs, LANES)
    shards = {
        "in": jnp.pad(w_in[0].astype(BF16), ((0, 0), (0, cp - cs))),
        "brf": w_br_fox[0].astype(BF16), "brd": w_br_dil[0].astype(BF16), "out": w_out[0].astype(BF16),
        "up": w_up[0].astype(BF16), "down": w_down[0].astype(BF16),
    }
    names = tuple(shards)
    conv_pad = jnp.pad(w_conv[0], ((0, 8 - w_conv.shape[1]), (0, 0)))
    first = [(shards["in"], True), (conv_pad, False)]
    rest_names = names[1:]
    rest = [(shards[n], True) for n in rest_names]
    (started_first, started_rest), token = _gather_start([first, rest])
    own_first, passed_first, token = _gather_pass(first, started_first, token, "gather_pass_in")
    land_in, land_conv = _gather_wait(first, passed_first, token, "gather_wait_in")
    conv_all = _own_slab(land_conv, own_first[1])
    wg = {"in": _own_slab(land_in, own_first[0]), "bconv": b_conv,
          "conv": jnp.transpose(conv_all[:, :w_conv.shape[1], :], (1, 0, 2)).reshape(w_conv.shape[1], -1)}
    small = {n: w[n] for n in ("g_attn", "b_forget", "g_q_fox", "g_k_fox", "g_q_dil", "g_k_dil", "g_ffn")}
    small = {n: (a[0] if a.ndim == 3 else a) for n, a in small.items()}
    in_flight = {}

    def rest_pass(after):
        in_flight["own"], in_flight["passed"], tok = _gather_pass(rest, started_rest, after, "gather_pass_rest")
        return tok

    def rest_wait(after):
        lands = _gather_wait(rest, in_flight["passed"], after, "gather_wait_rest")
        return {n: _own_slab(land, own) for n, land, own in zip(rest_names, lands, in_flight["own"])}

    reducer = _Reducer(jnp.stack([chip, ci.astype(jnp.int32)]))
    small_grads, grad_x = _layer_grads(x[0], loss_target[0], small, wg, rest_pass, rest_wait, reducer)

    mine, theirs = {}, {}
    for key, members in (("down", ("down",)), ("up", ("up",)), ("mix", ("out", "brf", "brd"))):
        mine_k, theirs_k = reducer.finish(key, grad_x)
        mine.update(zip(members, mine_k))
        theirs.update(zip(members, theirs_k))

    flat = jnp.concatenate([small_grads[n].reshape(-1) for n in SMALL_ORDER])
    rows = _round_up(flat.shape[0], 8 * LANES) // LANES
    pack = jnp.pad(flat, (0, rows * LANES - flat.shape[0])).reshape(rows, LANES)
    total = _sum_devices(_gather_packs(pack)).reshape(-1)
    red, at = {}, 0
    for n in SMALL_ORDER:
        size = small_grads[n].size
        red[n] = total[at:at + size].reshape(small_grads[n].shape)
        at += size
    loss = red["loss"].reshape(())
    c2 = w_conv.shape[2]
    red["w_conv"] = lax.dynamic_slice_in_dim(red["w_conv"], chip * c2, c2, axis=1)

    g_out, d_out, m_out, v_out = {}, {}, {}, {}
    last = [n for n in WEIGHT_ORDER if n != "w_in"] + ["w_in"]
    for n in last:
        shape = w[n].shape
        r2 = (shape[-2], shape[-1]) if n not in ("g_attn", "b_forget", "g_ffn", "b_conv") else (1, shape[-1])
        if n == "w_in":
            reducer.to_core("in", v_out[last[-2]])
            (mine_in,), (theirs_in,) = reducer.finish("in", v_out[last[-2]])
            mine["in"], theirs["in"] = mine_in[:, :cs], theirs_in[:, :cs]
        if n in BIG:
            g2, dl, mn, vn = _adamw_halves(w[n].reshape(r2), mine[BIG[n]], theirs[BIG[n]], c_idx,
                                           m[n].reshape(r2), v[n].reshape(r2), name="adamw_" + n)
        else:
            g2 = red[n].reshape(r2)
            dl, mn, vn = _adamw(w[n].reshape(r2), g2, m[n].reshape(r2), v[n].reshape(r2), name="adamw_" + n)
        g_out[n], d_out[n], m_out[n], v_out[n] = (a.reshape(shape) for a in (g2, dl, mn, vn))

    return (loss, grad_x[None], *[g_out[n] for n in WEIGHT_ORDER], *[d_out[n] for n in WEIGHT_ORDER],
            *[m_out[n] for n in WEIGHT_ORDER], *[v_out[n] for n in WEIGHT_ORDER])
```

```python
import functools
import math

import jax
import jax.numpy as jnp
import numpy as np
from jax import lax
from jax.experimental import pallas as pl
from jax.experimental.pallas import tpu as pltpu

F32 = jnp.float32
BF16 = jnp.bfloat16
HEAD_DIM = 128
N_HEADS = 8
EPS = 1e-6
NEG = -1e30
N_CHIPS = 4
N_DEV = 8
LANES = 128
VMEM_LIMIT_BYTES = 56 * 1024 * 1024
DIL_PATTERNS = ((128, 1), (512, 4), (2048, 16))
ATTN_TILE = 512
ADAM_LR, ADAM_B1, ADAM_B2, ADAM_EPS, ADAM_WD, ADAM_STEP = 0.001, 0.9, 0.999, 1e-08, 0.01, 10
MESH = pl.DeviceIdType.MESH


def _params(*sem):
    return pltpu.CompilerParams(dimension_semantics=sem, vmem_limit_bytes=VMEM_LIMIT_BYTES)


def _round_up(n, m):
    return -(-n // m) * m


def _pick(dim, prefs):
    for p in prefs:
        if dim % p == 0:
            return p
    raise ValueError(f"no tile for {dim} in {prefs}")


def _logical_shape(arr, kind):
    if kind is None:
        return arr.shape
    s, r, c = arr.shape
    return (r, s * c) if kind == "col" else (s * r, c)


def _spec(shape, kind, br, bc, fi, fj):
    if kind is None:
        return pl.BlockSpec((br, bc), lambda *g: (fi(*g), fj(*g)))
    _, r, c = shape
    if kind == "col":
        nb = c // bc
        assert nb * bc == c, (shape, bc)
        return pl.BlockSpec((None, br, bc), lambda *g: (fj(*g) // nb, fi(*g), fj(*g) % nb))
    nb = r // br
    assert nb * br == r, (shape, br)
    return pl.BlockSpec((None, br, bc), lambda *g: (fi(*g) // nb, fi(*g) % nb, fj(*g)))


def _mm(a, b, *, mode, tm, tn, tk, name, a_kind=None, b_kind=None, out_kind=None,
        out_dtype=F32, res=None, deps=()):
    la, lb = _logical_shape(a, a_kind), _logical_shape(b, b_kind)
    if mode == "nn":
        (m, k), (k2, n) = la, lb
    elif mode == "nt":
        (m, k), (n, k2) = la, lb
    else:
        (k, m), (k2, n) = la, lb
    assert k == k2, (name, la, lb)
    assert m % tm == 0 and n % tn == 0 and k % tk == 0, (name, m, n, k, tm, tn, tk)
    nk = k // tk
    im = lambda i, j, l: i
    jn = lambda i, j, l: j
    lk = lambda i, j, l: l
    if mode == "tn":
        a_spec = _spec(a.shape, a_kind, tk, tm, lk, im)
        dims = (((0,), (0,)), ((), ()))
    else:
        a_spec = _spec(a.shape, a_kind, tm, tk, im, lk)
        dims = (((1,), (1,)), ((), ())) if mode == "nt" else (((1,), (0,)), ((), ()))
    if mode == "nt":
        b_spec = _spec(b.shape, b_kind, tn, tk, jn, lk)
    else:
        b_spec = _spec(b.shape, b_kind, tk, tn, lk, jn)
    if out_kind is None:
        oshape = (m, n)
    elif out_kind == "col":
        oshape = (N_CHIPS, m, n // N_CHIPS)
    else:
        oshape = (N_CHIPS, m // N_CHIPS, n)
    o_spec = _spec(oshape, out_kind, tm, tn, im, jn)
    in_specs = [a_spec, b_spec]
    args = [a, b]
    if res is not None:
        in_specs.append(pl.BlockSpec((tm, tn), lambda i, j, l: (i, j)))
        args.append(res)
    in_specs += [pl.BlockSpec(memory_space=pl.ANY)] * len(deps)
    args += list(deps)

    def body(*refs):
        a_ref, b_ref = refs[0], refs[1]
        res_ref = refs[2] if res is not None else None
        o_ref, acc_ref = refs[-2], refs[-1]
        step = pl.program_id(2)

        @pl.when(step == 0)
        def _():
            acc_ref[...] = jnp.zeros_like(acc_ref)

        acc_ref[...] += lax.dot_general(a_ref[...], b_ref[...], dims, preferred_element_type=F32)

        @pl.when(step == nk - 1)
        def _():
            out = acc_ref[...]
            if res_ref is not None:
                out = out + res_ref[...]
            o_ref[...] = out.astype(o_ref.dtype)

    return pl.pallas_call(
        body, name=name, grid=(m // tm, n // tn, nk),
        in_specs=in_specs, out_specs=o_spec,
        out_shape=jax.ShapeDtypeStruct(oshape, out_dtype),
        scratch_shapes=[pltpu.VMEM((tm, tn), F32)],
        compiler_params=_params("parallel", "parallel", "arbitrary"),
    )(*args)


def _pieces(splits, cs, cp):
    out, g0 = [], 0
    for width in splits:
        g1, runs = g0 + width, []
        for j in range(N_CHIPS):
            a, b = max(g0, cs * j), min(g1, cs * (j + 1))
            if a < b:
                runs.append((j * cp + a - cs * j, a - g0, b - a))
        out.append(runs)
        g0 = g1
    return out


def _proj_split(proj_p, splits, cs, dtypes, tm=128):
    s, wp = proj_p.shape
    pieces = _pieces(splits, cs, wp // N_CHIPS)

    def body(p_ref, *o_refs):
        for o_ref, runs in zip(o_refs, pieces):
            for src, dst, n in runs:
                o_ref[:, dst:dst + n] = p_ref[:, src:src + n].astype(o_ref.dtype)

    return pl.pallas_call(
        body, name="proj_split", grid=(s // tm,), in_specs=[pl.BlockSpec((tm, wp), lambda i: (i, 0))],
        out_specs=[pl.BlockSpec((tm, w), lambda i: (i, 0)) for w in splits],
        out_shape=[jax.ShapeDtypeStruct((s, w), dt) for w, dt in zip(splits, dtypes)],
        compiler_params=_params("parallel"),
    )(proj_p)


def _dproj_merge(parts, splits, cs, cp, tm=128):
    s = parts[0].shape[0]
    wp = N_CHIPS * cp
    pieces = _pieces(splits, cs, cp)

    def body(*refs):
        o_ref, stage = refs[-2], refs[-1]
        for j in range(N_CHIPS):
            stage[:, j * cp + cs:(j + 1) * cp] = jnp.zeros((tm, cp - cs), F32)
        for p_ref, runs in zip(refs, pieces):
            for dst, src, n in runs:
                stage[:, dst:dst + n] = p_ref[:, src:src + n].astype(F32)
        o_ref[...] = stage[...].astype(o_ref.dtype)

    return pl.pallas_call(
        body, name="dproj_merge", grid=(s // tm,),
        in_specs=[pl.BlockSpec((tm, w), lambda i: (i, 0)) for w in splits],
        out_specs=pl.BlockSpec((tm, wp), lambda i: (i, 0)),
        out_shape=jax.ShapeDtypeStruct((s, wp), BF16), scratch_shapes=[pltpu.VMEM((tm, wp), F32)],
        compiler_params=_params("parallel"),
    )(*parts)


def _norm_fwd(x, g, *, group, name, tm=256):
    s, w = x.shape
    ng = w // group

    def body(x_ref, g_ref, o_ref):
        for i in range(ng):
            cols = slice(i * group, (i + 1) * group)
            xv = x_ref[:, cols]
            r = lax.rsqrt(jnp.mean(xv * xv, axis=-1, keepdims=True) + EPS)
            o_ref[:, cols] = ((xv * r) * g_ref[:, cols]).astype(o_ref.dtype)

    return pl.pallas_call(
        body, name=name, grid=(s // tm,),
        in_specs=[pl.BlockSpec((tm, w), lambda i: (i, 0)), pl.BlockSpec((1, w), lambda i: (0, 0))],
        out_specs=pl.BlockSpec((tm, w), lambda i: (i, 0)),
        out_shape=jax.ShapeDtypeStruct((s, w), BF16),
        compiler_params=_params("parallel"),
    )(x, g)


def _norm_bwd(dy, x, g, *, group, name, res=None, out_dtypes=(BF16,), tm=256):
    s, w = x.shape
    ng = w // group
    n_in = 4 if res is not None else 3

    def body(*refs):
        dy_ref, x_ref, g_ref = refs[:3]
        res_ref = refs[3] if res is not None else None
        outs = refs[n_in:]
        dx_refs, dg_ref = outs[:-1], outs[-1]

        @pl.when(pl.program_id(0) == 0)
        def _():
            dg_ref[...] = jnp.zeros_like(dg_ref)

        for i in range(ng):
            cols = slice(i * group, (i + 1) * group)
            xv = x_ref[:, cols]
            dyv = dy_ref[:, cols].astype(F32)
            r = lax.rsqrt(jnp.mean(xv * xv, axis=-1, keepdims=True) + EPS)
            xr = xv * r
            dg_ref[:, cols] += jnp.sum(dyv * xr, axis=0, keepdims=True)
            gdy = dyv * g_ref[:, cols]
            dx = r * (gdy - xr * jnp.mean(gdy * xr, axis=-1, keepdims=True))
            if res_ref is not None:
                dx = dx + res_ref[:, cols]
            for dx_ref in dx_refs:
                dx_ref[:, cols] = dx.astype(dx_ref.dtype)

    row = pl.BlockSpec((tm, w), lambda i: (i, 0))
    vec = pl.BlockSpec((1, w), lambda i: (0, 0))
    in_specs = [row, row, vec] + ([row] if res is not None else [])
    args = [dy, x, g] + ([res] if res is not None else [])
    out_specs = [row] * len(out_dtypes) + [vec]
    out_shape = [jax.ShapeDtypeStruct((s, w), dt) for dt in out_dtypes] + [jax.ShapeDtypeStruct((1, w), F32)]
    return pl.pallas_call(
        body, name=name, grid=(s // tm,), in_specs=in_specs, out_specs=out_specs,
        out_shape=out_shape, compiler_params=_params("arbitrary"),
    )(*args)


def _split3(v):
    p1 = v.astype(BF16)
    r1 = v - p1.astype(F32)
    p2 = r1.astype(BF16)
    p3 = (r1 - p2.astype(F32)).astype(BF16)
    return p1, p2, p3


def _tri_sum(v, reverse, tcol=512):
    h, s = v.shape
    tcol = min(tcol, s)
    parts = _split3(v)
    outs = []
    for j in range(s // tcol):
        src = lax.broadcasted_iota(jnp.int32, (s, tcol), 0)
        dst = lax.broadcasted_iota(jnp.int32, (s, tcol), 1) + j * tcol
        keep = (src >= dst) if reverse else (src <= dst)
        tri = jnp.where(keep, 1.0, 0.0).astype(BF16)
        acc = jnp.zeros((h, tcol), F32)
        for p in parts:
            acc = acc + jnp.dot(p, tri, preferred_element_type=F32)
        outs.append(acc)
    return outs


def _forget_fwd(fa_t, b):
    h, s = fa_t.shape
    tcol = min(512, s)

    def body(f_ref, b_ref, c_ref):
        z = f_ref[...] + b_ref[...]
        logf = jnp.minimum(z, 0.0) - jnp.log(1.0 + jnp.exp(-jnp.abs(z)))
        for j, blk in enumerate(_tri_sum(logf, reverse=False, tcol=tcol)):
            c_ref[:, j * tcol:(j + 1) * tcol] = blk

    return pl.pallas_call(
        body, name="forget_fwd", out_shape=jax.ShapeDtypeStruct((h, s), F32),
        compiler_params=_params(),
    )(fa_t, b)


def _forget_bwd(dacol, fa_t, b):
    h, s = fa_t.shape
    tcol = min(512, s)

    def body(d_ref, f_ref, b_ref, dfa_ref, db_ref):
        z = f_ref[...] + b_ref[...]
        dc = -d_ref[...]
        total = jnp.zeros((h, 1), F32)
        for j, blk in enumerate(_tri_sum(dc, reverse=True, tcol=tcol)):
            cols = slice(j * tcol, (j + 1) * tcol)
            dfa = blk * (1.0 - jax.nn.sigmoid(z[:, cols]))
            dfa_ref[:, cols] = dfa
            total = total + jnp.sum(dfa, axis=-1, keepdims=True)
        db_ref[...] = total

    return pl.pallas_call(
        body, name="forget_bwd",
        out_shape=[jax.ShapeDtypeStruct((h, s), F32), jax.ShapeDtypeStruct((h, 1), F32)],
        compiler_params=_params(),
    )(dacol, fa_t, b)


def _logits(q, k, arow, acol, q0, k0, dilated):
    tq, tk = q.shape[0], k.shape[0]
    s = lax.dot_general(q, k, (((1,), (1,)), ((), ())), preferred_element_type=F32)
    s = s * (1.0 / math.sqrt(HEAD_DIM)) + arow - acol
    dist = (q0 + lax.broadcasted_iota(jnp.int32, (tq, tk), 0)) - (k0 + lax.broadcasted_iota(jnp.int32, (tq, tk), 1))
    valid = dist >= 0
    if dilated:
        mult = jnp.zeros((tq, tk), jnp.int32)
        for window, dil in DIL_PATTERNS:
            mult = mult + ((dist <= window) & ((dist & (dil - 1)) == 0)).astype(jnp.int32)
        s = s + jnp.where(mult == 3, math.log(3.0), jnp.where(mult == 2, math.log(2.0), 0.0))
        valid = valid & (mult > 0)
    return jnp.where(valid, s, NEG)


def _attn_fwd(q, k, v, arow, acol, *, dilated, name, tq=ATTN_TILE, tk=ATTN_TILE):
    s, w = q.shape
    nh = w // HEAD_DIM
    assert tq == tk
    tq = tk = min(tq, s)
    nq, nk = s // tq, s // tk

    def body(q_ref, k_ref, v_ref, ar_ref, ac_ref, o_ref, of_ref, lse_ref, m_ref, l_ref, acc_ref):
        qi, ki = pl.program_id(1), pl.program_id(2)

        @pl.when(ki == 0)
        def _():
            m_ref[...] = jnp.full_like(m_ref, NEG)
            l_ref[...] = jnp.zeros_like(l_ref)
            acc_ref[...] = jnp.zeros_like(acc_ref)

        @pl.when(ki <= qi)
        def _():
            sc = _logits(q_ref[...], k_ref[...], ar_ref[...], ac_ref[...], qi * tq, ki * tk, dilated)
            m_new = jnp.maximum(m_ref[...], jnp.max(sc, axis=-1, keepdims=True))
            alpha = jnp.exp(m_ref[...] - m_new)
            p = jnp.exp(sc - m_new)
            l_ref[...] = alpha * l_ref[...] + jnp.sum(p, axis=-1, keepdims=True)
            p_hi = p.astype(BF16)
            p_lo = (p - p_hi.astype(F32)).astype(BF16)
            vv = v_ref[...]
            acc_ref[...] = (alpha * acc_ref[...] + jnp.dot(p_hi, vv, preferred_element_type=F32)
                            + jnp.dot(p_lo, vv, preferred_element_type=F32))
            m_ref[...] = m_new

        @pl.when(ki == nk - 1)
        def _():
            out = acc_ref[...] / l_ref[...]
            o_ref[...] = out.astype(o_ref.dtype)
            of_ref[...] = out
            lse_ref[...] = m_ref[...] + jnp.log(l_ref[...])

    kv = pl.BlockSpec((tk, HEAD_DIM), lambda h, i, j: (jnp.minimum(j, i), h))
    return pl.pallas_call(
        body, name=name, grid=(nh, nq, nk),
        in_specs=[pl.BlockSpec((tq, HEAD_DIM), lambda h, i, j: (i, h)), kv, kv,
                  pl.BlockSpec((None, tq, 1), lambda h, i, j: (h, i, 0)),
                  pl.BlockSpec((None, 1, tk), lambda h, i, j: (h, 0, jnp.minimum(j, i)))],
        out_specs=[pl.BlockSpec((tq, HEAD_DIM), lambda h, i, j: (i, h)),
                   pl.BlockSpec((tq, HEAD_DIM), lambda h, i, j: (i, h)),
                   pl.BlockSpec((None, tq, 1), lambda h, i, j: (h, i, 0))],
        out_shape=[jax.ShapeDtypeStruct((s, w), BF16), jax.ShapeDtypeStruct((s, w), F32),
                   jax.ShapeDtypeStruct((nh, s, 1), F32)],
        scratch_shapes=[pltpu.VMEM((tq, 1), F32), pltpu.VMEM((tq, 1), F32), pltpu.VMEM((tq, HEAD_DIM), F32)],
        compiler_params=_params("parallel", "parallel", "arbitrary"),
    )(q, k, v, arow, acol)


def _attn_bwd(q, k, v, o, do, lse, arow, acol, *, dilated, name, tq=ATTN_TILE, tk=ATTN_TILE):
    s, w = q.shape
    nh = w // HEAD_DIM
    assert tq == tk
    tq = tk = min(tq, s)
    nq, nk = s // tq, s // tk
    scale = 1.0 / math.sqrt(HEAD_DIM)

    def body(q_ref, k_ref, v_ref, o_ref, do_ref, lse_ref, ar_ref, ac_ref,
             dq_ref, dk_ref, dv_ref, dac_ref, dk_acc, dv_acc, dac_acc):
        ki, qi = pl.program_id(1), pl.program_id(2)

        @pl.when((ki == 0) & (qi == 0))
        def _():
            dq_ref[...] = jnp.zeros_like(dq_ref)

        @pl.when(qi == 0)
        def _():
            dk_acc[...] = jnp.zeros_like(dk_acc)
            dv_acc[...] = jnp.zeros_like(dv_acc)
            dac_acc[...] = jnp.zeros_like(dac_acc)

        @pl.when(qi >= ki)
        def _():
            qv, kvv, dov = q_ref[...], k_ref[...], do_ref[...]
            sc = _logits(qv, kvv, ar_ref[...], ac_ref[...], qi * tq, ki * tk, dilated)
            p = jnp.exp(sc - lse_ref[...])
            dp = lax.dot_general(dov, v_ref[...], (((1,), (1,)), ((), ())), preferred_element_type=F32)
            delta = jnp.sum(dov.astype(F32) * o_ref[...].astype(F32), axis=-1, keepdims=True)
            ds = p * (dp - delta)
            dsb = ds.astype(BF16)
            dv_acc[...] += lax.dot_general(p.astype(BF16), dov, (((0,), (0,)), ((), ())), preferred_element_type=F32)
            dk_acc[...] += lax.dot_general(dsb, qv, (((0,), (0,)), ((), ())), preferred_element_type=F32)
            rows = pl.ds(pl.multiple_of(qi * tq, tq), tq)
            dq_ref[rows, :] += jnp.dot(dsb, kvv, preferred_element_type=F32) * scale
            dac_acc[...] += jnp.sum(ds, axis=0, keepdims=True)

        @pl.when(qi == nq - 1)
        def _():
            dk_ref[...] = dk_acc[...] * scale
            dv_ref[...] = dv_acc[...]
            dac_ref[...] = dac_acc[...]

    qs = pl.BlockSpec((tq, HEAD_DIM), lambda h, j, i: (jnp.maximum(i, j), h))
    ks = pl.BlockSpec((tk, HEAD_DIM), lambda h, j, i: (j, h))
    rowv = pl.BlockSpec((None, tq, 1), lambda h, j, i: (h, jnp.maximum(i, j), 0))
    colv = pl.BlockSpec((None, 1, tk), lambda h, j, i: (h, 0, j))
    return pl.pallas_call(
        body, name=name, grid=(nh, nk, nq),
        in_specs=[qs, ks, ks, qs, qs, rowv, rowv, colv],
        out_specs=[pl.BlockSpec((s, HEAD_DIM), lambda h, j, i: (0, h)), ks, ks, colv],
        out_shape=[jax.ShapeDtypeStruct((s, w), F32), jax.ShapeDtypeStruct((s, w), F32),
                   jax.ShapeDtypeStruct((s, w), F32), jax.ShapeDtypeStruct((nh, 1, s), F32)],
        scratch_shapes=[pltpu.VMEM((tk, HEAD_DIM), F32), pltpu.VMEM((tk, HEAD_DIM), F32), pltpu.VMEM((1, tk), F32)],
        compiler_params=_params("arbitrary", "arbitrary", "arbitrary"),
    )(q, k, v, o, do, lse, arow, acol)


def _gate_fwd(ga, gb, pa, pb, tm=256):
    s, d = ga.shape

    def body(ga_ref, gb_ref, pa_ref, pb_ref, o_ref):
        o_ref[...] = (jax.nn.sigmoid(ga_ref[...]) * pa_ref[...]
                      + jax.nn.sigmoid(gb_ref[...]) * pb_ref[...]).astype(o_ref.dtype)

    row = pl.BlockSpec((tm, d), lambda i: (i, 0))
    return pl.pallas_call(
        body, name="gate_fwd", grid=(s // tm,), in_specs=[row] * 4, out_specs=row,
        out_shape=jax.ShapeDtypeStruct((s, d), BF16), compiler_params=_params("parallel"),
    )(ga, gb, pa, pb)


def _gate_bwd(dm, ga, gb, pa, pb, tm=256):
    s, d = ga.shape

    def body(dm_ref, ga_ref, gb_ref, pa_ref, pb_ref, dpa_ref, dpb_ref, dga_ref, dgb_ref):
        dmv = dm_ref[...]
        for g_ref, p_ref, dp_ref, dg_ref in ((ga_ref, pa_ref, dpa_ref, dga_ref), (gb_ref, pb_ref, dpb_ref, dgb_ref)):
            sg = jax.nn.sigmoid(g_ref[...])
            dp_ref[...] = (dmv * sg).astype(BF16)
            dg_ref[...] = (dmv * p_ref[...] * (sg * (1.0 - sg))).astype(BF16)

    row = pl.BlockSpec((tm, d), lambda i: (i, 0))
    return pl.pallas_call(
        body, name="gate_bwd", grid=(s // tm,), in_specs=[row] * 5, out_specs=[row] * 4,
        out_shape=[jax.ShapeDtypeStruct((s, d), BF16)] * 4, compiler_params=_params("parallel"),
    )(dm, ga, gb, pa, pb)


def _shift_down(u, k):
    row = lax.broadcasted_iota(jnp.int32, u.shape, 0)
    return jnp.where(row >= k, pltpu.roll(u, k, 0), 0.0)


def _shift_up(u, k):
    n = u.shape[0]
    row = lax.broadcasted_iota(jnp.int32, u.shape, 0)
    return jnp.where(row < n - k, pltpu.roll(u, n - k, 0), 0.0)


def _conv3(u, wc, b):
    return wc[0:1, :] * _shift_down(u, 2) + wc[1:2, :] * _shift_down(u, 1) + wc[2:3, :] * u + b


def _conv_glu_fwd(u, wc, b, tn=256):
    s, f2 = u.shape
    f = f2 // 2
    nb = f // tn

    def body(ug_ref, uv_ref, wg_ref, wv_ref, bg_ref, bv_ref, o_ref):
        cg = _conv3(ug_ref[...], wg_ref[...], bg_ref[...])
        cv = _conv3(uv_ref[...], wv_ref[...], bv_ref[...])
        o_ref[...] = (cg * jax.nn.sigmoid(cg) * cv).astype(o_ref.dtype)

    def cols(rows, off):
        return pl.BlockSpec((rows, tn), lambda j: (0, j + off))

    return pl.pallas_call(
        body, name="conv_glu_fwd", grid=(nb,),
        in_specs=[cols(s, 0), cols(s, nb), cols(3, 0), cols(3, nb), cols(1, 0), cols(1, nb)],
        out_specs=cols(s, 0), out_shape=jax.ShapeDtypeStruct((s, f), BF16),
        compiler_params=_params("parallel"),
    )(u, u, wc, wc, b, b)


def _conv_glu_bwd(u, da, wc, b, tn=256):
    s, f2 = u.shape
    f = f2 // 2
    nb = f // tn

    def body(ug_ref, uv_ref, da_ref, wg_ref, wv_ref, bg_ref, bv_ref, dug_ref, duv_ref, sg_ref, sv_ref):
        ug, uv, wg, wv = ug_ref[...], uv_ref[...], wg_ref[...], wv_ref[...]
        cg = _conv3(ug, wg, bg_ref[...])
        cv = _conv3(uv, wv, bv_ref[...])
        sig = jax.nn.sigmoid(cg)
        dav = da_ref[...]
        dcv = dav * (cg * sig)
        dcg = dav * cv * (sig * (1.0 + cg * (1.0 - sig)))
        for dc, uu, w, du_ref, st_ref in ((dcg, ug, wg, dug_ref, sg_ref), (dcv, uv, wv, duv_ref, sv_ref)):
            du = w[2:3, :] * dc + w[1:2, :] * _shift_up(dc, 1) + w[0:1, :] * _shift_up(dc, 2)
            du_ref[...] = du.astype(BF16)
            st_ref[...] = jnp.zeros_like(st_ref)
            st_ref[0:1, :] = jnp.sum(dc * _shift_down(uu, 2), axis=0, keepdims=True)
            st_ref[1:2, :] = jnp.sum(dc * _shift_down(uu, 1), axis=0, keepdims=True)
            st_ref[2:3, :] = jnp.sum(dc * uu, axis=0, keepdims=True)
            st_ref[3:4, :] = jnp.sum(dc, axis=0, keepdims=True)

    def cols(rows, off):
        return pl.BlockSpec((rows, tn), lambda j: (0, j + off))

    return pl.pallas_call(
        body, name="conv_glu_bwd", grid=(nb,),
        in_specs=[cols(s, 0), cols(s, nb), cols(s, 0), cols(3, 0), cols(3, nb), cols(1, 0), cols(1, nb)],
        out_specs=[cols(s, 0), cols(s, 0), cols(8, 0), cols(8, 0)],
        out_shape=[jax.ShapeDtypeStruct((s, f), BF16), jax.ShapeDtypeStruct((s, f), BF16),
                   jax.ShapeDtypeStruct((8, f), F32), jax.ShapeDtypeStruct((8, f), F32)],
        compiler_params=_params("parallel"),
    )(u, u, da, wc, wc, b, b)


def _loss_head(y, target, tm=256):
    s, d = y.shape

    def body(y_ref, t_ref, dyf_ref, dyb_ref, l_ref):
        @pl.when(pl.program_id(0) == 0)
        def _():
            l_ref[...] = jnp.zeros_like(l_ref)

        err = y_ref[...] - t_ref[...]
        dy = err * (1.0 / d)
        dyf_ref[...] = dy
        dyb_ref[...] = dy.astype(BF16)
        l_ref[...] += 0.5 * jnp.sum(jnp.sum(err * err, axis=-1, keepdims=True) * (1.0 / d), axis=0, keepdims=True)

    row = pl.BlockSpec((tm, d), lambda i: (i, 0))
    return pl.pallas_call(
        body, name="loss_head", grid=(s // tm,), in_specs=[row, row],
        out_specs=[row, row, pl.BlockSpec((8, LANES), lambda i: (0, 0))],
        out_shape=[jax.ShapeDtypeStruct((s, d), F32), jax.ShapeDtypeStruct((s, d), BF16),
                   jax.ShapeDtypeStruct((8, LANES), F32)],
        compiler_params=_params("arbitrary"),
    )(y, target)


ROW_TILES = (256, 128, 64, 32, 16, 8)
BLOCK_BYTES = 1 << 20


def _add_halves(g, r1, place):
    ns, r, c = g.shape
    rh = r // 2
    tr = _pick(rh, ROW_TILES)
    g4 = g.reshape(ns, 2, rh, c)

    def body(p_ref, g_ref, r_ref, o_ref):
        o_ref[...] = (g_ref[...].astype(F32) + r_ref[...].astype(F32)).astype(o_ref.dtype)

    return pl.pallas_call(
        body, name="add_halves",
        grid_spec=pltpu.PrefetchScalarGridSpec(
            num_scalar_prefetch=1, grid=(ns, rh // tr),
            in_specs=[pl.BlockSpec((None, None, tr, c), lambda s, i, pr: (s, pr[1], i, 0)),
                      pl.BlockSpec((None, tr, c), lambda s, i, pr: (s, i, 0))],
            out_specs=pl.BlockSpec((None, tr, c), lambda s, i, pr: (s, i, 0))),
        out_shape=jax.ShapeDtypeStruct((ns, rh, c), BF16),
        compiler_params=_params("parallel", "parallel"),
    )(place, g4, r1)


def _sum_chips(g, r1, recv, place):
    ns, r, c = g.shape
    rh = r // 2
    tr = _pick(rh, ROW_TILES)
    g4 = g.reshape(ns, 2, rh, c)

    def body(p_ref, g_ref, r_ref, t0_ref, t1_ref, t2_ref, o_ref):
        own = g_ref[...].astype(F32) + r_ref[...].astype(F32)
        o_ref[...] = ((own + t0_ref[...].astype(F32)) + t1_ref[...].astype(F32)) + t2_ref[...].astype(F32)

    def peer(k):
        return pl.BlockSpec((None, tr, c), lambda i, pr: (k, i, 0))

    return pl.pallas_call(
        body, name="sum_chips",
        grid_spec=pltpu.PrefetchScalarGridSpec(
            num_scalar_prefetch=1, grid=(rh // tr,),
            in_specs=[pl.BlockSpec((None, None, tr, c), lambda i, pr: (pr[0], pr[1], i, 0)),
                      pl.BlockSpec((None, tr, c), lambda i, pr: (pr[0], i, 0)), peer(0), peer(1), peer(2)],
            out_specs=pl.BlockSpec((tr, c), lambda i, pr: (i, 0))),
        out_shape=jax.ShapeDtypeStruct((rh, c), F32),
        compiler_params=_params("parallel"),
    )(place, g4, r1, recv, recv, recv)


def _sum_devices(packs):
    n, r, c = packs.shape

    def body(p_ref, o_ref):
        acc = p_ref[0]
        for d in range(1, n):
            acc = acc + p_ref[d]
        o_ref[...] = acc

    return pl.pallas_call(
        body, name="sum_devices", out_shape=jax.ShapeDtypeStruct((r, c), F32), compiler_params=_params(),
    )(packs)


def _adamw_update(wv, gv, mv, vv):
    c1 = 1.0 - ADAM_B1 ** ADAM_STEP
    c2 = 1.0 - ADAM_B2 ** ADAM_STEP
    mn = ADAM_B1 * mv + (1.0 - ADAM_B1) * gv
    vn = ADAM_B2 * vv + (1.0 - ADAM_B2) * (gv * gv)
    m_hat = mn / c1
    v_hat = vn / c2
    return -ADAM_LR * (m_hat / (jnp.sqrt(v_hat) + ADAM_EPS) + ADAM_WD * wv), mn, vn


def _adamw(w, g, m, v, name):
    r, c = w.shape
    tr = _pick(r, ROW_TILES) if r >= 8 else r

    def body(w_ref, g_ref, m_ref, v_ref, d_ref, mo_ref, vo_ref):
        d_ref[...], mo_ref[...], vo_ref[...] = _adamw_update(w_ref[...], g_ref[...], m_ref[...], v_ref[...])

    blk = pl.BlockSpec((tr, c), lambda i: (i, 0))
    return pl.pallas_call(
        body, name=name, grid=(r // tr,), in_specs=[blk] * 4, out_specs=[blk] * 3,
        out_shape=[jax.ShapeDtypeStruct((r, c), F32)] * 3, compiler_params=_params("parallel"),
    )(w, g, m, v)


def _adamw_halves(w, mine, theirs, c_idx, m, v, name):
    r, c = w.shape
    rh = r // 2
    tr = _pick(rh, [t for t in ROW_TILES if t * c * 4 <= BLOCK_BYTES])
    nb = rh // tr

    def body(c_ref, w_ref, a_ref, b_ref, m_ref, v_ref, g_ref, d_ref, mo_ref, vo_ref):
        gv = jnp.where(pl.program_id(0) // nb == c_ref[0], a_ref[...], b_ref[...])
        g_ref[...] = gv
        d_ref[...], mo_ref[...], vo_ref[...] = _adamw_update(w_ref[...], gv, m_ref[...], v_ref[...])

    blk = pl.BlockSpec((tr, c), lambda i, cr: (i, 0))
    mine_spec = pl.BlockSpec((tr, c), lambda i, cr: (jnp.clip(i - cr[0] * nb, 0, nb - 1), 0))
    theirs_spec = pl.BlockSpec((tr, c), lambda i, cr: (jnp.clip(i - (1 - cr[0]) * nb, 0, nb - 1), 0))
    return pl.pallas_call(
        body, name=name,
        grid_spec=pltpu.PrefetchScalarGridSpec(
            num_scalar_prefetch=1, grid=(r // tr,),
            in_specs=[blk, mine_spec, theirs_spec, blk, blk], out_specs=[blk] * 4),
        out_shape=[jax.ShapeDtypeStruct((r, c), F32)] * 4, compiler_params=_params("arbitrary"),
    )(c_idx, w, mine, theirs, m, v)


ANY = pl.BlockSpec(memory_space=pl.ANY)


def _place():
    x, y, c = lax.axis_index("x"), lax.axis_index("y"), lax.axis_index("c")
    chips = [(1 - x, y), (x, 1 - y), (1 - x, 1 - y)]
    return x, y, c, chips


def _remote(src, dst, send_sem, recv_sem, to):
    return pltpu.make_async_remote_copy(src_ref=src, dst_ref=dst, send_sem=send_sem, recv_sem=recv_sem,
                                        device_id=to, device_id_type=MESH)


HBM = pl.BlockSpec(memory_space=pltpu.HBM)
SEM = pl.BlockSpec(memory_space=pltpu.SEMAPHORE)
EFFECT = pltpu.SideEffectType.DATAFLOW_SIDE_EFFECTING


def _in_hbm(a):
    return pltpu.with_memory_space_constraint(a, pltpu.HBM)


def _half(ref_rows, who):
    return pl.ds(who * (ref_rows // 2), ref_rows // 2)


def _gather_start(groups):
    items = [it for g in groups for it in g]
    n = len(items)
    sizes = [len(g) for g in groups]

    def body(*refs):
        srcs, lands = refs[:n], refs[n:2 * n]
        sems = refs[2 * n:2 * n + 2 * len(groups)]
        token = refs[-1]
        x, y, c, chips = _place()
        j = 2 * x + y
        at = 0
        for gi, g in enumerate(groups):
            send, recv = sems[2 * gi], sems[2 * gi + 1]
            for i, (shard, split) in enumerate(g):
                src, land = srcs[at], lands[at]
                at += 1
                rows = _half(shard.shape[0], c) if split else slice(None)
                for k, chip in enumerate(chips):
                    _remote(src.at[rows], land.at[j, rows], send.at[3 * i + k], recv.at[3 * i + k], (*chip, c)).start()
        token[...] = jnp.zeros_like(token)

    sem_shapes = []
    for sz in sizes:
        sem_shapes += [pltpu.SemaphoreType.DMA((3 * sz,)), pltpu.SemaphoreType.DMA((3 * sz,))]
    out_shape = (sem_shapes + [pltpu.HBM(sh.shape, sh.dtype) for sh, _ in items]
                 + [pltpu.HBM((N_CHIPS,) + sh.shape, sh.dtype) for sh, _ in items]
                 + [jax.ShapeDtypeStruct((8, LANES), F32)])
    ns = len(sem_shapes)
    outs = pl.pallas_call(
        body, name="gather_start", in_specs=[HBM] * (2 * n),
        out_specs=[SEM] * ns + [HBM] * (2 * n) + [pl.BlockSpec(memory_space=pltpu.VMEM)],
        out_shape=out_shape, input_output_aliases={i: ns + i for i in range(2 * n)},
        compiler_params=pltpu.CompilerParams(has_side_effects=EFFECT),
    )(*[_in_hbm(sh) for sh, _ in items], *[_in_hbm(lax.empty((N_CHIPS,) + sh.shape, sh.dtype)) for sh, _ in items])
    sems, shards, lands, token = outs[:ns], outs[ns:ns + n], outs[ns + n:ns + 2 * n], outs[-1]
    res, at = [], 0
    for gi, sz in enumerate(sizes):
        res.append((shards[at:at + sz], lands[at:at + sz], sems[2 * gi], sems[2 * gi + 1]))
        at += sz
    return res, token


def _gather_pass(group, started, after, name):
    shards, lands, send, recv = started
    n = len(group)
    split_ix = [i for i, (_, split) in enumerate(group) if split]

    def body(*refs):
        lnds, send1, recv1 = refs[n:2 * n], refs[2 * n], refs[2 * n + 1]
        outs = refs[2 * n + 3:]
        send2, recv2, token = outs[2 * n], outs[2 * n + 1], outs[2 * n + 2]
        x, y, c, chips = _place()
        sib = (x, y, 1 - c)
        for i, (shard, split) in enumerate(group):
            rows = _half(shard.shape[0], c) if split else slice(None)
            for k, (cx, cy) in enumerate(chips):
                landed = lnds[i].at[2 * cx + cy, rows]
                cp = _remote(landed, landed, send1.at[3 * i + k], recv1.at[3 * i + k], sib)
                cp.wait_send()
                cp.wait_recv()
        for i2, i in enumerate(split_ix):
            rows = _half(group[i][0].shape[0], c)
            for k, (cx, cy) in enumerate(chips):
                landed = lnds[i].at[2 * cx + cy, rows]
                _remote(landed, landed, send2.at[3 * i2 + k], recv2.at[3 * i2 + k], sib).start()
        token[...] = jnp.zeros_like(token)

    n2 = len(split_ix)
    out_shape = ([pltpu.HBM(a.shape, a.dtype) for a in (*shards, *lands)]
                 + [pltpu.SemaphoreType.DMA((3 * n2,)), pltpu.SemaphoreType.DMA((3 * n2,)), jax.ShapeDtypeStruct((8, LANES), F32)])
    outs = pl.pallas_call(
        body, name=name, in_specs=[HBM] * (2 * n) + [SEM, SEM, ANY],
        out_specs=[HBM] * (2 * n) + [SEM, SEM, pl.BlockSpec(memory_space=pltpu.VMEM)],
        out_shape=out_shape, input_output_aliases={i: i for i in range(2 * n)},
        compiler_params=pltpu.CompilerParams(has_side_effects=EFFECT),
    )(*shards, *lands, send, recv, after)
    return outs[:n], (outs[n:2 * n], outs[2 * n], outs[2 * n + 1]), outs[2 * n + 2]


def _gather_wait(group, passed, after, name):
    lands, send2, recv2 = passed
    n = len(group)
    split_ix = [i for i, (_, split) in enumerate(group) if split]

    def body(*refs):
        lnds, s2, r2 = refs[:n], refs[n], refs[n + 1]
        x, y, c, chips = _place()
        sib = (x, y, 1 - c)
        for i2, i in enumerate(split_ix):
            rows = _half(group[i][0].shape[0], 1 - c)
            for k, (cx, cy) in enumerate(chips):
                landed = lnds[i].at[2 * cx + cy, rows]
                cp = _remote(landed, landed, s2.at[3 * i2 + k], r2.at[3 * i2 + k], sib)
                cp.wait_send()
                cp.wait_recv()

    return pl.pallas_call(
        body, name=name, in_specs=[HBM] * n + [SEM, SEM, ANY], out_specs=[HBM] * n,
        out_shape=[pltpu.HBM(a.shape, a.dtype) for a in lands], input_output_aliases={i: i for i in range(n)},
        compiler_params=pltpu.CompilerParams(has_side_effects=EFFECT),
    )(*lands, send2, recv2, after)


def _own_slab(land, shard):
    chip = 2 * lax.axis_index("x") + lax.axis_index("y")
    return lax.dynamic_update_slice(land, shard[None], (chip, 0, 0))


def _xfer_start(name, srcs, land_shapes, n_copies, copies, after):
    n = len(srcs)

    def body(*refs):
        src_refs, land_refs = refs[:n], refs[n:2 * n]
        send, recv, token = refs[2 * n + 1], refs[2 * n + 2], refs[-1]
        for cp in copies(src_refs, land_refs, send, recv):
            cp.start()
        token[...] = jnp.zeros_like(token)

    lands = [_in_hbm(lax.empty(shape, dtype)) for shape, dtype in land_shapes]
    out_shape = ([pltpu.SemaphoreType.DMA((n_copies,)), pltpu.SemaphoreType.DMA((n_copies,))]
                 + [pltpu.HBM(a.shape, a.dtype) for a in (*srcs, *lands)] + [jax.ShapeDtypeStruct((8, LANES), F32)])
    outs = pl.pallas_call(
        body, name=name, in_specs=[HBM] * (2 * n) + [ANY],
        out_specs=[SEM, SEM] + [HBM] * (2 * n) + [pl.BlockSpec(memory_space=pltpu.VMEM)],
        out_shape=out_shape, input_output_aliases={i: 2 + i for i in range(2 * n)},
        compiler_params=pltpu.CompilerParams(has_side_effects=EFFECT),
    )(*[_in_hbm(a) for a in srcs], *lands, after)
    return (outs[2:2 + n], outs[2 + n:2 + 2 * n], outs[0], outs[1]), outs[-1]


def _xfer_wait(name, started, copies, after):
    srcs, lands, send, recv = started
    n = len(srcs)

    def body(*refs):
        src_refs, land_refs, s_ref, r_ref = refs[:n], refs[n:2 * n], refs[2 * n], refs[2 * n + 1]
        for cp in copies(src_refs, land_refs, s_ref, r_ref):
            cp.wait_send()
            cp.wait_recv()

    outs = pl.pallas_call(
        body, name=name, in_specs=[HBM] * (2 * n) + [SEM, SEM, ANY], out_specs=[HBM] * (2 * n),
        out_shape=[pltpu.HBM(a.shape, a.dtype) for a in (*srcs, *lands)],
        input_output_aliases={i: i for i in range(2 * n)},
        compiler_params=pltpu.CompilerParams(has_side_effects=EFFECT),
    )(*srcs, *lands, send, recv, after)
    return outs[:n], outs[n:]


def _swap_copies(srcs, lands, send, recv):
    x, y, c, _ = _place()
    return [_remote(src.at[:, _half(src.shape[1], 1 - c)], land, send.at[i], recv.at[i], (x, y, 1 - c))
            for i, (src, land) in enumerate(zip(srcs, lands))]


def _scatter_copies(srcs, lands, send, recv):
    x, y, c, chips = _place()
    return [_remote(src.at[2 * cx + cy], land.at[k], send.at[3 * i + k], recv.at[3 * i + k], (cx, cy, c))
            for i, (src, land) in enumerate(zip(srcs, lands)) for k, (cx, cy) in enumerate(chips)]


def _join_copies(srcs, lands, send, recv):
    x, y, c, _ = _place()
    return [_remote(src, land, send.at[i], recv.at[i], (x, y, 1 - c)) for i, (src, land) in enumerate(zip(srcs, lands))]


class _Reducer:
    def __init__(self, place):
        self.place = place
        self.state = {}

    def swap(self, key, grads, after):
        shapes = [((g.shape[0], g.shape[1] // 2, g.shape[2]), g.dtype) for g in grads]
        self.state[key], token = _xfer_start("swap_start_" + key, grads, shapes, len(grads), _swap_copies, after)
        return token

    def to_chips(self, key, after):
        grads, from_sibling = _xfer_wait("swap_wait_" + key, self.state[key], _swap_copies, after)
        sums = [_add_halves(g, r, self.place) for g, r in zip(grads, from_sibling)]
        shapes = [((3,) + s.shape[1:], s.dtype) for s in sums]
        started, token = _xfer_start("scatter_start_" + key, sums, shapes, 3 * len(sums), _scatter_copies, sums[-1])
        self.state[key] = (grads, from_sibling, started)
        return token

    def to_core(self, key, after):
        grads, from_sibling, started = self.state[key]
        _, from_chips = _xfer_wait("scatter_wait_" + key, started, _scatter_copies, after)
        halves = [_sum_chips(g, r, rc, self.place) for g, r, rc in zip(grads, from_sibling, from_chips)]
        shapes = [(h.shape, h.dtype) for h in halves]
        self.state[key], token = _xfer_start("join_start_" + key, halves, shapes, len(halves), _join_copies, halves[-1])
        return token

    def finish(self, key, after):
        return _xfer_wait("join_wait_" + key, self.state.pop(key), _join_copies, after)


def _gather_packs(pack):
    def body(p_ref, o_ref, lsem, ssem, rsem):
        x, y, c, _ = _place()
        me = 4 * x + 2 * y + c
        local = pltpu.make_async_copy(p_ref, o_ref.at[me], lsem)
        local.start()
        cps = []
        for k in range(1, N_DEV):
            fx, fy, fc = (k >> 2) & 1, (k >> 1) & 1, k & 1
            to = (x ^ fx, y ^ fy, c ^ fc)
            cps.append(_remote(p_ref, o_ref.at[me], ssem.at[k - 1], rsem.at[k - 1], to))
        for cp in cps:
            cp.start()
        for k in range(1, N_DEV):
            fx, fy, fc = (k >> 2) & 1, (k >> 1) & 1, k & 1
            src = o_ref.at[4 * (x ^ fx) + 2 * (y ^ fy) + (c ^ fc)]
            _remote(src, src, ssem.at[k - 1], rsem.at[k - 1], (x, y, c)).wait_recv()
        for cp in cps:
            cp.wait_send()
        local.wait()

    return pl.pallas_call(
        body, name="gather_packs", in_specs=[ANY], out_specs=ANY,
        out_shape=jax.ShapeDtypeStruct((N_DEV,) + pack.shape, pack.dtype),
        scratch_shapes=[pltpu.SemaphoreType.DMA, pltpu.SemaphoreType.DMA((N_DEV - 1,)), pltpu.SemaphoreType.DMA((N_DEV - 1,))],
    )(pack)


LANE_TILES = (512, 896, 1408, 704, 384, 256, 128)


def _layer_grads(x, target, small, wg, rest_pass, rest_wait, red):
    s, d = x.shape
    f = wg["conv"].shape[1] // 2
    w_att = N_HEADS * HEAD_DIM
    in_splits = (w_att, w_att, w_att, N_HEADS, w_att, w_att, w_att, d, d)
    in_cols = sum(in_splits)
    cs = in_cols // N_CHIPS
    cp = wg["in"].shape[2]
    tm = min(s, 1024)
    t_in = _pick(cp, LANE_TILES)
    t_d = _pick(d, LANE_TILES)
    t_dq = _pick(d // N_CHIPS, LANE_TILES)
    t_w = _pick(w_att, LANE_TILES)
    t_up = _pick(2 * f // N_CHIPS, LANE_TILES)
    t_fq = _pick(f // N_CHIPS, LANE_TILES)
    offs = np.cumsum(in_splits)[:-1].tolist()

    h1 = _norm_fwd(x, small["g_attn"], group=d, name="rms1_fwd")
    proj_p = _mm(h1, wg["in"], mode="nn", b_kind="col", tm=tm, tn=t_in, tk=t_d, name="mm_in")
    qa, ka, va_b, fa, qb, kb, vb_b, ga, gb = _proj_split(
        proj_p, in_splits, cs, (F32, F32, BF16, F32, F32, F32, BF16, F32, F32))
    gains = {n: small[n].reshape(1, w_att) for n in ("g_q_fox", "g_k_fox", "g_q_dil", "g_k_dil")}
    qa_n = _norm_fwd(qa, gains["g_q_fox"], group=HEAD_DIM, name="qnorm_fox")
    ka_n = _norm_fwd(ka, gains["g_k_fox"], group=HEAD_DIM, name="knorm_fox")
    qb_n = _norm_fwd(qb, gains["g_q_dil"], group=HEAD_DIM, name="qnorm_dil")
    kb_n = _norm_fwd(kb, gains["g_k_dil"], group=HEAD_DIM, name="knorm_dil")
    fa_t = fa.T
    b_f = small["b_forget"].reshape(N_HEADS, 1)
    c_f = _forget_fwd(fa_t, b_f)
    slopes = jnp.asarray(2.0 ** (-8.0 * np.arange(1, N_HEADS + 1) / N_HEADS), dtype=F32)
    a_d = -(slopes[:, None] * jnp.arange(s, dtype=F32)[None, :])
    rows_f, cols_f = c_f[:, :, None], c_f[:, None, :]
    rows_d, cols_d = a_d[:, :, None], a_d[:, None, :]
    o_a, o_a32, lse_a = _attn_fwd(qa_n, ka_n, va_b, rows_f, cols_f, dilated=False, name="attn_fox_fwd")
    token = rest_pass(o_a)
    rows_d = rows_d + token[0, 0]
    o_b, o_b32, lse_b = _attn_fwd(qb_n, kb_n, vb_b, rows_d, cols_d, dilated=True, name="attn_dil_fwd")
    wg = dict(wg, **rest_wait(o_b))
    pa = _mm(o_a, wg["brf"], mode="nn", b_kind="col", tm=tm, tn=t_dq, tk=t_w, name="mm_brf")
    pb = _mm(o_b, wg["brd"], mode="nn", b_kind="col", tm=tm, tn=t_dq, tk=t_w, name="mm_brd")
    merged = _gate_fwd(ga, gb, pa, pb)
    x1 = _mm(merged, wg["out"], mode="nn", b_kind="row", res=x, tm=tm, tn=t_d, tk=t_dq, name="mm_out")
    h2 = _norm_fwd(x1, small["g_ffn"], group=d, name="rms2_fwd")
    u = _mm(h2, wg["up"], mode="nn", b_kind="col", tm=tm, tn=t_up, tk=t_d, name="mm_up")
    act = _conv_glu_fwd(u, wg["conv"], wg["bconv"])
    y = _mm(act, wg["down"], mode="nn", b_kind="row", res=x1, tm=tm, tn=t_d, tk=t_fq, name="mm_down")
    dy_f, dy_b, loss_blk = _loss_head(y, target)

    d_act = _mm(dy_b, wg["down"], mode="nt", b_kind="row", tm=tm, tn=t_fq, tk=t_d, name="mm_down_dx")
    g_down = _mm(act, dy_b, mode="tn", out_dtype=BF16, out_kind="row", tm=t_fq, tn=t_d, tk=tm, name="mm_down_dw")
    tok = red.swap("down", [g_down], g_down)
    du_g, du_v, st_g, st_v = _conv_glu_bwd(u, d_act, wg["conv"] + tok[0, 0], wg["bconv"])
    tok = red.to_chips("down", du_g)
    du = jnp.concatenate([du_g, du_v], axis=1)
    g_up = _mm(h2, du, mode="tn", out_dtype=BF16, out_kind="col", tm=t_d, tn=t_up, tk=tm, name="mm_up_dw", deps=(tok,))
    tok = red.to_core("down", g_up)
    tok2 = red.swap("up", [g_up], g_up)
    dh2 = _mm(du, wg["up"], mode="nt", b_kind="col", tm=tm, tn=t_d, tk=t_up, name="mm_up_dx", deps=(tok, tok2))
    tok = red.to_chips("up", dh2)
    dx1_b, dx1_f, dg_ffn = _norm_bwd(dh2, x1, small["g_ffn"], group=d, res=dy_f, out_dtypes=(BF16, F32), name="rms2_bwd")
    d_merged = _mm(dx1_b, wg["out"], mode="nt", b_kind="row", tm=tm, tn=t_dq, tk=t_d, name="mm_out_dx", deps=(tok,))
    g_out = _mm(merged, dx1_b, mode="tn", out_dtype=BF16, out_kind="row", tm=t_dq, tn=t_d, tk=tm, name="mm_out_dw")
    dpa, dpb, dga, dgb = _gate_bwd(d_merged, ga, gb, pa, pb)
    do_a = _mm(dpa, wg["brf"], mode="nt", b_kind="col", out_dtype=BF16, tm=tm, tn=t_w, tk=t_dq, name="mm_brf_dx")
    do_b = _mm(dpb, wg["brd"], mode="nt", b_kind="col", out_dtype=BF16, tm=tm, tn=t_w, tk=t_dq, name="mm_brd_dx")
    g_brf = _mm(o_a, dpa, mode="tn", out_dtype=BF16, out_kind="col", tm=t_w, tn=t_dq, tk=tm, name="mm_brf_dw")
    g_brd = _mm(o_b, dpb, mode="tn", out_dtype=BF16, out_kind="col", tm=t_w, tn=t_dq, tk=tm, name="mm_brd_dw")
    tok = red.swap("mix", [g_out, g_brf, g_brd], g_brd)
    dqa_n, dka_n, dva, dac_a = _attn_bwd(qa_n, ka_n, va_b, o_a32, do_a, lse_a, rows_f + tok[0, 0], cols_f, dilated=False, name="attn_fox_bwd")
    tok = red.to_core("up", dqa_n)
    tok2 = red.to_chips("mix", dqa_n)
    dqb_n, dkb_n, dvb, _ = _attn_bwd(qb_n, kb_n, vb_b, o_b32, do_b, lse_b, rows_d + (tok[0, 0] + tok2[0, 0]), cols_d, dilated=True, name="attn_dil_bwd")
    tok = red.to_core("mix", dqb_n)
    dqa, dg_qf = _norm_bwd(dqa_n, qa, gains["g_q_fox"], group=HEAD_DIM, name="qnorm_fox_bwd")
    dka, dg_kf = _norm_bwd(dka_n, ka, gains["g_k_fox"], group=HEAD_DIM, name="knorm_fox_bwd")
    dqb, dg_qd = _norm_bwd(dqb_n, qb, gains["g_q_dil"], group=HEAD_DIM, name="qnorm_dil_bwd")
    dkb, dg_kd = _norm_bwd(dkb_n, kb, gains["g_k_dil"], group=HEAD_DIM, name="knorm_dil_bwd")
    dfa_t, db_f = _forget_bwd(dac_a[:, 0, :], fa_t, b_f)
    dproj_p = _dproj_merge([dqa, dka, dva, dfa_t.T, dqb, dkb, dvb, dga, dgb], in_splits, cs, cp)
    g_in = _mm(h1, dproj_p, mode="tn", out_dtype=BF16, out_kind="col", tm=t_d, tn=t_in, tk=tm, name="mm_in_dw", deps=(tok,))
    tok = red.swap("in", [g_in], g_in)
    dh1 = _mm(dproj_p, wg["in"], mode="nt", b_kind="col", tm=tm, tn=t_d, tk=t_in, name="mm_in_dx", deps=(tok,))
    tok = red.to_chips("in", dh1)
    grad_x, dg_attn = _norm_bwd(dh1, x, small["g_attn"], group=d, res=dx1_f, out_dtypes=(F32,), name="rms1_bwd")

    small_grads = {
        "g_attn": dg_attn, "b_forget": db_f.reshape(1, N_HEADS),
        "g_q_fox": dg_qf, "g_k_fox": dg_kf, "g_q_dil": dg_qd, "g_k_dil": dg_kd, "g_ffn": dg_ffn,
        "w_conv": jnp.concatenate([st_g[0:3], st_v[0:3]], axis=1),
        "b_conv": jnp.concatenate([st_g[3:4], st_v[3:4]], axis=1),
        "loss": loss_blk[0:1, 0:1],
    }
    return small_grads, grad_x


SMALL_ORDER = ("g_attn", "b_forget", "g_q_fox", "g_k_fox", "g_q_dil", "g_k_dil", "g_ffn", "w_conv", "b_conv", "loss")
WEIGHT_ORDER = ("g_attn", "w_in", "b_forget", "g_q_fox", "g_k_fox", "g_q_dil", "g_k_dil", "w_br_fox", "w_br_dil",
                "w_out", "g_ffn", "w_up", "w_conv", "b_conv", "w_down")
BIG = {"w_in": "in", "w_br_fox": "brf", "w_br_dil": "brd", "w_out": "out", "w_up": "up", "w_down": "down"}


def kernel(x, g_attn, w_in, b_forget, g_q_fox, g_k_fox, g_q_dil, g_k_dil, w_br_fox, w_br_dil, w_out, g_ffn, w_up, w_conv, b_conv, w_down, loss_target, m_g_attn, m_w_in, m_b_forget, m_g_q_fox, m_g_k_fox, m_g_q_dil, m_g_k_dil, m_w_br_fox, m_w_br_dil, m_w_out, m_g_ffn, m_w_up, m_w_conv, m_b_conv, m_w_down, v_g_attn, v_w_in, v_b_forget, v_g_q_fox, v_g_k_fox, v_g_q_dil, v_g_k_dil, v_w_br_fox, v_w_br_dil, v_w_out, v_g_ffn, v_w_up, v_w_conv, v_b_conv, v_w_down):
    w = dict(g_attn=g_attn, w_in=w_in, b_forget=b_forget, g_q_fox=g_q_fox, g_k_fox=g_k_fox, g_q_dil=g_q_dil,
             g_k_dil=g_k_dil, w_br_fox=w_br_fox, w_br_dil=w_br_dil, w_out=w_out, g_ffn=g_ffn, w_up=w_up,
             w_conv=w_conv, b_conv=b_conv, w_down=w_down)
    m = dict(g_attn=m_g_attn, w_in=m_w_in, b_forget=m_b_forget, g_q_fox=m_g_q_fox, g_k_fox=m_g_k_fox,
             g_q_dil=m_g_q_dil, g_k_dil=m_g_k_dil, w_br_fox=m_w_br_fox, w_br_dil=m_w_br_dil, w_out=m_w_out,
             g_ffn=m_g_ffn, w_up=m_w_up, w_conv=m_w_conv, b_conv=m_b_conv, w_down=m_w_down)
    v = dict(g_attn=v_g_attn, w_in=v_w_in, b_forget=v_b_forget, g_q_fox=v_g_q_fox, g_k_fox=v_g_k_fox,
             g_q_dil=v_g_q_dil, g_k_dil=v_g_k_dil, w_br_fox=v_w_br_fox, w_br_dil=v_w_br_dil, w_out=v_w_out,
             g_ffn=v_g_ffn, w_up=v_w_up, w_conv=v_w_conv, b_conv=v_b_conv, w_down=v_w_down)
    xi, yi, ci = lax.axis_index("x"), lax.axis_index("y"), lax.axis_index("c")
    chip = (2 * xi + yi).astype(jnp.int32)
    c_idx = ci.astype(jnp.int32).reshape(1)
    j_idx = chip.reshape(1)

    cs = w_in.shape[2]
    cp = _round_up(cs, LANES)
    shards = {
        "in": jnp.pad(w_in[0].astype(BF16), ((0, 0), (0, cp - cs))),
        "brf": w_br_fox[0].astype(BF16), "brd": w_br_dil[0].astype(BF16), "out": w_out[0].astype(BF16),
        "up": w_up[0].astype(BF16), "down": w_down[0].astype(BF16),
    }
    names = tuple(shards)
    conv_pad = jnp.pad(w_conv[0], ((0, 8 - w_conv.shape[1]), (0, 0)))
    first = [(shards["in"], True), (conv_pad, False)]
    rest_names = names[1:]
    rest = [(shards[n], True) for n in rest_names]
    (started_first, started_rest), token = _gather_start([first, rest])
    own_first, passed_first, token = _gather_pass(first, started_first, token, "gather_pass_in")
    land_in, land_conv = _gather_wait(first, passed_first, token, "gather_wait_in")
    conv_all = _own_slab(land_conv, own_first[1])
    wg = {"in": _own_slab(land_in, own_first[0]), "bconv": b_conv,
          "conv": jnp.transpose(conv_all[:, :w_conv.shape[1], :], (1, 0, 2)).reshape(w_conv.shape[1], -1)}
    small = {n: w[n] for n in ("g_attn", "b_forget", "g_q_fox", "g_k_fox", "g_q_dil", "g_k_dil", "g_ffn")}
    small = {n: (a[0] if a.ndim == 3 else a) for n, a in small.items()}
    in_flight = {}

    def rest_pass(after):
        in_flight["own"], in_flight["passed"], tok = _gather_pass(rest, started_rest, after, "gather_pass_rest")
        return tok

    def rest_wait(after):
        lands = _gather_wait(rest, in_flight["passed"], after, "gather_wait_rest")
        return {n: _own_slab(land, own) for n, land, own in zip(rest_names, lands, in_flight["own"])}

    reducer = _Reducer(jnp.stack([chip, ci.astype(jnp.int32)]))
    small_grads, grad_x = _layer_grads(x[0], loss_target[0], small, wg, rest_pass, rest_wait, reducer)

    mine, theirs = {}, {}
    for key, members in (("down", ("down",)), ("up", ("up",)), ("mix", ("out", "brf", "brd"))):
        mine_k, theirs_k = reducer.finish(key, grad_x)
        mine.update(zip(members, mine_k))
        theirs.update(zip(members, theirs_k))

    flat = jnp.concatenate([small_grads[n].reshape(-1) for n in SMALL_ORDER])
    rows = _round_up(flat.shape[0], 8 * LANES) // LANES
    pack = jnp.pad(flat, (0, rows * LANES - flat.shape[0])).reshape(rows, LANES)
    total = _sum_devices(_gather_packs(pack)).reshape(-1)
    red, at = {}, 0
    for n in SMALL_ORDER:
        size = small_grads[n].size
        red[n] = total[at:at + size].reshape(small_grads[n].shape)
        at += size
    loss = red["loss"].reshape(())
    c2 = w_conv.shape[2]
    red["w_conv"] = lax.dynamic_slice_in_dim(red["w_conv"], chip * c2, c2, axis=1)

    g_out, d_out, m_out, v_out = {}, {}, {}, {}
    last = [n for n in WEIGHT_ORDER if n != "w_in"] + ["w_in"]
    for n in last:
        shape = w[n].shape
        r2 = (shape[-2], shape[-1]) if n not in ("g_attn", "b_forget", "g_ffn", "b_conv") else (1, shape[-1])
        if n == "w_in":
            done = jnp.stack([v_out[k][(0,) * v_out[k].ndim] for k in last[:-1]])
            tok = reducer.to_core("in", done)
            (mine_in,), (theirs_in,) = reducer.finish("in", tok)
            mine["in"], theirs["in"] = mine_in[:, :cs], theirs_in[:, :cs]
        if n in BIG:
            g2, dl, mn, vn = _adamw_halves(w[n].reshape(r2), mine[BIG[n]], theirs[BIG[n]], c_idx,
                                           m[n].reshape(r2), v[n].reshape(r2), name="adamw_" + n)
        else:
            g2 = red[n].reshape(r2)
            dl, mn, vn = _adamw(w[n].reshape(r2), g2, m[n].reshape(r2), v[n].reshape(r2), name="adamw_" + n)
        g_out[n], d_out[n], m_out[n], v_out[n] = (a.reshape(shape) for a in (g2, dl, mn, vn))

    return (loss, grad_x[None], *[g_out[n] for n in WEIGHT_ORDER], *[d_out[n] for n in WEIGHT_ORDER],
            *[m_out[n] for n in WEIGHT_ORDER], *[v_out[n] for n in WEIGHT_ORDER])
```

```python
import functools
import math

import jax
import jax.numpy as jnp
import numpy as np
from jax import lax
from jax.experimental import pallas as pl
from jax.experimental.pallas import tpu as pltpu

F32 = jnp.float32
BF16 = jnp.bfloat16
HEAD_DIM = 128
N_HEADS = 8
EPS = 1e-6
NEG = -1e30
N_CHIPS = 4
N_DEV = 8
LANES = 128
VMEM_LIMIT_BYTES = 56 * 1024 * 1024
DIL_PATTERNS = ((128, 1), (512, 4), (2048, 16))
ATTN_TILE = 512
ADAM_LR, ADAM_B1, ADAM_B2, ADAM_EPS, ADAM_WD, ADAM_STEP = 0.001, 0.9, 0.999, 1e-08, 0.01, 10
MESH = pl.DeviceIdType.MESH


def _params(*sem):
    return pltpu.CompilerParams(dimension_semantics=sem, vmem_limit_bytes=VMEM_LIMIT_BYTES)


def _round_up(n, m):
    return -(-n // m) * m


def _pick(dim, prefs):
    for p in prefs:
        if dim % p == 0:
            return p
    raise ValueError(f"no tile for {dim} in {prefs}")


def _logical_shape(arr, kind):
    if kind is None:
        return arr.shape
    s, r, c = arr.shape
    return (r, s * c) if kind == "col" else (s * r, c)


def _spec(shape, kind, br, bc, fi, fj):
    if kind is None:
        return pl.BlockSpec((br, bc), lambda *g: (fi(*g), fj(*g)))
    _, r, c = shape
    if kind == "col":
        nb = c // bc
        assert nb * bc == c, (shape, bc)
        return pl.BlockSpec((None, br, bc), lambda *g: (fj(*g) // nb, fi(*g), fj(*g) % nb))
    nb = r // br
    assert nb * br == r, (shape, br)
    return pl.BlockSpec((None, br, bc), lambda *g: (fi(*g) // nb, fi(*g) % nb, fj(*g)))


def _mm(a, b, *, mode, tm, tn, tk, name, a_kind=None, b_kind=None, out_kind=None,
        out_dtype=F32, res=None, deps=()):
    la, lb = _logical_shape(a, a_kind), _logical_shape(b, b_kind)
    if mode == "nn":
        (m, k), (k2, n) = la, lb
    elif mode == "nt":
        (m, k), (n, k2) = la, lb
    else:
        (k, m), (k2, n) = la, lb
    assert k == k2, (name, la, lb)
    assert m % tm == 0 and n % tn == 0 and k % tk == 0, (name, m, n, k, tm, tn, tk)
    nk = k // tk
    im = lambda i, j, l: i
    jn = lambda i, j, l: j
    lk = lambda i, j, l: l
    if mode == "tn":
        a_spec = _spec(a.shape, a_kind, tk, tm, lk, im)
        dims = (((0,), (0,)), ((), ()))
    else:
        a_spec = _spec(a.shape, a_kind, tm, tk, im, lk)
        dims = (((1,), (1,)), ((), ())) if mode == "nt" else (((1,), (0,)), ((), ()))
    if mode == "nt":
        b_spec = _spec(b.shape, b_kind, tn, tk, jn, lk)
    else:
        b_spec = _spec(b.shape, b_kind, tk, tn, lk, jn)
    if out_kind is None:
        oshape = (m, n)
    elif out_kind == "col":
        oshape = (N_CHIPS, m, n // N_CHIPS)
    else:
        oshape = (N_CHIPS, m // N_CHIPS, n)
    o_spec = _spec(oshape, out_kind, tm, tn, im, jn)
    in_specs = [a_spec, b_spec]
    args = [a, b]
    if res is not None:
        in_specs.append(pl.BlockSpec((tm, tn), lambda i, j, l: (i, j)))
        args.append(res)
    in_specs += [pl.BlockSpec(memory_space=pl.ANY)] * len(deps)
    args += list(deps)

    def body(*refs):
        a_ref, b_ref = refs[0], refs[1]
        res_ref = refs[2] if res is not None else None
        o_ref, acc_ref = refs[-2], refs[-1]
        step = pl.program_id(2)

        @pl.when(step == 0)
        def _():
            acc_ref[...] = jnp.zeros_like(acc_ref)

        acc_ref[...] += lax.dot_general(a_ref[...], b_ref[...], dims, preferred_element_type=F32)

        @pl.when(step == nk - 1)
        def _():
            out = acc_ref[...]
            if res_ref is not None:
                out = out + res_ref[...]
            o_ref[...] = out.astype(o_ref.dtype)

    return pl.pallas_call(
        body, name=name, grid=(m // tm, n // tn, nk),
        in_specs=in_specs, out_specs=o_spec,
        out_shape=jax.ShapeDtypeStruct(oshape, out_dtype),
        scratch_shapes=[pltpu.VMEM((tm, tn), F32)],
        compiler_params=_params("parallel", "parallel", "arbitrary"),
    )(*args)


def _pieces(splits, cs, cp):
    out, g0 = [], 0
    for width in splits:
        g1, runs = g0 + width, []
        for j in range(N_CHIPS):
            a, b = max(g0, cs * j), min(g1, cs * (j + 1))
            if a < b:
                runs.append((j * cp + a - cs * j, a - g0, b - a))
        out.append(runs)
        g0 = g1
    return out


def _proj_split(proj_p, splits, cs, dtypes, tm=128):
    s, wp = proj_p.shape
    pieces = _pieces(splits, cs, wp // N_CHIPS)

    def body(p_ref, *o_refs):
        for o_ref, runs in zip(o_refs, pieces):
            for src, dst, n in runs:
                o_ref[:, dst:dst + n] = p_ref[:, src:src + n].astype(o_ref.dtype)

    return pl.pallas_call(
        body, name="proj_split", grid=(s // tm,), in_specs=[pl.BlockSpec((tm, wp), lambda i: (i, 0))],
        out_specs=[pl.BlockSpec((tm, w), lambda i: (i, 0)) for w in splits],
        out_shape=[jax.ShapeDtypeStruct((s, w), dt) for w, dt in zip(splits, dtypes)],
        compiler_params=_params("parallel"),
    )(proj_p)


def _dproj_merge(parts, splits, cs, cp, tm=128):
    s = parts[0].shape[0]
    wp = N_CHIPS * cp
    pieces = _pieces(splits, cs, cp)

    def body(*refs):
        o_ref, stage = refs[-2], refs[-1]
        for j in range(N_CHIPS):
            stage[:, j * cp + cs:(j + 1) * cp] = jnp.zeros((tm, cp - cs), F32)
        for p_ref, runs in zip(refs, pieces):
            for dst, src, n in runs:
                stage[:, dst:dst + n] = p_ref[:, src:src + n].astype(F32)
        o_ref[...] = stage[...].astype(o_ref.dtype)

    return pl.pallas_call(
        body, name="dproj_merge", grid=(s // tm,),
        in_specs=[pl.BlockSpec((tm, w), lambda i: (i, 0)) for w in splits],
        out_specs=pl.BlockSpec((tm, wp), lambda i: (i, 0)),
        out_shape=jax.ShapeDtypeStruct((s, wp), BF16), scratch_shapes=[pltpu.VMEM((tm, wp), F32)],
        compiler_params=_params("parallel"),
    )(*parts)


def _norm_fwd(x, g, *, group, name, tm=256):
    s, w = x.shape
    ng = w // group

    def body(x_ref, g_ref, o_ref):
        for i in range(ng):
            cols = slice(i * group, (i + 1) * group)
            xv = x_ref[:, cols]
            r = lax.rsqrt(jnp.mean(xv * xv, axis=-1, keepdims=True) + EPS)
            o_ref[:, cols] = ((xv * r) * g_ref[:, cols]).astype(o_ref.dtype)

    return pl.pallas_call(
        body, name=name, grid=(s // tm,),
        in_specs=[pl.BlockSpec((tm, w), lambda i: (i, 0)), pl.BlockSpec((1, w), lambda i: (0, 0))],
        out_specs=pl.BlockSpec((tm, w), lambda i: (i, 0)),
        out_shape=jax.ShapeDtypeStruct((s, w), BF16),
        compiler_params=_params("parallel"),
    )(x, g)


def _norm_bwd(dy, x, g, *, group, name, res=None, out_dtypes=(BF16,), tm=256):
    s, w = x.shape
    ng = w // group
    n_in = 4 if res is not None else 3

    def body(*refs):
        dy_ref, x_ref, g_ref = refs[:3]
        res_ref = refs[3] if res is not None else None
        outs = refs[n_in:]
        dx_refs, dg_ref = outs[:-1], outs[-1]

        @pl.when(pl.program_id(0) == 0)
        def _():
            dg_ref[...] = jnp.zeros_like(dg_ref)

        for i in range(ng):
            cols = slice(i * group, (i + 1) * group)
            xv = x_ref[:, cols]
            dyv = dy_ref[:, cols].astype(F32)
            r = lax.rsqrt(jnp.mean(xv * xv, axis=-1, keepdims=True) + EPS)
            xr = xv * r
            dg_ref[:, cols] += jnp.sum(dyv * xr, axis=0, keepdims=True)
            gdy = dyv * g_ref[:, cols]
            dx = r * (gdy - xr * jnp.mean(gdy * xr, axis=-1, keepdims=True))
            if res_ref is not None:
                dx = dx + res_ref[:, cols]
            for dx_ref in dx_refs:
                dx_ref[:, cols] = dx.astype(dx_ref.dtype)

    row = pl.BlockSpec((tm, w), lambda i: (i, 0))
    vec = pl.BlockSpec((1, w), lambda i: (0, 0))
    in_specs = [row, row, vec] + ([row] if res is not None else [])
    args = [dy, x, g] + ([res] if res is not None else [])
    out_specs = [row] * len(out_dtypes) + [vec]
    out_shape = [jax.ShapeDtypeStruct((s, w), dt) for dt in out_dtypes] + [jax.ShapeDtypeStruct((1, w), F32)]
    return pl.pallas_call(
        body, name=name, grid=(s // tm,), in_specs=in_specs, out_specs=out_specs,
        out_shape=out_shape, compiler_params=_params("arbitrary"),
    )(*args)


def _split3(v):
    p1 = v.astype(BF16)
    r1 = v - p1.astype(F32)
    p2 = r1.astype(BF16)
    p3 = (r1 - p2.astype(F32)).astype(BF16)
    return p1, p2, p3


def _tri_sum(v, reverse, tcol=512):
    h, s = v.shape
    tcol = min(tcol, s)
    parts = _split3(v)
    outs = []
    for j in range(s // tcol):
        src = lax.broadcasted_iota(jnp.int32, (s, tcol), 0)
        dst = lax.broadcasted_iota(jnp.int32, (s, tcol), 1) + j * tcol
        keep = (src >= dst) if reverse else (src <= dst)
        tri = jnp.where(keep, 1.0, 0.0).astype(BF16)
        acc = jnp.zeros((h, tcol), F32)
        for p in parts:
            acc = acc + jnp.dot(p, tri, preferred_element_type=F32)
        outs.append(acc)
    return outs


def _forget_fwd(fa_t, b):
    h, s = fa_t.shape
    tcol = min(512, s)

    def body(f_ref, b_ref, c_ref):
        z = f_ref[...] + b_ref[...]
        logf = jnp.minimum(z, 0.0) - jnp.log(1.0 + jnp.exp(-jnp.abs(z)))
        for j, blk in enumerate(_tri_sum(logf, reverse=False, tcol=tcol)):
            c_ref[:, j * tcol:(j + 1) * tcol] = blk

    return pl.pallas_call(
        body, name="forget_fwd", out_shape=jax.ShapeDtypeStruct((h, s), F32),
        compiler_params=_params(),
    )(fa_t, b)


def _forget_bwd(dacol, fa_t, b):
    h, s = fa_t.shape
    tcol = min(512, s)

    def body(d_ref, f_ref, b_ref, dfa_ref, db_ref):
        z = f_ref[...] + b_ref[...]
        dc = -d_ref[...]
        total = jnp.zeros((h, 1), F32)
        for j, blk in enumerate(_tri_sum(dc, reverse=True, tcol=tcol)):
            cols = slice(j * tcol, (j + 1) * tcol)
            dfa = blk * (1.0 - jax.nn.sigmoid(z[:, cols]))
            dfa_ref[:, cols] = dfa
            total = total + jnp.sum(dfa, axis=-1, keepdims=True)
        db_ref[...] = total

    return pl.pallas_call(
        body, name="forget_bwd",
        out_shape=[jax.ShapeDtypeStruct((h, s), F32), jax.ShapeDtypeStruct((h, 1), F32)],
        compiler_params=_params(),
    )(dacol, fa_t, b)


def _logits(q, k, arow, acol, q0, k0, dilated):
    tq, tk = q.shape[0], k.shape[0]
    s = lax.dot_general(q, k, (((1,), (1,)), ((), ())), preferred_element_type=F32)
    s = s * (1.0 / math.sqrt(HEAD_DIM)) + arow - acol
    dist = (q0 + lax.broadcasted_iota(jnp.int32, (tq, tk), 0)) - (k0 + lax.broadcasted_iota(jnp.int32, (tq, tk), 1))
    valid = dist >= 0
    if dilated:
        mult = jnp.zeros((tq, tk), jnp.int32)
        for window, dil in DIL_PATTERNS:
            mult = mult + ((dist <= window) & ((dist & (dil - 1)) == 0)).astype(jnp.int32)
        s = s + jnp.where(mult == 3, math.log(3.0), jnp.where(mult == 2, math.log(2.0), 0.0))
        valid = valid & (mult > 0)
    return jnp.where(valid, s, NEG)


def _attn_fwd(q, k, v, arow, acol, *, dilated, name, tq=ATTN_TILE, tk=ATTN_TILE):
    s, w = q.shape
    nh = w // HEAD_DIM
    assert tq == tk
    tq = tk = min(tq, s)
    nq, nk = s // tq, s // tk

    def body(q_ref, k_ref, v_ref, ar_ref, ac_ref, o_ref, of_ref, lse_ref, m_ref, l_ref, acc_ref):
        qi, ki = pl.program_id(1), pl.program_id(2)

        @pl.when(ki == 0)
        def _():
            m_ref[...] = jnp.full_like(m_ref, NEG)
            l_ref[...] = jnp.zeros_like(l_ref)
            acc_ref[...] = jnp.zeros_like(acc_ref)

        @pl.when(ki <= qi)
        def _():
            sc = _logits(q_ref[...], k_ref[...], ar_ref[...], ac_ref[...], qi * tq, ki * tk, dilated)
            m_new = jnp.maximum(m_ref[...], jnp.max(sc, axis=-1, keepdims=True))
            alpha = jnp.exp(m_ref[...] - m_new)
            p = jnp.exp(sc - m_new)
            l_ref[...] = alpha * l_ref[...] + jnp.sum(p, axis=-1, keepdims=True)
            p_hi = p.astype(BF16)
            p_lo = (p - p_hi.astype(F32)).astype(BF16)
            vv = v_ref[...]
            acc_ref[...] = (alpha * acc_ref[...] + jnp.dot(p_hi, vv, preferred_element_type=F32)
                            + jnp.dot(p_lo, vv, preferred_element_type=F32))
            m_ref[...] = m_new

        @pl.when(ki == nk - 1)
        def _():
            out = acc_ref[...] / l_ref[...]
            o_ref[...] = out.astype(o_ref.dtype)
            of_ref[...] = out
            lse_ref[...] = m_ref[...] + jnp.log(l_ref[...])

    kv = pl.BlockSpec((tk, HEAD_DIM), lambda h, i, j: (jnp.minimum(j, i), h))
    return pl.pallas_call(
        body, name=name, grid=(nh, nq, nk),
        in_specs=[pl.BlockSpec((tq, HEAD_DIM), lambda h, i, j: (i, h)), kv, kv,
                  pl.BlockSpec((None, tq, 1), lambda h, i, j: (h, i, 0)),
                  pl.BlockSpec((None, 1, tk), lambda h, i, j: (h, 0, jnp.minimum(j, i)))],
        out_specs=[pl.BlockSpec((tq, HEAD_DIM), lambda h, i, j: (i, h)),
                   pl.BlockSpec((tq, HEAD_DIM), lambda h, i, j: (i, h)),
                   pl.BlockSpec((None, tq, 1), lambda h, i, j: (h, i, 0))],
        out_shape=[jax.ShapeDtypeStruct((s, w), BF16), jax.ShapeDtypeStruct((s, w), F32),
                   jax.ShapeDtypeStruct((nh, s, 1), F32)],
        scratch_shapes=[pltpu.VMEM((tq, 1), F32), pltpu.VMEM((tq, 1), F32), pltpu.VMEM((tq, HEAD_DIM), F32)],
        compiler_params=_params("parallel", "parallel", "arbitrary"),
    )(q, k, v, arow, acol)


def _attn_bwd(q, k, v, o, do, lse, arow, acol, *, dilated, name, tq=ATTN_TILE, tk=ATTN_TILE):
    s, w = q.shape
    nh = w // HEAD_DIM
    assert tq == tk
    tq = tk = min(tq, s)
    nq, nk = s // tq, s // tk
    scale = 1.0 / math.sqrt(HEAD_DIM)

    def body(q_ref, k_ref, v_ref, o_ref, do_ref, lse_ref, ar_ref, ac_ref,
             dq_ref, dk_ref, dv_ref, dac_ref, dk_acc, dv_acc, dac_acc):
        ki, qi = pl.program_id(1), pl.program_id(2)

        @pl.when((ki == 0) & (qi == 0))
        def _():
            dq_ref[...] = jnp.zeros_like(dq_ref)

        @pl.when(qi == 0)
        def _():
            dk_acc[...] = jnp.zeros_like(dk_acc)
            dv_acc[...] = jnp.zeros_like(dv_acc)
            dac_acc[...] = jnp.zeros_like(dac_acc)

        @pl.when(qi >= ki)
        def _():
            qv, kvv, dov = q_ref[...], k_ref[...], do_ref[...]
            sc = _logits(qv, kvv, ar_ref[...], ac_ref[...], qi * tq, ki * tk, dilated)
            p = jnp.exp(sc - lse_ref[...])
            dp = lax.dot_general(dov, v_ref[...], (((1,), (1,)), ((), ())), preferred_element_type=F32)
            delta = jnp.sum(dov.astype(F32) * o_ref[...].astype(F32), axis=-1, keepdims=True)
            ds = p * (dp - delta)
            dsb = ds.astype(BF16)
            dv_acc[...] += lax.dot_general(p.astype(BF16), dov, (((0,), (0,)), ((), ())), preferred_element_type=F32)
            dk_acc[...] += lax.dot_general(dsb, qv, (((0,), (0,)), ((), ())), preferred_element_type=F32)
            rows = pl.ds(pl.multiple_of(qi * tq, tq), tq)
            dq_ref[rows, :] += jnp.dot(dsb, kvv, preferred_element_type=F32) * scale
            dac_acc[...] += jnp.sum(ds, axis=0, keepdims=True)

        @pl.when(qi == nq - 1)
        def _():
            dk_ref[...] = dk_acc[...] * scale
            dv_ref[...] = dv_acc[...]
            dac_ref[...] = dac_acc[...]

    qs = pl.BlockSpec((tq, HEAD_DIM), lambda h, j, i: (jnp.maximum(i, j), h))
    ks = pl.BlockSpec((tk, HEAD_DIM), lambda h, j, i: (j, h))
    rowv = pl.BlockSpec((None, tq, 1), lambda h, j, i: (h, jnp.maximum(i, j), 0))
    colv = pl.BlockSpec((None, 1, tk), lambda h, j, i: (h, 0, j))
    return pl.pallas_call(
        body, name=name, grid=(nh, nk, nq),
        in_specs=[qs, ks, ks, qs, qs, rowv, rowv, colv],
        out_specs=[pl.BlockSpec((s, HEAD_DIM), lambda h, j, i: (0, h)), ks, ks, colv],
        out_shape=[jax.ShapeDtypeStruct((s, w), F32), jax.ShapeDtypeStruct((s, w), F32),
                   jax.ShapeDtypeStruct((s, w), F32), jax.ShapeDtypeStruct((nh, 1, s), F32)],
        scratch_shapes=[pltpu.VMEM((tk, HEAD_DIM), F32), pltpu.VMEM((tk, HEAD_DIM), F32), pltpu.VMEM((1, tk), F32)],
        compiler_params=_params("arbitrary", "arbitrary", "arbitrary"),
    )(q, k, v, o, do, lse, arow, acol)


def _gate_fwd(ga, gb, pa, pb, tm=256):
    s, d = ga.shape

    def body(ga_ref, gb_ref, pa_ref, pb_ref, o_ref):
        o_ref[...] = (jax.nn.sigmoid(ga_ref[...]) * pa_ref[...]
                      + jax.nn.sigmoid(gb_ref[...]) * pb_ref[...]).astype(o_ref.dtype)

    row = pl.BlockSpec((tm, d), lambda i: (i, 0))
    return pl.pallas_call(
        body, name="gate_fwd", grid=(s // tm,), in_specs=[row] * 4, out_specs=row,
        out_shape=jax.ShapeDtypeStruct((s, d), BF16), compiler_params=_params("parallel"),
    )(ga, gb, pa, pb)


def _gate_bwd(dm, ga, gb, pa, pb, tm=256):
    s, d = ga.shape

    def body(dm_ref, ga_ref, gb_ref, pa_ref, pb_ref, dpa_ref, dpb_ref, dga_ref, dgb_ref):
        dmv = dm_ref[...]
        for g_ref, p_ref, dp_ref, dg_ref in ((ga_ref, pa_ref, dpa_ref, dga_ref), (gb_ref, pb_ref, dpb_ref, dgb_ref)):
            sg = jax.nn.sigmoid(g_ref[...])
            dp_ref[...] = (dmv * sg).astype(BF16)
            dg_ref[...] = (dmv * p_ref[...] * (sg * (1.0 - sg))).astype(BF16)

    row = pl.BlockSpec((tm, d), lambda i: (i, 0))
    return pl.pallas_call(
        body, name="gate_bwd", grid=(s // tm,), in_specs=[row] * 5, out_specs=[row] * 4,
        out_shape=[jax.ShapeDtypeStruct((s, d), BF16)] * 4, compiler_params=_params("parallel"),
    )(dm, ga, gb, pa, pb)


def _shift_down(u, k):
    row = lax.broadcasted_iota(jnp.int32, u.shape, 0)
    return jnp.where(row >= k, pltpu.roll(u, k, 0), 0.0)


def _shift_up(u, k):
    n = u.shape[0]
    row = lax.broadcasted_iota(jnp.int32, u.shape, 0)
    return jnp.where(row < n - k, pltpu.roll(u, n - k, 0), 0.0)


def _conv3(u, wc, b):
    return wc[0:1, :] * _shift_down(u, 2) + wc[1:2, :] * _shift_down(u, 1) + wc[2:3, :] * u + b


def _conv_glu_fwd(u, wc, b, tn=256):
    s, f2 = u.shape
    f = f2 // 2
    nb = f // tn

    def body(ug_ref, uv_ref, wg_ref, wv_ref, bg_ref, bv_ref, o_ref):
        cg = _conv3(ug_ref[...], wg_ref[...], bg_ref[...])
        cv = _conv3(uv_ref[...], wv_ref[...], bv_ref[...])
        o_ref[...] = (cg * jax.nn.sigmoid(cg) * cv).astype(o_ref.dtype)

    def cols(rows, off):
        return pl.BlockSpec((rows, tn), lambda j: (0, j + off))

    return pl.pallas_call(
        body, name="conv_glu_fwd", grid=(nb,),
        in_specs=[cols(s, 0), cols(s, nb), cols(3, 0), cols(3, nb), cols(1, 0), cols(1, nb)],
        out_specs=cols(s, 0), out_shape=jax.ShapeDtypeStruct((s, f), BF16),
        compiler_params=_params("parallel"),
    )(u, u, wc, wc, b, b)


def _conv_glu_bwd(u, da, wc, b, tn=256):
    s, f2 = u.shape
    f = f2 // 2
    nb = f // tn

    def body(ug_ref, uv_ref, da_ref, wg_ref, wv_ref, bg_ref, bv_ref, dug_ref, duv_ref, sg_ref, sv_ref):
        ug, uv, wg, wv = ug_ref[...], uv_ref[...], wg_ref[...], wv_ref[...]
        cg = _conv3(ug, wg, bg_ref[...])
        cv = _conv3(uv, wv, bv_ref[...])
        sig = jax.nn.sigmoid(cg)
        dav = da_ref[...]
        dcv = dav * (cg * sig)
        dcg = dav * cv * (sig * (1.0 + cg * (1.0 - sig)))
        for dc, uu, w, du_ref, st_ref in ((dcg, ug, wg, dug_ref, sg_ref), (dcv, uv, wv, duv_ref, sv_ref)):
            du = w[2:3, :] * dc + w[1:2, :] * _shift_up(dc, 1) + w[0:1, :] * _shift_up(dc, 2)
            du_ref[...] = du.astype(BF16)
            st_ref[...] = jnp.zeros_like(st_ref)
            st_ref[0:1, :] = jnp.sum(dc * _shift_down(uu, 2), axis=0, keepdims=True)
            st_ref[1:2, :] = jnp.sum(dc * _shift_down(uu, 1), axis=0, keepdims=True)
            st_ref[2:3, :] = jnp.sum(dc * uu, axis=0, keepdims=True)
            st_ref[3:4, :] = jnp.sum(dc, axis=0, keepdims=True)

    def cols(rows, off):
        return pl.BlockSpec((rows, tn), lambda j: (0, j + off))

    return pl.pallas_call(
        body, name="conv_glu_bwd", grid=(nb,),
        in_specs=[cols(s, 0), cols(s, nb), cols(s, 0), cols(3, 0), cols(3, nb), cols(1, 0), cols(1, nb)],
        out_specs=[cols(s, 0), cols(s, 0), cols(8, 0), cols(8, 0)],
        out_shape=[jax.ShapeDtypeStruct((s, f), BF16), jax.ShapeDtypeStruct((s, f), BF16),
                   jax.ShapeDtypeStruct((8, f), F32), jax.ShapeDtypeStruct((8, f), F32)],
        compiler_params=_params("parallel"),
    )(u, u, da, wc, wc, b, b)


def _loss_head(y, target, tm=256):
    s, d = y.shape

    def body(y_ref, t_ref, dyf_ref, dyb_ref, l_ref):
        @pl.when(pl.program_id(0) == 0)
        def _():
            l_ref[...] = jnp.zeros_like(l_ref)

        err = y_ref[...] - t_ref[...]
        dy = err * (1.0 / d)
        dyf_ref[...] = dy
        dyb_ref[...] = dy.astype(BF16)
        l_ref[...] += 0.5 * jnp.sum(jnp.sum(err * err, axis=-1, keepdims=True) * (1.0 / d), axis=0, keepdims=True)

    row = pl.BlockSpec((tm, d), lambda i: (i, 0))
    return pl.pallas_call(
        body, name="loss_head", grid=(s // tm,), in_specs=[row, row],
        out_specs=[row, row, pl.BlockSpec((8, LANES), lambda i: (0, 0))],
        out_shape=[jax.ShapeDtypeStruct((s, d), F32), jax.ShapeDtypeStruct((s, d), BF16),
                   jax.ShapeDtypeStruct((8, LANES), F32)],
        compiler_params=_params("arbitrary"),
    )(y, target)


ROW_TILES = (256, 128, 64, 32, 16, 8)
BLOCK_BYTES = 1 << 20


def _add_halves(g, r1, place):
    ns, r, c = g.shape
    rh = r // 2
    tr = _pick(rh, ROW_TILES)
    g4 = g.reshape(ns, 2, rh, c)

    def body(p_ref, g_ref, r_ref, o_ref):
        o_ref[...] = (g_ref[...].astype(F32) + r_ref[...].astype(F32)).astype(o_ref.dtype)

    return pl.pallas_call(
        body, name="add_halves",
        grid_spec=pltpu.PrefetchScalarGridSpec(
            num_scalar_prefetch=1, grid=(ns, rh // tr),
            in_specs=[pl.BlockSpec((None, None, tr, c), lambda s, i, pr: (s, pr[1], i, 0)),
                      pl.BlockSpec((None, tr, c), lambda s, i, pr: (s, i, 0))],
            out_specs=pl.BlockSpec((None, tr, c), lambda s, i, pr: (s, i, 0))),
        out_shape=jax.ShapeDtypeStruct((ns, rh, c), BF16),
        compiler_params=_params("parallel", "parallel"),
    )(place, g4, r1)


def _sum_chips(g, r1, recv, place):
    ns, r, c = g.shape
    rh = r // 2
    tr = _pick(rh, ROW_TILES)
    g4 = g.reshape(ns, 2, rh, c)

    def body(p_ref, g_ref, r_ref, t0_ref, t1_ref, t2_ref, o_ref):
        own = g_ref[...].astype(F32) + r_ref[...].astype(F32)
        o_ref[...] = ((own + t0_ref[...].astype(F32)) + t1_ref[...].astype(F32)) + t2_ref[...].astype(F32)

    def peer(k):
        return pl.BlockSpec((None, tr, c), lambda i, pr: (k, i, 0))

    return pl.pallas_call(
        body, name="sum_chips",
        grid_spec=pltpu.PrefetchScalarGridSpec(
            num_scalar_prefetch=1, grid=(rh // tr,),
            in_specs=[pl.BlockSpec((None, None, tr, c), lambda i, pr: (pr[0], pr[1], i, 0)),
                      pl.BlockSpec((None, tr, c), lambda i, pr: (pr[0], i, 0)), peer(0), peer(1), peer(2)],
            out_specs=pl.BlockSpec((tr, c), lambda i, pr: (i, 0))),
        out_shape=jax.ShapeDtypeStruct((rh, c), F32),
        compiler_params=_params("parallel"),
    )(place, g4, r1, recv, recv, recv)


def _sum_devices(packs):
    n, r, c = packs.shape

    def body(p_ref, o_ref):
        acc = p_ref[0]
        for d in range(1, n):
            acc = acc + p_ref[d]
        o_ref[...] = acc

    return pl.pallas_call(
        body, name="sum_devices", out_shape=jax.ShapeDtypeStruct((r, c), F32), compiler_params=_params(),
    )(packs)


def _adamw_update(wv, gv, mv, vv):
    c1 = 1.0 - ADAM_B1 ** ADAM_STEP
    c2 = 1.0 - ADAM_B2 ** ADAM_STEP
    mn = ADAM_B1 * mv + (1.0 - ADAM_B1) * gv
    vn = ADAM_B2 * vv + (1.0 - ADAM_B2) * (gv * gv)
    m_hat = mn / c1
    v_hat = vn / c2
    return -ADAM_LR * (m_hat / (jnp.sqrt(v_hat) + ADAM_EPS) + ADAM_WD * wv), mn, vn


def _adamw(w, g, m, v, name, deps=()):
    r, c = w.shape
    tr = _pick(r, ROW_TILES) if r >= 8 else r

    def body(w_ref, g_ref, m_ref, v_ref, *rest):
        d_ref, mo_ref, vo_ref = rest[-3:]
        d_ref[...], mo_ref[...], vo_ref[...] = _adamw_update(w_ref[...], g_ref[...], m_ref[...], v_ref[...])

    blk = pl.BlockSpec((tr, c), lambda i: (i, 0))
    return pl.pallas_call(
        body, name=name, grid=(r // tr,), in_specs=[blk] * 4 + [ANY] * len(deps), out_specs=[blk] * 3,
        out_shape=[jax.ShapeDtypeStruct((r, c), F32)] * 3, compiler_params=_params("parallel"),
    )(w, g, m, v, *deps)


def _adamw_halves(w, mine, theirs, c_idx, m, v, name, deps=()):
    r, c = w.shape
    rh = r // 2
    tr = _pick(rh, [t for t in ROW_TILES if t * c * 4 <= BLOCK_BYTES])
    nb = rh // tr

    def body(c_ref, w_ref, a_ref, b_ref, m_ref, v_ref, *rest):
        g_ref, d_ref, mo_ref, vo_ref = rest[-4:]
        gv = jnp.where(pl.program_id(0) // nb == c_ref[0], a_ref[...], b_ref[...])
        g_ref[...] = gv
        d_ref[...], mo_ref[...], vo_ref[...] = _adamw_update(w_ref[...], gv, m_ref[...], v_ref[...])

    blk = pl.BlockSpec((tr, c), lambda i, cr: (i, 0))
    mine_spec = pl.BlockSpec((tr, c), lambda i, cr: (jnp.clip(i - cr[0] * nb, 0, nb - 1), 0))
    theirs_spec = pl.BlockSpec((tr, c), lambda i, cr: (jnp.clip(i - (1 - cr[0]) * nb, 0, nb - 1), 0))
    return pl.pallas_call(
        body, name=name,
        grid_spec=pltpu.PrefetchScalarGridSpec(
            num_scalar_prefetch=1, grid=(r // tr,),
            in_specs=[blk, mine_spec, theirs_spec, blk, blk] + [ANY] * len(deps), out_specs=[blk] * 4),
        out_shape=[jax.ShapeDtypeStruct((r, c), F32)] * 4, compiler_params=_params("arbitrary"),
    )(c_idx, w, mine, theirs, m, v, *deps)


ANY = pl.BlockSpec(memory_space=pl.ANY)


def _place():
    x, y, c = lax.axis_index("x"), lax.axis_index("y"), lax.axis_index("c")
    chips = [(1 - x, y), (x, 1 - y), (1 - x, 1 - y)]
    return x, y, c, chips


def _remote(src, dst, send_sem, recv_sem, to):
    return pltpu.make_async_remote_copy(src_ref=src, dst_ref=dst, send_sem=send_sem, recv_sem=recv_sem,
                                        device_id=to, device_id_type=MESH)


HBM = pl.BlockSpec(memory_space=pltpu.HBM)
SEM = pl.BlockSpec(memory_space=pltpu.SEMAPHORE)
EFFECT = pltpu.SideEffectType.DATAFLOW_SIDE_EFFECTING


def _in_hbm(a):
    return pltpu.with_memory_space_constraint(a, pltpu.HBM)


def _half(ref_rows, who):
    return pl.ds(who * (ref_rows // 2), ref_rows // 2)


def _gather_start(groups):
    items = [it for g in groups for it in g]
    n = len(items)
    sizes = [len(g) for g in groups]

    def body(*refs):
        srcs, lands = refs[:n], refs[n:2 * n]
        sems = refs[2 * n:2 * n + 2 * len(groups)]
        token = refs[-1]
        x, y, c, chips = _place()
        j = 2 * x + y
        at = 0
        for gi, g in enumerate(groups):
            send, recv = sems[2 * gi], sems[2 * gi + 1]
            for i, (shard, split) in enumerate(g):
                src, land = srcs[at], lands[at]
                at += 1
                rows = _half(shard.shape[0], c) if split else slice(None)
                for k, chip in enumerate(chips):
                    _remote(src.at[rows], land.at[j, rows], send.at[3 * i + k], recv.at[3 * i + k], (*chip, c)).start()
        token[...] = jnp.zeros_like(token)

    sem_shapes = []
    for sz in sizes:
        sem_shapes += [pltpu.SemaphoreType.DMA((3 * sz,)), pltpu.SemaphoreType.DMA((3 * sz,))]
    out_shape = (sem_shapes + [pltpu.HBM(sh.shape, sh.dtype) for sh, _ in items]
                 + [pltpu.HBM((N_CHIPS,) + sh.shape, sh.dtype) for sh, _ in items]
                 + [jax.ShapeDtypeStruct((8, LANES), F32)])
    ns = len(sem_shapes)
    outs = pl.pallas_call(
        body, name="gather_start", in_specs=[HBM] * (2 * n),
        out_specs=[SEM] * ns + [HBM] * (2 * n) + [pl.BlockSpec(memory_space=pltpu.VMEM)],
        out_shape=out_shape, input_output_aliases={i: ns + i for i in range(2 * n)},
        compiler_params=pltpu.CompilerParams(has_side_effects=EFFECT),
    )(*[_in_hbm(sh) for sh, _ in items], *[_in_hbm(lax.empty((N_CHIPS,) + sh.shape, sh.dtype)) for sh, _ in items])
    sems, shards, lands, token = outs[:ns], outs[ns:ns + n], outs[ns + n:ns + 2 * n], outs[-1]
    res, at = [], 0
    for gi, sz in enumerate(sizes):
        res.append((shards[at:at + sz], lands[at:at + sz], sems[2 * gi], sems[2 * gi + 1]))
        at += sz
    return res, token


def _gather_pass(group, started, after, name):
    shards, lands, send, recv = started
    n = len(group)
    split_ix = [i for i, (_, split) in enumerate(group) if split]

    def body(*refs):
        lnds, send1, recv1 = refs[n:2 * n], refs[2 * n], refs[2 * n + 1]
        outs = refs[2 * n + 2 + len(after):]
        send2, recv2, token = outs[2 * n], outs[2 * n + 1], outs[2 * n + 2]
        x, y, c, chips = _place()
        sib = (x, y, 1 - c)
        for i, (shard, split) in enumerate(group):
            rows = _half(shard.shape[0], c) if split else slice(None)
            for k, (cx, cy) in enumerate(chips):
                landed = lnds[i].at[2 * cx + cy, rows]
                cp = _remote(landed, landed, send1.at[3 * i + k], recv1.at[3 * i + k], sib)
                cp.wait_send()
                cp.wait_recv()
        for i2, i in enumerate(split_ix):
            rows = _half(group[i][0].shape[0], c)
            for k, (cx, cy) in enumerate(chips):
                landed = lnds[i].at[2 * cx + cy, rows]
                _remote(landed, landed, send2.at[3 * i2 + k], recv2.at[3 * i2 + k], sib).start()
        token[...] = jnp.zeros_like(token)

    n2 = len(split_ix)
    out_shape = ([pltpu.HBM(a.shape, a.dtype) for a in (*shards, *lands)]
                 + [pltpu.SemaphoreType.DMA((3 * n2,)), pltpu.SemaphoreType.DMA((3 * n2,)), jax.ShapeDtypeStruct((8, LANES), F32)])
    outs = pl.pallas_call(
        body, name=name, in_specs=[HBM] * (2 * n) + [SEM, SEM] + [ANY] * len(after),
        out_specs=[HBM] * (2 * n) + [SEM, SEM, pl.BlockSpec(memory_space=pltpu.VMEM)],
        out_shape=out_shape, input_output_aliases={i: i for i in range(2 * n)},
        compiler_params=pltpu.CompilerParams(has_side_effects=EFFECT),
    )(*shards, *lands, send, recv, *after)
    return outs[:n], (outs[n:2 * n], outs[2 * n], outs[2 * n + 1]), outs[2 * n + 2]


def _gather_wait(group, passed, after, name):
    lands, send2, recv2 = passed
    n = len(group)
    split_ix = [i for i, (_, split) in enumerate(group) if split]

    def body(*refs):
        lnds, s2, r2 = refs[:n], refs[n], refs[n + 1]
        x, y, c, chips = _place()
        sib = (x, y, 1 - c)
        for i2, i in enumerate(split_ix):
            rows = _half(group[i][0].shape[0], 1 - c)
            for k, (cx, cy) in enumerate(chips):
                landed = lnds[i].at[2 * cx + cy, rows]
                cp = _remote(landed, landed, s2.at[3 * i2 + k], r2.at[3 * i2 + k], sib)
                cp.wait_send()
                cp.wait_recv()

    return pl.pallas_call(
        body, name=name, in_specs=[HBM] * n + [SEM, SEM, ANY], out_specs=[HBM] * n,
        out_shape=[pltpu.HBM(a.shape, a.dtype) for a in lands], input_output_aliases={i: i for i in range(n)},
        compiler_params=pltpu.CompilerParams(has_side_effects=EFFECT),
    )(*lands, send2, recv2, after)


def _own_slab(land, shard):
    chip = 2 * lax.axis_index("x") + lax.axis_index("y")
    return lax.dynamic_update_slice(land, shard[None], (chip, 0, 0))


def _xfer_start(name, srcs, land_shapes, n_copies, copies, after):
    n = len(srcs)

    def body(*refs):
        src_refs, land_refs = refs[:n], refs[n:2 * n]
        send, recv, token = refs[2 * n + 1], refs[2 * n + 2], refs[-1]
        for cp in copies(src_refs, land_refs, send, recv):
            cp.start()
        token[...] = jnp.zeros_like(token)

    lands = [_in_hbm(lax.empty(shape, dtype)) for shape, dtype in land_shapes]
    out_shape = ([pltpu.SemaphoreType.DMA((n_copies,)), pltpu.SemaphoreType.DMA((n_copies,))]
                 + [pltpu.HBM(a.shape, a.dtype) for a in (*srcs, *lands)] + [jax.ShapeDtypeStruct((8, LANES), F32)])
    outs = pl.pallas_call(
        body, name=name, in_specs=[HBM] * (2 * n) + [ANY],
        out_specs=[SEM, SEM] + [HBM] * (2 * n) + [pl.BlockSpec(memory_space=pltpu.VMEM)],
        out_shape=out_shape, input_output_aliases={i: 2 + i for i in range(2 * n)},
        compiler_params=pltpu.CompilerParams(has_side_effects=EFFECT),
    )(*[_in_hbm(a) for a in srcs], *lands, after)
    return (outs[2:2 + n], outs[2 + n:2 + 2 * n], outs[0], outs[1]), outs[-1]


def _xfer_wait(name, started, copies, after):
    srcs, lands, send, recv = started
    n = len(srcs)

    def body(*refs):
        src_refs, land_refs, s_ref, r_ref = refs[:n], refs[n:2 * n], refs[2 * n], refs[2 * n + 1]
        for cp in copies(src_refs, land_refs, s_ref, r_ref):
            cp.wait_send()
            cp.wait_recv()

    outs = pl.pallas_call(
        body, name=name, in_specs=[HBM] * (2 * n) + [SEM, SEM, ANY], out_specs=[HBM] * (2 * n),
        out_shape=[pltpu.HBM(a.shape, a.dtype) for a in (*srcs, *lands)],
        input_output_aliases={i: i for i in range(2 * n)},
        compiler_params=pltpu.CompilerParams(has_side_effects=EFFECT),
    )(*srcs, *lands, send, recv, after)
    return outs[:n], outs[n:]


def _swap_copies(srcs, lands, send, recv):
    x, y, c, _ = _place()
    return [_remote(src.at[:, _half(src.shape[1], 1 - c)], land, send.at[i], recv.at[i], (x, y, 1 - c))
            for i, (src, land) in enumerate(zip(srcs, lands))]


def _scatter_copies(srcs, lands, send, recv):
    x, y, c, chips = _place()
    return [_remote(src.at[2 * cx + cy], land.at[k], send.at[3 * i + k], recv.at[3 * i + k], (cx, cy, c))
            for i, (src, land) in enumerate(zip(srcs, lands)) for k, (cx, cy) in enumerate(chips)]


def _join_copies(srcs, lands, send, recv):
    x, y, c, _ = _place()
    return [_remote(src, land, send.at[i], recv.at[i], (x, y, 1 - c)) for i, (src, land) in enumerate(zip(srcs, lands))]


class _Reducer:
    def __init__(self, place):
        self.place = place
        self.state = {}

    def swap(self, key, grads, after):
        shapes = [((g.shape[0], g.shape[1] // 2, g.shape[2]), g.dtype) for g in grads]
        self.state[key], token = _xfer_start("swap_start_" + key, grads, shapes, len(grads), _swap_copies, after)
        return token

    def to_chips(self, key, after):
        grads, from_sibling = _xfer_wait("swap_wait_" + key, self.state[key], _swap_copies, after)
        sums = [_add_halves(g, r, self.place) for g, r in zip(grads, from_sibling)]
        shapes = [((3,) + s.shape[1:], s.dtype) for s in sums]
        started, token = _xfer_start("scatter_start_" + key, sums, shapes, 3 * len(sums), _scatter_copies, sums[-1])
        self.state[key] = (grads, from_sibling, started)
        return token

    def to_core(self, key, after):
        grads, from_sibling, started = self.state[key]
        _, from_chips = _xfer_wait("scatter_wait_" + key, started, _scatter_copies, after)
        halves = [_sum_chips(g, r, rc, self.place) for g, r, rc in zip(grads, from_sibling, from_chips)]
        shapes = [(h.shape, h.dtype) for h in halves]
        self.state[key], token = _xfer_start("join_start_" + key, halves, shapes, len(halves), _join_copies, halves[-1])
        return token

    def finish(self, key, after):
        return _xfer_wait("join_wait_" + key, self.state.pop(key), _join_copies, after)


def _gather_packs(pack):
    def body(p_ref, o_ref, lsem, ssem, rsem):
        x, y, c, _ = _place()
        me = 4 * x + 2 * y + c
        local = pltpu.make_async_copy(p_ref, o_ref.at[me], lsem)
        local.start()
        cps = []
        for k in range(1, N_DEV):
            fx, fy, fc = (k >> 2) & 1, (k >> 1) & 1, k & 1
            to = (x ^ fx, y ^ fy, c ^ fc)
            cps.append(_remote(p_ref, o_ref.at[me], ssem.at[k - 1], rsem.at[k - 1], to))
        for cp in cps:
            cp.start()
        for k in range(1, N_DEV):
            fx, fy, fc = (k >> 2) & 1, (k >> 1) & 1, k & 1
            src = o_ref.at[4 * (x ^ fx) + 2 * (y ^ fy) + (c ^ fc)]
            _remote(src, src, ssem.at[k - 1], rsem.at[k - 1], (x, y, c)).wait_recv()
        for cp in cps:
            cp.wait_send()
        local.wait()

    return pl.pallas_call(
        body, name="gather_packs", in_specs=[ANY], out_specs=ANY,
        out_shape=jax.ShapeDtypeStruct((N_DEV,) + pack.shape, pack.dtype),
        scratch_shapes=[pltpu.SemaphoreType.DMA, pltpu.SemaphoreType.DMA((N_DEV - 1,)), pltpu.SemaphoreType.DMA((N_DEV - 1,))],
    )(pack)


LANE_TILES = (512, 896, 1408, 704, 384, 256, 128)


def _layer_grads(x, target, small, wg, rest_pass, rest_wait, red):
    s, d = x.shape
    f = wg["conv"].shape[1] // 2
    w_att = N_HEADS * HEAD_DIM
    in_splits = (w_att, w_att, w_att, N_HEADS, w_att, w_att, w_att, d, d)
    in_cols = sum(in_splits)
    cs = in_cols // N_CHIPS
    cp = wg["in"].shape[2]
    tm = min(s, 1024)
    t_in = _pick(cp, LANE_TILES)
    t_d = _pick(d, LANE_TILES)
    t_d2 = min(d, 1024)
    t_dq = _pick(d // N_CHIPS, LANE_TILES)
    t_w = _pick(w_att, LANE_TILES)
    t_up = _pick(2 * f // N_CHIPS, LANE_TILES)
    t_fq = _pick(f // N_CHIPS, LANE_TILES)
    offs = np.cumsum(in_splits)[:-1].tolist()

    h1 = _norm_fwd(x, small["g_attn"], group=d, name="rms1_fwd")
    proj_p = _mm(h1, wg["in"], mode="nn", b_kind="col", tm=tm, tn=t_in, tk=d, name="mm_in")
    qa, ka, va_b, fa, qb, kb, vb_b, ga, gb = _proj_split(
        proj_p, in_splits, cs, (F32, F32, BF16, F32, F32, F32, BF16, F32, F32))
    gains = {n: small[n].reshape(1, w_att) for n in ("g_q_fox", "g_k_fox", "g_q_dil", "g_k_dil")}
    qa_n = _norm_fwd(qa, gains["g_q_fox"], group=HEAD_DIM, name="qnorm_fox")
    ka_n = _norm_fwd(ka, gains["g_k_fox"], group=HEAD_DIM, name="knorm_fox")
    qb_n = _norm_fwd(qb, gains["g_q_dil"], group=HEAD_DIM, name="qnorm_dil")
    kb_n = _norm_fwd(kb, gains["g_k_dil"], group=HEAD_DIM, name="knorm_dil")
    fa_t = fa.T
    b_f = small["b_forget"].reshape(N_HEADS, 1)
    c_f = _forget_fwd(fa_t, b_f)
    slopes = jnp.asarray(2.0 ** (-8.0 * np.arange(1, N_HEADS + 1) / N_HEADS), dtype=F32)
    a_d = -(slopes[:, None] * jnp.arange(s, dtype=F32)[None, :])
    rows_f, cols_f = c_f[:, :, None], c_f[:, None, :]
    rows_d, cols_d = a_d[:, :, None], a_d[:, None, :]
    o_a, o_a32, lse_a = _attn_fwd(qa_n, ka_n, va_b, rows_f, cols_f, dilated=False, name="attn_fox_fwd")
    token = rest_pass(o_a)
    rows_d = rows_d + token[0, 0]
    o_b, o_b32, lse_b = _attn_fwd(qb_n, kb_n, vb_b, rows_d, cols_d, dilated=True, name="attn_dil_fwd")
    wg = dict(wg, **rest_wait(o_b))
    pa = _mm(o_a, wg["brf"], mode="nn", b_kind="col", tm=tm, tn=t_dq, tk=w_att, name="mm_brf")
    pb = _mm(o_b, wg["brd"], mode="nn", b_kind="col", tm=tm, tn=t_dq, tk=w_att, name="mm_brd")
    merged = _gate_fwd(ga, gb, pa, pb)
    x1 = _mm(merged, wg["out"], mode="nn", b_kind="row", res=x, tm=tm, tn=t_d, tk=t_dq, name="mm_out")
    h2 = _norm_fwd(x1, small["g_ffn"], group=d, name="rms2_fwd")
    u = _mm(h2, wg["up"], mode="nn", b_kind="col", tm=tm, tn=t_up, tk=d, name="mm_up")
    act = _conv_glu_fwd(u, wg["conv"], wg["bconv"])
    y = _mm(act, wg["down"], mode="nn", b_kind="row", res=x1, tm=tm, tn=t_d2, tk=t_fq, name="mm_down")
    dy_f, dy_b, loss_blk = _loss_head(y, target)

    d_act = _mm(dy_b, wg["down"], mode="nt", b_kind="row", tm=tm, tn=t_fq, tk=d, name="mm_down_dx")
    g_down = _mm(act, dy_b, mode="tn", out_dtype=BF16, out_kind="row", tm=t_fq, tn=t_d2, tk=s, name="mm_down_dw")
    tok = red.swap("down", [g_down], g_down)
    du_g, du_v, st_g, st_v = _conv_glu_bwd(u, d_act, wg["conv"] + tok[0, 0], wg["bconv"])
    tok = red.to_chips("down", du_g)
    du = jnp.concatenate([du_g, du_v], axis=1)
    g_up = _mm(h2, du, mode="tn", out_dtype=BF16, out_kind="col", tm=t_d2, tn=t_up, tk=s, name="mm_up_dw", deps=(tok,))
    tok = red.to_core("down", g_up)
    tok2 = red.swap("up", [g_up], g_up)
    dh2 = _mm(du, wg["up"], mode="nt", b_kind="col", tm=s, tn=t_d2, tk=t_up, name="mm_up_dx", deps=(tok, tok2))
    tok = red.to_chips("up", dh2)
    dx1_b, dx1_f, dg_ffn = _norm_bwd(dh2, x1, small["g_ffn"], group=d, res=dy_f, out_dtypes=(BF16, F32), name="rms2_bwd")
    d_merged = _mm(dx1_b, wg["out"], mode="nt", b_kind="row", tm=tm, tn=t_dq, tk=d, name="mm_out_dx", deps=(tok,))
    g_out = _mm(merged, dx1_b, mode="tn", out_dtype=BF16, out_kind="row", tm=t_dq, tn=t_d2, tk=s, name="mm_out_dw")
    dpa, dpb, dga, dgb = _gate_bwd(d_merged, ga, gb, pa, pb)
    do_a = _mm(dpa, wg["brf"], mode="nt", b_kind="col", out_dtype=BF16, tm=s, tn=w_att, tk=t_dq, name="mm_brf_dx")
    do_b = _mm(dpb, wg["brd"], mode="nt", b_kind="col", out_dtype=BF16, tm=s, tn=w_att, tk=t_dq, name="mm_brd_dx")
    g_brf = _mm(o_a, dpa, mode="tn", out_dtype=BF16, out_kind="col", tm=w_att, tn=t_dq, tk=s, name="mm_brf_dw")
    g_brd = _mm(o_b, dpb, mode="tn", out_dtype=BF16, out_kind="col", tm=w_att, tn=t_dq, tk=s, name="mm_brd_dw")
    tok = red.swap("mix", [g_out, g_brf, g_brd], g_brd)
    dqa_n, dka_n, dva, dac_a = _attn_bwd(qa_n, ka_n, va_b, o_a32, do_a, lse_a, rows_f + tok[0, 0], cols_f, dilated=False, name="attn_fox_bwd")
    tok = red.to_core("up", dqa_n)
    tok2 = red.to_chips("mix", dqa_n)
    dqb_n, dkb_n, dvb, _ = _attn_bwd(qb_n, kb_n, vb_b, o_b32, do_b, lse_b, rows_d + (tok[0, 0] + tok2[0, 0]), cols_d, dilated=True, name="attn_dil_bwd")
    tok = red.to_core("mix", dqb_n)
    dqa, dg_qf = _norm_bwd(dqa_n, qa, gains["g_q_fox"], group=HEAD_DIM, name="qnorm_fox_bwd")
    dka, dg_kf = _norm_bwd(dka_n, ka, gains["g_k_fox"], group=HEAD_DIM, name="knorm_fox_bwd")
    dqb, dg_qd = _norm_bwd(dqb_n, qb, gains["g_q_dil"], group=HEAD_DIM, name="qnorm_dil_bwd")
    dkb, dg_kd = _norm_bwd(dkb_n, kb, gains["g_k_dil"], group=HEAD_DIM, name="knorm_dil_bwd")
    dfa_t, db_f = _forget_bwd(dac_a[:, 0, :], fa_t, b_f)
    dproj_p = _dproj_merge([dqa, dka, dva, dfa_t.T, dqb, dkb, dvb, dga, dgb], in_splits, cs, cp)
    g_in = _mm(h1, dproj_p, mode="tn", out_dtype=BF16, out_kind="col", tm=t_d2, tn=t_in, tk=s, name="mm_in_dw", deps=(tok,))
    tok = red.swap("in", [g_in], g_in)
    dh1 = _mm(dproj_p, wg["in"], mode="nt", b_kind="col", tm=s, tn=t_d2, tk=t_in, name="mm_in_dx", deps=(tok,))
    tok = red.to_chips("in", dh1)
    grad_x, dg_attn = _norm_bwd(dh1, x, small["g_attn"], group=d, res=dx1_f, out_dtypes=(F32,), name="rms1_bwd")

    small_grads = {
        "g_attn": dg_attn, "b_forget": db_f.reshape(1, N_HEADS),
        "g_q_fox": dg_qf, "g_k_fox": dg_kf, "g_q_dil": dg_qd, "g_k_dil": dg_kd, "g_ffn": dg_ffn,
        "w_conv": jnp.concatenate([st_g[0:3], st_v[0:3]], axis=1),
        "b_conv": jnp.concatenate([st_g[3:4], st_v[3:4]], axis=1),
        "loss": loss_blk[0:1, 0:1],
    }
    return small_grads, grad_x, tok


SMALL_ORDER = ("g_attn", "b_forget", "g_q_fox", "g_k_fox", "g_q_dil", "g_k_dil", "g_ffn", "w_conv", "b_conv", "loss")
WEIGHT_ORDER = ("g_attn", "w_in", "b_forget", "g_q_fox", "g_k_fox", "g_q_dil", "g_k_dil", "w_br_fox", "w_br_dil",
                "w_out", "g_ffn", "w_up", "w_conv", "b_conv", "w_down")
BIG = {"w_in": "in", "w_br_fox": "brf", "w_br_dil": "brd", "w_out": "out", "w_up": "up", "w_down": "down"}


def kernel(x, g_attn, w_in, b_forget, g_q_fox, g_k_fox, g_q_dil, g_k_dil, w_br_fox, w_br_dil, w_out, g_ffn, w_up, w_conv, b_conv, w_down, loss_target, m_g_attn, m_w_in, m_b_forget, m_g_q_fox, m_g_k_fox, m_g_q_dil, m_g_k_dil, m_w_br_fox, m_w_br_dil, m_w_out, m_g_ffn, m_w_up, m_w_conv, m_b_conv, m_w_down, v_g_attn, v_w_in, v_b_forget, v_g_q_fox, v_g_k_fox, v_g_q_dil, v_g_k_dil, v_w_br_fox, v_w_br_dil, v_w_out, v_g_ffn, v_w_up, v_w_conv, v_b_conv, v_w_down):
    w = dict(g_attn=g_attn, w_in=w_in, b_forget=b_forget, g_q_fox=g_q_fox, g_k_fox=g_k_fox, g_q_dil=g_q_dil,
             g_k_dil=g_k_dil, w_br_fox=w_br_fox, w_br_dil=w_br_dil, w_out=w_out, g_ffn=g_ffn, w_up=w_up,
             w_conv=w_conv, b_conv=b_conv, w_down=w_down)
    m = dict(g_attn=m_g_attn, w_in=m_w_in, b_forget=m_b_forget, g_q_fox=m_g_q_fox, g_k_fox=m_g_k_fox,
             g_q_dil=m_g_q_dil, g_k_dil=m_g_k_dil, w_br_fox=m_w_br_fox, w_br_dil=m_w_br_dil, w_out=m_w_out,
             g_ffn=m_g_ffn, w_up=m_w_up, w_conv=m_w_conv, b_conv=m_b_conv, w_down=m_w_down)
    v = dict(g_attn=v_g_attn, w_in=v_w_in, b_forget=v_b_forget, g_q_fox=v_g_q_fox, g_k_fox=v_g_k_fox,
             g_q_dil=v_g_q_dil, g_k_dil=v_g_k_dil, w_br_fox=v_w_br_fox, w_br_dil=v_w_br_dil, w_out=v_w_out,
             g_ffn=v_g_ffn, w_up=v_w_up, w_conv=v_w_conv, b_conv=v_b_conv, w_down=v_w_down)
    xi, yi, ci = lax.axis_index("x"), lax.axis_index("y"), lax.axis_index("c")
    chip = (2 * xi + yi).astype(jnp.int32)
    c_idx = ci.astype(jnp.int32).reshape(1)
    j_idx = chip.reshape(1)

    cs = w_in.shape[2]
    cp = _round_up(cs, LANES)
    shards = {
        "in": jnp.pad(w_in[0].astype(BF16), ((0, 0), (0, cp - cs))),
        "brf": w_br_fox[0].astype(BF16), "brd": w_br_dil[0].astype(BF16), "out": w_out[0].astype(BF16),
        "up": w_up[0].astype(BF16), "down": w_down[0].astype(BF16),
    }
    names = tuple(shards)
    conv_pad = jnp.pad(w_conv[0], ((0, 8 - w_conv.shape[1]), (0, 0)))
    first = [(shards["in"], True), (conv_pad, False)]
    rest_names = names[1:]
    rest = [(shards[n], True) for n in rest_names]
    (started_first, started_rest), token = _gather_start([first, rest])
    w2, m2, v2 = ({n: a[n].reshape(a[n].shape[-2], a[n].shape[-1]) for n in BIG} for a in (w, m, v))
    early = (token, w2["w_in"], m2["w_in"], v2["w_in"])
    own_first, passed_first, token = _gather_pass(first, started_first, early, "gather_pass_in")
    land_in, land_conv = _gather_wait(first, passed_first, token, "gather_wait_in")
    conv_all = _own_slab(land_conv, own_first[1])
    wg = {"in": _own_slab(land_in, own_first[0]), "bconv": b_conv,
          "conv": jnp.transpose(conv_all[:, :w_conv.shape[1], :], (1, 0, 2)).reshape(w_conv.shape[1], -1)}
    small = {n: w[n] for n in ("g_attn", "b_forget", "g_q_fox", "g_k_fox", "g_q_dil", "g_k_dil", "g_ffn")}
    small = {n: (a[0] if a.ndim == 3 else a) for n, a in small.items()}
    in_flight = {}

    def rest_pass(after):
        in_flight["own"], in_flight["passed"], tok = _gather_pass(rest, started_rest, (after,), "gather_pass_rest")
        return tok

    def rest_wait(after):
        lands = _gather_wait(rest, in_flight["passed"], after, "gather_wait_rest")
        return {n: _own_slab(land, own) for n, land, own in zip(rest_names, lands, in_flight["own"])}

    reducer = _Reducer(jnp.stack([chip, ci.astype(jnp.int32)]))
    small_grads, grad_x, tok_in = _layer_grads(x[0], loss_target[0], small, wg, rest_pass, rest_wait, reducer)

    mine, theirs = {}, {}
    for key, members in (("down", ("down",)), ("up", ("up",)), ("mix", ("out", "brf", "brd"))):
        mine_k, theirs_k = reducer.finish(key, grad_x)
        mine.update(zip(members, mine_k))
        theirs.update(zip(members, theirs_k))

    flat = jnp.concatenate([small_grads[n].reshape(-1) for n in SMALL_ORDER])
    rows = _round_up(flat.shape[0], 8 * LANES) // LANES
    pack = jnp.pad(flat, (0, rows * LANES - flat.shape[0])).reshape(rows, LANES)
    total = _sum_devices(_gather_packs(pack)).reshape(-1)
    red, at = {}, 0
    for n in SMALL_ORDER:
        size = small_grads[n].size
        red[n] = total[at:at + size].reshape(small_grads[n].shape)
        at += size
    loss = red["loss"].reshape(())
    c2 = w_conv.shape[2]
    red["w_conv"] = lax.dynamic_slice_in_dim(red["w_conv"], chip * c2, c2, axis=1)

    g_out, d_out, m_out, v_out = {}, {}, {}, {}
    last = [n for n in WEIGHT_ORDER if n != "w_in"] + ["w_in"]
    for n in last:
        shape = w[n].shape
        r2 = (shape[-2], shape[-1]) if n not in ("g_attn", "b_forget", "g_ffn", "b_conv") else (1, shape[-1])
        if n == "w_in":
            done = jnp.stack([v_out[k][(0,) * v_out[k].ndim] for k in last[:-1]])
            tok = reducer.to_core("in", done)
            (mine_in,), (theirs_in,) = reducer.finish("in", tok)
            mine["in"], theirs["in"] = mine_in[:, :cs], theirs_in[:, :cs]
        if n in BIG:
            g2, dl, mn, vn = _adamw_halves(w2[n], mine[BIG[n]], theirs[BIG[n]], c_idx, m2[n], v2[n],
                                           name="adamw_" + n, deps=(tok_in,))
        else:
            g2 = red[n].reshape(r2)
            dl, mn, vn = _adamw(w[n].reshape(r2), g2, m[n].reshape(r2), v[n].reshape(r2), name="adamw_" + n,
                                deps=(tok_in,))
        g_out[n], d_out[n], m_out[n], v_out[n] = (a.reshape(shape) for a in (g2, dl, mn, vn))

    return (loss, grad_x[None], *[g_out[n] for n in WEIGHT_ORDER], *[d_out[n] for n in WEIGHT_ORDER],
            *[m_out[n] for n in WEIGHT_ORDER], *[v_out[n] for n in WEIGHT_ORDER])
```

```python
import functools
import math

import jax
import jax.numpy as jnp
import numpy as np
from jax import lax
from jax.experimental import pallas as pl
from jax.experimental.pallas import tpu as pltpu

F32 = jnp.float32
BF16 = jnp.bfloat16
HEAD_DIM = 128
N_HEADS = 8
EPS = 1e-6
NEG = -1e30
N_CHIPS = 4
N_DEV = 8
LANES = 128
VMEM_LIMIT_BYTES = 56 * 1024 * 1024
DIL_PATTERNS = ((128, 1), (512, 4), (2048, 16))
ATTN_TILE = 512
ADAM_LR, ADAM_B1, ADAM_B2, ADAM_EPS, ADAM_WD, ADAM_STEP = 0.001, 0.9, 0.999, 1e-08, 0.01, 10
MESH = pl.DeviceIdType.MESH


def _params(*sem):
    return pltpu.CompilerParams(dimension_semantics=sem, vmem_limit_bytes=VMEM_LIMIT_BYTES)


def _round_up(n, m):
    return -(-n // m) * m


def _pick(dim, prefs):
    for p in prefs:
        if dim % p == 0:
            return p
    raise ValueError(f"no tile for {dim} in {prefs}")


def _logical_shape(arr, kind):
    if kind is None:
        return arr.shape
    s, r, c = arr.shape
    return (r, s * c) if kind == "col" else (s * r, c)


def _spec(shape, kind, br, bc, fi, fj):
    if kind is None:
        return pl.BlockSpec((br, bc), lambda *g: (fi(*g), fj(*g)))
    _, r, c = shape
    if kind == "col":
        nb = c // bc
        assert nb * bc == c, (shape, bc)
        return pl.BlockSpec((None, br, bc), lambda *g: (fj(*g) // nb, fi(*g), fj(*g) % nb))
    nb = r // br
    assert nb * br == r, (shape, br)
    return pl.BlockSpec((None, br, bc), lambda *g: (fi(*g) // nb, fi(*g) % nb, fj(*g)))


def _mm(a, b, *, mode, tm, tn, tk, name, a_kind=None, b_kind=None, out_kind=None,
        out_dtype=F32, res=None, deps=()):
    la, lb = _logical_shape(a, a_kind), _logical_shape(b, b_kind)
    if mode == "nn":
        (m, k), (k2, n) = la, lb
    elif mode == "nt":
        (m, k), (n, k2) = la, lb
    else:
        (k, m), (k2, n) = la, lb
    assert k == k2, (name, la, lb)
    assert m % tm == 0 and n % tn == 0 and k % tk == 0, (name, m, n, k, tm, tn, tk)
    nk = k // tk
    im = lambda i, j, l: i
    jn = lambda i, j, l: j
    lk = lambda i, j, l: l
    if mode == "tn":
        a_spec = _spec(a.shape, a_kind, tk, tm, lk, im)
        dims = (((0,), (0,)), ((), ()))
    else:
        a_spec = _spec(a.shape, a_kind, tm, tk, im, lk)
        dims = (((1,), (1,)), ((), ())) if mode == "nt" else (((1,), (0,)), ((), ()))
    if mode == "nt":
        b_spec = _spec(b.shape, b_kind, tn, tk, jn, lk)
    else:
        b_spec = _spec(b.shape, b_kind, tk, tn, lk, jn)
    if out_kind is None:
        oshape = (m, n)
    elif out_kind == "col":
        oshape = (N_CHIPS, m, n // N_CHIPS)
    else:
        oshape = (N_CHIPS, m // N_CHIPS, n)
    o_spec = _spec(oshape, out_kind, tm, tn, im, jn)
    in_specs = [a_spec, b_spec]
    args = [a, b]
    if res is not None:
        in_specs.append(pl.BlockSpec((tm, tn), lambda i, j, l: (i, j)))
        args.append(res)
    in_specs += [pl.BlockSpec(memory_space=pl.ANY)] * len(deps)
    args += list(deps)

    def body(*refs):
        a_ref, b_ref = refs[0], refs[1]
        res_ref = refs[2] if res is not None else None
        o_ref, acc_ref = refs[-2], refs[-1]
        step = pl.program_id(2)

        @pl.when(step == 0)
        def _():
            acc_ref[...] = jnp.zeros_like(acc_ref)

        acc_ref[...] += lax.dot_general(a_ref[...], b_ref[...], dims, preferred_element_type=F32)

        @pl.when(step == nk - 1)
        def _():
            out = acc_ref[...]
            if res_ref is not None:
                out = out + res_ref[...]
            o_ref[...] = out.astype(o_ref.dtype)

    return pl.pallas_call(
        body, name=name, grid=(m // tm, n // tn, nk),
        in_specs=in_specs, out_specs=o_spec,
        out_shape=jax.ShapeDtypeStruct(oshape, out_dtype),
        scratch_shapes=[pltpu.VMEM((tm, tn), F32)],
        compiler_params=_params("parallel", "parallel", "arbitrary"),
    )(*args)


def _pieces(splits, cs, cp):
    out, g0 = [], 0
    for width in splits:
        g1, runs = g0 + width, []
        for j in range(N_CHIPS):
            a, b = max(g0, cs * j), min(g1, cs * (j + 1))
            if a < b:
                runs.append((j * cp + a - cs * j, a - g0, b - a))
        out.append(runs)
        g0 = g1
    return out


def _proj_split(proj_p, splits, cs, dtypes, tm=128):
    s, wp = proj_p.shape
    pieces = _pieces(splits, cs, wp // N_CHIPS)

    def body(p_ref, *o_refs):
        for o_ref, runs in zip(o_refs, pieces):
            for src, dst, n in runs:
                o_ref[:, dst:dst + n] = p_ref[:, src:src + n].astype(o_ref.dtype)

    return pl.pallas_call(
        body, name="proj_split", grid=(s // tm,), in_specs=[pl.BlockSpec((tm, wp), lambda i: (i, 0))],
        out_specs=[pl.BlockSpec((tm, w), lambda i: (i, 0)) for w in splits],
        out_shape=[jax.ShapeDtypeStruct((s, w), dt) for w, dt in zip(splits, dtypes)],
        compiler_params=_params("parallel"),
    )(proj_p)


def _dproj_merge(parts, splits, cs, cp, tm=128):
    s = parts[0].shape[0]
    wp = N_CHIPS * cp
    pieces = _pieces(splits, cs, cp)

    def body(*refs):
        o_ref, stage = refs[-2], refs[-1]
        for j in range(N_CHIPS):
            stage[:, j * cp + cs:(j + 1) * cp] = jnp.zeros((tm, cp - cs), F32)
        for p_ref, runs in zip(refs, pieces):
            for dst, src, n in runs:
                stage[:, dst:dst + n] = p_ref[:, src:src + n].astype(F32)
        o_ref[...] = stage[...].astype(o_ref.dtype)

    return pl.pallas_call(
        body, name="dproj_merge", grid=(s // tm,),
        in_specs=[pl.BlockSpec((tm, w), lambda i: (i, 0)) for w in splits],
        out_specs=pl.BlockSpec((tm, wp), lambda i: (i, 0)),
        out_shape=jax.ShapeDtypeStruct((s, wp), BF16), scratch_shapes=[pltpu.VMEM((tm, wp), F32)],
        compiler_params=_params("parallel"),
    )(*parts)


def _norm_fwd(x, g, *, group, name, tm=256):
    s, w = x.shape
    ng = w // group

    def body(x_ref, g_ref, o_ref):
        for i in range(ng):
            cols = slice(i * group, (i + 1) * group)
            xv = x_ref[:, cols]
            r = lax.rsqrt(jnp.mean(xv * xv, axis=-1, keepdims=True) + EPS)
            o_ref[:, cols] = ((xv * r) * g_ref[:, cols]).astype(o_ref.dtype)

    return pl.pallas_call(
        body, name=name, grid=(s // tm,),
        in_specs=[pl.BlockSpec((tm, w), lambda i: (i, 0)), pl.BlockSpec((1, w), lambda i: (0, 0))],
        out_specs=pl.BlockSpec((tm, w), lambda i: (i, 0)),
        out_shape=jax.ShapeDtypeStruct((s, w), BF16),
        compiler_params=_params("parallel"),
    )(x, g)


def _norm_bwd(dy, x, g, *, group, name, res=None, out_dtypes=(BF16,), tm=256):
    s, w = x.shape
    ng = w // group
    n_in = 4 if res is not None else 3

    def body(*refs):
        dy_ref, x_ref, g_ref = refs[:3]
        res_ref = refs[3] if res is not None else None
        outs = refs[n_in:]
        dx_refs, dg_ref = outs[:-1], outs[-1]

        @pl.when(pl.program_id(0) == 0)
        def _():
            dg_ref[...] = jnp.zeros_like(dg_ref)

        for i in range(ng):
            cols = slice(i * group, (i + 1) * group)
            xv = x_ref[:, cols]
            dyv = dy_ref[:, cols].astype(F32)
            r = lax.rsqrt(jnp.mean(xv * xv, axis=-1, keepdims=True) + EPS)
            xr = xv * r
            dg_ref[:, cols] += jnp.sum(dyv * xr, axis=0, keepdims=True)
            gdy = dyv * g_ref[:, cols]
            dx = r * (gdy - xr * jnp.mean(gdy * xr, axis=-1, keepdims=True))
            if res_ref is not None:
                dx = dx + res_ref[:, cols]
            for dx_ref in dx_refs:
                dx_ref[:, cols] = dx.astype(dx_ref.dtype)

    row = pl.BlockSpec((tm, w), lambda i: (i, 0))
    vec = pl.BlockSpec((1, w), lambda i: (0, 0))
    in_specs = [row, row, vec] + ([row] if res is not None else [])
    args = [dy, x, g] + ([res] if res is not None else [])
    out_specs = [row] * len(out_dtypes) + [vec]
    out_shape = [jax.ShapeDtypeStruct((s, w), dt) for dt in out_dtypes] + [jax.ShapeDtypeStruct((1, w), F32)]
    return pl.pallas_call(
        body, name=name, grid=(s // tm,), in_specs=in_specs, out_specs=out_specs,
        out_shape=out_shape, compiler_params=_params("arbitrary"),
    )(*args)


def _split3(v):
    p1 = v.astype(BF16)
    r1 = v - p1.astype(F32)
    p2 = r1.astype(BF16)
    p3 = (r1 - p2.astype(F32)).astype(BF16)
    return p1, p2, p3


def _tri_sum(v, reverse, tcol=512):
    h, s = v.shape
    tcol = min(tcol, s)
    parts = _split3(v)
    outs = []
    for j in range(s // tcol):
        src = lax.broadcasted_iota(jnp.int32, (s, tcol), 0)
        dst = lax.broadcasted_iota(jnp.int32, (s, tcol), 1) + j * tcol
        keep = (src >= dst) if reverse else (src <= dst)
        tri = jnp.where(keep, 1.0, 0.0).astype(BF16)
        acc = jnp.zeros((h, tcol), F32)
        for p in parts:
            acc = acc + jnp.dot(p, tri, preferred_element_type=F32)
        outs.append(acc)
    return outs


def _forget_fwd(fa_t, b):
    h, s = fa_t.shape
    tcol = min(512, s)

    def body(f_ref, b_ref, c_ref):
        z = f_ref[...] + b_ref[...]
        logf = jnp.minimum(z, 0.0) - jnp.log(1.0 + jnp.exp(-jnp.abs(z)))
        for j, blk in enumerate(_tri_sum(logf, reverse=False, tcol=tcol)):
            c_ref[:, j * tcol:(j + 1) * tcol] = blk

    return pl.pallas_call(
        body, name="forget_fwd", out_shape=jax.ShapeDtypeStruct((h, s), F32),
        compiler_params=_params(),
    )(fa_t, b)


def _forget_bwd(dacol, fa_t, b):
    h, s = fa_t.shape
    tcol = min(512, s)

    def body(d_ref, f_ref, b_ref, dfa_ref, db_ref):
        z = f_ref[...] + b_ref[...]
        dc = -d_ref[...]
        total = jnp.zeros((h, 1), F32)
        for j, blk in enumerate(_tri_sum(dc, reverse=True, tcol=tcol)):
            cols = slice(j * tcol, (j + 1) * tcol)
            dfa = blk * (1.0 - jax.nn.sigmoid(z[:, cols]))
            dfa_ref[:, cols] = dfa
            total = total + jnp.sum(dfa, axis=-1, keepdims=True)
        db_ref[...] = total

    return pl.pallas_call(
        body, name="forget_bwd",
        out_shape=[jax.ShapeDtypeStruct((h, s), F32), jax.ShapeDtypeStruct((h, 1), F32)],
        compiler_params=_params(),
    )(dacol, fa_t, b)


def _distance_bias(s, tile, dilated):
    nb = s // tile
    b = lax.broadcasted_iota(jnp.int32, (nb, tile, tile), 0)
    dist = b * tile + lax.broadcasted_iota(jnp.int32, (nb, tile, tile), 1) - lax.broadcasted_iota(jnp.int32, (nb, tile, tile), 2)
    if not dilated:
        return jnp.where(dist >= 0, 0.0, NEG).astype(F32)
    mult = jnp.zeros(dist.shape, jnp.int32)
    for window, dil in DIL_PATTERNS:
        mult = mult + ((dist >= 0) & (dist <= window) & ((dist & (dil - 1)) == 0)).astype(jnp.int32)
    logm = jnp.where(mult == 3, math.log(3.0), jnp.where(mult == 2, math.log(2.0), 0.0))
    return jnp.where(mult > 0, logm, NEG).astype(F32)


def _logits(q, k, arow, acol, bias):
    s = lax.dot_general(q, k, (((1,), (1,)), ((), ())), preferred_element_type=F32)
    return s * (1.0 / math.sqrt(HEAD_DIM)) + arow - acol + bias


def _attn_fwd(q, k, v, arow, acol, *, dilated, name, tq=ATTN_TILE, tk=ATTN_TILE):
    s, w = q.shape
    nh = w // HEAD_DIM
    assert tq == tk
    tq = tk = min(tq, s)
    nq, nk = s // tq, s // tk

    def body(q_ref, k_ref, v_ref, ar_ref, ac_ref, b_ref, o_ref, of_ref, lse_ref, m_ref, l_ref, acc_ref):
        qi, ki = pl.program_id(1), pl.program_id(2)

        @pl.when(ki == 0)
        def _():
            m_ref[...] = jnp.full_like(m_ref, NEG)
            l_ref[...] = jnp.zeros_like(l_ref)
            acc_ref[...] = jnp.zeros_like(acc_ref)

        @pl.when(ki <= qi)
        def _():
            sc = _logits(q_ref[...], k_ref[...], ar_ref[...], ac_ref[...], b_ref[...])
            m_new = jnp.maximum(m_ref[...], jnp.max(sc, axis=-1, keepdims=True))
            alpha = jnp.exp(m_ref[...] - m_new)
            p = jnp.exp(sc - m_new)
            l_ref[...] = alpha * l_ref[...] + jnp.sum(p, axis=-1, keepdims=True)
            p_hi = p.astype(BF16)
            p_lo = (p - p_hi.astype(F32)).astype(BF16)
            vv = v_ref[...]
            acc_ref[...] = (alpha * acc_ref[...] + jnp.dot(p_hi, vv, preferred_element_type=F32)
                            + jnp.dot(p_lo, vv, preferred_element_type=F32))
            m_ref[...] = m_new

        @pl.when(ki == nk - 1)
        def _():
            out = acc_ref[...] / l_ref[...]
            o_ref[...] = out.astype(o_ref.dtype)
            of_ref[...] = out
            lse_ref[...] = m_ref[...] + jnp.log(l_ref[...])

    kv = pl.BlockSpec((tk, HEAD_DIM), lambda h, i, j: (jnp.minimum(j, i), h))
    return pl.pallas_call(
        body, name=name, grid=(nh, nq, nk),
        in_specs=[pl.BlockSpec((tq, HEAD_DIM), lambda h, i, j: (i, h)), kv, kv,
                  pl.BlockSpec((None, tq, 1), lambda h, i, j: (h, i, 0)),
                  pl.BlockSpec((None, 1, tk), lambda h, i, j: (h, 0, jnp.minimum(j, i))),
                  pl.BlockSpec((None, tq, tk), lambda h, i, j: (jnp.maximum(i - j, 0), 0, 0))],
        out_specs=[pl.BlockSpec((tq, HEAD_DIM), lambda h, i, j: (i, h)),
                   pl.BlockSpec((tq, HEAD_DIM), lambda h, i, j: (i, h)),
                   pl.BlockSpec((None, tq, 1), lambda h, i, j: (h, i, 0))],
        out_shape=[jax.ShapeDtypeStruct((s, w), BF16), jax.ShapeDtypeStruct((s, w), F32),
                   jax.ShapeDtypeStruct((nh, s, 1), F32)],
        scratch_shapes=[pltpu.VMEM((tq, 1), F32), pltpu.VMEM((tq, 1), F32), pltpu.VMEM((tq, HEAD_DIM), F32)],
        compiler_params=_params("parallel", "parallel", "arbitrary"),
    )(q, k, v, arow, acol, _distance_bias(s, tq, dilated))


def _attn_bwd(q, k, v, o, do, lse, arow, acol, *, dilated, name, tq=ATTN_TILE, tk=ATTN_TILE):
    s, w = q.shape
    nh = w // HEAD_DIM
    assert tq == tk
    tq = tk = min(tq, s)
    nq, nk = s // tq, s // tk
    scale = 1.0 / math.sqrt(HEAD_DIM)

    def body(q_ref, k_ref, v_ref, o_ref, do_ref, lse_ref, ar_ref, ac_ref, b_ref,
             dq_ref, dk_ref, dv_ref, dac_ref, dk_acc, dv_acc, dac_acc):
        ki, qi = pl.program_id(1), pl.program_id(2)

        @pl.when((ki == 0) & (qi == 0))
        def _():
            dq_ref[...] = jnp.zeros_like(dq_ref)

        @pl.when(qi == 0)
        def _():
            dk_acc[...] = jnp.zeros_like(dk_acc)
            dv_acc[...] = jnp.zeros_like(dv_acc)
            dac_acc[...] = jnp.zeros_like(dac_acc)

        @pl.when(qi >= ki)
        def _():
            qv, kvv, dov = q_ref[...], k_ref[...], do_ref[...]
            sc = _logits(qv, kvv, ar_ref[...], ac_ref[...], b_ref[...])
            p = jnp.exp(sc - lse_ref[...])
            dp = lax.dot_general(dov, v_ref[...], (((1,), (1,)), ((), ())), preferred_element_type=F32)
            delta = jnp.sum(dov.astype(F32) * o_ref[...].astype(F32), axis=-1, keepdims=True)
            ds = p * (dp - delta)
            dsb = ds.astype(BF16)
            dv_acc[...] += lax.dot_general(p.astype(BF16), dov, (((0,), (0,)), ((), ())), preferred_element_type=F32)
            dk_acc[...] += lax.dot_general(dsb, qv, (((0,), (0,)), ((), ())), preferred_element_type=F32)
            rows = pl.ds(pl.multiple_of(qi * tq, tq), tq)
            dq_ref[rows, :] += jnp.dot(dsb, kvv, preferred_element_type=F32) * scale
            dac_acc[...] += jnp.sum(ds, axis=0, keepdims=True)

        @pl.when(qi == nq - 1)
        def _():
            dk_ref[...] = dk_acc[...] * scale
            dv_ref[...] = dv_acc[...]
            dac_ref[...] = dac_acc[...]

    qs = pl.BlockSpec((tq, HEAD_DIM), lambda h, j, i: (jnp.maximum(i, j), h))
    ks = pl.BlockSpec((tk, HEAD_DIM), lambda h, j, i: (j, h))
    rowv = pl.BlockSpec((None, tq, 1), lambda h, j, i: (h, jnp.maximum(i, j), 0))
    colv = pl.BlockSpec((None, 1, tk), lambda h, j, i: (h, 0, j))
    return pl.pallas_call(
        body, name=name, grid=(nh, nk, nq),
        in_specs=[qs, ks, ks, qs, qs, rowv, rowv, colv,
                  pl.BlockSpec((None, tq, tk), lambda h, j, i: (jnp.maximum(i - j, 0), 0, 0))],
        out_specs=[pl.BlockSpec((s, HEAD_DIM), lambda h, j, i: (0, h)), ks, ks, colv],
        out_shape=[jax.ShapeDtypeStruct((s, w), F32), jax.ShapeDtypeStruct((s, w), F32),
                   jax.ShapeDtypeStruct((s, w), F32), jax.ShapeDtypeStruct((nh, 1, s), F32)],
        scratch_shapes=[pltpu.VMEM((tk, HEAD_DIM), F32), pltpu.VMEM((tk, HEAD_DIM), F32), pltpu.VMEM((1, tk), F32)],
        compiler_params=_params("arbitrary", "arbitrary", "arbitrary"),
    )(q, k, v, o, do, lse, arow, acol, _distance_bias(s, tq, dilated))


def _gate_fwd(ga, gb, pa, pb, tm=256):
    s, d = ga.shape

    def body(ga_ref, gb_ref, pa_ref, pb_ref, o_ref):
        o_ref[...] = (jax.nn.sigmoid(ga_ref[...]) * pa_ref[...]
                      + jax.nn.sigmoid(gb_ref[...]) * pb_ref[...]).astype(o_ref.dtype)

    row = pl.BlockSpec((tm, d), lambda i: (i, 0))
    return pl.pallas_call(
        body, name="gate_fwd", grid=(s // tm,), in_specs=[row] * 4, out_specs=row,
        out_shape=jax.ShapeDtypeStruct((s, d), BF16), compiler_params=_params("parallel"),
    )(ga, gb, pa, pb)


def _gate_bwd(dm, ga, gb, pa, pb, tm=256):
    s, d = ga.shape

    def body(dm_ref, ga_ref, gb_ref, pa_ref, pb_ref, dpa_ref, dpb_ref, dga_ref, dgb_ref):
        dmv = dm_ref[...]
        for g_ref, p_ref, dp_ref, dg_ref in ((ga_ref, pa_ref, dpa_ref, dga_ref), (gb_ref, pb_ref, dpb_ref, dgb_ref)):
            sg = jax.nn.sigmoid(g_ref[...])
            dp_ref[...] = (dmv * sg).astype(BF16)
            dg_ref[...] = (dmv * p_ref[...] * (sg * (1.0 - sg))).astype(BF16)

    row = pl.BlockSpec((tm, d), lambda i: (i, 0))
    return pl.pallas_call(
        body, name="gate_bwd", grid=(s // tm,), in_specs=[row] * 5, out_specs=[row] * 4,
        out_shape=[jax.ShapeDtypeStruct((s, d), BF16)] * 4, compiler_params=_params("parallel"),
    )(dm, ga, gb, pa, pb)


def _shift_down(u, k):
    row = lax.broadcasted_iota(jnp.int32, u.shape, 0)
    return jnp.where(row >= k, pltpu.roll(u, k, 0), 0.0)


def _shift_up(u, k):
    n = u.shape[0]
    row = lax.broadcasted_iota(jnp.int32, u.shape, 0)
    return jnp.where(row < n - k, pltpu.roll(u, n - k, 0), 0.0)


def _conv3(u, wc, b):
    return wc[0:1, :] * _shift_down(u, 2) + wc[1:2, :] * _shift_down(u, 1) + wc[2:3, :] * u + b


def _conv_glu_fwd(u, wc, b, tn=256):
    s, f2 = u.shape
    f = f2 // 2
    nb = f // tn

    def body(ug_ref, uv_ref, wg_ref, wv_ref, bg_ref, bv_ref, o_ref):
        cg = _conv3(ug_ref[...], wg_ref[...], bg_ref[...])
        cv = _conv3(uv_ref[...], wv_ref[...], bv_ref[...])
        o_ref[...] = (cg * jax.nn.sigmoid(cg) * cv).astype(o_ref.dtype)

    def cols(rows, off):
        return pl.BlockSpec((rows, tn), lambda j: (0, j + off))

    return pl.pallas_call(
        body, name="conv_glu_fwd", grid=(nb,),
        in_specs=[cols(s, 0), cols(s, nb), cols(3, 0), cols(3, nb), cols(1, 0), cols(1, nb)],
        out_specs=cols(s, 0), out_shape=jax.ShapeDtypeStruct((s, f), BF16),
        compiler_params=_params("parallel"),
    )(u, u, wc, wc, b, b)


def _conv_glu_bwd(u, da, wc, b, tn=256):
    s, f2 = u.shape
    f = f2 // 2
    nb = f // tn

    def body(ug_ref, uv_ref, da_ref, wg_ref, wv_ref, bg_ref, bv_ref, dug_ref, duv_ref, sg_ref, sv_ref):
        ug, uv, wg, wv = ug_ref[...], uv_ref[...], wg_ref[...], wv_ref[...]
        cg = _conv3(ug, wg, bg_ref[...])
        cv = _conv3(uv, wv, bv_ref[...])
        sig = jax.nn.sigmoid(cg)
        dav = da_ref[...]
        dcv = dav * (cg * sig)
        dcg = dav * cv * (sig * (1.0 + cg * (1.0 - sig)))
        for dc, uu, w, du_ref, st_ref in ((dcg, ug, wg, dug_ref, sg_ref), (dcv, uv, wv, duv_ref, sv_ref)):
            du = w[2:3, :] * dc + w[1:2, :] * _shift_up(dc, 1) + w[0:1, :] * _shift_up(dc, 2)
            du_ref[...] = du.astype(BF16)
            st_ref[...] = jnp.zeros_like(st_ref)
            st_ref[0:1, :] = jnp.sum(dc * _shift_down(uu, 2), axis=0, keepdims=True)
            st_ref[1:2, :] = jnp.sum(dc * _shift_down(uu, 1), axis=0, keepdims=True)
            st_ref[2:3, :] = jnp.sum(dc * uu, axis=0, keepdims=True)
            st_ref[3:4, :] = jnp.sum(dc, axis=0, keepdims=True)

    def cols(rows, off):
        return pl.BlockSpec((rows, tn), lambda j: (0, j + off))

    return pl.pallas_call(
        body, name="conv_glu_bwd", grid=(nb,),
        in_specs=[cols(s, 0), cols(s, nb), cols(s, 0), cols(3, 0), cols(3, nb), cols(1, 0), cols(1, nb)],
        out_specs=[cols(s, 0), cols(s, 0), cols(8, 0), cols(8, 0)],
        out_shape=[jax.ShapeDtypeStruct((s, f), BF16), jax.ShapeDtypeStruct((s, f), BF16),
                   jax.ShapeDtypeStruct((8, f), F32), jax.ShapeDtypeStruct((8, f), F32)],
        compiler_params=_params("parallel"),
    )(u, u, da, wc, wc, b, b)


def _loss_head(y, target, tm=256):
    s, d = y.shape

    def body(y_ref, t_ref, dyf_ref, dyb_ref, l_ref):
        @pl.when(pl.program_id(0) == 0)
        def _():
            l_ref[...] = jnp.zeros_like(l_ref)

        err = y_ref[...] - t_ref[...]
        dy = err * (1.0 / d)
        dyf_ref[...] = dy
        dyb_ref[...] = dy.astype(BF16)
        l_ref[...] += 0.5 * jnp.sum(jnp.sum(err * err, axis=-1, keepdims=True) * (1.0 / d), axis=0, keepdims=True)

    row = pl.BlockSpec((tm, d), lambda i: (i, 0))
    return pl.pallas_call(
        body, name="loss_head", grid=(s // tm,), in_specs=[row, row],
        out_specs=[row, row, pl.BlockSpec((8, LANES), lambda i: (0, 0))],
        out_shape=[jax.ShapeDtypeStruct((s, d), F32), jax.ShapeDtypeStruct((s, d), BF16),
                   jax.ShapeDtypeStruct((8, LANES), F32)],
        compiler_params=_params("arbitrary"),
    )(y, target)


ROW_TILES = (256, 128, 64, 32, 16, 8)
BLOCK_BYTES = 1 << 20


def _add_halves(g, r1, place):
    ns, r, c = g.shape
    rh = r // 2
    tr = _pick(rh, ROW_TILES)
    g4 = g.reshape(ns, 2, rh, c)

    def body(p_ref, g_ref, r_ref, o_ref):
        o_ref[...] = (g_ref[...].astype(F32) + r_ref[...].astype(F32)).astype(o_ref.dtype)

    return pl.pallas_call(
        body, name="add_halves",
        grid_spec=pltpu.PrefetchScalarGridSpec(
            num_scalar_prefetch=1, grid=(ns, rh // tr),
            in_specs=[pl.BlockSpec((None, None, tr, c), lambda s, i, pr: (s, pr[1], i, 0)),
                      pl.BlockSpec((None, tr, c), lambda s, i, pr: (s, i, 0))],
            out_specs=pl.BlockSpec((None, tr, c), lambda s, i, pr: (s, i, 0))),
        out_shape=jax.ShapeDtypeStruct((ns, rh, c), BF16),
        compiler_params=_params("parallel", "parallel"),
    )(place, g4, r1)


def _sum_chips(g, r1, recv, place):
    ns, r, c = g.shape
    rh = r // 2
    tr = _pick(rh, ROW_TILES)
    g4 = g.reshape(ns, 2, rh, c)

    def body(p_ref, g_ref, r_ref, t0_ref, t1_ref, t2_ref, o_ref):
        own = g_ref[...].astype(F32) + r_ref[...].astype(F32)
        o_ref[...] = ((own + t0_ref[...].astype(F32)) + t1_ref[...].astype(F32)) + t2_ref[...].astype(F32)

    def peer(k):
        return pl.BlockSpec((None, tr, c), lambda i, pr: (k, i, 0))

    return pl.pallas_call(
        body, name="sum_chips",
        grid_spec=pltpu.PrefetchScalarGridSpec(
            num_scalar_prefetch=1, grid=(rh // tr,),
            in_specs=[pl.BlockSpec((None, None, tr, c), lambda i, pr: (pr[0], pr[1], i, 0)),
                      pl.BlockSpec((None, tr, c), lambda i, pr: (pr[0], i, 0)), peer(0), peer(1), peer(2)],
            out_specs=pl.BlockSpec((tr, c), lambda i, pr: (i, 0))),
        out_shape=jax.ShapeDtypeStruct((rh, c), F32),
        compiler_params=_params("parallel"),
    )(place, g4, r1, recv, recv, recv)


def _sum_devices(packs):
    n, r, c = packs.shape

    def body(p_ref, o_ref):
        acc = p_ref[0]
        for d in range(1, n):
            acc = acc + p_ref[d]
        o_ref[...] = acc

    return pl.pallas_call(
        body, name="sum_devices", out_shape=jax.ShapeDtypeStruct((r, c), F32), compiler_params=_params(),
    )(packs)


def _adamw_update(wv, gv, mv, vv):
    c1 = 1.0 - ADAM_B1 ** ADAM_STEP
    c2 = 1.0 - ADAM_B2 ** ADAM_STEP
    mn = ADAM_B1 * mv + (1.0 - ADAM_B1) * gv
    vn = ADAM_B2 * vv + (1.0 - ADAM_B2) * (gv * gv)
    m_hat = mn / c1
    v_hat = vn / c2
    return -ADAM_LR * (m_hat / (jnp.sqrt(v_hat) + ADAM_EPS) + ADAM_WD * wv), mn, vn


def _adamw(w, g, m, v, name, deps=()):
    r, c = w.shape
    tr = _pick(r, ROW_TILES) if r >= 8 else r

    def body(w_ref, g_ref, m_ref, v_ref, *rest):
        d_ref, mo_ref, vo_ref = rest[-3:]
        d_ref[...], mo_ref[...], vo_ref[...] = _adamw_update(w_ref[...], g_ref[...], m_ref[...], v_ref[...])

    blk = pl.BlockSpec((tr, c), lambda i: (i, 0))
    return pl.pallas_call(
        body, name=name, grid=(r // tr,), in_specs=[blk] * 4 + [ANY] * len(deps), out_specs=[blk] * 3,
        out_shape=[jax.ShapeDtypeStruct((r, c), F32)] * 3, compiler_params=_params("parallel"),
    )(w, g, m, v, *deps)


def _adamw_halves(w, mine, theirs, c_idx, m, v, name, deps=()):
    r, c = w.shape
    rh = r // 2
    tr = _pick(rh, [t for t in ROW_TILES if t * c * 4 <= BLOCK_BYTES])
    nb = rh // tr

    def body(c_ref, w_ref, a_ref, b_ref, m_ref, v_ref, *rest):
        g_ref, d_ref, mo_ref, vo_ref = rest[-4:]
        gv = jnp.where(pl.program_id(0) // nb == c_ref[0], a_ref[...], b_ref[...])
        g_ref[...] = gv
        d_ref[...], mo_ref[...], vo_ref[...] = _adamw_update(w_ref[...], gv, m_ref[...], v_ref[...])

    blk = pl.BlockSpec((tr, c), lambda i, cr: (i, 0))
    mine_spec = pl.BlockSpec((tr, c), lambda i, cr: (jnp.clip(i - cr[0] * nb, 0, nb - 1), 0))
    theirs_spec = pl.BlockSpec((tr, c), lambda i, cr: (jnp.clip(i - (1 - cr[0]) * nb, 0, nb - 1), 0))
    return pl.pallas_call(
        body, name=name,
        grid_spec=pltpu.PrefetchScalarGridSpec(
            num_scalar_prefetch=1, grid=(r // tr,),
            in_specs=[blk, mine_spec, theirs_spec, blk, blk] + [ANY] * len(deps), out_specs=[blk] * 4),
        out_shape=[jax.ShapeDtypeStruct((r, c), F32)] * 4, compiler_params=_params("arbitrary"),
    )(c_idx, w, mine, theirs, m, v, *deps)


ANY = pl.BlockSpec(memory_space=pl.ANY)


def _place():
    x, y, c = lax.axis_index("x"), lax.axis_index("y"), lax.axis_index("c")
    chips = [(1 - x, y), (x, 1 - y), (1 - x, 1 - y)]
    return x, y, c, chips


def _remote(src, dst, send_sem, recv_sem, to):
    return pltpu.make_async_remote_copy(src_ref=src, dst_ref=dst, send_sem=send_sem, recv_sem=recv_sem,
                                        device_id=to, device_id_type=MESH)


HBM = pl.BlockSpec(memory_space=pltpu.HBM)
SEM = pl.BlockSpec(memory_space=pltpu.SEMAPHORE)
EFFECT = pltpu.SideEffectType.DATAFLOW_SIDE_EFFECTING


def _in_hbm(a):
    return pltpu.with_memory_space_constraint(a, pltpu.HBM)


def _half(ref_rows, who):
    return pl.ds(who * (ref_rows // 2), ref_rows // 2)


def _gather_start(groups):
    items = [it for g in groups for it in g]
    n = len(items)
    sizes = [len(g) for g in groups]

    def body(*refs):
        srcs, lands = refs[:n], refs[n:2 * n]
        sems = refs[2 * n:2 * n + 2 * len(groups)]
        token = refs[-1]
        x, y, c, chips = _place()
        j = 2 * x + y
        at = 0
        for gi, g in enumerate(groups):
            send, recv = sems[2 * gi], sems[2 * gi + 1]
            for i, (shard, split) in enumerate(g):
                src, land = srcs[at], lands[at]
                at += 1
                rows = _half(shard.shape[0], c) if split else slice(None)
                for k, chip in enumerate(chips):
                    _remote(src.at[rows], land.at[j, rows], send.at[3 * i + k], recv.at[3 * i + k], (*chip, c)).start()
        token[...] = jnp.zeros_like(token)

    sem_shapes = []
    for sz in sizes:
        sem_shapes += [pltpu.SemaphoreType.DMA((3 * sz,)), pltpu.SemaphoreType.DMA((3 * sz,))]
    out_shape = (sem_shapes + [pltpu.HBM(sh.shape, sh.dtype) for sh, _ in items]
                 + [pltpu.HBM((N_CHIPS,) + sh.shape, sh.dtype) for sh, _ in items]
                 + [jax.ShapeDtypeStruct((8, LANES), F32)])
    ns = len(sem_shapes)
    outs = pl.pallas_call(
        body, name="gather_start", in_specs=[HBM] * (2 * n),
        out_specs=[SEM] * ns + [HBM] * (2 * n) + [pl.BlockSpec(memory_space=pltpu.VMEM)],
        out_shape=out_shape, input_output_aliases={i: ns + i for i in range(2 * n)},
        compiler_params=pltpu.CompilerParams(has_side_effects=EFFECT),
    )(*[_in_hbm(sh) for sh, _ in items], *[_in_hbm(lax.empty((N_CHIPS,) + sh.shape, sh.dtype)) for sh, _ in items])
    sems, shards, lands, token = outs[:ns], outs[ns:ns + n], outs[ns + n:ns + 2 * n], outs[-1]
    res, at = [], 0
    for gi, sz in enumerate(sizes):
        res.append((shards[at:at + sz], lands[at:at + sz], sems[2 * gi], sems[2 * gi + 1]))
        at += sz
    return res, token


def _gather_pass(group, started, after, name):
    shards, lands, send, recv = started
    n = len(group)
    split_ix = [i for i, (_, split) in enumerate(group) if split]

    def body(*refs):
        lnds, send1, recv1 = refs[n:2 * n], refs[2 * n], refs[2 * n + 1]
        outs = refs[2 * n + 2 + len(after):]
        send2, recv2, token = outs[2 * n], outs[2 * n + 1], outs[2 * n + 2]
        x, y, c, chips = _place()
        sib = (x, y, 1 - c)
        for i, (shard, split) in enumerate(group):
            rows = _half(shard.shape[0], c) if split else slice(None)
            for k, (cx, cy) in enumerate(chips):
                landed = lnds[i].at[2 * cx + cy, rows]
                cp = _remote(landed, landed, send1.at[3 * i + k], recv1.at[3 * i + k], sib)
                cp.wait_send()
                cp.wait_recv()
        for i2, i in enumerate(split_ix):
            rows = _half(group[i][0].shape[0], c)
            for k, (cx, cy) in enumerate(chips):
                landed = lnds[i].at[2 * cx + cy, rows]
                _remote(landed, landed, send2.at[3 * i2 + k], recv2.at[3 * i2 + k], sib).start()
        token[...] = jnp.zeros_like(token)

    n2 = len(split_ix)
    out_shape = ([pltpu.HBM(a.shape, a.dtype) for a in (*shards, *lands)]
                 + [pltpu.SemaphoreType.DMA((3 * n2,)), pltpu.SemaphoreType.DMA((3 * n2,)), jax.ShapeDtypeStruct((8, LANES), F32)])
    outs = pl.pallas_call(
        body, name=name, in_specs=[HBM] * (2 * n) + [SEM, SEM] + [ANY] * len(after),
        out_specs=[HBM] * (2 * n) + [SEM, SEM, pl.BlockSpec(memory_space=pltpu.VMEM)],
        out_shape=out_shape, input_output_aliases={i: i for i in range(2 * n)},
        compiler_params=pltpu.CompilerParams(has_side_effects=EFFECT),
    )(*shards, *lands, send, recv, *after)
    return outs[:n], (outs[n:2 * n], outs[2 * n], outs[2 * n + 1]), outs[2 * n + 2]


def _gather_wait(group, passed, after, name):
    lands, send2, recv2 = passed
    n = len(group)
    split_ix = [i for i, (_, split) in enumerate(group) if split]

    def body(*refs):
        lnds, s2, r2 = refs[:n], refs[n], refs[n + 1]
        x, y, c, chips = _place()
        sib = (x, y, 1 - c)
        for i2, i in enumerate(split_ix):
            rows = _half(group[i][0].shape[0], 1 - c)
            for k, (cx, cy) in enumerate(chips):
                landed = lnds[i].at[2 * cx + cy, rows]
                cp = _remote(landed, landed, s2.at[3 * i2 + k], r2.at[3 * i2 + k], sib)
                cp.wait_send()
                cp.wait_recv()

    return pl.pallas_call(
        body, name=name, in_specs=[HBM] * n + [SEM, SEM, ANY], out_specs=[HBM] * n,
        out_shape=[pltpu.HBM(a.shape, a.dtype) for a in lands], input_output_aliases={i: i for i in range(n)},
        compiler_params=pltpu.CompilerParams(has_side_effects=EFFECT),
    )(*lands, send2, recv2, after)


def _own_slab(land, shard):
    chip = 2 * lax.axis_index("x") + lax.axis_index("y")
    return lax.dynamic_update_slice(land, shard[None], (chip, 0, 0))


def _xfer_start(name, srcs, land_shapes, n_copies, copies, after):
    n = len(srcs)

    def body(*refs):
        src_refs, land_refs = refs[:n], refs[n:2 * n]
        send, recv, token = refs[2 * n + 1], refs[2 * n + 2], refs[-1]
        for cp in copies(src_refs, land_refs, send, recv):
            cp.start()
        token[...] = jnp.zeros_like(token)

    lands = [_in_hbm(lax.empty(shape, dtype)) for shape, dtype in land_shapes]
    out_shape = ([pltpu.SemaphoreType.DMA((n_copies,)), pltpu.SemaphoreType.DMA((n_copies,))]
                 + [pltpu.HBM(a.shape, a.dtype) for a in (*srcs, *lands)] + [jax.ShapeDtypeStruct((8, LANES), F32)])
    outs = pl.pallas_call(
        body, name=name, in_specs=[HBM] * (2 * n) + [ANY],
        out_specs=[SEM, SEM] + [HBM] * (2 * n) + [pl.BlockSpec(memory_space=pltpu.VMEM)],
        out_shape=out_shape, input_output_aliases={i: 2 + i for i in range(2 * n)},
        compiler_params=pltpu.CompilerParams(has_side_effects=EFFECT),
    )(*[_in_hbm(a) for a in srcs], *lands, after)
    return (outs[2:2 + n], outs[2 + n:2 + 2 * n], outs[0], outs[1]), outs[-1]


def _xfer_wait(name, started, copies, after):
    srcs, lands, send, recv = started
    n = len(srcs)

    def body(*refs):
        src_refs, land_refs, s_ref, r_ref = refs[:n], refs[n:2 * n], refs[2 * n], refs[2 * n + 1]
        for cp in copies(src_refs, land_refs, s_ref, r_ref):
            cp.wait_send()
            cp.wait_recv()

    outs = pl.pallas_call(
        body, name=name, in_specs=[HBM] * (2 * n) + [SEM, SEM, ANY], out_specs=[HBM] * (2 * n),
        out_shape=[pltpu.HBM(a.shape, a.dtype) for a in (*srcs, *lands)],
        input_output_aliases={i: i for i in range(2 * n)},
        compiler_params=pltpu.CompilerParams(has_side_effects=EFFECT),
    )(*srcs, *lands, send, recv, after)
    return outs[:n], outs[n:]


def _swap_copies(srcs, lands, send, recv):
    x, y, c, _ = _place()
    return [_remote(src.at[:, _half(src.shape[1], 1 - c)], land, send.at[i], recv.at[i], (x, y, 1 - c))
            for i, (src, land) in enumerate(zip(srcs, lands))]


def _scatter_copies(srcs, lands, send, recv):
    x, y, c, chips = _place()
    return [_remote(src.at[2 * cx + cy], land.at[k], send.at[3 * i + k], recv.at[3 * i + k], (cx, cy, c))
            for i, (src, land) in enumerate(zip(srcs, lands)) for k, (cx, cy) in enumerate(chips)]


def _join_copies(srcs, lands, send, recv):
    x, y, c, _ = _place()
    return [_remote(src, land, send.at[i], recv.at[i], (x, y, 1 - c)) for i, (src, land) in enumerate(zip(srcs, lands))]


def _corner(a):
    return a[(slice(0, 1),) * a.ndim]


class _Reducer:
    def __init__(self, place):
        self.place = place
        self.state = {}

    def swap(self, key, grads, after):
        shapes = [((g.shape[0], g.shape[1] // 2, g.shape[2]), g.dtype) for g in grads]
        self.state[key], token = _xfer_start("swap_start_" + key, grads, shapes, len(grads), _swap_copies, _corner(after))
        return token

    def to_chips(self, key, after):
        grads, from_sibling = _xfer_wait("swap_wait_" + key, self.state[key], _swap_copies, after)
        sums = [_add_halves(g, r, self.place) for g, r in zip(grads, from_sibling)]
        shapes = [((3,) + s.shape[1:], s.dtype) for s in sums]
        started, token = _xfer_start("scatter_start_" + key, sums, shapes, 3 * len(sums), _scatter_copies, _corner(sums[-1]))
        self.state[key] = (grads, from_sibling, started)
        return token

    def to_core(self, key, after):
        grads, from_sibling, started = self.state[key]
        _, from_chips = _xfer_wait("scatter_wait_" + key, started, _scatter_copies, after)
        halves = [_sum_chips(g, r, rc, self.place) for g, r, rc in zip(grads, from_sibling, from_chips)]
        shapes = [(h.shape, h.dtype) for h in halves]
        self.state[key], token = _xfer_start("join_start_" + key, halves, shapes, len(halves), _join_copies, _corner(halves[-1]))
        return token

    def finish(self, key, after):
        return _xfer_wait("join_wait_" + key, self.state.pop(key), _join_copies, after)


def _gather_packs(pack):
    def body(p_ref, o_ref, lsem, ssem, rsem):
        x, y, c, _ = _place()
        me = 4 * x + 2 * y + c
        local = pltpu.make_async_copy(p_ref, o_ref.at[me], lsem)
        local.start()
        cps = []
        for k in range(1, N_DEV):
            fx, fy, fc = (k >> 2) & 1, (k >> 1) & 1, k & 1
            to = (x ^ fx, y ^ fy, c ^ fc)
            cps.append(_remote(p_ref, o_ref.at[me], ssem.at[k - 1], rsem.at[k - 1], to))
        for cp in cps:
            cp.start()
        for k in range(1, N_DEV):
            fx, fy, fc = (k >> 2) & 1, (k >> 1) & 1, k & 1
            src = o_ref.at[4 * (x ^ fx) + 2 * (y ^ fy) + (c ^ fc)]
            _remote(src, src, ssem.at[k - 1], rsem.at[k - 1], (x, y, c)).wait_recv()
        for cp in cps:
            cp.wait_send()
        local.wait()

    return pl.pallas_call(
        body, name="gather_packs", in_specs=[ANY], out_specs=ANY,
        out_shape=jax.ShapeDtypeStruct((N_DEV,) + pack.shape, pack.dtype),
        scratch_shapes=[pltpu.SemaphoreType.DMA, pltpu.SemaphoreType.DMA((N_DEV - 1,)), pltpu.SemaphoreType.DMA((N_DEV - 1,))],
    )(pack)


LANE_TILES = (512, 896, 1408, 704, 384, 256, 128)


def _layer_grads(x, target, small, wg, rest_pass, rest_wait, red):
    s, d = x.shape
    f = wg["conv"].shape[1] // 2
    w_att = N_HEADS * HEAD_DIM
    in_splits = (w_att, w_att, w_att, N_HEADS, w_att, w_att, w_att, d, d)
    in_cols = sum(in_splits)
    cs = in_cols // N_CHIPS
    cp = wg["in"].shape[2]
    tm = min(s, 1024)
    t_in = _pick(cp, LANE_TILES)
    t_d = _pick(d, LANE_TILES)
    t_d2 = min(d, 1024)
    t_dq = _pick(d // N_CHIPS, LANE_TILES)
    t_w = _pick(w_att, LANE_TILES)
    t_up = _pick(2 * f // N_CHIPS, LANE_TILES)
    t_fq = _pick(f // N_CHIPS, LANE_TILES)
    offs = np.cumsum(in_splits)[:-1].tolist()

    h1 = _norm_fwd(x, small["g_attn"], group=d, name="rms1_fwd")
    proj_p = _mm(h1, wg["in"], mode="nn", b_kind="col", tm=tm, tn=t_in, tk=d, name="mm_in")
    qa, ka, va_b, fa, qb, kb, vb_b, ga, gb = _proj_split(
        proj_p, in_splits, cs, (F32, F32, BF16, F32, F32, F32, BF16, F32, F32))
    gains = {n: small[n].reshape(1, w_att) for n in ("g_q_fox", "g_k_fox", "g_q_dil", "g_k_dil")}
    qa_n = _norm_fwd(qa, gains["g_q_fox"], group=HEAD_DIM, name="qnorm_fox")
    ka_n = _norm_fwd(ka, gains["g_k_fox"], group=HEAD_DIM, name="knorm_fox")
    qb_n = _norm_fwd(qb, gains["g_q_dil"], group=HEAD_DIM, name="qnorm_dil")
    kb_n = _norm_fwd(kb, gains["g_k_dil"], group=HEAD_DIM, name="knorm_dil")
    fa_t = fa.T
    b_f = small["b_forget"].reshape(N_HEADS, 1)
    c_f = _forget_fwd(fa_t, b_f)
    slopes = jnp.asarray(2.0 ** (-8.0 * np.arange(1, N_HEADS + 1) / N_HEADS), dtype=F32)
    a_d = -(slopes[:, None] * jnp.arange(s, dtype=F32)[None, :])
    rows_f, cols_f = c_f[:, :, None], c_f[:, None, :]
    rows_d, cols_d = a_d[:, :, None], a_d[:, None, :]
    o_a, o_a32, lse_a = _attn_fwd(qa_n, ka_n, va_b, rows_f, cols_f, dilated=False, name="attn_fox_fwd")
    token = rest_pass(o_a)
    rows_d = rows_d + token[0, 0]
    o_b, o_b32, lse_b = _attn_fwd(qb_n, kb_n, vb_b, rows_d, cols_d, dilated=True, name="attn_dil_fwd")
    wg = dict(wg, **rest_wait(o_b))
    pa = _mm(o_a, wg["brf"], mode="nn", b_kind="col", tm=tm, tn=t_dq, tk=w_att, name="mm_brf")
    pb = _mm(o_b, wg["brd"], mode="nn", b_kind="col", tm=tm, tn=t_dq, tk=w_att, name="mm_brd")
    merged = _gate_fwd(ga, gb, pa, pb)
    x1 = _mm(merged, wg["out"], mode="nn", b_kind="row", res=x, tm=tm, tn=t_d, tk=t_dq, name="mm_out")
    h2 = _norm_fwd(x1, small["g_ffn"], group=d, name="rms2_fwd")
    u = _mm(h2, wg["up"], mode="nn", b_kind="col", tm=tm, tn=t_up, tk=d, name="mm_up")
    act = _conv_glu_fwd(u, wg["conv"], wg["bconv"])
    y = _mm(act, wg["down"], mode="nn", b_kind="row", res=x1, tm=tm, tn=t_d2, tk=t_fq, name="mm_down")
    dy_f, dy_b, loss_blk = _loss_head(y, target)

    d_act = _mm(dy_b, wg["down"], mode="nt", b_kind="row", tm=tm, tn=t_fq, tk=d, name="mm_down_dx")
    g_down = _mm(act, dy_b, mode="tn", out_dtype=BF16, out_kind="row", tm=t_fq, tn=t_d2, tk=s, name="mm_down_dw")
    tok = red.swap("down", [g_down], g_down)
    du_g, du_v, st_g, st_v = _conv_glu_bwd(u, d_act, wg["conv"] + tok[0, 0], wg["bconv"])
    tok = red.to_chips("down", du_g)
    du = jnp.concatenate([du_g, du_v], axis=1)
    g_up = _mm(h2, du, mode="tn", out_dtype=BF16, out_kind="col", tm=t_d2, tn=t_up, tk=s, name="mm_up_dw", deps=(tok,))
    tok = red.to_core("down", g_up)
    tok2 = red.swap("up", [g_up], g_up)
    dh2 = _mm(du, wg["up"], mode="nt", b_kind="col", tm=s, tn=t_d2, tk=t_up, name="mm_up_dx", deps=(tok, tok2))
    tok = red.to_chips("up", dh2)
    dx1_b, dx1_f, dg_ffn = _norm_bwd(dh2, x1, small["g_ffn"], group=d, res=dy_f, out_dtypes=(BF16, F32), name="rms2_bwd")
    d_merged = _mm(dx1_b, wg["out"], mode="nt", b_kind="row", tm=tm, tn=t_dq, tk=d, name="mm_out_dx", deps=(tok,))
    g_out = _mm(merged, dx1_b, mode="tn", out_dtype=BF16, out_kind="row", tm=t_dq, tn=t_d2, tk=s, name="mm_out_dw")
    dpa, dpb, dga, dgb = _gate_bwd(d_merged, ga, gb, pa, pb)
    do_a = _mm(dpa, wg["brf"], mode="nt", b_kind="col", out_dtype=BF16, tm=s, tn=w_att, tk=t_dq, name="mm_brf_dx")
    do_b = _mm(dpb, wg["brd"], mode="nt", b_kind="col", out_dtype=BF16, tm=s, tn=w_att, tk=t_dq, name="mm_brd_dx")
    g_brf = _mm(o_a, dpa, mode="tn", out_dtype=BF16, out_kind="col", tm=w_att, tn=t_dq, tk=s, name="mm_brf_dw")
    g_brd = _mm(o_b, dpb, mode="tn", out_dtype=BF16, out_kind="col", tm=w_att, tn=t_dq, tk=s, name="mm_brd_dw")
    tok = red.swap("mix", [g_out, g_brf, g_brd], g_brd)
    dqa_n, dka_n, dva, dac_a = _attn_bwd(qa_n, ka_n, va_b, o_a32, do_a, lse_a, rows_f + tok[0, 0], cols_f, dilated=False, name="attn_fox_bwd")
    tok = red.to_core("up", dqa_n)
    tok2 = red.to_chips("mix", dqa_n)
    dqb_n, dkb_n, dvb, _ = _attn_bwd(qb_n, kb_n, vb_b, o_b32, do_b, lse_b, rows_d + (tok[0, 0] + tok2[0, 0]), cols_d, dilated=True, name="attn_dil_bwd")
    tok = red.to_core("mix", dqb_n)
    dqa, dg_qf = _norm_bwd(dqa_n, qa, gains["g_q_fox"], group=HEAD_DIM, name="qnorm_fox_bwd")
    dka, dg_kf = _norm_bwd(dka_n, ka, gains["g_k_fox"], group=HEAD_DIM, name="knorm_fox_bwd")
    dqb, dg_qd = _norm_bwd(dqb_n, qb, gains["g_q_dil"], group=HEAD_DIM, name="qnorm_dil_bwd")
    dkb, dg_kd = _norm_bwd(dkb_n, kb, gains["g_k_dil"], group=HEAD_DIM, name="knorm_dil_bwd")
    dfa_t, db_f = _forget_bwd(dac_a[:, 0, :], fa_t, b_f)
    dproj_p = _dproj_merge([dqa, dka, dva, dfa_t.T, dqb, dkb, dvb, dga, dgb], in_splits, cs, cp)
    g_in = _mm(h1, dproj_p, mode="tn", out_dtype=BF16, out_kind="col", tm=t_d2, tn=t_in, tk=s, name="mm_in_dw", deps=(tok,))
    tok = red.swap("in", [g_in], g_in)
    dh1 = _mm(dproj_p, wg["in"], mode="nt", b_kind="col", tm=s, tn=t_d2, tk=t_in, name="mm_in_dx", deps=(tok,))
    tok = red.to_chips("in", dh1)
    grad_x, dg_attn = _norm_bwd(dh1, x, small["g_attn"], group=d, res=dx1_f, out_dtypes=(F32,), name="rms1_bwd")

    small_grads = {
        "g_attn": dg_attn, "b_forget": db_f.reshape(1, N_HEADS),
        "g_q_fox": dg_qf, "g_k_fox": dg_kf, "g_q_dil": dg_qd, "g_k_dil": dg_kd, "g_ffn": dg_ffn,
        "w_conv": jnp.concatenate([st_g[0:3], st_v[0:3]], axis=1),
        "b_conv": jnp.concatenate([st_g[3:4], st_v[3:4]], axis=1),
        "loss": loss_blk[0:1, 0:1],
    }
    return small_grads, grad_x, tok


SMALL_ORDER = ("g_attn", "b_forget", "g_q_fox", "g_k_fox", "g_q_dil", "g_k_dil", "g_ffn", "w_conv", "b_conv", "loss")
WEIGHT_ORDER = ("g_attn", "w_in", "b_forget", "g_q_fox", "g_k_fox", "g_q_dil", "g_k_dil", "w_br_fox", "w_br_dil",
                "w_out", "g_ffn", "w_up", "w_conv", "b_conv", "w_down")
BIG = {"w_in": "in", "w_br_fox": "brf", "w_br_dil": "brd", "w_out": "out", "w_up": "up", "w_down": "down"}


def kernel(x, g_attn, w_in, b_forget, g_q_fox, g_k_fox, g_q_dil, g_k_dil, w_br_fox, w_br_dil, w_out, g_ffn, w_up, w_conv, b_conv, w_down, loss_target, m_g_attn, m_w_in, m_b_forget, m_g_q_fox, m_g_k_fox, m_g_q_dil, m_g_k_dil, m_w_br_fox, m_w_br_dil, m_w_out, m_g_ffn, m_w_up, m_w_conv, m_b_conv, m_w_down, v_g_attn, v_w_in, v_b_forget, v_g_q_fox, v_g_k_fox, v_g_q_dil, v_g_k_dil, v_w_br_fox, v_w_br_dil, v_w_out, v_g_ffn, v_w_up, v_w_conv, v_b_conv, v_w_down):
    w = dict(g_attn=g_attn, w_in=w_in, b_forget=b_forget, g_q_fox=g_q_fox, g_k_fox=g_k_fox, g_q_dil=g_q_dil,
             g_k_dil=g_k_dil, w_br_fox=w_br_fox, w_br_dil=w_br_dil, w_out=w_out, g_ffn=g_ffn, w_up=w_up,
             w_conv=w_conv, b_conv=b_conv, w_down=w_down)
    m = dict(g_attn=m_g_attn, w_in=m_w_in, b_forget=m_b_forget, g_q_fox=m_g_q_fox, g_k_fox=m_g_k_fox,
             g_q_dil=m_g_q_dil, g_k_dil=m_g_k_dil, w_br_fox=m_w_br_fox, w_br_dil=m_w_br_dil, w_out=m_w_out,
             g_ffn=m_g_ffn, w_up=m_w_up, w_conv=m_w_conv, b_conv=m_b_conv, w_down=m_w_down)
    v = dict(g_attn=v_g_attn, w_in=v_w_in, b_forget=v_b_forget, g_q_fox=v_g_q_fox, g_k_fox=v_g_k_fox,
             g_q_dil=v_g_q_dil, g_k_dil=v_g_k_dil, w_br_fox=v_w_br_fox, w_br_dil=v_w_br_dil, w_out=v_w_out,
             g_ffn=v_g_ffn, w_up=v_w_up, w_conv=v_w_conv, b_conv=v_b_conv, w_down=v_w_down)
    xi, yi, ci = lax.axis_index("x"), lax.axis_index("y"), lax.axis_index("c")
    chip = (2 * xi + yi).astype(jnp.int32)
    c_idx = ci.astype(jnp.int32).reshape(1)
    j_idx = chip.reshape(1)

    cs = w_in.shape[2]
    cp = _round_up(cs, LANES)
    shards = {
        "in": jnp.pad(w_in[0].astype(BF16), ((0, 0), (0, cp - cs))),
        "brf": w_br_fox[0].astype(BF16), "brd": w_br_dil[0].astype(BF16), "out": w_out[0].astype(BF16),
        "up": w_up[0].astype(BF16), "down": w_down[0].astype(BF16),
    }
    names = tuple(shards)
    conv_pad = jnp.pad(w_conv[0], ((0, 8 - w_conv.shape[1]), (0, 0)))
    first = [(shards["in"], True), (conv_pad, False)]
    rest_names = names[1:]
    rest = [(shards[n], True) for n in rest_names]
    (started_first, started_rest), token = _gather_start([first, rest])
    w2, m2, v2 = ({n: a[n].reshape(a[n].shape[-2], a[n].shape[-1]) for n in BIG} for a in (w, m, v))
    one = 1.0 + token[0, 0]
    m2["w_in"], v2["w_in"] = m2["w_in"] * one, v2["w_in"] * one
    early = (token, m2["w_in"], v2["w_in"])
    own_first, passed_first, token = _gather_pass(first, started_first, early, "gather_pass_in")
    land_in, land_conv = _gather_wait(first, passed_first, token, "gather_wait_in")
    conv_all = _own_slab(land_conv, own_first[1])
    wg = {"in": _own_slab(land_in, own_first[0]), "bconv": b_conv,
          "conv": jnp.transpose(conv_all[:, :w_conv.shape[1], :], (1, 0, 2)).reshape(w_conv.shape[1], -1)}
    small = {n: w[n] for n in ("g_attn", "b_forget", "g_q_fox", "g_k_fox", "g_q_dil", "g_k_dil", "g_ffn")}
    small = {n: (a[0] if a.ndim == 3 else a) for n, a in small.items()}
    in_flight = {}

    def rest_pass(after):
        in_flight["own"], in_flight["passed"], tok = _gather_pass(rest, started_rest, (after,), "gather_pass_rest")
        return tok

    def rest_wait(after):
        lands = _gather_wait(rest, in_flight["passed"], after, "gather_wait_rest")
        return {n: _own_slab(land, own) for n, land, own in zip(rest_names, lands, in_flight["own"])}

    reducer = _Reducer(jnp.stack([chip, ci.astype(jnp.int32)]))
    small_grads, grad_x, tok_in = _layer_grads(x[0], loss_target[0], small, wg, rest_pass, rest_wait, reducer)

    mine, theirs = {}, {}
    for key, members in (("down", ("down",)), ("up", ("up",)), ("mix", ("out", "brf", "brd"))):
        mine_k, theirs_k = reducer.finish(key, grad_x)
        mine.update(zip(members, mine_k))
        theirs.update(zip(members, theirs_k))

    flat = jnp.concatenate([small_grads[n].reshape(-1) for n in SMALL_ORDER])
    rows = _round_up(flat.shape[0], 8 * LANES) // LANES
    pack = jnp.pad(flat, (0, rows * LANES - flat.shape[0])).reshape(rows, LANES)
    total = _sum_devices(_gather_packs(pack)).reshape(-1)
    red, at = {}, 0
    for n in SMALL_ORDER:
        size = small_grads[n].size
        red[n] = total[at:at + size].reshape(small_grads[n].shape)
        at += size
    loss = red["loss"].reshape(())
    c2 = w_conv.shape[2]
    red["w_conv"] = lax.dynamic_slice_in_dim(red["w_conv"], chip * c2, c2, axis=1)

    g_out, d_out, m_out, v_out = {}, {}, {}, {}
    last = [n for n in WEIGHT_ORDER if n != "w_in"] + ["w_in"]
    for n in last:
        shape = w[n].shape
        r2 = (shape[-2], shape[-1]) if n not in ("g_attn", "b_forget", "g_ffn", "b_conv") else (1, shape[-1])
        if n == "w_in":
            done = jnp.stack([v_out[k][(0,) * v_out[k].ndim] for k in last[:-1]])
            tok = reducer.to_core("in", done)
            (mine_in,), (theirs_in,) = reducer.finish("in", tok)
            mine["in"], theirs["in"] = mine_in[:, :cs], theirs_in[:, :cs]
        if n in BIG:
            g2, dl, mn, vn = _adamw_halves(w2[n], mine[BIG[n]], theirs[BIG[n]], c_idx, m2[n], v2[n],
                                           name="adamw_" + n, deps=(tok_in,))
        else:
            g2 = red[n].reshape(r2)
            dl, mn, vn = _adamw(w[n].reshape(r2), g2, m[n].reshape(r2), v[n].reshape(r2), name="adamw_" + n,
                                deps=(tok_in,))
        g_out[n], d_out[n], m_out[n], v_out[n] = (a.reshape(shape) for a in (g2, dl, mn, vn))

    return (loss, grad_x[None], *[g_out[n] for n in WEIGHT_ORDER], *[d_out[n] for n in WEIGHT_ORDER],
            *[m_out[n] for n in WEIGHT_ORDER], *[v_out[n] for n in WEIGHT_ORDER])
```

```python
import functools
import math

import jax
import jax.numpy as jnp
import numpy as np
from jax import lax
from jax.experimental import pallas as pl
from jax.experimental.pallas import tpu as pltpu

F32 = jnp.float32
BF16 = jnp.bfloat16
HEAD_DIM = 128
N_HEADS = 8
EPS = 1e-6
NEG = -1e30
N_CHIPS = 4
N_DEV = 8
LANES = 128
VMEM_LIMIT_BYTES = 56 * 1024 * 1024
DIL_PATTERNS = ((128, 1), (512, 4), (2048, 16))
ATTN_TILE = 512
ADAM_LR, ADAM_B1, ADAM_B2, ADAM_EPS, ADAM_WD, ADAM_STEP = 0.001, 0.9, 0.999, 1e-08, 0.01, 10
MESH = pl.DeviceIdType.MESH


def _params(*sem):
    return pltpu.CompilerParams(dimension_semantics=sem, vmem_limit_bytes=VMEM_LIMIT_BYTES)


def _round_up(n, m):
    return -(-n // m) * m


def _pick(dim, prefs):
    for p in prefs:
        if dim % p == 0:
            return p
    raise ValueError(f"no tile for {dim} in {prefs}")


def _logical_shape(arr, kind):
    if kind is None:
        return arr.shape
    s, r, c = arr.shape
    return (r, s * c) if kind == "col" else (s * r, c)


def _spec(shape, kind, br, bc, fi, fj):
    if kind is None:
        return pl.BlockSpec((br, bc), lambda *g: (fi(*g), fj(*g)))
    _, r, c = shape
    if kind == "col":
        nb = c // bc
        assert nb * bc == c, (shape, bc)
        return pl.BlockSpec((None, br, bc), lambda *g: (fj(*g) // nb, fi(*g), fj(*g) % nb))
    nb = r // br
    assert nb * br == r, (shape, br)
    return pl.BlockSpec((None, br, bc), lambda *g: (fi(*g) // nb, fi(*g) % nb, fj(*g)))


def _mm(a, b, *, mode, tm, tn, tk, name, a_kind=None, b_kind=None, out_kind=None,
        out_dtype=F32, res=None, deps=()):
    la, lb = _logical_shape(a, a_kind), _logical_shape(b, b_kind)
    if mode == "nn":
        (m, k), (k2, n) = la, lb
    elif mode == "nt":
        (m, k), (n, k2) = la, lb
    else:
        (k, m), (k2, n) = la, lb
    assert k == k2, (name, la, lb)
    assert m % tm == 0 and n % tn == 0 and k % tk == 0, (name, m, n, k, tm, tn, tk)
    nk = k // tk
    im = lambda i, j, l: i
    jn = lambda i, j, l: j
    lk = lambda i, j, l: l
    if mode == "tn":
        a_spec = _spec(a.shape, a_kind, tk, tm, lk, im)
        dims = (((0,), (0,)), ((), ()))
    else:
        a_spec = _spec(a.shape, a_kind, tm, tk, im, lk)
        dims = (((1,), (1,)), ((), ())) if mode == "nt" else (((1,), (0,)), ((), ()))
    if mode == "nt":
        b_spec = _spec(b.shape, b_kind, tn, tk, jn, lk)
    else:
        b_spec = _spec(b.shape, b_kind, tk, tn, lk, jn)
    if out_kind is None:
        oshape = (m, n)
    elif out_kind == "col":
        oshape = (N_CHIPS, m, n // N_CHIPS)
    else:
        oshape = (N_CHIPS, m // N_CHIPS, n)
    o_spec = _spec(oshape, out_kind, tm, tn, im, jn)
    in_specs = [a_spec, b_spec]
    args = [a, b]
    if res is not None:
        in_specs.append(pl.BlockSpec((tm, tn), lambda i, j, l: (i, j)))
        args.append(res)
    in_specs += [pl.BlockSpec(memory_space=pl.ANY)] * len(deps)
    args += list(deps)

    def finish(out, res_ref, o_ref):
        if res_ref is not None:
            out = out + res_ref[...]
        o_ref[...] = out.astype(o_ref.dtype)

    def body_whole_k(*refs):
        res_ref = refs[2] if res is not None else None
        finish(lax.dot_general(refs[0][...], refs[1][...], dims, preferred_element_type=F32), res_ref, refs[-1])

    def body(*refs):
        a_ref, b_ref = refs[0], refs[1]
        res_ref = refs[2] if res is not None else None
        o_ref, acc_ref = refs[-2], refs[-1]
        step = pl.program_id(2)

        @pl.when(step == 0)
        def _():
            acc_ref[...] = jnp.zeros_like(acc_ref)

        acc_ref[...] += lax.dot_general(a_ref[...], b_ref[...], dims, preferred_element_type=F32)

        @pl.when(step == nk - 1)
        def _():
            finish(acc_ref[...], res_ref, o_ref)

    return pl.pallas_call(
        body_whole_k if nk == 1 else body, name=name, grid=(m // tm, n // tn, nk),
        in_specs=in_specs, out_specs=o_spec,
        out_shape=jax.ShapeDtypeStruct(oshape, out_dtype),
        scratch_shapes=[] if nk == 1 else [pltpu.VMEM((tm, tn), F32)],
        compiler_params=_params("parallel", "parallel", "arbitrary"),
    )(*args)


def _pieces(splits, cs, cp):
    out, g0 = [], 0
    for width in splits:
        g1, runs = g0 + width, []
        for j in range(N_CHIPS):
            a, b = max(g0, cs * j), min(g1, cs * (j + 1))
            if a < b:
                runs.append((j * cp + a - cs * j, a - g0, b - a))
        out.append(runs)
        g0 = g1
    return out


def _proj_split(proj_p, splits, cs, dtypes, tm=128):
    s, wp = proj_p.shape
    pieces = _pieces(splits, cs, wp // N_CHIPS)

    def body(p_ref, *o_refs):
        for o_ref, runs in zip(o_refs, pieces):
            for src, dst, n in runs:
                o_ref[:, dst:dst + n] = p_ref[:, src:src + n].astype(o_ref.dtype)

    return pl.pallas_call(
        body, name="proj_split", grid=(s // tm,), in_specs=[pl.BlockSpec((tm, wp), lambda i: (i, 0))],
        out_specs=[pl.BlockSpec((tm, w), lambda i: (i, 0)) for w in splits],
        out_shape=[jax.ShapeDtypeStruct((s, w), dt) for w, dt in zip(splits, dtypes)],
        compiler_params=_params("parallel"),
    )(proj_p)


def _dproj_merge(parts, splits, cs, cp, tm=128):
    s = parts[0].shape[0]
    wp = N_CHIPS * cp
    pieces = _pieces(splits, cs, cp)

    def body(*refs):
        o_ref, stage = refs[-2], refs[-1]
        for j in range(N_CHIPS):
            stage[:, j * cp + cs:(j + 1) * cp] = jnp.zeros((tm, cp - cs), F32)
        for p_ref, runs in zip(refs, pieces):
            for dst, src, n in runs:
                stage[:, dst:dst + n] = p_ref[:, src:src + n].astype(F32)
        o_ref[...] = stage[...].astype(o_ref.dtype)

    return pl.pallas_call(
        body, name="dproj_merge", grid=(s // tm,),
        in_specs=[pl.BlockSpec((tm, w), lambda i: (i, 0)) for w in splits],
        out_specs=pl.BlockSpec((tm, wp), lambda i: (i, 0)),
        out_shape=jax.ShapeDtypeStruct((s, wp), BF16), scratch_shapes=[pltpu.VMEM((tm, wp), F32)],
        compiler_params=_params("parallel"),
    )(*parts)


def _norm_fwd(x, g, *, group, name, tm=256):
    s, w = x.shape
    ng = w // group

    def body(x_ref, g_ref, o_ref):
        for i in range(ng):
            cols = slice(i * group, (i + 1) * group)
            xv = x_ref[:, cols]
            r = lax.rsqrt(jnp.mean(xv * xv, axis=-1, keepdims=True) + EPS)
            o_ref[:, cols] = ((xv * r) * g_ref[:, cols]).astype(o_ref.dtype)

    return pl.pallas_call(
        body, name=name, grid=(s // tm,),
        in_specs=[pl.BlockSpec((tm, w), lambda i: (i, 0)), pl.BlockSpec((1, w), lambda i: (0, 0))],
        out_specs=pl.BlockSpec((tm, w), lambda i: (i, 0)),
        out_shape=jax.ShapeDtypeStruct((s, w), BF16),
        compiler_params=_params("parallel"),
    )(x, g)


def _norm_bwd(dy, x, g, *, group, name, res=None, out_dtypes=(BF16,), tm=256):
    s, w = x.shape
    ng = w // group
    n_in = 4 if res is not None else 3

    def body(*refs):
        dy_ref, x_ref, g_ref = refs[:3]
        res_ref = refs[3] if res is not None else None
        outs = refs[n_in:]
        dx_refs, dg_ref = outs[:-1], outs[-1]

        @pl.when(pl.program_id(0) == 0)
        def _():
            dg_ref[...] = jnp.zeros_like(dg_ref)

        for i in range(ng):
            cols = slice(i * group, (i + 1) * group)
            xv = x_ref[:, cols]
            dyv = dy_ref[:, cols].astype(F32)
            r = lax.rsqrt(jnp.mean(xv * xv, axis=-1, keepdims=True) + EPS)
            xr = xv * r
            dg_ref[:, cols] += jnp.sum(dyv * xr, axis=0, keepdims=True)
            gdy = dyv * g_ref[:, cols]
            dx = r * (gdy - xr * jnp.mean(gdy * xr, axis=-1, keepdims=True))
            if res_ref is not None:
                dx = dx + res_ref[:, cols]
            for dx_ref in dx_refs:
                dx_ref[:, cols] = dx.astype(dx_ref.dtype)

    row = pl.BlockSpec((tm, w), lambda i: (i, 0))
    vec = pl.BlockSpec((1, w), lambda i: (0, 0))
    in_specs = [row, row, vec] + ([row] if res is not None else [])
    args = [dy, x, g] + ([res] if res is not None else [])
    out_specs = [row] * len(out_dtypes) + [vec]
    out_shape = [jax.ShapeDtypeStruct((s, w), dt) for dt in out_dtypes] + [jax.ShapeDtypeStruct((1, w), F32)]
    return pl.pallas_call(
        body, name=name, grid=(s // tm,), in_specs=in_specs, out_specs=out_specs,
        out_shape=out_shape, compiler_params=_params("arbitrary"),
    )(*args)


def _split3(v):
    p1 = v.astype(BF16)
    r1 = v - p1.astype(F32)
    p2 = r1.astype(BF16)
    p3 = (r1 - p2.astype(F32)).astype(BF16)
    return p1, p2, p3


def _tri_sum(v, reverse, tcol=512):
    h, s = v.shape
    tcol = min(tcol, s)
    parts = _split3(v)
    outs = []
    for j in range(s // tcol):
        src = lax.broadcasted_iota(jnp.int32, (s, tcol), 0)
        dst = lax.broadcasted_iota(jnp.int32, (s, tcol), 1) + j * tcol
        keep = (src >= dst) if reverse else (src <= dst)
        tri = jnp.where(keep, 1.0, 0.0).astype(BF16)
        acc = jnp.zeros((h, tcol), F32)
        for p in parts:
            acc = acc + jnp.dot(p, tri, preferred_element_type=F32)
        outs.append(acc)
    return outs


def _forget_fwd(fa_t, b):
    h, s = fa_t.shape
    tcol = min(512, s)

    def body(f_ref, b_ref, c_ref):
        z = f_ref[...] + b_ref[...]
        logf = jnp.minimum(z, 0.0) - jnp.log(1.0 + jnp.exp(-jnp.abs(z)))
        for j, blk in enumerate(_tri_sum(logf, reverse=False, tcol=tcol)):
            c_ref[:, j * tcol:(j + 1) * tcol] = blk

    return pl.pallas_call(
        body, name="forget_fwd", out_shape=jax.ShapeDtypeStruct((h, s), F32),
        compiler_params=_params(),
    )(fa_t, b)


def _forget_bwd(dacol, fa_t, b):
    h, s = fa_t.shape
    tcol = min(512, s)

    def body(d_ref, f_ref, b_ref, dfa_ref, db_ref):
        z = f_ref[...] + b_ref[...]
        dc = -d_ref[...]
        total = jnp.zeros((h, 1), F32)
        for j, blk in enumerate(_tri_sum(dc, reverse=True, tcol=tcol)):
            cols = slice(j * tcol, (j + 1) * tcol)
            dfa = blk * (1.0 - jax.nn.sigmoid(z[:, cols]))
            dfa_ref[:, cols] = dfa
            total = total + jnp.sum(dfa, axis=-1, keepdims=True)
        db_ref[...] = total

    return pl.pallas_call(
        body, name="forget_bwd",
        out_shape=[jax.ShapeDtypeStruct((h, s), F32), jax.ShapeDtypeStruct((h, 1), F32)],
        compiler_params=_params(),
    )(dacol, fa_t, b)


def _distance_bias(s, tile, dilated):
    nb = s // tile
    b = lax.broadcasted_iota(jnp.int32, (nb, tile, tile), 0)
    dist = b * tile + lax.broadcasted_iota(jnp.int32, (nb, tile, tile), 1) - lax.broadcasted_iota(jnp.int32, (nb, tile, tile), 2)
    if not dilated:
        return jnp.where(dist >= 0, 0.0, NEG).astype(F32)
    mult = jnp.zeros(dist.shape, jnp.int32)
    for window, dil in DIL_PATTERNS:
        mult = mult + ((dist >= 0) & (dist <= window) & ((dist & (dil - 1)) == 0)).astype(jnp.int32)
    logm = jnp.where(mult == 3, math.log(3.0), jnp.where(mult == 2, math.log(2.0), 0.0))
    return jnp.where(mult > 0, logm, NEG).astype(F32)


def _logits(q, k, arow, acol, bias):
    s = lax.dot_general(q, k, (((1,), (1,)), ((), ())), preferred_element_type=F32)
    return s * (1.0 / math.sqrt(HEAD_DIM)) + arow - acol + bias


def _attn_fwd(q, k, v, arow, acol, *, dilated, name, tq=ATTN_TILE, tk=ATTN_TILE):
    two_term = not dilated
    s, w = q.shape
    nh = w // HEAD_DIM
    assert tq == tk
    tq = tk = min(tq, s)
    nq, nk = s // tq, s // tk

    def body(q_ref, k_ref, v_ref, ar_ref, ac_ref, b_ref, o_ref, of_ref, lse_ref, m_ref, l_ref, acc_ref):
        qi, ki = pl.program_id(1), pl.program_id(2)

        @pl.when(ki == 0)
        def _():
            m_ref[...] = jnp.full_like(m_ref, NEG)
            l_ref[...] = jnp.zeros_like(l_ref)
            acc_ref[...] = jnp.zeros_like(acc_ref)

        @pl.when(ki <= qi)
        def _():
            sc = _logits(q_ref[...], k_ref[...], ar_ref[...], ac_ref[...], b_ref[...])
            m_new = jnp.maximum(m_ref[...], jnp.max(sc, axis=-1, keepdims=True))
            alpha = jnp.exp(m_ref[...] - m_new)
            p = jnp.exp(sc - m_new)
            l_ref[...] = alpha * l_ref[...] + jnp.sum(p, axis=-1, keepdims=True)
            p_hi = p.astype(BF16)
            vv = v_ref[...]
            pv = jnp.dot(p_hi, vv, preferred_element_type=F32)
            if two_term:
                pv = pv + jnp.dot((p - p_hi.astype(F32)).astype(BF16), vv, preferred_element_type=F32)
            acc_ref[...] = alpha * acc_ref[...] + pv
            m_ref[...] = m_new

        @pl.when(ki == nk - 1)
        def _():
            out = acc_ref[...] / l_ref[...]
            o_ref[...] = out.astype(o_ref.dtype)
            of_ref[...] = out
            lse_ref[...] = m_ref[...] + jnp.log(l_ref[...])

    kv = pl.BlockSpec((tk, HEAD_DIM), lambda h, i, j: (jnp.minimum(j, i), h))
    return pl.pallas_call(
        body, name=name, grid=(nh, nq, nk),
        in_specs=[pl.BlockSpec((tq, HEAD_DIM), lambda h, i, j: (i, h)), kv, kv,
                  pl.BlockSpec((None, tq, 1), lambda h, i, j: (h, i, 0)),
                  pl.BlockSpec((None, 1, tk), lambda h, i, j: (h, 0, jnp.minimum(j, i))),
                  pl.BlockSpec((None, tq, tk), lambda h, i, j: (jnp.maximum(i - j, 0), 0, 0))],
        out_specs=[pl.BlockSpec((tq, HEAD_DIM), lambda h, i, j: (i, h)),
                   pl.BlockSpec((tq, HEAD_DIM), lambda h, i, j: (i, h)),
                   pl.BlockSpec((None, tq, 1), lambda h, i, j: (h, i, 0))],
        out_shape=[jax.ShapeDtypeStruct((s, w), BF16), jax.ShapeDtypeStruct((s, w), F32),
                   jax.ShapeDtypeStruct((nh, s, 1), F32)],
        scratch_shapes=[pltpu.VMEM((tq, 1), F32), pltpu.VMEM((tq, 1), F32), pltpu.VMEM((tq, HEAD_DIM), F32)],
        compiler_params=_params("parallel", "parallel", "arbitrary"),
    )(q, k, v, arow, acol, _distance_bias(s, tq, dilated))


def _attn_bwd(q, k, v, o, do, lse, arow, acol, *, dilated, name, tq=ATTN_TILE, tk=ATTN_TILE):
    s, w = q.shape
    nh = w // HEAD_DIM
    assert tq == tk
    tq = tk = min(tq, s)
    nq, nk = s // tq, s // tk
    scale = 1.0 / math.sqrt(HEAD_DIM)

    def body(q_ref, k_ref, v_ref, o_ref, do_ref, lse_ref, ar_ref, ac_ref, b_ref,
             dq_ref, dk_ref, dv_ref, dac_ref, dk_acc, dv_acc, dac_acc):
        ki, qi = pl.program_id(1), pl.program_id(2)

        @pl.when((ki == 0) & (qi == 0))
        def _():
            dq_ref[...] = jnp.zeros_like(dq_ref)

        @pl.when(qi == 0)
        def _():
            dk_acc[...] = jnp.zeros_like(dk_acc)
            dv_acc[...] = jnp.zeros_like(dv_acc)
            dac_acc[...] = jnp.zeros_like(dac_acc)

        @pl.when(qi >= ki)
        def _():
            qv, kvv, dov = q_ref[...], k_ref[...], do_ref[...]
            sc = _logits(qv, kvv, ar_ref[...], ac_ref[...], b_ref[...])
            p = jnp.exp(sc - lse_ref[...])
            dp = lax.dot_general(dov, v_ref[...], (((1,), (1,)), ((), ())), preferred_element_type=F32)
            delta = jnp.sum(dov.astype(F32) * o_ref[...].astype(F32), axis=-1, keepdims=True)
            ds = p * (dp - delta)
            dsb = ds.astype(BF16)
            dv_acc[...] += lax.dot_general(p.astype(BF16), dov, (((0,), (0,)), ((), ())), preferred_element_type=F32)
            dk_acc[...] += lax.dot_general(dsb, qv, (((0,), (0,)), ((), ())), preferred_element_type=F32)
            rows = pl.ds(pl.multiple_of(qi * tq, tq), tq)
            dq_ref[rows, :] += jnp.dot(dsb, kvv, preferred_element_type=F32) * scale
            dac_acc[...] += jnp.sum(ds, axis=0, keepdims=True)

        @pl.when(qi == nq - 1)
        def _():
            dk_ref[...] = dk_acc[...] * scale
            dv_ref[...] = dv_acc[...]
            dac_ref[...] = dac_acc[...]

    qs = pl.BlockSpec((tq, HEAD_DIM), lambda h, j, i: (jnp.maximum(i, j), h))
    ks = pl.BlockSpec((tk, HEAD_DIM), lambda h, j, i: (j, h))
    rowv = pl.BlockSpec((None, tq, 1), lambda h, j, i: (h, jnp.maximum(i, j), 0))
    colv = pl.BlockSpec((None, 1, tk), lambda h, j, i: (h, 0, j))
    return pl.pallas_call(
        body, name=name, grid=(nh, nk, nq),
        in_specs=[qs, ks, ks, qs, qs, rowv, rowv, colv,
                  pl.BlockSpec((None, tq, tk), lambda h, j, i: (jnp.maximum(i - j, 0), 0, 0))],
        out_specs=[pl.BlockSpec((s, HEAD_DIM), lambda h, j, i: (0, h)), ks, ks, colv],
        out_shape=[jax.ShapeDtypeStruct((s, w), F32), jax.ShapeDtypeStruct((s, w), F32),
                   jax.ShapeDtypeStruct((s, w), F32), jax.ShapeDtypeStruct((nh, 1, s), F32)],
        scratch_shapes=[pltpu.VMEM((tk, HEAD_DIM), F32), pltpu.VMEM((tk, HEAD_DIM), F32), pltpu.VMEM((1, tk), F32)],
        compiler_params=_params("arbitrary", "arbitrary", "arbitrary"),
    )(q, k, v, o, do, lse, arow, acol, _distance_bias(s, tq, dilated))


def _gate_fwd(ga, gb, pa, pb, tm=256):
    s, d = ga.shape

    def body(ga_ref, gb_ref, pa_ref, pb_ref, o_ref):
        o_ref[...] = (jax.nn.sigmoid(ga_ref[...]) * pa_ref[...]
                      + jax.nn.sigmoid(gb_ref[...]) * pb_ref[...]).astype(o_ref.dtype)

    row = pl.BlockSpec((tm, d), lambda i: (i, 0))
    return pl.pallas_call(
        body, name="gate_fwd", grid=(s // tm,), in_specs=[row] * 4, out_specs=row,
        out_shape=jax.ShapeDtypeStruct((s, d), BF16), compiler_params=_params("parallel"),
    )(ga, gb, pa, pb)


def _gate_bwd(dm, ga, gb, pa, pb, tm=256):
    s, d = ga.shape

    def body(dm_ref, ga_ref, gb_ref, pa_ref, pb_ref, dpa_ref, dpb_ref, dga_ref, dgb_ref):
        dmv = dm_ref[...]
        for g_ref, p_ref, dp_ref, dg_ref in ((ga_ref, pa_ref, dpa_ref, dga_ref), (gb_ref, pb_ref, dpb_ref, dgb_ref)):
            sg = jax.nn.sigmoid(g_ref[...])
            dp_ref[...] = (dmv * sg).astype(BF16)
            dg_ref[...] = (dmv * p_ref[...] * (sg * (1.0 - sg))).astype(BF16)

    row = pl.BlockSpec((tm, d), lambda i: (i, 0))
    return pl.pallas_call(
        body, name="gate_bwd", grid=(s // tm,), in_specs=[row] * 5, out_specs=[row] * 4,
        out_shape=[jax.ShapeDtypeStruct((s, d), BF16)] * 4, compiler_params=_params("parallel"),
    )(dm, ga, gb, pa, pb)


def _shift_down(u, k):
    row = lax.broadcasted_iota(jnp.int32, u.shape, 0)
    return jnp.where(row >= k, pltpu.roll(u, k, 0), 0.0)


def _shift_up(u, k):
    n = u.shape[0]
    row = lax.broadcasted_iota(jnp.int32, u.shape, 0)
    return jnp.where(row < n - k, pltpu.roll(u, n - k, 0), 0.0)


def _conv3(u, wc, b):
    return wc[0:1, :] * _shift_down(u, 2) + wc[1:2, :] * _shift_down(u, 1) + wc[2:3, :] * u + b


def _conv_glu_fwd(u, wc, b, tn=256):
    s, f2 = u.shape
    f = f2 // 2
    nb = f // tn

    def body(ug_ref, uv_ref, wg_ref, wv_ref, bg_ref, bv_ref, o_ref):
        cg = _conv3(ug_ref[...], wg_ref[...], bg_ref[...])
        cv = _conv3(uv_ref[...], wv_ref[...], bv_ref[...])
        o_ref[...] = (cg * jax.nn.sigmoid(cg) * cv).astype(o_ref.dtype)

    def cols(rows, off):
        return pl.BlockSpec((rows, tn), lambda j: (0, j + off))

    return pl.pallas_call(
        body, name="conv_glu_fwd", grid=(nb,),
        in_specs=[cols(s, 0), cols(s, nb), cols(3, 0), cols(3, nb), cols(1, 0), cols(1, nb)],
        out_specs=cols(s, 0), out_shape=jax.ShapeDtypeStruct((s, f), BF16),
        compiler_params=_params("parallel"),
    )(u, u, wc, wc, b, b)


def _conv_glu_bwd(u, da, wc, b, tn=256):
    s, f2 = u.shape
    f = f2 // 2
    nb = f // tn

    def body(ug_ref, uv_ref, da_ref, wg_ref, wv_ref, bg_ref, bv_ref, dug_ref, duv_ref, sg_ref, sv_ref):
        ug, uv, wg, wv = ug_ref[...], uv_ref[...], wg_ref[...], wv_ref[...]
        cg = _conv3(ug, wg, bg_ref[...])
        cv = _conv3(uv, wv, bv_ref[...])
        sig = jax.nn.sigmoid(cg)
        dav = da_ref[...]
        dcv = dav * (cg * sig)
        dcg = dav * cv * (sig * (1.0 + cg * (1.0 - sig)))
        for dc, uu, w, du_ref, st_ref in ((dcg, ug, wg, dug_ref, sg_ref), (dcv, uv, wv, duv_ref, sv_ref)):
            du = w[2:3, :] * dc + w[1:2, :] * _shift_up(dc, 1) + w[0:1, :] * _shift_up(dc, 2)
            du_ref[...] = du.astype(BF16)
            st_ref[...] = jnp.zeros_like(st_ref)
            st_ref[0:1, :] = jnp.sum(dc * _shift_down(uu, 2), axis=0, keepdims=True)
            st_ref[1:2, :] = jnp.sum(dc * _shift_down(uu, 1), axis=0, keepdims=True)
            st_ref[2:3, :] = jnp.sum(dc * uu, axis=0, keepdims=True)
            st_ref[3:4, :] = jnp.sum(dc, axis=0, keepdims=True)

    def cols(rows, off):
        return pl.BlockSpec((rows, tn), lambda j: (0, j + off))

    return pl.pallas_call(
        body, name="conv_glu_bwd", grid=(nb,),
        in_specs=[cols(s, 0), cols(s, nb), cols(s, 0), cols(3, 0), cols(3, nb), cols(1, 0), cols(1, nb)],
        out_specs=[cols(s, 0), cols(s, 0), cols(8, 0), cols(8, 0)],
        out_shape=[jax.ShapeDtypeStruct((s, f), BF16), jax.ShapeDtypeStruct((s, f), BF16),
                   jax.ShapeDtypeStruct((8, f), F32), jax.ShapeDtypeStruct((8, f), F32)],
        compiler_params=_params("parallel"),
    )(u, u, da, wc, wc, b, b)


def _loss_head(y, target, tm=256):
    s, d = y.shape

    def body(y_ref, t_ref, dyf_ref, dyb_ref, l_ref):
        @pl.when(pl.program_id(0) == 0)
        def _():
            l_ref[...] = jnp.zeros_like(l_ref)

        err = y_ref[...] - t_ref[...]
        dy = err * (1.0 / d)
        dyf_ref[...] = dy
        dyb_ref[...] = dy.astype(BF16)
        l_ref[...] += 0.5 * jnp.sum(jnp.sum(err * err, axis=-1, keepdims=True) * (1.0 / d), axis=0, keepdims=True)

    row = pl.BlockSpec((tm, d), lambda i: (i, 0))
    return pl.pallas_call(
        body, name="loss_head", grid=(s // tm,), in_specs=[row, row],
        out_specs=[row, row, pl.BlockSpec((8, LANES), lambda i: (0, 0))],
        out_shape=[jax.ShapeDtypeStruct((s, d), F32), jax.ShapeDtypeStruct((s, d), BF16),
                   jax.ShapeDtypeStruct((8, LANES), F32)],
        compiler_params=_params("arbitrary"),
    )(y, target)


ROW_TILES = (256, 128, 64, 32, 16, 8)
BLOCK_BYTES = 1 << 20


def _add_halves(g, r1, place):
    ns, r, c = g.shape
    rh = r // 2
    tr = _pick(rh, ROW_TILES)
    g4 = g.reshape(ns, 2, rh, c)

    def body(p_ref, g_ref, r_ref, o_ref):
        o_ref[...] = (g_ref[...].astype(F32) + r_ref[...].astype(F32)).astype(o_ref.dtype)

    return pl.pallas_call(
        body, name="add_halves",
        grid_spec=pltpu.PrefetchScalarGridSpec(
            num_scalar_prefetch=1, grid=(ns, rh // tr),
            in_specs=[pl.BlockSpec((None, None, tr, c), lambda s, i, pr: (s, pr[1], i, 0)),
                      pl.BlockSpec((None, tr, c), lambda s, i, pr: (s, i, 0))],
            out_specs=pl.BlockSpec((None, tr, c), lambda s, i, pr: (s, i, 0))),
        out_shape=jax.ShapeDtypeStruct((ns, rh, c), BF16),
        compiler_params=_params("parallel", "parallel"),
    )(place, g4, r1)


def _sum_chips(g, r1, recv, place):
    ns, r, c = g.shape
    rh = r // 2
    tr = _pick(rh, ROW_TILES)
    g4 = g.reshape(ns, 2, rh, c)

    def body(p_ref, g_ref, r_ref, t0_ref, t1_ref, t2_ref, o_ref):
        own = g_ref[...].astype(F32) + r_ref[...].astype(F32)
        o_ref[...] = ((own + t0_ref[...].astype(F32)) + t1_ref[...].astype(F32)) + t2_ref[...].astype(F32)

    def peer(k):
        return pl.BlockSpec((None, tr, c), lambda i, pr: (k, i, 0))

    return pl.pallas_call(
        body, name="sum_chips",
        grid_spec=pltpu.PrefetchScalarGridSpec(
            num_scalar_prefetch=1, grid=(rh // tr,),
            in_specs=[pl.BlockSpec((None, None, tr, c), lambda i, pr: (pr[0], pr[1], i, 0)),
                      pl.BlockSpec((None, tr, c), lambda i, pr: (pr[0], i, 0)), peer(0), peer(1), peer(2)],
            out_specs=pl.BlockSpec((tr, c), lambda i, pr: (i, 0))),
        out_shape=jax.ShapeDtypeStruct((rh, c), F32),
        compiler_params=_params("parallel"),
    )(place, g4, r1, recv, recv, recv)


def _sum_devices(packs):
    n, r, c = packs.shape

    def body(p_ref, o_ref):
        acc = p_ref[0]
        for d in range(1, n):
            acc = acc + p_ref[d]
        o_ref[...] = acc

    return pl.pallas_call(
        body, name="sum_devices", out_shape=jax.ShapeDtypeStruct((r, c), F32), compiler_params=_params(),
    )(packs)


def _adamw_update(wv, gv, mv, vv):
    c1 = 1.0 - ADAM_B1 ** ADAM_STEP
    c2 = 1.0 - ADAM_B2 ** ADAM_STEP
    mn = ADAM_B1 * mv + (1.0 - ADAM_B1) * gv
    vn = ADAM_B2 * vv + (1.0 - ADAM_B2) * (gv * gv)
    m_hat = mn / c1
    v_hat = vn / c2
    return -ADAM_LR * (m_hat / (jnp.sqrt(v_hat) + ADAM_EPS) + ADAM_WD * wv), mn, vn


def _adamw(w, g, m, v, name, deps=()):
    r, c = w.shape
    tr = _pick(r, ROW_TILES) if r >= 8 else r

    def body(w_ref, g_ref, m_ref, v_ref, *rest):
        d_ref, mo_ref, vo_ref = rest[-3:]
        d_ref[...], mo_ref[...], vo_ref[...] = _adamw_update(w_ref[...], g_ref[...], m_ref[...], v_ref[...])

    blk = pl.BlockSpec((tr, c), lambda i: (i, 0))
    return pl.pallas_call(
        body, name=name, grid=(r // tr,), in_specs=[blk] * 4 + [ANY] * len(deps), out_specs=[blk] * 3,
        out_shape=[jax.ShapeDtypeStruct((r, c), F32)] * 3, compiler_params=_params("parallel"),
    )(w, g, m, v, *deps)


def _adamw_halves(w, mine, theirs, c_idx, m, v, name, deps=()):
    r, c = w.shape
    rh = r // 2
    tr = _pick(rh, [t for t in ROW_TILES if t * c * 4 <= BLOCK_BYTES])
    nb = rh // tr

    def body(c_ref, w_ref, a_ref, b_ref, m_ref, v_ref, *rest):
        g_ref, d_ref, mo_ref, vo_ref = rest[-4:]
        gv = jnp.where(pl.program_id(0) // nb == c_ref[0], a_ref[...], b_ref[...])
        g_ref[...] = gv
        d_ref[...], mo_ref[...], vo_ref[...] = _adamw_update(w_ref[...], gv, m_ref[...], v_ref[...])

    blk = pl.BlockSpec((tr, c), lambda i, cr: (i, 0))
    mine_spec = pl.BlockSpec((tr, c), lambda i, cr: (jnp.clip(i - cr[0] * nb, 0, nb - 1), 0))
    theirs_spec = pl.BlockSpec((tr, c), lambda i, cr: (jnp.clip(i - (1 - cr[0]) * nb, 0, nb - 1), 0))
    return pl.pallas_call(
        body, name=name,
        grid_spec=pltpu.PrefetchScalarGridSpec(
            num_scalar_prefetch=1, grid=(r // tr,),
            in_specs=[blk, mine_spec, theirs_spec, blk, blk] + [ANY] * len(deps), out_specs=[blk] * 4),
        out_shape=[jax.ShapeDtypeStruct((r, c), F32)] * 4, compiler_params=_params("arbitrary"),
    )(c_idx, w, mine, theirs, m, v, *deps)


ANY = pl.BlockSpec(memory_space=pl.ANY)


def _place():
    x, y, c = lax.axis_index("x"), lax.axis_index("y"), lax.axis_index("c")
    chips = [(1 - x, y), (x, 1 - y), (1 - x, 1 - y)]
    return x, y, c, chips


def _remote(src, dst, send_sem, recv_sem, to):
    return pltpu.make_async_remote_copy(src_ref=src, dst_ref=dst, send_sem=send_sem, recv_sem=recv_sem,
                                        device_id=to, device_id_type=MESH)


HBM = pl.BlockSpec(memory_space=pltpu.HBM)
SEM = pl.BlockSpec(memory_space=pltpu.SEMAPHORE)
EFFECT = pltpu.SideEffectType.DATAFLOW_SIDE_EFFECTING


def _in_hbm(a):
    return pltpu.with_memory_space_constraint(a, pltpu.HBM)


def _half(ref_rows, who):
    return pl.ds(who * (ref_rows // 2), ref_rows // 2)


def _gather_start(groups):
    items = [it for g in groups for it in g]
    n = len(items)
    sizes = [len(g) for g in groups]

    def body(*refs):
        srcs, lands = refs[:n], refs[n:2 * n]
        sems = refs[2 * n:2 * n + 2 * len(groups)]
        token = refs[-1]
        x, y, c, chips = _place()
        j = 2 * x + y
        at = 0
        for gi, g in enumerate(groups):
            send, recv = sems[2 * gi], sems[2 * gi + 1]
            for i, (shard, split) in enumerate(g):
                src, land = srcs[at], lands[at]
                at += 1
                rows = _half(shard.shape[0], c) if split else slice(None)
                for k, chip in enumerate(chips):
                    _remote(src.at[rows], land.at[j, rows], send.at[3 * i + k], recv.at[3 * i + k], (*chip, c)).start()
        token[...] = jnp.zeros_like(token)

    sem_shapes = []
    for sz in sizes:
        sem_shapes += [pltpu.SemaphoreType.DMA((3 * sz,)), pltpu.SemaphoreType.DMA((3 * sz,))]
    out_shape = (sem_shapes + [pltpu.HBM(sh.shape, sh.dtype) for sh, _ in items]
                 + [pltpu.HBM((N_CHIPS,) + sh.shape, sh.dtype) for sh, _ in items]
                 + [jax.ShapeDtypeStruct((8, LANES), F32)])
    ns = len(sem_shapes)
    outs = pl.pallas_call(
        body, name="gather_start", in_specs=[HBM] * (2 * n),
        out_specs=[SEM] * ns + [HBM] * (2 * n) + [pl.BlockSpec(memory_space=pltpu.VMEM)],
        out_shape=out_shape, input_output_aliases={i: ns + i for i in range(2 * n)},
        compiler_params=pltpu.CompilerParams(has_side_effects=EFFECT),
    )(*[_in_hbm(sh) for sh, _ in items], *[_in_hbm(lax.empty((N_CHIPS,) + sh.shape, sh.dtype)) for sh, _ in items])
    sems, shards, lands, token = outs[:ns], outs[ns:ns + n], outs[ns + n:ns + 2 * n], outs[-1]
    res, at = [], 0
    for gi, sz in enumerate(sizes):
        res.append((shards[at:at + sz], lands[at:at + sz], sems[2 * gi], sems[2 * gi + 1]))
        at += sz
    return res, token


def _gather_pass(group, started, after, name):
    shards, lands, send, recv = started
    n = len(group)
    split_ix = [i for i, (_, split) in enumerate(group) if split]

    def body(*refs):
        lnds, send1, recv1 = refs[n:2 * n], refs[2 * n], refs[2 * n + 1]
        outs = refs[2 * n + 2 + len(after):]
        send2, recv2, token = outs[2 * n], outs[2 * n + 1], outs[2 * n + 2]
        x, y, c, chips = _place()
        sib = (x, y, 1 - c)
        for i, (shard, split) in enumerate(group):
            rows = _half(shard.shape[0], c) if split else slice(None)
            for k, (cx, cy) in enumerate(chips):
                landed = lnds[i].at[2 * cx + cy, rows]
                cp = _remote(landed, landed, send1.at[3 * i + k], recv1.at[3 * i + k], sib)
                cp.wait_send()
                cp.wait_recv()
        for i2, i in enumerate(split_ix):
            rows = _half(group[i][0].shape[0], c)
            for k, (cx, cy) in enumerate(chips):
                landed = lnds[i].at[2 * cx + cy, rows]
                _remote(landed, landed, send2.at[3 * i2 + k], recv2.at[3 * i2 + k], sib).start()
        token[...] = jnp.zeros_like(token)

    n2 = len(split_ix)
    out_shape = ([pltpu.HBM(a.shape, a.dtype) for a in (*shards, *lands)]
                 + [pltpu.SemaphoreType.DMA((3 * n2,)), pltpu.SemaphoreType.DMA((3 * n2,)), jax.ShapeDtypeStruct((8, LANES), F32)])
    outs = pl.pallas_call(
        body, name=name, in_specs=[HBM] * (2 * n) + [SEM, SEM] + [ANY] * len(after),
        out_specs=[HBM] * (2 * n) + [SEM, SEM, pl.BlockSpec(memory_space=pltpu.VMEM)],
        out_shape=out_shape, input_output_aliases={i: i for i in range(2 * n)},
        compiler_params=pltpu.CompilerParams(has_side_effects=EFFECT),
    )(*shards, *lands, send, recv, *after)
    return outs[:n], (outs[n:2 * n], outs[2 * n], outs[2 * n + 1]), outs[2 * n + 2]


def _gather_wait(group, passed, after, name):
    lands, send2, recv2 = passed
    n = len(group)
    split_ix = [i for i, (_, split) in enumerate(group) if split]

    def body(*refs):
        lnds, s2, r2 = refs[:n], refs[n], refs[n + 1]
        x, y, c, chips = _place()
        sib = (x, y, 1 - c)
        for i2, i in enumerate(split_ix):
            rows = _half(group[i][0].shape[0], 1 - c)
            for k, (cx, cy) in enumerate(chips):
                landed = lnds[i].at[2 * cx + cy, rows]
                cp = _remote(landed, landed, s2.at[3 * i2 + k], r2.at[3 * i2 + k], sib)
                cp.wait_send()
                cp.wait_recv()

    return pl.pallas_call(
        body, name=name, in_specs=[HBM] * n + [SEM, SEM, ANY], out_specs=[HBM] * n,
        out_shape=[pltpu.HBM(a.shape, a.dtype) for a in lands], input_output_aliases={i: i for i in range(n)},
        compiler_params=pltpu.CompilerParams(has_side_effects=EFFECT),
    )(*lands, send2, recv2, after)


def _own_slab(land, shard):
    chip = 2 * lax.axis_index("x") + lax.axis_index("y")
    return lax.dynamic_update_slice(land, shard[None], (chip, 0, 0))


def _xfer_start(name, srcs, land_shapes, n_copies, copies, after):
    n = len(srcs)

    def body(*refs):
        src_refs, land_refs = refs[:n], refs[n:2 * n]
        send, recv, token = refs[2 * n + 1], refs[2 * n + 2], refs[-1]
        for cp in copies(src_refs, land_refs, send, recv):
            cp.start()
        token[...] = jnp.zeros_like(token)

    lands = [_in_hbm(lax.empty(shape, dtype)) for shape, dtype in land_shapes]
    out_shape = ([pltpu.SemaphoreType.DMA((n_copies,)), pltpu.SemaphoreType.DMA((n_copies,))]
                 + [pltpu.HBM(a.shape, a.dtype) for a in (*srcs, *lands)] + [jax.ShapeDtypeStruct((8, LANES), F32)])
    outs = pl.pallas_call(
        body, name=name, in_specs=[HBM] * (2 * n) + [ANY],
        out_specs=[SEM, SEM] + [HBM] * (2 * n) + [pl.BlockSpec(memory_space=pltpu.VMEM)],
        out_shape=out_shape, input_output_aliases={i: 2 + i for i in range(2 * n)},
        compiler_params=pltpu.CompilerParams(has_side_effects=EFFECT),
    )(*[_in_hbm(a) for a in srcs], *lands, after)
    return (outs[2:2 + n], outs[2 + n:2 + 2 * n], outs[0], outs[1]), outs[-1]


def _xfer_wait(name, started, copies, after):
    srcs, lands, send, recv = started
    n = len(srcs)

    def body(*refs):
        src_refs, land_refs, s_ref, r_ref = refs[:n], refs[n:2 * n], refs[2 * n], refs[2 * n + 1]
        for cp in copies(src_refs, land_refs, s_ref, r_ref):
            cp.wait_send()
            cp.wait_recv()

    outs = pl.pallas_call(
        body, name=name, in_specs=[HBM] * (2 * n) + [SEM, SEM, ANY], out_specs=[HBM] * (2 * n),
        out_shape=[pltpu.HBM(a.shape, a.dtype) for a in (*srcs, *lands)],
        input_output_aliases={i: i for i in range(2 * n)},
        compiler_params=pltpu.CompilerParams(has_side_effects=EFFECT),
    )(*srcs, *lands, send, recv, after)
    return outs[:n], outs[n:]


def _swap_copies(srcs, lands, send, recv):
    x, y, c, _ = _place()
    return [_remote(src.at[:, _half(src.shape[1], 1 - c)], land, send.at[i], recv.at[i], (x, y, 1 - c))
            for i, (src, land) in enumerate(zip(srcs, lands))]


def _scatter_copies(srcs, lands, send, recv):
    x, y, c, chips = _place()
    return [_remote(src.at[2 * cx + cy], land.at[k], send.at[3 * i + k], recv.at[3 * i + k], (cx, cy, c))
            for i, (src, land) in enumerate(zip(srcs, lands)) for k, (cx, cy) in enumerate(chips)]


def _join_copies(srcs, lands, send, recv):
    x, y, c, _ = _place()
    return [_remote(src, land, send.at[i], recv.at[i], (x, y, 1 - c)) for i, (src, land) in enumerate(zip(srcs, lands))]


def _corner(a):
    return a[(slice(0, 1),) * a.ndim]


class _Reducer:
    def __init__(self, place):
        self.place = place
        self.state = {}

    def swap(self, key, grads, after):
        shapes = [((g.shape[0], g.shape[1] // 2, g.shape[2]), g.dtype) for g in grads]
        self.state[key], token = _xfer_start("swap_start_" + key, grads, shapes, len(grads), _swap_copies, _corner(after))
        return token

    def to_chips(self, key, after):
        grads, from_sibling = _xfer_wait("swap_wait_" + key, self.state[key], _swap_copies, after)
        sums = [_add_halves(g, r, self.place) for g, r in zip(grads, from_sibling)]
        shapes = [((3,) + s.shape[1:], s.dtype) for s in sums]
        started, token = _xfer_start("scatter_start_" + key, sums, shapes, 3 * len(sums), _scatter_copies, _corner(sums[-1]))
        self.state[key] = (grads, from_sibling, started)
        return token

    def to_core(self, key, after):
        grads, from_sibling, started = self.state[key]
        _, from_chips = _xfer_wait("scatter_wait_" + key, started, _scatter_copies, after)
        halves = [_sum_chips(g, r, rc, self.place) for g, r, rc in zip(grads, from_sibling, from_chips)]
        shapes = [(h.shape, h.dtype) for h in halves]
        self.state[key], token = _xfer_start("join_start_" + key, halves, shapes, len(halves), _join_copies, _corner(halves[-1]))
        return token

    def finish(self, key, after):
        return _xfer_wait("join_wait_" + key, self.state.pop(key), _join_copies, after)


def _gather_packs(pack):
    def body(p_ref, o_ref, lsem, ssem, rsem):
        x, y, c, _ = _place()
        me = 4 * x + 2 * y + c
        local = pltpu.make_async_copy(p_ref, o_ref.at[me], lsem)
        local.start()
        cps = []
        for k in range(1, N_DEV):
            fx, fy, fc = (k >> 2) & 1, (k >> 1) & 1, k & 1
            to = (x ^ fx, y ^ fy, c ^ fc)
            cps.append(_remote(p_ref, o_ref.at[me], ssem.at[k - 1], rsem.at[k - 1], to))
        for cp in cps:
            cp.start()
        for k in range(1, N_DEV):
            fx, fy, fc = (k >> 2) & 1, (k >> 1) & 1, k & 1
            src = o_ref.at[4 * (x ^ fx) + 2 * (y ^ fy) + (c ^ fc)]
            _remote(src, src, ssem.at[k - 1], rsem.at[k - 1], (x, y, c)).wait_recv()
        for cp in cps:
            cp.wait_send()
        local.wait()

    return pl.pallas_call(
        body, name="gather_packs", in_specs=[ANY], out_specs=ANY,
        out_shape=jax.ShapeDtypeStruct((N_DEV,) + pack.shape, pack.dtype),
        scratch_shapes=[pltpu.SemaphoreType.DMA, pltpu.SemaphoreType.DMA((N_DEV - 1,)), pltpu.SemaphoreType.DMA((N_DEV - 1,))],
    )(pack)


LANE_TILES = (512, 896, 1408, 704, 384, 256, 128)


def _layer_grads(x, target, small, wg, rest_pass, rest_wait, red):
    s, d = x.shape
    f = wg["conv"].shape[1] // 2
    w_att = N_HEADS * HEAD_DIM
    in_splits = (w_att, w_att, w_att, N_HEADS, w_att, w_att, w_att, d, d)
    in_cols = sum(in_splits)
    cs = in_cols // N_CHIPS
    cp = wg["in"].shape[2]
    tm = min(s, 1024)
    t_in = cp
    t_d = _pick(d, LANE_TILES)
    t_d2 = min(d, 1024)
    t_dq = _pick(d // N_CHIPS, LANE_TILES)
    t_w = _pick(w_att, LANE_TILES)
    t_up = 2 * f // N_CHIPS
    tm_wide = min(s, 512)
    t_fq = _pick(f // N_CHIPS, LANE_TILES)
    offs = np.cumsum(in_splits)[:-1].tolist()

    h1 = _norm_fwd(x, small["g_attn"], group=d, name="rms1_fwd")
    proj_p = _mm(h1, wg["in"], mode="nn", b_kind="col", tm=tm_wide, tn=t_in, tk=d, name="mm_in")
    qa, ka, va_b, fa, qb, kb, vb_b, ga, gb = _proj_split(
        proj_p, in_splits, cs, (F32, F32, BF16, F32, F32, F32, BF16, F32, F32))
    gains = {n: small[n].reshape(1, w_att) for n in ("g_q_fox", "g_k_fox", "g_q_dil", "g_k_dil")}
    qa_n = _norm_fwd(qa, gains["g_q_fox"], group=HEAD_DIM, name="qnorm_fox")
    ka_n = _norm_fwd(ka, gains["g_k_fox"], group=HEAD_DIM, name="knorm_fox")
    qb_n = _norm_fwd(qb, gains["g_q_dil"], group=HEAD_DIM, name="qnorm_dil")
    kb_n = _norm_fwd(kb, gains["g_k_dil"], group=HEAD_DIM, name="knorm_dil")
    fa_t = fa.T
    b_f = small["b_forget"].reshape(N_HEADS, 1)
    c_f = _forget_fwd(fa_t, b_f)
    slopes = jnp.asarray(2.0 ** (-8.0 * np.arange(1, N_HEADS + 1) / N_HEADS), dtype=F32)
    a_d = -(slopes[:, None] * jnp.arange(s, dtype=F32)[None, :])
    rows_f, cols_f = c_f[:, :, None], c_f[:, None, :]
    rows_d, cols_d = a_d[:, :, None], a_d[:, None, :]
    o_a, o_a32, lse_a = _attn_fwd(qa_n, ka_n, va_b, rows_f, cols_f, dilated=False, name="attn_fox_fwd")
    token = rest_pass(o_a)
    rows_d = rows_d + token[0, 0]
    o_b, o_b32, lse_b = _attn_fwd(qb_n, kb_n, vb_b, rows_d, cols_d, dilated=True, name="attn_dil_fwd")
    wg = dict(wg, **rest_wait(o_b))
    pa = _mm(o_a, wg["brf"], mode="nn", b_kind="col", tm=tm, tn=t_dq, tk=w_att, name="mm_brf")
    pb = _mm(o_b, wg["brd"], mode="nn", b_kind="col", tm=tm, tn=t_dq, tk=w_att, name="mm_brd")
    merged = _gate_fwd(ga, gb, pa, pb)
    x1 = _mm(merged, wg["out"], mode="nn", b_kind="row", res=x, tm=tm, tn=t_d, tk=t_dq, name="mm_out")
    h2 = _norm_fwd(x1, small["g_ffn"], group=d, name="rms2_fwd")
    u = _mm(h2, wg["up"], mode="nn", b_kind="col", tm=tm_wide, tn=t_up, tk=d, name="mm_up")
    act = _conv_glu_fwd(u, wg["conv"], wg["bconv"])
    y = _mm(act, wg["down"], mode="nn", b_kind="row", res=x1, tm=tm, tn=t_d2, tk=t_fq, name="mm_down")
    dy_f, dy_b, loss_blk = _loss_head(y, target)

    d_act = _mm(dy_b, wg["down"], mode="nt", b_kind="row", tm=tm, tn=t_fq, tk=d, name="mm_down_dx")
    g_down = _mm(act, dy_b, mode="tn", out_dtype=BF16, out_kind="row", tm=t_fq, tn=t_d2, tk=s, name="mm_down_dw")
    tok = red.swap("down", [g_down], g_down)
    du_g, du_v, st_g, st_v = _conv_glu_bwd(u, d_act, wg["conv"] + tok[0, 0], wg["bconv"])
    tok = red.to_chips("down", du_g)
    du = jnp.concatenate([du_g, du_v], axis=1)
    g_up = _mm(h2, du, mode="tn", out_dtype=BF16, out_kind="col", tm=t_d2, tn=t_up, tk=s, name="mm_up_dw", deps=(tok,))
    tok = red.to_core("down", g_up)
    tok2 = red.swap("up", [g_up], g_up)
    dh2 = _mm(du, wg["up"], mode="nt", b_kind="col", tm=tm, tn=t_d2, tk=t_up, name="mm_up_dx", deps=(tok, tok2))
    tok = red.to_chips("up", dh2)
    dx1_b, dx1_f, dg_ffn = _norm_bwd(dh2, x1, small["g_ffn"], group=d, res=dy_f, out_dtypes=(BF16, F32), name="rms2_bwd")
    d_merged = _mm(dx1_b, wg["out"], mode="nt", b_kind="row", tm=tm, tn=t_dq, tk=d, name="mm_out_dx", deps=(tok,))
    g_out = _mm(merged, dx1_b, mode="tn", out_dtype=BF16, out_kind="row", tm=t_dq, tn=t_d2, tk=s, name="mm_out_dw")
    dpa, dpb, dga, dgb = _gate_bwd(d_merged, ga, gb, pa, pb)
    do_a = _mm(dpa, wg["brf"], mode="nt", b_kind="col", out_dtype=BF16, tm=s, tn=w_att, tk=t_dq, name="mm_brf_dx")
    do_b = _mm(dpb, wg["brd"], mode="nt", b_kind="col", out_dtype=BF16, tm=s, tn=w_att, tk=t_dq, name="mm_brd_dx")
    g_brf = _mm(o_a, dpa, mode="tn", out_dtype=BF16, out_kind="col", tm=w_att, tn=t_dq, tk=s, name="mm_brf_dw")
    g_brd = _mm(o_b, dpb, mode="tn", out_dtype=BF16, out_kind="col", tm=w_att, tn=t_dq, tk=s, name="mm_brd_dw")
    tok = red.swap("mix", [g_out, g_brf, g_brd], g_brd)
    dqa_n, dka_n, dva, dac_a = _attn_bwd(qa_n, ka_n, va_b, o_a32, do_a, lse_a, rows_f + tok[0, 0], cols_f, dilated=False, name="attn_fox_bwd")
    tok = red.to_core("up", dqa_n)
    tok2 = red.to_chips("mix", dqa_n)
    dqb_n, dkb_n, dvb, _ = _attn_bwd(qb_n, kb_n, vb_b, o_b32, do_b, lse_b, rows_d + (tok[0, 0] + tok2[0, 0]), cols_d, dilated=True, name="attn_dil_bwd")
    tok = red.to_core("mix", dqb_n)
    dqa, dg_qf = _norm_bwd(dqa_n, qa, gains["g_q_fox"], group=HEAD_DIM, name="qnorm_fox_bwd")
    dka, dg_kf = _norm_bwd(dka_n, ka, gains["g_k_fox"], group=HEAD_DIM, name="knorm_fox_bwd")
    dqb, dg_qd = _norm_bwd(dqb_n, qb, gains["g_q_dil"], group=HEAD_DIM, name="qnorm_dil_bwd")
    dkb, dg_kd = _norm_bwd(dkb_n, kb, gains["g_k_dil"], group=HEAD_DIM, name="knorm_dil_bwd")
    dfa_t, db_f = _forget_bwd(dac_a[:, 0, :], fa_t, b_f)
    dproj_p = _dproj_merge([dqa, dka, dva, dfa_t.T, dqb, dkb, dvb, dga, dgb], in_splits, cs, cp)
    g_in = _mm(h1, dproj_p, mode="tn", out_dtype=BF16, out_kind="col", tm=t_d2, tn=t_in, tk=s, name="mm_in_dw", deps=(tok,))
    tok = red.swap("in", [g_in], g_in)
    dh1 = _mm(dproj_p, wg["in"], mode="nt", b_kind="col", tm=tm, tn=t_d2, tk=t_in, name="mm_in_dx", deps=(tok,))
    tok = red.to_chips("in", dh1)
    grad_x, dg_attn = _norm_bwd(dh1, x, small["g_attn"], group=d, res=dx1_f, out_dtypes=(F32,), name="rms1_bwd")

    small_grads = {
        "g_attn": dg_attn, "b_forget": db_f.reshape(1, N_HEADS),
        "g_q_fox": dg_qf, "g_k_fox": dg_kf, "g_q_dil": dg_qd, "g_k_dil": dg_kd, "g_ffn": dg_ffn,
        "w_conv": jnp.concatenate([st_g[0:3], st_v[0:3]], axis=1),
        "b_conv": jnp.concatenate([st_g[3:4], st_v[3:4]], axis=1),
        "loss": loss_blk[0:1, 0:1],
    }
    return small_grads, grad_x, tok


SMALL_ORDER = ("g_attn", "b_forget", "g_q_fox", "g_k_fox", "g_q_dil", "g_k_dil", "g_ffn", "w_conv", "b_conv", "loss")
WEIGHT_ORDER = ("g_attn", "w_in", "b_forget", "g_q_fox", "g_k_fox", "g_q_dil", "g_k_dil", "w_br_fox", "w_br_dil",
                "w_out", "g_ffn", "w_up", "w_conv", "b_conv", "w_down")
BIG = {"w_in": "in", "w_br_fox": "brf", "w_br_dil": "brd", "w_out": "out", "w_up": "up", "w_down": "down"}


def kernel(x, g_attn, w_in, b_forget, g_q_fox, g_k_fox, g_q_dil, g_k_dil, w_br_fox, w_br_dil, w_out, g_ffn, w_up, w_conv, b_conv, w_down, loss_target, m_g_attn, m_w_in, m_b_forget, m_g_q_fox, m_g_k_fox, m_g_q_dil, m_g_k_dil, m_w_br_fox, m_w_br_dil, m_w_out, m_g_ffn, m_w_up, m_w_conv, m_b_conv, m_w_down, v_g_attn, v_w_in, v_b_forget, v_g_q_fox, v_g_k_fox, v_g_q_dil, v_g_k_dil, v_w_br_fox, v_w_br_dil, v_w_out, v_g_ffn, v_w_up, v_w_conv, v_b_conv, v_w_down):
    w = dict(g_attn=g_attn, w_in=w_in, b_forget=b_forget, g_q_fox=g_q_fox, g_k_fox=g_k_fox, g_q_dil=g_q_dil,
             g_k_dil=g_k_dil, w_br_fox=w_br_fox, w_br_dil=w_br_dil, w_out=w_out, g_ffn=g_ffn, w_up=w_up,
             w_conv=w_conv, b_conv=b_conv, w_down=w_down)
    m = dict(g_attn=m_g_attn, w_in=m_w_in, b_forget=m_b_forget, g_q_fox=m_g_q_fox, g_k_fox=m_g_k_fox,
             g_q_dil=m_g_q_dil, g_k_dil=m_g_k_dil, w_br_fox=m_w_br_fox, w_br_dil=m_w_br_dil, w_out=m_w_out,
             g_ffn=m_g_ffn, w_up=m_w_up, w_conv=m_w_conv, b_conv=m_b_conv, w_down=m_w_down)
    v = dict(g_attn=v_g_attn, w_in=v_w_in, b_forget=v_b_forget, g_q_fox=v_g_q_fox, g_k_fox=v_g_k_fox,
             g_q_dil=v_g_q_dil, g_k_dil=v_g_k_dil, w_br_fox=v_w_br_fox, w_br_dil=v_w_br_dil, w_out=v_w_out,
             g_ffn=v_g_ffn, w_up=v_w_up, w_conv=v_w_conv, b_conv=v_b_conv, w_down=v_w_down)
    xi, yi, ci = lax.axis_index("x"), lax.axis_index("y"), lax.axis_index("c")
    chip = (2 * xi + yi).astype(jnp.int32)
    c_idx = ci.astype(jnp.int32).reshape(1)
    j_idx = chip.reshape(1)

    cs = w_in.shape[2]
    cp = _round_up(cs, LANES)
    shards = {
        "in": jnp.pad(w_in[0].astype(BF16), ((0, 0), (0, cp - cs))),
        "brf": w_br_fox[0].astype(BF16), "brd": w_br_dil[0].astype(BF16), "out": w_out[0].astype(BF16),
        "up": w_up[0].astype(BF16), "down": w_down[0].astype(BF16),
    }
    names = tuple(shards)
    conv_pad = jnp.pad(w_conv[0], ((0, 8 - w_conv.shape[1]), (0, 0)))
    first = [(shards["in"], True), (conv_pad, False)]
    rest_names = names[1:]
    rest = [(shards[n], True) for n in rest_names]
    (started_first, started_rest), token = _gather_start([first, rest])
    w2, m2, v2 = ({n: a[n].reshape(a[n].shape[-2], a[n].shape[-1]) for n in BIG} for a in (w, m, v))
    one = 1.0 + token[0, 0]
    m2["w_in"], v2["w_in"] = m2["w_in"] * one, v2["w_in"] * one
    early = (token, m2["w_in"], v2["w_in"])
    own_first, passed_first, token = _gather_pass(first, started_first, early, "gather_pass_in")
    land_in, land_conv = _gather_wait(first, passed_first, token, "gather_wait_in")
    conv_all = _own_slab(land_conv, own_first[1])
    wg = {"in": _own_slab(land_in, own_first[0]), "bconv": b_conv,
          "conv": jnp.transpose(conv_all[:, :w_conv.shape[1], :], (1, 0, 2)).reshape(w_conv.shape[1], -1)}
    small = {n: w[n] for n in ("g_attn", "b_forget", "g_q_fox", "g_k_fox", "g_q_dil", "g_k_dil", "g_ffn")}
    small = {n: (a[0] if a.ndim == 3 else a) for n, a in small.items()}
    in_flight = {}

    def rest_pass(after):
        in_flight["own"], in_flight["passed"], tok = _gather_pass(rest, started_rest, (after,), "gather_pass_rest")
        return tok

    def rest_wait(after):
        lands = _gather_wait(rest, in_flight["passed"], after, "gather_wait_rest")
        return {n: _own_slab(land, own) for n, land, own in zip(rest_names, lands, in_flight["own"])}

    reducer = _Reducer(jnp.stack([chip, ci.astype(jnp.int32)]))
    small_grads, grad_x, tok_in = _layer_grads(x[0], loss_target[0], small, wg, rest_pass, rest_wait, reducer)

    mine, theirs = {}, {}
    for key, members in (("down", ("down",)), ("up", ("up",)), ("mix", ("out", "brf", "brd"))):
        mine_k, theirs_k = reducer.finish(key, grad_x)
        mine.update(zip(members, mine_k))
        theirs.update(zip(members, theirs_k))

    flat = jnp.concatenate([small_grads[n].reshape(-1) for n in SMALL_ORDER])
    rows = _round_up(flat.shape[0], 8 * LANES) // LANES
    pack = jnp.pad(flat, (0, rows * LANES - flat.shape[0])).reshape(rows, LANES)
    total = _sum_devices(_gather_packs(pack)).reshape(-1)
    red, at = {}, 0
    for n in SMALL_ORDER:
        size = small_grads[n].size
        red[n] = total[at:at + size].reshape(small_grads[n].shape)
        at += size
    loss = red["loss"].reshape(())
    c2 = w_conv.shape[2]
    red["w_conv"] = lax.dynamic_slice_in_dim(red["w_conv"], chip * c2, c2, axis=1)

    g_out, d_out, m_out, v_out = {}, {}, {}, {}
    last = [n for n in WEIGHT_ORDER if n != "w_in"] + ["w_in"]
    for n in last:
        shape = w[n].shape
        r2 = (shape[-2], shape[-1]) if n not in ("g_attn", "b_forget", "g_ffn", "b_conv") else (1, shape[-1])
        if n == "w_in":
            done = jnp.stack([v_out[k][(0,) * v_out[k].ndim] for k in last[:-1]])
            tok = reducer.to_core("in", done)
            (mine_in,), (theirs_in,) = reducer.finish("in", tok)
            mine["in"], theirs["in"] = mine_in[:, :cs], theirs_in[:, :cs]
        if n in BIG:
            g2, dl, mn, vn = _adamw_halves(w2[n], mine[BIG[n]], theirs[BIG[n]], c_idx, m2[n], v2[n],
                                           name="adamw_" + n, deps=(tok_in,))
        else:
            g2 = red[n].reshape(r2)
            dl, mn, vn = _adamw(w[n].reshape(r2), g2, m[n].reshape(r2), v[n].reshape(r2), name="adamw_" + n,
                                deps=(tok_in,))
        g_out[n], d_out[n], m_out[n], v_out[n] = (a.reshape(shape) for a in (g2, dl, mn, vn))

    return (loss, grad_x[None], *[g_out[n] for n in WEIGHT_ORDER], *[d_out[n] for n in WEIGHT_ORDER],
            *[m_out[n] for n in WEIGHT_ORDER], *[v_out[n] for n in WEIGHT_ORDER])
```

```python
import functools
import math

import jax
import jax.numpy as jnp
import numpy as np
from jax import lax
from jax.experimental import pallas as pl
from jax.experimental.pallas import tpu as pltpu

F32 = jnp.float32
BF16 = jnp.bfloat16
HEAD_DIM = 128
N_HEADS = 8
EPS = 1e-6
NEG = -1e30
N_CHIPS = 4
N_DEV = 8
LANES = 128
VMEM_LIMIT_BYTES = 56 * 1024 * 1024
DIL_PATTERNS = ((128, 1), (512, 4), (2048, 16))
ATTN_TILE = 512
ADAM_LR, ADAM_B1, ADAM_B2, ADAM_EPS, ADAM_WD, ADAM_STEP = 0.001, 0.9, 0.999, 1e-08, 0.01, 10
MESH = pl.DeviceIdType.MESH


def _params(*sem):
    return pltpu.CompilerParams(dimension_semantics=sem, vmem_limit_bytes=VMEM_LIMIT_BYTES)


def _round_up(n, m):
    return -(-n // m) * m


def _pick(dim, prefs):
    for p in prefs:
        if dim % p == 0:
            return p
    raise ValueError(f"no tile for {dim} in {prefs}")


def _logical_shape(arr, kind):
    if kind is None:
        return arr.shape
    s, r, c = arr.shape
    return (r, s * c) if kind == "col" else (s * r, c)


def _spec(shape, kind, br, bc, fi, fj):
    if kind is None:
        return pl.BlockSpec((br, bc), lambda *g: (fi(*g), fj(*g)))
    _, r, c = shape
    if kind == "col":
        nb = c // bc
        assert nb * bc == c, (shape, bc)
        return pl.BlockSpec((None, br, bc), lambda *g: (fj(*g) // nb, fi(*g), fj(*g) % nb))
    nb = r // br
    assert nb * br == r, (shape, br)
    return pl.BlockSpec((None, br, bc), lambda *g: (fi(*g) // nb, fi(*g) % nb, fj(*g)))


def _mm(a, b, *, mode, tm, tn, tk, name, a_kind=None, b_kind=None, out_kind=None,
        out_dtype=F32, res=None, deps=()):
    la, lb = _logical_shape(a, a_kind), _logical_shape(b, b_kind)
    if mode == "nn":
        (m, k), (k2, n) = la, lb
    elif mode == "nt":
        (m, k), (n, k2) = la, lb
    else:
        (k, m), (k2, n) = la, lb
    assert k == k2, (name, la, lb)
    assert m % tm == 0 and n % tn == 0 and k % tk == 0, (name, m, n, k, tm, tn, tk)
    nk = k // tk
    im = lambda i, j, l: i
    jn = lambda i, j, l: j
    lk = lambda i, j, l: l
    if mode == "tn":
        a_spec = _spec(a.shape, a_kind, tk, tm, lk, im)
        dims = (((0,), (0,)), ((), ()))
    else:
        a_spec = _spec(a.shape, a_kind, tm, tk, im, lk)
        dims = (((1,), (1,)), ((), ())) if mode == "nt" else (((1,), (0,)), ((), ()))
    if mode == "nt":
        b_spec = _spec(b.shape, b_kind, tn, tk, jn, lk)
    else:
        b_spec = _spec(b.shape, b_kind, tk, tn, lk, jn)
    if out_kind is None:
        oshape = (m, n)
    elif out_kind == "col":
        oshape = (N_CHIPS, m, n // N_CHIPS)
    else:
        oshape = (N_CHIPS, m // N_CHIPS, n)
    o_spec = _spec(oshape, out_kind, tm, tn, im, jn)
    in_specs = [a_spec, b_spec]
    args = [a, b]
    if res is not None:
        in_specs.append(pl.BlockSpec((tm, tn), lambda i, j, l: (i, j)))
        args.append(res)
    in_specs += [pl.BlockSpec(memory_space=pl.ANY)] * len(deps)
    args += list(deps)

    def finish(out, res_ref, o_ref):
        if res_ref is not None:
            out = out + res_ref[...]
        o_ref[...] = out.astype(o_ref.dtype)

    def body_whole_k(*refs):
        res_ref = refs[2] if res is not None else None
        finish(lax.dot_general(refs[0][...], refs[1][...], dims, preferred_element_type=F32), res_ref, refs[-1])

    def body(*refs):
        a_ref, b_ref = refs[0], refs[1]
        res_ref = refs[2] if res is not None else None
        o_ref, acc_ref = refs[-2], refs[-1]
        step = pl.program_id(2)

        @pl.when(step == 0)
        def _():
            acc_ref[...] = jnp.zeros_like(acc_ref)

        acc_ref[...] += lax.dot_general(a_ref[...], b_ref[...], dims, preferred_element_type=F32)

        @pl.when(step == nk - 1)
        def _():
            finish(acc_ref[...], res_ref, o_ref)

    return pl.pallas_call(
        body_whole_k if nk == 1 else body, name=name, grid=(m // tm, n // tn, nk),
        in_specs=in_specs, out_specs=o_spec,
        out_shape=jax.ShapeDtypeStruct(oshape, out_dtype),
        scratch_shapes=[] if nk == 1 else [pltpu.VMEM((tm, tn), F32)],
        compiler_params=_params("parallel", "parallel", "arbitrary"),
    )(*args)


def _pieces(splits, cs, cp):
    out, g0 = [], 0
    for width in splits:
        g1, runs = g0 + width, []
        for j in range(N_CHIPS):
            a, b = max(g0, cs * j), min(g1, cs * (j + 1))
            if a < b:
                runs.append((j * cp + a - cs * j, a - g0, b - a))
        out.append(runs)
        g0 = g1
    return out


def _head_norm(xv, gv):
    r = lax.rsqrt(jnp.mean(xv * xv, axis=-1, keepdims=True) + EPS)
    return (xv * r) * gv


def _head_norm_bwd(dyv, xv, gv):
    r = lax.rsqrt(jnp.mean(xv * xv, axis=-1, keepdims=True) + EPS)
    xr = xv * r
    gdy = dyv * gv
    return r * (gdy - xr * jnp.mean(gdy * xr, axis=-1, keepdims=True)), jnp.sum(dyv * xr, axis=0, keepdims=True)


def _proj_split(proj_p, splits, cs, dtypes, gains, tm=128):
    s, wp = proj_p.shape
    pieces = _pieces(splits, cs, wp // N_CHIPS)
    normed = sorted(gains)
    nseg = len(splits)

    def body(p_ref, *refs):
        g_refs, o_refs, n_refs = refs[:len(normed)], refs[len(normed):len(normed) + nseg], refs[len(normed) + nseg:]
        for o_ref, runs in zip(o_refs, pieces):
            for src, dst, n in runs:
                o_ref[:, dst:dst + n] = p_ref[:, src:src + n].astype(o_ref.dtype)
        for g_ref, n_ref, i in zip(g_refs, n_refs, normed):
            for c0 in range(0, splits[i], HEAD_DIM):
                cols = slice(c0, c0 + HEAD_DIM)
                n_ref[:, cols] = _head_norm(o_refs[i][:, cols], g_ref[:, cols]).astype(n_ref.dtype)

    return pl.pallas_call(
        body, name="proj_split", grid=(s // tm,),
        in_specs=[pl.BlockSpec((tm, wp), lambda i: (i, 0))] + [pl.BlockSpec((1, splits[i]), lambda i: (0, 0)) for i in normed],
        out_specs=[pl.BlockSpec((tm, w), lambda i: (i, 0)) for w in splits]
        + [pl.BlockSpec((tm, splits[i]), lambda i: (i, 0)) for i in normed],
        out_shape=[jax.ShapeDtypeStruct((s, w), dt) for w, dt in zip(splits, dtypes)]
        + [jax.ShapeDtypeStruct((s, splits[i]), BF16) for i in normed],
        compiler_params=_params("parallel"),
    )(proj_p, *[gains[i] for i in normed])


def _dproj_merge(parts, splits, cs, cp, norms, tm=128):
    s = parts[0].shape[0]
    wp = N_CHIPS * cp
    pieces = _pieces(splits, cs, cp)
    normed = sorted(norms)
    nseg, nn = len(splits), len(normed)

    def body(*refs):
        p_refs, x_refs, g_refs = refs[:nseg], refs[nseg:nseg + nn], refs[nseg + nn:nseg + 2 * nn]
        o_ref, dg_refs = refs[nseg + 2 * nn], refs[nseg + 2 * nn + 1:nseg + 3 * nn + 1]
        stage, tmp = refs[-2], refs[-1]

        @pl.when(pl.program_id(0) == 0)
        def _():
            for dg_ref in dg_refs:
                dg_ref[...] = jnp.zeros_like(dg_ref)

        for j in range(N_CHIPS):
            stage[:, j * cp + cs:(j + 1) * cp] = jnp.zeros((tm, cp - cs), F32)
        for i, (p_ref, runs) in enumerate(zip(p_refs, pieces)):
            src_ref = p_ref
            if i in norms:
                k = normed.index(i)
                for c0 in range(0, splits[i], HEAD_DIM):
                    cols = slice(c0, c0 + HEAD_DIM)
                    dx, dg = _head_norm_bwd(p_ref[:, cols].astype(F32), x_refs[k][:, cols], g_refs[k][:, cols])
                    tmp[:, cols] = dx
                    dg_refs[k][:, cols] += dg
                src_ref = tmp
            for dst, src, n in runs:
                stage[:, dst:dst + n] = src_ref[:, src:src + n].astype(F32)
        o_ref[...] = stage[...].astype(o_ref.dtype)

    wmax = max(splits[i] for i in normed)
    row = lambda w: pl.BlockSpec((tm, w), lambda i: (i, 0))
    vec = lambda w: pl.BlockSpec((1, w), lambda i: (0, 0))
    outs = pl.pallas_call(
        body, name="dproj_merge", grid=(s // tm,),
        in_specs=[row(w) for w in splits] + [row(splits[i]) for i in normed] + [vec(splits[i]) for i in normed],
        out_specs=[row(wp)] + [vec(splits[i]) for i in normed],
        out_shape=[jax.ShapeDtypeStruct((s, wp), BF16)] + [jax.ShapeDtypeStruct((1, splits[i]), F32) for i in normed],
        scratch_shapes=[pltpu.VMEM((tm, wp), F32), pltpu.VMEM((tm, wmax), F32)],
        compiler_params=_params("arbitrary"),
    )(*parts, *[norms[i][0] for i in normed], *[norms[i][1] for i in normed])
    return outs[0], dict(zip(normed, outs[1:]))


def _norm_fwd(x, g, *, group, name, tm=256):
    s, w = x.shape
    ng = w // group

    def body(x_ref, g_ref, o_ref):
        for i in range(ng):
            cols = slice(i * group, (i + 1) * group)
            xv = x_ref[:, cols]
            r = lax.rsqrt(jnp.mean(xv * xv, axis=-1, keepdims=True) + EPS)
            o_ref[:, cols] = ((xv * r) * g_ref[:, cols]).astype(o_ref.dtype)

    return pl.pallas_call(
        body, name=name, grid=(s // tm,),
        in_specs=[pl.BlockSpec((tm, w), lambda i: (i, 0)), pl.BlockSpec((1, w), lambda i: (0, 0))],
        out_specs=pl.BlockSpec((tm, w), lambda i: (i, 0)),
        out_shape=jax.ShapeDtypeStruct((s, w), BF16),
        compiler_params=_params("parallel"),
    )(x, g)


def _norm_bwd(dy, x, g, *, group, name, res=None, out_dtypes=(BF16,), tm=256):
    s, w = x.shape
    ng = w // group
    n_in = 4 if res is not None else 3

    def body(*refs):
        dy_ref, x_ref, g_ref = refs[:3]
        res_ref = refs[3] if res is not None else None
        outs = refs[n_in:]
        dx_refs, dg_ref = outs[:-1], outs[-1]

        @pl.when(pl.program_id(0) == 0)
        def _():
            dg_ref[...] = jnp.zeros_like(dg_ref)

        for i in range(ng):
            cols = slice(i * group, (i + 1) * group)
            xv = x_ref[:, cols]
            dyv = dy_ref[:, cols].astype(F32)
            r = lax.rsqrt(jnp.mean(xv * xv, axis=-1, keepdims=True) + EPS)
            xr = xv * r
            dg_ref[:, cols] += jnp.sum(dyv * xr, axis=0, keepdims=True)
            gdy = dyv * g_ref[:, cols]
            dx = r * (gdy - xr * jnp.mean(gdy * xr, axis=-1, keepdims=True))
            if res_ref is not None:
                dx = dx + res_ref[:, cols]
            for dx_ref in dx_refs:
                dx_ref[:, cols] = dx.astype(dx_ref.dtype)

    row = pl.BlockSpec((tm, w), lambda i: (i, 0))
    vec = pl.BlockSpec((1, w), lambda i: (0, 0))
    in_specs = [row, row, vec] + ([row] if res is not None else [])
    args = [dy, x, g] + ([res] if res is not None else [])
    out_specs = [row] * len(out_dtypes) + [vec]
    out_shape = [jax.ShapeDtypeStruct((s, w), dt) for dt in out_dtypes] + [jax.ShapeDtypeStruct((1, w), F32)]
    return pl.pallas_call(
        body, name=name, grid=(s // tm,), in_specs=in_specs, out_specs=out_specs,
        out_shape=out_shape, compiler_params=_params("arbitrary"),
    )(*args)


def _split3(v):
    p1 = v.astype(BF16)
    r1 = v - p1.astype(F32)
    p2 = r1.astype(BF16)
    p3 = (r1 - p2.astype(F32)).astype(BF16)
    return p1, p2, p3


def _tri_sum(v, reverse, tcol=512):
    h, s = v.shape
    tcol = min(tcol, s)
    parts = _split3(v)
    outs = []
    for j in range(s // tcol):
        src = lax.broadcasted_iota(jnp.int32, (s, tcol), 0)
        dst = lax.broadcasted_iota(jnp.int32, (s, tcol), 1) + j * tcol
        keep = (src >= dst) if reverse else (src <= dst)
        tri = jnp.where(keep, 1.0, 0.0).astype(BF16)
        acc = jnp.zeros((h, tcol), F32)
        for p in parts:
            acc = acc + jnp.dot(p, tri, preferred_element_type=F32)
        outs.append(acc)
    return outs


def _forget_fwd(fa_t, b):
    h, s = fa_t.shape
    tcol = min(512, s)

    def body(f_ref, b_ref, c_ref):
        z = f_ref[...] + b_ref[...]
        logf = jnp.minimum(z, 0.0) - jnp.log(1.0 + jnp.exp(-jnp.abs(z)))
        for j, blk in enumerate(_tri_sum(logf, reverse=False, tcol=tcol)):
            c_ref[:, j * tcol:(j + 1) * tcol] = blk

    return pl.pallas_call(
        body, name="forget_fwd", out_shape=jax.ShapeDtypeStruct((h, s), F32),
        compiler_params=_params(),
    )(fa_t, b)


def _forget_bwd(dacol, fa_t, b):
    h, s = fa_t.shape
    tcol = min(512, s)

    def body(d_ref, f_ref, b_ref, dfa_ref, db_ref):
        z = f_ref[...] + b_ref[...]
        dc = -d_ref[...]
        total = jnp.zeros((h, 1), F32)
        for j, blk in enumerate(_tri_sum(dc, reverse=True, tcol=tcol)):
            cols = slice(j * tcol, (j + 1) * tcol)
            dfa = blk * (1.0 - jax.nn.sigmoid(z[:, cols]))
            dfa_ref[:, cols] = dfa
            total = total + jnp.sum(dfa, axis=-1, keepdims=True)
        db_ref[...] = total

    return pl.pallas_call(
        body, name="forget_bwd",
        out_shape=[jax.ShapeDtypeStruct((h, s), F32), jax.ShapeDtypeStruct((h, 1), F32)],
        compiler_params=_params(),
    )(dacol, fa_t, b)


def _distance_bias(s, tile, dilated):
    nb = s // tile
    b = lax.broadcasted_iota(jnp.int32, (nb, tile, tile), 0)
    dist = b * tile + lax.broadcasted_iota(jnp.int32, (nb, tile, tile), 1) - lax.broadcasted_iota(jnp.int32, (nb, tile, tile), 2)
    if not dilated:
        return jnp.where(dist >= 0, 0.0, NEG).astype(F32)
    mult = jnp.zeros(dist.shape, jnp.int32)
    for window, dil in DIL_PATTERNS:
        mult = mult + ((dist >= 0) & (dist <= window) & ((dist & (dil - 1)) == 0)).astype(jnp.int32)
    logm = jnp.where(mult == 3, math.log(3.0), jnp.where(mult == 2, math.log(2.0), 0.0))
    return jnp.where(mult > 0, logm, NEG).astype(F32)


def _logits(q, k, arow, acol, bias):
    s = lax.dot_general(q, k, (((1,), (1,)), ((), ())), preferred_element_type=F32)
    return s * (1.0 / math.sqrt(HEAD_DIM)) + arow - acol + bias


def _attn_fwd(q, k, v, arow, acol, *, dilated, name, tq=ATTN_TILE, tk=ATTN_TILE):
    two_term = not dilated
    s, w = q.shape
    nh = w // HEAD_DIM
    assert tq == tk
    tq = tk = min(tq, s)
    nq, nk = s // tq, s // tk

    def body(q_ref, k_ref, v_ref, ar_ref, ac_ref, b_ref, o_ref, of_ref, lse_ref, m_ref, l_ref, acc_ref):
        qi, ki = pl.program_id(1), pl.program_id(2)

        @pl.when(ki == 0)
        def _():
            m_ref[...] = jnp.full_like(m_ref, NEG)
            l_ref[...] = jnp.zeros_like(l_ref)
            acc_ref[...] = jnp.zeros_like(acc_ref)

        @pl.when(ki <= qi)
        def _():
            sc = _logits(q_ref[...], k_ref[...], ar_ref[...], ac_ref[...], b_ref[...])
            m_new = jnp.maximum(m_ref[...], jnp.max(sc, axis=-1, keepdims=True))
            alpha = jnp.exp(m_ref[...] - m_new)
            p = jnp.exp(sc - m_new)
            l_ref[...] = alpha * l_ref[...] + jnp.sum(p, axis=-1, keepdims=True)
            p_hi = p.astype(BF16)
            vv = v_ref[...]
            pv = jnp.dot(p_hi, vv, preferred_element_type=F32)
            if two_term:
                pv = pv + jnp.dot((p - p_hi.astype(F32)).astype(BF16), vv, preferred_element_type=F32)
            acc_ref[...] = alpha * acc_ref[...] + pv
            m_ref[...] = m_new

        @pl.when(ki == nk - 1)
        def _():
            out = acc_ref[...] / l_ref[...]
            o_ref[...] = out.astype(o_ref.dtype)
            of_ref[...] = out
            lse_ref[...] = m_ref[...] + jnp.log(l_ref[...])

    kv = pl.BlockSpec((tk, HEAD_DIM), lambda h, i, j: (jnp.minimum(j, i), h))
    return pl.pallas_call(
        body, name=name, grid=(nh, nq, nk),
        in_specs=[pl.BlockSpec((tq, HEAD_DIM), lambda h, i, j: (i, h)), kv, kv,
                  pl.BlockSpec((None, tq, 1), lambda h, i, j: (h, i, 0)),
                  pl.BlockSpec((None, 1, tk), lambda h, i, j: (h, 0, jnp.minimum(j, i))),
                  pl.BlockSpec((None, tq, tk), lambda h, i, j: (jnp.maximum(i - j, 0), 0, 0))],
        out_specs=[pl.BlockSpec((tq, HEAD_DIM), lambda h, i, j: (i, h)),
                   pl.BlockSpec((tq, HEAD_DIM), lambda h, i, j: (i, h)),
                   pl.BlockSpec((None, tq, 1), lambda h, i, j: (h, i, 0))],
        out_shape=[jax.ShapeDtypeStruct((s, w), BF16), jax.ShapeDtypeStruct((s, w), F32),
                   jax.ShapeDtypeStruct((nh, s, 1), F32)],
        scratch_shapes=[pltpu.VMEM((tq, 1), F32), pltpu.VMEM((tq, 1), F32), pltpu.VMEM((tq, HEAD_DIM), F32)],
        compiler_params=_params("parallel", "parallel", "arbitrary"),
    )(q, k, v, arow, acol, _distance_bias(s, tq, dilated))


def _attn_bwd(q, k, v, o, do, lse, arow, acol, *, dilated, name, tq=ATTN_TILE, tk=ATTN_TILE):
    s, w = q.shape
    nh = w // HEAD_DIM
    assert tq == tk
    tq = tk = min(tq, s)
    nq, nk = s // tq, s // tk
    scale = 1.0 / math.sqrt(HEAD_DIM)

    def body(q_ref, k_ref, v_ref, o_ref, do_ref, lse_ref, ar_ref, ac_ref, b_ref,
             dq_ref, dk_ref, dv_ref, dac_ref, dk_acc, dv_acc, dac_acc):
        ki, qi = pl.program_id(1), pl.program_id(2)

        @pl.when((ki == 0) & (qi == 0))
        def _():
            dq_ref[...] = jnp.zeros_like(dq_ref)

        @pl.when(qi == 0)
        def _():
            dk_acc[...] = jnp.zeros_like(dk_acc)
            dv_acc[...] = jnp.zeros_like(dv_acc)
            dac_acc[...] = jnp.zeros_like(dac_acc)

        @pl.when(qi >= ki)
        def _():
            qv, kvv, dov = q_ref[...], k_ref[...], do_ref[...]
            sc = _logits(qv, kvv, ar_ref[...], ac_ref[...], b_ref[...])
            p = jnp.exp(sc - lse_ref[...])
            dp = lax.dot_general(dov, v_ref[...], (((1,), (1,)), ((), ())), preferred_element_type=F32)
            delta = jnp.sum(dov.astype(F32) * o_ref[...].astype(F32), axis=-1, keepdims=True)
            ds = p * (dp - delta)
            dsb = ds.astype(BF16)
            dv_acc[...] += lax.dot_general(p.astype(BF16), dov, (((0,), (0,)), ((), ())), preferred_element_type=F32)
            dk_acc[...] += lax.dot_general(dsb, qv, (((0,), (0,)), ((), ())), preferred_element_type=F32)
            rows = pl.ds(pl.multiple_of(qi * tq, tq), tq)
            dq_ref[rows, :] += jnp.dot(dsb, kvv, preferred_element_type=F32) * scale
            dac_acc[...] += jnp.sum(ds, axis=0, keepdims=True)

        @pl.when(qi == nq - 1)
        def _():
            dk_ref[...] = dk_acc[...] * scale
            dv_ref[...] = dv_acc[...]
            dac_ref[...] = dac_acc[...]

    qs = pl.BlockSpec((tq, HEAD_DIM), lambda h, j, i: (jnp.maximum(i, j), h))
    ks = pl.BlockSpec((tk, HEAD_DIM), lambda h, j, i: (j, h))
    rowv = pl.BlockSpec((None, tq, 1), lambda h, j, i: (h, jnp.maximum(i, j), 0))
    colv = pl.BlockSpec((None, 1, tk), lambda h, j, i: (h, 0, j))
    return pl.pallas_call(
        body, name=name, grid=(nh, nk, nq),
        in_specs=[qs, ks, ks, qs, qs, rowv, rowv, colv,
                  pl.BlockSpec((None, tq, tk), lambda h, j, i: (jnp.maximum(i - j, 0), 0, 0))],
        out_specs=[pl.BlockSpec((s, HEAD_DIM), lambda h, j, i: (0, h)), ks, ks, colv],
        out_shape=[jax.ShapeDtypeStruct((s, w), F32), jax.ShapeDtypeStruct((s, w), F32),
                   jax.ShapeDtypeStruct((s, w), F32), jax.ShapeDtypeStruct((nh, 1, s), F32)],
        scratch_shapes=[pltpu.VMEM((tk, HEAD_DIM), F32), pltpu.VMEM((tk, HEAD_DIM), F32), pltpu.VMEM((1, tk), F32)],
        compiler_params=_params("arbitrary", "arbitrary", "arbitrary"),
    )(q, k, v, o, do, lse, arow, acol, _distance_bias(s, tq, dilated))


def _gate_fwd(ga, gb, pa, pb, tm=256):
    s, d = ga.shape

    def body(ga_ref, gb_ref, pa_ref, pb_ref, o_ref):
        o_ref[...] = (jax.nn.sigmoid(ga_ref[...]) * pa_ref[...]
                      + jax.nn.sigmoid(gb_ref[...]) * pb_ref[...]).astype(o_ref.dtype)

    row = pl.BlockSpec((tm, d), lambda i: (i, 0))
    return pl.pallas_call(
        body, name="gate_fwd", grid=(s // tm,), in_specs=[row] * 4, out_specs=row,
        out_shape=jax.ShapeDtypeStruct((s, d), BF16), compiler_params=_params("parallel"),
    )(ga, gb, pa, pb)


def _gate_bwd(dm, ga, gb, pa, pb, tm=256):
    s, d = ga.shape

    def body(dm_ref, ga_ref, gb_ref, pa_ref, pb_ref, dpa_ref, dpb_ref, dga_ref, dgb_ref):
        dmv = dm_ref[...]
        for g_ref, p_ref, dp_ref, dg_ref in ((ga_ref, pa_ref, dpa_ref, dga_ref), (gb_ref, pb_ref, dpb_ref, dgb_ref)):
            sg = jax.nn.sigmoid(g_ref[...])
            dp_ref[...] = (dmv * sg).astype(BF16)
            dg_ref[...] = (dmv * p_ref[...] * (sg * (1.0 - sg))).astype(BF16)

    row = pl.BlockSpec((tm, d), lambda i: (i, 0))
    return pl.pallas_call(
        body, name="gate_bwd", grid=(s // tm,), in_specs=[row] * 5, out_specs=[row] * 4,
        out_shape=[jax.ShapeDtypeStruct((s, d), BF16)] * 4, compiler_params=_params("parallel"),
    )(dm, ga, gb, pa, pb)


def _shift_down(u, k):
    row = lax.broadcasted_iota(jnp.int32, u.shape, 0)
    return jnp.where(row >= k, pltpu.roll(u, k, 0), 0.0)


def _shift_up(u, k):
    n = u.shape[0]
    row = lax.broadcasted_iota(jnp.int32, u.shape, 0)
    return jnp.where(row < n - k, pltpu.roll(u, n - k, 0), 0.0)


def _conv3(u, wc, b):
    return wc[0:1, :] * _shift_down(u, 2) + wc[1:2, :] * _shift_down(u, 1) + wc[2:3, :] * u + b


def _conv_glu_fwd(u, wc, b, tn=256):
    s, f2 = u.shape
    f = f2 // 2
    nb = f // tn

    def body(ug_ref, uv_ref, wg_ref, wv_ref, bg_ref, bv_ref, o_ref):
        cg = _conv3(ug_ref[...], wg_ref[...], bg_ref[...])
        cv = _conv3(uv_ref[...], wv_ref[...], bv_ref[...])
        o_ref[...] = (cg * jax.nn.sigmoid(cg) * cv).astype(o_ref.dtype)

    def cols(rows, off):
        return pl.BlockSpec((rows, tn), lambda j: (0, j + off))

    return pl.pallas_call(
        body, name="conv_glu_fwd", grid=(nb,),
        in_specs=[cols(s, 0), cols(s, nb), cols(3, 0), cols(3, nb), cols(1, 0), cols(1, nb)],
        out_specs=cols(s, 0), out_shape=jax.ShapeDtypeStruct((s, f), BF16),
        compiler_params=_params("parallel"),
    )(u, u, wc, wc, b, b)


def _conv_glu_bwd(u, da, wc, b, tn=256):
    s, f2 = u.shape
    f = f2 // 2
    nb = f // tn

    def body(ug_ref, uv_ref, da_ref, wg_ref, wv_ref, bg_ref, bv_ref, dug_ref, duv_ref, sg_ref, sv_ref):
        ug, uv, wg, wv = ug_ref[...], uv_ref[...], wg_ref[...], wv_ref[...]
        cg = _conv3(ug, wg, bg_ref[...])
        cv = _conv3(uv, wv, bv_ref[...])
        sig = jax.nn.sigmoid(cg)
        dav = da_ref[...]
        dcv = dav * (cg * sig)
        dcg = dav * cv * (sig * (1.0 + cg * (1.0 - sig)))
        for dc, uu, w, du_ref, st_ref in ((dcg, ug, wg, dug_ref, sg_ref), (dcv, uv, wv, duv_ref, sv_ref)):
            du = w[2:3, :] * dc + w[1:2, :] * _shift_up(dc, 1) + w[0:1, :] * _shift_up(dc, 2)
            du_ref[...] = du.astype(BF16)
            st_ref[...] = jnp.zeros_like(st_ref)
            st_ref[0:1, :] = jnp.sum(dc * _shift_down(uu, 2), axis=0, keepdims=True)
            st_ref[1:2, :] = jnp.sum(dc * _shift_down(uu, 1), axis=0, keepdims=True)
            st_ref[2:3, :] = jnp.sum(dc * uu, axis=0, keepdims=True)
            st_ref[3:4, :] = jnp.sum(dc, axis=0, keepdims=True)

    def cols(rows, off):
        return pl.BlockSpec((rows, tn), lambda j: (0, j + off))

    return pl.pallas_call(
        body, name="conv_glu_bwd", grid=(nb,),
        in_specs=[cols(s, 0), cols(s, nb), cols(s, 0), cols(3, 0), cols(3, nb), cols(1, 0), cols(1, nb)],
        out_specs=[cols(s, 0), cols(s, 0), cols(8, 0), cols(8, 0)],
        out_shape=[jax.ShapeDtypeStruct((s, f), BF16), jax.ShapeDtypeStruct((s, f), BF16),
                   jax.ShapeDtypeStruct((8, f), F32), jax.ShapeDtypeStruct((8, f), F32)],
        compiler_params=_params("parallel"),
    )(u, u, da, wc, wc, b, b)


def _loss_head(y, target, tm=256):
    s, d = y.shape

    def body(y_ref, t_ref, dyf_ref, dyb_ref, l_ref):
        @pl.when(pl.program_id(0) == 0)
        def _():
            l_ref[...] = jnp.zeros_like(l_ref)

        err = y_ref[...] - t_ref[...]
        dy = err * (1.0 / d)
        dyf_ref[...] = dy
        dyb_ref[...] = dy.astype(BF16)
        l_ref[...] += 0.5 * jnp.sum(jnp.sum(err * err, axis=-1, keepdims=True) * (1.0 / d), axis=0, keepdims=True)

    row = pl.BlockSpec((tm, d), lambda i: (i, 0))
    return pl.pallas_call(
        body, name="loss_head", grid=(s // tm,), in_specs=[row, row],
        out_specs=[row, row, pl.BlockSpec((8, LANES), lambda i: (0, 0))],
        out_shape=[jax.ShapeDtypeStruct((s, d), F32), jax.ShapeDtypeStruct((s, d), BF16),
                   jax.ShapeDtypeStruct((8, LANES), F32)],
        compiler_params=_params("arbitrary"),
    )(y, target)


ROW_TILES = (256, 128, 64, 32, 16, 8)
BLOCK_BYTES = 2 << 20


def _add_halves(g, r1, place):
    ns, r, c = g.shape
    rh = r // 2
    tr = _pick(rh, ROW_TILES)
    g4 = g.reshape(ns, 2, rh, c)

    def body(p_ref, g_ref, r_ref, o_ref):
        o_ref[...] = (g_ref[...].astype(F32) + r_ref[...].astype(F32)).astype(o_ref.dtype)

    return pl.pallas_call(
        body, name="add_halves",
        grid_spec=pltpu.PrefetchScalarGridSpec(
            num_scalar_prefetch=1, grid=(ns, rh // tr),
            in_specs=[pl.BlockSpec((None, None, tr, c), lambda s, i, pr: (s, pr[1], i, 0)),
                      pl.BlockSpec((None, tr, c), lambda s, i, pr: (s, i, 0))],
            out_specs=pl.BlockSpec((None, tr, c), lambda s, i, pr: (s, i, 0))),
        out_shape=jax.ShapeDtypeStruct((ns, rh, c), BF16),
        compiler_params=_params("parallel", "parallel"),
    )(place, g4, r1)


def _sum_chips(g, r1, recv, place):
    ns, r, c = g.shape
    rh = r // 2
    tr = _pick(rh, ROW_TILES)
    g4 = g.reshape(ns, 2, rh, c)

    def body(p_ref, g_ref, r_ref, t0_ref, t1_ref, t2_ref, o_ref):
        own = g_ref[...].astype(F32) + r_ref[...].astype(F32)
        o_ref[...] = ((own + t0_ref[...].astype(F32)) + t1_ref[...].astype(F32)) + t2_ref[...].astype(F32)

    def peer(k):
        return pl.BlockSpec((None, tr, c), lambda i, pr: (k, i, 0))

    return pl.pallas_call(
        body, name="sum_chips",
        grid_spec=pltpu.PrefetchScalarGridSpec(
            num_scalar_prefetch=1, grid=(rh // tr,),
            in_specs=[pl.BlockSpec((None, None, tr, c), lambda i, pr: (pr[0], pr[1], i, 0)),
                      pl.BlockSpec((None, tr, c), lambda i, pr: (pr[0], i, 0)), peer(0), peer(1), peer(2)],
            out_specs=pl.BlockSpec((tr, c), lambda i, pr: (i, 0))),
        out_shape=jax.ShapeDtypeStruct((rh, c), F32),
        compiler_params=_params("parallel"),
    )(place, g4, r1, recv, recv, recv)


def _sum_devices(packs):
    n, r, c = packs.shape

    def body(p_ref, o_ref):
        acc = p_ref[0]
        for d in range(1, n):
            acc = acc + p_ref[d]
        o_ref[...] = acc

    return pl.pallas_call(
        body, name="sum_devices", out_shape=jax.ShapeDtypeStruct((r, c), F32), compiler_params=_params(),
    )(packs)


def _adamw_update(wv, gv, mv, vv):
    c1 = 1.0 - ADAM_B1 ** ADAM_STEP
    c2 = 1.0 - ADAM_B2 ** ADAM_STEP
    mn = ADAM_B1 * mv + (1.0 - ADAM_B1) * gv
    vn = ADAM_B2 * vv + (1.0 - ADAM_B2) * (gv * gv)
    m_hat = mn / c1
    v_hat = vn / c2
    return -ADAM_LR * (m_hat / (jnp.sqrt(v_hat) + ADAM_EPS) + ADAM_WD * wv), mn, vn


def _adamw(w, g, m, v, name, deps=()):
    r, c = w.shape
    tr = _pick(r, ROW_TILES) if r >= 8 else r

    def body(w_ref, g_ref, m_ref, v_ref, *rest):
        d_ref, mo_ref, vo_ref = rest[-3:]
        d_ref[...], mo_ref[...], vo_ref[...] = _adamw_update(w_ref[...], g_ref[...], m_ref[...], v_ref[...])

    blk = pl.BlockSpec((tr, c), lambda i: (i, 0))
    return pl.pallas_call(
        body, name=name, grid=(r // tr,), in_specs=[blk] * 4 + [ANY] * len(deps), out_specs=[blk] * 3,
        out_shape=[jax.ShapeDtypeStruct((r, c), F32)] * 3, compiler_params=_params("parallel"),
    )(w, g, m, v, *deps)


def _adamw_halves(w, mine, theirs, c_idx, m, v, name, deps=()):
    r, c = w.shape
    rh = r // 2
    tr = _pick(rh, [t for t in ROW_TILES if t * c * 4 <= BLOCK_BYTES])
    nb = rh // tr

    def body(c_ref, w_ref, a_ref, b_ref, m_ref, v_ref, *rest):
        g_ref, d_ref, mo_ref, vo_ref = rest[-4:]
        gv = jnp.where(pl.program_id(0) // nb == c_ref[0], a_ref[...], b_ref[...])
        g_ref[...] = gv
        d_ref[...], mo_ref[...], vo_ref[...] = _adamw_update(w_ref[...], gv, m_ref[...], v_ref[...])

    blk = pl.BlockSpec((tr, c), lambda i, cr: (i, 0))
    mine_spec = pl.BlockSpec((tr, c), lambda i, cr: (jnp.clip(i - cr[0] * nb, 0, nb - 1), 0))
    theirs_spec = pl.BlockSpec((tr, c), lambda i, cr: (jnp.clip(i - (1 - cr[0]) * nb, 0, nb - 1), 0))
    return pl.pallas_call(
        body, name=name,
        grid_spec=pltpu.PrefetchScalarGridSpec(
            num_scalar_prefetch=1, grid=(r // tr,),
            in_specs=[blk, mine_spec, theirs_spec, blk, blk] + [ANY] * len(deps), out_specs=[blk] * 4),
        out_shape=[jax.ShapeDtypeStruct((r, c), F32)] * 4, compiler_params=_params("arbitrary"),
    )(c_idx, w, mine, theirs, m, v, *deps)


ANY = pl.BlockSpec(memory_space=pl.ANY)


def _place():
    x, y, c = lax.axis_index("x"), lax.axis_index("y"), lax.axis_index("c")
    chips = [(1 - x, y), (x, 1 - y), (1 - x, 1 - y)]
    return x, y, c, chips


def _remote(src, dst, send_sem, recv_sem, to):
    return pltpu.make_async_remote_copy(src_ref=src, dst_ref=dst, send_sem=send_sem, recv_sem=recv_sem,
                                        device_id=to, device_id_type=MESH)


HBM = pl.BlockSpec(memory_space=pltpu.HBM)
SEM = pl.BlockSpec(memory_space=pltpu.SEMAPHORE)
EFFECT = pltpu.SideEffectType.DATAFLOW_SIDE_EFFECTING


def _in_hbm(a):
    return pltpu.with_memory_space_constraint(a, pltpu.HBM)


def _half(ref_rows, who):
    return pl.ds(who * (ref_rows // 2), ref_rows // 2)


def _gather_start(groups):
    items = [it for g in groups for it in g]
    n = len(items)
    sizes = [len(g) for g in groups]

    def body(*refs):
        srcs, lands = refs[:n], refs[n:2 * n]
        sems = refs[2 * n:2 * n + 2 * len(groups)]
        token = refs[-1]
        x, y, c, chips = _place()
        j = 2 * x + y
        at = 0
        for gi, g in enumerate(groups):
            send, recv = sems[2 * gi], sems[2 * gi + 1]
            for i, (shard, split) in enumerate(g):
                src, land = srcs[at], lands[at]
                at += 1
                rows = _half(shard.shape[0], c) if split else slice(None)
                for k, chip in enumerate(chips):
                    _remote(src.at[rows], land.at[j, rows], send.at[3 * i + k], recv.at[3 * i + k], (*chip, c)).start()
        token[...] = jnp.zeros_like(token)

    sem_shapes = []
    for sz in sizes:
        sem_shapes += [pltpu.SemaphoreType.DMA((3 * sz,)), pltpu.SemaphoreType.DMA((3 * sz,))]
    out_shape = (sem_shapes + [pltpu.HBM(sh.shape, sh.dtype) for sh, _ in items]
                 + [pltpu.HBM((N_CHIPS,) + sh.shape, sh.dtype) for sh, _ in items]
                 + [jax.ShapeDtypeStruct((8, LANES), F32)])
    ns = len(sem_shapes)
    outs = pl.pallas_call(
        body, name="gather_start", in_specs=[HBM] * (2 * n),
        out_specs=[SEM] * ns + [HBM] * (2 * n) + [pl.BlockSpec(memory_space=pltpu.VMEM)],
        out_shape=out_shape, input_output_aliases={i: ns + i for i in range(2 * n)},
        compiler_params=pltpu.CompilerParams(has_side_effects=EFFECT),
    )(*[_in_hbm(sh) for sh, _ in items], *[_in_hbm(lax.empty((N_CHIPS,) + sh.shape, sh.dtype)) for sh, _ in items])
    sems, shards, lands, token = outs[:ns], outs[ns:ns + n], outs[ns + n:ns + 2 * n], outs[-1]
    res, at = [], 0
    for gi, sz in enumerate(sizes):
        res.append((shards[at:at + sz], lands[at:at + sz], sems[2 * gi], sems[2 * gi + 1]))
        at += sz
    return res, token


def _gather_pass(group, started, after, name):
    shards, lands, send, recv = started
    n = len(group)
    split_ix = [i for i, (_, split) in enumerate(group) if split]

    def body(*refs):
        lnds, send1, recv1 = refs[n:2 * n], refs[2 * n], refs[2 * n + 1]
        outs = refs[2 * n + 2 + len(after):]
        send2, recv2, token = outs[2 * n], outs[2 * n + 1], outs[2 * n + 2]
        x, y, c, chips = _place()
        sib = (x, y, 1 - c)
        for i, (shard, split) in enumerate(group):
            rows = _half(shard.shape[0], c) if split else slice(None)
            for k, (cx, cy) in enumerate(chips):
                landed = lnds[i].at[2 * cx + cy, rows]
                cp = _remote(landed, landed, send1.at[3 * i + k], recv1.at[3 * i + k], sib)
                cp.wait_send()
                cp.wait_recv()
        for i2, i in enumerate(split_ix):
            rows = _half(group[i][0].shape[0], c)
            for k, (cx, cy) in enumerate(chips):
                landed = lnds[i].at[2 * cx + cy, rows]
                _remote(landed, landed, send2.at[3 * i2 + k], recv2.at[3 * i2 + k], sib).start()
        token[...] = jnp.zeros_like(token)

    n2 = len(split_ix)
    out_shape = ([pltpu.HBM(a.shape, a.dtype) for a in (*shards, *lands)]
                 + [pltpu.SemaphoreType.DMA((3 * n2,)), pltpu.SemaphoreType.DMA((3 * n2,)), jax.ShapeDtypeStruct((8, LANES), F32)])
    outs = pl.pallas_call(
        body, name=name, in_specs=[HBM] * (2 * n) + [SEM, SEM] + [ANY] * len(after),
        out_specs=[HBM] * (2 * n) + [SEM, SEM, pl.BlockSpec(memory_space=pltpu.VMEM)],
        out_shape=out_shape, input_output_aliases={i: i for i in range(2 * n)},
        compiler_params=pltpu.CompilerParams(has_side_effects=EFFECT),
    )(*shards, *lands, send, recv, *after)
    return outs[:n], (outs[n:2 * n], outs[2 * n], outs[2 * n + 1]), outs[2 * n + 2]


def _gather_wait(group, passed, after, name):
    lands, send2, recv2 = passed
    n = len(group)
    split_ix = [i for i, (_, split) in enumerate(group) if split]

    def body(*refs):
        lnds, s2, r2 = refs[:n], refs[n], refs[n + 1]
        x, y, c, chips = _place()
        sib = (x, y, 1 - c)
        for i2, i in enumerate(split_ix):
            rows = _half(group[i][0].shape[0], 1 - c)
            for k, (cx, cy) in enumerate(chips):
                landed = lnds[i].at[2 * cx + cy, rows]
                cp = _remote(landed, landed, s2.at[3 * i2 + k], r2.at[3 * i2 + k], sib)
                cp.wait_send()
                cp.wait_recv()

    return pl.pallas_call(
        body, name=name, in_specs=[HBM] * n + [SEM, SEM, ANY], out_specs=[HBM] * n,
        out_shape=[pltpu.HBM(a.shape, a.dtype) for a in lands], input_output_aliases={i: i for i in range(n)},
        compiler_params=pltpu.CompilerParams(has_side_effects=EFFECT),
    )(*lands, send2, recv2, after)


def _own_slab(land, shard):
    chip = 2 * lax.axis_index("x") + lax.axis_index("y")
    return lax.dynamic_update_slice(land, shard[None], (chip, 0, 0))


def _xfer_start(name, srcs, land_shapes, n_copies, copies, after):
    n = len(srcs)

    def body(*refs):
        src_refs, land_refs = refs[:n], refs[n:2 * n]
        send, recv, token = refs[2 * n + 1], refs[2 * n + 2], refs[-1]
        for cp in copies(src_refs, land_refs, send, recv):
            cp.start()
        token[...] = jnp.zeros_like(token)

    lands = [_in_hbm(lax.empty(shape, dtype)) for shape, dtype in land_shapes]
    out_shape = ([pltpu.SemaphoreType.DMA((n_copies,)), pltpu.SemaphoreType.DMA((n_copies,))]
                 + [pltpu.HBM(a.shape, a.dtype) for a in (*srcs, *lands)] + [jax.ShapeDtypeStruct((8, LANES), F32)])
    outs = pl.pallas_call(
        body, name=name, in_specs=[HBM] * (2 * n) + [ANY],
        out_specs=[SEM, SEM] + [HBM] * (2 * n) + [pl.BlockSpec(memory_space=pltpu.VMEM)],
        out_shape=out_shape, input_output_aliases={i: 2 + i for i in range(2 * n)},
        compiler_params=pltpu.CompilerParams(has_side_effects=EFFECT),
    )(*[_in_hbm(a) for a in srcs], *lands, after)
    return (outs[2:2 + n], outs[2 + n:2 + 2 * n], outs[0], outs[1]), outs[-1]


def _xfer_wait(name, started, copies, after):
    srcs, lands, send, recv = started
    n = len(srcs)

    def body(*refs):
        src_refs, land_refs, s_ref, r_ref = refs[:n], refs[n:2 * n], refs[2 * n], refs[2 * n + 1]
        for cp in copies(src_refs, land_refs, s_ref, r_ref):
            cp.wait_send()
            cp.wait_recv()

    outs = pl.pallas_call(
        body, name=name, in_specs=[HBM] * (2 * n) + [SEM, SEM, ANY], out_specs=[HBM] * (2 * n),
        out_shape=[pltpu.HBM(a.shape, a.dtype) for a in (*srcs, *lands)],
        input_output_aliases={i: i for i in range(2 * n)},
        compiler_params=pltpu.CompilerParams(has_side_effects=EFFECT),
    )(*srcs, *lands, send, recv, after)
    return outs[:n], outs[n:]


def _swap_copies(srcs, lands, send, recv):
    x, y, c, _ = _place()
    return [_remote(src.at[:, _half(src.shape[1], 1 - c)], land, send.at[i], recv.at[i], (x, y, 1 - c))
            for i, (src, land) in enumerate(zip(srcs, lands))]


def _scatter_copies(srcs, lands, send, recv):
    x, y, c, chips = _place()
    return [_remote(src.at[2 * cx + cy], land.at[k], send.at[3 * i + k], recv.at[3 * i + k], (cx, cy, c))
            for i, (src, land) in enumerate(zip(srcs, lands)) for k, (cx, cy) in enumerate(chips)]


def _join_copies(srcs, lands, send, recv):
    x, y, c, _ = _place()
    return [_remote(src, land, send.at[i], recv.at[i], (x, y, 1 - c)) for i, (src, land) in enumerate(zip(srcs, lands))]


def _corner(a):
    return a[(slice(0, 1),) * a.ndim]


class _Reducer:
    def __init__(self, place):
        self.place = place
        self.state = {}

    def swap(self, key, grads, after):
        shapes = [((g.shape[0], g.shape[1] // 2, g.shape[2]), g.dtype) for g in grads]
        self.state[key], token = _xfer_start("swap_start_" + key, grads, shapes, len(grads), _swap_copies, _corner(after))
        return token

    def to_chips(self, key, after):
        grads, from_sibling = _xfer_wait("swap_wait_" + key, self.state[key], _swap_copies, after)
        sums = [_add_halves(g, r, self.place) for g, r in zip(grads, from_sibling)]
        shapes = [((3,) + s.shape[1:], s.dtype) for s in sums]
        started, token = _xfer_start("scatter_start_" + key, sums, shapes, 3 * len(sums), _scatter_copies, _corner(sums[-1]))
        self.state[key] = (grads, from_sibling, started)
        return token

    def to_core(self, key, after):
        grads, from_sibling, started = self.state[key]
        _, from_chips = _xfer_wait("scatter_wait_" + key, started, _scatter_copies, after)
        halves = [_sum_chips(g, r, rc, self.place) for g, r, rc in zip(grads, from_sibling, from_chips)]
        shapes = [(h.shape, h.dtype) for h in halves]
        self.state[key], token = _xfer_start("join_start_" + key, halves, shapes, len(halves), _join_copies, _corner(halves[-1]))
        return token

    def finish(self, key, after):
        return _xfer_wait("join_wait_" + key, self.state.pop(key), _join_copies, after)


def _gather_packs(pack):
    def body(p_ref, o_ref, lsem, ssem, rsem):
        x, y, c, _ = _place()
        me = 4 * x + 2 * y + c
        local = pltpu.make_async_copy(p_ref, o_ref.at[me], lsem)
        local.start()
        cps = []
        for k in range(1, N_DEV):
            fx, fy, fc = (k >> 2) & 1, (k >> 1) & 1, k & 1
            to = (x ^ fx, y ^ fy, c ^ fc)
            cps.append(_remote(p_ref, o_ref.at[me], ssem.at[k - 1], rsem.at[k - 1], to))
        for cp in cps:
            cp.start()
        for k in range(1, N_DEV):
            fx, fy, fc = (k >> 2) & 1, (k >> 1) & 1, k & 1
            src = o_ref.at[4 * (x ^ fx) + 2 * (y ^ fy) + (c ^ fc)]
            _remote(src, src, ssem.at[k - 1], rsem.at[k - 1], (x, y, c)).wait_recv()
        for cp in cps:
            cp.wait_send()
        local.wait()

    return pl.pallas_call(
        body, name="gather_packs", in_specs=[ANY], out_specs=ANY,
        out_shape=jax.ShapeDtypeStruct((N_DEV,) + pack.shape, pack.dtype),
        scratch_shapes=[pltpu.SemaphoreType.DMA, pltpu.SemaphoreType.DMA((N_DEV - 1,)), pltpu.SemaphoreType.DMA((N_DEV - 1,))],
    )(pack)


LANE_TILES = (512, 896, 1408, 704, 384, 256, 128)


def _layer_grads(x, target, small, wg, rest_pass, rest_wait, red):
    s, d = x.shape
    f = wg["conv"].shape[1] // 2
    w_att = N_HEADS * HEAD_DIM
    in_splits = (w_att, w_att, w_att, N_HEADS, w_att, w_att, w_att, d, d)
    in_cols = sum(in_splits)
    cs = in_cols // N_CHIPS
    cp = wg["in"].shape[2]
    tm = min(s, 1024)
    t_in = cp
    t_d = _pick(d, LANE_TILES)
    t_d2 = min(d, 1024)
    t_dq = _pick(d // N_CHIPS, LANE_TILES)
    t_w = _pick(w_att, LANE_TILES)
    t_up = 2 * f // N_CHIPS
    tm_wide = min(s, 512)
    t_fq = _pick(f // N_CHIPS, LANE_TILES)
    offs = np.cumsum(in_splits)[:-1].tolist()

    h1 = _norm_fwd(x, small["g_attn"], group=d, name="rms1_fwd")
    proj_p = _mm(h1, wg["in"], mode="nn", b_kind="col", tm=tm_wide, tn=t_in, tk=d, name="mm_in")
    gains = {n: small[n].reshape(1, w_att) for n in ("g_q_fox", "g_k_fox", "g_q_dil", "g_k_dil")}
    qa, ka, va_b, fa, qb, kb, vb_b, ga, gb, qa_n, ka_n, qb_n, kb_n = _proj_split(
        proj_p, in_splits, cs, (F32, F32, BF16, F32, F32, F32, BF16, F32, F32),
        {0: gains["g_q_fox"], 1: gains["g_k_fox"], 4: gains["g_q_dil"], 5: gains["g_k_dil"]})
    fa_t = fa.T
    b_f = small["b_forget"].reshape(N_HEADS, 1)
    c_f = _forget_fwd(fa_t, b_f)
    slopes = jnp.asarray(2.0 ** (-8.0 * np.arange(1, N_HEADS + 1) / N_HEADS), dtype=F32)
    a_d = -(slopes[:, None] * jnp.arange(s, dtype=F32)[None, :])
    rows_f, cols_f = c_f[:, :, None], c_f[:, None, :]
    rows_d, cols_d = a_d[:, :, None], a_d[:, None, :]
    o_a, o_a32, lse_a = _attn_fwd(qa_n, ka_n, va_b, rows_f, cols_f, dilated=False, name="attn_fox_fwd")
    token = rest_pass("mid", o_a)
    rows_d = rows_d + token[0, 0]
    o_b, o_b32, lse_b = _attn_fwd(qb_n, kb_n, vb_b, rows_d, cols_d, dilated=True, name="attn_dil_fwd")
    wg = dict(wg, **rest_wait("mid", o_b))
    token = rest_pass("late", o_b)
    pa = _mm(o_a, wg["brf"], mode="nn", b_kind="col", tm=tm, tn=t_dq, tk=w_att, name="mm_brf", deps=(token,))
    pb = _mm(o_b, wg["brd"], mode="nn", b_kind="col", tm=tm, tn=t_dq, tk=w_att, name="mm_brd")
    merged = _gate_fwd(ga, gb, pa, pb)
    x1 = _mm(merged, wg["out"], mode="nn", b_kind="row", res=x, tm=tm, tn=t_d, tk=t_dq, name="mm_out")
    wg = dict(wg, **rest_wait("late", x1))
    h2 = _norm_fwd(x1, small["g_ffn"], group=d, name="rms2_fwd")
    u = _mm(h2, wg["up"], mode="nn", b_kind="col", tm=tm_wide, tn=t_up, tk=d, name="mm_up")
    act = _conv_glu_fwd(u, wg["conv"], wg["bconv"])
    y = _mm(act, wg["down"], mode="nn", b_kind="row", res=x1, tm=tm, tn=t_d2, tk=t_fq, name="mm_down")
    dy_f, dy_b, loss_blk = _loss_head(y, target)

    d_act = _mm(dy_b, wg["down"], mode="nt", b_kind="row", tm=tm, tn=t_fq, tk=d, name="mm_down_dx")
    g_down = _mm(act, dy_b, mode="tn", out_dtype=BF16, out_kind="row", tm=t_fq, tn=t_d2, tk=s, name="mm_down_dw")
    tok = red.swap("down", [g_down], g_down)
    du_g, du_v, st_g, st_v = _conv_glu_bwd(u, d_act, wg["conv"] + tok[0, 0], wg["bconv"])
    tok = red.to_chips("down", du_g)
    du = jnp.concatenate([du_g, du_v], axis=1)
    g_up = _mm(h2, du, mode="tn", out_dtype=BF16, out_kind="col", tm=t_d2, tn=t_up, tk=s, name="mm_up_dw", deps=(tok,))
    tok = red.to_core("down", g_up)
    tok2 = red.swap("up", [g_up], g_up)
    dh2 = _mm(du, wg["up"], mode="nt", b_kind="col", tm=tm, tn=t_d2, tk=t_up, name="mm_up_dx", deps=(tok, tok2))
    tok = red.to_chips("up", dh2)
    dx1_b, dx1_f, dg_ffn = _norm_bwd(dh2, x1, small["g_ffn"], group=d, res=dy_f, out_dtypes=(BF16, F32), name="rms2_bwd")
    d_merged = _mm(dx1_b, wg["out"], mode="nt", b_kind="row", tm=tm, tn=t_dq, tk=d, name="mm_out_dx", deps=(tok,))
    g_out = _mm(merged, dx1_b, mode="tn", out_dtype=BF16, out_kind="row", tm=t_dq, tn=t_d2, tk=s, name="mm_out_dw")
    dpa, dpb, dga, dgb = _gate_bwd(d_merged, ga, gb, pa, pb)
    do_a = _mm(dpa, wg["brf"], mode="nt", b_kind="col", out_dtype=BF16, tm=s, tn=w_att, tk=t_dq, name="mm_brf_dx")
    do_b = _mm(dpb, wg["brd"], mode="nt", b_kind="col", out_dtype=BF16, tm=s, tn=w_att, tk=t_dq, name="mm_brd_dx")
    g_brf = _mm(o_a, dpa, mode="tn", out_dtype=BF16, out_kind="col", tm=w_att, tn=t_dq, tk=s, name="mm_brf_dw")
    g_brd = _mm(o_b, dpb, mode="tn", out_dtype=BF16, out_kind="col", tm=w_att, tn=t_dq, tk=s, name="mm_brd_dw")
    tok = red.swap("mix", [g_out, g_brf, g_brd], g_brd)
    dqa_n, dka_n, dva, dac_a = _attn_bwd(qa_n, ka_n, va_b, o_a32, do_a, lse_a, rows_f + tok[0, 0], cols_f, dilated=False, name="attn_fox_bwd")
    tok = red.to_core("up", dqa_n)
    tok2 = red.to_chips("mix", dqa_n)
    dqb_n, dkb_n, dvb, _ = _attn_bwd(qb_n, kb_n, vb_b, o_b32, do_b, lse_b, rows_d + (tok[0, 0] + tok2[0, 0]), cols_d, dilated=True, name="attn_dil_bwd")
    tok = red.to_core("mix", dqb_n)
    dfa_t, db_f = _forget_bwd(dac_a[:, 0, :], fa_t, b_f)
    dproj_p, dgains = _dproj_merge(
        [dqa_n, dka_n, dva, dfa_t.T, dqb_n, dkb_n, dvb, dga, dgb], in_splits, cs, cp,
        {0: (qa, gains["g_q_fox"]), 1: (ka, gains["g_k_fox"]), 4: (qb, gains["g_q_dil"]), 5: (kb, gains["g_k_dil"])})
    dg_qf, dg_kf, dg_qd, dg_kd = dgains[0], dgains[1], dgains[4], dgains[5]
    g_in = _mm(h1, dproj_p, mode="tn", out_dtype=BF16, out_kind="col", tm=t_d2, tn=t_in, tk=s, name="mm_in_dw", deps=(tok,))
    tok = red.swap("in", [g_in], g_in)
    dh1 = _mm(dproj_p, wg["in"], mode="nt", b_kind="col", tm=tm, tn=t_d2, tk=t_in, name="mm_in_dx", deps=(tok,))
    tok = red.to_chips("in", dh1)
    grad_x, dg_attn = _norm_bwd(dh1, x, small["g_attn"], group=d, res=dx1_f, out_dtypes=(F32,), name="rms1_bwd")

    small_grads = {
        "g_attn": dg_attn, "b_forget": db_f.reshape(1, N_HEADS),
        "g_q_fox": dg_qf, "g_k_fox": dg_kf, "g_q_dil": dg_qd, "g_k_dil": dg_kd, "g_ffn": dg_ffn,
        "w_conv": jnp.concatenate([st_g[0:3], st_v[0:3]], axis=1),
        "b_conv": jnp.concatenate([st_g[3:4], st_v[3:4]], axis=1),
        "loss": loss_blk[0:1, 0:1],
    }
    return small_grads, grad_x, tok


SMALL_ORDER = ("g_attn", "b_forget", "g_q_fox", "g_k_fox", "g_q_dil", "g_k_dil", "g_ffn", "w_conv", "b_conv", "loss")
WEIGHT_ORDER = ("g_attn", "w_in", "b_forget", "g_q_fox", "g_k_fox", "g_q_dil", "g_k_dil", "w_br_fox", "w_br_dil",
                "w_out", "g_ffn", "w_up", "w_conv", "b_conv", "w_down")
BIG = {"w_in": "in", "w_br_fox": "brf", "w_br_dil": "brd", "w_out": "out", "w_up": "up", "w_down": "down"}


def kernel(x, g_attn, w_in, b_forget, g_q_fox, g_k_fox, g_q_dil, g_k_dil, w_br_fox, w_br_dil, w_out, g_ffn, w_up, w_conv, b_conv, w_down, loss_target, m_g_attn, m_w_in, m_b_forget, m_g_q_fox, m_g_k_fox, m_g_q_dil, m_g_k_dil, m_w_br_fox, m_w_br_dil, m_w_out, m_g_ffn, m_w_up, m_w_conv, m_b_conv, m_w_down, v_g_attn, v_w_in, v_b_forget, v_g_q_fox, v_g_k_fox, v_g_q_dil, v_g_k_dil, v_w_br_fox, v_w_br_dil, v_w_out, v_g_ffn, v_w_up, v_w_conv, v_b_conv, v_w_down):
    w = dict(g_attn=g_attn, w_in=w_in, b_forget=b_forget, g_q_fox=g_q_fox, g_k_fox=g_k_fox, g_q_dil=g_q_dil,
             g_k_dil=g_k_dil, w_br_fox=w_br_fox, w_br_dil=w_br_dil, w_out=w_out, g_ffn=g_ffn, w_up=w_up,
             w_conv=w_conv, b_conv=b_conv, w_down=w_down)
    m = dict(g_attn=m_g_attn, w_in=m_w_in, b_forget=m_b_forget, g_q_fox=m_g_q_fox, g_k_fox=m_g_k_fox,
             g_q_dil=m_g_q_dil, g_k_dil=m_g_k_dil, w_br_fox=m_w_br_fox, w_br_dil=m_w_br_dil, w_out=m_w_out,
             g_ffn=m_g_ffn, w_up=m_w_up, w_conv=m_w_conv, b_conv=m_b_conv, w_down=m_w_down)
    v = dict(g_attn=v_g_attn, w_in=v_w_in, b_forget=v_b_forget, g_q_fox=v_g_q_fox, g_k_fox=v_g_k_fox,
             g_q_dil=v_g_q_dil, g_k_dil=v_g_k_dil, w_br_fox=v_w_br_fox, w_br_dil=v_w_br_dil, w_out=v_w_out,
             g_ffn=v_g_ffn, w_up=v_w_up, w_conv=v_w_conv, b_conv=v_b_conv, w_down=v_w_down)
    xi, yi, ci = lax.axis_index("x"), lax.axis_index("y"), lax.axis_index("c")
    chip = (2 * xi + yi).astype(jnp.int32)
    c_idx = ci.astype(jnp.int32).reshape(1)
    j_idx = chip.reshape(1)

    cs = w_in.shape[2]
    cp = _round_up(cs, LANES)
    shards = {
        "in": jnp.pad(w_in[0].astype(BF16), ((0, 0), (0, cp - cs))),
        "brf": w_br_fox[0].astype(BF16), "brd": w_br_dil[0].astype(BF16), "out": w_out[0].astype(BF16),
        "up": w_up[0].astype(BF16), "down": w_down[0].astype(BF16),
    }
    names = tuple(shards)
    conv_pad = jnp.pad(w_conv[0], ((0, 8 - w_conv.shape[1]), (0, 0)))
    first = [(shards["in"], True), (conv_pad, False)]
    later = {"mid": ("brf", "brd", "out"), "late": ("up", "down")}
    groups = {key: [(shards[n], True) for n in members] for key, members in later.items()}
    (started_first, *started_later), token = _gather_start([first, *groups.values()])
    started = dict(zip(later, started_later))
    w2, m2, v2 = ({n: a[n].reshape(a[n].shape[-2], a[n].shape[-1]) for n in BIG} for a in (w, m, v))
    one = 1.0 + token[0, 0]
    m2["w_in"], v2["w_in"] = m2["w_in"] * one, v2["w_in"] * one
    early = (token, m2["w_in"], v2["w_in"])
    own_first, passed_first, token = _gather_pass(first, started_first, early, "gather_pass_in")
    land_in, land_conv = _gather_wait(first, passed_first, token, "gather_wait_in")
    conv_all = _own_slab(land_conv, own_first[1])
    wg = {"in": _own_slab(land_in, own_first[0]), "bconv": b_conv,
          "conv": jnp.transpose(conv_all[:, :w_conv.shape[1], :], (1, 0, 2)).reshape(w_conv.shape[1], -1)}
    small = {n: w[n] for n in ("g_attn", "b_forget", "g_q_fox", "g_k_fox", "g_q_dil", "g_k_dil", "g_ffn")}
    small = {n: (a[0] if a.ndim == 3 else a) for n, a in small.items()}
    in_flight = {}

    def rest_pass(key, after):
        own, passed, tok = _gather_pass(groups[key], started[key], (after,), "gather_pass_" + key)
        in_flight[key] = (own, passed)
        return tok

    def rest_wait(key, after):
        own, passed = in_flight.pop(key)
        lands = _gather_wait(groups[key], passed, after, "gather_wait_" + key)
        return {n: _own_slab(land, o) for n, land, o in zip(later[key], lands, own)}

    reducer = _Reducer(jnp.stack([chip, ci.astype(jnp.int32)]))
    small_grads, grad_x, tok_in = _layer_grads(x[0], loss_target[0], small, wg, rest_pass, rest_wait, reducer)

    mine, theirs = {}, {}
    for key, members in (("down", ("down",)), ("up", ("up",)), ("mix", ("out", "brf", "brd"))):
        mine_k, theirs_k = reducer.finish(key, grad_x)
        mine.update(zip(members, mine_k))
        theirs.update(zip(members, theirs_k))

    flat = jnp.concatenate([small_grads[n].reshape(-1) for n in SMALL_ORDER])
    rows = _round_up(flat.shape[0], 8 * LANES) // LANES
    pack = jnp.pad(flat, (0, rows * LANES - flat.shape[0])).reshape(rows, LANES)
    total = _sum_devices(_gather_packs(pack)).reshape(-1)
    red, at = {}, 0
    for n in SMALL_ORDER:
        size = small_grads[n].size
        red[n] = total[at:at + size].reshape(small_grads[n].shape)
        at += size
    loss = red["loss"].reshape(())
    c2 = w_conv.shape[2]
    red["w_conv"] = lax.dynamic_slice_in_dim(red["w_conv"], chip * c2, c2, axis=1)

    g_out, d_out, m_out, v_out = {}, {}, {}, {}
    last = [n for n in WEIGHT_ORDER if n != "w_in"] + ["w_in"]
    for n in last:
        shape = w[n].shape
        r2 = (shape[-2], shape[-1]) if n not in ("g_attn", "b_forget", "g_ffn", "b_conv") else (1, shape[-1])
        if n == "w_in":
            done = jnp.stack([v_out[k][(0,) * v_out[k].ndim] for k in last[:-1]])
            tok = reducer.to_core("in", done)
            (mine_in,), (theirs_in,) = reducer.finish("in", tok)
            mine["in"], theirs["in"] = mine_in[:, :cs], theirs_in[:, :cs]
        if n in BIG:
            g2, dl, mn, vn = _adamw_halves(w2[n], mine[BIG[n]], theirs[BIG[n]], c_idx, m2[n], v2[n],
                                           name="adamw_" + n, deps=(tok_in,))
        else:
            g2 = red[n].reshape(r2)
            dl, mn, vn = _adamw(w[n].reshape(r2), g2, m[n].reshape(r2), v[n].reshape(r2), name="adamw_" + n,
                                deps=(tok_in,))
        g_out[n], d_out[n], m_out[n], v_out[n] = (a.reshape(shape) for a in (g2, dl, mn, vn))

    return (loss, grad_x[None], *[g_out[n] for n in WEIGHT_ORDER], *[d_out[n] for n in WEIGHT_ORDER],
            *[m_out[n] for n in WEIGHT_ORDER], *[v_out[n] for n in WEIGHT_ORDER])
```

```python
import functools
import math

import jax
import jax.numpy as jnp
import numpy as np
from jax import lax
from jax.experimental import pallas as pl
from jax.experimental.pallas import tpu as pltpu

F32 = jnp.float32
BF16 = jnp.bfloat16
HEAD_DIM = 128
N_HEADS = 8
EPS = 1e-6
NEG = -1e30
N_CHIPS = 4
N_DEV = 8
LANES = 128
VMEM_LIMIT_BYTES = 56 * 1024 * 1024
DIL_PATTERNS = ((128, 1), (512, 4), (2048, 16))
ATTN_TILE = 512
ADAM_LR, ADAM_B1, ADAM_B2, ADAM_EPS, ADAM_WD, ADAM_STEP = 0.001, 0.9, 0.999, 1e-08, 0.01, 10
MESH = pl.DeviceIdType.MESH


def _params(*sem):
    return pltpu.CompilerParams(dimension_semantics=sem, vmem_limit_bytes=VMEM_LIMIT_BYTES)


def _round_up(n, m):
    return -(-n // m) * m


def _pick(dim, prefs):
    for p in prefs:
        if dim % p == 0:
            return p
    raise ValueError(f"no tile for {dim} in {prefs}")


def _logical_shape(arr, kind):
    if kind is None:
        return arr.shape
    s, r, c = arr.shape
    return (r, s * c) if kind == "col" else (s * r, c)


def _spec(shape, kind, br, bc, fi, fj):
    if kind is None:
        return pl.BlockSpec((br, bc), lambda *g: (fi(*g), fj(*g)))
    _, r, c = shape
    if kind == "col":
        nb = c // bc
        assert nb * bc == c, (shape, bc)
        return pl.BlockSpec((None, br, bc), lambda *g: (fj(*g) // nb, fi(*g), fj(*g) % nb))
    nb = r // br
    assert nb * br == r, (shape, br)
    return pl.BlockSpec((None, br, bc), lambda *g: (fi(*g) // nb, fi(*g) % nb, fj(*g)))


def _mm(a, b, *, mode, tm, tn, tk, name, a_kind=None, b_kind=None, out_kind=None,
        out_dtype=F32, res=None, deps=()):
    la, lb = _logical_shape(a, a_kind), _logical_shape(b, b_kind)
    if mode == "nn":
        (m, k), (k2, n) = la, lb
    elif mode == "nt":
        (m, k), (n, k2) = la, lb
    else:
        (k, m), (k2, n) = la, lb
    assert k == k2, (name, la, lb)
    assert m % tm == 0 and n % tn == 0 and k % tk == 0, (name, m, n, k, tm, tn, tk)
    nk = k // tk
    im = lambda i, j, l: i
    jn = lambda i, j, l: j
    lk = lambda i, j, l: l
    if mode == "tn":
        a_spec = _spec(a.shape, a_kind, tk, tm, lk, im)
        dims = (((0,), (0,)), ((), ()))
    else:
        a_spec = _spec(a.shape, a_kind, tm, tk, im, lk)
        dims = (((1,), (1,)), ((), ())) if mode == "nt" else (((1,), (0,)), ((), ()))
    if mode == "nt":
        b_spec = _spec(b.shape, b_kind, tn, tk, jn, lk)
    else:
        b_spec = _spec(b.shape, b_kind, tk, tn, lk, jn)
    if out_kind is None:
        oshape = (m, n)
    elif out_kind == "col":
        oshape = (N_CHIPS, m, n // N_CHIPS)
    else:
        oshape = (N_CHIPS, m // N_CHIPS, n)
    o_spec = _spec(oshape, out_kind, tm, tn, im, jn)
    in_specs = [a_spec, b_spec]
    args = [a, b]
    if res is not None:
        in_specs.append(pl.BlockSpec((tm, tn), lambda i, j, l: (i, j)))
        args.append(res)
    in_specs += [pl.BlockSpec(memory_space=pl.ANY)] * len(deps)
    args += list(deps)

    def finish(out, res_ref, o_ref):
        if res_ref is not None:
            out = out + res_ref[...]
        o_ref[...] = out.astype(o_ref.dtype)

    def body_whole_k(*refs):
        res_ref = refs[2] if res is not None else None
        finish(lax.dot_general(refs[0][...], refs[1][...], dims, preferred_element_type=F32), res_ref, refs[-1])

    def body(*refs):
        a_ref, b_ref = refs[0], refs[1]
        res_ref = refs[2] if res is not None else None
        o_ref, acc_ref = refs[-2], refs[-1]
        step = pl.program_id(2)

        @pl.when(step == 0)
        def _():
            acc_ref[...] = jnp.zeros_like(acc_ref)

        acc_ref[...] += lax.dot_general(a_ref[...], b_ref[...], dims, preferred_element_type=F32)

        @pl.when(step == nk - 1)
        def _():
            finish(acc_ref[...], res_ref, o_ref)

    return pl.pallas_call(
        body_whole_k if nk == 1 else body, name=name, grid=(m // tm, n // tn, nk),
        in_specs=in_specs, out_specs=o_spec,
        out_shape=jax.ShapeDtypeStruct(oshape, out_dtype),
        scratch_shapes=[] if nk == 1 else [pltpu.VMEM((tm, tn), F32)],
        compiler_params=_params("parallel", "parallel", "arbitrary"),
    )(*args)


def _pieces(splits, cs, cp):
    out, g0 = [], 0
    for width in splits:
        g1, runs = g0 + width, []
        for j in range(N_CHIPS):
            a, b = max(g0, cs * j), min(g1, cs * (j + 1))
            if a < b:
                runs.append((j * cp + a - cs * j, a - g0, b - a))
        out.append(runs)
        g0 = g1
    return out


def _head_norm(xv, gv):
    r = lax.rsqrt(jnp.mean(xv * xv, axis=-1, keepdims=True) + EPS)
    return (xv * r) * gv


def _head_norm_bwd(dyv, xv, gv):
    r = lax.rsqrt(jnp.mean(xv * xv, axis=-1, keepdims=True) + EPS)
    xr = xv * r
    gdy = dyv * gv
    return r * (gdy - xr * jnp.mean(gdy * xr, axis=-1, keepdims=True)), jnp.sum(dyv * xr, axis=0, keepdims=True)


def _proj_split(proj_p, splits, cs, dtypes, gains, tm=128):
    s, wp = proj_p.shape
    pieces = _pieces(splits, cs, wp // N_CHIPS)
    normed = sorted(gains)
    nseg = len(splits)

    def body(p_ref, *refs):
        g_refs, o_refs, n_refs = refs[:len(normed)], refs[len(normed):len(normed) + nseg], refs[len(normed) + nseg:]
        for o_ref, runs in zip(o_refs, pieces):
            for src, dst, n in runs:
                o_ref[:, dst:dst + n] = p_ref[:, src:src + n].astype(o_ref.dtype)
        for g_ref, n_ref, i in zip(g_refs, n_refs, normed):
            for c0 in range(0, splits[i], HEAD_DIM):
                cols = slice(c0, c0 + HEAD_DIM)
                n_ref[:, cols] = _head_norm(o_refs[i][:, cols], g_ref[:, cols]).astype(n_ref.dtype)

    return pl.pallas_call(
        body, name="proj_split", grid=(s // tm,),
        in_specs=[pl.BlockSpec((tm, wp), lambda i: (i, 0))] + [pl.BlockSpec((1, splits[i]), lambda i: (0, 0)) for i in normed],
        out_specs=[pl.BlockSpec((tm, w), lambda i: (i, 0)) for w in splits]
        + [pl.BlockSpec((tm, splits[i]), lambda i: (i, 0)) for i in normed],
        out_shape=[jax.ShapeDtypeStruct((s, w), dt) for w, dt in zip(splits, dtypes)]
        + [jax.ShapeDtypeStruct((s, splits[i]), BF16) for i in normed],
        compiler_params=_params("parallel"),
    )(proj_p, *[gains[i] for i in normed])


def _dproj_merge(parts, splits, cs, cp, norms, tm=128):
    s = parts[0].shape[0]
    wp = N_CHIPS * cp
    pieces = _pieces(splits, cs, cp)
    normed = sorted(norms)
    nseg, nn = len(splits), len(normed)

    def body(*refs):
        p_refs, x_refs, g_refs = refs[:nseg], refs[nseg:nseg + nn], refs[nseg + nn:nseg + 2 * nn]
        o_ref, dg_refs = refs[nseg + 2 * nn], refs[nseg + 2 * nn + 1:nseg + 3 * nn + 1]
        stage, tmp = refs[-2], refs[-1]

        @pl.when(pl.program_id(0) == 0)
        def _():
            for dg_ref in dg_refs:
                dg_ref[...] = jnp.zeros_like(dg_ref)

        for j in range(N_CHIPS):
            stage[:, j * cp + cs:(j + 1) * cp] = jnp.zeros((tm, cp - cs), F32)
        for i, (p_ref, runs) in enumerate(zip(p_refs, pieces)):
            src_ref = p_ref
            if i in norms:
                k = normed.index(i)
                for c0 in range(0, splits[i], HEAD_DIM):
                    cols = slice(c0, c0 + HEAD_DIM)
                    dx, dg = _head_norm_bwd(p_ref[:, cols].astype(F32), x_refs[k][:, cols], g_refs[k][:, cols])
                    tmp[:, cols] = dx
                    dg_refs[k][:, cols] += dg
                src_ref = tmp
            for dst, src, n in runs:
                stage[:, dst:dst + n] = src_ref[:, src:src + n].astype(F32)
        o_ref[...] = stage[...].astype(o_ref.dtype)

    wmax = max(splits[i] for i in normed)
    row = lambda w: pl.BlockSpec((tm, w), lambda i: (i, 0))
    vec = lambda w: pl.BlockSpec((1, w), lambda i: (0, 0))
    outs = pl.pallas_call(
        body, name="dproj_merge", grid=(s // tm,),
        in_specs=[row(w) for w in splits] + [row(splits[i]) for i in normed] + [vec(splits[i]) for i in normed],
        out_specs=[row(wp)] + [vec(splits[i]) for i in normed],
        out_shape=[jax.ShapeDtypeStruct((s, wp), BF16)] + [jax.ShapeDtypeStruct((1, splits[i]), F32) for i in normed],
        scratch_shapes=[pltpu.VMEM((tm, wp), F32), pltpu.VMEM((tm, wmax), F32)],
        compiler_params=_params("arbitrary"),
    )(*parts, *[norms[i][0] for i in normed], *[norms[i][1] for i in normed])
    return outs[0], dict(zip(normed, outs[1:]))


def _norm_fwd(x, g, *, group, name, tm=256):
    s, w = x.shape
    ng = w // group

    def body(x_ref, g_ref, o_ref):
        for i in range(ng):
            cols = slice(i * group, (i + 1) * group)
            xv = x_ref[:, cols]
            r = lax.rsqrt(jnp.mean(xv * xv, axis=-1, keepdims=True) + EPS)
            o_ref[:, cols] = ((xv * r) * g_ref[:, cols]).astype(o_ref.dtype)

    return pl.pallas_call(
        body, name=name, grid=(s // tm,),
        in_specs=[pl.BlockSpec((tm, w), lambda i: (i, 0)), pl.BlockSpec((1, w), lambda i: (0, 0))],
        out_specs=pl.BlockSpec((tm, w), lambda i: (i, 0)),
        out_shape=jax.ShapeDtypeStruct((s, w), BF16),
        compiler_params=_params("parallel"),
    )(x, g)


def _norm_bwd(dy, x, g, *, group, name, res=None, out_dtypes=(BF16,), tm=256):
    s, w = x.shape
    ng = w // group
    n_in = 4 if res is not None else 3

    def body(*refs):
        dy_ref, x_ref, g_ref = refs[:3]
        res_ref = refs[3] if res is not None else None
        outs = refs[n_in:]
        dx_refs, dg_ref = outs[:-1], outs[-1]

        @pl.when(pl.program_id(0) == 0)
        def _():
            dg_ref[...] = jnp.zeros_like(dg_ref)

        for i in range(ng):
            cols = slice(i * group, (i + 1) * group)
            xv = x_ref[:, cols]
            dyv = dy_ref[:, cols].astype(F32)
            r = lax.rsqrt(jnp.mean(xv * xv, axis=-1, keepdims=True) + EPS)
            xr = xv * r
            dg_ref[:, cols] += jnp.sum(dyv * xr, axis=0, keepdims=True)
            gdy = dyv * g_ref[:, cols]
            dx = r * (gdy - xr * jnp.mean(gdy * xr, axis=-1, keepdims=True))
            if res_ref is not None:
                dx = dx + res_ref[:, cols]
            for dx_ref in dx_refs:
                dx_ref[:, cols] = dx.astype(dx_ref.dtype)

    row = pl.BlockSpec((tm, w), lambda i: (i, 0))
    vec = pl.BlockSpec((1, w), lambda i: (0, 0))
    in_specs = [row, row, vec] + ([row] if res is not None else [])
    args = [dy, x, g] + ([res] if res is not None else [])
    out_specs = [row] * len(out_dtypes) + [vec]
    out_shape = [jax.ShapeDtypeStruct((s, w), dt) for dt in out_dtypes] + [jax.ShapeDtypeStruct((1, w), F32)]
    return pl.pallas_call(
        body, name=name, grid=(s // tm,), in_specs=in_specs, out_specs=out_specs,
        out_shape=out_shape, compiler_params=_params("arbitrary"),
    )(*args)


def _split3(v):
    p1 = v.astype(BF16)
    r1 = v - p1.astype(F32)
    p2 = r1.astype(BF16)
    p3 = (r1 - p2.astype(F32)).astype(BF16)
    return p1, p2, p3


def _tri_sum(v, reverse, tcol=512):
    h, s = v.shape
    tcol = min(tcol, s)
    parts = _split3(v)
    outs = []
    for j in range(s // tcol):
        src = lax.broadcasted_iota(jnp.int32, (s, tcol), 0)
        dst = lax.broadcasted_iota(jnp.int32, (s, tcol), 1) + j * tcol
        keep = (src >= dst) if reverse else (src <= dst)
        tri = jnp.where(keep, 1.0, 0.0).astype(BF16)
        acc = jnp.zeros((h, tcol), F32)
        for p in parts:
            acc = acc + jnp.dot(p, tri, preferred_element_type=F32)
        outs.append(acc)
    return outs


def _forget_fwd(fa_t, b):
    h, s = fa_t.shape
    tcol = min(512, s)

    def body(f_ref, b_ref, c_ref):
        z = f_ref[...] + b_ref[...]
        logf = jnp.minimum(z, 0.0) - jnp.log(1.0 + jnp.exp(-jnp.abs(z)))
        for j, blk in enumerate(_tri_sum(logf, reverse=False, tcol=tcol)):
            c_ref[:, j * tcol:(j + 1) * tcol] = blk

    return pl.pallas_call(
        body, name="forget_fwd", out_shape=jax.ShapeDtypeStruct((h, s), F32),
        compiler_params=_params(),
    )(fa_t, b)


def _forget_bwd(dacol, fa_t, b):
    h, s = fa_t.shape
    tcol = min(512, s)

    def body(d_ref, f_ref, b_ref, dfa_ref, db_ref):
        z = f_ref[...] + b_ref[...]
        dc = -d_ref[...]
        total = jnp.zeros((h, 1), F32)
        for j, blk in enumerate(_tri_sum(dc, reverse=True, tcol=tcol)):
            cols = slice(j * tcol, (j + 1) * tcol)
            dfa = blk * (1.0 - jax.nn.sigmoid(z[:, cols]))
            dfa_ref[:, cols] = dfa
            total = total + jnp.sum(dfa, axis=-1, keepdims=True)
        db_ref[...] = total

    return pl.pallas_call(
        body, name="forget_bwd",
        out_shape=[jax.ShapeDtypeStruct((h, s), F32), jax.ShapeDtypeStruct((h, 1), F32)],
        compiler_params=_params(),
    )(dacol, fa_t, b)


def _distance_bias(s, tile, dilated):
    nb = s // tile
    b = lax.broadcasted_iota(jnp.int32, (nb, tile, tile), 0)
    dist = b * tile + lax.broadcasted_iota(jnp.int32, (nb, tile, tile), 1) - lax.broadcasted_iota(jnp.int32, (nb, tile, tile), 2)
    if not dilated:
        return jnp.where(dist >= 0, 0.0, NEG).astype(F32)
    mult = jnp.zeros(dist.shape, jnp.int32)
    for window, dil in DIL_PATTERNS:
        mult = mult + ((dist >= 0) & (dist <= window) & ((dist & (dil - 1)) == 0)).astype(jnp.int32)
    logm = jnp.where(mult == 3, math.log(3.0), jnp.where(mult == 2, math.log(2.0), 0.0))
    return jnp.where(mult > 0, logm, NEG).astype(F32)


def _logits(q, k, arow, acol, bias):
    s = lax.dot_general(q, k, (((1,), (1,)), ((), ())), preferred_element_type=F32)
    return s * (1.0 / math.sqrt(HEAD_DIM)) + arow - acol + bias


def _attn_fwd(q, k, v, arow, acol, *, dilated, name, tq=ATTN_TILE, tk=ATTN_TILE):
    two_term = not dilated
    s, w = q.shape
    nh = w // HEAD_DIM
    assert tq == tk
    tq = tk = min(tq, s)
    nq, nk = s // tq, s // tk

    def body(q_ref, k_ref, v_ref, ar_ref, ac_ref, b_ref, o_ref, of_ref, lse_ref, m_ref, l_ref, acc_ref):
        qi, ki = pl.program_id(1), pl.program_id(2)

        @pl.when(ki == 0)
        def _():
            m_ref[...] = jnp.full_like(m_ref, NEG)
            l_ref[...] = jnp.zeros_like(l_ref)
            acc_ref[...] = jnp.zeros_like(acc_ref)

        @pl.when(ki <= qi)
        def _():
            sc = _logits(q_ref[...], k_ref[...], ar_ref[...], ac_ref[...], b_ref[...])
            m_new = jnp.maximum(m_ref[...], jnp.max(sc, axis=-1, keepdims=True))
            alpha = jnp.exp(m_ref[...] - m_new)
            p = jnp.exp(sc - m_new)
            l_ref[...] = alpha * l_ref[...] + jnp.sum(p, axis=-1, keepdims=True)
            p_hi = p.astype(BF16)
            vv = v_ref[...]
            pv = jnp.dot(p_hi, vv, preferred_element_type=F32)
            if two_term:
                pv = pv + jnp.dot((p - p_hi.astype(F32)).astype(BF16), vv, preferred_element_type=F32)
            acc_ref[...] = alpha * acc_ref[...] + pv
            m_ref[...] = m_new

        @pl.when(ki == nk - 1)
        def _():
            out = acc_ref[...] / l_ref[...]
            o_ref[...] = out.astype(o_ref.dtype)
            of_ref[...] = out
            lse_ref[...] = m_ref[...] + jnp.log(l_ref[...])

    kv = pl.BlockSpec((tk, HEAD_DIM), lambda h, i, j: (jnp.minimum(j, i), h))
    return pl.pallas_call(
        body, name=name, grid=(nh, nq, nk),
        in_specs=[pl.BlockSpec((tq, HEAD_DIM), lambda h, i, j: (i, h)), kv, kv,
                  pl.BlockSpec((None, tq, 1), lambda h, i, j: (h, i, 0)),
                  pl.BlockSpec((None, 1, tk), lambda h, i, j: (h, 0, jnp.minimum(j, i))),
                  pl.BlockSpec((None, tq, tk), lambda h, i, j: (jnp.maximum(i - j, 0), 0, 0))],
        out_specs=[pl.BlockSpec((tq, HEAD_DIM), lambda h, i, j: (i, h)),
                   pl.BlockSpec((tq, HEAD_DIM), lambda h, i, j: (i, h)),
                   pl.BlockSpec((None, tq, 1), lambda h, i, j: (h, i, 0))],
        out_shape=[jax.ShapeDtypeStruct((s, w), BF16), jax.ShapeDtypeStruct((s, w), F32),
                   jax.ShapeDtypeStruct((nh, s, 1), F32)],
        scratch_shapes=[pltpu.VMEM((tq, 1), F32), pltpu.VMEM((tq, 1), F32), pltpu.VMEM((tq, HEAD_DIM), F32)],
        compiler_params=_params("parallel", "parallel", "arbitrary"),
    )(q, k, v, arow, acol, _distance_bias(s, tq, dilated))


def _attn_bwd(q, k, v, o, do, lse, arow, acol, *, dilated, name, tq=ATTN_TILE, tk=ATTN_TILE):
    s, w = q.shape
    nh = w // HEAD_DIM
    assert tq == tk
    tq = tk = min(tq, s)
    nq, nk = s // tq, s // tk
    scale = 1.0 / math.sqrt(HEAD_DIM)

    def body(q_ref, k_ref, v_ref, o_ref, do_ref, lse_ref, ar_ref, ac_ref, b_ref,
             dq_ref, dk_ref, dv_ref, dac_ref, dk_acc, dv_acc, dac_acc):
        ki, qi = pl.program_id(1), pl.program_id(2)

        @pl.when((ki == 0) & (qi == 0))
        def _():
            dq_ref[...] = jnp.zeros_like(dq_ref)

        @pl.when(qi == 0)
        def _():
            dk_acc[...] = jnp.zeros_like(dk_acc)
            dv_acc[...] = jnp.zeros_like(dv_acc)
            dac_acc[...] = jnp.zeros_like(dac_acc)

        @pl.when(qi >= ki)
        def _():
            qv, kvv, dov = q_ref[...], k_ref[...], do_ref[...]
            sc = _logits(qv, kvv, ar_ref[...], ac_ref[...], b_ref[...])
            p = jnp.exp(sc - lse_ref[...])
            dp = lax.dot_general(dov, v_ref[...], (((1,), (1,)), ((), ())), preferred_element_type=F32)
            delta = jnp.sum(dov.astype(F32) * o_ref[...].astype(F32), axis=-1, keepdims=True)
            ds = p * (dp - delta)
            dsb = ds.astype(BF16)
            dv_acc[...] += lax.dot_general(p.astype(BF16), dov, (((0,), (0,)), ((), ())), preferred_element_type=F32)
            dk_acc[...] += lax.dot_general(dsb, qv, (((0,), (0,)), ((), ())), preferred_element_type=F32)
            rows = pl.ds(pl.multiple_of(qi * tq, tq), tq)
            dq_ref[rows, :] += jnp.dot(dsb, kvv, preferred_element_type=F32) * scale
            dac_acc[...] += jnp.sum(ds, axis=0, keepdims=True)

        @pl.when(qi == nq - 1)
        def _():
            dk_ref[...] = dk_acc[...] * scale
            dv_ref[...] = dv_acc[...]
            dac_ref[...] = dac_acc[...]

    qs = pl.BlockSpec((tq, HEAD_DIM), lambda h, j, i: (jnp.maximum(i, j), h))
    ks = pl.BlockSpec((tk, HEAD_DIM), lambda h, j, i: (j, h))
    rowv = pl.BlockSpec((None, tq, 1), lambda h, j, i: (h, jnp.maximum(i, j), 0))
    colv = pl.BlockSpec((None, 1, tk), lambda h, j, i: (h, 0, j))
    return pl.pallas_call(
        body, name=name, grid=(nh, nk, nq),
        in_specs=[qs, ks, ks, qs, qs, rowv, rowv, colv,
                  pl.BlockSpec((None, tq, tk), lambda h, j, i: (jnp.maximum(i - j, 0), 0, 0))],
        out_specs=[pl.BlockSpec((s, HEAD_DIM), lambda h, j, i: (0, h)), ks, ks, colv],
        out_shape=[jax.ShapeDtypeStruct((s, w), F32), jax.ShapeDtypeStruct((s, w), F32),
                   jax.ShapeDtypeStruct((s, w), F32), jax.ShapeDtypeStruct((nh, 1, s), F32)],
        scratch_shapes=[pltpu.VMEM((tk, HEAD_DIM), F32), pltpu.VMEM((tk, HEAD_DIM), F32), pltpu.VMEM((1, tk), F32)],
        compiler_params=_params("arbitrary", "arbitrary", "arbitrary"),
    )(q, k, v, o, do, lse, arow, acol, _distance_bias(s, tq, dilated))


def _gate_fwd(ga, gb, pa, pb, tm=256):
    s, d = ga.shape

    def body(ga_ref, gb_ref, pa_ref, pb_ref, o_ref):
        o_ref[...] = (jax.nn.sigmoid(ga_ref[...]) * pa_ref[...]
                      + jax.nn.sigmoid(gb_ref[...]) * pb_ref[...]).astype(o_ref.dtype)

    row = pl.BlockSpec((tm, d), lambda i: (i, 0))
    return pl.pallas_call(
        body, name="gate_fwd", grid=(s // tm,), in_specs=[row] * 4, out_specs=row,
        out_shape=jax.ShapeDtypeStruct((s, d), BF16), compiler_params=_params("parallel"),
    )(ga, gb, pa, pb)


def _gate_bwd(dm, ga, gb, pa, pb, tm=256):
    s, d = ga.shape

    def body(dm_ref, ga_ref, gb_ref, pa_ref, pb_ref, dpa_ref, dpb_ref, dga_ref, dgb_ref):
        dmv = dm_ref[...]
        for g_ref, p_ref, dp_ref, dg_ref in ((ga_ref, pa_ref, dpa_ref, dga_ref), (gb_ref, pb_ref, dpb_ref, dgb_ref)):
            sg = jax.nn.sigmoid(g_ref[...])
            dp_ref[...] = (dmv * sg).astype(BF16)
            dg_ref[...] = (dmv * p_ref[...] * (sg * (1.0 - sg))).astype(BF16)

    row = pl.BlockSpec((tm, d), lambda i: (i, 0))
    return pl.pallas_call(
        body, name="gate_bwd", grid=(s // tm,), in_specs=[row] * 5, out_specs=[row] * 4,
        out_shape=[jax.ShapeDtypeStruct((s, d), BF16)] * 4, compiler_params=_params("parallel"),
    )(dm, ga, gb, pa, pb)


def _shift_down(u, k):
    row = lax.broadcasted_iota(jnp.int32, u.shape, 0)
    return jnp.where(row >= k, pltpu.roll(u, k, 0), 0.0)


def _shift_up(u, k):
    n = u.shape[0]
    row = lax.broadcasted_iota(jnp.int32, u.shape, 0)
    return jnp.where(row < n - k, pltpu.roll(u, n - k, 0), 0.0)


def _conv3(u, wc, b):
    return wc[0:1, :] * _shift_down(u, 2) + wc[1:2, :] * _shift_down(u, 1) + wc[2:3, :] * u + b


def _conv_glu_fwd(u, wc, b, tn=256):
    s, f2 = u.shape
    f = f2 // 2
    nb = f // tn

    def body(ug_ref, uv_ref, wg_ref, wv_ref, bg_ref, bv_ref, o_ref):
        cg = _conv3(ug_ref[...], wg_ref[...], bg_ref[...])
        cv = _conv3(uv_ref[...], wv_ref[...], bv_ref[...])
        o_ref[...] = (cg * jax.nn.sigmoid(cg) * cv).astype(o_ref.dtype)

    def cols(rows, off):
        return pl.BlockSpec((rows, tn), lambda j: (0, j + off))

    return pl.pallas_call(
        body, name="conv_glu_fwd", grid=(nb,),
        in_specs=[cols(s, 0), cols(s, nb), cols(3, 0), cols(3, nb), cols(1, 0), cols(1, nb)],
        out_specs=cols(s, 0), out_shape=jax.ShapeDtypeStruct((s, f), BF16),
        compiler_params=_params("parallel"),
    )(u, u, wc, wc, b, b)


def _conv_glu_bwd(u, da, wc, b, tn=256):
    s, f2 = u.shape
    f = f2 // 2
    nb = f // tn

    def body(ug_ref, uv_ref, da_ref, wg_ref, wv_ref, bg_ref, bv_ref, dug_ref, duv_ref, sg_ref, sv_ref):
        ug, uv, wg, wv = ug_ref[...], uv_ref[...], wg_ref[...], wv_ref[...]
        cg = _conv3(ug, wg, bg_ref[...])
        cv = _conv3(uv, wv, bv_ref[...])
        sig = jax.nn.sigmoid(cg)
        dav = da_ref[...]
        dcv = dav * (cg * sig)
        dcg = dav * cv * (sig * (1.0 + cg * (1.0 - sig)))
        for dc, uu, w, du_ref, st_ref in ((dcg, ug, wg, dug_ref, sg_ref), (dcv, uv, wv, duv_ref, sv_ref)):
            du = w[2:3, :] * dc + w[1:2, :] * _shift_up(dc, 1) + w[0:1, :] * _shift_up(dc, 2)
            du_ref[...] = du.astype(BF16)
            st_ref[...] = jnp.zeros_like(st_ref)
            st_ref[0:1, :] = jnp.sum(dc * _shift_down(uu, 2), axis=0, keepdims=True)
            st_ref[1:2, :] = jnp.sum(dc * _shift_down(uu, 1), axis=0, keepdims=True)
            st_ref[2:3, :] = jnp.sum(dc * uu, axis=0, keepdims=True)
            st_ref[3:4, :] = jnp.sum(dc, axis=0, keepdims=True)

    def cols(rows, off):
        return pl.BlockSpec((rows, tn), lambda j: (0, j + off))

    return pl.pallas_call(
        body, name="conv_glu_bwd", grid=(nb,),
        in_specs=[cols(s, 0), cols(s, nb), cols(s, 0), cols(3, 0), cols(3, nb), cols(1, 0), cols(1, nb)],
        out_specs=[cols(s, 0), cols(s, 0), cols(8, 0), cols(8, 0)],
        out_shape=[jax.ShapeDtypeStruct((s, f), BF16), jax.ShapeDtypeStruct((s, f), BF16),
                   jax.ShapeDtypeStruct((8, f), F32), jax.ShapeDtypeStruct((8, f), F32)],
        compiler_params=_params("parallel"),
    )(u, u, da, wc, wc, b, b)


def _loss_head(y, target, tm=256):
    s, d = y.shape

    def body(y_ref, t_ref, dyf_ref, dyb_ref, l_ref):
        @pl.when(pl.program_id(0) == 0)
        def _():
            l_ref[...] = jnp.zeros_like(l_ref)

        err = y_ref[...] - t_ref[...]
        dy = err * (1.0 / d)
        dyf_ref[...] = dy
        dyb_ref[...] = dy.astype(BF16)
        l_ref[...] += 0.5 * jnp.sum(jnp.sum(err * err, axis=-1, keepdims=True) * (1.0 / d), axis=0, keepdims=True)

    row = pl.BlockSpec((tm, d), lambda i: (i, 0))
    return pl.pallas_call(
        body, name="loss_head", grid=(s // tm,), in_specs=[row, row],
        out_specs=[row, row, pl.BlockSpec((8, LANES), lambda i: (0, 0))],
        out_shape=[jax.ShapeDtypeStruct((s, d), F32), jax.ShapeDtypeStruct((s, d), BF16),
                   jax.ShapeDtypeStruct((8, LANES), F32)],
        compiler_params=_params("arbitrary"),
    )(y, target)


ROW_TILES = (256, 128, 64, 32, 16, 8)
BLOCK_BYTES = 2 << 20


def _add_halves(g, r1, place):
    ns, r, c = g.shape
    rh = r // 2
    tr = _pick(rh, ROW_TILES)
    g4 = g.reshape(ns, 2, rh, c)

    def body(p_ref, g_ref, r_ref, o_ref):
        o_ref[...] = (g_ref[...].astype(F32) + r_ref[...].astype(F32)).astype(o_ref.dtype)

    return pl.pallas_call(
        body, name="add_halves",
        grid_spec=pltpu.PrefetchScalarGridSpec(
            num_scalar_prefetch=1, grid=(ns, rh // tr),
            in_specs=[pl.BlockSpec((None, None, tr, c), lambda s, i, pr: (s, pr[1], i, 0)),
                      pl.BlockSpec((None, tr, c), lambda s, i, pr: (s, i, 0))],
            out_specs=pl.BlockSpec((None, tr, c), lambda s, i, pr: (s, i, 0))),
        out_shape=jax.ShapeDtypeStruct((ns, rh, c), BF16),
        compiler_params=_params("parallel", "parallel"),
    )(place, g4, r1)


def _sum_chips(g, r1, recv, place):
    ns, r, c = g.shape
    rh = r // 2
    tr = _pick(rh, ROW_TILES)
    g4 = g.reshape(ns, 2, rh, c)

    def body(p_ref, g_ref, r_ref, t0_ref, t1_ref, t2_ref, o_ref):
        own = g_ref[...].astype(F32) + r_ref[...].astype(F32)
        o_ref[...] = ((own + t0_ref[...].astype(F32)) + t1_ref[...].astype(F32)) + t2_ref[...].astype(F32)

    def peer(k):
        return pl.BlockSpec((None, tr, c), lambda i, pr: (k, i, 0))

    return pl.pallas_call(
        body, name="sum_chips",
        grid_spec=pltpu.PrefetchScalarGridSpec(
            num_scalar_prefetch=1, grid=(rh // tr,),
            in_specs=[pl.BlockSpec((None, None, tr, c), lambda i, pr: (pr[0], pr[1], i, 0)),
                      pl.BlockSpec((None, tr, c), lambda i, pr: (pr[0], i, 0)), peer(0), peer(1), peer(2)],
            out_specs=pl.BlockSpec((tr, c), lambda i, pr: (i, 0))),
        out_shape=jax.ShapeDtypeStruct((rh, c), F32),
        compiler_params=_params("parallel"),
    )(place, g4, r1, recv, recv, recv)


def _sum_devices(packs):
    n, r, c = packs.shape

    def body(p_ref, o_ref):
        acc = p_ref[0]
        for d in range(1, n):
            acc = acc + p_ref[d]
        o_ref[...] = acc

    return pl.pallas_call(
        body, name="sum_devices", out_shape=jax.ShapeDtypeStruct((r, c), F32), compiler_params=_params(),
    )(packs)


def _adamw_update(wv, gv, mv, vv):
    c1 = 1.0 - ADAM_B1 ** ADAM_STEP
    c2 = 1.0 - ADAM_B2 ** ADAM_STEP
    mn = ADAM_B1 * mv + (1.0 - ADAM_B1) * gv
    vn = ADAM_B2 * vv + (1.0 - ADAM_B2) * (gv * gv)
    m_hat = mn / c1
    v_hat = vn / c2
    return -ADAM_LR * (m_hat / (jnp.sqrt(v_hat) + ADAM_EPS) + ADAM_WD * wv), mn, vn


def _adamw(w, g, m, v, name, deps=()):
    r, c = w.shape
    tr = _pick(r, ROW_TILES) if r >= 8 else r

    def body(w_ref, g_ref, m_ref, v_ref, *rest):
        d_ref, mo_ref, vo_ref = rest[-3:]
        d_ref[...], mo_ref[...], vo_ref[...] = _adamw_update(w_ref[...], g_ref[...], m_ref[...], v_ref[...])

    blk = pl.BlockSpec((tr, c), lambda i: (i, 0))
    return pl.pallas_call(
        body, name=name, grid=(r // tr,), in_specs=[blk] * 4 + [ANY] * len(deps), out_specs=[blk] * 3,
        out_shape=[jax.ShapeDtypeStruct((r, c), F32)] * 3, compiler_params=_params("parallel"),
    )(w, g, m, v, *deps)


def _adamw_halves(w, mine, theirs, c_idx, m, v, name, deps=()):
    r, c = w.shape
    rh = r // 2
    tr = _pick(rh, [t for t in ROW_TILES if t * c * 4 <= BLOCK_BYTES])
    nb = rh // tr

    def body(c_ref, w_ref, a_ref, b_ref, m_ref, v_ref, *rest):
        g_ref, d_ref, mo_ref, vo_ref = rest[-4:]
        gv = jnp.where(pl.program_id(0) // nb == c_ref[0], a_ref[...], b_ref[...])
        g_ref[...] = gv
        d_ref[...], mo_ref[...], vo_ref[...] = _adamw_update(w_ref[...], gv, m_ref[...], v_ref[...])

    blk = pl.BlockSpec((tr, c), lambda i, cr: (i, 0))
    mine_spec = pl.BlockSpec((tr, c), lambda i, cr: (jnp.clip(i - cr[0] * nb, 0, nb - 1), 0))
    theirs_spec = pl.BlockSpec((tr, c), lambda i, cr: (jnp.clip(i - (1 - cr[0]) * nb, 0, nb - 1), 0))
    return pl.pallas_call(
        body, name=name,
        grid_spec=pltpu.PrefetchScalarGridSpec(
            num_scalar_prefetch=1, grid=(r // tr,),
            in_specs=[blk, mine_spec, theirs_spec, blk, blk] + [ANY] * len(deps), out_specs=[blk] * 4),
        out_shape=[jax.ShapeDtypeStruct((r, c), F32)] * 4, compiler_params=_params("arbitrary"),
    )(c_idx, w, mine, theirs, m, v, *deps)


ANY = pl.BlockSpec(memory_space=pl.ANY)


def _place():
    x, y, c = lax.axis_index("x"), lax.axis_index("y"), lax.axis_index("c")
    chips = [(1 - x, y), (x, 1 - y), (1 - x, 1 - y)]
    return x, y, c, chips


def _remote(src, dst, send_sem, recv_sem, to):
    return pltpu.make_async_remote_copy(src_ref=src, dst_ref=dst, send_sem=send_sem, recv_sem=recv_sem,
                                        device_id=to, device_id_type=MESH)


HBM = pl.BlockSpec(memory_space=pltpu.HBM)
SEM = pl.BlockSpec(memory_space=pltpu.SEMAPHORE)
EFFECT = pltpu.SideEffectType.DATAFLOW_SIDE_EFFECTING


def _in_hbm(a):
    return pltpu.with_memory_space_constraint(a, pltpu.HBM)


def _half(ref_rows, who):
    return pl.ds(who * (ref_rows // 2), ref_rows // 2)


def _gather_start(groups):
    items = [it for g in groups for it in g]
    n = len(items)
    sizes = [len(g) for g in groups]

    def body(*refs):
        srcs, lands = refs[:n], refs[n:2 * n]
        sems = refs[2 * n:2 * n + 2 * len(groups)]
        token = refs[-1]
        x, y, c, chips = _place()
        j = 2 * x + y
        at = 0
        for gi, g in enumerate(groups):
            send, recv = sems[2 * gi], sems[2 * gi + 1]
            for i, (shard, split) in enumerate(g):
                src, land = srcs[at], lands[at]
                at += 1
                rows = _half(shard.shape[0], c) if split else slice(None)
                for k, chip in enumerate(chips):
                    _remote(src.at[rows], land.at[j, rows], send.at[3 * i + k], recv.at[3 * i + k], (*chip, c)).start()
        token[...] = jnp.zeros_like(token)

    sem_shapes = []
    for sz in sizes:
        sem_shapes += [pltpu.SemaphoreType.DMA((3 * sz,)), pltpu.SemaphoreType.DMA((3 * sz,))]
    out_shape = (sem_shapes + [pltpu.HBM(sh.shape, sh.dtype) for sh, _ in items]
                 + [pltpu.HBM((N_CHIPS,) + sh.shape, sh.dtype) for sh, _ in items]
                 + [jax.ShapeDtypeStruct((8, LANES), F32)])
    ns = len(sem_shapes)
    outs = pl.pallas_call(
        body, name="gather_start", in_specs=[HBM] * (2 * n),
        out_specs=[SEM] * ns + [HBM] * (2 * n) + [pl.BlockSpec(memory_space=pltpu.VMEM)],
        out_shape=out_shape, input_output_aliases={i: ns + i for i in range(2 * n)},
        compiler_params=pltpu.CompilerParams(has_side_effects=EFFECT),
    )(*[_in_hbm(sh) for sh, _ in items], *[_in_hbm(lax.empty((N_CHIPS,) + sh.shape, sh.dtype)) for sh, _ in items])
    sems, shards, lands, token = outs[:ns], outs[ns:ns + n], outs[ns + n:ns + 2 * n], outs[-1]
    res, at = [], 0
    for gi, sz in enumerate(sizes):
        res.append((shards[at:at + sz], lands[at:at + sz], sems[2 * gi], sems[2 * gi + 1]))
        at += sz
    return res, token


def _gather_pass(group, started, after, name):
    shards, lands, send, recv = started
    n = len(group)
    split_ix = [i for i, (_, split) in enumerate(group) if split]

    def body(*refs):
        lnds, send1, recv1 = refs[n:2 * n], refs[2 * n], refs[2 * n + 1]
        outs = refs[2 * n + 2 + len(after):]
        send2, recv2, token = outs[2 * n], outs[2 * n + 1], outs[2 * n + 2]
        x, y, c, chips = _place()
        sib = (x, y, 1 - c)
        for i, (shard, split) in enumerate(group):
            rows = _half(shard.shape[0], c) if split else slice(None)
            for k, (cx, cy) in enumerate(chips):
                landed = lnds[i].at[2 * cx + cy, rows]
                cp = _remote(landed, landed, send1.at[3 * i + k], recv1.at[3 * i + k], sib)
                cp.wait_send()
                cp.wait_recv()
        for i2, i in enumerate(split_ix):
            rows = _half(group[i][0].shape[0], c)
            for k, (cx, cy) in enumerate(chips):
                landed = lnds[i].at[2 * cx + cy, rows]
                _remote(landed, landed, send2.at[3 * i2 + k], recv2.at[3 * i2 + k], sib).start()
        token[...] = jnp.zeros_like(token)

    n2 = len(split_ix)
    out_shape = ([pltpu.HBM(a.shape, a.dtype) for a in (*shards, *lands)]
                 + [pltpu.SemaphoreType.DMA((3 * n2,)), pltpu.SemaphoreType.DMA((3 * n2,)), jax.ShapeDtypeStruct((8, LANES), F32)])
    outs = pl.pallas_call(
        body, name=name, in_specs=[HBM] * (2 * n) + [SEM, SEM] + [ANY] * len(after),
        out_specs=[HBM] * (2 * n) + [SEM, SEM, pl.BlockSpec(memory_space=pltpu.VMEM)],
        out_shape=out_shape, input_output_aliases={i: i for i in range(2 * n)},
        compiler_params=pltpu.CompilerParams(has_side_effects=EFFECT),
    )(*shards, *lands, send, recv, *after)
    return outs[:n], (outs[n:2 * n], outs[2 * n], outs[2 * n + 1]), outs[2 * n + 2]


def _gather_wait(group, passed, after, name):
    lands, send2, recv2 = passed
    n = len(group)
    split_ix = [i for i, (_, split) in enumerate(group) if split]

    def body(*refs):
        lnds, s2, r2 = refs[:n], refs[n], refs[n + 1]
        x, y, c, chips = _place()
        sib = (x, y, 1 - c)
        for i2, i in enumerate(split_ix):
            rows = _half(group[i][0].shape[0], 1 - c)
            for k, (cx, cy) in enumerate(chips):
                landed = lnds[i].at[2 * cx + cy, rows]
                cp = _remote(landed, landed, s2.at[3 * i2 + k], r2.at[3 * i2 + k], sib)
                cp.wait_send()
                cp.wait_recv()

    return pl.pallas_call(
        body, name=name, in_specs=[HBM] * n + [SEM, SEM, ANY], out_specs=[HBM] * n,
        out_shape=[pltpu.HBM(a.shape, a.dtype) for a in lands], input_output_aliases={i: i for i in range(n)},
        compiler_params=pltpu.CompilerParams(has_side_effects=EFFECT),
    )(*lands, send2, recv2, after)


def _own_slab(land, shard):
    chip = 2 * lax.axis_index("x") + lax.axis_index("y")
    return lax.dynamic_update_slice(land, shard[None], (chip, 0, 0))


def _xfer_start(name, srcs, land_shapes, n_copies, copies, after):
    n = len(srcs)

    def body(*refs):
        src_refs, land_refs = refs[:n], refs[n:2 * n]
        send, recv, token = refs[2 * n + 1], refs[2 * n + 2], refs[-1]
        for cp in copies(src_refs, land_refs, send, recv):
            cp.start()
        token[...] = jnp.zeros_like(token)

    lands = [_in_hbm(lax.empty(shape, dtype)) for shape, dtype in land_shapes]
    out_shape = ([pltpu.SemaphoreType.DMA((n_copies,)), pltpu.SemaphoreType.DMA((n_copies,))]
                 + [pltpu.HBM(a.shape, a.dtype) for a in (*srcs, *lands)] + [jax.ShapeDtypeStruct((8, LANES), F32)])
    outs = pl.pallas_call(
        body, name=name, in_specs=[HBM] * (2 * n) + [ANY],
        out_specs=[SEM, SEM] + [HBM] * (2 * n) + [pl.BlockSpec(memory_space=pltpu.VMEM)],
        out_shape=out_shape, input_output_aliases={i: 2 + i for i in range(2 * n)},
        compiler_params=pltpu.CompilerParams(has_side_effects=EFFECT),
    )(*[_in_hbm(a) for a in srcs], *lands, after)
    return (outs[2:2 + n], outs[2 + n:2 + 2 * n], outs[0], outs[1]), outs[-1]


def _xfer_wait(name, started, copies, after):
    srcs, lands, send, recv = started
    n = len(srcs)

    def body(*refs):
        src_refs, land_refs, s_ref, r_ref = refs[:n], refs[n:2 * n], refs[2 * n], refs[2 * n + 1]
        for cp in copies(src_refs, land_refs, s_ref, r_ref):
            cp.wait_send()
            cp.wait_recv()

    outs = pl.pallas_call(
        body, name=name, in_specs=[HBM] * (2 * n) + [SEM, SEM, ANY], out_specs=[HBM] * (2 * n),
        out_shape=[pltpu.HBM(a.shape, a.dtype) for a in (*srcs, *lands)],
        input_output_aliases={i: i for i in range(2 * n)},
        compiler_params=pltpu.CompilerParams(has_side_effects=EFFECT),
    )(*srcs, *lands, send, recv, after)
    return outs[:n], outs[n:]


def _swap_copies(srcs, lands, send, recv):
    x, y, c, _ = _place()
    return [_remote(src.at[:, _half(src.shape[1], 1 - c)], land, send.at[i], recv.at[i], (x, y, 1 - c))
            for i, (src, land) in enumerate(zip(srcs, lands))]


def _scatter_copies(srcs, lands, send, recv):
    x, y, c, chips = _place()
    return [_remote(src.at[2 * cx + cy], land.at[k], send.at[3 * i + k], recv.at[3 * i + k], (cx, cy, c))
            for i, (src, land) in enumerate(zip(srcs, lands)) for k, (cx, cy) in enumerate(chips)]


def _join_copies(srcs, lands, send, recv):
    x, y, c, _ = _place()
    return [_remote(src, land, send.at[i], recv.at[i], (x, y, 1 - c)) for i, (src, land) in enumerate(zip(srcs, lands))]


def _corner(a):
    return a[(slice(0, 1),) * a.ndim]


class _Reducer:
    def __init__(self, place):
        self.place = place
        self.state = {}

    def swap(self, key, grads, after):
        shapes = [((g.shape[0], g.shape[1] // 2, g.shape[2]), g.dtype) for g in grads]
        self.state[key], token = _xfer_start("swap_start_" + key, grads, shapes, len(grads), _swap_copies, _corner(after))
        return token

    def to_chips(self, key, after):
        grads, from_sibling = _xfer_wait("swap_wait_" + key, self.state[key], _swap_copies, after)
        sums = [_add_halves(g, r, self.place) for g, r in zip(grads, from_sibling)]
        shapes = [((3,) + s.shape[1:], s.dtype) for s in sums]
        started, token = _xfer_start("scatter_start_" + key, sums, shapes, 3 * len(sums), _scatter_copies, _corner(sums[-1]))
        self.state[key] = (grads, from_sibling, started)
        return token

    def to_core(self, key, after):
        grads, from_sibling, started = self.state[key]
        _, from_chips = _xfer_wait("scatter_wait_" + key, started, _scatter_copies, after)
        halves = [_sum_chips(g, r, rc, self.place) for g, r, rc in zip(grads, from_sibling, from_chips)]
        shapes = [(h.shape, h.dtype) for h in halves]
        self.state[key], token = _xfer_start("join_start_" + key, halves, shapes, len(halves), _join_copies, _corner(halves[-1]))
        return token

    def finish(self, key, after):
        return _xfer_wait("join_wait_" + key, self.state.pop(key), _join_copies, after)


def _gather_packs(pack):
    def body(p_ref, o_ref, lsem, ssem, rsem):
        x, y, c, _ = _place()
        me = 4 * x + 2 * y + c
        local = pltpu.make_async_copy(p_ref, o_ref.at[me], lsem)
        local.start()
        cps = []
        for k in range(1, N_DEV):
            fx, fy, fc = (k >> 2) & 1, (k >> 1) & 1, k & 1
            to = (x ^ fx, y ^ fy, c ^ fc)
            cps.append(_remote(p_ref, o_ref.at[me], ssem.at[k - 1], rsem.at[k - 1], to))
        for cp in cps:
            cp.start()
        for k in range(1, N_DEV):
            fx, fy, fc = (k >> 2) & 1, (k >> 1) & 1, k & 1
            src = o_ref.at[4 * (x ^ fx) + 2 * (y ^ fy) + (c ^ fc)]
            _remote(src, src, ssem.at[k - 1], rsem.at[k - 1], (x, y, c)).wait_recv()
        for cp in cps:
            cp.wait_send()
        local.wait()

    return pl.pallas_call(
        body, name="gather_packs", in_specs=[ANY], out_specs=ANY,
        out_shape=jax.ShapeDtypeStruct((N_DEV,) + pack.shape, pack.dtype),
        scratch_shapes=[pltpu.SemaphoreType.DMA, pltpu.SemaphoreType.DMA((N_DEV - 1,)), pltpu.SemaphoreType.DMA((N_DEV - 1,))],
    )(pack)


LANE_TILES = (512, 896, 1408, 704, 384, 256, 128)


def _layer_grads(x, target, small, wg, rest_pass, rest_wait, red):
    s, d = x.shape
    f = wg["conv"].shape[1] // 2
    w_att = N_HEADS * HEAD_DIM
    in_splits = (w_att, w_att, w_att, N_HEADS, w_att, w_att, w_att, d, d)
    in_cols = sum(in_splits)
    cs = in_cols // N_CHIPS
    cp = wg["in"].shape[2]
    tm = min(s, 1024)
    t_in = cp
    t_d = _pick(d, LANE_TILES)
    t_d2 = min(d, 1024)
    t_dq = _pick(d // N_CHIPS, LANE_TILES)
    t_w = _pick(w_att, LANE_TILES)
    t_up = 2 * f // N_CHIPS
    tm_wide = min(s, 512)
    t_fq = _pick(f // N_CHIPS, LANE_TILES)
    offs = np.cumsum(in_splits)[:-1].tolist()

    h1 = _norm_fwd(x, small["g_attn"], group=d, name="rms1_fwd")
    proj_p = _mm(h1, wg["in"], mode="nn", b_kind="col", tm=tm_wide, tn=t_in, tk=d, name="mm_in")
    gains = {n: small[n].reshape(1, w_att) for n in ("g_q_fox", "g_k_fox", "g_q_dil", "g_k_dil")}
    qa, ka, va_b, fa, qb, kb, vb_b, ga, gb, qa_n, ka_n, qb_n, kb_n = _proj_split(
        proj_p, in_splits, cs, (F32, F32, BF16, F32, F32, F32, BF16, F32, F32),
        {0: gains["g_q_fox"], 1: gains["g_k_fox"], 4: gains["g_q_dil"], 5: gains["g_k_dil"]})
    fa_t = fa.T
    b_f = small["b_forget"].reshape(N_HEADS, 1)
    c_f = _forget_fwd(fa_t, b_f)
    slopes = jnp.asarray(2.0 ** (-8.0 * np.arange(1, N_HEADS + 1) / N_HEADS), dtype=F32)
    a_d = -(slopes[:, None] * jnp.arange(s, dtype=F32)[None, :])
    rows_f, cols_f = c_f[:, :, None], c_f[:, None, :]
    rows_d, cols_d = a_d[:, :, None], a_d[:, None, :]
    o_a, o_a32, lse_a = _attn_fwd(qa_n, ka_n, va_b, rows_f, cols_f, dilated=False, name="attn_fox_fwd")
    token = rest_pass("mid", o_a)
    rows_d = rows_d + token[0, 0]
    o_b, o_b32, lse_b = _attn_fwd(qb_n, kb_n, vb_b, rows_d, cols_d, dilated=True, name="attn_dil_fwd")
    wg = dict(wg, **rest_wait("mid", o_b))
    token = rest_pass("late", o_b)
    pa = _mm(o_a, wg["brf"], mode="nn", b_kind="col", tm=tm, tn=t_dq, tk=w_att, name="mm_brf", deps=(token,))
    pb = _mm(o_b, wg["brd"], mode="nn", b_kind="col", tm=tm, tn=t_dq, tk=w_att, name="mm_brd")
    merged = _gate_fwd(ga, gb, pa, pb)
    x1 = _mm(merged, wg["out"], mode="nn", b_kind="row", res=x, tm=tm, tn=t_d, tk=t_dq, name="mm_out")
    wg = dict(wg, **rest_wait("late", x1))
    h2 = _norm_fwd(x1, small["g_ffn"], group=d, name="rms2_fwd")
    u = _mm(h2, wg["up"], mode="nn", b_kind="col", tm=tm_wide, tn=t_up, tk=d, name="mm_up")
    act = _conv_glu_fwd(u, wg["conv"], wg["bconv"])
    y = _mm(act, wg["down"], mode="nn", b_kind="row", res=x1, tm=tm, tn=t_d2, tk=t_fq, name="mm_down")
    dy_f, dy_b, loss_blk = _loss_head(y, target)

    d_act = _mm(dy_b, wg["down"], mode="nt", b_kind="row", tm=tm, tn=t_fq, tk=d, name="mm_down_dx")
    g_down = _mm(act, dy_b, mode="tn", out_dtype=BF16, out_kind="row", tm=t_fq, tn=t_d2, tk=s, name="mm_down_dw")
    tok = red.swap("down", [g_down], g_down)
    du_g, du_v, st_g, st_v = _conv_glu_bwd(u, d_act, wg["conv"] + tok[0, 0], wg["bconv"])
    tok = red.to_chips("down", du_g)
    du = jnp.concatenate([du_g, du_v], axis=1)
    g_up = _mm(h2, du, mode="tn", out_dtype=BF16, out_kind="col", tm=t_d2, tn=t_up, tk=s, name="mm_up_dw", deps=(tok,))
    tok = red.to_core("down", g_up)
    tok2 = red.swap("up", [g_up], g_up)
    dh2 = _mm(du, wg["up"], mode="nt", b_kind="col", tm=tm, tn=t_d2, tk=t_up, name="mm_up_dx", deps=(tok, tok2))
    tok = red.to_chips("up", dh2)
    dx1_b, dx1_f, dg_ffn = _norm_bwd(dh2, x1, small["g_ffn"], group=d, res=dy_f, out_dtypes=(BF16, F32), name="rms2_bwd")
    d_merged = _mm(dx1_b, wg["out"], mode="nt", b_kind="row", tm=tm, tn=t_dq, tk=d, name="mm_out_dx", deps=(tok,))
    g_out = _mm(merged, dx1_b, mode="tn", out_dtype=BF16, out_kind="row", tm=t_dq, tn=t_d2, tk=s, name="mm_out_dw")
    dpa, dpb, dga, dgb = _gate_bwd(d_merged, ga, gb, pa, pb)
    do_a = _mm(dpa, wg["brf"], mode="nt", b_kind="col", out_dtype=BF16, tm=s, tn=w_att, tk=t_dq, name="mm_brf_dx")
    do_b = _mm(dpb, wg["brd"], mode="nt", b_kind="col", out_dtype=BF16, tm=s, tn=w_att, tk=t_dq, name="mm_brd_dx")
    g_brf = _mm(o_a, dpa, mode="tn", out_dtype=BF16, out_kind="col", tm=w_att, tn=t_dq, tk=s, name="mm_brf_dw")
    g_brd = _mm(o_b, dpb, mode="tn", out_dtype=BF16, out_kind="col", tm=w_att, tn=t_dq, tk=s, name="mm_brd_dw")
    tok = red.swap("mix", [g_out, g_brf, g_brd], g_brd)
    dqa_n, dka_n, dva, dac_a = _attn_bwd(qa_n, ka_n, va_b, o_a32, do_a, lse_a, rows_f + tok[0, 0], cols_f, dilated=False, name="attn_fox_bwd")
    tok = red.to_core("up", dqa_n)
    tok2 = red.to_chips("mix", dqa_n)
    dqb_n, dkb_n, dvb, _ = _attn_bwd(qb_n, kb_n, vb_b, o_b32, do_b, lse_b, rows_d + (tok[0, 0] + tok2[0, 0]), cols_d, dilated=True, name="attn_dil_bwd")
    tok = red.to_core("mix", dqb_n)
    dfa_t, db_f = _forget_bwd(dac_a[:, 0, :], fa_t, b_f)
    dproj_p, dgains = _dproj_merge(
        [dqa_n, dka_n, dva, dfa_t.T, dqb_n, dkb_n, dvb, dga, dgb], in_splits, cs, cp,
        {0: (qa, gains["g_q_fox"]), 1: (ka, gains["g_k_fox"]), 4: (qb, gains["g_q_dil"]), 5: (kb, gains["g_k_dil"])})
    dg_qf, dg_kf, dg_qd, dg_kd = dgains[0], dgains[1], dgains[4], dgains[5]
    g_in = _mm(h1, dproj_p, mode="tn", out_dtype=BF16, out_kind="col", tm=t_d2, tn=t_in, tk=s, name="mm_in_dw", deps=(tok,))
    tok = red.swap("in", [g_in], g_in)
    dh1 = _mm(dproj_p, wg["in"], mode="nt", b_kind="col", tm=tm, tn=t_d2, tk=t_in, name="mm_in_dx", deps=(tok,))
    tok = red.to_chips("in", dh1)
    grad_x, dg_attn = _norm_bwd(dh1, x, small["g_attn"], group=d, res=dx1_f, out_dtypes=(F32,), name="rms1_bwd")

    small_grads = {
        "g_attn": dg_attn, "b_forget": db_f.reshape(1, N_HEADS),
        "g_q_fox": dg_qf, "g_k_fox": dg_kf, "g_q_dil": dg_qd, "g_k_dil": dg_kd, "g_ffn": dg_ffn,
        "w_conv": jnp.concatenate([st_g[0:3], st_v[0:3]], axis=1),
        "b_conv": jnp.concatenate([st_g[3:4], st_v[3:4]], axis=1),
        "loss": loss_blk[0:1, 0:1],
    }
    return small_grads, grad_x, tok


SMALL_ORDER = ("g_attn", "b_forget", "g_q_fox", "g_k_fox", "g_q_dil", "g_k_dil", "g_ffn", "w_conv", "b_conv", "loss")
WEIGHT_ORDER = ("g_attn", "w_in", "b_forget", "g_q_fox", "g_k_fox", "g_q_dil", "g_k_dil", "w_br_fox", "w_br_dil",
                "w_out", "g_ffn", "w_up", "w_conv", "b_conv", "w_down")
BIG = {"w_in": "in", "w_br_fox": "brf", "w_br_dil": "brd", "w_out": "out", "w_up": "up", "w_down": "down"}


def kernel(x, g_attn, w_in, b_forget, g_q_fox, g_k_fox, g_q_dil, g_k_dil, w_br_fox, w_br_dil, w_out, g_ffn, w_up, w_conv, b_conv, w_down, loss_target, m_g_attn, m_w_in, m_b_forget, m_g_q_fox, m_g_k_fox, m_g_q_dil, m_g_k_dil, m_w_br_fox, m_w_br_dil, m_w_out, m_g_ffn, m_w_up, m_w_conv, m_b_conv, m_w_down, v_g_attn, v_w_in, v_b_forget, v_g_q_fox, v_g_k_fox, v_g_q_dil, v_g_k_dil, v_w_br_fox, v_w_br_dil, v_w_out, v_g_ffn, v_w_up, v_w_conv, v_b_conv, v_w_down):
    w = dict(g_attn=g_attn, w_in=w_in, b_forget=b_forget, g_q_fox=g_q_fox, g_k_fox=g_k_fox, g_q_dil=g_q_dil,
             g_k_dil=g_k_dil, w_br_fox=w_br_fox, w_br_dil=w_br_dil, w_out=w_out, g_ffn=g_ffn, w_up=w_up,
             w_conv=w_conv, b_conv=b_conv, w_down=w_down)
    m = dict(g_attn=m_g_attn, w_in=m_w_in, b_forget=m_b_forget, g_q_fox=m_g_q_fox, g_k_fox=m_g_k_fox,
             g_q_dil=m_g_q_dil, g_k_dil=m_g_k_dil, w_br_fox=m_w_br_fox, w_br_dil=m_w_br_dil, w_out=m_w_out,
             g_ffn=m_g_ffn, w_up=m_w_up, w_conv=m_w_conv, b_conv=m_b_conv, w_down=m_w_down)
    v = dict(g_attn=v_g_attn, w_in=v_w_in, b_forget=v_b_forget, g_q_fox=v_g_q_fox, g_k_fox=v_g_k_fox,
             g_q_dil=v_g_q_dil, g_k_dil=v_g_k_dil, w_br_fox=v_w_br_fox, w_br_dil=v_w_br_dil, w_out=v_w_out,
             g_ffn=v_g_ffn, w_up=v_w_up, w_conv=v_w_conv, b_conv=v_b_conv, w_down=v_w_down)
    xi, yi, ci = lax.axis_index("x"), lax.axis_index("y"), lax.axis_index("c")
    chip = (2 * xi + yi).astype(jnp.int32)
    c_idx = ci.astype(jnp.int32).reshape(1)
    j_idx = chip.reshape(1)

    cs = w_in.shape[2]
    cp = _round_up(cs, LANES)
    shards = {
        "in": jnp.pad(w_in[0].astype(BF16), ((0, 0), (0, cp - cs))),
        "brf": w_br_fox[0].astype(BF16), "brd": w_br_dil[0].astype(BF16), "out": w_out[0].astype(BF16),
        "up": w_up[0].astype(BF16), "down": w_down[0].astype(BF16),
    }
    names = tuple(shards)
    conv_pad = jnp.pad(w_conv[0], ((0, 8 - w_conv.shape[1]), (0, 0)))
    first = [(shards["in"], True), (conv_pad, False)]
    later = {"mid": ("brf", "brd", "out"), "late": ("up", "down")}
    groups = {key: [(shards[n], True) for n in members] for key, members in later.items()}
    (started_first, *started_later), token = _gather_start([first, *groups.values()])
    started = dict(zip(later, started_later))
    token, w["w_in"], m["w_in"], v["w_in"] = lax.optimization_barrier((token, w["w_in"], m["w_in"], v["w_in"]))
    w2, m2, v2 = ({n: a[n].reshape(a[n].shape[-2], a[n].shape[-1]) for n in BIG} for a in (w, m, v))
    early = (token, w2["w_in"], m2["w_in"], v2["w_in"])
    own_first, passed_first, token = _gather_pass(first, started_first, early, "gather_pass_in")
    land_in, land_conv = _gather_wait(first, passed_first, token, "gather_wait_in")
    conv_all = _own_slab(land_conv, own_first[1])
    wg = {"in": _own_slab(land_in, own_first[0]), "bconv": b_conv,
          "conv": jnp.transpose(conv_all[:, :w_conv.shape[1], :], (1, 0, 2)).reshape(w_conv.shape[1], -1)}
    small = {n: w[n] for n in ("g_attn", "b_forget", "g_q_fox", "g_k_fox", "g_q_dil", "g_k_dil", "g_ffn")}
    small = {n: (a[0] if a.ndim == 3 else a) for n, a in small.items()}
    in_flight = {}

    def rest_pass(key, after):
        own, passed, tok = _gather_pass(groups[key], started[key], (after,), "gather_pass_" + key)
        in_flight[key] = (own, passed)
        return tok

    def rest_wait(key, after):
        own, passed = in_flight.pop(key)
        lands = _gather_wait(groups[key], passed, after, "gather_wait_" + key)
        return {n: _own_slab(land, o) for n, land, o in zip(later[key], lands, own)}

    reducer = _Reducer(jnp.stack([chip, ci.astype(jnp.int32)]))
    small_grads, grad_x, tok_in = _layer_grads(x[0], loss_target[0], small, wg, rest_pass, rest_wait, reducer)

    mine, theirs = {}, {}
    for key, members in (("down", ("down",)), ("up", ("up",)), ("mix", ("out", "brf", "brd"))):
        mine_k, theirs_k = reducer.finish(key, grad_x)
        mine.update(zip(members, mine_k))
        theirs.update(zip(members, theirs_k))

    flat = jnp.concatenate([small_grads[n].reshape(-1) for n in SMALL_ORDER])
    rows = _round_up(flat.shape[0], 8 * LANES) // LANES
    pack = jnp.pad(flat, (0, rows * LANES - flat.shape[0])).reshape(rows, LANES)
    total = _sum_devices(_gather_packs(pack)).reshape(-1)
    red, at = {}, 0
    for n in SMALL_ORDER:
        size = small_grads[n].size
        red[n] = total[at:at + size].reshape(small_grads[n].shape)
        at += size
    loss = red["loss"].reshape(())
    c2 = w_conv.shape[2]
    red["w_conv"] = lax.dynamic_slice_in_dim(red["w_conv"], chip * c2, c2, axis=1)

    g_out, d_out, m_out, v_out = {}, {}, {}, {}
    last = [n for n in WEIGHT_ORDER if n != "w_in"] + ["w_in"]
    for n in last:
        shape = w[n].shape
        r2 = (shape[-2], shape[-1]) if n not in ("g_attn", "b_forget", "g_ffn", "b_conv") else (1, shape[-1])
        if n == "w_in":
            done = jnp.stack([v_out[k][(0,) * v_out[k].ndim] for k in last[:-1]])
            tok = reducer.to_core("in", done)
            (mine_in,), (theirs_in,) = reducer.finish("in", tok)
            mine["in"], theirs["in"] = mine_in[:, :cs], theirs_in[:, :cs]
        if n in BIG:
            g2, dl, mn, vn = _adamw_halves(w2[n], mine[BIG[n]], theirs[BIG[n]], c_idx, m2[n], v2[n],
                                           name="adamw_" + n, deps=(tok_in,))
        else:
            g2 = red[n].reshape(r2)
            dl, mn, vn = _adamw(w[n].reshape(r2), g2, m[n].reshape(r2), v[n].reshape(r2), name="adamw_" + n,
                                deps=(tok_in,))
        g_out[n], d_out[n], m_out[n], v_out[n] = (a.reshape(shape) for a in (g2, dl, mn, vn))

    return (loss, grad_x[None], *[g_out[n] for n in WEIGHT_ORDER], *[d_out[n] for n in WEIGHT_ORDER],
            *[m_out[n] for n in WEIGHT_ORDER], *[v_out[n] for n in WEIGHT_ORDER])
```

```python
import functools
import math

import jax
import jax.numpy as jnp
import numpy as np
from jax import lax
from jax.experimental import pallas as pl
from jax.experimental.pallas import tpu as pltpu

F32 = jnp.float32
BF16 = jnp.bfloat16
HEAD_DIM = 128
N_HEADS = 8
EPS = 1e-6
NEG = -1e30
N_CHIPS = 4
N_DEV = 8
LANES = 128
VMEM_LIMIT_BYTES = 56 * 1024 * 1024
DIL_PATTERNS = ((128, 1), (512, 4), (2048, 16))
ATTN_TILE = 512
ADAM_LR, ADAM_B1, ADAM_B2, ADAM_EPS, ADAM_WD, ADAM_STEP = 0.001, 0.9, 0.999, 1e-08, 0.01, 10
MESH = pl.DeviceIdType.MESH


def _params(*sem):
    return pltpu.CompilerParams(dimension_semantics=sem, vmem_limit_bytes=VMEM_LIMIT_BYTES)


def _round_up(n, m):
    return -(-n // m) * m


def _pick(dim, prefs):
    for p in prefs:
        if dim % p == 0:
            return p
    raise ValueError(f"no tile for {dim} in {prefs}")


def _logical_shape(arr, kind):
    if kind is None:
        return arr.shape
    s, r, c = arr.shape
    return (r, s * c) if kind == "col" else (s * r, c)


def _spec(shape, kind, br, bc, fi, fj):
    if kind is None:
        return pl.BlockSpec((br, bc), lambda *g: (fi(*g), fj(*g)))
    _, r, c = shape
    if kind == "col":
        nb = c // bc
        assert nb * bc == c, (shape, bc)
        return pl.BlockSpec((None, br, bc), lambda *g: (fj(*g) // nb, fi(*g), fj(*g) % nb))
    nb = r // br
    assert nb * br == r, (shape, br)
    return pl.BlockSpec((None, br, bc), lambda *g: (fi(*g) // nb, fi(*g) % nb, fj(*g)))


def _mm(a, b, *, mode, tm, tn, tk, name, a_kind=None, b_kind=None, out_kind=None,
        out_dtype=F32, res=None, deps=()):
    la, lb = _logical_shape(a, a_kind), _logical_shape(b, b_kind)
    if mode == "nn":
        (m, k), (k2, n) = la, lb
    elif mode == "nt":
        (m, k), (n, k2) = la, lb
    else:
        (k, m), (k2, n) = la, lb
    assert k == k2, (name, la, lb)
    assert m % tm == 0 and n % tn == 0 and k % tk == 0, (name, m, n, k, tm, tn, tk)
    nk = k // tk
    im = lambda i, j, l: i
    jn = lambda i, j, l: j
    lk = lambda i, j, l: l
    if mode == "tn":
        a_spec = _spec(a.shape, a_kind, tk, tm, lk, im)
        dims = (((0,), (0,)), ((), ()))
    else:
        a_spec = _spec(a.shape, a_kind, tm, tk, im, lk)
        dims = (((1,), (1,)), ((), ())) if mode == "nt" else (((1,), (0,)), ((), ()))
    if mode == "nt":
        b_spec = _spec(b.shape, b_kind, tn, tk, jn, lk)
    else:
        b_spec = _spec(b.shape, b_kind, tk, tn, lk, jn)
    if out_kind is None:
        oshape = (m, n)
    elif out_kind == "col":
        oshape = (N_CHIPS, m, n // N_CHIPS)
    else:
        oshape = (N_CHIPS, m // N_CHIPS, n)
    o_spec = _spec(oshape, out_kind, tm, tn, im, jn)
    in_specs = [a_spec, b_spec]
    args = [a, b]
    if res is not None:
        in_specs.append(pl.BlockSpec((tm, tn), lambda i, j, l: (i, j)))
        args.append(res)
    in_specs += [pl.BlockSpec(memory_space=pl.ANY)] * len(deps)
    args += list(deps)

    def finish(out, res_ref, o_ref):
        if res_ref is not None:
            out = out + res_ref[...]
        o_ref[...] = out.astype(o_ref.dtype)

    def body_whole_k(*refs):
        res_ref = refs[2] if res is not None else None
        finish(lax.dot_general(refs[0][...], refs[1][...], dims, preferred_element_type=F32), res_ref, refs[-1])

    def body(*refs):
        a_ref, b_ref = refs[0], refs[1]
        res_ref = refs[2] if res is not None else None
        o_ref, acc_ref = refs[-2], refs[-1]
        step = pl.program_id(2)

        @pl.when(step == 0)
        def _():
            acc_ref[...] = jnp.zeros_like(acc_ref)

        acc_ref[...] += lax.dot_general(a_ref[...], b_ref[...], dims, preferred_element_type=F32)

        @pl.when(step == nk - 1)
        def _():
            finish(acc_ref[...], res_ref, o_ref)

    return pl.pallas_call(
        body_whole_k if nk == 1 else body, name=name, grid=(m // tm, n // tn, nk),
        in_specs=in_specs, out_specs=o_spec,
        out_shape=jax.ShapeDtypeStruct(oshape, out_dtype),
        scratch_shapes=[] if nk == 1 else [pltpu.VMEM((tm, tn), F32)],
        compiler_params=_params("parallel", "parallel", "arbitrary"),
    )(*args)


def _pieces(splits, cs, cp):
    out, g0 = [], 0
    for width in splits:
        g1, runs = g0 + width, []
        for j in range(N_CHIPS):
            a, b = max(g0, cs * j), min(g1, cs * (j + 1))
            if a < b:
                runs.append((j * cp + a - cs * j, a - g0, b - a))
        out.append(runs)
        g0 = g1
    return out


def _head_norm(xv, gv):
    r = lax.rsqrt(jnp.mean(xv * xv, axis=-1, keepdims=True) + EPS)
    return (xv * r) * gv


def _head_norm_bwd(dyv, xv, gv):
    r = lax.rsqrt(jnp.mean(xv * xv, axis=-1, keepdims=True) + EPS)
    xr = xv * r
    gdy = dyv * gv
    return r * (gdy - xr * jnp.mean(gdy * xr, axis=-1, keepdims=True)), jnp.sum(dyv * xr, axis=0, keepdims=True)


def _proj_split(proj_p, splits, cs, dtypes, gains, tm=128):
    s, wp = proj_p.shape
    pieces = _pieces(splits, cs, wp // N_CHIPS)
    normed = sorted(gains)
    nseg = len(splits)

    def body(p_ref, *refs):
        g_refs, o_refs, n_refs = refs[:len(normed)], refs[len(normed):len(normed) + nseg], refs[len(normed) + nseg:]
        for o_ref, runs in zip(o_refs, pieces):
            for src, dst, n in runs:
                o_ref[:, dst:dst + n] = p_ref[:, src:src + n].astype(o_ref.dtype)
        for g_ref, n_ref, i in zip(g_refs, n_refs, normed):
            for c0 in range(0, splits[i], HEAD_DIM):
                cols = slice(c0, c0 + HEAD_DIM)
                n_ref[:, cols] = _head_norm(o_refs[i][:, cols], g_ref[:, cols]).astype(n_ref.dtype)

    return pl.pallas_call(
        body, name="proj_split", grid=(s // tm,),
        in_specs=[pl.BlockSpec((tm, wp), lambda i: (i, 0))] + [pl.BlockSpec((1, splits[i]), lambda i: (0, 0)) for i in normed],
        out_specs=[pl.BlockSpec((tm, w), lambda i: (i, 0)) for w in splits]
        + [pl.BlockSpec((tm, splits[i]), lambda i: (i, 0)) for i in normed],
        out_shape=[jax.ShapeDtypeStruct((s, w), dt) for w, dt in zip(splits, dtypes)]
        + [jax.ShapeDtypeStruct((s, splits[i]), BF16) for i in normed],
        compiler_params=_params("parallel"),
    )(proj_p, *[gains[i] for i in normed])


def _dproj_merge(parts, splits, cs, cp, norms, tm=128):
    s = parts[0].shape[0]
    wp = N_CHIPS * cp
    pieces = _pieces(splits, cs, cp)
    normed = sorted(norms)
    nseg, nn = len(splits), len(normed)

    def body(*refs):
        p_refs, x_refs, g_refs = refs[:nseg], refs[nseg:nseg + nn], refs[nseg + nn:nseg + 2 * nn]
        o_ref, dg_refs = refs[nseg + 2 * nn], refs[nseg + 2 * nn + 1:nseg + 3 * nn + 1]
        stage, tmp = refs[-2], refs[-1]

        @pl.when(pl.program_id(0) == 0)
        def _():
            for dg_ref in dg_refs:
                dg_ref[...] = jnp.zeros_like(dg_ref)

        for j in range(N_CHIPS):
            stage[:, j * cp + cs:(j + 1) * cp] = jnp.zeros((tm, cp - cs), F32)
        for i, (p_ref, runs) in enumerate(zip(p_refs, pieces)):
            src_ref = p_ref
            if i in norms:
                k = normed.index(i)
                for c0 in range(0, splits[i], HEAD_DIM):
                    cols = slice(c0, c0 + HEAD_DIM)
                    dx, dg = _head_norm_bwd(p_ref[:, cols].astype(F32), x_refs[k][:, cols], g_refs[k][:, cols])
                    tmp[:, cols] = dx
                    dg_refs[k][:, cols] += dg
                src_ref = tmp
            for dst, src, n in runs:
                stage[:, dst:dst + n] = src_ref[:, src:src + n].astype(F32)
        o_ref[...] = stage[...].astype(o_ref.dtype)

    wmax = max(splits[i] for i in normed)
    row = lambda w: pl.BlockSpec((tm, w), lambda i: (i, 0))
    vec = lambda w: pl.BlockSpec((1, w), lambda i: (0, 0))
    outs = pl.pallas_call(
        body, name="dproj_merge", grid=(s // tm,),
        in_specs=[row(w) for w in splits] + [row(splits[i]) for i in normed] + [vec(splits[i]) for i in normed],
        out_specs=[row(wp)] + [vec(splits[i]) for i in normed],
        out_shape=[jax.ShapeDtypeStruct((s, wp), BF16)] + [jax.ShapeDtypeStruct((1, splits[i]), F32) for i in normed],
        scratch_shapes=[pltpu.VMEM((tm, wp), F32), pltpu.VMEM((tm, wmax), F32)],
        compiler_params=_params("arbitrary"),
    )(*parts, *[norms[i][0] for i in normed], *[norms[i][1] for i in normed])
    return outs[0], dict(zip(normed, outs[1:]))


def _norm_fwd(x, g, *, group, name, tm=256):
    s, w = x.shape
    ng = w // group

    def body(x_ref, g_ref, o_ref):
        for i in range(ng):
            cols = slice(i * group, (i + 1) * group)
            xv = x_ref[:, cols]
            r = lax.rsqrt(jnp.mean(xv * xv, axis=-1, keepdims=True) + EPS)
            o_ref[:, cols] = ((xv * r) * g_ref[:, cols]).astype(o_ref.dtype)

    return pl.pallas_call(
        body, name=name, grid=(s // tm,),
        in_specs=[pl.BlockSpec((tm, w), lambda i: (i, 0)), pl.BlockSpec((1, w), lambda i: (0, 0))],
        out_specs=pl.BlockSpec((tm, w), lambda i: (i, 0)),
        out_shape=jax.ShapeDtypeStruct((s, w), BF16),
        compiler_params=_params("parallel"),
    )(x, g)


def _norm_bwd(dy, x, g, *, group, name, res=None, out_dtypes=(BF16,), tm=256, deps=()):
    s, w = x.shape
    ng = w // group
    n_in = 4 if res is not None else 3

    def body(*refs):
        dy_ref, x_ref, g_ref = refs[:3]
        res_ref = refs[3] if res is not None else None
        outs = refs[n_in + len(deps):]
        dx_refs, dg_ref = outs[:-1], outs[-1]

        @pl.when(pl.program_id(0) == 0)
        def _():
            dg_ref[...] = jnp.zeros_like(dg_ref)

        for i in range(ng):
            cols = slice(i * group, (i + 1) * group)
            xv = x_ref[:, cols]
            dyv = dy_ref[:, cols].astype(F32)
            r = lax.rsqrt(jnp.mean(xv * xv, axis=-1, keepdims=True) + EPS)
            xr = xv * r
            dg_ref[:, cols] += jnp.sum(dyv * xr, axis=0, keepdims=True)
            gdy = dyv * g_ref[:, cols]
            dx = r * (gdy - xr * jnp.mean(gdy * xr, axis=-1, keepdims=True))
            if res_ref is not None:
                dx = dx + res_ref[:, cols]
            for dx_ref in dx_refs:
                dx_ref[:, cols] = dx.astype(dx_ref.dtype)

    row = pl.BlockSpec((tm, w), lambda i: (i, 0))
    vec = pl.BlockSpec((1, w), lambda i: (0, 0))
    in_specs = [row, row, vec] + ([row] if res is not None else []) + [pl.BlockSpec(memory_space=pl.ANY)] * len(deps)
    args = [dy, x, g] + ([res] if res is not None else []) + list(deps)
    out_specs = [row] * len(out_dtypes) + [vec]
    out_shape = [jax.ShapeDtypeStruct((s, w), dt) for dt in out_dtypes] + [jax.ShapeDtypeStruct((1, w), F32)]
    return pl.pallas_call(
        body, name=name, grid=(s // tm,), in_specs=in_specs, out_specs=out_specs,
        out_shape=out_shape, compiler_params=_params("arbitrary"),
    )(*args)


def _split3(v):
    p1 = v.astype(BF16)
    r1 = v - p1.astype(F32)
    p2 = r1.astype(BF16)
    p3 = (r1 - p2.astype(F32)).astype(BF16)
    return p1, p2, p3


def _tri_sum(v, reverse, tcol=512):
    h, s = v.shape
    tcol = min(tcol, s)
    parts = _split3(v)
    outs = []
    for j in range(s // tcol):
        src = lax.broadcasted_iota(jnp.int32, (s, tcol), 0)
        dst = lax.broadcasted_iota(jnp.int32, (s, tcol), 1) + j * tcol
        keep = (src >= dst) if reverse else (src <= dst)
        tri = jnp.where(keep, 1.0, 0.0).astype(BF16)
        acc = jnp.zeros((h, tcol), F32)
        for p in parts:
            acc = acc + jnp.dot(p, tri, preferred_element_type=F32)
        outs.append(acc)
    return outs


def _forget_fwd(fa_t, b):
    h, s = fa_t.shape
    tcol = min(512, s)

    def body(f_ref, b_ref, c_ref):
        z = f_ref[...] + b_ref[...]
        logf = jnp.minimum(z, 0.0) - jnp.log(1.0 + jnp.exp(-jnp.abs(z)))
        for j, blk in enumerate(_tri_sum(logf, reverse=False, tcol=tcol)):
            c_ref[:, j * tcol:(j + 1) * tcol] = blk

    return pl.pallas_call(
        body, name="forget_fwd", out_shape=jax.ShapeDtypeStruct((h, s), F32),
        compiler_params=_params(),
    )(fa_t, b)


def _forget_bwd(dacol, fa_t, b):
    h, s = fa_t.shape
    tcol = min(512, s)

    def body(d_ref, f_ref, b_ref, dfa_ref, db_ref):
        z = f_ref[...] + b_ref[...]
        dc = -d_ref[...]
        total = jnp.zeros((h, 1), F32)
        for j, blk in enumerate(_tri_sum(dc, reverse=True, tcol=tcol)):
            cols = slice(j * tcol, (j + 1) * tcol)
            dfa = blk * (1.0 - jax.nn.sigmoid(z[:, cols]))
            dfa_ref[:, cols] = dfa
            total = total + jnp.sum(dfa, axis=-1, keepdims=True)
        db_ref[...] = total

    return pl.pallas_call(
        body, name="forget_bwd",
        out_shape=[jax.ShapeDtypeStruct((h, s), F32), jax.ShapeDtypeStruct((h, 1), F32)],
        compiler_params=_params(),
    )(dacol, fa_t, b)


def _distance_bias(s, tile, dilated):
    nb = s // tile
    b = lax.broadcasted_iota(jnp.int32, (nb, tile, tile), 0)
    dist = b * tile + lax.broadcasted_iota(jnp.int32, (nb, tile, tile), 1) - lax.broadcasted_iota(jnp.int32, (nb, tile, tile), 2)
    if not dilated:
        return jnp.where(dist >= 0, 0.0, NEG).astype(F32)
    mult = jnp.zeros(dist.shape, jnp.int32)
    for window, dil in DIL_PATTERNS:
        mult = mult + ((dist >= 0) & (dist <= window) & ((dist & (dil - 1)) == 0)).astype(jnp.int32)
    logm = jnp.where(mult == 3, math.log(3.0), jnp.where(mult == 2, math.log(2.0), 0.0))
    return jnp.where(mult > 0, logm, NEG).astype(F32)


def _logits(q, k, arow, acol, bias):
    s = lax.dot_general(q, k, (((1,), (1,)), ((), ())), preferred_element_type=F32)
    return s * (1.0 / math.sqrt(HEAD_DIM)) + arow - acol + bias


def _attn_fwd(q, k, v, arow, acol, *, dilated, name, tq=ATTN_TILE, tk=ATTN_TILE):
    two_term = not dilated
    s, w = q.shape
    nh = w // HEAD_DIM
    assert tq == tk
    tq = tk = min(tq, s)
    nq, nk = s // tq, s // tk

    def body(q_ref, k_ref, v_ref, ar_ref, ac_ref, b_ref, o_ref, of_ref, lse_ref, m_ref, l_ref, acc_ref):
        qi, ki = pl.program_id(1), pl.program_id(2)

        @pl.when(ki == 0)
        def _():
            m_ref[...] = jnp.full_like(m_ref, NEG)
            l_ref[...] = jnp.zeros_like(l_ref)
            acc_ref[...] = jnp.zeros_like(acc_ref)

        @pl.when(ki <= qi)
        def _():
            sc = _logits(q_ref[...], k_ref[...], ar_ref[...], ac_ref[...], b_ref[...])
            m_new = jnp.maximum(m_ref[...], jnp.max(sc, axis=-1, keepdims=True))
            alpha = jnp.exp(m_ref[...] - m_new)
            p = jnp.exp(sc - m_new)
            l_ref[...] = alpha * l_ref[...] + jnp.sum(p, axis=-1, keepdims=True)
            p_hi = p.astype(BF16)
            vv = v_ref[...]
            pv = jnp.dot(p_hi, vv, preferred_element_type=F32)
            if two_term:
                pv = pv + jnp.dot((p - p_hi.astype(F32)).astype(BF16), vv, preferred_element_type=F32)
            acc_ref[...] = alpha * acc_ref[...] + pv
            m_ref[...] = m_new

        @pl.when(ki == nk - 1)
        def _():
            out = acc_ref[...] / l_ref[...]
            o_ref[...] = out.astype(o_ref.dtype)
            of_ref[...] = out
            lse_ref[...] = m_ref[...] + jnp.log(l_ref[...])

    kv = pl.BlockSpec((tk, HEAD_DIM), lambda h, i, j: (jnp.minimum(j, i), h))
    return pl.pallas_call(
        body, name=name, grid=(nh, nq, nk),
        in_specs=[pl.BlockSpec((tq, HEAD_DIM), lambda h, i, j: (i, h)), kv, kv,
                  pl.BlockSpec((None, tq, 1), lambda h, i, j: (h, i, 0)),
                  pl.BlockSpec((None, 1, tk), lambda h, i, j: (h, 0, jnp.minimum(j, i))),
                  pl.BlockSpec((None, tq, tk), lambda h, i, j: (jnp.maximum(i - j, 0), 0, 0))],
        out_specs=[pl.BlockSpec((tq, HEAD_DIM), lambda h, i, j: (i, h)),
                   pl.BlockSpec((tq, HEAD_DIM), lambda h, i, j: (i, h)),
                   pl.BlockSpec((None, tq, 1), lambda h, i, j: (h, i, 0))],
        out_shape=[jax.ShapeDtypeStruct((s, w), BF16), jax.ShapeDtypeStruct((s, w), F32),
                   jax.ShapeDtypeStruct((nh, s, 1), F32)],
        scratch_shapes=[pltpu.VMEM((tq, 1), F32), pltpu.VMEM((tq, 1), F32), pltpu.VMEM((tq, HEAD_DIM), F32)],
        compiler_params=_params("parallel", "parallel", "arbitrary"),
    )(q, k, v, arow, acol, _distance_bias(s, tq, dilated))


def _attn_bwd(q, k, v, o, do, lse, arow, acol, *, dilated, name, tq=ATTN_TILE, tk=ATTN_TILE):
    s, w = q.shape
    nh = w // HEAD_DIM
    assert tq == tk
    tq = tk = min(tq, s)
    nq, nk = s // tq, s // tk
    scale = 1.0 / math.sqrt(HEAD_DIM)

    def body(q_ref, k_ref, v_ref, o_ref, do_ref, lse_ref, ar_ref, ac_ref, b_ref,
             dq_ref, dk_ref, dv_ref, dac_ref, dk_acc, dv_acc, dac_acc):
        ki, qi = pl.program_id(1), pl.program_id(2)

        @pl.when((ki == 0) & (qi == 0))
        def _():
            dq_ref[...] = jnp.zeros_like(dq_ref)

        @pl.when(qi == 0)
        def _():
            dk_acc[...] = jnp.zeros_like(dk_acc)
            dv_acc[...] = jnp.zeros_like(dv_acc)
            dac_acc[...] = jnp.zeros_like(dac_acc)

        @pl.when(qi >= ki)
        def _():
            qv, kvv, dov = q_ref[...], k_ref[...], do_ref[...]
            sc = _logits(qv, kvv, ar_ref[...], ac_ref[...], b_ref[...])
            p = jnp.exp(sc - lse_ref[...])
            dp = lax.dot_general(dov, v_ref[...], (((1,), (1,)), ((), ())), preferred_element_type=F32)
            delta = jnp.sum(dov.astype(F32) * o_ref[...].astype(F32), axis=-1, keepdims=True)
            ds = p * (dp - delta)
            dsb = ds.astype(BF16)
            dv_acc[...] += lax.dot_general(p.astype(BF16), dov, (((0,), (0,)), ((), ())), preferred_element_type=F32)
            dk_acc[...] += lax.dot_general(dsb, qv, (((0,), (0,)), ((), ())), preferred_element_type=F32)
            rows = pl.ds(pl.multiple_of(qi * tq, tq), tq)
            dq_ref[rows, :] += jnp.dot(dsb, kvv, preferred_element_type=F32) * scale
            dac_acc[...] += jnp.sum(ds, axis=0, keepdims=True)

        @pl.when(qi == nq - 1)
        def _():
            dk_ref[...] = dk_acc[...] * scale
            dv_ref[...] = dv_acc[...]
            dac_ref[...] = dac_acc[...]

    qs = pl.BlockSpec((tq, HEAD_DIM), lambda h, j, i: (jnp.maximum(i, j), h))
    ks = pl.BlockSpec((tk, HEAD_DIM), lambda h, j, i: (j, h))
    rowv = pl.BlockSpec((None, tq, 1), lambda h, j, i: (h, jnp.maximum(i, j), 0))
    colv = pl.BlockSpec((None, 1, tk), lambda h, j, i: (h, 0, j))
    return pl.pallas_call(
        body, name=name, grid=(nh, nk, nq),
        in_specs=[qs, ks, ks, qs, qs, rowv, rowv, colv,
                  pl.BlockSpec((None, tq, tk), lambda h, j, i: (jnp.maximum(i - j, 0), 0, 0))],
        out_specs=[pl.BlockSpec((s, HEAD_DIM), lambda h, j, i: (0, h)), ks, ks, colv],
        out_shape=[jax.ShapeDtypeStruct((s, w), F32), jax.ShapeDtypeStruct((s, w), F32),
                   jax.ShapeDtypeStruct((s, w), F32), jax.ShapeDtypeStruct((nh, 1, s), F32)],
        scratch_shapes=[pltpu.VMEM((tk, HEAD_DIM), F32), pltpu.VMEM((tk, HEAD_DIM), F32), pltpu.VMEM((1, tk), F32)],
        compiler_params=_params("arbitrary", "arbitrary", "arbitrary"),
    )(q, k, v, o, do, lse, arow, acol, _distance_bias(s, tq, dilated))


def _gate_fwd(ga, gb, pa, pb, tm=256):
    s, d = ga.shape

    def body(ga_ref, gb_ref, pa_ref, pb_ref, o_ref):
        o_ref[...] = (jax.nn.sigmoid(ga_ref[...]) * pa_ref[...]
                      + jax.nn.sigmoid(gb_ref[...]) * pb_ref[...]).astype(o_ref.dtype)

    row = pl.BlockSpec((tm, d), lambda i: (i, 0))
    return pl.pallas_call(
        body, name="gate_fwd", grid=(s // tm,), in_specs=[row] * 4, out_specs=row,
        out_shape=jax.ShapeDtypeStruct((s, d), BF16), compiler_params=_params("parallel"),
    )(ga, gb, pa, pb)


def _gate_bwd(dm, ga, gb, pa, pb, tm=256):
    s, d = ga.shape

    def body(dm_ref, ga_ref, gb_ref, pa_ref, pb_ref, dpa_ref, dpb_ref, dga_ref, dgb_ref):
        dmv = dm_ref[...]
        for g_ref, p_ref, dp_ref, dg_ref in ((ga_ref, pa_ref, dpa_ref, dga_ref), (gb_ref, pb_ref, dpb_ref, dgb_ref)):
            sg = jax.nn.sigmoid(g_ref[...])
            dp_ref[...] = (dmv * sg).astype(BF16)
            dg_ref[...] = (dmv * p_ref[...] * (sg * (1.0 - sg))).astype(BF16)

    row = pl.BlockSpec((tm, d), lambda i: (i, 0))
    return pl.pallas_call(
        body, name="gate_bwd", grid=(s // tm,), in_specs=[row] * 5, out_specs=[row] * 4,
        out_shape=[jax.ShapeDtypeStruct((s, d), BF16)] * 4, compiler_params=_params("parallel"),
    )(dm, ga, gb, pa, pb)


def _shift_down(u, k):
    row = lax.broadcasted_iota(jnp.int32, u.shape, 0)
    return jnp.where(row >= k, pltpu.roll(u, k, 0), 0.0)


def _shift_up(u, k):
    n = u.shape[0]
    row = lax.broadcasted_iota(jnp.int32, u.shape, 0)
    return jnp.where(row < n - k, pltpu.roll(u, n - k, 0), 0.0)


def _conv3(u, wc, b):
    return wc[0:1, :] * _shift_down(u, 2) + wc[1:2, :] * _shift_down(u, 1) + wc[2:3, :] * u + b


def _conv_glu_fwd(u, wc, b, tn=256):
    s, f2 = u.shape
    f = f2 // 2
    nb = f // tn

    def body(ug_ref, uv_ref, wg_ref, wv_ref, bg_ref, bv_ref, o_ref):
        cg = _conv3(ug_ref[...], wg_ref[...], bg_ref[...])
        cv = _conv3(uv_ref[...], wv_ref[...], bv_ref[...])
        o_ref[...] = (cg * jax.nn.sigmoid(cg) * cv).astype(o_ref.dtype)

    def cols(rows, off):
        return pl.BlockSpec((rows, tn), lambda j: (0, j + off))

    return pl.pallas_call(
        body, name="conv_glu_fwd", grid=(nb,),
        in_specs=[cols(s, 0), cols(s, nb), cols(3, 0), cols(3, nb), cols(1, 0), cols(1, nb)],
        out_specs=cols(s, 0), out_shape=jax.ShapeDtypeStruct((s, f), BF16),
        compiler_params=_params("parallel"),
    )(u, u, wc, wc, b, b)


def _conv_glu_bwd(u, da, wc, b, tn=256):
    s, f2 = u.shape
    f = f2 // 2
    nb = f // tn

    def body(ug_ref, uv_ref, da_ref, wg_ref, wv_ref, bg_ref, bv_ref, dug_ref, duv_ref, sg_ref, sv_ref):
        ug, uv, wg, wv = ug_ref[...], uv_ref[...], wg_ref[...], wv_ref[...]
        cg = _conv3(ug, wg, bg_ref[...])
        cv = _conv3(uv, wv, bv_ref[...])
        sig = jax.nn.sigmoid(cg)
        dav = da_ref[...]
        dcv = dav * (cg * sig)
        dcg = dav * cv * (sig * (1.0 + cg * (1.0 - sig)))
        for dc, uu, w, du_ref, st_ref in ((dcg, ug, wg, dug_ref, sg_ref), (dcv, uv, wv, duv_ref, sv_ref)):
            du = w[2:3, :] * dc + w[1:2, :] * _shift_up(dc, 1) + w[0:1, :] * _shift_up(dc, 2)
            du_ref[...] = du.astype(BF16)
            st_ref[...] = jnp.zeros_like(st_ref)
            st_ref[0:1, :] = jnp.sum(dc * _shift_down(uu, 2), axis=0, keepdims=True)
            st_ref[1:2, :] = jnp.sum(dc * _shift_down(uu, 1), axis=0, keepdims=True)
            st_ref[2:3, :] = jnp.sum(dc * uu, axis=0, keepdims=True)
            st_ref[3:4, :] = jnp.sum(dc, axis=0, keepdims=True)

    def cols(rows, off):
        return pl.BlockSpec((rows, tn), lambda j: (0, j + off))

    return pl.pallas_call(
        body, name="conv_glu_bwd", grid=(nb,),
        in_specs=[cols(s, 0), cols(s, nb), cols(s, 0), cols(3, 0), cols(3, nb), cols(1, 0), cols(1, nb)],
        out_specs=[cols(s, 0), cols(s, 0), cols(8, 0), cols(8, 0)],
        out_shape=[jax.ShapeDtypeStruct((s, f), BF16), jax.ShapeDtypeStruct((s, f), BF16),
                   jax.ShapeDtypeStruct((8, f), F32), jax.ShapeDtypeStruct((8, f), F32)],
        compiler_params=_params("parallel"),
    )(u, u, da, wc, wc, b, b)


def _loss_head(y, target, tm=256):
    s, d = y.shape

    def body(y_ref, t_ref, dyf_ref, dyb_ref, l_ref):
        @pl.when(pl.program_id(0) == 0)
        def _():
            l_ref[...] = jnp.zeros_like(l_ref)

        err = y_ref[...] - t_ref[...]
        dy = err * (1.0 / d)
        dyf_ref[...] = dy
        dyb_ref[...] = dy.astype(BF16)
        l_ref[...] += 0.5 * jnp.sum(jnp.sum(err * err, axis=-1, keepdims=True) * (1.0 / d), axis=0, keepdims=True)

    row = pl.BlockSpec((tm, d), lambda i: (i, 0))
    return pl.pallas_call(
        body, name="loss_head", grid=(s // tm,), in_specs=[row, row],
        out_specs=[row, row, pl.BlockSpec((8, LANES), lambda i: (0, 0))],
        out_shape=[jax.ShapeDtypeStruct((s, d), F32), jax.ShapeDtypeStruct((s, d), BF16),
                   jax.ShapeDtypeStruct((8, LANES), F32)],
        compiler_params=_params("arbitrary"),
    )(y, target)


ROW_TILES = (256, 128, 64, 32, 16, 8)
BLOCK_BYTES = 2 << 20


def _add_halves(g, r1, place):
    ns, r, c = g.shape
    rh = r // 2
    tr = _pick(rh, ROW_TILES)
    g4 = g.reshape(ns, 2, rh, c)

    def body(p_ref, g_ref, r_ref, o_ref):
        o_ref[...] = (g_ref[...].astype(F32) + r_ref[...].astype(F32)).astype(o_ref.dtype)

    def slab(s, pr):
        return s + (s >= pr[0]).astype(jnp.int32)

    return pl.pallas_call(
        body, name="add_halves",
        grid_spec=pltpu.PrefetchScalarGridSpec(
            num_scalar_prefetch=1, grid=(ns - 1, rh // tr),
            in_specs=[pl.BlockSpec((None, None, tr, c), lambda s, i, pr: (slab(s, pr), pr[1], i, 0)),
                      pl.BlockSpec((None, tr, c), lambda s, i, pr: (slab(s, pr), i, 0))],
            out_specs=pl.BlockSpec((None, tr, c), lambda s, i, pr: (slab(s, pr), i, 0))),
        out_shape=jax.ShapeDtypeStruct((ns, rh, c), BF16),
        compiler_params=_params("parallel", "parallel"),
    )(place, g4, r1)


def _sum_chips(g, r1, recv, place):
    ns, r, c = g.shape
    rh = r // 2
    tr = _pick(rh, ROW_TILES)
    g4 = g.reshape(ns, 2, rh, c)

    def body(p_ref, g_ref, r_ref, t0_ref, t1_ref, t2_ref, o_ref):
        own = g_ref[...].astype(F32) + r_ref[...].astype(F32)
        o_ref[...] = ((own + t0_ref[...].astype(F32)) + t1_ref[...].astype(F32)) + t2_ref[...].astype(F32)

    def peer(k):
        return pl.BlockSpec((None, tr, c), lambda i, pr: (k, i, 0))

    return pl.pallas_call(
        body, name="sum_chips",
        grid_spec=pltpu.PrefetchScalarGridSpec(
            num_scalar_prefetch=1, grid=(rh // tr,),
            in_specs=[pl.BlockSpec((None, None, tr, c), lambda i, pr: (pr[0], pr[1], i, 0)),
                      pl.BlockSpec((None, tr, c), lambda i, pr: (pr[0], i, 0)), peer(0), peer(1), peer(2)],
            out_specs=pl.BlockSpec((tr, c), lambda i, pr: (i, 0))),
        out_shape=jax.ShapeDtypeStruct((rh, c), F32),
        compiler_params=_params("parallel"),
    )(place, g4, r1, recv, recv, recv)


def _sum_devices(packs):
    n, r, c = packs.shape

    def body(p_ref, o_ref):
        acc = p_ref[0]
        for d in range(1, n):
            acc = acc + p_ref[d]
        o_ref[...] = acc

    return pl.pallas_call(
        body, name="sum_devices", out_shape=jax.ShapeDtypeStruct((r, c), F32), compiler_params=_params(),
    )(packs)


def _adamw_update(wv, gv, mv, vv):
    c1 = 1.0 - ADAM_B1 ** ADAM_STEP
    c2 = 1.0 - ADAM_B2 ** ADAM_STEP
    mn = ADAM_B1 * mv + (1.0 - ADAM_B1) * gv
    vn = ADAM_B2 * vv + (1.0 - ADAM_B2) * (gv * gv)
    m_hat = mn / c1
    v_hat = vn / c2
    return -ADAM_LR * (m_hat / (jnp.sqrt(v_hat) + ADAM_EPS) + ADAM_WD * wv), mn, vn


def _adamw(w, g, m, v, name, deps=()):
    r, c = w.shape
    tr = _pick(r, ROW_TILES) if r >= 8 else r

    def body(w_ref, g_ref, m_ref, v_ref, *rest):
        d_ref, mo_ref, vo_ref = rest[-3:]
        d_ref[...], mo_ref[...], vo_ref[...] = _adamw_update(w_ref[...], g_ref[...], m_ref[...], v_ref[...])

    blk = pl.BlockSpec((tr, c), lambda i: (i, 0))
    return pl.pallas_call(
        body, name=name, grid=(r // tr,), in_specs=[blk] * 4 + [ANY] * len(deps), out_specs=[blk] * 3,
        out_shape=[jax.ShapeDtypeStruct((r, c), F32)] * 3, compiler_params=_params("parallel"),
    )(w, g, m, v, *deps)


def _adamw_halves(w, mine, theirs, c_idx, m, v, name, deps=()):
    r, c = w.shape
    rh = r // 2
    tr = _pick(rh, [t for t in ROW_TILES if t * c * 4 <= BLOCK_BYTES])
    nb = rh // tr

    def body(c_ref, w_ref, a_ref, b_ref, m_ref, v_ref, *rest):
        g_ref, d_ref, mo_ref, vo_ref = rest[-4:]
        gv = jnp.where(pl.program_id(0) // nb == c_ref[0], a_ref[...], b_ref[...])
        g_ref[...] = gv
        d_ref[...], mo_ref[...], vo_ref[...] = _adamw_update(w_ref[...], gv, m_ref[...], v_ref[...])

    blk = pl.BlockSpec((tr, c), lambda i, cr: (i, 0))
    mine_spec = pl.BlockSpec((tr, c), lambda i, cr: (jnp.clip(i - cr[0] * nb, 0, nb - 1), 0))
    theirs_spec = pl.BlockSpec((tr, c), lambda i, cr: (jnp.clip(i - (1 - cr[0]) * nb, 0, nb - 1), 0))
    return pl.pallas_call(
        body, name=name,
        grid_spec=pltpu.PrefetchScalarGridSpec(
            num_scalar_prefetch=1, grid=(r // tr,),
            in_specs=[blk, mine_spec, theirs_spec, blk, blk] + [ANY] * len(deps), out_specs=[blk] * 4),
        out_shape=[jax.ShapeDtypeStruct((r, c), F32)] * 4, compiler_params=_params("arbitrary"),
    )(c_idx, w, mine, theirs, m, v, *deps)


ANY = pl.BlockSpec(memory_space=pl.ANY)


def _place():
    x, y, c = lax.axis_index("x"), lax.axis_index("y"), lax.axis_index("c")
    chips = [(1 - x, y), (x, 1 - y), (1 - x, 1 - y)]
    return x, y, c, chips


def _remote(src, dst, send_sem, recv_sem, to):
    return pltpu.make_async_remote_copy(src_ref=src, dst_ref=dst, send_sem=send_sem, recv_sem=recv_sem,
                                        device_id=to, device_id_type=MESH)


HBM = pl.BlockSpec(memory_space=pltpu.HBM)
SEM = pl.BlockSpec(memory_space=pltpu.SEMAPHORE)
EFFECT = pltpu.SideEffectType.DATAFLOW_SIDE_EFFECTING


def _in_hbm(a):
    return pltpu.with_memory_space_constraint(a, pltpu.HBM)


def _half(ref_rows, who):
    return pl.ds(who * (ref_rows // 2), ref_rows // 2)


def _gather_start(groups):
    items = [it for g in groups for it in g]
    n = len(items)
    sizes = [len(g) for g in groups]

    def body(*refs):
        srcs, lands = refs[:n], refs[n:2 * n]
        sems = refs[2 * n:2 * n + 2 * len(groups)]
        token = refs[-1]
        x, y, c, chips = _place()
        j = 2 * x + y
        at = 0
        for gi, g in enumerate(groups):
            send, recv = sems[2 * gi], sems[2 * gi + 1]
            for i, (shard, split) in enumerate(g):
                src, land = srcs[at], lands[at]
                at += 1
                rows = _half(shard.shape[0], c) if split else slice(None)
                for k, chip in enumerate(chips):
                    _remote(src.at[rows], land.at[j, rows], send.at[4 * i + k], recv.at[4 * i + k], (*chip, c)).start()
                _remote(src, land.at[j], send.at[4 * i + 3], recv.at[4 * i + 3], (x, y, 1 - c)).start()
        token[...] = jnp.zeros_like(token)

    sem_shapes = []
    for sz in sizes:
        sem_shapes += [pltpu.SemaphoreType.DMA((4 * sz,)), pltpu.SemaphoreType.DMA((4 * sz,))]
    out_shape = (sem_shapes + [pltpu.HBM(sh.shape, sh.dtype) for sh, _ in items]
                 + [pltpu.HBM((N_CHIPS,) + sh.shape, sh.dtype) for sh, _ in items]
                 + [jax.ShapeDtypeStruct((8, LANES), F32)])
    ns = len(sem_shapes)
    outs = pl.pallas_call(
        body, name="gather_start", in_specs=[HBM] * (2 * n),
        out_specs=[SEM] * ns + [HBM] * (2 * n) + [pl.BlockSpec(memory_space=pltpu.VMEM)],
        out_shape=out_shape, input_output_aliases={i: ns + i for i in range(2 * n)},
        compiler_params=pltpu.CompilerParams(has_side_effects=EFFECT),
    )(*[_in_hbm(sh) for sh, _ in items], *[_in_hbm(lax.empty((N_CHIPS,) + sh.shape, sh.dtype)) for sh, _ in items])
    sems, shards, lands, token = outs[:ns], outs[ns:ns + n], outs[ns + n:ns + 2 * n], outs[-1]
    res, at = [], 0
    for gi, sz in enumerate(sizes):
        res.append((shards[at:at + sz], lands[at:at + sz], sems[2 * gi], sems[2 * gi + 1]))
        at += sz
    return res, token


def _gather_pass(group, started, after, name):
    shards, lands, send, recv = started
    n = len(group)
    split_ix = [i for i, (_, split) in enumerate(group) if split]

    def body(*refs):
        lnds, send1, recv1 = refs[n:2 * n], refs[2 * n], refs[2 * n + 1]
        outs = refs[2 * n + 2 + len(after):]
        send2, recv2, token = outs[2 * n], outs[2 * n + 1], outs[2 * n + 2]
        x, y, c, chips = _place()
        sib = (x, y, 1 - c)
        for i, (shard, split) in enumerate(group):
            rows = _half(shard.shape[0], c) if split else slice(None)
            for k, (cx, cy) in enumerate(chips):
                landed = lnds[i].at[2 * cx + cy, rows]
                cp = _remote(landed, landed, send1.at[4 * i + k], recv1.at[4 * i + k], sib)
                cp.wait_send()
                cp.wait_recv()
            own = lnds[i].at[2 * x + y]
            cp = _remote(own, own, send1.at[4 * i + 3], recv1.at[4 * i + 3], sib)
            cp.wait_send()
            cp.wait_recv()
        for i2, i in enumerate(split_ix):
            rows = _half(group[i][0].shape[0], c)
            for k, (cx, cy) in enumerate(chips):
                landed = lnds[i].at[2 * cx + cy, rows]
                _remote(landed, landed, send2.at[3 * i2 + k], recv2.at[3 * i2 + k], sib).start()
        token[...] = jnp.zeros_like(token)

    n2 = len(split_ix)
    out_shape = ([pltpu.HBM(a.shape, a.dtype) for a in (*shards, *lands)]
                 + [pltpu.SemaphoreType.DMA((3 * n2,)), pltpu.SemaphoreType.DMA((3 * n2,)), jax.ShapeDtypeStruct((8, LANES), F32)])
    outs = pl.pallas_call(
        body, name=name, in_specs=[HBM] * (2 * n) + [SEM, SEM] + [ANY] * len(after),
        out_specs=[HBM] * (2 * n) + [SEM, SEM, pl.BlockSpec(memory_space=pltpu.VMEM)],
        out_shape=out_shape, input_output_aliases={i: i for i in range(2 * n)},
        compiler_params=pltpu.CompilerParams(has_side_effects=EFFECT),
    )(*shards, *lands, send, recv, *after)
    return outs[:n], (outs[n:2 * n], outs[2 * n], outs[2 * n + 1]), outs[2 * n + 2]


def _gather_wait(group, passed, after, name):
    lands, send2, recv2 = passed
    n = len(group)
    split_ix = [i for i, (_, split) in enumerate(group) if split]

    def body(*refs):
        lnds, s2, r2 = refs[:n], refs[n], refs[n + 1]
        x, y, c, chips = _place()
        sib = (x, y, 1 - c)
        for i2, i in enumerate(split_ix):
            rows = _half(group[i][0].shape[0], 1 - c)
            for k, (cx, cy) in enumerate(chips):
                landed = lnds[i].at[2 * cx + cy, rows]
                cp = _remote(landed, landed, s2.at[3 * i2 + k], r2.at[3 * i2 + k], sib)
                cp.wait_send()
                cp.wait_recv()

    return pl.pallas_call(
        body, name=name, in_specs=[HBM] * n + [SEM, SEM, ANY], out_specs=[HBM] * n,
        out_shape=[pltpu.HBM(a.shape, a.dtype) for a in lands], input_output_aliases={i: i for i in range(n)},
        compiler_params=pltpu.CompilerParams(has_side_effects=EFFECT),
    )(*lands, send2, recv2, after)


def _xfer_start(name, srcs, land_shapes, n_copies, copies, after):
    n = len(srcs)

    def body(*refs):
        src_refs, land_refs = refs[:n], refs[n:2 * n]
        send, recv, token = refs[2 * n + 1], refs[2 * n + 2], refs[-1]
        for cp in copies(src_refs, land_refs, send, recv):
            cp.start()
        token[...] = jnp.zeros_like(token)

    lands = [_in_hbm(lax.empty(shape, dtype)) for shape, dtype in land_shapes]
    out_shape = ([pltpu.SemaphoreType.DMA((n_copies,)), pltpu.SemaphoreType.DMA((n_copies,))]
                 + [pltpu.HBM(a.shape, a.dtype) for a in (*srcs, *lands)] + [jax.ShapeDtypeStruct((8, LANES), F32)])
    outs = pl.pallas_call(
        body, name=name, in_specs=[HBM] * (2 * n) + [ANY],
        out_specs=[SEM, SEM] + [HBM] * (2 * n) + [pl.BlockSpec(memory_space=pltpu.VMEM)],
        out_shape=out_shape, input_output_aliases={i: 2 + i for i in range(2 * n)},
        compiler_params=pltpu.CompilerParams(has_side_effects=EFFECT),
    )(*[_in_hbm(a) for a in srcs], *lands, after)
    return (outs[2:2 + n], outs[2 + n:2 + 2 * n], outs[0], outs[1]), outs[-1]


def _xfer_wait(name, started, copies, after):
    srcs, lands, send, recv = started
    n = len(srcs)

    def body(*refs):
        src_refs, land_refs, s_ref, r_ref = refs[:n], refs[n:2 * n], refs[2 * n], refs[2 * n + 1]
        for cp in copies(src_refs, land_refs, s_ref, r_ref):
            cp.wait_send()
            cp.wait_recv()

    outs = pl.pallas_call(
        body, name=name, in_specs=[HBM] * (2 * n) + [SEM, SEM, ANY], out_specs=[HBM] * (2 * n),
        out_shape=[pltpu.HBM(a.shape, a.dtype) for a in (*srcs, *lands)],
        input_output_aliases={i: i for i in range(2 * n)},
        compiler_params=pltpu.CompilerParams(has_side_effects=EFFECT),
    )(*srcs, *lands, send, recv, after)
    return outs[:n], outs[n:]


def _swap_copies(srcs, lands, send, recv):
    x, y, c, _ = _place()
    return [_remote(src.at[:, _half(src.shape[1], 1 - c)], land, send.at[i], recv.at[i], (x, y, 1 - c))
            for i, (src, land) in enumerate(zip(srcs, lands))]


def _scatter_copies(srcs, lands, send, recv):
    x, y, c, chips = _place()
    return [_remote(src.at[2 * cx + cy], land.at[k], send.at[3 * i + k], recv.at[3 * i + k], (cx, cy, c))
            for i, (src, land) in enumerate(zip(srcs, lands)) for k, (cx, cy) in enumerate(chips)]


def _join_copies(srcs, lands, send, recv):
    x, y, c, _ = _place()
    return [_remote(src, land, send.at[i], recv.at[i], (x, y, 1 - c)) for i, (src, land) in enumerate(zip(srcs, lands))]


def _corner(a):
    return a[(slice(0, 1),) * a.ndim]


class _Reducer:
    def __init__(self, place):
        self.place = place
        self.state = {}

    def swap(self, key, grads, after):
        shapes = [((g.shape[0], g.shape[1] // 2, g.shape[2]), g.dtype) for g in grads]
        self.state[key], token = _xfer_start("swap_start_" + key, grads, shapes, len(grads), _swap_copies, _corner(after))
        return token

    def to_chips(self, key, after):
        grads, from_sibling = _xfer_wait("swap_wait_" + key, self.state[key], _swap_copies, after)
        sums = [_add_halves(g, r, self.place) for g, r in zip(grads, from_sibling)]
        shapes = [((3,) + s.shape[1:], s.dtype) for s in sums]
        started, token = _xfer_start("scatter_start_" + key, sums, shapes, 3 * len(sums), _scatter_copies, _corner(sums[-1]))
        self.state[key] = (grads, from_sibling, started)
        return token

    def to_core(self, key, after):
        grads, from_sibling, started = self.state[key]
        _, from_chips = _xfer_wait("scatter_wait_" + key, started, _scatter_copies, after)
        halves = [_sum_chips(g, r, rc, self.place) for g, r, rc in zip(grads, from_sibling, from_chips)]
        shapes = [(h.shape, h.dtype) for h in halves]
        self.state[key], token = _xfer_start("join_start_" + key, halves, shapes, len(halves), _join_copies, _corner(halves[-1]))
        return token

    def finish(self, key, after):
        return _xfer_wait("join_wait_" + key, self.state.pop(key), _join_copies, after)


def _gather_packs(pack):
    def body(p_ref, o_ref, lsem, ssem, rsem):
        x, y, c, _ = _place()
        me = 4 * x + 2 * y + c
        local = pltpu.make_async_copy(p_ref, o_ref.at[me], lsem)
        local.start()
        cps = []
        for k in range(1, N_DEV):
            fx, fy, fc = (k >> 2) & 1, (k >> 1) & 1, k & 1
            to = (x ^ fx, y ^ fy, c ^ fc)
            cps.append(_remote(p_ref, o_ref.at[me], ssem.at[k - 1], rsem.at[k - 1], to))
        for cp in cps:
            cp.start()
        for k in range(1, N_DEV):
            fx, fy, fc = (k >> 2) & 1, (k >> 1) & 1, k & 1
            src = o_ref.at[4 * (x ^ fx) + 2 * (y ^ fy) + (c ^ fc)]
            _remote(src, src, ssem.at[k - 1], rsem.at[k - 1], (x, y, c)).wait_recv()
        for cp in cps:
            cp.wait_send()
        local.wait()

    return pl.pallas_call(
        body, name="gather_packs", in_specs=[ANY], out_specs=ANY,
        out_shape=jax.ShapeDtypeStruct((N_DEV,) + pack.shape, pack.dtype),
        scratch_shapes=[pltpu.SemaphoreType.DMA, pltpu.SemaphoreType.DMA((N_DEV - 1,)), pltpu.SemaphoreType.DMA((N_DEV - 1,))],
    )(pack)


LANE_TILES = (512, 896, 1408, 704, 384, 256, 128)


def _layer_grads(x, target, small, wg, rest_pass, rest_wait, red):
    s, d = x.shape
    f = wg["conv"].shape[1] // 2
    w_att = N_HEADS * HEAD_DIM
    in_splits = (w_att, w_att, w_att, N_HEADS, w_att, w_att, w_att, d, d)
    in_cols = sum(in_splits)
    cs = in_cols // N_CHIPS
    cp = wg["in"].shape[2]
    tm = min(s, 1024)
    t_in = cp
    t_d = _pick(d, LANE_TILES)
    t_d2 = min(d, 1024)
    t_dq = _pick(d // N_CHIPS, LANE_TILES)
    t_w = _pick(w_att, LANE_TILES)
    t_up = 2 * f // N_CHIPS
    tm_wide = min(s, 512)
    t_fq = _pick(f // N_CHIPS, LANE_TILES)
    offs = np.cumsum(in_splits)[:-1].tolist()

    h1 = _norm_fwd(x, small["g_attn"], group=d, name="rms1_fwd")
    proj_p = _mm(h1, wg["in"], mode="nn", b_kind="col", tm=tm_wide, tn=t_in, tk=d, name="mm_in")
    gains = {n: small[n].reshape(1, w_att) for n in ("g_q_fox", "g_k_fox", "g_q_dil", "g_k_dil")}
    qa, ka, va_b, fa, qb, kb, vb_b, ga, gb, qa_n, ka_n, qb_n, kb_n = _proj_split(
        proj_p, in_splits, cs, (F32, F32, BF16, F32, F32, F32, BF16, F32, F32),
        {0: gains["g_q_fox"], 1: gains["g_k_fox"], 4: gains["g_q_dil"], 5: gains["g_k_dil"]})
    fa_t = fa.T
    b_f = small["b_forget"].reshape(N_HEADS, 1)
    c_f = _forget_fwd(fa_t, b_f)
    slopes = jnp.asarray(2.0 ** (-8.0 * np.arange(1, N_HEADS + 1) / N_HEADS), dtype=F32)
    a_d = -(slopes[:, None] * jnp.arange(s, dtype=F32)[None, :])
    rows_f, cols_f = c_f[:, :, None], c_f[:, None, :]
    rows_d, cols_d = a_d[:, :, None], a_d[:, None, :]
    o_a, o_a32, lse_a = _attn_fwd(qa_n, ka_n, va_b, rows_f, cols_f, dilated=False, name="attn_fox_fwd")
    token = rest_pass("mid", o_a)
    rows_d = rows_d + token[0, 0]
    o_b, o_b32, lse_b = _attn_fwd(qb_n, kb_n, vb_b, rows_d, cols_d, dilated=True, name="attn_dil_fwd")
    wg = dict(wg, **rest_wait("mid", o_b))
    token = rest_pass("late", o_b)
    pa = _mm(o_a, wg["brf"], mode="nn", b_kind="col", tm=tm, tn=t_dq, tk=w_att, name="mm_brf", deps=(token,))
    pb = _mm(o_b, wg["brd"], mode="nn", b_kind="col", tm=tm, tn=t_dq, tk=w_att, name="mm_brd")
    merged = _gate_fwd(ga, gb, pa, pb)
    x1 = _mm(merged, wg["out"], mode="nn", b_kind="row", res=x, tm=tm, tn=t_d, tk=t_dq, name="mm_out")
    wg = dict(wg, **rest_wait("late", x1))
    h2 = _norm_fwd(x1, small["g_ffn"], group=d, name="rms2_fwd")
    u = _mm(h2, wg["up"], mode="nn", b_kind="col", tm=tm_wide, tn=t_up, tk=d, name="mm_up")
    act = _conv_glu_fwd(u, wg["conv"], wg["bconv"])
    y = _mm(act, wg["down"], mode="nn", b_kind="row", res=x1, tm=tm, tn=t_d2, tk=t_fq, name="mm_down")
    dy_f, dy_b, loss_blk = _loss_head(y, target)

    d_act = _mm(dy_b, wg["down"], mode="nt", b_kind="row", tm=tm, tn=t_fq, tk=d, name="mm_down_dx")
    g_down = _mm(act, dy_b, mode="tn", out_dtype=BF16, out_kind="row", tm=t_fq, tn=t_d2, tk=s, name="mm_down_dw")
    tok = red.swap("down", [g_down], g_down)
    du_g, du_v, st_g, st_v = _conv_glu_bwd(u, d_act, wg["conv"] + tok[0, 0], wg["bconv"])
    tok = red.to_chips("down", du_g)
    du = jnp.concatenate([du_g, du_v], axis=1)
    g_up = _mm(h2, du, mode="tn", out_dtype=BF16, out_kind="col", tm=t_d2, tn=t_up, tk=s, name="mm_up_dw", deps=(tok,))
    tok = red.to_core("down", g_up)
    tok2 = red.swap("up", [g_up], g_up)
    dh2 = _mm(du, wg["up"], mode="nt", b_kind="col", tm=tm, tn=t_d2, tk=t_up, name="mm_up_dx", deps=(tok, tok2))
    tok = red.to_chips("up", dh2)
    dx1_b, dx1_f, dg_ffn = _norm_bwd(dh2, x1, small["g_ffn"], group=d, res=dy_f, out_dtypes=(BF16, F32), name="rms2_bwd")
    d_merged = _mm(dx1_b, wg["out"], mode="nt", b_kind="row", tm=tm, tn=t_dq, tk=d, name="mm_out_dx", deps=(tok,))
    g_out = _mm(merged, dx1_b, mode="tn", out_dtype=BF16, out_kind="row", tm=t_dq, tn=t_d2, tk=s, name="mm_out_dw")
    dpa, dpb, dga, dgb = _gate_bwd(d_merged, ga, gb, pa, pb)
    do_a = _mm(dpa, wg["brf"], mode="nt", b_kind="col", out_dtype=BF16, tm=s, tn=w_att, tk=t_dq, name="mm_brf_dx")
    do_b = _mm(dpb, wg["brd"], mode="nt", b_kind="col", out_dtype=BF16, tm=s, tn=w_att, tk=t_dq, name="mm_brd_dx")
    g_brf = _mm(o_a, dpa, mode="tn", out_dtype=BF16, out_kind="col", tm=w_att, tn=t_dq, tk=s, name="mm_brf_dw")
    g_brd = _mm(o_b, dpb, mode="tn", out_dtype=BF16, out_kind="col", tm=w_att, tn=t_dq, tk=s, name="mm_brd_dw")
    tok = red.swap("mix", [g_out, g_brf, g_brd], g_brd)
    dqa_n, dka_n, dva, dac_a = _attn_bwd(qa_n, ka_n, va_b, o_a32, do_a, lse_a, rows_f + tok[0, 0], cols_f, dilated=False, name="attn_fox_bwd")
    tok = red.to_core("up", dqa_n)
    tok2 = red.to_chips("mix", dqa_n)
    dqb_n, dkb_n, dvb, _ = _attn_bwd(qb_n, kb_n, vb_b, o_b32, do_b, lse_b, rows_d + (tok[0, 0] + tok2[0, 0]), cols_d, dilated=True, name="attn_dil_bwd")
    tok = red.to_core("mix", dqb_n)
    dfa_t, db_f = _forget_bwd(dac_a[:, 0, :], fa_t, b_f)
    dproj_p, dgains = _dproj_merge(
        [dqa_n, dka_n, dva, dfa_t.T, dqb_n, dkb_n, dvb, dga, dgb], in_splits, cs, cp,
        {0: (qa, gains["g_q_fox"]), 1: (ka, gains["g_k_fox"]), 4: (qb, gains["g_q_dil"]), 5: (kb, gains["g_k_dil"])})
    dg_qf, dg_kf, dg_qd, dg_kd = dgains[0], dgains[1], dgains[4], dgains[5]
    g_in = _mm(h1, dproj_p, mode="tn", out_dtype=BF16, out_kind="col", tm=t_d2, tn=t_in, tk=s, name="mm_in_dw", deps=(tok,))
    tok = red.swap("in", [g_in], g_in)
    dh1 = _mm(dproj_p, wg["in"], mode="nt", b_kind="col", tm=tm, tn=t_d2, tk=t_in, name="mm_in_dx", deps=(tok,))
    tok = red.to_chips("in", dh1)
    grad_x, dg_attn = _norm_bwd(dh1, x, small["g_attn"], group=d, res=dx1_f, out_dtypes=(F32,), name="rms1_bwd", deps=(tok,))

    small_grads = {
        "g_attn": dg_attn, "b_forget": db_f.reshape(1, N_HEADS),
        "g_q_fox": dg_qf, "g_k_fox": dg_kf, "g_q_dil": dg_qd, "g_k_dil": dg_kd, "g_ffn": dg_ffn,
        "w_conv": jnp.concatenate([st_g[0:3], st_v[0:3]], axis=1),
        "b_conv": jnp.concatenate([st_g[3:4], st_v[3:4]], axis=1),
        "loss": loss_blk[0:1, 0:1],
    }
    return small_grads, grad_x, tok


SMALL_ORDER = ("g_attn", "b_forget", "g_q_fox", "g_k_fox", "g_q_dil", "g_k_dil", "g_ffn", "w_conv", "b_conv", "loss")
WEIGHT_ORDER = ("g_attn", "w_in", "b_forget", "g_q_fox", "g_k_fox", "g_q_dil", "g_k_dil", "w_br_fox", "w_br_dil",
                "w_out", "g_ffn", "w_up", "w_conv", "b_conv", "w_down")
BIG = {"w_in": "in", "w_br_fox": "brf", "w_br_dil": "brd", "w_out": "out", "w_up": "up", "w_down": "down"}


def kernel(x, g_attn, w_in, b_forget, g_q_fox, g_k_fox, g_q_dil, g_k_dil, w_br_fox, w_br_dil, w_out, g_ffn, w_up, w_conv, b_conv, w_down, loss_target, m_g_attn, m_w_in, m_b_forget, m_g_q_fox, m_g_k_fox, m_g_q_dil, m_g_k_dil, m_w_br_fox, m_w_br_dil, m_w_out, m_g_ffn, m_w_up, m_w_conv, m_b_conv, m_w_down, v_g_attn, v_w_in, v_b_forget, v_g_q_fox, v_g_k_fox, v_g_q_dil, v_g_k_dil, v_w_br_fox, v_w_br_dil, v_w_out, v_g_ffn, v_w_up, v_w_conv, v_b_conv, v_w_down):
    w = dict(g_attn=g_attn, w_in=w_in, b_forget=b_forget, g_q_fox=g_q_fox, g_k_fox=g_k_fox, g_q_dil=g_q_dil,
             g_k_dil=g_k_dil, w_br_fox=w_br_fox, w_br_dil=w_br_dil, w_out=w_out, g_ffn=g_ffn, w_up=w_up,
             w_conv=w_conv, b_conv=b_conv, w_down=w_down)
    m = dict(g_attn=m_g_attn, w_in=m_w_in, b_forget=m_b_forget, g_q_fox=m_g_q_fox, g_k_fox=m_g_k_fox,
             g_q_dil=m_g_q_dil, g_k_dil=m_g_k_dil, w_br_fox=m_w_br_fox, w_br_dil=m_w_br_dil, w_out=m_w_out,
             g_ffn=m_g_ffn, w_up=m_w_up, w_conv=m_w_conv, b_conv=m_b_conv, w_down=m_w_down)
    v = dict(g_attn=v_g_attn, w_in=v_w_in, b_forget=v_b_forget, g_q_fox=v_g_q_fox, g_k_fox=v_g_k_fox,
             g_q_dil=v_g_q_dil, g_k_dil=v_g_k_dil, w_br_fox=v_w_br_fox, w_br_dil=v_w_br_dil, w_out=v_w_out,
             g_ffn=v_g_ffn, w_up=v_w_up, w_conv=v_w_conv, b_conv=v_b_conv, w_down=v_w_down)
    xi, yi, ci = lax.axis_index("x"), lax.axis_index("y"), lax.axis_index("c")
    chip = (2 * xi + yi).astype(jnp.int32)
    c_idx = ci.astype(jnp.int32).reshape(1)
    j_idx = chip.reshape(1)

    cs = w_in.shape[2]
    cp = _round_up(cs, LANES)
    shards = {
        "in": jnp.pad(w_in[0].astype(BF16), ((0, 0), (0, cp - cs))),
        "brf": w_br_fox[0].astype(BF16), "brd": w_br_dil[0].astype(BF16), "out": w_out[0].astype(BF16),
        "up": w_up[0].astype(BF16), "down": w_down[0].astype(BF16),
    }
    names = tuple(shards)
    conv_pad = jnp.pad(w_conv[0], ((0, 8 - w_conv.shape[1]), (0, 0)))
    first = [(shards["in"], True), (conv_pad, False)]
    later = {"mid": ("brf", "brd", "out"), "late": ("up", "down")}
    groups = {key: [(shards[n], True) for n in members] for key, members in later.items()}
    (started_first, *started_later), token = _gather_start([first, *groups.values()])
    started = dict(zip(later, started_later))
    token, w["w_in"], m["w_in"], v["w_in"] = lax.optimization_barrier((token, w["w_in"], m["w_in"], v["w_in"]))
    w2, m2, v2 = ({n: a[n].reshape(a[n].shape[-2], a[n].shape[-1]) for n in BIG} for a in (w, m, v))
    early = (token, w2["w_in"], m2["w_in"], v2["w_in"])
    own_first, passed_first, token = _gather_pass(first, started_first, early, "gather_pass_in")
    land_in, land_conv = _gather_wait(first, passed_first, token, "gather_wait_in")
    wg = {"in": land_in, "bconv": b_conv,
          "conv": jnp.transpose(land_conv[:, :w_conv.shape[1], :], (1, 0, 2)).reshape(w_conv.shape[1], -1)}
    small = {n: w[n] for n in ("g_attn", "b_forget", "g_q_fox", "g_k_fox", "g_q_dil", "g_k_dil", "g_ffn")}
    small = {n: (a[0] if a.ndim == 3 else a) for n, a in small.items()}
    in_flight = {}

    def rest_pass(key, after):
        own, passed, tok = _gather_pass(groups[key], started[key], (after,), "gather_pass_" + key)
        in_flight[key] = (own, passed)
        return tok

    def rest_wait(key, after):
        own, passed = in_flight.pop(key)
        lands = _gather_wait(groups[key], passed, after, "gather_wait_" + key)
        return dict(zip(later[key], lands))

    reducer = _Reducer(jnp.stack([chip, ci.astype(jnp.int32)]))
    small_grads, grad_x, tok_in = _layer_grads(x[0], loss_target[0], small, wg, rest_pass, rest_wait, reducer)

    mine, theirs = {}, {}
    for key, members in (("down", ("down",)), ("up", ("up",)), ("mix", ("out", "brf", "brd"))):
        mine_k, theirs_k = reducer.finish(key, grad_x)
        mine.update(zip(members, mine_k))
        theirs.update(zip(members, theirs_k))

    flat = jnp.concatenate([small_grads[n].reshape(-1) for n in SMALL_ORDER])
    rows = _round_up(flat.shape[0], 8 * LANES) // LANES
    pack = jnp.pad(flat, (0, rows * LANES - flat.shape[0])).reshape(rows, LANES)
    total = _sum_devices(_gather_packs(pack)).reshape(-1)
    red, at = {}, 0
    for n in SMALL_ORDER:
        size = small_grads[n].size
        red[n] = total[at:at + size].reshape(small_grads[n].shape)
        at += size
    loss = red["loss"].reshape(())
    c2 = w_conv.shape[2]
    red["w_conv"] = lax.dynamic_slice_in_dim(red["w_conv"], chip * c2, c2, axis=1)

    g_out, d_out, m_out, v_out = {}, {}, {}, {}
    last = [n for n in WEIGHT_ORDER if n != "w_in"] + ["w_in"]
    for n in last:
        shape = w[n].shape
        r2 = (shape[-2], shape[-1]) if n not in ("g_attn", "b_forget", "g_ffn", "b_conv") else (1, shape[-1])
        if n == "w_in":
            done = jnp.stack([v_out[k][(0,) * v_out[k].ndim] for k in last[:-1]])
            tok = reducer.to_core("in", done)
            (mine_in,), (theirs_in,) = reducer.finish("in", tok)
            mine["in"], theirs["in"] = mine_in[:, :cs], theirs_in[:, :cs]
        if n in BIG:
            g2, dl, mn, vn = _adamw_halves(w2[n], mine[BIG[n]], theirs[BIG[n]], c_idx, m2[n], v2[n],
                                           name="adamw_" + n, deps=(tok_in,))
        else:
            g2 = red[n].reshape(r2)
            dl, mn, vn = _adamw(w[n].reshape(r2), g2, m[n].reshape(r2), v[n].reshape(r2), name="adamw_" + n,
                                deps=(tok_in,))
        g_out[n], d_out[n], m_out[n], v_out[n] = (a.reshape(shape) for a in (g2, dl, mn, vn))

    return (loss, grad_x[None], *[g_out[n] for n in WEIGHT_ORDER], *[d_out[n] for n in WEIGHT_ORDER],
            *[m_out[n] for n in WEIGHT_ORDER], *[v_out[n] for n in WEIGHT_ORDER])
```

```python
import functools
import math

import jax
import jax.numpy as jnp
import numpy as np
from jax import lax
from jax.experimental import pallas as pl
from jax.experimental.pallas import tpu as pltpu

F32 = jnp.float32
BF16 = jnp.bfloat16
HEAD_DIM = 128
N_HEADS = 8
EPS = 1e-6
NEG = -1e30
N_CHIPS = 4
N_DEV = 8
LANES = 128
VMEM_LIMIT_BYTES = 56 * 1024 * 1024
DIL_PATTERNS = ((128, 1), (512, 4), (2048, 16))
ATTN_TILE = 512
ADAM_LR, ADAM_B1, ADAM_B2, ADAM_EPS, ADAM_WD, ADAM_STEP = 0.001, 0.9, 0.999, 1e-08, 0.01, 10
MESH = pl.DeviceIdType.MESH


def _params(*sem):
    return pltpu.CompilerParams(dimension_semantics=sem, vmem_limit_bytes=VMEM_LIMIT_BYTES)


def _round_up(n, m):
    return -(-n // m) * m


def _pick(dim, prefs):
    for p in prefs:
        if dim % p == 0:
            return p
    raise ValueError(f"no tile for {dim} in {prefs}")


def _logical_shape(arr, kind):
    if kind is None:
        return arr.shape
    s, r, c = arr.shape
    return (r, s * c) if kind == "col" else (s * r, c)


def _spec(shape, kind, br, bc, fi, fj):
    if kind is None:
        return pl.BlockSpec((br, bc), lambda *g: (fi(*g), fj(*g)))
    _, r, c = shape
    if kind == "col":
        nb = c // bc
        assert nb * bc == c, (shape, bc)
        return pl.BlockSpec((None, br, bc), lambda *g: (fj(*g) // nb, fi(*g), fj(*g) % nb))
    nb = r // br
    assert nb * br == r, (shape, br)
    return pl.BlockSpec((None, br, bc), lambda *g: (fi(*g) // nb, fi(*g) % nb, fj(*g)))


def _mm(a, b, *, mode, tm, tn, tk, name, a_kind=None, b_kind=None, out_kind=None,
        out_dtype=F32, res=None, deps=()):
    la, lb = _logical_shape(a, a_kind), _logical_shape(b, b_kind)
    if mode == "nn":
        (m, k), (k2, n) = la, lb
    elif mode == "nt":
        (m, k), (n, k2) = la, lb
    else:
        (k, m), (k2, n) = la, lb
    assert k == k2, (name, la, lb)
    assert m % tm == 0 and n % tn == 0 and k % tk == 0, (name, m, n, k, tm, tn, tk)
    nk = k // tk
    im = lambda i, j, l: i
    jn = lambda i, j, l: j
    lk = lambda i, j, l: l
    if mode == "tn":
        a_spec = _spec(a.shape, a_kind, tk, tm, lk, im)
        dims = (((0,), (0,)), ((), ()))
    else:
        a_spec = _spec(a.shape, a_kind, tm, tk, im, lk)
        dims = (((1,), (1,)), ((), ())) if mode == "nt" else (((1,), (0,)), ((), ()))
    if mode == "nt":
        b_spec = _spec(b.shape, b_kind, tn, tk, jn, lk)
    else:
        b_spec = _spec(b.shape, b_kind, tk, tn, lk, jn)
    if out_kind is None:
        oshape = (m, n)
    elif out_kind == "col":
        oshape = (N_CHIPS, m, n // N_CHIPS)
    else:
        oshape = (N_CHIPS, m // N_CHIPS, n)
    o_spec = _spec(oshape, out_kind, tm, tn, im, jn)
    in_specs = [a_spec, b_spec]
    args = [a, b]
    if res is not None:
        in_specs.append(pl.BlockSpec((tm, tn), lambda i, j, l: (i, j)))
        args.append(res)
    in_specs += [pl.BlockSpec(memory_space=pl.ANY)] * len(deps)
    args += list(deps)

    def finish(out, res_ref, o_ref):
        if res_ref is not None:
            out = out + res_ref[...]
        o_ref[...] = out.astype(o_ref.dtype)

    def body_whole_k(*refs):
        res_ref = refs[2] if res is not None else None
        finish(lax.dot_general(refs[0][...], refs[1][...], dims, preferred_element_type=F32), res_ref, refs[-1])

    def body(*refs):
        a_ref, b_ref = refs[0], refs[1]
        res_ref = refs[2] if res is not None else None
        o_ref, acc_ref = refs[-2], refs[-1]
        step = pl.program_id(2)

        @pl.when(step == 0)
        def _():
            acc_ref[...] = jnp.zeros_like(acc_ref)

        acc_ref[...] += lax.dot_general(a_ref[...], b_ref[...], dims, preferred_element_type=F32)

        @pl.when(step == nk - 1)
        def _():
            finish(acc_ref[...], res_ref, o_ref)

    return pl.pallas_call(
        body_whole_k if nk == 1 else body, name=name, grid=(m // tm, n // tn, nk),
        in_specs=in_specs, out_specs=o_spec,
        out_shape=jax.ShapeDtypeStruct(oshape, out_dtype),
        scratch_shapes=[] if nk == 1 else [pltpu.VMEM((tm, tn), F32)],
        compiler_params=_params("parallel", "parallel", "arbitrary"),
    )(*args)


def _pieces(splits, cs, cp):
    out, g0 = [], 0
    for width in splits:
        g1, runs = g0 + width, []
        for j in range(N_CHIPS):
            a, b = max(g0, cs * j), min(g1, cs * (j + 1))
            if a < b:
                runs.append((j * cp + a - cs * j, a - g0, b - a))
        out.append(runs)
        g0 = g1
    return out


def _head_norm(xv, gv):
    r = lax.rsqrt(jnp.mean(xv * xv, axis=-1, keepdims=True) + EPS)
    return (xv * r) * gv


def _head_norm_bwd(dyv, xv, gv):
    r = lax.rsqrt(jnp.mean(xv * xv, axis=-1, keepdims=True) + EPS)
    xr = xv * r
    gdy = dyv * gv
    return r * (gdy - xr * jnp.mean(gdy * xr, axis=-1, keepdims=True)), jnp.sum(dyv * xr, axis=0, keepdims=True)


def _proj_split(proj_p, splits, cs, dtypes, gains, tm=128):
    s, wp = proj_p.shape
    pieces = _pieces(splits, cs, wp // N_CHIPS)
    normed = sorted(gains)
    nseg = len(splits)

    def body(p_ref, *refs):
        g_refs, o_refs, n_refs = refs[:len(normed)], refs[len(normed):len(normed) + nseg], refs[len(normed) + nseg:]
        for o_ref, runs in zip(o_refs, pieces):
            for src, dst, n in runs:
                o_ref[:, dst:dst + n] = p_ref[:, src:src + n].astype(o_ref.dtype)
        for g_ref, n_ref, i in zip(g_refs, n_refs, normed):
            for c0 in range(0, splits[i], HEAD_DIM):
                cols = slice(c0, c0 + HEAD_DIM)
                n_ref[:, cols] = _head_norm(o_refs[i][:, cols], g_ref[:, cols]).astype(n_ref.dtype)

    return pl.pallas_call(
        body, name="proj_split", grid=(s // tm,),
        in_specs=[pl.BlockSpec((tm, wp), lambda i: (i, 0))] + [pl.BlockSpec((1, splits[i]), lambda i: (0, 0)) for i in normed],
        out_specs=[pl.BlockSpec((tm, w), lambda i: (i, 0)) for w in splits]
        + [pl.BlockSpec((tm, splits[i]), lambda i: (i, 0)) for i in normed],
        out_shape=[jax.ShapeDtypeStruct((s, w), dt) for w, dt in zip(splits, dtypes)]
        + [jax.ShapeDtypeStruct((s, splits[i]), BF16) for i in normed],
        compiler_params=_params("parallel"),
    )(proj_p, *[gains[i] for i in normed])


def _dproj_merge(parts, splits, cs, cp, norms, tm=128):
    s = parts[0].shape[0]
    wp = N_CHIPS * cp
    pieces = _pieces(splits, cs, cp)
    normed = sorted(norms)
    nseg, nn = len(splits), len(normed)

    def body(*refs):
        p_refs, x_refs, g_refs = refs[:nseg], refs[nseg:nseg + nn], refs[nseg + nn:nseg + 2 * nn]
        o_ref, dg_refs = refs[nseg + 2 * nn], refs[nseg + 2 * nn + 1:nseg + 3 * nn + 1]
        stage, tmp = refs[-2], refs[-1]

        @pl.when(pl.program_id(0) == 0)
        def _():
            for dg_ref in dg_refs:
                dg_ref[...] = jnp.zeros_like(dg_ref)

        for j in range(N_CHIPS):
            stage[:, j * cp + cs:(j + 1) * cp] = jnp.zeros((tm, cp - cs), F32)
        for i, (p_ref, runs) in enumerate(zip(p_refs, pieces)):
            src_ref = p_ref
            if i in norms:
                k = normed.index(i)
                for c0 in range(0, splits[i], HEAD_DIM):
                    cols = slice(c0, c0 + HEAD_DIM)
                    dx, dg = _head_norm_bwd(p_ref[:, cols].astype(F32), x_refs[k][:, cols], g_refs[k][:, cols])
                    tmp[:, cols] = dx
                    dg_refs[k][:, cols] += dg
                src_ref = tmp
            for dst, src, n in runs:
                stage[:, dst:dst + n] = src_ref[:, src:src + n].astype(F32)
        o_ref[...] = stage[...].astype(o_ref.dtype)

    wmax = max(splits[i] for i in normed)
    row = lambda w: pl.BlockSpec((tm, w), lambda i: (i, 0))
    vec = lambda w: pl.BlockSpec((1, w), lambda i: (0, 0))
    outs = pl.pallas_call(
        body, name="dproj_merge", grid=(s // tm,),
        in_specs=[row(w) for w in splits] + [row(splits[i]) for i in normed] + [vec(splits[i]) for i in normed],
        out_specs=[row(wp)] + [vec(splits[i]) for i in normed],
        out_shape=[jax.ShapeDtypeStruct((s, wp), BF16)] + [jax.ShapeDtypeStruct((1, splits[i]), F32) for i in normed],
        scratch_shapes=[pltpu.VMEM((tm, wp), F32), pltpu.VMEM((tm, wmax), F32)],
        compiler_params=_params("arbitrary"),
    )(*parts, *[norms[i][0] for i in normed], *[norms[i][1] for i in normed])
    return outs[0], dict(zip(normed, outs[1:]))


def _norm_fwd(x, g, *, group, name, tm=256):
    s, w = x.shape
    ng = w // group

    def body(x_ref, g_ref, o_ref):
        for i in range(ng):
            cols = slice(i * group, (i + 1) * group)
            xv = x_ref[:, cols]
            r = lax.rsqrt(jnp.mean(xv * xv, axis=-1, keepdims=True) + EPS)
            o_ref[:, cols] = ((xv * r) * g_ref[:, cols]).astype(o_ref.dtype)

    return pl.pallas_call(
        body, name=name, grid=(s // tm,),
        in_specs=[pl.BlockSpec((tm, w), lambda i: (i, 0)), pl.BlockSpec((1, w), lambda i: (0, 0))],
        out_specs=pl.BlockSpec((tm, w), lambda i: (i, 0)),
        out_shape=jax.ShapeDtypeStruct((s, w), BF16),
        compiler_params=_params("parallel"),
    )(x, g)


def _norm_bwd(dy, x, g, *, group, name, res=None, out_dtypes=(BF16,), tm=256, deps=()):
    s, w = x.shape
    ng = w // group
    n_in = 4 if res is not None else 3

    def body(*refs):
        dy_ref, x_ref, g_ref = refs[:3]
        res_ref = refs[3] if res is not None else None
        outs = refs[n_in + len(deps):]
        dx_refs, dg_ref = outs[:-1], outs[-1]

        @pl.when(pl.program_id(0) == 0)
        def _():
            dg_ref[...] = jnp.zeros_like(dg_ref)

        for i in range(ng):
            cols = slice(i * group, (i + 1) * group)
            xv = x_ref[:, cols]
            dyv = dy_ref[:, cols].astype(F32)
            r = lax.rsqrt(jnp.mean(xv * xv, axis=-1, keepdims=True) + EPS)
            xr = xv * r
            dg_ref[:, cols] += jnp.sum(dyv * xr, axis=0, keepdims=True)
            gdy = dyv * g_ref[:, cols]
            dx = r * (gdy - xr * jnp.mean(gdy * xr, axis=-1, keepdims=True))
            if res_ref is not None:
                dx = dx + res_ref[:, cols]
            for dx_ref in dx_refs:
                dx_ref[:, cols] = dx.astype(dx_ref.dtype)

    row = pl.BlockSpec((tm, w), lambda i: (i, 0))
    vec = pl.BlockSpec((1, w), lambda i: (0, 0))
    in_specs = [row, row, vec] + ([row] if res is not None else []) + [pl.BlockSpec(memory_space=pl.ANY)] * len(deps)
    args = [dy, x, g] + ([res] if res is not None else []) + list(deps)
    out_specs = [row] * len(out_dtypes) + [vec]
    out_shape = [jax.ShapeDtypeStruct((s, w), dt) for dt in out_dtypes] + [jax.ShapeDtypeStruct((1, w), F32)]
    return pl.pallas_call(
        body, name=name, grid=(s // tm,), in_specs=in_specs, out_specs=out_specs,
        out_shape=out_shape, compiler_params=_params("arbitrary"),
    )(*args)


def _split3(v):
    p1 = v.astype(BF16)
    r1 = v - p1.astype(F32)
    p2 = r1.astype(BF16)
    p3 = (r1 - p2.astype(F32)).astype(BF16)
    return p1, p2, p3


def _tri_sum(v, reverse, tcol=512):
    h, s = v.shape
    tcol = min(tcol, s)
    parts = _split3(v)
    outs = []
    for j in range(s // tcol):
        src = lax.broadcasted_iota(jnp.int32, (s, tcol), 0)
        dst = lax.broadcasted_iota(jnp.int32, (s, tcol), 1) + j * tcol
        keep = (src >= dst) if reverse else (src <= dst)
        tri = jnp.where(keep, 1.0, 0.0).astype(BF16)
        acc = jnp.zeros((h, tcol), F32)
        for p in parts:
            acc = acc + jnp.dot(p, tri, preferred_element_type=F32)
        outs.append(acc)
    return outs


def _forget_fwd(fa_t, b):
    h, s = fa_t.shape
    tcol = min(512, s)

    def body(f_ref, b_ref, c_ref):
        z = f_ref[...] + b_ref[...]
        logf = jnp.minimum(z, 0.0) - jnp.log(1.0 + jnp.exp(-jnp.abs(z)))
        for j, blk in enumerate(_tri_sum(logf, reverse=False, tcol=tcol)):
            c_ref[:, j * tcol:(j + 1) * tcol] = blk

    return pl.pallas_call(
        body, name="forget_fwd", out_shape=jax.ShapeDtypeStruct((h, s), F32),
        compiler_params=_params(),
    )(fa_t, b)


def _forget_bwd(dacol, fa_t, b):
    h, s = fa_t.shape
    tcol = min(512, s)

    def body(d_ref, f_ref, b_ref, dfa_ref, db_ref):
        z = f_ref[...] + b_ref[...]
        dc = -d_ref[...]
        total = jnp.zeros((h, 1), F32)
        for j, blk in enumerate(_tri_sum(dc, reverse=True, tcol=tcol)):
            cols = slice(j * tcol, (j + 1) * tcol)
            dfa = blk * (1.0 - jax.nn.sigmoid(z[:, cols]))
            dfa_ref[:, cols] = dfa
            total = total + jnp.sum(dfa, axis=-1, keepdims=True)
        db_ref[...] = total

    return pl.pallas_call(
        body, name="forget_bwd",
        out_shape=[jax.ShapeDtypeStruct((h, s), F32), jax.ShapeDtypeStruct((h, 1), F32)],
        compiler_params=_params(),
    )(dacol, fa_t, b)


def _distance_bias(s, tile, dilated):
    nb = s // tile
    b = lax.broadcasted_iota(jnp.int32, (nb, tile, tile), 0)
    dist = b * tile + lax.broadcasted_iota(jnp.int32, (nb, tile, tile), 1) - lax.broadcasted_iota(jnp.int32, (nb, tile, tile), 2)
    if not dilated:
        return jnp.where(dist >= 0, 0.0, NEG).astype(F32)
    mult = jnp.zeros(dist.shape, jnp.int32)
    for window, dil in DIL_PATTERNS:
        mult = mult + ((dist >= 0) & (dist <= window) & ((dist & (dil - 1)) == 0)).astype(jnp.int32)
    logm = jnp.where(mult == 3, math.log(3.0), jnp.where(mult == 2, math.log(2.0), 0.0))
    return jnp.where(mult > 0, logm, NEG).astype(F32)


def _logits(q, k, arow, acol, bias):
    s = lax.dot_general(q, k, (((1,), (1,)), ((), ())), preferred_element_type=F32)
    return s * (1.0 / math.sqrt(HEAD_DIM)) + arow - acol + bias


def _attn_fwd(q, k, v, arow, acol, *, dilated, name, tq=ATTN_TILE, tk=ATTN_TILE):
    two_term = not dilated
    s, w = q.shape
    nh = w // HEAD_DIM
    assert tq == tk
    tq = tk = min(tq, s)
    nq, nk = s // tq, s // tk

    def body(q_ref, k_ref, v_ref, ar_ref, ac_ref, b_ref, o_ref, of_ref, lse_ref, m_ref, l_ref, acc_ref):
        qi, ki = pl.program_id(1), pl.program_id(2)

        @pl.when(ki == 0)
        def _():
            m_ref[...] = jnp.full_like(m_ref, NEG)
            l_ref[...] = jnp.zeros_like(l_ref)
            acc_ref[...] = jnp.zeros_like(acc_ref)

        @pl.when(ki <= qi)
        def _():
            sc = _logits(q_ref[...], k_ref[...], ar_ref[...], ac_ref[...], b_ref[...])
            m_new = jnp.maximum(m_ref[...], jnp.max(sc, axis=-1, keepdims=True))
            alpha = jnp.exp(m_ref[...] - m_new)
            p = jnp.exp(sc - m_new)
            l_ref[...] = alpha * l_ref[...] + jnp.sum(p, axis=-1, keepdims=True)
            p_hi = p.astype(BF16)
            vv = v_ref[...]
            pv = jnp.dot(p_hi, vv, preferred_element_type=F32)
            if two_term:
                pv = pv + jnp.dot((p - p_hi.astype(F32)).astype(BF16), vv, preferred_element_type=F32)
            acc_ref[...] = alpha * acc_ref[...] + pv
            m_ref[...] = m_new

        @pl.when(ki == nk - 1)
        def _():
            out = acc_ref[...] / l_ref[...]
            o_ref[...] = out.astype(o_ref.dtype)
            of_ref[...] = out
            lse_ref[...] = m_ref[...] + jnp.log(l_ref[...])

    kv = pl.BlockSpec((tk, HEAD_DIM), lambda h, i, j: (jnp.minimum(j, i), h))
    return pl.pallas_call(
        body, name=name, grid=(nh, nq, nk),
        in_specs=[pl.BlockSpec((tq, HEAD_DIM), lambda h, i, j: (i, h)), kv, kv,
                  pl.BlockSpec((None, tq, 1), lambda h, i, j: (h, i, 0)),
                  pl.BlockSpec((None, 1, tk), lambda h, i, j: (h, 0, jnp.minimum(j, i))),
                  pl.BlockSpec((None, tq, tk), lambda h, i, j: (jnp.maximum(i - j, 0), 0, 0))],
        out_specs=[pl.BlockSpec((tq, HEAD_DIM), lambda h, i, j: (i, h)),
                   pl.BlockSpec((tq, HEAD_DIM), lambda h, i, j: (i, h)),
                   pl.BlockSpec((None, tq, 1), lambda h, i, j: (h, i, 0))],
        out_shape=[jax.ShapeDtypeStruct((s, w), BF16), jax.ShapeDtypeStruct((s, w), F32),
                   jax.ShapeDtypeStruct((nh, s, 1), F32)],
        scratch_shapes=[pltpu.VMEM((tq, 1), F32), pltpu.VMEM((tq, 1), F32), pltpu.VMEM((tq, HEAD_DIM), F32)],
        compiler_params=_params("parallel", "parallel", "arbitrary"),
    )(q, k, v, arow, acol, _distance_bias(s, tq, dilated))


def _attn_bwd(q, k, v, o, do, lse, arow, acol, *, dilated, name, tq=ATTN_TILE, tk=ATTN_TILE):
    s, w = q.shape
    nh = w // HEAD_DIM
    assert tq == tk
    tq = tk = min(tq, s)
    nq, nk = s // tq, s // tk
    scale = 1.0 / math.sqrt(HEAD_DIM)

    def body(q_ref, k_ref, v_ref, o_ref, do_ref, lse_ref, ar_ref, ac_ref, b_ref,
             dq_ref, dk_ref, dv_ref, dac_ref, dk_acc, dv_acc, dac_acc):
        ki, qi = pl.program_id(1), pl.program_id(2)

        @pl.when((ki == 0) & (qi == 0))
        def _():
            dq_ref[...] = jnp.zeros_like(dq_ref)

        @pl.when(qi == 0)
        def _():
            dk_acc[...] = jnp.zeros_like(dk_acc)
            dv_acc[...] = jnp.zeros_like(dv_acc)
            dac_acc[...] = jnp.zeros_like(dac_acc)

        @pl.when(qi >= ki)
        def _():
            qv, kvv, dov = q_ref[...], k_ref[...], do_ref[...]
            sc = _logits(qv, kvv, ar_ref[...], ac_ref[...], b_ref[...])
            p = jnp.exp(sc - lse_ref[...])
            dp = lax.dot_general(dov, v_ref[...], (((1,), (1,)), ((), ())), preferred_element_type=F32)
            delta = jnp.sum(dov.astype(F32) * o_ref[...].astype(F32), axis=-1, keepdims=True)
            ds = p * (dp - delta)
            dsb = ds.astype(BF16)
            dv_acc[...] += lax.dot_general(p.astype(BF16), dov, (((0,), (0,)), ((), ())), preferred_element_type=F32)
            dk_acc[...] += lax.dot_general(dsb, qv, (((0,), (0,)), ((), ())), preferred_element_type=F32)
            rows = pl.ds(pl.multiple_of(qi * tq, tq), tq)
            dq_ref[rows, :] += jnp.dot(dsb, kvv, preferred_element_type=F32) * scale
            dac_acc[...] += jnp.sum(ds, axis=0, keepdims=True)

        @pl.when(qi == nq - 1)
        def _():
            dk_ref[...] = dk_acc[...] * scale
            dv_ref[...] = dv_acc[...]
            dac_ref[...] = dac_acc[...]

    qs = pl.BlockSpec((tq, HEAD_DIM), lambda h, j, i: (jnp.maximum(i, j), h))
    ks = pl.BlockSpec((tk, HEAD_DIM), lambda h, j, i: (j, h))
    rowv = pl.BlockSpec((None, tq, 1), lambda h, j, i: (h, jnp.maximum(i, j), 0))
    colv = pl.BlockSpec((None, 1, tk), lambda h, j, i: (h, 0, j))
    return pl.pallas_call(
        body, name=name, grid=(nh, nk, nq),
        in_specs=[qs, ks, ks, qs, qs, rowv, rowv, colv,
                  pl.BlockSpec((None, tq, tk), lambda h, j, i: (jnp.maximum(i - j, 0), 0, 0))],
        out_specs=[pl.BlockSpec((s, HEAD_DIM), lambda h, j, i: (0, h)), ks, ks, colv],
        out_shape=[jax.ShapeDtypeStruct((s, w), F32), jax.ShapeDtypeStruct((s, w), F32),
                   jax.ShapeDtypeStruct((s, w), F32), jax.ShapeDtypeStruct((nh, 1, s), F32)],
        scratch_shapes=[pltpu.VMEM((tk, HEAD_DIM), F32), pltpu.VMEM((tk, HEAD_DIM), F32), pltpu.VMEM((1, tk), F32)],
        compiler_params=_params("arbitrary", "arbitrary", "arbitrary"),
    )(q, k, v, o, do, lse, arow, acol, _distance_bias(s, tq, dilated))


def _gate_fwd(ga, gb, pa, pb, tm=256):
    s, d = ga.shape

    def body(ga_ref, gb_ref, pa_ref, pb_ref, o_ref):
        o_ref[...] = (jax.nn.sigmoid(ga_ref[...]) * pa_ref[...]
                      + jax.nn.sigmoid(gb_ref[...]) * pb_ref[...]).astype(o_ref.dtype)

    row = pl.BlockSpec((tm, d), lambda i: (i, 0))
    return pl.pallas_call(
        body, name="gate_fwd", grid=(s // tm,), in_specs=[row] * 4, out_specs=row,
        out_shape=jax.ShapeDtypeStruct((s, d), BF16), compiler_params=_params("parallel"),
    )(ga, gb, pa, pb)


def _gate_bwd(dm, ga, gb, pa, pb, tm=256):
    s, d = ga.shape

    def body(dm_ref, ga_ref, gb_ref, pa_ref, pb_ref, dpa_ref, dpb_ref, dga_ref, dgb_ref):
        dmv = dm_ref[...]
        for g_ref, p_ref, dp_ref, dg_ref in ((ga_ref, pa_ref, dpa_ref, dga_ref), (gb_ref, pb_ref, dpb_ref, dgb_ref)):
            sg = jax.nn.sigmoid(g_ref[...])
            dp_ref[...] = (dmv * sg).astype(BF16)
            dg_ref[...] = (dmv * p_ref[...] * (sg * (1.0 - sg))).astype(BF16)

    row = pl.BlockSpec((tm, d), lambda i: (i, 0))
    return pl.pallas_call(
        body, name="gate_bwd", grid=(s // tm,), in_specs=[row] * 5, out_specs=[row] * 4,
        out_shape=[jax.ShapeDtypeStruct((s, d), BF16)] * 4, compiler_params=_params("parallel"),
    )(dm, ga, gb, pa, pb)


def _shift_down(u, k):
    row = lax.broadcasted_iota(jnp.int32, u.shape, 0)
    return jnp.where(row >= k, pltpu.roll(u, k, 0), 0.0)


def _shift_up(u, k):
    n = u.shape[0]
    row = lax.broadcasted_iota(jnp.int32, u.shape, 0)
    return jnp.where(row < n - k, pltpu.roll(u, n - k, 0), 0.0)


def _conv3(u, wc, b):
    return wc[0:1, :] * _shift_down(u, 2) + wc[1:2, :] * _shift_down(u, 1) + wc[2:3, :] * u + b


def _conv_glu_fwd(u, wc, b, tn=256):
    s, f2 = u.shape
    f = f2 // 2
    nb = f // tn

    def body(ug_ref, uv_ref, wg_ref, wv_ref, bg_ref, bv_ref, o_ref):
        cg = _conv3(ug_ref[...], wg_ref[...], bg_ref[...])
        cv = _conv3(uv_ref[...], wv_ref[...], bv_ref[...])
        o_ref[...] = (cg * jax.nn.sigmoid(cg) * cv).astype(o_ref.dtype)

    def cols(rows, off):
        return pl.BlockSpec((rows, tn), lambda j: (0, j + off))

    return pl.pallas_call(
        body, name="conv_glu_fwd", grid=(nb,),
        in_specs=[cols(s, 0), cols(s, nb), cols(3, 0), cols(3, nb), cols(1, 0), cols(1, nb)],
        out_specs=cols(s, 0), out_shape=jax.ShapeDtypeStruct((s, f), BF16),
        compiler_params=_params("parallel"),
    )(u, u, wc, wc, b, b)


def _conv_glu_bwd(u, da, wc, b, tn=256):
    s, f2 = u.shape
    f = f2 // 2
    nb = f // tn

    def body(ug_ref, uv_ref, da_ref, wg_ref, wv_ref, bg_ref, bv_ref, dug_ref, duv_ref, sg_ref, sv_ref):
        ug, uv, wg, wv = ug_ref[...], uv_ref[...], wg_ref[...], wv_ref[...]
        cg = _conv3(ug, wg, bg_ref[...])
        cv = _conv3(uv, wv, bv_ref[...])
        sig = jax.nn.sigmoid(cg)
        dav = da_ref[...]
        dcv = dav * (cg * sig)
        dcg = dav * cv * (sig * (1.0 + cg * (1.0 - sig)))
        for dc, uu, w, du_ref, st_ref in ((dcg, ug, wg, dug_ref, sg_ref), (dcv, uv, wv, duv_ref, sv_ref)):
            du = w[2:3, :] * dc + w[1:2, :] * _shift_up(dc, 1) + w[0:1, :] * _shift_up(dc, 2)
            du_ref[...] = du.astype(BF16)
            st_ref[...] = jnp.zeros_like(st_ref)
            st_ref[0:1, :] = jnp.sum(dc * _shift_down(uu, 2), axis=0, keepdims=True)
            st_ref[1:2, :] = jnp.sum(dc * _shift_down(uu, 1), axis=0, keepdims=True)
            st_ref[2:3, :] = jnp.sum(dc * uu, axis=0, keepdims=True)
            st_ref[3:4, :] = jnp.sum(dc, axis=0, keepdims=True)

    def cols(rows, off):
        return pl.BlockSpec((rows, tn), lambda j: (0, j + off))

    return pl.pallas_call(
        body, name="conv_glu_bwd", grid=(nb,),
        in_specs=[cols(s, 0), cols(s, nb), cols(s, 0), cols(3, 0), cols(3, nb), cols(1, 0), cols(1, nb)],
        out_specs=[cols(s, 0), cols(s, 0), cols(8, 0), cols(8, 0)],
        out_shape=[jax.ShapeDtypeStruct((s, f), BF16), jax.ShapeDtypeStruct((s, f), BF16),
                   jax.ShapeDtypeStruct((8, f), F32), jax.ShapeDtypeStruct((8, f), F32)],
        compiler_params=_params("parallel"),
    )(u, u, da, wc, wc, b, b)


def _loss_head(y, target, tm=256):
    s, d = y.shape

    def body(y_ref, t_ref, dyf_ref, dyb_ref, l_ref):
        @pl.when(pl.program_id(0) == 0)
        def _():
            l_ref[...] = jnp.zeros_like(l_ref)

        err = y_ref[...] - t_ref[...]
        dy = err * (1.0 / d)
        dyf_ref[...] = dy
        dyb_ref[...] = dy.astype(BF16)
        l_ref[...] += 0.5 * jnp.sum(jnp.sum(err * err, axis=-1, keepdims=True) * (1.0 / d), axis=0, keepdims=True)

    row = pl.BlockSpec((tm, d), lambda i: (i, 0))
    return pl.pallas_call(
        body, name="loss_head", grid=(s // tm,), in_specs=[row, row],
        out_specs=[row, row, pl.BlockSpec((8, LANES), lambda i: (0, 0))],
        out_shape=[jax.ShapeDtypeStruct((s, d), F32), jax.ShapeDtypeStruct((s, d), BF16),
                   jax.ShapeDtypeStruct((8, LANES), F32)],
        compiler_params=_params("arbitrary"),
    )(y, target)


ROW_TILES = (256, 128, 64, 32, 16, 8)
BLOCK_BYTES = 2 << 20


def _add_halves(g, r1, place):
    ns, r, c = g.shape
    rh = r // 2
    tr = _pick(rh, ROW_TILES)
    g4 = g.reshape(ns, 2, rh, c)

    def body(p_ref, g_ref, r_ref, o_ref):
        o_ref[...] = (g_ref[...].astype(F32) + r_ref[...].astype(F32)).astype(o_ref.dtype)

    def slab(s, pr):
        return s + (s >= pr[0]).astype(jnp.int32)

    return pl.pallas_call(
        body, name="add_halves",
        grid_spec=pltpu.PrefetchScalarGridSpec(
            num_scalar_prefetch=1, grid=(ns - 1, rh // tr),
            in_specs=[pl.BlockSpec((None, None, tr, c), lambda s, i, pr: (slab(s, pr), pr[1], i, 0)),
                      pl.BlockSpec((None, tr, c), lambda s, i, pr: (slab(s, pr), i, 0))],
            out_specs=pl.BlockSpec((None, tr, c), lambda s, i, pr: (slab(s, pr), i, 0))),
        out_shape=jax.ShapeDtypeStruct((ns, rh, c), BF16),
        compiler_params=_params("parallel", "parallel"),
    )(place, g4, r1)


def _sum_chips(g, r1, recv, place):
    ns, r, c = g.shape
    rh = r // 2
    tr = _pick(rh, ROW_TILES)
    g4 = g.reshape(ns, 2, rh, c)

    def body(p_ref, g_ref, r_ref, t0_ref, t1_ref, t2_ref, o_ref):
        own = g_ref[...].astype(F32) + r_ref[...].astype(F32)
        o_ref[...] = ((own + t0_ref[...].astype(F32)) + t1_ref[...].astype(F32)) + t2_ref[...].astype(F32)

    def peer(k):
        return pl.BlockSpec((None, tr, c), lambda i, pr: (k, i, 0))

    return pl.pallas_call(
        body, name="sum_chips",
        grid_spec=pltpu.PrefetchScalarGridSpec(
            num_scalar_prefetch=1, grid=(rh // tr,),
            in_specs=[pl.BlockSpec((None, None, tr, c), lambda i, pr: (pr[0], pr[1], i, 0)),
                      pl.BlockSpec((None, tr, c), lambda i, pr: (pr[0], i, 0)), peer(0), peer(1), peer(2)],
            out_specs=pl.BlockSpec((tr, c), lambda i, pr: (i, 0))),
        out_shape=jax.ShapeDtypeStruct((rh, c), F32),
        compiler_params=_params("parallel"),
    )(place, g4, r1, recv, recv, recv)


def _sum_devices(packs):
    n, r, c = packs.shape

    def body(p_ref, o_ref):
        acc = p_ref[0]
        for d in range(1, n):
            acc = acc + p_ref[d]
        o_ref[...] = acc

    return pl.pallas_call(
        body, name="sum_devices", out_shape=jax.ShapeDtypeStruct((r, c), F32), compiler_params=_params(),
    )(packs)


def _adamw_update(wv, gv, mv, vv):
    c1 = 1.0 - ADAM_B1 ** ADAM_STEP
    c2 = 1.0 - ADAM_B2 ** ADAM_STEP
    mn = ADAM_B1 * mv + (1.0 - ADAM_B1) * gv
    vn = ADAM_B2 * vv + (1.0 - ADAM_B2) * (gv * gv)
    m_hat = mn / c1
    v_hat = vn / c2
    return -ADAM_LR * (m_hat / (jnp.sqrt(v_hat) + ADAM_EPS) + ADAM_WD * wv), mn, vn


def _adamw(w, g, m, v, name, deps=()):
    r, c = w.shape
    tr = _pick(r, ROW_TILES) if r >= 8 else r

    def body(w_ref, g_ref, m_ref, v_ref, *rest):
        d_ref, mo_ref, vo_ref = rest[-3:]
        d_ref[...], mo_ref[...], vo_ref[...] = _adamw_update(w_ref[...], g_ref[...], m_ref[...], v_ref[...])

    blk = pl.BlockSpec((tr, c), lambda i: (i, 0))
    return pl.pallas_call(
        body, name=name, grid=(r // tr,), in_specs=[blk] * 4 + [ANY] * len(deps), out_specs=[blk] * 3,
        out_shape=[jax.ShapeDtypeStruct((r, c), F32)] * 3, compiler_params=_params("parallel"),
    )(w, g, m, v, *deps)


def _adamw_halves(w, mine, theirs, c_idx, m, v, name, deps=()):
    r, c = w.shape
    rh = r // 2
    tr = _pick(rh, [t for t in ROW_TILES if t * c * 4 <= BLOCK_BYTES])
    nb = rh // tr

    def body(c_ref, w_ref, a_ref, b_ref, m_ref, v_ref, *rest):
        g_ref, d_ref, mo_ref, vo_ref = rest[-4:]
        gv = jnp.where(pl.program_id(0) // nb == c_ref[0], a_ref[...], b_ref[...])
        g_ref[...] = gv
        d_ref[...], mo_ref[...], vo_ref[...] = _adamw_update(w_ref[...], gv, m_ref[...], v_ref[...])

    blk = pl.BlockSpec((tr, c), lambda i, cr: (i, 0))
    mine_spec = pl.BlockSpec((tr, c), lambda i, cr: (jnp.clip(i - cr[0] * nb, 0, nb - 1), 0))
    theirs_spec = pl.BlockSpec((tr, c), lambda i, cr: (jnp.clip(i - (1 - cr[0]) * nb, 0, nb - 1), 0))
    return pl.pallas_call(
        body, name=name,
        grid_spec=pltpu.PrefetchScalarGridSpec(
            num_scalar_prefetch=1, grid=(r // tr,),
            in_specs=[blk, mine_spec, theirs_spec, blk, blk] + [ANY] * len(deps), out_specs=[blk] * 4),
        out_shape=[jax.ShapeDtypeStruct((r, c), F32)] * 4, compiler_params=_params("arbitrary"),
    )(c_idx, w, mine, theirs, m, v, *deps)


ANY = pl.BlockSpec(memory_space=pl.ANY)


def _place():
    x, y, c = lax.axis_index("x"), lax.axis_index("y"), lax.axis_index("c")
    chips = [(1 - x, y), (x, 1 - y), (1 - x, 1 - y)]
    return x, y, c, chips


def _remote(src, dst, send_sem, recv_sem, to):
    return pltpu.make_async_remote_copy(src_ref=src, dst_ref=dst, send_sem=send_sem, recv_sem=recv_sem,
                                        device_id=to, device_id_type=MESH)


HBM = pl.BlockSpec(memory_space=pltpu.HBM)
SEM = pl.BlockSpec(memory_space=pltpu.SEMAPHORE)
EFFECT = pltpu.SideEffectType.DATAFLOW_SIDE_EFFECTING


def _in_hbm(a):
    return pltpu.with_memory_space_constraint(a, pltpu.HBM)


def _half(ref_rows, who):
    return pl.ds(who * (ref_rows // 2), ref_rows // 2)


def _gather_start(groups):
    items = [it for g in groups for it in g]
    n = len(items)
    sizes = [len(g) for g in groups]

    def body(*refs):
        srcs, lands = refs[:n], refs[n:2 * n]
        sems = refs[2 * n:2 * n + 2 * len(groups)]
        token = refs[-1]
        x, y, c, chips = _place()
        j = 2 * x + y
        at = 0
        for gi, g in enumerate(groups):
            send, recv = sems[2 * gi], sems[2 * gi + 1]
            for i, (shard, split) in enumerate(g):
                src, land = srcs[at], lands[at]
                at += 1
                rows = _half(shard.shape[0], c) if split else slice(None)
                for k, chip in enumerate(chips):
                    _remote(src.at[rows], land.at[j, rows], send.at[4 * i + k], recv.at[4 * i + k], (*chip, c)).start()
                _remote(src, land.at[j], send.at[4 * i + 3], recv.at[4 * i + 3], (x, y, 1 - c)).start()
        token[...] = jnp.zeros_like(token)

    sem_shapes = []
    for sz in sizes:
        sem_shapes += [pltpu.SemaphoreType.DMA((4 * sz,)), pltpu.SemaphoreType.DMA((4 * sz,))]
    out_shape = (sem_shapes + [pltpu.HBM(sh.shape, sh.dtype) for sh, _ in items]
                 + [pltpu.HBM((N_CHIPS,) + sh.shape, sh.dtype) for sh, _ in items]
                 + [jax.ShapeDtypeStruct((8, LANES), F32)])
    ns = len(sem_shapes)
    outs = pl.pallas_call(
        body, name="gather_start", in_specs=[HBM] * (2 * n),
        out_specs=[SEM] * ns + [HBM] * (2 * n) + [pl.BlockSpec(memory_space=pltpu.VMEM)],
        out_shape=out_shape, input_output_aliases={i: ns + i for i in range(2 * n)},
        compiler_params=pltpu.CompilerParams(has_side_effects=EFFECT),
    )(*[_in_hbm(sh) for sh, _ in items], *[_in_hbm(lax.empty((N_CHIPS,) + sh.shape, sh.dtype)) for sh, _ in items])
    sems, shards, lands, token = outs[:ns], outs[ns:ns + n], outs[ns + n:ns + 2 * n], outs[-1]
    res, at = [], 0
    for gi, sz in enumerate(sizes):
        res.append((shards[at:at + sz], lands[at:at + sz], sems[2 * gi], sems[2 * gi + 1]))
        at += sz
    return res, token


def _gather_pass(group, started, after, name):
    shards, lands, send, recv = started
    n = len(group)
    split_ix = [i for i, (_, split) in enumerate(group) if split]

    def body(*refs):
        lnds, send1, recv1 = refs[n:2 * n], refs[2 * n], refs[2 * n + 1]
        outs = refs[2 * n + 2 + len(after):]
        send2, recv2, token = outs[2 * n], outs[2 * n + 1], outs[2 * n + 2]
        x, y, c, chips = _place()
        sib = (x, y, 1 - c)
        for i, (shard, split) in enumerate(group):
            rows = _half(shard.shape[0], c) if split else slice(None)
            for k, (cx, cy) in enumerate(chips):
                landed = lnds[i].at[2 * cx + cy, rows]
                cp = _remote(landed, landed, send1.at[4 * i + k], recv1.at[4 * i + k], sib)
                cp.wait_send()
                cp.wait_recv()
            own = lnds[i].at[2 * x + y]
            cp = _remote(own, own, send1.at[4 * i + 3], recv1.at[4 * i + 3], sib)
            cp.wait_send()
            cp.wait_recv()
        for i2, i in enumerate(split_ix):
            rows = _half(group[i][0].shape[0], c)
            for k, (cx, cy) in enumerate(chips):
                landed = lnds[i].at[2 * cx + cy, rows]
                _remote(landed, landed, send2.at[3 * i2 + k], recv2.at[3 * i2 + k], sib).start()
        token[...] = jnp.zeros_like(token)

    n2 = len(split_ix)
    out_shape = ([pltpu.HBM(a.shape, a.dtype) for a in (*shards, *lands)]
                 + [pltpu.SemaphoreType.DMA((3 * n2,)), pltpu.SemaphoreType.DMA((3 * n2,)), jax.ShapeDtypeStruct((8, LANES), F32)])
    outs = pl.pallas_call(
        body, name=name, in_specs=[HBM] * (2 * n) + [SEM, SEM] + [ANY] * len(after),
        out_specs=[HBM] * (2 * n) + [SEM, SEM, pl.BlockSpec(memory_space=pltpu.VMEM)],
        out_shape=out_shape, input_output_aliases={i: i for i in range(2 * n)},
        compiler_params=pltpu.CompilerParams(has_side_effects=EFFECT),
    )(*shards, *lands, send, recv, *after)
    return outs[:n], (outs[n:2 * n], outs[2 * n], outs[2 * n + 1]), outs[2 * n + 2]


def _gather_wait(group, passed, after, name):
    lands, send2, recv2 = passed
    n = len(group)
    split_ix = [i for i, (_, split) in enumerate(group) if split]

    def body(*refs):
        lnds, s2, r2 = refs[:n], refs[n], refs[n + 1]
        x, y, c, chips = _place()
        sib = (x, y, 1 - c)
        for i2, i in enumerate(split_ix):
            rows = _half(group[i][0].shape[0], 1 - c)
            for k, (cx, cy) in enumerate(chips):
                landed = lnds[i].at[2 * cx + cy, rows]
                cp = _remote(landed, landed, s2.at[3 * i2 + k], r2.at[3 * i2 + k], sib)
                cp.wait_send()
                cp.wait_recv()

    return pl.pallas_call(
        body, name=name, in_specs=[HBM] * n + [SEM, SEM, ANY], out_specs=[HBM] * n,
        out_shape=[pltpu.HBM(a.shape, a.dtype) for a in lands], input_output_aliases={i: i for i in range(n)},
        compiler_params=pltpu.CompilerParams(has_side_effects=EFFECT),
    )(*lands, send2, recv2, after)


def _xfer_start(name, srcs, land_shapes, n_copies, copies, after):
    n = len(srcs)

    def body(*refs):
        src_refs, land_refs = refs[:n], refs[n:2 * n]
        send, recv, token = refs[2 * n + 1], refs[2 * n + 2], refs[-1]
        for cp in copies(src_refs, land_refs, send, recv):
            cp.start()
        token[...] = jnp.zeros_like(token)

    lands = [_in_hbm(lax.empty(shape, dtype)) for shape, dtype in land_shapes]
    out_shape = ([pltpu.SemaphoreType.DMA((n_copies,)), pltpu.SemaphoreType.DMA((n_copies,))]
                 + [pltpu.HBM(a.shape, a.dtype) for a in (*srcs, *lands)] + [jax.ShapeDtypeStruct((8, LANES), F32)])
    outs = pl.pallas_call(
        body, name=name, in_specs=[HBM] * (2 * n) + [ANY],
        out_specs=[SEM, SEM] + [HBM] * (2 * n) + [pl.BlockSpec(memory_space=pltpu.VMEM)],
        out_shape=out_shape, input_output_aliases={i: 2 + i for i in range(2 * n)},
        compiler_params=pltpu.CompilerParams(has_side_effects=EFFECT),
    )(*[_in_hbm(a) for a in srcs], *lands, after)
    return (outs[2:2 + n], outs[2 + n:2 + 2 * n], outs[0], outs[1]), outs[-1]


def _xfer_wait(name, started, copies, after):
    srcs, lands, send, recv = started
    n = len(srcs)

    def body(*refs):
        src_refs, land_refs, s_ref, r_ref = refs[:n], refs[n:2 * n], refs[2 * n], refs[2 * n + 1]
        for cp in copies(src_refs, land_refs, s_ref, r_ref):
            cp.wait_send()
            cp.wait_recv()

    outs = pl.pallas_call(
        body, name=name, in_specs=[HBM] * (2 * n) + [SEM, SEM, ANY], out_specs=[HBM] * (2 * n),
        out_shape=[pltpu.HBM(a.shape, a.dtype) for a in (*srcs, *lands)],
        input_output_aliases={i: i for i in range(2 * n)},
        compiler_params=pltpu.CompilerParams(has_side_effects=EFFECT),
    )(*srcs, *lands, send, recv, after)
    return outs[:n], outs[n:]


def _swap_copies(srcs, lands, send, recv):
    x, y, c, _ = _place()
    return [_remote(src.at[:, _half(src.shape[1], 1 - c)], land, send.at[i], recv.at[i], (x, y, 1 - c))
            for i, (src, land) in enumerate(zip(srcs, lands))]


def _scatter_copies(srcs, lands, send, recv):
    x, y, c, chips = _place()
    return [_remote(src.at[2 * cx + cy], land.at[k], send.at[3 * i + k], recv.at[3 * i + k], (cx, cy, c))
            for i, (src, land) in enumerate(zip(srcs, lands)) for k, (cx, cy) in enumerate(chips)]


def _join_copies(srcs, lands, send, recv):
    x, y, c, _ = _place()
    return [_remote(src, land, send.at[i], recv.at[i], (x, y, 1 - c)) for i, (src, land) in enumerate(zip(srcs, lands))]


def _corner(a):
    return a[(slice(0, 1),) * a.ndim]


class _Reducer:
    def __init__(self, place):
        self.place = place
        self.state = {}

    def swap(self, key, grads, after):
        shapes = [((g.shape[0], g.shape[1] // 2, g.shape[2]), g.dtype) for g in grads]
        self.state[key], token = _xfer_start("swap_start_" + key, grads, shapes, len(grads), _swap_copies, _corner(after))
        return token

    def to_chips(self, key, after):
        grads, from_sibling = _xfer_wait("swap_wait_" + key, self.state[key], _swap_copies, after)
        sums = [_add_halves(g, r, self.place) for g, r in zip(grads, from_sibling)]
        shapes = [((3,) + s.shape[1:], s.dtype) for s in sums]
        started, token = _xfer_start("scatter_start_" + key, sums, shapes, 3 * len(sums), _scatter_copies, _corner(sums[-1]))
        self.state[key] = (grads, from_sibling, started)
        return token

    def to_core(self, key, after):
        grads, from_sibling, started = self.state[key]
        _, from_chips = _xfer_wait("scatter_wait_" + key, started, _scatter_copies, after)
        halves = [_sum_chips(g, r, rc, self.place) for g, r, rc in zip(grads, from_sibling, from_chips)]
        shapes = [(h.shape, h.dtype) for h in halves]
        self.state[key], token = _xfer_start("join_start_" + key, halves, shapes, len(halves), _join_copies, _corner(halves[-1]))
        return token

    def finish(self, key, after):
        return _xfer_wait("join_wait_" + key, self.state.pop(key), _join_copies, after)


def _gather_packs(pack):
    def body(p_ref, o_ref, lsem, ssem, rsem):
        x, y, c, _ = _place()
        me = 4 * x + 2 * y + c
        local = pltpu.make_async_copy(p_ref, o_ref.at[me], lsem)
        local.start()
        cps = []
        for k in range(1, N_DEV):
            fx, fy, fc = (k >> 2) & 1, (k >> 1) & 1, k & 1
            to = (x ^ fx, y ^ fy, c ^ fc)
            cps.append(_remote(p_ref, o_ref.at[me], ssem.at[k - 1], rsem.at[k - 1], to))
        for cp in cps:
            cp.start()
        for k in range(1, N_DEV):
            fx, fy, fc = (k >> 2) & 1, (k >> 1) & 1, k & 1
            src = o_ref.at[4 * (x ^ fx) + 2 * (y ^ fy) + (c ^ fc)]
            _remote(src, src, ssem.at[k - 1], rsem.at[k - 1], (x, y, c)).wait_recv()
        for cp in cps:
            cp.wait_send()
        local.wait()

    return pl.pallas_call(
        body, name="gather_packs", in_specs=[ANY], out_specs=ANY,
        out_shape=jax.ShapeDtypeStruct((N_DEV,) + pack.shape, pack.dtype),
        scratch_shapes=[pltpu.SemaphoreType.DMA, pltpu.SemaphoreType.DMA((N_DEV - 1,)), pltpu.SemaphoreType.DMA((N_DEV - 1,))],
    )(pack)


LANE_TILES = (512, 896, 1408, 704, 384, 256, 128)


def _layer_grads(x, target, small, wg, rest_pass, rest_wait, red):
    s, d = x.shape
    f = wg["conv"].shape[1] // 2
    w_att = N_HEADS * HEAD_DIM
    in_splits = (w_att, w_att, w_att, N_HEADS, w_att, w_att, w_att, d, d)
    in_cols = sum(in_splits)
    cs = in_cols // N_CHIPS
    cp = wg["in"].shape[2]
    tm = min(s, 1024)
    t_in = cp
    t_d = _pick(d, LANE_TILES)
    t_d2 = min(d, 1024)
    t_dq = _pick(d // N_CHIPS, LANE_TILES)
    t_w = _pick(w_att, LANE_TILES)
    t_up = 2 * f // N_CHIPS
    tm_wide = min(s, 512)
    t_fq = _pick(f // N_CHIPS, LANE_TILES)
    offs = np.cumsum(in_splits)[:-1].tolist()

    h1 = _norm_fwd(x, small["g_attn"], group=d, name="rms1_fwd")
    proj_p = _mm(h1, wg["in"], mode="nn", b_kind="col", tm=tm_wide, tn=t_in, tk=d, name="mm_in")
    gains = {n: small[n].reshape(1, w_att) for n in ("g_q_fox", "g_k_fox", "g_q_dil", "g_k_dil")}
    qa, ka, va_b, fa, qb, kb, vb_b, ga, gb, qa_n, ka_n, qb_n, kb_n = _proj_split(
        proj_p, in_splits, cs, (F32, F32, BF16, F32, F32, F32, BF16, F32, F32),
        {0: gains["g_q_fox"], 1: gains["g_k_fox"], 4: gains["g_q_dil"], 5: gains["g_k_dil"]})
    fa_t = fa.T
    b_f = small["b_forget"].reshape(N_HEADS, 1)
    c_f = _forget_fwd(fa_t, b_f)
    slopes = jnp.asarray(2.0 ** (-8.0 * np.arange(1, N_HEADS + 1) / N_HEADS), dtype=F32)
    a_d = -(slopes[:, None] * jnp.arange(s, dtype=F32)[None, :])
    rows_f, cols_f = c_f[:, :, None], c_f[:, None, :]
    rows_d, cols_d = a_d[:, :, None], a_d[:, None, :]
    o_a, o_a32, lse_a = _attn_fwd(qa_n, ka_n, va_b, rows_f, cols_f, dilated=False, name="attn_fox_fwd")
    token = rest_pass("mid", o_a)
    rows_d = rows_d + token[0, 0]
    o_b, o_b32, lse_b = _attn_fwd(qb_n, kb_n, vb_b, rows_d, cols_d, dilated=True, name="attn_dil_fwd")
    wg = dict(wg, **rest_wait("mid", o_b))
    token = rest_pass("late", o_b)
    pa = _mm(o_a, wg["brf"], mode="nn", b_kind="col", tm=tm, tn=t_dq, tk=w_att, name="mm_brf", deps=(token,))
    pb = _mm(o_b, wg["brd"], mode="nn", b_kind="col", tm=tm, tn=t_dq, tk=w_att, name="mm_brd")
    merged = _gate_fwd(ga, gb, pa, pb)
    x1 = _mm(merged, wg["out"], mode="nn", b_kind="row", res=x, tm=tm, tn=t_d, tk=t_dq, name="mm_out")
    wg = dict(wg, **rest_wait("late", x1))
    h2 = _norm_fwd(x1, small["g_ffn"], group=d, name="rms2_fwd")
    u = _mm(h2, wg["up"], mode="nn", b_kind="col", tm=tm_wide, tn=t_up, tk=d, name="mm_up")
    act = _conv_glu_fwd(u, wg["conv"], wg["bconv"])
    y = _mm(act, wg["down"], mode="nn", b_kind="row", res=x1, tm=tm, tn=t_d2, tk=t_fq, name="mm_down")
    dy_f, dy_b, loss_blk = _loss_head(y, target)

    d_act = _mm(dy_b, wg["down"], mode="nt", b_kind="row", tm=tm, tn=t_fq, tk=d, name="mm_down_dx")
    g_down = _mm(act, dy_b, mode="tn", out_dtype=BF16, out_kind="row", tm=t_fq, tn=t_d2, tk=s, name="mm_down_dw")
    tok = red.swap("down", [g_down], g_down)
    du_g, du_v, st_g, st_v = _conv_glu_bwd(u, d_act, wg["conv"] + tok[0, 0], wg["bconv"])
    tok = red.to_chips("down", du_g)
    du = jnp.concatenate([du_g, du_v], axis=1)
    g_up = _mm(h2, du, mode="tn", out_dtype=BF16, out_kind="col", tm=t_d2, tn=t_up, tk=s, name="mm_up_dw", deps=(tok,))
    tok = red.to_core("down", g_up)
    tok2 = red.swap("up", [g_up], g_up)
    dh2 = _mm(du, wg["up"], mode="nt", b_kind="col", tm=tm, tn=t_d2, tk=t_up, name="mm_up_dx", deps=(tok, tok2))
    tok = red.to_chips("up", dh2)
    dx1_b, dx1_f, dg_ffn = _norm_bwd(dh2, x1, small["g_ffn"], group=d, res=dy_f, out_dtypes=(BF16, F32), name="rms2_bwd")
    d_merged = _mm(dx1_b, wg["out"], mode="nt", b_kind="row", tm=tm, tn=t_dq, tk=d, name="mm_out_dx", deps=(tok,))
    g_out = _mm(merged, dx1_b, mode="tn", out_dtype=BF16, out_kind="row", tm=t_dq, tn=t_d2, tk=s, name="mm_out_dw")
    dpa, dpb, dga, dgb = _gate_bwd(d_merged, ga, gb, pa, pb)
    do_a = _mm(dpa, wg["brf"], mode="nt", b_kind="col", out_dtype=BF16, tm=s, tn=w_att, tk=t_dq, name="mm_brf_dx")
    do_b = _mm(dpb, wg["brd"], mode="nt", b_kind="col", out_dtype=BF16, tm=s, tn=w_att, tk=t_dq, name="mm_brd_dx")
    g_brf = _mm(o_a, dpa, mode="tn", out_dtype=BF16, out_kind="col", tm=w_att, tn=t_dq, tk=s, name="mm_brf_dw")
    g_brd = _mm(o_b, dpb, mode="tn", out_dtype=BF16, out_kind="col", tm=w_att, tn=t_dq, tk=s, name="mm_brd_dw")
    tok = red.swap("mix", [g_out, g_brf, g_brd], g_brd)
    dqa_n, dka_n, dva, dac_a = _attn_bwd(qa_n, ka_n, va_b, o_a32, do_a, lse_a, rows_f + tok[0, 0], cols_f, dilated=False, name="attn_fox_bwd")
    tok = red.to_core("up", dqa_n)
    tok2 = red.to_chips("mix", dqa_n)
    dqb_n, dkb_n, dvb, _ = _attn_bwd(qb_n, kb_n, vb_b, o_b32, do_b, lse_b, rows_d + (tok[0, 0] + tok2[0, 0]), cols_d, dilated=True, name="attn_dil_bwd")
    tok = red.to_core("mix", dqb_n)
    dfa_t, db_f = _forget_bwd(dac_a[:, 0, :], fa_t, b_f)
    dproj_p, dgains = _dproj_merge(
        [dqa_n, dka_n, dva, dfa_t.T, dqb_n, dkb_n, dvb, dga, dgb], in_splits, cs, cp,
        {0: (qa, gains["g_q_fox"]), 1: (ka, gains["g_k_fox"]), 4: (qb, gains["g_q_dil"]), 5: (kb, gains["g_k_dil"])})
    dg_qf, dg_kf, dg_qd, dg_kd = dgains[0], dgains[1], dgains[4], dgains[5]
    g_in = _mm(h1, dproj_p, mode="tn", out_dtype=BF16, out_kind="col", tm=t_d2, tn=t_in, tk=s, name="mm_in_dw", deps=(tok,))
    tok = red.swap("in", [g_in], g_in)
    dh1 = _mm(dproj_p, wg["in"], mode="nt", b_kind="col", tm=tm, tn=t_d2, tk=t_in, name="mm_in_dx", deps=(tok,))
    grad_x, dg_attn = _norm_bwd(dh1, x, small["g_attn"], group=d, res=dx1_f, out_dtypes=(F32,), name="rms1_bwd")

    small_grads = {
        "g_attn": dg_attn, "b_forget": db_f.reshape(1, N_HEADS),
        "g_q_fox": dg_qf, "g_k_fox": dg_kf, "g_q_dil": dg_qd, "g_k_dil": dg_kd, "g_ffn": dg_ffn,
        "w_conv": jnp.concatenate([st_g[0:3], st_v[0:3]], axis=1),
        "b_conv": jnp.concatenate([st_g[3:4], st_v[3:4]], axis=1),
        "loss": loss_blk[0:1, 0:1],
    }
    return small_grads, grad_x


SMALL_ORDER = ("g_attn", "b_forget", "g_q_fox", "g_k_fox", "g_q_dil", "g_k_dil", "g_ffn", "w_conv", "b_conv", "loss")
WEIGHT_ORDER = ("g_attn", "w_in", "b_forget", "g_q_fox", "g_k_fox", "g_q_dil", "g_k_dil", "w_br_fox", "w_br_dil",
                "w_out", "g_ffn", "w_up", "w_conv", "b_conv", "w_down")
BIG = {"w_in": "in", "w_br_fox": "brf", "w_br_dil": "brd", "w_out": "out", "w_up": "up", "w_down": "down"}


def kernel(x, g_attn, w_in, b_forget, g_q_fox, g_k_fox, g_q_dil, g_k_dil, w_br_fox, w_br_dil, w_out, g_ffn, w_up, w_conv, b_conv, w_down, loss_target, m_g_attn, m_w_in, m_b_forget, m_g_q_fox, m_g_k_fox, m_g_q_dil, m_g_k_dil, m_w_br_fox, m_w_br_dil, m_w_out, m_g_ffn, m_w_up, m_w_conv, m_b_conv, m_w_down, v_g_attn, v_w_in, v_b_forget, v_g_q_fox, v_g_k_fox, v_g_q_dil, v_g_k_dil, v_w_br_fox, v_w_br_dil, v_w_out, v_g_ffn, v_w_up, v_w_conv, v_b_conv, v_w_down):
    w = dict(g_attn=g_attn, w_in=w_in, b_forget=b_forget, g_q_fox=g_q_fox, g_k_fox=g_k_fox, g_q_dil=g_q_dil,
             g_k_dil=g_k_dil, w_br_fox=w_br_fox, w_br_dil=w_br_dil, w_out=w_out, g_ffn=g_ffn, w_up=w_up,
             w_conv=w_conv, b_conv=b_conv, w_down=w_down)
    m = dict(g_attn=m_g_attn, w_in=m_w_in, b_forget=m_b_forget, g_q_fox=m_g_q_fox, g_k_fox=m_g_k_fox,
             g_q_dil=m_g_q_dil, g_k_dil=m_g_k_dil, w_br_fox=m_w_br_fox, w_br_dil=m_w_br_dil, w_out=m_w_out,
             g_ffn=m_g_ffn, w_up=m_w_up, w_conv=m_w_conv, b_conv=m_b_conv, w_down=m_w_down)
    v = dict(g_attn=v_g_attn, w_in=v_w_in, b_forget=v_b_forget, g_q_fox=v_g_q_fox, g_k_fox=v_g_k_fox,
             g_q_dil=v_g_q_dil, g_k_dil=v_g_k_dil, w_br_fox=v_w_br_fox, w_br_dil=v_w_br_dil, w_out=v_w_out,
             g_ffn=v_g_ffn, w_up=v_w_up, w_conv=v_w_conv, b_conv=v_b_conv, w_down=v_w_down)
    xi, yi, ci = lax.axis_index("x"), lax.axis_index("y"), lax.axis_index("c")
    chip = (2 * xi + yi).astype(jnp.int32)
    c_idx = ci.astype(jnp.int32).reshape(1)
    j_idx = chip.reshape(1)

    cs = w_in.shape[2]
    cp = _round_up(cs, LANES)
    shards = {
        "in": jnp.pad(w_in[0].astype(BF16), ((0, 0), (0, cp - cs))),
        "brf": w_br_fox[0].astype(BF16), "brd": w_br_dil[0].astype(BF16), "out": w_out[0].astype(BF16),
        "up": w_up[0].astype(BF16), "down": w_down[0].astype(BF16),
    }
    names = tuple(shards)
    conv_pad = jnp.pad(w_conv[0], ((0, 8 - w_conv.shape[1]), (0, 0)))
    first = [(shards["in"], True), (conv_pad, False)]
    later = {"mid": ("brf", "brd", "out"), "late": ("up", "down")}
    groups = {key: [(shards[n], True) for n in members] for key, members in later.items()}
    (started_first, *started_later), token = _gather_start([first, *groups.values()])
    started = dict(zip(later, started_later))
    token, w["w_in"], m["w_in"], v["w_in"] = lax.optimization_barrier((token, w["w_in"], m["w_in"], v["w_in"]))
    w2, m2, v2 = ({n: a[n].reshape(a[n].shape[-2], a[n].shape[-1]) for n in BIG} for a in (w, m, v))
    early = (token, w2["w_in"], m2["w_in"], v2["w_in"])
    own_first, passed_first, token = _gather_pass(first, started_first, early, "gather_pass_in")
    land_in, land_conv = _gather_wait(first, passed_first, token, "gather_wait_in")
    wg = {"in": land_in, "bconv": b_conv,
          "conv": jnp.transpose(land_conv[:, :w_conv.shape[1], :], (1, 0, 2)).reshape(w_conv.shape[1], -1)}
    small = {n: w[n] for n in ("g_attn", "b_forget", "g_q_fox", "g_k_fox", "g_q_dil", "g_k_dil", "g_ffn")}
    small = {n: (a[0] if a.ndim == 3 else a) for n, a in small.items()}
    in_flight = {}

    def rest_pass(key, after):
        own, passed, tok = _gather_pass(groups[key], started[key], (after,), "gather_pass_" + key)
        in_flight[key] = (own, passed)
        return tok

    def rest_wait(key, after):
        own, passed = in_flight.pop(key)
        lands = _gather_wait(groups[key], passed, after, "gather_wait_" + key)
        return dict(zip(later[key], lands))

    reducer = _Reducer(jnp.stack([chip, ci.astype(jnp.int32)]))
    small_grads, grad_x = _layer_grads(x[0], loss_target[0], small, wg, rest_pass, rest_wait, reducer)

    mine, theirs = {}, {}
    for key, members in (("down", ("down",)), ("up", ("up",)), ("mix", ("out", "brf", "brd"))):
        mine_k, theirs_k = reducer.finish(key, grad_x)
        mine.update(zip(members, mine_k))
        theirs.update(zip(members, theirs_k))

    flat = jnp.concatenate([small_grads[n].reshape(-1) for n in SMALL_ORDER])
    rows = _round_up(flat.shape[0], 8 * LANES) // LANES
    pack = jnp.pad(flat, (0, rows * LANES - flat.shape[0])).reshape(rows, LANES)
    packs = _gather_packs(pack)
    tok_in = reducer.to_chips("in", packs)
    total = _sum_devices(packs).reshape(-1)
    red, at = {}, 0
    for n in SMALL_ORDER:
        size = small_grads[n].size
        red[n] = total[at:at + size].reshape(small_grads[n].shape)
        at += size
    loss = red["loss"].reshape(())
    c2 = w_conv.shape[2]
    red["w_conv"] = lax.dynamic_slice_in_dim(red["w_conv"], chip * c2, c2, axis=1)

    g_out, d_out, m_out, v_out = {}, {}, {}, {}
    last = [n for n in WEIGHT_ORDER if n != "w_in"] + ["w_in"]
    for n in last:
        shape = w[n].shape
        r2 = (shape[-2], shape[-1]) if n not in ("g_attn", "b_forget", "g_ffn", "b_conv") else (1, shape[-1])
        if n == "w_in":
            done = jnp.stack([v_out[k][(0,) * v_out[k].ndim] for k in last[:-1]])
            tok = reducer.to_core("in", done)
            (mine_in,), (theirs_in,) = reducer.finish("in", tok)
            mine["in"], theirs["in"] = mine_in[:, :cs], theirs_in[:, :cs]
        if n in BIG:
            g2, dl, mn, vn = _adamw_halves(w2[n], mine[BIG[n]], theirs[BIG[n]], c_idx, m2[n], v2[n],
                                           name="adamw_" + n, deps=(tok_in,))
        else:
            g2 = red[n].reshape(r2)
            dl, mn, vn = _adamw(w[n].reshape(r2), g2, m[n].reshape(r2), v[n].reshape(r2), name="adamw_" + n,
                                deps=(tok_in,))
        g_out[n], d_out[n], m_out[n], v_out[n] = (a.reshape(shape) for a in (g2, dl, mn, vn))

    return (loss, grad_x[None], *[g_out[n] for n in WEIGHT_ORDER], *[d_out[n] for n in WEIGHT_ORDER],
            *[m_out[n] for n in WEIGHT_ORDER], *[v_out[n] for n in WEIGHT_ORDER])
```

```python
import functools
import math

import jax
import jax.numpy as jnp
import numpy as np
from jax import lax
from jax.experimental import pallas as pl
from jax.experimental.pallas import tpu as pltpu

F32 = jnp.float32
BF16 = jnp.bfloat16
HEAD_DIM = 128
N_HEADS = 8
EPS = 1e-6
NEG = -1e30
N_CHIPS = 4
N_DEV = 8
LANES = 128
VMEM_LIMIT_BYTES = 56 * 1024 * 1024
DIL_PATTERNS = ((128, 1), (512, 4), (2048, 16))
ATTN_TILE = 512
ADAM_LR, ADAM_B1, ADAM_B2, ADAM_EPS, ADAM_WD, ADAM_STEP = 0.001, 0.9, 0.999, 1e-08, 0.01, 10
MESH = pl.DeviceIdType.MESH


def _params(*sem):
    return pltpu.CompilerParams(dimension_semantics=sem, vmem_limit_bytes=VMEM_LIMIT_BYTES)


def _round_up(n, m):
    return -(-n // m) * m


def _pick(dim, prefs):
    for p in prefs:
        if dim % p == 0:
            return p
    raise ValueError(f"no tile for {dim} in {prefs}")


def _logical_shape(arr, kind):
    if kind is None:
        return arr.shape
    s, r, c = arr.shape
    return (r, s * c) if kind == "col" else (s * r, c)


def _spec(shape, kind, br, bc, fi, fj):
    if kind is None:
        return pl.BlockSpec((br, bc), lambda *g: (fi(*g), fj(*g)))
    _, r, c = shape
    if kind == "col":
        nb = c // bc
        assert nb * bc == c, (shape, bc)
        return pl.BlockSpec((None, br, bc), lambda *g: (fj(*g) // nb, fi(*g), fj(*g) % nb))
    nb = r // br
    assert nb * br == r, (shape, br)
    return pl.BlockSpec((None, br, bc), lambda *g: (fi(*g) // nb, fi(*g) % nb, fj(*g)))


def _mm(a, b, *, mode, tm, tn, tk, name, a_kind=None, b_kind=None, out_kind=None,
        out_dtype=F32, res=None, deps=()):
    la, lb = _logical_shape(a, a_kind), _logical_shape(b, b_kind)
    if mode == "nn":
        (m, k), (k2, n) = la, lb
    elif mode == "nt":
        (m, k), (n, k2) = la, lb
    else:
        (k, m), (k2, n) = la, lb
    assert k == k2, (name, la, lb)
    assert m % tm == 0 and n % tn == 0 and k % tk == 0, (name, m, n, k, tm, tn, tk)
    nk = k // tk
    im = lambda i, j, l: i
    jn = lambda i, j, l: j
    lk = lambda i, j, l: l
    if mode == "tn":
        a_spec = _spec(a.shape, a_kind, tk, tm, lk, im)
        dims = (((0,), (0,)), ((), ()))
    else:
        a_spec = _spec(a.shape, a_kind, tm, tk, im, lk)
        dims = (((1,), (1,)), ((), ())) if mode == "nt" else (((1,), (0,)), ((), ()))
    if mode == "nt":
        b_spec = _spec(b.shape, b_kind, tn, tk, jn, lk)
    else:
        b_spec = _spec(b.shape, b_kind, tk, tn, lk, jn)
    if out_kind is None:
        oshape = (m, n)
    elif out_kind == "col":
        oshape = (N_CHIPS, m, n // N_CHIPS)
    else:
        oshape = (N_CHIPS, m // N_CHIPS, n)
    o_spec = _spec(oshape, out_kind, tm, tn, im, jn)
    in_specs = [a_spec, b_spec]
    args = [a, b]
    if res is not None:
        in_specs.append(pl.BlockSpec((tm, tn), lambda i, j, l: (i, j)))
        args.append(res)
    in_specs += [pl.BlockSpec(memory_space=pl.ANY)] * len(deps)
    args += list(deps)

    def finish(out, res_ref, o_ref):
        if res_ref is not None:
            out = out + res_ref[...]
        o_ref[...] = out.astype(o_ref.dtype)

    def body_whole_k(*refs):
        res_ref = refs[2] if res is not None else None
        finish(lax.dot_general(refs[0][...], refs[1][...], dims, preferred_element_type=F32), res_ref, refs[-1])

    def body(*refs):
        a_ref, b_ref = refs[0], refs[1]
        res_ref = refs[2] if res is not None else None
        o_ref, acc_ref = refs[-2], refs[-1]
        step = pl.program_id(2)

        @pl.when(step == 0)
        def _():
            acc_ref[...] = jnp.zeros_like(acc_ref)

        acc_ref[...] += lax.dot_general(a_ref[...], b_ref[...], dims, preferred_element_type=F32)

        @pl.when(step == nk - 1)
        def _():
            finish(acc_ref[...], res_ref, o_ref)

    return pl.pallas_call(
        body_whole_k if nk == 1 else body, name=name, grid=(m // tm, n // tn, nk),
        in_specs=in_specs, out_specs=o_spec,
        out_shape=jax.ShapeDtypeStruct(oshape, out_dtype),
        scratch_shapes=[] if nk == 1 else [pltpu.VMEM((tm, tn), F32)],
        compiler_params=_params("parallel", "parallel", "arbitrary"),
    )(*args)


def _pieces(splits, cs, cp):
    out, g0 = [], 0
    for width in splits:
        g1, runs = g0 + width, []
        for j in range(N_CHIPS):
            a, b = max(g0, cs * j), min(g1, cs * (j + 1))
            if a < b:
                runs.append((j * cp + a - cs * j, a - g0, b - a))
        out.append(runs)
        g0 = g1
    return out


def _head_norm(xv, gv):
    r = lax.rsqrt(jnp.mean(xv * xv, axis=-1, keepdims=True) + EPS)
    return (xv * r) * gv


def _head_norm_bwd(dyv, xv, gv):
    r = lax.rsqrt(jnp.mean(xv * xv, axis=-1, keepdims=True) + EPS)
    xr = xv * r
    gdy = dyv * gv
    return r * (gdy - xr * jnp.mean(gdy * xr, axis=-1, keepdims=True)), jnp.sum(dyv * xr, axis=0, keepdims=True)


def _proj_split(proj_p, splits, cs, dtypes, gains, tm=128):
    s, wp = proj_p.shape
    pieces = _pieces(splits, cs, wp // N_CHIPS)
    normed = sorted(gains)
    nseg = len(splits)

    def body(p_ref, *refs):
        g_refs, o_refs, n_refs = refs[:len(normed)], refs[len(normed):len(normed) + nseg], refs[len(normed) + nseg:]
        for o_ref, runs in zip(o_refs, pieces):
            for src, dst, n in runs:
                o_ref[:, dst:dst + n] = p_ref[:, src:src + n].astype(o_ref.dtype)
        for g_ref, n_ref, i in zip(g_refs, n_refs, normed):
            for c0 in range(0, splits[i], HEAD_DIM):
                cols = slice(c0, c0 + HEAD_DIM)
                n_ref[:, cols] = _head_norm(o_refs[i][:, cols], g_ref[:, cols]).astype(n_ref.dtype)

    return pl.pallas_call(
        body, name="proj_split", grid=(s // tm,),
        in_specs=[pl.BlockSpec((tm, wp), lambda i: (i, 0))] + [pl.BlockSpec((1, splits[i]), lambda i: (0, 0)) for i in normed],
        out_specs=[pl.BlockSpec((tm, w), lambda i: (i, 0)) for w in splits]
        + [pl.BlockSpec((tm, splits[i]), lambda i: (i, 0)) for i in normed],
        out_shape=[jax.ShapeDtypeStruct((s, w), dt) for w, dt in zip(splits, dtypes)]
        + [jax.ShapeDtypeStruct((s, splits[i]), BF16) for i in normed],
        compiler_params=_params("parallel"),
    )(proj_p, *[gains[i] for i in normed])


def _dproj_merge(parts, splits, cs, cp, norms, tm=128):
    s = parts[0].shape[0]
    wp = N_CHIPS * cp
    pieces = _pieces(splits, cs, cp)
    normed = sorted(norms)
    nseg, nn = len(splits), len(normed)

    def body(*refs):
        p_refs, x_refs, g_refs = refs[:nseg], refs[nseg:nseg + nn], refs[nseg + nn:nseg + 2 * nn]
        o_ref, dg_refs = refs[nseg + 2 * nn], refs[nseg + 2 * nn + 1:nseg + 3 * nn + 1]
        stage, tmp = refs[-2], refs[-1]

        @pl.when(pl.program_id(0) == 0)
        def _():
            for dg_ref in dg_refs:
                dg_ref[...] = jnp.zeros_like(dg_ref)

        for j in range(N_CHIPS):
            stage[:, j * cp + cs:(j + 1) * cp] = jnp.zeros((tm, cp - cs), F32)
        for i, (p_ref, runs) in enumerate(zip(p_refs, pieces)):
            src_ref = p_ref
            if i in norms:
                k = normed.index(i)
                for c0 in range(0, splits[i], HEAD_DIM):
                    cols = slice(c0, c0 + HEAD_DIM)
                    dx, dg = _head_norm_bwd(p_ref[:, cols].astype(F32), x_refs[k][:, cols], g_refs[k][:, cols])
                    tmp[:, cols] = dx
                    dg_refs[k][:, cols] += dg
                src_ref = tmp
            for dst, src, n in runs:
                stage[:, dst:dst + n] = src_ref[:, src:src + n].astype(F32)
        o_ref[...] = stage[...].astype(o_ref.dtype)

    wmax = max(splits[i] for i in normed)
    row = lambda w: pl.BlockSpec((tm, w), lambda i: (i, 0))
    vec = lambda w: pl.BlockSpec((1, w), lambda i: (0, 0))
    outs = pl.pallas_call(
        body, name="dproj_merge", grid=(s // tm,),
        in_specs=[row(w) for w in splits] + [row(splits[i]) for i in normed] + [vec(splits[i]) for i in normed],
        out_specs=[row(wp)] + [vec(splits[i]) for i in normed],
        out_shape=[jax.ShapeDtypeStruct((s, wp), BF16)] + [jax.ShapeDtypeStruct((1, splits[i]), F32) for i in normed],
        scratch_shapes=[pltpu.VMEM((tm, wp), F32), pltpu.VMEM((tm, wmax), F32)],
        compiler_params=_params("arbitrary"),
    )(*parts, *[norms[i][0] for i in normed], *[norms[i][1] for i in normed])
    return outs[0], dict(zip(normed, outs[1:]))


def _norm_fwd(x, g, *, group, name, tm=256):
    s, w = x.shape
    ng = w // group

    def body(x_ref, g_ref, o_ref):
        for i in range(ng):
            cols = slice(i * group, (i + 1) * group)
            xv = x_ref[:, cols]
            r = lax.rsqrt(jnp.mean(xv * xv, axis=-1, keepdims=True) + EPS)
            o_ref[:, cols] = ((xv * r) * g_ref[:, cols]).astype(o_ref.dtype)

    return pl.pallas_call(
        body, name=name, grid=(s // tm,),
        in_specs=[pl.BlockSpec((tm, w), lambda i: (i, 0)), pl.BlockSpec((1, w), lambda i: (0, 0))],
        out_specs=pl.BlockSpec((tm, w), lambda i: (i, 0)),
        out_shape=jax.ShapeDtypeStruct((s, w), BF16),
        compiler_params=_params("parallel"),
    )(x, g)


def _norm_bwd(dy, x, g, *, group, name, res=None, out_dtypes=(BF16,), tm=256, deps=()):
    s, w = x.shape
    ng = w // group
    n_in = 4 if res is not None else 3

    def body(*refs):
        dy_ref, x_ref, g_ref = refs[:3]
        res_ref = refs[3] if res is not None else None
        outs = refs[n_in + len(deps):]
        dx_refs, dg_ref = outs[:-1], outs[-1]

        @pl.when(pl.program_id(0) == 0)
        def _():
            dg_ref[...] = jnp.zeros_like(dg_ref)

        for i in range(ng):
            cols = slice(i * group, (i + 1) * group)
            xv = x_ref[:, cols]
            dyv = dy_ref[:, cols].astype(F32)
            r = lax.rsqrt(jnp.mean(xv * xv, axis=-1, keepdims=True) + EPS)
            xr = xv * r
            dg_ref[:, cols] += jnp.sum(dyv * xr, axis=0, keepdims=True)
            gdy = dyv * g_ref[:, cols]
            dx = r * (gdy - xr * jnp.mean(gdy * xr, axis=-1, keepdims=True))
            if res_ref is not None:
                dx = dx + res_ref[:, cols]
            for dx_ref in dx_refs:
                dx_ref[:, cols] = dx.astype(dx_ref.dtype)

    row = pl.BlockSpec((tm, w), lambda i: (i, 0))
    vec = pl.BlockSpec((1, w), lambda i: (0, 0))
    in_specs = [row, row, vec] + ([row] if res is not None else []) + [pl.BlockSpec(memory_space=pl.ANY)] * len(deps)
    args = [dy, x, g] + ([res] if res is not None else []) + list(deps)
    out_specs = [row] * len(out_dtypes) + [vec]
    out_shape = [jax.ShapeDtypeStruct((s, w), dt) for dt in out_dtypes] + [jax.ShapeDtypeStruct((1, w), F32)]
    return pl.pallas_call(
        body, name=name, grid=(s // tm,), in_specs=in_specs, out_specs=out_specs,
        out_shape=out_shape, compiler_params=_params("arbitrary"),
    )(*args)


def _split3(v):
    p1 = v.astype(BF16)
    r1 = v - p1.astype(F32)
    p2 = r1.astype(BF16)
    p3 = (r1 - p2.astype(F32)).astype(BF16)
    return p1, p2, p3


def _tri_sum(v, reverse, tcol=512):
    h, s = v.shape
    tcol = min(tcol, s)
    parts = _split3(v)
    outs = []
    for j in range(s // tcol):
        src = lax.broadcasted_iota(jnp.int32, (s, tcol), 0)
        dst = lax.broadcasted_iota(jnp.int32, (s, tcol), 1) + j * tcol
        keep = (src >= dst) if reverse else (src <= dst)
        tri = jnp.where(keep, 1.0, 0.0).astype(BF16)
        acc = jnp.zeros((h, tcol), F32)
        for p in parts:
            acc = acc + jnp.dot(p, tri, preferred_element_type=F32)
        outs.append(acc)
    return outs


def _forget_fwd(fa_t, b):
    h, s = fa_t.shape
    tcol = min(512, s)

    def body(f_ref, b_ref, c_ref):
        z = f_ref[...] + b_ref[...]
        logf = jnp.minimum(z, 0.0) - jnp.log(1.0 + jnp.exp(-jnp.abs(z)))
        for j, blk in enumerate(_tri_sum(logf, reverse=False, tcol=tcol)):
            c_ref[:, j * tcol:(j + 1) * tcol] = blk

    return pl.pallas_call(
        body, name="forget_fwd", out_shape=jax.ShapeDtypeStruct((h, s), F32),
        compiler_params=_params(),
    )(fa_t, b)


def _forget_bwd(dacol, fa_t, b):
    h, s = fa_t.shape
    tcol = min(512, s)

    def body(d_ref, f_ref, b_ref, dfa_ref, db_ref):
        z = f_ref[...] + b_ref[...]
        dc = -d_ref[...]
        total = jnp.zeros((h, 1), F32)
        for j, blk in enumerate(_tri_sum(dc, reverse=True, tcol=tcol)):
            cols = slice(j * tcol, (j + 1) * tcol)
            dfa = blk * (1.0 - jax.nn.sigmoid(z[:, cols]))
            dfa_ref[:, cols] = dfa
            total = total + jnp.sum(dfa, axis=-1, keepdims=True)
        db_ref[...] = total

    return pl.pallas_call(
        body, name="forget_bwd",
        out_shape=[jax.ShapeDtypeStruct((h, s), F32), jax.ShapeDtypeStruct((h, 1), F32)],
        compiler_params=_params(),
    )(dacol, fa_t, b)


def _distance_bias(s, tile, dilated):
    nb = s // tile
    b = lax.broadcasted_iota(jnp.int32, (nb, tile, tile), 0)
    dist = b * tile + lax.broadcasted_iota(jnp.int32, (nb, tile, tile), 1) - lax.broadcasted_iota(jnp.int32, (nb, tile, tile), 2)
    if not dilated:
        return jnp.where(dist >= 0, 0.0, NEG).astype(F32)
    mult = jnp.zeros(dist.shape, jnp.int32)
    for window, dil in DIL_PATTERNS:
        mult = mult + ((dist >= 0) & (dist <= window) & ((dist & (dil - 1)) == 0)).astype(jnp.int32)
    logm = jnp.where(mult == 3, math.log(3.0), jnp.where(mult == 2, math.log(2.0), 0.0))
    return jnp.where(mult > 0, logm, NEG).astype(F32)


def _logits(q, k, arow, acol, bias):
    s = lax.dot_general(q, k, (((1,), (1,)), ((), ())), preferred_element_type=F32)
    return s * (1.0 / math.sqrt(HEAD_DIM)) + arow - acol + bias


def _attn_fwd(q, k, v, arow, acol, *, dilated, name, tq=ATTN_TILE, tk=ATTN_TILE):
    two_term = not dilated
    s, w = q.shape
    nh = w // HEAD_DIM
    assert tq == tk
    tq = tk = min(tq, s)
    nq, nk = s // tq, s // tk

    def body(q_ref, k_ref, v_ref, ar_ref, ac_ref, b_ref, o_ref, of_ref, lse_ref, m_ref, l_ref, acc_ref):
        qi, ki = pl.program_id(1), pl.program_id(2)

        @pl.when(ki == 0)
        def _():
            m_ref[...] = jnp.full_like(m_ref, NEG)
            l_ref[...] = jnp.zeros_like(l_ref)
            acc_ref[...] = jnp.zeros_like(acc_ref)

        @pl.when(ki <= qi)
        def _():
            sc = _logits(q_ref[...], k_ref[...], ar_ref[...], ac_ref[...], b_ref[...])
            m_new = jnp.maximum(m_ref[...], jnp.max(sc, axis=-1, keepdims=True))
            alpha = jnp.exp(m_ref[...] - m_new)
            p = jnp.exp(sc - m_new)
            l_ref[...] = alpha * l_ref[...] + jnp.sum(p, axis=-1, keepdims=True)
            p_hi = p.astype(BF16)
            vv = v_ref[...]
            pv = jnp.dot(p_hi, vv, preferred_element_type=F32)
            if two_term:
                pv = pv + jnp.dot((p - p_hi.astype(F32)).astype(BF16), vv, preferred_element_type=F32)
            acc_ref[...] = alpha * acc_ref[...] + pv
            m_ref[...] = m_new

        @pl.when(ki == nk - 1)
        def _():
            out = acc_ref[...] / l_ref[...]
            o_ref[...] = out.astype(o_ref.dtype)
            of_ref[...] = out
            lse_ref[...] = m_ref[...] + jnp.log(l_ref[...])

    kv = pl.BlockSpec((tk, HEAD_DIM), lambda h, i, j: (jnp.minimum(j, i), h))
    return pl.pallas_call(
        body, name=name, grid=(nh, nq, nk),
        in_specs=[pl.BlockSpec((tq, HEAD_DIM), lambda h, i, j: (i, h)), kv, kv,
                  pl.BlockSpec((None, tq, 1), lambda h, i, j: (h, i, 0)),
                  pl.BlockSpec((None, 1, tk), lambda h, i, j: (h, 0, jnp.minimum(j, i))),
                  pl.BlockSpec((None, tq, tk), lambda h, i, j: (jnp.maximum(i - j, 0), 0, 0))],
        out_specs=[pl.BlockSpec((tq, HEAD_DIM), lambda h, i, j: (i, h)),
                   pl.BlockSpec((tq, HEAD_DIM), lambda h, i, j: (i, h)),
                   pl.BlockSpec((None, tq, 1), lambda h, i, j: (h, i, 0))],
        out_shape=[jax.ShapeDtypeStruct((s, w), BF16), jax.ShapeDtypeStruct((s, w), F32),
                   jax.ShapeDtypeStruct((nh, s, 1), F32)],
        scratch_shapes=[pltpu.VMEM((tq, 1), F32), pltpu.VMEM((tq, 1), F32), pltpu.VMEM((tq, HEAD_DIM), F32)],
        compiler_params=_params("parallel", "parallel", "arbitrary"),
    )(q, k, v, arow, acol, _distance_bias(s, tq, dilated))


def _attn_bwd(q, k, v, o, do, lse, arow, acol, *, dilated, name, tq=ATTN_TILE, tk=ATTN_TILE):
    s, w = q.shape
    nh = w // HEAD_DIM
    assert tq == tk
    tq = tk = min(tq, s)
    nq, nk = s // tq, s // tk
    scale = 1.0 / math.sqrt(HEAD_DIM)

    def body(q_ref, k_ref, v_ref, o_ref, do_ref, lse_ref, ar_ref, ac_ref, b_ref,
             dq_ref, dk_ref, dv_ref, dac_ref, dk_acc, dv_acc, dac_acc):
        ki, qi = pl.program_id(1), pl.program_id(2)

        @pl.when((ki == 0) & (qi == 0))
        def _():
            dq_ref[...] = jnp.zeros_like(dq_ref)

        @pl.when(qi == 0)
        def _():
            dk_acc[...] = jnp.zeros_like(dk_acc)
            dv_acc[...] = jnp.zeros_like(dv_acc)
            dac_acc[...] = jnp.zeros_like(dac_acc)

        @pl.when(qi >= ki)
        def _():
            qv, kvv, dov = q_ref[...], k_ref[...], do_ref[...]
            sc = _logits(qv, kvv, ar_ref[...], ac_ref[...], b_ref[...])
            p = jnp.exp(sc - lse_ref[...])
            dp = lax.dot_general(dov, v_ref[...], (((1,), (1,)), ((), ())), preferred_element_type=F32)
            delta = jnp.sum(dov.astype(F32) * o_ref[...].astype(F32), axis=-1, keepdims=True)
            ds = p * (dp - delta)
            dsb = ds.astype(BF16)
            dv_acc[...] += lax.dot_general(p.astype(BF16), dov, (((0,), (0,)), ((), ())), preferred_element_type=F32)
            dk_acc[...] += lax.dot_general(dsb, qv, (((0,), (0,)), ((), ())), preferred_element_type=F32)
            rows = pl.ds(pl.multiple_of(qi * tq, tq), tq)
            dq_ref[rows, :] += jnp.dot(dsb, kvv, preferred_element_type=F32) * scale
            dac_acc[...] += jnp.sum(ds, axis=0, keepdims=True)

        @pl.when(qi == nq - 1)
        def _():
            dk_ref[...] = dk_acc[...] * scale
            dv_ref[...] = dv_acc[...]
            dac_ref[...] = dac_acc[...]

    qs = pl.BlockSpec((tq, HEAD_DIM), lambda h, j, i: (jnp.maximum(i, j), h))
    ks = pl.BlockSpec((tk, HEAD_DIM), lambda h, j, i: (j, h))
    rowv = pl.BlockSpec((None, tq, 1), lambda h, j, i: (h, jnp.maximum(i, j), 0))
    colv = pl.BlockSpec((None, 1, tk), lambda h, j, i: (h, 0, j))
    return pl.pallas_call(
        body, name=name, grid=(nh, nk, nq),
        in_specs=[qs, ks, ks, qs, qs, rowv, rowv, colv,
                  pl.BlockSpec((None, tq, tk), lambda h, j, i: (jnp.maximum(i - j, 0), 0, 0))],
        out_specs=[pl.BlockSpec((s, HEAD_DIM), lambda h, j, i: (0, h)), ks, ks, colv],
        out_shape=[jax.ShapeDtypeStruct((s, w), F32), jax.ShapeDtypeStruct((s, w), F32),
                   jax.ShapeDtypeStruct((s, w), F32), jax.ShapeDtypeStruct((nh, 1, s), F32)],
        scratch_shapes=[pltpu.VMEM((tk, HEAD_DIM), F32), pltpu.VMEM((tk, HEAD_DIM), F32), pltpu.VMEM((1, tk), F32)],
        compiler_params=_params("arbitrary", "arbitrary", "arbitrary"),
    )(q, k, v, o, do, lse, arow, acol, _distance_bias(s, tq, dilated))


def _gate_fwd(ga, gb, pa, pb, tm=256):
    s, d = ga.shape

    def body(ga_ref, gb_ref, pa_ref, pb_ref, o_ref):
        o_ref[...] = (jax.nn.sigmoid(ga_ref[...]) * pa_ref[...]
                      + jax.nn.sigmoid(gb_ref[...]) * pb_ref[...]).astype(o_ref.dtype)

    row = pl.BlockSpec((tm, d), lambda i: (i, 0))
    return pl.pallas_call(
        body, name="gate_fwd", grid=(s // tm,), in_specs=[row] * 4, out_specs=row,
        out_shape=jax.ShapeDtypeStruct((s, d), BF16), compiler_params=_params("parallel"),
    )(ga, gb, pa, pb)


def _gate_bwd(dm, ga, gb, pa, pb, tm=256):
    s, d = ga.shape

    def body(dm_ref, ga_ref, gb_ref, pa_ref, pb_ref, dpa_ref, dpb_ref, dga_ref, dgb_ref):
        dmv = dm_ref[...]
        for g_ref, p_ref, dp_ref, dg_ref in ((ga_ref, pa_ref, dpa_ref, dga_ref), (gb_ref, pb_ref, dpb_ref, dgb_ref)):
            sg = jax.nn.sigmoid(g_ref[...])
            dp_ref[...] = (dmv * sg).astype(BF16)
            dg_ref[...] = (dmv * p_ref[...] * (sg * (1.0 - sg))).astype(BF16)

    row = pl.BlockSpec((tm, d), lambda i: (i, 0))
    return pl.pallas_call(
        body, name="gate_bwd", grid=(s // tm,), in_specs=[row] * 5, out_specs=[row] * 4,
        out_shape=[jax.ShapeDtypeStruct((s, d), BF16)] * 4, compiler_params=_params("parallel"),
    )(dm, ga, gb, pa, pb)


def _shift_down(u, k):
    row = lax.broadcasted_iota(jnp.int32, u.shape, 0)
    return jnp.where(row >= k, pltpu.roll(u, k, 0), 0.0)


def _shift_up(u, k):
    n = u.shape[0]
    row = lax.broadcasted_iota(jnp.int32, u.shape, 0)
    return jnp.where(row < n - k, pltpu.roll(u, n - k, 0), 0.0)


def _conv3(u, wc, b):
    return wc[0:1, :] * _shift_down(u, 2) + wc[1:2, :] * _shift_down(u, 1) + wc[2:3, :] * u + b


def _conv_glu_fwd(u, wc, b, tn=256):
    s, f2 = u.shape
    f = f2 // 2
    nb = f // tn

    def body(ug_ref, uv_ref, wg_ref, wv_ref, bg_ref, bv_ref, o_ref):
        cg = _conv3(ug_ref[...], wg_ref[...], bg_ref[...])
        cv = _conv3(uv_ref[...], wv_ref[...], bv_ref[...])
        o_ref[...] = (cg * jax.nn.sigmoid(cg) * cv).astype(o_ref.dtype)

    def cols(rows, off):
        return pl.BlockSpec((rows, tn), lambda j: (0, j + off))

    return pl.pallas_call(
        body, name="conv_glu_fwd", grid=(nb,),
        in_specs=[cols(s, 0), cols(s, nb), cols(3, 0), cols(3, nb), cols(1, 0), cols(1, nb)],
        out_specs=cols(s, 0), out_shape=jax.ShapeDtypeStruct((s, f), BF16),
        compiler_params=_params("parallel"),
    )(u, u, wc, wc, b, b)


def _conv_glu_bwd(u, da, wc, b, tn=256):
    s, f2 = u.shape
    f = f2 // 2
    nb = f // tn

    def body(ug_ref, uv_ref, da_ref, wg_ref, wv_ref, bg_ref, bv_ref, dug_ref, duv_ref, sg_ref, sv_ref):
        ug, uv, wg, wv = ug_ref[...], uv_ref[...], wg_ref[...], wv_ref[...]
        cg = _conv3(ug, wg, bg_ref[...])
        cv = _conv3(uv, wv, bv_ref[...])
        sig = jax.nn.sigmoid(cg)
        dav = da_ref[...]
        dcv = dav * (cg * sig)
        dcg = dav * cv * (sig * (1.0 + cg * (1.0 - sig)))
        for dc, uu, w, du_ref, st_ref in ((dcg, ug, wg, dug_ref, sg_ref), (dcv, uv, wv, duv_ref, sv_ref)):
            du = w[2:3, :] * dc + w[1:2, :] * _shift_up(dc, 1) + w[0:1, :] * _shift_up(dc, 2)
            du_ref[...] = du.astype(BF16)
            st_ref[...] = jnp.zeros_like(st_ref)
            st_ref[0:1, :] = jnp.sum(dc * _shift_down(uu, 2), axis=0, keepdims=True)
            st_ref[1:2, :] = jnp.sum(dc * _shift_down(uu, 1), axis=0, keepdims=True)
            st_ref[2:3, :] = jnp.sum(dc * uu, axis=0, keepdims=True)
            st_ref[3:4, :] = jnp.sum(dc, axis=0, keepdims=True)

    def cols(rows, off):
        return pl.BlockSpec((rows, tn), lambda j: (0, j + off))

    return pl.pallas_call(
        body, name="conv_glu_bwd", grid=(nb,),
        in_specs=[cols(s, 0), cols(s, nb), cols(s, 0), cols(3, 0), cols(3, nb), cols(1, 0), cols(1, nb)],
        out_specs=[cols(s, 0), cols(s, 0), cols(8, 0), cols(8, 0)],
        out_shape=[jax.ShapeDtypeStruct((s, f), BF16), jax.ShapeDtypeStruct((s, f), BF16),
                   jax.ShapeDtypeStruct((8, f), F32), jax.ShapeDtypeStruct((8, f), F32)],
        compiler_params=_params("parallel"),
    )(u, u, da, wc, wc, b, b)


def _loss_head(y, target, tm=256):
    s, d = y.shape

    def body(y_ref, t_ref, dyf_ref, dyb_ref, l_ref):
        @pl.when(pl.program_id(0) == 0)
        def _():
            l_ref[...] = jnp.zeros_like(l_ref)

        err = y_ref[...] - t_ref[...]
        dy = err * (1.0 / d)
        dyf_ref[...] = dy
        dyb_ref[...] = dy.astype(BF16)
        l_ref[...] += 0.5 * jnp.sum(jnp.sum(err * err, axis=-1, keepdims=True) * (1.0 / d), axis=0, keepdims=True)

    row = pl.BlockSpec((tm, d), lambda i: (i, 0))
    return pl.pallas_call(
        body, name="loss_head", grid=(s // tm,), in_specs=[row, row],
        out_specs=[row, row, pl.BlockSpec((8, LANES), lambda i: (0, 0))],
        out_shape=[jax.ShapeDtypeStruct((s, d), F32), jax.ShapeDtypeStruct((s, d), BF16),
                   jax.ShapeDtypeStruct((8, LANES), F32)],
        compiler_params=_params("arbitrary"),
    )(y, target)


ROW_TILES = (256, 128, 64, 32, 16, 8)
BLOCK_BYTES = 2 << 20


def _add_halves(g, r1, place):
    ns, r, c = g.shape
    rh = r // 2
    tr = _pick(rh, ROW_TILES)
    g4 = g.reshape(ns, 2, rh, c)

    def body(p_ref, g_ref, r_ref, o_ref):
        o_ref[...] = (g_ref[...].astype(F32) + r_ref[...].astype(F32)).astype(o_ref.dtype)

    def slab(s, pr):
        return s + (s >= pr[0]).astype(jnp.int32)

    return pl.pallas_call(
        body, name="add_halves",
        grid_spec=pltpu.PrefetchScalarGridSpec(
            num_scalar_prefetch=1, grid=(ns - 1, rh // tr),
            in_specs=[pl.BlockSpec((None, None, tr, c), lambda s, i, pr: (slab(s, pr), pr[1], i, 0)),
                      pl.BlockSpec((None, tr, c), lambda s, i, pr: (slab(s, pr), i, 0))],
            out_specs=pl.BlockSpec((None, tr, c), lambda s, i, pr: (slab(s, pr), i, 0))),
        out_shape=jax.ShapeDtypeStruct((ns, rh, c), BF16),
        compiler_params=_params("parallel", "parallel"),
    )(place, g4, r1)


def _sum_chips(g, r1, recv, place):
    ns, r, c = g.shape
    rh = r // 2
    tr = _pick(rh, ROW_TILES)
    g4 = g.reshape(ns, 2, rh, c)

    def body(p_ref, g_ref, r_ref, t0_ref, t1_ref, t2_ref, o_ref):
        own = g_ref[...].astype(F32) + r_ref[...].astype(F32)
        o_ref[...] = ((own + t0_ref[...].astype(F32)) + t1_ref[...].astype(F32)) + t2_ref[...].astype(F32)

    def peer(k):
        return pl.BlockSpec((None, tr, c), lambda i, pr: (k, i, 0))

    return pl.pallas_call(
        body, name="sum_chips",
        grid_spec=pltpu.PrefetchScalarGridSpec(
            num_scalar_prefetch=1, grid=(rh // tr,),
            in_specs=[pl.BlockSpec((None, None, tr, c), lambda i, pr: (pr[0], pr[1], i, 0)),
                      pl.BlockSpec((None, tr, c), lambda i, pr: (pr[0], i, 0)), peer(0), peer(1), peer(2)],
            out_specs=pl.BlockSpec((tr, c), lambda i, pr: (i, 0))),
        out_shape=jax.ShapeDtypeStruct((rh, c), F32),
        compiler_params=_params("parallel"),
    )(place, g4, r1, recv, recv, recv)


def _sum_devices(packs):
    n, r, c = packs.shape

    def body(p_ref, o_ref):
        acc = p_ref[0]
        for d in range(1, n):
            acc = acc + p_ref[d]
        o_ref[...] = acc

    return pl.pallas_call(
        body, name="sum_devices", out_shape=jax.ShapeDtypeStruct((r, c), F32), compiler_params=_params(),
    )(packs)


def _adamw_update(wv, gv, mv, vv):
    c1 = 1.0 - ADAM_B1 ** ADAM_STEP
    c2 = 1.0 - ADAM_B2 ** ADAM_STEP
    mn = ADAM_B1 * mv + (1.0 - ADAM_B1) * gv
    vn = ADAM_B2 * vv + (1.0 - ADAM_B2) * (gv * gv)
    m_hat = mn / c1
    v_hat = vn / c2
    return -ADAM_LR * (m_hat / (jnp.sqrt(v_hat) + ADAM_EPS) + ADAM_WD * wv), mn, vn


def _adamw(w, g, m, v, name, deps=()):
    r, c = w.shape
    tr = _pick(r, ROW_TILES) if r >= 8 else r

    def body(w_ref, g_ref, m_ref, v_ref, *rest):
        d_ref, mo_ref, vo_ref = rest[-3:]
        d_ref[...], mo_ref[...], vo_ref[...] = _adamw_update(w_ref[...], g_ref[...], m_ref[...], v_ref[...])

    blk = pl.BlockSpec((tr, c), lambda i: (i, 0))
    return pl.pallas_call(
        body, name=name, grid=(r // tr,), in_specs=[blk] * 4 + [ANY] * len(deps), out_specs=[blk] * 3,
        out_shape=[jax.ShapeDtypeStruct((r, c), F32)] * 3, compiler_params=_params("parallel"),
    )(w, g, m, v, *deps)


def _adamw_halves(w, mine, theirs, c_idx, m, v, name, deps=()):
    r, c = w.shape
    rh = r // 2
    tr = _pick(rh, [t for t in ROW_TILES if t * c * 4 <= BLOCK_BYTES])
    nb = rh // tr

    def body(c_ref, w_ref, a_ref, b_ref, m_ref, v_ref, *rest):
        g_ref, d_ref, mo_ref, vo_ref = rest[-4:]
        gv = jnp.where(pl.program_id(0) // nb == c_ref[0], a_ref[...], b_ref[...])
        g_ref[...] = gv
        d_ref[...], mo_ref[...], vo_ref[...] = _adamw_update(w_ref[...], gv, m_ref[...], v_ref[...])

    blk = pl.BlockSpec((tr, c), lambda i, cr: (i, 0))
    mine_spec = pl.BlockSpec((tr, c), lambda i, cr: (jnp.clip(i - cr[0] * nb, 0, nb - 1), 0))
    theirs_spec = pl.BlockSpec((tr, c), lambda i, cr: (jnp.clip(i - (1 - cr[0]) * nb, 0, nb - 1), 0))
    return pl.pallas_call(
        body, name=name,
        grid_spec=pltpu.PrefetchScalarGridSpec(
            num_scalar_prefetch=1, grid=(r // tr,),
            in_specs=[blk, mine_spec, theirs_spec, blk, blk] + [ANY] * len(deps), out_specs=[blk] * 4),
        out_shape=[jax.ShapeDtypeStruct((r, c), F32)] * 4, compiler_params=_params("arbitrary"),
    )(c_idx, w, mine, theirs, m, v, *deps)


ANY = pl.BlockSpec(memory_space=pl.ANY)


def _place():
    x, y, c = lax.axis_index("x"), lax.axis_index("y"), lax.axis_index("c")
    chips = [(1 - x, y), (x, 1 - y), (1 - x, 1 - y)]
    return x, y, c, chips


def _remote(src, dst, send_sem, recv_sem, to):
    return pltpu.make_async_remote_copy(src_ref=src, dst_ref=dst, send_sem=send_sem, recv_sem=recv_sem,
                                        device_id=to, device_id_type=MESH)


HBM = pl.BlockSpec(memory_space=pltpu.HBM)
SEM = pl.BlockSpec(memory_space=pltpu.SEMAPHORE)
EFFECT = pltpu.SideEffectType.DATAFLOW_SIDE_EFFECTING


def _in_hbm(a):
    return pltpu.with_memory_space_constraint(a, pltpu.HBM)


def _half(ref_rows, who):
    return pl.ds(who * (ref_rows // 2), ref_rows // 2)


def _gather_start(groups):
    items = [it for g in groups for it in g]
    n = len(items)
    sizes = [len(g) for g in groups]

    def body(*refs):
        srcs, lands = refs[:n], refs[n:2 * n]
        sems = refs[2 * n:2 * n + 2 * len(groups)]
        token = refs[-1]
        x, y, c, chips = _place()
        j = 2 * x + y
        at = 0
        for gi, g in enumerate(groups):
            send, recv = sems[2 * gi], sems[2 * gi + 1]
            for i, (shard, split) in enumerate(g):
                src, land = srcs[at], lands[at]
                at += 1
                rows = _half(shard.shape[0], c) if split else slice(None)
                for k, chip in enumerate(chips):
                    _remote(src.at[rows], land.at[j, rows], send.at[4 * i + k], recv.at[4 * i + k], (*chip, c)).start()
                _remote(src, land.at[j], send.at[4 * i + 3], recv.at[4 * i + 3], (x, y, 1 - c)).start()
        token[...] = jnp.zeros_like(token)

    sem_shapes = []
    for sz in sizes:
        sem_shapes += [pltpu.SemaphoreType.DMA((4 * sz,)), pltpu.SemaphoreType.DMA((4 * sz,))]
    out_shape = (sem_shapes + [pltpu.HBM(sh.shape, sh.dtype) for sh, _ in items]
                 + [pltpu.HBM((N_CHIPS,) + sh.shape, sh.dtype) for sh, _ in items]
                 + [jax.ShapeDtypeStruct((8, LANES), F32)])
    ns = len(sem_shapes)
    outs = pl.pallas_call(
        body, name="gather_start", in_specs=[HBM] * (2 * n),
        out_specs=[SEM] * ns + [HBM] * (2 * n) + [pl.BlockSpec(memory_space=pltpu.VMEM)],
        out_shape=out_shape, input_output_aliases={i: ns + i for i in range(2 * n)},
        compiler_params=pltpu.CompilerParams(has_side_effects=EFFECT),
    )(*[_in_hbm(sh) for sh, _ in items], *[_in_hbm(lax.empty((N_CHIPS,) + sh.shape, sh.dtype)) for sh, _ in items])
    sems, shards, lands, token = outs[:ns], outs[ns:ns + n], outs[ns + n:ns + 2 * n], outs[-1]
    res, at = [], 0
    for gi, sz in enumerate(sizes):
        res.append((shards[at:at + sz], lands[at:at + sz], sems[2 * gi], sems[2 * gi + 1]))
        at += sz
    return res, token


def _gather_pass(group, started, after, name):
    shards, lands, send, recv = started
    n = len(group)
    split_ix = [i for i, (_, split) in enumerate(group) if split]

    def body(*refs):
        lnds, send1, recv1 = refs[n:2 * n], refs[2 * n], refs[2 * n + 1]
        outs = refs[2 * n + 2 + len(after):]
        send2, recv2, token = outs[2 * n], outs[2 * n + 1], outs[2 * n + 2]
        x, y, c, chips = _place()
        sib = (x, y, 1 - c)
        for i, (shard, split) in enumerate(group):
            rows = _half(shard.shape[0], c) if split else slice(None)
            for k, (cx, cy) in enumerate(chips):
                landed = lnds[i].at[2 * cx + cy, rows]
                cp = _remote(landed, landed, send1.at[4 * i + k], recv1.at[4 * i + k], sib)
                cp.wait_send()
                cp.wait_recv()
            own = lnds[i].at[2 * x + y]
            cp = _remote(own, own, send1.at[4 * i + 3], recv1.at[4 * i + 3], sib)
            cp.wait_send()
            cp.wait_recv()
        for i2, i in enumerate(split_ix):
            rows = _half(group[i][0].shape[0], c)
            for k, (cx, cy) in enumerate(chips):
                landed = lnds[i].at[2 * cx + cy, rows]
                _remote(landed, landed, send2.at[3 * i2 + k], recv2.at[3 * i2 + k], sib).start()
        token[...] = jnp.zeros_like(token)

    n2 = len(split_ix)
    out_shape = ([pltpu.HBM(a.shape, a.dtype) for a in (*shards, *lands)]
                 + [pltpu.SemaphoreType.DMA((3 * n2,)), pltpu.SemaphoreType.DMA((3 * n2,)), jax.ShapeDtypeStruct((8, LANES), F32)])
    outs = pl.pallas_call(
        body, name=name, in_specs=[HBM] * (2 * n) + [SEM, SEM] + [ANY] * len(after),
        out_specs=[HBM] * (2 * n) + [SEM, SEM, pl.BlockSpec(memory_space=pltpu.VMEM)],
        out_shape=out_shape, input_output_aliases={i: i for i in range(2 * n)},
        compiler_params=pltpu.CompilerParams(has_side_effects=EFFECT),
    )(*shards, *lands, send, recv, *after)
    return outs[:n], (outs[n:2 * n], outs[2 * n], outs[2 * n + 1]), outs[2 * n + 2]


def _gather_wait(group, passed, after, name):
    lands, send2, recv2 = passed
    n = len(group)
    split_ix = [i for i, (_, split) in enumerate(group) if split]

    def body(*refs):
        lnds, s2, r2 = refs[:n], refs[n], refs[n + 1]
        x, y, c, chips = _place()
        sib = (x, y, 1 - c)
        for i2, i in enumerate(split_ix):
            rows = _half(group[i][0].shape[0], 1 - c)
            for k, (cx, cy) in enumerate(chips):
                landed = lnds[i].at[2 * cx + cy, rows]
                cp = _remote(landed, landed, s2.at[3 * i2 + k], r2.at[3 * i2 + k], sib)
                cp.wait_send()
                cp.wait_recv()

    return pl.pallas_call(
        body, name=name, in_specs=[HBM] * n + [SEM, SEM, ANY], out_specs=[HBM] * n,
        out_shape=[pltpu.HBM(a.shape, a.dtype) for a in lands], input_output_aliases={i: i for i in range(n)},
        compiler_params=pltpu.CompilerParams(has_side_effects=EFFECT),
    )(*lands, send2, recv2, after)


def _xfer_start(name, srcs, land_shapes, n_copies, copies, after):
    n = len(srcs)

    def body(*refs):
        src_refs, land_refs = refs[:n], refs[n:2 * n]
        send, recv, token = refs[2 * n + 1], refs[2 * n + 2], refs[-1]
        for cp in copies(src_refs, land_refs, send, recv):
            cp.start()
        token[...] = jnp.zeros_like(token)

    lands = [_in_hbm(lax.empty(shape, dtype)) for shape, dtype in land_shapes]
    out_shape = ([pltpu.SemaphoreType.DMA((n_copies,)), pltpu.SemaphoreType.DMA((n_copies,))]
                 + [pltpu.HBM(a.shape, a.dtype) for a in (*srcs, *lands)] + [jax.ShapeDtypeStruct((8, LANES), F32)])
    outs = pl.pallas_call(
        body, name=name, in_specs=[HBM] * (2 * n) + [ANY],
        out_specs=[SEM, SEM] + [HBM] * (2 * n) + [pl.BlockSpec(memory_space=pltpu.VMEM)],
        out_shape=out_shape, input_output_aliases={i: 2 + i for i in range(2 * n)},
        compiler_params=pltpu.CompilerParams(has_side_effects=EFFECT),
    )(*[_in_hbm(a) for a in srcs], *lands, after)
    return (outs[2:2 + n], outs[2 + n:2 + 2 * n], outs[0], outs[1]), outs[-1]


def _xfer_wait(name, started, copies, after):
    srcs, lands, send, recv = started
    n = len(srcs)

    def body(*refs):
        src_refs, land_refs, s_ref, r_ref = refs[:n], refs[n:2 * n], refs[2 * n], refs[2 * n + 1]
        for cp in copies(src_refs, land_refs, s_ref, r_ref):
            cp.wait_send()
            cp.wait_recv()

    outs = pl.pallas_call(
        body, name=name, in_specs=[HBM] * (2 * n) + [SEM, SEM, ANY], out_specs=[HBM] * (2 * n),
        out_shape=[pltpu.HBM(a.shape, a.dtype) for a in (*srcs, *lands)],
        input_output_aliases={i: i for i in range(2 * n)},
        compiler_params=pltpu.CompilerParams(has_side_effects=EFFECT),
    )(*srcs, *lands, send, recv, after)
    return outs[:n], outs[n:]


def _swap_copies(srcs, lands, send, recv):
    x, y, c, _ = _place()
    return [_remote(src.at[:, _half(src.shape[1], 1 - c)], land, send.at[i], recv.at[i], (x, y, 1 - c))
            for i, (src, land) in enumerate(zip(srcs, lands))]


def _scatter_copies(srcs, lands, send, recv):
    x, y, c, chips = _place()
    return [_remote(src.at[2 * cx + cy], land.at[k], send.at[3 * i + k], recv.at[3 * i + k], (cx, cy, c))
            for i, (src, land) in enumerate(zip(srcs, lands)) for k, (cx, cy) in enumerate(chips)]


def _join_copies(srcs, lands, send, recv):
    x, y, c, _ = _place()
    return [_remote(src, land, send.at[i], recv.at[i], (x, y, 1 - c)) for i, (src, land) in enumerate(zip(srcs, lands))]


def _corner(a):
    return a[(slice(0, 1),) * a.ndim]


class _Reducer:
    def __init__(self, place):
        self.place = place
        self.state = {}

    def swap(self, key, grads, after):
        shapes = [((g.shape[0], g.shape[1] // 2, g.shape[2]), g.dtype) for g in grads]
        self.state[key], token = _xfer_start("swap_start_" + key, grads, shapes, len(grads), _swap_copies, _corner(after))
        return token

    def to_chips(self, key, after):
        grads, from_sibling = _xfer_wait("swap_wait_" + key, self.state[key], _swap_copies, after)
        sums = [_add_halves(g, r, self.place) for g, r in zip(grads, from_sibling)]
        shapes = [((3,) + s.shape[1:], s.dtype) for s in sums]
        started, token = _xfer_start("scatter_start_" + key, sums, shapes, 3 * len(sums), _scatter_copies, _corner(sums[-1]))
        self.state[key] = (grads, from_sibling, started)
        return token

    def to_core(self, key, after):
        grads, from_sibling, started = self.state[key]
        _, from_chips = _xfer_wait("scatter_wait_" + key, started, _scatter_copies, after)
        halves = [_sum_chips(g, r, rc, self.place) for g, r, rc in zip(grads, from_sibling, from_chips)]
        shapes = [(h.shape, h.dtype) for h in halves]
        self.state[key], token = _xfer_start("join_start_" + key, halves, shapes, len(halves), _join_copies, _corner(halves[-1]))
        return token

    def finish(self, key, after):
        return _xfer_wait("join_wait_" + key, self.state.pop(key), _join_copies, after)


def _gather_packs(pack, deps=()):
    def body(p_ref, *rest):
        o_ref, lsem, ssem, rsem = rest[-4:]
        x, y, c, _ = _place()
        me = 4 * x + 2 * y + c
        local = pltpu.make_async_copy(p_ref, o_ref.at[me], lsem)
        local.start()
        cps = []
        for k in range(1, N_DEV):
            fx, fy, fc = (k >> 2) & 1, (k >> 1) & 1, k & 1
            to = (x ^ fx, y ^ fy, c ^ fc)
            cps.append(_remote(p_ref, o_ref.at[me], ssem.at[k - 1], rsem.at[k - 1], to))
        for cp in cps:
            cp.start()
        for k in range(1, N_DEV):
            fx, fy, fc = (k >> 2) & 1, (k >> 1) & 1, k & 1
            src = o_ref.at[4 * (x ^ fx) + 2 * (y ^ fy) + (c ^ fc)]
            _remote(src, src, ssem.at[k - 1], rsem.at[k - 1], (x, y, c)).wait_recv()
        for cp in cps:
            cp.wait_send()
        local.wait()

    return pl.pallas_call(
        body, name="gather_packs", in_specs=[ANY] * (1 + len(deps)), out_specs=ANY,
        out_shape=jax.ShapeDtypeStruct((N_DEV,) + pack.shape, pack.dtype),
        scratch_shapes=[pltpu.SemaphoreType.DMA, pltpu.SemaphoreType.DMA((N_DEV - 1,)), pltpu.SemaphoreType.DMA((N_DEV - 1,))],
    )(pack, *deps)


LANE_TILES = (512, 896, 1408, 704, 384, 256, 128)


def _layer_grads(x, target, small, wg, rest_pass, rest_wait, red, filler):
    s, d = x.shape
    f = wg["conv"].shape[1] // 2
    w_att = N_HEADS * HEAD_DIM
    in_splits = (w_att, w_att, w_att, N_HEADS, w_att, w_att, w_att, d, d)
    in_cols = sum(in_splits)
    cs = in_cols // N_CHIPS
    cp = wg["in"].shape[2]
    tm = min(s, 1024)
    t_in = cp
    t_d = _pick(d, LANE_TILES)
    t_d2 = min(d, 1024)
    t_dq = _pick(d // N_CHIPS, LANE_TILES)
    t_w = _pick(w_att, LANE_TILES)
    t_up = 2 * f // N_CHIPS
    tm_wide = min(s, 512)
    t_fq = _pick(f // N_CHIPS, LANE_TILES)
    offs = np.cumsum(in_splits)[:-1].tolist()

    h1 = _norm_fwd(x, small["g_attn"], group=d, name="rms1_fwd")
    proj_p = _mm(h1, wg["in"], mode="nn", b_kind="col", tm=tm_wide, tn=t_in, tk=d, name="mm_in")
    gains = {n: small[n].reshape(1, w_att) for n in ("g_q_fox", "g_k_fox", "g_q_dil", "g_k_dil")}
    qa, ka, va_b, fa, qb, kb, vb_b, ga, gb, qa_n, ka_n, qb_n, kb_n = _proj_split(
        proj_p, in_splits, cs, (F32, F32, BF16, F32, F32, F32, BF16, F32, F32),
        {0: gains["g_q_fox"], 1: gains["g_k_fox"], 4: gains["g_q_dil"], 5: gains["g_k_dil"]})
    fa_t = fa.T
    b_f = small["b_forget"].reshape(N_HEADS, 1)
    c_f = _forget_fwd(fa_t, b_f)
    slopes = jnp.asarray(2.0 ** (-8.0 * np.arange(1, N_HEADS + 1) / N_HEADS), dtype=F32)
    a_d = -(slopes[:, None] * jnp.arange(s, dtype=F32)[None, :])
    rows_f, cols_f = c_f[:, :, None], c_f[:, None, :]
    rows_d, cols_d = a_d[:, :, None], a_d[:, None, :]
    o_a, o_a32, lse_a = _attn_fwd(qa_n, ka_n, va_b, rows_f, cols_f, dilated=False, name="attn_fox_fwd")
    token = rest_pass("mid", o_a)
    rows_d = rows_d + token[0, 0]
    o_b, o_b32, lse_b = _attn_fwd(qb_n, kb_n, vb_b, rows_d, cols_d, dilated=True, name="attn_dil_fwd")
    wg = dict(wg, **rest_wait("mid", o_b))
    token = rest_pass("late", o_b)
    pa = _mm(o_a, wg["brf"], mode="nn", b_kind="col", tm=tm, tn=t_dq, tk=w_att, name="mm_brf", deps=(token,))
    pb = _mm(o_b, wg["brd"], mode="nn", b_kind="col", tm=tm, tn=t_dq, tk=w_att, name="mm_brd")
    merged = _gate_fwd(ga, gb, pa, pb)
    x1 = _mm(merged, wg["out"], mode="nn", b_kind="row", res=x, tm=tm, tn=t_d, tk=t_dq, name="mm_out")
    wg = dict(wg, **rest_wait("late", x1))
    h2 = _norm_fwd(x1, small["g_ffn"], group=d, name="rms2_fwd")
    u = _mm(h2, wg["up"], mode="nn", b_kind="col", tm=tm_wide, tn=t_up, tk=d, name="mm_up")
    act = _conv_glu_fwd(u, wg["conv"], wg["bconv"])
    y = _mm(act, wg["down"], mode="nn", b_kind="row", res=x1, tm=tm, tn=t_d2, tk=t_fq, name="mm_down")
    dy_f, dy_b, loss_blk = _loss_head(y, target)

    d_act = _mm(dy_b, wg["down"], mode="nt", b_kind="row", tm=tm, tn=t_fq, tk=d, name="mm_down_dx")
    g_down = _mm(act, dy_b, mode="tn", out_dtype=BF16, out_kind="row", tm=t_fq, tn=t_d2, tk=s, name="mm_down_dw")
    tok = red.swap("down", [g_down], g_down)
    du_g, du_v, st_g, st_v = _conv_glu_bwd(u, d_act, wg["conv"] + tok[0, 0], wg["bconv"])
    tok = red.to_chips("down", du_g)
    du = jnp.concatenate([du_g, du_v], axis=1)
    g_up = _mm(h2, du, mode="tn", out_dtype=BF16, out_kind="col", tm=t_d2, tn=t_up, tk=s, name="mm_up_dw", deps=(tok,))
    tok = red.to_core("down", g_up)
    tok2 = red.swap("up", [g_up], g_up)
    dh2 = _mm(du, wg["up"], mode="nt", b_kind="col", tm=tm, tn=t_d2, tk=t_up, name="mm_up_dx", deps=(tok, tok2))
    tok = red.to_chips("up", dh2)
    dx1_b, dx1_f, dg_ffn = _norm_bwd(dh2, x1, small["g_ffn"], group=d, res=dy_f, out_dtypes=(BF16, F32), name="rms2_bwd")
    d_merged = _mm(dx1_b, wg["out"], mode="nt", b_kind="row", tm=tm, tn=t_dq, tk=d, name="mm_out_dx", deps=(tok,))
    g_out = _mm(merged, dx1_b, mode="tn", out_dtype=BF16, out_kind="row", tm=t_dq, tn=t_d2, tk=s, name="mm_out_dw")
    dpa, dpb, dga, dgb = _gate_bwd(d_merged, ga, gb, pa, pb)
    do_a = _mm(dpa, wg["brf"], mode="nt", b_kind="col", out_dtype=BF16, tm=s, tn=w_att, tk=t_dq, name="mm_brf_dx")
    do_b = _mm(dpb, wg["brd"], mode="nt", b_kind="col", out_dtype=BF16, tm=s, tn=w_att, tk=t_dq, name="mm_brd_dx")
    g_brf = _mm(o_a, dpa, mode="tn", out_dtype=BF16, out_kind="col", tm=w_att, tn=t_dq, tk=s, name="mm_brf_dw")
    g_brd = _mm(o_b, dpb, mode="tn", out_dtype=BF16, out_kind="col", tm=w_att, tn=t_dq, tk=s, name="mm_brd_dw")
    tok = red.swap("mix", [g_out, g_brf, g_brd], g_brd)
    dqa_n, dka_n, dva, dac_a = _attn_bwd(qa_n, ka_n, va_b, o_a32, do_a, lse_a, rows_f + tok[0, 0], cols_f, dilated=False, name="attn_fox_bwd")
    tok = red.to_core("up", dqa_n)
    tok2 = red.to_chips("mix", dqa_n)
    dqb_n, dkb_n, dvb, _ = _attn_bwd(qb_n, kb_n, vb_b, o_b32, do_b, lse_b, rows_d + (tok[0, 0] + tok2[0, 0]), cols_d, dilated=True, name="attn_dil_bwd")
    tok = red.to_core("mix", dqb_n)
    dfa_t, db_f = _forget_bwd(dac_a[:, 0, :], fa_t, b_f)
    dproj_p, dgains = _dproj_merge(
        [dqa_n, dka_n, dva, dfa_t.T, dqb_n, dkb_n, dvb, dga, dgb], in_splits, cs, cp,
        {0: (qa, gains["g_q_fox"]), 1: (ka, gains["g_k_fox"]), 4: (qb, gains["g_q_dil"]), 5: (kb, gains["g_k_dil"])})
    dg_qf, dg_kf, dg_qd, dg_kd = dgains[0], dgains[1], dgains[4], dgains[5]
    g_in = _mm(h1, dproj_p, mode="tn", out_dtype=BF16, out_kind="col", tm=t_d2, tn=t_in, tk=s, name="mm_in_dw", deps=(tok,))
    tok = red.swap("in", [g_in], g_in)
    tok = red.to_chips("in", filler(tok))
    dh1 = _mm(dproj_p, wg["in"], mode="nt", b_kind="col", tm=tm, tn=t_d2, tk=t_in, name="mm_in_dx", deps=(tok,))
    grad_x, dg_attn = _norm_bwd(dh1, x, small["g_attn"], group=d, res=dx1_f, out_dtypes=(F32,), name="rms1_bwd")

    small_grads = {
        "g_attn": dg_attn, "b_forget": db_f.reshape(1, N_HEADS),
        "g_q_fox": dg_qf, "g_k_fox": dg_kf, "g_q_dil": dg_qd, "g_k_dil": dg_kd, "g_ffn": dg_ffn,
        "w_conv": jnp.concatenate([st_g[0:3], st_v[0:3]], axis=1),
        "b_conv": jnp.concatenate([st_g[3:4], st_v[3:4]], axis=1),
        "loss": loss_blk[0:1, 0:1],
    }
    return small_grads, grad_x


SMALL_ORDER = ("g_attn", "b_forget", "g_q_fox", "g_k_fox", "g_q_dil", "g_k_dil", "g_ffn", "w_conv", "b_conv", "loss")
WEIGHT_ORDER = ("g_attn", "w_in", "b_forget", "g_q_fox", "g_k_fox", "g_q_dil", "g_k_dil", "w_br_fox", "w_br_dil",
                "w_out", "g_ffn", "w_up", "w_conv", "b_conv", "w_down")
BIG = {"w_in": "in", "w_br_fox": "brf", "w_br_dil": "brd", "w_out": "out", "w_up": "up", "w_down": "down"}


def kernel(x, g_attn, w_in, b_forget, g_q_fox, g_k_fox, g_q_dil, g_k_dil, w_br_fox, w_br_dil, w_out, g_ffn, w_up, w_conv, b_conv, w_down, loss_target, m_g_attn, m_w_in, m_b_forget, m_g_q_fox, m_g_k_fox, m_g_q_dil, m_g_k_dil, m_w_br_fox, m_w_br_dil, m_w_out, m_g_ffn, m_w_up, m_w_conv, m_b_conv, m_w_down, v_g_attn, v_w_in, v_b_forget, v_g_q_fox, v_g_k_fox, v_g_q_dil, v_g_k_dil, v_w_br_fox, v_w_br_dil, v_w_out, v_g_ffn, v_w_up, v_w_conv, v_b_conv, v_w_down):
    w = dict(g_attn=g_attn, w_in=w_in, b_forget=b_forget, g_q_fox=g_q_fox, g_k_fox=g_k_fox, g_q_dil=g_q_dil,
             g_k_dil=g_k_dil, w_br_fox=w_br_fox, w_br_dil=w_br_dil, w_out=w_out, g_ffn=g_ffn, w_up=w_up,
             w_conv=w_conv, b_conv=b_conv, w_down=w_down)
    m = dict(g_attn=m_g_attn, w_in=m_w_in, b_forget=m_b_forget, g_q_fox=m_g_q_fox, g_k_fox=m_g_k_fox,
             g_q_dil=m_g_q_dil, g_k_dil=m_g_k_dil, w_br_fox=m_w_br_fox, w_br_dil=m_w_br_dil, w_out=m_w_out,
             g_ffn=m_g_ffn, w_up=m_w_up, w_conv=m_w_conv, b_conv=m_b_conv, w_down=m_w_down)
    v = dict(g_attn=v_g_attn, w_in=v_w_in, b_forget=v_b_forget, g_q_fox=v_g_q_fox, g_k_fox=v_g_k_fox,
             g_q_dil=v_g_q_dil, g_k_dil=v_g_k_dil, w_br_fox=v_w_br_fox, w_br_dil=v_w_br_dil, w_out=v_w_out,
             g_ffn=v_g_ffn, w_up=v_w_up, w_conv=v_w_conv, b_conv=v_b_conv, w_down=v_w_down)
    xi, yi, ci = lax.axis_index("x"), lax.axis_index("y"), lax.axis_index("c")
    chip = (2 * xi + yi).astype(jnp.int32)
    c_idx = ci.astype(jnp.int32).reshape(1)
    j_idx = chip.reshape(1)

    cs = w_in.shape[2]
    cp = _round_up(cs, LANES)
    shards = {
        "in": jnp.pad(w_in[0].astype(BF16), ((0, 0), (0, cp - cs))),
        "brf": w_br_fox[0].astype(BF16), "brd": w_br_dil[0].astype(BF16), "out": w_out[0].astype(BF16),
        "up": w_up[0].astype(BF16), "down": w_down[0].astype(BF16),
    }
    names = tuple(shards)
    conv_pad = jnp.pad(w_conv[0], ((0, 8 - w_conv.shape[1]), (0, 0)))
    first = [(shards["in"], True), (conv_pad, False)]
    later = {"mid": ("brf", "brd", "out"), "late": ("up", "down")}
    groups = {key: [(shards[n], True) for n in members] for key, members in later.items()}
    (started_first, *started_later), token = _gather_start([first, *groups.values()])
    started = dict(zip(later, started_later))
    token, w["w_in"], m["w_in"], v["w_in"] = lax.optimization_barrier((token, w["w_in"], m["w_in"], v["w_in"]))
    w2, m2, v2 = ({n: a[n].reshape(a[n].shape[-2], a[n].shape[-1]) for n in BIG} for a in (w, m, v))
    early = (token, w2["w_in"], m2["w_in"], v2["w_in"])
    own_first, passed_first, token = _gather_pass(first, started_first, early, "gather_pass_in")
    land_in, land_conv = _gather_wait(first, passed_first, token, "gather_wait_in")
    wg = {"in": land_in, "bconv": b_conv,
          "conv": jnp.transpose(land_conv[:, :w_conv.shape[1], :], (1, 0, 2)).reshape(w_conv.shape[1], -1)}
    small = {n: w[n] for n in ("g_attn", "b_forget", "g_q_fox", "g_k_fox", "g_q_dil", "g_k_dil", "g_ffn")}
    small = {n: (a[0] if a.ndim == 3 else a) for n, a in small.items()}
    in_flight = {}

    def rest_pass(key, after):
        own, passed, tok = _gather_pass(groups[key], started[key], (after,), "gather_pass_" + key)
        in_flight[key] = (own, passed)
        return tok

    def rest_wait(key, after):
        own, passed = in_flight.pop(key)
        lands = _gather_wait(groups[key], passed, after, "gather_wait_" + key)
        return dict(zip(later[key], lands))

    reducer = _Reducer(jnp.stack([chip, ci.astype(jnp.int32)]))
    g_out, d_out, m_out, v_out = {}, {}, {}, {}
    mine, theirs = {}, {}

    def first_element(arrays):
        return jnp.stack([a[(0,) * a.ndim] for a in arrays])

    def update_big(n, deps):
        g2, dl, mn, vn = _adamw_halves(w2[n], mine[BIG[n]], theirs[BIG[n]], c_idx, m2[n], v2[n],
                                       name="adamw_" + n, deps=deps)
        g_out[n], d_out[n], m_out[n], v_out[n] = (a.reshape(w[n].shape) for a in (g2, dl, mn, vn))

    def update_down(tok):
        (mine["down"],), (theirs["down"],) = reducer.finish("down", tok)
        update_big("w_down", (tok,))
        return v_out["w_down"]

    small_grads, grad_x = _layer_grads(x[0], loss_target[0], small, wg, rest_pass, rest_wait, reducer, update_down)

    for key, members in (("up", ("up",)), ("mix", ("out", "brf", "brd"))):
        mine_k, theirs_k = reducer.finish(key, grad_x)
        mine.update(zip(members, mine_k))
        theirs.update(zip(members, theirs_k))
    others = ("w_up", "w_out", "w_br_fox", "w_br_dil")
    for n in others:
        update_big(n, (grad_x,))

    flat = jnp.concatenate([small_grads[n].reshape(-1) for n in SMALL_ORDER])
    rows = _round_up(flat.shape[0], 8 * LANES) // LANES
    pack = jnp.pad(flat, (0, rows * LANES - flat.shape[0])).reshape(rows, LANES)
    packs = _gather_packs(pack, deps=(first_element([v_out[n] for n in others]),))
    total = _sum_devices(packs).reshape(-1)
    red, at = {}, 0
    for n in SMALL_ORDER:
        size = small_grads[n].size
        red[n] = total[at:at + size].reshape(small_grads[n].shape)
        at += size
    loss = red["loss"].reshape(())
    c2 = w_conv.shape[2]
    red["w_conv"] = lax.dynamic_slice_in_dim(red["w_conv"], chip * c2, c2, axis=1)

    smalls = [n for n in WEIGHT_ORDER if n not in BIG]
    for n in smalls:
        shape = w[n].shape
        r2 = (shape[-2], shape[-1]) if n not in ("g_attn", "b_forget", "g_ffn", "b_conv") else (1, shape[-1])
        g2 = red[n].reshape(r2)
        dl, mn, vn = _adamw(w[n].reshape(r2), g2, m[n].reshape(r2), v[n].reshape(r2), name="adamw_" + n)
        g_out[n], d_out[n], m_out[n], v_out[n] = (a.reshape(shape) for a in (g2, dl, mn, vn))
    tok = reducer.to_core("in", first_element([v_out[n] for n in smalls]))
    (mine_in,), (theirs_in,) = reducer.finish("in", tok)
    mine["in"], theirs["in"] = mine_in[:, :cs], theirs_in[:, :cs]
    update_big("w_in", (tok,))

    return (loss, grad_x[None], *[g_out[n] for n in WEIGHT_ORDER], *[d_out[n] for n in WEIGHT_ORDER],
            *[m_out[n] for n in WEIGHT_ORDER], *[v_out[n] for n in WEIGHT_ORDER])
```

```python
import functools
import math

import jax
import jax.numpy as jnp
import numpy as np
from jax import lax
from jax.experimental import pallas as pl
from jax.experimental.pallas import tpu as pltpu

F32 = jnp.float32
BF16 = jnp.bfloat16
HEAD_DIM = 128
N_HEADS = 8
EPS = 1e-6
NEG = -1e30
N_CHIPS = 4
N_DEV = 8
LANES = 128
VMEM_LIMIT_BYTES = 56 * 1024 * 1024
DIL_PATTERNS = ((128, 1), (512, 4), (2048, 16))
ATTN_TILE = 512
ADAM_LR, ADAM_B1, ADAM_B2, ADAM_EPS, ADAM_WD, ADAM_STEP = 0.001, 0.9, 0.999, 1e-08, 0.01, 10
MESH = pl.DeviceIdType.MESH


def _params(*sem):
    return pltpu.CompilerParams(dimension_semantics=sem, vmem_limit_bytes=VMEM_LIMIT_BYTES)


def _round_up(n, m):
    return -(-n // m) * m


def _pick(dim, prefs):
    for p in prefs:
        if dim % p == 0:
            return p
    raise ValueError(f"no tile for {dim} in {prefs}")


def _logical_shape(arr, kind):
    if kind is None:
        return arr.shape
    s, r, c = arr.shape
    return (r, s * c) if kind == "col" else (s * r, c)


def _spec(shape, kind, br, bc, fi, fj):
    if kind is None:
        return pl.BlockSpec((br, bc), lambda *g: (fi(*g), fj(*g)))
    _, r, c = shape
    if kind == "col":
        nb = c // bc
        assert nb * bc == c, (shape, bc)
        return pl.BlockSpec((None, br, bc), lambda *g: (fj(*g) // nb, fi(*g), fj(*g) % nb))
    nb = r // br
    assert nb * br == r, (shape, br)
    return pl.BlockSpec((None, br, bc), lambda *g: (fi(*g) // nb, fi(*g) % nb, fj(*g)))


def _mm(a, b, *, mode, tm, tn, tk, name, a_kind=None, b_kind=None, out_kind=None,
        out_dtype=F32, res=None, deps=()):
    la, lb = _logical_shape(a, a_kind), _logical_shape(b, b_kind)
    if mode == "nn":
        (m, k), (k2, n) = la, lb
    elif mode == "nt":
        (m, k), (n, k2) = la, lb
    else:
        (k, m), (k2, n) = la, lb
    assert k == k2, (name, la, lb)
    assert m % tm == 0 and n % tn == 0 and k % tk == 0, (name, m, n, k, tm, tn, tk)
    nk = k // tk
    im = lambda i, j, l: i
    jn = lambda i, j, l: j
    lk = lambda i, j, l: l
    if mode == "tn":
        a_spec = _spec(a.shape, a_kind, tk, tm, lk, im)
        dims = (((0,), (0,)), ((), ()))
    else:
        a_spec = _spec(a.shape, a_kind, tm, tk, im, lk)
        dims = (((1,), (1,)), ((), ())) if mode == "nt" else (((1,), (0,)), ((), ()))
    if mode == "nt":
        b_spec = _spec(b.shape, b_kind, tn, tk, jn, lk)
    else:
        b_spec = _spec(b.shape, b_kind, tk, tn, lk, jn)
    if out_kind is None:
        oshape = (m, n)
    elif out_kind == "col":
        oshape = (N_CHIPS, m, n // N_CHIPS)
    else:
        oshape = (N_CHIPS, m // N_CHIPS, n)
    o_spec = _spec(oshape, out_kind, tm, tn, im, jn)
    in_specs = [a_spec, b_spec]
    args = [a, b]
    if res is not None:
        in_specs.append(pl.BlockSpec((tm, tn), lambda i, j, l: (i, j)))
        args.append(res)
    in_specs += [pl.BlockSpec(memory_space=pl.ANY)] * len(deps)
    args += list(deps)

    def finish(out, res_ref, o_ref):
        if res_ref is not None:
            out = out + res_ref[...]
        o_ref[...] = out.astype(o_ref.dtype)

    def body_whole_k(*refs):
        res_ref = refs[2] if res is not None else None
        finish(lax.dot_general(refs[0][...], refs[1][...], dims, preferred_element_type=F32), res_ref, refs[-1])

    def body(*refs):
        a_ref, b_ref = refs[0], refs[1]
        res_ref = refs[2] if res is not None else None
        o_ref, acc_ref = refs[-2], refs[-1]
        step = pl.program_id(2)

        @pl.when(step == 0)
        def _():
            acc_ref[...] = jnp.zeros_like(acc_ref)

        acc_ref[...] += lax.dot_general(a_ref[...], b_ref[...], dims, preferred_element_type=F32)

        @pl.when(step == nk - 1)
        def _():
            finish(acc_ref[...], res_ref, o_ref)

    return pl.pallas_call(
        body_whole_k if nk == 1 else body, name=name, grid=(m // tm, n // tn, nk),
        in_specs=in_specs, out_specs=o_spec,
        out_shape=jax.ShapeDtypeStruct(oshape, out_dtype),
        scratch_shapes=[] if nk == 1 else [pltpu.VMEM((tm, tn), F32)],
        compiler_params=_params("parallel", "parallel", "arbitrary"),
    )(*args)


def _pieces(splits, cs, cp):
    out, g0 = [], 0
    for width in splits:
        g1, runs = g0 + width, []
        for j in range(N_CHIPS):
            a, b = max(g0, cs * j), min(g1, cs * (j + 1))
            if a < b:
                runs.append((j * cp + a - cs * j, a - g0, b - a))
        out.append(runs)
        g0 = g1
    return out


def _head_norm(xv, gv):
    r = lax.rsqrt(jnp.mean(xv * xv, axis=-1, keepdims=True) + EPS)
    return (xv * r) * gv


def _head_norm_bwd(dyv, xv, gv):
    r = lax.rsqrt(jnp.mean(xv * xv, axis=-1, keepdims=True) + EPS)
    xr = xv * r
    gdy = dyv * gv
    return r * (gdy - xr * jnp.mean(gdy * xr, axis=-1, keepdims=True)), jnp.sum(dyv * xr, axis=0, keepdims=True)


def _proj_split(proj_p, splits, cs, dtypes, gains, tm=128):
    s, wp = proj_p.shape
    pieces = _pieces(splits, cs, wp // N_CHIPS)
    normed = sorted(gains)
    nseg = len(splits)

    def body(p_ref, *refs):
        g_refs, o_refs, n_refs = refs[:len(normed)], refs[len(normed):len(normed) + nseg], refs[len(normed) + nseg:]
        for o_ref, runs in zip(o_refs, pieces):
            for src, dst, n in runs:
                o_ref[:, dst:dst + n] = p_ref[:, src:src + n].astype(o_ref.dtype)
        for g_ref, n_ref, i in zip(g_refs, n_refs, normed):
            for c0 in range(0, splits[i], HEAD_DIM):
                cols = slice(c0, c0 + HEAD_DIM)
                n_ref[:, cols] = _head_norm(o_refs[i][:, cols], g_ref[:, cols]).astype(n_ref.dtype)

    return pl.pallas_call(
        body, name="proj_split", grid=(s // tm,),
        in_specs=[pl.BlockSpec((tm, wp), lambda i: (i, 0))] + [pl.BlockSpec((1, splits[i]), lambda i: (0, 0)) for i in normed],
        out_specs=[pl.BlockSpec((tm, w), lambda i: (i, 0)) for w in splits]
        + [pl.BlockSpec((tm, splits[i]), lambda i: (i, 0)) for i in normed],
        out_shape=[jax.ShapeDtypeStruct((s, w), dt) for w, dt in zip(splits, dtypes)]
        + [jax.ShapeDtypeStruct((s, splits[i]), BF16) for i in normed],
        compiler_params=_params("parallel"),
    )(proj_p, *[gains[i] for i in normed])


def _dproj_merge(parts, splits, cs, cp, norms, tm=128):
    s = parts[0].shape[0]
    wp = N_CHIPS * cp
    pieces = _pieces(splits, cs, cp)
    normed = sorted(norms)
    nseg, nn = len(splits), len(normed)

    def body(*refs):
        p_refs, x_refs, g_refs = refs[:nseg], refs[nseg:nseg + nn], refs[nseg + nn:nseg + 2 * nn]
        o_ref, dg_refs = refs[nseg + 2 * nn], refs[nseg + 2 * nn + 1:nseg + 3 * nn + 1]
        stage, tmp = refs[-2], refs[-1]

        @pl.when(pl.program_id(0) == 0)
        def _():
            for dg_ref in dg_refs:
                dg_ref[...] = jnp.zeros_like(dg_ref)

        for j in range(N_CHIPS):
            stage[:, j * cp + cs:(j + 1) * cp] = jnp.zeros((tm, cp - cs), F32)
        for i, (p_ref, runs) in enumerate(zip(p_refs, pieces)):
            src_ref = p_ref
            if i in norms:
                k = normed.index(i)
                for c0 in range(0, splits[i], HEAD_DIM):
                    cols = slice(c0, c0 + HEAD_DIM)
                    dx, dg = _head_norm_bwd(p_ref[:, cols].astype(F32), x_refs[k][:, cols], g_refs[k][:, cols])
                    tmp[:, cols] = dx
                    dg_refs[k][:, cols] += dg
                src_ref = tmp
            for dst, src, n in runs:
                stage[:, dst:dst + n] = src_ref[:, src:src + n].astype(F32)
        o_ref[...] = stage[...].astype(o_ref.dtype)

    wmax = max(splits[i] for i in normed)
    row = lambda w: pl.BlockSpec((tm, w), lambda i: (i, 0))
    vec = lambda w: pl.BlockSpec((1, w), lambda i: (0, 0))
    outs = pl.pallas_call(
        body, name="dproj_merge", grid=(s // tm,),
        in_specs=[row(w) for w in splits] + [row(splits[i]) for i in normed] + [vec(splits[i]) for i in normed],
        out_specs=[row(wp)] + [vec(splits[i]) for i in normed],
        out_shape=[jax.ShapeDtypeStruct((s, wp), BF16)] + [jax.ShapeDtypeStruct((1, splits[i]), F32) for i in normed],
        scratch_shapes=[pltpu.VMEM((tm, wp), F32), pltpu.VMEM((tm, wmax), F32)],
        compiler_params=_params("arbitrary"),
    )(*parts, *[norms[i][0] for i in normed], *[norms[i][1] for i in normed])
    return outs[0], dict(zip(normed, outs[1:]))


def _norm_fwd(x, g, *, group, name, tm=256):
    s, w = x.shape
    ng = w // group

    def body(x_ref, g_ref, o_ref):
        for i in range(ng):
            cols = slice(i * group, (i + 1) * group)
            xv = x_ref[:, cols]
            r = lax.rsqrt(jnp.mean(xv * xv, axis=-1, keepdims=True) + EPS)
            o_ref[:, cols] = ((xv * r) * g_ref[:, cols]).astype(o_ref.dtype)

    return pl.pallas_call(
        body, name=name, grid=(s // tm,),
        in_specs=[pl.BlockSpec((tm, w), lambda i: (i, 0)), pl.BlockSpec((1, w), lambda i: (0, 0))],
        out_specs=pl.BlockSpec((tm, w), lambda i: (i, 0)),
        out_shape=jax.ShapeDtypeStruct((s, w), BF16),
        compiler_params=_params("parallel"),
    )(x, g)


def _norm_bwd(dy, x, g, *, group, name, res=None, out_dtypes=(BF16,), tm=256, deps=()):
    s, w = x.shape
    ng = w // group
    n_in = 4 if res is not None else 3

    def body(*refs):
        dy_ref, x_ref, g_ref = refs[:3]
        res_ref = refs[3] if res is not None else None
        outs = refs[n_in + len(deps):]
        dx_refs, dg_ref = outs[:-1], outs[-1]

        @pl.when(pl.program_id(0) == 0)
        def _():
            dg_ref[...] = jnp.zeros_like(dg_ref)

        for i in range(ng):
            cols = slice(i * group, (i + 1) * group)
            xv = x_ref[:, cols]
            dyv = dy_ref[:, cols].astype(F32)
            r = lax.rsqrt(jnp.mean(xv * xv, axis=-1, keepdims=True) + EPS)
            xr = xv * r
            dg_ref[:, cols] += jnp.sum(dyv * xr, axis=0, keepdims=True)
            gdy = dyv * g_ref[:, cols]
            dx = r * (gdy - xr * jnp.mean(gdy * xr, axis=-1, keepdims=True))
            if res_ref is not None:
                dx = dx + res_ref[:, cols]
            for dx_ref in dx_refs:
                dx_ref[:, cols] = dx.astype(dx_ref.dtype)

    row = pl.BlockSpec((tm, w), lambda i: (i, 0))
    vec = pl.BlockSpec((1, w), lambda i: (0, 0))
    in_specs = [row, row, vec] + ([row] if res is not None else []) + [pl.BlockSpec(memory_space=pl.ANY)] * len(deps)
    args = [dy, x, g] + ([res] if res is not None else []) + list(deps)
    out_specs = [row] * len(out_dtypes) + [vec]
    out_shape = [jax.ShapeDtypeStruct((s, w), dt) for dt in out_dtypes] + [jax.ShapeDtypeStruct((1, w), F32)]
    return pl.pallas_call(
        body, name=name, grid=(s // tm,), in_specs=in_specs, out_specs=out_specs,
        out_shape=out_shape, compiler_params=_params("arbitrary"),
    )(*args)


def _split3(v):
    p1 = v.astype(BF16)
    r1 = v - p1.astype(F32)
    p2 = r1.astype(BF16)
    p3 = (r1 - p2.astype(F32)).astype(BF16)
    return p1, p2, p3


def _tri_sum(v, reverse, tcol=512):
    h, s = v.shape
    tcol = min(tcol, s)
    parts = _split3(v)
    outs = []
    for j in range(s // tcol):
        src = lax.broadcasted_iota(jnp.int32, (s, tcol), 0)
        dst = lax.broadcasted_iota(jnp.int32, (s, tcol), 1) + j * tcol
        keep = (src >= dst) if reverse else (src <= dst)
        tri = jnp.where(keep, 1.0, 0.0).astype(BF16)
        acc = jnp.zeros((h, tcol), F32)
        for p in parts:
            acc = acc + jnp.dot(p, tri, preferred_element_type=F32)
        outs.append(acc)
    return outs


def _forget_fwd(fa_t, b):
    h, s = fa_t.shape
    tcol = min(512, s)

    def body(f_ref, b_ref, c_ref):
        z = f_ref[...] + b_ref[...]
        logf = jnp.minimum(z, 0.0) - jnp.log(1.0 + jnp.exp(-jnp.abs(z)))
        for j, blk in enumerate(_tri_sum(logf, reverse=False, tcol=tcol)):
            c_ref[:, j * tcol:(j + 1) * tcol] = blk

    return pl.pallas_call(
        body, name="forget_fwd", out_shape=jax.ShapeDtypeStruct((h, s), F32),
        compiler_params=_params(),
    )(fa_t, b)


def _forget_bwd(dacol, fa_t, b):
    h, s = fa_t.shape
    tcol = min(512, s)

    def body(d_ref, f_ref, b_ref, dfa_ref, db_ref):
        z = f_ref[...] + b_ref[...]
        dc = -d_ref[...]
        total = jnp.zeros((h, 1), F32)
        for j, blk in enumerate(_tri_sum(dc, reverse=True, tcol=tcol)):
            cols = slice(j * tcol, (j + 1) * tcol)
            dfa = blk * (1.0 - jax.nn.sigmoid(z[:, cols]))
            dfa_ref[:, cols] = dfa
            total = total + jnp.sum(dfa, axis=-1, keepdims=True)
        db_ref[...] = total

    return pl.pallas_call(
        body, name="forget_bwd",
        out_shape=[jax.ShapeDtypeStruct((h, s), F32), jax.ShapeDtypeStruct((h, 1), F32)],
        compiler_params=_params(),
    )(dacol, fa_t, b)


def _distance_bias(s, tile, dilated):
    nb = s // tile
    b = lax.broadcasted_iota(jnp.int32, (nb, tile, tile), 0)
    dist = b * tile + lax.broadcasted_iota(jnp.int32, (nb, tile, tile), 1) - lax.broadcasted_iota(jnp.int32, (nb, tile, tile), 2)
    if not dilated:
        return jnp.where(dist >= 0, 0.0, NEG).astype(F32)
    mult = jnp.zeros(dist.shape, jnp.int32)
    for window, dil in DIL_PATTERNS:
        mult = mult + ((dist >= 0) & (dist <= window) & ((dist & (dil - 1)) == 0)).astype(jnp.int32)
    logm = jnp.where(mult == 3, math.log(3.0), jnp.where(mult == 2, math.log(2.0), 0.0))
    return jnp.where(mult > 0, logm, NEG).astype(F32)


def _logits(q, k, arow, acol, bias):
    s = lax.dot_general(q, k, (((1,), (1,)), ((), ())), preferred_element_type=F32)
    return s * (1.0 / math.sqrt(HEAD_DIM)) + arow - acol + bias


def _attn_fwd(q, k, v, arow, acol, *, dilated, name, tq=ATTN_TILE, tk=ATTN_TILE):
    two_term = not dilated
    s, w = q.shape
    nh = w // HEAD_DIM
    assert tq == tk
    tq = tk = min(tq, s)
    nq, nk = s // tq, s // tk

    def body(q_ref, k_ref, v_ref, ar_ref, ac_ref, b_ref, o_ref, of_ref, lse_ref, m_ref, l_ref, acc_ref):
        qi, ki = pl.program_id(1), pl.program_id(2)

        @pl.when(ki == 0)
        def _():
            m_ref[...] = jnp.full_like(m_ref, NEG)
            l_ref[...] = jnp.zeros_like(l_ref)
            acc_ref[...] = jnp.zeros_like(acc_ref)

        @pl.when(ki <= qi)
        def _():
            sc = _logits(q_ref[...], k_ref[...], ar_ref[...], ac_ref[...], b_ref[...])
            m_new = jnp.maximum(m_ref[...], jnp.max(sc, axis=-1, keepdims=True))
            alpha = jnp.exp(m_ref[...] - m_new)
            p = jnp.exp(sc - m_new)
            l_ref[...] = alpha * l_ref[...] + jnp.sum(p, axis=-1, keepdims=True)
            p_hi = p.astype(BF16)
            vv = v_ref[...]
            pv = jnp.dot(p_hi, vv, preferred_element_type=F32)
            if two_term:
                pv = pv + jnp.dot((p - p_hi.astype(F32)).astype(BF16), vv, preferred_element_type=F32)
            acc_ref[...] = alpha * acc_ref[...] + pv
            m_ref[...] = m_new

        @pl.when(ki == nk - 1)
        def _():
            out = acc_ref[...] / l_ref[...]
            o_ref[...] = out.astype(o_ref.dtype)
            of_ref[...] = out
            lse_ref[...] = m_ref[...] + jnp.log(l_ref[...])

    kv = pl.BlockSpec((tk, HEAD_DIM), lambda h, i, j: (jnp.minimum(j, i), h))
    return pl.pallas_call(
        body, name=name, grid=(nh, nq, nk),
        in_specs=[pl.BlockSpec((tq, HEAD_DIM), lambda h, i, j: (i, h)), kv, kv,
                  pl.BlockSpec((None, tq, 1), lambda h, i, j: (h, i, 0)),
                  pl.BlockSpec((None, 1, tk), lambda h, i, j: (h, 0, jnp.minimum(j, i))),
                  pl.BlockSpec((None, tq, tk), lambda h, i, j: (jnp.maximum(i - j, 0), 0, 0))],
        out_specs=[pl.BlockSpec((tq, HEAD_DIM), lambda h, i, j: (i, h)),
                   pl.BlockSpec((tq, HEAD_DIM), lambda h, i, j: (i, h)),
                   pl.BlockSpec((None, tq, 1), lambda h, i, j: (h, i, 0))],
        out_shape=[jax.ShapeDtypeStruct((s, w), BF16), jax.ShapeDtypeStruct((s, w), F32),
                   jax.ShapeDtypeStruct((nh, s, 1), F32)],
        scratch_shapes=[pltpu.VMEM((tq, 1), F32), pltpu.VMEM((tq, 1), F32), pltpu.VMEM((tq, HEAD_DIM), F32)],
        compiler_params=_params("parallel", "parallel", "arbitrary"),
    )(q, k, v, arow, acol, _distance_bias(s, tq, dilated))


def _attn_bwd(q, k, v, o, do, lse, arow, acol, *, dilated, name, tq=ATTN_TILE, tk=ATTN_TILE):
    s, w = q.shape
    nh = w // HEAD_DIM
    assert tq == tk
    tq = tk = min(tq, s)
    nq, nk = s // tq, s // tk
    scale = 1.0 / math.sqrt(HEAD_DIM)

    def body(q_ref, k_ref, v_ref, o_ref, do_ref, lse_ref, ar_ref, ac_ref, b_ref,
             dq_ref, dk_ref, dv_ref, dac_ref, dk_acc, dv_acc, dac_acc):
        ki, qi = pl.program_id(1), pl.program_id(2)

        @pl.when((ki == 0) & (qi == 0))
        def _():
            dq_ref[...] = jnp.zeros_like(dq_ref)

        @pl.when(qi == 0)
        def _():
            dk_acc[...] = jnp.zeros_like(dk_acc)
            dv_acc[...] = jnp.zeros_like(dv_acc)
            dac_acc[...] = jnp.zeros_like(dac_acc)

        @pl.when(qi >= ki)
        def _():
            qv, kvv, dov = q_ref[...], k_ref[...], do_ref[...]
            sc = _logits(qv, kvv, ar_ref[...], ac_ref[...], b_ref[...])
            p = jnp.exp(sc - lse_ref[...])
            dp = lax.dot_general(dov, v_ref[...], (((1,), (1,)), ((), ())), preferred_element_type=F32)
            delta = jnp.sum(dov.astype(F32) * o_ref[...].astype(F32), axis=-1, keepdims=True)
            ds = p * (dp - delta)
            dsb = ds.astype(BF16)
            dv_acc[...] += lax.dot_general(p.astype(BF16), dov, (((0,), (0,)), ((), ())), preferred_element_type=F32)
            dk_acc[...] += lax.dot_general(dsb, qv, (((0,), (0,)), ((), ())), preferred_element_type=F32)
            rows = pl.ds(pl.multiple_of(qi * tq, tq), tq)
            dq_ref[rows, :] += jnp.dot(dsb, kvv, preferred_element_type=F32) * scale
            dac_acc[...] += jnp.sum(ds, axis=0, keepdims=True)

        @pl.when(qi == nq - 1)
        def _():
            dk_ref[...] = dk_acc[...] * scale
            dv_ref[...] = dv_acc[...]
            dac_ref[...] = dac_acc[...]

    qs = pl.BlockSpec((tq, HEAD_DIM), lambda h, j, i: (jnp.maximum(i, j), h))
    ks = pl.BlockSpec((tk, HEAD_DIM), lambda h, j, i: (j, h))
    rowv = pl.BlockSpec((None, tq, 1), lambda h, j, i: (h, jnp.maximum(i, j), 0))
    colv = pl.BlockSpec((None, 1, tk), lambda h, j, i: (h, 0, j))
    return pl.pallas_call(
        body, name=name, grid=(nh, nk, nq),
        in_specs=[qs, ks, ks, qs, qs, rowv, rowv, colv,
                  pl.BlockSpec((None, tq, tk), lambda h, j, i: (jnp.maximum(i - j, 0), 0, 0))],
        out_specs=[pl.BlockSpec((s, HEAD_DIM), lambda h, j, i: (0, h)), ks, ks, colv],
        out_shape=[jax.ShapeDtypeStruct((s, w), F32), jax.ShapeDtypeStruct((s, w), F32),
                   jax.ShapeDtypeStruct((s, w), F32), jax.ShapeDtypeStruct((nh, 1, s), F32)],
        scratch_shapes=[pltpu.VMEM((tk, HEAD_DIM), F32), pltpu.VMEM((tk, HEAD_DIM), F32), pltpu.VMEM((1, tk), F32)],
        compiler_params=_params("arbitrary", "arbitrary", "arbitrary"),
    )(q, k, v, o, do, lse, arow, acol, _distance_bias(s, tq, dilated))


def _gate_fwd(ga, gb, pa, pb, tm=256):
    s, d = ga.shape

    def body(ga_ref, gb_ref, pa_ref, pb_ref, o_ref):
        o_ref[...] = (jax.nn.sigmoid(ga_ref[...]) * pa_ref[...]
                      + jax.nn.sigmoid(gb_ref[...]) * pb_ref[...]).astype(o_ref.dtype)

    row = pl.BlockSpec((tm, d), lambda i: (i, 0))
    return pl.pallas_call(
        body, name="gate_fwd", grid=(s // tm,), in_specs=[row] * 4, out_specs=row,
        out_shape=jax.ShapeDtypeStruct((s, d), BF16), compiler_params=_params("parallel"),
    )(ga, gb, pa, pb)


def _gate_bwd(dm, ga, gb, pa, pb, tm=256):
    s, d = ga.shape

    def body(dm_ref, ga_ref, gb_ref, pa_ref, pb_ref, dpa_ref, dpb_ref, dga_ref, dgb_ref):
        dmv = dm_ref[...]
        for g_ref, p_ref, dp_ref, dg_ref in ((ga_ref, pa_ref, dpa_ref, dga_ref), (gb_ref, pb_ref, dpb_ref, dgb_ref)):
            sg = jax.nn.sigmoid(g_ref[...])
            dp_ref[...] = (dmv * sg).astype(BF16)
            dg_ref[...] = (dmv * p_ref[...] * (sg * (1.0 - sg))).astype(BF16)

    row = pl.BlockSpec((tm, d), lambda i: (i, 0))
    return pl.pallas_call(
        body, name="gate_bwd", grid=(s // tm,), in_specs=[row] * 5, out_specs=[row] * 4,
        out_shape=[jax.ShapeDtypeStruct((s, d), BF16)] * 4, compiler_params=_params("parallel"),
    )(dm, ga, gb, pa, pb)


def _shift_down(u, k):
    row = lax.broadcasted_iota(jnp.int32, u.shape, 0)
    return jnp.where(row >= k, pltpu.roll(u, k, 0), 0.0)


def _shift_up(u, k):
    n = u.shape[0]
    row = lax.broadcasted_iota(jnp.int32, u.shape, 0)
    return jnp.where(row < n - k, pltpu.roll(u, n - k, 0), 0.0)


def _conv3(u, wc, b):
    return wc[0:1, :] * _shift_down(u, 2) + wc[1:2, :] * _shift_down(u, 1) + wc[2:3, :] * u + b


def _conv_glu_fwd(u, wc, b, tn=256):
    s, f2 = u.shape
    f = f2 // 2
    nb = f // tn

    def body(ug_ref, uv_ref, wg_ref, wv_ref, bg_ref, bv_ref, o_ref):
        cg = _conv3(ug_ref[...], wg_ref[...], bg_ref[...])
        cv = _conv3(uv_ref[...], wv_ref[...], bv_ref[...])
        o_ref[...] = (cg * jax.nn.sigmoid(cg) * cv).astype(o_ref.dtype)

    def cols(rows, off):
        return pl.BlockSpec((rows, tn), lambda j: (0, j + off))

    return pl.pallas_call(
        body, name="conv_glu_fwd", grid=(nb,),
        in_specs=[cols(s, 0), cols(s, nb), cols(3, 0), cols(3, nb), cols(1, 0), cols(1, nb)],
        out_specs=cols(s, 0), out_shape=jax.ShapeDtypeStruct((s, f), BF16),
        compiler_params=_params("parallel"),
    )(u, u, wc, wc, b, b)


def _conv_glu_bwd(u, da, wc, b, tn=256):
    s, f2 = u.shape
    f = f2 // 2
    nb = f // tn

    def body(ug_ref, uv_ref, da_ref, wg_ref, wv_ref, bg_ref, bv_ref, dug_ref, duv_ref, sg_ref, sv_ref):
        ug, uv, wg, wv = ug_ref[...], uv_ref[...], wg_ref[...], wv_ref[...]
        cg = _conv3(ug, wg, bg_ref[...])
        cv = _conv3(uv, wv, bv_ref[...])
        sig = jax.nn.sigmoid(cg)
        dav = da_ref[...]
        dcv = dav * (cg * sig)
        dcg = dav * cv * (sig * (1.0 + cg * (1.0 - sig)))
        for dc, uu, w, du_ref, st_ref in ((dcg, ug, wg, dug_ref, sg_ref), (dcv, uv, wv, duv_ref, sv_ref)):
            du = w[2:3, :] * dc + w[1:2, :] * _shift_up(dc, 1) + w[0:1, :] * _shift_up(dc, 2)
            du_ref[...] = du.astype(BF16)
            st_ref[...] = jnp.zeros_like(st_ref)
            st_ref[0:1, :] = jnp.sum(dc * _shift_down(uu, 2), axis=0, keepdims=True)
            st_ref[1:2, :] = jnp.sum(dc * _shift_down(uu, 1), axis=0, keepdims=True)
            st_ref[2:3, :] = jnp.sum(dc * uu, axis=0, keepdims=True)
            st_ref[3:4, :] = jnp.sum(dc, axis=0, keepdims=True)

    def cols(rows, off):
        return pl.BlockSpec((rows, tn), lambda j: (0, j + off))

    return pl.pallas_call(
        body, name="conv_glu_bwd", grid=(nb,),
        in_specs=[cols(s, 0), cols(s, nb), cols(s, 0), cols(3, 0), cols(3, nb), cols(1, 0), cols(1, nb)],
        out_specs=[cols(s, 0), cols(s, 0), cols(8, 0), cols(8, 0)],
        out_shape=[jax.ShapeDtypeStruct((s, f), BF16), jax.ShapeDtypeStruct((s, f), BF16),
                   jax.ShapeDtypeStruct((8, f), F32), jax.ShapeDtypeStruct((8, f), F32)],
        compiler_params=_params("parallel"),
    )(u, u, da, wc, wc, b, b)


def _loss_head(y, target, tm=256):
    s, d = y.shape

    def body(y_ref, t_ref, dyf_ref, dyb_ref, l_ref):
        @pl.when(pl.program_id(0) == 0)
        def _():
            l_ref[...] = jnp.zeros_like(l_ref)

        err = y_ref[...] - t_ref[...]
        dy = err * (1.0 / d)
        dyf_ref[...] = dy
        dyb_ref[...] = dy.astype(BF16)
        l_ref[...] += 0.5 * jnp.sum(jnp.sum(err * err, axis=-1, keepdims=True) * (1.0 / d), axis=0, keepdims=True)

    row = pl.BlockSpec((tm, d), lambda i: (i, 0))
    return pl.pallas_call(
        body, name="loss_head", grid=(s // tm,), in_specs=[row, row],
        out_specs=[row, row, pl.BlockSpec((8, LANES), lambda i: (0, 0))],
        out_shape=[jax.ShapeDtypeStruct((s, d), F32), jax.ShapeDtypeStruct((s, d), BF16),
                   jax.ShapeDtypeStruct((8, LANES), F32)],
        compiler_params=_params("arbitrary"),
    )(y, target)


ROW_TILES = (256, 128, 64, 32, 16, 8)
BLOCK_BYTES = 2 << 20


def _add_halves(g, r1, place):
    ns, r, c = g.shape
    rh = r // 2
    tr = _pick(rh, ROW_TILES)
    g4 = g.reshape(ns, 2, rh, c)

    def body(p_ref, g_ref, r_ref, o_ref):
        o_ref[...] = (g_ref[...].astype(F32) + r_ref[...].astype(F32)).astype(o_ref.dtype)

    def slab(s, pr):
        return s + (s >= pr[0]).astype(jnp.int32)

    return pl.pallas_call(
        body, name="add_halves",
        grid_spec=pltpu.PrefetchScalarGridSpec(
            num_scalar_prefetch=1, grid=(ns - 1, rh // tr),
            in_specs=[pl.BlockSpec((None, None, tr, c), lambda s, i, pr: (slab(s, pr), pr[1], i, 0)),
                      pl.BlockSpec((None, tr, c), lambda s, i, pr: (slab(s, pr), i, 0))],
            out_specs=pl.BlockSpec((None, tr, c), lambda s, i, pr: (slab(s, pr), i, 0))),
        out_shape=jax.ShapeDtypeStruct((ns, rh, c), BF16),
        compiler_params=_params("parallel", "parallel"),
    )(place, g4, r1)


def _sum_chips(g, r1, recv, place):
    ns, r, c = g.shape
    rh = r // 2
    tr = _pick(rh, ROW_TILES)
    g4 = g.reshape(ns, 2, rh, c)

    def body(p_ref, g_ref, r_ref, t0_ref, t1_ref, t2_ref, o_ref):
        own = g_ref[...].astype(F32) + r_ref[...].astype(F32)
        o_ref[...] = ((own + t0_ref[...].astype(F32)) + t1_ref[...].astype(F32)) + t2_ref[...].astype(F32)

    def peer(k):
        return pl.BlockSpec((None, tr, c), lambda i, pr: (k, i, 0))

    return pl.pallas_call(
        body, name="sum_chips",
        grid_spec=pltpu.PrefetchScalarGridSpec(
            num_scalar_prefetch=1, grid=(rh // tr,),
            in_specs=[pl.BlockSpec((None, None, tr, c), lambda i, pr: (pr[0], pr[1], i, 0)),
                      pl.BlockSpec((None, tr, c), lambda i, pr: (pr[0], i, 0)), peer(0), peer(1), peer(2)],
            out_specs=pl.BlockSpec((tr, c), lambda i, pr: (pr[1] * (rh // tr) + i, 0))),
        out_shape=jax.ShapeDtypeStruct((r, c), F32),
        compiler_params=_params("parallel"),
    )(place, g4, r1, recv, recv, recv)


def _sum_devices(packs):
    n, r, c = packs.shape

    def body(p_ref, o_ref):
        acc = p_ref[0]
        for d in range(1, n):
            acc = acc + p_ref[d]
        o_ref[...] = acc

    return pl.pallas_call(
        body, name="sum_devices", out_shape=jax.ShapeDtypeStruct((r, c), F32), compiler_params=_params(),
    )(packs)


def _adamw_update(wv, gv, mv, vv):
    c1 = 1.0 - ADAM_B1 ** ADAM_STEP
    c2 = 1.0 - ADAM_B2 ** ADAM_STEP
    mn = ADAM_B1 * mv + (1.0 - ADAM_B1) * gv
    vn = ADAM_B2 * vv + (1.0 - ADAM_B2) * (gv * gv)
    m_hat = mn / c1
    v_hat = vn / c2
    return -ADAM_LR * (m_hat / (jnp.sqrt(v_hat) + ADAM_EPS) + ADAM_WD * wv), mn, vn


def _adamw(w, g, m, v, name, deps=()):
    r, c = w.shape
    tr = _pick(r, [t for t in ROW_TILES if t * c * 4 <= BLOCK_BYTES]) if r >= 8 else r

    def body(w_ref, g_ref, m_ref, v_ref, *rest):
        d_ref, mo_ref, vo_ref = rest[-3:]
        d_ref[...], mo_ref[...], vo_ref[...] = _adamw_update(w_ref[...], g_ref[...], m_ref[...], v_ref[...])

    blk = pl.BlockSpec((tr, c), lambda i: (i, 0))
    return pl.pallas_call(
        body, name=name, grid=(r // tr,), in_specs=[blk] * 4 + [ANY] * len(deps), out_specs=[blk] * 3,
        out_shape=[jax.ShapeDtypeStruct((r, c), F32)] * 3, compiler_params=_params("parallel"),
    )(w, g, m, v, *deps)


ANY = pl.BlockSpec(memory_space=pl.ANY)


def _place():
    x, y, c = lax.axis_index("x"), lax.axis_index("y"), lax.axis_index("c")
    chips = [(1 - x, y), (x, 1 - y), (1 - x, 1 - y)]
    return x, y, c, chips


def _remote(src, dst, send_sem, recv_sem, to):
    return pltpu.make_async_remote_copy(src_ref=src, dst_ref=dst, send_sem=send_sem, recv_sem=recv_sem,
                                        device_id=to, device_id_type=MESH)


HBM = pl.BlockSpec(memory_space=pltpu.HBM)
SEM = pl.BlockSpec(memory_space=pltpu.SEMAPHORE)
EFFECT = pltpu.SideEffectType.DATAFLOW_SIDE_EFFECTING


def _in_hbm(a):
    return pltpu.with_memory_space_constraint(a, pltpu.HBM)


def _half(ref_rows, who):
    return pl.ds(who * (ref_rows // 2), ref_rows // 2)


def _gather_start(groups):
    items = [it for g in groups for it in g]
    n = len(items)
    sizes = [len(g) for g in groups]

    def body(*refs):
        srcs, lands = refs[:n], refs[n:2 * n]
        sems = refs[2 * n:2 * n + 2 * len(groups)]
        token = refs[-1]
        x, y, c, chips = _place()
        j = 2 * x + y
        at = 0
        for gi, g in enumerate(groups):
            send, recv = sems[2 * gi], sems[2 * gi + 1]
            for i, (shard, split) in enumerate(g):
                src, land = srcs[at], lands[at]
                at += 1
                rows = _half(shard.shape[0], c) if split else slice(None)
                for k, chip in enumerate(chips):
                    _remote(src.at[rows], land.at[j, rows], send.at[4 * i + k], recv.at[4 * i + k], (*chip, c)).start()
                _remote(src, land.at[j], send.at[4 * i + 3], recv.at[4 * i + 3], (x, y, 1 - c)).start()
        token[...] = jnp.zeros_like(token)

    sem_shapes = []
    for sz in sizes:
        sem_shapes += [pltpu.SemaphoreType.DMA((4 * sz,)), pltpu.SemaphoreType.DMA((4 * sz,))]
    out_shape = (sem_shapes + [pltpu.HBM(sh.shape, sh.dtype) for sh, _ in items]
                 + [pltpu.HBM((N_CHIPS,) + sh.shape, sh.dtype) for sh, _ in items]
                 + [jax.ShapeDtypeStruct((8, LANES), F32)])
    ns = len(sem_shapes)
    outs = pl.pallas_call(
        body, name="gather_start", in_specs=[HBM] * (2 * n),
        out_specs=[SEM] * ns + [HBM] * (2 * n) + [pl.BlockSpec(memory_space=pltpu.VMEM)],
        out_shape=out_shape, input_output_aliases={i: ns + i for i in range(2 * n)},
        compiler_params=pltpu.CompilerParams(has_side_effects=EFFECT),
    )(*[_in_hbm(sh) for sh, _ in items], *[_in_hbm(lax.empty((N_CHIPS,) + sh.shape, sh.dtype)) for sh, _ in items])
    sems, shards, lands, token = outs[:ns], outs[ns:ns + n], outs[ns + n:ns + 2 * n], outs[-1]
    res, at = [], 0
    for gi, sz in enumerate(sizes):
        res.append((shards[at:at + sz], lands[at:at + sz], sems[2 * gi], sems[2 * gi + 1]))
        at += sz
    return res, token


def _gather_pass(group, started, after, name):
    shards, lands, send, recv = started
    n = len(group)
    split_ix = [i for i, (_, split) in enumerate(group) if split]

    def body(*refs):
        lnds, send1, recv1 = refs[n:2 * n], refs[2 * n], refs[2 * n + 1]
        outs = refs[2 * n + 2 + len(after):]
        send2, recv2, token = outs[2 * n], outs[2 * n + 1], outs[2 * n + 2]
        x, y, c, chips = _place()
        sib = (x, y, 1 - c)
        for i, (shard, split) in enumerate(group):
            rows = _half(shard.shape[0], c) if split else slice(None)
            for k, (cx, cy) in enumerate(chips):
                landed = lnds[i].at[2 * cx + cy, rows]
                cp = _remote(landed, landed, send1.at[4 * i + k], recv1.at[4 * i + k], sib)
                cp.wait_send()
                cp.wait_recv()
            own = lnds[i].at[2 * x + y]
            cp = _remote(own, own, send1.at[4 * i + 3], recv1.at[4 * i + 3], sib)
            cp.wait_send()
            cp.wait_recv()
        for i2, i in enumerate(split_ix):
            rows = _half(group[i][0].shape[0], c)
            for k, (cx, cy) in enumerate(chips):
                landed = lnds[i].at[2 * cx + cy, rows]
                _remote(landed, landed, send2.at[3 * i2 + k], recv2.at[3 * i2 + k], sib).start()
        token[...] = jnp.zeros_like(token)

    n2 = len(split_ix)
    out_shape = ([pltpu.HBM(a.shape, a.dtype) for a in (*shards, *lands)]
                 + [pltpu.SemaphoreType.DMA((3 * n2,)), pltpu.SemaphoreType.DMA((3 * n2,)), jax.ShapeDtypeStruct((8, LANES), F32)])
    outs = pl.pallas_call(
        body, name=name, in_specs=[HBM] * (2 * n) + [SEM, SEM] + [ANY] * len(after),
        out_specs=[HBM] * (2 * n) + [SEM, SEM, pl.BlockSpec(memory_space=pltpu.VMEM)],
        out_shape=out_shape, input_output_aliases={i: i for i in range(2 * n)},
        compiler_params=pltpu.CompilerParams(has_side_effects=EFFECT),
    )(*shards, *lands, send, recv, *after)
    return outs[:n], (outs[n:2 * n], outs[2 * n], outs[2 * n + 1]), outs[2 * n + 2]


def _gather_wait(group, passed, after, name):
    lands, send2, recv2 = passed
    n = len(group)
    split_ix = [i for i, (_, split) in enumerate(group) if split]

    def body(*refs):
        lnds, s2, r2 = refs[:n], refs[n], refs[n + 1]
        x, y, c, chips = _place()
        sib = (x, y, 1 - c)
        for i2, i in enumerate(split_ix):
            rows = _half(group[i][0].shape[0], 1 - c)
            for k, (cx, cy) in enumerate(chips):
                landed = lnds[i].at[2 * cx + cy, rows]
                cp = _remote(landed, landed, s2.at[3 * i2 + k], r2.at[3 * i2 + k], sib)
                cp.wait_send()
                cp.wait_recv()

    return pl.pallas_call(
        body, name=name, in_specs=[HBM] * n + [SEM, SEM, ANY], out_specs=[HBM] * n,
        out_shape=[pltpu.HBM(a.shape, a.dtype) for a in lands], input_output_aliases={i: i for i in range(n)},
        compiler_params=pltpu.CompilerParams(has_side_effects=EFFECT),
    )(*lands, send2, recv2, after)


def _xfer_start(name, srcs, land_shapes, n_copies, copies, after):
    n, nl = len(srcs), len(land_shapes)

    def body(*refs):
        src_refs, land_refs = refs[:n], refs[n:n + nl]
        send, recv, token = refs[n + nl + 1], refs[n + nl + 2], refs[-1]
        for cp in copies(src_refs, land_refs, send, recv):
            cp.start()
        token[...] = jnp.zeros_like(token)

    lands = [_in_hbm(lax.empty(shape, dtype)) for shape, dtype in land_shapes]
    out_shape = ([pltpu.SemaphoreType.DMA((n_copies,)), pltpu.SemaphoreType.DMA((n_copies,))]
                 + [pltpu.HBM(a.shape, a.dtype) for a in (*srcs, *lands)] + [jax.ShapeDtypeStruct((8, LANES), F32)])
    outs = pl.pallas_call(
        body, name=name, in_specs=[HBM] * (n + nl) + [ANY],
        out_specs=[SEM, SEM] + [HBM] * (n + nl) + [pl.BlockSpec(memory_space=pltpu.VMEM)],
        out_shape=out_shape, input_output_aliases={i: 2 + i for i in range(n + nl)},
        compiler_params=pltpu.CompilerParams(has_side_effects=EFFECT),
    )(*[_in_hbm(a) for a in srcs], *lands, after)
    return (outs[2:2 + n], outs[2 + n:2 + n + nl], outs[0], outs[1]), outs[-1]


def _xfer_wait(name, started, copies, after):
    srcs, lands, send, recv = started
    n, nl = len(srcs), len(lands)

    def body(*refs):
        src_refs, land_refs, s_ref, r_ref = refs[:n], refs[n:n + nl], refs[n + nl], refs[n + nl + 1]
        for cp in copies(src_refs, land_refs, s_ref, r_ref):
            cp.wait_send()
            cp.wait_recv()

    outs = pl.pallas_call(
        body, name=name, in_specs=[HBM] * (n + nl) + [SEM, SEM, ANY], out_specs=[HBM] * (n + nl),
        out_shape=[pltpu.HBM(a.shape, a.dtype) for a in (*srcs, *lands)],
        input_output_aliases={i: i for i in range(n + nl)},
        compiler_params=pltpu.CompilerParams(has_side_effects=EFFECT),
    )(*srcs, *lands, send, recv, after)
    return outs[:n], outs[n:]


def _swap_copies(srcs, lands, send, recv):
    x, y, c, _ = _place()
    return [_remote(src.at[:, _half(src.shape[1], 1 - c)], land, send.at[i], recv.at[i], (x, y, 1 - c))
            for i, (src, land) in enumerate(zip(srcs, lands))]


def _scatter_copies(srcs, lands, send, recv):
    x, y, c, chips = _place()
    return [_remote(src.at[2 * cx + cy], land.at[k], send.at[3 * i + k], recv.at[3 * i + k], (cx, cy, c))
            for i, (src, land) in enumerate(zip(srcs, lands)) for k, (cx, cy) in enumerate(chips)]


def _join_copies(srcs, lands, send, recv):
    x, y, c, _ = _place()
    return [_remote(src.at[_half(src.shape[0], c)], src.at[_half(src.shape[0], c)], send.at[i], recv.at[i], (x, y, 1 - c))
            for i, src in enumerate(srcs)]


def _corner(a):
    return a[(slice(0, 1),) * a.ndim]


class _Reducer:
    def __init__(self, place):
        self.place = place
        self.state = {}

    def swap(self, key, grads, after):
        shapes = [((g.shape[0], g.shape[1] // 2, g.shape[2]), g.dtype) for g in grads]
        self.state[key], token = _xfer_start("swap_start_" + key, grads, shapes, len(grads), _swap_copies, _corner(after))
        return token

    def to_chips(self, key, after):
        grads, from_sibling = _xfer_wait("swap_wait_" + key, self.state[key], _swap_copies, after)
        sums = [_add_halves(g, r, self.place) for g, r in zip(grads, from_sibling)]
        shapes = [((3,) + s.shape[1:], s.dtype) for s in sums]
        started, token = _xfer_start("scatter_start_" + key, sums, shapes, 3 * len(sums), _scatter_copies, _corner(sums[-1]))
        self.state[key] = (grads, from_sibling, started)
        return token

    def to_core(self, key, after):
        grads, from_sibling, started = self.state[key]
        _, from_chips = _xfer_wait("scatter_wait_" + key, started, _scatter_copies, after)
        shards = [_sum_chips(g, r, rc, self.place) for g, r, rc in zip(grads, from_sibling, from_chips)]
        self.state[key], token = _xfer_start("join_start_" + key, shards, [], len(shards), _join_copies, _corner(shards[-1]))
        return token

    def finish(self, key, after):
        return _xfer_wait("join_wait_" + key, self.state.pop(key), _join_copies, after)[0]


def _gather_packs(pack, deps=()):
    def body(p_ref, *rest):
        o_ref, lsem, ssem, rsem = rest[-4:]
        x, y, c, _ = _place()
        me = 4 * x + 2 * y + c
        local = pltpu.make_async_copy(p_ref, o_ref.at[me], lsem)
        local.start()
        cps = []
        for k in range(1, N_DEV):
            fx, fy, fc = (k >> 2) & 1, (k >> 1) & 1, k & 1
            to = (x ^ fx, y ^ fy, c ^ fc)
            cps.append(_remote(p_ref, o_ref.at[me], ssem.at[k - 1], rsem.at[k - 1], to))
        for cp in cps:
            cp.start()
        for k in range(1, N_DEV):
            fx, fy, fc = (k >> 2) & 1, (k >> 1) & 1, k & 1
            src = o_ref.at[4 * (x ^ fx) + 2 * (y ^ fy) + (c ^ fc)]
            _remote(src, src, ssem.at[k - 1], rsem.at[k - 1], (x, y, c)).wait_recv()
        for cp in cps:
            cp.wait_send()
        local.wait()

    return pl.pallas_call(
        body, name="gather_packs", in_specs=[ANY] * (1 + len(deps)), out_specs=ANY,
        out_shape=jax.ShapeDtypeStruct((N_DEV,) + pack.shape, pack.dtype),
        scratch_shapes=[pltpu.SemaphoreType.DMA, pltpu.SemaphoreType.DMA((N_DEV - 1,)), pltpu.SemaphoreType.DMA((N_DEV - 1,))],
    )(pack, *deps)


LANE_TILES = (512, 896, 1408, 704, 384, 256, 128)


def _layer_grads(x, target, small, wg, rest_pass, rest_wait, red, filler):
    s, d = x.shape
    f = wg["conv"].shape[1] // 2
    w_att = N_HEADS * HEAD_DIM
    in_splits = (w_att, w_att, w_att, N_HEADS, w_att, w_att, w_att, d, d)
    in_cols = sum(in_splits)
    cs = in_cols // N_CHIPS
    cp = wg["in"].shape[2]
    tm = min(s, 1024)
    t_in = cp
    t_d = _pick(d, LANE_TILES)
    t_d2 = min(d, 1024)
    t_dq = _pick(d // N_CHIPS, LANE_TILES)
    t_w = _pick(w_att, LANE_TILES)
    t_up = 2 * f // N_CHIPS
    tm_wide = min(s, 512)
    t_fq = _pick(f // N_CHIPS, LANE_TILES)
    offs = np.cumsum(in_splits)[:-1].tolist()

    h1 = _norm_fwd(x, small["g_attn"], group=d, name="rms1_fwd")
    proj_p = _mm(h1, wg["in"], mode="nn", b_kind="col", tm=tm_wide, tn=t_in, tk=d, name="mm_in")
    gains = {n: small[n].reshape(1, w_att) for n in ("g_q_fox", "g_k_fox", "g_q_dil", "g_k_dil")}
    qa, ka, va_b, fa, qb, kb, vb_b, ga, gb, qa_n, ka_n, qb_n, kb_n = _proj_split(
        proj_p, in_splits, cs, (F32, F32, BF16, F32, F32, F32, BF16, F32, F32),
        {0: gains["g_q_fox"], 1: gains["g_k_fox"], 4: gains["g_q_dil"], 5: gains["g_k_dil"]})
    fa_t = fa.T
    b_f = small["b_forget"].reshape(N_HEADS, 1)
    c_f = _forget_fwd(fa_t, b_f)
    slopes = jnp.asarray(2.0 ** (-8.0 * np.arange(1, N_HEADS + 1) / N_HEADS), dtype=F32)
    a_d = -(slopes[:, None] * jnp.arange(s, dtype=F32)[None, :])
    rows_f, cols_f = c_f[:, :, None], c_f[:, None, :]
    rows_d, cols_d = a_d[:, :, None], a_d[:, None, :]
    o_a, o_a32, lse_a = _attn_fwd(qa_n, ka_n, va_b, rows_f, cols_f, dilated=False, name="attn_fox_fwd")
    token = rest_pass("mid", o_a)
    rows_d = rows_d + token[0, 0]
    o_b, o_b32, lse_b = _attn_fwd(qb_n, kb_n, vb_b, rows_d, cols_d, dilated=True, name="attn_dil_fwd")
    wg = dict(wg, **rest_wait("mid", o_b))
    token = rest_pass("late", o_b)
    pa = _mm(o_a, wg["brf"], mode="nn", b_kind="col", tm=tm, tn=t_dq, tk=w_att, name="mm_brf", deps=(token,))
    pb = _mm(o_b, wg["brd"], mode="nn", b_kind="col", tm=tm, tn=t_dq, tk=w_att, name="mm_brd")
    merged = _gate_fwd(ga, gb, pa, pb)
    x1 = _mm(merged, wg["out"], mode="nn", b_kind="row", res=x, tm=tm, tn=t_d, tk=t_dq, name="mm_out")
    wg = dict(wg, **rest_wait("late", x1))
    h2 = _norm_fwd(x1, small["g_ffn"], group=d, name="rms2_fwd")
    u = _mm(h2, wg["up"], mode="nn", b_kind="col", tm=tm_wide, tn=t_up, tk=d, name="mm_up")
    act = _conv_glu_fwd(u, wg["conv"], wg["bconv"])
    y = _mm(act, wg["down"], mode="nn", b_kind="row", res=x1, tm=tm, tn=t_d2, tk=t_fq, name="mm_down")
    dy_f, dy_b, loss_blk = _loss_head(y, target)

    d_act = _mm(dy_b, wg["down"], mode="nt", b_kind="row", tm=tm, tn=t_fq, tk=d, name="mm_down_dx")
    g_down = _mm(act, dy_b, mode="tn", out_dtype=BF16, out_kind="row", tm=t_fq, tn=t_d2, tk=s, name="mm_down_dw")
    tok = red.swap("down", [g_down], g_down)
    du_g, du_v, st_g, st_v = _conv_glu_bwd(u, d_act, wg["conv"] + tok[0, 0], wg["bconv"])
    tok = red.to_chips("down", du_g)
    du = jnp.concatenate([du_g, du_v], axis=1)
    g_up = _mm(h2, du, mode="tn", out_dtype=BF16, out_kind="col", tm=t_d2, tn=t_up, tk=s, name="mm_up_dw", deps=(tok,))
    tok = red.to_core("down", g_up)
    tok2 = red.swap("up", [g_up], g_up)
    dh2 = _mm(du, wg["up"], mode="nt", b_kind="col", tm=tm, tn=t_d2, tk=t_up, name="mm_up_dx", deps=(tok, tok2))
    tok = red.to_chips("up", dh2)
    dx1_b, dx1_f, dg_ffn = _norm_bwd(dh2, x1, small["g_ffn"], group=d, res=dy_f, out_dtypes=(BF16, F32), name="rms2_bwd")
    d_merged = _mm(dx1_b, wg["out"], mode="nt", b_kind="row", tm=tm, tn=t_dq, tk=d, name="mm_out_dx", deps=(tok,))
    g_out = _mm(merged, dx1_b, mode="tn", out_dtype=BF16, out_kind="row", tm=t_dq, tn=t_d2, tk=s, name="mm_out_dw")
    dpa, dpb, dga, dgb = _gate_bwd(d_merged, ga, gb, pa, pb)
    do_a = _mm(dpa, wg["brf"], mode="nt", b_kind="col", out_dtype=BF16, tm=s, tn=w_att, tk=t_dq, name="mm_brf_dx")
    do_b = _mm(dpb, wg["brd"], mode="nt", b_kind="col", out_dtype=BF16, tm=s, tn=w_att, tk=t_dq, name="mm_brd_dx")
    g_brf = _mm(o_a, dpa, mode="tn", out_dtype=BF16, out_kind="col", tm=w_att, tn=t_dq, tk=s, name="mm_brf_dw")
    g_brd = _mm(o_b, dpb, mode="tn", out_dtype=BF16, out_kind="col", tm=w_att, tn=t_dq, tk=s, name="mm_brd_dw")
    tok = red.swap("mix", [g_out, g_brf, g_brd], g_brd)
    dqa_n, dka_n, dva, dac_a = _attn_bwd(qa_n, ka_n, va_b, o_a32, do_a, lse_a, rows_f + tok[0, 0], cols_f, dilated=False, name="attn_fox_bwd")
    tok = red.to_core("up", dqa_n)
    tok2 = red.to_chips("mix", dqa_n)
    dqb_n, dkb_n, dvb, _ = _attn_bwd(qb_n, kb_n, vb_b, o_b32, do_b, lse_b, rows_d + (tok[0, 0] + tok2[0, 0]), cols_d, dilated=True, name="attn_dil_bwd")
    tok = red.to_core("mix", dqb_n)
    dfa_t, db_f = _forget_bwd(dac_a[:, 0, :], fa_t, b_f)
    dproj_p, dgains = _dproj_merge(
        [dqa_n, dka_n, dva, dfa_t.T, dqb_n, dkb_n, dvb, dga, dgb], in_splits, cs, cp,
        {0: (qa, gains["g_q_fox"]), 1: (ka, gains["g_k_fox"]), 4: (qb, gains["g_q_dil"]), 5: (kb, gains["g_k_dil"])})
    dg_qf, dg_kf, dg_qd, dg_kd = dgains[0], dgains[1], dgains[4], dgains[5]
    g_in = _mm(h1, dproj_p, mode="tn", out_dtype=BF16, out_kind="col", tm=t_d2, tn=t_in, tk=s, name="mm_in_dw", deps=(tok,))
    tok = red.swap("in", [g_in], g_in)
    tok = red.to_chips("in", filler(tok))
    dh1 = _mm(dproj_p, wg["in"], mode="nt", b_kind="col", tm=tm, tn=t_d2, tk=t_in, name="mm_in_dx", deps=(tok,))
    grad_x, dg_attn = _norm_bwd(dh1, x, small["g_attn"], group=d, res=dx1_f, out_dtypes=(F32,), name="rms1_bwd")

    small_grads = {
        "g_attn": dg_attn, "b_forget": db_f.reshape(1, N_HEADS),
        "g_q_fox": dg_qf, "g_k_fox": dg_kf, "g_q_dil": dg_qd, "g_k_dil": dg_kd, "g_ffn": dg_ffn,
        "w_conv": jnp.concatenate([st_g[0:3], st_v[0:3]], axis=1),
        "b_conv": jnp.concatenate([st_g[3:4], st_v[3:4]], axis=1),
        "loss": loss_blk[0:1, 0:1],
    }
    return small_grads, grad_x


SMALL_ORDER = ("g_attn", "b_forget", "g_q_fox", "g_k_fox", "g_q_dil", "g_k_dil", "g_ffn", "w_conv", "b_conv", "loss")
WEIGHT_ORDER = ("g_attn", "w_in", "b_forget", "g_q_fox", "g_k_fox", "g_q_dil", "g_k_dil", "w_br_fox", "w_br_dil",
                "w_out", "g_ffn", "w_up", "w_conv", "b_conv", "w_down")
BIG = {"w_in": "in", "w_br_fox": "brf", "w_br_dil": "brd", "w_out": "out", "w_up": "up", "w_down": "down"}


def kernel(x, g_attn, w_in, b_forget, g_q_fox, g_k_fox, g_q_dil, g_k_dil, w_br_fox, w_br_dil, w_out, g_ffn, w_up, w_conv, b_conv, w_down, loss_target, m_g_attn, m_w_in, m_b_forget, m_g_q_fox, m_g_k_fox, m_g_q_dil, m_g_k_dil, m_w_br_fox, m_w_br_dil, m_w_out, m_g_ffn, m_w_up, m_w_conv, m_b_conv, m_w_down, v_g_attn, v_w_in, v_b_forget, v_g_q_fox, v_g_k_fox, v_g_q_dil, v_g_k_dil, v_w_br_fox, v_w_br_dil, v_w_out, v_g_ffn, v_w_up, v_w_conv, v_b_conv, v_w_down):
    w = dict(g_attn=g_attn, w_in=w_in, b_forget=b_forget, g_q_fox=g_q_fox, g_k_fox=g_k_fox, g_q_dil=g_q_dil,
             g_k_dil=g_k_dil, w_br_fox=w_br_fox, w_br_dil=w_br_dil, w_out=w_out, g_ffn=g_ffn, w_up=w_up,
             w_conv=w_conv, b_conv=b_conv, w_down=w_down)
    m = dict(g_attn=m_g_attn, w_in=m_w_in, b_forget=m_b_forget, g_q_fox=m_g_q_fox, g_k_fox=m_g_k_fox,
             g_q_dil=m_g_q_dil, g_k_dil=m_g_k_dil, w_br_fox=m_w_br_fox, w_br_dil=m_w_br_dil, w_out=m_w_out,
             g_ffn=m_g_ffn, w_up=m_w_up, w_conv=m_w_conv, b_conv=m_b_conv, w_down=m_w_down)
    v = dict(g_attn=v_g_attn, w_in=v_w_in, b_forget=v_b_forget, g_q_fox=v_g_q_fox, g_k_fox=v_g_k_fox,
             g_q_dil=v_g_q_dil, g_k_dil=v_g_k_dil, w_br_fox=v_w_br_fox, w_br_dil=v_w_br_dil, w_out=v_w_out,
             g_ffn=v_g_ffn, w_up=v_w_up, w_conv=v_w_conv, b_conv=v_b_conv, w_down=v_w_down)
    xi, yi, ci = lax.axis_index("x"), lax.axis_index("y"), lax.axis_index("c")
    chip = (2 * xi + yi).astype(jnp.int32)

    cs = w_in.shape[2]
    cp = _round_up(cs, LANES)
    shards = {
        "in": jnp.pad(w_in[0].astype(BF16), ((0, 0), (0, cp - cs))),
        "brf": w_br_fox[0].astype(BF16), "brd": w_br_dil[0].astype(BF16), "out": w_out[0].astype(BF16),
        "up": w_up[0].astype(BF16), "down": w_down[0].astype(BF16),
    }
    names = tuple(shards)
    conv_pad = jnp.pad(w_conv[0], ((0, 8 - w_conv.shape[1]), (0, 0)))
    first = [(shards["in"], True), (conv_pad, False)]
    later = {"mid": ("brf", "brd", "out"), "late": ("up", "down")}
    groups = {key: [(shards[n], True) for n in members] for key, members in later.items()}
    (started_first, *started_later), token = _gather_start([first, *groups.values()])
    started = dict(zip(later, started_later))
    token, w["w_in"], m["w_in"], v["w_in"] = lax.optimization_barrier((token, w["w_in"], m["w_in"], v["w_in"]))
    w2, m2, v2 = ({n: a[n].reshape(a[n].shape[-2], a[n].shape[-1]) for n in BIG} for a in (w, m, v))
    early = (token, w2["w_in"], m2["w_in"], v2["w_in"])
    own_first, passed_first, token = _gather_pass(first, started_first, early, "gather_pass_in")
    land_in, land_conv = _gather_wait(first, passed_first, token, "gather_wait_in")
    wg = {"in": land_in, "bconv": b_conv,
          "conv": jnp.transpose(land_conv[:, :w_conv.shape[1], :], (1, 0, 2)).reshape(w_conv.shape[1], -1)}
    small = {n: w[n] for n in ("g_attn", "b_forget", "g_q_fox", "g_k_fox", "g_q_dil", "g_k_dil", "g_ffn")}
    small = {n: (a[0] if a.ndim == 3 else a) for n, a in small.items()}
    in_flight = {}

    def rest_pass(key, after):
        own, passed, tok = _gather_pass(groups[key], started[key], (after,), "gather_pass_" + key)
        in_flight[key] = (own, passed)
        return tok

    def rest_wait(key, after):
        own, passed = in_flight.pop(key)
        lands = _gather_wait(groups[key], passed, after, "gather_wait_" + key)
        return dict(zip(later[key], lands))

    reducer = _Reducer(jnp.stack([chip, ci.astype(jnp.int32)]))
    g_out, d_out, m_out, v_out = {}, {}, {}, {}
    reduced = {}

    def first_element(arrays):
        return jnp.stack([a[(0,) * a.ndim] for a in arrays])

    def update_big(n, deps):
        g2 = reduced[BIG[n]]
        dl, mn, vn = _adamw(w2[n], g2, m2[n], v2[n], name="adamw_" + n, deps=deps)
        g_out[n], d_out[n], m_out[n], v_out[n] = (a.reshape(w[n].shape) for a in (g2, dl, mn, vn))

    def update_down(tok):
        (reduced["down"],) = reducer.finish("down", tok)
        update_big("w_down", (tok,))
        return v_out["w_down"]

    small_grads, grad_x = _layer_grads(x[0], loss_target[0], small, wg, rest_pass, rest_wait, reducer, update_down)

    for key, members in (("up", ("up",)), ("mix", ("out", "brf", "brd"))):
        reduced.update(zip(members, reducer.finish(key, grad_x)))
    others = ("w_up", "w_out", "w_br_fox", "w_br_dil")
    for n in others:
        update_big(n, (grad_x,))

    flat = jnp.concatenate([small_grads[n].reshape(-1) for n in SMALL_ORDER])
    rows = _round_up(flat.shape[0], 8 * LANES) // LANES
    pack = jnp.pad(flat, (0, rows * LANES - flat.shape[0])).reshape(rows, LANES)
    packs = _gather_packs(pack, deps=(first_element([v_out[n] for n in others]),))
    total = _sum_devices(packs).reshape(-1)
    red, at = {}, 0
    for n in SMALL_ORDER:
        size = small_grads[n].size
        red[n] = total[at:at + size].reshape(small_grads[n].shape)
        at += size
    loss = red["loss"].reshape(())
    c2 = w_conv.shape[2]
    red["w_conv"] = lax.dynamic_slice_in_dim(red["w_conv"], chip * c2, c2, axis=1)

    smalls = [n for n in WEIGHT_ORDER if n not in BIG]
    for n in smalls:
        shape = w[n].shape
        r2 = (shape[-2], shape[-1]) if n not in ("g_attn", "b_forget", "g_ffn", "b_conv") else (1, shape[-1])
        g2 = red[n].reshape(r2)
        dl, mn, vn = _adamw(w[n].reshape(r2), g2, m[n].reshape(r2), v[n].reshape(r2), name="adamw_" + n)
        g_out[n], d_out[n], m_out[n], v_out[n] = (a.reshape(shape) for a in (g2, dl, mn, vn))
    tok = reducer.to_core("in", first_element([v_out[n] for n in smalls]))
    (padded,) = reducer.finish("in", tok)
    reduced["in"] = padded[:, :cs]
    update_big("w_in", (tok,))

    return (loss, grad_x[None], *[g_out[n] for n in WEIGHT_ORDER], *[d_out[n] for n in WEIGHT_ORDER],
            *[m_out[n] for n in WEIGHT_ORDER], *[v_out[n] for n in WEIGHT_ORDER])
```

```python
import functools
import math

import jax
import jax.numpy as jnp
import numpy as np
from jax import lax
from jax.experimental import pallas as pl
from jax.experimental.pallas import tpu as pltpu

F32 = jnp.float32
BF16 = jnp.bfloat16
HEAD_DIM = 128
N_HEADS = 8
EPS = 1e-6
NEG = -1e30
N_CHIPS = 4
N_DEV = 8
LANES = 128
VMEM_LIMIT_BYTES = 56 * 1024 * 1024
DIL_PATTERNS = ((128, 1), (512, 4), (2048, 16))
ATTN_TILE = 512
ADAM_LR, ADAM_B1, ADAM_B2, ADAM_EPS, ADAM_WD, ADAM_STEP = 0.001, 0.9, 0.999, 1e-08, 0.01, 10
MESH = pl.DeviceIdType.MESH


def _params(*sem):
    return pltpu.CompilerParams(dimension_semantics=sem, vmem_limit_bytes=VMEM_LIMIT_BYTES)


def _round_up(n, m):
    return -(-n // m) * m


def _pick(dim, prefs):
    for p in prefs:
        if dim % p == 0:
            return p
    raise ValueError(f"no tile for {dim} in {prefs}")


def _logical_shape(arr, kind):
    if kind is None:
        return arr.shape
    s, r, c = arr.shape
    return (r, s * c) if kind == "col" else (s * r, c)


def _spec(shape, kind, br, bc, fi, fj):
    if kind is None:
        return pl.BlockSpec((br, bc), lambda *g: (fi(*g), fj(*g)))
    _, r, c = shape
    if kind == "col":
        nb = c // bc
        assert nb * bc == c, (shape, bc)
        return pl.BlockSpec((None, br, bc), lambda *g: (fj(*g) // nb, fi(*g), fj(*g) % nb))
    nb = r // br
    assert nb * br == r, (shape, br)
    return pl.BlockSpec((None, br, bc), lambda *g: (fi(*g) // nb, fi(*g) % nb, fj(*g)))


def _mm(a, b, *, mode, tm, tn, tk, name, a_kind=None, b_kind=None, out_kind=None,
        out_dtype=F32, res=None, deps=()):
    la, lb = _logical_shape(a, a_kind), _logical_shape(b, b_kind)
    if mode == "nn":
        (m, k), (k2, n) = la, lb
    elif mode == "nt":
        (m, k), (n, k2) = la, lb
    else:
        (k, m), (k2, n) = la, lb
    assert k == k2, (name, la, lb)
    assert m % tm == 0 and n % tn == 0 and k % tk == 0, (name, m, n, k, tm, tn, tk)
    nk = k // tk
    im = lambda i, j, l: i
    jn = lambda i, j, l: j
    lk = lambda i, j, l: l
    if mode == "tn":
        a_spec = _spec(a.shape, a_kind, tk, tm, lk, im)
        dims = (((0,), (0,)), ((), ()))
    else:
        a_spec = _spec(a.shape, a_kind, tm, tk, im, lk)
        dims = (((1,), (1,)), ((), ())) if mode == "nt" else (((1,), (0,)), ((), ()))
    if mode == "nt":
        b_spec = _spec(b.shape, b_kind, tn, tk, jn, lk)
    else:
        b_spec = _spec(b.shape, b_kind, tk, tn, lk, jn)
    if out_kind is None:
        oshape = (m, n)
    elif out_kind == "col":
        oshape = (N_CHIPS, m, n // N_CHIPS)
    else:
        oshape = (N_CHIPS, m // N_CHIPS, n)
    o_spec = _spec(oshape, out_kind, tm, tn, im, jn)
    in_specs = [a_spec, b_spec]
    args = [a, b]
    if res is not None:
        in_specs.append(pl.BlockSpec((tm, tn), lambda i, j, l: (i, j)))
        args.append(res)
    in_specs += [pl.BlockSpec(memory_space=pl.ANY)] * len(deps)
    args += list(deps)

    def finish(out, res_ref, o_ref):
        if res_ref is not None:
            out = out + res_ref[...]
        o_ref[...] = out.astype(o_ref.dtype)

    def body_whole_k(*refs):
        res_ref = refs[2] if res is not None else None
        finish(lax.dot_general(refs[0][...], refs[1][...], dims, preferred_element_type=F32), res_ref, refs[-1])

    def body(*refs):
        a_ref, b_ref = refs[0], refs[1]
        res_ref = refs[2] if res is not None else None
        o_ref, acc_ref = refs[-2], refs[-1]
        step = pl.program_id(2)

        @pl.when(step == 0)
        def _():
            acc_ref[...] = jnp.zeros_like(acc_ref)

        acc_ref[...] += lax.dot_general(a_ref[...], b_ref[...], dims, preferred_element_type=F32)

        @pl.when(step == nk - 1)
        def _():
            finish(acc_ref[...], res_ref, o_ref)

    return pl.pallas_call(
        body_whole_k if nk == 1 else body, name=name, grid=(m // tm, n // tn, nk),
        in_specs=in_specs, out_specs=o_spec,
        out_shape=jax.ShapeDtypeStruct(oshape, out_dtype),
        scratch_shapes=[] if nk == 1 else [pltpu.VMEM((tm, tn), F32)],
        compiler_params=_params("parallel", "parallel", "arbitrary"),
    )(*args)


def _pieces(splits, cs, cp):
    out, g0 = [], 0
    for width in splits:
        g1, runs = g0 + width, []
        for j in range(N_CHIPS):
            a, b = max(g0, cs * j), min(g1, cs * (j + 1))
            if a < b:
                runs.append((j * cp + a - cs * j, a - g0, b - a))
        out.append(runs)
        g0 = g1
    return out


def _head_norm(xv, gv):
    r = lax.rsqrt(jnp.mean(xv * xv, axis=-1, keepdims=True) + EPS)
    return (xv * r) * gv


def _head_norm_bwd(dyv, xv, gv):
    r = lax.rsqrt(jnp.mean(xv * xv, axis=-1, keepdims=True) + EPS)
    xr = xv * r
    gdy = dyv * gv
    return r * (gdy - xr * jnp.mean(gdy * xr, axis=-1, keepdims=True)), jnp.sum(dyv * xr, axis=0, keepdims=True)


def _proj_split(proj_p, splits, cs, dtypes, gains, tm=128):
    s, wp = proj_p.shape
    pieces = _pieces(splits, cs, wp // N_CHIPS)
    normed = sorted(gains)
    nseg = len(splits)

    def body(p_ref, *refs):
        g_refs, o_refs, n_refs = refs[:len(normed)], refs[len(normed):len(normed) + nseg], refs[len(normed) + nseg:]
        for o_ref, runs in zip(o_refs, pieces):
            for src, dst, n in runs:
                o_ref[:, dst:dst + n] = p_ref[:, src:src + n].astype(o_ref.dtype)
        for g_ref, n_ref, i in zip(g_refs, n_refs, normed):
            for c0 in range(0, splits[i], HEAD_DIM):
                cols = slice(c0, c0 + HEAD_DIM)
                n_ref[:, cols] = _head_norm(o_refs[i][:, cols], g_ref[:, cols]).astype(n_ref.dtype)

    return pl.pallas_call(
        body, name="proj_split", grid=(s // tm,),
        in_specs=[pl.BlockSpec((tm, wp), lambda i: (i, 0))] + [pl.BlockSpec((1, splits[i]), lambda i: (0, 0)) for i in normed],
        out_specs=[pl.BlockSpec((tm, w), lambda i: (i, 0)) for w in splits]
        + [pl.BlockSpec((tm, splits[i]), lambda i: (i, 0)) for i in normed],
        out_shape=[jax.ShapeDtypeStruct((s, w), dt) for w, dt in zip(splits, dtypes)]
        + [jax.ShapeDtypeStruct((s, splits[i]), BF16) for i in normed],
        compiler_params=_params("parallel"),
    )(proj_p, *[gains[i] for i in normed])


def _dproj_merge(parts, splits, cs, cp, norms, tm=128):
    s = parts[0].shape[0]
    wp = N_CHIPS * cp
    pieces = _pieces(splits, cs, cp)
    normed = sorted(norms)
    nseg, nn = len(splits), len(normed)

    def body(*refs):
        p_refs, x_refs, g_refs = refs[:nseg], refs[nseg:nseg + nn], refs[nseg + nn:nseg + 2 * nn]
        o_ref, dg_refs = refs[nseg + 2 * nn], refs[nseg + 2 * nn + 1:nseg + 3 * nn + 1]
        stage, tmp = refs[-2], refs[-1]

        @pl.when(pl.program_id(0) == 0)
        def _():
            for dg_ref in dg_refs:
                dg_ref[...] = jnp.zeros_like(dg_ref)

        for j in range(N_CHIPS):
            stage[:, j * cp + cs:(j + 1) * cp] = jnp.zeros((tm, cp - cs), F32)
        for i, (p_ref, runs) in enumerate(zip(p_refs, pieces)):
            src_ref = p_ref
            if i in norms:
                k = normed.index(i)
                for c0 in range(0, splits[i], HEAD_DIM):
                    cols = slice(c0, c0 + HEAD_DIM)
                    dx, dg = _head_norm_bwd(p_ref[:, cols].astype(F32), x_refs[k][:, cols], g_refs[k][:, cols])
                    tmp[:, cols] = dx
                    dg_refs[k][:, cols] += dg
                src_ref = tmp
            for dst, src, n in runs:
                stage[:, dst:dst + n] = src_ref[:, src:src + n].astype(F32)
        o_ref[...] = stage[...].astype(o_ref.dtype)

    wmax = max(splits[i] for i in normed)
    row = lambda w: pl.BlockSpec((tm, w), lambda i: (i, 0))
    vec = lambda w: pl.BlockSpec((1, w), lambda i: (0, 0))
    outs = pl.pallas_call(
        body, name="dproj_merge", grid=(s // tm,),
        in_specs=[row(w) for w in splits] + [row(splits[i]) for i in normed] + [vec(splits[i]) for i in normed],
        out_specs=[row(wp)] + [vec(splits[i]) for i in normed],
        out_shape=[jax.ShapeDtypeStruct((s, wp), BF16)] + [jax.ShapeDtypeStruct((1, splits[i]), F32) for i in normed],
        scratch_shapes=[pltpu.VMEM((tm, wp), F32), pltpu.VMEM((tm, wmax), F32)],
        compiler_params=_params("arbitrary"),
    )(*parts, *[norms[i][0] for i in normed], *[norms[i][1] for i in normed])
    return outs[0], dict(zip(normed, outs[1:]))


def _norm_fwd(x, g, *, group, name, tm=256):
    s, w = x.shape
    ng = w // group

    def body(x_ref, g_ref, o_ref):
        for i in range(ng):
            cols = slice(i * group, (i + 1) * group)
            xv = x_ref[:, cols]
            r = lax.rsqrt(jnp.mean(xv * xv, axis=-1, keepdims=True) + EPS)
            o_ref[:, cols] = ((xv * r) * g_ref[:, cols]).astype(o_ref.dtype)

    return pl.pallas_call(
        body, name=name, grid=(s // tm,),
        in_specs=[pl.BlockSpec((tm, w), lambda i: (i, 0)), pl.BlockSpec((1, w), lambda i: (0, 0))],
        out_specs=pl.BlockSpec((tm, w), lambda i: (i, 0)),
        out_shape=jax.ShapeDtypeStruct((s, w), BF16),
        compiler_params=_params("parallel"),
    )(x, g)


def _norm_bwd(dy, x, g, *, group, name, res=None, out_dtypes=(BF16,), tm=256, deps=()):
    s, w = x.shape
    ng = w // group
    n_in = 4 if res is not None else 3

    def body(*refs):
        dy_ref, x_ref, g_ref = refs[:3]
        res_ref = refs[3] if res is not None else None
        outs = refs[n_in + len(deps):]
        dx_refs, dg_ref = outs[:-1], outs[-1]

        @pl.when(pl.program_id(0) == 0)
        def _():
            dg_ref[...] = jnp.zeros_like(dg_ref)

        for i in range(ng):
            cols = slice(i * group, (i + 1) * group)
            xv = x_ref[:, cols]
            dyv = dy_ref[:, cols].astype(F32)
            r = lax.rsqrt(jnp.mean(xv * xv, axis=-1, keepdims=True) + EPS)
            xr = xv * r
            dg_ref[:, cols] += jnp.sum(dyv * xr, axis=0, keepdims=True)
            gdy = dyv * g_ref[:, cols]
            dx = r * (gdy - xr * jnp.mean(gdy * xr, axis=-1, keepdims=True))
            if res_ref is not None:
                dx = dx + res_ref[:, cols]
            for dx_ref in dx_refs:
                dx_ref[:, cols] = dx.astype(dx_ref.dtype)

    row = pl.BlockSpec((tm, w), lambda i: (i, 0))
    vec = pl.BlockSpec((1, w), lambda i: (0, 0))
    in_specs = [row, row, vec] + ([row] if res is not None else []) + [pl.BlockSpec(memory_space=pl.ANY)] * len(deps)
    args = [dy, x, g] + ([res] if res is not None else []) + list(deps)
    out_specs = [row] * len(out_dtypes) + [vec]
    out_shape = [jax.ShapeDtypeStruct((s, w), dt) for dt in out_dtypes] + [jax.ShapeDtypeStruct((1, w), F32)]
    return pl.pallas_call(
        body, name=name, grid=(s // tm,), in_specs=in_specs, out_specs=out_specs,
        out_shape=out_shape, compiler_params=_params("arbitrary"),
    )(*args)


def _split3(v):
    p1 = v.astype(BF16)
    r1 = v - p1.astype(F32)
    p2 = r1.astype(BF16)
    p3 = (r1 - p2.astype(F32)).astype(BF16)
    return p1, p2, p3


def _tri_sum(v, reverse, tcol=512):
    h, s = v.shape
    tcol = min(tcol, s)
    parts = _split3(v)
    outs = []
    for j in range(s // tcol):
        src = lax.broadcasted_iota(jnp.int32, (s, tcol), 0)
        dst = lax.broadcasted_iota(jnp.int32, (s, tcol), 1) + j * tcol
        keep = (src >= dst) if reverse else (src <= dst)
        tri = jnp.where(keep, 1.0, 0.0).astype(BF16)
        acc = jnp.zeros((h, tcol), F32)
        for p in parts:
            acc = acc + jnp.dot(p, tri, preferred_element_type=F32)
        outs.append(acc)
    return outs


def _forget_fwd(fa_t, b):
    h, s = fa_t.shape
    tcol = min(512, s)

    def body(f_ref, b_ref, c_ref):
        z = f_ref[...] + b_ref[...]
        logf = jnp.minimum(z, 0.0) - jnp.log(1.0 + jnp.exp(-jnp.abs(z)))
        for j, blk in enumerate(_tri_sum(logf, reverse=False, tcol=tcol)):
            c_ref[:, j * tcol:(j + 1) * tcol] = blk

    return pl.pallas_call(
        body, name="forget_fwd", out_shape=jax.ShapeDtypeStruct((h, s), F32),
        compiler_params=_params(),
    )(fa_t, b)


def _forget_bwd(dacol, fa_t, b):
    h, s = fa_t.shape
    tcol = min(512, s)

    def body(d_ref, f_ref, b_ref, dfa_ref, db_ref):
        z = f_ref[...] + b_ref[...]
        dc = -d_ref[...]
        total = jnp.zeros((h, 1), F32)
        for j, blk in enumerate(_tri_sum(dc, reverse=True, tcol=tcol)):
            cols = slice(j * tcol, (j + 1) * tcol)
            dfa = blk * (1.0 - jax.nn.sigmoid(z[:, cols]))
            dfa_ref[:, cols] = dfa
            total = total + jnp.sum(dfa, axis=-1, keepdims=True)
        db_ref[...] = total

    return pl.pallas_call(
        body, name="forget_bwd",
        out_shape=[jax.ShapeDtypeStruct((h, s), F32), jax.ShapeDtypeStruct((h, 1), F32)],
        compiler_params=_params(),
    )(dacol, fa_t, b)


def _distance_bias(s, tile, dilated):
    nb = s // tile
    b = lax.broadcasted_iota(jnp.int32, (nb, tile, tile), 0)
    dist = b * tile + lax.broadcasted_iota(jnp.int32, (nb, tile, tile), 1) - lax.broadcasted_iota(jnp.int32, (nb, tile, tile), 2)
    if not dilated:
        return jnp.where(dist >= 0, 0.0, NEG).astype(F32)
    mult = jnp.zeros(dist.shape, jnp.int32)
    for window, dil in DIL_PATTERNS:
        mult = mult + ((dist >= 0) & (dist <= window) & ((dist & (dil - 1)) == 0)).astype(jnp.int32)
    logm = jnp.where(mult == 3, math.log(3.0), jnp.where(mult == 2, math.log(2.0), 0.0))
    return jnp.where(mult > 0, logm, NEG).astype(F32)


def _logits(q, k, arow, acol, bias):
    s = lax.dot_general(q, k, (((1,), (1,)), ((), ())), preferred_element_type=F32)
    return s * (1.0 / math.sqrt(HEAD_DIM)) + arow - acol + bias


def _attn_fwd(q, k, v, arow, acol, *, dilated, name, tq=ATTN_TILE, tk=ATTN_TILE):
    two_term = not dilated
    s, w = q.shape
    nh = w // HEAD_DIM
    assert tq == tk
    tq = tk = min(tq, s)
    nq, nk = s // tq, s // tk

    def body(q_ref, k_ref, v_ref, ar_ref, ac_ref, b_ref, o_ref, of_ref, lse_ref, m_ref, l_ref, acc_ref):
        qi, ki = pl.program_id(1), pl.program_id(2)

        @pl.when(ki == 0)
        def _():
            m_ref[...] = jnp.full_like(m_ref, NEG)
            l_ref[...] = jnp.zeros_like(l_ref)
            acc_ref[...] = jnp.zeros_like(acc_ref)

        @pl.when(ki <= qi)
        def _():
            sc = _logits(q_ref[...], k_ref[...], ar_ref[...], ac_ref[...], b_ref[...])
            m_new = jnp.maximum(m_ref[...], jnp.max(sc, axis=-1, keepdims=True))
            alpha = jnp.exp(m_ref[...] - m_new)
            p = jnp.exp(sc - m_new)
            l_ref[...] = alpha * l_ref[...] + jnp.sum(p, axis=-1, keepdims=True)
            p_hi = p.astype(BF16)
            vv = v_ref[...]
            pv = jnp.dot(p_hi, vv, preferred_element_type=F32)
            if two_term:
                pv = pv + jnp.dot((p - p_hi.astype(F32)).astype(BF16), vv, preferred_element_type=F32)
            acc_ref[...] = alpha * acc_ref[...] + pv
            m_ref[...] = m_new

        @pl.when(ki == nk - 1)
        def _():
            out = acc_ref[...] / l_ref[...]
            o_ref[...] = out.astype(o_ref.dtype)
            of_ref[...] = out
            lse_ref[...] = m_ref[...] + jnp.log(l_ref[...])

    kv = pl.BlockSpec((tk, HEAD_DIM), lambda h, i, j: (jnp.minimum(j, i), h))
    return pl.pallas_call(
        body, name=name, grid=(nh, nq, nk),
        in_specs=[pl.BlockSpec((tq, HEAD_DIM), lambda h, i, j: (i, h)), kv, kv,
                  pl.BlockSpec((None, tq, 1), lambda h, i, j: (h, i, 0)),
                  pl.BlockSpec((None, 1, tk), lambda h, i, j: (h, 0, jnp.minimum(j, i))),
                  pl.BlockSpec((None, tq, tk), lambda h, i, j: (jnp.maximum(i - j, 0), 0, 0))],
        out_specs=[pl.BlockSpec((tq, HEAD_DIM), lambda h, i, j: (i, h)),
                   pl.BlockSpec((tq, HEAD_DIM), lambda h, i, j: (i, h)),
                   pl.BlockSpec((None, tq, 1), lambda h, i, j: (h, i, 0))],
        out_shape=[jax.ShapeDtypeStruct((s, w), BF16), jax.ShapeDtypeStruct((s, w), F32),
                   jax.ShapeDtypeStruct((nh, s, 1), F32)],
        scratch_shapes=[pltpu.VMEM((tq, 1), F32), pltpu.VMEM((tq, 1), F32), pltpu.VMEM((tq, HEAD_DIM), F32)],
        compiler_params=_params("parallel", "parallel", "arbitrary"),
    )(q, k, v, arow, acol, _distance_bias(s, tq, dilated))


def _attn_bwd(q, k, v, o, do, lse, arow, acol, *, dilated, name, tq=ATTN_TILE, tk=ATTN_TILE):
    s, w = q.shape
    nh = w // HEAD_DIM
    assert tq == tk
    tq = tk = min(tq, s)
    nq, nk = s // tq, s // tk
    scale = 1.0 / math.sqrt(HEAD_DIM)

    def body(q_ref, k_ref, v_ref, o_ref, do_ref, lse_ref, ar_ref, ac_ref, b_ref,
             dq_ref, dk_ref, dv_ref, dac_ref, dk_acc, dv_acc, dac_acc):
        ki, qi = pl.program_id(1), pl.program_id(2)

        @pl.when((ki == 0) & (qi == 0))
        def _():
            dq_ref[...] = jnp.zeros_like(dq_ref)

        @pl.when(qi == 0)
        def _():
            dk_acc[...] = jnp.zeros_like(dk_acc)
            dv_acc[...] = jnp.zeros_like(dv_acc)
            dac_acc[...] = jnp.zeros_like(dac_acc)

        @pl.when(qi >= ki)
        def _():
            qv, kvv, dov = q_ref[...], k_ref[...], do_ref[...]
            sc = _logits(qv, kvv, ar_ref[...], ac_ref[...], b_ref[...])
            p = jnp.exp(sc - lse_ref[...])
            dp = lax.dot_general(dov, v_ref[...], (((1,), (1,)), ((), ())), preferred_element_type=F32)
            delta = jnp.sum(dov.astype(F32) * o_ref[...].astype(F32), axis=-1, keepdims=True)
            ds = p * (dp - delta)
            dsb = ds.astype(BF16)
            dv_acc[...] += lax.dot_general(p.astype(BF16), dov, (((0,), (0,)), ((), ())), preferred_element_type=F32)
            dk_acc[...] += lax.dot_general(dsb, qv, (((0,), (0,)), ((), ())), preferred_element_type=F32)
            rows = pl.ds(pl.multiple_of(qi * tq, tq), tq)
            dq_ref[rows, :] += jnp.dot(dsb, kvv, preferred_element_type=F32) * scale
            dac_acc[...] += jnp.sum(ds, axis=0, keepdims=True)

        @pl.when(qi == nq - 1)
        def _():
            dk_ref[...] = dk_acc[...] * scale
            dv_ref[...] = dv_acc[...]
            dac_ref[...] = dac_acc[...]

    qs = pl.BlockSpec((tq, HEAD_DIM), lambda h, j, i: (jnp.maximum(i, j), h))
    ks = pl.BlockSpec((tk, HEAD_DIM), lambda h, j, i: (j, h))
    rowv = pl.BlockSpec((None, tq, 1), lambda h, j, i: (h, jnp.maximum(i, j), 0))
    colv = pl.BlockSpec((None, 1, tk), lambda h, j, i: (h, 0, j))
    return pl.pallas_call(
        body, name=name, grid=(nh, nk, nq),
        in_specs=[qs, ks, ks, qs, qs, rowv, rowv, colv,
                  pl.BlockSpec((None, tq, tk), lambda h, j, i: (jnp.maximum(i - j, 0), 0, 0))],
        out_specs=[pl.BlockSpec((s, HEAD_DIM), lambda h, j, i: (0, h)), ks, ks, colv],
        out_shape=[jax.ShapeDtypeStruct((s, w), F32), jax.ShapeDtypeStruct((s, w), F32),
                   jax.ShapeDtypeStruct((s, w), F32), jax.ShapeDtypeStruct((nh, 1, s), F32)],
        scratch_shapes=[pltpu.VMEM((tk, HEAD_DIM), F32), pltpu.VMEM((tk, HEAD_DIM), F32), pltpu.VMEM((1, tk), F32)],
        compiler_params=_params("arbitrary", "arbitrary", "arbitrary"),
    )(q, k, v, o, do, lse, arow, acol, _distance_bias(s, tq, dilated))


def _gate_fwd(ga, gb, pa, pb, tm=256):
    s, d = ga.shape

    def body(ga_ref, gb_ref, pa_ref, pb_ref, o_ref):
        o_ref[...] = (jax.nn.sigmoid(ga_ref[...]) * pa_ref[...]
                      + jax.nn.sigmoid(gb_ref[...]) * pb_ref[...]).astype(o_ref.dtype)

    row = pl.BlockSpec((tm, d), lambda i: (i, 0))
    return pl.pallas_call(
        body, name="gate_fwd", grid=(s // tm,), in_specs=[row] * 4, out_specs=row,
        out_shape=jax.ShapeDtypeStruct((s, d), BF16), compiler_params=_params("parallel"),
    )(ga, gb, pa, pb)


def _gate_bwd(dm, ga, gb, pa, pb, tm=256):
    s, d = ga.shape

    def body(dm_ref, ga_ref, gb_ref, pa_ref, pb_ref, dpa_ref, dpb_ref, dga_ref, dgb_ref):
        dmv = dm_ref[...]
        for g_ref, p_ref, dp_ref, dg_ref in ((ga_ref, pa_ref, dpa_ref, dga_ref), (gb_ref, pb_ref, dpb_ref, dgb_ref)):
            sg = jax.nn.sigmoid(g_ref[...])
            dp_ref[...] = (dmv * sg).astype(BF16)
            dg_ref[...] = (dmv * p_ref[...] * (sg * (1.0 - sg))).astype(BF16)

    row = pl.BlockSpec((tm, d), lambda i: (i, 0))
    return pl.pallas_call(
        body, name="gate_bwd", grid=(s // tm,), in_specs=[row] * 5, out_specs=[row] * 4,
        out_shape=[jax.ShapeDtypeStruct((s, d), BF16)] * 4, compiler_params=_params("parallel"),
    )(dm, ga, gb, pa, pb)


def _shift_down(u, k):
    row = lax.broadcasted_iota(jnp.int32, u.shape, 0)
    return jnp.where(row >= k, pltpu.roll(u, k, 0), 0.0)


def _shift_up(u, k):
    n = u.shape[0]
    row = lax.broadcasted_iota(jnp.int32, u.shape, 0)
    return jnp.where(row < n - k, pltpu.roll(u, n - k, 0), 0.0)


def _conv3(u, wc, b):
    return wc[0:1, :] * _shift_down(u, 2) + wc[1:2, :] * _shift_down(u, 1) + wc[2:3, :] * u + b


def _conv_glu_fwd(u, wc, b, tn=256):
    s, f2 = u.shape
    f = f2 // 2
    nb = f // tn

    def body(ug_ref, uv_ref, wg_ref, wv_ref, bg_ref, bv_ref, o_ref):
        cg = _conv3(ug_ref[...], wg_ref[...], bg_ref[...])
        cv = _conv3(uv_ref[...], wv_ref[...], bv_ref[...])
        o_ref[...] = (cg * jax.nn.sigmoid(cg) * cv).astype(o_ref.dtype)

    def cols(rows, off):
        return pl.BlockSpec((rows, tn), lambda j: (0, j + off))

    return pl.pallas_call(
        body, name="conv_glu_fwd", grid=(nb,),
        in_specs=[cols(s, 0), cols(s, nb), cols(3, 0), cols(3, nb), cols(1, 0), cols(1, nb)],
        out_specs=cols(s, 0), out_shape=jax.ShapeDtypeStruct((s, f), BF16),
        compiler_params=_params("parallel"),
    )(u, u, wc, wc, b, b)


def _conv_glu_bwd(u, da, wc, b, tn=256):
    s, f2 = u.shape
    f = f2 // 2
    nb = f // tn

    def body(ug_ref, uv_ref, da_ref, wg_ref, wv_ref, bg_ref, bv_ref, dug_ref, duv_ref, sg_ref, sv_ref):
        ug, uv, wg, wv = ug_ref[...], uv_ref[...], wg_ref[...], wv_ref[...]
        cg = _conv3(ug, wg, bg_ref[...])
        cv = _conv3(uv, wv, bv_ref[...])
        sig = jax.nn.sigmoid(cg)
        dav = da_ref[...]
        dcv = dav * (cg * sig)
        dcg = dav * cv * (sig * (1.0 + cg * (1.0 - sig)))
        for dc, uu, w, du_ref, st_ref in ((dcg, ug, wg, dug_ref, sg_ref), (dcv, uv, wv, duv_ref, sv_ref)):
            du = w[2:3, :] * dc + w[1:2, :] * _shift_up(dc, 1) + w[0:1, :] * _shift_up(dc, 2)
            du_ref[...] = du.astype(BF16)
            st_ref[...] = jnp.zeros_like(st_ref)
            st_ref[0:1, :] = jnp.sum(dc * _shift_down(uu, 2), axis=0, keepdims=True)
            st_ref[1:2, :] = jnp.sum(dc * _shift_down(uu, 1), axis=0, keepdims=True)
            st_ref[2:3, :] = jnp.sum(dc * uu, axis=0, keepdims=True)
            st_ref[3:4, :] = jnp.sum(dc, axis=0, keepdims=True)

    def cols(rows, off):
        return pl.BlockSpec((rows, tn), lambda j: (0, j + off))

    return pl.pallas_call(
        body, name="conv_glu_bwd", grid=(nb,),
        in_specs=[cols(s, 0), cols(s, nb), cols(s, 0), cols(3, 0), cols(3, nb), cols(1, 0), cols(1, nb)],
        out_specs=[cols(s, 0), cols(s, 0), cols(8, 0), cols(8, 0)],
        out_shape=[jax.ShapeDtypeStruct((s, f), BF16), jax.ShapeDtypeStruct((s, f), BF16),
                   jax.ShapeDtypeStruct((8, f), F32), jax.ShapeDtypeStruct((8, f), F32)],
        compiler_params=_params("parallel"),
    )(u, u, da, wc, wc, b, b)


def _loss_head(y, target, tm=256):
    s, d = y.shape

    def body(y_ref, t_ref, dyf_ref, dyb_ref, l_ref):
        @pl.when(pl.program_id(0) == 0)
        def _():
            l_ref[...] = jnp.zeros_like(l_ref)

        err = y_ref[...] - t_ref[...]
        dy = err * (1.0 / d)
        dyf_ref[...] = dy
        dyb_ref[...] = dy.astype(BF16)
        l_ref[...] += 0.5 * jnp.sum(jnp.sum(err * err, axis=-1, keepdims=True) * (1.0 / d), axis=0, keepdims=True)

    row = pl.BlockSpec((tm, d), lambda i: (i, 0))
    return pl.pallas_call(
        body, name="loss_head", grid=(s // tm,), in_specs=[row, row],
        out_specs=[row, row, pl.BlockSpec((8, LANES), lambda i: (0, 0))],
        out_shape=[jax.ShapeDtypeStruct((s, d), F32), jax.ShapeDtypeStruct((s, d), BF16),
                   jax.ShapeDtypeStruct((8, LANES), F32)],
        compiler_params=_params("arbitrary"),
    )(y, target)


ROW_TILES = (256, 128, 64, 32, 16, 8)
BLOCK_BYTES = 2 << 20


def _add_halves(g, r1, place):
    ns, r, c = g.shape
    rh = r // 2
    tr = _pick(rh, ROW_TILES)
    g4 = g.reshape(ns, 2, rh, c)

    def body(p_ref, g_ref, r_ref, o_ref):
        o_ref[...] = (g_ref[...].astype(F32) + r_ref[...].astype(F32)).astype(o_ref.dtype)

    def slab(s, pr):
        return s + (s >= pr[0]).astype(jnp.int32)

    return pl.pallas_call(
        body, name="add_halves",
        grid_spec=pltpu.PrefetchScalarGridSpec(
            num_scalar_prefetch=1, grid=(ns - 1, rh // tr),
            in_specs=[pl.BlockSpec((None, None, tr, c), lambda s, i, pr: (slab(s, pr), pr[1], i, 0)),
                      pl.BlockSpec((None, tr, c), lambda s, i, pr: (slab(s, pr), i, 0))],
            out_specs=pl.BlockSpec((None, tr, c), lambda s, i, pr: (slab(s, pr), i, 0))),
        out_shape=jax.ShapeDtypeStruct((ns, rh, c), BF16),
        compiler_params=_params("parallel", "parallel"),
    )(place, g4, r1)


def _sum_chips(g, r1, recv, place):
    ns, r, c = g.shape
    rh = r // 2
    tr = _pick(rh, ROW_TILES)
    g4 = g.reshape(ns, 2, rh, c)

    def body(p_ref, g_ref, r_ref, t0_ref, t1_ref, t2_ref, o_ref):
        own = g_ref[...].astype(F32) + r_ref[...].astype(F32)
        o_ref[...] = ((own + t0_ref[...].astype(F32)) + t1_ref[...].astype(F32)) + t2_ref[...].astype(F32)

    def peer(k):
        return pl.BlockSpec((None, tr, c), lambda i, pr: (k, i, 0))

    return pl.pallas_call(
        body, name="sum_chips",
        grid_spec=pltpu.PrefetchScalarGridSpec(
            num_scalar_prefetch=1, grid=(rh // tr,),
            in_specs=[pl.BlockSpec((None, None, tr, c), lambda i, pr: (pr[0], pr[1], i, 0)),
                      pl.BlockSpec((None, tr, c), lambda i, pr: (pr[0], i, 0)), peer(0), peer(1), peer(2)],
            out_specs=pl.BlockSpec((tr, c), lambda i, pr: (pr[1] * (rh // tr) + i, 0))),
        out_shape=jax.ShapeDtypeStruct((r, c), F32),
        compiler_params=_params("parallel"),
    )(place, g4, r1, recv, recv, recv)


def _sum_devices(packs):
    n, r, c = packs.shape

    def body(p_ref, o_ref):
        acc = p_ref[0]
        for d in range(1, n):
            acc = acc + p_ref[d]
        o_ref[...] = acc

    return pl.pallas_call(
        body, name="sum_devices", out_shape=jax.ShapeDtypeStruct((r, c), F32), compiler_params=_params(),
    )(packs)


def _adamw_update(wv, gv, mv, vv):
    c1 = 1.0 - ADAM_B1 ** ADAM_STEP
    c2 = 1.0 - ADAM_B2 ** ADAM_STEP
    mn = ADAM_B1 * mv + (1.0 - ADAM_B1) * gv
    vn = ADAM_B2 * vv + (1.0 - ADAM_B2) * (gv * gv)
    m_hat = mn / c1
    v_hat = vn / c2
    return -ADAM_LR * (m_hat / (jnp.sqrt(v_hat) + ADAM_EPS) + ADAM_WD * wv), mn, vn


def _adamw(w, g, m, v, name, deps=(), emit_grad=False):
    r, c = w.shape
    tr = _pick(r, [t for t in ROW_TILES if t * c * 4 <= BLOCK_BYTES]) if r >= 8 else r
    n_out = 4 if emit_grad else 3

    def body(w_ref, g_ref, m_ref, v_ref, *rest):
        outs = rest[-n_out:]
        gv = g_ref[:, :c]
        if emit_grad:
            outs[0][...] = gv
        outs[-3][...], outs[-2][...], outs[-1][...] = _adamw_update(w_ref[...], gv, m_ref[...], v_ref[...])

    blk = pl.BlockSpec((tr, c), lambda i: (i, 0))
    g_blk = pl.BlockSpec((tr, g.shape[1]), lambda i: (i, 0))
    return pl.pallas_call(
        body, name=name, grid=(r // tr,), in_specs=[blk, g_blk, blk, blk] + [ANY] * len(deps), out_specs=[blk] * n_out,
        out_shape=[jax.ShapeDtypeStruct((r, c), F32)] * n_out, compiler_params=_params("parallel"),
    )(w, g, m, v, *deps)


ANY = pl.BlockSpec(memory_space=pl.ANY)


def _place():
    x, y, c = lax.axis_index("x"), lax.axis_index("y"), lax.axis_index("c")
    chips = [(1 - x, y), (x, 1 - y), (1 - x, 1 - y)]
    return x, y, c, chips


def _remote(src, dst, send_sem, recv_sem, to):
    return pltpu.make_async_remote_copy(src_ref=src, dst_ref=dst, send_sem=send_sem, recv_sem=recv_sem,
                                        device_id=to, device_id_type=MESH)


HBM = pl.BlockSpec(memory_space=pltpu.HBM)
SEM = pl.BlockSpec(memory_space=pltpu.SEMAPHORE)
EFFECT = pltpu.SideEffectType.DATAFLOW_SIDE_EFFECTING


def _in_hbm(a):
    return pltpu.with_memory_space_constraint(a, pltpu.HBM)


def _half(ref_rows, who):
    return pl.ds(who * (ref_rows // 2), ref_rows // 2)


def _gather_start(groups):
    items = [it for g in groups for it in g]
    n = len(items)
    sizes = [len(g) for g in groups]

    def body(*refs):
        srcs, lands = refs[:n], refs[n:2 * n]
        sems = refs[2 * n:2 * n + 2 * len(groups)]
        token = refs[-1]
        x, y, c, chips = _place()
        j = 2 * x + y
        at = 0
        for gi, g in enumerate(groups):
            send, recv = sems[2 * gi], sems[2 * gi + 1]
            for i, (shard, split) in enumerate(g):
                src, land = srcs[at], lands[at]
                at += 1
                rows = _half(shard.shape[0], c) if split else slice(None)
                for k, chip in enumerate(chips):
                    _remote(src.at[rows], land.at[j, rows], send.at[4 * i + k], recv.at[4 * i + k], (*chip, c)).start()
                _remote(src, land.at[j], send.at[4 * i + 3], recv.at[4 * i + 3], (x, y, 1 - c)).start()
        token[...] = jnp.zeros_like(token)

    sem_shapes = []
    for sz in sizes:
        sem_shapes += [pltpu.SemaphoreType.DMA((4 * sz,)), pltpu.SemaphoreType.DMA((4 * sz,))]
    out_shape = (sem_shapes + [pltpu.HBM(sh.shape, sh.dtype) for sh, _ in items]
                 + [pltpu.HBM((N_CHIPS,) + sh.shape, sh.dtype) for sh, _ in items]
                 + [jax.ShapeDtypeStruct((8, LANES), F32)])
    ns = len(sem_shapes)
    outs = pl.pallas_call(
        body, name="gather_start", in_specs=[HBM] * (2 * n),
        out_specs=[SEM] * ns + [HBM] * (2 * n) + [pl.BlockSpec(memory_space=pltpu.VMEM)],
        out_shape=out_shape, input_output_aliases={i: ns + i for i in range(2 * n)},
        compiler_params=pltpu.CompilerParams(has_side_effects=EFFECT),
    )(*[_in_hbm(sh) for sh, _ in items], *[_in_hbm(lax.empty((N_CHIPS,) + sh.shape, sh.dtype)) for sh, _ in items])
    sems, shards, lands, token = outs[:ns], outs[ns:ns + n], outs[ns + n:ns + 2 * n], outs[-1]
    res, at = [], 0
    for gi, sz in enumerate(sizes):
        res.append((shards[at:at + sz], lands[at:at + sz], sems[2 * gi], sems[2 * gi + 1]))
        at += sz
    return res, token


def _gather_pass(group, started, after, name):
    shards, lands, send, recv = started
    n = len(group)
    split_ix = [i for i, (_, split) in enumerate(group) if split]

    def body(*refs):
        lnds, send1, recv1 = refs[n:2 * n], refs[2 * n], refs[2 * n + 1]
        outs = refs[2 * n + 2 + len(after):]
        send2, recv2, token = outs[2 * n], outs[2 * n + 1], outs[2 * n + 2]
        x, y, c, chips = _place()
        sib = (x, y, 1 - c)
        for i, (shard, split) in enumerate(group):
            rows = _half(shard.shape[0], c) if split else slice(None)
            for k, (cx, cy) in enumerate(chips):
                landed = lnds[i].at[2 * cx + cy, rows]
                cp = _remote(landed, landed, send1.at[4 * i + k], recv1.at[4 * i + k], sib)
                cp.wait_send()
                cp.wait_recv()
            own = lnds[i].at[2 * x + y]
            cp = _remote(own, own, send1.at[4 * i + 3], recv1.at[4 * i + 3], sib)
            cp.wait_send()
            cp.wait_recv()
        for i2, i in enumerate(split_ix):
            rows = _half(group[i][0].shape[0], c)
            for k, (cx, cy) in enumerate(chips):
                landed = lnds[i].at[2 * cx + cy, rows]
                _remote(landed, landed, send2.at[3 * i2 + k], recv2.at[3 * i2 + k], sib).start()
        token[...] = jnp.zeros_like(token)

    n2 = len(split_ix)
    out_shape = ([pltpu.HBM(a.shape, a.dtype) for a in (*shards, *lands)]
                 + [pltpu.SemaphoreType.DMA((3 * n2,)), pltpu.SemaphoreType.DMA((3 * n2,)), jax.ShapeDtypeStruct((8, LANES), F32)])
    outs = pl.pallas_call(
        body, name=name, in_specs=[HBM] * (2 * n) + [SEM, SEM] + [ANY] * len(after),
        out_specs=[HBM] * (2 * n) + [SEM, SEM, pl.BlockSpec(memory_space=pltpu.VMEM)],
        out_shape=out_shape, input_output_aliases={i: i for i in range(2 * n)},
        compiler_params=pltpu.CompilerParams(has_side_effects=EFFECT),
    )(*shards, *lands, send, recv, *after)
    return outs[:n], (outs[n:2 * n], outs[2 * n], outs[2 * n + 1]), outs[2 * n + 2]


def _gather_wait(group, passed, after, name):
    lands, send2, recv2 = passed
    n = len(group)
    split_ix = [i for i, (_, split) in enumerate(group) if split]

    def body(*refs):
        lnds, s2, r2 = refs[:n], refs[n], refs[n + 1]
        x, y, c, chips = _place()
        sib = (x, y, 1 - c)
        for i2, i in enumerate(split_ix):
            rows = _half(group[i][0].shape[0], 1 - c)
            for k, (cx, cy) in enumerate(chips):
                landed = lnds[i].at[2 * cx + cy, rows]
                cp = _remote(landed, landed, s2.at[3 * i2 + k], r2.at[3 * i2 + k], sib)
                cp.wait_send()
                cp.wait_recv()

    return pl.pallas_call(
        body, name=name, in_specs=[HBM] * n + [SEM, SEM, ANY], out_specs=[HBM] * n,
        out_shape=[pltpu.HBM(a.shape, a.dtype) for a in lands], input_output_aliases={i: i for i in range(n)},
        compiler_params=pltpu.CompilerParams(has_side_effects=EFFECT),
    )(*lands, send2, recv2, after)


def _xfer_start(name, srcs, land_shapes, n_copies, copies, after):
    n, nl = len(srcs), len(land_shapes)

    def body(*refs):
        src_refs, land_refs = refs[:n], refs[n:n + nl]
        send, recv, token = refs[n + nl + 1], refs[n + nl + 2], refs[-1]
        for cp in copies(src_refs, land_refs, send, recv):
            cp.start()
        token[...] = jnp.zeros_like(token)

    lands = [_in_hbm(lax.empty(shape, dtype)) for shape, dtype in land_shapes]
    out_shape = ([pltpu.SemaphoreType.DMA((n_copies,)), pltpu.SemaphoreType.DMA((n_copies,))]
                 + [pltpu.HBM(a.shape, a.dtype) for a in (*srcs, *lands)] + [jax.ShapeDtypeStruct((8, LANES), F32)])
    outs = pl.pallas_call(
        body, name=name, in_specs=[HBM] * (n + nl) + [ANY],
        out_specs=[SEM, SEM] + [HBM] * (n + nl) + [pl.BlockSpec(memory_space=pltpu.VMEM)],
        out_shape=out_shape, input_output_aliases={i: 2 + i for i in range(n + nl)},
        compiler_params=pltpu.CompilerParams(has_side_effects=EFFECT),
    )(*[_in_hbm(a) for a in srcs], *lands, after)
    return (outs[2:2 + n], outs[2 + n:2 + n + nl], outs[0], outs[1]), outs[-1]


def _xfer_wait(name, started, copies, after):
    srcs, lands, send, recv = started
    n, nl = len(srcs), len(lands)

    def body(*refs):
        src_refs, land_refs, s_ref, r_ref = refs[:n], refs[n:n + nl], refs[n + nl], refs[n + nl + 1]
        for cp in copies(src_refs, land_refs, s_ref, r_ref):
            cp.wait_send()
            cp.wait_recv()

    outs = pl.pallas_call(
        body, name=name, in_specs=[HBM] * (n + nl) + [SEM, SEM, ANY], out_specs=[HBM] * (n + nl),
        out_shape=[pltpu.HBM(a.shape, a.dtype) for a in (*srcs, *lands)],
        input_output_aliases={i: i for i in range(n + nl)},
        compiler_params=pltpu.CompilerParams(has_side_effects=EFFECT),
    )(*srcs, *lands, send, recv, after)
    return outs[:n], outs[n:]


def _swap_copies(srcs, lands, send, recv):
    x, y, c, _ = _place()
    return [_remote(src.at[:, _half(src.shape[1], 1 - c)], land, send.at[i], recv.at[i], (x, y, 1 - c))
            for i, (src, land) in enumerate(zip(srcs, lands))]


def _scatter_copies(srcs, lands, send, recv):
    x, y, c, chips = _place()
    return [_remote(src.at[2 * cx + cy], land.at[k], send.at[3 * i + k], recv.at[3 * i + k], (cx, cy, c))
            for i, (src, land) in enumerate(zip(srcs, lands)) for k, (cx, cy) in enumerate(chips)]


def _join_copies(srcs, lands, send, recv):
    x, y, c, _ = _place()
    return [_remote(src.at[_half(src.shape[0], c)], src.at[_half(src.shape[0], c)], send.at[i], recv.at[i], (x, y, 1 - c))
            for i, src in enumerate(srcs)]


def _corner(a):
    return a[(slice(0, 1),) * a.ndim]


class _Reducer:
    def __init__(self, place):
        self.place = place
        self.state = {}

    def swap(self, key, grads, after):
        shapes = [((g.shape[0], g.shape[1] // 2, g.shape[2]), g.dtype) for g in grads]
        self.state[key], token = _xfer_start("swap_start_" + key, grads, shapes, len(grads), _swap_copies, _corner(after))
        return token

    def to_chips(self, key, after):
        grads, from_sibling = _xfer_wait("swap_wait_" + key, self.state[key], _swap_copies, after)
        sums = [_add_halves(g, r, self.place) for g, r in zip(grads, from_sibling)]
        shapes = [((3,) + s.shape[1:], s.dtype) for s in sums]
        started, token = _xfer_start("scatter_start_" + key, sums, shapes, 3 * len(sums), _scatter_copies, _corner(sums[-1]))
        self.state[key] = (grads, from_sibling, started)
        return token

    def to_core(self, key, after):
        grads, from_sibling, started = self.state[key]
        _, from_chips = _xfer_wait("scatter_wait_" + key, started, _scatter_copies, after)
        shards = [_sum_chips(g, r, rc, self.place) for g, r, rc in zip(grads, from_sibling, from_chips)]
        self.state[key], token = _xfer_start("join_start_" + key, shards, [], len(shards), _join_copies, _corner(shards[-1]))
        return token

    def finish(self, key, after):
        return _xfer_wait("join_wait_" + key, self.state.pop(key), _join_copies, after)[0]


def _gather_packs(pack, deps=()):
    def body(p_ref, *rest):
        o_ref, lsem, ssem, rsem = rest[-4:]
        x, y, c, _ = _place()
        me = 4 * x + 2 * y + c
        local = pltpu.make_async_copy(p_ref, o_ref.at[me], lsem)
        local.start()
        cps = []
        for k in range(1, N_DEV):
            fx, fy, fc = (k >> 2) & 1, (k >> 1) & 1, k & 1
            to = (x ^ fx, y ^ fy, c ^ fc)
            cps.append(_remote(p_ref, o_ref.at[me], ssem.at[k - 1], rsem.at[k - 1], to))
        for cp in cps:
            cp.start()
        for k in range(1, N_DEV):
            fx, fy, fc = (k >> 2) & 1, (k >> 1) & 1, k & 1
            src = o_ref.at[4 * (x ^ fx) + 2 * (y ^ fy) + (c ^ fc)]
            _remote(src, src, ssem.at[k - 1], rsem.at[k - 1], (x, y, c)).wait_recv()
        for cp in cps:
            cp.wait_send()
        local.wait()

    return pl.pallas_call(
        body, name="gather_packs", in_specs=[ANY] * (1 + len(deps)), out_specs=ANY,
        out_shape=jax.ShapeDtypeStruct((N_DEV,) + pack.shape, pack.dtype),
        scratch_shapes=[pltpu.SemaphoreType.DMA, pltpu.SemaphoreType.DMA((N_DEV - 1,)), pltpu.SemaphoreType.DMA((N_DEV - 1,))],
    )(pack, *deps)


LANE_TILES = (512, 896, 1408, 704, 384, 256, 128)


def _layer_grads(x, target, small, wg, rest_pass, rest_wait, red, filler):
    s, d = x.shape
    f = wg["conv"].shape[1] // 2
    w_att = N_HEADS * HEAD_DIM
    in_splits = (w_att, w_att, w_att, N_HEADS, w_att, w_att, w_att, d, d)
    in_cols = sum(in_splits)
    cs = in_cols // N_CHIPS
    cp = wg["in"].shape[2]
    tm = min(s, 1024)
    t_in = cp
    t_d = _pick(d, LANE_TILES)
    t_d2 = min(d, 1024)
    t_dq = _pick(d // N_CHIPS, LANE_TILES)
    t_w = _pick(w_att, LANE_TILES)
    t_up = 2 * f // N_CHIPS
    tm_wide = min(s, 512)
    t_fq = _pick(f // N_CHIPS, LANE_TILES)
    offs = np.cumsum(in_splits)[:-1].tolist()

    h1 = _norm_fwd(x, small["g_attn"], group=d, name="rms1_fwd")
    proj_p = _mm(h1, wg["in"], mode="nn", b_kind="col", tm=tm_wide, tn=t_in, tk=d, name="mm_in")
    gains = {n: small[n].reshape(1, w_att) for n in ("g_q_fox", "g_k_fox", "g_q_dil", "g_k_dil")}
    qa, ka, va_b, fa, qb, kb, vb_b, ga, gb, qa_n, ka_n, qb_n, kb_n = _proj_split(
        proj_p, in_splits, cs, (F32, F32, BF16, F32, F32, F32, BF16, F32, F32),
        {0: gains["g_q_fox"], 1: gains["g_k_fox"], 4: gains["g_q_dil"], 5: gains["g_k_dil"]})
    fa_t = fa.T
    b_f = small["b_forget"].reshape(N_HEADS, 1)
    c_f = _forget_fwd(fa_t, b_f)
    slopes = jnp.asarray(2.0 ** (-8.0 * np.arange(1, N_HEADS + 1) / N_HEADS), dtype=F32)
    a_d = -(slopes[:, None] * jnp.arange(s, dtype=F32)[None, :])
    rows_f, cols_f = c_f[:, :, None], c_f[:, None, :]
    rows_d, cols_d = a_d[:, :, None], a_d[:, None, :]
    o_a, o_a32, lse_a = _attn_fwd(qa_n, ka_n, va_b, rows_f, cols_f, dilated=False, name="attn_fox_fwd")
    token = rest_pass("mid", o_a)
    rows_d = rows_d + token[0, 0]
    o_b, o_b32, lse_b = _attn_fwd(qb_n, kb_n, vb_b, rows_d, cols_d, dilated=True, name="attn_dil_fwd")
    wg = dict(wg, **rest_wait("mid", o_b))
    token = rest_pass("late", o_b)
    pa = _mm(o_a, wg["brf"], mode="nn", b_kind="col", tm=tm, tn=t_dq, tk=w_att, name="mm_brf", deps=(token,))
    pb = _mm(o_b, wg["brd"], mode="nn", b_kind="col", tm=tm, tn=t_dq, tk=w_att, name="mm_brd")
    merged = _gate_fwd(ga, gb, pa, pb)
    x1 = _mm(merged, wg["out"], mode="nn", b_kind="row", res=x, tm=tm, tn=t_d, tk=t_dq, name="mm_out")
    wg = dict(wg, **rest_wait("late", x1))
    h2 = _norm_fwd(x1, small["g_ffn"], group=d, name="rms2_fwd")
    u = _mm(h2, wg["up"], mode="nn", b_kind="col", tm=tm_wide, tn=t_up, tk=d, name="mm_up")
    act = _conv_glu_fwd(u, wg["conv"], wg["bconv"])
    y = _mm(act, wg["down"], mode="nn", b_kind="row", res=x1, tm=tm, tn=t_d2, tk=t_fq, name="mm_down")
    dy_f, dy_b, loss_blk = _loss_head(y, target)

    d_act = _mm(dy_b, wg["down"], mode="nt", b_kind="row", tm=tm, tn=t_fq, tk=d, name="mm_down_dx")
    g_down = _mm(act, dy_b, mode="tn", out_dtype=BF16, out_kind="row", tm=t_fq, tn=t_d2, tk=s, name="mm_down_dw")
    tok = red.swap("down", [g_down], g_down)
    du_g, du_v, st_g, st_v = _conv_glu_bwd(u, d_act, wg["conv"] + tok[0, 0], wg["bconv"])
    tok = red.to_chips("down", du_g)
    du = jnp.concatenate([du_g, du_v], axis=1)
    g_up = _mm(h2, du, mode="tn", out_dtype=BF16, out_kind="col", tm=t_d2, tn=t_up, tk=s, name="mm_up_dw", deps=(tok,))
    tok = red.to_core("down", g_up)
    tok2 = red.swap("up", [g_up], g_up)
    dh2 = _mm(du, wg["up"], mode="nt", b_kind="col", tm=tm, tn=t_d2, tk=t_up, name="mm_up_dx", deps=(tok, tok2))
    tok = red.to_chips("up", dh2)
    dx1_b, dx1_f, dg_ffn = _norm_bwd(dh2, x1, small["g_ffn"], group=d, res=dy_f, out_dtypes=(BF16, F32), name="rms2_bwd")
    d_merged = _mm(dx1_b, wg["out"], mode="nt", b_kind="row", tm=tm, tn=t_dq, tk=d, name="mm_out_dx", deps=(tok,))
    g_out = _mm(merged, dx1_b, mode="tn", out_dtype=BF16, out_kind="row", tm=t_dq, tn=t_d2, tk=s, name="mm_out_dw")
    dpa, dpb, dga, dgb = _gate_bwd(d_merged, ga, gb, pa, pb)
    do_a = _mm(dpa, wg["brf"], mode="nt", b_kind="col", out_dtype=BF16, tm=s, tn=w_att, tk=t_dq, name="mm_brf_dx")
    do_b = _mm(dpb, wg["brd"], mode="nt", b_kind="col", out_dtype=BF16, tm=s, tn=w_att, tk=t_dq, name="mm_brd_dx")
    g_brf = _mm(o_a, dpa, mode="tn", out_dtype=BF16, out_kind="col", tm=w_att, tn=t_dq, tk=s, name="mm_brf_dw")
    g_brd = _mm(o_b, dpb, mode="tn", out_dtype=BF16, out_kind="col", tm=w_att, tn=t_dq, tk=s, name="mm_brd_dw")
    tok = red.swap("mix", [g_out, g_brf, g_brd], g_brd)
    dqa_n, dka_n, dva, dac_a = _attn_bwd(qa_n, ka_n, va_b, o_a32, do_a, lse_a, rows_f + tok[0, 0], cols_f, dilated=False, name="attn_fox_bwd")
    tok = red.to_core("up", dqa_n)
    tok2 = red.to_chips("mix", dqa_n)
    dqb_n, dkb_n, dvb, _ = _attn_bwd(qb_n, kb_n, vb_b, o_b32, do_b, lse_b, rows_d + (tok[0, 0] + tok2[0, 0]), cols_d, dilated=True, name="attn_dil_bwd")
    tok = red.to_core("mix", dqb_n)
    dfa_t, db_f = _forget_bwd(dac_a[:, 0, :], fa_t, b_f)
    dproj_p, dgains = _dproj_merge(
        [dqa_n, dka_n, dva, dfa_t.T, dqb_n, dkb_n, dvb, dga, dgb], in_splits, cs, cp,
        {0: (qa, gains["g_q_fox"]), 1: (ka, gains["g_k_fox"]), 4: (qb, gains["g_q_dil"]), 5: (kb, gains["g_k_dil"])})
    dg_qf, dg_kf, dg_qd, dg_kd = dgains[0], dgains[1], dgains[4], dgains[5]
    g_in = _mm(h1, dproj_p, mode="tn", out_dtype=BF16, out_kind="col", tm=t_d2, tn=t_in, tk=s, name="mm_in_dw", deps=(tok,))
    tok = red.swap("in", [g_in], g_in)
    tok = red.to_chips("in", filler(tok))
    dh1 = _mm(dproj_p, wg["in"], mode="nt", b_kind="col", tm=tm, tn=t_d2, tk=t_in, name="mm_in_dx", deps=(tok,))
    grad_x, dg_attn = _norm_bwd(dh1, x, small["g_attn"], group=d, res=dx1_f, out_dtypes=(F32,), name="rms1_bwd")

    small_grads = {
        "g_attn": dg_attn, "b_forget": db_f.reshape(1, N_HEADS),
        "g_q_fox": dg_qf, "g_k_fox": dg_kf, "g_q_dil": dg_qd, "g_k_dil": dg_kd, "g_ffn": dg_ffn,
        "w_conv": jnp.concatenate([st_g[0:3], st_v[0:3]], axis=1),
        "b_conv": jnp.concatenate([st_g[3:4], st_v[3:4]], axis=1),
        "loss": loss_blk[0:1, 0:1],
    }
    return small_grads, grad_x


SMALL_ORDER = ("g_attn", "b_forget", "g_q_fox", "g_k_fox", "g_q_dil", "g_k_dil", "g_ffn", "w_conv", "b_conv", "loss")
WEIGHT_ORDER = ("g_attn", "w_in", "b_forget", "g_q_fox", "g_k_fox", "g_q_dil", "g_k_dil", "w_br_fox", "w_br_dil",
                "w_out", "g_ffn", "w_up", "w_conv", "b_conv", "w_down")
BIG = {"w_in": "in", "w_br_fox": "brf", "w_br_dil": "brd", "w_out": "out", "w_up": "up", "w_down": "down"}


def kernel(x, g_attn, w_in, b_forget, g_q_fox, g_k_fox, g_q_dil, g_k_dil, w_br_fox, w_br_dil, w_out, g_ffn, w_up, w_conv, b_conv, w_down, loss_target, m_g_attn, m_w_in, m_b_forget, m_g_q_fox, m_g_k_fox, m_g_q_dil, m_g_k_dil, m_w_br_fox, m_w_br_dil, m_w_out, m_g_ffn, m_w_up, m_w_conv, m_b_conv, m_w_down, v_g_attn, v_w_in, v_b_forget, v_g_q_fox, v_g_k_fox, v_g_q_dil, v_g_k_dil, v_w_br_fox, v_w_br_dil, v_w_out, v_g_ffn, v_w_up, v_w_conv, v_b_conv, v_w_down):
    w = dict(g_attn=g_attn, w_in=w_in, b_forget=b_forget, g_q_fox=g_q_fox, g_k_fox=g_k_fox, g_q_dil=g_q_dil,
             g_k_dil=g_k_dil, w_br_fox=w_br_fox, w_br_dil=w_br_dil, w_out=w_out, g_ffn=g_ffn, w_up=w_up,
             w_conv=w_conv, b_conv=b_conv, w_down=w_down)
    m = dict(g_attn=m_g_attn, w_in=m_w_in, b_forget=m_b_forget, g_q_fox=m_g_q_fox, g_k_fox=m_g_k_fox,
             g_q_dil=m_g_q_dil, g_k_dil=m_g_k_dil, w_br_fox=m_w_br_fox, w_br_dil=m_w_br_dil, w_out=m_w_out,
             g_ffn=m_g_ffn, w_up=m_w_up, w_conv=m_w_conv, b_conv=m_b_conv, w_down=m_w_down)
    v = dict(g_attn=v_g_attn, w_in=v_w_in, b_forget=v_b_forget, g_q_fox=v_g_q_fox, g_k_fox=v_g_k_fox,
             g_q_dil=v_g_q_dil, g_k_dil=v_g_k_dil, w_br_fox=v_w_br_fox, w_br_dil=v_w_br_dil, w_out=v_w_out,
             g_ffn=v_g_ffn, w_up=v_w_up, w_conv=v_w_conv, b_conv=v_b_conv, w_down=v_w_down)
    xi, yi, ci = lax.axis_index("x"), lax.axis_index("y"), lax.axis_index("c")
    chip = (2 * xi + yi).astype(jnp.int32)

    cs = w_in.shape[2]
    cp = _round_up(cs, LANES)
    shards = {
        "in": jnp.pad(w_in[0].astype(BF16), ((0, 0), (0, cp - cs))),
        "brf": w_br_fox[0].astype(BF16), "brd": w_br_dil[0].astype(BF16), "out": w_out[0].astype(BF16),
        "up": w_up[0].astype(BF16), "down": w_down[0].astype(BF16),
    }
    names = tuple(shards)
    conv_pad = jnp.pad(w_conv[0], ((0, 8 - w_conv.shape[1]), (0, 0)))
    first = [(shards["in"], True), (conv_pad, False)]
    later = {"mid": ("brf", "brd", "out"), "late": ("up", "down")}
    groups = {key: [(shards[n], True) for n in members] for key, members in later.items()}
    (started_first, *started_later), token = _gather_start([first, *groups.values()])
    started = dict(zip(later, started_later))
    token, w["w_in"], m["w_in"], v["w_in"] = lax.optimization_barrier((token, w["w_in"], m["w_in"], v["w_in"]))
    w2, m2, v2 = ({n: a[n].reshape(a[n].shape[-2], a[n].shape[-1]) for n in BIG} for a in (w, m, v))
    early = (token, w2["w_in"], m2["w_in"], v2["w_in"])
    own_first, passed_first, token = _gather_pass(first, started_first, early, "gather_pass_in")
    land_in, land_conv = _gather_wait(first, passed_first, token, "gather_wait_in")
    wg = {"in": land_in, "bconv": b_conv,
          "conv": jnp.transpose(land_conv[:, :w_conv.shape[1], :], (1, 0, 2)).reshape(w_conv.shape[1], -1)}
    small = {n: w[n] for n in ("g_attn", "b_forget", "g_q_fox", "g_k_fox", "g_q_dil", "g_k_dil", "g_ffn")}
    small = {n: (a[0] if a.ndim == 3 else a) for n, a in small.items()}
    in_flight = {}

    def rest_pass(key, after):
        own, passed, tok = _gather_pass(groups[key], started[key], (after,), "gather_pass_" + key)
        in_flight[key] = (own, passed)
        return tok

    def rest_wait(key, after):
        own, passed = in_flight.pop(key)
        lands = _gather_wait(groups[key], passed, after, "gather_wait_" + key)
        return dict(zip(later[key], lands))

    reducer = _Reducer(jnp.stack([chip, ci.astype(jnp.int32)]))
    g_out, d_out, m_out, v_out = {}, {}, {}, {}
    reduced = {}

    def first_element(arrays):
        return jnp.stack([a[(0,) * a.ndim] for a in arrays])

    def update_big(n, deps):
        g2, dl, mn, vn = _adamw(w2[n], reduced[BIG[n]], m2[n], v2[n], name="adamw_" + n, deps=deps, emit_grad=True)
        g_out[n], d_out[n], m_out[n], v_out[n] = (a.reshape(w[n].shape) for a in (g2, dl, mn, vn))

    def update_down(tok):
        (reduced["down"],) = reducer.finish("down", tok)
        update_big("w_down", (tok,))
        return v_out["w_down"]

    small_grads, grad_x = _layer_grads(x[0], loss_target[0], small, wg, rest_pass, rest_wait, reducer, update_down)

    for key, members in (("up", ("up",)), ("mix", ("out", "brf", "brd"))):
        reduced.update(zip(members, reducer.finish(key, grad_x)))
    others = ("w_up", "w_out", "w_br_fox", "w_br_dil")
    for n in others:
        update_big(n, (grad_x,))

    flat = jnp.concatenate([small_grads[n].reshape(-1) for n in SMALL_ORDER])
    rows = _round_up(flat.shape[0], 8 * LANES) // LANES
    pack = jnp.pad(flat, (0, rows * LANES - flat.shape[0])).reshape(rows, LANES)
    packs = _gather_packs(pack, deps=(first_element([v_out[n] for n in others]),))
    total = _sum_devices(packs).reshape(-1)
    red, at = {}, 0
    for n in SMALL_ORDER:
        size = small_grads[n].size
        red[n] = total[at:at + size].reshape(small_grads[n].shape)
        at += size
    loss = red["loss"].reshape(())
    c2 = w_conv.shape[2]
    red["w_conv"] = lax.dynamic_slice_in_dim(red["w_conv"], chip * c2, c2, axis=1)

    smalls = [n for n in WEIGHT_ORDER if n not in BIG]
    for n in smalls:
        shape = w[n].shape
        r2 = (shape[-2], shape[-1]) if n not in ("g_attn", "b_forget", "g_ffn", "b_conv") else (1, shape[-1])
        g2 = red[n].reshape(r2)
        dl, mn, vn = _adamw(w[n].reshape(r2), g2, m[n].reshape(r2), v[n].reshape(r2), name="adamw_" + n)
        g_out[n], d_out[n], m_out[n], v_out[n] = (a.reshape(shape) for a in (g2, dl, mn, vn))
    tok = reducer.to_core("in", first_element([v_out[n] for n in smalls]))
    (reduced["in"],) = reducer.finish("in", tok)
    update_big("w_in", (tok,))

    return (loss, grad_x[None], *[g_out[n] for n in WEIGHT_ORDER], *[d_out[n] for n in WEIGHT_ORDER],
            *[m_out[n] for n in WEIGHT_ORDER], *[v_out[n] for n in WEIGHT_ORDER])
```

```python
import functools
import math

import jax
import jax.numpy as jnp
import numpy as np
from jax import lax
from jax.experimental import pallas as pl
from jax.experimental.pallas import tpu as pltpu

F32 = jnp.float32
BF16 = jnp.bfloat16
HEAD_DIM = 128
N_HEADS = 8
EPS = 1e-6
NEG = -1e30
N_CHIPS = 4
N_DEV = 8
LANES = 128
VMEM_LIMIT_BYTES = 56 * 1024 * 1024
DIL_PATTERNS = ((128, 1), (512, 4), (2048, 16))
ATTN_TILE = 512
ADAM_LR, ADAM_B1, ADAM_B2, ADAM_EPS, ADAM_WD, ADAM_STEP = 0.001, 0.9, 0.999, 1e-08, 0.01, 10
MESH = pl.DeviceIdType.MESH


def _params(*sem):
    return pltpu.CompilerParams(dimension_semantics=sem, vmem_limit_bytes=VMEM_LIMIT_BYTES)


def _round_up(n, m):
    return -(-n // m) * m


def _pick(dim, prefs):
    for p in prefs:
        if dim % p == 0:
            return p
    raise ValueError(f"no tile for {dim} in {prefs}")


def _logical_shape(arr, kind):
    if kind is None:
        return arr.shape
    s, r, c = arr.shape
    return (r, s * c) if kind == "col" else (s * r, c)


def _spec(shape, kind, br, bc, fi, fj):
    if kind is None:
        return pl.BlockSpec((br, bc), lambda *g: (fi(*g), fj(*g)))
    _, r, c = shape
    if kind == "col":
        nb = c // bc
        assert nb * bc == c, (shape, bc)
        return pl.BlockSpec((None, br, bc), lambda *g: (fj(*g) // nb, fi(*g), fj(*g) % nb))
    nb = r // br
    assert nb * br == r, (shape, br)
    return pl.BlockSpec((None, br, bc), lambda *g: (fi(*g) // nb, fi(*g) % nb, fj(*g)))


def _mm(a, b, *, mode, tm, tn, tk, name, a_kind=None, b_kind=None, out_kind=None,
        out_dtype=F32, res=None, deps=()):
    la, lb = _logical_shape(a, a_kind), _logical_shape(b, b_kind)
    if mode == "nn":
        (m, k), (k2, n) = la, lb
    elif mode == "nt":
        (m, k), (n, k2) = la, lb
    else:
        (k, m), (k2, n) = la, lb
    assert k == k2, (name, la, lb)
    assert m % tm == 0 and n % tn == 0 and k % tk == 0, (name, m, n, k, tm, tn, tk)
    nk = k // tk
    im = lambda i, j, l: i
    jn = lambda i, j, l: j
    lk = lambda i, j, l: l
    if mode == "tn":
        a_spec = _spec(a.shape, a_kind, tk, tm, lk, im)
        dims = (((0,), (0,)), ((), ()))
    else:
        a_spec = _spec(a.shape, a_kind, tm, tk, im, lk)
        dims = (((1,), (1,)), ((), ())) if mode == "nt" else (((1,), (0,)), ((), ()))
    if mode == "nt":
        b_spec = _spec(b.shape, b_kind, tn, tk, jn, lk)
    else:
        b_spec = _spec(b.shape, b_kind, tk, tn, lk, jn)
    if out_kind is None:
        oshape = (m, n)
    elif out_kind == "col":
        oshape = (N_CHIPS, m, n // N_CHIPS)
    else:
        oshape = (N_CHIPS, m // N_CHIPS, n)
    o_spec = _spec(oshape, out_kind, tm, tn, im, jn)
    in_specs = [a_spec, b_spec]
    args = [a, b]
    if res is not None:
        in_specs.append(pl.BlockSpec((tm, tn), lambda i, j, l: (i, j)))
        args.append(res)
    in_specs += [pl.BlockSpec(memory_space=pl.ANY)] * len(deps)
    args += list(deps)

    def finish(out, res_ref, o_ref):
        if res_ref is not None:
            out = out + res_ref[...]
        o_ref[...] = out.astype(o_ref.dtype)

    def body_whole_k(*refs):
        res_ref = refs[2] if res is not None else None
        finish(lax.dot_general(refs[0][...], refs[1][...], dims, preferred_element_type=F32), res_ref, refs[-1])

    def body(*refs):
        a_ref, b_ref = refs[0], refs[1]
        res_ref = refs[2] if res is not None else None
        o_ref, acc_ref = refs[-2], refs[-1]
        step = pl.program_id(2)

        @pl.when(step == 0)
        def _():
            acc_ref[...] = jnp.zeros_like(acc_ref)

        acc_ref[...] += lax.dot_general(a_ref[...], b_ref[...], dims, preferred_element_type=F32)

        @pl.when(step == nk - 1)
        def _():
            finish(acc_ref[...], res_ref, o_ref)

    return pl.pallas_call(
        body_whole_k if nk == 1 else body, name=name, grid=(m // tm, n // tn, nk),
        in_specs=in_specs, out_specs=o_spec,
        out_shape=jax.ShapeDtypeStruct(oshape, out_dtype),
        scratch_shapes=[] if nk == 1 else [pltpu.VMEM((tm, tn), F32)],
        compiler_params=_params("parallel", "parallel", "arbitrary"),
    )(*args)


def _pieces(splits, cs, cp):
    out, g0 = [], 0
    for width in splits:
        g1, runs = g0 + width, []
        for j in range(N_CHIPS):
            a, b = max(g0, cs * j), min(g1, cs * (j + 1))
            if a < b:
                runs.append((j * cp + a - cs * j, a - g0, b - a))
        out.append(runs)
        g0 = g1
    return out


def _head_norm(xv, gv):
    r = lax.rsqrt(jnp.mean(xv * xv, axis=-1, keepdims=True) + EPS)
    return (xv * r) * gv


def _head_norm_bwd(dyv, xv, gv):
    r = lax.rsqrt(jnp.mean(xv * xv, axis=-1, keepdims=True) + EPS)
    xr = xv * r
    gdy = dyv * gv
    return r * (gdy - xr * jnp.mean(gdy * xr, axis=-1, keepdims=True)), jnp.sum(dyv * xr, axis=0, keepdims=True)


def _proj_split(proj_p, splits, cs, dtypes, gains, tm=128):
    s, wp = proj_p.shape
    pieces = _pieces(splits, cs, wp // N_CHIPS)
    normed = sorted(gains)
    nseg = len(splits)

    def body(p_ref, *refs):
        g_refs, o_refs, n_refs = refs[:len(normed)], refs[len(normed):len(normed) + nseg], refs[len(normed) + nseg:]
        for o_ref, runs in zip(o_refs, pieces):
            for src, dst, n in runs:
                o_ref[:, dst:dst + n] = p_ref[:, src:src + n].astype(o_ref.dtype)
        for g_ref, n_ref, i in zip(g_refs, n_refs, normed):
            for c0 in range(0, splits[i], HEAD_DIM):
                cols = slice(c0, c0 + HEAD_DIM)
                n_ref[:, cols] = _head_norm(o_refs[i][:, cols], g_ref[:, cols]).astype(n_ref.dtype)

    return pl.pallas_call(
        body, name="proj_split", grid=(s // tm,),
        in_specs=[pl.BlockSpec((tm, wp), lambda i: (i, 0))] + [pl.BlockSpec((1, splits[i]), lambda i: (0, 0)) for i in normed],
        out_specs=[pl.BlockSpec((tm, w), lambda i: (i, 0)) for w in splits]
        + [pl.BlockSpec((tm, splits[i]), lambda i: (i, 0)) for i in normed],
        out_shape=[jax.ShapeDtypeStruct((s, w), dt) for w, dt in zip(splits, dtypes)]
        + [jax.ShapeDtypeStruct((s, splits[i]), BF16) for i in normed],
        compiler_params=_params("parallel"),
    )(proj_p, *[gains[i] for i in normed])


def _dproj_merge(parts, splits, cs, cp, norms, tm=128):
    s = parts[0].shape[0]
    wp = N_CHIPS * cp
    pieces = _pieces(splits, cs, cp)
    normed = sorted(norms)
    nseg, nn = len(splits), len(normed)

    def body(*refs):
        p_refs, x_refs, g_refs = refs[:nseg], refs[nseg:nseg + nn], refs[nseg + nn:nseg + 2 * nn]
        o_ref, dg_refs = refs[nseg + 2 * nn], refs[nseg + 2 * nn + 1:nseg + 3 * nn + 1]
        stage, tmp = refs[-2], refs[-1]

        @pl.when(pl.program_id(0) == 0)
        def _():
            for dg_ref in dg_refs:
                dg_ref[...] = jnp.zeros_like(dg_ref)

        for j in range(N_CHIPS):
            stage[:, j * cp + cs:(j + 1) * cp] = jnp.zeros((tm, cp - cs), F32)
        for i, (p_ref, runs) in enumerate(zip(p_refs, pieces)):
            src_ref = p_ref
            if i in norms:
                k = normed.index(i)
                for c0 in range(0, splits[i], HEAD_DIM):
                    cols = slice(c0, c0 + HEAD_DIM)
                    dx, dg = _head_norm_bwd(p_ref[:, cols].astype(F32), x_refs[k][:, cols], g_refs[k][:, cols])
                    tmp[:, cols] = dx
                    dg_refs[k][:, cols] += dg
                src_ref = tmp
            for dst, src, n in runs:
                stage[:, dst:dst + n] = src_ref[:, src:src + n].astype(F32)
        o_ref[...] = stage[...].astype(o_ref.dtype)

    wmax = max(splits[i] for i in normed)
    row = lambda w: pl.BlockSpec((tm, w), lambda i: (i, 0))
    vec = lambda w: pl.BlockSpec((1, w), lambda i: (0, 0))
    outs = pl.pallas_call(
        body, name="dproj_merge", grid=(s // tm,),
        in_specs=[row(w) for w in splits] + [row(splits[i]) for i in normed] + [vec(splits[i]) for i in normed],
        out_specs=[row(wp)] + [vec(splits[i]) for i in normed],
        out_shape=[jax.ShapeDtypeStruct((s, wp), BF16)] + [jax.ShapeDtypeStruct((1, splits[i]), F32) for i in normed],
        scratch_shapes=[pltpu.VMEM((tm, wp), F32), pltpu.VMEM((tm, wmax), F32)],
        compiler_params=_params("arbitrary"),
    )(*parts, *[norms[i][0] for i in normed], *[norms[i][1] for i in normed])
    return outs[0], dict(zip(normed, outs[1:]))


def _norm_fwd(x, g, *, group, name, tm=256):
    s, w = x.shape
    ng = w // group

    def body(x_ref, g_ref, o_ref):
        for i in range(ng):
            cols = slice(i * group, (i + 1) * group)
            xv = x_ref[:, cols]
            r = lax.rsqrt(jnp.mean(xv * xv, axis=-1, keepdims=True) + EPS)
            o_ref[:, cols] = ((xv * r) * g_ref[:, cols]).astype(o_ref.dtype)

    return pl.pallas_call(
        body, name=name, grid=(s // tm,),
        in_specs=[pl.BlockSpec((tm, w), lambda i: (i, 0)), pl.BlockSpec((1, w), lambda i: (0, 0))],
        out_specs=pl.BlockSpec((tm, w), lambda i: (i, 0)),
        out_shape=jax.ShapeDtypeStruct((s, w), BF16),
        compiler_params=_params("parallel"),
    )(x, g)


def _norm_bwd(dy, x, g, *, group, name, res=None, out_dtypes=(BF16,), tm=256, deps=()):
    s, w = x.shape
    ng = w // group
    n_in = 4 if res is not None else 3

    def body(*refs):
        dy_ref, x_ref, g_ref = refs[:3]
        res_ref = refs[3] if res is not None else None
        outs = refs[n_in + len(deps):]
        dx_refs, dg_ref = outs[:-1], outs[-1]

        @pl.when(pl.program_id(0) == 0)
        def _():
            dg_ref[...] = jnp.zeros_like(dg_ref)

        for i in range(ng):
            cols = slice(i * group, (i + 1) * group)
            xv = x_ref[:, cols]
            dyv = dy_ref[:, cols].astype(F32)
            r = lax.rsqrt(jnp.mean(xv * xv, axis=-1, keepdims=True) + EPS)
            xr = xv * r
            dg_ref[:, cols] += jnp.sum(dyv * xr, axis=0, keepdims=True)
            gdy = dyv * g_ref[:, cols]
            dx = r * (gdy - xr * jnp.mean(gdy * xr, axis=-1, keepdims=True))
            if res_ref is not None:
                dx = dx + res_ref[:, cols]
            for dx_ref in dx_refs:
                dx_ref[:, cols] = dx.astype(dx_ref.dtype)

    row = pl.BlockSpec((tm, w), lambda i: (i, 0))
    vec = pl.BlockSpec((1, w), lambda i: (0, 0))
    in_specs = [row, row, vec] + ([row] if res is not None else []) + [pl.BlockSpec(memory_space=pl.ANY)] * len(deps)
    args = [dy, x, g] + ([res] if res is not None else []) + list(deps)
    out_specs = [row] * len(out_dtypes) + [vec]
    out_shape = [jax.ShapeDtypeStruct((s, w), dt) for dt in out_dtypes] + [jax.ShapeDtypeStruct((1, w), F32)]
    return pl.pallas_call(
        body, name=name, grid=(s // tm,), in_specs=in_specs, out_specs=out_specs,
        out_shape=out_shape, compiler_params=_params("arbitrary"),
    )(*args)


def _split3(v):
    p1 = v.astype(BF16)
    r1 = v - p1.astype(F32)
    p2 = r1.astype(BF16)
    p3 = (r1 - p2.astype(F32)).astype(BF16)
    return p1, p2, p3


def _tri_sum(v, reverse, tcol=512):
    h, s = v.shape
    tcol = min(tcol, s)
    parts = _split3(v)
    outs = []
    for j in range(s // tcol):
        src = lax.broadcasted_iota(jnp.int32, (s, tcol), 0)
        dst = lax.broadcasted_iota(jnp.int32, (s, tcol), 1) + j * tcol
        keep = (src >= dst) if reverse else (src <= dst)
        tri = jnp.where(keep, 1.0, 0.0).astype(BF16)
        acc = jnp.zeros((h, tcol), F32)
        for p in parts:
            acc = acc + jnp.dot(p, tri, preferred_element_type=F32)
        outs.append(acc)
    return outs


def _forget_fwd(fa_t, b):
    h, s = fa_t.shape
    tcol = min(512, s)

    def body(f_ref, b_ref, c_ref):
        z = f_ref[...] + b_ref[...]
        logf = jnp.minimum(z, 0.0) - jnp.log(1.0 + jnp.exp(-jnp.abs(z)))
        for j, blk in enumerate(_tri_sum(logf, reverse=False, tcol=tcol)):
            c_ref[:, j * tcol:(j + 1) * tcol] = blk

    return pl.pallas_call(
        body, name="forget_fwd", out_shape=jax.ShapeDtypeStruct((h, s), F32),
        compiler_params=_params(),
    )(fa_t, b)


def _forget_bwd(dacol, fa_t, b):
    h, s = fa_t.shape
    tcol = min(512, s)

    def body(d_ref, f_ref, b_ref, dfa_ref, db_ref):
        z = f_ref[...] + b_ref[...]
        dc = -d_ref[...]
        total = jnp.zeros((h, 1), F32)
        for j, blk in enumerate(_tri_sum(dc, reverse=True, tcol=tcol)):
            cols = slice(j * tcol, (j + 1) * tcol)
            dfa = blk * (1.0 - jax.nn.sigmoid(z[:, cols]))
            dfa_ref[:, cols] = dfa
            total = total + jnp.sum(dfa, axis=-1, keepdims=True)
        db_ref[...] = total

    return pl.pallas_call(
        body, name="forget_bwd",
        out_shape=[jax.ShapeDtypeStruct((h, s), F32), jax.ShapeDtypeStruct((h, 1), F32)],
        compiler_params=_params(),
    )(dacol, fa_t, b)


def _distance_bias(s, tile, dilated):
    nb = s // tile
    b = lax.broadcasted_iota(jnp.int32, (nb, tile, tile), 0)
    dist = b * tile + lax.broadcasted_iota(jnp.int32, (nb, tile, tile), 1) - lax.broadcasted_iota(jnp.int32, (nb, tile, tile), 2)
    if not dilated:
        return jnp.where(dist >= 0, 0.0, NEG).astype(F32)
    mult = jnp.zeros(dist.shape, jnp.int32)
    for window, dil in DIL_PATTERNS:
        mult = mult + ((dist >= 0) & (dist <= window) & ((dist & (dil - 1)) == 0)).astype(jnp.int32)
    logm = jnp.where(mult == 3, math.log(3.0), jnp.where(mult == 2, math.log(2.0), 0.0))
    return jnp.where(mult > 0, logm, NEG).astype(F32)


def _logits(q, k, arow, acol, bias):
    s = lax.dot_general(q, k, (((1,), (1,)), ((), ())), preferred_element_type=F32)
    return s * (1.0 / math.sqrt(HEAD_DIM)) + arow - acol + bias


def _attn_fwd(q, k, v, arow, acol, *, dilated, name, tq=ATTN_TILE, tk=ATTN_TILE):
    two_term = not dilated
    s, w = q.shape
    nh = w // HEAD_DIM
    assert tq == tk
    tq = tk = min(tq, s)
    nq, nk = s // tq, s // tk

    def body(q_ref, k_ref, v_ref, ar_ref, ac_ref, b_ref, o_ref, of_ref, lse_ref, m_ref, l_ref, acc_ref):
        qi, ki = pl.program_id(1), pl.program_id(2)

        @pl.when(ki == 0)
        def _():
            m_ref[...] = jnp.full_like(m_ref, NEG)
            l_ref[...] = jnp.zeros_like(l_ref)
            acc_ref[...] = jnp.zeros_like(acc_ref)

        @pl.when(ki <= qi)
        def _():
            sc = _logits(q_ref[...], k_ref[...], ar_ref[...], ac_ref[...], b_ref[...])
            m_new = jnp.maximum(m_ref[...], jnp.max(sc, axis=-1, keepdims=True))
            alpha = jnp.exp(m_ref[...] - m_new)
            p = jnp.exp(sc - m_new)
            l_ref[...] = alpha * l_ref[...] + jnp.sum(p, axis=-1, keepdims=True)
            p_hi = p.astype(BF16)
            vv = v_ref[...]
            pv = jnp.dot(p_hi, vv, preferred_element_type=F32)
            if two_term:
                pv = pv + jnp.dot((p - p_hi.astype(F32)).astype(BF16), vv, preferred_element_type=F32)
            acc_ref[...] = alpha * acc_ref[...] + pv
            m_ref[...] = m_new

        @pl.when(ki == nk - 1)
        def _():
            out = acc_ref[...] / l_ref[...]
            o_ref[...] = out.astype(o_ref.dtype)
            of_ref[...] = out
            lse_ref[...] = m_ref[...] + jnp.log(l_ref[...])

    kv = pl.BlockSpec((tk, HEAD_DIM), lambda h, i, j: (jnp.minimum(j, i), h))
    return pl.pallas_call(
        body, name=name, grid=(nh, nq, nk),
        in_specs=[pl.BlockSpec((tq, HEAD_DIM), lambda h, i, j: (i, h)), kv, kv,
                  pl.BlockSpec((None, tq, 1), lambda h, i, j: (h, i, 0)),
                  pl.BlockSpec((None, 1, tk), lambda h, i, j: (h, 0, jnp.minimum(j, i))),
                  pl.BlockSpec((None, tq, tk), lambda h, i, j: (jnp.maximum(i - j, 0), 0, 0))],
        out_specs=[pl.BlockSpec((tq, HEAD_DIM), lambda h, i, j: (i, h)),
                   pl.BlockSpec((tq, HEAD_DIM), lambda h, i, j: (i, h)),
                   pl.BlockSpec((None, tq, 1), lambda h, i, j: (h, i, 0))],
        out_shape=[jax.ShapeDtypeStruct((s, w), BF16), jax.ShapeDtypeStruct((s, w), F32),
                   jax.ShapeDtypeStruct((nh, s, 1), F32)],
        scratch_shapes=[pltpu.VMEM((tq, 1), F32), pltpu.VMEM((tq, 1), F32), pltpu.VMEM((tq, HEAD_DIM), F32)],
        compiler_params=_params("parallel", "parallel", "arbitrary"),
    )(q, k, v, arow, acol, _distance_bias(s, tq, dilated))


def _attn_bwd(q, k, v, o, do, lse, arow, acol, *, dilated, name, tq=ATTN_TILE, tk=ATTN_TILE):
    s, w = q.shape
    nh = w // HEAD_DIM
    assert tq == tk
    tq = tk = min(tq, s)
    nq, nk = s // tq, s // tk
    scale = 1.0 / math.sqrt(HEAD_DIM)

    def body(q_ref, k_ref, v_ref, o_ref, do_ref, lse_ref, ar_ref, ac_ref, b_ref,
             dq_ref, dk_ref, dv_ref, dac_ref, dk_acc, dv_acc, dac_acc):
        ki, qi = pl.program_id(1), pl.program_id(2)

        @pl.when((ki == 0) & (qi == 0))
        def _():
            dq_ref[...] = jnp.zeros_like(dq_ref)

        @pl.when(qi == 0)
        def _():
            dk_acc[...] = jnp.zeros_like(dk_acc)
            dv_acc[...] = jnp.zeros_like(dv_acc)
            dac_acc[...] = jnp.zeros_like(dac_acc)

        @pl.when(qi >= ki)
        def _():
            qv, kvv, dov = q_ref[...], k_ref[...], do_ref[...]
            sc = _logits(qv, kvv, ar_ref[...], ac_ref[...], b_ref[...])
            p = jnp.exp(sc - lse_ref[...])
            dp = lax.dot_general(dov, v_ref[...], (((1,), (1,)), ((), ())), preferred_element_type=F32)
            delta = jnp.sum(dov.astype(F32) * o_ref[...].astype(F32), axis=-1, keepdims=True)
            ds = p * (dp - delta)
            dsb = ds.astype(BF16)
            dv_acc[...] += lax.dot_general(p.astype(BF16), dov, (((0,), (0,)), ((), ())), preferred_element_type=F32)
            dk_acc[...] += lax.dot_general(dsb, qv, (((0,), (0,)), ((), ())), preferred_element_type=F32)
            rows = pl.ds(pl.multiple_of(qi * tq, tq), tq)
            dq_ref[rows, :] += jnp.dot(dsb, kvv, preferred_element_type=F32) * scale
            dac_acc[...] += jnp.sum(ds, axis=0, keepdims=True)

        @pl.when(qi == nq - 1)
        def _():
            dk_ref[...] = dk_acc[...] * scale
            dv_ref[...] = dv_acc[...]
            dac_ref[...] = dac_acc[...]

    qs = pl.BlockSpec((tq, HEAD_DIM), lambda h, j, i: (jnp.maximum(i, j), h))
    ks = pl.BlockSpec((tk, HEAD_DIM), lambda h, j, i: (j, h))
    rowv = pl.BlockSpec((None, tq, 1), lambda h, j, i: (h, jnp.maximum(i, j), 0))
    colv = pl.BlockSpec((None, 1, tk), lambda h, j, i: (h, 0, j))
    return pl.pallas_call(
        body, name=name, grid=(nh, nk, nq),
        in_specs=[qs, ks, ks, qs, qs, rowv, rowv, colv,
                  pl.BlockSpec((None, tq, tk), lambda h, j, i: (jnp.maximum(i - j, 0), 0, 0))],
        out_specs=[pl.BlockSpec((s, HEAD_DIM), lambda h, j, i: (0, h)), ks, ks, colv],
        out_shape=[jax.ShapeDtypeStruct((s, w), F32), jax.ShapeDtypeStruct((s, w), F32),
                   jax.ShapeDtypeStruct((s, w), F32), jax.ShapeDtypeStruct((nh, 1, s), F32)],
        scratch_shapes=[pltpu.VMEM((tk, HEAD_DIM), F32), pltpu.VMEM((tk, HEAD_DIM), F32), pltpu.VMEM((1, tk), F32)],
        compiler_params=_params("arbitrary", "arbitrary", "arbitrary"),
    )(q, k, v, o, do, lse, arow, acol, _distance_bias(s, tq, dilated))


def _gate_fwd(ga, gb, pa, pb, tm=256):
    s, d = ga.shape

    def body(ga_ref, gb_ref, pa_ref, pb_ref, o_ref):
        o_ref[...] = (jax.nn.sigmoid(ga_ref[...]) * pa_ref[...]
                      + jax.nn.sigmoid(gb_ref[...]) * pb_ref[...]).astype(o_ref.dtype)

    row = pl.BlockSpec((tm, d), lambda i: (i, 0))
    return pl.pallas_call(
        body, name="gate_fwd", grid=(s // tm,), in_specs=[row] * 4, out_specs=row,
        out_shape=jax.ShapeDtypeStruct((s, d), BF16), compiler_params=_params("parallel"),
    )(ga, gb, pa, pb)


def _gate_bwd(dm, ga, gb, pa, pb, tm=256):
    s, d = ga.shape

    def body(dm_ref, ga_ref, gb_ref, pa_ref, pb_ref, dpa_ref, dpb_ref, dga_ref, dgb_ref):
        dmv = dm_ref[...]
        for g_ref, p_ref, dp_ref, dg_ref in ((ga_ref, pa_ref, dpa_ref, dga_ref), (gb_ref, pb_ref, dpb_ref, dgb_ref)):
            sg = jax.nn.sigmoid(g_ref[...])
            dp_ref[...] = (dmv * sg).astype(BF16)
            dg_ref[...] = (dmv * p_ref[...] * (sg * (1.0 - sg))).astype(BF16)

    row = pl.BlockSpec((tm, d), lambda i: (i, 0))
    return pl.pallas_call(
        body, name="gate_bwd", grid=(s // tm,), in_specs=[row] * 5, out_specs=[row] * 4,
        out_shape=[jax.ShapeDtypeStruct((s, d), BF16)] * 4, compiler_params=_params("parallel"),
    )(dm, ga, gb, pa, pb)


def _shift_down(u, k):
    row = lax.broadcasted_iota(jnp.int32, u.shape, 0)
    return jnp.where(row >= k, pltpu.roll(u, k, 0), 0.0)


def _shift_up(u, k):
    n = u.shape[0]
    row = lax.broadcasted_iota(jnp.int32, u.shape, 0)
    return jnp.where(row < n - k, pltpu.roll(u, n - k, 0), 0.0)


def _conv3(u, wc, b):
    return wc[0:1, :] * _shift_down(u, 2) + wc[1:2, :] * _shift_down(u, 1) + wc[2:3, :] * u + b


def _conv_glu_fwd(u, wc, b, tn=256):
    s, f2 = u.shape
    f = f2 // 2
    nb = f // tn

    def body(ug_ref, uv_ref, wg_ref, wv_ref, bg_ref, bv_ref, o_ref):
        cg = _conv3(ug_ref[...], wg_ref[...], bg_ref[...])
        cv = _conv3(uv_ref[...], wv_ref[...], bv_ref[...])
        o_ref[...] = (cg * jax.nn.sigmoid(cg) * cv).astype(o_ref.dtype)

    def cols(rows, off):
        return pl.BlockSpec((rows, tn), lambda j: (0, j + off))

    return pl.pallas_call(
        body, name="conv_glu_fwd", grid=(nb,),
        in_specs=[cols(s, 0), cols(s, nb), cols(3, 0), cols(3, nb), cols(1, 0), cols(1, nb)],
        out_specs=cols(s, 0), out_shape=jax.ShapeDtypeStruct((s, f), BF16),
        compiler_params=_params("parallel"),
    )(u, u, wc, wc, b, b)


def _conv_glu_bwd(u, da, wc, b, tn=256):
    s, f2 = u.shape
    f = f2 // 2
    nb = f // tn

    def body(ug_ref, uv_ref, da_ref, wg_ref, wv_ref, bg_ref, bv_ref, du2_ref, sg_ref, sv_ref):
        dug_ref, duv_ref = du2_ref.at[0], du2_ref.at[1]
        ug, uv, wg, wv = ug_ref[...], uv_ref[...], wg_ref[...], wv_ref[...]
        cg = _conv3(ug, wg, bg_ref[...])
        cv = _conv3(uv, wv, bv_ref[...])
        sig = jax.nn.sigmoid(cg)
        dav = da_ref[...]
        dcv = dav * (cg * sig)
        dcg = dav * cv * (sig * (1.0 + cg * (1.0 - sig)))
        for dc, uu, w, du_ref, st_ref in ((dcg, ug, wg, dug_ref, sg_ref), (dcv, uv, wv, duv_ref, sv_ref)):
            du = w[2:3, :] * dc + w[1:2, :] * _shift_up(dc, 1) + w[0:1, :] * _shift_up(dc, 2)
            du_ref[...] = du.astype(BF16)
            st_ref[...] = jnp.zeros_like(st_ref)
            st_ref[0:1, :] = jnp.sum(dc * _shift_down(uu, 2), axis=0, keepdims=True)
            st_ref[1:2, :] = jnp.sum(dc * _shift_down(uu, 1), axis=0, keepdims=True)
            st_ref[2:3, :] = jnp.sum(dc * uu, axis=0, keepdims=True)
            st_ref[3:4, :] = jnp.sum(dc, axis=0, keepdims=True)

    def cols(rows, off):
        return pl.BlockSpec((rows, tn), lambda j: (0, j + off))

    return pl.pallas_call(
        body, name="conv_glu_bwd", grid=(nb,),
        in_specs=[cols(s, 0), cols(s, nb), cols(s, 0), cols(3, 0), cols(3, nb), cols(1, 0), cols(1, nb)],
        out_specs=[pl.BlockSpec((2, s, tn), lambda j: (0, 0, j)), cols(8, 0), cols(8, 0)],
        out_shape=[jax.ShapeDtypeStruct((2, s, f), BF16),
                   jax.ShapeDtypeStruct((8, f), F32), jax.ShapeDtypeStruct((8, f), F32)],
        compiler_params=_params("parallel"),
    )(u, u, da, wc, wc, b, b)


def _loss_head(y, target, tm=256):
    s, d = y.shape

    def body(y_ref, t_ref, dyf_ref, dyb_ref, l_ref):
        @pl.when(pl.program_id(0) == 0)
        def _():
            l_ref[...] = jnp.zeros_like(l_ref)

        err = y_ref[...] - t_ref[...]
        dy = err * (1.0 / d)
        dyf_ref[...] = dy
        dyb_ref[...] = dy.astype(BF16)
        l_ref[...] += 0.5 * jnp.sum(jnp.sum(err * err, axis=-1, keepdims=True) * (1.0 / d), axis=0, keepdims=True)

    row = pl.BlockSpec((tm, d), lambda i: (i, 0))
    return pl.pallas_call(
        body, name="loss_head", grid=(s // tm,), in_specs=[row, row],
        out_specs=[row, row, pl.BlockSpec((8, LANES), lambda i: (0, 0))],
        out_shape=[jax.ShapeDtypeStruct((s, d), F32), jax.ShapeDtypeStruct((s, d), BF16),
                   jax.ShapeDtypeStruct((8, LANES), F32)],
        compiler_params=_params("arbitrary"),
    )(y, target)


ROW_TILES = (256, 128, 64, 32, 16, 8)
BLOCK_BYTES = 2 << 20


def _add_halves(g, r1, place):
    ns, r, c = g.shape
    rh = r // 2
    tr = _pick(rh, ROW_TILES)
    g4 = g.reshape(ns, 2, rh, c)

    def body(p_ref, g_ref, r_ref, o_ref):
        o_ref[...] = (g_ref[...].astype(F32) + r_ref[...].astype(F32)).astype(o_ref.dtype)

    def slab(s, pr):
        return s + (s >= pr[0]).astype(jnp.int32)

    return pl.pallas_call(
        body, name="add_halves",
        grid_spec=pltpu.PrefetchScalarGridSpec(
            num_scalar_prefetch=1, grid=(ns - 1, rh // tr),
            in_specs=[pl.BlockSpec((None, None, tr, c), lambda s, i, pr: (slab(s, pr), pr[1], i, 0)),
                      pl.BlockSpec((None, tr, c), lambda s, i, pr: (slab(s, pr), i, 0))],
            out_specs=pl.BlockSpec((None, tr, c), lambda s, i, pr: (slab(s, pr), i, 0))),
        out_shape=jax.ShapeDtypeStruct((ns, rh, c), BF16),
        compiler_params=_params("parallel", "parallel"),
    )(place, g4, r1)


def _sum_chips(g, r1, recv, place):
    ns, r, c = g.shape
    rh = r // 2
    tr = _pick(rh, ROW_TILES)
    g4 = g.reshape(ns, 2, rh, c)

    def body(p_ref, g_ref, r_ref, t0_ref, t1_ref, t2_ref, o_ref):
        own = g_ref[...].astype(F32) + r_ref[...].astype(F32)
        o_ref[...] = ((own + t0_ref[...].astype(F32)) + t1_ref[...].astype(F32)) + t2_ref[...].astype(F32)

    def peer(k):
        return pl.BlockSpec((None, tr, c), lambda i, pr: (k, i, 0))

    return pl.pallas_call(
        body, name="sum_chips",
        grid_spec=pltpu.PrefetchScalarGridSpec(
            num_scalar_prefetch=1, grid=(rh // tr,),
            in_specs=[pl.BlockSpec((None, None, tr, c), lambda i, pr: (pr[0], pr[1], i, 0)),
                      pl.BlockSpec((None, tr, c), lambda i, pr: (pr[0], i, 0)), peer(0), peer(1), peer(2)],
            out_specs=pl.BlockSpec((tr, c), lambda i, pr: (pr[1] * (rh // tr) + i, 0))),
        out_shape=jax.ShapeDtypeStruct((r, c), F32),
        compiler_params=_params("parallel"),
    )(place, g4, r1, recv, recv, recv)


def _sum_devices(packs):
    n, r, c = packs.shape

    def body(p_ref, o_ref):
        acc = p_ref[0]
        for d in range(1, n):
            acc = acc + p_ref[d]
        o_ref[...] = acc

    return pl.pallas_call(
        body, name="sum_devices", out_shape=jax.ShapeDtypeStruct((r, c), F32), compiler_params=_params(),
    )(packs)


def _adamw_update(wv, gv, mv, vv):
    c1 = 1.0 - ADAM_B1 ** ADAM_STEP
    c2 = 1.0 - ADAM_B2 ** ADAM_STEP
    mn = ADAM_B1 * mv + (1.0 - ADAM_B1) * gv
    vn = ADAM_B2 * vv + (1.0 - ADAM_B2) * (gv * gv)
    m_hat = mn / c1
    v_hat = vn / c2
    return -ADAM_LR * (m_hat / (jnp.sqrt(v_hat) + ADAM_EPS) + ADAM_WD * wv), mn, vn


def _adamw(w, g, m, v, name, deps=(), emit_grad=False):
    r, c = w.shape
    tr = _pick(r, [t for t in ROW_TILES if t * c * 4 <= BLOCK_BYTES]) if r >= 8 else r
    n_out = 4 if emit_grad else 3

    def body(w_ref, g_ref, m_ref, v_ref, *rest):
        outs = rest[-n_out:]
        gv = g_ref[:, :c]
        if emit_grad:
            outs[0][...] = gv
        outs[-3][...], outs[-2][...], outs[-1][...] = _adamw_update(w_ref[...], gv, m_ref[...], v_ref[...])

    blk = pl.BlockSpec((tr, c), lambda i: (i, 0))
    g_blk = pl.BlockSpec((tr, g.shape[1]), lambda i: (i, 0))
    return pl.pallas_call(
        body, name=name, grid=(r // tr,), in_specs=[blk, g_blk, blk, blk] + [ANY] * len(deps), out_specs=[blk] * n_out,
        out_shape=[jax.ShapeDtypeStruct((r, c), F32)] * n_out, compiler_params=_params("parallel"),
    )(w, g, m, v, *deps)


ANY = pl.BlockSpec(memory_space=pl.ANY)


def _place():
    x, y, c = lax.axis_index("x"), lax.axis_index("y"), lax.axis_index("c")
    chips = [(1 - x, y), (x, 1 - y), (1 - x, 1 - y)]
    return x, y, c, chips


def _remote(src, dst, send_sem, recv_sem, to):
    return pltpu.make_async_remote_copy(src_ref=src, dst_ref=dst, send_sem=send_sem, recv_sem=recv_sem,
                                        device_id=to, device_id_type=MESH)


HBM = pl.BlockSpec(memory_space=pltpu.HBM)
SEM = pl.BlockSpec(memory_space=pltpu.SEMAPHORE)
EFFECT = pltpu.SideEffectType.DATAFLOW_SIDE_EFFECTING


def _in_hbm(a):
    return pltpu.with_memory_space_constraint(a, pltpu.HBM)


def _half(ref_rows, who):
    return pl.ds(who * (ref_rows // 2), ref_rows // 2)


def _gather_start(groups):
    items = [it for g in groups for it in g]
    n = len(items)
    sizes = [len(g) for g in groups]

    def body(*refs):
        srcs, lands = refs[:n], refs[n:2 * n]
        sems = refs[2 * n:2 * n + 2 * len(groups)]
        token = refs[-1]
        x, y, c, chips = _place()
        j = 2 * x + y
        at = 0
        for gi, g in enumerate(groups):
            send, recv = sems[2 * gi], sems[2 * gi + 1]
            for i, (shard, split) in enumerate(g):
                src, land = srcs[at], lands[at]
                at += 1
                rows = _half(shard.shape[0], c) if split else slice(None)
                for k, chip in enumerate(chips):
                    _remote(src.at[rows], land.at[j, rows], send.at[4 * i + k], recv.at[4 * i + k], (*chip, c)).start()
                _remote(src, land.at[j], send.at[4 * i + 3], recv.at[4 * i + 3], (x, y, 1 - c)).start()
        token[...] = jnp.zeros_like(token)

    sem_shapes = []
    for sz in sizes:
        sem_shapes += [pltpu.SemaphoreType.DMA((4 * sz,)), pltpu.SemaphoreType.DMA((4 * sz,))]
    out_shape = (sem_shapes + [pltpu.HBM(sh.shape, sh.dtype) for sh, _ in items]
                 + [pltpu.HBM((N_CHIPS,) + sh.shape, sh.dtype) for sh, _ in items]
                 + [jax.ShapeDtypeStruct((8, LANES), F32)])
    ns = len(sem_shapes)
    outs = pl.pallas_call(
        body, name="gather_start", in_specs=[HBM] * (2 * n),
        out_specs=[SEM] * ns + [HBM] * (2 * n) + [pl.BlockSpec(memory_space=pltpu.VMEM)],
        out_shape=out_shape, input_output_aliases={i: ns + i for i in range(2 * n)},
        compiler_params=pltpu.CompilerParams(has_side_effects=EFFECT),
    )(*[_in_hbm(sh) for sh, _ in items], *[_in_hbm(lax.empty((N_CHIPS,) + sh.shape, sh.dtype)) for sh, _ in items])
    sems, shards, lands, token = outs[:ns], outs[ns:ns + n], outs[ns + n:ns + 2 * n], outs[-1]
    res, at = [], 0
    for gi, sz in enumerate(sizes):
        res.append((shards[at:at + sz], lands[at:at + sz], sems[2 * gi], sems[2 * gi + 1]))
        at += sz
    return res, token


def _gather_pass(group, started, after, name):
    shards, lands, send, recv = started
    n = len(group)
    split_ix = [i for i, (_, split) in enumerate(group) if split]

    def body(*refs):
        lnds, send1, recv1 = refs[n:2 * n], refs[2 * n], refs[2 * n + 1]
        outs = refs[2 * n + 2 + len(after):]
        send2, recv2, token = outs[2 * n], outs[2 * n + 1], outs[2 * n + 2]
        x, y, c, chips = _place()
        sib = (x, y, 1 - c)
        for i, (shard, split) in enumerate(group):
            rows = _half(shard.shape[0], c) if split else slice(None)
            for k, (cx, cy) in enumerate(chips):
                landed = lnds[i].at[2 * cx + cy, rows]
                cp = _remote(landed, landed, send1.at[4 * i + k], recv1.at[4 * i + k], sib)
                cp.wait_send()
                cp.wait_recv()
            own = lnds[i].at[2 * x + y]
            cp = _remote(own, own, send1.at[4 * i + 3], recv1.at[4 * i + 3], sib)
            cp.wait_send()
            cp.wait_recv()
        for i2, i in enumerate(split_ix):
            rows = _half(group[i][0].shape[0], c)
            for k, (cx, cy) in enumerate(chips):
                landed = lnds[i].at[2 * cx + cy, rows]
                _remote(landed, landed, send2.at[3 * i2 + k], recv2.at[3 * i2 + k], sib).start()
        token[...] = jnp.zeros_like(token)

    n2 = len(split_ix)
    out_shape = ([pltpu.HBM(a.shape, a.dtype) for a in (*shards, *lands)]
                 + [pltpu.SemaphoreType.DMA((3 * n2,)), pltpu.SemaphoreType.DMA((3 * n2,)), jax.ShapeDtypeStruct((8, LANES), F32)])
    outs = pl.pallas_call(
        body, name=name, in_specs=[HBM] * (2 * n) + [SEM, SEM] + [ANY] * len(after),
        out_specs=[HBM] * (2 * n) + [SEM, SEM, pl.BlockSpec(memory_space=pltpu.VMEM)],
        out_shape=out_shape, input_output_aliases={i: i for i in range(2 * n)},
        compiler_params=pltpu.CompilerParams(has_side_effects=EFFECT),
    )(*shards, *lands, send, recv, *after)
    return outs[:n], (outs[n:2 * n], outs[2 * n], outs[2 * n + 1]), outs[2 * n + 2]


def _gather_wait(group, passed, after, name):
    lands, send2, recv2 = passed
    n = len(group)
    split_ix = [i for i, (_, split) in enumerate(group) if split]

    def body(*refs):
        lnds, s2, r2 = refs[:n], refs[n], refs[n + 1]
        x, y, c, chips = _place()
        sib = (x, y, 1 - c)
        for i2, i in enumerate(split_ix):
            rows = _half(group[i][0].shape[0], 1 - c)
            for k, (cx, cy) in enumerate(chips):
                landed = lnds[i].at[2 * cx + cy, rows]
                cp = _remote(landed, landed, s2.at[3 * i2 + k], r2.at[3 * i2 + k], sib)
                cp.wait_send()
                cp.wait_recv()

    return pl.pallas_call(
        body, name=name, in_specs=[HBM] * n + [SEM, SEM, ANY], out_specs=[HBM] * n,
        out_shape=[pltpu.HBM(a.shape, a.dtype) for a in lands], input_output_aliases={i: i for i in range(n)},
        compiler_params=pltpu.CompilerParams(has_side_effects=EFFECT),
    )(*lands, send2, recv2, after)


def _xfer_start(name, srcs, land_shapes, n_copies, copies, after):
    n, nl = len(srcs), len(land_shapes)

    def body(*refs):
        src_refs, land_refs = refs[:n], refs[n:n + nl]
        send, recv, token = refs[n + nl + 1], refs[n + nl + 2], refs[-1]
        for cp in copies(src_refs, land_refs, send, recv):
            cp.start()
        token[...] = jnp.zeros_like(token)

    lands = [_in_hbm(lax.empty(shape, dtype)) for shape, dtype in land_shapes]
    out_shape = ([pltpu.SemaphoreType.DMA((n_copies,)), pltpu.SemaphoreType.DMA((n_copies,))]
                 + [pltpu.HBM(a.shape, a.dtype) for a in (*srcs, *lands)] + [jax.ShapeDtypeStruct((8, LANES), F32)])
    outs = pl.pallas_call(
        body, name=name, in_specs=[HBM] * (n + nl) + [ANY],
        out_specs=[SEM, SEM] + [HBM] * (n + nl) + [pl.BlockSpec(memory_space=pltpu.VMEM)],
        out_shape=out_shape, input_output_aliases={i: 2 + i for i in range(n + nl)},
        compiler_params=pltpu.CompilerParams(has_side_effects=EFFECT),
    )(*[_in_hbm(a) for a in srcs], *lands, after)
    return (outs[2:2 + n], outs[2 + n:2 + n + nl], outs[0], outs[1]), outs[-1]


def _xfer_wait(name, started, copies, after):
    srcs, lands, send, recv = started
    n, nl = len(srcs), len(lands)

    def body(*refs):
        src_refs, land_refs, s_ref, r_ref = refs[:n], refs[n:n + nl], refs[n + nl], refs[n + nl + 1]
        for cp in copies(src_refs, land_refs, s_ref, r_ref):
            cp.wait_send()
            cp.wait_recv()

    outs = pl.pallas_call(
        body, name=name, in_specs=[HBM] * (n + nl) + [SEM, SEM, ANY], out_specs=[HBM] * (n + nl),
        out_shape=[pltpu.HBM(a.shape, a.dtype) for a in (*srcs, *lands)],
        input_output_aliases={i: i for i in range(n + nl)},
        compiler_params=pltpu.CompilerParams(has_side_effects=EFFECT),
    )(*srcs, *lands, send, recv, after)
    return outs[:n], outs[n:]


def _swap_copies(srcs, lands, send, recv):
    x, y, c, _ = _place()
    return [_remote(src.at[:, _half(src.shape[1], 1 - c)], land, send.at[i], recv.at[i], (x, y, 1 - c))
            for i, (src, land) in enumerate(zip(srcs, lands))]


def _scatter_copies(srcs, lands, send, recv):
    x, y, c, chips = _place()
    return [_remote(src.at[2 * cx + cy], land.at[k], send.at[3 * i + k], recv.at[3 * i + k], (cx, cy, c))
            for i, (src, land) in enumerate(zip(srcs, lands)) for k, (cx, cy) in enumerate(chips)]


def _join_copies(srcs, lands, send, recv):
    x, y, c, _ = _place()
    return [_remote(src.at[_half(src.shape[0], c)], src.at[_half(src.shape[0], c)], send.at[i], recv.at[i], (x, y, 1 - c))
            for i, src in enumerate(srcs)]


def _corner(a):
    return a[(slice(0, 1),) * a.ndim]


class _Reducer:
    def __init__(self, place):
        self.place = place
        self.state = {}

    def swap(self, key, grads, after):
        shapes = [((g.shape[0], g.shape[1] // 2, g.shape[2]), g.dtype) for g in grads]
        self.state[key], token = _xfer_start("swap_start_" + key, grads, shapes, len(grads), _swap_copies, _corner(after))
        return token

    def to_chips(self, key, after):
        grads, from_sibling = _xfer_wait("swap_wait_" + key, self.state[key], _swap_copies, after)
        sums = [_add_halves(g, r, self.place) for g, r in zip(grads, from_sibling)]
        shapes = [((3,) + s.shape[1:], s.dtype) for s in sums]
        started, token = _xfer_start("scatter_start_" + key, sums, shapes, 3 * len(sums), _scatter_copies, _corner(sums[-1]))
        self.state[key] = (grads, from_sibling, started)
        return token

    def to_core(self, key, after):
        grads, from_sibling, started = self.state[key]
        _, from_chips = _xfer_wait("scatter_wait_" + key, started, _scatter_copies, after)
        shards = [_sum_chips(g, r, rc, self.place) for g, r, rc in zip(grads, from_sibling, from_chips)]
        self.state[key], token = _xfer_start("join_start_" + key, shards, [], len(shards), _join_copies, _corner(shards[-1]))
        return token

    def finish(self, key, after):
        return _xfer_wait("join_wait_" + key, self.state.pop(key), _join_copies, after)[0]


def _gather_packs(pack, deps=()):
    def body(p_ref, *rest):
        o_ref, lsem, ssem, rsem = rest[-4:]
        x, y, c, _ = _place()
        me = 4 * x + 2 * y + c
        local = pltpu.make_async_copy(p_ref, o_ref.at[me], lsem)
        local.start()
        cps = []
        for k in range(1, N_DEV):
            fx, fy, fc = (k >> 2) & 1, (k >> 1) & 1, k & 1
            to = (x ^ fx, y ^ fy, c ^ fc)
            cps.append(_remote(p_ref, o_ref.at[me], ssem.at[k - 1], rsem.at[k - 1], to))
        for cp in cps:
            cp.start()
        for k in range(1, N_DEV):
            fx, fy, fc = (k >> 2) & 1, (k >> 1) & 1, k & 1
            src = o_ref.at[4 * (x ^ fx) + 2 * (y ^ fy) + (c ^ fc)]
            _remote(src, src, ssem.at[k - 1], rsem.at[k - 1], (x, y, c)).wait_recv()
        for cp in cps:
            cp.wait_send()
        local.wait()

    return pl.pallas_call(
        body, name="gather_packs", in_specs=[ANY] * (1 + len(deps)), out_specs=ANY,
        out_shape=jax.ShapeDtypeStruct((N_DEV,) + pack.shape, pack.dtype),
        scratch_shapes=[pltpu.SemaphoreType.DMA, pltpu.SemaphoreType.DMA((N_DEV - 1,)), pltpu.SemaphoreType.DMA((N_DEV - 1,))],
    )(pack, *deps)


LANE_TILES = (512, 896, 1408, 704, 384, 256, 128)


def _layer_grads(x, target, small, wg, rest_pass, rest_wait, red, filler):
    s, d = x.shape
    f = wg["conv"].shape[1] // 2
    w_att = N_HEADS * HEAD_DIM
    in_splits = (w_att, w_att, w_att, N_HEADS, w_att, w_att, w_att, d, d)
    in_cols = sum(in_splits)
    cs = in_cols // N_CHIPS
    cp = wg["in"].shape[2]
    tm = min(s, 1024)
    t_in = cp
    t_d = _pick(d, LANE_TILES)
    t_d2 = min(d, 1024)
    t_dq = _pick(d // N_CHIPS, LANE_TILES)
    t_w = _pick(w_att, LANE_TILES)
    t_up = 2 * f // N_CHIPS
    tm_wide = min(s, 512)
    t_fq = _pick(f // N_CHIPS, LANE_TILES)
    offs = np.cumsum(in_splits)[:-1].tolist()

    h1 = _norm_fwd(x, small["g_attn"], group=d, name="rms1_fwd")
    proj_p = _mm(h1, wg["in"], mode="nn", b_kind="col", tm=tm_wide, tn=t_in, tk=d, name="mm_in")
    gains = {n: small[n].reshape(1, w_att) for n in ("g_q_fox", "g_k_fox", "g_q_dil", "g_k_dil")}
    qa, ka, va_b, fa, qb, kb, vb_b, ga, gb, qa_n, ka_n, qb_n, kb_n = _proj_split(
        proj_p, in_splits, cs, (F32, F32, BF16, F32, F32, F32, BF16, F32, F32),
        {0: gains["g_q_fox"], 1: gains["g_k_fox"], 4: gains["g_q_dil"], 5: gains["g_k_dil"]})
    fa_t = fa.T
    b_f = small["b_forget"].reshape(N_HEADS, 1)
    c_f = _forget_fwd(fa_t, b_f)
    slopes = jnp.asarray(2.0 ** (-8.0 * np.arange(1, N_HEADS + 1) / N_HEADS), dtype=F32)
    a_d = -(slopes[:, None] * jnp.arange(s, dtype=F32)[None, :])
    rows_f, cols_f = c_f[:, :, None], c_f[:, None, :]
    rows_d, cols_d = a_d[:, :, None], a_d[:, None, :]
    o_a, o_a32, lse_a = _attn_fwd(qa_n, ka_n, va_b, rows_f, cols_f, dilated=False, name="attn_fox_fwd")
    token = rest_pass("mid", o_a)
    rows_d = rows_d + token[0, 0]
    o_b, o_b32, lse_b = _attn_fwd(qb_n, kb_n, vb_b, rows_d, cols_d, dilated=True, name="attn_dil_fwd")
    wg = dict(wg, **rest_wait("mid", o_b))
    token = rest_pass("late", o_b)
    pa = _mm(o_a, wg["brf"], mode="nn", b_kind="col", tm=tm, tn=t_dq, tk=w_att, name="mm_brf", deps=(token,))
    pb = _mm(o_b, wg["brd"], mode="nn", b_kind="col", tm=tm, tn=t_dq, tk=w_att, name="mm_brd")
    merged = _gate_fwd(ga, gb, pa, pb)
    x1 = _mm(merged, wg["out"], mode="nn", b_kind="row", res=x, tm=tm, tn=t_d, tk=t_dq, name="mm_out")
    wg = dict(wg, **rest_wait("late", x1))
    h2 = _norm_fwd(x1, small["g_ffn"], group=d, name="rms2_fwd")
    u = _mm(h2, wg["up"], mode="nn", b_kind="col", tm=tm_wide, tn=t_up, tk=d, name="mm_up")
    act = _conv_glu_fwd(u, wg["conv"], wg["bconv"])
    y = _mm(act, wg["down"], mode="nn", b_kind="row", res=x1, tm=tm, tn=t_d2, tk=t_fq, name="mm_down")
    dy_f, dy_b, loss_blk = _loss_head(y, target)

    d_act = _mm(dy_b, wg["down"], mode="nt", b_kind="row", tm=tm, tn=t_fq, tk=d, name="mm_down_dx")
    g_down = _mm(act, dy_b, mode="tn", out_dtype=BF16, out_kind="row", tm=t_fq, tn=t_d2, tk=s, name="mm_down_dw")
    tok = red.swap("down", [g_down], g_down)
    du, st_g, st_v = _conv_glu_bwd(u, d_act, wg["conv"] + tok[0, 0], wg["bconv"])
    tok = red.to_chips("down", du)
    g_up = _mm(h2, du, mode="tn", b_kind="col", out_dtype=BF16, out_kind="col", tm=t_d2, tn=t_up, tk=s,
               name="mm_up_dw", deps=(tok,))
    tok = red.to_core("down", g_up)
    tok2 = red.swap("up", [g_up], g_up)
    dh2 = _mm(du, wg["up"], mode="nt", a_kind="col", b_kind="col", tm=tm, tn=t_d2, tk=t_up, name="mm_up_dx",
              deps=(tok, tok2))
    tok = red.to_chips("up", dh2)
    dx1_b, dx1_f, dg_ffn = _norm_bwd(dh2, x1, small["g_ffn"], group=d, res=dy_f, out_dtypes=(BF16, F32), name="rms2_bwd")
    d_merged = _mm(dx1_b, wg["out"], mode="nt", b_kind="row", tm=tm, tn=t_dq, tk=d, name="mm_out_dx", deps=(tok,))
    g_out = _mm(merged, dx1_b, mode="tn", out_dtype=BF16, out_kind="row", tm=t_dq, tn=t_d2, tk=s, name="mm_out_dw")
    dpa, dpb, dga, dgb = _gate_bwd(d_merged, ga, gb, pa, pb)
    do_a = _mm(dpa, wg["brf"], mode="nt", b_kind="col", out_dtype=BF16, tm=s, tn=w_att, tk=t_dq, name="mm_brf_dx")
    do_b = _mm(dpb, wg["brd"], mode="nt", b_kind="col", out_dtype=BF16, tm=s, tn=w_att, tk=t_dq, name="mm_brd_dx")
    g_brf = _mm(o_a, dpa, mode="tn", out_dtype=BF16, out_kind="col", tm=w_att, tn=t_dq, tk=s, name="mm_brf_dw")
    g_brd = _mm(o_b, dpb, mode="tn", out_dtype=BF16, out_kind="col", tm=w_att, tn=t_dq, tk=s, name="mm_brd_dw")
    tok = red.swap("mix", [g_out, g_brf, g_brd], g_brd)
    dqa_n, dka_n, dva, dac_a = _attn_bwd(qa_n, ka_n, va_b, o_a32, do_a, lse_a, rows_f + tok[0, 0], cols_f, dilated=False, name="attn_fox_bwd")
    tok = red.to_core("up", dqa_n)
    tok2 = red.to_chips("mix", dqa_n)
    dqb_n, dkb_n, dvb, _ = _attn_bwd(qb_n, kb_n, vb_b, o_b32, do_b, lse_b, rows_d + (tok[0, 0] + tok2[0, 0]), cols_d, dilated=True, name="attn_dil_bwd")
    tok = red.to_core("mix", dqb_n)
    dfa_t, db_f = _forget_bwd(dac_a[:, 0, :], fa_t, b_f)
    dproj_p, dgains = _dproj_merge(
        [dqa_n, dka_n, dva, dfa_t.T, dqb_n, dkb_n, dvb, dga, dgb], in_splits, cs, cp,
        {0: (qa, gains["g_q_fox"]), 1: (ka, gains["g_k_fox"]), 4: (qb, gains["g_q_dil"]), 5: (kb, gains["g_k_dil"])})
    dg_qf, dg_kf, dg_qd, dg_kd = dgains[0], dgains[1], dgains[4], dgains[5]
    g_in = _mm(h1, dproj_p, mode="tn", out_dtype=BF16, out_kind="col", tm=t_d2, tn=t_in, tk=s, name="mm_in_dw", deps=(tok,))
    tok = red.swap("in", [g_in], g_in)
    tok = red.to_chips("in", filler(tok))
    dh1 = _mm(dproj_p, wg["in"], mode="nt", b_kind="col", tm=tm, tn=t_d2, tk=t_in, name="mm_in_dx", deps=(tok,))
    grad_x, dg_attn = _norm_bwd(dh1, x, small["g_attn"], group=d, res=dx1_f, out_dtypes=(F32,), name="rms1_bwd")

    small_grads = {
        "g_attn": dg_attn, "b_forget": db_f.reshape(1, N_HEADS),
        "g_q_fox": dg_qf, "g_k_fox": dg_kf, "g_q_dil": dg_qd, "g_k_dil": dg_kd, "g_ffn": dg_ffn,
        "w_conv": jnp.concatenate([st_g[0:3], st_v[0:3]], axis=1),
        "b_conv": jnp.concatenate([st_g[3:4], st_v[3:4]], axis=1),
        "loss": loss_blk[0:1, 0:1],
    }
    return small_grads, grad_x


SMALL_ORDER = ("g_attn", "b_forget", "g_q_fox", "g_k_fox", "g_q_dil", "g_k_dil", "g_ffn", "w_conv", "b_conv", "loss")
WEIGHT_ORDER = ("g_attn", "w_in", "b_forget", "g_q_fox", "g_k_fox", "g_q_dil", "g_k_dil", "w_br_fox", "w_br_dil",
                "w_out", "g_ffn", "w_up", "w_conv", "b_conv", "w_down")
BIG = {"w_in": "in", "w_br_fox": "brf", "w_br_dil": "brd", "w_out": "out", "w_up": "up", "w_down": "down"}


def kernel(x, g_attn, w_in, b_forget, g_q_fox, g_k_fox, g_q_dil, g_k_dil, w_br_fox, w_br_dil, w_out, g_ffn, w_up, w_conv, b_conv, w_down, loss_target, m_g_attn, m_w_in, m_b_forget, m_g_q_fox, m_g_k_fox, m_g_q_dil, m_g_k_dil, m_w_br_fox, m_w_br_dil, m_w_out, m_g_ffn, m_w_up, m_w_conv, m_b_conv, m_w_down, v_g_attn, v_w_in, v_b_forget, v_g_q_fox, v_g_k_fox, v_g_q_dil, v_g_k_dil, v_w_br_fox, v_w_br_dil, v_w_out, v_g_ffn, v_w_up, v_w_conv, v_b_conv, v_w_down):
    w = dict(g_attn=g_attn, w_in=w_in, b_forget=b_forget, g_q_fox=g_q_fox, g_k_fox=g_k_fox, g_q_dil=g_q_dil,
             g_k_dil=g_k_dil, w_br_fox=w_br_fox, w_br_dil=w_br_dil, w_out=w_out, g_ffn=g_ffn, w_up=w_up,
             w_conv=w_conv, b_conv=b_conv, w_down=w_down)
    m = dict(g_attn=m_g_attn, w_in=m_w_in, b_forget=m_b_forget, g_q_fox=m_g_q_fox, g_k_fox=m_g_k_fox,
             g_q_dil=m_g_q_dil, g_k_dil=m_g_k_dil, w_br_fox=m_w_br_fox, w_br_dil=m_w_br_dil, w_out=m_w_out,
             g_ffn=m_g_ffn, w_up=m_w_up, w_conv=m_w_conv, b_conv=m_b_conv, w_down=m_w_down)
    v = dict(g_attn=v_g_attn, w_in=v_w_in, b_forget=v_b_forget, g_q_fox=v_g_q_fox, g_k_fox=v_g_k_fox,
             g_q_dil=v_g_q_dil, g_k_dil=v_g_k_dil, w_br_fox=v_w_br_fox, w_br_dil=v_w_br_dil, w_out=v_w_out,
             g_ffn=v_g_ffn, w_up=v_w_up, w_conv=v_w_conv, b_conv=v_b_conv, w_down=v_w_down)
    xi, yi, ci = lax.axis_index("x"), lax.axis_index("y"), lax.axis_index("c")
    chip = (2 * xi + yi).astype(jnp.int32)

    cs = w_in.shape[2]
    cp = _round_up(cs, LANES)
    shards = {
        "in": jnp.pad(w_in[0].astype(BF16), ((0, 0), (0, cp - cs))),
        "brf": w_br_fox[0].astype(BF16), "brd": w_br_dil[0].astype(BF16), "out": w_out[0].astype(BF16),
        "up": w_up[0].astype(BF16), "down": w_down[0].astype(BF16),
    }
    names = tuple(shards)
    conv_pad = jnp.pad(w_conv[0], ((0, 8 - w_conv.shape[1]), (0, 0)))
    first = [(shards["in"], True), (conv_pad, False)]
    later = {"mid": ("brf", "brd", "out"), "late": ("up", "down")}
    groups = {key: [(shards[n], True) for n in members] for key, members in later.items()}
    (started_first, *started_later), token = _gather_start([first, *groups.values()])
    started = dict(zip(later, started_later))
    token, w["w_in"], m["w_in"], v["w_in"] = lax.optimization_barrier((token, w["w_in"], m["w_in"], v["w_in"]))
    w2, m2, v2 = ({n: a[n].reshape(a[n].shape[-2], a[n].shape[-1]) for n in BIG} for a in (w, m, v))
    early = (token, w2["w_in"], m2["w_in"], v2["w_in"])
    own_first, passed_first, token = _gather_pass(first, started_first, early, "gather_pass_in")
    land_in, land_conv = _gather_wait(first, passed_first, token, "gather_wait_in")
    wg = {"in": land_in, "bconv": b_conv,
          "conv": jnp.transpose(land_conv[:, :w_conv.shape[1], :], (1, 0, 2)).reshape(w_conv.shape[1], -1)}
    small = {n: w[n] for n in ("g_attn", "b_forget", "g_q_fox", "g_k_fox", "g_q_dil", "g_k_dil", "g_ffn")}
    small = {n: (a[0] if a.ndim == 3 else a) for n, a in small.items()}
    in_flight = {}

    def rest_pass(key, after):
        own, passed, tok = _gather_pass(groups[key], started[key], (after,), "gather_pass_" + key)
        in_flight[key] = (own, passed)
        return tok

    def rest_wait(key, after):
        own, passed = in_flight.pop(key)
        lands = _gather_wait(groups[key], passed, after, "gather_wait_" + key)
        return dict(zip(later[key], lands))

    reducer = _Reducer(jnp.stack([chip, ci.astype(jnp.int32)]))
    g_out, d_out, m_out, v_out = {}, {}, {}, {}
    reduced = {}

    def first_element(arrays):
        return jnp.stack([a[(0,) * a.ndim] for a in arrays])

    def update_big(n, deps):
        g2, dl, mn, vn = _adamw(w2[n], reduced[BIG[n]], m2[n], v2[n], name="adamw_" + n, deps=deps, emit_grad=True)
        g_out[n], d_out[n], m_out[n], v_out[n] = (a.reshape(w[n].shape) for a in (g2, dl, mn, vn))

    def update_down(tok):
        (reduced["down"],) = reducer.finish("down", tok)
        update_big("w_down", (tok,))
        return v_out["w_down"]

    small_grads, grad_x = _layer_grads(x[0], loss_target[0], small, wg, rest_pass, rest_wait, reducer, update_down)

    for key, members in (("up", ("up",)), ("mix", ("out", "brf", "brd"))):
        reduced.update(zip(members, reducer.finish(key, grad_x)))
    others = ("w_up", "w_out", "w_br_fox", "w_br_dil")
    for n in others:
        update_big(n, (grad_x,))

    flat = jnp.concatenate([small_grads[n].reshape(-1) for n in SMALL_ORDER])
    rows = _round_up(flat.shape[0], 8 * LANES) // LANES
    pack = jnp.pad(flat, (0, rows * LANES - flat.shape[0])).reshape(rows, LANES)
    packs = _gather_packs(pack, deps=(first_element([v_out[n] for n in others]),))
    total = _sum_devices(packs).reshape(-1)
    red, at = {}, 0
    for n in SMALL_ORDER:
        size = small_grads[n].size
        red[n] = total[at:at + size].reshape(small_grads[n].shape)
        at += size
    loss = red["loss"].reshape(())
    c2 = w_conv.shape[2]
    red["w_conv"] = lax.dynamic_slice_in_dim(red["w_conv"], chip * c2, c2, axis=1)

    smalls = [n for n in WEIGHT_ORDER if n not in BIG]
    for n in smalls:
        shape = w[n].shape
        r2 = (shape[-2], shape[-1]) if n not in ("g_attn", "b_forget", "g_ffn", "b_conv") else (1, shape[-1])
        g2 = red[n].reshape(r2)
        dl, mn, vn = _adamw(w[n].reshape(r2), g2, m[n].reshape(r2), v[n].reshape(r2), name="adamw_" + n)
        g_out[n], d_out[n], m_out[n], v_out[n] = (a.reshape(shape) for a in (g2, dl, mn, vn))
    tok = reducer.to_core("in", first_element([v_out[n] for n in smalls]))
    (reduced["in"],) = reducer.finish("in", tok)
    update_big("w_in", (tok,))

    return (loss, grad_x[None], *[g_out[n] for n in WEIGHT_ORDER], *[d_out[n] for n in WEIGHT_ORDER],
            *[m_out[n] for n in WEIGHT_ORDER], *[v_out[n] for n in WEIGHT_ORDER])
```

```python
import functools
import math

import jax
import jax.numpy as jnp
import numpy as np
from jax import lax
from jax.experimental import pallas as pl
from jax.experimental.pallas import tpu as pltpu

F32 = jnp.float32
BF16 = jnp.bfloat16
HEAD_DIM = 128
N_HEADS = 8
EPS = 1e-6
NEG = -1e30
N_CHIPS = 4
N_DEV = 8
LANES = 128
VMEM_LIMIT_BYTES = 56 * 1024 * 1024
DIL_PATTERNS = ((128, 1), (512, 4), (2048, 16))
ATTN_TILE = 512
ADAM_LR, ADAM_B1, ADAM_B2, ADAM_EPS, ADAM_WD, ADAM_STEP = 0.001, 0.9, 0.999, 1e-08, 0.01, 10
MESH = pl.DeviceIdType.MESH


def _params(*sem):
    return pltpu.CompilerParams(dimension_semantics=sem, vmem_limit_bytes=VMEM_LIMIT_BYTES)


def _round_up(n, m):
    return -(-n // m) * m


def _pick(dim, prefs):
    for p in prefs:
        if dim % p == 0:
            return p
    raise ValueError(f"no tile for {dim} in {prefs}")


def _logical_shape(arr, kind):
    if kind is None:
        return arr.shape
    s, r, c = arr.shape
    return (r, s * c) if kind == "col" else (s * r, c)


def _spec(shape, kind, br, bc, fi, fj):
    if kind is None:
        return pl.BlockSpec((br, bc), lambda *g: (fi(*g), fj(*g)))
    _, r, c = shape
    if kind == "col":
        nb = c // bc
        assert nb * bc == c, (shape, bc)
        return pl.BlockSpec((None, br, bc), lambda *g: (fj(*g) // nb, fi(*g), fj(*g) % nb))
    nb = r // br
    assert nb * br == r, (shape, br)
    return pl.BlockSpec((None, br, bc), lambda *g: (fi(*g) // nb, fi(*g) % nb, fj(*g)))


def _mm(a, b, *, mode, tm, tn, tk, name, a_kind=None, b_kind=None, out_kind=None,
        out_dtype=F32, res=None, deps=()):
    pair_a, pair_b = isinstance(a, tuple), isinstance(b, tuple)
    if pair_a or pair_b:
        return _mm_pair(a, b, mode=mode, tm=tm, tn=tn, tk=tk, name=name, b_kind=b_kind, out_kind=out_kind,
                        out_dtype=out_dtype, deps=deps)
    la, lb = _logical_shape(a, a_kind), _logical_shape(b, b_kind)
    if mode == "nn":
        (m, k), (k2, n) = la, lb
    elif mode == "nt":
        (m, k), (n, k2) = la, lb
    else:
        (k, m), (k2, n) = la, lb
    assert k == k2, (name, la, lb)
    assert m % tm == 0 and n % tn == 0 and k % tk == 0, (name, m, n, k, tm, tn, tk)
    nk = k // tk
    im = lambda i, j, l: i
    jn = lambda i, j, l: j
    lk = lambda i, j, l: l
    if mode == "tn":
        a_spec = _spec(a.shape, a_kind, tk, tm, lk, im)
        dims = (((0,), (0,)), ((), ()))
    else:
        a_spec = _spec(a.shape, a_kind, tm, tk, im, lk)
        dims = (((1,), (1,)), ((), ())) if mode == "nt" else (((1,), (0,)), ((), ()))
    if mode == "nt":
        b_spec = _spec(b.shape, b_kind, tn, tk, jn, lk)
    else:
        b_spec = _spec(b.shape, b_kind, tk, tn, lk, jn)
    if out_kind is None:
        oshape = (m, n)
    elif out_kind == "col":
        oshape = (N_CHIPS, m, n // N_CHIPS)
    else:
        oshape = (N_CHIPS, m // N_CHIPS, n)
    o_spec = _spec(oshape, out_kind, tm, tn, im, jn)
    in_specs = [a_spec, b_spec]
    args = [a, b]
    if res is not None:
        in_specs.append(pl.BlockSpec((tm, tn), lambda i, j, l: (i, j)))
        args.append(res)
    in_specs += [pl.BlockSpec(memory_space=pl.ANY)] * len(deps)
    args += list(deps)

    def finish(out, res_ref, o_ref):
        if res_ref is not None:
            out = out + res_ref[...]
        o_ref[...] = out.astype(o_ref.dtype)

    def body_whole_k(*refs):
        res_ref = refs[2] if res is not None else None
        finish(lax.dot_general(refs[0][...], refs[1][...], dims, preferred_element_type=F32), res_ref, refs[-1])

    def body(*refs):
        a_ref, b_ref = refs[0], refs[1]
        res_ref = refs[2] if res is not None else None
        o_ref, acc_ref = refs[-2], refs[-1]
        step = pl.program_id(2)

        @pl.when(step == 0)
        def _():
            acc_ref[...] = jnp.zeros_like(acc_ref)

        acc_ref[...] += lax.dot_general(a_ref[...], b_ref[...], dims, preferred_element_type=F32)

        @pl.when(step == nk - 1)
        def _():
            finish(acc_ref[...], res_ref, o_ref)

    return pl.pallas_call(
        body_whole_k if nk == 1 else body, name=name, grid=(m // tm, n // tn, nk),
        in_specs=in_specs, out_specs=o_spec,
        out_shape=jax.ShapeDtypeStruct(oshape, out_dtype),
        scratch_shapes=[] if nk == 1 else [pltpu.VMEM((tm, tn), F32)],
        compiler_params=_params("parallel", "parallel", "arbitrary"),
    )(*args)


def _mm_pair(a, b, *, mode, tm, tn, tk, name, b_kind, out_kind, out_dtype, deps):
    anyspec = [pl.BlockSpec(memory_space=pl.ANY)] * len(deps)
    if mode == "tn":
        assert isinstance(b, tuple) and out_kind == "col" and a.shape[0] == tk
        k, m = a.shape
        n0 = b[0].shape[1]
        n, nb0 = 2 * n0, n0 // tn
        oshape = (N_CHIPS, m, n // N_CHIPS)

        def body(a_ref, b0_ref, b1_ref, *rest):
            o_ref = rest[-1]
            for first, b_ref in ((True, b0_ref), (False, b1_ref)):
                @pl.when((pl.program_id(1) < nb0) == first)
                def _():
                    o_ref[...] = lax.dot_general(a_ref[...], b_ref[...], (((0,), (0,)), ((), ())),
                                                 preferred_element_type=F32).astype(o_ref.dtype)

        return pl.pallas_call(
            body, name=name, grid=(m // tm, n // tn),
            in_specs=[pl.BlockSpec((tk, tm), lambda i, j: (0, i)),
                      pl.BlockSpec((tk, tn), lambda i, j: (0, jnp.minimum(j, nb0 - 1))),
                      pl.BlockSpec((tk, tn), lambda i, j: (0, jnp.maximum(j - nb0, 0)))] + anyspec,
            out_specs=_spec(oshape, "col", tm, tn, lambda i, j: i, lambda i, j: j),
            out_shape=jax.ShapeDtypeStruct(oshape, out_dtype), compiler_params=_params("parallel", "arbitrary"),
        )(a, *b, *deps)
    assert mode == "nt" and isinstance(a, tuple) and out_kind is None
    m, k0 = a[0].shape
    n = _logical_shape(b, b_kind)[0]
    nk0 = k0 // tk
    nk = 2 * nk0

    def body(a0_ref, a1_ref, b_ref, *rest):
        o_ref, acc_ref = rest[-2], rest[-1]
        step = pl.program_id(2)

        @pl.when(step == 0)
        def _():
            acc_ref[...] = jnp.zeros_like(acc_ref)

        for first, a_ref in ((True, a0_ref), (False, a1_ref)):
            @pl.when((step < nk0) == first)
            def _():
                acc_ref[...] += lax.dot_general(a_ref[...], b_ref[...], (((1,), (1,)), ((), ())), preferred_element_type=F32)

        @pl.when(step == nk - 1)
        def _():
            o_ref[...] = acc_ref[...].astype(o_ref.dtype)

    return pl.pallas_call(
        body, name=name, grid=(m // tm, n // tn, nk),
        in_specs=[pl.BlockSpec((tm, tk), lambda i, j, l: (i, jnp.minimum(l, nk0 - 1))),
                  pl.BlockSpec((tm, tk), lambda i, j, l: (i, jnp.maximum(l - nk0, 0))),
                  _spec(b.shape, b_kind, tn, tk, lambda i, j, l: j, lambda i, j, l: l)] + anyspec,
        out_specs=pl.BlockSpec((tm, tn), lambda i, j, l: (i, j)),
        out_shape=jax.ShapeDtypeStruct((m, n), out_dtype), scratch_shapes=[pltpu.VMEM((tm, tn), F32)],
        compiler_params=_params("parallel", "parallel", "arbitrary"),
    )(*a, b, *deps)


def _pieces(splits, cs, cp):
    out, g0 = [], 0
    for width in splits:
        g1, runs = g0 + width, []
        for j in range(N_CHIPS):
            a, b = max(g0, cs * j), min(g1, cs * (j + 1))
            if a < b:
                runs.append((j * cp + a - cs * j, a - g0, b - a))
        out.append(runs)
        g0 = g1
    return out


def _head_norm(xv, gv):
    r = lax.rsqrt(jnp.mean(xv * xv, axis=-1, keepdims=True) + EPS)
    return (xv * r) * gv


def _head_norm_bwd(dyv, xv, gv):
    r = lax.rsqrt(jnp.mean(xv * xv, axis=-1, keepdims=True) + EPS)
    xr = xv * r
    gdy = dyv * gv
    return r * (gdy - xr * jnp.mean(gdy * xr, axis=-1, keepdims=True)), jnp.sum(dyv * xr, axis=0, keepdims=True)


def _proj_split(proj_p, splits, cs, dtypes, gains, tm=128):
    s, wp = proj_p.shape
    pieces = _pieces(splits, cs, wp // N_CHIPS)
    normed = sorted(gains)
    nseg = len(splits)

    def body(p_ref, *refs):
        g_refs, o_refs, n_refs = refs[:len(normed)], refs[len(normed):len(normed) + nseg], refs[len(normed) + nseg:]
        for o_ref, runs in zip(o_refs, pieces):
            for src, dst, n in runs:
                o_ref[:, dst:dst + n] = p_ref[:, src:src + n].astype(o_ref.dtype)
        for g_ref, n_ref, i in zip(g_refs, n_refs, normed):
            for c0 in range(0, splits[i], HEAD_DIM):
                cols = slice(c0, c0 + HEAD_DIM)
                n_ref[:, cols] = _head_norm(o_refs[i][:, cols], g_ref[:, cols]).astype(n_ref.dtype)

    return pl.pallas_call(
        body, name="proj_split", grid=(s // tm,),
        in_specs=[pl.BlockSpec((tm, wp), lambda i: (i, 0))] + [pl.BlockSpec((1, splits[i]), lambda i: (0, 0)) for i in normed],
        out_specs=[pl.BlockSpec((tm, w), lambda i: (i, 0)) for w in splits]
        + [pl.BlockSpec((tm, splits[i]), lambda i: (i, 0)) for i in normed],
        out_shape=[jax.ShapeDtypeStruct((s, w), dt) for w, dt in zip(splits, dtypes)]
        + [jax.ShapeDtypeStruct((s, splits[i]), BF16) for i in normed],
        compiler_params=_params("parallel"),
    )(proj_p, *[gains[i] for i in normed])


def _dproj_merge(parts, splits, cs, cp, norms, tm=128):
    s = parts[0].shape[0]
    wp = N_CHIPS * cp
    pieces = _pieces(splits, cs, cp)
    normed = sorted(norms)
    nseg, nn = len(splits), len(normed)

    def body(*refs):
        p_refs, x_refs, g_refs = refs[:nseg], refs[nseg:nseg + nn], refs[nseg + nn:nseg + 2 * nn]
        o_ref, dg_refs = refs[nseg + 2 * nn], refs[nseg + 2 * nn + 1:nseg + 3 * nn + 1]
        stage, tmp = refs[-2], refs[-1]

        @pl.when(pl.program_id(0) == 0)
        def _():
            for dg_ref in dg_refs:
                dg_ref[...] = jnp.zeros_like(dg_ref)

        for j in range(N_CHIPS):
            stage[:, j * cp + cs:(j + 1) * cp] = jnp.zeros((tm, cp - cs), F32)
        for i, (p_ref, runs) in enumerate(zip(p_refs, pieces)):
            src_ref = p_ref
            if i in norms:
                k = normed.index(i)
                for c0 in range(0, splits[i], HEAD_DIM):
                    cols = slice(c0, c0 + HEAD_DIM)
                    dx, dg = _head_norm_bwd(p_ref[:, cols].astype(F32), x_refs[k][:, cols], g_refs[k][:, cols])
                    tmp[:, cols] = dx
                    dg_refs[k][:, cols] += dg
                src_ref = tmp
            for dst, src, n in runs:
                stage[:, dst:dst + n] = src_ref[:, src:src + n].astype(F32)
        o_ref[...] = stage[...].astype(o_ref.dtype)

    wmax = max(splits[i] for i in normed)
    row = lambda w: pl.BlockSpec((tm, w), lambda i: (i, 0))
    vec = lambda w: pl.BlockSpec((1, w), lambda i: (0, 0))
    outs = pl.pallas_call(
        body, name="dproj_merge", grid=(s // tm,),
        in_specs=[row(w) for w in splits] + [row(splits[i]) for i in normed] + [vec(splits[i]) for i in normed],
        out_specs=[row(wp)] + [vec(splits[i]) for i in normed],
        out_shape=[jax.ShapeDtypeStruct((s, wp), BF16)] + [jax.ShapeDtypeStruct((1, splits[i]), F32) for i in normed],
        scratch_shapes=[pltpu.VMEM((tm, wp), F32), pltpu.VMEM((tm, wmax), F32)],
        compiler_params=_params("arbitrary"),
    )(*parts, *[norms[i][0] for i in normed], *[norms[i][1] for i in normed])
    return outs[0], dict(zip(normed, outs[1:]))


def _norm_fwd(x, g, *, group, name, tm=256):
    s, w = x.shape
    ng = w // group

    def body(x_ref, g_ref, o_ref):
        for i in range(ng):
            cols = slice(i * group, (i + 1) * group)
            xv = x_ref[:, cols]
            r = lax.rsqrt(jnp.mean(xv * xv, axis=-1, keepdims=True) + EPS)
            o_ref[:, cols] = ((xv * r) * g_ref[:, cols]).astype(o_ref.dtype)

    return pl.pallas_call(
        body, name=name, grid=(s // tm,),
        in_specs=[pl.BlockSpec((tm, w), lambda i: (i, 0)), pl.BlockSpec((1, w), lambda i: (0, 0))],
        out_specs=pl.BlockSpec((tm, w), lambda i: (i, 0)),
        out_shape=jax.ShapeDtypeStruct((s, w), BF16),
        compiler_params=_params("parallel"),
    )(x, g)


def _norm_bwd(dy, x, g, *, group, name, res=None, out_dtypes=(BF16,), tm=256, deps=()):
    s, w = x.shape
    ng = w // group
    n_in = 4 if res is not None else 3

    def body(*refs):
        dy_ref, x_ref, g_ref = refs[:3]
        res_ref = refs[3] if res is not None else None
        outs = refs[n_in + len(deps):]
        dx_refs, dg_ref = outs[:-1], outs[-1]

        @pl.when(pl.program_id(0) == 0)
        def _():
            dg_ref[...] = jnp.zeros_like(dg_ref)

        for i in range(ng):
            cols = slice(i * group, (i + 1) * group)
            xv = x_ref[:, cols]
            dyv = dy_ref[:, cols].astype(F32)
            r = lax.rsqrt(jnp.mean(xv * xv, axis=-1, keepdims=True) + EPS)
            xr = xv * r
            dg_ref[:, cols] += jnp.sum(dyv * xr, axis=0, keepdims=True)
            gdy = dyv * g_ref[:, cols]
            dx = r * (gdy - xr * jnp.mean(gdy * xr, axis=-1, keepdims=True))
            if res_ref is not None:
                dx = dx + res_ref[:, cols]
            for dx_ref in dx_refs:
                dx_ref[:, cols] = dx.astype(dx_ref.dtype)

    row = pl.BlockSpec((tm, w), lambda i: (i, 0))
    vec = pl.BlockSpec((1, w), lambda i: (0, 0))
    in_specs = [row, row, vec] + ([row] if res is not None else []) + [pl.BlockSpec(memory_space=pl.ANY)] * len(deps)
    args = [dy, x, g] + ([res] if res is not None else []) + list(deps)
    out_specs = [row] * len(out_dtypes) + [vec]
    out_shape = [jax.ShapeDtypeStruct((s, w), dt) for dt in out_dtypes] + [jax.ShapeDtypeStruct((1, w), F32)]
    return pl.pallas_call(
        body, name=name, grid=(s // tm,), in_specs=in_specs, out_specs=out_specs,
        out_shape=out_shape, compiler_params=_params("arbitrary"),
    )(*args)


def _split3(v):
    p1 = v.astype(BF16)
    r1 = v - p1.astype(F32)
    p2 = r1.astype(BF16)
    p3 = (r1 - p2.astype(F32)).astype(BF16)
    return p1, p2, p3


def _tri_sum(v, reverse, tcol=512):
    h, s = v.shape
    tcol = min(tcol, s)
    parts = _split3(v)
    outs = []
    for j in range(s // tcol):
        src = lax.broadcasted_iota(jnp.int32, (s, tcol), 0)
        dst = lax.broadcasted_iota(jnp.int32, (s, tcol), 1) + j * tcol
        keep = (src >= dst) if reverse else (src <= dst)
        tri = jnp.where(keep, 1.0, 0.0).astype(BF16)
        acc = jnp.zeros((h, tcol), F32)
        for p in parts:
            acc = acc + jnp.dot(p, tri, preferred_element_type=F32)
        outs.append(acc)
    return outs


def _forget_fwd(fa_t, b):
    h, s = fa_t.shape
    tcol = min(512, s)

    def body(f_ref, b_ref, c_ref):
        z = f_ref[...] + b_ref[...]
        logf = jnp.minimum(z, 0.0) - jnp.log(1.0 + jnp.exp(-jnp.abs(z)))
        for j, blk in enumerate(_tri_sum(logf, reverse=False, tcol=tcol)):
            c_ref[:, j * tcol:(j + 1) * tcol] = blk

    return pl.pallas_call(
        body, name="forget_fwd", out_shape=jax.ShapeDtypeStruct((h, s), F32),
        compiler_params=_params(),
    )(fa_t, b)


def _forget_bwd(dacol, fa_t, b):
    h, s = fa_t.shape
    tcol = min(512, s)

    def body(d_ref, f_ref, b_ref, dfa_ref, db_ref):
        z = f_ref[...] + b_ref[...]
        dc = -d_ref[...]
        total = jnp.zeros((h, 1), F32)
        for j, blk in enumerate(_tri_sum(dc, reverse=True, tcol=tcol)):
            cols = slice(j * tcol, (j + 1) * tcol)
            dfa = blk * (1.0 - jax.nn.sigmoid(z[:, cols]))
            dfa_ref[:, cols] = dfa
            total = total + jnp.sum(dfa, axis=-1, keepdims=True)
        db_ref[...] = total

    return pl.pallas_call(
        body, name="forget_bwd",
        out_shape=[jax.ShapeDtypeStruct((h, s), F32), jax.ShapeDtypeStruct((h, 1), F32)],
        compiler_params=_params(),
    )(dacol, fa_t, b)


def _distance_bias(s, tile, dilated):
    nb = s // tile
    b = lax.broadcasted_iota(jnp.int32, (nb, tile, tile), 0)
    dist = b * tile + lax.broadcasted_iota(jnp.int32, (nb, tile, tile), 1) - lax.broadcasted_iota(jnp.int32, (nb, tile, tile), 2)
    if not dilated:
        return jnp.where(dist >= 0, 0.0, NEG).astype(F32)
    mult = jnp.zeros(dist.shape, jnp.int32)
    for window, dil in DIL_PATTERNS:
        mult = mult + ((dist >= 0) & (dist <= window) & ((dist & (dil - 1)) == 0)).astype(jnp.int32)
    logm = jnp.where(mult == 3, math.log(3.0), jnp.where(mult == 2, math.log(2.0), 0.0))
    return jnp.where(mult > 0, logm, NEG).astype(F32)


def _logits(q, k, arow, acol, bias):
    s = lax.dot_general(q, k, (((1,), (1,)), ((), ())), preferred_element_type=F32)
    return s * (1.0 / math.sqrt(HEAD_DIM)) + arow - acol + bias


def _attn_fwd(q, k, v, arow, acol, *, dilated, name, tq=ATTN_TILE, tk=ATTN_TILE):
    two_term = not dilated
    s, w = q.shape
    nh = w // HEAD_DIM
    assert tq == tk
    tq = tk = min(tq, s)
    nq, nk = s // tq, s // tk

    def body(q_ref, k_ref, v_ref, ar_ref, ac_ref, b_ref, o_ref, of_ref, lse_ref, m_ref, l_ref, acc_ref):
        qi, ki = pl.program_id(1), pl.program_id(2)

        @pl.when(ki == 0)
        def _():
            m_ref[...] = jnp.full_like(m_ref, NEG)
            l_ref[...] = jnp.zeros_like(l_ref)
            acc_ref[...] = jnp.zeros_like(acc_ref)

        @pl.when(ki <= qi)
        def _():
            sc = _logits(q_ref[...], k_ref[...], ar_ref[...], ac_ref[...], b_ref[...])
            m_new = jnp.maximum(m_ref[...], jnp.max(sc, axis=-1, keepdims=True))
            alpha = jnp.exp(m_ref[...] - m_new)
            p = jnp.exp(sc - m_new)
            l_ref[...] = alpha * l_ref[...] + jnp.sum(p, axis=-1, keepdims=True)
            p_hi = p.astype(BF16)
            vv = v_ref[...]
            pv = jnp.dot(p_hi, vv, preferred_element_type=F32)
            if two_term:
                pv = pv + jnp.dot((p - p_hi.astype(F32)).astype(BF16), vv, preferred_element_type=F32)
            acc_ref[...] = alpha * acc_ref[...] + pv
            m_ref[...] = m_new

        @pl.when(ki == nk - 1)
        def _():
            out = acc_ref[...] / l_ref[...]
            o_ref[...] = out.astype(o_ref.dtype)
            of_ref[...] = out
            lse_ref[...] = m_ref[...] + jnp.log(l_ref[...])

    kv = pl.BlockSpec((tk, HEAD_DIM), lambda h, i, j: (jnp.minimum(j, i), h))
    return pl.pallas_call(
        body, name=name, grid=(nh, nq, nk),
        in_specs=[pl.BlockSpec((tq, HEAD_DIM), lambda h, i, j: (i, h)), kv, kv,
                  pl.BlockSpec((None, tq, 1), lambda h, i, j: (h, i, 0)),
                  pl.BlockSpec((None, 1, tk), lambda h, i, j: (h, 0, jnp.minimum(j, i))),
                  pl.BlockSpec((None, tq, tk), lambda h, i, j: (jnp.maximum(i - j, 0), 0, 0))],
        out_specs=[pl.BlockSpec((tq, HEAD_DIM), lambda h, i, j: (i, h)),
                   pl.BlockSpec((tq, HEAD_DIM), lambda h, i, j: (i, h)),
                   pl.BlockSpec((None, tq, 1), lambda h, i, j: (h, i, 0))],
        out_shape=[jax.ShapeDtypeStruct((s, w), BF16), jax.ShapeDtypeStruct((s, w), F32),
                   jax.ShapeDtypeStruct((nh, s, 1), F32)],
        scratch_shapes=[pltpu.VMEM((tq, 1), F32), pltpu.VMEM((tq, 1), F32), pltpu.VMEM((tq, HEAD_DIM), F32)],
        compiler_params=_params("parallel", "parallel", "arbitrary"),
    )(q, k, v, arow, acol, _distance_bias(s, tq, dilated))


def _attn_bwd(q, k, v, o, do, lse, arow, acol, *, dilated, name, tq=ATTN_TILE, tk=ATTN_TILE):
    s, w = q.shape
    nh = w // HEAD_DIM
    assert tq == tk
    tq = tk = min(tq, s)
    nq, nk = s // tq, s // tk
    scale = 1.0 / math.sqrt(HEAD_DIM)

    def body(q_ref, k_ref, v_ref, o_ref, do_ref, lse_ref, ar_ref, ac_ref, b_ref,
             dq_ref, dk_ref, dv_ref, dac_ref, dk_acc, dv_acc, dac_acc):
        ki, qi = pl.program_id(1), pl.program_id(2)

        @pl.when((ki == 0) & (qi == 0))
        def _():
            dq_ref[...] = jnp.zeros_like(dq_ref)

        @pl.when(qi == 0)
        def _():
            dk_acc[...] = jnp.zeros_like(dk_acc)
            dv_acc[...] = jnp.zeros_like(dv_acc)
            dac_acc[...] = jnp.zeros_like(dac_acc)

        @pl.when(qi >= ki)
        def _():
            qv, kvv, dov = q_ref[...], k_ref[...], do_ref[...]
            sc = _logits(qv, kvv, ar_ref[...], ac_ref[...], b_ref[...])
            p = jnp.exp(sc - lse_ref[...])
            dp = lax.dot_general(dov, v_ref[...], (((1,), (1,)), ((), ())), preferred_element_type=F32)
            delta = jnp.sum(dov.astype(F32) * o_ref[...].astype(F32), axis=-1, keepdims=True)
            ds = p * (dp - delta)
            dsb = ds.astype(BF16)
            dv_acc[...] += lax.dot_general(p.astype(BF16), dov, (((0,), (0,)), ((), ())), preferred_element_type=F32)
            dk_acc[...] += lax.dot_general(dsb, qv, (((0,), (0,)), ((), ())), preferred_element_type=F32)
            rows = pl.ds(pl.multiple_of(qi * tq, tq), tq)
            dq_ref[rows, :] += jnp.dot(dsb, kvv, preferred_element_type=F32) * scale
            dac_acc[...] += jnp.sum(ds, axis=0, keepdims=True)

        @pl.when(qi == nq - 1)
        def _():
            dk_ref[...] = dk_acc[...] * scale
            dv_ref[...] = dv_acc[...]
            dac_ref[...] = dac_acc[...]

    qs = pl.BlockSpec((tq, HEAD_DIM), lambda h, j, i: (jnp.maximum(i, j), h))
    ks = pl.BlockSpec((tk, HEAD_DIM), lambda h, j, i: (j, h))
    rowv = pl.BlockSpec((None, tq, 1), lambda h, j, i: (h, jnp.maximum(i, j), 0))
    colv = pl.BlockSpec((None, 1, tk), lambda h, j, i: (h, 0, j))
    return pl.pallas_call(
        body, name=name, grid=(nh, nk, nq),
        in_specs=[qs, ks, ks, qs, qs, rowv, rowv, colv,
                  pl.BlockSpec((None, tq, tk), lambda h, j, i: (jnp.maximum(i - j, 0), 0, 0))],
        out_specs=[pl.BlockSpec((s, HEAD_DIM), lambda h, j, i: (0, h)), ks, ks, colv],
        out_shape=[jax.ShapeDtypeStruct((s, w), F32), jax.ShapeDtypeStruct((s, w), F32),
                   jax.ShapeDtypeStruct((s, w), F32), jax.ShapeDtypeStruct((nh, 1, s), F32)],
        scratch_shapes=[pltpu.VMEM((tk, HEAD_DIM), F32), pltpu.VMEM((tk, HEAD_DIM), F32), pltpu.VMEM((1, tk), F32)],
        compiler_params=_params("arbitrary", "arbitrary", "arbitrary"),
    )(q, k, v, o, do, lse, arow, acol, _distance_bias(s, tq, dilated))


def _gate_fwd(ga, gb, pa, pb, tm=256):
    s, d = ga.shape

    def body(ga_ref, gb_ref, pa_ref, pb_ref, o_ref):
        o_ref[...] = (jax.nn.sigmoid(ga_ref[...]) * pa_ref[...]
                      + jax.nn.sigmoid(gb_ref[...]) * pb_ref[...]).astype(o_ref.dtype)

    row = pl.BlockSpec((tm, d), lambda i: (i, 0))
    return pl.pallas_call(
        body, name="gate_fwd", grid=(s // tm,), in_specs=[row] * 4, out_specs=row,
        out_shape=jax.ShapeDtypeStruct((s, d), BF16), compiler_params=_params("parallel"),
    )(ga, gb, pa, pb)


def _gate_bwd(dm, ga, gb, pa, pb, tm=256):
    s, d = ga.shape

    def body(dm_ref, ga_ref, gb_ref, pa_ref, pb_ref, dpa_ref, dpb_ref, dga_ref, dgb_ref):
        dmv = dm_ref[...]
        for g_ref, p_ref, dp_ref, dg_ref in ((ga_ref, pa_ref, dpa_ref, dga_ref), (gb_ref, pb_ref, dpb_ref, dgb_ref)):
            sg = jax.nn.sigmoid(g_ref[...])
            dp_ref[...] = (dmv * sg).astype(BF16)
            dg_ref[...] = (dmv * p_ref[...] * (sg * (1.0 - sg))).astype(BF16)

    row = pl.BlockSpec((tm, d), lambda i: (i, 0))
    return pl.pallas_call(
        body, name="gate_bwd", grid=(s // tm,), in_specs=[row] * 5, out_specs=[row] * 4,
        out_shape=[jax.ShapeDtypeStruct((s, d), BF16)] * 4, compiler_params=_params("parallel"),
    )(dm, ga, gb, pa, pb)


def _shift_down(u, k):
    row = lax.broadcasted_iota(jnp.int32, u.shape, 0)
    return jnp.where(row >= k, pltpu.roll(u, k, 0), 0.0)


def _shift_up(u, k):
    n = u.shape[0]
    row = lax.broadcasted_iota(jnp.int32, u.shape, 0)
    return jnp.where(row < n - k, pltpu.roll(u, n - k, 0), 0.0)


def _conv3(u, wc, b):
    return wc[0:1, :] * _shift_down(u, 2) + wc[1:2, :] * _shift_down(u, 1) + wc[2:3, :] * u + b


def _conv_glu_fwd(u, wc, b, tn=256):
    s, f2 = u.shape
    f = f2 // 2
    nb = f // tn

    def body(ug_ref, uv_ref, wg_ref, wv_ref, bg_ref, bv_ref, o_ref):
        cg = _conv3(ug_ref[...], wg_ref[...], bg_ref[...])
        cv = _conv3(uv_ref[...], wv_ref[...], bv_ref[...])
        o_ref[...] = (cg * jax.nn.sigmoid(cg) * cv).astype(o_ref.dtype)

    def cols(rows, off):
        return pl.BlockSpec((rows, tn), lambda j: (0, j + off))

    return pl.pallas_call(
        body, name="conv_glu_fwd", grid=(nb,),
        in_specs=[cols(s, 0), cols(s, nb), cols(3, 0), cols(3, nb), cols(1, 0), cols(1, nb)],
        out_specs=cols(s, 0), out_shape=jax.ShapeDtypeStruct((s, f), BF16),
        compiler_params=_params("parallel"),
    )(u, u, wc, wc, b, b)


def _conv_glu_bwd(u, da, wc, b, tn=256):
    s, f2 = u.shape
    f = f2 // 2
    nb = f // tn

    def body(ug_ref, uv_ref, da_ref, wg_ref, wv_ref, bg_ref, bv_ref, dug_ref, duv_ref, sg_ref, sv_ref):
        ug, uv, wg, wv = ug_ref[...], uv_ref[...], wg_ref[...], wv_ref[...]
        cg = _conv3(ug, wg, bg_ref[...])
        cv = _conv3(uv, wv, bv_ref[...])
        sig = jax.nn.sigmoid(cg)
        dav = da_ref[...]
        dcv = dav * (cg * sig)
        dcg = dav * cv * (sig * (1.0 + cg * (1.0 - sig)))
        for dc, uu, w, du_ref, st_ref in ((dcg, ug, wg, dug_ref, sg_ref), (dcv, uv, wv, duv_ref, sv_ref)):
            du = w[2:3, :] * dc + w[1:2, :] * _shift_up(dc, 1) + w[0:1, :] * _shift_up(dc, 2)
            du_ref[...] = du.astype(BF16)
            st_ref[...] = jnp.zeros_like(st_ref)
            st_ref[0:1, :] = jnp.sum(dc * _shift_down(uu, 2), axis=0, keepdims=True)
            st_ref[1:2, :] = jnp.sum(dc * _shift_down(uu, 1), axis=0, keepdims=True)
            st_ref[2:3, :] = jnp.sum(dc * uu, axis=0, keepdims=True)
            st_ref[3:4, :] = jnp.sum(dc, axis=0, keepdims=True)

    def cols(rows, off):
        return pl.BlockSpec((rows, tn), lambda j: (0, j + off))

    return pl.pallas_call(
        body, name="conv_glu_bwd", grid=(nb,),
        in_specs=[cols(s, 0), cols(s, nb), cols(s, 0), cols(3, 0), cols(3, nb), cols(1, 0), cols(1, nb)],
        out_specs=[cols(s, 0), cols(s, 0), cols(8, 0), cols(8, 0)],
        out_shape=[jax.ShapeDtypeStruct((s, f), BF16), jax.ShapeDtypeStruct((s, f), BF16),
                   jax.ShapeDtypeStruct((8, f), F32), jax.ShapeDtypeStruct((8, f), F32)],
        compiler_params=_params("parallel"),
    )(u, u, da, wc, wc, b, b)


def _loss_head(y, target, tm=256):
    s, d = y.shape

    def body(y_ref, t_ref, dyf_ref, dyb_ref, l_ref):
        @pl.when(pl.program_id(0) == 0)
        def _():
            l_ref[...] = jnp.zeros_like(l_ref)

        err = y_ref[...] - t_ref[...]
        dy = err * (1.0 / d)
        dyf_ref[...] = dy
        dyb_ref[...] = dy.astype(BF16)
        l_ref[...] += 0.5 * jnp.sum(jnp.sum(err * err, axis=-1, keepdims=True) * (1.0 / d), axis=0, keepdims=True)

    row = pl.BlockSpec((tm, d), lambda i: (i, 0))
    return pl.pallas_call(
        body, name="loss_head", grid=(s // tm,), in_specs=[row, row],
        out_specs=[row, row, pl.BlockSpec((8, LANES), lambda i: (0, 0))],
        out_shape=[jax.ShapeDtypeStruct((s, d), F32), jax.ShapeDtypeStruct((s, d), BF16),
                   jax.ShapeDtypeStruct((8, LANES), F32)],
        compiler_params=_params("arbitrary"),
    )(y, target)


ROW_TILES = (256, 128, 64, 32, 16, 8)
BLOCK_BYTES = 2 << 20


def _add_halves(g, r1, place):
    ns, r, c = g.shape
    rh = r // 2
    tr = _pick(rh, ROW_TILES)
    g4 = g.reshape(ns, 2, rh, c)

    def body(p_ref, g_ref, r_ref, o_ref):
        o_ref[...] = (g_ref[...].astype(F32) + r_ref[...].astype(F32)).astype(o_ref.dtype)

    def slab(s, pr):
        return s + (s >= pr[0]).astype(jnp.int32)

    return pl.pallas_call(
        body, name="add_halves",
        grid_spec=pltpu.PrefetchScalarGridSpec(
            num_scalar_prefetch=1, grid=(ns - 1, rh // tr),
            in_specs=[pl.BlockSpec((None, None, tr, c), lambda s, i, pr: (slab(s, pr), pr[1], i, 0)),
                      pl.BlockSpec((None, tr, c), lambda s, i, pr: (slab(s, pr), i, 0))],
            out_specs=pl.BlockSpec((None, tr, c), lambda s, i, pr: (slab(s, pr), i, 0))),
        out_shape=jax.ShapeDtypeStruct((ns, rh, c), BF16),
        compiler_params=_params("parallel", "parallel"),
    )(place, g4, r1)


def _sum_chips(g, r1, recv, place):
    ns, r, c = g.shape
    rh = r // 2
    tr = _pick(rh, ROW_TILES)
    g4 = g.reshape(ns, 2, rh, c)

    def body(p_ref, g_ref, r_ref, t0_ref, t1_ref, t2_ref, o_ref):
        own = g_ref[...].astype(F32) + r_ref[...].astype(F32)
        o_ref[...] = ((own + t0_ref[...].astype(F32)) + t1_ref[...].astype(F32)) + t2_ref[...].astype(F32)

    def peer(k):
        return pl.BlockSpec((None, tr, c), lambda i, pr: (k, i, 0))

    return pl.pallas_call(
        body, name="sum_chips",
        grid_spec=pltpu.PrefetchScalarGridSpec(
            num_scalar_prefetch=1, grid=(rh // tr,),
            in_specs=[pl.BlockSpec((None, None, tr, c), lambda i, pr: (pr[0], pr[1], i, 0)),
                      pl.BlockSpec((None, tr, c), lambda i, pr: (pr[0], i, 0)), peer(0), peer(1), peer(2)],
            out_specs=pl.BlockSpec((tr, c), lambda i, pr: (pr[1] * (rh // tr) + i, 0))),
        out_shape=jax.ShapeDtypeStruct((r, c), F32),
        compiler_params=_params("parallel"),
    )(place, g4, r1, recv, recv, recv)


def _sum_devices(packs):
    n, r, c = packs.shape

    def body(p_ref, o_ref):
        acc = p_ref[0]
        for d in range(1, n):
            acc = acc + p_ref[d]
        o_ref[...] = acc

    return pl.pallas_call(
        body, name="sum_devices", out_shape=jax.ShapeDtypeStruct((r, c), F32), compiler_params=_params(),
    )(packs)


def _adamw_update(wv, gv, mv, vv):
    c1 = 1.0 - ADAM_B1 ** ADAM_STEP
    c2 = 1.0 - ADAM_B2 ** ADAM_STEP
    mn = ADAM_B1 * mv + (1.0 - ADAM_B1) * gv
    vn = ADAM_B2 * vv + (1.0 - ADAM_B2) * (gv * gv)
    m_hat = mn / c1
    v_hat = vn / c2
    return -ADAM_LR * (m_hat / (jnp.sqrt(v_hat) + ADAM_EPS) + ADAM_WD * wv), mn, vn


def _adamw(w, g, m, v, name, deps=(), emit_grad=False):
    r, c = w.shape
    tr = _pick(r, [t for t in ROW_TILES if t * c * 4 <= BLOCK_BYTES]) if r >= 8 else r
    n_out = 4 if emit_grad else 3

    def body(w_ref, g_ref, m_ref, v_ref, *rest):
        outs = rest[-n_out:]
        gv = g_ref[:, :c]
        if emit_grad:
            outs[0][...] = gv
        outs[-3][...], outs[-2][...], outs[-1][...] = _adamw_update(w_ref[...], gv, m_ref[...], v_ref[...])

    blk = pl.BlockSpec((tr, c), lambda i: (i, 0))
    g_blk = pl.BlockSpec((tr, g.shape[1]), lambda i: (i, 0))
    return pl.pallas_call(
        body, name=name, grid=(r // tr,), in_specs=[blk, g_blk, blk, blk] + [ANY] * len(deps), out_specs=[blk] * n_out,
        out_shape=[jax.ShapeDtypeStruct((r, c), F32)] * n_out, compiler_params=_params("parallel"),
    )(w, g, m, v, *deps)


ANY = pl.BlockSpec(memory_space=pl.ANY)


def _place():
    x, y, c = lax.axis_index("x"), lax.axis_index("y"), lax.axis_index("c")
    chips = [(1 - x, y), (x, 1 - y), (1 - x, 1 - y)]
    return x, y, c, chips


def _remote(src, dst, send_sem, recv_sem, to):
    return pltpu.make_async_remote_copy(src_ref=src, dst_ref=dst, send_sem=send_sem, recv_sem=recv_sem,
                                        device_id=to, device_id_type=MESH)


HBM = pl.BlockSpec(memory_space=pltpu.HBM)
SEM = pl.BlockSpec(memory_space=pltpu.SEMAPHORE)
EFFECT = pltpu.SideEffectType.DATAFLOW_SIDE_EFFECTING


def _in_hbm(a):
    return pltpu.with_memory_space_constraint(a, pltpu.HBM)


def _half(ref_rows, who):
    return pl.ds(who * (ref_rows // 2), ref_rows // 2)


def _gather_start(groups):
    items = [it for g in groups for it in g]
    n = len(items)
    sizes = [len(g) for g in groups]

    def body(*refs):
        srcs, lands = refs[:n], refs[n:2 * n]
        sems = refs[2 * n:2 * n + 2 * len(groups)]
        token = refs[-1]
        x, y, c, chips = _place()
        j = 2 * x + y
        at = 0
        for gi, g in enumerate(groups):
            send, recv = sems[2 * gi], sems[2 * gi + 1]
            for i, (shard, split) in enumerate(g):
                src, land = srcs[at], lands[at]
                at += 1
                rows = _half(shard.shape[0], c) if split else slice(None)
                for k, chip in enumerate(chips):
                    _remote(src.at[rows], land.at[j, rows], send.at[4 * i + k], recv.at[4 * i + k], (*chip, c)).start()
                _remote(src, land.at[j], send.at[4 * i + 3], recv.at[4 * i + 3], (x, y, 1 - c)).start()
        token[...] = jnp.zeros_like(token)

    sem_shapes = []
    for sz in sizes:
        sem_shapes += [pltpu.SemaphoreType.DMA((4 * sz,)), pltpu.SemaphoreType.DMA((4 * sz,))]
    out_shape = (sem_shapes + [pltpu.HBM(sh.shape, sh.dtype) for sh, _ in items]
                 + [pltpu.HBM((N_CHIPS,) + sh.shape, sh.dtype) for sh, _ in items]
                 + [jax.ShapeDtypeStruct((8, LANES), F32)])
    ns = len(sem_shapes)
    outs = pl.pallas_call(
        body, name="gather_start", in_specs=[HBM] * (2 * n),
        out_specs=[SEM] * ns + [HBM] * (2 * n) + [pl.BlockSpec(memory_space=pltpu.VMEM)],
        out_shape=out_shape, input_output_aliases={i: ns + i for i in range(2 * n)},
        compiler_params=pltpu.CompilerParams(has_side_effects=EFFECT),
    )(*[_in_hbm(sh) for sh, _ in items], *[_in_hbm(lax.empty((N_CHIPS,) + sh.shape, sh.dtype)) for sh, _ in items])
    sems, shards, lands, token = outs[:ns], outs[ns:ns + n], outs[ns + n:ns + 2 * n], outs[-1]
    res, at = [], 0
    for gi, sz in enumerate(sizes):
        res.append((shards[at:at + sz], lands[at:at + sz], sems[2 * gi], sems[2 * gi + 1]))
        at += sz
    return res, token


def _gather_pass(group, started, after, name):
    shards, lands, send, recv = started
    n = len(group)
    split_ix = [i for i, (_, split) in enumerate(group) if split]

    def body(*refs):
        lnds, send1, recv1 = refs[n:2 * n], refs[2 * n], refs[2 * n + 1]
        outs = refs[2 * n + 2 + len(after):]
        send2, recv2, token = outs[2 * n], outs[2 * n + 1], outs[2 * n + 2]
        x, y, c, chips = _place()
        sib = (x, y, 1 - c)
        for i, (shard, split) in enumerate(group):
            rows = _half(shard.shape[0], c) if split else slice(None)
            for k, (cx, cy) in enumerate(chips):
                landed = lnds[i].at[2 * cx + cy, rows]
                cp = _remote(landed, landed, send1.at[4 * i + k], recv1.at[4 * i + k], sib)
                cp.wait_send()
                cp.wait_recv()
            own = lnds[i].at[2 * x + y]
            cp = _remote(own, own, send1.at[4 * i + 3], recv1.at[4 * i + 3], sib)
            cp.wait_send()
            cp.wait_recv()
        for i2, i in enumerate(split_ix):
            rows = _half(group[i][0].shape[0], c)
            for k, (cx, cy) in enumerate(chips):
                landed = lnds[i].at[2 * cx + cy, rows]
                _remote(landed, landed, send2.at[3 * i2 + k], recv2.at[3 * i2 + k], sib).start()
        token[...] = jnp.zeros_like(token)

    n2 = len(split_ix)
    out_shape = ([pltpu.HBM(a.shape, a.dtype) for a in (*shards, *lands)]
                 + [pltpu.SemaphoreType.DMA((3 * n2,)), pltpu.SemaphoreType.DMA((3 * n2,)), jax.ShapeDtypeStruct((8, LANES), F32)])
    outs = pl.pallas_call(
        body, name=name, in_specs=[HBM] * (2 * n) + [SEM, SEM] + [ANY] * len(after),
        out_specs=[HBM] * (2 * n) + [SEM, SEM, pl.BlockSpec(memory_space=pltpu.VMEM)],
        out_shape=out_shape, input_output_aliases={i: i for i in range(2 * n)},
        compiler_params=pltpu.CompilerParams(has_side_effects=EFFECT),
    )(*shards, *lands, send, recv, *after)
    return outs[:n], (outs[n:2 * n], outs[2 * n], outs[2 * n + 1]), outs[2 * n + 2]


def _gather_wait(group, passed, after, name):
    lands, send2, recv2 = passed
    n = len(group)
    split_ix = [i for i, (_, split) in enumerate(group) if split]

    def body(*refs):
        lnds, s2, r2 = refs[:n], refs[n], refs[n + 1]
        x, y, c, chips = _place()
        sib = (x, y, 1 - c)
        for i2, i in enumerate(split_ix):
            rows = _half(group[i][0].shape[0], 1 - c)
            for k, (cx, cy) in enumerate(chips):
                landed = lnds[i].at[2 * cx + cy, rows]
                cp = _remote(landed, landed, s2.at[3 * i2 + k], r2.at[3 * i2 + k], sib)
                cp.wait_send()
                cp.wait_recv()

    return pl.pallas_call(
        body, name=name, in_specs=[HBM] * n + [SEM, SEM, ANY], out_specs=[HBM] * n,
        out_shape=[pltpu.HBM(a.shape, a.dtype) for a in lands], input_output_aliases={i: i for i in range(n)},
        compiler_params=pltpu.CompilerParams(has_side_effects=EFFECT),
    )(*lands, send2, recv2, after)


def _xfer_start(name, srcs, land_shapes, n_copies, copies, after):
    n, nl = len(srcs), len(land_shapes)

    def body(*refs):
        src_refs, land_refs = refs[:n], refs[n:n + nl]
        send, recv, token = refs[n + nl + 1], refs[n + nl + 2], refs[-1]
        for cp in copies(src_refs, land_refs, send, recv):
            cp.start()
        token[...] = jnp.zeros_like(token)

    lands = [_in_hbm(lax.empty(shape, dtype)) for shape, dtype in land_shapes]
    out_shape = ([pltpu.SemaphoreType.DMA((n_copies,)), pltpu.SemaphoreType.DMA((n_copies,))]
                 + [pltpu.HBM(a.shape, a.dtype) for a in (*srcs, *lands)] + [jax.ShapeDtypeStruct((8, LANES), F32)])
    outs = pl.pallas_call(
        body, name=name, in_specs=[HBM] * (n + nl) + [ANY],
        out_specs=[SEM, SEM] + [HBM] * (n + nl) + [pl.BlockSpec(memory_space=pltpu.VMEM)],
        out_shape=out_shape, input_output_aliases={i: 2 + i for i in range(n + nl)},
        compiler_params=pltpu.CompilerParams(has_side_effects=EFFECT),
    )(*[_in_hbm(a) for a in srcs], *lands, after)
    return (outs[2:2 + n], outs[2 + n:2 + n + nl], outs[0], outs[1]), outs[-1]


def _xfer_wait(name, started, copies, after):
    srcs, lands, send, recv = started
    n, nl = len(srcs), len(lands)

    def body(*refs):
        src_refs, land_refs, s_ref, r_ref = refs[:n], refs[n:n + nl], refs[n + nl], refs[n + nl + 1]
        for cp in copies(src_refs, land_refs, s_ref, r_ref):
            cp.wait_send()
            cp.wait_recv()

    outs = pl.pallas_call(
        body, name=name, in_specs=[HBM] * (n + nl) + [SEM, SEM, ANY], out_specs=[HBM] * (n + nl),
        out_shape=[pltpu.HBM(a.shape, a.dtype) for a in (*srcs, *lands)],
        input_output_aliases={i: i for i in range(n + nl)},
        compiler_params=pltpu.CompilerParams(has_side_effects=EFFECT),
    )(*srcs, *lands, send, recv, after)
    return outs[:n], outs[n:]


def _swap_copies(srcs, lands, send, recv):
    x, y, c, _ = _place()
    return [_remote(src.at[:, _half(src.shape[1], 1 - c)], land, send.at[i], recv.at[i], (x, y, 1 - c))
            for i, (src, land) in enumerate(zip(srcs, lands))]


def _scatter_copies(srcs, lands, send, recv):
    x, y, c, chips = _place()
    return [_remote(src.at[2 * cx + cy], land.at[k], send.at[3 * i + k], recv.at[3 * i + k], (cx, cy, c))
            for i, (src, land) in enumerate(zip(srcs, lands)) for k, (cx, cy) in enumerate(chips)]


def _join_copies(srcs, lands, send, recv):
    x, y, c, _ = _place()
    return [_remote(src.at[_half(src.shape[0], c)], src.at[_half(src.shape[0], c)], send.at[i], recv.at[i], (x, y, 1 - c))
            for i, src in enumerate(srcs)]


def _corner(a):
    return a[(slice(0, 1),) * a.ndim]


class _Reducer:
    def __init__(self, place):
        self.place = place
        self.state = {}

    def swap(self, key, grads, after):
        shapes = [((g.shape[0], g.shape[1] // 2, g.shape[2]), g.dtype) for g in grads]
        self.state[key], token = _xfer_start("swap_start_" + key, grads, shapes, len(grads), _swap_copies, _corner(after))
        return token

    def to_chips(self, key, after):
        grads, from_sibling = _xfer_wait("swap_wait_" + key, self.state[key], _swap_copies, after)
        sums = [_add_halves(g, r, self.place) for g, r in zip(grads, from_sibling)]
        shapes = [((3,) + s.shape[1:], s.dtype) for s in sums]
        started, token = _xfer_start("scatter_start_" + key, sums, shapes, 3 * len(sums), _scatter_copies, _corner(sums[-1]))
        self.state[key] = (grads, from_sibling, started)
        return token

    def to_core(self, key, after):
        grads, from_sibling, started = self.state[key]
        _, from_chips = _xfer_wait("scatter_wait_" + key, started, _scatter_copies, after)
        shards = [_sum_chips(g, r, rc, self.place) for g, r, rc in zip(grads, from_sibling, from_chips)]
        self.state[key], token = _xfer_start("join_start_" + key, shards, [], len(shards), _join_copies, _corner(shards[-1]))
        return token

    def finish(self, key, after):
        return _xfer_wait("join_wait_" + key, self.state.pop(key), _join_copies, after)[0]


def _gather_packs(pack, deps=()):
    def body(p_ref, *rest):
        o_ref, lsem, ssem, rsem = rest[-4:]
        x, y, c, _ = _place()
        me = 4 * x + 2 * y + c
        local = pltpu.make_async_copy(p_ref, o_ref.at[me], lsem)
        local.start()
        cps = []
        for k in range(1, N_DEV):
            fx, fy, fc = (k >> 2) & 1, (k >> 1) & 1, k & 1
            to = (x ^ fx, y ^ fy, c ^ fc)
            cps.append(_remote(p_ref, o_ref.at[me], ssem.at[k - 1], rsem.at[k - 1], to))
        for cp in cps:
            cp.start()
        for k in range(1, N_DEV):
            fx, fy, fc = (k >> 2) & 1, (k >> 1) & 1, k & 1
            src = o_ref.at[4 * (x ^ fx) + 2 * (y ^ fy) + (c ^ fc)]
            _remote(src, src, ssem.at[k - 1], rsem.at[k - 1], (x, y, c)).wait_recv()
        for cp in cps:
            cp.wait_send()
        local.wait()

    return pl.pallas_call(
        body, name="gather_packs", in_specs=[ANY] * (1 + len(deps)), out_specs=ANY,
        out_shape=jax.ShapeDtypeStruct((N_DEV,) + pack.shape, pack.dtype),
        scratch_shapes=[pltpu.SemaphoreType.DMA, pltpu.SemaphoreType.DMA((N_DEV - 1,)), pltpu.SemaphoreType.DMA((N_DEV - 1,))],
    )(pack, *deps)


LANE_TILES = (512, 896, 1408, 704, 384, 256, 128)


def _layer_grads(x, target, small, wg, rest_pass, rest_wait, red, filler):
    s, d = x.shape
    f = wg["conv"].shape[1] // 2
    w_att = N_HEADS * HEAD_DIM
    in_splits = (w_att, w_att, w_att, N_HEADS, w_att, w_att, w_att, d, d)
    in_cols = sum(in_splits)
    cs = in_cols // N_CHIPS
    cp = wg["in"].shape[2]
    tm = min(s, 1024)
    t_in = cp
    t_d = _pick(d, LANE_TILES)
    t_d2 = min(d, 1024)
    t_dq = _pick(d // N_CHIPS, LANE_TILES)
    t_w = _pick(w_att, LANE_TILES)
    t_up = 2 * f // N_CHIPS
    tm_wide = min(s, 512)
    t_fq = _pick(f // N_CHIPS, LANE_TILES)
    offs = np.cumsum(in_splits)[:-1].tolist()

    h1 = _norm_fwd(x, small["g_attn"], group=d, name="rms1_fwd")
    proj_p = _mm(h1, wg["in"], mode="nn", b_kind="col", tm=tm_wide, tn=t_in, tk=d, name="mm_in")
    gains = {n: small[n].reshape(1, w_att) for n in ("g_q_fox", "g_k_fox", "g_q_dil", "g_k_dil")}
    qa, ka, va_b, fa, qb, kb, vb_b, ga, gb, qa_n, ka_n, qb_n, kb_n = _proj_split(
        proj_p, in_splits, cs, (F32, F32, BF16, F32, F32, F32, BF16, F32, F32),
        {0: gains["g_q_fox"], 1: gains["g_k_fox"], 4: gains["g_q_dil"], 5: gains["g_k_dil"]})
    fa_t = fa.T
    b_f = small["b_forget"].reshape(N_HEADS, 1)
    c_f = _forget_fwd(fa_t, b_f)
    slopes = jnp.asarray(2.0 ** (-8.0 * np.arange(1, N_HEADS + 1) / N_HEADS), dtype=F32)
    a_d = -(slopes[:, None] * jnp.arange(s, dtype=F32)[None, :])
    rows_f, cols_f = c_f[:, :, None], c_f[:, None, :]
    rows_d, cols_d = a_d[:, :, None], a_d[:, None, :]
    o_a, o_a32, lse_a = _attn_fwd(qa_n, ka_n, va_b, rows_f, cols_f, dilated=False, name="attn_fox_fwd")
    token = rest_pass("mid", o_a)
    rows_d = rows_d + token[0, 0]
    o_b, o_b32, lse_b = _attn_fwd(qb_n, kb_n, vb_b, rows_d, cols_d, dilated=True, name="attn_dil_fwd")
    wg = dict(wg, **rest_wait("mid", o_b))
    token = rest_pass("late", o_b)
    pa = _mm(o_a, wg["brf"], mode="nn", b_kind="col", tm=tm, tn=t_dq, tk=w_att, name="mm_brf", deps=(token,))
    pb = _mm(o_b, wg["brd"], mode="nn", b_kind="col", tm=tm, tn=t_dq, tk=w_att, name="mm_brd")
    merged = _gate_fwd(ga, gb, pa, pb)
    x1 = _mm(merged, wg["out"], mode="nn", b_kind="row", res=x, tm=tm, tn=t_d, tk=t_dq, name="mm_out")
    wg = dict(wg, **rest_wait("late", x1))
    h2 = _norm_fwd(x1, small["g_ffn"], group=d, name="rms2_fwd")
    u = _mm(h2, wg["up"], mode="nn", b_kind="col", tm=tm_wide, tn=t_up, tk=d, name="mm_up")
    act = _conv_glu_fwd(u, wg["conv"], wg["bconv"])
    y = _mm(act, wg["down"], mode="nn", b_kind="row", res=x1, tm=tm, tn=t_d2, tk=t_fq, name="mm_down")
    dy_f, dy_b, loss_blk = _loss_head(y, target)

    d_act = _mm(dy_b, wg["down"], mode="nt", b_kind="row", tm=tm, tn=t_fq, tk=d, name="mm_down_dx")
    g_down = _mm(act, dy_b, mode="tn", out_dtype=BF16, out_kind="row", tm=t_fq, tn=t_d2, tk=s, name="mm_down_dw")
    tok = red.swap("down", [g_down], g_down)
    du_g, du_v, st_g, st_v = _conv_glu_bwd(u, d_act, wg["conv"] + tok[0, 0], wg["bconv"])
    tok = red.to_chips("down", du_g)
    du = (du_g, du_v)
    g_up = _mm(h2, du, mode="tn", out_dtype=BF16, out_kind="col", tm=t_d2, tn=t_up // 2, tk=s, name="mm_up_dw", deps=(tok,))
    tok = red.to_core("down", g_up)
    tok2 = red.swap("up", [g_up], g_up)
    dh2 = _mm(du, wg["up"], mode="nt", b_kind="col", tm=tm, tn=t_d2, tk=t_up, name="mm_up_dx", deps=(tok, tok2))
    tok = red.to_chips("up", dh2)
    dx1_b, dx1_f, dg_ffn = _norm_bwd(dh2, x1, small["g_ffn"], group=d, res=dy_f, out_dtypes=(BF16, F32), name="rms2_bwd")
    d_merged = _mm(dx1_b, wg["out"], mode="nt", b_kind="row", tm=tm, tn=t_dq, tk=d, name="mm_out_dx", deps=(tok,))
    g_out = _mm(merged, dx1_b, mode="tn", out_dtype=BF16, out_kind="row", tm=t_dq, tn=t_d2, tk=s, name="mm_out_dw")
    dpa, dpb, dga, dgb = _gate_bwd(d_merged, ga, gb, pa, pb)
    do_a = _mm(dpa, wg["brf"], mode="nt", b_kind="col", out_dtype=BF16, tm=s, tn=w_att, tk=t_dq, name="mm_brf_dx")
    do_b = _mm(dpb, wg["brd"], mode="nt", b_kind="col", out_dtype=BF16, tm=s, tn=w_att, tk=t_dq, name="mm_brd_dx")
    g_brf = _mm(o_a, dpa, mode="tn", out_dtype=BF16, out_kind="col", tm=w_att, tn=t_dq, tk=s, name="mm_brf_dw")
    g_brd = _mm(o_b, dpb, mode="tn", out_dtype=BF16, out_kind="col", tm=w_att, tn=t_dq, tk=s, name="mm_brd_dw")
    tok = red.swap("mix", [g_out, g_brf, g_brd], g_brd)
    dqa_n, dka_n, dva, dac_a = _attn_bwd(qa_n, ka_n, va_b, o_a32, do_a, lse_a, rows_f + tok[0, 0], cols_f, dilated=False, name="attn_fox_bwd")
    tok = red.to_core("up", dqa_n)
    tok2 = red.to_chips("mix", dqa_n)
    dqb_n, dkb_n, dvb, _ = _attn_bwd(qb_n, kb_n, vb_b, o_b32, do_b, lse_b, rows_d + (tok[0, 0] + tok2[0, 0]), cols_d, dilated=True, name="attn_dil_bwd")
    tok = red.to_core("mix", dqb_n)
    dfa_t, db_f = _forget_bwd(dac_a[:, 0, :], fa_t, b_f)
    dproj_p, dgains = _dproj_merge(
        [dqa_n, dka_n, dva, dfa_t.T, dqb_n, dkb_n, dvb, dga, dgb], in_splits, cs, cp,
        {0: (qa, gains["g_q_fox"]), 1: (ka, gains["g_k_fox"]), 4: (qb, gains["g_q_dil"]), 5: (kb, gains["g_k_dil"])})
    dg_qf, dg_kf, dg_qd, dg_kd = dgains[0], dgains[1], dgains[4], dgains[5]
    g_in = _mm(h1, dproj_p, mode="tn", out_dtype=BF16, out_kind="col", tm=t_d2, tn=t_in, tk=s, name="mm_in_dw", deps=(tok,))
    tok = red.swap("in", [g_in], g_in)
    tok = red.to_chips("in", filler(tok))
    dh1 = _mm(dproj_p, wg["in"], mode="nt", b_kind="col", tm=tm, tn=t_d2, tk=t_in, name="mm_in_dx", deps=(tok,))
    grad_x, dg_attn = _norm_bwd(dh1, x, small["g_attn"], group=d, res=dx1_f, out_dtypes=(F32,), name="rms1_bwd")

    small_grads = {
        "g_attn": dg_attn, "b_forget": db_f.reshape(1, N_HEADS),
        "g_q_fox": dg_qf, "g_k_fox": dg_kf, "g_q_dil": dg_qd, "g_k_dil": dg_kd, "g_ffn": dg_ffn,
        "w_conv": jnp.concatenate([st_g[0:3], st_v[0:3]], axis=1),
        "b_conv": jnp.concatenate([st_g[3:4], st_v[3:4]], axis=1),
        "loss": loss_blk[0:1, 0:1],
    }
    return small_grads, grad_x


SMALL_ORDER = ("g_attn", "b_forget", "g_q_fox", "g_k_fox", "g_q_dil", "g_k_dil", "g_ffn", "w_conv", "b_conv", "loss")
WEIGHT_ORDER = ("g_attn", "w_in", "b_forget", "g_q_fox", "g_k_fox", "g_q_dil", "g_k_dil", "w_br_fox", "w_br_dil",
                "w_out", "g_ffn", "w_up", "w_conv", "b_conv", "w_down")
BIG = {"w_in": "in", "w_br_fox": "brf", "w_br_dil": "brd", "w_out": "out", "w_up": "up", "w_down": "down"}


def kernel(x, g_attn, w_in, b_forget, g_q_fox, g_k_fox, g_q_dil, g_k_dil, w_br_fox, w_br_dil, w_out, g_ffn, w_up, w_conv, b_conv, w_down, loss_target, m_g_attn, m_w_in, m_b_forget, m_g_q_fox, m_g_k_fox, m_g_q_dil, m_g_k_dil, m_w_br_fox, m_w_br_dil, m_w_out, m_g_ffn, m_w_up, m_w_conv, m_b_conv, m_w_down, v_g_attn, v_w_in, v_b_forget, v_g_q_fox, v_g_k_fox, v_g_q_dil, v_g_k_dil, v_w_br_fox, v_w_br_dil, v_w_out, v_g_ffn, v_w_up, v_w_conv, v_b_conv, v_w_down):
    w = dict(g_attn=g_attn, w_in=w_in, b_forget=b_forget, g_q_fox=g_q_fox, g_k_fox=g_k_fox, g_q_dil=g_q_dil,
             g_k_dil=g_k_dil, w_br_fox=w_br_fox, w_br_dil=w_br_dil, w_out=w_out, g_ffn=g_ffn, w_up=w_up,
             w_conv=w_conv, b_conv=b_conv, w_down=w_down)
    m = dict(g_attn=m_g_attn, w_in=m_w_in, b_forget=m_b_forget, g_q_fox=m_g_q_fox, g_k_fox=m_g_k_fox,
             g_q_dil=m_g_q_dil, g_k_dil=m_g_k_dil, w_br_fox=m_w_br_fox, w_br_dil=m_w_br_dil, w_out=m_w_out,
             g_ffn=m_g_ffn, w_up=m_w_up, w_conv=m_w_conv, b_conv=m_b_conv, w_down=m_w_down)
    v = dict(g_attn=v_g_attn, w_in=v_w_in, b_forget=v_b_forget, g_q_fox=v_g_q_fox, g_k_fox=v_g_k_fox,
             g_q_dil=v_g_q_dil, g_k_dil=v_g_k_dil, w_br_fox=v_w_br_fox, w_br_dil=v_w_br_dil, w_out=v_w_out,
             g_ffn=v_g_ffn, w_up=v_w_up, w_conv=v_w_conv, b_conv=v_b_conv, w_down=v_w_down)
    xi, yi, ci = lax.axis_index("x"), lax.axis_index("y"), lax.axis_index("c")
    chip = (2 * xi + yi).astype(jnp.int32)

    cs = w_in.shape[2]
    cp = _round_up(cs, LANES)
    shards = {
        "in": jnp.pad(w_in[0].astype(BF16), ((0, 0), (0, cp - cs))),
        "brf": w_br_fox[0].astype(BF16), "brd": w_br_dil[0].astype(BF16), "out": w_out[0].astype(BF16),
        "up": w_up[0].astype(BF16), "down": w_down[0].astype(BF16),
    }
    names = tuple(shards)
    conv_pad = jnp.pad(w_conv[0], ((0, 8 - w_conv.shape[1]), (0, 0)))
    first = [(shards["in"], True), (conv_pad, False)]
    later = {"mid": ("brf", "brd", "out"), "late": ("up", "down")}
    groups = {key: [(shards[n], True) for n in members] for key, members in later.items()}
    (started_first, *started_later), token = _gather_start([first, *groups.values()])
    started = dict(zip(later, started_later))
    token, w["w_in"], m["w_in"], v["w_in"] = lax.optimization_barrier((token, w["w_in"], m["w_in"], v["w_in"]))
    w2, m2, v2 = ({n: a[n].reshape(a[n].shape[-2], a[n].shape[-1]) for n in BIG} for a in (w, m, v))
    early = (token, w2["w_in"], m2["w_in"], v2["w_in"])
    own_first, passed_first, token = _gather_pass(first, started_first, early, "gather_pass_in")
    land_in, land_conv = _gather_wait(first, passed_first, token, "gather_wait_in")
    wg = {"in": land_in, "bconv": b_conv,
          "conv": jnp.transpose(land_conv[:, :w_conv.shape[1], :], (1, 0, 2)).reshape(w_conv.shape[1], -1)}
    small = {n: w[n] for n in ("g_attn", "b_forget", "g_q_fox", "g_k_fox", "g_q_dil", "g_k_dil", "g_ffn")}
    small = {n: (a[0] if a.ndim == 3 else a) for n, a in small.items()}
    in_flight = {}

    def rest_pass(key, after):
        own, passed, tok = _gather_pass(groups[key], started[key], (after,), "gather_pass_" + key)
        in_flight[key] = (own, passed)
        return tok

    def rest_wait(key, after):
        own, passed = in_flight.pop(key)
        lands = _gather_wait(groups[key], passed, after, "gather_wait_" + key)
        return dict(zip(later[key], lands))

    reducer = _Reducer(jnp.stack([chip, ci.astype(jnp.int32)]))
    g_out, d_out, m_out, v_out = {}, {}, {}, {}
    reduced = {}

    def first_element(arrays):
        return jnp.stack([a[(0,) * a.ndim] for a in arrays])

    def update_big(n, deps):
        g2, dl, mn, vn = _adamw(w2[n], reduced[BIG[n]], m2[n], v2[n], name="adamw_" + n, deps=deps, emit_grad=True)
        g_out[n], d_out[n], m_out[n], v_out[n] = (a.reshape(w[n].shape) for a in (g2, dl, mn, vn))

    def update_down(tok):
        (reduced["down"],) = reducer.finish("down", tok)
        update_big("w_down", (tok,))
        return v_out["w_down"]

    small_grads, grad_x = _layer_grads(x[0], loss_target[0], small, wg, rest_pass, rest_wait, reducer, update_down)

    for key, members in (("up", ("up",)), ("mix", ("out", "brf", "brd"))):
        reduced.update(zip(members, reducer.finish(key, grad_x)))
    others = ("w_up", "w_out", "w_br_fox", "w_br_dil")
    for n in others:
        update_big(n, (grad_x,))

    flat = jnp.concatenate([small_grads[n].reshape(-1) for n in SMALL_ORDER])
    rows = _round_up(flat.shape[0], 8 * LANES) // LANES
    pack = jnp.pad(flat, (0, rows * LANES - flat.shape[0])).reshape(rows, LANES)
    packs = _gather_packs(pack, deps=(first_element([v_out[n] for n in others]),))
    total = _sum_devices(packs).reshape(-1)
    red, at = {}, 0
    for n in SMALL_ORDER:
        size = small_grads[n].size
        red[n] = total[at:at + size].reshape(small_grads[n].shape)
        at += size
    loss = red["loss"].reshape(())
    c2 = w_conv.shape[2]
    red["w_conv"] = lax.dynamic_slice_in_dim(red["w_conv"], chip * c2, c2, axis=1)

    smalls = [n for n in WEIGHT_ORDER if n not in BIG]
    for n in smalls:
        shape = w[n].shape
        r2 = (shape[-2], shape[-1]) if n not in ("g_attn", "b_forget", "g_ffn", "b_conv") else (1, shape[-1])
        g2 = red[n].reshape(r2)
        dl, mn, vn = _adamw(w[n].reshape(r2), g2, m[n].reshape(r2), v[n].reshape(r2), name="adamw_" + n)
        g_out[n], d_out[n], m_out[n], v_out[n] = (a.reshape(shape) for a in (g2, dl, mn, vn))
    tok = reducer.to_core("in", first_element([v_out[n] for n in smalls]))
    (reduced["in"],) = reducer.finish("in", tok)
    update_big("w_in", (tok,))

    return (loss, grad_x[None], *[g_out[n] for n in WEIGHT_ORDER], *[d_out[n] for n in WEIGHT_ORDER],
            *[m_out[n] for n in WEIGHT_ORDER], *[v_out[n] for n in WEIGHT_ORDER])
```

```python
import math

import jax
import jax.numpy as jnp
import numpy as np
from jax import lax
from jax.experimental import pallas as pl
from jax.experimental.pallas import tpu as pltpu

F32 = jnp.float32
BF16 = jnp.bfloat16
HEAD_DIM = 128
N_HEADS = 8
EPS = 1e-6
NEG = -1e30
N_CHIPS = 4
N_DEV = 8
LANES = 128
VMEM_LIMIT_BYTES = 56 * 1024 * 1024
DIL_PATTERNS = ((128, 1), (512, 4), (2048, 16))
ATTN_TILE = 512
ADAM_LR, ADAM_B1, ADAM_B2, ADAM_EPS, ADAM_WD, ADAM_STEP = 0.001, 0.9, 0.999, 1e-08, 0.01, 10
MESH = pl.DeviceIdType.MESH


def _params(*sem):
    return pltpu.CompilerParams(dimension_semantics=sem, vmem_limit_bytes=VMEM_LIMIT_BYTES)


def _round_up(n, m):
    return -(-n // m) * m


def _pick(dim, prefs):
    for p in prefs:
        if dim % p == 0:
            return p
    raise ValueError(f"no tile for {dim} in {prefs}")


def _logical_shape(arr, kind):
    if kind is None:
        return arr.shape
    s, r, c = arr.shape
    return (r, s * c) if kind == "col" else (s * r, c)


def _spec(shape, kind, br, bc, fi, fj):
    if kind is None:
        return pl.BlockSpec((br, bc), lambda *g: (fi(*g), fj(*g)))
    _, r, c = shape
    if kind == "col":
        nb = c // bc
        assert nb * bc == c, (shape, bc)
        return pl.BlockSpec((None, br, bc), lambda *g: (fj(*g) // nb, fi(*g), fj(*g) % nb))
    nb = r // br
    assert nb * br == r, (shape, br)
    return pl.BlockSpec((None, br, bc), lambda *g: (fi(*g) // nb, fi(*g) % nb, fj(*g)))


def _mm(a, b, *, mode, tm, tn, tk, name, a_kind=None, b_kind=None, out_kind=None,
        out_dtype=F32, res=None, deps=(), loss_target=None):
    pair_a, pair_b = isinstance(a, tuple), isinstance(b, tuple)
    if pair_a or pair_b:
        return _mm_pair(a, b, mode=mode, tm=tm, tn=tn, tk=tk, name=name, b_kind=b_kind, out_kind=out_kind,
                        out_dtype=out_dtype, deps=deps)
    la, lb = _logical_shape(a, a_kind), _logical_shape(b, b_kind)
    if mode == "nn":
        (m, k), (k2, n) = la, lb
    elif mode == "nt":
        (m, k), (n, k2) = la, lb
    else:
        (k, m), (k2, n) = la, lb
    assert k == k2, (name, la, lb)
    assert m % tm == 0 and n % tn == 0 and k % tk == 0, (name, m, n, k, tm, tn, tk)
    nk = k // tk
    im = lambda i, j, l: i
    jn = lambda i, j, l: j
    lk = lambda i, j, l: l
    if mode == "tn":
        a_spec = _spec(a.shape, a_kind, tk, tm, lk, im)
        dims = (((0,), (0,)), ((), ()))
    else:
        a_spec = _spec(a.shape, a_kind, tm, tk, im, lk)
        dims = (((1,), (1,)), ((), ())) if mode == "nt" else (((1,), (0,)), ((), ()))
    if mode == "nt":
        b_spec = _spec(b.shape, b_kind, tn, tk, jn, lk)
    else:
        b_spec = _spec(b.shape, b_kind, tk, tn, lk, jn)
    if out_kind is None:
        oshape = (m, n)
    elif out_kind == "col":
        oshape = (N_CHIPS, m, n // N_CHIPS)
    else:
        oshape = (N_CHIPS, m // N_CHIPS, n)
    o_spec = _spec(oshape, out_kind, tm, tn, im, jn)
    tile = pl.BlockSpec((tm, tn), lambda i, j, l: (i, j))
    in_specs = [a_spec, b_spec]
    args = [a, b]
    for extra in (res, loss_target):
        if extra is not None:
            in_specs.append(tile)
            args.append(extra)
    in_specs += [pl.BlockSpec(memory_space=pl.ANY)] * len(deps)
    args += list(deps)
    if loss_target is None:
        out_specs, out_shape = [o_spec], [jax.ShapeDtypeStruct(oshape, out_dtype)]
    else:
        assert out_kind is None and res is not None
        out_specs = [tile, tile, pl.BlockSpec((8, LANES), lambda i, j, l: (0, 0))]
        out_shape = [jax.ShapeDtypeStruct(oshape, F32), jax.ShapeDtypeStruct(oshape, BF16),
                     jax.ShapeDtypeStruct((8, LANES), F32)]
    n_in, n_out = len(args), len(out_specs)

    def finish(out, refs, first):
        res_ref = refs[2] if res is not None else None
        outs = refs[n_in:n_in + n_out]
        if res_ref is not None:
            out = out + res_ref[...]
        if loss_target is None:
            outs[0][...] = out.astype(outs[0].dtype)
            return

        @pl.when(first)
        def _():
            outs[2][...] = jnp.zeros_like(outs[2])

        err = out - refs[3][...]
        dy = err * (1.0 / n)
        outs[0][...] = dy
        outs[1][...] = dy.astype(BF16)
        outs[2][...] += 0.5 * jnp.sum(jnp.sum(err * err, axis=-1, keepdims=True) * (1.0 / n), axis=0, keepdims=True)

    def first_tile():
        return (pl.program_id(0) == 0) & (pl.program_id(1) == 0)

    def body_whole_k(*refs):
        finish(lax.dot_general(refs[0][...], refs[1][...], dims, preferred_element_type=F32), refs, first_tile())

    def body(*refs):
        acc_ref = refs[-1]
        step = pl.program_id(2)
        first = first_tile()

        @pl.when(step == 0)
        def _():
            acc_ref[...] = jnp.zeros_like(acc_ref)

        acc_ref[...] += lax.dot_general(refs[0][...], refs[1][...], dims, preferred_element_type=F32)

        @pl.when(step == nk - 1)
        def _():
            finish(acc_ref[...], refs, first)

    outs = pl.pallas_call(
        body_whole_k if nk == 1 else body, name=name, grid=(m // tm, n // tn, nk),
        in_specs=in_specs, out_specs=out_specs, out_shape=out_shape,
        scratch_shapes=[] if nk == 1 else [pltpu.VMEM((tm, tn), F32)],
        compiler_params=_params(*(["arbitrary"] * 3 if loss_target is not None else ["parallel", "parallel", "arbitrary"])),
    )(*args)
    return outs[0] if loss_target is None else outs


def _mm_pair(a, b, *, mode, tm, tn, tk, name, b_kind, out_kind, out_dtype, deps):
    anyspec = [pl.BlockSpec(memory_space=pl.ANY)] * len(deps)
    if mode == "tn":
        assert isinstance(b, tuple) and out_kind == "col" and a.shape[0] == tk
        k, m = a.shape
        n0 = b[0].shape[1]
        n, nb0 = 2 * n0, n0 // tn
        oshape = (N_CHIPS, m, n // N_CHIPS)

        def body(a_ref, b0_ref, b1_ref, *rest):
            o_ref = rest[-1]
            for first, b_ref in ((True, b0_ref), (False, b1_ref)):
                @pl.when((pl.program_id(1) < nb0) == first)
                def _():
                    o_ref[...] = lax.dot_general(a_ref[...], b_ref[...], (((0,), (0,)), ((), ())),
                                                 preferred_element_type=F32).astype(o_ref.dtype)

        return pl.pallas_call(
            body, name=name, grid=(m // tm, n // tn),
            in_specs=[pl.BlockSpec((tk, tm), lambda i, j: (0, i)),
                      pl.BlockSpec((tk, tn), lambda i, j: (0, jnp.minimum(j, nb0 - 1))),
                      pl.BlockSpec((tk, tn), lambda i, j: (0, jnp.maximum(j - nb0, 0)))] + anyspec,
            out_specs=_spec(oshape, "col", tm, tn, lambda i, j: i, lambda i, j: j),
            out_shape=jax.ShapeDtypeStruct(oshape, out_dtype), compiler_params=_params("parallel", "arbitrary"),
        )(a, *b, *deps)
    assert mode == "nt" and isinstance(a, tuple) and out_kind is None
    m, k0 = a[0].shape
    n = _logical_shape(b, b_kind)[0]
    nk0 = k0 // tk
    nk = 2 * nk0

    def body(a0_ref, a1_ref, b_ref, *rest):
        o_ref, acc_ref = rest[-2], rest[-1]
        step = pl.program_id(2)

        @pl.when(step == 0)
        def _():
            acc_ref[...] = jnp.zeros_like(acc_ref)

        for first, a_ref in ((True, a0_ref), (False, a1_ref)):
            @pl.when((step < nk0) == first)
            def _():
                acc_ref[...] += lax.dot_general(a_ref[...], b_ref[...], (((1,), (1,)), ((), ())), preferred_element_type=F32)

        @pl.when(step == nk - 1)
        def _():
            o_ref[...] = acc_ref[...].astype(o_ref.dtype)

    return pl.pallas_call(
        body, name=name, grid=(m // tm, n // tn, nk),
        in_specs=[pl.BlockSpec((tm, tk), lambda i, j, l: (i, jnp.minimum(l, nk0 - 1))),
                  pl.BlockSpec((tm, tk), lambda i, j, l: (i, jnp.maximum(l - nk0, 0))),
                  _spec(b.shape, b_kind, tn, tk, lambda i, j, l: j, lambda i, j, l: l)] + anyspec,
        out_specs=pl.BlockSpec((tm, tn), lambda i, j, l: (i, j)),
        out_shape=jax.ShapeDtypeStruct((m, n), out_dtype), scratch_shapes=[pltpu.VMEM((tm, tn), F32)],
        compiler_params=_params("parallel", "parallel", "arbitrary"),
    )(*a, b, *deps)


def _pieces(splits, cs, cp):
    out, g0 = [], 0
    for width in splits:
        g1, runs = g0 + width, []
        for j in range(N_CHIPS):
            a, b = max(g0, cs * j), min(g1, cs * (j + 1))
            if a < b:
                runs.append((j * cp + a - cs * j, a - g0, b - a))
        out.append(runs)
        g0 = g1
    return out


def _head_norm(xv, gv):
    r = lax.rsqrt(jnp.mean(xv * xv, axis=-1, keepdims=True) + EPS)
    return (xv * r) * gv


def _head_norm_bwd(dyv, xv, gv):
    r = lax.rsqrt(jnp.mean(xv * xv, axis=-1, keepdims=True) + EPS)
    xr = xv * r
    gdy = dyv * gv
    return r * (gdy - xr * jnp.mean(gdy * xr, axis=-1, keepdims=True)), jnp.sum(dyv * xr, axis=0, keepdims=True)


def _proj_split(proj_p, splits, cs, dtypes, gains, tm=128):
    s, wp = proj_p.shape
    pieces = _pieces(splits, cs, wp // N_CHIPS)
    normed = sorted(gains)
    nseg = len(splits)

    def body(p_ref, *refs):
        g_refs, o_refs, n_refs = refs[:len(normed)], refs[len(normed):len(normed) + nseg], refs[len(normed) + nseg:]
        for o_ref, runs in zip(o_refs, pieces):
            for src, dst, n in runs:
                o_ref[:, dst:dst + n] = p_ref[:, src:src + n].astype(o_ref.dtype)
        for g_ref, n_ref, i in zip(g_refs, n_refs, normed):
            for c0 in range(0, splits[i], HEAD_DIM):
                cols = slice(c0, c0 + HEAD_DIM)
                n_ref[:, cols] = _head_norm(o_refs[i][:, cols], g_ref[:, cols]).astype(n_ref.dtype)

    return pl.pallas_call(
        body, name="proj_split", grid=(s // tm,),
        in_specs=[pl.BlockSpec((tm, wp), lambda i: (i, 0))] + [pl.BlockSpec((1, splits[i]), lambda i: (0, 0)) for i in normed],
        out_specs=[pl.BlockSpec((tm, w), lambda i: (i, 0)) for w in splits]
        + [pl.BlockSpec((tm, splits[i]), lambda i: (i, 0)) for i in normed],
        out_shape=[jax.ShapeDtypeStruct((s, w), dt) for w, dt in zip(splits, dtypes)]
        + [jax.ShapeDtypeStruct((s, splits[i]), BF16) for i in normed],
        compiler_params=_params("parallel"),
    )(proj_p, *[gains[i] for i in normed])


def _dproj_merge(parts, splits, cs, cp, norms, tm=128):
    s = parts[0].shape[0]
    wp = N_CHIPS * cp
    pieces = _pieces(splits, cs, cp)
    normed = sorted(norms)
    nseg, nn = len(splits), len(normed)

    def body(*refs):
        p_refs, x_refs, g_refs = refs[:nseg], refs[nseg:nseg + nn], refs[nseg + nn:nseg + 2 * nn]
        o_ref, dg_refs = refs[nseg + 2 * nn], refs[nseg + 2 * nn + 1:nseg + 3 * nn + 1]
        stage, tmp = refs[-2], refs[-1]

        @pl.when(pl.program_id(0) == 0)
        def _():
            for dg_ref in dg_refs:
                dg_ref[...] = jnp.zeros_like(dg_ref)

        for j in range(N_CHIPS):
            stage[:, j * cp + cs:(j + 1) * cp] = jnp.zeros((tm, cp - cs), F32)
        for i, (p_ref, runs) in enumerate(zip(p_refs, pieces)):
            src_ref = p_ref
            if i in norms:
                k = normed.index(i)
                for c0 in range(0, splits[i], HEAD_DIM):
                    cols = slice(c0, c0 + HEAD_DIM)
                    dx, dg = _head_norm_bwd(p_ref[:, cols].astype(F32), x_refs[k][:, cols], g_refs[k][:, cols])
                    tmp[:, cols] = dx
                    dg_refs[k][:, cols] += dg
                src_ref = tmp
            for dst, src, n in runs:
                stage[:, dst:dst + n] = src_ref[:, src:src + n].astype(F32)
        o_ref[...] = stage[...].astype(o_ref.dtype)

    wmax = max(splits[i] for i in normed)
    row = lambda w: pl.BlockSpec((tm, w), lambda i: (i, 0))
    vec = lambda w: pl.BlockSpec((1, w), lambda i: (0, 0))
    outs = pl.pallas_call(
        body, name="dproj_merge", grid=(s // tm,),
        in_specs=[row(w) for w in splits] + [row(splits[i]) for i in normed] + [vec(splits[i]) for i in normed],
        out_specs=[row(wp)] + [vec(splits[i]) for i in normed],
        out_shape=[jax.ShapeDtypeStruct((s, wp), BF16)] + [jax.ShapeDtypeStruct((1, splits[i]), F32) for i in normed],
        scratch_shapes=[pltpu.VMEM((tm, wp), F32), pltpu.VMEM((tm, wmax), F32)],
        compiler_params=_params("arbitrary"),
    )(*parts, *[norms[i][0] for i in normed], *[norms[i][1] for i in normed])
    return outs[0], dict(zip(normed, outs[1:]))


def _norm_fwd(x, g, *, group, name, tm=256):
    s, w = x.shape
    ng = w // group

    def body(x_ref, g_ref, o_ref):
        for i in range(ng):
            cols = slice(i * group, (i + 1) * group)
            xv = x_ref[:, cols]
            r = lax.rsqrt(jnp.mean(xv * xv, axis=-1, keepdims=True) + EPS)
            o_ref[:, cols] = ((xv * r) * g_ref[:, cols]).astype(o_ref.dtype)

    return pl.pallas_call(
        body, name=name, grid=(s // tm,),
        in_specs=[pl.BlockSpec((tm, w), lambda i: (i, 0)), pl.BlockSpec((1, w), lambda i: (0, 0))],
        out_specs=pl.BlockSpec((tm, w), lambda i: (i, 0)),
        out_shape=jax.ShapeDtypeStruct((s, w), BF16),
        compiler_params=_params("parallel"),
    )(x, g)


def _norm_bwd(dy, x, g, *, group, name, res=None, out_dtypes=(BF16,), tm=256, deps=()):
    s, w = x.shape
    ng = w // group
    n_in = 4 if res is not None else 3

    def body(*refs):
        dy_ref, x_ref, g_ref = refs[:3]
        res_ref = refs[3] if res is not None else None
        outs = refs[n_in + len(deps):]
        dx_refs, dg_ref = outs[:-1], outs[-1]

        @pl.when(pl.program_id(0) == 0)
        def _():
            dg_ref[...] = jnp.zeros_like(dg_ref)

        for i in range(ng):
            cols = slice(i * group, (i + 1) * group)
            xv = x_ref[:, cols]
            dyv = dy_ref[:, cols].astype(F32)
            r = lax.rsqrt(jnp.mean(xv * xv, axis=-1, keepdims=True) + EPS)
            xr = xv * r
            dg_ref[:, cols] += jnp.sum(dyv * xr, axis=0, keepdims=True)
            gdy = dyv * g_ref[:, cols]
            dx = r * (gdy - xr * jnp.mean(gdy * xr, axis=-1, keepdims=True))
            if res_ref is not None:
                dx = dx + res_ref[:, cols]
            for dx_ref in dx_refs:
                dx_ref[:, cols] = dx.astype(dx_ref.dtype)

    row = pl.BlockSpec((tm, w), lambda i: (i, 0))
    vec = pl.BlockSpec((1, w), lambda i: (0, 0))
    in_specs = [row, row, vec] + ([row] if res is not None else []) + [pl.BlockSpec(memory_space=pl.ANY)] * len(deps)
    args = [dy, x, g] + ([res] if res is not None else []) + list(deps)
    out_specs = [row] * len(out_dtypes) + [vec]
    out_shape = [jax.ShapeDtypeStruct((s, w), dt) for dt in out_dtypes] + [jax.ShapeDtypeStruct((1, w), F32)]
    return pl.pallas_call(
        body, name=name, grid=(s // tm,), in_specs=in_specs, out_specs=out_specs,
        out_shape=out_shape, compiler_params=_params("arbitrary"),
    )(*args)


def _split3(v):
    p1 = v.astype(BF16)
    r1 = v - p1.astype(F32)
    p2 = r1.astype(BF16)
    p3 = (r1 - p2.astype(F32)).astype(BF16)
    return p1, p2, p3


def _tri_sum(v, reverse, tcol=512):
    h, s = v.shape
    tcol = min(tcol, s)
    parts = _split3(v)
    outs = []
    for j in range(s // tcol):
        src = lax.broadcasted_iota(jnp.int32, (s, tcol), 0)
        dst = lax.broadcasted_iota(jnp.int32, (s, tcol), 1) + j * tcol
        keep = (src >= dst) if reverse else (src <= dst)
        tri = jnp.where(keep, 1.0, 0.0).astype(BF16)
        acc = jnp.zeros((h, tcol), F32)
        for p in parts:
            acc = acc + jnp.dot(p, tri, preferred_element_type=F32)
        outs.append(acc)
    return outs


def _forget_fwd(fa_t, b):
    h, s = fa_t.shape
    tcol = min(512, s)

    def body(f_ref, b_ref, c_ref):
        z = f_ref[...] + b_ref[...]
        logf = jnp.minimum(z, 0.0) - jnp.log(1.0 + jnp.exp(-jnp.abs(z)))
        for j, blk in enumerate(_tri_sum(logf, reverse=False, tcol=tcol)):
            c_ref[:, j * tcol:(j + 1) * tcol] = blk

    return pl.pallas_call(
        body, name="forget_fwd", out_shape=jax.ShapeDtypeStruct((h, s), F32),
        compiler_params=_params(),
    )(fa_t, b)


def _forget_bwd(dacol, fa_t, b):
    h, s = fa_t.shape
    tcol = min(512, s)

    def body(d_ref, f_ref, b_ref, dfa_ref, db_ref):
        z = f_ref[...] + b_ref[...]
        dc = -d_ref[...]
        total = jnp.zeros((h, 1), F32)
        for j, blk in enumerate(_tri_sum(dc, reverse=True, tcol=tcol)):
            cols = slice(j * tcol, (j + 1) * tcol)
            dfa = blk * (1.0 - jax.nn.sigmoid(z[:, cols]))
            dfa_ref[:, cols] = dfa
            total = total + jnp.sum(dfa, axis=-1, keepdims=True)
        db_ref[...] = total

    return pl.pallas_call(
        body, name="forget_bwd",
        out_shape=[jax.ShapeDtypeStruct((h, s), F32), jax.ShapeDtypeStruct((h, 1), F32)],
        compiler_params=_params(),
    )(dacol, fa_t, b)


def _distance_bias(s, tile, dilated):
    nb = s // tile
    b = lax.broadcasted_iota(jnp.int32, (nb, tile, tile), 0)
    dist = b * tile + lax.broadcasted_iota(jnp.int32, (nb, tile, tile), 1) - lax.broadcasted_iota(jnp.int32, (nb, tile, tile), 2)
    if not dilated:
        return jnp.where(dist >= 0, 0.0, NEG).astype(F32)
    mult = jnp.zeros(dist.shape, jnp.int32)
    for window, dil in DIL_PATTERNS:
        mult = mult + ((dist >= 0) & (dist <= window) & ((dist & (dil - 1)) == 0)).astype(jnp.int32)
    logm = jnp.where(mult == 3, math.log(3.0), jnp.where(mult == 2, math.log(2.0), 0.0))
    return jnp.where(mult > 0, logm, NEG).astype(F32)


def _logits(q, k, arow, acol, bias):
    s = lax.dot_general(q, k, (((1,), (1,)), ((), ())), preferred_element_type=F32)
    return s * (1.0 / math.sqrt(HEAD_DIM)) + arow - acol + bias


def _attn_fwd(q, k, v, arow, acol, *, dilated, name, tq=ATTN_TILE, tk=ATTN_TILE):
    two_term = not dilated
    s, w = q.shape
    nh = w // HEAD_DIM
    assert tq == tk
    tq = tk = min(tq, s)
    nq, nk = s // tq, s // tk

    def body(q_ref, k_ref, v_ref, ar_ref, ac_ref, b_ref, o_ref, of_ref, lse_ref, m_ref, l_ref, acc_ref):
        qi, ki = pl.program_id(1), pl.program_id(2)

        @pl.when(ki == 0)
        def _():
            m_ref[...] = jnp.full_like(m_ref, NEG)
            l_ref[...] = jnp.zeros_like(l_ref)
            acc_ref[...] = jnp.zeros_like(acc_ref)

        @pl.when(ki <= qi)
        def _():
            sc = _logits(q_ref[...], k_ref[...], ar_ref[...], ac_ref[...], b_ref[...])
            m_new = jnp.maximum(m_ref[...], jnp.max(sc, axis=-1, keepdims=True))
            alpha = jnp.exp(m_ref[...] - m_new)
            p = jnp.exp(sc - m_new)
            l_ref[...] = alpha * l_ref[...] + jnp.sum(p, axis=-1, keepdims=True)
            p_hi = p.astype(BF16)
            vv = v_ref[...]
            pv = jnp.dot(p_hi, vv, preferred_element_type=F32)
            if two_term:
                pv = pv + jnp.dot((p - p_hi.astype(F32)).astype(BF16), vv, preferred_element_type=F32)
            acc_ref[...] = alpha * acc_ref[...] + pv
            m_ref[...] = m_new

        @pl.when(ki == nk - 1)
        def _():
            out = acc_ref[...] / l_ref[...]
            o_ref[...] = out.astype(o_ref.dtype)
            of_ref[...] = out
            lse_ref[...] = m_ref[...] + jnp.log(l_ref[...])

    kv = pl.BlockSpec((tk, HEAD_DIM), lambda h, i, j: (jnp.minimum(j, i), h))
    return pl.pallas_call(
        body, name=name, grid=(nh, nq, nk),
        in_specs=[pl.BlockSpec((tq, HEAD_DIM), lambda h, i, j: (i, h)), kv, kv,
                  pl.BlockSpec((None, tq, 1), lambda h, i, j: (h, i, 0)),
                  pl.BlockSpec((None, 1, tk), lambda h, i, j: (h, 0, jnp.minimum(j, i))),
                  pl.BlockSpec((None, tq, tk), lambda h, i, j: (jnp.maximum(i - j, 0), 0, 0))],
        out_specs=[pl.BlockSpec((tq, HEAD_DIM), lambda h, i, j: (i, h)),
                   pl.BlockSpec((tq, HEAD_DIM), lambda h, i, j: (i, h)),
                   pl.BlockSpec((None, tq, 1), lambda h, i, j: (h, i, 0))],
        out_shape=[jax.ShapeDtypeStruct((s, w), BF16), jax.ShapeDtypeStruct((s, w), F32),
                   jax.ShapeDtypeStruct((nh, s, 1), F32)],
        scratch_shapes=[pltpu.VMEM((tq, 1), F32), pltpu.VMEM((tq, 1), F32), pltpu.VMEM((tq, HEAD_DIM), F32)],
        compiler_params=_params("parallel", "parallel", "arbitrary"),
    )(q, k, v, arow, acol, _distance_bias(s, tq, dilated))


def _attn_bwd(q, k, v, o, do, lse, arow, acol, *, dilated, name, tq=ATTN_TILE, tk=ATTN_TILE):
    s, w = q.shape
    nh = w // HEAD_DIM
    assert tq == tk
    tq = tk = min(tq, s)
    nq, nk = s // tq, s // tk
    scale = 1.0 / math.sqrt(HEAD_DIM)

    def body(q_ref, k_ref, v_ref, o_ref, do_ref, lse_ref, ar_ref, ac_ref, b_ref,
             dq_ref, dk_ref, dv_ref, dac_ref, dk_acc, dv_acc, dac_acc):
        ki, qi = pl.program_id(1), pl.program_id(2)

        @pl.when((ki == 0) & (qi == 0))
        def _():
            dq_ref[...] = jnp.zeros_like(dq_ref)

        @pl.when(qi == 0)
        def _():
            dk_acc[...] = jnp.zeros_like(dk_acc)
            dv_acc[...] = jnp.zeros_like(dv_acc)
            dac_acc[...] = jnp.zeros_like(dac_acc)

        @pl.when(qi >= ki)
        def _():
            qv, kvv, dov = q_ref[...], k_ref[...], do_ref[...]
            sc = _logits(qv, kvv, ar_ref[...], ac_ref[...], b_ref[...])
            p = jnp.exp(sc - lse_ref[...])
            dp = lax.dot_general(dov, v_ref[...], (((1,), (1,)), ((), ())), preferred_element_type=F32)
            delta = jnp.sum(dov.astype(F32) * o_ref[...].astype(F32), axis=-1, keepdims=True)
            ds = p * (dp - delta)
            dsb = ds.astype(BF16)
            dv_acc[...] += lax.dot_general(p.astype(BF16), dov, (((0,), (0,)), ((), ())), preferred_element_type=F32)
            dk_acc[...] += lax.dot_general(dsb, qv, (((0,), (0,)), ((), ())), preferred_element_type=F32)
            rows = pl.ds(pl.multiple_of(qi * tq, tq), tq)
            dq_ref[rows, :] += jnp.dot(dsb, kvv, preferred_element_type=F32) * scale
            dac_acc[...] += jnp.sum(ds, axis=0, keepdims=True)

        @pl.when(qi == nq - 1)
        def _():
            dk_ref[...] = dk_acc[...] * scale
            dv_ref[...] = dv_acc[...]
            dac_ref[...] = dac_acc[...]

    qs = pl.BlockSpec((tq, HEAD_DIM), lambda h, j, i: (jnp.maximum(i, j), h))
    ks = pl.BlockSpec((tk, HEAD_DIM), lambda h, j, i: (j, h))
    rowv = pl.BlockSpec((None, tq, 1), lambda h, j, i: (h, jnp.maximum(i, j), 0))
    colv = pl.BlockSpec((None, 1, tk), lambda h, j, i: (h, 0, j))
    return pl.pallas_call(
        body, name=name, grid=(nh, nk, nq),
        in_specs=[qs, ks, ks, qs, qs, rowv, rowv, colv,
                  pl.BlockSpec((None, tq, tk), lambda h, j, i: (jnp.maximum(i - j, 0), 0, 0))],
        out_specs=[pl.BlockSpec((s, HEAD_DIM), lambda h, j, i: (0, h)), ks, ks, colv],
        out_shape=[jax.ShapeDtypeStruct((s, w), F32), jax.ShapeDtypeStruct((s, w), F32),
                   jax.ShapeDtypeStruct((s, w), F32), jax.ShapeDtypeStruct((nh, 1, s), F32)],
        scratch_shapes=[pltpu.VMEM((tk, HEAD_DIM), F32), pltpu.VMEM((tk, HEAD_DIM), F32), pltpu.VMEM((1, tk), F32)],
        compiler_params=_params("arbitrary", "arbitrary", "arbitrary"),
    )(q, k, v, o, do, lse, arow, acol, _distance_bias(s, tq, dilated))


def _gate_fwd(ga, gb, pa, pb, tm=256):
    s, d = ga.shape

    def body(ga_ref, gb_ref, pa_ref, pb_ref, o_ref):
        o_ref[...] = (jax.nn.sigmoid(ga_ref[...]) * pa_ref[...]
                      + jax.nn.sigmoid(gb_ref[...]) * pb_ref[...]).astype(o_ref.dtype)

    row = pl.BlockSpec((tm, d), lambda i: (i, 0))
    return pl.pallas_call(
        body, name="gate_fwd", grid=(s // tm,), in_specs=[row] * 4, out_specs=row,
        out_shape=jax.ShapeDtypeStruct((s, d), BF16), compiler_params=_params("parallel"),
    )(ga, gb, pa, pb)


def _gate_bwd(dm, ga, gb, pa, pb, tm=256):
    s, d = ga.shape

    def body(dm_ref, ga_ref, gb_ref, pa_ref, pb_ref, dpa_ref, dpb_ref, dga_ref, dgb_ref):
        dmv = dm_ref[...]
        for g_ref, p_ref, dp_ref, dg_ref in ((ga_ref, pa_ref, dpa_ref, dga_ref), (gb_ref, pb_ref, dpb_ref, dgb_ref)):
            sg = jax.nn.sigmoid(g_ref[...])
            dp_ref[...] = (dmv * sg).astype(BF16)
            dg_ref[...] = (dmv * p_ref[...] * (sg * (1.0 - sg))).astype(BF16)

    row = pl.BlockSpec((tm, d), lambda i: (i, 0))
    return pl.pallas_call(
        body, name="gate_bwd", grid=(s // tm,), in_specs=[row] * 5, out_specs=[row] * 4,
        out_shape=[jax.ShapeDtypeStruct((s, d), BF16)] * 4, compiler_params=_params("parallel"),
    )(dm, ga, gb, pa, pb)


def _shift_down(u, k):
    row = lax.broadcasted_iota(jnp.int32, u.shape, 0)
    return jnp.where(row >= k, pltpu.roll(u, k, 0), 0.0)


def _shift_up(u, k):
    n = u.shape[0]
    row = lax.broadcasted_iota(jnp.int32, u.shape, 0)
    return jnp.where(row < n - k, pltpu.roll(u, n - k, 0), 0.0)


def _conv3(u, wc, b):
    return wc[0:1, :] * _shift_down(u, 2) + wc[1:2, :] * _shift_down(u, 1) + wc[2:3, :] * u + b


def _conv_glu_fwd(u, wc, b, tn=256):
    s, f2 = u.shape
    f = f2 // 2
    nb = f // tn

    def body(ug_ref, uv_ref, wg_ref, wv_ref, bg_ref, bv_ref, o_ref):
        cg = _conv3(ug_ref[...], wg_ref[...], bg_ref[...])
        cv = _conv3(uv_ref[...], wv_ref[...], bv_ref[...])
        o_ref[...] = (cg * jax.nn.sigmoid(cg) * cv).astype(o_ref.dtype)

    def cols(rows, off):
        return pl.BlockSpec((rows, tn), lambda j: (0, j + off))

    return pl.pallas_call(
        body, name="conv_glu_fwd", grid=(nb,),
        in_specs=[cols(s, 0), cols(s, nb), cols(3, 0), cols(3, nb), cols(1, 0), cols(1, nb)],
        out_specs=cols(s, 0), out_shape=jax.ShapeDtypeStruct((s, f), BF16),
        compiler_params=_params("parallel"),
    )(u, u, wc, wc, b, b)


def _conv_glu_bwd(u, da, wc, b, tn=256):
    s, f2 = u.shape
    f = f2 // 2
    nb = f // tn

    def body(ug_ref, uv_ref, da_ref, wg_ref, wv_ref, bg_ref, bv_ref, dug_ref, duv_ref, sg_ref, sv_ref):
        ug, uv, wg, wv = ug_ref[...], uv_ref[...], wg_ref[...], wv_ref[...]
        cg = _conv3(ug, wg, bg_ref[...])
        cv = _conv3(uv, wv, bv_ref[...])
        sig = jax.nn.sigmoid(cg)
        dav = da_ref[...]
        dcv = dav * (cg * sig)
        dcg = dav * cv * (sig * (1.0 + cg * (1.0 - sig)))
        for dc, uu, w, du_ref, st_ref in ((dcg, ug, wg, dug_ref, sg_ref), (dcv, uv, wv, duv_ref, sv_ref)):
            du = w[2:3, :] * dc + w[1:2, :] * _shift_up(dc, 1) + w[0:1, :] * _shift_up(dc, 2)
            du_ref[...] = du.astype(BF16)
            st_ref[...] = jnp.zeros_like(st_ref)
            st_ref[0:1, :] = jnp.sum(dc * _shift_down(uu, 2), axis=0, keepdims=True)
            st_ref[1:2, :] = jnp.sum(dc * _shift_down(uu, 1), axis=0, keepdims=True)
            st_ref[2:3, :] = jnp.sum(dc * uu, axis=0, keepdims=True)
            st_ref[3:4, :] = jnp.sum(dc, axis=0, keepdims=True)

    def cols(rows, off):
        return pl.BlockSpec((rows, tn), lambda j: (0, j + off))

    return pl.pallas_call(
        body, name="conv_glu_bwd", grid=(nb,),
        in_specs=[cols(s, 0), cols(s, nb), cols(s, 0), cols(3, 0), cols(3, nb), cols(1, 0), cols(1, nb)],
        out_specs=[cols(s, 0), cols(s, 0), cols(8, 0), cols(8, 0)],
        out_shape=[jax.ShapeDtypeStruct((s, f), BF16), jax.ShapeDtypeStruct((s, f), BF16),
                   jax.ShapeDtypeStruct((8, f), F32), jax.ShapeDtypeStruct((8, f), F32)],
        compiler_params=_params("parallel"),
    )(u, u, da, wc, wc, b, b)


ROW_TILES = (256, 128, 64, 32, 16, 8)
BLOCK_BYTES = 2 << 20


def _add_halves(g, r1, place):
    ns, r, c = g.shape
    rh = r // 2
    tr = _pick(rh, ROW_TILES)
    g4 = g.reshape(ns, 2, rh, c)

    def body(p_ref, g_ref, r_ref, o_ref):
        o_ref[...] = (g_ref[...].astype(F32) + r_ref[...].astype(F32)).astype(o_ref.dtype)

    def slab(s, pr):
        return s + (s >= pr[0]).astype(jnp.int32)

    return pl.pallas_call(
        body, name="add_halves",
        grid_spec=pltpu.PrefetchScalarGridSpec(
            num_scalar_prefetch=1, grid=(ns - 1, rh // tr),
            in_specs=[pl.BlockSpec((None, None, tr, c), lambda s, i, pr: (slab(s, pr), pr[1], i, 0)),
                      pl.BlockSpec((None, tr, c), lambda s, i, pr: (slab(s, pr), i, 0))],
            out_specs=pl.BlockSpec((None, tr, c), lambda s, i, pr: (slab(s, pr), i, 0))),
        out_shape=jax.ShapeDtypeStruct((ns, rh, c), BF16),
        compiler_params=_params("parallel", "parallel"),
    )(place, g4, r1)


def _sum_chips(g, r1, recv, place):
    ns, r, c = g.shape
    rh = r // 2
    tr = _pick(rh, ROW_TILES)
    g4 = g.reshape(ns, 2, rh, c)

    def body(p_ref, g_ref, r_ref, t0_ref, t1_ref, t2_ref, o_ref):
        own = g_ref[...].astype(F32) + r_ref[...].astype(F32)
        o_ref[...] = ((own + t0_ref[...].astype(F32)) + t1_ref[...].astype(F32)) + t2_ref[...].astype(F32)

    def peer(k):
        return pl.BlockSpec((None, tr, c), lambda i, pr: (k, i, 0))

    return pl.pallas_call(
        body, name="sum_chips",
        grid_spec=pltpu.PrefetchScalarGridSpec(
            num_scalar_prefetch=1, grid=(rh // tr,),
            in_specs=[pl.BlockSpec((None, None, tr, c), lambda i, pr: (pr[0], pr[1], i, 0)),
                      pl.BlockSpec((None, tr, c), lambda i, pr: (pr[0], i, 0)), peer(0), peer(1), peer(2)],
            out_specs=pl.BlockSpec((tr, c), lambda i, pr: (pr[1] * (rh // tr) + i, 0))),
        out_shape=jax.ShapeDtypeStruct((r, c), F32),
        compiler_params=_params("parallel"),
    )(place, g4, r1, recv, recv, recv)


def _sum_devices(packs):
    n, r, c = packs.shape

    def body(p_ref, o_ref):
        acc = p_ref[0]
        for d in range(1, n):
            acc = acc + p_ref[d]
        o_ref[...] = acc

    return pl.pallas_call(
        body, name="sum_devices", out_shape=jax.ShapeDtypeStruct((r, c), F32), compiler_params=_params(),
    )(packs)


def _adamw_update(wv, gv, mv, vv):
    c1 = 1.0 - ADAM_B1 ** ADAM_STEP
    c2 = 1.0 - ADAM_B2 ** ADAM_STEP
    mn = ADAM_B1 * mv + (1.0 - ADAM_B1) * gv
    vn = ADAM_B2 * vv + (1.0 - ADAM_B2) * (gv * gv)
    m_hat = mn / c1
    v_hat = vn / c2
    return -ADAM_LR * (m_hat / (jnp.sqrt(v_hat) + ADAM_EPS) + ADAM_WD * wv), mn, vn


def _adamw(w, g, m, v, name, deps=(), emit_grad=False):
    r, c = w.shape
    tr = _pick(r, [t for t in ROW_TILES if t * c * 4 <= BLOCK_BYTES]) if r >= 8 else r
    n_out = 4 if emit_grad else 3

    def body(w_ref, g_ref, m_ref, v_ref, *rest):
        outs = rest[-n_out:]
        gv = g_ref[:, :c]
        if emit_grad:
            outs[0][...] = gv
        outs[-3][...], outs[-2][...], outs[-1][...] = _adamw_update(w_ref[...], gv, m_ref[...], v_ref[...])

    blk = pl.BlockSpec((tr, c), lambda i: (i, 0))
    g_blk = pl.BlockSpec((tr, g.shape[1]), lambda i: (i, 0))
    return pl.pallas_call(
        body, name=name, grid=(r // tr,), in_specs=[blk, g_blk, blk, blk] + [ANY] * len(deps), out_specs=[blk] * n_out,
        out_shape=[jax.ShapeDtypeStruct((r, c), F32)] * n_out, compiler_params=_params("parallel"),
    )(w, g, m, v, *deps)


ANY = pl.BlockSpec(memory_space=pl.ANY)


def _place():
    x, y, c = lax.axis_index("x"), lax.axis_index("y"), lax.axis_index("c")
    chips = [(1 - x, y), (x, 1 - y), (1 - x, 1 - y)]
    return x, y, c, chips


def _remote(src, dst, send_sem, recv_sem, to):
    return pltpu.make_async_remote_copy(src_ref=src, dst_ref=dst, send_sem=send_sem, recv_sem=recv_sem,
                                        device_id=to, device_id_type=MESH)


HBM = pl.BlockSpec(memory_space=pltpu.HBM)
SEM = pl.BlockSpec(memory_space=pltpu.SEMAPHORE)
EFFECT = pltpu.SideEffectType.DATAFLOW_SIDE_EFFECTING


def _in_hbm(a):
    return pltpu.with_memory_space_constraint(a, pltpu.HBM)


def _half(ref_rows, who):
    return pl.ds(who * (ref_rows // 2), ref_rows // 2)


def _gather_start(groups, name):
    items = [it for g in groups for it in g]
    n = len(items)
    sizes = [len(g) for g in groups]

    def body(*refs):
        srcs, lands = refs[:n], refs[n:2 * n]
        sems = refs[2 * n:2 * n + 2 * len(groups)]
        token = refs[-1]
        x, y, c, chips = _place()
        j = 2 * x + y
        at = 0
        for gi, g in enumerate(groups):
            send, recv = sems[2 * gi], sems[2 * gi + 1]
            for i, (shard, split) in enumerate(g):
                src, land = srcs[at], lands[at]
                at += 1
                rows = _half(shard.shape[0], c) if split else slice(None)
                for k, chip in enumerate(chips):
                    _remote(src.at[rows], land.at[j, rows], send.at[4 * i + k], recv.at[4 * i + k], (*chip, c)).start()
                _remote(src, land.at[j], send.at[4 * i + 3], recv.at[4 * i + 3], (x, y, 1 - c)).start()
        token[...] = jnp.zeros_like(token)

    sem_shapes = []
    for sz in sizes:
        sem_shapes += [pltpu.SemaphoreType.DMA((4 * sz,)), pltpu.SemaphoreType.DMA((4 * sz,))]
    out_shape = (sem_shapes + [pltpu.HBM(sh.shape, sh.dtype) for sh, _ in items]
                 + [pltpu.HBM((N_CHIPS,) + sh.shape, sh.dtype) for sh, _ in items]
                 + [jax.ShapeDtypeStruct((8, LANES), F32)])
    ns = len(sem_shapes)
    outs = pl.pallas_call(
        body, name=name, in_specs=[HBM] * (2 * n),
        out_specs=[SEM] * ns + [HBM] * (2 * n) + [pl.BlockSpec(memory_space=pltpu.VMEM)],
        out_shape=out_shape, input_output_aliases={i: ns + i for i in range(2 * n)},
        compiler_params=pltpu.CompilerParams(has_side_effects=EFFECT),
    )(*[_in_hbm(sh) for sh, _ in items], *[_in_hbm(lax.empty((N_CHIPS,) + sh.shape, sh.dtype)) for sh, _ in items])
    sems, shards, lands, token = outs[:ns], outs[ns:ns + n], outs[ns + n:ns + 2 * n], outs[-1]
    res, at = [], 0
    for gi, sz in enumerate(sizes):
        res.append((shards[at:at + sz], lands[at:at + sz], sems[2 * gi], sems[2 * gi + 1]))
        at += sz
    return res, token


def _gather_pass(group, started, after, name):
    shards, lands, send, recv = started
    n = len(group)
    split_ix = [i for i, (_, split) in enumerate(group) if split]

    def body(*refs):
        lnds, send1, recv1 = refs[n:2 * n], refs[2 * n], refs[2 * n + 1]
        outs = refs[2 * n + 2 + len(after):]
        send2, recv2, token = outs[2 * n], outs[2 * n + 1], outs[2 * n + 2]
        x, y, c, chips = _place()
        sib = (x, y, 1 - c)
        for i, (shard, split) in enumerate(group):
            rows = _half(shard.shape[0], c) if split else slice(None)
            for k, (cx, cy) in enumerate(chips):
                landed = lnds[i].at[2 * cx + cy, rows]
                cp = _remote(landed, landed, send1.at[4 * i + k], recv1.at[4 * i + k], sib)
                cp.wait_send()
                cp.wait_recv()
            own = lnds[i].at[2 * x + y]
            cp = _remote(own, own, send1.at[4 * i + 3], recv1.at[4 * i + 3], sib)
            cp.wait_send()
            cp.wait_recv()
        for i2, i in enumerate(split_ix):
            rows = _half(group[i][0].shape[0], c)
            for k, (cx, cy) in enumerate(chips):
                landed = lnds[i].at[2 * cx + cy, rows]
                _remote(landed, landed, send2.at[3 * i2 + k], recv2.at[3 * i2 + k], sib).start()
        token[...] = jnp.zeros_like(token)

    n2 = len(split_ix)
    out_shape = ([pltpu.HBM(a.shape, a.dtype) for a in (*shards, *lands)]
                 + [pltpu.SemaphoreType.DMA((3 * n2,)), pltpu.SemaphoreType.DMA((3 * n2,)), jax.ShapeDtypeStruct((8, LANES), F32)])
    outs = pl.pallas_call(
        body, name=name, in_specs=[HBM] * (2 * n) + [SEM, SEM] + [ANY] * len(after),
        out_specs=[HBM] * (2 * n) + [SEM, SEM, pl.BlockSpec(memory_space=pltpu.VMEM)],
        out_shape=out_shape, input_output_aliases={i: i for i in range(2 * n)},
        compiler_params=pltpu.CompilerParams(has_side_effects=EFFECT),
    )(*shards, *lands, send, recv, *after)
    return outs[:n], (outs[n:2 * n], outs[2 * n], outs[2 * n + 1]), outs[2 * n + 2]


def _gather_wait(group, passed, after, name):
    lands, send2, recv2 = passed
    n = len(group)
    split_ix = [i for i, (_, split) in enumerate(group) if split]

    def body(*refs):
        lnds, s2, r2 = refs[:n], refs[n], refs[n + 1]
        x, y, c, chips = _place()
        sib = (x, y, 1 - c)
        for i2, i in enumerate(split_ix):
            rows = _half(group[i][0].shape[0], 1 - c)
            for k, (cx, cy) in enumerate(chips):
                landed = lnds[i].at[2 * cx + cy, rows]
                cp = _remote(landed, landed, s2.at[3 * i2 + k], r2.at[3 * i2 + k], sib)
                cp.wait_send()
                cp.wait_recv()

    return pl.pallas_call(
        body, name=name, in_specs=[HBM] * n + [SEM, SEM, ANY], out_specs=[HBM] * n,
        out_shape=[pltpu.HBM(a.shape, a.dtype) for a in lands], input_output_aliases={i: i for i in range(n)},
        compiler_params=pltpu.CompilerParams(has_side_effects=EFFECT),
    )(*lands, send2, recv2, after)


def _xfer_start(name, srcs, land_shapes, n_copies, copies, after):
    n, nl = len(srcs), len(land_shapes)

    def body(*refs):
        src_refs, land_refs = refs[:n], refs[n:n + nl]
        send, recv, token = refs[n + nl + 1], refs[n + nl + 2], refs[-1]
        for cp in copies(src_refs, land_refs, send, recv):
            cp.start()
        token[...] = jnp.zeros_like(token)

    lands = [_in_hbm(lax.empty(shape, dtype)) for shape, dtype in land_shapes]
    out_shape = ([pltpu.SemaphoreType.DMA((n_copies,)), pltpu.SemaphoreType.DMA((n_copies,))]
                 + [pltpu.HBM(a.shape, a.dtype) for a in (*srcs, *lands)] + [jax.ShapeDtypeStruct((8, LANES), F32)])
    outs = pl.pallas_call(
        body, name=name, in_specs=[HBM] * (n + nl) + [ANY],
        out_specs=[SEM, SEM] + [HBM] * (n + nl) + [pl.BlockSpec(memory_space=pltpu.VMEM)],
        out_shape=out_shape, input_output_aliases={i: 2 + i for i in range(n + nl)},
        compiler_params=pltpu.CompilerParams(has_side_effects=EFFECT),
    )(*[_in_hbm(a) for a in srcs], *lands, after)
    return (outs[2:2 + n], outs[2 + n:2 + n + nl], outs[0], outs[1]), outs[-1]


def _xfer_wait(name, started, copies, after):
    srcs, lands, send, recv = started
    n, nl = len(srcs), len(lands)

    def body(*refs):
        src_refs, land_refs, s_ref, r_ref = refs[:n], refs[n:n + nl], refs[n + nl], refs[n + nl + 1]
        for cp in copies(src_refs, land_refs, s_ref, r_ref):
            cp.wait_send()
            cp.wait_recv()

    outs = pl.pallas_call(
        body, name=name, in_specs=[HBM] * (n + nl) + [SEM, SEM, ANY], out_specs=[HBM] * (n + nl),
        out_shape=[pltpu.HBM(a.shape, a.dtype) for a in (*srcs, *lands)],
        input_output_aliases={i: i for i in range(n + nl)},
        compiler_params=pltpu.CompilerParams(has_side_effects=EFFECT),
    )(*srcs, *lands, send, recv, after)
    return outs[:n], outs[n:]


def _swap_copies(srcs, lands, send, recv):
    x, y, c, _ = _place()
    return [_remote(src.at[:, _half(src.shape[1], 1 - c)], land, send.at[i], recv.at[i], (x, y, 1 - c))
            for i, (src, land) in enumerate(zip(srcs, lands))]


def _scatter_copies(srcs, lands, send, recv):
    x, y, c, chips = _place()
    return [_remote(src.at[2 * cx + cy], land.at[k], send.at[3 * i + k], recv.at[3 * i + k], (cx, cy, c))
            for i, (src, land) in enumerate(zip(srcs, lands)) for k, (cx, cy) in enumerate(chips)]


def _join_copies(srcs, lands, send, recv):
    x, y, c, _ = _place()
    return [_remote(src.at[_half(src.shape[0], c)], src.at[_half(src.shape[0], c)], send.at[i], recv.at[i], (x, y, 1 - c))
            for i, src in enumerate(srcs)]


def _corner(a):
    return a[(slice(0, 1),) * a.ndim]


class _Reducer:
    def __init__(self, place):
        self.place = place
        self.state = {}

    def swap(self, key, grads, after):
        shapes = [((g.shape[0], g.shape[1] // 2, g.shape[2]), g.dtype) for g in grads]
        self.state[key], token = _xfer_start("swap_start_" + key, grads, shapes, len(grads), _swap_copies, _corner(after))
        return token

    def to_chips(self, key, after):
        grads, from_sibling = _xfer_wait("swap_wait_" + key, self.state[key], _swap_copies, after)
        sums = [_add_halves(g, r, self.place) for g, r in zip(grads, from_sibling)]
        shapes = [((3,) + s.shape[1:], s.dtype) for s in sums]
        started, token = _xfer_start("scatter_start_" + key, sums, shapes, 3 * len(sums), _scatter_copies, _corner(sums[-1]))
        self.state[key] = (grads, from_sibling, started)
        return token

    def to_core(self, key, after):
        grads, from_sibling, started = self.state[key]
        _, from_chips = _xfer_wait("scatter_wait_" + key, started, _scatter_copies, after)
        shards = [_sum_chips(g, r, rc, self.place) for g, r, rc in zip(grads, from_sibling, from_chips)]
        self.state[key], token = _xfer_start("join_start_" + key, shards, [], len(shards), _join_copies, _corner(shards[-1]))
        return token

    def finish(self, key, after):
        return _xfer_wait("join_wait_" + key, self.state.pop(key), _join_copies, after)[0]


def _gather_packs(pack, deps=()):
    def body(p_ref, *rest):
        o_ref, lsem, ssem, rsem = rest[-4:]
        x, y, c, _ = _place()
        me = 4 * x + 2 * y + c
        local = pltpu.make_async_copy(p_ref, o_ref.at[me], lsem)
        local.start()
        cps = []
        for k in range(1, N_DEV):
            fx, fy, fc = (k >> 2) & 1, (k >> 1) & 1, k & 1
            to = (x ^ fx, y ^ fy, c ^ fc)
            cps.append(_remote(p_ref, o_ref.at[me], ssem.at[k - 1], rsem.at[k - 1], to))
        for cp in cps:
            cp.start()
        for k in range(1, N_DEV):
            fx, fy, fc = (k >> 2) & 1, (k >> 1) & 1, k & 1
            src = o_ref.at[4 * (x ^ fx) + 2 * (y ^ fy) + (c ^ fc)]
            _remote(src, src, ssem.at[k - 1], rsem.at[k - 1], (x, y, c)).wait_recv()
        for cp in cps:
            cp.wait_send()
        local.wait()

    return pl.pallas_call(
        body, name="gather_packs", in_specs=[ANY] * (1 + len(deps)), out_specs=ANY,
        out_shape=jax.ShapeDtypeStruct((N_DEV,) + pack.shape, pack.dtype),
        scratch_shapes=[pltpu.SemaphoreType.DMA, pltpu.SemaphoreType.DMA((N_DEV - 1,)), pltpu.SemaphoreType.DMA((N_DEV - 1,))],
    )(pack, *deps)


LANE_TILES = (512, 896, 1408, 704, 384, 256, 128)


def _layer_grads(x, target, small, wg, rest_pass, rest_wait, red, filler):
    s, d = x.shape
    f = wg["conv"].shape[1] // 2
    w_att = N_HEADS * HEAD_DIM
    in_splits = (w_att, w_att, w_att, N_HEADS, w_att, w_att, w_att, d, d)
    in_cols = sum(in_splits)
    cs = in_cols // N_CHIPS
    cp = wg["in"].shape[2]
    tm = min(s, 1024)
    t_in = cp
    t_d = _pick(d, LANE_TILES)
    t_d2 = min(d, 1024)
    t_dq = _pick(d // N_CHIPS, LANE_TILES)
    t_w = _pick(w_att, LANE_TILES)
    t_up = 2 * f // N_CHIPS
    tm_wide = min(s, 512)
    t_fq = _pick(f // N_CHIPS, LANE_TILES)

    h1 = _norm_fwd(x, small["g_attn"], group=d, name="rms1_fwd")
    proj_p = _mm(h1, wg["in"], mode="nn", b_kind="col", tm=tm_wide, tn=t_in, tk=d, name="mm_in")
    gains = {n: small[n].reshape(1, w_att) for n in ("g_q_fox", "g_k_fox", "g_q_dil", "g_k_dil")}
    qa, ka, va_b, fa, qb, kb, vb_b, ga, gb, qa_n, ka_n, qb_n, kb_n = _proj_split(
        proj_p, in_splits, cs, (F32, F32, BF16, F32, F32, F32, BF16, F32, F32),
        {0: gains["g_q_fox"], 1: gains["g_k_fox"], 4: gains["g_q_dil"], 5: gains["g_k_dil"]})
    fa_t = fa.T
    b_f = small["b_forget"].reshape(N_HEADS, 1)
    c_f = _forget_fwd(fa_t, b_f)
    slopes = jnp.asarray(2.0 ** (-8.0 * np.arange(1, N_HEADS + 1) / N_HEADS), dtype=F32)
    a_d = -(slopes[:, None] * jnp.arange(s, dtype=F32)[None, :])
    rows_f, cols_f = c_f[:, :, None], c_f[:, None, :]
    rows_d, cols_d = a_d[:, :, None], a_d[:, None, :]
    o_a, o_a32, lse_a = _attn_fwd(qa_n, ka_n, va_b, rows_f, cols_f, dilated=False, name="attn_fox_fwd")
    token = rest_pass("mid", o_a)
    rows_d = rows_d + token[0, 0]
    o_b, o_b32, lse_b = _attn_fwd(qb_n, kb_n, vb_b, rows_d, cols_d, dilated=True, name="attn_dil_fwd")
    wg = dict(wg, **rest_wait("mid", o_b))
    token = rest_pass("late", o_b)
    pa = _mm(o_a, wg["brf"], mode="nn", b_kind="col", tm=tm, tn=t_dq, tk=w_att, name="mm_brf", deps=(token,))
    pb = _mm(o_b, wg["brd"], mode="nn", b_kind="col", tm=tm, tn=t_dq, tk=w_att, name="mm_brd")
    merged = _gate_fwd(ga, gb, pa, pb)
    x1 = _mm(merged, wg["out"], mode="nn", b_kind="row", res=x, tm=tm, tn=t_d, tk=t_dq, name="mm_out")
    wg = dict(wg, **rest_wait("late", x1))
    h2 = _norm_fwd(x1, small["g_ffn"], group=d, name="rms2_fwd")
    u = _mm(h2, wg["up"], mode="nn", b_kind="col", tm=tm_wide, tn=t_up, tk=d, name="mm_up")
    act = _conv_glu_fwd(u, wg["conv"], wg["bconv"])
    dy_f, dy_b, loss_blk = _mm(act, wg["down"], mode="nn", b_kind="row", res=x1, loss_target=target,
                               tm=tm, tn=t_d2, tk=t_fq, name="mm_down")

    d_act = _mm(dy_b, wg["down"], mode="nt", b_kind="row", tm=tm, tn=t_fq, tk=d, name="mm_down_dx")
    g_down = _mm(act, dy_b, mode="tn", out_dtype=BF16, out_kind="row", tm=t_fq, tn=t_d2, tk=s, name="mm_down_dw")
    tok = red.swap("down", [g_down], g_down)
    du_g, du_v, st_g, st_v = _conv_glu_bwd(u, d_act, wg["conv"] + tok[0, 0], wg["bconv"])
    tok = red.to_chips("down", du_g)
    du = (du_g, du_v)
    g_up = _mm(h2, du, mode="tn", out_dtype=BF16, out_kind="col", tm=t_d2, tn=t_up // 2, tk=s, name="mm_up_dw", deps=(tok,))
    tok = red.to_core("down", g_up)
    tok2 = red.swap("up", [g_up], g_up)
    dh2 = _mm(du, wg["up"], mode="nt", b_kind="col", tm=tm, tn=t_d2, tk=t_up, name="mm_up_dx", deps=(tok, tok2))
    tok = red.to_chips("up", dh2)
    dx1_b, dx1_f, dg_ffn = _norm_bwd(dh2, x1, small["g_ffn"], group=d, res=dy_f, out_dtypes=(BF16, F32), name="rms2_bwd")
    d_merged = _mm(dx1_b, wg["out"], mode="nt", b_kind="row", tm=tm, tn=t_dq, tk=d, name="mm_out_dx", deps=(tok,))
    g_out = _mm(merged, dx1_b, mode="tn", out_dtype=BF16, out_kind="row", tm=t_dq, tn=t_d2, tk=s, name="mm_out_dw")
    dpa, dpb, dga, dgb = _gate_bwd(d_merged, ga, gb, pa, pb)
    do_a = _mm(dpa, wg["brf"], mode="nt", b_kind="col", out_dtype=BF16, tm=s, tn=w_att, tk=t_dq, name="mm_brf_dx")
    do_b = _mm(dpb, wg["brd"], mode="nt", b_kind="col", out_dtype=BF16, tm=s, tn=w_att, tk=t_dq, name="mm_brd_dx")
    g_brf = _mm(o_a, dpa, mode="tn", out_dtype=BF16, out_kind="col", tm=w_att, tn=t_dq, tk=s, name="mm_brf_dw")
    g_brd = _mm(o_b, dpb, mode="tn", out_dtype=BF16, out_kind="col", tm=w_att, tn=t_dq, tk=s, name="mm_brd_dw")
    tok = red.swap("mix", [g_out, g_brf, g_brd], g_brd)
    dqa_n, dka_n, dva, dac_a = _attn_bwd(qa_n, ka_n, va_b, o_a32, do_a, lse_a, rows_f + tok[0, 0], cols_f, dilated=False, name="attn_fox_bwd")
    tok = red.to_core("up", dqa_n)
    tok2 = red.to_chips("mix", dqa_n)
    dqb_n, dkb_n, dvb, _ = _attn_bwd(qb_n, kb_n, vb_b, o_b32, do_b, lse_b, rows_d + (tok[0, 0] + tok2[0, 0]), cols_d, dilated=True, name="attn_dil_bwd")
    tok = red.to_core("mix", dqb_n)
    dfa_t, db_f = _forget_bwd(dac_a[:, 0, :], fa_t, b_f)
    dproj_p, dgains = _dproj_merge(
        [dqa_n, dka_n, dva, dfa_t.T, dqb_n, dkb_n, dvb, dga, dgb], in_splits, cs, cp,
        {0: (qa, gains["g_q_fox"]), 1: (ka, gains["g_k_fox"]), 4: (qb, gains["g_q_dil"]), 5: (kb, gains["g_k_dil"])})
    dg_qf, dg_kf, dg_qd, dg_kd = dgains[0], dgains[1], dgains[4], dgains[5]
    g_in = _mm(h1, dproj_p, mode="tn", out_dtype=BF16, out_kind="col", tm=t_d2, tn=t_in, tk=s, name="mm_in_dw", deps=(tok,))
    tok = red.swap("in", [g_in], g_in)
    tok = red.to_chips("in", filler(tok))
    dh1 = _mm(dproj_p, wg["in"], mode="nt", b_kind="col", tm=tm, tn=t_d2, tk=t_in, name="mm_in_dx", deps=(tok,))
    grad_x, dg_attn = _norm_bwd(dh1, x, small["g_attn"], group=d, res=dx1_f, out_dtypes=(F32,), name="rms1_bwd")

    small_grads = {
        "g_attn": dg_attn, "b_forget": db_f.reshape(1, N_HEADS),
        "g_q_fox": dg_qf, "g_k_fox": dg_kf, "g_q_dil": dg_qd, "g_k_dil": dg_kd, "g_ffn": dg_ffn,
        "w_conv": jnp.concatenate([st_g[0:3], st_v[0:3]], axis=1),
        "b_conv": jnp.concatenate([st_g[3:4], st_v[3:4]], axis=1),
        "loss": loss_blk[0:1, 0:1],
    }
    return small_grads, grad_x


SMALL_ORDER = ("g_attn", "b_forget", "g_q_fox", "g_k_fox", "g_q_dil", "g_k_dil", "g_ffn", "w_conv", "b_conv", "loss")
WEIGHT_ORDER = ("g_attn", "w_in", "b_forget", "g_q_fox", "g_k_fox", "g_q_dil", "g_k_dil", "w_br_fox", "w_br_dil",
                "w_out", "g_ffn", "w_up", "w_conv", "b_conv", "w_down")
BIG = {"w_in": "in", "w_br_fox": "brf", "w_br_dil": "brd", "w_out": "out", "w_up": "up", "w_down": "down"}


def kernel(x, g_attn, w_in, b_forget, g_q_fox, g_k_fox, g_q_dil, g_k_dil, w_br_fox, w_br_dil, w_out, g_ffn, w_up, w_conv, b_conv, w_down, loss_target, m_g_attn, m_w_in, m_b_forget, m_g_q_fox, m_g_k_fox, m_g_q_dil, m_g_k_dil, m_w_br_fox, m_w_br_dil, m_w_out, m_g_ffn, m_w_up, m_w_conv, m_b_conv, m_w_down, v_g_attn, v_w_in, v_b_forget, v_g_q_fox, v_g_k_fox, v_g_q_dil, v_g_k_dil, v_w_br_fox, v_w_br_dil, v_w_out, v_g_ffn, v_w_up, v_w_conv, v_b_conv, v_w_down):
    w = dict(g_attn=g_attn, w_in=w_in, b_forget=b_forget, g_q_fox=g_q_fox, g_k_fox=g_k_fox, g_q_dil=g_q_dil,
             g_k_dil=g_k_dil, w_br_fox=w_br_fox, w_br_dil=w_br_dil, w_out=w_out, g_ffn=g_ffn, w_up=w_up,
             w_conv=w_conv, b_conv=b_conv, w_down=w_down)
    m = dict(g_attn=m_g_attn, w_in=m_w_in, b_forget=m_b_forget, g_q_fox=m_g_q_fox, g_k_fox=m_g_k_fox,
             g_q_dil=m_g_q_dil, g_k_dil=m_g_k_dil, w_br_fox=m_w_br_fox, w_br_dil=m_w_br_dil, w_out=m_w_out,
             g_ffn=m_g_ffn, w_up=m_w_up, w_conv=m_w_conv, b_conv=m_b_conv, w_down=m_w_down)
    v = dict(g_attn=v_g_attn, w_in=v_w_in, b_forget=v_b_forget, g_q_fox=v_g_q_fox, g_k_fox=v_g_k_fox,
             g_q_dil=v_g_q_dil, g_k_dil=v_g_k_dil, w_br_fox=v_w_br_fox, w_br_dil=v_w_br_dil, w_out=v_w_out,
             g_ffn=v_g_ffn, w_up=v_w_up, w_conv=v_w_conv, b_conv=v_b_conv, w_down=v_w_down)
    xi, yi, ci = lax.axis_index("x"), lax.axis_index("y"), lax.axis_index("c")
    chip = (2 * xi + yi).astype(jnp.int32)

    cs = w_in.shape[2]
    cp = _round_up(cs, LANES)
    conv_pad = jnp.pad(w_conv[0], ((0, 8 - w_conv.shape[1]), (0, 0)))
    first = [(jnp.pad(w_in[0].astype(BF16), ((0, 0), (0, cp - cs))), True), (conv_pad, False)]
    (started_first,), token = _gather_start([first], "gather_start_in")
    token, *raw = lax.optimization_barrier((token, w_br_fox, w_br_dil, w_out, w_up, w_down))
    shards = {n: a[0].astype(BF16) for n, a in zip(("brf", "brd", "out", "up", "down"), raw)}
    later = {"mid": ("brf", "brd", "out"), "late": ("up", "down")}
    groups = {key: [(shards[n], True) for n in members] for key, members in later.items()}
    started_later, token = _gather_start(list(groups.values()), "gather_start_rest")
    started = dict(zip(later, started_later))
    token, w["w_in"], m["w_in"], v["w_in"] = lax.optimization_barrier((token, w["w_in"], m["w_in"], v["w_in"]))
    w2, m2, v2 = ({n: a[n].reshape(a[n].shape[-2], a[n].shape[-1]) for n in BIG} for a in (w, m, v))
    early = (token, w2["w_in"], m2["w_in"], v2["w_in"])
    own_first, passed_first, token = _gather_pass(first, started_first, early, "gather_pass_in")
    land_in, land_conv = _gather_wait(first, passed_first, token, "gather_wait_in")
    wg = {"in": land_in, "bconv": b_conv,
          "conv": jnp.transpose(land_conv[:, :w_conv.shape[1], :], (1, 0, 2)).reshape(w_conv.shape[1], -1)}
    small = {n: w[n] for n in ("g_attn", "b_forget", "g_q_fox", "g_k_fox", "g_q_dil", "g_k_dil", "g_ffn")}
    small = {n: (a[0] if a.ndim == 3 else a) for n, a in small.items()}
    in_flight = {}

    def rest_pass(key, after):
        own, passed, tok = _gather_pass(groups[key], started[key], (after,), "gather_pass_" + key)
        in_flight[key] = (own, passed)
        return tok

    def rest_wait(key, after):
        own, passed = in_flight.pop(key)
        lands = _gather_wait(groups[key], passed, after, "gather_wait_" + key)
        return dict(zip(later[key], lands))

    reducer = _Reducer(jnp.stack([chip, ci.astype(jnp.int32)]))
    g_out, d_out, m_out, v_out = {}, {}, {}, {}
    reduced = {}

    def first_element(arrays):
        return jnp.stack([a[(0,) * a.ndim] for a in arrays])

    def update_big(n, deps):
        g2, dl, mn, vn = _adamw(w2[n], reduced[BIG[n]], m2[n], v2[n], name="adamw_" + n, deps=deps, emit_grad=True)
        g_out[n], d_out[n], m_out[n], v_out[n] = (a.reshape(w[n].shape) for a in (g2, dl, mn, vn))

    def update_down(tok):
        (reduced["down"],) = reducer.finish("down", tok)
        update_big("w_down", (tok,))
        return v_out["w_down"]

    small_grads, grad_x = _layer_grads(x[0], loss_target[0], small, wg, rest_pass, rest_wait, reducer, update_down)

    for key, members in (("up", ("up",)), ("mix", ("out", "brf", "brd"))):
        reduced.update(zip(members, reducer.finish(key, grad_x)))
    others = ("w_up", "w_out", "w_br_fox", "w_br_dil")
    for n in others:
        update_big(n, (grad_x,))

    flat = jnp.concatenate([small_grads[n].reshape(-1) for n in SMALL_ORDER])
    rows = _round_up(flat.shape[0], 8 * LANES) // LANES
    pack = jnp.pad(flat, (0, rows * LANES - flat.shape[0])).reshape(rows, LANES)
    packs = _gather_packs(pack, deps=(first_element([v_out[n] for n in others]),))
    total = _sum_devices(packs).reshape(-1)
    red, at = {}, 0
    for n in SMALL_ORDER:
        size = small_grads[n].size
        red[n] = total[at:at + size].reshape(small_grads[n].shape)
        at += size
    loss = red["loss"].reshape(())
    c2 = w_conv.shape[2]
    red["w_conv"] = lax.dynamic_slice_in_dim(red["w_conv"], chip * c2, c2, axis=1)

    smalls = [n for n in WEIGHT_ORDER if n not in BIG]
    for n in smalls:
        shape = w[n].shape
        r2 = (shape[-2], shape[-1]) if n not in ("g_attn", "b_forget", "g_ffn", "b_conv") else (1, shape[-1])
        g2 = red[n].reshape(r2)
        dl, mn, vn = _adamw(w[n].reshape(r2), g2, m[n].reshape(r2), v[n].reshape(r2), name="adamw_" + n)
        g_out[n], d_out[n], m_out[n], v_out[n] = (a.reshape(shape) for a in (g2, dl, mn, vn))
    tok = reducer.to_core("in", first_element([v_out[n] for n in smalls]))
    (reduced["in"],) = reducer.finish("in", tok)
    update_big("w_in", (tok,))

    return (loss, grad_x[None], *[g_out[n] for n in WEIGHT_ORDER], *[d_out[n] for n in WEIGHT_ORDER],
            *[m_out[n] for n in WEIGHT_ORDER], *[v_out[n] for n in WEIGHT_ORDER])
```

```python
import math

import jax
import jax.numpy as jnp
import numpy as np
from jax import lax
from jax.experimental import pallas as pl
from jax.experimental.pallas import tpu as pltpu

F32 = jnp.float32
BF16 = jnp.bfloat16
HEAD_DIM = 128
N_HEADS = 8
EPS = 1e-6
NEG = -1e30
N_CHIPS = 4
N_DEV = 8
LANES = 128
VMEM_LIMIT_BYTES = 56 * 1024 * 1024
DIL_PATTERNS = ((128, 1), (512, 4), (2048, 16))
ATTN_TILE = 512
ADAM_LR, ADAM_B1, ADAM_B2, ADAM_EPS, ADAM_WD, ADAM_STEP = 0.001, 0.9, 0.999, 1e-08, 0.01, 10
MESH = pl.DeviceIdType.MESH


def _params(*sem):
    return pltpu.CompilerParams(dimension_semantics=sem, vmem_limit_bytes=VMEM_LIMIT_BYTES)


def _round_up(n, m):
    return -(-n // m) * m


def _pick(dim, prefs):
    for p in prefs:
        if dim % p == 0:
            return p
    raise ValueError(f"no tile for {dim} in {prefs}")


def _logical_shape(arr, kind):
    if kind is None:
        return arr.shape
    s, r, c = arr.shape
    return (r, s * c) if kind == "col" else (s * r, c)


def _spec(shape, kind, br, bc, fi, fj):
    if kind is None:
        return pl.BlockSpec((br, bc), lambda *g: (fi(*g), fj(*g)))
    _, r, c = shape
    if kind == "col":
        nb = c // bc
        assert nb * bc == c, (shape, bc)
        return pl.BlockSpec((None, br, bc), lambda *g: (fj(*g) // nb, fi(*g), fj(*g) % nb))
    nb = r // br
    assert nb * br == r, (shape, br)
    return pl.BlockSpec((None, br, bc), lambda *g: (fi(*g) // nb, fi(*g) % nb, fj(*g)))


def _mm(a, b, *, mode, tm, tn, tk, name, a_kind=None, b_kind=None, out_kind=None,
        out_dtype=F32, res=None, deps=(), loss_target=None):
    pair_a, pair_b = isinstance(a, tuple), isinstance(b, tuple)
    if pair_a or pair_b:
        return _mm_pair(a, b, mode=mode, tm=tm, tn=tn, tk=tk, name=name, b_kind=b_kind, out_kind=out_kind,
                        out_dtype=out_dtype, deps=deps)
    la, lb = _logical_shape(a, a_kind), _logical_shape(b, b_kind)
    if mode == "nn":
        (m, k), (k2, n) = la, lb
    elif mode == "nt":
        (m, k), (n, k2) = la, lb
    else:
        (k, m), (k2, n) = la, lb
    assert k == k2, (name, la, lb)
    assert m % tm == 0 and n % tn == 0 and k % tk == 0, (name, m, n, k, tm, tn, tk)
    nk = k // tk
    im = lambda i, j, l: i
    jn = lambda i, j, l: j
    lk = lambda i, j, l: l
    if mode == "tn":
        a_spec = _spec(a.shape, a_kind, tk, tm, lk, im)
        dims = (((0,), (0,)), ((), ()))
    else:
        a_spec = _spec(a.shape, a_kind, tm, tk, im, lk)
        dims = (((1,), (1,)), ((), ())) if mode == "nt" else (((1,), (0,)), ((), ()))
    if mode == "nt":
        b_spec = _spec(b.shape, b_kind, tn, tk, jn, lk)
    else:
        b_spec = _spec(b.shape, b_kind, tk, tn, lk, jn)
    if out_kind is None:
        oshape = (m, n)
    elif out_kind == "col":
        oshape = (N_CHIPS, m, n // N_CHIPS)
    else:
        oshape = (N_CHIPS, m // N_CHIPS, n)
    o_spec = _spec(oshape, out_kind, tm, tn, im, jn)
    tile = pl.BlockSpec((tm, tn), lambda i, j, l: (i, j))
    in_specs = [a_spec, b_spec]
    args = [a, b]
    for extra in (res, loss_target):
        if extra is not None:
            in_specs.append(tile)
            args.append(extra)
    in_specs += [pl.BlockSpec(memory_space=pl.ANY)] * len(deps)
    args += list(deps)
    if loss_target is None:
        out_specs, out_shape = [o_spec], [jax.ShapeDtypeStruct(oshape, out_dtype)]
    else:
        assert out_kind is None and res is not None
        out_specs = [tile, tile, pl.BlockSpec((8, LANES), lambda i, j, l: (0, 0))]
        out_shape = [jax.ShapeDtypeStruct(oshape, F32), jax.ShapeDtypeStruct(oshape, BF16),
                     jax.ShapeDtypeStruct((8, LANES), F32)]
    n_in, n_out = len(args), len(out_specs)

    def finish(out, refs, first):
        res_ref = refs[2] if res is not None else None
        outs = refs[n_in:n_in + n_out]
        if res_ref is not None:
            out = out + res_ref[...]
        if loss_target is None:
            outs[0][...] = out.astype(outs[0].dtype)
            return

        @pl.when(first)
        def _():
            outs[2][...] = jnp.zeros_like(outs[2])

        err = out - refs[3][...]
        dy = err * (1.0 / n)
        outs[0][...] = dy
        outs[1][...] = dy.astype(BF16)
        outs[2][...] += 0.5 * jnp.sum(jnp.sum(err * err, axis=-1, keepdims=True) * (1.0 / n), axis=0, keepdims=True)

    def first_tile():
        return (pl.program_id(0) == 0) & (pl.program_id(1) == 0)

    def body_whole_k(*refs):
        finish(lax.dot_general(refs[0][...], refs[1][...], dims, preferred_element_type=F32), refs, first_tile())

    def body(*refs):
        acc_ref = refs[-1]
        step = pl.program_id(2)
        first = first_tile()

        @pl.when(step == 0)
        def _():
            acc_ref[...] = jnp.zeros_like(acc_ref)

        acc_ref[...] += lax.dot_general(refs[0][...], refs[1][...], dims, preferred_element_type=F32)

        @pl.when(step == nk - 1)
        def _():
            finish(acc_ref[...], refs, first)

    outs = pl.pallas_call(
        body_whole_k if nk == 1 else body, name=name, grid=(m // tm, n // tn, nk),
        in_specs=in_specs, out_specs=out_specs, out_shape=out_shape,
        scratch_shapes=[] if nk == 1 else [pltpu.VMEM((tm, tn), F32)],
        compiler_params=_params(*(["arbitrary"] * 3 if loss_target is not None else ["parallel", "parallel", "arbitrary"])),
    )(*args)
    return outs[0] if loss_target is None else outs


def _mm_pair(a, b, *, mode, tm, tn, tk, name, b_kind, out_kind, out_dtype, deps):
    anyspec = [pl.BlockSpec(memory_space=pl.ANY)] * len(deps)
    if mode == "tn":
        assert isinstance(b, tuple) and out_kind == "col" and a.shape[0] == tk
        k, m = a.shape
        n0 = b[0].shape[1]
        n, nb0 = 2 * n0, n0 // tn
        oshape = (N_CHIPS, m, n // N_CHIPS)

        def body(a_ref, b0_ref, b1_ref, *rest):
            o_ref = rest[-1]
            for first, b_ref in ((True, b0_ref), (False, b1_ref)):
                @pl.when((pl.program_id(1) < nb0) == first)
                def _():
                    o_ref[...] = lax.dot_general(a_ref[...], b_ref[...], (((0,), (0,)), ((), ())),
                                                 preferred_element_type=F32).astype(o_ref.dtype)

        return pl.pallas_call(
            body, name=name, grid=(m // tm, n // tn),
            in_specs=[pl.BlockSpec((tk, tm), lambda i, j: (0, i)),
                      pl.BlockSpec((tk, tn), lambda i, j: (0, jnp.minimum(j, nb0 - 1))),
                      pl.BlockSpec((tk, tn), lambda i, j: (0, jnp.maximum(j - nb0, 0)))] + anyspec,
            out_specs=_spec(oshape, "col", tm, tn, lambda i, j: i, lambda i, j: j),
            out_shape=jax.ShapeDtypeStruct(oshape, out_dtype), compiler_params=_params("parallel", "arbitrary"),
        )(a, *b, *deps)
    assert mode == "nt" and isinstance(a, tuple) and out_kind is None
    m, k0 = a[0].shape
    n = _logical_shape(b, b_kind)[0]
    nk0 = k0 // tk
    nk = 2 * nk0

    def body(a0_ref, a1_ref, b_ref, *rest):
        o_ref, acc_ref = rest[-2], rest[-1]
        step = pl.program_id(2)

        @pl.when(step == 0)
        def _():
            acc_ref[...] = jnp.zeros_like(acc_ref)

        for first, a_ref in ((True, a0_ref), (False, a1_ref)):
            @pl.when((step < nk0) == first)
            def _():
                acc_ref[...] += lax.dot_general(a_ref[...], b_ref[...], (((1,), (1,)), ((), ())), preferred_element_type=F32)

        @pl.when(step == nk - 1)
        def _():
            o_ref[...] = acc_ref[...].astype(o_ref.dtype)

    return pl.pallas_call(
        body, name=name, grid=(m // tm, n // tn, nk),
        in_specs=[pl.BlockSpec((tm, tk), lambda i, j, l: (i, jnp.minimum(l, nk0 - 1))),
                  pl.BlockSpec((tm, tk), lambda i, j, l: (i, jnp.maximum(l - nk0, 0))),
                  _spec(b.shape, b_kind, tn, tk, lambda i, j, l: j, lambda i, j, l: l)] + anyspec,
        out_specs=pl.BlockSpec((tm, tn), lambda i, j, l: (i, j)),
        out_shape=jax.ShapeDtypeStruct((m, n), out_dtype), scratch_shapes=[pltpu.VMEM((tm, tn), F32)],
        compiler_params=_params("parallel", "parallel", "arbitrary"),
    )(*a, b, *deps)


def _pieces(splits, cs, cp):
    out, g0 = [], 0
    for width in splits:
        g1, runs = g0 + width, []
        for j in range(N_CHIPS):
            a, b = max(g0, cs * j), min(g1, cs * (j + 1))
            if a < b:
                runs.append((j * cp + a - cs * j, a - g0, b - a))
        out.append(runs)
        g0 = g1
    return out


def _head_norm(xv, gv):
    r = lax.rsqrt(jnp.mean(xv * xv, axis=-1, keepdims=True) + EPS)
    return (xv * r) * gv


def _head_norm_bwd(dyv, xv, gv):
    r = lax.rsqrt(jnp.mean(xv * xv, axis=-1, keepdims=True) + EPS)
    xr = xv * r
    gdy = dyv * gv
    return r * (gdy - xr * jnp.mean(gdy * xr, axis=-1, keepdims=True)), jnp.sum(dyv * xr, axis=0, keepdims=True)


def _proj_split(proj_p, splits, cs, dtypes, gains, tm=128):
    s, wp = proj_p.shape
    pieces = _pieces(splits, cs, wp // N_CHIPS)
    normed = sorted(gains)
    nseg = len(splits)

    def body(p_ref, *refs):
        g_refs, o_refs, n_refs = refs[:len(normed)], refs[len(normed):len(normed) + nseg], refs[len(normed) + nseg:]
        for o_ref, runs in zip(o_refs, pieces):
            for src, dst, n in runs:
                o_ref[:, dst:dst + n] = p_ref[:, src:src + n].astype(o_ref.dtype)
        for g_ref, n_ref, i in zip(g_refs, n_refs, normed):
            for c0 in range(0, splits[i], HEAD_DIM):
                cols = slice(c0, c0 + HEAD_DIM)
                n_ref[:, cols] = _head_norm(o_refs[i][:, cols], g_ref[:, cols]).astype(n_ref.dtype)

    return pl.pallas_call(
        body, name="proj_split", grid=(s // tm,),
        in_specs=[pl.BlockSpec((tm, wp), lambda i: (i, 0))] + [pl.BlockSpec((1, splits[i]), lambda i: (0, 0)) for i in normed],
        out_specs=[pl.BlockSpec((tm, w), lambda i: (i, 0)) for w in splits]
        + [pl.BlockSpec((tm, splits[i]), lambda i: (i, 0)) for i in normed],
        out_shape=[jax.ShapeDtypeStruct((s, w), dt) for w, dt in zip(splits, dtypes)]
        + [jax.ShapeDtypeStruct((s, splits[i]), BF16) for i in normed],
        compiler_params=_params("parallel"),
    )(proj_p, *[gains[i] for i in normed])


def _dproj_merge(parts, splits, cs, cp, norms, tm=128):
    s = parts[0].shape[0]
    wp = N_CHIPS * cp
    pieces = _pieces(splits, cs, cp)
    normed = sorted(norms)
    nseg, nn = len(splits), len(normed)

    def body(*refs):
        p_refs, x_refs, g_refs = refs[:nseg], refs[nseg:nseg + nn], refs[nseg + nn:nseg + 2 * nn]
        o_ref, dg_refs = refs[nseg + 2 * nn], refs[nseg + 2 * nn + 1:nseg + 3 * nn + 1]
        stage, tmp = refs[-2], refs[-1]

        @pl.when(pl.program_id(0) == 0)
        def _():
            for dg_ref in dg_refs:
                dg_ref[...] = jnp.zeros_like(dg_ref)

        for j in range(N_CHIPS):
            stage[:, j * cp + cs:(j + 1) * cp] = jnp.zeros((tm, cp - cs), F32)
        for i, (p_ref, runs) in enumerate(zip(p_refs, pieces)):
            src_ref = p_ref
            if i in norms:
                k = normed.index(i)
                for c0 in range(0, splits[i], HEAD_DIM):
                    cols = slice(c0, c0 + HEAD_DIM)
                    dx, dg = _head_norm_bwd(p_ref[:, cols].astype(F32), x_refs[k][:, cols], g_refs[k][:, cols])
                    tmp[:, cols] = dx
                    dg_refs[k][:, cols] += dg
                src_ref = tmp
            for dst, src, n in runs:
                stage[:, dst:dst + n] = src_ref[:, src:src + n].astype(F32)
        o_ref[...] = stage[...].astype(o_ref.dtype)

    wmax = max(splits[i] for i in normed)
    row = lambda w: pl.BlockSpec((tm, w), lambda i: (i, 0))
    vec = lambda w: pl.BlockSpec((1, w), lambda i: (0, 0))
    outs = pl.pallas_call(
        body, name="dproj_merge", grid=(s // tm,),
        in_specs=[row(w) for w in splits] + [row(splits[i]) for i in normed] + [vec(splits[i]) for i in normed],
        out_specs=[row(wp)] + [vec(splits[i]) for i in normed],
        out_shape=[jax.ShapeDtypeStruct((s, wp), BF16)] + [jax.ShapeDtypeStruct((1, splits[i]), F32) for i in normed],
        scratch_shapes=[pltpu.VMEM((tm, wp), F32), pltpu.VMEM((tm, wmax), F32)],
        compiler_params=_params("arbitrary"),
    )(*parts, *[norms[i][0] for i in normed], *[norms[i][1] for i in normed])
    return outs[0], dict(zip(normed, outs[1:]))


def _norm_fwd(x, g, *, group, name, tm=256):
    s, w = x.shape
    ng = w // group

    def body(x_ref, g_ref, o_ref):
        for i in range(ng):
            cols = slice(i * group, (i + 1) * group)
            xv = x_ref[:, cols]
            r = lax.rsqrt(jnp.mean(xv * xv, axis=-1, keepdims=True) + EPS)
            o_ref[:, cols] = ((xv * r) * g_ref[:, cols]).astype(o_ref.dtype)

    return pl.pallas_call(
        body, name=name, grid=(s // tm,),
        in_specs=[pl.BlockSpec((tm, w), lambda i: (i, 0)), pl.BlockSpec((1, w), lambda i: (0, 0))],
        out_specs=pl.BlockSpec((tm, w), lambda i: (i, 0)),
        out_shape=jax.ShapeDtypeStruct((s, w), BF16),
        compiler_params=_params("parallel"),
    )(x, g)


def _norm_bwd(dy, x, g, *, group, name, res=None, out_dtypes=(BF16,), tm=256, deps=()):
    s, w = x.shape
    ng = w // group
    n_in = 4 if res is not None else 3

    def body(*refs):
        dy_ref, x_ref, g_ref = refs[:3]
        res_ref = refs[3] if res is not None else None
        outs = refs[n_in + len(deps):]
        dx_refs, dg_ref = outs[:-1], outs[-1]

        @pl.when(pl.program_id(0) == 0)
        def _():
            dg_ref[...] = jnp.zeros_like(dg_ref)

        for i in range(ng):
            cols = slice(i * group, (i + 1) * group)
            xv = x_ref[:, cols]
            dyv = dy_ref[:, cols].astype(F32)
            r = lax.rsqrt(jnp.mean(xv * xv, axis=-1, keepdims=True) + EPS)
            xr = xv * r
            dg_ref[:, cols] += jnp.sum(dyv * xr, axis=0, keepdims=True)
            gdy = dyv * g_ref[:, cols]
            dx = r * (gdy - xr * jnp.mean(gdy * xr, axis=-1, keepdims=True))
            if res_ref is not None:
                dx = dx + res_ref[:, cols]
            for dx_ref in dx_refs:
                dx_ref[:, cols] = dx.astype(dx_ref.dtype)

    row = pl.BlockSpec((tm, w), lambda i: (i, 0))
    vec = pl.BlockSpec((1, w), lambda i: (0, 0))
    in_specs = [row, row, vec] + ([row] if res is not None else []) + [pl.BlockSpec(memory_space=pl.ANY)] * len(deps)
    args = [dy, x, g] + ([res] if res is not None else []) + list(deps)
    out_specs = [row] * len(out_dtypes) + [vec]
    out_shape = [jax.ShapeDtypeStruct((s, w), dt) for dt in out_dtypes] + [jax.ShapeDtypeStruct((1, w), F32)]
    return pl.pallas_call(
        body, name=name, grid=(s // tm,), in_specs=in_specs, out_specs=out_specs,
        out_shape=out_shape, compiler_params=_params("arbitrary"),
    )(*args)


def _split3(v):
    p1 = v.astype(BF16)
    r1 = v - p1.astype(F32)
    p2 = r1.astype(BF16)
    p3 = (r1 - p2.astype(F32)).astype(BF16)
    return p1, p2, p3


def _tri_sum(v, reverse, tcol=512):
    h, s = v.shape
    tcol = min(tcol, s)
    parts = _split3(v)
    outs = []
    for j in range(s // tcol):
        src = lax.broadcasted_iota(jnp.int32, (s, tcol), 0)
        dst = lax.broadcasted_iota(jnp.int32, (s, tcol), 1) + j * tcol
        keep = (src >= dst) if reverse else (src <= dst)
        tri = jnp.where(keep, 1.0, 0.0).astype(BF16)
        acc = jnp.zeros((h, tcol), F32)
        for p in parts:
            acc = acc + jnp.dot(p, tri, preferred_element_type=F32)
        outs.append(acc)
    return outs


def _forget_fwd(fa_t, b):
    h, s = fa_t.shape
    tcol = min(512, s)

    def body(f_ref, b_ref, c_ref):
        z = f_ref[...] + b_ref[...]
        logf = jnp.minimum(z, 0.0) - jnp.log(1.0 + jnp.exp(-jnp.abs(z)))
        for j, blk in enumerate(_tri_sum(logf, reverse=False, tcol=tcol)):
            c_ref[:, j * tcol:(j + 1) * tcol] = blk

    return pl.pallas_call(
        body, name="forget_fwd", out_shape=jax.ShapeDtypeStruct((h, s), F32),
        compiler_params=_params(),
    )(fa_t, b)


def _forget_bwd(dacol, fa_t, b):
    h, s = fa_t.shape
    tcol = min(512, s)

    def body(d_ref, f_ref, b_ref, dfa_ref, db_ref):
        z = f_ref[...] + b_ref[...]
        dc = -d_ref[...]
        total = jnp.zeros((h, 1), F32)
        for j, blk in enumerate(_tri_sum(dc, reverse=True, tcol=tcol)):
            cols = slice(j * tcol, (j + 1) * tcol)
            dfa = blk * (1.0 - jax.nn.sigmoid(z[:, cols]))
            dfa_ref[:, cols] = dfa
            total = total + jnp.sum(dfa, axis=-1, keepdims=True)
        db_ref[...] = total

    return pl.pallas_call(
        body, name="forget_bwd",
        out_shape=[jax.ShapeDtypeStruct((h, s), F32), jax.ShapeDtypeStruct((h, 1), F32)],
        compiler_params=_params(),
    )(dacol, fa_t, b)


def _distance_bias(s, tile, dilated):
    nb = s // tile
    b = lax.broadcasted_iota(jnp.int32, (nb, tile, tile), 0)
    dist = b * tile + lax.broadcasted_iota(jnp.int32, (nb, tile, tile), 1) - lax.broadcasted_iota(jnp.int32, (nb, tile, tile), 2)
    if not dilated:
        return jnp.where(dist >= 0, 0.0, NEG).astype(F32)
    mult = jnp.zeros(dist.shape, jnp.int32)
    for window, dil in DIL_PATTERNS:
        mult = mult + ((dist >= 0) & (dist <= window) & ((dist & (dil - 1)) == 0)).astype(jnp.int32)
    logm = jnp.where(mult == 3, math.log(3.0), jnp.where(mult == 2, math.log(2.0), 0.0))
    return jnp.where(mult > 0, logm, NEG).astype(F32)


def _logits(q, k, arow, acol, bias):
    s = lax.dot_general(q, k, (((1,), (1,)), ((), ())), preferred_element_type=F32)
    return s * (1.0 / math.sqrt(HEAD_DIM)) + arow - acol + bias


def _attn_fwd(q, k, v, arow, acol, *, dilated, name, tq=ATTN_TILE, tk=ATTN_TILE):
    two_term = not dilated
    s, w = q.shape
    nh = w // HEAD_DIM
    assert tq == tk
    tq = tk = min(tq, s)
    nq, nk = s // tq, s // tk

    def body(q_ref, k_ref, v_ref, ar_ref, ac_ref, b_ref, o_ref, of_ref, lse_ref, m_ref, l_ref, acc_ref):
        qi, ki = pl.program_id(1), pl.program_id(2)

        @pl.when(ki == 0)
        def _():
            m_ref[...] = jnp.full_like(m_ref, NEG)
            l_ref[...] = jnp.zeros_like(l_ref)
            acc_ref[...] = jnp.zeros_like(acc_ref)

        @pl.when(ki <= qi)
        def _():
            sc = _logits(q_ref[...], k_ref[...], ar_ref[...], ac_ref[...], b_ref[...])
            m_new = jnp.maximum(m_ref[...], jnp.max(sc, axis=-1, keepdims=True))
            alpha = jnp.exp(m_ref[...] - m_new)
            p = jnp.exp(sc - m_new)
            l_ref[...] = alpha * l_ref[...] + jnp.sum(p, axis=-1, keepdims=True)
            p_hi = p.astype(BF16)
            vv = v_ref[...]
            pv = jnp.dot(p_hi, vv, preferred_element_type=F32)
            if two_term:
                pv = pv + jnp.dot((p - p_hi.astype(F32)).astype(BF16), vv, preferred_element_type=F32)
            acc_ref[...] = alpha * acc_ref[...] + pv
            m_ref[...] = m_new

        @pl.when(ki == nk - 1)
        def _():
            out = acc_ref[...] / l_ref[...]
            o_ref[...] = out.astype(o_ref.dtype)
            of_ref[...] = out
            lse_ref[...] = m_ref[...] + jnp.log(l_ref[...])

    kv = pl.BlockSpec((tk, HEAD_DIM), lambda h, i, j: (jnp.minimum(j, i), h))
    return pl.pallas_call(
        body, name=name, grid=(nh, nq, nk),
        in_specs=[pl.BlockSpec((tq, HEAD_DIM), lambda h, i, j: (i, h)), kv, kv,
                  pl.BlockSpec((None, tq, 1), lambda h, i, j: (h, i, 0)),
                  pl.BlockSpec((None, 1, tk), lambda h, i, j: (h, 0, jnp.minimum(j, i))),
                  pl.BlockSpec((None, tq, tk), lambda h, i, j: (jnp.maximum(i - j, 0), 0, 0))],
        out_specs=[pl.BlockSpec((tq, HEAD_DIM), lambda h, i, j: (i, h)),
                   pl.BlockSpec((tq, HEAD_DIM), lambda h, i, j: (i, h)),
                   pl.BlockSpec((None, tq, 1), lambda h, i, j: (h, i, 0))],
        out_shape=[jax.ShapeDtypeStruct((s, w), BF16), jax.ShapeDtypeStruct((s, w), F32),
                   jax.ShapeDtypeStruct((nh, s, 1), F32)],
        scratch_shapes=[pltpu.VMEM((tq, 1), F32), pltpu.VMEM((tq, 1), F32), pltpu.VMEM((tq, HEAD_DIM), F32)],
        compiler_params=_params("parallel", "parallel", "arbitrary"),
    )(q, k, v, arow, acol, _distance_bias(s, tq, dilated))


def _attn_bwd(q, k, v, o, do, lse, arow, acol, *, dilated, name, tq=ATTN_TILE, tk=ATTN_TILE):
    s, w = q.shape
    nh = w // HEAD_DIM
    assert tq == tk
    tq = tk = min(tq, s)
    nq, nk = s // tq, s // tk
    scale = 1.0 / math.sqrt(HEAD_DIM)

    def body(q_ref, k_ref, v_ref, o_ref, do_ref, lse_ref, ar_ref, ac_ref, b_ref,
             dq_ref, dk_ref, dv_ref, dac_ref, dk_acc, dv_acc, dac_acc):
        ki, qi = pl.program_id(1), pl.program_id(2)

        @pl.when((ki == 0) & (qi == 0))
        def _():
            dq_ref[...] = jnp.zeros_like(dq_ref)

        @pl.when(qi == 0)
        def _():
            dk_acc[...] = jnp.zeros_like(dk_acc)
            dv_acc[...] = jnp.zeros_like(dv_acc)
            dac_acc[...] = jnp.zeros_like(dac_acc)

        @pl.when(qi >= ki)
        def _():
            qv, kvv, dov = q_ref[...], k_ref[...], do_ref[...]
            sc = _logits(qv, kvv, ar_ref[...], ac_ref[...], b_ref[...])
            p = jnp.exp(sc - lse_ref[...])
            dp = lax.dot_general(dov, v_ref[...], (((1,), (1,)), ((), ())), preferred_element_type=F32)
            delta = jnp.sum(dov.astype(F32) * o_ref[...].astype(F32), axis=-1, keepdims=True)
            ds = p * (dp - delta)
            dsb = ds.astype(BF16)
            dv_acc[...] += lax.dot_general(p.astype(BF16), dov, (((0,), (0,)), ((), ())), preferred_element_type=F32)
            dk_acc[...] += lax.dot_general(dsb, qv, (((0,), (0,)), ((), ())), preferred_element_type=F32)
            rows = pl.ds(pl.multiple_of(qi * tq, tq), tq)
            dq_ref[rows, :] += jnp.dot(dsb, kvv, preferred_element_type=F32) * scale
            dac_acc[...] += jnp.sum(ds, axis=0, keepdims=True)

        @pl.when(qi == nq - 1)
        def _():
            dk_ref[...] = dk_acc[...] * scale
            dv_ref[...] = dv_acc[...]
            dac_ref[...] = dac_acc[...]

    qs = pl.BlockSpec((tq, HEAD_DIM), lambda h, j, i: (jnp.maximum(i, j), h))
    ks = pl.BlockSpec((tk, HEAD_DIM), lambda h, j, i: (j, h))
    rowv = pl.BlockSpec((None, tq, 1), lambda h, j, i: (h, jnp.maximum(i, j), 0))
    colv = pl.BlockSpec((None, 1, tk), lambda h, j, i: (h, 0, j))
    return pl.pallas_call(
        body, name=name, grid=(nh, nk, nq),
        in_specs=[qs, ks, ks, qs, qs, rowv, rowv, colv,
                  pl.BlockSpec((None, tq, tk), lambda h, j, i: (jnp.maximum(i - j, 0), 0, 0))],
        out_specs=[pl.BlockSpec((s, HEAD_DIM), lambda h, j, i: (0, h)), ks, ks, colv],
        out_shape=[jax.ShapeDtypeStruct((s, w), F32), jax.ShapeDtypeStruct((s, w), F32),
                   jax.ShapeDtypeStruct((s, w), F32), jax.ShapeDtypeStruct((nh, 1, s), F32)],
        scratch_shapes=[pltpu.VMEM((tk, HEAD_DIM), F32), pltpu.VMEM((tk, HEAD_DIM), F32), pltpu.VMEM((1, tk), F32)],
        compiler_params=_params("arbitrary", "arbitrary", "arbitrary"),
    )(q, k, v, o, do, lse, arow, acol, _distance_bias(s, tq, dilated))


def _gate_fwd(ga, gb, pa, pb, tm=256):
    s, d = ga.shape

    def body(ga_ref, gb_ref, pa_ref, pb_ref, o_ref):
        o_ref[...] = (jax.nn.sigmoid(ga_ref[...]) * pa_ref[...]
                      + jax.nn.sigmoid(gb_ref[...]) * pb_ref[...]).astype(o_ref.dtype)

    row = pl.BlockSpec((tm, d), lambda i: (i, 0))
    return pl.pallas_call(
        body, name="gate_fwd", grid=(s // tm,), in_specs=[row] * 4, out_specs=row,
        out_shape=jax.ShapeDtypeStruct((s, d), BF16), compiler_params=_params("parallel"),
    )(ga, gb, pa, pb)


def _gate_bwd(dm, ga, gb, pa, pb, tm=256):
    s, d = ga.shape

    def body(dm_ref, ga_ref, gb_ref, pa_ref, pb_ref, dpa_ref, dpb_ref, dga_ref, dgb_ref):
        dmv = dm_ref[...]
        for g_ref, p_ref, dp_ref, dg_ref in ((ga_ref, pa_ref, dpa_ref, dga_ref), (gb_ref, pb_ref, dpb_ref, dgb_ref)):
            sg = jax.nn.sigmoid(g_ref[...])
            dp_ref[...] = (dmv * sg).astype(BF16)
            dg_ref[...] = (dmv * p_ref[...] * (sg * (1.0 - sg))).astype(BF16)

    row = pl.BlockSpec((tm, d), lambda i: (i, 0))
    return pl.pallas_call(
        body, name="gate_bwd", grid=(s // tm,), in_specs=[row] * 5, out_specs=[row] * 4,
        out_shape=[jax.ShapeDtypeStruct((s, d), BF16)] * 4, compiler_params=_params("parallel"),
    )(dm, ga, gb, pa, pb)


def _shift_down(u, k):
    row = lax.broadcasted_iota(jnp.int32, u.shape, 0)
    return jnp.where(row >= k, pltpu.roll(u, k, 0), 0.0)


def _shift_up(u, k):
    n = u.shape[0]
    row = lax.broadcasted_iota(jnp.int32, u.shape, 0)
    return jnp.where(row < n - k, pltpu.roll(u, n - k, 0), 0.0)


def _conv3(u, wc, b):
    return wc[0:1, :] * _shift_down(u, 2) + wc[1:2, :] * _shift_down(u, 1) + wc[2:3, :] * u + b


def _conv_glu_fwd(u, wc, b, tn=256):
    s, f2 = u.shape
    f = f2 // 2
    nb = f // tn

    def body(ug_ref, uv_ref, wg_ref, wv_ref, bg_ref, bv_ref, o_ref):
        cg = _conv3(ug_ref[...], wg_ref[...], bg_ref[...])
        cv = _conv3(uv_ref[...], wv_ref[...], bv_ref[...])
        o_ref[...] = (cg * jax.nn.sigmoid(cg) * cv).astype(o_ref.dtype)

    def cols(rows, off):
        return pl.BlockSpec((rows, tn), lambda j: (0, j + off))

    return pl.pallas_call(
        body, name="conv_glu_fwd", grid=(nb,),
        in_specs=[cols(s, 0), cols(s, nb), cols(3, 0), cols(3, nb), cols(1, 0), cols(1, nb)],
        out_specs=cols(s, 0), out_shape=jax.ShapeDtypeStruct((s, f), BF16),
        compiler_params=_params("parallel"),
    )(u, u, wc, wc, b, b)


def _conv_glu_bwd(u, da, wc, b, tn=256):
    s, f2 = u.shape
    f = f2 // 2
    nb = f // tn

    def body(ug_ref, uv_ref, da_ref, wg_ref, wv_ref, bg_ref, bv_ref, dug_ref, duv_ref, sg_ref, sv_ref):
        ug, uv, wg, wv = ug_ref[...], uv_ref[...], wg_ref[...], wv_ref[...]
        cg = _conv3(ug, wg, bg_ref[...])
        cv = _conv3(uv, wv, bv_ref[...])
        sig = jax.nn.sigmoid(cg)
        dav = da_ref[...]
        dcv = dav * (cg * sig)
        dcg = dav * cv * (sig * (1.0 + cg * (1.0 - sig)))
        for dc, uu, w, du_ref, st_ref in ((dcg, ug, wg, dug_ref, sg_ref), (dcv, uv, wv, duv_ref, sv_ref)):
            du = w[2:3, :] * dc + w[1:2, :] * _shift_up(dc, 1) + w[0:1, :] * _shift_up(dc, 2)
            du_ref[...] = du.astype(BF16)
            st_ref[...] = jnp.zeros_like(st_ref)
            st_ref[0:1, :] = jnp.sum(dc * _shift_down(uu, 2), axis=0, keepdims=True)
            st_ref[1:2, :] = jnp.sum(dc * _shift_down(uu, 1), axis=0, keepdims=True)
            st_ref[2:3, :] = jnp.sum(dc * uu, axis=0, keepdims=True)
            st_ref[3:4, :] = jnp.sum(dc, axis=0, keepdims=True)

    def cols(rows, off):
        return pl.BlockSpec((rows, tn), lambda j: (0, j + off))

    return pl.pallas_call(
        body, name="conv_glu_bwd", grid=(nb,),
        in_specs=[cols(s, 0), cols(s, nb), cols(s, 0), cols(3, 0), cols(3, nb), cols(1, 0), cols(1, nb)],
        out_specs=[cols(s, 0), cols(s, 0), cols(8, 0), cols(8, 0)],
        out_shape=[jax.ShapeDtypeStruct((s, f), BF16), jax.ShapeDtypeStruct((s, f), BF16),
                   jax.ShapeDtypeStruct((8, f), F32), jax.ShapeDtypeStruct((8, f), F32)],
        compiler_params=_params("parallel"),
    )(u, u, da, wc, wc, b, b)


ROW_TILES = (256, 128, 64, 32, 16, 8)
BLOCK_BYTES = 2 << 20


def _add_halves(g, r1, place):
    ns, r, c = g.shape
    rh = r // 2
    tr = _pick(rh, ROW_TILES)
    g4 = g.reshape(ns, 2, rh, c)

    def body(p_ref, g_ref, r_ref, o_ref):
        o_ref[...] = (g_ref[...].astype(F32) + r_ref[...].astype(F32)).astype(o_ref.dtype)

    def slab(s, pr):
        return s + (s >= pr[0]).astype(jnp.int32)

    return pl.pallas_call(
        body, name="add_halves",
        grid_spec=pltpu.PrefetchScalarGridSpec(
            num_scalar_prefetch=1, grid=(ns - 1, rh // tr),
            in_specs=[pl.BlockSpec((None, None, tr, c), lambda s, i, pr: (slab(s, pr), pr[1], i, 0)),
                      pl.BlockSpec((None, tr, c), lambda s, i, pr: (slab(s, pr), i, 0))],
            out_specs=pl.BlockSpec((None, tr, c), lambda s, i, pr: (slab(s, pr), i, 0))),
        out_shape=jax.ShapeDtypeStruct((ns, rh, c), BF16),
        compiler_params=_params("parallel", "parallel"),
    )(place, g4, r1)


def _sum_chips(g, r1, recv, place):
    ns, r, c = g.shape
    rh = r // 2
    tr = _pick(rh, ROW_TILES)
    g4 = g.reshape(ns, 2, rh, c)

    def body(p_ref, g_ref, r_ref, t0_ref, t1_ref, t2_ref, o_ref):
        own = g_ref[...].astype(F32) + r_ref[...].astype(F32)
        o_ref[...] = ((own + t0_ref[...].astype(F32)) + t1_ref[...].astype(F32)) + t2_ref[...].astype(F32)

    def peer(k):
        return pl.BlockSpec((None, tr, c), lambda i, pr: (k, i, 0))

    return pl.pallas_call(
        body, name="sum_chips",
        grid_spec=pltpu.PrefetchScalarGridSpec(
            num_scalar_prefetch=1, grid=(rh // tr,),
            in_specs=[pl.BlockSpec((None, None, tr, c), lambda i, pr: (pr[0], pr[1], i, 0)),
                      pl.BlockSpec((None, tr, c), lambda i, pr: (pr[0], i, 0)), peer(0), peer(1), peer(2)],
            out_specs=pl.BlockSpec((tr, c), lambda i, pr: (pr[1] * (rh // tr) + i, 0))),
        out_shape=jax.ShapeDtypeStruct((r, c), F32),
        compiler_params=_params("parallel"),
    )(place, g4, r1, recv, recv, recv)


def _sum_devices(packs):
    n, r, c = packs.shape

    def body(p_ref, o_ref):
        acc = p_ref[0]
        for d in range(1, n):
            acc = acc + p_ref[d]
        o_ref[...] = acc

    return pl.pallas_call(
        body, name="sum_devices", out_shape=jax.ShapeDtypeStruct((r, c), F32), compiler_params=_params(),
    )(packs)


def _adamw_update(wv, gv, mv, vv):
    c1 = 1.0 - ADAM_B1 ** ADAM_STEP
    c2 = 1.0 - ADAM_B2 ** ADAM_STEP
    mn = ADAM_B1 * mv + (1.0 - ADAM_B1) * gv
    vn = ADAM_B2 * vv + (1.0 - ADAM_B2) * (gv * gv)
    m_hat = mn / c1
    v_hat = vn / c2
    return -ADAM_LR * (m_hat / (jnp.sqrt(v_hat) + ADAM_EPS) + ADAM_WD * wv), mn, vn


def _adamw(w, g, m, v, name, deps=(), emit_grad=False):
    r, c = w.shape
    tr = _pick(r, [t for t in ROW_TILES if t * c * 4 <= BLOCK_BYTES]) if r >= 8 else r
    n_out = 4 if emit_grad else 3

    def body(w_ref, g_ref, m_ref, v_ref, *rest):
        outs = rest[-n_out:]
        gv = g_ref[:, :c]
        if emit_grad:
            outs[0][...] = gv
        outs[-3][...], outs[-2][...], outs[-1][...] = _adamw_update(w_ref[...], gv, m_ref[...], v_ref[...])

    blk = pl.BlockSpec((tr, c), lambda i: (i, 0))
    g_blk = pl.BlockSpec((tr, g.shape[1]), lambda i: (i, 0))
    return pl.pallas_call(
        body, name=name, grid=(r // tr,), in_specs=[blk, g_blk, blk, blk] + [ANY] * len(deps), out_specs=[blk] * n_out,
        out_shape=[jax.ShapeDtypeStruct((r, c), F32)] * n_out, compiler_params=_params("parallel"),
    )(w, g, m, v, *deps)


ANY = pl.BlockSpec(memory_space=pl.ANY)


def _place():
    x, y, c = lax.axis_index("x"), lax.axis_index("y"), lax.axis_index("c")
    chips = [(1 - x, y), (x, 1 - y), (1 - x, 1 - y)]
    return x, y, c, chips


def _remote(src, dst, send_sem, recv_sem, to):
    return pltpu.make_async_remote_copy(src_ref=src, dst_ref=dst, send_sem=send_sem, recv_sem=recv_sem,
                                        device_id=to, device_id_type=MESH)


HBM = pl.BlockSpec(memory_space=pltpu.HBM)
SEM = pl.BlockSpec(memory_space=pltpu.SEMAPHORE)
EFFECT = pltpu.SideEffectType.DATAFLOW_SIDE_EFFECTING


def _in_hbm(a):
    return pltpu.with_memory_space_constraint(a, pltpu.HBM)


def _half(ref_rows, who):
    return pl.ds(who * (ref_rows // 2), ref_rows // 2)


def _gather_start(groups, name):
    items = [it for g in groups for it in g]
    n = len(items)
    sizes = [len(g) for g in groups]

    def body(*refs):
        srcs, lands = refs[:n], refs[n:2 * n]
        sems = refs[2 * n:2 * n + 2 * len(groups)]
        token = refs[-1]
        x, y, c, chips = _place()
        j = 2 * x + y
        at = 0
        for gi, g in enumerate(groups):
            send, recv = sems[2 * gi], sems[2 * gi + 1]
            for i, (shard, split) in enumerate(g):
                src, land = srcs[at], lands[at]
                at += 1
                rows = _half(shard.shape[0], c) if split else slice(None)
                for k, chip in enumerate(chips):
                    _remote(src.at[rows], land.at[j, rows], send.at[4 * i + k], recv.at[4 * i + k], (*chip, c)).start()
                _remote(src, land.at[j], send.at[4 * i + 3], recv.at[4 * i + 3], (x, y, 1 - c)).start()
        token[...] = jnp.zeros_like(token)

    sem_shapes = []
    for sz in sizes:
        sem_shapes += [pltpu.SemaphoreType.DMA((4 * sz,)), pltpu.SemaphoreType.DMA((4 * sz,))]
    out_shape = (sem_shapes + [pltpu.HBM(sh.shape, sh.dtype) for sh, _ in items]
                 + [pltpu.HBM((N_CHIPS,) + sh.shape, sh.dtype) for sh, _ in items]
                 + [jax.ShapeDtypeStruct((8, LANES), F32)])
    ns = len(sem_shapes)
    outs = pl.pallas_call(
        body, name=name, in_specs=[HBM] * (2 * n),
        out_specs=[SEM] * ns + [HBM] * (2 * n) + [pl.BlockSpec(memory_space=pltpu.VMEM)],
        out_shape=out_shape, input_output_aliases={i: ns + i for i in range(2 * n)},
        compiler_params=pltpu.CompilerParams(has_side_effects=EFFECT),
    )(*[_in_hbm(sh) for sh, _ in items], *[_in_hbm(lax.empty((N_CHIPS,) + sh.shape, sh.dtype)) for sh, _ in items])
    sems, shards, lands, token = outs[:ns], outs[ns:ns + n], outs[ns + n:ns + 2 * n], outs[-1]
    res, at = [], 0
    for gi, sz in enumerate(sizes):
        res.append((shards[at:at + sz], lands[at:at + sz], sems[2 * gi], sems[2 * gi + 1]))
        at += sz
    return res, token


def _gather_pass(group, started, after, name):
    shards, lands, send, recv = started
    n = len(group)
    split_ix = [i for i, (_, split) in enumerate(group) if split]

    def body(*refs):
        lnds, send1, recv1 = refs[n:2 * n], refs[2 * n], refs[2 * n + 1]
        outs = refs[2 * n + 2 + len(after):]
        send2, recv2, token = outs[2 * n], outs[2 * n + 1], outs[2 * n + 2]
        x, y, c, chips = _place()
        sib = (x, y, 1 - c)
        for i, (shard, split) in enumerate(group):
            rows = _half(shard.shape[0], c) if split else slice(None)
            for k, (cx, cy) in enumerate(chips):
                landed = lnds[i].at[2 * cx + cy, rows]
                cp = _remote(landed, landed, send1.at[4 * i + k], recv1.at[4 * i + k], sib)
                cp.wait_send()
                cp.wait_recv()
            own = lnds[i].at[2 * x + y]
            cp = _remote(own, own, send1.at[4 * i + 3], recv1.at[4 * i + 3], sib)
            cp.wait_send()
            cp.wait_recv()
        for i2, i in enumerate(split_ix):
            rows = _half(group[i][0].shape[0], c)
            for k, (cx, cy) in enumerate(chips):
                landed = lnds[i].at[2 * cx + cy, rows]
                _remote(landed, landed, send2.at[3 * i2 + k], recv2.at[3 * i2 + k], sib).start()
        token[...] = jnp.zeros_like(token)

    n2 = len(split_ix)
    out_shape = ([pltpu.HBM(a.shape, a.dtype) for a in (*shards, *lands)]
                 + [pltpu.SemaphoreType.DMA((3 * n2,)), pltpu.SemaphoreType.DMA((3 * n2,)), jax.ShapeDtypeStruct((8, LANES), F32)])
    outs = pl.pallas_call(
        body, name=name, in_specs=[HBM] * (2 * n) + [SEM, SEM] + [ANY] * len(after),
        out_specs=[HBM] * (2 * n) + [SEM, SEM, pl.BlockSpec(memory_space=pltpu.VMEM)],
        out_shape=out_shape, input_output_aliases={i: i for i in range(2 * n)},
        compiler_params=pltpu.CompilerParams(has_side_effects=EFFECT),
    )(*shards, *lands, send, recv, *after)
    return outs[:n], (outs[n:2 * n], outs[2 * n], outs[2 * n + 1]), outs[2 * n + 2]


def _gather_wait(group, passed, after, name):
    lands, send2, recv2 = passed
    n = len(group)
    split_ix = [i for i, (_, split) in enumerate(group) if split]

    def body(*refs):
        lnds, s2, r2 = refs[:n], refs[n], refs[n + 1]
        x, y, c, chips = _place()
        sib = (x, y, 1 - c)
        for i2, i in enumerate(split_ix):
            rows = _half(group[i][0].shape[0], 1 - c)
            for k, (cx, cy) in enumerate(chips):
                landed = lnds[i].at[2 * cx + cy, rows]
                cp = _remote(landed, landed, s2.at[3 * i2 + k], r2.at[3 * i2 + k], sib)
                cp.wait_send()
                cp.wait_recv()

    return pl.pallas_call(
        body, name=name, in_specs=[HBM] * n + [SEM, SEM, ANY], out_specs=[HBM] * n,
        out_shape=[pltpu.HBM(a.shape, a.dtype) for a in lands], input_output_aliases={i: i for i in range(n)},
        compiler_params=pltpu.CompilerParams(has_side_effects=EFFECT),
    )(*lands, send2, recv2, after)


def _xfer_start(name, srcs, land_shapes, n_copies, copies, after):
    n, nl = len(srcs), len(land_shapes)

    def body(*refs):
        src_refs, land_refs = refs[:n], refs[n:n + nl]
        send, recv, token = refs[n + nl + 1], refs[n + nl + 2], refs[-1]
        for cp in copies(src_refs, land_refs, send, recv):
            cp.start()
        token[...] = jnp.zeros_like(token)

    lands = [_in_hbm(lax.empty(shape, dtype)) for shape, dtype in land_shapes]
    out_shape = ([pltpu.SemaphoreType.DMA((n_copies,)), pltpu.SemaphoreType.DMA((n_copies,))]
                 + [pltpu.HBM(a.shape, a.dtype) for a in (*srcs, *lands)] + [jax.ShapeDtypeStruct((8, LANES), F32)])
    outs = pl.pallas_call(
        body, name=name, in_specs=[HBM] * (n + nl) + [ANY],
        out_specs=[SEM, SEM] + [HBM] * (n + nl) + [pl.BlockSpec(memory_space=pltpu.VMEM)],
        out_shape=out_shape, input_output_aliases={i: 2 + i for i in range(n + nl)},
        compiler_params=pltpu.CompilerParams(has_side_effects=EFFECT),
    )(*[_in_hbm(a) for a in srcs], *lands, after)
    return (outs[2:2 + n], outs[2 + n:2 + n + nl], outs[0], outs[1]), outs[-1]


def _xfer_wait(name, started, copies, after):
    srcs, lands, send, recv = started
    n, nl = len(srcs), len(lands)

    def body(*refs):
        src_refs, land_refs, s_ref, r_ref = refs[:n], refs[n:n + nl], refs[n + nl], refs[n + nl + 1]
        for cp in copies(src_refs, land_refs, s_ref, r_ref):
            cp.wait_send()
            cp.wait_recv()

    outs = pl.pallas_call(
        body, name=name, in_specs=[HBM] * (n + nl) + [SEM, SEM, ANY], out_specs=[HBM] * (n + nl),
        out_shape=[pltpu.HBM(a.shape, a.dtype) for a in (*srcs, *lands)],
        input_output_aliases={i: i for i in range(n + nl)},
        compiler_params=pltpu.CompilerParams(has_side_effects=EFFECT),
    )(*srcs, *lands, send, recv, after)
    return outs[:n], outs[n:]


def _swap_copies(srcs, lands, send, recv):
    x, y, c, _ = _place()
    return [_remote(src.at[:, _half(src.shape[1], 1 - c)], land, send.at[i], recv.at[i], (x, y, 1 - c))
            for i, (src, land) in enumerate(zip(srcs, lands))]


def _scatter_copies(srcs, lands, send, recv):
    x, y, c, chips = _place()
    return [_remote(src.at[2 * cx + cy], land.at[k], send.at[3 * i + k], recv.at[3 * i + k], (cx, cy, c))
            for i, (src, land) in enumerate(zip(srcs, lands)) for k, (cx, cy) in enumerate(chips)]


def _join_copies(srcs, lands, send, recv):
    x, y, c, _ = _place()
    return [_remote(src.at[_half(src.shape[0], c)], src.at[_half(src.shape[0], c)], send.at[i], recv.at[i], (x, y, 1 - c))
            for i, src in enumerate(srcs)]


def _corner(a):
    return a[(slice(0, 1),) * a.ndim]


class _Reducer:
    def __init__(self, place):
        self.place = place
        self.state = {}

    def swap(self, key, grads, after):
        shapes = [((g.shape[0], g.shape[1] // 2, g.shape[2]), g.dtype) for g in grads]
        self.state[key], token = _xfer_start("swap_start_" + key, grads, shapes, len(grads), _swap_copies, _corner(after))
        return token

    def to_chips(self, key, after):
        grads, from_sibling = _xfer_wait("swap_wait_" + key, self.state[key], _swap_copies, after)
        sums = [_add_halves(g, r, self.place) for g, r in zip(grads, from_sibling)]
        shapes = [((3,) + s.shape[1:], s.dtype) for s in sums]
        started, token = _xfer_start("scatter_start_" + key, sums, shapes, 3 * len(sums), _scatter_copies, _corner(sums[-1]))
        self.state[key] = (grads, from_sibling, started)
        return token

    def to_core(self, key, after):
        grads, from_sibling, started = self.state[key]
        _, from_chips = _xfer_wait("scatter_wait_" + key, started, _scatter_copies, after)
        shards = [_sum_chips(g, r, rc, self.place) for g, r, rc in zip(grads, from_sibling, from_chips)]
        self.state[key], token = _xfer_start("join_start_" + key, shards, [], len(shards), _join_copies, _corner(shards[-1]))
        return token

    def finish(self, key, after):
        return _xfer_wait("join_wait_" + key, self.state.pop(key), _join_copies, after)[0]


def _gather_packs(pack, deps=()):
    def body(p_ref, *rest):
        o_ref, lsem, ssem, rsem = rest[-4:]
        x, y, c, _ = _place()
        me = 4 * x + 2 * y + c
        local = pltpu.make_async_copy(p_ref, o_ref.at[me], lsem)
        local.start()
        cps = []
        for k in range(1, N_DEV):
            fx, fy, fc = (k >> 2) & 1, (k >> 1) & 1, k & 1
            to = (x ^ fx, y ^ fy, c ^ fc)
            cps.append(_remote(p_ref, o_ref.at[me], ssem.at[k - 1], rsem.at[k - 1], to))
        for cp in cps:
            cp.start()
        for k in range(1, N_DEV):
            fx, fy, fc = (k >> 2) & 1, (k >> 1) & 1, k & 1
            src = o_ref.at[4 * (x ^ fx) + 2 * (y ^ fy) + (c ^ fc)]
            _remote(src, src, ssem.at[k - 1], rsem.at[k - 1], (x, y, c)).wait_recv()
        for cp in cps:
            cp.wait_send()
        local.wait()

    return pl.pallas_call(
        body, name="gather_packs", in_specs=[ANY] * (1 + len(deps)), out_specs=ANY,
        out_shape=jax.ShapeDtypeStruct((N_DEV,) + pack.shape, pack.dtype),
        scratch_shapes=[pltpu.SemaphoreType.DMA, pltpu.SemaphoreType.DMA((N_DEV - 1,)), pltpu.SemaphoreType.DMA((N_DEV - 1,))],
    )(pack, *deps)


LANE_TILES = (512, 896, 1408, 704, 384, 256, 128)


def _layer_grads(x, target, small, wg, rest_pass, rest_wait, red, filler):
    s, d = x.shape
    f = wg["conv"].shape[1] // 2
    w_att = N_HEADS * HEAD_DIM
    in_splits = (w_att, w_att, w_att, N_HEADS, w_att, w_att, w_att, d, d)
    in_cols = sum(in_splits)
    cs = in_cols // N_CHIPS
    cp = wg["in"].shape[2]
    tm = min(s, 1024)
    t_in = cp
    t_d = _pick(d, LANE_TILES)
    t_d2 = min(d, 1024)
    t_dq = _pick(d // N_CHIPS, LANE_TILES)
    t_w = _pick(w_att, LANE_TILES)
    t_up = 2 * f // N_CHIPS
    tm_wide = min(s, 512)
    t_fq = _pick(f // N_CHIPS, LANE_TILES)

    h1 = _norm_fwd(x, small["g_attn"], group=d, name="rms1_fwd")
    proj_p = _mm(h1, wg["in"], mode="nn", b_kind="col", tm=tm_wide, tn=t_in, tk=d, name="mm_in")
    gains = {n: small[n].reshape(1, w_att) for n in ("g_q_fox", "g_k_fox", "g_q_dil", "g_k_dil")}
    qa, ka, va_b, fa, qb, kb, vb_b, ga, gb, qa_n, ka_n, qb_n, kb_n = _proj_split(
        proj_p, in_splits, cs, (F32, F32, BF16, F32, F32, F32, BF16, F32, F32),
        {0: gains["g_q_fox"], 1: gains["g_k_fox"], 4: gains["g_q_dil"], 5: gains["g_k_dil"]})
    fa_t = fa.T
    b_f = small["b_forget"].reshape(N_HEADS, 1)
    c_f = _forget_fwd(fa_t, b_f)
    slopes = jnp.asarray(2.0 ** (-8.0 * np.arange(1, N_HEADS + 1) / N_HEADS), dtype=F32)
    a_d = -(slopes[:, None] * jnp.arange(s, dtype=F32)[None, :])
    rows_f, cols_f = c_f[:, :, None], c_f[:, None, :]
    rows_d, cols_d = a_d[:, :, None], a_d[:, None, :]
    o_a, o_a32, lse_a = _attn_fwd(qa_n, ka_n, va_b, rows_f, cols_f, dilated=False, name="attn_fox_fwd")
    token = rest_pass("mid", o_a)
    rows_d = rows_d + token[0, 0]
    o_b, o_b32, lse_b = _attn_fwd(qb_n, kb_n, vb_b, rows_d, cols_d, dilated=True, name="attn_dil_fwd")
    wg = dict(wg, **rest_wait("mid", o_b))
    token = rest_pass("late", o_b)
    pa = _mm(o_a, wg["brf"], mode="nn", b_kind="col", tm=tm, tn=t_dq, tk=w_att, name="mm_brf", deps=(token,))
    pb = _mm(o_b, wg["brd"], mode="nn", b_kind="col", tm=tm, tn=t_dq, tk=w_att, name="mm_brd")
    merged = _gate_fwd(ga, gb, pa, pb)
    x1 = _mm(merged, wg["out"], mode="nn", b_kind="row", res=x, tm=tm, tn=t_d, tk=t_dq, name="mm_out")
    wg = dict(wg, **rest_wait("late", x1))
    h2 = _norm_fwd(x1, small["g_ffn"], group=d, name="rms2_fwd")
    u = _mm(h2, wg["up"], mode="nn", b_kind="col", tm=tm_wide, tn=t_up, tk=d, name="mm_up")
    act = _conv_glu_fwd(u, wg["conv"], wg["bconv"])
    dy_f, dy_b, loss_blk = _mm(act, wg["down"], mode="nn", b_kind="row", res=x1, loss_target=target,
                               tm=tm, tn=t_d2, tk=t_fq, name="mm_down")

    d_act = _mm(dy_b, wg["down"], mode="nt", b_kind="row", tm=tm, tn=t_fq, tk=d, name="mm_down_dx")
    g_down = _mm(act, dy_b, mode="tn", out_dtype=BF16, out_kind="row", tm=t_fq, tn=t_d2, tk=s, name="mm_down_dw")
    tok = red.swap("down", [g_down], g_down)
    du_g, du_v, st_g, st_v = _conv_glu_bwd(u, d_act, wg["conv"] + tok[0, 0], wg["bconv"])
    tok = red.to_chips("down", du_g)
    du = (du_g, du_v)
    g_up = _mm(h2, du, mode="tn", out_dtype=BF16, out_kind="col", tm=t_d2, tn=t_up // 2, tk=s, name="mm_up_dw", deps=(tok,))
    tok = red.to_core("down", g_up)
    tok2 = red.swap("up", [g_up], g_up)
    dh2 = _mm(du, wg["up"], mode="nt", b_kind="col", tm=tm, tn=t_d2, tk=t_up, name="mm_up_dx", deps=(tok, tok2))
    tok = red.to_chips("up", dh2)
    dx1_b, dx1_f, dg_ffn = _norm_bwd(dh2, x1, small["g_ffn"], group=d, res=dy_f, out_dtypes=(BF16, F32), name="rms2_bwd")
    d_merged = _mm(dx1_b, wg["out"], mode="nt", b_kind="row", tm=tm, tn=t_dq, tk=d, name="mm_out_dx", deps=(tok,))
    g_out = _mm(merged, dx1_b, mode="tn", out_dtype=BF16, out_kind="row", tm=t_dq, tn=t_d2, tk=s, name="mm_out_dw")
    dpa, dpb, dga, dgb = _gate_bwd(d_merged, ga, gb, pa, pb)
    do_a = _mm(dpa, wg["brf"], mode="nt", b_kind="col", out_dtype=BF16, tm=s, tn=w_att, tk=t_dq, name="mm_brf_dx")
    do_b = _mm(dpb, wg["brd"], mode="nt", b_kind="col", out_dtype=BF16, tm=s, tn=w_att, tk=t_dq, name="mm_brd_dx")
    g_brf = _mm(o_a, dpa, mode="tn", out_dtype=BF16, out_kind="col", tm=w_att, tn=t_dq, tk=s, name="mm_brf_dw")
    g_brd = _mm(o_b, dpb, mode="tn", out_dtype=BF16, out_kind="col", tm=w_att, tn=t_dq, tk=s, name="mm_brd_dw")
    tok = red.swap("mix", [g_out, g_brf, g_brd], g_brd)
    dqa_n, dka_n, dva, dac_a = _attn_bwd(qa_n, ka_n, va_b, o_a32, do_a, lse_a, rows_f + tok[0, 0], cols_f, dilated=False, name="attn_fox_bwd")
    tok = red.to_core("up", dqa_n)
    tok2 = red.to_chips("mix", dqa_n)
    dqb_n, dkb_n, dvb, _ = _attn_bwd(qb_n, kb_n, vb_b, o_b32, do_b, lse_b, rows_d + (tok[0, 0] + tok2[0, 0]), cols_d, dilated=True, name="attn_dil_bwd")
    tok = red.to_core("mix", dqb_n)
    dfa_t, db_f = _forget_bwd(dac_a[:, 0, :], fa_t, b_f)
    dproj_p, dgains = _dproj_merge(
        [dqa_n, dka_n, dva, dfa_t.T, dqb_n, dkb_n, dvb, dga, dgb], in_splits, cs, cp,
        {0: (qa, gains["g_q_fox"]), 1: (ka, gains["g_k_fox"]), 4: (qb, gains["g_q_dil"]), 5: (kb, gains["g_k_dil"])})
    dg_qf, dg_kf, dg_qd, dg_kd = dgains[0], dgains[1], dgains[4], dgains[5]
    g_in = _mm(h1, dproj_p, mode="tn", out_dtype=BF16, out_kind="col", tm=t_d2, tn=t_in, tk=s, name="mm_in_dw", deps=(tok,))
    tok = red.swap("in", [g_in], g_in)
    tok = red.to_chips("in", filler(tok))
    dh1 = _mm(dproj_p, wg["in"], mode="nt", b_kind="col", tm=tm, tn=t_d2, tk=t_in, name="mm_in_dx", deps=(tok,))
    grad_x, dg_attn = _norm_bwd(dh1, x, small["g_attn"], group=d, res=dx1_f, out_dtypes=(F32,), name="rms1_bwd")

    small_grads = {
        "g_attn": dg_attn, "b_forget": db_f.reshape(1, N_HEADS),
        "g_q_fox": dg_qf, "g_k_fox": dg_kf, "g_q_dil": dg_qd, "g_k_dil": dg_kd, "g_ffn": dg_ffn,
        "w_conv": jnp.concatenate([st_g[0:3], st_v[0:3]], axis=1),
        "b_conv": jnp.concatenate([st_g[3:4], st_v[3:4]], axis=1),
        "loss": loss_blk[0:1, 0:1],
    }
    return small_grads, grad_x


SMALL_ORDER = ("g_attn", "b_forget", "g_q_fox", "g_k_fox", "g_q_dil", "g_k_dil", "g_ffn", "w_conv", "b_conv", "loss")
WEIGHT_ORDER = ("g_attn", "w_in", "b_forget", "g_q_fox", "g_k_fox", "g_q_dil", "g_k_dil", "w_br_fox", "w_br_dil",
                "w_out", "g_ffn", "w_up", "w_conv", "b_conv", "w_down")
BIG = {"w_in": "in", "w_br_fox": "brf", "w_br_dil": "brd", "w_out": "out", "w_up": "up", "w_down": "down"}


def kernel(x, g_attn, w_in, b_forget, g_q_fox, g_k_fox, g_q_dil, g_k_dil, w_br_fox, w_br_dil, w_out, g_ffn, w_up, w_conv, b_conv, w_down, loss_target, m_g_attn, m_w_in, m_b_forget, m_g_q_fox, m_g_k_fox, m_g_q_dil, m_g_k_dil, m_w_br_fox, m_w_br_dil, m_w_out, m_g_ffn, m_w_up, m_w_conv, m_b_conv, m_w_down, v_g_attn, v_w_in, v_b_forget, v_g_q_fox, v_g_k_fox, v_g_q_dil, v_g_k_dil, v_w_br_fox, v_w_br_dil, v_w_out, v_g_ffn, v_w_up, v_w_conv, v_b_conv, v_w_down):
    w = dict(g_attn=g_attn, w_in=w_in, b_forget=b_forget, g_q_fox=g_q_fox, g_k_fox=g_k_fox, g_q_dil=g_q_dil,
             g_k_dil=g_k_dil, w_br_fox=w_br_fox, w_br_dil=w_br_dil, w_out=w_out, g_ffn=g_ffn, w_up=w_up,
             w_conv=w_conv, b_conv=b_conv, w_down=w_down)
    m = dict(g_attn=m_g_attn, w_in=m_w_in, b_forget=m_b_forget, g_q_fox=m_g_q_fox, g_k_fox=m_g_k_fox,
             g_q_dil=m_g_q_dil, g_k_dil=m_g_k_dil, w_br_fox=m_w_br_fox, w_br_dil=m_w_br_dil, w_out=m_w_out,
             g_ffn=m_g_ffn, w_up=m_w_up, w_conv=m_w_conv, b_conv=m_b_conv, w_down=m_w_down)
    v = dict(g_attn=v_g_attn, w_in=v_w_in, b_forget=v_b_forget, g_q_fox=v_g_q_fox, g_k_fox=v_g_k_fox,
             g_q_dil=v_g_q_dil, g_k_dil=v_g_k_dil, w_br_fox=v_w_br_fox, w_br_dil=v_w_br_dil, w_out=v_w_out,
             g_ffn=v_g_ffn, w_up=v_w_up, w_conv=v_w_conv, b_conv=v_b_conv, w_down=v_w_down)
    xi, yi, ci = lax.axis_index("x"), lax.axis_index("y"), lax.axis_index("c")
    chip = (2 * xi + yi).astype(jnp.int32)

    cs = w_in.shape[2]
    cp = _round_up(cs, LANES)
    shards = {
        "in": jnp.pad(w_in[0].astype(BF16), ((0, 0), (0, cp - cs))),
        "brf": w_br_fox[0].astype(BF16), "brd": w_br_dil[0].astype(BF16), "out": w_out[0].astype(BF16),
        "up": w_up[0].astype(BF16), "down": w_down[0].astype(BF16),
    }
    conv_pad = jnp.pad(w_conv[0], ((0, 8 - w_conv.shape[1]), (0, 0)))
    first = [(shards["in"], True), (conv_pad, False)]
    later = {"mid": ("brf", "brd", "out"), "late": ("up", "down")}
    groups = {key: [(shards[n], True) for n in members] for key, members in later.items()}
    (started_first, *started_later), token = _gather_start([first, *groups.values()], "gather_start")
    started = dict(zip(later, started_later))
    token, w["w_in"], m["w_in"], v["w_in"] = lax.optimization_barrier((token, w["w_in"], m["w_in"], v["w_in"]))
    w2, m2, v2 = ({n: a[n].reshape(a[n].shape[-2], a[n].shape[-1]) for n in BIG} for a in (w, m, v))
    early = (token, w2["w_in"], m2["w_in"], v2["w_in"])
    own_first, passed_first, token = _gather_pass(first, started_first, early, "gather_pass_in")
    land_in, land_conv = _gather_wait(first, passed_first, token, "gather_wait_in")
    wg = {"in": land_in, "bconv": b_conv,
          "conv": jnp.transpose(land_conv[:, :w_conv.shape[1], :], (1, 0, 2)).reshape(w_conv.shape[1], -1)}
    small = {n: w[n] for n in ("g_attn", "b_forget", "g_q_fox", "g_k_fox", "g_q_dil", "g_k_dil", "g_ffn")}
    small = {n: (a[0] if a.ndim == 3 else a) for n, a in small.items()}
    in_flight = {}

    def rest_pass(key, after):
        own, passed, tok = _gather_pass(groups[key], started[key], (after,), "gather_pass_" + key)
        in_flight[key] = (own, passed)
        return tok

    def rest_wait(key, after):
        own, passed = in_flight.pop(key)
        lands = _gather_wait(groups[key], passed, after, "gather_wait_" + key)
        return dict(zip(later[key], lands))

    reducer = _Reducer(jnp.stack([chip, ci.astype(jnp.int32)]))
    g_out, d_out, m_out, v_out = {}, {}, {}, {}
    reduced = {}

    def first_element(arrays):
        return jnp.stack([a[(0,) * a.ndim] for a in arrays])

    def update_big(n, deps):
        g2, dl, mn, vn = _adamw(w2[n], reduced[BIG[n]], m2[n], v2[n], name="adamw_" + n, deps=deps, emit_grad=True)
        g_out[n], d_out[n], m_out[n], v_out[n] = (a.reshape(w[n].shape) for a in (g2, dl, mn, vn))

    def update_down(tok):
        (reduced["down"],) = reducer.finish("down", tok)
        update_big("w_down", (tok,))
        return v_out["w_down"]

    small_grads, grad_x = _layer_grads(x[0], loss_target[0], small, wg, rest_pass, rest_wait, reducer, update_down)

    for key, members in (("up", ("up",)), ("mix", ("out", "brf", "brd"))):
        reduced.update(zip(members, reducer.finish(key, grad_x)))
    others = ("w_up", "w_out", "w_br_fox", "w_br_dil")
    for n in others:
        update_big(n, (grad_x,))

    flat = jnp.concatenate([small_grads[n].reshape(-1) for n in SMALL_ORDER])
    rows = _round_up(flat.shape[0], 8 * LANES) // LANES
    pack = jnp.pad(flat, (0, rows * LANES - flat.shape[0])).reshape(rows, LANES)
    packs = _gather_packs(pack, deps=(first_element([v_out[n] for n in others]),))
    total = _sum_devices(packs).reshape(-1)
    red, at = {}, 0
    for n in SMALL_ORDER:
        size = small_grads[n].size
        red[n] = total[at:at + size].reshape(small_grads[n].shape)
        at += size
    loss = red["loss"].reshape(())
    c2 = w_conv.shape[2]
    red["w_conv"] = lax.dynamic_slice_in_dim(red["w_conv"], chip * c2, c2, axis=1)

    smalls = [n for n in WEIGHT_ORDER if n not in BIG]
    for n in smalls:
        shape = w[n].shape
        r2 = (shape[-2], shape[-1]) if n not in ("g_attn", "b_forget", "g_ffn", "b_conv") else (1, shape[-1])
        g2 = red[n].reshape(r2)
        dl, mn, vn = _adamw(w[n].reshape(r2), g2, m[n].reshape(r2), v[n].reshape(r2), name="adamw_" + n)
        g_out[n], d_out[n], m_out[n], v_out[n] = (a.reshape(shape) for a in (g2, dl, mn, vn))
    tok = reducer.to_core("in", first_element([v_out[n] for n in smalls]))
    (reduced["in"],) = reducer.finish("in", tok)
    update_big("w_in", (tok,))

    return (loss, grad_x[None], *[g_out[n] for n in WEIGHT_ORDER], *[d_out[n] for n in WEIGHT_ORDER],
            *[m_out[n] for n in WEIGHT_ORDER], *[v_out[n] for n in WEIGHT_ORDER])
```

```python
import math

import jax
import jax.numpy as jnp
import numpy as np
from jax import lax
from jax.experimental import pallas as pl
from jax.experimental.pallas import tpu as pltpu

F32 = jnp.float32
BF16 = jnp.bfloat16
HEAD_DIM = 128
N_HEADS = 8
EPS = 1e-6
NEG = -1e30
LOG2E = math.log2(math.e)
N_CHIPS = 4
N_DEV = 8
LANES = 128
VMEM_LIMIT_BYTES = 56 * 1024 * 1024
DIL_PATTERNS = ((128, 1), (512, 4), (2048, 16))
ATTN_TILE = 512
MM_TILE = 1024
ADAM_LR, ADAM_B1, ADAM_B2, ADAM_EPS, ADAM_WD, ADAM_STEP = 0.001, 0.9, 0.999, 1e-08, 0.01, 10
MESH = pl.DeviceIdType.MESH


def _params(*sem):
    return pltpu.CompilerParams(dimension_semantics=sem, vmem_limit_bytes=VMEM_LIMIT_BYTES)


def _round_up(n, m):
    return -(-n // m) * m


def _pick(dim, prefs):
    for p in prefs:
        if dim % p == 0:
            return p
    raise ValueError(f"no tile for {dim} in {prefs}")


def _logical_shape(arr, kind):
    if kind is None:
        return arr.shape
    s, r, c = arr.shape
    return (r, s * c) if kind == "col" else (s * r, c)


def _spec(shape, kind, br, bc, fi, fj):
    if kind is None:
        return pl.BlockSpec((br, bc), lambda *g: (fi(*g), fj(*g)))
    _, r, c = shape
    if kind == "col":
        nb = c // bc
        assert nb * bc == c, (shape, bc)
        return pl.BlockSpec((None, br, bc), lambda *g: (fj(*g) // nb, fi(*g), fj(*g) % nb))
    nb = r // br
    assert nb * br == r, (shape, br)
    return pl.BlockSpec((None, br, bc), lambda *g: (fi(*g) // nb, fi(*g) % nb, fj(*g)))


def _mm(a, b, *, mode, tm, tn, tk, name, a_kind=None, b_kind=None, out_kind=None,
        out_dtype=F32, res=None, deps=(), loss_target=None):
    pair_a, pair_b = isinstance(a, tuple), isinstance(b, tuple)
    if pair_a or pair_b:
        return _mm_pair(a, b, mode=mode, tm=tm, tn=tn, tk=tk, name=name, b_kind=b_kind, out_kind=out_kind,
                        out_dtype=out_dtype, deps=deps)
    la, lb = _logical_shape(a, a_kind), _logical_shape(b, b_kind)
    if mode == "nn":
        (m, k), (k2, n) = la, lb
    elif mode == "nt":
        (m, k), (n, k2) = la, lb
    else:
        (k, m), (k2, n) = la, lb
    assert k == k2, (name, la, lb)
    assert m % tm == 0 and n % tn == 0 and k % tk == 0, (name, m, n, k, tm, tn, tk)
    nk = k // tk
    im = lambda i, j, l: i
    jn = lambda i, j, l: j
    lk = lambda i, j, l: l
    if mode == "tn":
        a_spec = _spec(a.shape, a_kind, tk, tm, lk, im)
        dims = (((0,), (0,)), ((), ()))
    else:
        a_spec = _spec(a.shape, a_kind, tm, tk, im, lk)
        dims = (((1,), (1,)), ((), ())) if mode == "nt" else (((1,), (0,)), ((), ()))
    if mode == "nt":
        b_spec = _spec(b.shape, b_kind, tn, tk, jn, lk)
    else:
        b_spec = _spec(b.shape, b_kind, tk, tn, lk, jn)
    if out_kind is None:
        oshape = (m, n)
    elif out_kind == "col":
        oshape = (N_CHIPS, m, n // N_CHIPS)
    else:
        oshape = (N_CHIPS, m // N_CHIPS, n)
    o_spec = _spec(oshape, out_kind, tm, tn, im, jn)
    tile = pl.BlockSpec((tm, tn), lambda i, j, l: (i, j))
    in_specs = [a_spec, b_spec]
    args = [a, b]
    for extra in (res, loss_target):
        if extra is not None:
            in_specs.append(tile)
            args.append(extra)
    in_specs += [pl.BlockSpec(memory_space=pl.ANY)] * len(deps)
    args += list(deps)
    if loss_target is None:
        out_specs, out_shape = [o_spec], [jax.ShapeDtypeStruct(oshape, out_dtype)]
    else:
        assert out_kind is None and res is not None
        out_specs = [tile, tile, pl.BlockSpec((8, LANES), lambda i, j, l: (0, 0))]
        out_shape = [jax.ShapeDtypeStruct(oshape, F32), jax.ShapeDtypeStruct(oshape, BF16),
                     jax.ShapeDtypeStruct((8, LANES), F32)]
    n_in, n_out = len(args), len(out_specs)

    def finish(out, refs, first):
        res_ref = refs[2] if res is not None else None
        outs = refs[n_in:n_in + n_out]
        if res_ref is not None:
            out = out + res_ref[...]
        if loss_target is None:
            outs[0][...] = out.astype(outs[0].dtype)
            return

        @pl.when(first)
        def _():
            outs[2][...] = jnp.zeros_like(outs[2])

        err = out - refs[3][...]
        dy = err * (1.0 / n)
        outs[0][...] = dy
        outs[1][...] = dy.astype(BF16)
        outs[2][...] += 0.5 * jnp.sum(jnp.sum(err * err, axis=-1, keepdims=True) * (1.0 / n), axis=0, keepdims=True)

    def first_tile():
        return (pl.program_id(0) == 0) & (pl.program_id(1) == 0)

    def body_whole_k(*refs):
        finish(lax.dot_general(refs[0][...], refs[1][...], dims, preferred_element_type=F32), refs, first_tile())

    def body(*refs):
        acc_ref = refs[-1]
        step = pl.program_id(2)
        first = first_tile()

        @pl.when(step == 0)
        def _():
            acc_ref[...] = jnp.zeros_like(acc_ref)

        acc_ref[...] += lax.dot_general(refs[0][...], refs[1][...], dims, preferred_element_type=F32)

        @pl.when(step == nk - 1)
        def _():
            finish(acc_ref[...], refs, first)

    outs = pl.pallas_call(
        body_whole_k if nk == 1 else body, name=name, grid=(m // tm, n // tn, nk),
        in_specs=in_specs, out_specs=out_specs, out_shape=out_shape,
        scratch_shapes=[] if nk == 1 else [pltpu.VMEM((tm, tn), F32)],
        compiler_params=_params(*(["arbitrary"] * 3 if loss_target is not None else ["parallel", "parallel", "arbitrary"])),
    )(*args)
    return outs[0] if loss_target is None else outs


def _mm_pair(a, b, *, mode, tm, tn, tk, name, b_kind, out_kind, out_dtype, deps):
    anyspec = [pl.BlockSpec(memory_space=pl.ANY)] * len(deps)
    if mode == "tn":
        assert isinstance(b, tuple) and out_kind == "col" and a.shape[0] == tk
        k, m = a.shape
        n0 = b[0].shape[1]
        n, nb0 = 2 * n0, n0 // tn
        oshape = (N_CHIPS, m, n // N_CHIPS)

        def body(a_ref, b0_ref, b1_ref, *rest):
            o_ref = rest[-1]
            for first, b_ref in ((True, b0_ref), (False, b1_ref)):
                @pl.when((pl.program_id(1) < nb0) == first)
                def _():
                    o_ref[...] = lax.dot_general(a_ref[...], b_ref[...], (((0,), (0,)), ((), ())),
                                                 preferred_element_type=F32).astype(o_ref.dtype)

        return pl.pallas_call(
            body, name=name, grid=(m // tm, n // tn),
            in_specs=[pl.BlockSpec((tk, tm), lambda i, j: (0, i)),
                      pl.BlockSpec((tk, tn), lambda i, j: (0, jnp.minimum(j, nb0 - 1))),
                      pl.BlockSpec((tk, tn), lambda i, j: (0, jnp.maximum(j - nb0, 0)))] + anyspec,
            out_specs=_spec(oshape, "col", tm, tn, lambda i, j: i, lambda i, j: j),
            out_shape=jax.ShapeDtypeStruct(oshape, out_dtype), compiler_params=_params("parallel", "arbitrary"),
        )(a, *b, *deps)
    assert mode == "nt" and isinstance(a, tuple) and out_kind is None
    m, k0 = a[0].shape
    n = _logical_shape(b, b_kind)[0]
    nk0 = k0 // tk
    nk = 2 * nk0

    def body(a0_ref, a1_ref, b_ref, *rest):
        o_ref, acc_ref = rest[-2], rest[-1]
        step = pl.program_id(2)

        @pl.when(step == 0)
        def _():
            acc_ref[...] = jnp.zeros_like(acc_ref)

        for first, a_ref in ((True, a0_ref), (False, a1_ref)):
            @pl.when((step < nk0) == first)
            def _():
                acc_ref[...] += lax.dot_general(a_ref[...], b_ref[...], (((1,), (1,)), ((), ())), preferred_element_type=F32)

        @pl.when(step == nk - 1)
        def _():
            o_ref[...] = acc_ref[...].astype(o_ref.dtype)

    return pl.pallas_call(
        body, name=name, grid=(m // tm, n // tn, nk),
        in_specs=[pl.BlockSpec((tm, tk), lambda i, j, l: (i, jnp.minimum(l, nk0 - 1))),
                  pl.BlockSpec((tm, tk), lambda i, j, l: (i, jnp.maximum(l - nk0, 0))),
                  _spec(b.shape, b_kind, tn, tk, lambda i, j, l: j, lambda i, j, l: l)] + anyspec,
        out_specs=pl.BlockSpec((tm, tn), lambda i, j, l: (i, j)),
        out_shape=jax.ShapeDtypeStruct((m, n), out_dtype), scratch_shapes=[pltpu.VMEM((tm, tn), F32)],
        compiler_params=_params("parallel", "parallel", "arbitrary"),
    )(*a, b, *deps)


def _pieces(splits, cs, cp):
    out, g0 = [], 0
    for width in splits:
        g1, runs = g0 + width, []
        for j in range(N_CHIPS):
            a, b = max(g0, cs * j), min(g1, cs * (j + 1))
            if a < b:
                runs.append((j * cp + a - cs * j, a - g0, b - a))
        out.append(runs)
        g0 = g1
    return out


def _head_norm(xv, gv):
    r = lax.rsqrt(jnp.mean(xv * xv, axis=-1, keepdims=True) + EPS)
    return (xv * r) * gv


def _head_norm_bwd(dyv, xv, gv):
    r = lax.rsqrt(jnp.mean(xv * xv, axis=-1, keepdims=True) + EPS)
    xr = xv * r
    gdy = dyv * gv
    return r * (gdy - xr * jnp.mean(gdy * xr, axis=-1, keepdims=True)), jnp.sum(dyv * xr, axis=0, keepdims=True)


def _proj_split(proj_p, splits, cs, dtypes, gains, tm=128):
    s, wp = proj_p.shape
    pieces = _pieces(splits, cs, wp // N_CHIPS)
    normed = sorted(gains)
    nseg = len(splits)

    def body(p_ref, *refs):
        g_refs, o_refs, n_refs = refs[:len(normed)], refs[len(normed):len(normed) + nseg], refs[len(normed) + nseg:]
        for o_ref, runs in zip(o_refs, pieces):
            for src, dst, n in runs:
                o_ref[:, dst:dst + n] = p_ref[:, src:src + n].astype(o_ref.dtype)
        for g_ref, n_ref, i in zip(g_refs, n_refs, normed):
            for c0 in range(0, splits[i], HEAD_DIM):
                cols = slice(c0, c0 + HEAD_DIM)
                n_ref[:, cols] = _head_norm(o_refs[i][:, cols], g_ref[:, cols]).astype(n_ref.dtype)

    return pl.pallas_call(
        body, name="proj_split", grid=(s // tm,),
        in_specs=[pl.BlockSpec((tm, wp), lambda i: (i, 0))] + [pl.BlockSpec((1, splits[i]), lambda i: (0, 0)) for i in normed],
        out_specs=[pl.BlockSpec((tm, w), lambda i: (i, 0)) for w in splits]
        + [pl.BlockSpec((tm, splits[i]), lambda i: (i, 0)) for i in normed],
        out_shape=[jax.ShapeDtypeStruct((s, w), dt) for w, dt in zip(splits, dtypes)]
        + [jax.ShapeDtypeStruct((s, splits[i]), BF16) for i in normed],
        compiler_params=_params("parallel"),
    )(proj_p, *[gains[i] for i in normed])


def _dproj_merge(parts, splits, cs, cp, norms, tm=128):
    s = parts[0].shape[0]
    wp = N_CHIPS * cp
    pieces = _pieces(splits, cs, cp)
    normed = sorted(norms)
    nseg, nn = len(splits), len(normed)

    def body(*refs):
        p_refs, x_refs, g_refs = refs[:nseg], refs[nseg:nseg + nn], refs[nseg + nn:nseg + 2 * nn]
        o_ref, dg_refs = refs[nseg + 2 * nn], refs[nseg + 2 * nn + 1:nseg + 3 * nn + 1]
        stage, tmp = refs[-2], refs[-1]

        @pl.when(pl.program_id(0) == 0)
        def _():
            for dg_ref in dg_refs:
                dg_ref[...] = jnp.zeros_like(dg_ref)

        for j in range(N_CHIPS):
            stage[:, j * cp + cs:(j + 1) * cp] = jnp.zeros((tm, cp - cs), F32)
        for i, (p_ref, runs) in enumerate(zip(p_refs, pieces)):
            src_ref = p_ref
            if i in norms:
                k = normed.index(i)
                for c0 in range(0, splits[i], HEAD_DIM):
                    cols = slice(c0, c0 + HEAD_DIM)
                    dx, dg = _head_norm_bwd(p_ref[:, cols].astype(F32), x_refs[k][:, cols], g_refs[k][:, cols])
                    tmp[:, cols] = dx
                    dg_refs[k][:, cols] += dg
                src_ref = tmp
            for dst, src, n in runs:
                stage[:, dst:dst + n] = src_ref[:, src:src + n].astype(F32)
        o_ref[...] = stage[...].astype(o_ref.dtype)

    wmax = max(splits[i] for i in normed)
    row = lambda w: pl.BlockSpec((tm, w), lambda i: (i, 0))
    vec = lambda w: pl.BlockSpec((1, w), lambda i: (0, 0))
    outs = pl.pallas_call(
        body, name="dproj_merge", grid=(s // tm,),
        in_specs=[row(w) for w in splits] + [row(splits[i]) for i in normed] + [vec(splits[i]) for i in normed],
        out_specs=[row(wp)] + [vec(splits[i]) for i in normed],
        out_shape=[jax.ShapeDtypeStruct((s, wp), BF16)] + [jax.ShapeDtypeStruct((1, splits[i]), F32) for i in normed],
        scratch_shapes=[pltpu.VMEM((tm, wp), F32), pltpu.VMEM((tm, wmax), F32)],
        compiler_params=_params("arbitrary"),
    )(*parts, *[norms[i][0] for i in normed], *[norms[i][1] for i in normed])
    return outs[0], dict(zip(normed, outs[1:]))


def _norm_fwd(x, g, *, group, name, tm=256):
    s, w = x.shape
    ng = w // group

    def body(x_ref, g_ref, o_ref):
        for i in range(ng):
            cols = slice(i * group, (i + 1) * group)
            xv = x_ref[:, cols]
            r = lax.rsqrt(jnp.mean(xv * xv, axis=-1, keepdims=True) + EPS)
            o_ref[:, cols] = ((xv * r) * g_ref[:, cols]).astype(o_ref.dtype)

    return pl.pallas_call(
        body, name=name, grid=(s // tm,),
        in_specs=[pl.BlockSpec((tm, w), lambda i: (i, 0)), pl.BlockSpec((1, w), lambda i: (0, 0))],
        out_specs=pl.BlockSpec((tm, w), lambda i: (i, 0)),
        out_shape=jax.ShapeDtypeStruct((s, w), BF16),
        compiler_params=_params("parallel"),
    )(x, g)


def _norm_bwd(dy, x, g, *, group, name, res=None, out_dtypes=(BF16,), tm=256, deps=()):
    s, w = x.shape
    ng = w // group
    n_in = 4 if res is not None else 3

    def body(*refs):
        dy_ref, x_ref, g_ref = refs[:3]
        res_ref = refs[3] if res is not None else None
        outs = refs[n_in + len(deps):]
        dx_refs, dg_ref = outs[:-1], outs[-1]

        @pl.when(pl.program_id(0) == 0)
        def _():
            dg_ref[...] = jnp.zeros_like(dg_ref)

        for i in range(ng):
            cols = slice(i * group, (i + 1) * group)
            xv = x_ref[:, cols]
            dyv = dy_ref[:, cols].astype(F32)
            r = lax.rsqrt(jnp.mean(xv * xv, axis=-1, keepdims=True) + EPS)
            xr = xv * r
            dg_ref[:, cols] += jnp.sum(dyv * xr, axis=0, keepdims=True)
            gdy = dyv * g_ref[:, cols]
            dx = r * (gdy - xr * jnp.mean(gdy * xr, axis=-1, keepdims=True))
            if res_ref is not None:
                dx = dx + res_ref[:, cols]
            for dx_ref in dx_refs:
                dx_ref[:, cols] = dx.astype(dx_ref.dtype)

    row = pl.BlockSpec((tm, w), lambda i: (i, 0))
    vec = pl.BlockSpec((1, w), lambda i: (0, 0))
    in_specs = [row, row, vec] + ([row] if res is not None else []) + [pl.BlockSpec(memory_space=pl.ANY)] * len(deps)
    args = [dy, x, g] + ([res] if res is not None else []) + list(deps)
    out_specs = [row] * len(out_dtypes) + [vec]
    out_shape = [jax.ShapeDtypeStruct((s, w), dt) for dt in out_dtypes] + [jax.ShapeDtypeStruct((1, w), F32)]
    return pl.pallas_call(
        body, name=name, grid=(s // tm,), in_specs=in_specs, out_specs=out_specs,
        out_shape=out_shape, compiler_params=_params("arbitrary"),
    )(*args)


def _split3(v):
    p1 = v.astype(BF16)
    r1 = v - p1.astype(F32)
    p2 = r1.astype(BF16)
    p3 = (r1 - p2.astype(F32)).astype(BF16)
    return p1, p2, p3


def _tri_sum(v, reverse, tcol=512):
    h, s = v.shape
    tcol = min(tcol, s)
    parts = _split3(v)
    outs = []
    for j in range(s // tcol):
        src = lax.broadcasted_iota(jnp.int32, (s, tcol), 0)
        dst = lax.broadcasted_iota(jnp.int32, (s, tcol), 1) + j * tcol
        keep = (src >= dst) if reverse else (src <= dst)
        tri = jnp.where(keep, 1.0, 0.0).astype(BF16)
        acc = jnp.zeros((h, tcol), F32)
        for p in parts:
            acc = acc + jnp.dot(p, tri, preferred_element_type=F32)
        outs.append(acc)
    return outs


def _forget_fwd(fa_t, b):
    h, s = fa_t.shape
    tcol = min(512, s)

    def body(f_ref, b_ref, c_ref):
        z = f_ref[...] + b_ref[...]
        logf = jnp.minimum(z, 0.0) - jnp.log(1.0 + jnp.exp(-jnp.abs(z)))
        for j, blk in enumerate(_tri_sum(logf, reverse=False, tcol=tcol)):
            c_ref[:, j * tcol:(j + 1) * tcol] = blk

    return pl.pallas_call(
        body, name="forget_fwd", out_shape=jax.ShapeDtypeStruct((h, s), F32),
        compiler_params=_params(),
    )(fa_t, b)


def _forget_bwd(dacol, fa_t, b):
    h, s = fa_t.shape
    tcol = min(512, s)

    def body(d_ref, f_ref, b_ref, dfa_ref, db_ref):
        z = f_ref[...] + b_ref[...]
        dc = -d_ref[...]
        total = jnp.zeros((h, 1), F32)
        for j, blk in enumerate(_tri_sum(dc, reverse=True, tcol=tcol)):
            cols = slice(j * tcol, (j + 1) * tcol)
            dfa = blk * (1.0 - jax.nn.sigmoid(z[:, cols]))
            dfa_ref[:, cols] = dfa
            total = total + jnp.sum(dfa, axis=-1, keepdims=True)
        db_ref[...] = total

    return pl.pallas_call(
        body, name="forget_bwd",
        out_shape=[jax.ShapeDtypeStruct((h, s), F32), jax.ShapeDtypeStruct((h, 1), F32)],
        compiler_params=_params(),
    )(dacol, fa_t, b)


def _distance_bias(s, tile, dilated):
    nb = s // tile
    b = lax.broadcasted_iota(jnp.int32, (nb, tile, tile), 0)
    dist = b * tile + lax.broadcasted_iota(jnp.int32, (nb, tile, tile), 1) - lax.broadcasted_iota(jnp.int32, (nb, tile, tile), 2)
    if not dilated:
        return jnp.where(dist >= 0, 0.0, NEG).astype(F32)
    mult = jnp.zeros(dist.shape, jnp.int32)
    for window, dil in DIL_PATTERNS:
        mult = mult + ((dist >= 0) & (dist <= window) & ((dist & (dil - 1)) == 0)).astype(jnp.int32)
    logm = jnp.where(mult == 3, math.log2(3.0), jnp.where(mult == 2, 1.0, 0.0))
    return jnp.where(mult > 0, logm, NEG).astype(F32)


def _logits(q, k, arow, acol, bias):
    s = lax.dot_general(q, k, (((1,), (1,)), ((), ())), preferred_element_type=F32)
    return s * (LOG2E / math.sqrt(HEAD_DIM)) + arow - acol + bias


def _attn_fwd(q, k, v, arow, acol, *, dilated, name, tq=ATTN_TILE, tk=ATTN_TILE):
    two_term = not dilated
    s, w = q.shape
    nh = w // HEAD_DIM
    assert tq == tk
    tq = tk = min(tq, s)
    nq, nk = s // tq, s // tk

    def body(q_ref, k_ref, v_ref, ar_ref, ac_ref, b_ref, o_ref, of_ref, lse_ref, m_ref, l_ref, acc_ref):
        qi, ki = pl.program_id(1), pl.program_id(2)

        @pl.when(ki == 0)
        def _():
            m_ref[...] = jnp.full_like(m_ref, NEG)
            l_ref[...] = jnp.zeros_like(l_ref)
            acc_ref[...] = jnp.zeros_like(acc_ref)

        @pl.when(ki <= qi)
        def _():
            sc = _logits(q_ref[...], k_ref[...], ar_ref[...], ac_ref[...], b_ref[...])
            m_new = jnp.maximum(m_ref[...], jnp.max(sc, axis=-1, keepdims=True))
            alpha = jnp.exp2(m_ref[...] - m_new)
            p = jnp.exp2(sc - m_new)
            l_ref[...] = alpha * l_ref[...] + jnp.sum(p, axis=-1, keepdims=True)
            p_hi = p.astype(BF16)
            vv = v_ref[...]
            pv = jnp.dot(p_hi, vv, preferred_element_type=F32)
            if two_term:
                pv = pv + jnp.dot((p - p_hi.astype(F32)).astype(BF16), vv, preferred_element_type=F32)
            acc_ref[...] = alpha * acc_ref[...] + pv
            m_ref[...] = m_new

        @pl.when(ki == nk - 1)
        def _():
            out = acc_ref[...] / l_ref[...]
            o_ref[...] = out.astype(o_ref.dtype)
            of_ref[...] = out
            lse_ref[...] = m_ref[...] + jnp.log2(l_ref[...])

    kv = pl.BlockSpec((tk, HEAD_DIM), lambda h, i, j: (jnp.minimum(j, i), h))
    return pl.pallas_call(
        body, name=name, grid=(nh, nq, nk),
        in_specs=[pl.BlockSpec((tq, HEAD_DIM), lambda h, i, j: (i, h)), kv, kv,
                  pl.BlockSpec((None, tq, 1), lambda h, i, j: (h, i, 0)),
                  pl.BlockSpec((None, 1, tk), lambda h, i, j: (h, 0, jnp.minimum(j, i))),
                  pl.BlockSpec((None, tq, tk), lambda h, i, j: (jnp.maximum(i - j, 0), 0, 0))],
        out_specs=[pl.BlockSpec((tq, HEAD_DIM), lambda h, i, j: (i, h)),
                   pl.BlockSpec((tq, HEAD_DIM), lambda h, i, j: (i, h)),
                   pl.BlockSpec((None, tq, 1), lambda h, i, j: (h, i, 0))],
        out_shape=[jax.ShapeDtypeStruct((s, w), BF16), jax.ShapeDtypeStruct((s, w), F32),
                   jax.ShapeDtypeStruct((nh, s, 1), F32)],
        scratch_shapes=[pltpu.VMEM((tq, 1), F32), pltpu.VMEM((tq, 1), F32), pltpu.VMEM((tq, HEAD_DIM), F32)],
        compiler_params=_params("parallel", "parallel", "arbitrary"),
    )(q, k, v, arow * LOG2E, acol * LOG2E, _distance_bias(s, tq, dilated))


def _attn_bwd(q, k, v, o, do, lse, arow, acol, *, dilated, name, tq=ATTN_TILE, tk=ATTN_TILE):
    s, w = q.shape
    nh = w // HEAD_DIM
    assert tq == tk
    tq = tk = min(tq, s)
    nq, nk = s // tq, s // tk
    scale = 1.0 / math.sqrt(HEAD_DIM)

    def body(q_ref, k_ref, v_ref, o_ref, do_ref, lse_ref, ar_ref, ac_ref, b_ref,
             dq_ref, dk_ref, dv_ref, dac_ref, dk_acc, dv_acc, dac_acc):
        ki, qi = pl.program_id(1), pl.program_id(2)

        @pl.when((ki == 0) & (qi == 0))
        def _():
            dq_ref[...] = jnp.zeros_like(dq_ref)

        @pl.when(qi == 0)
        def _():
            dk_acc[...] = jnp.zeros_like(dk_acc)
            dv_acc[...] = jnp.zeros_like(dv_acc)
            dac_acc[...] = jnp.zeros_like(dac_acc)

        @pl.when(qi >= ki)
        def _():
            qv, kvv, dov = q_ref[...], k_ref[...], do_ref[...]
            sc = _logits(qv, kvv, ar_ref[...], ac_ref[...], b_ref[...])
            p = jnp.exp2(sc - lse_ref[...])
            dp = lax.dot_general(dov, v_ref[...], (((1,), (1,)), ((), ())), preferred_element_type=F32)
            delta = jnp.sum(dov.astype(F32) * o_ref[...].astype(F32), axis=-1, keepdims=True)
            ds = p * (dp - delta)
            dsb = ds.astype(BF16)
            dv_acc[...] += lax.dot_general(p.astype(BF16), dov, (((0,), (0,)), ((), ())), preferred_element_type=F32)
            dk_acc[...] += lax.dot_general(dsb, qv, (((0,), (0,)), ((), ())), preferred_element_type=F32)
            rows = pl.ds(pl.multiple_of(qi * tq, tq), tq)
            dq_ref[rows, :] += jnp.dot(dsb, kvv, preferred_element_type=F32) * scale
            dac_acc[...] += jnp.sum(ds, axis=0, keepdims=True)

        @pl.when(qi == nq - 1)
        def _():
            dk_ref[...] = dk_acc[...] * scale
            dv_ref[...] = dv_acc[...]
            dac_ref[...] = dac_acc[...]

    qs = pl.BlockSpec((tq, HEAD_DIM), lambda h, j, i: (jnp.maximum(i, j), h))
    ks = pl.BlockSpec((tk, HEAD_DIM), lambda h, j, i: (j, h))
    rowv = pl.BlockSpec((None, tq, 1), lambda h, j, i: (h, jnp.maximum(i, j), 0))
    colv = pl.BlockSpec((None, 1, tk), lambda h, j, i: (h, 0, j))
    return pl.pallas_call(
        body, name=name, grid=(nh, nk, nq),
        in_specs=[qs, ks, ks, qs, qs, rowv, rowv, colv,
                  pl.BlockSpec((None, tq, tk), lambda h, j, i: (jnp.maximum(i - j, 0), 0, 0))],
        out_specs=[pl.BlockSpec((s, HEAD_DIM), lambda h, j, i: (0, h)), ks, ks, colv],
        out_shape=[jax.ShapeDtypeStruct((s, w), F32), jax.ShapeDtypeStruct((s, w), F32),
                   jax.ShapeDtypeStruct((s, w), F32), jax.ShapeDtypeStruct((nh, 1, s), F32)],
        scratch_shapes=[pltpu.VMEM((tk, HEAD_DIM), F32), pltpu.VMEM((tk, HEAD_DIM), F32), pltpu.VMEM((1, tk), F32)],
        compiler_params=_params("arbitrary", "arbitrary", "arbitrary"),
    )(q, k, v, o, do, lse, arow * LOG2E, acol * LOG2E, _distance_bias(s, tq, dilated))


def _gate_fwd(ga, gb, pa, pb, tm=256):
    s, d = ga.shape

    def body(ga_ref, gb_ref, pa_ref, pb_ref, o_ref):
        o_ref[...] = (jax.nn.sigmoid(ga_ref[...]) * pa_ref[...]
                      + jax.nn.sigmoid(gb_ref[...]) * pb_ref[...]).astype(o_ref.dtype)

    row = pl.BlockSpec((tm, d), lambda i: (i, 0))
    return pl.pallas_call(
        body, name="gate_fwd", grid=(s // tm,), in_specs=[row] * 4, out_specs=row,
        out_shape=jax.ShapeDtypeStruct((s, d), BF16), compiler_params=_params("parallel"),
    )(ga, gb, pa, pb)


def _gate_bwd(dm, ga, gb, pa, pb, tm=256):
    s, d = ga.shape

    def body(dm_ref, ga_ref, gb_ref, pa_ref, pb_ref, dpa_ref, dpb_ref, dga_ref, dgb_ref):
        dmv = dm_ref[...]
        for g_ref, p_ref, dp_ref, dg_ref in ((ga_ref, pa_ref, dpa_ref, dga_ref), (gb_ref, pb_ref, dpb_ref, dgb_ref)):
            sg = jax.nn.sigmoid(g_ref[...])
            dp_ref[...] = (dmv * sg).astype(BF16)
            dg_ref[...] = (dmv * p_ref[...] * (sg * (1.0 - sg))).astype(BF16)

    row = pl.BlockSpec((tm, d), lambda i: (i, 0))
    return pl.pallas_call(
        body, name="gate_bwd", grid=(s // tm,), in_specs=[row] * 5, out_specs=[row] * 4,
        out_shape=[jax.ShapeDtypeStruct((s, d), BF16)] * 4, compiler_params=_params("parallel"),
    )(dm, ga, gb, pa, pb)


def _shift_down(u, k):
    row = lax.broadcasted_iota(jnp.int32, u.shape, 0)
    return jnp.where(row >= k, pltpu.roll(u, k, 0), 0.0)


def _shift_up(u, k):
    n = u.shape[0]
    row = lax.broadcasted_iota(jnp.int32, u.shape, 0)
    return jnp.where(row < n - k, pltpu.roll(u, n - k, 0), 0.0)


def _conv3(u, wc, b):
    return wc[0:1, :] * _shift_down(u, 2) + wc[1:2, :] * _shift_down(u, 1) + wc[2:3, :] * u + b


def _conv_glu_fwd(u, wc, b, tn=256):
    s, f2 = u.shape
    f = f2 // 2
    nb = f // tn

    def body(ug_ref, uv_ref, wg_ref, wv_ref, bg_ref, bv_ref, o_ref):
        cg = _conv3(ug_ref[...], wg_ref[...], bg_ref[...])
        cv = _conv3(uv_ref[...], wv_ref[...], bv_ref[...])
        o_ref[...] = (cg * jax.nn.sigmoid(cg) * cv).astype(o_ref.dtype)

    def cols(rows, off):
        return pl.BlockSpec((rows, tn), lambda j: (0, j + off))

    return pl.pallas_call(
        body, name="conv_glu_fwd", grid=(nb,),
        in_specs=[cols(s, 0), cols(s, nb), cols(3, 0), cols(3, nb), cols(1, 0), cols(1, nb)],
        out_specs=cols(s, 0), out_shape=jax.ShapeDtypeStruct((s, f), BF16),
        compiler_params=_params("parallel"),
    )(u, u, wc, wc, b, b)


def _conv_glu_bwd(u, da, wc, b, tn=256):
    s, f2 = u.shape
    f = f2 // 2
    nb = f // tn

    def body(ug_ref, uv_ref, da_ref, wg_ref, wv_ref, bg_ref, bv_ref, dug_ref, duv_ref, sg_ref, sv_ref):
        ug, uv, wg, wv = ug_ref[...], uv_ref[...], wg_ref[...], wv_ref[...]
        cg = _conv3(ug, wg, bg_ref[...])
        cv = _conv3(uv, wv, bv_ref[...])
        sig = jax.nn.sigmoid(cg)
        dav = da_ref[...]
        dcv = dav * (cg * sig)
        dcg = dav * cv * (sig * (1.0 + cg * (1.0 - sig)))
        for dc, uu, w, du_ref, st_ref in ((dcg, ug, wg, dug_ref, sg_ref), (dcv, uv, wv, duv_ref, sv_ref)):
            du = w[2:3, :] * dc + w[1:2, :] * _shift_up(dc, 1) + w[0:1, :] * _shift_up(dc, 2)
            du_ref[...] = du.astype(BF16)
            st_ref[...] = jnp.zeros_like(st_ref)
            st_ref[0:1, :] = jnp.sum(dc * _shift_down(uu, 2), axis=0, keepdims=True)
            st_ref[1:2, :] = jnp.sum(dc * _shift_down(uu, 1), axis=0, keepdims=True)
            st_ref[2:3, :] = jnp.sum(dc * uu, axis=0, keepdims=True)
            st_ref[3:4, :] = jnp.sum(dc, axis=0, keepdims=True)

    def cols(rows, off):
        return pl.BlockSpec((rows, tn), lambda j: (0, j + off))

    return pl.pallas_call(
        body, name="conv_glu_bwd", grid=(nb,),
        in_specs=[cols(s, 0), cols(s, nb), cols(s, 0), cols(3, 0), cols(3, nb), cols(1, 0), cols(1, nb)],
        out_specs=[cols(s, 0), cols(s, 0), cols(8, 0), cols(8, 0)],
        out_shape=[jax.ShapeDtypeStruct((s, f), BF16), jax.ShapeDtypeStruct((s, f), BF16),
                   jax.ShapeDtypeStruct((8, f), F32), jax.ShapeDtypeStruct((8, f), F32)],
        compiler_params=_params("parallel"),
    )(u, u, da, wc, wc, b, b)


ROW_TILES = (256, 128, 64, 32, 16, 8)
BLOCK_BYTES = 2 << 20


def _add_halves(g, r1, place):
    ns, r, c = g.shape
    rh = r // 2
    tr = _pick(rh, ROW_TILES)
    g4 = g.reshape(ns, 2, rh, c)

    def body(p_ref, g_ref, r_ref, o_ref):
        o_ref[...] = (g_ref[...].astype(F32) + r_ref[...].astype(F32)).astype(o_ref.dtype)

    def slab(s, pr):
        return s + (s >= pr[0]).astype(jnp.int32)

    return pl.pallas_call(
        body, name="add_halves",
        grid_spec=pltpu.PrefetchScalarGridSpec(
            num_scalar_prefetch=1, grid=(ns - 1, rh // tr),
            in_specs=[pl.BlockSpec((None, None, tr, c), lambda s, i, pr: (slab(s, pr), pr[1], i, 0)),
                      pl.BlockSpec((None, tr, c), lambda s, i, pr: (slab(s, pr), i, 0))],
            out_specs=pl.BlockSpec((None, tr, c), lambda s, i, pr: (slab(s, pr), i, 0))),
        out_shape=jax.ShapeDtypeStruct((ns, rh, c), BF16),
        compiler_params=_params("parallel", "parallel"),
    )(place, g4, r1)


def _sum_chips(g, r1, recv, place):
    ns, r, c = g.shape
    rh = r // 2
    tr = _pick(rh, ROW_TILES)
    g4 = g.reshape(ns, 2, rh, c)

    def body(p_ref, g_ref, r_ref, t0_ref, t1_ref, t2_ref, o_ref):
        own = g_ref[...].astype(F32) + r_ref[...].astype(F32)
        o_ref[...] = ((own + t0_ref[...].astype(F32)) + t1_ref[...].astype(F32)) + t2_ref[...].astype(F32)

    def peer(k):
        return pl.BlockSpec((None, tr, c), lambda i, pr: (k, i, 0))

    return pl.pallas_call(
        body, name="sum_chips",
        grid_spec=pltpu.PrefetchScalarGridSpec(
            num_scalar_prefetch=1, grid=(rh // tr,),
            in_specs=[pl.BlockSpec((None, None, tr, c), lambda i, pr: (pr[0], pr[1], i, 0)),
                      pl.BlockSpec((None, tr, c), lambda i, pr: (pr[0], i, 0)), peer(0), peer(1), peer(2)],
            out_specs=pl.BlockSpec((tr, c), lambda i, pr: (pr[1] * (rh // tr) + i, 0))),
        out_shape=jax.ShapeDtypeStruct((r, c), F32),
        compiler_params=_params("parallel"),
    )(place, g4, r1, recv, recv, recv)


def _sum_devices(packs):
    n, r, c = packs.shape

    def body(p_ref, o_ref):
        acc = p_ref[0]
        for d in range(1, n):
            acc = acc + p_ref[d]
        o_ref[...] = acc

    return pl.pallas_call(
        body, name="sum_devices", out_shape=jax.ShapeDtypeStruct((r, c), F32), compiler_params=_params(),
    )(packs)


def _adamw_update(wv, gv, mv, vv):
    c1 = 1.0 - ADAM_B1 ** ADAM_STEP
    c2 = 1.0 - ADAM_B2 ** ADAM_STEP
    mn = ADAM_B1 * mv + (1.0 - ADAM_B1) * gv
    vn = ADAM_B2 * vv + (1.0 - ADAM_B2) * (gv * gv)
    m_hat = mn / c1
    v_hat = vn / c2
    return -ADAM_LR * (m_hat / (jnp.sqrt(v_hat) + ADAM_EPS) + ADAM_WD * wv), mn, vn


def _adamw(w, g, m, v, name, deps=(), emit_grad=False):
    r, c = w.shape
    tr = _pick(r, [t for t in ROW_TILES if t * c * 4 <= BLOCK_BYTES]) if r >= 8 else r
    n_out = 4 if emit_grad else 3

    def body(w_ref, g_ref, m_ref, v_ref, *rest):
        outs = rest[-n_out:]
        gv = g_ref[:, :c]
        if emit_grad:
            outs[0][...] = gv
        outs[-3][...], outs[-2][...], outs[-1][...] = _adamw_update(w_ref[...], gv, m_ref[...], v_ref[...])

    blk = pl.BlockSpec((tr, c), lambda i: (i, 0))
    g_blk = pl.BlockSpec((tr, g.shape[1]), lambda i: (i, 0))
    return pl.pallas_call(
        body, name=name, grid=(r // tr,), in_specs=[blk, g_blk, blk, blk] + [ANY] * len(deps), out_specs=[blk] * n_out,
        out_shape=[jax.ShapeDtypeStruct((r, c), F32)] * n_out, compiler_params=_params("parallel"),
    )(w, g, m, v, *deps)


ANY = pl.BlockSpec(memory_space=pl.ANY)


def _place():
    x, y, c = lax.axis_index("x"), lax.axis_index("y"), lax.axis_index("c")
    chips = [(1 - x, y), (x, 1 - y), (1 - x, 1 - y)]
    return x, y, c, chips


def _remote(src, dst, send_sem, recv_sem, to):
    return pltpu.make_async_remote_copy(src_ref=src, dst_ref=dst, send_sem=send_sem, recv_sem=recv_sem,
                                        device_id=to, device_id_type=MESH)


HBM = pl.BlockSpec(memory_space=pltpu.HBM)
SEM = pl.BlockSpec(memory_space=pltpu.SEMAPHORE)
EFFECT = pltpu.SideEffectType.DATAFLOW_SIDE_EFFECTING


def _in_hbm(a):
    return pltpu.with_memory_space_constraint(a, pltpu.HBM)


def _half(ref_rows, who):
    return pl.ds(who * (ref_rows // 2), ref_rows // 2)


def _gather_start(groups, name):
    items = [it for g in groups for it in g]
    n = len(items)
    sizes = [len(g) for g in groups]

    def body(*refs):
        srcs, lands = refs[:n], refs[n:2 * n]
        sems = refs[2 * n:2 * n + 2 * len(groups)]
        token = refs[-1]
        x, y, c, chips = _place()
        j = 2 * x + y
        at = 0
        for gi, g in enumerate(groups):
            send, recv = sems[2 * gi], sems[2 * gi + 1]
            for i, (shard, split) in enumerate(g):
                src, land = srcs[at], lands[at]
                at += 1
                rows = _half(shard.shape[0], c) if split else slice(None)
                for k, chip in enumerate(chips):
                    _remote(src.at[rows], land.at[j, rows], send.at[4 * i + k], recv.at[4 * i + k], (*chip, c)).start()
                _remote(src, land.at[j], send.at[4 * i + 3], recv.at[4 * i + 3], (x, y, 1 - c)).start()
        token[...] = jnp.zeros_like(token)

    sem_shapes = []
    for sz in sizes:
        sem_shapes += [pltpu.SemaphoreType.DMA((4 * sz,)), pltpu.SemaphoreType.DMA((4 * sz,))]
    out_shape = (sem_shapes + [pltpu.HBM(sh.shape, sh.dtype) for sh, _ in items]
                 + [pltpu.HBM((N_CHIPS,) + sh.shape, sh.dtype) for sh, _ in items]
                 + [jax.ShapeDtypeStruct((8, LANES), F32)])
    ns = len(sem_shapes)
    outs = pl.pallas_call(
        body, name=name, in_specs=[HBM] * (2 * n),
        out_specs=[SEM] * ns + [HBM] * (2 * n) + [pl.BlockSpec(memory_space=pltpu.VMEM)],
        out_shape=out_shape, input_output_aliases={i: ns + i for i in range(2 * n)},
        compiler_params=pltpu.CompilerParams(has_side_effects=EFFECT),
    )(*[_in_hbm(sh) for sh, _ in items], *[_in_hbm(lax.empty((N_CHIPS,) + sh.shape, sh.dtype)) for sh, _ in items])
    sems, shards, lands, token = outs[:ns], outs[ns:ns + n], outs[ns + n:ns + 2 * n], outs[-1]
    res, at = [], 0
    for gi, sz in enumerate(sizes):
        res.append((shards[at:at + sz], lands[at:at + sz], sems[2 * gi], sems[2 * gi + 1]))
        at += sz
    return res, token


def _gather_pass(group, started, after, name):
    shards, lands, send, recv = started
    n = len(group)
    split_ix = [i for i, (_, split) in enumerate(group) if split]

    def body(*refs):
        lnds, send1, recv1 = refs[n:2 * n], refs[2 * n], refs[2 * n + 1]
        outs = refs[2 * n + 2 + len(after):]
        send2, recv2, token = outs[2 * n], outs[2 * n + 1], outs[2 * n + 2]
        x, y, c, chips = _place()
        sib = (x, y, 1 - c)
        for i, (shard, split) in enumerate(group):
            rows = _half(shard.shape[0], c) if split else slice(None)
            for k, (cx, cy) in enumerate(chips):
                landed = lnds[i].at[2 * cx + cy, rows]
                cp = _remote(landed, landed, send1.at[4 * i + k], recv1.at[4 * i + k], sib)
                cp.wait_send()
                cp.wait_recv()
            own = lnds[i].at[2 * x + y]
            cp = _remote(own, own, send1.at[4 * i + 3], recv1.at[4 * i + 3], sib)
            cp.wait_send()
            cp.wait_recv()
        for i2, i in enumerate(split_ix):
            rows = _half(group[i][0].shape[0], c)
            for k, (cx, cy) in enumerate(chips):
                landed = lnds[i].at[2 * cx + cy, rows]
                _remote(landed, landed, send2.at[3 * i2 + k], recv2.at[3 * i2 + k], sib).start()
        token[...] = jnp.zeros_like(token)

    n2 = len(split_ix)
    out_shape = ([pltpu.HBM(a.shape, a.dtype) for a in (*shards, *lands)]
                 + [pltpu.SemaphoreType.DMA((3 * n2,)), pltpu.SemaphoreType.DMA((3 * n2,)), jax.ShapeDtypeStruct((8, LANES), F32)])
    outs = pl.pallas_call(
        body, name=name, in_specs=[HBM] * (2 * n) + [SEM, SEM] + [ANY] * len(after),
        out_specs=[HBM] * (2 * n) + [SEM, SEM, pl.BlockSpec(memory_space=pltpu.VMEM)],
        out_shape=out_shape, input_output_aliases={i: i for i in range(2 * n)},
        compiler_params=pltpu.CompilerParams(has_side_effects=EFFECT),
    )(*shards, *lands, send, recv, *after)
    return outs[:n], (outs[n:2 * n], outs[2 * n], outs[2 * n + 1]), outs[2 * n + 2]


def _gather_wait(group, passed, after, name):
    lands, send2, recv2 = passed
    n = len(group)
    split_ix = [i for i, (_, split) in enumerate(group) if split]

    def body(*refs):
        lnds, s2, r2 = refs[:n], refs[n], refs[n + 1]
        x, y, c, chips = _place()
        sib = (x, y, 1 - c)
        for i2, i in enumerate(split_ix):
            rows = _half(group[i][0].shape[0], 1 - c)
            for k, (cx, cy) in enumerate(chips):
                landed = lnds[i].at[2 * cx + cy, rows]
                cp = _remote(landed, landed, s2.at[3 * i2 + k], r2.at[3 * i2 + k], sib)
                cp.wait_send()
                cp.wait_recv()

    return pl.pallas_call(
        body, name=name, in_specs=[HBM] * n + [SEM, SEM, ANY], out_specs=[HBM] * n,
        out_shape=[pltpu.HBM(a.shape, a.dtype) for a in lands], input_output_aliases={i: i for i in range(n)},
        compiler_params=pltpu.CompilerParams(has_side_effects=EFFECT),
    )(*lands, send2, recv2, after)


def _xfer_start(name, srcs, land_shapes, n_copies, copies, after):
    n, nl = len(srcs), len(land_shapes)

    def body(*refs):
        src_refs, land_refs = refs[:n], refs[n:n + nl]
        send, recv, token = refs[n + nl + 1], refs[n + nl + 2], refs[-1]
        for cp in copies(src_refs, land_refs, send, recv):
            cp.start()
        token[...] = jnp.zeros_like(token)

    lands = [_in_hbm(lax.empty(shape, dtype)) for shape, dtype in land_shapes]
    out_shape = ([pltpu.SemaphoreType.DMA((n_copies,)), pltpu.SemaphoreType.DMA((n_copies,))]
                 + [pltpu.HBM(a.shape, a.dtype) for a in (*srcs, *lands)] + [jax.ShapeDtypeStruct((8, LANES), F32)])
    outs = pl.pallas_call(
        body, name=name, in_specs=[HBM] * (n + nl) + [ANY],
        out_specs=[SEM, SEM] + [HBM] * (n + nl) + [pl.BlockSpec(memory_space=pltpu.VMEM)],
        out_shape=out_shape, input_output_aliases={i: 2 + i for i in range(n + nl)},
        compiler_params=pltpu.CompilerParams(has_side_effects=EFFECT),
    )(*[_in_hbm(a) for a in srcs], *lands, after)
    return (outs[2:2 + n], outs[2 + n:2 + n + nl], outs[0], outs[1]), outs[-1]


def _xfer_wait(name, started, copies, after):
    srcs, lands, send, recv = started
    n, nl = len(srcs), len(lands)

    def body(*refs):
        src_refs, land_refs, s_ref, r_ref = refs[:n], refs[n:n + nl], refs[n + nl], refs[n + nl + 1]
        for cp in copies(src_refs, land_refs, s_ref, r_ref):
            cp.wait_send()
            cp.wait_recv()

    outs = pl.pallas_call(
        body, name=name, in_specs=[HBM] * (n + nl) + [SEM, SEM, ANY], out_specs=[HBM] * (n + nl),
        out_shape=[pltpu.HBM(a.shape, a.dtype) for a in (*srcs, *lands)],
        input_output_aliases={i: i for i in range(n + nl)},
        compiler_params=pltpu.CompilerParams(has_side_effects=EFFECT),
    )(*srcs, *lands, send, recv, after)
    return outs[:n], outs[n:]


def _swap_copies(srcs, lands, send, recv):
    x, y, c, _ = _place()
    return [_remote(src.at[:, _half(src.shape[1], 1 - c)], land, send.at[i], recv.at[i], (x, y, 1 - c))
            for i, (src, land) in enumerate(zip(srcs, lands))]


def _scatter_copies(srcs, lands, send, recv):
    x, y, c, chips = _place()
    return [_remote(src.at[2 * cx + cy], land.at[k], send.at[3 * i + k], recv.at[3 * i + k], (cx, cy, c))
            for i, (src, land) in enumerate(zip(srcs, lands)) for k, (cx, cy) in enumerate(chips)]


def _join_copies(srcs, lands, send, recv):
    x, y, c, _ = _place()
    return [_remote(src.at[_half(src.shape[0], c)], src.at[_half(src.shape[0], c)], send.at[i], recv.at[i], (x, y, 1 - c))
            for i, src in enumerate(srcs)]


def _corner(a):
    return a[(slice(0, 1),) * a.ndim]


class _Reducer:
    def __init__(self, place):
        self.place = place
        self.state = {}

    def swap(self, key, grads, after):
        shapes = [((g.shape[0], g.shape[1] // 2, g.shape[2]), g.dtype) for g in grads]
        self.state[key], token = _xfer_start("swap_start_" + key, grads, shapes, len(grads), _swap_copies, _corner(after))
        return token

    def to_chips(self, key, after):
        grads, from_sibling = _xfer_wait("swap_wait_" + key, self.state[key], _swap_copies, after)
        sums = [_add_halves(g, r, self.place) for g, r in zip(grads, from_sibling)]
        shapes = [((3,) + s.shape[1:], s.dtype) for s in sums]
        started, token = _xfer_start("scatter_start_" + key, sums, shapes, 3 * len(sums), _scatter_copies, _corner(sums[-1]))
        self.state[key] = (grads, from_sibling, started)
        return token

    def to_core(self, key, after):
        grads, from_sibling, started = self.state[key]
        _, from_chips = _xfer_wait("scatter_wait_" + key, started, _scatter_copies, after)
        shards = [_sum_chips(g, r, rc, self.place) for g, r, rc in zip(grads, from_sibling, from_chips)]
        self.state[key], token = _xfer_start("join_start_" + key, shards, [], len(shards), _join_copies, _corner(shards[-1]))
        return token

    def finish(self, key, after):
        return _xfer_wait("join_wait_" + key, self.state.pop(key), _join_copies, after)[0]


def _gather_packs(pack, deps=()):
    def body(p_ref, *rest):
        o_ref, lsem, ssem, rsem = rest[-4:]
        x, y, c, _ = _place()
        me = 4 * x + 2 * y + c
        local = pltpu.make_async_copy(p_ref, o_ref.at[me], lsem)
        local.start()
        cps = []
        for k in range(1, N_DEV):
            fx, fy, fc = (k >> 2) & 1, (k >> 1) & 1, k & 1
            to = (x ^ fx, y ^ fy, c ^ fc)
            cps.append(_remote(p_ref, o_ref.at[me], ssem.at[k - 1], rsem.at[k - 1], to))
        for cp in cps:
            cp.start()
        for k in range(1, N_DEV):
            fx, fy, fc = (k >> 2) & 1, (k >> 1) & 1, k & 1
            src = o_ref.at[4 * (x ^ fx) + 2 * (y ^ fy) + (c ^ fc)]
            _remote(src, src, ssem.at[k - 1], rsem.at[k - 1], (x, y, c)).wait_recv()
        for cp in cps:
            cp.wait_send()
        local.wait()

    return pl.pallas_call(
        body, name="gather_packs", in_specs=[ANY] * (1 + len(deps)), out_specs=ANY,
        out_shape=jax.ShapeDtypeStruct((N_DEV,) + pack.shape, pack.dtype),
        scratch_shapes=[pltpu.SemaphoreType.DMA, pltpu.SemaphoreType.DMA((N_DEV - 1,)), pltpu.SemaphoreType.DMA((N_DEV - 1,))],
    )(pack, *deps)


LANE_TILES = (512, 896, 1408, 704, 384, 256, 128)


def _layer_grads(x, target, small, wg, rest_pass, rest_wait, red, filler):
    s, d = x.shape
    f = wg["conv"].shape[1] // 2
    w_att = N_HEADS * HEAD_DIM
    in_splits = (w_att, w_att, w_att, N_HEADS, w_att, w_att, w_att, d, d)
    in_cols = sum(in_splits)
    cs = in_cols // N_CHIPS
    cp = wg["in"].shape[2]
    tm = min(s, MM_TILE)
    tm_wide = min(s, MM_TILE // 2)
    t_in = cp
    t_up = 2 * f // N_CHIPS
    t_d = _pick(d, LANE_TILES)
    t_d2 = min(d, MM_TILE)
    t_dq = _pick(d // N_CHIPS, LANE_TILES)
    t_fq = _pick(f // N_CHIPS, LANE_TILES)

    h1 = _norm_fwd(x, small["g_attn"], group=d, name="rms1_fwd")
    proj_p = _mm(h1, wg["in"], mode="nn", b_kind="col", tm=tm_wide, tn=t_in, tk=d, name="mm_in")
    gains = {n: small[n].reshape(1, w_att) for n in ("g_q_fox", "g_k_fox", "g_q_dil", "g_k_dil")}
    qa, ka, va_b, fa, qb, kb, vb_b, ga, gb, qa_n, ka_n, qb_n, kb_n = _proj_split(
        proj_p, in_splits, cs, (F32, F32, BF16, F32, F32, F32, BF16, F32, F32),
        {0: gains["g_q_fox"], 1: gains["g_k_fox"], 4: gains["g_q_dil"], 5: gains["g_k_dil"]})
    fa_t = fa.T
    b_f = small["b_forget"].reshape(N_HEADS, 1)
    c_f = _forget_fwd(fa_t, b_f)
    slopes = jnp.asarray(2.0 ** (-8.0 * np.arange(1, N_HEADS + 1) / N_HEADS), dtype=F32)
    a_d = -(slopes[:, None] * jnp.arange(s, dtype=F32)[None, :])
    rows_f, cols_f = c_f[:, :, None], c_f[:, None, :]
    rows_d, cols_d = a_d[:, :, None], a_d[:, None, :]
    o_a, o_a32, lse_a = _attn_fwd(qa_n, ka_n, va_b, rows_f, cols_f, dilated=False, name="attn_fox_fwd")
    token = rest_pass("mid", o_a)
    rows_d = rows_d + token[0, 0]
    o_b, o_b32, lse_b = _attn_fwd(qb_n, kb_n, vb_b, rows_d, cols_d, dilated=True, name="attn_dil_fwd")
    wg = dict(wg, **rest_wait("mid", o_b))
    token = rest_pass("late", o_b)
    pa = _mm(o_a, wg["brf"], mode="nn", b_kind="col", tm=tm, tn=t_dq, tk=w_att, name="mm_brf", deps=(token,))
    pb = _mm(o_b, wg["brd"], mode="nn", b_kind="col", tm=tm, tn=t_dq, tk=w_att, name="mm_brd")
    merged = _gate_fwd(ga, gb, pa, pb)
    x1 = _mm(merged, wg["out"], mode="nn", b_kind="row", res=x, tm=tm, tn=t_d, tk=t_dq, name="mm_out")
    wg = dict(wg, **rest_wait("late", x1))
    h2 = _norm_fwd(x1, small["g_ffn"], group=d, name="rms2_fwd")
    u = _mm(h2, wg["up"], mode="nn", b_kind="col", tm=tm_wide, tn=t_up, tk=d, name="mm_up")
    act = _conv_glu_fwd(u, wg["conv"], wg["bconv"])
    dy_f, dy_b, loss_blk = _mm(act, wg["down"], mode="nn", b_kind="row", res=x1, loss_target=target,
                               tm=tm, tn=t_d2, tk=t_fq, name="mm_down")

    d_act = _mm(dy_b, wg["down"], mode="nt", b_kind="row", tm=tm, tn=t_fq, tk=d, name="mm_down_dx")
    g_down = _mm(act, dy_b, mode="tn", out_dtype=BF16, out_kind="row", tm=t_fq, tn=t_d2, tk=s, name="mm_down_dw")
    tok = red.swap("down", [g_down], g_down)
    du_g, du_v, st_g, st_v = _conv_glu_bwd(u, d_act, wg["conv"] + tok[0, 0], wg["bconv"])
    tok = red.to_chips("down", du_g)
    du = (du_g, du_v)
    g_up = _mm(h2, du, mode="tn", out_dtype=BF16, out_kind="col", tm=t_d2, tn=t_up // 2, tk=s, name="mm_up_dw", deps=(tok,))
    tok = red.to_core("down", g_up)
    tok2 = red.swap("up", [g_up], g_up)
    dh2 = _mm(du, wg["up"], mode="nt", b_kind="col", tm=tm, tn=t_d2, tk=t_up, name="mm_up_dx", deps=(tok, tok2))
    tok = red.to_chips("up", dh2)
    dx1_b, dx1_f, dg_ffn = _norm_bwd(dh2, x1, small["g_ffn"], group=d, res=dy_f, out_dtypes=(BF16, F32), name="rms2_bwd")
    d_merged = _mm(dx1_b, wg["out"], mode="nt", b_kind="row", tm=tm, tn=t_dq, tk=d, name="mm_out_dx", deps=(tok,))
    g_out = _mm(merged, dx1_b, mode="tn", out_dtype=BF16, out_kind="row", tm=t_dq, tn=t_d2, tk=s, name="mm_out_dw")
    dpa, dpb, dga, dgb = _gate_bwd(d_merged, ga, gb, pa, pb)
    do_a = _mm(dpa, wg["brf"], mode="nt", b_kind="col", out_dtype=BF16, tm=s, tn=w_att, tk=t_dq, name="mm_brf_dx")
    do_b = _mm(dpb, wg["brd"], mode="nt", b_kind="col", out_dtype=BF16, tm=s, tn=w_att, tk=t_dq, name="mm_brd_dx")
    g_brf = _mm(o_a, dpa, mode="tn", out_dtype=BF16, out_kind="col", tm=w_att, tn=t_dq, tk=s, name="mm_brf_dw")
    g_brd = _mm(o_b, dpb, mode="tn", out_dtype=BF16, out_kind="col", tm=w_att, tn=t_dq, tk=s, name="mm_brd_dw")
    tok = red.swap("mix", [g_out, g_brf, g_brd], g_brd)
    dqa_n, dka_n, dva, dac_a = _attn_bwd(qa_n, ka_n, va_b, o_a32, do_a, lse_a, rows_f + tok[0, 0], cols_f, dilated=False, name="attn_fox_bwd")
    tok = red.to_core("up", dqa_n)
    tok2 = red.to_chips("mix", dqa_n)
    dqb_n, dkb_n, dvb, _ = _attn_bwd(qb_n, kb_n, vb_b, o_b32, do_b, lse_b, rows_d + (tok[0, 0] + tok2[0, 0]), cols_d, dilated=True, name="attn_dil_bwd")
    tok = red.to_core("mix", dqb_n)
    dfa_t, db_f = _forget_bwd(dac_a[:, 0, :], fa_t, b_f)
    dproj_p, dgains = _dproj_merge(
        [dqa_n, dka_n, dva, dfa_t.T, dqb_n, dkb_n, dvb, dga, dgb], in_splits, cs, cp,
        {0: (qa, gains["g_q_fox"]), 1: (ka, gains["g_k_fox"]), 4: (qb, gains["g_q_dil"]), 5: (kb, gains["g_k_dil"])})
    dg_qf, dg_kf, dg_qd, dg_kd = dgains[0], dgains[1], dgains[4], dgains[5]
    g_in = _mm(h1, dproj_p, mode="tn", out_dtype=BF16, out_kind="col", tm=t_d2, tn=t_in, tk=s, name="mm_in_dw", deps=(tok,))
    tok = red.swap("in", [g_in], g_in)
    tok = red.to_chips("in", filler(tok))
    dh1 = _mm(dproj_p, wg["in"], mode="nt", b_kind="col", tm=tm, tn=t_d2, tk=t_in, name="mm_in_dx", deps=(tok,))
    grad_x, dg_attn = _norm_bwd(dh1, x, small["g_attn"], group=d, res=dx1_f, out_dtypes=(F32,), name="rms1_bwd")

    small_grads = {
        "g_attn": dg_attn, "b_forget": db_f.reshape(1, N_HEADS),
        "g_q_fox": dg_qf, "g_k_fox": dg_kf, "g_q_dil": dg_qd, "g_k_dil": dg_kd, "g_ffn": dg_ffn,
        "w_conv": jnp.concatenate([st_g[0:3], st_v[0:3]], axis=1),
        "b_conv": jnp.concatenate([st_g[3:4], st_v[3:4]], axis=1),
        "loss": loss_blk[0:1, 0:1],
    }
    return small_grads, grad_x


SMALL_ORDER = ("g_attn", "b_forget", "g_q_fox", "g_k_fox", "g_q_dil", "g_k_dil", "g_ffn", "w_conv", "b_conv", "loss")
WEIGHT_ORDER = ("g_attn", "w_in", "b_forget", "g_q_fox", "g_k_fox", "g_q_dil", "g_k_dil", "w_br_fox", "w_br_dil",
                "w_out", "g_ffn", "w_up", "w_conv", "b_conv", "w_down")
BIG = {"w_in": "in", "w_br_fox": "brf", "w_br_dil": "brd", "w_out": "out", "w_up": "up", "w_down": "down"}


def kernel(x, g_attn, w_in, b_forget, g_q_fox, g_k_fox, g_q_dil, g_k_dil, w_br_fox, w_br_dil, w_out, g_ffn, w_up, w_conv, b_conv, w_down, loss_target, m_g_attn, m_w_in, m_b_forget, m_g_q_fox, m_g_k_fox, m_g_q_dil, m_g_k_dil, m_w_br_fox, m_w_br_dil, m_w_out, m_g_ffn, m_w_up, m_w_conv, m_b_conv, m_w_down, v_g_attn, v_w_in, v_b_forget, v_g_q_fox, v_g_k_fox, v_g_q_dil, v_g_k_dil, v_w_br_fox, v_w_br_dil, v_w_out, v_g_ffn, v_w_up, v_w_conv, v_b_conv, v_w_down):
    w = dict(g_attn=g_attn, w_in=w_in, b_forget=b_forget, g_q_fox=g_q_fox, g_k_fox=g_k_fox, g_q_dil=g_q_dil,
             g_k_dil=g_k_dil, w_br_fox=w_br_fox, w_br_dil=w_br_dil, w_out=w_out, g_ffn=g_ffn, w_up=w_up,
             w_conv=w_conv, b_conv=b_conv, w_down=w_down)
    m = dict(g_attn=m_g_attn, w_in=m_w_in, b_forget=m_b_forget, g_q_fox=m_g_q_fox, g_k_fox=m_g_k_fox,
             g_q_dil=m_g_q_dil, g_k_dil=m_g_k_dil, w_br_fox=m_w_br_fox, w_br_dil=m_w_br_dil, w_out=m_w_out,
             g_ffn=m_g_ffn, w_up=m_w_up, w_conv=m_w_conv, b_conv=m_b_conv, w_down=m_w_down)
    v = dict(g_attn=v_g_attn, w_in=v_w_in, b_forget=v_b_forget, g_q_fox=v_g_q_fox, g_k_fox=v_g_k_fox,
             g_q_dil=v_g_q_dil, g_k_dil=v_g_k_dil, w_br_fox=v_w_br_fox, w_br_dil=v_w_br_dil, w_out=v_w_out,
             g_ffn=v_g_ffn, w_up=v_w_up, w_conv=v_w_conv, b_conv=v_b_conv, w_down=v_w_down)
    xi, yi, ci = lax.axis_index("x"), lax.axis_index("y"), lax.axis_index("c")
    chip = (2 * xi + yi).astype(jnp.int32)

    cs = w_in.shape[2]
    cp = _round_up(cs, LANES)
    shards = {
        "in": jnp.pad(w_in[0].astype(BF16), ((0, 0), (0, cp - cs))),
        "brf": w_br_fox[0].astype(BF16), "brd": w_br_dil[0].astype(BF16), "out": w_out[0].astype(BF16),
        "up": w_up[0].astype(BF16), "down": w_down[0].astype(BF16),
    }
    conv_pad = jnp.pad(w_conv[0], ((0, 8 - w_conv.shape[1]), (0, 0)))
    first = [(shards["in"], True), (conv_pad, False)]
    later = {"mid": ("brf", "brd", "out"), "late": ("up", "down")}
    groups = {key: [(shards[n], True) for n in members] for key, members in later.items()}
    (started_first, *started_later), token = _gather_start([first, *groups.values()], "gather_start")
    started = dict(zip(later, started_later))
    token, w["w_in"], m["w_in"], v["w_in"] = lax.optimization_barrier((token, w["w_in"], m["w_in"], v["w_in"]))
    w2, m2, v2 = ({n: a[n].reshape(a[n].shape[-2], a[n].shape[-1]) for n in BIG} for a in (w, m, v))
    early = (token, w2["w_in"], m2["w_in"], v2["w_in"])
    own_first, passed_first, token = _gather_pass(first, started_first, early, "gather_pass_in")
    land_in, land_conv = _gather_wait(first, passed_first, token, "gather_wait_in")
    wg = {"in": land_in, "bconv": b_conv,
          "conv": jnp.transpose(land_conv[:, :w_conv.shape[1], :], (1, 0, 2)).reshape(w_conv.shape[1], -1)}
    small = {n: w[n] for n in ("g_attn", "b_forget", "g_q_fox", "g_k_fox", "g_q_dil", "g_k_dil", "g_ffn")}
    small = {n: (a[0] if a.ndim == 3 else a) for n, a in small.items()}
    in_flight = {}

    def rest_pass(key, after):
        own, passed, tok = _gather_pass(groups[key], started[key], (after,), "gather_pass_" + key)
        in_flight[key] = (own, passed)
        return tok

    def rest_wait(key, after):
        own, passed = in_flight.pop(key)
        lands = _gather_wait(groups[key], passed, after, "gather_wait_" + key)
        return dict(zip(later[key], lands))

    reducer = _Reducer(jnp.stack([chip, ci.astype(jnp.int32)]))
    g_out, d_out, m_out, v_out = {}, {}, {}, {}
    reduced = {}

    def first_element(arrays):
        return jnp.stack([a[(0,) * a.ndim] for a in arrays])

    def update_big(n, deps):
        g2, dl, mn, vn = _adamw(w2[n], reduced[BIG[n]], m2[n], v2[n], name="adamw_" + n, deps=deps, emit_grad=True)
        g_out[n], d_out[n], m_out[n], v_out[n] = (a.reshape(w[n].shape) for a in (g2, dl, mn, vn))

    def update_down(tok):
        (reduced["down"],) = reducer.finish("down", tok)
        update_big("w_down", (tok,))
        return v_out["w_down"]

    small_grads, grad_x = _layer_grads(x[0], loss_target[0], small, wg, rest_pass, rest_wait, reducer, update_down)

    for key, members in (("up", ("up",)), ("mix", ("out", "brf", "brd"))):
        reduced.update(zip(members, reducer.finish(key, grad_x)))
    others = ("w_up", "w_out", "w_br_fox", "w_br_dil")
    for n in others:
        update_big(n, (grad_x,))

    flat = jnp.concatenate([small_grads[n].reshape(-1) for n in SMALL_ORDER])
    rows = _round_up(flat.shape[0], 8 * LANES) // LANES
    pack = jnp.pad(flat, (0, rows * LANES - flat.shape[0])).reshape(rows, LANES)
    packs = _gather_packs(pack, deps=(first_element([v_out[n] for n in others]),))
    total = _sum_devices(packs).reshape(-1)
    red, at = {}, 0
    for n in SMALL_ORDER:
        size = small_grads[n].size
        red[n] = total[at:at + size].reshape(small_grads[n].shape)
        at += size
    loss = red["loss"].reshape(())
    c2 = w_conv.shape[2]
    red["w_conv"] = lax.dynamic_slice_in_dim(red["w_conv"], chip * c2, c2, axis=1)

    smalls = [n for n in WEIGHT_ORDER if n not in BIG]
    for n in smalls:
        shape = w[n].shape
        r2 = (shape[-2], shape[-1]) if n not in ("g_attn", "b_forget", "g_ffn", "b_conv") else (1, shape[-1])
        g2 = red[n].reshape(r2)
        dl, mn, vn = _adamw(w[n].reshape(r2), g2, m[n].reshape(r2), v[n].reshape(r2), name="adamw_" + n)
        g_out[n], d_out[n], m_out[n], v_out[n] = (a.reshape(shape) for a in (g2, dl, mn, vn))
    tok = reducer.to_core("in", first_element([v_out[n] for n in smalls]))
    (reduced["in"],) = reducer.finish("in", tok)
    update_big("w_in", (tok,))

    return (loss, grad_x[None], *[g_out[n] for n in WEIGHT_ORDER], *[d_out[n] for n in WEIGHT_ORDER],
            *[m_out[n] for n in WEIGHT_ORDER], *[v_out[n] for n in WEIGHT_ORDER])
```

```python
import math

import jax
import jax.numpy as jnp
import numpy as np
from jax import lax
from jax.experimental import pallas as pl
from jax.experimental.pallas import tpu as pltpu

F32 = jnp.float32
BF16 = jnp.bfloat16
HEAD_DIM = 128
N_HEADS = 8
EPS = 1e-6
NEG = -1e30
LOG2E = math.log2(math.e)
N_CHIPS = 4
N_DEV = 8
LANES = 128
VMEM_LIMIT_BYTES = 56 * 1024 * 1024
DIL_PATTERNS = ((128, 1), (512, 4), (2048, 16))
ATTN_TILE = 512
MM_TILE = 1024
ADAM_LR, ADAM_B1, ADAM_B2, ADAM_EPS, ADAM_WD, ADAM_STEP = 0.001, 0.9, 0.999, 1e-08, 0.01, 10
MESH = pl.DeviceIdType.MESH


def _params(*sem):
    return pltpu.CompilerParams(dimension_semantics=sem, vmem_limit_bytes=VMEM_LIMIT_BYTES)


def _round_up(n, m):
    return -(-n // m) * m


def _pick(dim, prefs):
    for p in prefs:
        if dim % p == 0:
            return p
    raise ValueError(f"no tile for {dim} in {prefs}")


def _logical_shape(arr, kind):
    if kind is None:
        return arr.shape
    s, r, c = arr.shape
    return (r, s * c) if kind == "col" else (s * r, c)


def _spec(shape, kind, br, bc, fi, fj):
    if kind is None:
        return pl.BlockSpec((br, bc), lambda *g: (fi(*g), fj(*g)))
    _, r, c = shape
    if kind == "col":
        nb = c // bc
        assert nb * bc == c, (shape, bc)
        return pl.BlockSpec((None, br, bc), lambda *g: (fj(*g) // nb, fi(*g), fj(*g) % nb))
    nb = r // br
    assert nb * br == r, (shape, br)
    return pl.BlockSpec((None, br, bc), lambda *g: (fi(*g) // nb, fi(*g) % nb, fj(*g)))


def _mm(a, b, *, mode, tm, tn, tk, name, a_kind=None, b_kind=None, out_kind=None,
        out_dtype=F32, res=None, deps=(), loss_target=None):
    pair_a, pair_b = isinstance(a, tuple), isinstance(b, tuple)
    if pair_a or pair_b:
        return _mm_pair(a, b, mode=mode, tm=tm, tn=tn, tk=tk, name=name, b_kind=b_kind, out_kind=out_kind,
                        out_dtype=out_dtype, deps=deps)
    la, lb = _logical_shape(a, a_kind), _logical_shape(b, b_kind)
    if mode == "nn":
        (m, k), (k2, n) = la, lb
    elif mode == "nt":
        (m, k), (n, k2) = la, lb
    else:
        (k, m), (k2, n) = la, lb
    assert k == k2, (name, la, lb)
    assert m % tm == 0 and n % tn == 0 and k % tk == 0, (name, m, n, k, tm, tn, tk)
    nk = k // tk
    im = lambda i, j, l: i
    jn = lambda i, j, l: j
    lk = lambda i, j, l: l
    if mode == "tn":
        a_spec = _spec(a.shape, a_kind, tk, tm, lk, im)
        dims = (((0,), (0,)), ((), ()))
    else:
        a_spec = _spec(a.shape, a_kind, tm, tk, im, lk)
        dims = (((1,), (1,)), ((), ())) if mode == "nt" else (((1,), (0,)), ((), ()))
    if mode == "nt":
        b_spec = _spec(b.shape, b_kind, tn, tk, jn, lk)
    else:
        b_spec = _spec(b.shape, b_kind, tk, tn, lk, jn)
    if out_kind is None:
        oshape = (m, n)
    elif out_kind == "col":
        oshape = (N_CHIPS, m, n // N_CHIPS)
    else:
        oshape = (N_CHIPS, m // N_CHIPS, n)
    o_spec = _spec(oshape, out_kind, tm, tn, im, jn)
    tile = pl.BlockSpec((tm, tn), lambda i, j, l: (i, j))
    in_specs = [a_spec, b_spec]
    args = [a, b]
    for extra in (res, loss_target):
        if extra is not None:
            in_specs.append(tile)
            args.append(extra)
    in_specs += [pl.BlockSpec(memory_space=pl.ANY)] * len(deps)
    args += list(deps)
    if loss_target is None:
        out_specs, out_shape = [o_spec], [jax.ShapeDtypeStruct(oshape, out_dtype)]
    else:
        assert out_kind is None and res is not None
        out_specs = [tile, tile, pl.BlockSpec((8, LANES), lambda i, j, l: (0, 0))]
        out_shape = [jax.ShapeDtypeStruct(oshape, F32), jax.ShapeDtypeStruct(oshape, BF16),
                     jax.ShapeDtypeStruct((8, LANES), F32)]
    n_in, n_out = len(args), len(out_specs)

    def finish(out, refs, first):
        res_ref = refs[2] if res is not None else None
        outs = refs[n_in:n_in + n_out]
        if res_ref is not None:
            out = out + res_ref[...]
        if loss_target is None:
            outs[0][...] = out.astype(outs[0].dtype)
            return

        @pl.when(first)
        def _():
            outs[2][...] = jnp.zeros_like(outs[2])

        err = out - refs[3][...]
        dy = err * (1.0 / n)
        outs[0][...] = dy
        outs[1][...] = dy.astype(BF16)
        outs[2][...] += 0.5 * jnp.sum(jnp.sum(err * err, axis=-1, keepdims=True) * (1.0 / n), axis=0, keepdims=True)

    def first_tile():
        return (pl.program_id(0) == 0) & (pl.program_id(1) == 0)

    def body_whole_k(*refs):
        finish(lax.dot_general(refs[0][...], refs[1][...], dims, preferred_element_type=F32), refs, first_tile())

    def body(*refs):
        acc_ref = refs[-1]
        step = pl.program_id(2)
        first = first_tile()

        @pl.when(step == 0)
        def _():
            acc_ref[...] = jnp.zeros_like(acc_ref)

        acc_ref[...] += lax.dot_general(refs[0][...], refs[1][...], dims, preferred_element_type=F32)

        @pl.when(step == nk - 1)
        def _():
            finish(acc_ref[...], refs, first)

    outs = pl.pallas_call(
        body_whole_k if nk == 1 else body, name=name, grid=(m // tm, n // tn, nk),
        in_specs=in_specs, out_specs=out_specs, out_shape=out_shape,
        scratch_shapes=[] if nk == 1 else [pltpu.VMEM((tm, tn), F32)],
        compiler_params=_params(*(["arbitrary"] * 3 if loss_target is not None else ["parallel", "parallel", "arbitrary"])),
    )(*args)
    return outs[0] if loss_target is None else outs


def _mm_pair(a, b, *, mode, tm, tn, tk, name, b_kind, out_kind, out_dtype, deps):
    anyspec = [pl.BlockSpec(memory_space=pl.ANY)] * len(deps)
    if mode == "tn":
        assert isinstance(b, tuple) and out_kind == "col" and a.shape[0] == tk
        k, m = a.shape
        n0 = b[0].shape[1]
        n, nb0 = 2 * n0, n0 // tn
        oshape = (N_CHIPS, m, n // N_CHIPS)

        def body(a_ref, b0_ref, b1_ref, *rest):
            o_ref = rest[-1]
            for first, b_ref in ((True, b0_ref), (False, b1_ref)):
                @pl.when((pl.program_id(1) < nb0) == first)
                def _():
                    o_ref[...] = lax.dot_general(a_ref[...], b_ref[...], (((0,), (0,)), ((), ())),
                                                 preferred_element_type=F32).astype(o_ref.dtype)

        return pl.pallas_call(
            body, name=name, grid=(m // tm, n // tn),
            in_specs=[pl.BlockSpec((tk, tm), lambda i, j: (0, i)),
                      pl.BlockSpec((tk, tn), lambda i, j: (0, jnp.minimum(j, nb0 - 1))),
                      pl.BlockSpec((tk, tn), lambda i, j: (0, jnp.maximum(j - nb0, 0)))] + anyspec,
            out_specs=_spec(oshape, "col", tm, tn, lambda i, j: i, lambda i, j: j),
            out_shape=jax.ShapeDtypeStruct(oshape, out_dtype), compiler_params=_params("parallel", "arbitrary"),
        )(a, *b, *deps)
    assert mode == "nt" and isinstance(a, tuple) and out_kind is None
    m, k0 = a[0].shape
    n = _logical_shape(b, b_kind)[0]
    nk0 = k0 // tk
    nk = 2 * nk0

    def body(a0_ref, a1_ref, b_ref, *rest):
        o_ref, acc_ref = rest[-2], rest[-1]
        step = pl.program_id(2)

        @pl.when(step == 0)
        def _():
            acc_ref[...] = jnp.zeros_like(acc_ref)

        for first, a_ref in ((True, a0_ref), (False, a1_ref)):
            @pl.when((step < nk0) == first)
            def _():
                acc_ref[...] += lax.dot_general(a_ref[...], b_ref[...], (((1,), (1,)), ((), ())), preferred_element_type=F32)

        @pl.when(step == nk - 1)
        def _():
            o_ref[...] = acc_ref[...].astype(o_ref.dtype)

    return pl.pallas_call(
        body, name=name, grid=(m // tm, n // tn, nk),
        in_specs=[pl.BlockSpec((tm, tk), lambda i, j, l: (i, jnp.minimum(l, nk0 - 1))),
                  pl.BlockSpec((tm, tk), lambda i, j, l: (i, jnp.maximum(l - nk0, 0))),
                  _spec(b.shape, b_kind, tn, tk, lambda i, j, l: j, lambda i, j, l: l)] + anyspec,
        out_specs=pl.BlockSpec((tm, tn), lambda i, j, l: (i, j)),
        out_shape=jax.ShapeDtypeStruct((m, n), out_dtype), scratch_shapes=[pltpu.VMEM((tm, tn), F32)],
        compiler_params=_params("parallel", "parallel", "arbitrary"),
    )(*a, b, *deps)


def _pieces(splits, cs, cp):
    out, g0 = [], 0
    for width in splits:
        g1, runs = g0 + width, []
        for j in range(N_CHIPS):
            a, b = max(g0, cs * j), min(g1, cs * (j + 1))
            if a < b:
                runs.append((j * cp + a - cs * j, a - g0, b - a))
        out.append(runs)
        g0 = g1
    return out


def _head_norm(xv, gv):
    r = lax.rsqrt(jnp.mean(xv * xv, axis=-1, keepdims=True) + EPS)
    return (xv * r) * gv


def _head_norm_bwd(dyv, xv, gv):
    r = lax.rsqrt(jnp.mean(xv * xv, axis=-1, keepdims=True) + EPS)
    xr = xv * r
    gdy = dyv * gv
    return r * (gdy - xr * jnp.mean(gdy * xr, axis=-1, keepdims=True)), jnp.sum(dyv * xr, axis=0, keepdims=True)


def _proj_split(proj_p, splits, cs, dtypes, gains, tm=128):
    s, wp = proj_p.shape
    pieces = _pieces(splits, cs, wp // N_CHIPS)
    normed = sorted(gains)
    nseg = len(splits)

    def body(p_ref, *refs):
        g_refs, o_refs, n_refs = refs[:len(normed)], refs[len(normed):len(normed) + nseg], refs[len(normed) + nseg:]
        for o_ref, runs in zip(o_refs, pieces):
            for src, dst, n in runs:
                o_ref[:, dst:dst + n] = p_ref[:, src:src + n].astype(o_ref.dtype)
        for g_ref, n_ref, i in zip(g_refs, n_refs, normed):
            for c0 in range(0, splits[i], HEAD_DIM):
                cols = slice(c0, c0 + HEAD_DIM)
                n_ref[:, cols] = _head_norm(o_refs[i][:, cols], g_ref[:, cols]).astype(n_ref.dtype)

    return pl.pallas_call(
        body, name="proj_split", grid=(s // tm,),
        in_specs=[pl.BlockSpec((tm, wp), lambda i: (i, 0))] + [pl.BlockSpec((1, splits[i]), lambda i: (0, 0)) for i in normed],
        out_specs=[pl.BlockSpec((tm, w), lambda i: (i, 0)) for w in splits]
        + [pl.BlockSpec((tm, splits[i]), lambda i: (i, 0)) for i in normed],
        out_shape=[jax.ShapeDtypeStruct((s, w), dt) for w, dt in zip(splits, dtypes)]
        + [jax.ShapeDtypeStruct((s, splits[i]), BF16) for i in normed],
        compiler_params=_params("parallel"),
    )(proj_p, *[gains[i] for i in normed])


def _dproj_merge(parts, splits, cs, cp, norms, tm=128):
    s = parts[0].shape[0]
    wp = N_CHIPS * cp
    pieces = _pieces(splits, cs, cp)
    normed = sorted(norms)
    nseg, nn = len(splits), len(normed)

    def body(*refs):
        p_refs, x_refs, g_refs = refs[:nseg], refs[nseg:nseg + nn], refs[nseg + nn:nseg + 2 * nn]
        o_ref, dg_refs = refs[nseg + 2 * nn], refs[nseg + 2 * nn + 1:nseg + 3 * nn + 1]
        stage, tmp = refs[-2], refs[-1]

        @pl.when(pl.program_id(0) == 0)
        def _():
            for dg_ref in dg_refs:
                dg_ref[...] = jnp.zeros_like(dg_ref)

        for j in range(N_CHIPS):
            stage[:, j * cp + cs:(j + 1) * cp] = jnp.zeros((tm, cp - cs), F32)
        for i, (p_ref, runs) in enumerate(zip(p_refs, pieces)):
            src_ref = p_ref
            if i in norms:
                k = normed.index(i)
                for c0 in range(0, splits[i], HEAD_DIM):
                    cols = slice(c0, c0 + HEAD_DIM)
                    dx, dg = _head_norm_bwd(p_ref[:, cols].astype(F32), x_refs[k][:, cols], g_refs[k][:, cols])
                    tmp[:, cols] = dx
                    dg_refs[k][:, cols] += dg
                src_ref = tmp
            for dst, src, n in runs:
                stage[:, dst:dst + n] = src_ref[:, src:src + n].astype(F32)
        o_ref[...] = stage[...].astype(o_ref.dtype)

    wmax = max(splits[i] for i in normed)
    row = lambda w: pl.BlockSpec((tm, w), lambda i: (i, 0))
    vec = lambda w: pl.BlockSpec((1, w), lambda i: (0, 0))
    outs = pl.pallas_call(
        body, name="dproj_merge", grid=(s // tm,),
        in_specs=[row(w) for w in splits] + [row(splits[i]) for i in normed] + [vec(splits[i]) for i in normed],
        out_specs=[row(wp)] + [vec(splits[i]) for i in normed],
        out_shape=[jax.ShapeDtypeStruct((s, wp), BF16)] + [jax.ShapeDtypeStruct((1, splits[i]), F32) for i in normed],
        scratch_shapes=[pltpu.VMEM((tm, wp), F32), pltpu.VMEM((tm, wmax), F32)],
        compiler_params=_params("arbitrary"),
    )(*parts, *[norms[i][0] for i in normed], *[norms[i][1] for i in normed])
    return outs[0], dict(zip(normed, outs[1:]))


def _norm_fwd(x, g, *, group, name, tm=256):
    s, w = x.shape
    ng = w // group

    def body(x_ref, g_ref, o_ref):
        for i in range(ng):
            cols = slice(i * group, (i + 1) * group)
            xv = x_ref[:, cols]
            r = lax.rsqrt(jnp.mean(xv * xv, axis=-1, keepdims=True) + EPS)
            o_ref[:, cols] = ((xv * r) * g_ref[:, cols]).astype(o_ref.dtype)

    return pl.pallas_call(
        body, name=name, grid=(s // tm,),
        in_specs=[pl.BlockSpec((tm, w), lambda i: (i, 0)), pl.BlockSpec((1, w), lambda i: (0, 0))],
        out_specs=pl.BlockSpec((tm, w), lambda i: (i, 0)),
        out_shape=jax.ShapeDtypeStruct((s, w), BF16),
        compiler_params=_params("parallel"),
    )(x, g)


def _norm_bwd(dy, x, g, *, group, name, res=None, out_dtypes=(BF16,), tm=256, deps=()):
    s, w = x.shape
    ng = w // group
    n_in = 4 if res is not None else 3

    def body(*refs):
        dy_ref, x_ref, g_ref = refs[:3]
        res_ref = refs[3] if res is not None else None
        outs = refs[n_in + len(deps):]
        dx_refs, dg_ref = outs[:-1], outs[-1]

        @pl.when(pl.program_id(0) == 0)
        def _():
            dg_ref[...] = jnp.zeros_like(dg_ref)

        for i in range(ng):
            cols = slice(i * group, (i + 1) * group)
            xv = x_ref[:, cols]
            dyv = dy_ref[:, cols].astype(F32)
            r = lax.rsqrt(jnp.mean(xv * xv, axis=-1, keepdims=True) + EPS)
            xr = xv * r
            dg_ref[:, cols] += jnp.sum(dyv * xr, axis=0, keepdims=True)
            gdy = dyv * g_ref[:, cols]
            dx = r * (gdy - xr * jnp.mean(gdy * xr, axis=-1, keepdims=True))
            if res_ref is not None:
                dx = dx + res_ref[:, cols]
            for dx_ref in dx_refs:
                dx_ref[:, cols] = dx.astype(dx_ref.dtype)

    row = pl.BlockSpec((tm, w), lambda i: (i, 0))
    vec = pl.BlockSpec((1, w), lambda i: (0, 0))
    in_specs = [row, row, vec] + ([row] if res is not None else []) + [pl.BlockSpec(memory_space=pl.ANY)] * len(deps)
    args = [dy, x, g] + ([res] if res is not None else []) + list(deps)
    out_specs = [row] * len(out_dtypes) + [vec]
    out_shape = [jax.ShapeDtypeStruct((s, w), dt) for dt in out_dtypes] + [jax.ShapeDtypeStruct((1, w), F32)]
    return pl.pallas_call(
        body, name=name, grid=(s // tm,), in_specs=in_specs, out_specs=out_specs,
        out_shape=out_shape, compiler_params=_params("arbitrary"),
    )(*args)


def _split3(v):
    p1 = v.astype(BF16)
    r1 = v - p1.astype(F32)
    p2 = r1.astype(BF16)
    p3 = (r1 - p2.astype(F32)).astype(BF16)
    return p1, p2, p3


def _tri_sum(v, reverse, tcol=512):
    h, s = v.shape
    tcol = min(tcol, s)
    parts = _split3(v)
    outs = []
    for j in range(s // tcol):
        src = lax.broadcasted_iota(jnp.int32, (s, tcol), 0)
        dst = lax.broadcasted_iota(jnp.int32, (s, tcol), 1) + j * tcol
        keep = (src >= dst) if reverse else (src <= dst)
        tri = jnp.where(keep, 1.0, 0.0).astype(BF16)
        acc = jnp.zeros((h, tcol), F32)
        for p in parts:
            acc = acc + jnp.dot(p, tri, preferred_element_type=F32)
        outs.append(acc)
    return outs


def _forget_fwd(fa_t, b):
    h, s = fa_t.shape
    tcol = min(512, s)

    def body(f_ref, b_ref, c_ref):
        z = f_ref[...] + b_ref[...]
        logf = jnp.minimum(z, 0.0) - jnp.log(1.0 + jnp.exp(-jnp.abs(z)))
        for j, blk in enumerate(_tri_sum(logf, reverse=False, tcol=tcol)):
            c_ref[:, j * tcol:(j + 1) * tcol] = blk

    return pl.pallas_call(
        body, name="forget_fwd", out_shape=jax.ShapeDtypeStruct((h, s), F32),
        compiler_params=_params(),
    )(fa_t, b)


def _forget_bwd(dacol, fa_t, b):
    h, s = fa_t.shape
    tcol = min(512, s)

    def body(d_ref, f_ref, b_ref, dfa_ref, db_ref):
        z = f_ref[...] + b_ref[...]
        dc = -d_ref[...]
        total = jnp.zeros((h, 1), F32)
        for j, blk in enumerate(_tri_sum(dc, reverse=True, tcol=tcol)):
            cols = slice(j * tcol, (j + 1) * tcol)
            dfa = blk * (1.0 - jax.nn.sigmoid(z[:, cols]))
            dfa_ref[:, cols] = dfa
            total = total + jnp.sum(dfa, axis=-1, keepdims=True)
        db_ref[...] = total

    return pl.pallas_call(
        body, name="forget_bwd",
        out_shape=[jax.ShapeDtypeStruct((h, s), F32), jax.ShapeDtypeStruct((h, 1), F32)],
        compiler_params=_params(),
    )(dacol, fa_t, b)


def _distance_bias(s, tile, dilated):
    nb = s // tile
    b = lax.broadcasted_iota(jnp.int32, (nb, tile, tile), 0)
    dist = b * tile + lax.broadcasted_iota(jnp.int32, (nb, tile, tile), 1) - lax.broadcasted_iota(jnp.int32, (nb, tile, tile), 2)
    if not dilated:
        return jnp.where(dist >= 0, 0.0, NEG).astype(F32)
    mult = jnp.zeros(dist.shape, jnp.int32)
    for window, dil in DIL_PATTERNS:
        mult = mult + ((dist >= 0) & (dist <= window) & ((dist & (dil - 1)) == 0)).astype(jnp.int32)
    logm = jnp.where(mult == 3, math.log2(3.0), jnp.where(mult == 2, 1.0, 0.0))
    return jnp.where(mult > 0, logm, NEG).astype(F32)


def _logits(q, k, arow, acol, bias):
    s = lax.dot_general(q, k, (((1,), (1,)), ((), ())), preferred_element_type=F32)
    return s * (LOG2E / math.sqrt(HEAD_DIM)) + arow - acol + bias


def _attn_fwd(q, k, v, arow, acol, *, dilated, name, tq=ATTN_TILE, tk=ATTN_TILE):
    two_term = not dilated
    s, w = q.shape
    nh = w // HEAD_DIM
    assert tq == tk
    tq = tk = min(tq, s)
    nq, nk = s // tq, s // tk

    pairs = [(i, j) for i in range(nq) for j in range(i + 1)]
    q_of, k_of = (jnp.asarray(t, jnp.int32) for t in zip(*pairs))

    def body(qo_ref, ko_ref, q_ref, k_ref, v_ref, ar_ref, ac_ref, b_ref, o_ref, of_ref, lse_ref, m_ref, l_ref, acc_ref):
        t = pl.program_id(1)
        qi, ki = qo_ref[t], ko_ref[t]

        @pl.when(ki == 0)
        def _():
            m_ref[...] = jnp.full_like(m_ref, NEG)
            l_ref[...] = jnp.zeros_like(l_ref)
            acc_ref[...] = jnp.zeros_like(acc_ref)

        sc = _logits(q_ref[...], k_ref[...], ar_ref[...], ac_ref[...], b_ref[...])
        m_new = jnp.maximum(m_ref[...], jnp.max(sc, axis=-1, keepdims=True))
        alpha = jnp.exp2(m_ref[...] - m_new)
        p = jnp.exp2(sc - m_new)
        l_ref[...] = alpha * l_ref[...] + jnp.sum(p, axis=-1, keepdims=True)
        p_hi = p.astype(BF16)
        vv = v_ref[...]
        pv = jnp.dot(p_hi, vv, preferred_element_type=F32)
        if two_term:
            pv = pv + jnp.dot((p - p_hi.astype(F32)).astype(BF16), vv, preferred_element_type=F32)
        acc_ref[...] = alpha * acc_ref[...] + pv
        m_ref[...] = m_new

        @pl.when(ki == qi)
        def _():
            out = acc_ref[...] / l_ref[...]
            o_ref[...] = out.astype(o_ref.dtype)
            of_ref[...] = out
            lse_ref[...] = m_ref[...] + jnp.log2(l_ref[...])

    qs = pl.BlockSpec((tq, HEAD_DIM), lambda h, t, qo, ko: (qo[t], h))
    kv = pl.BlockSpec((tk, HEAD_DIM), lambda h, t, qo, ko: (ko[t], h))
    rowv = pl.BlockSpec((None, tq, 1), lambda h, t, qo, ko: (h, qo[t], 0))
    return pl.pallas_call(
        body, name=name,
        grid_spec=pltpu.PrefetchScalarGridSpec(
            num_scalar_prefetch=2, grid=(nh, len(pairs)),
            in_specs=[qs, kv, kv, rowv,
                      pl.BlockSpec((None, 1, tk), lambda h, t, qo, ko: (h, 0, ko[t])),
                      pl.BlockSpec((None, tq, tk), lambda h, t, qo, ko: (qo[t] - ko[t], 0, 0))],
            out_specs=[qs, qs, rowv],
            scratch_shapes=[pltpu.VMEM((tq, 1), F32), pltpu.VMEM((tq, 1), F32), pltpu.VMEM((tq, HEAD_DIM), F32)]),
        out_shape=[jax.ShapeDtypeStruct((s, w), BF16), jax.ShapeDtypeStruct((s, w), F32),
                   jax.ShapeDtypeStruct((nh, s, 1), F32)],
        compiler_params=_params("parallel", "arbitrary"),
    )(q_of, k_of, q, k, v, arow * LOG2E, acol * LOG2E, _distance_bias(s, tq, dilated))


def _attn_bwd(q, k, v, o, do, lse, arow, acol, *, dilated, name, tq=ATTN_TILE, tk=ATTN_TILE):
    s, w = q.shape
    nh = w // HEAD_DIM
    assert tq == tk
    tq = tk = min(tq, s)
    nq, nk = s // tq, s // tk
    scale = 1.0 / math.sqrt(HEAD_DIM)

    pairs = [(i, j) for j in range(nk) for i in range(j, nq)]
    q_of, k_of = (jnp.asarray(t, jnp.int32) for t in zip(*pairs))

    def body(qo_ref, ko_ref, q_ref, k_ref, v_ref, o_ref, do_ref, lse_ref, ar_ref, ac_ref, b_ref,
             dq_ref, dk_ref, dv_ref, dac_ref, dk_acc, dv_acc, dac_acc):
        t = pl.program_id(1)
        qi, ki = qo_ref[t], ko_ref[t]

        @pl.when(t == 0)
        def _():
            dq_ref[...] = jnp.zeros_like(dq_ref)

        @pl.when(qi == ki)
        def _():
            dk_acc[...] = jnp.zeros_like(dk_acc)
            dv_acc[...] = jnp.zeros_like(dv_acc)
            dac_acc[...] = jnp.zeros_like(dac_acc)

        qv, kvv, dov = q_ref[...], k_ref[...], do_ref[...]
        sc = _logits(qv, kvv, ar_ref[...], ac_ref[...], b_ref[...])
        p = jnp.exp2(sc - lse_ref[...])
        dp = lax.dot_general(dov, v_ref[...], (((1,), (1,)), ((), ())), preferred_element_type=F32)
        delta = jnp.sum(dov.astype(F32) * o_ref[...].astype(F32), axis=-1, keepdims=True)
        ds = p * (dp - delta)
        dsb = ds.astype(BF16)
        dv_acc[...] += lax.dot_general(p.astype(BF16), dov, (((0,), (0,)), ((), ())), preferred_element_type=F32)
        dk_acc[...] += lax.dot_general(dsb, qv, (((0,), (0,)), ((), ())), preferred_element_type=F32)
        rows = pl.ds(pl.multiple_of(qi * tq, tq), tq)
        dq_ref[rows, :] += jnp.dot(dsb, kvv, preferred_element_type=F32) * scale
        dac_acc[...] += jnp.sum(ds, axis=0, keepdims=True)

        @pl.when(qi == nq - 1)
        def _():
            dk_ref[...] = dk_acc[...] * scale
            dv_ref[...] = dv_acc[...]
            dac_ref[...] = dac_acc[...]

    qs = pl.BlockSpec((tq, HEAD_DIM), lambda h, t, qo, ko: (qo[t], h))
    ks = pl.BlockSpec((tk, HEAD_DIM), lambda h, t, qo, ko: (ko[t], h))
    rowv = pl.BlockSpec((None, tq, 1), lambda h, t, qo, ko: (h, qo[t], 0))
    colv = pl.BlockSpec((None, 1, tk), lambda h, t, qo, ko: (h, 0, ko[t]))
    return pl.pallas_call(
        body, name=name,
        grid_spec=pltpu.PrefetchScalarGridSpec(
            num_scalar_prefetch=2, grid=(nh, len(pairs)),
            in_specs=[qs, ks, ks, qs, qs, rowv, rowv, colv,
                      pl.BlockSpec((None, tq, tk), lambda h, t, qo, ko: (qo[t] - ko[t], 0, 0))],
            out_specs=[pl.BlockSpec((s, HEAD_DIM), lambda h, t, qo, ko: (0, h)), ks, ks, colv],
            scratch_shapes=[pltpu.VMEM((tk, HEAD_DIM), F32), pltpu.VMEM((tk, HEAD_DIM), F32), pltpu.VMEM((1, tk), F32)]),
        out_shape=[jax.ShapeDtypeStruct((s, w), F32), jax.ShapeDtypeStruct((s, w), F32),
                   jax.ShapeDtypeStruct((s, w), F32), jax.ShapeDtypeStruct((nh, 1, s), F32)],
        compiler_params=_params("arbitrary", "arbitrary"),
    )(q_of, k_of, q, k, v, o, do, lse, arow * LOG2E, acol * LOG2E, _distance_bias(s, tq, dilated))


def _gate_fwd(ga, gb, pa, pb, tm=256):
    s, d = ga.shape

    def body(ga_ref, gb_ref, pa_ref, pb_ref, o_ref):
        o_ref[...] = (jax.nn.sigmoid(ga_ref[...]) * pa_ref[...]
                      + jax.nn.sigmoid(gb_ref[...]) * pb_ref[...]).astype(o_ref.dtype)

    row = pl.BlockSpec((tm, d), lambda i: (i, 0))
    return pl.pallas_call(
        body, name="gate_fwd", grid=(s // tm,), in_specs=[row] * 4, out_specs=row,
        out_shape=jax.ShapeDtypeStruct((s, d), BF16), compiler_params=_params("parallel"),
    )(ga, gb, pa, pb)


def _gate_bwd(dm, ga, gb, pa, pb, tm=256):
    s, d = ga.shape

    def body(dm_ref, ga_ref, gb_ref, pa_ref, pb_ref, dpa_ref, dpb_ref, dga_ref, dgb_ref):
        dmv = dm_ref[...]
        for g_ref, p_ref, dp_ref, dg_ref in ((ga_ref, pa_ref, dpa_ref, dga_ref), (gb_ref, pb_ref, dpb_ref, dgb_ref)):
            sg = jax.nn.sigmoid(g_ref[...])
            dp_ref[...] = (dmv * sg).astype(BF16)
            dg_ref[...] = (dmv * p_ref[...] * (sg * (1.0 - sg))).astype(BF16)

    row = pl.BlockSpec((tm, d), lambda i: (i, 0))
    return pl.pallas_call(
        body, name="gate_bwd", grid=(s // tm,), in_specs=[row] * 5, out_specs=[row] * 4,
        out_shape=[jax.ShapeDtypeStruct((s, d), BF16)] * 4, compiler_params=_params("parallel"),
    )(dm, ga, gb, pa, pb)


def _shift_down(u, k):
    row = lax.broadcasted_iota(jnp.int32, u.shape, 0)
    return jnp.where(row >= k, pltpu.roll(u, k, 0), 0.0)


def _shift_up(u, k):
    n = u.shape[0]
    row = lax.broadcasted_iota(jnp.int32, u.shape, 0)
    return jnp.where(row < n - k, pltpu.roll(u, n - k, 0), 0.0)


def _conv3(u, wc, b):
    return wc[0:1, :] * _shift_down(u, 2) + wc[1:2, :] * _shift_down(u, 1) + wc[2:3, :] * u + b


def _conv_glu_fwd(u, wc, b, tn=256):
    s, f2 = u.shape
    f = f2 // 2
    nb = f // tn

    def body(ug_ref, uv_ref, wg_ref, wv_ref, bg_ref, bv_ref, o_ref):
        cg = _conv3(ug_ref[...], wg_ref[...], bg_ref[...])
        cv = _conv3(uv_ref[...], wv_ref[...], bv_ref[...])
        o_ref[...] = (cg * jax.nn.sigmoid(cg) * cv).astype(o_ref.dtype)

    def cols(rows, off):
        return pl.BlockSpec((rows, tn), lambda j: (0, j + off))

    return pl.pallas_call(
        body, name="conv_glu_fwd", grid=(nb,),
        in_specs=[cols(s, 0), cols(s, nb), cols(3, 0), cols(3, nb), cols(1, 0), cols(1, nb)],
        out_specs=cols(s, 0), out_shape=jax.ShapeDtypeStruct((s, f), BF16),
        compiler_params=_params("parallel"),
    )(u, u, wc, wc, b, b)


def _conv_glu_bwd(u, da, wc, b, tn=256):
    s, f2 = u.shape
    f = f2 // 2
    nb = f // tn

    def body(ug_ref, uv_ref, da_ref, wg_ref, wv_ref, bg_ref, bv_ref, dug_ref, duv_ref, sg_ref, sv_ref):
        ug, uv, wg, wv = ug_ref[...], uv_ref[...], wg_ref[...], wv_ref[...]
        cg = _conv3(ug, wg, bg_ref[...])
        cv = _conv3(uv, wv, bv_ref[...])
        sig = jax.nn.sigmoid(cg)
        dav = da_ref[...]
        dcv = dav * (cg * sig)
        dcg = dav * cv * (sig * (1.0 + cg * (1.0 - sig)))
        for dc, uu, w, du_ref, st_ref in ((dcg, ug, wg, dug_ref, sg_ref), (dcv, uv, wv, duv_ref, sv_ref)):
            du = w[2:3, :] * dc + w[1:2, :] * _shift_up(dc, 1) + w[0:1, :] * _shift_up(dc, 2)
            du_ref[...] = du.astype(BF16)
            st_ref[...] = jnp.zeros_like(st_ref)
            st_ref[0:1, :] = jnp.sum(dc * _shift_down(uu, 2), axis=0, keepdims=True)
            st_ref[1:2, :] = jnp.sum(dc * _shift_down(uu, 1), axis=0, keepdims=True)
            st_ref[2:3, :] = jnp.sum(dc * uu, axis=0, keepdims=True)
            st_ref[3:4, :] = jnp.sum(dc, axis=0, keepdims=True)

    def cols(rows, off):
        return pl.BlockSpec((rows, tn), lambda j: (0, j + off))

    return pl.pallas_call(
        body, name="conv_glu_bwd", grid=(nb,),
        in_specs=[cols(s, 0), cols(s, nb), cols(s, 0), cols(3, 0), cols(3, nb), cols(1, 0), cols(1, nb)],
        out_specs=[cols(s, 0), cols(s, 0), cols(8, 0), cols(8, 0)],
        out_shape=[jax.ShapeDtypeStruct((s, f), BF16), jax.ShapeDtypeStruct((s, f), BF16),
                   jax.ShapeDtypeStruct((8, f), F32), jax.ShapeDtypeStruct((8, f), F32)],
        compiler_params=_params("parallel"),
    )(u, u, da, wc, wc, b, b)


ROW_TILES = (256, 128, 64, 32, 16, 8)
BLOCK_BYTES = 2 << 20


def _add_halves(g, r1, place):
    ns, r, c = g.shape
    rh = r // 2
    tr = _pick(rh, ROW_TILES)
    g4 = g.reshape(ns, 2, rh, c)

    def body(p_ref, g_ref, r_ref, o_ref):
        o_ref[...] = (g_ref[...].astype(F32) + r_ref[...].astype(F32)).astype(o_ref.dtype)

    def slab(s, pr):
        return s + (s >= pr[0]).astype(jnp.int32)

    return pl.pallas_call(
        body, name="add_halves",
        grid_spec=pltpu.PrefetchScalarGridSpec(
            num_scalar_prefetch=1, grid=(ns - 1, rh // tr),
            in_specs=[pl.BlockSpec((None, None, tr, c), lambda s, i, pr: (slab(s, pr), pr[1], i, 0)),
                      pl.BlockSpec((None, tr, c), lambda s, i, pr: (slab(s, pr), i, 0))],
            out_specs=pl.BlockSpec((None, tr, c), lambda s, i, pr: (slab(s, pr), i, 0))),
        out_shape=jax.ShapeDtypeStruct((ns, rh, c), BF16),
        compiler_params=_params("parallel", "parallel"),
    )(place, g4, r1)


def _sum_chips(g, r1, recv, place):
    ns, r, c = g.shape
    rh = r // 2
    tr = _pick(rh, ROW_TILES)
    g4 = g.reshape(ns, 2, rh, c)

    def body(p_ref, g_ref, r_ref, t0_ref, t1_ref, t2_ref, o_ref):
        own = g_ref[...].astype(F32) + r_ref[...].astype(F32)
        o_ref[...] = ((own + t0_ref[...].astype(F32)) + t1_ref[...].astype(F32)) + t2_ref[...].astype(F32)

    def peer(k):
        return pl.BlockSpec((None, tr, c), lambda i, pr: (k, i, 0))

    return pl.pallas_call(
        body, name="sum_chips",
        grid_spec=pltpu.PrefetchScalarGridSpec(
            num_scalar_prefetch=1, grid=(rh // tr,),
            in_specs=[pl.BlockSpec((None, None, tr, c), lambda i, pr: (pr[0], pr[1], i, 0)),
                      pl.BlockSpec((None, tr, c), lambda i, pr: (pr[0], i, 0)), peer(0), peer(1), peer(2)],
            out_specs=pl.BlockSpec((tr, c), lambda i, pr: (pr[1] * (rh // tr) + i, 0))),
        out_shape=jax.ShapeDtypeStruct((r, c), F32),
        compiler_params=_params("parallel"),
    )(place, g4, r1, recv, recv, recv)


def _sum_devices(packs):
    n, r, c = packs.shape

    def body(p_ref, o_ref):
        acc = p_ref[0]
        for d in range(1, n):
            acc = acc + p_ref[d]
        o_ref[...] = acc

    return pl.pallas_call(
        body, name="sum_devices", out_shape=jax.ShapeDtypeStruct((r, c), F32), compiler_params=_params(),
    )(packs)


def _adamw_update(wv, gv, mv, vv):
    c1 = 1.0 - ADAM_B1 ** ADAM_STEP
    c2 = 1.0 - ADAM_B2 ** ADAM_STEP
    mn = ADAM_B1 * mv + (1.0 - ADAM_B1) * gv
    vn = ADAM_B2 * vv + (1.0 - ADAM_B2) * (gv * gv)
    m_hat = mn / c1
    v_hat = vn / c2
    return -ADAM_LR * (m_hat / (jnp.sqrt(v_hat) + ADAM_EPS) + ADAM_WD * wv), mn, vn


def _adamw(w, g, m, v, name, deps=(), emit_grad=False):
    r, c = w.shape
    tr = _pick(r, [t for t in ROW_TILES if t * c * 4 <= BLOCK_BYTES]) if r >= 8 else r
    n_out = 4 if emit_grad else 3

    def body(w_ref, g_ref, m_ref, v_ref, *rest):
        outs = rest[-n_out:]
        gv = g_ref[:, :c]
        if emit_grad:
            outs[0][...] = gv
        outs[-3][...], outs[-2][...], outs[-1][...] = _adamw_update(w_ref[...], gv, m_ref[...], v_ref[...])

    blk = pl.BlockSpec((tr, c), lambda i: (i, 0))
    g_blk = pl.BlockSpec((tr, g.shape[1]), lambda i: (i, 0))
    return pl.pallas_call(
        body, name=name, grid=(r // tr,), in_specs=[blk, g_blk, blk, blk] + [ANY] * len(deps), out_specs=[blk] * n_out,
        out_shape=[jax.ShapeDtypeStruct((r, c), F32)] * n_out, compiler_params=_params("parallel"),
    )(w, g, m, v, *deps)


ANY = pl.BlockSpec(memory_space=pl.ANY)


def _place():
    x, y, c = lax.axis_index("x"), lax.axis_index("y"), lax.axis_index("c")
    chips = [(1 - x, y), (x, 1 - y), (1 - x, 1 - y)]
    return x, y, c, chips


def _remote(src, dst, send_sem, recv_sem, to):
    return pltpu.make_async_remote_copy(src_ref=src, dst_ref=dst, send_sem=send_sem, recv_sem=recv_sem,
                                        device_id=to, device_id_type=MESH)


HBM = pl.BlockSpec(memory_space=pltpu.HBM)
SEM = pl.BlockSpec(memory_space=pltpu.SEMAPHORE)
EFFECT = pltpu.SideEffectType.DATAFLOW_SIDE_EFFECTING


def _in_hbm(a):
    return pltpu.with_memory_space_constraint(a, pltpu.HBM)


def _half(ref_rows, who):
    return pl.ds(who * (ref_rows // 2), ref_rows // 2)


def _gather_start(groups, name):
    items = [it for g in groups for it in g]
    n = len(items)
    sizes = [len(g) for g in groups]

    def body(*refs):
        srcs, lands = refs[:n], refs[n:2 * n]
        sems = refs[2 * n:2 * n + 2 * len(groups)]
        token = refs[-1]
        x, y, c, chips = _place()
        j = 2 * x + y
        at = 0
        for gi, g in enumerate(groups):
            send, recv = sems[2 * gi], sems[2 * gi + 1]
            for i, (shard, split) in enumerate(g):
                src, land = srcs[at], lands[at]
                at += 1
                rows = _half(shard.shape[0], c) if split else slice(None)
                for k, chip in enumerate(chips):
                    _remote(src.at[rows], land.at[j, rows], send.at[4 * i + k], recv.at[4 * i + k], (*chip, c)).start()
                _remote(src, land.at[j], send.at[4 * i + 3], recv.at[4 * i + 3], (x, y, 1 - c)).start()
        token[...] = jnp.zeros_like(token)

    sem_shapes = []
    for sz in sizes:
        sem_shapes += [pltpu.SemaphoreType.DMA((4 * sz,)), pltpu.SemaphoreType.DMA((4 * sz,))]
    out_shape = (sem_shapes + [pltpu.HBM(sh.shape, sh.dtype) for sh, _ in items]
                 + [pltpu.HBM((N_CHIPS,) + sh.shape, sh.dtype) for sh, _ in items]
                 + [jax.ShapeDtypeStruct((8, LANES), F32)])
    ns = len(sem_shapes)
    outs = pl.pallas_call(
        body, name=name, in_specs=[HBM] * (2 * n),
        out_specs=[SEM] * ns + [HBM] * (2 * n) + [pl.BlockSpec(memory_space=pltpu.VMEM)],
        out_shape=out_shape, input_output_aliases={i: ns + i for i in range(2 * n)},
        compiler_params=pltpu.CompilerParams(has_side_effects=EFFECT),
    )(*[_in_hbm(sh) for sh, _ in items], *[_in_hbm(lax.empty((N_CHIPS,) + sh.shape, sh.dtype)) for sh, _ in items])
    sems, shards, lands, token = outs[:ns], outs[ns:ns + n], outs[ns + n:ns + 2 * n], outs[-1]
    res, at = [], 0
    for gi, sz in enumerate(sizes):
        res.append((shards[at:at + sz], lands[at:at + sz], sems[2 * gi], sems[2 * gi + 1]))
        at += sz
    return res, token


def _gather_pass(group, started, after, name):
    shards, lands, send, recv = started
    n = len(group)
    split_ix = [i for i, (_, split) in enumerate(group) if split]

    def body(*refs):
        lnds, send1, recv1 = refs[n:2 * n], refs[2 * n], refs[2 * n + 1]
        outs = refs[2 * n + 2 + len(after):]
        send2, recv2, token = outs[2 * n], outs[2 * n + 1], outs[2 * n + 2]
        x, y, c, chips = _place()
        sib = (x, y, 1 - c)
        for i, (shard, split) in enumerate(group):
            rows = _half(shard.shape[0], c) if split else slice(None)
            for k, (cx, cy) in enumerate(chips):
                landed = lnds[i].at[2 * cx + cy, rows]
                cp = _remote(landed, landed, send1.at[4 * i + k], recv1.at[4 * i + k], sib)
                cp.wait_send()
                cp.wait_recv()
            own = lnds[i].at[2 * x + y]
            cp = _remote(own, own, send1.at[4 * i + 3], recv1.at[4 * i + 3], sib)
            cp.wait_send()
            cp.wait_recv()
        for i2, i in enumerate(split_ix):
            rows = _half(group[i][0].shape[0], c)
            for k, (cx, cy) in enumerate(chips):
                landed = lnds[i].at[2 * cx + cy, rows]
                _remote(landed, landed, send2.at[3 * i2 + k], recv2.at[3 * i2 + k], sib).start()
        token[...] = jnp.zeros_like(token)

    n2 = len(split_ix)
    out_shape = ([pltpu.HBM(a.shape, a.dtype) for a in (*shards, *lands)]
                 + [pltpu.SemaphoreType.DMA((3 * n2,)), pltpu.SemaphoreType.DMA((3 * n2,)), jax.ShapeDtypeStruct((8, LANES), F32)])
    outs = pl.pallas_call(
        body, name=name, in_specs=[HBM] * (2 * n) + [SEM, SEM] + [ANY] * len(after),
        out_specs=[HBM] * (2 * n) + [SEM, SEM, pl.BlockSpec(memory_space=pltpu.VMEM)],
        out_shape=out_shape, input_output_aliases={i: i for i in range(2 * n)},
        compiler_params=pltpu.CompilerParams(has_side_effects=EFFECT),
    )(*shards, *lands, send, recv, *after)
    return outs[:n], (outs[n:2 * n], outs[2 * n], outs[2 * n + 1]), outs[2 * n + 2]


def _gather_wait(group, passed, after, name):
    lands, send2, recv2 = passed
    n = len(group)
    split_ix = [i for i, (_, split) in enumerate(group) if split]

    def body(*refs):
        lnds, s2, r2 = refs[:n], refs[n], refs[n + 1]
        x, y, c, chips = _place()
        sib = (x, y, 1 - c)
        for i2, i in enumerate(split_ix):
            rows = _half(group[i][0].shape[0], 1 - c)
            for k, (cx, cy) in enumerate(chips):
                landed = lnds[i].at[2 * cx + cy, rows]
                cp = _remote(landed, landed, s2.at[3 * i2 + k], r2.at[3 * i2 + k], sib)
                cp.wait_send()
                cp.wait_recv()

    return pl.pallas_call(
        body, name=name, in_specs=[HBM] * n + [SEM, SEM, ANY], out_specs=[HBM] * n,
        out_shape=[pltpu.HBM(a.shape, a.dtype) for a in lands], input_output_aliases={i: i for i in range(n)},
        compiler_params=pltpu.CompilerParams(has_side_effects=EFFECT),
    )(*lands, send2, recv2, after)


def _xfer_start(name, srcs, land_shapes, n_copies, copies, after):
    n, nl = len(srcs), len(land_shapes)

    def body(*refs):
        src_refs, land_refs = refs[:n], refs[n:n + nl]
        send, recv, token = refs[n + nl + 1], refs[n + nl + 2], refs[-1]
        for cp in copies(src_refs, land_refs, send, recv):
            cp.start()
        token[...] = jnp.zeros_like(token)

    lands = [_in_hbm(lax.empty(shape, dtype)) for shape, dtype in land_shapes]
    out_shape = ([pltpu.SemaphoreType.DMA((n_copies,)), pltpu.SemaphoreType.DMA((n_copies,))]
                 + [pltpu.HBM(a.shape, a.dtype) for a in (*srcs, *lands)] + [jax.ShapeDtypeStruct((8, LANES), F32)])
    outs = pl.pallas_call(
        body, name=name, in_specs=[HBM] * (n + nl) + [ANY],
        out_specs=[SEM, SEM] + [HBM] * (n + nl) + [pl.BlockSpec(memory_space=pltpu.VMEM)],
        out_shape=out_shape, input_output_aliases={i: 2 + i for i in range(n + nl)},
        compiler_params=pltpu.CompilerParams(has_side_effects=EFFECT),
    )(*[_in_hbm(a) for a in srcs], *lands, after)
    return (outs[2:2 + n], outs[2 + n:2 + n + nl], outs[0], outs[1]), outs[-1]


def _xfer_wait(name, started, copies, after):
    srcs, lands, send, recv = started
    n, nl = len(srcs), len(lands)

    def body(*refs):
        src_refs, land_refs, s_ref, r_ref = refs[:n], refs[n:n + nl], refs[n + nl], refs[n + nl + 1]
        for cp in copies(src_refs, land_refs, s_ref, r_ref):
            cp.wait_send()
            cp.wait_recv()

    outs = pl.pallas_call(
        body, name=name, in_specs=[HBM] * (n + nl) + [SEM, SEM, ANY], out_specs=[HBM] * (n + nl),
        out_shape=[pltpu.HBM(a.shape, a.dtype) for a in (*srcs, *lands)],
        input_output_aliases={i: i for i in range(n + nl)},
        compiler_params=pltpu.CompilerParams(has_side_effects=EFFECT),
    )(*srcs, *lands, send, recv, after)
    return outs[:n], outs[n:]


def _swap_copies(srcs, lands, send, recv):
    x, y, c, _ = _place()
    return [_remote(src.at[:, _half(src.shape[1], 1 - c)], land, send.at[i], recv.at[i], (x, y, 1 - c))
            for i, (src, land) in enumerate(zip(srcs, lands))]


def _scatter_copies(srcs, lands, send, recv):
    x, y, c, chips = _place()
    return [_remote(src.at[2 * cx + cy], land.at[k], send.at[3 * i + k], recv.at[3 * i + k], (cx, cy, c))
            for i, (src, land) in enumerate(zip(srcs, lands)) for k, (cx, cy) in enumerate(chips)]


def _join_copies(srcs, lands, send, recv):
    x, y, c, _ = _place()
    return [_remote(src.at[_half(src.shape[0], c)], src.at[_half(src.shape[0], c)], send.at[i], recv.at[i], (x, y, 1 - c))
            for i, src in enumerate(srcs)]


def _corner(a):
    return a[(slice(0, 1),) * a.ndim]


class _Reducer:
    def __init__(self, place):
        self.place = place
        self.state = {}

    def swap(self, key, grads, after):
        shapes = [((g.shape[0], g.shape[1] // 2, g.shape[2]), g.dtype) for g in grads]
        self.state[key], token = _xfer_start("swap_start_" + key, grads, shapes, len(grads), _swap_copies, _corner(after))
        return token

    def to_chips(self, key, after):
        grads, from_sibling = _xfer_wait("swap_wait_" + key, self.state[key], _swap_copies, after)
        sums = [_add_halves(g, r, self.place) for g, r in zip(grads, from_sibling)]
        shapes = [((3,) + s.shape[1:], s.dtype) for s in sums]
        started, token = _xfer_start("scatter_start_" + key, sums, shapes, 3 * len(sums), _scatter_copies, _corner(sums[-1]))
        self.state[key] = (grads, from_sibling, started)
        return token

    def to_core(self, key, after):
        grads, from_sibling, started = self.state[key]
        _, from_chips = _xfer_wait("scatter_wait_" + key, started, _scatter_copies, after)
        shards = [_sum_chips(g, r, rc, self.place) for g, r, rc in zip(grads, from_sibling, from_chips)]
        self.state[key], token = _xfer_start("join_start_" + key, shards, [], len(shards), _join_copies, _corner(shards[-1]))
        return token

    def finish(self, key, after):
        return _xfer_wait("join_wait_" + key, self.state.pop(key), _join_copies, after)[0]


def _gather_packs(pack, deps=()):
    def body(p_ref, *rest):
        o_ref, lsem, ssem, rsem = rest[-4:]
        x, y, c, _ = _place()
        me = 4 * x + 2 * y + c
        local = pltpu.make_async_copy(p_ref, o_ref.at[me], lsem)
        local.start()
        cps = []
        for k in range(1, N_DEV):
            fx, fy, fc = (k >> 2) & 1, (k >> 1) & 1, k & 1
            to = (x ^ fx, y ^ fy, c ^ fc)
            cps.append(_remote(p_ref, o_ref.at[me], ssem.at[k - 1], rsem.at[k - 1], to))
        for cp in cps:
            cp.start()
        for k in range(1, N_DEV):
            fx, fy, fc = (k >> 2) & 1, (k >> 1) & 1, k & 1
            src = o_ref.at[4 * (x ^ fx) + 2 * (y ^ fy) + (c ^ fc)]
            _remote(src, src, ssem.at[k - 1], rsem.at[k - 1], (x, y, c)).wait_recv()
        for cp in cps:
            cp.wait_send()
        local.wait()

    return pl.pallas_call(
        body, name="gather_packs", in_specs=[ANY] * (1 + len(deps)), out_specs=ANY,
        out_shape=jax.ShapeDtypeStruct((N_DEV,) + pack.shape, pack.dtype),
        scratch_shapes=[pltpu.SemaphoreType.DMA, pltpu.SemaphoreType.DMA((N_DEV - 1,)), pltpu.SemaphoreType.DMA((N_DEV - 1,))],
    )(pack, *deps)


LANE_TILES = (512, 896, 1408, 704, 384, 256, 128)


def _layer_grads(x, target, small, wg, rest_pass, rest_wait, red, filler):
    s, d = x.shape
    f = wg["conv"].shape[1] // 2
    w_att = N_HEADS * HEAD_DIM
    in_splits = (w_att, w_att, w_att, N_HEADS, w_att, w_att, w_att, d, d)
    in_cols = sum(in_splits)
    cs = in_cols // N_CHIPS
    cp = wg["in"].shape[2]
    tm = min(s, MM_TILE)
    tm_wide = min(s, MM_TILE // 2)
    t_in = cp
    t_up = 2 * f // N_CHIPS
    t_d = _pick(d, LANE_TILES)
    t_d2 = min(d, MM_TILE)
    t_dq = _pick(d // N_CHIPS, LANE_TILES)
    t_fq = _pick(f // N_CHIPS, LANE_TILES)

    h1 = _norm_fwd(x, small["g_attn"], group=d, name="rms1_fwd")
    proj_p = _mm(h1, wg["in"], mode="nn", b_kind="col", tm=tm_wide, tn=t_in, tk=d, name="mm_in")
    gains = {n: small[n].reshape(1, w_att) for n in ("g_q_fox", "g_k_fox", "g_q_dil", "g_k_dil")}
    qa, ka, va_b, fa, qb, kb, vb_b, ga, gb, qa_n, ka_n, qb_n, kb_n = _proj_split(
        proj_p, in_splits, cs, (F32, F32, BF16, F32, F32, F32, BF16, F32, F32),
        {0: gains["g_q_fox"], 1: gains["g_k_fox"], 4: gains["g_q_dil"], 5: gains["g_k_dil"]})
    fa_t = fa.T
    b_f = small["b_forget"].reshape(N_HEADS, 1)
    c_f = _forget_fwd(fa_t, b_f)
    slopes = jnp.asarray(2.0 ** (-8.0 * np.arange(1, N_HEADS + 1) / N_HEADS), dtype=F32)
    a_d = -(slopes[:, None] * jnp.arange(s, dtype=F32)[None, :])
    rows_f, cols_f = c_f[:, :, None], c_f[:, None, :]
    rows_d, cols_d = a_d[:, :, None], a_d[:, None, :]
    o_a, o_a32, lse_a = _attn_fwd(qa_n, ka_n, va_b, rows_f, cols_f, dilated=False, name="attn_fox_fwd")
    token = rest_pass("mid", o_a)
    rows_d = rows_d + token[0, 0]
    o_b, o_b32, lse_b = _attn_fwd(qb_n, kb_n, vb_b, rows_d, cols_d, dilated=True, name="attn_dil_fwd")
    wg = dict(wg, **rest_wait("mid", o_b))
    token = rest_pass("late", o_b)
    pa = _mm(o_a, wg["brf"], mode="nn", b_kind="col", tm=tm, tn=t_dq, tk=w_att, name="mm_brf", deps=(token,))
    pb = _mm(o_b, wg["brd"], mode="nn", b_kind="col", tm=tm, tn=t_dq, tk=w_att, name="mm_brd")
    merged = _gate_fwd(ga, gb, pa, pb)
    x1 = _mm(merged, wg["out"], mode="nn", b_kind="row", res=x, tm=tm, tn=t_d, tk=t_dq, name="mm_out")
    wg = dict(wg, **rest_wait("late", x1))
    h2 = _norm_fwd(x1, small["g_ffn"], group=d, name="rms2_fwd")
    u = _mm(h2, wg["up"], mode="nn", b_kind="col", tm=tm_wide, tn=t_up, tk=d, name="mm_up")
    act = _conv_glu_fwd(u, wg["conv"], wg["bconv"])
    dy_f, dy_b, loss_blk = _mm(act, wg["down"], mode="nn", b_kind="row", res=x1, loss_target=target,
                               tm=tm, tn=t_d2, tk=t_fq, name="mm_down")

    d_act = _mm(dy_b, wg["down"], mode="nt", b_kind="row", tm=tm, tn=t_fq, tk=d, name="mm_down_dx")
    g_down = _mm(act, dy_b, mode="tn", out_dtype=BF16, out_kind="row", tm=t_fq, tn=t_d2, tk=s, name="mm_down_dw")
    tok = red.swap("down", [g_down], g_down)
    du_g, du_v, st_g, st_v = _conv_glu_bwd(u, d_act, wg["conv"] + tok[0, 0], wg["bconv"])
    tok = red.to_chips("down", du_g)
    du = (du_g, du_v)
    g_up = _mm(h2, du, mode="tn", out_dtype=BF16, out_kind="col", tm=t_d2, tn=t_up // 2, tk=s, name="mm_up_dw", deps=(tok,))
    tok = red.to_core("down", g_up)
    tok2 = red.swap("up", [g_up], g_up)
    dh2 = _mm(du, wg["up"], mode="nt", b_kind="col", tm=tm, tn=t_d2, tk=t_up, name="mm_up_dx", deps=(tok, tok2))
    tok = red.to_chips("up", dh2)
    dx1_b, dx1_f, dg_ffn = _norm_bwd(dh2, x1, small["g_ffn"], group=d, res=dy_f, out_dtypes=(BF16, F32), name="rms2_bwd")
    d_merged = _mm(dx1_b, wg["out"], mode="nt", b_kind="row", tm=tm, tn=t_dq, tk=d, name="mm_out_dx", deps=(tok,))
    g_out = _mm(merged, dx1_b, mode="tn", out_dtype=BF16, out_kind="row", tm=t_dq, tn=t_d2, tk=s, name="mm_out_dw")
    dpa, dpb, dga, dgb = _gate_bwd(d_merged, ga, gb, pa, pb)
    do_a = _mm(dpa, wg["brf"], mode="nt", b_kind="col", out_dtype=BF16, tm=s, tn=w_att, tk=t_dq, name="mm_brf_dx")
    do_b = _mm(dpb, wg["brd"], mode="nt", b_kind="col", out_dtype=BF16, tm=s, tn=w_att, tk=t_dq, name="mm_brd_dx")
    g_brf = _mm(o_a, dpa, mode="tn", out_dtype=BF16, out_kind="col", tm=w_att, tn=t_dq, tk=s, name="mm_brf_dw")
    g_brd = _mm(o_b, dpb, mode="tn", out_dtype=BF16, out_kind="col", tm=w_att, tn=t_dq, tk=s, name="mm_brd_dw")
    tok = red.swap("mix", [g_out, g_brf, g_brd], g_brd)
    dqa_n, dka_n, dva, dac_a = _attn_bwd(qa_n, ka_n, va_b, o_a32, do_a, lse_a, rows_f + tok[0, 0], cols_f, dilated=False, name="attn_fox_bwd")
    tok = red.to_core("up", dqa_n)
    tok2 = red.to_chips("mix", dqa_n)
    dqb_n, dkb_n, dvb, _ = _attn_bwd(qb_n, kb_n, vb_b, o_b32, do_b, lse_b, rows_d + (tok[0, 0] + tok2[0, 0]), cols_d, dilated=True, name="attn_dil_bwd")
    tok = red.to_core("mix", dqb_n)
    dfa_t, db_f = _forget_bwd(dac_a[:, 0, :], fa_t, b_f)
    dproj_p, dgains = _dproj_merge(
        [dqa_n, dka_n, dva, dfa_t.T, dqb_n, dkb_n, dvb, dga, dgb], in_splits, cs, cp,
        {0: (qa, gains["g_q_fox"]), 1: (ka, gains["g_k_fox"]), 4: (qb, gains["g_q_dil"]), 5: (kb, gains["g_k_dil"])})
    dg_qf, dg_kf, dg_qd, dg_kd = dgains[0], dgains[1], dgains[4], dgains[5]
    g_in = _mm(h1, dproj_p, mode="tn", out_dtype=BF16, out_kind="col", tm=t_d2, tn=t_in, tk=s, name="mm_in_dw", deps=(tok,))
    tok = red.swap("in", [g_in], g_in)
    tok = red.to_chips("in", filler(tok))
    dh1 = _mm(dproj_p, wg["in"], mode="nt", b_kind="col", tm=tm, tn=t_d2, tk=t_in, name="mm_in_dx", deps=(tok,))
    grad_x, dg_attn = _norm_bwd(dh1, x, small["g_attn"], group=d, res=dx1_f, out_dtypes=(F32,), name="rms1_bwd")

    small_grads = {
        "g_attn": dg_attn, "b_forget": db_f.reshape(1, N_HEADS),
        "g_q_fox": dg_qf, "g_k_fox": dg_kf, "g_q_dil": dg_qd, "g_k_dil": dg_kd, "g_ffn": dg_ffn,
        "w_conv": jnp.concatenate([st_g[0:3], st_v[0:3]], axis=1),
        "b_conv": jnp.concatenate([st_g[3:4], st_v[3:4]], axis=1),
        "loss": loss_blk[0:1, 0:1],
    }
    return small_grads, grad_x


SMALL_ORDER = ("g_attn", "b_forget", "g_q_fox", "g_k_fox", "g_q_dil", "g_k_dil", "g_ffn", "w_conv", "b_conv", "loss")
WEIGHT_ORDER = ("g_attn", "w_in", "b_forget", "g_q_fox", "g_k_fox", "g_q_dil", "g_k_dil", "w_br_fox", "w_br_dil",
                "w_out", "g_ffn", "w_up", "w_conv", "b_conv", "w_down")
BIG = {"w_in": "in", "w_br_fox": "brf", "w_br_dil": "brd", "w_out": "out", "w_up": "up", "w_down": "down"}


def kernel(x, g_attn, w_in, b_forget, g_q_fox, g_k_fox, g_q_dil, g_k_dil, w_br_fox, w_br_dil, w_out, g_ffn, w_up, w_conv, b_conv, w_down, loss_target, m_g_attn, m_w_in, m_b_forget, m_g_q_fox, m_g_k_fox, m_g_q_dil, m_g_k_dil, m_w_br_fox, m_w_br_dil, m_w_out, m_g_ffn, m_w_up, m_w_conv, m_b_conv, m_w_down, v_g_attn, v_w_in, v_b_forget, v_g_q_fox, v_g_k_fox, v_g_q_dil, v_g_k_dil, v_w_br_fox, v_w_br_dil, v_w_out, v_g_ffn, v_w_up, v_w_conv, v_b_conv, v_w_down):
    w = dict(g_attn=g_attn, w_in=w_in, b_forget=b_forget, g_q_fox=g_q_fox, g_k_fox=g_k_fox, g_q_dil=g_q_dil,
             g_k_dil=g_k_dil, w_br_fox=w_br_fox, w_br_dil=w_br_dil, w_out=w_out, g_ffn=g_ffn, w_up=w_up,
             w_conv=w_conv, b_conv=b_conv, w_down=w_down)
    m = dict(g_attn=m_g_attn, w_in=m_w_in, b_forget=m_b_forget, g_q_fox=m_g_q_fox, g_k_fox=m_g_k_fox,
             g_q_dil=m_g_q_dil, g_k_dil=m_g_k_dil, w_br_fox=m_w_br_fox, w_br_dil=m_w_br_dil, w_out=m_w_out,
             g_ffn=m_g_ffn, w_up=m_w_up, w_conv=m_w_conv, b_conv=m_b_conv, w_down=m_w_down)
    v = dict(g_attn=v_g_attn, w_in=v_w_in, b_forget=v_b_forget, g_q_fox=v_g_q_fox, g_k_fox=v_g_k_fox,
             g_q_dil=v_g_q_dil, g_k_dil=v_g_k_dil, w_br_fox=v_w_br_fox, w_br_dil=v_w_br_dil, w_out=v_w_out,
             g_ffn=v_g_ffn, w_up=v_w_up, w_conv=v_w_conv, b_conv=v_b_conv, w_down=v_w_down)
    xi, yi, ci = lax.axis_index("x"), lax.axis_index("y"), lax.axis_index("c")
    chip = (2 * xi + yi).astype(jnp.int32)

    cs = w_in.shape[2]
    cp = _round_up(cs, LANES)
    shards = {
        "in": jnp.pad(w_in[0].astype(BF16), ((0, 0), (0, cp - cs))),
        "brf": w_br_fox[0].astype(BF16), "brd": w_br_dil[0].astype(BF16), "out": w_out[0].astype(BF16),
        "up": w_up[0].astype(BF16), "down": w_down[0].astype(BF16),
    }
    conv_pad = jnp.pad(w_conv[0], ((0, 8 - w_conv.shape[1]), (0, 0)))
    first = [(shards["in"], True), (conv_pad, False)]
    later = {"mid": ("brf", "brd", "out"), "late": ("up", "down")}
    groups = {key: [(shards[n], True) for n in members] for key, members in later.items()}
    (started_first, *started_later), token = _gather_start([first, *groups.values()], "gather_start")
    started = dict(zip(later, started_later))
    token, w["w_in"], m["w_in"], v["w_in"] = lax.optimization_barrier((token, w["w_in"], m["w_in"], v["w_in"]))
    w2, m2, v2 = ({n: a[n].reshape(a[n].shape[-2], a[n].shape[-1]) for n in BIG} for a in (w, m, v))
    early = (token, w2["w_in"], m2["w_in"], v2["w_in"])
    own_first, passed_first, token = _gather_pass(first, started_first, early, "gather_pass_in")
    land_in, land_conv = _gather_wait(first, passed_first, token, "gather_wait_in")
    wg = {"in": land_in, "bconv": b_conv,
          "conv": jnp.transpose(land_conv[:, :w_conv.shape[1], :], (1, 0, 2)).reshape(w_conv.shape[1], -1)}
    small = {n: w[n] for n in ("g_attn", "b_forget", "g_q_fox", "g_k_fox", "g_q_dil", "g_k_dil", "g_ffn")}
    small = {n: (a[0] if a.ndim == 3 else a) for n, a in small.items()}
    in_flight = {}

    def rest_pass(key, after):
        own, passed, tok = _gather_pass(groups[key], started[key], (after,), "gather_pass_" + key)
        in_flight[key] = (own, passed)
        return tok

    def rest_wait(key, after):
        own, passed = in_flight.pop(key)
        lands = _gather_wait(groups[key], passed, after, "gather_wait_" + key)
        return dict(zip(later[key], lands))

    reducer = _Reducer(jnp.stack([chip, ci.astype(jnp.int32)]))
    g_out, d_out, m_out, v_out = {}, {}, {}, {}
    reduced = {}

    def first_element(arrays):
        return jnp.stack([a[(0,) * a.ndim] for a in arrays])

    def update_big(n, deps):
        g2, dl, mn, vn = _adamw(w2[n], reduced[BIG[n]], m2[n], v2[n], name="adamw_" + n, deps=deps, emit_grad=True)
        g_out[n], d_out[n], m_out[n], v_out[n] = (a.reshape(w[n].shape) for a in (g2, dl, mn, vn))

    def update_down(tok):
        (reduced["down"],) = reducer.finish("down", tok)
        update_big("w_down", (tok,))
        return v_out["w_down"]

    small_grads, grad_x = _layer_grads(x[0], loss_target[0], small, wg, rest_pass, rest_wait, reducer, update_down)

    for key, members in (("up", ("up",)), ("mix", ("out", "brf", "brd"))):
        reduced.update(zip(members, reducer.finish(key, grad_x)))
    others = ("w_up", "w_out", "w_br_fox", "w_br_dil")
    for n in others:
        update_big(n, (grad_x,))

    flat = jnp.concatenate([small_grads[n].reshape(-1) for n in SMALL_ORDER])
    rows = _round_up(flat.shape[0], 8 * LANES) // LANES
    pack = jnp.pad(flat, (0, rows * LANES - flat.shape[0])).reshape(rows, LANES)
    packs = _gather_packs(pack, deps=(first_element([v_out[n] for n in others]),))
    total = _sum_devices(packs).reshape(-1)
    red, at = {}, 0
    for n in SMALL_ORDER:
        size = small_grads[n].size
        red[n] = total[at:at + size].reshape(small_grads[n].shape)
        at += size
    loss = red["loss"].reshape(())
    c2 = w_conv.shape[2]
    red["w_conv"] = lax.dynamic_slice_in_dim(red["w_conv"], chip * c2, c2, axis=1)

    smalls = [n for n in WEIGHT_ORDER if n not in BIG]
    for n in smalls:
        shape = w[n].shape
        r2 = (shape[-2], shape[-1]) if n not in ("g_attn", "b_forget", "g_ffn", "b_conv") else (1, shape[-1])
        g2 = red[n].reshape(r2)
        dl, mn, vn = _adamw(w[n].reshape(r2), g2, m[n].reshape(r2), v[n].reshape(r2), name="adamw_" + n)
        g_out[n], d_out[n], m_out[n], v_out[n] = (a.reshape(shape) for a in (g2, dl, mn, vn))
    tok = reducer.to_core("in", first_element([v_out[n] for n in smalls]))
    (reduced["in"],) = reducer.finish("in", tok)
    update_big("w_in", (tok,))

    return (loss, grad_x[None], *[g_out[n] for n in WEIGHT_ORDER], *[d_out[n] for n in WEIGHT_ORDER],
            *[m_out[n] for n in WEIGHT_ORDER], *[v_out[n] for n in WEIGHT_ORDER])
```

```python
import math

import jax
import jax.numpy as jnp
import numpy as np
from jax import lax
from jax.experimental import pallas as pl
from jax.experimental.pallas import tpu as pltpu

F32 = jnp.float32
BF16 = jnp.bfloat16
HEAD_DIM = 128
N_HEADS = 8
EPS = 1e-6
NEG = -1e30
LOG2E = math.log2(math.e)
N_CHIPS = 4
N_DEV = 8
LANES = 128
VMEM_LIMIT_BYTES = 56 * 1024 * 1024
DIL_PATTERNS = ((128, 1), (512, 4), (2048, 16))
ATTN_TILE = 512
MM_TILE = 1024
ADAM_LR, ADAM_B1, ADAM_B2, ADAM_EPS, ADAM_WD, ADAM_STEP = 0.001, 0.9, 0.999, 1e-08, 0.01, 10
MESH = pl.DeviceIdType.MESH


def _params(*sem):
    return pltpu.CompilerParams(dimension_semantics=sem, vmem_limit_bytes=VMEM_LIMIT_BYTES)


def _round_up(n, m):
    return -(-n // m) * m


def _pick(dim, prefs):
    for p in prefs:
        if dim % p == 0:
            return p
    raise ValueError(f"no tile for {dim} in {prefs}")


def _logical_shape(arr, kind):
    if kind is None:
        return arr.shape
    s, r, c = arr.shape
    return (r, s * c) if kind == "col" else (s * r, c)


def _spec(shape, kind, br, bc, fi, fj):
    if kind is None:
        return pl.BlockSpec((br, bc), lambda *g: (fi(*g), fj(*g)))
    _, r, c = shape
    if kind == "col":
        nb = c // bc
        assert nb * bc == c, (shape, bc)
        return pl.BlockSpec((None, br, bc), lambda *g: (fj(*g) // nb, fi(*g), fj(*g) % nb))
    nb = r // br
    assert nb * br == r, (shape, br)
    return pl.BlockSpec((None, br, bc), lambda *g: (fi(*g) // nb, fi(*g) % nb, fj(*g)))


def _mm(a, b, *, mode, tm, tn, tk, name, a_kind=None, b_kind=None, out_kind=None,
        out_dtype=F32, res=None, deps=(), loss_target=None):
    pair_a, pair_b = isinstance(a, tuple), isinstance(b, tuple)
    if pair_a or pair_b:
        return _mm_pair(a, b, mode=mode, tm=tm, tn=tn, tk=tk, name=name, b_kind=b_kind, out_kind=out_kind,
                        out_dtype=out_dtype, deps=deps)
    la, lb = _logical_shape(a, a_kind), _logical_shape(b, b_kind)
    if mode == "nn":
        (m, k), (k2, n) = la, lb
    elif mode == "nt":
        (m, k), (n, k2) = la, lb
    else:
        (k, m), (k2, n) = la, lb
    assert k == k2, (name, la, lb)
    assert m % tm == 0 and n % tn == 0 and k % tk == 0, (name, m, n, k, tm, tn, tk)
    nk = k // tk
    im = lambda i, j, l: i
    jn = lambda i, j, l: j
    lk = lambda i, j, l: l
    if mode == "tn":
        a_spec = _spec(a.shape, a_kind, tk, tm, lk, im)
        dims = (((0,), (0,)), ((), ()))
    else:
        a_spec = _spec(a.shape, a_kind, tm, tk, im, lk)
        dims = (((1,), (1,)), ((), ())) if mode == "nt" else (((1,), (0,)), ((), ()))
    if mode == "nt":
        b_spec = _spec(b.shape, b_kind, tn, tk, jn, lk)
    else:
        b_spec = _spec(b.shape, b_kind, tk, tn, lk, jn)
    if out_kind is None:
        oshape = (m, n)
    elif out_kind == "col":
        oshape = (N_CHIPS, m, n // N_CHIPS)
    else:
        oshape = (N_CHIPS, m // N_CHIPS, n)
    o_spec = _spec(oshape, out_kind, tm, tn, im, jn)
    tile = pl.BlockSpec((tm, tn), lambda i, j, l: (i, j))
    in_specs = [a_spec, b_spec]
    args = [a, b]
    for extra in (res, loss_target):
        if extra is not None:
            in_specs.append(tile)
            args.append(extra)
    in_specs += [pl.BlockSpec(memory_space=pl.ANY)] * len(deps)
    args += list(deps)
    if loss_target is None:
        out_specs, out_shape = [o_spec], [jax.ShapeDtypeStruct(oshape, out_dtype)]
    else:
        assert out_kind is None and res is not None
        out_specs = [tile, tile, pl.BlockSpec((8, LANES), lambda i, j, l: (0, 0))]
        out_shape = [jax.ShapeDtypeStruct(oshape, F32), jax.ShapeDtypeStruct(oshape, BF16),
                     jax.ShapeDtypeStruct((8, LANES), F32)]
    n_in, n_out = len(args), len(out_specs)

    def finish(out, refs, first):
        res_ref = refs[2] if res is not None else None
        outs = refs[n_in:n_in + n_out]
        if res_ref is not None:
            out = out + res_ref[...]
        if loss_target is None:
            outs[0][...] = out.astype(outs[0].dtype)
            return

        @pl.when(first)
        def _():
            outs[2][...] = jnp.zeros_like(outs[2])

        err = out - refs[3][...]
        dy = err * (1.0 / n)
        outs[0][...] = dy
        outs[1][...] = dy.astype(BF16)
        outs[2][...] += 0.5 * jnp.sum(jnp.sum(err * err, axis=-1, keepdims=True) * (1.0 / n), axis=0, keepdims=True)

    def first_tile():
        return (pl.program_id(0) == 0) & (pl.program_id(1) == 0)

    def body_whole_k(*refs):
        finish(lax.dot_general(refs[0][...], refs[1][...], dims, preferred_element_type=F32), refs, first_tile())

    def body(*refs):
        acc_ref = refs[-1]
        step = pl.program_id(2)
        first = first_tile()

        @pl.when(step == 0)
        def _():
            acc_ref[...] = jnp.zeros_like(acc_ref)

        acc_ref[...] += lax.dot_general(refs[0][...], refs[1][...], dims, preferred_element_type=F32)

        @pl.when(step == nk - 1)
        def _():
            finish(acc_ref[...], refs, first)

    outs = pl.pallas_call(
        body_whole_k if nk == 1 else body, name=name, grid=(m // tm, n // tn, nk),
        in_specs=in_specs, out_specs=out_specs, out_shape=out_shape,
        scratch_shapes=[] if nk == 1 else [pltpu.VMEM((tm, tn), F32)],
        compiler_params=_params(*(["arbitrary"] * 3 if loss_target is not None else ["parallel", "parallel", "arbitrary"])),
    )(*args)
    return outs[0] if loss_target is None else outs


def _mm_pair(a, b, *, mode, tm, tn, tk, name, b_kind, out_kind, out_dtype, deps):
    anyspec = [pl.BlockSpec(memory_space=pl.ANY)] * len(deps)
    if mode == "tn":
        assert isinstance(b, tuple) and out_kind == "col" and a.shape[0] == tk
        k, m = a.shape
        n0 = b[0].shape[1]
        n, nb0 = 2 * n0, n0 // tn
        oshape = (N_CHIPS, m, n // N_CHIPS)

        def body(a_ref, b0_ref, b1_ref, *rest):
            o_ref = rest[-1]
            for first, b_ref in ((True, b0_ref), (False, b1_ref)):
                @pl.when((pl.program_id(1) < nb0) == first)
                def _():
                    o_ref[...] = lax.dot_general(a_ref[...], b_ref[...], (((0,), (0,)), ((), ())),
                                                 preferred_element_type=F32).astype(o_ref.dtype)

        return pl.pallas_call(
            body, name=name, grid=(m // tm, n // tn),
            in_specs=[pl.BlockSpec((tk, tm), lambda i, j: (0, i)),
                      pl.BlockSpec((tk, tn), lambda i, j: (0, jnp.minimum(j, nb0 - 1))),
                      pl.BlockSpec((tk, tn), lambda i, j: (0, jnp.maximum(j - nb0, 0)))] + anyspec,
            out_specs=_spec(oshape, "col", tm, tn, lambda i, j: i, lambda i, j: j),
            out_shape=jax.ShapeDtypeStruct(oshape, out_dtype), compiler_params=_params("parallel", "arbitrary"),
        )(a, *b, *deps)
    assert mode == "nt" and isinstance(a, tuple) and out_kind is None
    m, k0 = a[0].shape
    n = _logical_shape(b, b_kind)[0]
    nk0 = k0 // tk
    nk = 2 * nk0

    def body(a0_ref, a1_ref, b_ref, *rest):
        o_ref, acc_ref = rest[-2], rest[-1]
        step = pl.program_id(2)

        @pl.when(step == 0)
        def _():
            acc_ref[...] = jnp.zeros_like(acc_ref)

        for first, a_ref in ((True, a0_ref), (False, a1_ref)):
            @pl.when((step < nk0) == first)
            def _():
                acc_ref[...] += lax.dot_general(a_ref[...], b_ref[...], (((1,), (1,)), ((), ())), preferred_element_type=F32)

        @pl.when(step == nk - 1)
        def _():
            o_ref[...] = acc_ref[...].astype(o_ref.dtype)

    return pl.pallas_call(
        body, name=name, grid=(m // tm, n // tn, nk),
        in_specs=[pl.BlockSpec((tm, tk), lambda i, j, l: (i, jnp.minimum(l, nk0 - 1))),
                  pl.BlockSpec((tm, tk), lambda i, j, l: (i, jnp.maximum(l - nk0, 0))),
                  _spec(b.shape, b_kind, tn, tk, lambda i, j, l: j, lambda i, j, l: l)] + anyspec,
        out_specs=pl.BlockSpec((tm, tn), lambda i, j, l: (i, j)),
        out_shape=jax.ShapeDtypeStruct((m, n), out_dtype), scratch_shapes=[pltpu.VMEM((tm, tn), F32)],
        compiler_params=_params("parallel", "parallel", "arbitrary"),
    )(*a, b, *deps)


def _pieces(splits, cs, cp):
    out, g0 = [], 0
    for width in splits:
        g1, runs = g0 + width, []
        for j in range(N_CHIPS):
            a, b = max(g0, cs * j), min(g1, cs * (j + 1))
            if a < b:
                runs.append((j * cp + a - cs * j, a - g0, b - a))
        out.append(runs)
        g0 = g1
    return out


def _head_norm(xv, gv):
    r = lax.rsqrt(jnp.mean(xv * xv, axis=-1, keepdims=True) + EPS)
    return (xv * r) * gv


def _head_norm_bwd(dyv, xv, gv):
    r = lax.rsqrt(jnp.mean(xv * xv, axis=-1, keepdims=True) + EPS)
    xr = xv * r
    gdy = dyv * gv
    return r * (gdy - xr * jnp.mean(gdy * xr, axis=-1, keepdims=True)), jnp.sum(dyv * xr, axis=0, keepdims=True)


def _proj_split(proj_p, splits, cs, dtypes, gains, tm=128):
    s, wp = proj_p.shape
    pieces = _pieces(splits, cs, wp // N_CHIPS)
    normed = sorted(gains)
    nseg = len(splits)

    def body(p_ref, *refs):
        g_refs, o_refs, n_refs = refs[:len(normed)], refs[len(normed):len(normed) + nseg], refs[len(normed) + nseg:]
        for o_ref, runs in zip(o_refs, pieces):
            for src, dst, n in runs:
                o_ref[:, dst:dst + n] = p_ref[:, src:src + n].astype(o_ref.dtype)
        for g_ref, n_ref, i in zip(g_refs, n_refs, normed):
            for c0 in range(0, splits[i], HEAD_DIM):
                cols = slice(c0, c0 + HEAD_DIM)
                n_ref[:, cols] = _head_norm(o_refs[i][:, cols], g_ref[:, cols]).astype(n_ref.dtype)

    return pl.pallas_call(
        body, name="proj_split", grid=(s // tm,),
        in_specs=[pl.BlockSpec((tm, wp), lambda i: (i, 0))] + [pl.BlockSpec((1, splits[i]), lambda i: (0, 0)) for i in normed],
        out_specs=[pl.BlockSpec((tm, w), lambda i: (i, 0)) for w in splits]
        + [pl.BlockSpec((tm, splits[i]), lambda i: (i, 0)) for i in normed],
        out_shape=[jax.ShapeDtypeStruct((s, w), dt) for w, dt in zip(splits, dtypes)]
        + [jax.ShapeDtypeStruct((s, splits[i]), BF16) for i in normed],
        compiler_params=_params("parallel"),
    )(proj_p, *[gains[i] for i in normed])


def _dproj_merge(parts, splits, cs, cp, norms, tm=128):
    s = parts[0].shape[0]
    wp = N_CHIPS * cp
    pieces = _pieces(splits, cs, cp)
    normed = sorted(norms)
    nseg, nn = len(splits), len(normed)

    def body(*refs):
        p_refs, x_refs, g_refs = refs[:nseg], refs[nseg:nseg + nn], refs[nseg + nn:nseg + 2 * nn]
        o_ref, dg_refs = refs[nseg + 2 * nn], refs[nseg + 2 * nn + 1:nseg + 3 * nn + 1]
        stage, tmp = refs[-2], refs[-1]

        @pl.when(pl.program_id(0) == 0)
        def _():
            for dg_ref in dg_refs:
                dg_ref[...] = jnp.zeros_like(dg_ref)

        for j in range(N_CHIPS):
            stage[:, j * cp + cs:(j + 1) * cp] = jnp.zeros((tm, cp - cs), F32)
        for i, (p_ref, runs) in enumerate(zip(p_refs, pieces)):
            src_ref = p_ref
            if i in norms:
                k = normed.index(i)
                for c0 in range(0, splits[i], HEAD_DIM):
                    cols = slice(c0, c0 + HEAD_DIM)
                    dx, dg = _head_norm_bwd(p_ref[:, cols].astype(F32), x_refs[k][:, cols], g_refs[k][:, cols])
                    tmp[:, cols] = dx
                    dg_refs[k][:, cols] += dg
                src_ref = tmp
            for dst, src, n in runs:
                stage[:, dst:dst + n] = src_ref[:, src:src + n].astype(F32)
        o_ref[...] = stage[...].astype(o_ref.dtype)

    wmax = max(splits[i] for i in normed)
    row = lambda w: pl.BlockSpec((tm, w), lambda i: (i, 0))
    vec = lambda w: pl.BlockSpec((1, w), lambda i: (0, 0))
    outs = pl.pallas_call(
        body, name="dproj_merge", grid=(s // tm,),
        in_specs=[row(w) for w in splits] + [row(splits[i]) for i in normed] + [vec(splits[i]) for i in normed],
        out_specs=[row(wp)] + [vec(splits[i]) for i in normed],
        out_shape=[jax.ShapeDtypeStruct((s, wp), BF16)] + [jax.ShapeDtypeStruct((1, splits[i]), F32) for i in normed],
        scratch_shapes=[pltpu.VMEM((tm, wp), F32), pltpu.VMEM((tm, wmax), F32)],
        compiler_params=_params("arbitrary"),
    )(*parts, *[norms[i][0] for i in normed], *[norms[i][1] for i in normed])
    return outs[0], dict(zip(normed, outs[1:]))


def _norm_fwd(x, g, *, group, name, tm=256):
    s, w = x.shape
    ng = w // group

    def body(x_ref, g_ref, o_ref):
        for i in range(ng):
            cols = slice(i * group, (i + 1) * group)
            xv = x_ref[:, cols]
            r = lax.rsqrt(jnp.mean(xv * xv, axis=-1, keepdims=True) + EPS)
            o_ref[:, cols] = ((xv * r) * g_ref[:, cols]).astype(o_ref.dtype)

    return pl.pallas_call(
        body, name=name, grid=(s // tm,),
        in_specs=[pl.BlockSpec((tm, w), lambda i: (i, 0)), pl.BlockSpec((1, w), lambda i: (0, 0))],
        out_specs=pl.BlockSpec((tm, w), lambda i: (i, 0)),
        out_shape=jax.ShapeDtypeStruct((s, w), BF16),
        compiler_params=_params("parallel"),
    )(x, g)


def _norm_bwd(dy, x, g, *, group, name, res=None, out_dtypes=(BF16,), tm=256, deps=()):
    s, w = x.shape
    ng = w // group
    n_in = 4 if res is not None else 3

    def body(*refs):
        dy_ref, x_ref, g_ref = refs[:3]
        res_ref = refs[3] if res is not None else None
        outs = refs[n_in + len(deps):]
        dx_refs, dg_ref = outs[:-1], outs[-1]

        @pl.when(pl.program_id(0) == 0)
        def _():
            dg_ref[...] = jnp.zeros_like(dg_ref)

        for i in range(ng):
            cols = slice(i * group, (i + 1) * group)
            xv = x_ref[:, cols]
            dyv = dy_ref[:, cols].astype(F32)
            r = lax.rsqrt(jnp.mean(xv * xv, axis=-1, keepdims=True) + EPS)
            xr = xv * r
            dg_ref[:, cols] += jnp.sum(dyv * xr, axis=0, keepdims=True)
            gdy = dyv * g_ref[:, cols]
            dx = r * (gdy - xr * jnp.mean(gdy * xr, axis=-1, keepdims=True))
            if res_ref is not None:
                dx = dx + res_ref[:, cols]
            for dx_ref in dx_refs:
                dx_ref[:, cols] = dx.astype(dx_ref.dtype)

    row = pl.BlockSpec((tm, w), lambda i: (i, 0))
    vec = pl.BlockSpec((1, w), lambda i: (0, 0))
    in_specs = [row, row, vec] + ([row] if res is not None else []) + [pl.BlockSpec(memory_space=pl.ANY)] * len(deps)
    args = [dy, x, g] + ([res] if res is not None else []) + list(deps)
    out_specs = [row] * len(out_dtypes) + [vec]
    out_shape = [jax.ShapeDtypeStruct((s, w), dt) for dt in out_dtypes] + [jax.ShapeDtypeStruct((1, w), F32)]
    return pl.pallas_call(
        body, name=name, grid=(s // tm,), in_specs=in_specs, out_specs=out_specs,
        out_shape=out_shape, compiler_params=_params("arbitrary"),
    )(*args)


def _split3(v):
    p1 = v.astype(BF16)
    r1 = v - p1.astype(F32)
    p2 = r1.astype(BF16)
    p3 = (r1 - p2.astype(F32)).astype(BF16)
    return p1, p2, p3


def _tri_sum(v, reverse, tcol=512):
    h, s = v.shape
    tcol = min(tcol, s)
    parts = _split3(v)
    outs = []
    for j in range(s // tcol):
        src = lax.broadcasted_iota(jnp.int32, (s, tcol), 0)
        dst = lax.broadcasted_iota(jnp.int32, (s, tcol), 1) + j * tcol
        keep = (src >= dst) if reverse else (src <= dst)
        tri = jnp.where(keep, 1.0, 0.0).astype(BF16)
        acc = jnp.zeros((h, tcol), F32)
        for p in parts:
            acc = acc + jnp.dot(p, tri, preferred_element_type=F32)
        outs.append(acc)
    return outs


def _forget_fwd(fa_t, b):
    h, s = fa_t.shape
    tcol = min(512, s)

    def body(f_ref, b_ref, c_ref):
        z = f_ref[...] + b_ref[...]
        logf = jnp.minimum(z, 0.0) - jnp.log(1.0 + jnp.exp(-jnp.abs(z)))
        for j, blk in enumerate(_tri_sum(logf, reverse=False, tcol=tcol)):
            c_ref[:, j * tcol:(j + 1) * tcol] = blk

    return pl.pallas_call(
        body, name="forget_fwd", out_shape=jax.ShapeDtypeStruct((h, s), F32),
        compiler_params=_params(),
    )(fa_t, b)


def _forget_bwd(dacol, fa_t, b):
    h, s = fa_t.shape
    tcol = min(512, s)

    def body(d_ref, f_ref, b_ref, dfa_ref, db_ref):
        z = f_ref[...] + b_ref[...]
        dc = -d_ref[...]
        total = jnp.zeros((h, 1), F32)
        for j, blk in enumerate(_tri_sum(dc, reverse=True, tcol=tcol)):
            cols = slice(j * tcol, (j + 1) * tcol)
            dfa = blk * (1.0 - jax.nn.sigmoid(z[:, cols]))
            dfa_ref[:, cols] = dfa
            total = total + jnp.sum(dfa, axis=-1, keepdims=True)
        db_ref[...] = total

    return pl.pallas_call(
        body, name="forget_bwd",
        out_shape=[jax.ShapeDtypeStruct((h, s), F32), jax.ShapeDtypeStruct((h, 1), F32)],
        compiler_params=_params(),
    )(dacol, fa_t, b)


def _distance_bias(s, tile, dilated):
    nb = s // tile
    b = lax.broadcasted_iota(jnp.int32, (nb, tile, tile), 0)
    dist = b * tile + lax.broadcasted_iota(jnp.int32, (nb, tile, tile), 1) - lax.broadcasted_iota(jnp.int32, (nb, tile, tile), 2)
    if not dilated:
        return jnp.where(dist >= 0, 0.0, NEG).astype(F32)
    mult = jnp.zeros(dist.shape, jnp.int32)
    for window, dil in DIL_PATTERNS:
        mult = mult + ((dist >= 0) & (dist <= window) & ((dist & (dil - 1)) == 0)).astype(jnp.int32)
    logm = jnp.where(mult == 3, math.log2(3.0), jnp.where(mult == 2, 1.0, 0.0))
    return jnp.where(mult > 0, logm, NEG).astype(F32)


def _logits(q, k, arow, acol, bias):
    s = lax.dot_general(q, k, (((1,), (1,)), ((), ())), preferred_element_type=F32)
    return s * (LOG2E / math.sqrt(HEAD_DIM)) + arow - acol + bias


def _attn_fwd(q, k, v, arow, acol, *, dilated, name, tq=ATTN_TILE, tk=ATTN_TILE):
    two_term = not dilated
    s, w = q.shape
    nh = w // HEAD_DIM
    assert tq == tk
    tq = tk = min(tq, s)
    nq, nk = s // tq, s // tk

    pairs = [(i, j) for i in range(nq) for j in range(i + 1)]
    q_of, k_of = (jnp.asarray(t, jnp.int32) for t in zip(*pairs))

    def body(qo_ref, ko_ref, q_ref, k_ref, v_ref, ar_ref, ac_ref, b_ref, o_ref, of_ref, lse_ref, m_ref, l_ref, acc_ref):
        t = pl.program_id(1)
        qi, ki = qo_ref[t], ko_ref[t]

        @pl.when(ki == 0)
        def _():
            m_ref[...] = jnp.full_like(m_ref, NEG)
            l_ref[...] = jnp.zeros_like(l_ref)
            acc_ref[...] = jnp.zeros_like(acc_ref)

        sc = _logits(q_ref[...], k_ref[...], ar_ref[...], ac_ref[...], b_ref[...])
        m_new = jnp.maximum(m_ref[...], jnp.max(sc, axis=-1, keepdims=True))
        alpha = jnp.exp2(m_ref[...] - m_new)
        p = jnp.exp2(sc - m_new)
        l_ref[...] = alpha * l_ref[...] + jnp.sum(p, axis=-1, keepdims=True)
        p_hi = p.astype(BF16)
        vv = v_ref[...]
        pv = jnp.dot(p_hi, vv, preferred_element_type=F32)
        if two_term:
            pv = pv + jnp.dot((p - p_hi.astype(F32)).astype(BF16), vv, preferred_element_type=F32)
        acc_ref[...] = alpha * acc_ref[...] + pv
        m_ref[...] = m_new

        @pl.when(ki == qi)
        def _():
            out = acc_ref[...] / l_ref[...]
            o_ref[...] = out.astype(o_ref.dtype)
            of_ref[...] = out
            lse_ref[...] = m_ref[...] + jnp.log2(l_ref[...])

    qs = pl.BlockSpec((tq, HEAD_DIM), lambda h, t, qo, ko: (qo[t], h))
    kv = pl.BlockSpec((tk, HEAD_DIM), lambda h, t, qo, ko: (ko[t], h))
    rowv = pl.BlockSpec((None, tq, 1), lambda h, t, qo, ko: (h, qo[t], 0))
    return pl.pallas_call(
        body, name=name,
        grid_spec=pltpu.PrefetchScalarGridSpec(
            num_scalar_prefetch=2, grid=(nh, len(pairs)),
            in_specs=[qs, kv, kv, rowv,
                      pl.BlockSpec((None, 1, tk), lambda h, t, qo, ko: (h, 0, ko[t])),
                      pl.BlockSpec((None, tq, tk), lambda h, t, qo, ko: (qo[t] - ko[t], 0, 0))],
            out_specs=[qs, qs, rowv],
            scratch_shapes=[pltpu.VMEM((tq, 1), F32), pltpu.VMEM((tq, 1), F32), pltpu.VMEM((tq, HEAD_DIM), F32)]),
        out_shape=[jax.ShapeDtypeStruct((s, w), BF16), jax.ShapeDtypeStruct((s, w), F32),
                   jax.ShapeDtypeStruct((nh, s, 1), F32)],
        compiler_params=_params("parallel", "arbitrary"),
    )(q_of, k_of, q, k, v, arow * LOG2E, acol * LOG2E, _distance_bias(s, tq, dilated))


def _attn_bwd(q, k, v, o, do, lse, arow, acol, *, dilated, name, tq=ATTN_TILE, tk=ATTN_TILE):
    s, w = q.shape
    nh = w // HEAD_DIM
    assert tq == tk
    tq = tk = min(tq, s)
    nq, nk = s // tq, s // tk
    scale = 1.0 / math.sqrt(HEAD_DIM)

    pairs = [(i, j) for j in range(nk) for i in range(j, nq)]
    q_of, k_of = (jnp.asarray(t, jnp.int32) for t in zip(*pairs))

    def body(qo_ref, ko_ref, q_ref, k_ref, v_ref, o_ref, do_ref, lse_ref, ar_ref, ac_ref, b_ref,
             dq_ref, dk_ref, dv_ref, dac_ref, dk_acc, dv_acc, dac_acc):
        t = pl.program_id(1)
        qi, ki = qo_ref[t], ko_ref[t]

        @pl.when(t == 0)
        def _():
            dq_ref[...] = jnp.zeros_like(dq_ref)

        @pl.when(qi == ki)
        def _():
            dk_acc[...] = jnp.zeros_like(dk_acc)
            dv_acc[...] = jnp.zeros_like(dv_acc)
            dac_acc[...] = jnp.zeros_like(dac_acc)

        qv, kvv, dov = q_ref[...], k_ref[...], do_ref[...]
        sc = _logits(qv, kvv, ar_ref[...], ac_ref[...], b_ref[...])
        p = jnp.exp2(sc - lse_ref[...])
        dp = lax.dot_general(dov, v_ref[...], (((1,), (1,)), ((), ())), preferred_element_type=F32)
        delta = jnp.sum(dov.astype(F32) * o_ref[...].astype(F32), axis=-1, keepdims=True)
        ds = p * (dp - delta)
        dsb = ds.astype(BF16)
        dv_acc[...] += lax.dot_general(p.astype(BF16), dov, (((0,), (0,)), ((), ())), preferred_element_type=F32)
        dk_acc[...] += lax.dot_general(dsb, qv, (((0,), (0,)), ((), ())), preferred_element_type=F32)
        rows = pl.ds(pl.multiple_of(qi * tq, tq), tq)
        dq_ref[rows, :] += jnp.dot(dsb, kvv, preferred_element_type=F32) * scale
        dac_acc[...] += jnp.sum(ds, axis=0, keepdims=True)

        @pl.when(qi == nq - 1)
        def _():
            dk_ref[...] = dk_acc[...] * scale
            dv_ref[...] = dv_acc[...]
            dac_ref[...] = dac_acc[...]

    qs = pl.BlockSpec((tq, HEAD_DIM), lambda h, t, qo, ko: (qo[t], h))
    ks = pl.BlockSpec((tk, HEAD_DIM), lambda h, t, qo, ko: (ko[t], h))
    rowv = pl.BlockSpec((None, tq, 1), lambda h, t, qo, ko: (h, qo[t], 0))
    colv = pl.BlockSpec((None, 1, tk), lambda h, t, qo, ko: (h, 0, ko[t]))
    return pl.pallas_call(
        body, name=name,
        grid_spec=pltpu.PrefetchScalarGridSpec(
            num_scalar_prefetch=2, grid=(nh, len(pairs)),
            in_specs=[qs, ks, ks, qs, qs, rowv, rowv, colv,
                      pl.BlockSpec((None, tq, tk), lambda h, t, qo, ko: (qo[t] - ko[t], 0, 0))],
            out_specs=[pl.BlockSpec((s, HEAD_DIM), lambda h, t, qo, ko: (0, h)), ks, ks, colv],
            scratch_shapes=[pltpu.VMEM((tk, HEAD_DIM), F32), pltpu.VMEM((tk, HEAD_DIM), F32), pltpu.VMEM((1, tk), F32)]),
        out_shape=[jax.ShapeDtypeStruct((s, w), F32), jax.ShapeDtypeStruct((s, w), F32),
                   jax.ShapeDtypeStruct((s, w), F32), jax.ShapeDtypeStruct((nh, 1, s), F32)],
        compiler_params=_params("arbitrary", "arbitrary"),
    )(q_of, k_of, q, k, v, o, do, lse, arow * LOG2E, acol * LOG2E, _distance_bias(s, tq, dilated))


def _gate_fwd(ga, gb, pa, pb, tm=256):
    s, d = ga.shape

    def body(ga_ref, gb_ref, pa_ref, pb_ref, o_ref):
        o_ref[...] = (jax.nn.sigmoid(ga_ref[...]) * pa_ref[...]
                      + jax.nn.sigmoid(gb_ref[...]) * pb_ref[...]).astype(o_ref.dtype)

    row = pl.BlockSpec((tm, d), lambda i: (i, 0))
    return pl.pallas_call(
        body, name="gate_fwd", grid=(s // tm,), in_specs=[row] * 4, out_specs=row,
        out_shape=jax.ShapeDtypeStruct((s, d), BF16), compiler_params=_params("parallel"),
    )(ga, gb, pa, pb)


def _gate_bwd(dm, ga, gb, pa, pb, tm=256):
    s, d = ga.shape

    def body(dm_ref, ga_ref, gb_ref, pa_ref, pb_ref, dpa_ref, dpb_ref, dga_ref, dgb_ref):
        dmv = dm_ref[...]
        for g_ref, p_ref, dp_ref, dg_ref in ((ga_ref, pa_ref, dpa_ref, dga_ref), (gb_ref, pb_ref, dpb_ref, dgb_ref)):
            sg = jax.nn.sigmoid(g_ref[...])
            dp_ref[...] = (dmv * sg).astype(BF16)
            dg_ref[...] = (dmv * p_ref[...] * (sg * (1.0 - sg))).astype(BF16)

    row = pl.BlockSpec((tm, d), lambda i: (i, 0))
    return pl.pallas_call(
        body, name="gate_bwd", grid=(s // tm,), in_specs=[row] * 5, out_specs=[row] * 4,
        out_shape=[jax.ShapeDtypeStruct((s, d), BF16)] * 4, compiler_params=_params("parallel"),
    )(dm, ga, gb, pa, pb)


def _shift_down(u, k):
    row = lax.broadcasted_iota(jnp.int32, u.shape, 0)
    return jnp.where(row >= k, pltpu.roll(u, k, 0), 0.0)


def _shift_up(u, k):
    n = u.shape[0]
    row = lax.broadcasted_iota(jnp.int32, u.shape, 0)
    return jnp.where(row < n - k, pltpu.roll(u, n - k, 0), 0.0)


def _conv3(u, wc, b):
    return wc[0:1, :] * _shift_down(u, 2) + wc[1:2, :] * _shift_down(u, 1) + wc[2:3, :] * u + b


def _conv_glu_fwd(u, wc, b, tn=256):
    s, f2 = u.shape
    f = f2 // 2
    nb = f // tn

    def body(ug_ref, uv_ref, wg_ref, wv_ref, bg_ref, bv_ref, o_ref):
        cg = _conv3(ug_ref[...], wg_ref[...], bg_ref[...])
        cv = _conv3(uv_ref[...], wv_ref[...], bv_ref[...])
        o_ref[...] = (cg * jax.nn.sigmoid(cg) * cv).astype(o_ref.dtype)

    def cols(rows, off):
        return pl.BlockSpec((rows, tn), lambda j: (0, j + off))

    return pl.pallas_call(
        body, name="conv_glu_fwd", grid=(nb,),
        in_specs=[cols(s, 0), cols(s, nb), cols(3, 0), cols(3, nb), cols(1, 0), cols(1, nb)],
        out_specs=cols(s, 0), out_shape=jax.ShapeDtypeStruct((s, f), BF16),
        compiler_params=_params("parallel"),
    )(u, u, wc, wc, b, b)


def _conv_glu_bwd(u, da, wc, b, tn=256):
    s, f2 = u.shape
    f = f2 // 2
    nb = f // tn

    def body(ug_ref, uv_ref, da_ref, wg_ref, wv_ref, bg_ref, bv_ref, dug_ref, duv_ref, sg_ref, sv_ref):
        ug, uv, wg, wv = ug_ref[...], uv_ref[...], wg_ref[...], wv_ref[...]
        cg = _conv3(ug, wg, bg_ref[...])
        cv = _conv3(uv, wv, bv_ref[...])
        sig = jax.nn.sigmoid(cg)
        dav = da_ref[...]
        dcv = dav * (cg * sig)
        dcg = dav * cv * (sig * (1.0 + cg * (1.0 - sig)))
        for dc, uu, w, du_ref, st_ref in ((dcg, ug, wg, dug_ref, sg_ref), (dcv, uv, wv, duv_ref, sv_ref)):
            du = w[2:3, :] * dc + w[1:2, :] * _shift_up(dc, 1) + w[0:1, :] * _shift_up(dc, 2)
            du_ref[...] = du.astype(BF16)
            st_ref[...] = jnp.zeros_like(st_ref)
            st_ref[0:1, :] = jnp.sum(dc * _shift_down(uu, 2), axis=0, keepdims=True)
            st_ref[1:2, :] = jnp.sum(dc * _shift_down(uu, 1), axis=0, keepdims=True)
            st_ref[2:3, :] = jnp.sum(dc * uu, axis=0, keepdims=True)
            st_ref[3:4, :] = jnp.sum(dc, axis=0, keepdims=True)

    def cols(rows, off):
        return pl.BlockSpec((rows, tn), lambda j: (0, j + off))

    return pl.pallas_call(
        body, name="conv_glu_bwd", grid=(nb,),
        in_specs=[cols(s, 0), cols(s, nb), cols(s, 0), cols(3, 0), cols(3, nb), cols(1, 0), cols(1, nb)],
        out_specs=[cols(s, 0), cols(s, 0), cols(8, 0), cols(8, 0)],
        out_shape=[jax.ShapeDtypeStruct((s, f), BF16), jax.ShapeDtypeStruct((s, f), BF16),
                   jax.ShapeDtypeStruct((8, f), F32), jax.ShapeDtypeStruct((8, f), F32)],
        compiler_params=_params("parallel"),
    )(u, u, da, wc, wc, b, b)


ROW_TILES = (256, 128, 64, 32, 16, 8)
BLOCK_BYTES = 2 << 20


def _add_halves(g, r1, place):
    ns, r, c = g.shape
    rh = r // 2
    tr = _pick(rh, ROW_TILES)
    g4 = g.reshape(ns, 2, rh, c)

    def body(p_ref, g_ref, r_ref, o_ref):
        o_ref[...] = (g_ref[...].astype(F32) + r_ref[...].astype(F32)).astype(o_ref.dtype)

    def slab(s, pr):
        return s + (s >= pr[0]).astype(jnp.int32)

    return pl.pallas_call(
        body, name="add_halves",
        grid_spec=pltpu.PrefetchScalarGridSpec(
            num_scalar_prefetch=1, grid=(ns - 1, rh // tr),
            in_specs=[pl.BlockSpec((None, None, tr, c), lambda s, i, pr: (slab(s, pr), pr[1], i, 0)),
                      pl.BlockSpec((None, tr, c), lambda s, i, pr: (slab(s, pr), i, 0))],
            out_specs=pl.BlockSpec((None, tr, c), lambda s, i, pr: (slab(s, pr), i, 0))),
        out_shape=jax.ShapeDtypeStruct((ns, rh, c), BF16),
        compiler_params=_params("parallel", "parallel"),
    )(place, g4, r1)


def _sum_chips(g, r1, recv, place):
    ns, r, c = g.shape
    rh = r // 2
    tr = _pick(rh, ROW_TILES)
    g4 = g.reshape(ns, 2, rh, c)

    def body(p_ref, g_ref, r_ref, t0_ref, t1_ref, t2_ref, o_ref):
        own = g_ref[...].astype(F32) + r_ref[...].astype(F32)
        o_ref[...] = ((own + t0_ref[...].astype(F32)) + t1_ref[...].astype(F32)) + t2_ref[...].astype(F32)

    def peer(k):
        return pl.BlockSpec((None, tr, c), lambda i, pr: (k, i, 0))

    return pl.pallas_call(
        body, name="sum_chips",
        grid_spec=pltpu.PrefetchScalarGridSpec(
            num_scalar_prefetch=1, grid=(rh // tr,),
            in_specs=[pl.BlockSpec((None, None, tr, c), lambda i, pr: (pr[0], pr[1], i, 0)),
                      pl.BlockSpec((None, tr, c), lambda i, pr: (pr[0], i, 0)), peer(0), peer(1), peer(2)],
            out_specs=pl.BlockSpec((tr, c), lambda i, pr: (pr[1] * (rh // tr) + i, 0))),
        out_shape=jax.ShapeDtypeStruct((r, c), F32),
        compiler_params=_params("parallel"),
    )(place, g4, r1, recv, recv, recv)


def _sum_devices(packs, own, me):
    n, r, c = packs.shape

    def body(me_ref, p_ref, own_ref, o_ref):
        acc = jnp.zeros((r, c), F32)
        for d in range(n):
            acc = acc + jnp.where(me_ref[0] == d, own_ref[...], p_ref[d])
        o_ref[...] = acc

    vmem = pl.BlockSpec(memory_space=pltpu.VMEM)
    return pl.pallas_call(
        body, name="sum_devices", in_specs=[pl.BlockSpec(memory_space=pltpu.SMEM), vmem, vmem], out_specs=vmem,
        out_shape=jax.ShapeDtypeStruct((r, c), F32), compiler_params=_params(),
    )(me, packs, own)


def _adamw_update(wv, gv, mv, vv):
    c1 = 1.0 - ADAM_B1 ** ADAM_STEP
    c2 = 1.0 - ADAM_B2 ** ADAM_STEP
    mn = ADAM_B1 * mv + (1.0 - ADAM_B1) * gv
    vn = ADAM_B2 * vv + (1.0 - ADAM_B2) * (gv * gv)
    m_hat = mn / c1
    v_hat = vn / c2
    return -ADAM_LR * (m_hat / (jnp.sqrt(v_hat) + ADAM_EPS) + ADAM_WD * wv), mn, vn


def _adamw(w, g, m, v, name, deps=(), emit_grad=False):
    r, c = w.shape
    tr = _pick(r, [t for t in ROW_TILES if t * c * 4 <= BLOCK_BYTES]) if r >= 8 else r
    n_out = 4 if emit_grad else 3

    def body(w_ref, g_ref, m_ref, v_ref, *rest):
        outs = rest[-n_out:]
        gv = g_ref[:, :c]
        if emit_grad:
            outs[0][...] = gv
        outs[-3][...], outs[-2][...], outs[-1][...] = _adamw_update(w_ref[...], gv, m_ref[...], v_ref[...])

    blk = pl.BlockSpec((tr, c), lambda i: (i, 0))
    g_blk = pl.BlockSpec((tr, g.shape[1]), lambda i: (i, 0))
    return pl.pallas_call(
        body, name=name, grid=(r // tr,), in_specs=[blk, g_blk, blk, blk] + [ANY] * len(deps), out_specs=[blk] * n_out,
        out_shape=[jax.ShapeDtypeStruct((r, c), F32)] * n_out, compiler_params=_params("parallel"),
    )(w, g, m, v, *deps)


ANY = pl.BlockSpec(memory_space=pl.ANY)


def _place():
    x, y, c = lax.axis_index("x"), lax.axis_index("y"), lax.axis_index("c")
    chips = [(1 - x, y), (x, 1 - y), (1 - x, 1 - y)]
    return x, y, c, chips


def _remote(src, dst, send_sem, recv_sem, to):
    return pltpu.make_async_remote_copy(src_ref=src, dst_ref=dst, send_sem=send_sem, recv_sem=recv_sem,
                                        device_id=to, device_id_type=MESH)


HBM = pl.BlockSpec(memory_space=pltpu.HBM)
SEM = pl.BlockSpec(memory_space=pltpu.SEMAPHORE)
EFFECT = pltpu.SideEffectType.DATAFLOW_SIDE_EFFECTING


def _in_hbm(a):
    return pltpu.with_memory_space_constraint(a, pltpu.HBM)


def _half(ref_rows, who):
    return pl.ds(who * (ref_rows // 2), ref_rows // 2)


def _gather_start(groups, name):
    items = [it for g in groups for it in g]
    n = len(items)
    sizes = [len(g) for g in groups]

    def body(*refs):
        srcs, lands = refs[:n], refs[n:2 * n]
        sems = refs[2 * n:2 * n + 2 * len(groups)]
        token = refs[-1]
        x, y, c, chips = _place()
        j = 2 * x + y
        at = 0
        for gi, g in enumerate(groups):
            send, recv = sems[2 * gi], sems[2 * gi + 1]
            for i, (shard, split) in enumerate(g):
                src, land = srcs[at], lands[at]
                at += 1
                rows = _half(shard.shape[0], c) if split else slice(None)
                for k, chip in enumerate(chips):
                    _remote(src.at[rows], land.at[j, rows], send.at[4 * i + k], recv.at[4 * i + k], (*chip, c)).start()
                _remote(src, land.at[j], send.at[4 * i + 3], recv.at[4 * i + 3], (x, y, 1 - c)).start()
        token[...] = jnp.zeros_like(token)

    sem_shapes = []
    for sz in sizes:
        sem_shapes += [pltpu.SemaphoreType.DMA((4 * sz,)), pltpu.SemaphoreType.DMA((4 * sz,))]
    out_shape = (sem_shapes + [pltpu.HBM(sh.shape, sh.dtype) for sh, _ in items]
                 + [pltpu.HBM((N_CHIPS,) + sh.shape, sh.dtype) for sh, _ in items]
                 + [jax.ShapeDtypeStruct((8, LANES), F32)])
    ns = len(sem_shapes)
    outs = pl.pallas_call(
        body, name=name, in_specs=[HBM] * (2 * n),
        out_specs=[SEM] * ns + [HBM] * (2 * n) + [pl.BlockSpec(memory_space=pltpu.VMEM)],
        out_shape=out_shape, input_output_aliases={i: ns + i for i in range(2 * n)},
        compiler_params=pltpu.CompilerParams(has_side_effects=EFFECT),
    )(*[_in_hbm(sh) for sh, _ in items], *[_in_hbm(lax.empty((N_CHIPS,) + sh.shape, sh.dtype)) for sh, _ in items])
    sems, shards, lands, token = outs[:ns], outs[ns:ns + n], outs[ns + n:ns + 2 * n], outs[-1]
    res, at = [], 0
    for gi, sz in enumerate(sizes):
        res.append((shards[at:at + sz], lands[at:at + sz], sems[2 * gi], sems[2 * gi + 1]))
        at += sz
    return res, token


def _gather_pass(group, started, after, name):
    shards, lands, send, recv = started
    n = len(group)
    split_ix = [i for i, (_, split) in enumerate(group) if split]

    def body(*refs):
        lnds, send1, recv1 = refs[n:2 * n], refs[2 * n], refs[2 * n + 1]
        outs = refs[2 * n + 2 + len(after):]
        send2, recv2, token = outs[2 * n], outs[2 * n + 1], outs[2 * n + 2]
        x, y, c, chips = _place()
        sib = (x, y, 1 - c)
        for i, (shard, split) in enumerate(group):
            rows = _half(shard.shape[0], c) if split else slice(None)
            for k, (cx, cy) in enumerate(chips):
                landed = lnds[i].at[2 * cx + cy, rows]
                cp = _remote(landed, landed, send1.at[4 * i + k], recv1.at[4 * i + k], sib)
                cp.wait_send()
                cp.wait_recv()
            own = lnds[i].at[2 * x + y]
            cp = _remote(own, own, send1.at[4 * i + 3], recv1.at[4 * i + 3], sib)
            cp.wait_send()
            cp.wait_recv()
        for i2, i in enumerate(split_ix):
            rows = _half(group[i][0].shape[0], c)
            for k, (cx, cy) in enumerate(chips):
                landed = lnds[i].at[2 * cx + cy, rows]
                _remote(landed, landed, send2.at[3 * i2 + k], recv2.at[3 * i2 + k], sib).start()
        token[...] = jnp.zeros_like(token)

    n2 = len(split_ix)
    out_shape = ([pltpu.HBM(a.shape, a.dtype) for a in (*shards, *lands)]
                 + [pltpu.SemaphoreType.DMA((3 * n2,)), pltpu.SemaphoreType.DMA((3 * n2,)), jax.ShapeDtypeStruct((8, LANES), F32)])
    outs = pl.pallas_call(
        body, name=name, in_specs=[HBM] * (2 * n) + [SEM, SEM] + [ANY] * len(after),
        out_specs=[HBM] * (2 * n) + [SEM, SEM, pl.BlockSpec(memory_space=pltpu.VMEM)],
        out_shape=out_shape, input_output_aliases={i: i for i in range(2 * n)},
        compiler_params=pltpu.CompilerParams(has_side_effects=EFFECT),
    )(*shards, *lands, send, recv, *after)
    return outs[:n], (outs[n:2 * n], outs[2 * n], outs[2 * n + 1]), outs[2 * n + 2]


def _gather_wait(group, passed, after, name):
    lands, send2, recv2 = passed
    n = len(group)
    split_ix = [i for i, (_, split) in enumerate(group) if split]

    def body(*refs):
        lnds, s2, r2 = refs[:n], refs[n], refs[n + 1]
        x, y, c, chips = _place()
        sib = (x, y, 1 - c)
        for i2, i in enumerate(split_ix):
            rows = _half(group[i][0].shape[0], 1 - c)
            for k, (cx, cy) in enumerate(chips):
                landed = lnds[i].at[2 * cx + cy, rows]
                cp = _remote(landed, landed, s2.at[3 * i2 + k], r2.at[3 * i2 + k], sib)
                cp.wait_send()
                cp.wait_recv()

    return pl.pallas_call(
        body, name=name, in_specs=[HBM] * n + [SEM, SEM, ANY], out_specs=[HBM] * n,
        out_shape=[pltpu.HBM(a.shape, a.dtype) for a in lands], input_output_aliases={i: i for i in range(n)},
        compiler_params=pltpu.CompilerParams(has_side_effects=EFFECT),
    )(*lands, send2, recv2, after)


def _xfer_start(name, srcs, land_shapes, n_copies, copies, after):
    n, nl = len(srcs), len(land_shapes)

    def body(*refs):
        src_refs, land_refs = refs[:n], refs[n:n + nl]
        send, recv, token = refs[n + nl + 1], refs[n + nl + 2], refs[-1]
        for cp in copies(src_refs, land_refs, send, recv):
            cp.start()
        token[...] = jnp.zeros_like(token)

    lands = [_in_hbm(lax.empty(shape, dtype)) for shape, dtype in land_shapes]
    out_shape = ([pltpu.SemaphoreType.DMA((n_copies,)), pltpu.SemaphoreType.DMA((n_copies,))]
                 + [pltpu.HBM(a.shape, a.dtype) for a in (*srcs, *lands)] + [jax.ShapeDtypeStruct((8, LANES), F32)])
    outs = pl.pallas_call(
        body, name=name, in_specs=[HBM] * (n + nl) + [ANY],
        out_specs=[SEM, SEM] + [HBM] * (n + nl) + [pl.BlockSpec(memory_space=pltpu.VMEM)],
        out_shape=out_shape, input_output_aliases={i: 2 + i for i in range(n + nl)},
        compiler_params=pltpu.CompilerParams(has_side_effects=EFFECT),
    )(*[_in_hbm(a) for a in srcs], *lands, after)
    return (outs[2:2 + n], outs[2 + n:2 + n + nl], outs[0], outs[1]), outs[-1]


def _xfer_wait(name, started, copies, after):
    srcs, lands, send, recv = started
    n, nl = len(srcs), len(lands)

    def body(*refs):
        src_refs, land_refs, s_ref, r_ref = refs[:n], refs[n:n + nl], refs[n + nl], refs[n + nl + 1]
        for cp in copies(src_refs, land_refs, s_ref, r_ref):
            cp.wait_send()
            cp.wait_recv()

    outs = pl.pallas_call(
        body, name=name, in_specs=[HBM] * (n + nl) + [SEM, SEM, ANY], out_specs=[HBM] * (n + nl),
        out_shape=[pltpu.HBM(a.shape, a.dtype) for a in (*srcs, *lands)],
        input_output_aliases={i: i for i in range(n + nl)},
        compiler_params=pltpu.CompilerParams(has_side_effects=EFFECT),
    )(*srcs, *lands, send, recv, after)
    return outs[:n], outs[n:]


def _swap_copies(srcs, lands, send, recv):
    x, y, c, _ = _place()
    return [_remote(src.at[:, _half(src.shape[1], 1 - c)], land, send.at[i], recv.at[i], (x, y, 1 - c))
            for i, (src, land) in enumerate(zip(srcs, lands))]


def _scatter_copies(srcs, lands, send, recv):
    x, y, c, chips = _place()
    return [_remote(src.at[2 * cx + cy], land.at[k], send.at[3 * i + k], recv.at[3 * i + k], (cx, cy, c))
            for i, (src, land) in enumerate(zip(srcs, lands)) for k, (cx, cy) in enumerate(chips)]


def _join_copies(srcs, lands, send, recv):
    x, y, c, _ = _place()
    return [_remote(src.at[_half(src.shape[0], c)], src.at[_half(src.shape[0], c)], send.at[i], recv.at[i], (x, y, 1 - c))
            for i, src in enumerate(srcs)]


def _corner(a):
    return a[(slice(0, 1),) * a.ndim]


class _Reducer:
    def __init__(self, place):
        self.place = place
        self.state = {}

    def swap(self, key, grads, after):
        shapes = [((g.shape[0], g.shape[1] // 2, g.shape[2]), g.dtype) for g in grads]
        self.state[key], token = _xfer_start("swap_start_" + key, grads, shapes, len(grads), _swap_copies, _corner(after))
        return token

    def to_chips(self, key, after):
        grads, from_sibling = _xfer_wait("swap_wait_" + key, self.state[key], _swap_copies, after)
        sums = [_add_halves(g, r, self.place) for g, r in zip(grads, from_sibling)]
        shapes = [((3,) + s.shape[1:], s.dtype) for s in sums]
        started, token = _xfer_start("scatter_start_" + key, sums, shapes, 3 * len(sums), _scatter_copies, _corner(sums[-1]))
        self.state[key] = (grads, from_sibling, started)
        return token

    def to_core(self, key, after):
        grads, from_sibling, started = self.state[key]
        _, from_chips = _xfer_wait("scatter_wait_" + key, started, _scatter_copies, after)
        shards = [_sum_chips(g, r, rc, self.place) for g, r, rc in zip(grads, from_sibling, from_chips)]
        self.state[key], token = _xfer_start("join_start_" + key, shards, [], len(shards), _join_copies, _corner(shards[-1]))
        return token

    def finish(self, key, after):
        return _xfer_wait("join_wait_" + key, self.state.pop(key), _join_copies, after)[0]


def _pack_copies(srcs, lands, send, recv):
    x, y, c, _ = _place()
    me = 4 * x + 2 * y + c
    return [_remote(srcs[0], lands[0].at[me], send.at[k - 1], recv.at[k - 1],
                    (x ^ ((k >> 2) & 1), y ^ ((k >> 1) & 1), c ^ (k & 1))) for k in range(1, N_DEV)]


LANE_TILES = (512, 896, 1408, 704, 384, 256, 128)


def _layer_grads(x, target, small, wg, rest_pass, rest_wait, red, filler):
    s, d = x.shape
    f = wg["conv"].shape[1] // 2
    w_att = N_HEADS * HEAD_DIM
    in_splits = (w_att, w_att, w_att, N_HEADS, w_att, w_att, w_att, d, d)
    in_cols = sum(in_splits)
    cs = in_cols // N_CHIPS
    cp = wg["in"].shape[2]
    tm = min(s, MM_TILE)
    tm_wide = min(s, MM_TILE // 2)
    t_in = cp
    t_up = 2 * f // N_CHIPS
    t_d = _pick(d, LANE_TILES)
    t_d2 = min(d, MM_TILE)
    t_dq = _pick(d // N_CHIPS, LANE_TILES)
    t_fq = _pick(f // N_CHIPS, LANE_TILES)

    h1 = _norm_fwd(x, small["g_attn"], group=d, name="rms1_fwd")
    proj_p = _mm(h1, wg["in"], mode="nn", b_kind="col", tm=tm_wide, tn=t_in, tk=d, name="mm_in")
    gains = {n: small[n].reshape(1, w_att) for n in ("g_q_fox", "g_k_fox", "g_q_dil", "g_k_dil")}
    qa, ka, va_b, fa, qb, kb, vb_b, ga, gb, qa_n, ka_n, qb_n, kb_n = _proj_split(
        proj_p, in_splits, cs, (F32, F32, BF16, F32, F32, F32, BF16, F32, F32),
        {0: gains["g_q_fox"], 1: gains["g_k_fox"], 4: gains["g_q_dil"], 5: gains["g_k_dil"]})
    fa_t = fa.T
    b_f = small["b_forget"].reshape(N_HEADS, 1)
    c_f = _forget_fwd(fa_t, b_f)
    slopes = jnp.asarray(2.0 ** (-8.0 * np.arange(1, N_HEADS + 1) / N_HEADS), dtype=F32)
    a_d = -(slopes[:, None] * jnp.arange(s, dtype=F32)[None, :])
    rows_f, cols_f = c_f[:, :, None], c_f[:, None, :]
    rows_d, cols_d = a_d[:, :, None], a_d[:, None, :]
    o_a, o_a32, lse_a = _attn_fwd(qa_n, ka_n, va_b, rows_f, cols_f, dilated=False, name="attn_fox_fwd")
    token = rest_pass("mid", o_a)
    rows_d = rows_d + token[0, 0]
    o_b, o_b32, lse_b = _attn_fwd(qb_n, kb_n, vb_b, rows_d, cols_d, dilated=True, name="attn_dil_fwd")
    wg = dict(wg, **rest_wait("mid", o_b))
    token = rest_pass("late", o_b)
    pa = _mm(o_a, wg["brf"], mode="nn", b_kind="col", tm=tm, tn=t_dq, tk=w_att, name="mm_brf", deps=(token,))
    pb = _mm(o_b, wg["brd"], mode="nn", b_kind="col", tm=tm, tn=t_dq, tk=w_att, name="mm_brd")
    merged = _gate_fwd(ga, gb, pa, pb)
    x1 = _mm(merged, wg["out"], mode="nn", b_kind="row", res=x, tm=tm, tn=t_d, tk=t_dq, name="mm_out")
    wg = dict(wg, **rest_wait("late", x1))
    h2 = _norm_fwd(x1, small["g_ffn"], group=d, name="rms2_fwd")
    u = _mm(h2, wg["up"], mode="nn", b_kind="col", tm=tm_wide, tn=t_up, tk=d, name="mm_up")
    act = _conv_glu_fwd(u, wg["conv"], wg["bconv"])
    dy_f, dy_b, loss_blk = _mm(act, wg["down"], mode="nn", b_kind="row", res=x1, loss_target=target,
                               tm=tm, tn=t_d2, tk=t_fq, name="mm_down")

    d_act = _mm(dy_b, wg["down"], mode="nt", b_kind="row", tm=tm, tn=t_fq, tk=d, name="mm_down_dx")
    g_down = _mm(act, dy_b, mode="tn", out_dtype=BF16, out_kind="row", tm=t_fq, tn=t_d2, tk=s, name="mm_down_dw")
    tok = red.swap("down", [g_down], g_down)
    du_g, du_v, st_g, st_v = _conv_glu_bwd(u, d_act, wg["conv"] + tok[0, 0], wg["bconv"])
    tok = red.to_chips("down", du_g)
    du = (du_g, du_v)
    g_up = _mm(h2, du, mode="tn", out_dtype=BF16, out_kind="col", tm=t_d2, tn=t_up // 2, tk=s, name="mm_up_dw", deps=(tok,))
    tok = red.to_core("down", g_up)
    tok2 = red.swap("up", [g_up], g_up)
    dh2 = _mm(du, wg["up"], mode="nt", b_kind="col", tm=tm, tn=t_d2, tk=t_up, name="mm_up_dx", deps=(tok, tok2))
    tok = red.to_chips("up", dh2)
    dx1_b, dx1_f, dg_ffn = _norm_bwd(dh2, x1, small["g_ffn"], group=d, res=dy_f, out_dtypes=(BF16, F32), name="rms2_bwd")
    d_merged = _mm(dx1_b, wg["out"], mode="nt", b_kind="row", tm=tm, tn=t_dq, tk=d, name="mm_out_dx", deps=(tok,))
    g_out = _mm(merged, dx1_b, mode="tn", out_dtype=BF16, out_kind="row", tm=t_dq, tn=t_d2, tk=s, name="mm_out_dw")
    dpa, dpb, dga, dgb = _gate_bwd(d_merged, ga, gb, pa, pb)
    do_a = _mm(dpa, wg["brf"], mode="nt", b_kind="col", out_dtype=BF16, tm=s, tn=w_att, tk=t_dq, name="mm_brf_dx")
    do_b = _mm(dpb, wg["brd"], mode="nt", b_kind="col", out_dtype=BF16, tm=s, tn=w_att, tk=t_dq, name="mm_brd_dx")
    g_brf = _mm(o_a, dpa, mode="tn", out_dtype=BF16, out_kind="col", tm=w_att, tn=t_dq, tk=s, name="mm_brf_dw")
    g_brd = _mm(o_b, dpb, mode="tn", out_dtype=BF16, out_kind="col", tm=w_att, tn=t_dq, tk=s, name="mm_brd_dw")
    tok = red.swap("mix", [g_out, g_brf, g_brd], g_brd)
    dqa_n, dka_n, dva, dac_a = _attn_bwd(qa_n, ka_n, va_b, o_a32, do_a, lse_a, rows_f + tok[0, 0], cols_f, dilated=False, name="attn_fox_bwd")
    tok = red.to_core("up", dqa_n)
    tok2 = red.to_chips("mix", dqa_n)
    dqb_n, dkb_n, dvb, _ = _attn_bwd(qb_n, kb_n, vb_b, o_b32, do_b, lse_b, rows_d + (tok[0, 0] + tok2[0, 0]), cols_d, dilated=True, name="attn_dil_bwd")
    tok = red.to_core("mix", dqb_n)
    dfa_t, db_f = _forget_bwd(dac_a[:, 0, :], fa_t, b_f)
    dproj_p, dgains = _dproj_merge(
        [dqa_n, dka_n, dva, dfa_t.T, dqb_n, dkb_n, dvb, dga, dgb], in_splits, cs, cp,
        {0: (qa, gains["g_q_fox"]), 1: (ka, gains["g_k_fox"]), 4: (qb, gains["g_q_dil"]), 5: (kb, gains["g_k_dil"])})
    dg_qf, dg_kf, dg_qd, dg_kd = dgains[0], dgains[1], dgains[4], dgains[5]
    g_in = _mm(h1, dproj_p, mode="tn", out_dtype=BF16, out_kind="col", tm=t_d2, tn=t_in, tk=s, name="mm_in_dw", deps=(tok,))
    tok = red.swap("in", [g_in], g_in)
    tok = red.to_chips("in", filler(tok))
    dh1 = _mm(dproj_p, wg["in"], mode="nt", b_kind="col", tm=tm, tn=t_d2, tk=t_in, name="mm_in_dx", deps=(tok,))
    grad_x, dg_attn = _norm_bwd(dh1, x, small["g_attn"], group=d, res=dx1_f, out_dtypes=(F32,), name="rms1_bwd")

    small_grads = {
        "g_attn": dg_attn, "b_forget": db_f.reshape(1, N_HEADS),
        "g_q_fox": dg_qf, "g_k_fox": dg_kf, "g_q_dil": dg_qd, "g_k_dil": dg_kd, "g_ffn": dg_ffn,
        "w_conv": jnp.concatenate([st_g[0:3], st_v[0:3]], axis=1),
        "b_conv": jnp.concatenate([st_g[3:4], st_v[3:4]], axis=1),
        "loss": loss_blk[0:1, 0:1],
    }
    return small_grads, grad_x


SMALL_ORDER = ("g_attn", "b_forget", "g_q_fox", "g_k_fox", "g_q_dil", "g_k_dil", "g_ffn", "w_conv", "b_conv", "loss")
WEIGHT_ORDER = ("g_attn", "w_in", "b_forget", "g_q_fox", "g_k_fox", "g_q_dil", "g_k_dil", "w_br_fox", "w_br_dil",
                "w_out", "g_ffn", "w_up", "w_conv", "b_conv", "w_down")
BIG = {"w_in": "in", "w_br_fox": "brf", "w_br_dil": "brd", "w_out": "out", "w_up": "up", "w_down": "down"}


def kernel(x, g_attn, w_in, b_forget, g_q_fox, g_k_fox, g_q_dil, g_k_dil, w_br_fox, w_br_dil, w_out, g_ffn, w_up, w_conv, b_conv, w_down, loss_target, m_g_attn, m_w_in, m_b_forget, m_g_q_fox, m_g_k_fox, m_g_q_dil, m_g_k_dil, m_w_br_fox, m_w_br_dil, m_w_out, m_g_ffn, m_w_up, m_w_conv, m_b_conv, m_w_down, v_g_attn, v_w_in, v_b_forget, v_g_q_fox, v_g_k_fox, v_g_q_dil, v_g_k_dil, v_w_br_fox, v_w_br_dil, v_w_out, v_g_ffn, v_w_up, v_w_conv, v_b_conv, v_w_down):
    w = dict(g_attn=g_attn, w_in=w_in, b_forget=b_forget, g_q_fox=g_q_fox, g_k_fox=g_k_fox, g_q_dil=g_q_dil,
             g_k_dil=g_k_dil, w_br_fox=w_br_fox, w_br_dil=w_br_dil, w_out=w_out, g_ffn=g_ffn, w_up=w_up,
             w_conv=w_conv, b_conv=b_conv, w_down=w_down)
    m = dict(g_attn=m_g_attn, w_in=m_w_in, b_forget=m_b_forget, g_q_fox=m_g_q_fox, g_k_fox=m_g_k_fox,
             g_q_dil=m_g_q_dil, g_k_dil=m_g_k_dil, w_br_fox=m_w_br_fox, w_br_dil=m_w_br_dil, w_out=m_w_out,
             g_ffn=m_g_ffn, w_up=m_w_up, w_conv=m_w_conv, b_conv=m_b_conv, w_down=m_w_down)
    v = dict(g_attn=v_g_attn, w_in=v_w_in, b_forget=v_b_forget, g_q_fox=v_g_q_fox, g_k_fox=v_g_k_fox,
             g_q_dil=v_g_q_dil, g_k_dil=v_g_k_dil, w_br_fox=v_w_br_fox, w_br_dil=v_w_br_dil, w_out=v_w_out,
             g_ffn=v_g_ffn, w_up=v_w_up, w_conv=v_w_conv, b_conv=v_b_conv, w_down=v_w_down)
    xi, yi, ci = lax.axis_index("x"), lax.axis_index("y"), lax.axis_index("c")
    chip = (2 * xi + yi).astype(jnp.int32)

    cs = w_in.shape[2]
    cp = _round_up(cs, LANES)
    shards = {
        "in": jnp.pad(w_in[0].astype(BF16), ((0, 0), (0, cp - cs))),
        "brf": w_br_fox[0].astype(BF16), "brd": w_br_dil[0].astype(BF16), "out": w_out[0].astype(BF16),
        "up": w_up[0].astype(BF16), "down": w_down[0].astype(BF16),
    }
    conv_pad = jnp.pad(w_conv[0], ((0, 8 - w_conv.shape[1]), (0, 0)))
    first = [(shards["in"], True), (conv_pad, False)]
    later = {"mid": ("brf", "brd", "out"), "late": ("up", "down")}
    groups = {key: [(shards[n], True) for n in members] for key, members in later.items()}
    (started_first, *started_later), token = _gather_start([first, *groups.values()], "gather_start")
    started = dict(zip(later, started_later))
    token, w["w_in"], m["w_in"], v["w_in"] = lax.optimization_barrier((token, w["w_in"], m["w_in"], v["w_in"]))
    w2, m2, v2 = ({n: a[n].reshape(a[n].shape[-2], a[n].shape[-1]) for n in BIG} for a in (w, m, v))
    early = (token, w2["w_in"], m2["w_in"], v2["w_in"])
    own_first, passed_first, token = _gather_pass(first, started_first, early, "gather_pass_in")
    land_in, land_conv = _gather_wait(first, passed_first, token, "gather_wait_in")
    wg = {"in": land_in, "bconv": b_conv,
          "conv": jnp.transpose(land_conv[:, :w_conv.shape[1], :], (1, 0, 2)).reshape(w_conv.shape[1], -1)}
    small = {n: w[n] for n in ("g_attn", "b_forget", "g_q_fox", "g_k_fox", "g_q_dil", "g_k_dil", "g_ffn")}
    small = {n: (a[0] if a.ndim == 3 else a) for n, a in small.items()}
    in_flight = {}

    def rest_pass(key, after):
        own, passed, tok = _gather_pass(groups[key], started[key], (after,), "gather_pass_" + key)
        in_flight[key] = (own, passed)
        return tok

    def rest_wait(key, after):
        own, passed = in_flight.pop(key)
        lands = _gather_wait(groups[key], passed, after, "gather_wait_" + key)
        return dict(zip(later[key], lands))

    reducer = _Reducer(jnp.stack([chip, ci.astype(jnp.int32)]))
    g_out, d_out, m_out, v_out = {}, {}, {}, {}
    reduced = {}

    def first_element(arrays):
        return jnp.stack([a[(0,) * a.ndim] for a in arrays])

    def update_big(n, deps):
        g2, dl, mn, vn = _adamw(w2[n], reduced[BIG[n]], m2[n], v2[n], name="adamw_" + n, deps=deps, emit_grad=True)
        g_out[n], d_out[n], m_out[n], v_out[n] = (a.reshape(w[n].shape) for a in (g2, dl, mn, vn))

    def update_down(tok):
        (reduced["down"],) = reducer.finish("down", tok)
        update_big("w_down", (tok,))
        return v_out["w_down"]

    small_grads, grad_x = _layer_grads(x[0], loss_target[0], small, wg, rest_pass, rest_wait, reducer, update_down)

    flat = jnp.concatenate([small_grads[n].reshape(-1) for n in SMALL_ORDER])
    rows = _round_up(flat.shape[0], 8 * LANES) // LANES
    pack = jnp.pad(flat, (0, rows * LANES - flat.shape[0])).reshape(rows, LANES)
    sent, tok_packs = _xfer_start("packs_start", [pack], [((N_DEV,) + pack.shape, F32)], N_DEV - 1, _pack_copies,
                                  _corner(pack))
    for key, members in (("up", ("up",)), ("mix", ("out", "brf", "brd"))):
        reduced.update(zip(members, reducer.finish(key, grad_x)))
    others = ("w_up", "w_out", "w_br_fox", "w_br_dil")
    for n in others:
        update_big(n, (grad_x, tok_packs))
    (own_pack,), (packs,) = _xfer_wait("packs_wait", sent, _pack_copies, first_element([v_out[n] for n in others]))
    me = (4 * xi + 2 * yi + ci).astype(jnp.int32).reshape(1)
    total = _sum_devices(packs, own_pack, me).reshape(-1)
    red, at = {}, 0
    for n in SMALL_ORDER:
        size = small_grads[n].size
        red[n] = total[at:at + size].reshape(small_grads[n].shape)
        at += size
    loss = red["loss"].reshape(())
    c2 = w_conv.shape[2]
    red["w_conv"] = lax.dynamic_slice_in_dim(red["w_conv"], chip * c2, c2, axis=1)

    smalls = [n for n in WEIGHT_ORDER if n not in BIG]
    for n in smalls:
        shape = w[n].shape
        r2 = (shape[-2], shape[-1]) if n not in ("g_attn", "b_forget", "g_ffn", "b_conv") else (1, shape[-1])
        g2 = red[n].reshape(r2)
        dl, mn, vn = _adamw(w[n].reshape(r2), g2, m[n].reshape(r2), v[n].reshape(r2), name="adamw_" + n)
        g_out[n], d_out[n], m_out[n], v_out[n] = (a.reshape(shape) for a in (g2, dl, mn, vn))
    tok = reducer.to_core("in", first_element([v_out[n] for n in smalls]))
    (reduced["in"],) = reducer.finish("in", tok)
    update_big("w_in", (tok,))

    return (loss, grad_x[None], *[g_out[n] for n in WEIGHT_ORDER], *[d_out[n] for n in WEIGHT_ORDER],
            *[m_out[n] for n in WEIGHT_ORDER], *[v_out[n] for n in WEIGHT_ORDER])
```

```python
import math

import jax
import jax.numpy as jnp
import numpy as np
from jax import lax
from jax.experimental import pallas as pl
from jax.experimental.pallas import tpu as pltpu

F32 = jnp.float32
BF16 = jnp.bfloat16
HEAD_DIM = 128
N_HEADS = 8
EPS = 1e-6
NEG = -1e30
LOG2E = math.log2(math.e)
N_CHIPS = 4
N_DEV = 8
LANES = 128
VMEM_LIMIT_BYTES = 56 * 1024 * 1024
DIL_PATTERNS = ((128, 1), (512, 4), (2048, 16))
ATTN_TILE = 512
MM_TILE = 1024
ADAM_LR, ADAM_B1, ADAM_B2, ADAM_EPS, ADAM_WD, ADAM_STEP = 0.001, 0.9, 0.999, 1e-08, 0.01, 10
MESH = pl.DeviceIdType.MESH


def _params(*sem):
    return pltpu.CompilerParams(dimension_semantics=sem, vmem_limit_bytes=VMEM_LIMIT_BYTES)


def _round_up(n, m):
    return -(-n // m) * m


def _pick(dim, prefs):
    for p in prefs:
        if dim % p == 0:
            return p
    raise ValueError(f"no tile for {dim} in {prefs}")


def _logical_shape(arr, kind):
    if kind is None:
        return arr.shape
    s, r, c = arr.shape
    return (r, s * c) if kind == "col" else (s * r, c)


def _spec(shape, kind, br, bc, fi, fj):
    if kind is None:
        return pl.BlockSpec((br, bc), lambda *g: (fi(*g), fj(*g)))
    _, r, c = shape
    if kind == "col":
        nb = c // bc
        assert nb * bc == c, (shape, bc)
        return pl.BlockSpec((None, br, bc), lambda *g: (fj(*g) // nb, fi(*g), fj(*g) % nb))
    nb = r // br
    assert nb * br == r, (shape, br)
    return pl.BlockSpec((None, br, bc), lambda *g: (fi(*g) // nb, fi(*g) % nb, fj(*g)))


def _mm(a, b, *, mode, tm, tn, tk, name, a_kind=None, b_kind=None, out_kind=None,
        out_dtype=F32, res=None, deps=(), loss_target=None):
    pair_a, pair_b = isinstance(a, tuple), isinstance(b, tuple)
    if pair_a or pair_b:
        return _mm_pair(a, b, mode=mode, tm=tm, tn=tn, tk=tk, name=name, b_kind=b_kind, out_kind=out_kind,
                        out_dtype=out_dtype, deps=deps)
    la, lb = _logical_shape(a, a_kind), _logical_shape(b, b_kind)
    if mode == "nn":
        (m, k), (k2, n) = la, lb
    elif mode == "nt":
        (m, k), (n, k2) = la, lb
    else:
        (k, m), (k2, n) = la, lb
    assert k == k2, (name, la, lb)
    assert m % tm == 0 and n % tn == 0 and k % tk == 0, (name, m, n, k, tm, tn, tk)
    nk = k // tk
    im = lambda i, j, l: i
    jn = lambda i, j, l: j
    lk = lambda i, j, l: l
    if mode == "tn":
        a_spec = _spec(a.shape, a_kind, tk, tm, lk, im)
        dims = (((0,), (0,)), ((), ()))
    else:
        a_spec = _spec(a.shape, a_kind, tm, tk, im, lk)
        dims = (((1,), (1,)), ((), ())) if mode == "nt" else (((1,), (0,)), ((), ()))
    if mode == "nt":
        b_spec = _spec(b.shape, b_kind, tn, tk, jn, lk)
    else:
        b_spec = _spec(b.shape, b_kind, tk, tn, lk, jn)
    if out_kind is None:
        oshape = (m, n)
    elif out_kind == "col":
        oshape = (N_CHIPS, m, n // N_CHIPS)
    else:
        oshape = (N_CHIPS, m // N_CHIPS, n)
    o_spec = _spec(oshape, out_kind, tm, tn, im, jn)
    tile = pl.BlockSpec((tm, tn), lambda i, j, l: (i, j))
    in_specs = [a_spec, b_spec]
    args = [a, b]
    for extra in (res, loss_target):
        if extra is not None:
            in_specs.append(tile)
            args.append(extra)
    in_specs += [pl.BlockSpec(memory_space=pl.ANY)] * len(deps)
    args += list(deps)
    if loss_target is None:
        out_specs, out_shape = [o_spec], [jax.ShapeDtypeStruct(oshape, out_dtype)]
    else:
        assert out_kind is None and res is not None
        out_specs = [tile, tile, pl.BlockSpec((8, LANES), lambda i, j, l: (0, 0))]
        out_shape = [jax.ShapeDtypeStruct(oshape, F32), jax.ShapeDtypeStruct(oshape, BF16),
                     jax.ShapeDtypeStruct((8, LANES), F32)]
    n_in, n_out = len(args), len(out_specs)

    def finish(out, refs, first):
        res_ref = refs[2] if res is not None else None
        outs = refs[n_in:n_in + n_out]
        if res_ref is not None:
            out = out + res_ref[...]
        if loss_target is None:
            outs[0][...] = out.astype(outs[0].dtype)
            return

        @pl.when(first)
        def _():
            outs[2][...] = jnp.zeros_like(outs[2])

        err = out - refs[3][...]
        dy = err * (1.0 / n)
        outs[0][...] = dy
        outs[1][...] = dy.astype(BF16)
        outs[2][...] += 0.5 * jnp.sum(jnp.sum(err * err, axis=-1, keepdims=True) * (1.0 / n), axis=0, keepdims=True)

    def first_tile():
        return (pl.program_id(0) == 0) & (pl.program_id(1) == 0)

    def body_whole_k(*refs):
        finish(lax.dot_general(refs[0][...], refs[1][...], dims, preferred_element_type=F32), refs, first_tile())

    def body(*refs):
        acc_ref = refs[-1]
        step = pl.program_id(2)
        first = first_tile()

        @pl.when(step == 0)
        def _():
            acc_ref[...] = jnp.zeros_like(acc_ref)

        acc_ref[...] += lax.dot_general(refs[0][...], refs[1][...], dims, preferred_element_type=F32)

        @pl.when(step == nk - 1)
        def _():
            finish(acc_ref[...], refs, first)

    outs = pl.pallas_call(
        body_whole_k if nk == 1 else body, name=name, grid=(m // tm, n // tn, nk),
        in_specs=in_specs, out_specs=out_specs, out_shape=out_shape,
        scratch_shapes=[] if nk == 1 else [pltpu.VMEM((tm, tn), F32)],
        compiler_params=_params(*(["arbitrary"] * 3 if loss_target is not None else ["parallel", "parallel", "arbitrary"])),
    )(*args)
    return outs[0] if loss_target is None else outs


def _mm_pair(a, b, *, mode, tm, tn, tk, name, b_kind, out_kind, out_dtype, deps):
    anyspec = [pl.BlockSpec(memory_space=pl.ANY)] * len(deps)
    if mode == "tn":
        assert isinstance(b, tuple) and out_kind == "col" and a.shape[0] == tk
        k, m = a.shape
        n0 = b[0].shape[1]
        n, nb0 = 2 * n0, n0 // tn
        oshape = (N_CHIPS, m, n // N_CHIPS)

        def body(a_ref, b0_ref, b1_ref, *rest):
            o_ref = rest[-1]
            for first, b_ref in ((True, b0_ref), (False, b1_ref)):
                @pl.when((pl.program_id(1) < nb0) == first)
                def _():
                    o_ref[...] = lax.dot_general(a_ref[...], b_ref[...], (((0,), (0,)), ((), ())),
                                                 preferred_element_type=F32).astype(o_ref.dtype)

        return pl.pallas_call(
            body, name=name, grid=(m // tm, n // tn),
            in_specs=[pl.BlockSpec((tk, tm), lambda i, j: (0, i)),
                      pl.BlockSpec((tk, tn), lambda i, j: (0, jnp.minimum(j, nb0 - 1))),
                      pl.BlockSpec((tk, tn), lambda i, j: (0, jnp.maximum(j - nb0, 0)))] + anyspec,
            out_specs=_spec(oshape, "col", tm, tn, lambda i, j: i, lambda i, j: j),
            out_shape=jax.ShapeDtypeStruct(oshape, out_dtype), compiler_params=_params("parallel", "arbitrary"),
        )(a, *b, *deps)
    assert mode == "nt" and isinstance(a, tuple) and out_kind is None
    m, k0 = a[0].shape
    n = _logical_shape(b, b_kind)[0]
    nk0 = k0 // tk
    nk = 2 * nk0

    def body(a0_ref, a1_ref, b_ref, *rest):
        o_ref, acc_ref = rest[-2], rest[-1]
        step = pl.program_id(2)

        @pl.when(step == 0)
        def _():
            acc_ref[...] = jnp.zeros_like(acc_ref)

        for first, a_ref in ((True, a0_ref), (False, a1_ref)):
            @pl.when((step < nk0) == first)
            def _():
                acc_ref[...] += lax.dot_general(a_ref[...], b_ref[...], (((1,), (1,)), ((), ())), preferred_element_type=F32)

        @pl.when(step == nk - 1)
        def _():
            o_ref[...] = acc_ref[...].astype(o_ref.dtype)

    return pl.pallas_call(
        body, name=name, grid=(m // tm, n // tn, nk),
        in_specs=[pl.BlockSpec((tm, tk), lambda i, j, l: (i, jnp.minimum(l, nk0 - 1))),
                  pl.BlockSpec((tm, tk), lambda i, j, l: (i, jnp.maximum(l - nk0, 0))),
                  _spec(b.shape, b_kind, tn, tk, lambda i, j, l: j, lambda i, j, l: l)] + anyspec,
        out_specs=pl.BlockSpec((tm, tn), lambda i, j, l: (i, j)),
        out_shape=jax.ShapeDtypeStruct((m, n), out_dtype), scratch_shapes=[pltpu.VMEM((tm, tn), F32)],
        compiler_params=_params("parallel", "parallel", "arbitrary"),
    )(*a, b, *deps)


def _pieces(splits, cs, cp):
    out, g0 = [], 0
    for width in splits:
        g1, runs = g0 + width, []
        for j in range(N_CHIPS):
            a, b = max(g0, cs * j), min(g1, cs * (j + 1))
            if a < b:
                runs.append((j * cp + a - cs * j, a - g0, b - a))
        out.append(runs)
        g0 = g1
    return out


def _head_norm(xv, gv):
    r = lax.rsqrt(jnp.mean(xv * xv, axis=-1, keepdims=True) + EPS)
    return (xv * r) * gv


def _head_norm_bwd(dyv, xv, gv):
    r = lax.rsqrt(jnp.mean(xv * xv, axis=-1, keepdims=True) + EPS)
    xr = xv * r
    gdy = dyv * gv
    return r * (gdy - xr * jnp.mean(gdy * xr, axis=-1, keepdims=True)), jnp.sum(dyv * xr, axis=0, keepdims=True)


def _proj_split(proj_p, splits, cs, dtypes, gains, tm=128):
    s, wp = proj_p.shape
    pieces = _pieces(splits, cs, wp // N_CHIPS)
    normed = sorted(gains)
    nseg = len(splits)

    def body(p_ref, *refs):
        g_refs, o_refs, n_refs = refs[:len(normed)], refs[len(normed):len(normed) + nseg], refs[len(normed) + nseg:]
        for o_ref, runs in zip(o_refs, pieces):
            for src, dst, n in runs:
                o_ref[:, dst:dst + n] = p_ref[:, src:src + n].astype(o_ref.dtype)
        for g_ref, n_ref, i in zip(g_refs, n_refs, normed):
            for c0 in range(0, splits[i], HEAD_DIM):
                cols = slice(c0, c0 + HEAD_DIM)
                n_ref[:, cols] = _head_norm(o_refs[i][:, cols], g_ref[:, cols]).astype(n_ref.dtype)

    return pl.pallas_call(
        body, name="proj_split", grid=(s // tm,),
        in_specs=[pl.BlockSpec((tm, wp), lambda i: (i, 0))] + [pl.BlockSpec((1, splits[i]), lambda i: (0, 0)) for i in normed],
        out_specs=[pl.BlockSpec((tm, w), lambda i: (i, 0)) for w in splits]
        + [pl.BlockSpec((tm, splits[i]), lambda i: (i, 0)) for i in normed],
        out_shape=[jax.ShapeDtypeStruct((s, w), dt) for w, dt in zip(splits, dtypes)]
        + [jax.ShapeDtypeStruct((s, splits[i]), BF16) for i in normed],
        compiler_params=_params("parallel"),
    )(proj_p, *[gains[i] for i in normed])


def _dproj_merge(parts, splits, cs, cp, norms, tm=128):
    s = parts[0].shape[0]
    wp = N_CHIPS * cp
    pieces = _pieces(splits, cs, cp)
    normed = sorted(norms)
    nseg, nn = len(splits), len(normed)

    def body(*refs):
        p_refs, x_refs, g_refs = refs[:nseg], refs[nseg:nseg + nn], refs[nseg + nn:nseg + 2 * nn]
        o_ref, dg_refs = refs[nseg + 2 * nn], refs[nseg + 2 * nn + 1:nseg + 3 * nn + 1]
        stage, tmp = refs[-2], refs[-1]

        @pl.when(pl.program_id(0) == 0)
        def _():
            for dg_ref in dg_refs:
                dg_ref[...] = jnp.zeros_like(dg_ref)

        for j in range(N_CHIPS):
            stage[:, j * cp + cs:(j + 1) * cp] = jnp.zeros((tm, cp - cs), F32)
        for i, (p_ref, runs) in enumerate(zip(p_refs, pieces)):
            src_ref = p_ref
            if i in norms:
                k = normed.index(i)
                for c0 in range(0, splits[i], HEAD_DIM):
                    cols = slice(c0, c0 + HEAD_DIM)
                    dx, dg = _head_norm_bwd(p_ref[:, cols].astype(F32), x_refs[k][:, cols], g_refs[k][:, cols])
                    tmp[:, cols] = dx
                    dg_refs[k][:, cols] += dg
                src_ref = tmp
            for dst, src, n in runs:
                stage[:, dst:dst + n] = src_ref[:, src:src + n].astype(F32)
        o_ref[...] = stage[...].astype(o_ref.dtype)

    wmax = max(splits[i] for i in normed)
    row = lambda w: pl.BlockSpec((tm, w), lambda i: (i, 0))
    vec = lambda w: pl.BlockSpec((1, w), lambda i: (0, 0))
    outs = pl.pallas_call(
        body, name="dproj_merge", grid=(s // tm,),
        in_specs=[row(w) for w in splits] + [row(splits[i]) for i in normed] + [vec(splits[i]) for i in normed],
        out_specs=[row(wp)] + [vec(splits[i]) for i in normed],
        out_shape=[jax.ShapeDtypeStruct((s, wp), BF16)] + [jax.ShapeDtypeStruct((1, splits[i]), F32) for i in normed],
        scratch_shapes=[pltpu.VMEM((tm, wp), F32), pltpu.VMEM((tm, wmax), F32)],
        compiler_params=_params("arbitrary"),
    )(*parts, *[norms[i][0] for i in normed], *[norms[i][1] for i in normed])
    return outs[0], dict(zip(normed, outs[1:]))


def _norm_fwd(x, g, *, group, name, tm=256):
    s, w = x.shape
    ng = w // group

    def body(x_ref, g_ref, o_ref):
        for i in range(ng):
            cols = slice(i * group, (i + 1) * group)
            xv = x_ref[:, cols]
            r = lax.rsqrt(jnp.mean(xv * xv, axis=-1, keepdims=True) + EPS)
            o_ref[:, cols] = ((xv * r) * g_ref[:, cols]).astype(o_ref.dtype)

    return pl.pallas_call(
        body, name=name, grid=(s // tm,),
        in_specs=[pl.BlockSpec((tm, w), lambda i: (i, 0)), pl.BlockSpec((1, w), lambda i: (0, 0))],
        out_specs=pl.BlockSpec((tm, w), lambda i: (i, 0)),
        out_shape=jax.ShapeDtypeStruct((s, w), BF16),
        compiler_params=_params("parallel"),
    )(x, g)


def _norm_bwd(dy, x, g, *, group, name, res=None, out_dtypes=(BF16,), tm=256, deps=()):
    s, w = x.shape
    ng = w // group
    n_in = 4 if res is not None else 3

    def body(*refs):
        dy_ref, x_ref, g_ref = refs[:3]
        res_ref = refs[3] if res is not None else None
        outs = refs[n_in + len(deps):]
        dx_refs, dg_ref = outs[:-1], outs[-1]

        @pl.when(pl.program_id(0) == 0)
        def _():
            dg_ref[...] = jnp.zeros_like(dg_ref)

        for i in range(ng):
            cols = slice(i * group, (i + 1) * group)
            xv = x_ref[:, cols]
            dyv = dy_ref[:, cols].astype(F32)
            r = lax.rsqrt(jnp.mean(xv * xv, axis=-1, keepdims=True) + EPS)
            xr = xv * r
            dg_ref[:, cols] += jnp.sum(dyv * xr, axis=0, keepdims=True)
            gdy = dyv * g_ref[:, cols]
            dx = r * (gdy - xr * jnp.mean(gdy * xr, axis=-1, keepdims=True))
            if res_ref is not None:
                dx = dx + res_ref[:, cols]
            for dx_ref in dx_refs:
                dx_ref[:, cols] = dx.astype(dx_ref.dtype)

    row = pl.BlockSpec((tm, w), lambda i: (i, 0))
    vec = pl.BlockSpec((1, w), lambda i: (0, 0))
    in_specs = [row, row, vec] + ([row] if res is not None else []) + [pl.BlockSpec(memory_space=pl.ANY)] * len(deps)
    args = [dy, x, g] + ([res] if res is not None else []) + list(deps)
    out_specs = [row] * len(out_dtypes) + [vec]
    out_shape = [jax.ShapeDtypeStruct((s, w), dt) for dt in out_dtypes] + [jax.ShapeDtypeStruct((1, w), F32)]
    return pl.pallas_call(
        body, name=name, grid=(s // tm,), in_specs=in_specs, out_specs=out_specs,
        out_shape=out_shape, compiler_params=_params("arbitrary"),
    )(*args)


def _split3(v):
    p1 = v.astype(BF16)
    r1 = v - p1.astype(F32)
    p2 = r1.astype(BF16)
    p3 = (r1 - p2.astype(F32)).astype(BF16)
    return p1, p2, p3


def _tri_sum(v, reverse, tcol=512):
    h, s = v.shape
    tcol = min(tcol, s)
    parts = _split3(v)
    outs = []
    for j in range(s // tcol):
        src = lax.broadcasted_iota(jnp.int32, (s, tcol), 0)
        dst = lax.broadcasted_iota(jnp.int32, (s, tcol), 1) + j * tcol
        keep = (src >= dst) if reverse else (src <= dst)
        tri = jnp.where(keep, 1.0, 0.0).astype(BF16)
        acc = jnp.zeros((h, tcol), F32)
        for p in parts:
            acc = acc + jnp.dot(p, tri, preferred_element_type=F32)
        outs.append(acc)
    return outs


def _forget_fwd(fa_t, b):
    h, s = fa_t.shape
    tcol = min(512, s)

    def body(f_ref, b_ref, c_ref):
        z = f_ref[...] + b_ref[...]
        logf = jnp.minimum(z, 0.0) - jnp.log(1.0 + jnp.exp(-jnp.abs(z)))
        for j, blk in enumerate(_tri_sum(logf, reverse=False, tcol=tcol)):
            c_ref[:, j * tcol:(j + 1) * tcol] = blk

    return pl.pallas_call(
        body, name="forget_fwd", out_shape=jax.ShapeDtypeStruct((h, s), F32),
        compiler_params=_params(),
    )(fa_t, b)


def _forget_bwd(dacol, fa_t, b):
    h, s = fa_t.shape
    tcol = min(512, s)

    def body(d_ref, f_ref, b_ref, dfa_ref, db_ref):
        z = f_ref[...] + b_ref[...]
        dc = -d_ref[...]
        total = jnp.zeros((h, 1), F32)
        for j, blk in enumerate(_tri_sum(dc, reverse=True, tcol=tcol)):
            cols = slice(j * tcol, (j + 1) * tcol)
            dfa = blk * (1.0 - jax.nn.sigmoid(z[:, cols]))
            dfa_ref[:, cols] = dfa
            total = total + jnp.sum(dfa, axis=-1, keepdims=True)
        db_ref[...] = total

    return pl.pallas_call(
        body, name="forget_bwd",
        out_shape=[jax.ShapeDtypeStruct((h, s), F32), jax.ShapeDtypeStruct((h, 1), F32)],
        compiler_params=_params(),
    )(dacol, fa_t, b)


def _distance_bias(s, tile, dilated):
    nb = s // tile
    b = lax.broadcasted_iota(jnp.int32, (nb, tile, tile), 0)
    dist = b * tile + lax.broadcasted_iota(jnp.int32, (nb, tile, tile), 1) - lax.broadcasted_iota(jnp.int32, (nb, tile, tile), 2)
    if not dilated:
        return jnp.where(dist >= 0, 0.0, NEG).astype(F32)
    mult = jnp.zeros(dist.shape, jnp.int32)
    for window, dil in DIL_PATTERNS:
        mult = mult + ((dist >= 0) & (dist <= window) & ((dist & (dil - 1)) == 0)).astype(jnp.int32)
    logm = jnp.where(mult == 3, math.log2(3.0), jnp.where(mult == 2, 1.0, 0.0))
    return jnp.where(mult > 0, logm, NEG).astype(F32)


def _logits(q, k, arow, acol, bias):
    s = lax.dot_general(q, k, (((1,), (1,)), ((), ())), preferred_element_type=F32)
    return s * (LOG2E / math.sqrt(HEAD_DIM)) + arow - acol + bias


def _attn_fwd(q, k, v, arow, acol, *, dilated, name, tq=ATTN_TILE, tk=ATTN_TILE):
    two_term = not dilated
    s, w = q.shape
    nh = w // HEAD_DIM
    assert tq == tk
    tq = tk = min(tq, s)
    nq, nk = s // tq, s // tk

    pairs = [(i, j) for i in range(nq) for j in range(i + 1)]
    q_of, k_of = (jnp.asarray(t, jnp.int32) for t in zip(*pairs))

    def body(qo_ref, ko_ref, q_ref, k_ref, v_ref, ar_ref, ac_ref, b_ref, o_ref, of_ref, lse_ref, m_ref, l_ref, acc_ref):
        t = pl.program_id(1)
        qi, ki = qo_ref[t], ko_ref[t]

        @pl.when(ki == 0)
        def _():
            m_ref[...] = jnp.full_like(m_ref, NEG)
            l_ref[...] = jnp.zeros_like(l_ref)
            acc_ref[...] = jnp.zeros_like(acc_ref)

        sc = _logits(q_ref[...], k_ref[...], ar_ref[...], ac_ref[...], b_ref[...])
        m_new = jnp.maximum(m_ref[...], jnp.max(sc, axis=-1, keepdims=True))
        alpha = jnp.exp2(m_ref[...] - m_new)
        p = jnp.exp2(sc - m_new)
        l_ref[...] = alpha * l_ref[...] + jnp.sum(p, axis=-1, keepdims=True)
        p_hi = p.astype(BF16)
        vv = v_ref[...]
        pv = jnp.dot(p_hi, vv, preferred_element_type=F32)
        if two_term:
            pv = pv + jnp.dot((p - p_hi.astype(F32)).astype(BF16), vv, preferred_element_type=F32)
        acc_ref[...] = alpha * acc_ref[...] + pv
        m_ref[...] = m_new

        @pl.when(ki == qi)
        def _():
            out = acc_ref[...] / l_ref[...]
            o_ref[...] = out.astype(o_ref.dtype)
            of_ref[...] = out
            lse_ref[...] = m_ref[...] + jnp.log2(l_ref[...])

    qs = pl.BlockSpec((tq, HEAD_DIM), lambda h, t, qo, ko: (qo[t], h))
    kv = pl.BlockSpec((tk, HEAD_DIM), lambda h, t, qo, ko: (ko[t], h))
    rowv = pl.BlockSpec((None, tq, 1), lambda h, t, qo, ko: (h, qo[t], 0))
    return pl.pallas_call(
        body, name=name,
        grid_spec=pltpu.PrefetchScalarGridSpec(
            num_scalar_prefetch=2, grid=(nh, len(pairs)),
            in_specs=[qs, kv, kv, rowv,
                      pl.BlockSpec((None, 1, tk), lambda h, t, qo, ko: (h, 0, ko[t])),
                      pl.BlockSpec((None, tq, tk), lambda h, t, qo, ko: (qo[t] - ko[t], 0, 0))],
            out_specs=[qs, qs, rowv],
            scratch_shapes=[pltpu.VMEM((tq, 1), F32), pltpu.VMEM((tq, 1), F32), pltpu.VMEM((tq, HEAD_DIM), F32)]),
        out_shape=[jax.ShapeDtypeStruct((s, w), BF16), jax.ShapeDtypeStruct((s, w), F32),
                   jax.ShapeDtypeStruct((nh, s, 1), F32)],
        compiler_params=_params("parallel", "arbitrary"),
    )(q_of, k_of, q, k, v, arow * LOG2E, acol * LOG2E, _distance_bias(s, tq, dilated))


def _attn_bwd(q, k, v, o, do, lse, arow, acol, *, dilated, name, tq=ATTN_TILE, tk=ATTN_TILE):
    s, w = q.shape
    nh = w // HEAD_DIM
    assert tq == tk
    tq = tk = min(tq, s)
    nq, nk = s // tq, s // tk
    scale = 1.0 / math.sqrt(HEAD_DIM)

    pairs = [(i, j) for j in range(nk) for i in range(j, nq)]
    q_of, k_of = (jnp.asarray(t, jnp.int32) for t in zip(*pairs))

    def body(qo_ref, ko_ref, q_ref, k_ref, v_ref, o_ref, do_ref, lse_ref, ar_ref, ac_ref, b_ref,
             dq_ref, dk_ref, dv_ref, dac_ref, dk_acc, dv_acc, dac_acc):
        t = pl.program_id(1)
        qi, ki = qo_ref[t], ko_ref[t]

        @pl.when(t == 0)
        def _():
            dq_ref[...] = jnp.zeros_like(dq_ref)

        @pl.when(qi == ki)
        def _():
            dk_acc[...] = jnp.zeros_like(dk_acc)
            dv_acc[...] = jnp.zeros_like(dv_acc)
            dac_acc[...] = jnp.zeros_like(dac_acc)

        qv, kvv, dov = q_ref[...], k_ref[...], do_ref[...]
        sc = _logits(qv, kvv, ar_ref[...], ac_ref[...], b_ref[...])
        p = jnp.exp2(sc - lse_ref[...])
        dp = lax.dot_general(dov, v_ref[...], (((1,), (1,)), ((), ())), preferred_element_type=F32)
        delta = jnp.sum(dov.astype(F32) * o_ref[...].astype(F32), axis=-1, keepdims=True)
        ds = p * (dp - delta)
        dsb = ds.astype(BF16)
        dv_acc[...] += lax.dot_general(p.astype(BF16), dov, (((0,), (0,)), ((), ())), preferred_element_type=F32)
        dk_acc[...] += lax.dot_general(dsb, qv, (((0,), (0,)), ((), ())), preferred_element_type=F32)
        rows = pl.ds(pl.multiple_of(qi * tq, tq), tq)
        dq_ref[rows, :] += jnp.dot(dsb, kvv, preferred_element_type=F32) * scale
        dac_acc[...] += jnp.sum(ds, axis=0, keepdims=True)

        @pl.when(qi == nq - 1)
        def _():
            dk_ref[...] = dk_acc[...] * scale
            dv_ref[...] = dv_acc[...]
            dac_ref[...] = dac_acc[...]

    qs = pl.BlockSpec((tq, HEAD_DIM), lambda h, t, qo, ko: (qo[t], h))
    ks = pl.BlockSpec((tk, HEAD_DIM), lambda h, t, qo, ko: (ko[t], h))
    rowv = pl.BlockSpec((None, tq, 1), lambda h, t, qo, ko: (h, qo[t], 0))
    colv = pl.BlockSpec((None, 1, tk), lambda h, t, qo, ko: (h, 0, ko[t]))
    return pl.pallas_call(
        body, name=name,
        grid_spec=pltpu.PrefetchScalarGridSpec(
            num_scalar_prefetch=2, grid=(nh, len(pairs)),
            in_specs=[qs, ks, ks, qs, qs, rowv, rowv, colv,
                      pl.BlockSpec((None, tq, tk), lambda h, t, qo, ko: (qo[t] - ko[t], 0, 0))],
            out_specs=[pl.BlockSpec((s, HEAD_DIM), lambda h, t, qo, ko: (0, h)), ks, ks, colv],
            scratch_shapes=[pltpu.VMEM((tk, HEAD_DIM), F32), pltpu.VMEM((tk, HEAD_DIM), F32), pltpu.VMEM((1, tk), F32)]),
        out_shape=[jax.ShapeDtypeStruct((s, w), F32), jax.ShapeDtypeStruct((s, w), F32),
                   jax.ShapeDtypeStruct((s, w), F32), jax.ShapeDtypeStruct((nh, 1, s), F32)],
        compiler_params=_params("arbitrary", "arbitrary"),
    )(q_of, k_of, q, k, v, o, do, lse, arow * LOG2E, acol * LOG2E, _distance_bias(s, tq, dilated))


def _gate_fwd(ga, gb, pa, pb, tm=256):
    s, d = ga.shape

    def body(ga_ref, gb_ref, pa_ref, pb_ref, o_ref):
        o_ref[...] = (jax.nn.sigmoid(ga_ref[...]) * pa_ref[...]
                      + jax.nn.sigmoid(gb_ref[...]) * pb_ref[...]).astype(o_ref.dtype)

    row = pl.BlockSpec((tm, d), lambda i: (i, 0))
    return pl.pallas_call(
        body, name="gate_fwd", grid=(s // tm,), in_specs=[row] * 4, out_specs=row,
        out_shape=jax.ShapeDtypeStruct((s, d), BF16), compiler_params=_params("parallel"),
    )(ga, gb, pa, pb)


def _gate_bwd(dm, ga, gb, pa, pb, tm=256):
    s, d = ga.shape

    def body(dm_ref, ga_ref, gb_ref, pa_ref, pb_ref, dpa_ref, dpb_ref, dga_ref, dgb_ref):
        dmv = dm_ref[...]
        for g_ref, p_ref, dp_ref, dg_ref in ((ga_ref, pa_ref, dpa_ref, dga_ref), (gb_ref, pb_ref, dpb_ref, dgb_ref)):
            sg = jax.nn.sigmoid(g_ref[...])
            dp_ref[...] = (dmv * sg).astype(BF16)
            dg_ref[...] = (dmv * p_ref[...] * (sg * (1.0 - sg))).astype(BF16)

    row = pl.BlockSpec((tm, d), lambda i: (i, 0))
    return pl.pallas_call(
        body, name="gate_bwd", grid=(s // tm,), in_specs=[row] * 5, out_specs=[row] * 4,
        out_shape=[jax.ShapeDtypeStruct((s, d), BF16)] * 4, compiler_params=_params("parallel"),
    )(dm, ga, gb, pa, pb)


def _shift_down(u, k):
    row = lax.broadcasted_iota(jnp.int32, u.shape, 0)
    return jnp.where(row >= k, pltpu.roll(u, k, 0), 0.0)


def _shift_up(u, k):
    n = u.shape[0]
    row = lax.broadcasted_iota(jnp.int32, u.shape, 0)
    return jnp.where(row < n - k, pltpu.roll(u, n - k, 0), 0.0)


def _conv3(u, wc, b):
    return wc[0:1, :] * _shift_down(u, 2) + wc[1:2, :] * _shift_down(u, 1) + wc[2:3, :] * u + b


def _conv_glu_fwd(u, wc, b, tn=256):
    s, f2 = u.shape
    f = f2 // 2
    nb = f // tn

    def body(ug_ref, uv_ref, wg_ref, wv_ref, bg_ref, bv_ref, o_ref):
        cg = _conv3(ug_ref[...], wg_ref[...], bg_ref[...])
        cv = _conv3(uv_ref[...], wv_ref[...], bv_ref[...])
        o_ref[...] = (cg * jax.nn.sigmoid(cg) * cv).astype(o_ref.dtype)

    def cols(rows, off):
        return pl.BlockSpec((rows, tn), lambda j: (0, j + off))

    return pl.pallas_call(
        body, name="conv_glu_fwd", grid=(nb,),
        in_specs=[cols(s, 0), cols(s, nb), cols(3, 0), cols(3, nb), cols(1, 0), cols(1, nb)],
        out_specs=cols(s, 0), out_shape=jax.ShapeDtypeStruct((s, f), BF16),
        compiler_params=_params("parallel"),
    )(u, u, wc, wc, b, b)


def _conv_glu_bwd(u, da, wc, b, tn=256):
    s, f2 = u.shape
    f = f2 // 2
    nb = f // tn

    def body(ug_ref, uv_ref, da_ref, wg_ref, wv_ref, bg_ref, bv_ref, dug_ref, duv_ref, sg_ref, sv_ref):
        ug, uv, wg, wv = ug_ref[...], uv_ref[...], wg_ref[...], wv_ref[...]
        cg = _conv3(ug, wg, bg_ref[...])
        cv = _conv3(uv, wv, bv_ref[...])
        sig = jax.nn.sigmoid(cg)
        dav = da_ref[...]
        dcv = dav * (cg * sig)
        dcg = dav * cv * (sig * (1.0 + cg * (1.0 - sig)))
        for dc, uu, w, du_ref, st_ref in ((dcg, ug, wg, dug_ref, sg_ref), (dcv, uv, wv, duv_ref, sv_ref)):
            du = w[2:3, :] * dc + w[1:2, :] * _shift_up(dc, 1) + w[0:1, :] * _shift_up(dc, 2)
            du_ref[...] = du.astype(BF16)
            st_ref[...] = jnp.zeros_like(st_ref)
            st_ref[0:1, :] = jnp.sum(dc * _shift_down(uu, 2), axis=0, keepdims=True)
            st_ref[1:2, :] = jnp.sum(dc * _shift_down(uu, 1), axis=0, keepdims=True)
            st_ref[2:3, :] = jnp.sum(dc * uu, axis=0, keepdims=True)
            st_ref[3:4, :] = jnp.sum(dc, axis=0, keepdims=True)

    def cols(rows, off):
        return pl.BlockSpec((rows, tn), lambda j: (0, j + off))

    return pl.pallas_call(
        body, name="conv_glu_bwd", grid=(nb,),
        in_specs=[cols(s, 0), cols(s, nb), cols(s, 0), cols(3, 0), cols(3, nb), cols(1, 0), cols(1, nb)],
        out_specs=[cols(s, 0), cols(s, 0), cols(8, 0), cols(8, 0)],
        out_shape=[jax.ShapeDtypeStruct((s, f), BF16), jax.ShapeDtypeStruct((s, f), BF16),
                   jax.ShapeDtypeStruct((8, f), F32), jax.ShapeDtypeStruct((8, f), F32)],
        compiler_params=_params("parallel"),
    )(u, u, da, wc, wc, b, b)


ROW_TILES = (256, 128, 64, 32, 16, 8)
BLOCK_BYTES = 2 << 20


def _add_halves(g, r1, place):
    ns, r, c = g.shape
    rh = r // 2
    tr = _pick(rh, ROW_TILES)
    g4 = g.reshape(ns, 2, rh, c)

    def body(p_ref, g_ref, r_ref, o_ref):
        o_ref[...] = (g_ref[...].astype(F32) + r_ref[...].astype(F32)).astype(o_ref.dtype)

    def slab(s, pr):
        return s + (s >= pr[0]).astype(jnp.int32)

    return pl.pallas_call(
        body, name="add_halves",
        grid_spec=pltpu.PrefetchScalarGridSpec(
            num_scalar_prefetch=1, grid=(ns - 1, rh // tr),
            in_specs=[pl.BlockSpec((None, None, tr, c), lambda s, i, pr: (slab(s, pr), pr[1], i, 0)),
                      pl.BlockSpec((None, tr, c), lambda s, i, pr: (slab(s, pr), i, 0))],
            out_specs=pl.BlockSpec((None, tr, c), lambda s, i, pr: (slab(s, pr), i, 0))),
        out_shape=jax.ShapeDtypeStruct((ns, rh, c), BF16),
        compiler_params=_params("parallel", "parallel"),
    )(place, g4, r1)


def _sum_chips(g, r1, recv, place):
    ns, r, c = g.shape
    rh = r // 2
    tr = _pick(rh, ROW_TILES)
    g4 = g.reshape(ns, 2, rh, c)

    def body(p_ref, g_ref, r_ref, t0_ref, t1_ref, t2_ref, o_ref):
        own = g_ref[...].astype(F32) + r_ref[...].astype(F32)
        o_ref[...] = ((own + t0_ref[...].astype(F32)) + t1_ref[...].astype(F32)) + t2_ref[...].astype(F32)

    def peer(k):
        return pl.BlockSpec((None, tr, c), lambda i, pr: (k, i, 0))

    return pl.pallas_call(
        body, name="sum_chips",
        grid_spec=pltpu.PrefetchScalarGridSpec(
            num_scalar_prefetch=1, grid=(rh // tr,),
            in_specs=[pl.BlockSpec((None, None, tr, c), lambda i, pr: (pr[0], pr[1], i, 0)),
                      pl.BlockSpec((None, tr, c), lambda i, pr: (pr[0], i, 0)), peer(0), peer(1), peer(2)],
            out_specs=pl.BlockSpec((tr, c), lambda i, pr: (pr[1] * (rh // tr) + i, 0))),
        out_shape=jax.ShapeDtypeStruct((r, c), F32),
        compiler_params=_params("parallel"),
    )(place, g4, r1, recv, recv, recv)


def _sum_devices(packs):
    n, r, c = packs.shape

    def body(p_ref, o_ref):
        acc = p_ref[0]
        for d in range(1, n):
            acc = acc + p_ref[d]
        o_ref[...] = acc

    return pl.pallas_call(
        body, name="sum_devices", out_shape=jax.ShapeDtypeStruct((r, c), F32), compiler_params=_params(),
    )(packs)


def _adamw_update(wv, gv, mv, vv):
    c1 = 1.0 - ADAM_B1 ** ADAM_STEP
    c2 = 1.0 - ADAM_B2 ** ADAM_STEP
    mn = ADAM_B1 * mv + (1.0 - ADAM_B1) * gv
    vn = ADAM_B2 * vv + (1.0 - ADAM_B2) * (gv * gv)
    m_hat = mn / c1
    v_hat = vn / c2
    return -ADAM_LR * (m_hat / (jnp.sqrt(v_hat) + ADAM_EPS) + ADAM_WD * wv), mn, vn


def _adamw(w, g, m, v, name, deps=(), emit_grad=False):
    r, c = w.shape
    tr = _pick(r, [t for t in ROW_TILES if t * c * 4 <= BLOCK_BYTES]) if r >= 8 else r
    n_out = 4 if emit_grad else 3

    def body(w_ref, g_ref, m_ref, v_ref, *rest):
        outs = rest[-n_out:]
        gv = g_ref[:, :c]
        if emit_grad:
            outs[0][...] = gv
        outs[-3][...], outs[-2][...], outs[-1][...] = _adamw_update(w_ref[...], gv, m_ref[...], v_ref[...])

    blk = pl.BlockSpec((tr, c), lambda i: (i, 0))
    g_blk = pl.BlockSpec((tr, g.shape[1]), lambda i: (i, 0))
    return pl.pallas_call(
        body, name=name, grid=(r // tr,), in_specs=[blk, g_blk, blk, blk] + [ANY] * len(deps), out_specs=[blk] * n_out,
        out_shape=[jax.ShapeDtypeStruct((r, c), F32)] * n_out, compiler_params=_params("parallel"),
    )(w, g, m, v, *deps)


ANY = pl.BlockSpec(memory_space=pl.ANY)


def _place():
    x, y, c = lax.axis_index("x"), lax.axis_index("y"), lax.axis_index("c")
    chips = [(1 - x, y), (x, 1 - y), (1 - x, 1 - y)]
    return x, y, c, chips


def _remote(src, dst, send_sem, recv_sem, to):
    return pltpu.make_async_remote_copy(src_ref=src, dst_ref=dst, send_sem=send_sem, recv_sem=recv_sem,
                                        device_id=to, device_id_type=MESH)


HBM = pl.BlockSpec(memory_space=pltpu.HBM)
SEM = pl.BlockSpec(memory_space=pltpu.SEMAPHORE)
EFFECT = pltpu.SideEffectType.DATAFLOW_SIDE_EFFECTING


def _in_hbm(a):
    return pltpu.with_memory_space_constraint(a, pltpu.HBM)


def _half(ref_rows, who):
    return pl.ds(who * (ref_rows // 2), ref_rows // 2)


def _gather_start(groups, name):
    items = [it for g in groups for it in g]
    n = len(items)
    sizes = [len(g) for g in groups]

    def body(*refs):
        srcs, lands = refs[:n], refs[n:2 * n]
        sems = refs[2 * n:2 * n + 2 * len(groups)]
        token = refs[-1]
        x, y, c, chips = _place()
        j = 2 * x + y
        at = 0
        for gi, g in enumerate(groups):
            send, recv = sems[2 * gi], sems[2 * gi + 1]
            for i, (shard, split) in enumerate(g):
                src, land = srcs[at], lands[at]
                at += 1
                rows = _half(shard.shape[0], c) if split else slice(None)
                for k, chip in enumerate(chips):
                    _remote(src.at[rows], land.at[j, rows], send.at[4 * i + k], recv.at[4 * i + k], (*chip, c)).start()
                _remote(src, land.at[j], send.at[4 * i + 3], recv.at[4 * i + 3], (x, y, 1 - c)).start()
        token[...] = jnp.zeros_like(token)

    sem_shapes = []
    for sz in sizes:
        sem_shapes += [pltpu.SemaphoreType.DMA((4 * sz,)), pltpu.SemaphoreType.DMA((4 * sz,))]
    out_shape = (sem_shapes + [pltpu.HBM(sh.shape, sh.dtype) for sh, _ in items]
                 + [pltpu.HBM((N_CHIPS,) + sh.shape, sh.dtype) for sh, _ in items]
                 + [jax.ShapeDtypeStruct((8, LANES), F32)])
    ns = len(sem_shapes)
    outs = pl.pallas_call(
        body, name=name, in_specs=[HBM] * (2 * n),
        out_specs=[SEM] * ns + [HBM] * (2 * n) + [pl.BlockSpec(memory_space=pltpu.VMEM)],
        out_shape=out_shape, input_output_aliases={i: ns + i for i in range(2 * n)},
        compiler_params=pltpu.CompilerParams(has_side_effects=EFFECT),
    )(*[_in_hbm(sh) for sh, _ in items], *[_in_hbm(lax.empty((N_CHIPS,) + sh.shape, sh.dtype)) for sh, _ in items])
    sems, shards, lands, token = outs[:ns], outs[ns:ns + n], outs[ns + n:ns + 2 * n], outs[-1]
    res, at = [], 0
    for gi, sz in enumerate(sizes):
        res.append((shards[at:at + sz], lands[at:at + sz], sems[2 * gi], sems[2 * gi + 1]))
        at += sz
    return res, token


def _gather_pass(group, started, after, name):
    shards, lands, send, recv = started
    n = len(group)
    split_ix = [i for i, (_, split) in enumerate(group) if split]

    def body(*refs):
        lnds, send1, recv1 = refs[n:2 * n], refs[2 * n], refs[2 * n + 1]
        outs = refs[2 * n + 2 + len(after):]
        send2, recv2, token = outs[2 * n], outs[2 * n + 1], outs[2 * n + 2]
        x, y, c, chips = _place()
        sib = (x, y, 1 - c)
        for i, (shard, split) in enumerate(group):
            rows = _half(shard.shape[0], c) if split else slice(None)
            for k, (cx, cy) in enumerate(chips):
                landed = lnds[i].at[2 * cx + cy, rows]
                cp = _remote(landed, landed, send1.at[4 * i + k], recv1.at[4 * i + k], sib)
                cp.wait_send()
                cp.wait_recv()
            own = lnds[i].at[2 * x + y]
            cp = _remote(own, own, send1.at[4 * i + 3], recv1.at[4 * i + 3], sib)
            cp.wait_send()
            cp.wait_recv()
        for i2, i in enumerate(split_ix):
            rows = _half(group[i][0].shape[0], c)
            for k, (cx, cy) in enumerate(chips):
                landed = lnds[i].at[2 * cx + cy, rows]
                _remote(landed, landed, send2.at[3 * i2 + k], recv2.at[3 * i2 + k], sib).start()
        token[...] = jnp.zeros_like(token)

    n2 = len(split_ix)
    out_shape = ([pltpu.HBM(a.shape, a.dtype) for a in (*shards, *lands)]
                 + [pltpu.SemaphoreType.DMA((3 * n2,)), pltpu.SemaphoreType.DMA((3 * n2,)), jax.ShapeDtypeStruct((8, LANES), F32)])
    outs = pl.pallas_call(
        body, name=name, in_specs=[HBM] * (2 * n) + [SEM, SEM] + [ANY] * len(after),
        out_specs=[HBM] * (2 * n) + [SEM, SEM, pl.BlockSpec(memory_space=pltpu.VMEM)],
        out_shape=out_shape, input_output_aliases={i: i for i in range(2 * n)},
        compiler_params=pltpu.CompilerParams(has_side_effects=EFFECT),
    )(*shards, *lands, send, recv, *after)
    return outs[:n], (outs[n:2 * n], outs[2 * n], outs[2 * n + 1]), outs[2 * n + 2]


def _gather_wait(group, passed, after, name):
    lands, send2, recv2 = passed
    n = len(group)
    split_ix = [i for i, (_, split) in enumerate(group) if split]

    def body(*refs):
        lnds, s2, r2 = refs[:n], refs[n], refs[n + 1]
        x, y, c, chips = _place()
        sib = (x, y, 1 - c)
        for i2, i in enumerate(split_ix):
            rows = _half(group[i][0].shape[0], 1 - c)
            for k, (cx, cy) in enumerate(chips):
                landed = lnds[i].at[2 * cx + cy, rows]
                cp = _remote(landed, landed, s2.at[3 * i2 + k], r2.at[3 * i2 + k], sib)
                cp.wait_send()
                cp.wait_recv()

    return pl.pallas_call(
        body, name=name, in_specs=[HBM] * n + [SEM, SEM, ANY], out_specs=[HBM] * n,
        out_shape=[pltpu.HBM(a.shape, a.dtype) for a in lands], input_output_aliases={i: i for i in range(n)},
        compiler_params=pltpu.CompilerParams(has_side_effects=EFFECT),
    )(*lands, send2, recv2, after)


def _xfer_start(name, srcs, land_shapes, n_copies, copies, after):
    n, nl = len(srcs), len(land_shapes)

    def body(*refs):
        src_refs, land_refs = refs[:n], refs[n:n + nl]
        send, recv, token = refs[n + nl + 1], refs[n + nl + 2], refs[-1]
        for cp in copies(src_refs, land_refs, send, recv):
            cp.start()
        token[...] = jnp.zeros_like(token)

    lands = [_in_hbm(lax.empty(shape, dtype)) for shape, dtype in land_shapes]
    out_shape = ([pltpu.SemaphoreType.DMA((n_copies,)), pltpu.SemaphoreType.DMA((n_copies,))]
                 + [pltpu.HBM(a.shape, a.dtype) for a in (*srcs, *lands)] + [jax.ShapeDtypeStruct((8, LANES), F32)])
    outs = pl.pallas_call(
        body, name=name, in_specs=[HBM] * (n + nl) + [ANY],
        out_specs=[SEM, SEM] + [HBM] * (n + nl) + [pl.BlockSpec(memory_space=pltpu.VMEM)],
        out_shape=out_shape, input_output_aliases={i: 2 + i for i in range(n + nl)},
        compiler_params=pltpu.CompilerParams(has_side_effects=EFFECT),
    )(*[_in_hbm(a) for a in srcs], *lands, after)
    return (outs[2:2 + n], outs[2 + n:2 + n + nl], outs[0], outs[1]), outs[-1]


def _xfer_wait(name, started, copies, after):
    srcs, lands, send, recv = started
    n, nl = len(srcs), len(lands)

    def body(*refs):
        src_refs, land_refs, s_ref, r_ref = refs[:n], refs[n:n + nl], refs[n + nl], refs[n + nl + 1]
        for cp in copies(src_refs, land_refs, s_ref, r_ref):
            cp.wait_send()
            cp.wait_recv()

    outs = pl.pallas_call(
        body, name=name, in_specs=[HBM] * (n + nl) + [SEM, SEM, ANY], out_specs=[HBM] * (n + nl),
        out_shape=[pltpu.HBM(a.shape, a.dtype) for a in (*srcs, *lands)],
        input_output_aliases={i: i for i in range(n + nl)},
        compiler_params=pltpu.CompilerParams(has_side_effects=EFFECT),
    )(*srcs, *lands, send, recv, after)
    return outs[:n], outs[n:]


def _swap_copies(srcs, lands, send, recv):
    x, y, c, _ = _place()
    return [_remote(src.at[:, _half(src.shape[1], 1 - c)], land, send.at[i], recv.at[i], (x, y, 1 - c))
            for i, (src, land) in enumerate(zip(srcs, lands))]


def _scatter_copies(srcs, lands, send, recv):
    x, y, c, chips = _place()
    return [_remote(src.at[2 * cx + cy], land.at[k], send.at[3 * i + k], recv.at[3 * i + k], (cx, cy, c))
            for i, (src, land) in enumerate(zip(srcs, lands)) for k, (cx, cy) in enumerate(chips)]


def _join_copies(srcs, lands, send, recv):
    x, y, c, _ = _place()
    return [_remote(src.at[_half(src.shape[0], c)], src.at[_half(src.shape[0], c)], send.at[i], recv.at[i], (x, y, 1 - c))
            for i, src in enumerate(srcs)]


def _corner(a):
    return a[(slice(0, 1),) * a.ndim]


class _Reducer:
    def __init__(self, place):
        self.place = place
        self.state = {}

    def swap(self, key, grads, after):
        shapes = [((g.shape[0], g.shape[1] // 2, g.shape[2]), g.dtype) for g in grads]
        self.state[key], token = _xfer_start("swap_start_" + key, grads, shapes, len(grads), _swap_copies, _corner(after))
        return token

    def to_chips(self, key, after):
        grads, from_sibling = _xfer_wait("swap_wait_" + key, self.state[key], _swap_copies, after)
        sums = [_add_halves(g, r, self.place) for g, r in zip(grads, from_sibling)]
        shapes = [((3,) + s.shape[1:], s.dtype) for s in sums]
        started, token = _xfer_start("scatter_start_" + key, sums, shapes, 3 * len(sums), _scatter_copies, _corner(sums[-1]))
        self.state[key] = (grads, from_sibling, started)
        return token

    def to_core(self, key, after):
        grads, from_sibling, started = self.state[key]
        _, from_chips = _xfer_wait("scatter_wait_" + key, started, _scatter_copies, after)
        shards = [_sum_chips(g, r, rc, self.place) for g, r, rc in zip(grads, from_sibling, from_chips)]
        self.state[key], token = _xfer_start("join_start_" + key, shards, [], len(shards), _join_copies, _corner(shards[-1]))
        return token

    def finish(self, key, after):
        return _xfer_wait("join_wait_" + key, self.state.pop(key), _join_copies, after)[0]


def _gather_packs(pack, deps=()):
    def body(p_ref, *rest):
        o_ref, lsem, ssem, rsem = rest[-4:]
        x, y, c, _ = _place()
        me = 4 * x + 2 * y + c
        local = pltpu.make_async_copy(p_ref, o_ref.at[me], lsem)
        local.start()
        cps = []
        for k in range(1, N_DEV):
            fx, fy, fc = (k >> 2) & 1, (k >> 1) & 1, k & 1
            to = (x ^ fx, y ^ fy, c ^ fc)
            cps.append(_remote(p_ref, o_ref.at[me], ssem.at[k - 1], rsem.at[k - 1], to))
        for cp in cps:
            cp.start()
        for k in range(1, N_DEV):
            fx, fy, fc = (k >> 2) & 1, (k >> 1) & 1, k & 1
            src = o_ref.at[4 * (x ^ fx) + 2 * (y ^ fy) + (c ^ fc)]
            _remote(src, src, ssem.at[k - 1], rsem.at[k - 1], (x, y, c)).wait_recv()
        for cp in cps:
            cp.wait_send()
        local.wait()

    return pl.pallas_call(
        body, name="gather_packs", in_specs=[ANY] * (1 + len(deps)), out_specs=ANY,
        out_shape=jax.ShapeDtypeStruct((N_DEV,) + pack.shape, pack.dtype),
        scratch_shapes=[pltpu.SemaphoreType.DMA, pltpu.SemaphoreType.DMA((N_DEV - 1,)), pltpu.SemaphoreType.DMA((N_DEV - 1,))],
    )(pack, *deps)


LANE_TILES = (512, 896, 1408, 704, 384, 256, 128)


def _layer_grads(x, target, small, wg, rest_pass, rest_wait, red, filler):
    s, d = x.shape
    f = wg["conv"].shape[1] // 2
    w_att = N_HEADS * HEAD_DIM
    in_splits = (w_att, w_att, w_att, N_HEADS, w_att, w_att, w_att, d, d)
    in_cols = sum(in_splits)
    cs = in_cols // N_CHIPS
    cp = wg["in"].shape[2]
    tm = min(s, MM_TILE)
    tm_wide = min(s, MM_TILE // 2)
    t_in = cp
    t_up = 2 * f // N_CHIPS
    t_d = _pick(d, LANE_TILES)
    t_d2 = min(d, MM_TILE)
    t_dq = _pick(d // N_CHIPS, LANE_TILES)
    t_fq = _pick(f // N_CHIPS, LANE_TILES)

    h1 = _norm_fwd(x, small["g_attn"], group=d, name="rms1_fwd")
    proj_p = _mm(h1, wg["in"], mode="nn", b_kind="col", tm=tm_wide, tn=t_in, tk=d, name="mm_in")
    gains = {n: small[n].reshape(1, w_att) for n in ("g_q_fox", "g_k_fox", "g_q_dil", "g_k_dil")}
    qa, ka, va_b, fa, qb, kb, vb_b, ga, gb, qa_n, ka_n, qb_n, kb_n = _proj_split(
        proj_p, in_splits, cs, (F32, F32, BF16, F32, F32, F32, BF16, F32, F32),
        {0: gains["g_q_fox"], 1: gains["g_k_fox"], 4: gains["g_q_dil"], 5: gains["g_k_dil"]})
    fa_t = fa.T
    b_f = small["b_forget"].reshape(N_HEADS, 1)
    c_f = _forget_fwd(fa_t, b_f)
    slopes = jnp.asarray(2.0 ** (-8.0 * np.arange(1, N_HEADS + 1) / N_HEADS), dtype=F32)
    a_d = -(slopes[:, None] * jnp.arange(s, dtype=F32)[None, :])
    rows_f, cols_f = c_f[:, :, None], c_f[:, None, :]
    rows_d, cols_d = a_d[:, :, None], a_d[:, None, :]
    o_a, o_a32, lse_a = _attn_fwd(qa_n, ka_n, va_b, rows_f, cols_f, dilated=False, name="attn_fox_fwd")
    token = rest_pass("mid", o_a)
    rows_d = rows_d + token[0, 0]
    o_b, o_b32, lse_b = _attn_fwd(qb_n, kb_n, vb_b, rows_d, cols_d, dilated=True, name="attn_dil_fwd")
    wg = dict(wg, **rest_wait("mid", o_b))
    token = rest_pass("late", o_b)
    pa = _mm(o_a, wg["brf"], mode="nn", b_kind="col", tm=tm, tn=t_dq, tk=w_att, name="mm_brf", deps=(token,))
    pb = _mm(o_b, wg["brd"], mode="nn", b_kind="col", tm=tm, tn=t_dq, tk=w_att, name="mm_brd")
    merged = _gate_fwd(ga, gb, pa, pb)
    x1 = _mm(merged, wg["out"], mode="nn", b_kind="row", res=x, tm=tm, tn=t_d, tk=t_dq, name="mm_out")
    wg = dict(wg, **rest_wait("late", x1))
    h2 = _norm_fwd(x1, small["g_ffn"], group=d, name="rms2_fwd")
    u = _mm(h2, wg["up"], mode="nn", b_kind="col", tm=tm_wide, tn=t_up, tk=d, name="mm_up")
    act = _conv_glu_fwd(u, wg["conv"], wg["bconv"])
    dy_f, dy_b, loss_blk = _mm(act, wg["down"], mode="nn", b_kind="row", res=x1, loss_target=target,
                               tm=tm, tn=t_d2, tk=t_fq, name="mm_down")

    d_act = _mm(dy_b, wg["down"], mode="nt", b_kind="row", tm=tm, tn=t_fq, tk=d, name="mm_down_dx")
    g_down = _mm(act, dy_b, mode="tn", out_dtype=BF16, out_kind="row", tm=t_fq, tn=t_d2, tk=s, name="mm_down_dw")
    tok = red.swap("down", [g_down], g_down)
    du_g, du_v, st_g, st_v = _conv_glu_bwd(u, d_act, wg["conv"] + tok[0, 0], wg["bconv"])
    tok = red.to_chips("down", du_g)
    du = (du_g, du_v)
    g_up = _mm(h2, du, mode="tn", out_dtype=BF16, out_kind="col", tm=t_d2, tn=t_up // 2, tk=s, name="mm_up_dw", deps=(tok,))
    tok = red.to_core("down", g_up)
    tok2 = red.swap("up", [g_up], g_up)
    dh2 = _mm(du, wg["up"], mode="nt", b_kind="col", tm=tm, tn=t_d2, tk=t_up, name="mm_up_dx", deps=(tok, tok2))
    tok = red.to_chips("up", dh2)
    dx1_b, dx1_f, dg_ffn = _norm_bwd(dh2, x1, small["g_ffn"], group=d, res=dy_f, out_dtypes=(BF16, F32), name="rms2_bwd")
    d_merged = _mm(dx1_b, wg["out"], mode="nt", b_kind="row", tm=tm, tn=t_dq, tk=d, name="mm_out_dx", deps=(tok,))
    g_out = _mm(merged, dx1_b, mode="tn", out_dtype=BF16, out_kind="row", tm=t_dq, tn=t_d2, tk=s, name="mm_out_dw")
    dpa, dpb, dga, dgb = _gate_bwd(d_merged, ga, gb, pa, pb)
    do_a = _mm(dpa, wg["brf"], mode="nt", b_kind="col", out_dtype=BF16, tm=s, tn=w_att, tk=t_dq, name="mm_brf_dx")
    do_b = _mm(dpb, wg["brd"], mode="nt", b_kind="col", out_dtype=BF16, tm=s, tn=w_att, tk=t_dq, name="mm_brd_dx")
    g_brf = _mm(o_a, dpa, mode="tn", out_dtype=BF16, out_kind="col", tm=w_att, tn=t_dq, tk=s, name="mm_brf_dw")
    g_brd = _mm(o_b, dpb, mode="tn", out_dtype=BF16, out_kind="col", tm=w_att, tn=t_dq, tk=s, name="mm_brd_dw")
    tok = red.swap("mix", [g_out, g_brf, g_brd], g_brd)
    dqa_n, dka_n, dva, dac_a = _attn_bwd(qa_n, ka_n, va_b, o_a32, do_a, lse_a, rows_f + tok[0, 0], cols_f, dilated=False, name="attn_fox_bwd")
    tok = red.to_core("up", dqa_n)
    tok2 = red.to_chips("mix", dqa_n)
    dqb_n, dkb_n, dvb, _ = _attn_bwd(qb_n, kb_n, vb_b, o_b32, do_b, lse_b, rows_d + (tok[0, 0] + tok2[0, 0]), cols_d, dilated=True, name="attn_dil_bwd")
    tok = red.to_core("mix", dqb_n)
    dfa_t, db_f = _forget_bwd(dac_a[:, 0, :], fa_t, b_f)
    dproj_p, dgains = _dproj_merge(
        [dqa_n, dka_n, dva, dfa_t.T, dqb_n, dkb_n, dvb, dga, dgb], in_splits, cs, cp,
        {0: (qa, gains["g_q_fox"]), 1: (ka, gains["g_k_fox"]), 4: (qb, gains["g_q_dil"]), 5: (kb, gains["g_k_dil"])})
    dg_qf, dg_kf, dg_qd, dg_kd = dgains[0], dgains[1], dgains[4], dgains[5]
    g_in = _mm(h1, dproj_p, mode="tn", out_dtype=BF16, out_kind="col", tm=t_d2, tn=t_in, tk=s, name="mm_in_dw", deps=(tok,))
    tok = red.swap("in", [g_in], g_in)
    tok = red.to_chips("in", filler(tok))
    dh1 = _mm(dproj_p, wg["in"], mode="nt", b_kind="col", tm=tm, tn=t_d2, tk=t_in, name="mm_in_dx", deps=(tok,))
    grad_x, dg_attn = _norm_bwd(dh1, x, small["g_attn"], group=d, res=dx1_f, out_dtypes=(F32,), name="rms1_bwd")

    small_grads = {
        "g_attn": dg_attn, "b_forget": db_f.reshape(1, N_HEADS),
        "g_q_fox": dg_qf, "g_k_fox": dg_kf, "g_q_dil": dg_qd, "g_k_dil": dg_kd, "g_ffn": dg_ffn,
        "w_conv": jnp.concatenate([st_g[0:3], st_v[0:3]], axis=1),
        "b_conv": jnp.concatenate([st_g[3:4], st_v[3:4]], axis=1),
        "loss": loss_blk[0:1, 0:1],
    }
    return small_grads, grad_x


SMALL_ORDER = ("g_attn", "b_forget", "g_q_fox", "g_k_fox", "g_q_dil", "g_k_dil", "g_ffn", "w_conv", "b_conv", "loss")
WEIGHT_ORDER = ("g_attn", "w_in", "b_forget", "g_q_fox", "g_k_fox", "g_q_dil", "g_k_dil", "w_br_fox", "w_br_dil",
                "w_out", "g_ffn", "w_up", "w_conv", "b_conv", "w_down")
BIG = {"w_in": "in", "w_br_fox": "brf", "w_br_dil": "brd", "w_out": "out", "w_up": "up", "w_down": "down"}


def kernel(x, g_attn, w_in, b_forget, g_q_fox, g_k_fox, g_q_dil, g_k_dil, w_br_fox, w_br_dil, w_out, g_ffn, w_up, w_conv, b_conv, w_down, loss_target, m_g_attn, m_w_in, m_b_forget, m_g_q_fox, m_g_k_fox, m_g_q_dil, m_g_k_dil, m_w_br_fox, m_w_br_dil, m_w_out, m_g_ffn, m_w_up, m_w_conv, m_b_conv, m_w_down, v_g_attn, v_w_in, v_b_forget, v_g_q_fox, v_g_k_fox, v_g_q_dil, v_g_k_dil, v_w_br_fox, v_w_br_dil, v_w_out, v_g_ffn, v_w_up, v_w_conv, v_b_conv, v_w_down):
    w = dict(g_attn=g_attn, w_in=w_in, b_forget=b_forget, g_q_fox=g_q_fox, g_k_fox=g_k_fox, g_q_dil=g_q_dil,
             g_k_dil=g_k_dil, w_br_fox=w_br_fox, w_br_dil=w_br_dil, w_out=w_out, g_ffn=g_ffn, w_up=w_up,
             w_conv=w_conv, b_conv=b_conv, w_down=w_down)
    m = dict(g_attn=m_g_attn, w_in=m_w_in, b_forget=m_b_forget, g_q_fox=m_g_q_fox, g_k_fox=m_g_k_fox,
             g_q_dil=m_g_q_dil, g_k_dil=m_g_k_dil, w_br_fox=m_w_br_fox, w_br_dil=m_w_br_dil, w_out=m_w_out,
             g_ffn=m_g_ffn, w_up=m_w_up, w_conv=m_w_conv, b_conv=m_b_conv, w_down=m_w_down)
    v = dict(g_attn=v_g_attn, w_in=v_w_in, b_forget=v_b_forget, g_q_fox=v_g_q_fox, g_k_fox=v_g_k_fox,
             g_q_dil=v_g_q_dil, g_k_dil=v_g_k_dil, w_br_fox=v_w_br_fox, w_br_dil=v_w_br_dil, w_out=v_w_out,
             g_ffn=v_g_ffn, w_up=v_w_up, w_conv=v_w_conv, b_conv=v_b_conv, w_down=v_w_down)
    xi, yi, ci = lax.axis_index("x"), lax.axis_index("y"), lax.axis_index("c")
    chip = (2 * xi + yi).astype(jnp.int32)

    cs = w_in.shape[2]
    cp = _round_up(cs, LANES)
    conv_pad = jnp.pad(w_conv[0], ((0, 8 - w_conv.shape[1]), (0, 0)))
    first = [(jnp.pad(w_in[0].astype(BF16), ((0, 0), (0, cp - cs))), True), (conv_pad, False)]
    (started_first,), token = _gather_start([first], "gather_start_in")
    one = 1.0 + token[0, 0]
    shards = {n: (a[0] * one).astype(BF16) for n, a in
              (("brf", w_br_fox), ("brd", w_br_dil), ("out", w_out), ("up", w_up), ("down", w_down))}
    later = {"mid": ("brf", "brd", "out"), "late": ("up", "down")}
    groups = {key: [(shards[n], True) for n in members] for key, members in later.items()}
    started_later, token = _gather_start(list(groups.values()), "gather_start_rest")
    started = dict(zip(later, started_later))
    token, w["w_in"], m["w_in"], v["w_in"] = lax.optimization_barrier((token, w["w_in"], m["w_in"], v["w_in"]))
    w2, m2, v2 = ({n: a[n].reshape(a[n].shape[-2], a[n].shape[-1]) for n in BIG} for a in (w, m, v))
    early = (token, w2["w_in"], m2["w_in"], v2["w_in"])
    own_first, passed_first, token = _gather_pass(first, started_first, early, "gather_pass_in")
    land_in, land_conv = _gather_wait(first, passed_first, token, "gather_wait_in")
    wg = {"in": land_in, "bconv": b_conv,
          "conv": jnp.transpose(land_conv[:, :w_conv.shape[1], :], (1, 0, 2)).reshape(w_conv.shape[1], -1)}
    small = {n: w[n] for n in ("g_attn", "b_forget", "g_q_fox", "g_k_fox", "g_q_dil", "g_k_dil", "g_ffn")}
    small = {n: (a[0] if a.ndim == 3 else a) for n, a in small.items()}
    in_flight = {}

    def rest_pass(key, after):
        own, passed, tok = _gather_pass(groups[key], started[key], (after,), "gather_pass_" + key)
        in_flight[key] = (own, passed)
        return tok

    def rest_wait(key, after):
        own, passed = in_flight.pop(key)
        lands = _gather_wait(groups[key], passed, after, "gather_wait_" + key)
        return dict(zip(later[key], lands))

    reducer = _Reducer(jnp.stack([chip, ci.astype(jnp.int32)]))
    g_out, d_out, m_out, v_out = {}, {}, {}, {}
    reduced = {}

    def first_element(arrays):
        return jnp.stack([a[(0,) * a.ndim] for a in arrays])

    def update_big(n, deps):
        g2, dl, mn, vn = _adamw(w2[n], reduced[BIG[n]], m2[n], v2[n], name="adamw_" + n, deps=deps, emit_grad=True)
        g_out[n], d_out[n], m_out[n], v_out[n] = (a.reshape(w[n].shape) for a in (g2, dl, mn, vn))

    def update_down(tok):
        (reduced["down"],) = reducer.finish("down", tok)
        update_big("w_down", (tok,))
        return v_out["w_down"]

    small_grads, grad_x = _layer_grads(x[0], loss_target[0], small, wg, rest_pass, rest_wait, reducer, update_down)

    for key, members in (("up", ("up",)), ("mix", ("out", "brf", "brd"))):
        reduced.update(zip(members, reducer.finish(key, grad_x)))
    others = ("w_up", "w_out", "w_br_fox", "w_br_dil")
    for n in others:
        update_big(n, (grad_x,))

    flat = jnp.concatenate([small_grads[n].reshape(-1) for n in SMALL_ORDER])
    rows = _round_up(flat.shape[0], 8 * LANES) // LANES
    pack = jnp.pad(flat, (0, rows * LANES - flat.shape[0])).reshape(rows, LANES)
    packs = _gather_packs(pack, deps=(first_element([v_out[n] for n in others]),))
    total = _sum_devices(packs).reshape(-1)
    red, at = {}, 0
    for n in SMALL_ORDER:
        size = small_grads[n].size
        red[n] = total[at:at + size].reshape(small_grads[n].shape)
        at += size
    loss = red["loss"].reshape(())
    c2 = w_conv.shape[2]
    red["w_conv"] = lax.dynamic_slice_in_dim(red["w_conv"], chip * c2, c2, axis=1)

    smalls = [n for n in WEIGHT_ORDER if n not in BIG]
    for n in smalls:
        shape = w[n].shape
        r2 = (shape[-2], shape[-1]) if n not in ("g_attn", "b_forget", "g_ffn", "b_conv") else (1, shape[-1])
        g2 = red[n].reshape(r2)
        dl, mn, vn = _adamw(w[n].reshape(r2), g2, m[n].reshape(r2), v[n].reshape(r2), name="adamw_" + n)
        g_out[n], d_out[n], m_out[n], v_out[n] = (a.reshape(shape) for a in (g2, dl, mn, vn))
    tok = reducer.to_core("in", first_element([v_out[n] for n in smalls]))
    (reduced["in"],) = reducer.finish("in", tok)
    update_big("w_in", (tok,))

    return (loss, grad_x[None], *[g_out[n] for n in WEIGHT_ORDER], *[d_out[n] for n in WEIGHT_ORDER],
            *[m_out[n] for n in WEIGHT_ORDER], *[v_out[n] for n in WEIGHT_ORDER])
```

```python
import math

import jax
import jax.numpy as jnp
import numpy as np
from jax import lax
from jax.experimental import pallas as pl
from jax.experimental.pallas import tpu as pltpu

F32 = jnp.float32
BF16 = jnp.bfloat16
HEAD_DIM = 128
N_HEADS = 8
EPS = 1e-6
NEG = -1e30
LOG2E = math.log2(math.e)
N_CHIPS = 4
N_DEV = 8
LANES = 128
VMEM_LIMIT_BYTES = 56 * 1024 * 1024
DIL_PATTERNS = ((128, 1), (512, 4), (2048, 16))
ATTN_TILE = 512
MM_TILE = 1024
ADAM_LR, ADAM_B1, ADAM_B2, ADAM_EPS, ADAM_WD, ADAM_STEP = 0.001, 0.9, 0.999, 1e-08, 0.01, 10
MESH = pl.DeviceIdType.MESH


def _params(*sem):
    return pltpu.CompilerParams(dimension_semantics=sem, vmem_limit_bytes=VMEM_LIMIT_BYTES)


def _round_up(n, m):
    return -(-n // m) * m


def _pick(dim, prefs):
    for p in prefs:
        if dim % p == 0:
            return p
    raise ValueError(f"no tile for {dim} in {prefs}")


def _logical_shape(arr, kind):
    if kind is None:
        return arr.shape
    s, r, c = arr.shape
    return (r, s * c) if kind == "col" else (s * r, c)


def _spec(shape, kind, br, bc, fi, fj):
    if kind is None:
        return pl.BlockSpec((br, bc), lambda *g: (fi(*g), fj(*g)))
    _, r, c = shape
    if kind == "col":
        nb = c // bc
        assert nb * bc == c, (shape, bc)
        return pl.BlockSpec((None, br, bc), lambda *g: (fj(*g) // nb, fi(*g), fj(*g) % nb))
    nb = r // br
    assert nb * br == r, (shape, br)
    return pl.BlockSpec((None, br, bc), lambda *g: (fi(*g) // nb, fi(*g) % nb, fj(*g)))


def _mm(a, b, *, mode, tm, tn, tk, name, a_kind=None, b_kind=None, out_kind=None,
        out_dtype=F32, res=None, deps=(), loss_target=None):
    pair_a, pair_b = isinstance(a, tuple), isinstance(b, tuple)
    if pair_a or pair_b:
        return _mm_pair(a, b, mode=mode, tm=tm, tn=tn, tk=tk, name=name, b_kind=b_kind, out_kind=out_kind,
                        out_dtype=out_dtype, deps=deps)
    la, lb = _logical_shape(a, a_kind), _logical_shape(b, b_kind)
    if mode == "nn":
        (m, k), (k2, n) = la, lb
    elif mode == "nt":
        (m, k), (n, k2) = la, lb
    else:
        (k, m), (k2, n) = la, lb
    assert k == k2, (name, la, lb)
    assert m % tm == 0 and n % tn == 0 and k % tk == 0, (name, m, n, k, tm, tn, tk)
    nk = k // tk
    im = lambda i, j, l: i
    jn = lambda i, j, l: j
    lk = lambda i, j, l: l
    if mode == "tn":
        a_spec = _spec(a.shape, a_kind, tk, tm, lk, im)
        dims = (((0,), (0,)), ((), ()))
    else:
        a_spec = _spec(a.shape, a_kind, tm, tk, im, lk)
        dims = (((1,), (1,)), ((), ())) if mode == "nt" else (((1,), (0,)), ((), ()))
    if mode == "nt":
        b_spec = _spec(b.shape, b_kind, tn, tk, jn, lk)
    else:
        b_spec = _spec(b.shape, b_kind, tk, tn, lk, jn)
    if out_kind is None:
        oshape = (m, n)
    elif out_kind == "col":
        oshape = (N_CHIPS, m, n // N_CHIPS)
    else:
        oshape = (N_CHIPS, m // N_CHIPS, n)
    o_spec = _spec(oshape, out_kind, tm, tn, im, jn)
    tile = pl.BlockSpec((tm, tn), lambda i, j, l: (i, j))
    in_specs = [a_spec, b_spec]
    args = [a, b]
    for extra in (res, loss_target):
        if extra is not None:
            in_specs.append(tile)
            args.append(extra)
    in_specs += [pl.BlockSpec(memory_space=pl.ANY)] * len(deps)
    args += list(deps)
    if loss_target is None:
        out_specs, out_shape = [o_spec], [jax.ShapeDtypeStruct(oshape, out_dtype)]
    else:
        assert out_kind is None and res is not None
        out_specs = [tile, tile, pl.BlockSpec((8, LANES), lambda i, j, l: (0, 0))]
        out_shape = [jax.ShapeDtypeStruct(oshape, F32), jax.ShapeDtypeStruct(oshape, BF16),
                     jax.ShapeDtypeStruct((8, LANES), F32)]
    n_in, n_out = len(args), len(out_specs)

    def finish(out, refs, first):
        res_ref = refs[2] if res is not None else None
        outs = refs[n_in:n_in + n_out]
        if res_ref is not None:
            out = out + res_ref[...]
        if loss_target is None:
            outs[0][...] = out.astype(outs[0].dtype)
            return

        @pl.when(first)
        def _():
            outs[2][...] = jnp.zeros_like(outs[2])

        err = out - refs[3][...]
        dy = err * (1.0 / n)
        outs[0][...] = dy
        outs[1][...] = dy.astype(BF16)
        outs[2][...] += 0.5 * jnp.sum(jnp.sum(err * err, axis=-1, keepdims=True) * (1.0 / n), axis=0, keepdims=True)

    def first_tile():
        return (pl.program_id(0) == 0) & (pl.program_id(1) == 0)

    def body_whole_k(*refs):
        finish(lax.dot_general(refs[0][...], refs[1][...], dims, preferred_element_type=F32), refs, first_tile())

    def body(*refs):
        acc_ref = refs[-1]
        step = pl.program_id(2)
        first = first_tile()

        @pl.when(step == 0)
        def _():
            acc_ref[...] = jnp.zeros_like(acc_ref)

        acc_ref[...] += lax.dot_general(refs[0][...], refs[1][...], dims, preferred_element_type=F32)

        @pl.when(step == nk - 1)
        def _():
            finish(acc_ref[...], refs, first)

    outs = pl.pallas_call(
        body_whole_k if nk == 1 else body, name=name, grid=(m // tm, n // tn, nk),
        in_specs=in_specs, out_specs=out_specs, out_shape=out_shape,
        scratch_shapes=[] if nk == 1 else [pltpu.VMEM((tm, tn), F32)],
        compiler_params=_params(*(["arbitrary"] * 3 if loss_target is not None else ["parallel", "parallel", "arbitrary"])),
    )(*args)
    return outs[0] if loss_target is None else outs


def _mm_pair(a, b, *, mode, tm, tn, tk, name, b_kind, out_kind, out_dtype, deps):
    anyspec = [pl.BlockSpec(memory_space=pl.ANY)] * len(deps)
    if mode == "tn":
        assert isinstance(b, tuple) and out_kind == "col" and a.shape[0] == tk
        k, m = a.shape
        n0 = b[0].shape[1]
        n, nb0 = 2 * n0, n0 // tn
        oshape = (N_CHIPS, m, n // N_CHIPS)

        def body(a_ref, b0_ref, b1_ref, *rest):
            o_ref = rest[-1]
            for first, b_ref in ((True, b0_ref), (False, b1_ref)):
                @pl.when((pl.program_id(1) < nb0) == first)
                def _():
                    o_ref[...] = lax.dot_general(a_ref[...], b_ref[...], (((0,), (0,)), ((), ())),
                                                 preferred_element_type=F32).astype(o_ref.dtype)

        return pl.pallas_call(
            body, name=name, grid=(m // tm, n // tn),
            in_specs=[pl.BlockSpec((tk, tm), lambda i, j: (0, i)),
                      pl.BlockSpec((tk, tn), lambda i, j: (0, jnp.minimum(j, nb0 - 1))),
                      pl.BlockSpec((tk, tn), lambda i, j: (0, jnp.maximum(j - nb0, 0)))] + anyspec,
            out_specs=_spec(oshape, "col", tm, tn, lambda i, j: i, lambda i, j: j),
            out_shape=jax.ShapeDtypeStruct(oshape, out_dtype), compiler_params=_params("parallel", "arbitrary"),
        )(a, *b, *deps)
    assert mode == "nt" and isinstance(a, tuple) and out_kind is None
    m, k0 = a[0].shape
    n = _logical_shape(b, b_kind)[0]
    nk0 = k0 // tk
    nk = 2 * nk0

    def body(a0_ref, a1_ref, b_ref, *rest):
        o_ref, acc_ref = rest[-2], rest[-1]
        step = pl.program_id(2)

        @pl.when(step == 0)
        def _():
            acc_ref[...] = jnp.zeros_like(acc_ref)

        for first, a_ref in ((True, a0_ref), (False, a1_ref)):
            @pl.when((step < nk0) == first)
            def _():
                acc_ref[...] += lax.dot_general(a_ref[...], b_ref[...], (((1,), (1,)), ((), ())), preferred_element_type=F32)

        @pl.when(step == nk - 1)
        def _():
            o_ref[...] = acc_ref[...].astype(o_ref.dtype)

    return pl.pallas_call(
        body, name=name, grid=(m // tm, n // tn, nk),
        in_specs=[pl.BlockSpec((tm, tk), lambda i, j, l: (i, jnp.minimum(l, nk0 - 1))),
                  pl.BlockSpec((tm, tk), lambda i, j, l: (i, jnp.maximum(l - nk0, 0))),
                  _spec(b.shape, b_kind, tn, tk, lambda i, j, l: j, lambda i, j, l: l)] + anyspec,
        out_specs=pl.BlockSpec((tm, tn), lambda i, j, l: (i, j)),
        out_shape=jax.ShapeDtypeStruct((m, n), out_dtype), scratch_shapes=[pltpu.VMEM((tm, tn), F32)],
        compiler_params=_params("parallel", "parallel", "arbitrary"),
    )(*a, b, *deps)


def _pieces(splits, cs, cp):
    out, g0 = [], 0
    for width in splits:
        g1, runs = g0 + width, []
        for j in range(N_CHIPS):
            a, b = max(g0, cs * j), min(g1, cs * (j + 1))
            if a < b:
                runs.append((j * cp + a - cs * j, a - g0, b - a))
        out.append(runs)
        g0 = g1
    return out


def _head_norm(xv, gv):
    r = lax.rsqrt(jnp.mean(xv * xv, axis=-1, keepdims=True) + EPS)
    return (xv * r) * gv


def _head_norm_bwd(dyv, xv, gv):
    r = lax.rsqrt(jnp.mean(xv * xv, axis=-1, keepdims=True) + EPS)
    xr = xv * r
    gdy = dyv * gv
    return r * (gdy - xr * jnp.mean(gdy * xr, axis=-1, keepdims=True)), jnp.sum(dyv * xr, axis=0, keepdims=True)


def _proj_split(proj_p, splits, cs, dtypes, gains, tm=128):
    s, wp = proj_p.shape
    pieces = _pieces(splits, cs, wp // N_CHIPS)
    normed = sorted(gains)
    nseg = len(splits)

    def body(p_ref, *refs):
        g_refs, o_refs, n_refs = refs[:len(normed)], refs[len(normed):len(normed) + nseg], refs[len(normed) + nseg:]
        for o_ref, runs in zip(o_refs, pieces):
            for src, dst, n in runs:
                o_ref[:, dst:dst + n] = p_ref[:, src:src + n].astype(o_ref.dtype)
        for g_ref, n_ref, i in zip(g_refs, n_refs, normed):
            for c0 in range(0, splits[i], HEAD_DIM):
                cols = slice(c0, c0 + HEAD_DIM)
                n_ref[:, cols] = _head_norm(o_refs[i][:, cols], g_ref[:, cols]).astype(n_ref.dtype)

    return pl.pallas_call(
        body, name="proj_split", grid=(s // tm,),
        in_specs=[pl.BlockSpec((tm, wp), lambda i: (i, 0))] + [pl.BlockSpec((1, splits[i]), lambda i: (0, 0)) for i in normed],
        out_specs=[pl.BlockSpec((tm, w), lambda i: (i, 0)) for w in splits]
        + [pl.BlockSpec((tm, splits[i]), lambda i: (i, 0)) for i in normed],
        out_shape=[jax.ShapeDtypeStruct((s, w), dt) for w, dt in zip(splits, dtypes)]
        + [jax.ShapeDtypeStruct((s, splits[i]), BF16) for i in normed],
        compiler_params=_params("parallel"),
    )(proj_p, *[gains[i] for i in normed])


def _dproj_merge(parts, splits, cs, cp, norms, tm=128):
    s = parts[0].shape[0]
    wp = N_CHIPS * cp
    pieces = _pieces(splits, cs, cp)
    normed = sorted(norms)
    nseg, nn = len(splits), len(normed)

    def body(*refs):
        p_refs, x_refs, g_refs = refs[:nseg], refs[nseg:nseg + nn], refs[nseg + nn:nseg + 2 * nn]
        o_ref, dg_refs = refs[nseg + 2 * nn], refs[nseg + 2 * nn + 1:nseg + 3 * nn + 1]
        stage, tmp = refs[-2], refs[-1]

        @pl.when(pl.program_id(0) == 0)
        def _():
            for dg_ref in dg_refs:
                dg_ref[...] = jnp.zeros_like(dg_ref)

        for j in range(N_CHIPS):
            stage[:, j * cp + cs:(j + 1) * cp] = jnp.zeros((tm, cp - cs), F32)
        for i, (p_ref, runs) in enumerate(zip(p_refs, pieces)):
            src_ref = p_ref
            if i in norms:
                k = normed.index(i)
                for c0 in range(0, splits[i], HEAD_DIM):
                    cols = slice(c0, c0 + HEAD_DIM)
                    dx, dg = _head_norm_bwd(p_ref[:, cols].astype(F32), x_refs[k][:, cols], g_refs[k][:, cols])
                    tmp[:, cols] = dx
                    dg_refs[k][:, cols] += dg
                src_ref = tmp
            for dst, src, n in runs:
                stage[:, dst:dst + n] = src_ref[:, src:src + n].astype(F32)
        o_ref[...] = stage[...].astype(o_ref.dtype)

    wmax = max(splits[i] for i in normed)
    row = lambda w: pl.BlockSpec((tm, w), lambda i: (i, 0))
    vec = lambda w: pl.BlockSpec((1, w), lambda i: (0, 0))
    outs = pl.pallas_call(
        body, name="dproj_merge", grid=(s // tm,),
        in_specs=[row(w) for w in splits] + [row(splits[i]) for i in normed] + [vec(splits[i]) for i in normed],
        out_specs=[row(wp)] + [vec(splits[i]) for i in normed],
        out_shape=[jax.ShapeDtypeStruct((s, wp), BF16)] + [jax.ShapeDtypeStruct((1, splits[i]), F32) for i in normed],
        scratch_shapes=[pltpu.VMEM((tm, wp), F32), pltpu.VMEM((tm, wmax), F32)],
        compiler_params=_params("arbitrary"),
    )(*parts, *[norms[i][0] for i in normed], *[norms[i][1] for i in normed])
    return outs[0], dict(zip(normed, outs[1:]))


def _norm_fwd(x, g, *, group, name, tm=256):
    s, w = x.shape
    ng = w // group

    def body(x_ref, g_ref, o_ref):
        for i in range(ng):
            cols = slice(i * group, (i + 1) * group)
            xv = x_ref[:, cols]
            r = lax.rsqrt(jnp.mean(xv * xv, axis=-1, keepdims=True) + EPS)
            o_ref[:, cols] = ((xv * r) * g_ref[:, cols]).astype(o_ref.dtype)

    return pl.pallas_call(
        body, name=name, grid=(s // tm,),
        in_specs=[pl.BlockSpec((tm, w), lambda i: (i, 0)), pl.BlockSpec((1, w), lambda i: (0, 0))],
        out_specs=pl.BlockSpec((tm, w), lambda i: (i, 0)),
        out_shape=jax.ShapeDtypeStruct((s, w), BF16),
        compiler_params=_params("parallel"),
    )(x, g)


def _norm_bwd(dy, x, g, *, group, name, res=None, out_dtypes=(BF16,), tm=256, deps=()):
    s, w = x.shape
    ng = w // group
    n_in = 4 if res is not None else 3

    def body(*refs):
        dy_ref, x_ref, g_ref = refs[:3]
        res_ref = refs[3] if res is not None else None
        outs = refs[n_in + len(deps):]
        dx_refs, dg_ref = outs[:-1], outs[-1]

        @pl.when(pl.program_id(0) == 0)
        def _():
            dg_ref[...] = jnp.zeros_like(dg_ref)

        for i in range(ng):
            cols = slice(i * group, (i + 1) * group)
            xv = x_ref[:, cols]
            dyv = dy_ref[:, cols].astype(F32)
            r = lax.rsqrt(jnp.mean(xv * xv, axis=-1, keepdims=True) + EPS)
            xr = xv * r
            dg_ref[:, cols] += jnp.sum(dyv * xr, axis=0, keepdims=True)
            gdy = dyv * g_ref[:, cols]
            dx = r * (gdy - xr * jnp.mean(gdy * xr, axis=-1, keepdims=True))
            if res_ref is not None:
                dx = dx + res_ref[:, cols]
            for dx_ref in dx_refs:
                dx_ref[:, cols] = dx.astype(dx_ref.dtype)

    row = pl.BlockSpec((tm, w), lambda i: (i, 0))
    vec = pl.BlockSpec((1, w), lambda i: (0, 0))
    in_specs = [row, row, vec] + ([row] if res is not None else []) + [pl.BlockSpec(memory_space=pl.ANY)] * len(deps)
    args = [dy, x, g] + ([res] if res is not None else []) + list(deps)
    out_specs = [row] * len(out_dtypes) + [vec]
    out_shape = [jax.ShapeDtypeStruct((s, w), dt) for dt in out_dtypes] + [jax.ShapeDtypeStruct((1, w), F32)]
    return pl.pallas_call(
        body, name=name, grid=(s // tm,), in_specs=in_specs, out_specs=out_specs,
        out_shape=out_shape, compiler_params=_params("arbitrary"),
    )(*args)


def _split3(v):
    p1 = v.astype(BF16)
    r1 = v - p1.astype(F32)
    p2 = r1.astype(BF16)
    p3 = (r1 - p2.astype(F32)).astype(BF16)
    return p1, p2, p3


def _tri_sum(v, reverse, tcol=512):
    h, s = v.shape
    tcol = min(tcol, s)
    parts = _split3(v)
    outs = []
    for j in range(s // tcol):
        src = lax.broadcasted_iota(jnp.int32, (s, tcol), 0)
        dst = lax.broadcasted_iota(jnp.int32, (s, tcol), 1) + j * tcol
        keep = (src >= dst) if reverse else (src <= dst)
        tri = jnp.where(keep, 1.0, 0.0).astype(BF16)
        acc = jnp.zeros((h, tcol), F32)
        for p in parts:
            acc = acc + jnp.dot(p, tri, preferred_element_type=F32)
        outs.append(acc)
    return outs


def _forget_fwd(fa_t, b):
    h, s = fa_t.shape
    tcol = min(512, s)

    def body(f_ref, b_ref, c_ref):
        z = f_ref[...] + b_ref[...]
        logf = jnp.minimum(z, 0.0) - jnp.log(1.0 + jnp.exp(-jnp.abs(z)))
        for j, blk in enumerate(_tri_sum(logf, reverse=False, tcol=tcol)):
            c_ref[:, j * tcol:(j + 1) * tcol] = blk

    return pl.pallas_call(
        body, name="forget_fwd", out_shape=jax.ShapeDtypeStruct((h, s), F32),
        compiler_params=_params(),
    )(fa_t, b)


def _forget_bwd(dacol, fa_t, b):
    h, s = fa_t.shape
    tcol = min(512, s)

    def body(d_ref, f_ref, b_ref, dfa_ref, db_ref):
        z = f_ref[...] + b_ref[...]
        dc = -d_ref[...]
        total = jnp.zeros((h, 1), F32)
        for j, blk in enumerate(_tri_sum(dc, reverse=True, tcol=tcol)):
            cols = slice(j * tcol, (j + 1) * tcol)
            dfa = blk * (1.0 - jax.nn.sigmoid(z[:, cols]))
            dfa_ref[:, cols] = dfa
            total = total + jnp.sum(dfa, axis=-1, keepdims=True)
        db_ref[...] = total

    return pl.pallas_call(
        body, name="forget_bwd",
        out_shape=[jax.ShapeDtypeStruct((h, s), F32), jax.ShapeDtypeStruct((h, 1), F32)],
        compiler_params=_params(),
    )(dacol, fa_t, b)


def _distance_bias(s, tile, dilated):
    nb = s // tile
    b = lax.broadcasted_iota(jnp.int32, (nb, tile, tile), 0)
    dist = b * tile + lax.broadcasted_iota(jnp.int32, (nb, tile, tile), 1) - lax.broadcasted_iota(jnp.int32, (nb, tile, tile), 2)
    if not dilated:
        return jnp.where(dist >= 0, 0.0, NEG).astype(F32)
    mult = jnp.zeros(dist.shape, jnp.int32)
    for window, dil in DIL_PATTERNS:
        mult = mult + ((dist >= 0) & (dist <= window) & ((dist & (dil - 1)) == 0)).astype(jnp.int32)
    logm = jnp.where(mult == 3, math.log2(3.0), jnp.where(mult == 2, 1.0, 0.0))
    return jnp.where(mult > 0, logm, NEG).astype(F32)


def _logits(q, k, arow, acol, bias):
    s = lax.dot_general(q, k, (((1,), (1,)), ((), ())), preferred_element_type=F32)
    return s * (LOG2E / math.sqrt(HEAD_DIM)) + arow - acol + bias


def _attn_fwd(q, k, v, arow, acol, *, dilated, name, tq=ATTN_TILE, tk=ATTN_TILE):
    two_term = not dilated
    s, w = q.shape
    nh = w // HEAD_DIM
    assert tq == tk
    tq = tk = min(tq, s)
    nq, nk = s // tq, s // tk

    pairs = [(i, j) for i in range(nq) for j in range(i + 1)]
    q_of, k_of = (jnp.asarray(t, jnp.int32) for t in zip(*pairs))

    def body(qo_ref, ko_ref, q_ref, k_ref, v_ref, ar_ref, ac_ref, b_ref, o_ref, of_ref, lse_ref, m_ref, l_ref, acc_ref):
        t = pl.program_id(1)
        qi, ki = qo_ref[t], ko_ref[t]

        @pl.when(ki == 0)
        def _():
            m_ref[...] = jnp.full_like(m_ref, NEG)
            l_ref[...] = jnp.zeros_like(l_ref)
            acc_ref[...] = jnp.zeros_like(acc_ref)

        sc = _logits(q_ref[...], k_ref[...], ar_ref[...], ac_ref[...], b_ref[...])
        m_new = jnp.maximum(m_ref[...], jnp.max(sc, axis=-1, keepdims=True))
        alpha = jnp.exp2(m_ref[...] - m_new)
        p = jnp.exp2(sc - m_new)
        l_ref[...] = alpha * l_ref[...] + jnp.sum(p, axis=-1, keepdims=True)
        p_hi = p.astype(BF16)
        vv = v_ref[...]
        pv = jnp.dot(p_hi, vv, preferred_element_type=F32)
        if two_term:
            pv = pv + jnp.dot((p - p_hi.astype(F32)).astype(BF16), vv, preferred_element_type=F32)
        acc_ref[...] = alpha * acc_ref[...] + pv
        m_ref[...] = m_new

        @pl.when(ki == qi)
        def _():
            out = acc_ref[...] / l_ref[...]
            o_ref[...] = out.astype(o_ref.dtype)
            of_ref[...] = out
            lse_ref[...] = m_ref[...] + jnp.log2(l_ref[...])

    qs = pl.BlockSpec((tq, HEAD_DIM), lambda h, t, qo, ko: (qo[t], h))
    kv = pl.BlockSpec((tk, HEAD_DIM), lambda h, t, qo, ko: (ko[t], h))
    rowv = pl.BlockSpec((None, tq, 1), lambda h, t, qo, ko: (h, qo[t], 0))
    return pl.pallas_call(
        body, name=name,
        grid_spec=pltpu.PrefetchScalarGridSpec(
            num_scalar_prefetch=2, grid=(nh, len(pairs)),
            in_specs=[qs, kv, kv, rowv,
                      pl.BlockSpec((None, 1, tk), lambda h, t, qo, ko: (h, 0, ko[t])),
                      pl.BlockSpec((None, tq, tk), lambda h, t, qo, ko: (qo[t] - ko[t], 0, 0))],
            out_specs=[qs, qs, rowv],
            scratch_shapes=[pltpu.VMEM((tq, 1), F32), pltpu.VMEM((tq, 1), F32), pltpu.VMEM((tq, HEAD_DIM), F32)]),
        out_shape=[jax.ShapeDtypeStruct((s, w), BF16), jax.ShapeDtypeStruct((s, w), F32),
                   jax.ShapeDtypeStruct((nh, s, 1), F32)],
        compiler_params=_params("parallel", "arbitrary"),
    )(q_of, k_of, q, k, v, arow * LOG2E, acol * LOG2E, _distance_bias(s, tq, dilated))


def _attn_bwd(q, k, v, o, do, lse, arow, acol, *, dilated, name, tq=ATTN_TILE, tk=ATTN_TILE):
    s, w = q.shape
    nh = w // HEAD_DIM
    assert tq == tk
    tq = tk = min(tq, s)
    nq, nk = s // tq, s // tk
    scale = 1.0 / math.sqrt(HEAD_DIM)

    pairs = [(i, j) for j in range(nk) for i in range(j, nq)]
    q_of, k_of = (jnp.asarray(t, jnp.int32) for t in zip(*pairs))

    def body(qo_ref, ko_ref, q_ref, k_ref, v_ref, o_ref, do_ref, lse_ref, ar_ref, ac_ref, b_ref,
             dq_ref, dk_ref, dv_ref, dac_ref, dk_acc, dv_acc, dac_acc):
        t = pl.program_id(1)
        qi, ki = qo_ref[t], ko_ref[t]

        @pl.when(t == 0)
        def _():
            dq_ref[...] = jnp.zeros_like(dq_ref)

        @pl.when(qi == ki)
        def _():
            dk_acc[...] = jnp.zeros_like(dk_acc)
            dv_acc[...] = jnp.zeros_like(dv_acc)
            dac_acc[...] = jnp.zeros_like(dac_acc)

        qv, kvv, dov = q_ref[...], k_ref[...], do_ref[...]
        sc = _logits(qv, kvv, ar_ref[...], ac_ref[...], b_ref[...])
        p = jnp.exp2(sc - lse_ref[...])
        dp = lax.dot_general(dov, v_ref[...], (((1,), (1,)), ((), ())), preferred_element_type=F32)
        delta = jnp.sum(dov.astype(F32) * o_ref[...].astype(F32), axis=-1, keepdims=True)
        ds = p * (dp - delta)
        dsb = ds.astype(BF16)
        dv_acc[...] += lax.dot_general(p.astype(BF16), dov, (((0,), (0,)), ((), ())), preferred_element_type=F32)
        dk_acc[...] += lax.dot_general(dsb, qv, (((0,), (0,)), ((), ())), preferred_element_type=F32)
        rows = pl.ds(pl.multiple_of(qi * tq, tq), tq)
        dq_ref[rows, :] += jnp.dot(dsb, kvv, preferred_element_type=F32) * scale
        dac_acc[...] += jnp.sum(ds, axis=0, keepdims=True)

        @pl.when(qi == nq - 1)
        def _():
            dk_ref[...] = dk_acc[...] * scale
            dv_ref[...] = dv_acc[...]
            dac_ref[...] = dac_acc[...]

    qs = pl.BlockSpec((tq, HEAD_DIM), lambda h, t, qo, ko: (qo[t], h))
    ks = pl.BlockSpec((tk, HEAD_DIM), lambda h, t, qo, ko: (ko[t], h))
    rowv = pl.BlockSpec((None, tq, 1), lambda h, t, qo, ko: (h, qo[t], 0))
    colv = pl.BlockSpec((None, 1, tk), lambda h, t, qo, ko: (h, 0, ko[t]))
    return pl.pallas_call(
        body, name=name,
        grid_spec=pltpu.PrefetchScalarGridSpec(
            num_scalar_prefetch=2, grid=(nh, len(pairs)),
            in_specs=[qs, ks, ks, qs, qs, rowv, rowv, colv,
                      pl.BlockSpec((None, tq, tk), lambda h, t, qo, ko: (qo[t] - ko[t], 0, 0))],
            out_specs=[pl.BlockSpec((s, HEAD_DIM), lambda h, t, qo, ko: (0, h)), ks, ks, colv],
            scratch_shapes=[pltpu.VMEM((tk, HEAD_DIM), F32), pltpu.VMEM((tk, HEAD_DIM), F32), pltpu.VMEM((1, tk), F32)]),
        out_shape=[jax.ShapeDtypeStruct((s, w), F32), jax.ShapeDtypeStruct((s, w), F32),
                   jax.ShapeDtypeStruct((s, w), F32), jax.ShapeDtypeStruct((nh, 1, s), F32)],
        compiler_params=_params("arbitrary", "arbitrary"),
    )(q_of, k_of, q, k, v, o, do, lse, arow * LOG2E, acol * LOG2E, _distance_bias(s, tq, dilated))


def _gate_fwd(ga, gb, pa, pb, tm=256):
    s, d = ga.shape

    def body(ga_ref, gb_ref, pa_ref, pb_ref, o_ref):
        o_ref[...] = (jax.nn.sigmoid(ga_ref[...]) * pa_ref[...]
                      + jax.nn.sigmoid(gb_ref[...]) * pb_ref[...]).astype(o_ref.dtype)

    row = pl.BlockSpec((tm, d), lambda i: (i, 0))
    return pl.pallas_call(
        body, name="gate_fwd", grid=(s // tm,), in_specs=[row] * 4, out_specs=row,
        out_shape=jax.ShapeDtypeStruct((s, d), BF16), compiler_params=_params("parallel"),
    )(ga, gb, pa, pb)


def _gate_bwd(dm, ga, gb, pa, pb, tm=256):
    s, d = ga.shape

    def body(dm_ref, ga_ref, gb_ref, pa_ref, pb_ref, dpa_ref, dpb_ref, dga_ref, dgb_ref):
        dmv = dm_ref[...]
        for g_ref, p_ref, dp_ref, dg_ref in ((ga_ref, pa_ref, dpa_ref, dga_ref), (gb_ref, pb_ref, dpb_ref, dgb_ref)):
            sg = jax.nn.sigmoid(g_ref[...])
            dp_ref[...] = (dmv * sg).astype(BF16)
            dg_ref[...] = (dmv * p_ref[...] * (sg * (1.0 - sg))).astype(BF16)

    row = pl.BlockSpec((tm, d), lambda i: (i, 0))
    return pl.pallas_call(
        body, name="gate_bwd", grid=(s // tm,), in_specs=[row] * 5, out_specs=[row] * 4,
        out_shape=[jax.ShapeDtypeStruct((s, d), BF16)] * 4, compiler_params=_params("parallel"),
    )(dm, ga, gb, pa, pb)


def _shift_down(u, k):
    row = lax.broadcasted_iota(jnp.int32, u.shape, 0)
    return jnp.where(row >= k, pltpu.roll(u, k, 0), 0.0)


def _shift_up(u, k):
    n = u.shape[0]
    row = lax.broadcasted_iota(jnp.int32, u.shape, 0)
    return jnp.where(row < n - k, pltpu.roll(u, n - k, 0), 0.0)


def _conv3(u, wc, b):
    return wc[0:1, :] * _shift_down(u, 2) + wc[1:2, :] * _shift_down(u, 1) + wc[2:3, :] * u + b


def _conv_glu_fwd(u, wc, b, tn=256):
    s, f2 = u.shape
    f = f2 // 2
    nb = f // tn

    def body(ug_ref, uv_ref, wg_ref, wv_ref, bg_ref, bv_ref, o_ref):
        cg = _conv3(ug_ref[...], wg_ref[...], bg_ref[...])
        cv = _conv3(uv_ref[...], wv_ref[...], bv_ref[...])
        o_ref[...] = (cg * jax.nn.sigmoid(cg) * cv).astype(o_ref.dtype)

    def cols(rows, off):
        return pl.BlockSpec((rows, tn), lambda j: (0, j + off))

    return pl.pallas_call(
        body, name="conv_glu_fwd", grid=(nb,),
        in_specs=[cols(s, 0), cols(s, nb), cols(3, 0), cols(3, nb), cols(1, 0), cols(1, nb)],
        out_specs=cols(s, 0), out_shape=jax.ShapeDtypeStruct((s, f), BF16),
        compiler_params=_params("parallel"),
    )(u, u, wc, wc, b, b)


def _conv_glu_bwd(u, da, wc, b, tn=256):
    s, f2 = u.shape
    f = f2 // 2
    nb = f // tn

    def body(ug_ref, uv_ref, da_ref, wg_ref, wv_ref, bg_ref, bv_ref, dug_ref, duv_ref, sg_ref, sv_ref):
        ug, uv, wg, wv = ug_ref[...], uv_ref[...], wg_ref[...], wv_ref[...]
        cg = _conv3(ug, wg, bg_ref[...])
        cv = _conv3(uv, wv, bv_ref[...])
        sig = jax.nn.sigmoid(cg)
        dav = da_ref[...]
        dcv = dav * (cg * sig)
        dcg = dav * cv * (sig * (1.0 + cg * (1.0 - sig)))
        for dc, uu, w, du_ref, st_ref in ((dcg, ug, wg, dug_ref, sg_ref), (dcv, uv, wv, duv_ref, sv_ref)):
            du = w[2:3, :] * dc + w[1:2, :] * _shift_up(dc, 1) + w[0:1, :] * _shift_up(dc, 2)
            du_ref[...] = du.astype(BF16)
            st_ref[...] = jnp.zeros_like(st_ref)
            st_ref[0:1, :] = jnp.sum(dc * _shift_down(uu, 2), axis=0, keepdims=True)
            st_ref[1:2, :] = jnp.sum(dc * _shift_down(uu, 1), axis=0, keepdims=True)
            st_ref[2:3, :] = jnp.sum(dc * uu, axis=0, keepdims=True)
            st_ref[3:4, :] = jnp.sum(dc, axis=0, keepdims=True)

    def cols(rows, off):
        return pl.BlockSpec((rows, tn), lambda j: (0, j + off))

    return pl.pallas_call(
        body, name="conv_glu_bwd", grid=(nb,),
        in_specs=[cols(s, 0), cols(s, nb), cols(s, 0), cols(3, 0), cols(3, nb), cols(1, 0), cols(1, nb)],
        out_specs=[cols(s, 0), cols(s, 0), cols(8, 0), cols(8, 0)],
        out_shape=[jax.ShapeDtypeStruct((s, f), BF16), jax.ShapeDtypeStruct((s, f), BF16),
                   jax.ShapeDtypeStruct((8, f), F32), jax.ShapeDtypeStruct((8, f), F32)],
        compiler_params=_params("parallel"),
    )(u, u, da, wc, wc, b, b)


ROW_TILES = (256, 128, 64, 32, 16, 8)
BLOCK_BYTES = 2 << 20


def _add_halves(g, r1, place):
    ns, r, c = g.shape
    rh = r // 2
    tr = _pick(rh, ROW_TILES)
    g4 = g.reshape(ns, 2, rh, c)

    def body(p_ref, g_ref, r_ref, o_ref):
        o_ref[...] = (g_ref[...].astype(F32) + r_ref[...].astype(F32)).astype(o_ref.dtype)

    def slab(s, pr):
        return s + (s >= pr[0]).astype(jnp.int32)

    return pl.pallas_call(
        body, name="add_halves",
        grid_spec=pltpu.PrefetchScalarGridSpec(
            num_scalar_prefetch=1, grid=(ns - 1, rh // tr),
            in_specs=[pl.BlockSpec((None, None, tr, c), lambda s, i, pr: (slab(s, pr), pr[1], i, 0)),
                      pl.BlockSpec((None, tr, c), lambda s, i, pr: (slab(s, pr), i, 0))],
            out_specs=pl.BlockSpec((None, tr, c), lambda s, i, pr: (slab(s, pr), i, 0))),
        out_shape=jax.ShapeDtypeStruct((ns, rh, c), BF16),
        compiler_params=_params("parallel", "parallel"),
    )(place, g4, r1)


def _sum_chips(g, r1, recv, place):
    ns, r, c = g.shape
    rh = r // 2
    tr = _pick(rh, ROW_TILES)
    g4 = g.reshape(ns, 2, rh, c)

    def body(p_ref, g_ref, r_ref, t0_ref, t1_ref, t2_ref, o_ref):
        own = g_ref[...].astype(F32) + r_ref[...].astype(F32)
        o_ref[...] = ((own + t0_ref[...].astype(F32)) + t1_ref[...].astype(F32)) + t2_ref[...].astype(F32)

    def peer(k):
        return pl.BlockSpec((None, tr, c), lambda i, pr: (k, i, 0))

    return pl.pallas_call(
        body, name="sum_chips",
        grid_spec=pltpu.PrefetchScalarGridSpec(
            num_scalar_prefetch=1, grid=(rh // tr,),
            in_specs=[pl.BlockSpec((None, None, tr, c), lambda i, pr: (pr[0], pr[1], i, 0)),
                      pl.BlockSpec((None, tr, c), lambda i, pr: (pr[0], i, 0)), peer(0), peer(1), peer(2)],
            out_specs=pl.BlockSpec((tr, c), lambda i, pr: (pr[1] * (rh // tr) + i, 0))),
        out_shape=jax.ShapeDtypeStruct((r, c), F32),
        compiler_params=_params("parallel"),
    )(place, g4, r1, recv, recv, recv)


def _sum_devices(packs):
    n, r, c = packs.shape

    def body(p_ref, o_ref):
        acc = p_ref[0]
        for d in range(1, n):
            acc = acc + p_ref[d]
        o_ref[...] = acc

    return pl.pallas_call(
        body, name="sum_devices", out_shape=jax.ShapeDtypeStruct((r, c), F32), compiler_params=_params(),
    )(packs)


def _adamw_update(wv, gv, mv, vv):
    c1 = 1.0 - ADAM_B1 ** ADAM_STEP
    c2 = 1.0 - ADAM_B2 ** ADAM_STEP
    mn = ADAM_B1 * mv + (1.0 - ADAM_B1) * gv
    vn = ADAM_B2 * vv + (1.0 - ADAM_B2) * (gv * gv)
    m_hat = mn / c1
    v_hat = vn / c2
    return -ADAM_LR * (m_hat / (jnp.sqrt(v_hat) + ADAM_EPS) + ADAM_WD * wv), mn, vn


def _adamw(w, g, m, v, name, deps=(), emit_grad=False):
    r, c = w.shape
    tr = _pick(r, [t for t in ROW_TILES if t * c * 4 <= BLOCK_BYTES]) if r >= 8 else r
    n_out = 4 if emit_grad else 3

    def body(w_ref, g_ref, m_ref, v_ref, *rest):
        outs = rest[-n_out:]
        gv = g_ref[:, :c]
        if emit_grad:
            outs[0][...] = gv
        outs[-3][...], outs[-2][...], outs[-1][...] = _adamw_update(w_ref[...], gv, m_ref[...], v_ref[...])

    blk = pl.BlockSpec((tr, c), lambda i: (i, 0))
    g_blk = pl.BlockSpec((tr, g.shape[1]), lambda i: (i, 0))
    return pl.pallas_call(
        body, name=name, grid=(r // tr,), in_specs=[blk, g_blk, blk, blk] + [ANY] * len(deps), out_specs=[blk] * n_out,
        out_shape=[jax.ShapeDtypeStruct((r, c), F32)] * n_out, compiler_params=_params("parallel"),
    )(w, g, m, v, *deps)


def _adamw_in_three(w, g, m, v, name, deps=()):
    r, c = w.shape
    tr = _pick(r, [t for t in ROW_TILES if t * c * 4 <= BLOCK_BYTES])
    blk = pl.BlockSpec((tr, c), lambda i: (i, 0))
    g_blk = pl.BlockSpec((tr, g.shape[1]), lambda i: (i, 0))
    shape = jax.ShapeDtypeStruct((r, c), F32)

    def call(body, part, in_specs, n_out, args, after):
        return pl.pallas_call(
            body, name=name + part, grid=(r // tr,), in_specs=in_specs + [ANY] * len(after), out_specs=[blk] * n_out,
            out_shape=[shape] * n_out, compiler_params=_params("parallel"),
        )(*args, *after)

    def first(g_ref, m_ref, *rest):
        gv = g_ref[:, :c]
        rest[-2][...] = gv
        rest[-1][...] = ADAM_B1 * m_ref[...] + (1.0 - ADAM_B1) * gv

    def second(g_ref, v_ref, *rest):
        gv = g_ref[:, :c]
        rest[-1][...] = ADAM_B2 * v_ref[...] + (1.0 - ADAM_B2) * (gv * gv)

    def third(w_ref, g_ref, m_ref, v_ref, *rest):
        rest[-1][...] = _adamw_update(w_ref[...], g_ref[:, :c], m_ref[...], v_ref[...])[0]

    grad, new_m = call(first, "_m", [g_blk, blk], 2, (g, m), tuple(deps))
    (new_v,) = call(second, "_v", [g_blk, blk], 1, (g, v), (new_m,))
    (delta,) = call(third, "_delta", [blk, g_blk, blk, blk], 1, (w, g, m, v), (new_v,))
    return grad, delta, new_m, new_v


ANY = pl.BlockSpec(memory_space=pl.ANY)


def _place():
    x, y, c = lax.axis_index("x"), lax.axis_index("y"), lax.axis_index("c")
    chips = [(1 - x, y), (x, 1 - y), (1 - x, 1 - y)]
    return x, y, c, chips


def _remote(src, dst, send_sem, recv_sem, to):
    return pltpu.make_async_remote_copy(src_ref=src, dst_ref=dst, send_sem=send_sem, recv_sem=recv_sem,
                                        device_id=to, device_id_type=MESH)


HBM = pl.BlockSpec(memory_space=pltpu.HBM)
SEM = pl.BlockSpec(memory_space=pltpu.SEMAPHORE)
EFFECT = pltpu.SideEffectType.DATAFLOW_SIDE_EFFECTING


def _in_hbm(a):
    return pltpu.with_memory_space_constraint(a, pltpu.HBM)


def _half(ref_rows, who):
    return pl.ds(who * (ref_rows // 2), ref_rows // 2)


def _gather_start(groups, name):
    items = [it for g in groups for it in g]
    n = len(items)
    sizes = [len(g) for g in groups]

    def body(*refs):
        srcs, lands = refs[:n], refs[n:2 * n]
        sems = refs[2 * n:2 * n + 2 * len(groups)]
        token = refs[-1]
        x, y, c, chips = _place()
        j = 2 * x + y
        at = 0
        for gi, g in enumerate(groups):
            send, recv = sems[2 * gi], sems[2 * gi + 1]
            for i, (shard, split) in enumerate(g):
                src, land = srcs[at], lands[at]
                at += 1
                rows = _half(shard.shape[0], c) if split else slice(None)
                for k, chip in enumerate(chips):
                    _remote(src.at[rows], land.at[j, rows], send.at[4 * i + k], recv.at[4 * i + k], (*chip, c)).start()
                _remote(src, land.at[j], send.at[4 * i + 3], recv.at[4 * i + 3], (x, y, 1 - c)).start()
        token[...] = jnp.zeros_like(token)

    sem_shapes = []
    for sz in sizes:
        sem_shapes += [pltpu.SemaphoreType.DMA((4 * sz,)), pltpu.SemaphoreType.DMA((4 * sz,))]
    out_shape = (sem_shapes + [pltpu.HBM(sh.shape, sh.dtype) for sh, _ in items]
                 + [pltpu.HBM((N_CHIPS,) + sh.shape, sh.dtype) for sh, _ in items]
                 + [jax.ShapeDtypeStruct((8, LANES), F32)])
    ns = len(sem_shapes)
    outs = pl.pallas_call(
        body, name=name, in_specs=[HBM] * (2 * n),
        out_specs=[SEM] * ns + [HBM] * (2 * n) + [pl.BlockSpec(memory_space=pltpu.VMEM)],
        out_shape=out_shape, input_output_aliases={i: ns + i for i in range(2 * n)},
        compiler_params=pltpu.CompilerParams(has_side_effects=EFFECT),
    )(*[_in_hbm(sh) for sh, _ in items], *[_in_hbm(lax.empty((N_CHIPS,) + sh.shape, sh.dtype)) for sh, _ in items])
    sems, shards, lands, token = outs[:ns], outs[ns:ns + n], outs[ns + n:ns + 2 * n], outs[-1]
    res, at = [], 0
    for gi, sz in enumerate(sizes):
        res.append((shards[at:at + sz], lands[at:at + sz], sems[2 * gi], sems[2 * gi + 1]))
        at += sz
    return res, token


def _gather_pass(group, started, after, name):
    shards, lands, send, recv = started
    n = len(group)
    split_ix = [i for i, (_, split) in enumerate(group) if split]

    def body(*refs):
        lnds, send1, recv1 = refs[n:2 * n], refs[2 * n], refs[2 * n + 1]
        outs = refs[2 * n + 2 + len(after):]
        send2, recv2, token = outs[2 * n], outs[2 * n + 1], outs[2 * n + 2]
        x, y, c, chips = _place()
        sib = (x, y, 1 - c)
        for i, (shard, split) in enumerate(group):
            rows = _half(shard.shape[0], c) if split else slice(None)
            for k, (cx, cy) in enumerate(chips):
                landed = lnds[i].at[2 * cx + cy, rows]
                cp = _remote(landed, landed, send1.at[4 * i + k], recv1.at[4 * i + k], sib)
                cp.wait_send()
                cp.wait_recv()
            own = lnds[i].at[2 * x + y]
            cp = _remote(own, own, send1.at[4 * i + 3], recv1.at[4 * i + 3], sib)
            cp.wait_send()
            cp.wait_recv()
        for i2, i in enumerate(split_ix):
            rows = _half(group[i][0].shape[0], c)
            for k, (cx, cy) in enumerate(chips):
                landed = lnds[i].at[2 * cx + cy, rows]
                _remote(landed, landed, send2.at[3 * i2 + k], recv2.at[3 * i2 + k], sib).start()
        token[...] = jnp.zeros_like(token)

    n2 = len(split_ix)
    out_shape = ([pltpu.HBM(a.shape, a.dtype) for a in (*shards, *lands)]
                 + [pltpu.SemaphoreType.DMA((3 * n2,)), pltpu.SemaphoreType.DMA((3 * n2,)), jax.ShapeDtypeStruct((8, LANES), F32)])
    outs = pl.pallas_call(
        body, name=name, in_specs=[HBM] * (2 * n) + [SEM, SEM] + [ANY] * len(after),
        out_specs=[HBM] * (2 * n) + [SEM, SEM, pl.BlockSpec(memory_space=pltpu.VMEM)],
        out_shape=out_shape, input_output_aliases={i: i for i in range(2 * n)},
        compiler_params=pltpu.CompilerParams(has_side_effects=EFFECT),
    )(*shards, *lands, send, recv, *after)
    return outs[:n], (outs[n:2 * n], outs[2 * n], outs[2 * n + 1]), outs[2 * n + 2]


def _gather_wait(group, passed, after, name):
    lands, send2, recv2 = passed
    n = len(group)
    split_ix = [i for i, (_, split) in enumerate(group) if split]

    def body(*refs):
        lnds, s2, r2 = refs[:n], refs[n], refs[n + 1]
        x, y, c, chips = _place()
        sib = (x, y, 1 - c)
        for i2, i in enumerate(split_ix):
            rows = _half(group[i][0].shape[0], 1 - c)
            for k, (cx, cy) in enumerate(chips):
                landed = lnds[i].at[2 * cx + cy, rows]
                cp = _remote(landed, landed, s2.at[3 * i2 + k], r2.at[3 * i2 + k], sib)
                cp.wait_send()
                cp.wait_recv()

    return pl.pallas_call(
        body, name=name, in_specs=[HBM] * n + [SEM, SEM, ANY], out_specs=[HBM] * n,
        out_shape=[pltpu.HBM(a.shape, a.dtype) for a in lands], input_output_aliases={i: i for i in range(n)},
        compiler_params=pltpu.CompilerParams(has_side_effects=EFFECT),
    )(*lands, send2, recv2, after)


def _xfer_start(name, srcs, land_shapes, n_copies, copies, after):
    n, nl = len(srcs), len(land_shapes)

    def body(*refs):
        src_refs, land_refs = refs[:n], refs[n:n + nl]
        send, recv, token = refs[n + nl + 1], refs[n + nl + 2], refs[-1]
        for cp in copies(src_refs, land_refs, send, recv):
            cp.start()
        token[...] = jnp.zeros_like(token)

    lands = [_in_hbm(lax.empty(shape, dtype)) for shape, dtype in land_shapes]
    out_shape = ([pltpu.SemaphoreType.DMA((n_copies,)), pltpu.SemaphoreType.DMA((n_copies,))]
                 + [pltpu.HBM(a.shape, a.dtype) for a in (*srcs, *lands)] + [jax.ShapeDtypeStruct((8, LANES), F32)])
    outs = pl.pallas_call(
        body, name=name, in_specs=[HBM] * (n + nl) + [ANY],
        out_specs=[SEM, SEM] + [HBM] * (n + nl) + [pl.BlockSpec(memory_space=pltpu.VMEM)],
        out_shape=out_shape, input_output_aliases={i: 2 + i for i in range(n + nl)},
        compiler_params=pltpu.CompilerParams(has_side_effects=EFFECT),
    )(*[_in_hbm(a) for a in srcs], *lands, after)
    return (outs[2:2 + n], outs[2 + n:2 + n + nl], outs[0], outs[1]), outs[-1]


def _xfer_wait(name, started, copies, after):
    srcs, lands, send, recv = started
    n, nl = len(srcs), len(lands)

    def body(*refs):
        src_refs, land_refs, s_ref, r_ref = refs[:n], refs[n:n + nl], refs[n + nl], refs[n + nl + 1]
        for cp in copies(src_refs, land_refs, s_ref, r_ref):
            cp.wait_send()
            cp.wait_recv()

    outs = pl.pallas_call(
        body, name=name, in_specs=[HBM] * (n + nl) + [SEM, SEM, ANY], out_specs=[HBM] * (n + nl),
        out_shape=[pltpu.HBM(a.shape, a.dtype) for a in (*srcs, *lands)],
        input_output_aliases={i: i for i in range(n + nl)},
        compiler_params=pltpu.CompilerParams(has_side_effects=EFFECT),
    )(*srcs, *lands, send, recv, after)
    return outs[:n], outs[n:]


def _swap_copies(srcs, lands, send, recv):
    x, y, c, _ = _place()
    return [_remote(src.at[:, _half(src.shape[1], 1 - c)], land, send.at[i], recv.at[i], (x, y, 1 - c))
            for i, (src, land) in enumerate(zip(srcs, lands))]


def _scatter_copies(srcs, lands, send, recv):
    x, y, c, chips = _place()
    return [_remote(src.at[2 * cx + cy], land.at[k], send.at[3 * i + k], recv.at[3 * i + k], (cx, cy, c))
            for i, (src, land) in enumerate(zip(srcs, lands)) for k, (cx, cy) in enumerate(chips)]


def _join_copies(srcs, lands, send, recv):
    x, y, c, _ = _place()
    return [_remote(src.at[_half(src.shape[0], c)], src.at[_half(src.shape[0], c)], send.at[i], recv.at[i], (x, y, 1 - c))
            for i, src in enumerate(srcs)]


def _corner(a):
    return a[(slice(0, 1),) * a.ndim]


class _Reducer:
    def __init__(self, place):
        self.place = place
        self.state = {}

    def swap(self, key, grads, after):
        shapes = [((g.shape[0], g.shape[1] // 2, g.shape[2]), g.dtype) for g in grads]
        self.state[key], token = _xfer_start("swap_start_" + key, grads, shapes, len(grads), _swap_copies, _corner(after))
        return token

    def to_chips(self, key, after):
        grads, from_sibling = _xfer_wait("swap_wait_" + key, self.state[key], _swap_copies, after)
        sums = [_add_halves(g, r, self.place) for g, r in zip(grads, from_sibling)]
        shapes = [((3,) + s.shape[1:], s.dtype) for s in sums]
        started, token = _xfer_start("scatter_start_" + key, sums, shapes, 3 * len(sums), _scatter_copies, _corner(sums[-1]))
        self.state[key] = (grads, from_sibling, started)
        return token

    def to_core(self, key, after):
        grads, from_sibling, started = self.state[key]
        _, from_chips = _xfer_wait("scatter_wait_" + key, started, _scatter_copies, after)
        shards = [_sum_chips(g, r, rc, self.place) for g, r, rc in zip(grads, from_sibling, from_chips)]
        self.state[key], token = _xfer_start("join_start_" + key, shards, [], len(shards), _join_copies, _corner(shards[-1]))
        return token

    def finish(self, key, after):
        return _xfer_wait("join_wait_" + key, self.state.pop(key), _join_copies, after)[0]


def _gather_packs(pack, deps=()):
    def body(p_ref, *rest):
        o_ref, lsem, ssem, rsem = rest[-4:]
        x, y, c, _ = _place()
        me = 4 * x + 2 * y + c
        local = pltpu.make_async_copy(p_ref, o_ref.at[me], lsem)
        local.start()
        cps = []
        for k in range(1, N_DEV):
            fx, fy, fc = (k >> 2) & 1, (k >> 1) & 1, k & 1
            to = (x ^ fx, y ^ fy, c ^ fc)
            cps.append(_remote(p_ref, o_ref.at[me], ssem.at[k - 1], rsem.at[k - 1], to))
        for cp in cps:
            cp.start()
        for k in range(1, N_DEV):
            fx, fy, fc = (k >> 2) & 1, (k >> 1) & 1, k & 1
            src = o_ref.at[4 * (x ^ fx) + 2 * (y ^ fy) + (c ^ fc)]
            _remote(src, src, ssem.at[k - 1], rsem.at[k - 1], (x, y, c)).wait_recv()
        for cp in cps:
            cp.wait_send()
        local.wait()

    return pl.pallas_call(
        body, name="gather_packs", in_specs=[ANY] * (1 + len(deps)), out_specs=ANY,
        out_shape=jax.ShapeDtypeStruct((N_DEV,) + pack.shape, pack.dtype),
        scratch_shapes=[pltpu.SemaphoreType.DMA, pltpu.SemaphoreType.DMA((N_DEV - 1,)), pltpu.SemaphoreType.DMA((N_DEV - 1,))],
    )(pack, *deps)


LANE_TILES = (512, 896, 1408, 704, 384, 256, 128)


def _layer_grads(x, target, small, wg, rest_pass, rest_wait, red, filler):
    s, d = x.shape
    f = wg["conv"].shape[1] // 2
    w_att = N_HEADS * HEAD_DIM
    in_splits = (w_att, w_att, w_att, N_HEADS, w_att, w_att, w_att, d, d)
    in_cols = sum(in_splits)
    cs = in_cols // N_CHIPS
    cp = wg["in"].shape[2]
    tm = min(s, MM_TILE)
    tm_wide = min(s, MM_TILE // 2)
    t_in = cp
    t_up = 2 * f // N_CHIPS
    t_d = _pick(d, LANE_TILES)
    t_d2 = min(d, MM_TILE)
    t_dq = _pick(d // N_CHIPS, LANE_TILES)
    t_fq = _pick(f // N_CHIPS, LANE_TILES)

    h1 = _norm_fwd(x, small["g_attn"], group=d, name="rms1_fwd")
    proj_p = _mm(h1, wg["in"], mode="nn", b_kind="col", tm=tm_wide, tn=t_in, tk=d, name="mm_in")
    gains = {n: small[n].reshape(1, w_att) for n in ("g_q_fox", "g_k_fox", "g_q_dil", "g_k_dil")}
    qa, ka, va_b, fa, qb, kb, vb_b, ga, gb, qa_n, ka_n, qb_n, kb_n = _proj_split(
        proj_p, in_splits, cs, (F32, F32, BF16, F32, F32, F32, BF16, F32, F32),
        {0: gains["g_q_fox"], 1: gains["g_k_fox"], 4: gains["g_q_dil"], 5: gains["g_k_dil"]})
    fa_t = fa.T
    b_f = small["b_forget"].reshape(N_HEADS, 1)
    c_f = _forget_fwd(fa_t, b_f)
    slopes = jnp.asarray(2.0 ** (-8.0 * np.arange(1, N_HEADS + 1) / N_HEADS), dtype=F32)
    a_d = -(slopes[:, None] * jnp.arange(s, dtype=F32)[None, :])
    rows_f, cols_f = c_f[:, :, None], c_f[:, None, :]
    rows_d, cols_d = a_d[:, :, None], a_d[:, None, :]
    o_a, o_a32, lse_a = _attn_fwd(qa_n, ka_n, va_b, rows_f, cols_f, dilated=False, name="attn_fox_fwd")
    token = rest_pass("mid", o_a)
    rows_d = rows_d + token[0, 0]
    o_b, o_b32, lse_b = _attn_fwd(qb_n, kb_n, vb_b, rows_d, cols_d, dilated=True, name="attn_dil_fwd")
    wg = dict(wg, **rest_wait("mid", o_b))
    token = rest_pass("late", o_b)
    pa = _mm(o_a, wg["brf"], mode="nn", b_kind="col", tm=tm, tn=t_dq, tk=w_att, name="mm_brf", deps=(token,))
    pb = _mm(o_b, wg["brd"], mode="nn", b_kind="col", tm=tm, tn=t_dq, tk=w_att, name="mm_brd")
    merged = _gate_fwd(ga, gb, pa, pb)
    x1 = _mm(merged, wg["out"], mode="nn", b_kind="row", res=x, tm=tm, tn=t_d, tk=t_dq, name="mm_out")
    wg = dict(wg, **rest_wait("late", x1))
    h2 = _norm_fwd(x1, small["g_ffn"], group=d, name="rms2_fwd")
    u = _mm(h2, wg["up"], mode="nn", b_kind="col", tm=tm_wide, tn=t_up, tk=d, name="mm_up")
    act = _conv_glu_fwd(u, wg["conv"], wg["bconv"])
    dy_f, dy_b, loss_blk = _mm(act, wg["down"], mode="nn", b_kind="row", res=x1, loss_target=target,
                               tm=tm, tn=t_d2, tk=t_fq, name="mm_down")

    d_act = _mm(dy_b, wg["down"], mode="nt", b_kind="row", tm=tm, tn=t_fq, tk=d, name="mm_down_dx")
    g_down = _mm(act, dy_b, mode="tn", out_dtype=BF16, out_kind="row", tm=t_fq, tn=t_d2, tk=s, name="mm_down_dw")
    tok = red.swap("down", [g_down], g_down)
    du_g, du_v, st_g, st_v = _conv_glu_bwd(u, d_act, wg["conv"] + tok[0, 0], wg["bconv"])
    tok = red.to_chips("down", du_g)
    du = (du_g, du_v)
    g_up = _mm(h2, du, mode="tn", out_dtype=BF16, out_kind="col", tm=t_d2, tn=t_up // 2, tk=s, name="mm_up_dw", deps=(tok,))
    tok = red.to_core("down", g_up)
    tok2 = red.swap("up", [g_up], g_up)
    dh2 = _mm(du, wg["up"], mode="nt", b_kind="col", tm=tm, tn=t_d2, tk=t_up, name="mm_up_dx", deps=(tok, tok2))
    tok = red.to_chips("up", dh2)
    dx1_b, dx1_f, dg_ffn = _norm_bwd(dh2, x1, small["g_ffn"], group=d, res=dy_f, out_dtypes=(BF16, F32), name="rms2_bwd")
    d_merged = _mm(dx1_b, wg["out"], mode="nt", b_kind="row", tm=tm, tn=t_dq, tk=d, name="mm_out_dx", deps=(tok,))
    g_out = _mm(merged, dx1_b, mode="tn", out_dtype=BF16, out_kind="row", tm=t_dq, tn=t_d2, tk=s, name="mm_out_dw")
    dpa, dpb, dga, dgb = _gate_bwd(d_merged, ga, gb, pa, pb)
    do_a = _mm(dpa, wg["brf"], mode="nt", b_kind="col", out_dtype=BF16, tm=s, tn=w_att, tk=t_dq, name="mm_brf_dx")
    do_b = _mm(dpb, wg["brd"], mode="nt", b_kind="col", out_dtype=BF16, tm=s, tn=w_att, tk=t_dq, name="mm_brd_dx")
    g_brf = _mm(o_a, dpa, mode="tn", out_dtype=BF16, out_kind="col", tm=w_att, tn=t_dq, tk=s, name="mm_brf_dw")
    g_brd = _mm(o_b, dpb, mode="tn", out_dtype=BF16, out_kind="col", tm=w_att, tn=t_dq, tk=s, name="mm_brd_dw")
    tok = red.swap("mix", [g_out, g_brf, g_brd], g_brd)
    dqa_n, dka_n, dva, dac_a = _attn_bwd(qa_n, ka_n, va_b, o_a32, do_a, lse_a, rows_f + tok[0, 0], cols_f, dilated=False, name="attn_fox_bwd")
    tok = red.to_core("up", dqa_n)
    tok2 = red.to_chips("mix", dqa_n)
    dqb_n, dkb_n, dvb, _ = _attn_bwd(qb_n, kb_n, vb_b, o_b32, do_b, lse_b, rows_d + (tok[0, 0] + tok2[0, 0]), cols_d, dilated=True, name="attn_dil_bwd")
    tok = red.to_core("mix", dqb_n)
    dfa_t, db_f = _forget_bwd(dac_a[:, 0, :], fa_t, b_f)
    dproj_p, dgains = _dproj_merge(
        [dqa_n, dka_n, dva, dfa_t.T, dqb_n, dkb_n, dvb, dga, dgb], in_splits, cs, cp,
        {0: (qa, gains["g_q_fox"]), 1: (ka, gains["g_k_fox"]), 4: (qb, gains["g_q_dil"]), 5: (kb, gains["g_k_dil"])})
    dg_qf, dg_kf, dg_qd, dg_kd = dgains[0], dgains[1], dgains[4], dgains[5]
    g_in = _mm(h1, dproj_p, mode="tn", out_dtype=BF16, out_kind="col", tm=t_d2, tn=t_in, tk=s, name="mm_in_dw", deps=(tok,))
    tok = red.swap("in", [g_in], g_in)
    tok = red.to_chips("in", filler(tok))
    dh1 = _mm(dproj_p, wg["in"], mode="nt", b_kind="col", tm=tm, tn=t_d2, tk=t_in, name="mm_in_dx", deps=(tok,))
    grad_x, dg_attn = _norm_bwd(dh1, x, small["g_attn"], group=d, res=dx1_f, out_dtypes=(F32,), name="rms1_bwd")

    small_grads = {
        "g_attn": dg_attn, "b_forget": db_f.reshape(1, N_HEADS),
        "g_q_fox": dg_qf, "g_k_fox": dg_kf, "g_q_dil": dg_qd, "g_k_dil": dg_kd, "g_ffn": dg_ffn,
        "w_conv": jnp.concatenate([st_g[0:3], st_v[0:3]], axis=1),
        "b_conv": jnp.concatenate([st_g[3:4], st_v[3:4]], axis=1),
        "loss": loss_blk[0:1, 0:1],
    }
    return small_grads, grad_x


SMALL_ORDER = ("g_attn", "b_forget", "g_q_fox", "g_k_fox", "g_q_dil", "g_k_dil", "g_ffn", "w_conv", "b_conv", "loss")
WEIGHT_ORDER = ("g_attn", "w_in", "b_forget", "g_q_fox", "g_k_fox", "g_q_dil", "g_k_dil", "w_br_fox", "w_br_dil",
                "w_out", "g_ffn", "w_up", "w_conv", "b_conv", "w_down")
BIG = {"w_in": "in", "w_br_fox": "brf", "w_br_dil": "brd", "w_out": "out", "w_up": "up", "w_down": "down"}


def kernel(x, g_attn, w_in, b_forget, g_q_fox, g_k_fox, g_q_dil, g_k_dil, w_br_fox, w_br_dil, w_out, g_ffn, w_up, w_conv, b_conv, w_down, loss_target, m_g_attn, m_w_in, m_b_forget, m_g_q_fox, m_g_k_fox, m_g_q_dil, m_g_k_dil, m_w_br_fox, m_w_br_dil, m_w_out, m_g_ffn, m_w_up, m_w_conv, m_b_conv, m_w_down, v_g_attn, v_w_in, v_b_forget, v_g_q_fox, v_g_k_fox, v_g_q_dil, v_g_k_dil, v_w_br_fox, v_w_br_dil, v_w_out, v_g_ffn, v_w_up, v_w_conv, v_b_conv, v_w_down):
    w = dict(g_attn=g_attn, w_in=w_in, b_forget=b_forget, g_q_fox=g_q_fox, g_k_fox=g_k_fox, g_q_dil=g_q_dil,
             g_k_dil=g_k_dil, w_br_fox=w_br_fox, w_br_dil=w_br_dil, w_out=w_out, g_ffn=g_ffn, w_up=w_up,
             w_conv=w_conv, b_conv=b_conv, w_down=w_down)
    m = dict(g_attn=m_g_attn, w_in=m_w_in, b_forget=m_b_forget, g_q_fox=m_g_q_fox, g_k_fox=m_g_k_fox,
             g_q_dil=m_g_q_dil, g_k_dil=m_g_k_dil, w_br_fox=m_w_br_fox, w_br_dil=m_w_br_dil, w_out=m_w_out,
             g_ffn=m_g_ffn, w_up=m_w_up, w_conv=m_w_conv, b_conv=m_b_conv, w_down=m_w_down)
    v = dict(g_attn=v_g_attn, w_in=v_w_in, b_forget=v_b_forget, g_q_fox=v_g_q_fox, g_k_fox=v_g_k_fox,
             g_q_dil=v_g_q_dil, g_k_dil=v_g_k_dil, w_br_fox=v_w_br_fox, w_br_dil=v_w_br_dil, w_out=v_w_out,
             g_ffn=v_g_ffn, w_up=v_w_up, w_conv=v_w_conv, b_conv=v_b_conv, w_down=v_w_down)
    xi, yi, ci = lax.axis_index("x"), lax.axis_index("y"), lax.axis_index("c")
    chip = (2 * xi + yi).astype(jnp.int32)

    cs = w_in.shape[2]
    cp = _round_up(cs, LANES)
    conv_pad = jnp.pad(w_conv[0], ((0, 8 - w_conv.shape[1]), (0, 0)))
    first = [(jnp.pad(w_in[0].astype(BF16), ((0, 0), (0, cp - cs))), True), (conv_pad, False)]
    (started_first,), token = _gather_start([first], "gather_start_in")
    one = 1.0 + token[0, 0]
    shards = {n: (a[0] * one).astype(BF16) for n, a in
              (("brf", w_br_fox), ("brd", w_br_dil), ("out", w_out), ("up", w_up), ("down", w_down))}
    later = {"mid": ("brf", "brd", "out"), "late": ("up", "down")}
    groups = {key: [(shards[n], True) for n in members] for key, members in later.items()}
    started_later, token = _gather_start(list(groups.values()), "gather_start_rest")
    started = dict(zip(later, started_later))
    token, w["w_in"], m["w_in"], v["w_in"] = lax.optimization_barrier((token, w["w_in"], m["w_in"], v["w_in"]))
    w2, m2, v2 = ({n: a[n].reshape(a[n].shape[-2], a[n].shape[-1]) for n in BIG} for a in (w, m, v))
    early = (token, w2["w_in"], m2["w_in"], v2["w_in"])
    own_first, passed_first, token = _gather_pass(first, started_first, early, "gather_pass_in")
    land_in, land_conv = _gather_wait(first, passed_first, token, "gather_wait_in")
    wg = {"in": land_in, "bconv": b_conv,
          "conv": jnp.transpose(land_conv[:, :w_conv.shape[1], :], (1, 0, 2)).reshape(w_conv.shape[1], -1)}
    small = {n: w[n] for n in ("g_attn", "b_forget", "g_q_fox", "g_k_fox", "g_q_dil", "g_k_dil", "g_ffn")}
    small = {n: (a[0] if a.ndim == 3 else a) for n, a in small.items()}
    in_flight = {}

    def rest_pass(key, after):
        own, passed, tok = _gather_pass(groups[key], started[key], (after,), "gather_pass_" + key)
        in_flight[key] = (own, passed)
        return tok

    def rest_wait(key, after):
        own, passed = in_flight.pop(key)
        lands = _gather_wait(groups[key], passed, after, "gather_wait_" + key)
        return dict(zip(later[key], lands))

    reducer = _Reducer(jnp.stack([chip, ci.astype(jnp.int32)]))
    g_out, d_out, m_out, v_out = {}, {}, {}, {}
    reduced = {}

    def first_element(arrays):
        return jnp.stack([a[(0,) * a.ndim] for a in arrays])

    def update_big(n, deps):
        if n == "w_in":
            g2, dl, mn, vn = _adamw_in_three(w2[n], reduced[BIG[n]], m2[n], v2[n], name="adamw_" + n, deps=deps)
        else:
            g2, dl, mn, vn = _adamw(w2[n], reduced[BIG[n]], m2[n], v2[n], name="adamw_" + n, deps=deps, emit_grad=True)
        g_out[n], d_out[n], m_out[n], v_out[n] = (a.reshape(w[n].shape) for a in (g2, dl, mn, vn))

    def update_down(tok):
        (reduced["down"],) = reducer.finish("down", tok)
        update_big("w_down", (tok,))
        return v_out["w_down"]

    small_grads, grad_x = _layer_grads(x[0], loss_target[0], small, wg, rest_pass, rest_wait, reducer, update_down)

    for key, members in (("up", ("up",)), ("mix", ("out", "brf", "brd"))):
        reduced.update(zip(members, reducer.finish(key, grad_x)))
    others = ("w_up", "w_out", "w_br_fox", "w_br_dil")
    for n in others:
        update_big(n, (grad_x,))

    flat = jnp.concatenate([small_grads[n].reshape(-1) for n in SMALL_ORDER])
    rows = _round_up(flat.shape[0], 8 * LANES) // LANES
    pack = jnp.pad(flat, (0, rows * LANES - flat.shape[0])).reshape(rows, LANES)
    packs = _gather_packs(pack, deps=(first_element([v_out[n] for n in others]),))
    total = _sum_devices(packs).reshape(-1)
    red, at = {}, 0
    for n in SMALL_ORDER:
        size = small_grads[n].size
        red[n] = total[at:at + size].reshape(small_grads[n].shape)
        at += size
    loss = red["loss"].reshape(())
    c2 = w_conv.shape[2]
    red["w_conv"] = lax.dynamic_slice_in_dim(red["w_conv"], chip * c2, c2, axis=1)

    smalls = [n for n in WEIGHT_ORDER if n not in BIG]
    for n in smalls:
        shape = w[n].shape
        r2 = (shape[-2], shape[-1]) if n not in ("g_attn", "b_forget", "g_ffn", "b_conv") else (1, shape[-1])
        g2 = red[n].reshape(r2)
        dl, mn, vn = _adamw(w[n].reshape(r2), g2, m[n].reshape(r2), v[n].reshape(r2), name="adamw_" + n)
        g_out[n], d_out[n], m_out[n], v_out[n] = (a.reshape(shape) for a in (g2, dl, mn, vn))
    tok = reducer.to_core("in", first_element([v_out[n] for n in smalls]))
    (reduced["in"],) = reducer.finish("in", tok)
    update_big("w_in", (tok,))

    return (loss, grad_x[None], *[g_out[n] for n in WEIGHT_ORDER], *[d_out[n] for n in WEIGHT_ORDER],
            *[m_out[n] for n in WEIGHT_ORDER], *[v_out[n] for n in WEIGHT_ORDER])
```

```python
import math

import jax
import jax.numpy as jnp
import numpy as np
from jax import lax
from jax.experimental import pallas as pl
from jax.experimental.pallas import tpu as pltpu

F32 = jnp.float32
BF16 = jnp.bfloat16
HEAD_DIM = 128
N_HEADS = 8
EPS = 1e-6
NEG = -1e30
LOG2E = math.log2(math.e)
N_CHIPS = 4
N_DEV = 8
LANES = 128
VMEM_LIMIT_BYTES = 56 * 1024 * 1024
DIL_PATTERNS = ((128, 1), (512, 4), (2048, 16))
ATTN_TILE = 512
MM_TILE = 1024
ADAM_LR, ADAM_B1, ADAM_B2, ADAM_EPS, ADAM_WD, ADAM_STEP = 0.001, 0.9, 0.999, 1e-08, 0.01, 10
MESH = pl.DeviceIdType.MESH


def _params(*sem):
    return pltpu.CompilerParams(dimension_semantics=sem, vmem_limit_bytes=VMEM_LIMIT_BYTES)


def _round_up(n, m):
    return -(-n // m) * m


def _pick(dim, prefs):
    for p in prefs:
        if dim % p == 0:
            return p
    raise ValueError(f"no tile for {dim} in {prefs}")


def _logical_shape(arr, kind):
    if kind is None:
        return arr.shape
    s, r, c = arr.shape
    return (r, s * c) if kind == "col" else (s * r, c)


def _spec(shape, kind, br, bc, fi, fj):
    if kind is None:
        return pl.BlockSpec((br, bc), lambda *g: (fi(*g), fj(*g)))
    _, r, c = shape
    if kind == "col":
        nb = c // bc
        assert nb * bc == c, (shape, bc)
        return pl.BlockSpec((None, br, bc), lambda *g: (fj(*g) // nb, fi(*g), fj(*g) % nb))
    nb = r // br
    assert nb * br == r, (shape, br)
    return pl.BlockSpec((None, br, bc), lambda *g: (fi(*g) // nb, fi(*g) % nb, fj(*g)))


def _mm(a, b, *, mode, tm, tn, tk, name, a_kind=None, b_kind=None, out_kind=None,
        out_dtype=F32, res=None, deps=(), loss_target=None):
    pair_a, pair_b = isinstance(a, tuple), isinstance(b, tuple)
    if pair_a or pair_b:
        return _mm_pair(a, b, mode=mode, tm=tm, tn=tn, tk=tk, name=name, b_kind=b_kind, out_kind=out_kind,
                        out_dtype=out_dtype, deps=deps)
    la, lb = _logical_shape(a, a_kind), _logical_shape(b, b_kind)
    if mode == "nn":
        (m, k), (k2, n) = la, lb
    elif mode == "nt":
        (m, k), (n, k2) = la, lb
    else:
        (k, m), (k2, n) = la, lb
    assert k == k2, (name, la, lb)
    assert m % tm == 0 and n % tn == 0 and k % tk == 0, (name, m, n, k, tm, tn, tk)
    nk = k // tk
    im = lambda i, j, l: i
    jn = lambda i, j, l: j
    lk = lambda i, j, l: l
    if mode == "tn":
        a_spec = _spec(a.shape, a_kind, tk, tm, lk, im)
        dims = (((0,), (0,)), ((), ()))
    else:
        a_spec = _spec(a.shape, a_kind, tm, tk, im, lk)
        dims = (((1,), (1,)), ((), ())) if mode == "nt" else (((1,), (0,)), ((), ()))
    if mode == "nt":
        b_spec = _spec(b.shape, b_kind, tn, tk, jn, lk)
    else:
        b_spec = _spec(b.shape, b_kind, tk, tn, lk, jn)
    if out_kind is None:
        oshape = (m, n)
    elif out_kind == "col":
        oshape = (N_CHIPS, m, n // N_CHIPS)
    else:
        oshape = (N_CHIPS, m // N_CHIPS, n)
    o_spec = _spec(oshape, out_kind, tm, tn, im, jn)
    tile = pl.BlockSpec((tm, tn), lambda i, j, l: (i, j))
    in_specs = [a_spec, b_spec]
    args = [a, b]
    for extra in (res, loss_target):
        if extra is not None:
            in_specs.append(tile)
            args.append(extra)
    in_specs += [pl.BlockSpec(memory_space=pl.ANY)] * len(deps)
    args += list(deps)
    if loss_target is None:
        out_specs, out_shape = [o_spec], [jax.ShapeDtypeStruct(oshape, out_dtype)]
    else:
        assert out_kind is None and res is not None
        out_specs = [tile, tile, pl.BlockSpec((8, LANES), lambda i, j, l: (0, 0))]
        out_shape = [jax.ShapeDtypeStruct(oshape, F32), jax.ShapeDtypeStruct(oshape, BF16),
                     jax.ShapeDtypeStruct((8, LANES), F32)]
    n_in, n_out = len(args), len(out_specs)

    def finish(out, refs, first):
        res_ref = refs[2] if res is not None else None
        outs = refs[n_in:n_in + n_out]
        if res_ref is not None:
            out = out + res_ref[...]
        if loss_target is None:
            outs[0][...] = out.astype(outs[0].dtype)
            return

        @pl.when(first)
        def _():
            outs[2][...] = jnp.zeros_like(outs[2])

        err = out - refs[3][...]
        dy = err * (1.0 / n)
        outs[0][...] = dy
        outs[1][...] = dy.astype(BF16)
        outs[2][...] += 0.5 * jnp.sum(jnp.sum(err * err, axis=-1, keepdims=True) * (1.0 / n), axis=0, keepdims=True)

    def first_tile():
        return (pl.program_id(0) == 0) & (pl.program_id(1) == 0)

    def body_whole_k(*refs):
        finish(lax.dot_general(refs[0][...], refs[1][...], dims, preferred_element_type=F32), refs, first_tile())

    def body(*refs):
        acc_ref = refs[-1]
        step = pl.program_id(2)
        first = first_tile()

        @pl.when(step == 0)
        def _():
            acc_ref[...] = jnp.zeros_like(acc_ref)

        acc_ref[...] += lax.dot_general(refs[0][...], refs[1][...], dims, preferred_element_type=F32)

        @pl.when(step == nk - 1)
        def _():
            finish(acc_ref[...], refs, first)

    outs = pl.pallas_call(
        body_whole_k if nk == 1 else body, name=name, grid=(m // tm, n // tn, nk),
        in_specs=in_specs, out_specs=out_specs, out_shape=out_shape,
        scratch_shapes=[] if nk == 1 else [pltpu.VMEM((tm, tn), F32)],
        compiler_params=_params(*(["arbitrary"] * 3 if loss_target is not None else ["parallel", "parallel", "arbitrary"])),
    )(*args)
    return outs[0] if loss_target is None else outs


def _mm_pair(a, b, *, mode, tm, tn, tk, name, b_kind, out_kind, out_dtype, deps):
    anyspec = [pl.BlockSpec(memory_space=pl.ANY)] * len(deps)
    if mode == "tn":
        assert isinstance(b, tuple) and out_kind == "col" and a.shape[0] == tk
        k, m = a.shape
        n0 = b[0].shape[1]
        n, nb0 = 2 * n0, n0 // tn
        oshape = (N_CHIPS, m, n // N_CHIPS)

        def body(a_ref, b0_ref, b1_ref, *rest):
            o_ref = rest[-1]
            for first, b_ref in ((True, b0_ref), (False, b1_ref)):
                @pl.when((pl.program_id(1) < nb0) == first)
                def _():
                    o_ref[...] = lax.dot_general(a_ref[...], b_ref[...], (((0,), (0,)), ((), ())),
                                                 preferred_element_type=F32).astype(o_ref.dtype)

        return pl.pallas_call(
            body, name=name, grid=(m // tm, n // tn),
            in_specs=[pl.BlockSpec((tk, tm), lambda i, j: (0, i)),
                      pl.BlockSpec((tk, tn), lambda i, j: (0, jnp.minimum(j, nb0 - 1))),
                      pl.BlockSpec((tk, tn), lambda i, j: (0, jnp.maximum(j - nb0, 0)))] + anyspec,
            out_specs=_spec(oshape, "col", tm, tn, lambda i, j: i, lambda i, j: j),
            out_shape=jax.ShapeDtypeStruct(oshape, out_dtype), compiler_params=_params("parallel", "arbitrary"),
        )(a, *b, *deps)
    assert mode == "nt" and isinstance(a, tuple) and out_kind is None
    m, k0 = a[0].shape
    n = _logical_shape(b, b_kind)[0]
    nk0 = k0 // tk
    nk = 2 * nk0

    def body(a0_ref, a1_ref, b_ref, *rest):
        o_ref, acc_ref = rest[-2], rest[-1]
        step = pl.program_id(2)

        @pl.when(step == 0)
        def _():
            acc_ref[...] = jnp.zeros_like(acc_ref)

        for first, a_ref in ((True, a0_ref), (False, a1_ref)):
            @pl.when((step < nk0) == first)
            def _():
                acc_ref[...] += lax.dot_general(a_ref[...], b_ref[...], (((1,), (1,)), ((), ())), preferred_element_type=F32)

        @pl.when(step == nk - 1)
        def _():
            o_ref[...] = acc_ref[...].astype(o_ref.dtype)

    return pl.pallas_call(
        body, name=name, grid=(m // tm, n // tn, nk),
        in_specs=[pl.BlockSpec((tm, tk), lambda i, j, l: (i, jnp.minimum(l, nk0 - 1))),
                  pl.BlockSpec((tm, tk), lambda i, j, l: (i, jnp.maximum(l - nk0, 0))),
                  _spec(b.shape, b_kind, tn, tk, lambda i, j, l: j, lambda i, j, l: l)] + anyspec,
        out_specs=pl.BlockSpec((tm, tn), lambda i, j, l: (i, j)),
        out_shape=jax.ShapeDtypeStruct((m, n), out_dtype), scratch_shapes=[pltpu.VMEM((tm, tn), F32)],
        compiler_params=_params("parallel", "parallel", "arbitrary"),
    )(*a, b, *deps)


def _pieces(splits, cs, cp):
    out, g0 = [], 0
    for width in splits:
        g1, runs = g0 + width, []
        for j in range(N_CHIPS):
            a, b = max(g0, cs * j), min(g1, cs * (j + 1))
            if a < b:
                runs.append((j * cp + a - cs * j, a - g0, b - a))
        out.append(runs)
        g0 = g1
    return out


def _head_norm(xv, gv):
    r = lax.rsqrt(jnp.mean(xv * xv, axis=-1, keepdims=True) + EPS)
    return (xv * r) * gv


def _head_norm_bwd(dyv, xv, gv):
    r = lax.rsqrt(jnp.mean(xv * xv, axis=-1, keepdims=True) + EPS)
    xr = xv * r
    gdy = dyv * gv
    return r * (gdy - xr * jnp.mean(gdy * xr, axis=-1, keepdims=True)), jnp.sum(dyv * xr, axis=0, keepdims=True)


def _proj_split(proj_p, splits, cs, dtypes, gains, tm=128):
    s, wp = proj_p.shape
    pieces = _pieces(splits, cs, wp // N_CHIPS)
    normed = sorted(gains)
    nseg = len(splits)

    def body(p_ref, *refs):
        g_refs, o_refs, n_refs = refs[:len(normed)], refs[len(normed):len(normed) + nseg], refs[len(normed) + nseg:]
        for o_ref, runs in zip(o_refs, pieces):
            for src, dst, n in runs:
                o_ref[:, dst:dst + n] = p_ref[:, src:src + n].astype(o_ref.dtype)
        for g_ref, n_ref, i in zip(g_refs, n_refs, normed):
            for c0 in range(0, splits[i], HEAD_DIM):
                cols = slice(c0, c0 + HEAD_DIM)
                n_ref[:, cols] = _head_norm(o_refs[i][:, cols], g_ref[:, cols]).astype(n_ref.dtype)

    return pl.pallas_call(
        body, name="proj_split", grid=(s // tm,),
        in_specs=[pl.BlockSpec((tm, wp), lambda i: (i, 0))] + [pl.BlockSpec((1, splits[i]), lambda i: (0, 0)) for i in normed],
        out_specs=[pl.BlockSpec((tm, w), lambda i: (i, 0)) for w in splits]
        + [pl.BlockSpec((tm, splits[i]), lambda i: (i, 0)) for i in normed],
        out_shape=[jax.ShapeDtypeStruct((s, w), dt) for w, dt in zip(splits, dtypes)]
        + [jax.ShapeDtypeStruct((s, splits[i]), BF16) for i in normed],
        compiler_params=_params("parallel"),
    )(proj_p, *[gains[i] for i in normed])


def _dproj_merge(parts, splits, cs, cp, norms, tm=128):
    s = parts[0].shape[0]
    wp = N_CHIPS * cp
    pieces = _pieces(splits, cs, cp)
    normed = sorted(norms)
    nseg, nn = len(splits), len(normed)

    def body(*refs):
        p_refs, x_refs, g_refs = refs[:nseg], refs[nseg:nseg + nn], refs[nseg + nn:nseg + 2 * nn]
        o_ref, dg_refs = refs[nseg + 2 * nn], refs[nseg + 2 * nn + 1:nseg + 3 * nn + 1]
        stage, tmp = refs[-2], refs[-1]

        @pl.when(pl.program_id(0) == 0)
        def _():
            for dg_ref in dg_refs:
                dg_ref[...] = jnp.zeros_like(dg_ref)

        for j in range(N_CHIPS):
            stage[:, j * cp + cs:(j + 1) * cp] = jnp.zeros((tm, cp - cs), F32)
        for i, (p_ref, runs) in enumerate(zip(p_refs, pieces)):
            src_ref = p_ref
            if i in norms:
                k = normed.index(i)
                for c0 in range(0, splits[i], HEAD_DIM):
                    cols = slice(c0, c0 + HEAD_DIM)
                    dx, dg = _head_norm_bwd(p_ref[:, cols].astype(F32), x_refs[k][:, cols], g_refs[k][:, cols])
                    tmp[:, cols] = dx
                    dg_refs[k][:, cols] += dg
                src_ref = tmp
            for dst, src, n in runs:
                stage[:, dst:dst + n] = src_ref[:, src:src + n].astype(F32)
        o_ref[...] = stage[...].astype(o_ref.dtype)

    wmax = max(splits[i] for i in normed)
    row = lambda w: pl.BlockSpec((tm, w), lambda i: (i, 0))
    vec = lambda w: pl.BlockSpec((1, w), lambda i: (0, 0))
    outs = pl.pallas_call(
        body, name="dproj_merge", grid=(s // tm,),
        in_specs=[row(w) for w in splits] + [row(splits[i]) for i in normed] + [vec(splits[i]) for i in normed],
        out_specs=[row(wp)] + [vec(splits[i]) for i in normed],
        out_shape=[jax.ShapeDtypeStruct((s, wp), BF16)] + [jax.ShapeDtypeStruct((1, splits[i]), F32) for i in normed],
        scratch_shapes=[pltpu.VMEM((tm, wp), F32), pltpu.VMEM((tm, wmax), F32)],
        compiler_params=_params("arbitrary"),
    )(*parts, *[norms[i][0] for i in normed], *[norms[i][1] for i in normed])
    return outs[0], dict(zip(normed, outs[1:]))


def _norm_fwd(x, g, *, group, name, tm=256):
    s, w = x.shape
    ng = w // group

    def body(x_ref, g_ref, o_ref):
        for i in range(ng):
            cols = slice(i * group, (i + 1) * group)
            xv = x_ref[:, cols]
            r = lax.rsqrt(jnp.mean(xv * xv, axis=-1, keepdims=True) + EPS)
            o_ref[:, cols] = ((xv * r) * g_ref[:, cols]).astype(o_ref.dtype)

    return pl.pallas_call(
        body, name=name, grid=(s // tm,),
        in_specs=[pl.BlockSpec((tm, w), lambda i: (i, 0)), pl.BlockSpec((1, w), lambda i: (0, 0))],
        out_specs=pl.BlockSpec((tm, w), lambda i: (i, 0)),
        out_shape=jax.ShapeDtypeStruct((s, w), BF16),
        compiler_params=_params("parallel"),
    )(x, g)


def _norm_bwd(dy, x, g, *, group, name, res=None, out_dtypes=(BF16,), tm=256, deps=()):
    s, w = x.shape
    ng = w // group
    n_in = 4 if res is not None else 3

    def body(*refs):
        dy_ref, x_ref, g_ref = refs[:3]
        res_ref = refs[3] if res is not None else None
        outs = refs[n_in + len(deps):]
        dx_refs, dg_ref = outs[:-1], outs[-1]

        @pl.when(pl.program_id(0) == 0)
        def _():
            dg_ref[...] = jnp.zeros_like(dg_ref)

        for i in range(ng):
            cols = slice(i * group, (i + 1) * group)
            xv = x_ref[:, cols]
            dyv = dy_ref[:, cols].astype(F32)
            r = lax.rsqrt(jnp.mean(xv * xv, axis=-1, keepdims=True) + EPS)
            xr = xv * r
            dg_ref[:, cols] += jnp.sum(dyv * xr, axis=0, keepdims=True)
            gdy = dyv * g_ref[:, cols]
            dx = r * (gdy - xr * jnp.mean(gdy * xr, axis=-1, keepdims=True))
            if res_ref is not None:
                dx = dx + res_ref[:, cols]
            for dx_ref in dx_refs:
                dx_ref[:, cols] = dx.astype(dx_ref.dtype)

    row = pl.BlockSpec((tm, w), lambda i: (i, 0))
    vec = pl.BlockSpec((1, w), lambda i: (0, 0))
    in_specs = [row, row, vec] + ([row] if res is not None else []) + [pl.BlockSpec(memory_space=pl.ANY)] * len(deps)
    args = [dy, x, g] + ([res] if res is not None else []) + list(deps)
    out_specs = [row] * len(out_dtypes) + [vec]
    out_shape = [jax.ShapeDtypeStruct((s, w), dt) for dt in out_dtypes] + [jax.ShapeDtypeStruct((1, w), F32)]
    return pl.pallas_call(
        body, name=name, grid=(s // tm,), in_specs=in_specs, out_specs=out_specs,
        out_shape=out_shape, compiler_params=_params("arbitrary"),
    )(*args)


def _split3(v):
    p1 = v.astype(BF16)
    r1 = v - p1.astype(F32)
    p2 = r1.astype(BF16)
    p3 = (r1 - p2.astype(F32)).astype(BF16)
    return p1, p2, p3


def _tri_sum(v, reverse, tcol=512):
    h, s = v.shape
    tcol = min(tcol, s)
    parts = _split3(v)
    outs = []
    for j in range(s // tcol):
        src = lax.broadcasted_iota(jnp.int32, (s, tcol), 0)
        dst = lax.broadcasted_iota(jnp.int32, (s, tcol), 1) + j * tcol
        keep = (src >= dst) if reverse else (src <= dst)
        tri = jnp.where(keep, 1.0, 0.0).astype(BF16)
        acc = jnp.zeros((h, tcol), F32)
        for p in parts:
            acc = acc + jnp.dot(p, tri, preferred_element_type=F32)
        outs.append(acc)
    return outs


def _forget_fwd(fa_t, b):
    h, s = fa_t.shape
    tcol = min(512, s)

    def body(f_ref, b_ref, c_ref):
        z = f_ref[...] + b_ref[...]
        logf = jnp.minimum(z, 0.0) - jnp.log(1.0 + jnp.exp(-jnp.abs(z)))
        for j, blk in enumerate(_tri_sum(logf, reverse=False, tcol=tcol)):
            c_ref[:, j * tcol:(j + 1) * tcol] = blk

    return pl.pallas_call(
        body, name="forget_fwd", out_shape=jax.ShapeDtypeStruct((h, s), F32),
        compiler_params=_params(),
    )(fa_t, b)


def _forget_bwd(dacol, fa_t, b):
    h, s = fa_t.shape
    tcol = min(512, s)

    def body(d_ref, f_ref, b_ref, dfa_ref, db_ref):
        z = f_ref[...] + b_ref[...]
        dc = -d_ref[...]
        total = jnp.zeros((h, 1), F32)
        for j, blk in enumerate(_tri_sum(dc, reverse=True, tcol=tcol)):
            cols = slice(j * tcol, (j + 1) * tcol)
            dfa = blk * (1.0 - jax.nn.sigmoid(z[:, cols]))
            dfa_ref[:, cols] = dfa
            total = total + jnp.sum(dfa, axis=-1, keepdims=True)
        db_ref[...] = total

    return pl.pallas_call(
        body, name="forget_bwd",
        out_shape=[jax.ShapeDtypeStruct((h, s), F32), jax.ShapeDtypeStruct((h, 1), F32)],
        compiler_params=_params(),
    )(dacol, fa_t, b)


def _distance_bias(s, tile, dilated):
    nb = s // tile
    b = lax.broadcasted_iota(jnp.int32, (nb, tile, tile), 0)
    dist = b * tile + lax.broadcasted_iota(jnp.int32, (nb, tile, tile), 1) - lax.broadcasted_iota(jnp.int32, (nb, tile, tile), 2)
    if not dilated:
        return jnp.where(dist >= 0, 0.0, NEG).astype(F32)
    mult = jnp.zeros(dist.shape, jnp.int32)
    for window, dil in DIL_PATTERNS:
        mult = mult + ((dist >= 0) & (dist <= window) & ((dist & (dil - 1)) == 0)).astype(jnp.int32)
    logm = jnp.where(mult == 3, math.log2(3.0), jnp.where(mult == 2, 1.0, 0.0))
    return jnp.where(mult > 0, logm, NEG).astype(F32)


def _logits(q, k, arow, acol, bias):
    s = lax.dot_general(q, k, (((1,), (1,)), ((), ())), preferred_element_type=F32)
    return s * (LOG2E / math.sqrt(HEAD_DIM)) + arow - acol + bias


def _attn_fwd(q, k, v, arow, acol, *, dilated, name, tq=ATTN_TILE, tk=ATTN_TILE):
    two_term = not dilated
    s, w = q.shape
    nh = w // HEAD_DIM
    assert tq == tk
    tq = tk = min(tq, s)
    nq, nk = s // tq, s // tk

    pairs = [(i, j) for i in range(nq) for j in range(i + 1)]
    q_of, k_of = (jnp.asarray(t, jnp.int32) for t in zip(*pairs))

    def body(qo_ref, ko_ref, q_ref, k_ref, v_ref, ar_ref, ac_ref, b_ref, o_ref, of_ref, lse_ref, m_ref, l_ref, acc_ref):
        t = pl.program_id(1)
        qi, ki = qo_ref[t], ko_ref[t]

        @pl.when(ki == 0)
        def _():
            m_ref[...] = jnp.full_like(m_ref, NEG)
            l_ref[...] = jnp.zeros_like(l_ref)
            acc_ref[...] = jnp.zeros_like(acc_ref)

        sc = _logits(q_ref[...], k_ref[...], ar_ref[...], ac_ref[...], b_ref[...])
        m_new = jnp.maximum(m_ref[...], jnp.max(sc, axis=-1, keepdims=True))
        alpha = jnp.exp2(m_ref[...] - m_new)
        p = jnp.exp2(sc - m_new)
        l_ref[...] = alpha * l_ref[...] + jnp.sum(p, axis=-1, keepdims=True)
        p_hi = p.astype(BF16)
        vv = v_ref[...]
        pv = jnp.dot(p_hi, vv, preferred_element_type=F32)
        if two_term:
            pv = pv + jnp.dot((p - p_hi.astype(F32)).astype(BF16), vv, preferred_element_type=F32)
        acc_ref[...] = alpha * acc_ref[...] + pv
        m_ref[...] = m_new

        @pl.when(ki == qi)
        def _():
            out = acc_ref[...] / l_ref[...]
            o_ref[...] = out.astype(o_ref.dtype)
            of_ref[...] = out
            lse_ref[...] = m_ref[...] + jnp.log2(l_ref[...])

    qs = pl.BlockSpec((tq, HEAD_DIM), lambda h, t, qo, ko: (qo[t], h))
    kv = pl.BlockSpec((tk, HEAD_DIM), lambda h, t, qo, ko: (ko[t], h))
    rowv = pl.BlockSpec((None, tq, 1), lambda h, t, qo, ko: (h, qo[t], 0))
    return pl.pallas_call(
        body, name=name,
        grid_spec=pltpu.PrefetchScalarGridSpec(
            num_scalar_prefetch=2, grid=(nh, len(pairs)),
            in_specs=[qs, kv, kv, rowv,
                      pl.BlockSpec((None, 1, tk), lambda h, t, qo, ko: (h, 0, ko[t])),
                      pl.BlockSpec((None, tq, tk), lambda h, t, qo, ko: (qo[t] - ko[t], 0, 0))],
            out_specs=[qs, qs, rowv],
            scratch_shapes=[pltpu.VMEM((tq, 1), F32), pltpu.VMEM((tq, 1), F32), pltpu.VMEM((tq, HEAD_DIM), F32)]),
        out_shape=[jax.ShapeDtypeStruct((s, w), BF16), jax.ShapeDtypeStruct((s, w), F32),
                   jax.ShapeDtypeStruct((nh, s, 1), F32)],
        compiler_params=_params("parallel", "arbitrary"),
    )(q_of, k_of, q, k, v, arow * LOG2E, acol * LOG2E, _distance_bias(s, tq, dilated))


def _attn_bwd(q, k, v, o, do, lse, arow, acol, *, dilated, name, tq=ATTN_TILE, tk=ATTN_TILE):
    s, w = q.shape
    nh = w // HEAD_DIM
    assert tq == tk
    tq = tk = min(tq, s)
    nq, nk = s // tq, s // tk
    scale = 1.0 / math.sqrt(HEAD_DIM)

    pairs = [(i, j) for j in range(nk) for i in range(j, nq)]
    q_of, k_of = (jnp.asarray(t, jnp.int32) for t in zip(*pairs))

    def body(qo_ref, ko_ref, q_ref, k_ref, v_ref, o_ref, do_ref, lse_ref, ar_ref, ac_ref, b_ref,
             dq_ref, dk_ref, dv_ref, dac_ref, dk_acc, dv_acc, dac_acc):
        t = pl.program_id(1)
        qi, ki = qo_ref[t], ko_ref[t]

        @pl.when(t == 0)
        def _():
            dq_ref[...] = jnp.zeros_like(dq_ref)

        @pl.when(qi == ki)
        def _():
            dk_acc[...] = jnp.zeros_like(dk_acc)
            dv_acc[...] = jnp.zeros_like(dv_acc)
            dac_acc[...] = jnp.zeros_like(dac_acc)

        qv, kvv, dov = q_ref[...], k_ref[...], do_ref[...]
        sc = _logits(qv, kvv, ar_ref[...], ac_ref[...], b_ref[...])
        p = jnp.exp2(sc - lse_ref[...])
        dp = lax.dot_general(dov, v_ref[...], (((1,), (1,)), ((), ())), preferred_element_type=F32)
        delta = jnp.sum(dov.astype(F32) * o_ref[...].astype(F32), axis=-1, keepdims=True)
        ds = p * (dp - delta)
        dsb = ds.astype(BF16)
        dv_acc[...] += lax.dot_general(p.astype(BF16), dov, (((0,), (0,)), ((), ())), preferred_element_type=F32)
        dk_acc[...] += lax.dot_general(dsb, qv, (((0,), (0,)), ((), ())), preferred_element_type=F32)
        rows = pl.ds(pl.multiple_of(qi * tq, tq), tq)
        dq_ref[rows, :] += jnp.dot(dsb, kvv, preferred_element_type=F32) * scale
        dac_acc[...] += jnp.sum(ds, axis=0, keepdims=True)

        @pl.when(qi == nq - 1)
        def _():
            dk_ref[...] = dk_acc[...] * scale
            dv_ref[...] = dv_acc[...]
            dac_ref[...] = dac_acc[...]

    qs = pl.BlockSpec((tq, HEAD_DIM), lambda h, t, qo, ko: (qo[t], h))
    ks = pl.BlockSpec((tk, HEAD_DIM), lambda h, t, qo, ko: (ko[t], h))
    rowv = pl.BlockSpec((None, tq, 1), lambda h, t, qo, ko: (h, qo[t], 0))
    colv = pl.BlockSpec((None, 1, tk), lambda h, t, qo, ko: (h, 0, ko[t]))
    return pl.pallas_call(
        body, name=name,
        grid_spec=pltpu.PrefetchScalarGridSpec(
            num_scalar_prefetch=2, grid=(nh, len(pairs)),
            in_specs=[qs, ks, ks, qs, qs, rowv, rowv, colv,
                      pl.BlockSpec((None, tq, tk), lambda h, t, qo, ko: (qo[t] - ko[t], 0, 0))],
            out_specs=[pl.BlockSpec((s, HEAD_DIM), lambda h, t, qo, ko: (0, h)), ks, ks, colv],
            scratch_shapes=[pltpu.VMEM((tk, HEAD_DIM), F32), pltpu.VMEM((tk, HEAD_DIM), F32), pltpu.VMEM((1, tk), F32)]),
        out_shape=[jax.ShapeDtypeStruct((s, w), F32), jax.ShapeDtypeStruct((s, w), F32),
                   jax.ShapeDtypeStruct((s, w), F32), jax.ShapeDtypeStruct((nh, 1, s), F32)],
        compiler_params=_params("arbitrary", "arbitrary"),
    )(q_of, k_of, q, k, v, o, do, lse, arow * LOG2E, acol * LOG2E, _distance_bias(s, tq, dilated))


def _gate_fwd(ga, gb, pa, pb, tm=256):
    s, d = ga.shape

    def body(ga_ref, gb_ref, pa_ref, pb_ref, o_ref):
        o_ref[...] = (jax.nn.sigmoid(ga_ref[...]) * pa_ref[...]
                      + jax.nn.sigmoid(gb_ref[...]) * pb_ref[...]).astype(o_ref.dtype)

    row = pl.BlockSpec((tm, d), lambda i: (i, 0))
    return pl.pallas_call(
        body, name="gate_fwd", grid=(s // tm,), in_specs=[row] * 4, out_specs=row,
        out_shape=jax.ShapeDtypeStruct((s, d), BF16), compiler_params=_params("parallel"),
    )(ga, gb, pa, pb)


def _gate_bwd(dm, ga, gb, pa, pb, tm=256):
    s, d = ga.shape

    def body(dm_ref, ga_ref, gb_ref, pa_ref, pb_ref, dpa_ref, dpb_ref, dga_ref, dgb_ref):
        dmv = dm_ref[...]
        for g_ref, p_ref, dp_ref, dg_ref in ((ga_ref, pa_ref, dpa_ref, dga_ref), (gb_ref, pb_ref, dpb_ref, dgb_ref)):
            sg = jax.nn.sigmoid(g_ref[...])
            dp_ref[...] = (dmv * sg).astype(BF16)
            dg_ref[...] = (dmv * p_ref[...] * (sg * (1.0 - sg))).astype(BF16)

    row = pl.BlockSpec((tm, d), lambda i: (i, 0))
    return pl.pallas_call(
        body, name="gate_bwd", grid=(s // tm,), in_specs=[row] * 5, out_specs=[row] * 4,
        out_shape=[jax.ShapeDtypeStruct((s, d), BF16)] * 4, compiler_params=_params("parallel"),
    )(dm, ga, gb, pa, pb)


def _shift_down(u, k):
    row = lax.broadcasted_iota(jnp.int32, u.shape, 0)
    return jnp.where(row >= k, pltpu.roll(u, k, 0), 0.0)


def _shift_up(u, k):
    n = u.shape[0]
    row = lax.broadcasted_iota(jnp.int32, u.shape, 0)
    return jnp.where(row < n - k, pltpu.roll(u, n - k, 0), 0.0)


def _conv3(u, wc, b):
    return wc[0:1, :] * _shift_down(u, 2) + wc[1:2, :] * _shift_down(u, 1) + wc[2:3, :] * u + b


def _conv_glu_fwd(u, wc, b, tn=256):
    s, f2 = u.shape
    f = f2 // 2
    nb = f // tn

    def body(ug_ref, uv_ref, wg_ref, wv_ref, bg_ref, bv_ref, o_ref):
        cg = _conv3(ug_ref[...], wg_ref[...], bg_ref[...])
        cv = _conv3(uv_ref[...], wv_ref[...], bv_ref[...])
        o_ref[...] = (cg * jax.nn.sigmoid(cg) * cv).astype(o_ref.dtype)

    def cols(rows, off):
        return pl.BlockSpec((rows, tn), lambda j: (0, j + off))

    return pl.pallas_call(
        body, name="conv_glu_fwd", grid=(nb,),
        in_specs=[cols(s, 0), cols(s, nb), cols(3, 0), cols(3, nb), cols(1, 0), cols(1, nb)],
        out_specs=cols(s, 0), out_shape=jax.ShapeDtypeStruct((s, f), BF16),
        compiler_params=_params("parallel"),
    )(u, u, wc, wc, b, b)


def _conv_glu_bwd(u, da, wc, b, tn=256):
    s, f2 = u.shape
    f = f2 // 2
    nb = f // tn

    def body(ug_ref, uv_ref, da_ref, wg_ref, wv_ref, bg_ref, bv_ref, dug_ref, duv_ref, sg_ref, sv_ref):
        ug, uv, wg, wv = ug_ref[...], uv_ref[...], wg_ref[...], wv_ref[...]
        cg = _conv3(ug, wg, bg_ref[...])
        cv = _conv3(uv, wv, bv_ref[...])
        sig = jax.nn.sigmoid(cg)
        dav = da_ref[...]
        dcv = dav * (cg * sig)
        dcg = dav * cv * (sig * (1.0 + cg * (1.0 - sig)))
        for dc, uu, w, du_ref, st_ref in ((dcg, ug, wg, dug_ref, sg_ref), (dcv, uv, wv, duv_ref, sv_ref)):
            du = w[2:3, :] * dc + w[1:2, :] * _shift_up(dc, 1) + w[0:1, :] * _shift_up(dc, 2)
            du_ref[...] = du.astype(BF16)
            st_ref[...] = jnp.zeros_like(st_ref)
            st_ref[0:1, :] = jnp.sum(dc * _shift_down(uu, 2), axis=0, keepdims=True)
            st_ref[1:2, :] = jnp.sum(dc * _shift_down(uu, 1), axis=0, keepdims=True)
            st_ref[2:3, :] = jnp.sum(dc * uu, axis=0, keepdims=True)
            st_ref[3:4, :] = jnp.sum(dc, axis=0, keepdims=True)

    def cols(rows, off):
        return pl.BlockSpec((rows, tn), lambda j: (0, j + off))

    return pl.pallas_call(
        body, name="conv_glu_bwd", grid=(nb,),
        in_specs=[cols(s, 0), cols(s, nb), cols(s, 0), cols(3, 0), cols(3, nb), cols(1, 0), cols(1, nb)],
        out_specs=[cols(s, 0), cols(s, 0), cols(8, 0), cols(8, 0)],
        out_shape=[jax.ShapeDtypeStruct((s, f), BF16), jax.ShapeDtypeStruct((s, f), BF16),
                   jax.ShapeDtypeStruct((8, f), F32), jax.ShapeDtypeStruct((8, f), F32)],
        compiler_params=_params("parallel"),
    )(u, u, da, wc, wc, b, b)


ROW_TILES = (256, 128, 64, 32, 16, 8)
BLOCK_BYTES = 2 << 20


def _add_halves(g, r1, place):
    ns, r, c = g.shape
    rh = r // 2
    tr = _pick(rh, ROW_TILES)
    g4 = g.reshape(ns, 2, rh, c)

    def body(p_ref, g_ref, r_ref, o_ref):
        o_ref[...] = (g_ref[...].astype(F32) + r_ref[...].astype(F32)).astype(o_ref.dtype)

    def slab(s, pr):
        return s + (s >= pr[0]).astype(jnp.int32)

    return pl.pallas_call(
        body, name="add_halves",
        grid_spec=pltpu.PrefetchScalarGridSpec(
            num_scalar_prefetch=1, grid=(ns - 1, rh // tr),
            in_specs=[pl.BlockSpec((None, None, tr, c), lambda s, i, pr: (slab(s, pr), pr[1], i, 0)),
                      pl.BlockSpec((None, tr, c), lambda s, i, pr: (slab(s, pr), i, 0))],
            out_specs=pl.BlockSpec((None, tr, c), lambda s, i, pr: (slab(s, pr), i, 0))),
        out_shape=jax.ShapeDtypeStruct((ns, rh, c), BF16),
        compiler_params=_params("parallel", "parallel"),
    )(place, g4, r1)


def _sum_chips(g, r1, recv, place):
    ns, r, c = g.shape
    rh = r // 2
    tr = _pick(rh, ROW_TILES)
    g4 = g.reshape(ns, 2, rh, c)

    def body(p_ref, g_ref, r_ref, t0_ref, t1_ref, t2_ref, o_ref):
        own = g_ref[...].astype(F32) + r_ref[...].astype(F32)
        o_ref[...] = ((own + t0_ref[...].astype(F32)) + t1_ref[...].astype(F32)) + t2_ref[...].astype(F32)

    def peer(k):
        return pl.BlockSpec((None, tr, c), lambda i, pr: (k, i, 0))

    return pl.pallas_call(
        body, name="sum_chips",
        grid_spec=pltpu.PrefetchScalarGridSpec(
            num_scalar_prefetch=1, grid=(rh // tr,),
            in_specs=[pl.BlockSpec((None, None, tr, c), lambda i, pr: (pr[0], pr[1], i, 0)),
                      pl.BlockSpec((None, tr, c), lambda i, pr: (pr[0], i, 0)), peer(0), peer(1), peer(2)],
            out_specs=pl.BlockSpec((tr, c), lambda i, pr: (pr[1] * (rh // tr) + i, 0))),
        out_shape=jax.ShapeDtypeStruct((r, c), F32),
        compiler_params=_params("parallel"),
    )(place, g4, r1, recv, recv, recv)


def _sum_devices(packs):
    n, r, c = packs.shape

    def body(p_ref, o_ref):
        acc = p_ref[0]
        for d in range(1, n):
            acc = acc + p_ref[d]
        o_ref[...] = acc

    return pl.pallas_call(
        body, name="sum_devices", out_shape=jax.ShapeDtypeStruct((r, c), F32), compiler_params=_params(),
    )(packs)


def _adamw_update(wv, gv, mv, vv):
    c1 = 1.0 - ADAM_B1 ** ADAM_STEP
    c2 = 1.0 - ADAM_B2 ** ADAM_STEP
    mn = ADAM_B1 * mv + (1.0 - ADAM_B1) * gv
    vn = ADAM_B2 * vv + (1.0 - ADAM_B2) * (gv * gv)
    m_hat = mn / c1
    v_hat = vn / c2
    return -ADAM_LR * (m_hat / (jnp.sqrt(v_hat) + ADAM_EPS) + ADAM_WD * wv), mn, vn


def _adamw(w, g, m, v, name, deps=(), emit_grad=False):
    r, c = w.shape
    tr = _pick(r, [t for t in ROW_TILES if t * c * 4 <= BLOCK_BYTES]) if r >= 8 else r
    n_out = 4 if emit_grad else 3

    def body(w_ref, g_ref, m_ref, v_ref, *rest):
        outs = rest[-n_out:]
        gv = g_ref[:, :c]
        if emit_grad:
            outs[0][...] = gv
        outs[-3][...], outs[-2][...], outs[-1][...] = _adamw_update(w_ref[...], gv, m_ref[...], v_ref[...])

    blk = pl.BlockSpec((tr, c), lambda i: (i, 0))
    g_blk = pl.BlockSpec((tr, g.shape[1]), lambda i: (i, 0))
    return pl.pallas_call(
        body, name=name, grid=(r // tr,), in_specs=[blk, g_blk, blk, blk] + [ANY] * len(deps), out_specs=[blk] * n_out,
        out_shape=[jax.ShapeDtypeStruct((r, c), F32)] * n_out, compiler_params=_params("parallel"),
    )(w, g, m, v, *deps)


ANY = pl.BlockSpec(memory_space=pl.ANY)


def _place():
    x, y, c = lax.axis_index("x"), lax.axis_index("y"), lax.axis_index("c")
    chips = [(1 - x, y), (x, 1 - y), (1 - x, 1 - y)]
    return x, y, c, chips


def _remote(src, dst, send_sem, recv_sem, to):
    return pltpu.make_async_remote_copy(src_ref=src, dst_ref=dst, send_sem=send_sem, recv_sem=recv_sem,
                                        device_id=to, device_id_type=MESH)


HBM = pl.BlockSpec(memory_space=pltpu.HBM)
SEM = pl.BlockSpec(memory_space=pltpu.SEMAPHORE)
EFFECT = pltpu.SideEffectType.DATAFLOW_SIDE_EFFECTING


def _in_hbm(a):
    return pltpu.with_memory_space_constraint(a, pltpu.HBM)


def _half(ref_rows, who):
    return pl.ds(who * (ref_rows // 2), ref_rows // 2)


def _gather_start(groups, name):
    items = [it for g in groups for it in g]
    n = len(items)
    sizes = [len(g) for g in groups]

    def body(*refs):
        srcs, lands = refs[:n], refs[n:2 * n]
        sems = refs[2 * n:2 * n + 2 * len(groups)]
        token = refs[-1]
        x, y, c, chips = _place()
        j = 2 * x + y
        at = 0
        for gi, g in enumerate(groups):
            send, recv = sems[2 * gi], sems[2 * gi + 1]
            for i, (shard, split) in enumerate(g):
                src, land = srcs[at], lands[at]
                at += 1
                rows = _half(shard.shape[0], c) if split else slice(None)
                for k, chip in enumerate(chips):
                    _remote(src.at[rows], land.at[j, rows], send.at[4 * i + k], recv.at[4 * i + k], (*chip, c)).start()
                _remote(src, land.at[j], send.at[4 * i + 3], recv.at[4 * i + 3], (x, y, 1 - c)).start()
        token[...] = jnp.zeros_like(token)

    sem_shapes = []
    for sz in sizes:
        sem_shapes += [pltpu.SemaphoreType.DMA((4 * sz,)), pltpu.SemaphoreType.DMA((4 * sz,))]
    out_shape = (sem_shapes + [pltpu.HBM(sh.shape, sh.dtype) for sh, _ in items]
                 + [pltpu.HBM((N_CHIPS,) + sh.shape, sh.dtype) for sh, _ in items]
                 + [jax.ShapeDtypeStruct((8, LANES), F32)])
    ns = len(sem_shapes)
    outs = pl.pallas_call(
        body, name=name, in_specs=[HBM] * (2 * n),
        out_specs=[SEM] * ns + [HBM] * (2 * n) + [pl.BlockSpec(memory_space=pltpu.VMEM)],
        out_shape=out_shape, input_output_aliases={i: ns + i for i in range(2 * n)},
        compiler_params=pltpu.CompilerParams(has_side_effects=EFFECT),
    )(*[_in_hbm(sh) for sh, _ in items], *[_in_hbm(lax.empty((N_CHIPS,) + sh.shape, sh.dtype)) for sh, _ in items])
    sems, shards, lands, token = outs[:ns], outs[ns:ns + n], outs[ns + n:ns + 2 * n], outs[-1]
    res, at = [], 0
    for gi, sz in enumerate(sizes):
        res.append((shards[at:at + sz], lands[at:at + sz], sems[2 * gi], sems[2 * gi + 1]))
        at += sz
    return res, token


def _gather_pass(group, started, after, name):
    shards, lands, send, recv = started
    n = len(group)
    split_ix = [i for i, (_, split) in enumerate(group) if split]

    def body(*refs):
        lnds, send1, recv1 = refs[n:2 * n], refs[2 * n], refs[2 * n + 1]
        outs = refs[2 * n + 2 + len(after):]
        send2, recv2, token = outs[2 * n], outs[2 * n + 1], outs[2 * n + 2]
        x, y, c, chips = _place()
        sib = (x, y, 1 - c)
        for i, (shard, split) in enumerate(group):
            rows = _half(shard.shape[0], c) if split else slice(None)
            for k, (cx, cy) in enumerate(chips):
                landed = lnds[i].at[2 * cx + cy, rows]
                cp = _remote(landed, landed, send1.at[4 * i + k], recv1.at[4 * i + k], sib)
                cp.wait_send()
                cp.wait_recv()
            own = lnds[i].at[2 * x + y]
            cp = _remote(own, own, send1.at[4 * i + 3], recv1.at[4 * i + 3], sib)
            cp.wait_send()
            cp.wait_recv()
        for i2, i in enumerate(split_ix):
            rows = _half(group[i][0].shape[0], c)
            for k, (cx, cy) in enumerate(chips):
                landed = lnds[i].at[2 * cx + cy, rows]
                _remote(landed, landed, send2.at[3 * i2 + k], recv2.at[3 * i2 + k], sib).start()
        token[...] = jnp.zeros_like(token)

    n2 = len(split_ix)
    out_shape = ([pltpu.HBM(a.shape, a.dtype) for a in (*shards, *lands)]
                 + [pltpu.SemaphoreType.DMA((3 * n2,)), pltpu.SemaphoreType.DMA((3 * n2,)), jax.ShapeDtypeStruct((8, LANES), F32)])
    outs = pl.pallas_call(
        body, name=name, in_specs=[HBM] * (2 * n) + [SEM, SEM] + [ANY] * len(after),
        out_specs=[HBM] * (2 * n) + [SEM, SEM, pl.BlockSpec(memory_space=pltpu.VMEM)],
        out_shape=out_shape, input_output_aliases={i: i for i in range(2 * n)},
        compiler_params=pltpu.CompilerParams(has_side_effects=EFFECT),
    )(*shards, *lands, send, recv, *after)
    return outs[:n], (outs[n:2 * n], outs[2 * n], outs[2 * n + 1]), outs[2 * n + 2]


def _gather_wait(group, passed, after, name):
    lands, send2, recv2 = passed
    n = len(group)
    split_ix = [i for i, (_, split) in enumerate(group) if split]

    def body(*refs):
        lnds, s2, r2 = refs[:n], refs[n], refs[n + 1]
        x, y, c, chips = _place()
        sib = (x, y, 1 - c)
        for i2, i in enumerate(split_ix):
            rows = _half(group[i][0].shape[0], 1 - c)
            for k, (cx, cy) in enumerate(chips):
                landed = lnds[i].at[2 * cx + cy, rows]
                cp = _remote(landed, landed, s2.at[3 * i2 + k], r2.at[3 * i2 + k], sib)
                cp.wait_send()
                cp.wait_recv()

    return pl.pallas_call(
        body, name=name, in_specs=[HBM] * n + [SEM, SEM, ANY], out_specs=[HBM] * n,
        out_shape=[pltpu.HBM(a.shape, a.dtype) for a in lands], input_output_aliases={i: i for i in range(n)},
        compiler_params=pltpu.CompilerParams(has_side_effects=EFFECT),
    )(*lands, send2, recv2, after)


def _xfer_start(name, srcs, land_shapes, n_copies, copies, after):
    n, nl = len(srcs), len(land_shapes)

    def body(*refs):
        src_refs, land_refs = refs[:n], refs[n:n + nl]
        send, recv, token = refs[n + nl + 1], refs[n + nl + 2], refs[-1]
        for cp in copies(src_refs, land_refs, send, recv):
            cp.start()
        token[...] = jnp.zeros_like(token)

    lands = [_in_hbm(lax.empty(shape, dtype)) for shape, dtype in land_shapes]
    out_shape = ([pltpu.SemaphoreType.DMA((n_copies,)), pltpu.SemaphoreType.DMA((n_copies,))]
                 + [pltpu.HBM(a.shape, a.dtype) for a in (*srcs, *lands)] + [jax.ShapeDtypeStruct((8, LANES), F32)])
    outs = pl.pallas_call(
        body, name=name, in_specs=[HBM] * (n + nl) + [ANY],
        out_specs=[SEM, SEM] + [HBM] * (n + nl) + [pl.BlockSpec(memory_space=pltpu.VMEM)],
        out_shape=out_shape, input_output_aliases={i: 2 + i for i in range(n + nl)},
        compiler_params=pltpu.CompilerParams(has_side_effects=EFFECT),
    )(*[_in_hbm(a) for a in srcs], *lands, after)
    return (outs[2:2 + n], outs[2 + n:2 + n + nl], outs[0], outs[1]), outs[-1]


def _xfer_wait(name, started, copies, after):
    srcs, lands, send, recv = started
    n, nl = len(srcs), len(lands)

    def body(*refs):
        src_refs, land_refs, s_ref, r_ref = refs[:n], refs[n:n + nl], refs[n + nl], refs[n + nl + 1]
        for cp in copies(src_refs, land_refs, s_ref, r_ref):
            cp.wait_send()
            cp.wait_recv()

    outs = pl.pallas_call(
        body, name=name, in_specs=[HBM] * (n + nl) + [SEM, SEM, ANY], out_specs=[HBM] * (n + nl),
        out_shape=[pltpu.HBM(a.shape, a.dtype) for a in (*srcs, *lands)],
        input_output_aliases={i: i for i in range(n + nl)},
        compiler_params=pltpu.CompilerParams(has_side_effects=EFFECT),
    )(*srcs, *lands, send, recv, after)
    return outs[:n], outs[n:]


def _swap_copies(srcs, lands, send, recv):
    x, y, c, _ = _place()
    return [_remote(src.at[:, _half(src.shape[1], 1 - c)], land, send.at[i], recv.at[i], (x, y, 1 - c))
            for i, (src, land) in enumerate(zip(srcs, lands))]


def _scatter_copies(srcs, lands, send, recv):
    x, y, c, chips = _place()
    return [_remote(src.at[2 * cx + cy], land.at[k], send.at[3 * i + k], recv.at[3 * i + k], (cx, cy, c))
            for i, (src, land) in enumerate(zip(srcs, lands)) for k, (cx, cy) in enumerate(chips)]


def _join_copies(srcs, lands, send, recv):
    x, y, c, _ = _place()
    return [_remote(src.at[_half(src.shape[0], c)], src.at[_half(src.shape[0], c)], send.at[i], recv.at[i], (x, y, 1 - c))
            for i, src in enumerate(srcs)]


def _corner(a):
    return a[(slice(0, 1),) * a.ndim]


class _Reducer:
    def __init__(self, place):
        self.place = place
        self.state = {}

    def swap(self, key, grads, after):
        shapes = [((g.shape[0], g.shape[1] // 2, g.shape[2]), g.dtype) for g in grads]
        self.state[key], token = _xfer_start("swap_start_" + key, grads, shapes, len(grads), _swap_copies, _corner(after))
        return token

    def to_chips(self, key, after):
        grads, from_sibling = _xfer_wait("swap_wait_" + key, self.state[key], _swap_copies, after)
        sums = [_add_halves(g, r, self.place) for g, r in zip(grads, from_sibling)]
        shapes = [((3,) + s.shape[1:], s.dtype) for s in sums]
        started, token = _xfer_start("scatter_start_" + key, sums, shapes, 3 * len(sums), _scatter_copies, _corner(sums[-1]))
        self.state[key] = (grads, from_sibling, started)
        return token

    def to_core(self, key, after):
        grads, from_sibling, started = self.state[key]
        _, from_chips = _xfer_wait("scatter_wait_" + key, started, _scatter_copies, after)
        shards = [_sum_chips(g, r, rc, self.place) for g, r, rc in zip(grads, from_sibling, from_chips)]
        self.state[key], token = _xfer_start("join_start_" + key, shards, [], len(shards), _join_copies, _corner(shards[-1]))
        return token

    def finish(self, key, after):
        return _xfer_wait("join_wait_" + key, self.state.pop(key), _join_copies, after)[0]


def _gather_packs(pack, deps=()):
    def body(p_ref, *rest):
        o_ref, lsem, ssem, rsem = rest[-4:]
        x, y, c, _ = _place()
        me = 4 * x + 2 * y + c
        local = pltpu.make_async_copy(p_ref, o_ref.at[me], lsem)
        local.start()
        cps = []
        for k in range(1, N_DEV):
            fx, fy, fc = (k >> 2) & 1, (k >> 1) & 1, k & 1
            to = (x ^ fx, y ^ fy, c ^ fc)
            cps.append(_remote(p_ref, o_ref.at[me], ssem.at[k - 1], rsem.at[k - 1], to))
        for cp in cps:
            cp.start()
        for k in range(1, N_DEV):
            fx, fy, fc = (k >> 2) & 1, (k >> 1) & 1, k & 1
            src = o_ref.at[4 * (x ^ fx) + 2 * (y ^ fy) + (c ^ fc)]
            _remote(src, src, ssem.at[k - 1], rsem.at[k - 1], (x, y, c)).wait_recv()
        for cp in cps:
            cp.wait_send()
        local.wait()

    return pl.pallas_call(
        body, name="gather_packs", in_specs=[ANY] * (1 + len(deps)), out_specs=ANY,
        out_shape=jax.ShapeDtypeStruct((N_DEV,) + pack.shape, pack.dtype),
        scratch_shapes=[pltpu.SemaphoreType.DMA, pltpu.SemaphoreType.DMA((N_DEV - 1,)), pltpu.SemaphoreType.DMA((N_DEV - 1,))],
    )(pack, *deps)


LANE_TILES = (512, 896, 1408, 704, 384, 256, 128)


def _layer_grads(x, target, small, wg, rest_pass, rest_wait, red, filler):
    s, d = x.shape
    f = wg["conv"].shape[1] // 2
    w_att = N_HEADS * HEAD_DIM
    in_splits = (w_att, w_att, w_att, N_HEADS, w_att, w_att, w_att, d, d)
    in_cols = sum(in_splits)
    cs = in_cols // N_CHIPS
    cp = wg["in"].shape[2]
    tm = min(s, MM_TILE)
    tm_wide = min(s, MM_TILE // 2)
    t_in = cp
    t_up = 2 * f // N_CHIPS
    t_d = _pick(d, LANE_TILES)
    t_d2 = min(d, MM_TILE)
    t_dq = _pick(d // N_CHIPS, LANE_TILES)
    t_fq = _pick(f // N_CHIPS, LANE_TILES)

    h1 = _norm_fwd(x, small["g_attn"], group=d, name="rms1_fwd")
    proj_p = _mm(h1, wg["in"], mode="nn", b_kind="col", tm=tm_wide, tn=t_in, tk=d, name="mm_in")
    gains = {n: small[n].reshape(1, w_att) for n in ("g_q_fox", "g_k_fox", "g_q_dil", "g_k_dil")}
    qa, ka, va_b, fa, qb, kb, vb_b, ga, gb, qa_n, ka_n, qb_n, kb_n = _proj_split(
        proj_p, in_splits, cs, (F32, F32, BF16, F32, F32, F32, BF16, F32, F32),
        {0: gains["g_q_fox"], 1: gains["g_k_fox"], 4: gains["g_q_dil"], 5: gains["g_k_dil"]})
    fa_t = fa.T
    b_f = small["b_forget"].reshape(N_HEADS, 1)
    c_f = _forget_fwd(fa_t, b_f)
    slopes = jnp.asarray(2.0 ** (-8.0 * np.arange(1, N_HEADS + 1) / N_HEADS), dtype=F32)
    a_d = -(slopes[:, None] * jnp.arange(s, dtype=F32)[None, :])
    rows_f, cols_f = c_f[:, :, None], c_f[:, None, :]
    rows_d, cols_d = a_d[:, :, None], a_d[:, None, :]
    o_a, o_a32, lse_a = _attn_fwd(qa_n, ka_n, va_b, rows_f, cols_f, dilated=False, name="attn_fox_fwd")
    token = rest_pass("mid", o_a)
    rows_d = rows_d + token[0, 0]
    o_b, o_b32, lse_b = _attn_fwd(qb_n, kb_n, vb_b, rows_d, cols_d, dilated=True, name="attn_dil_fwd")
    wg = dict(wg, **rest_wait("mid", o_b))
    token = rest_pass("late", o_b)
    pa = _mm(o_a, wg["brf"], mode="nn", b_kind="col", tm=tm, tn=t_dq, tk=w_att, name="mm_brf", deps=(token,))
    pb = _mm(o_b, wg["brd"], mode="nn", b_kind="col", tm=tm, tn=t_dq, tk=w_att, name="mm_brd")
    merged = _gate_fwd(ga, gb, pa, pb)
    x1 = _mm(merged, wg["out"], mode="nn", b_kind="row", res=x, tm=tm, tn=t_d, tk=t_dq, name="mm_out")
    wg = dict(wg, **rest_wait("late", x1))
    h2 = _norm_fwd(x1, small["g_ffn"], group=d, name="rms2_fwd")
    u = _mm(h2, wg["up"], mode="nn", b_kind="col", tm=tm_wide, tn=t_up, tk=d, name="mm_up")
    act = _conv_glu_fwd(u, wg["conv"], wg["bconv"])
    dy_f, dy_b, loss_blk = _mm(act, wg["down"], mode="nn", b_kind="row", res=x1, loss_target=target,
                               tm=tm, tn=t_d2, tk=t_fq, name="mm_down")

    d_act = _mm(dy_b, wg["down"], mode="nt", b_kind="row", tm=tm, tn=t_fq, tk=d, name="mm_down_dx")
    g_down = _mm(act, dy_b, mode="tn", out_dtype=BF16, out_kind="row", tm=t_fq, tn=t_d2, tk=s, name="mm_down_dw")
    tok = red.swap("down", [g_down], g_down)
    du_g, du_v, st_g, st_v = _conv_glu_bwd(u, d_act, wg["conv"] + tok[0, 0], wg["bconv"])
    tok = red.to_chips("down", du_g)
    du = (du_g, du_v)
    g_up = _mm(h2, du, mode="tn", out_dtype=BF16, out_kind="col", tm=t_d2, tn=t_up // 2, tk=s, name="mm_up_dw", deps=(tok,))
    tok = red.to_core("down", g_up)
    tok2 = red.swap("up", [g_up], g_up)
    dh2 = _mm(du, wg["up"], mode="nt", b_kind="col", tm=tm, tn=t_d2, tk=t_up, name="mm_up_dx", deps=(tok, tok2))
    tok = red.to_chips("up", dh2)
    dx1_b, dx1_f, dg_ffn = _norm_bwd(dh2, x1, small["g_ffn"], group=d, res=dy_f, out_dtypes=(BF16, F32), name="rms2_bwd")
    d_merged = _mm(dx1_b, wg["out"], mode="nt", b_kind="row", tm=tm, tn=t_dq, tk=d, name="mm_out_dx", deps=(tok,))
    g_out = _mm(merged, dx1_b, mode="tn", out_dtype=BF16, out_kind="row", tm=t_dq, tn=t_d2, tk=s, name="mm_out_dw")
    dpa, dpb, dga, dgb = _gate_bwd(d_merged, ga, gb, pa, pb)
    do_a = _mm(dpa, wg["brf"], mode="nt", b_kind="col", out_dtype=BF16, tm=s, tn=w_att, tk=t_dq, name="mm_brf_dx")
    do_b = _mm(dpb, wg["brd"], mode="nt", b_kind="col", out_dtype=BF16, tm=s, tn=w_att, tk=t_dq, name="mm_brd_dx")
    g_brf = _mm(o_a, dpa, mode="tn", out_dtype=BF16, out_kind="col", tm=w_att, tn=t_dq, tk=s, name="mm_brf_dw")
    g_brd = _mm(o_b, dpb, mode="tn", out_dtype=BF16, out_kind="col", tm=w_att, tn=t_dq, tk=s, name="mm_brd_dw")
    tok = red.swap("mix", [g_out, g_brf, g_brd], g_brd)
    dqa_n, dka_n, dva, dac_a = _attn_bwd(qa_n, ka_n, va_b, o_a32, do_a, lse_a, rows_f + tok[0, 0], cols_f, dilated=False, name="attn_fox_bwd")
    tok = red.to_core("up", dqa_n)
    tok2 = red.to_chips("mix", dqa_n)
    dqb_n, dkb_n, dvb, _ = _attn_bwd(qb_n, kb_n, vb_b, o_b32, do_b, lse_b, rows_d + (tok[0, 0] + tok2[0, 0]), cols_d, dilated=True, name="attn_dil_bwd")
    tok = red.to_core("mix", dqb_n)
    dfa_t, db_f = _forget_bwd(dac_a[:, 0, :], fa_t, b_f)
    dproj_p, dgains = _dproj_merge(
        [dqa_n, dka_n, dva, dfa_t.T, dqb_n, dkb_n, dvb, dga, dgb], in_splits, cs, cp,
        {0: (qa, gains["g_q_fox"]), 1: (ka, gains["g_k_fox"]), 4: (qb, gains["g_q_dil"]), 5: (kb, gains["g_k_dil"])})
    dg_qf, dg_kf, dg_qd, dg_kd = dgains[0], dgains[1], dgains[4], dgains[5]
    g_in = _mm(h1, dproj_p, mode="tn", out_dtype=BF16, out_kind="col", tm=t_d2, tn=t_in, tk=s, name="mm_in_dw", deps=(tok,))
    tok = red.swap("in", [g_in], g_in)
    tok = red.to_chips("in", filler(tok))
    dh1 = _mm(dproj_p, wg["in"], mode="nt", b_kind="col", tm=tm, tn=t_d2, tk=t_in, name="mm_in_dx", deps=(tok,))
    grad_x, dg_attn = _norm_bwd(dh1, x, small["g_attn"], group=d, res=dx1_f, out_dtypes=(F32,), name="rms1_bwd")

    small_grads = {
        "g_attn": dg_attn, "b_forget": db_f.reshape(1, N_HEADS),
        "g_q_fox": dg_qf, "g_k_fox": dg_kf, "g_q_dil": dg_qd, "g_k_dil": dg_kd, "g_ffn": dg_ffn,
        "w_conv": jnp.concatenate([st_g[0:3], st_v[0:3]], axis=1),
        "b_conv": jnp.concatenate([st_g[3:4], st_v[3:4]], axis=1),
        "loss": loss_blk[0:1, 0:1],
    }
    return small_grads, grad_x


SMALL_ORDER = ("g_attn", "b_forget", "g_q_fox", "g_k_fox", "g_q_dil", "g_k_dil", "g_ffn", "w_conv", "b_conv", "loss")
WEIGHT_ORDER = ("g_attn", "w_in", "b_forget", "g_q_fox", "g_k_fox", "g_q_dil", "g_k_dil", "w_br_fox", "w_br_dil",
                "w_out", "g_ffn", "w_up", "w_conv", "b_conv", "w_down")
BIG = {"w_in": "in", "w_br_fox": "brf", "w_br_dil": "brd", "w_out": "out", "w_up": "up", "w_down": "down"}


def kernel(x, g_attn, w_in, b_forget, g_q_fox, g_k_fox, g_q_dil, g_k_dil, w_br_fox, w_br_dil, w_out, g_ffn, w_up, w_conv, b_conv, w_down, loss_target, m_g_attn, m_w_in, m_b_forget, m_g_q_fox, m_g_k_fox, m_g_q_dil, m_g_k_dil, m_w_br_fox, m_w_br_dil, m_w_out, m_g_ffn, m_w_up, m_w_conv, m_b_conv, m_w_down, v_g_attn, v_w_in, v_b_forget, v_g_q_fox, v_g_k_fox, v_g_q_dil, v_g_k_dil, v_w_br_fox, v_w_br_dil, v_w_out, v_g_ffn, v_w_up, v_w_conv, v_b_conv, v_w_down):
    w = dict(g_attn=g_attn, w_in=w_in, b_forget=b_forget, g_q_fox=g_q_fox, g_k_fox=g_k_fox, g_q_dil=g_q_dil,
             g_k_dil=g_k_dil, w_br_fox=w_br_fox, w_br_dil=w_br_dil, w_out=w_out, g_ffn=g_ffn, w_up=w_up,
             w_conv=w_conv, b_conv=b_conv, w_down=w_down)
    m = dict(g_attn=m_g_attn, w_in=m_w_in, b_forget=m_b_forget, g_q_fox=m_g_q_fox, g_k_fox=m_g_k_fox,
             g_q_dil=m_g_q_dil, g_k_dil=m_g_k_dil, w_br_fox=m_w_br_fox, w_br_dil=m_w_br_dil, w_out=m_w_out,
             g_ffn=m_g_ffn, w_up=m_w_up, w_conv=m_w_conv, b_conv=m_b_conv, w_down=m_w_down)
    v = dict(g_attn=v_g_attn, w_in=v_w_in, b_forget=v_b_forget, g_q_fox=v_g_q_fox, g_k_fox=v_g_k_fox,
             g_q_dil=v_g_q_dil, g_k_dil=v_g_k_dil, w_br_fox=v_w_br_fox, w_br_dil=v_w_br_dil, w_out=v_w_out,
             g_ffn=v_g_ffn, w_up=v_w_up, w_conv=v_w_conv, b_conv=v_b_conv, w_down=v_w_down)
    xi, yi, ci = lax.axis_index("x"), lax.axis_index("y"), lax.axis_index("c")
    chip = (2 * xi + yi).astype(jnp.int32)

    cs = w_in.shape[2]
    cp = _round_up(cs, LANES)
    conv_pad = jnp.pad(w_conv[0], ((0, 8 - w_conv.shape[1]), (0, 0)))
    first = [(jnp.pad(w_in[0].astype(BF16), ((0, 0), (0, cp - cs))), True), (conv_pad, False)]
    (started_first,), token = _gather_start([first], "gather_start_in")
    one = 1.0 + token[0, 0]
    shards = {n: (a[0] * one).astype(BF16) for n, a in
              (("brf", w_br_fox), ("brd", w_br_dil), ("out", w_out), ("up", w_up), ("down", w_down))}
    later = {"mid": ("brf", "brd", "out"), "late": ("up", "down")}
    groups = {key: [(shards[n], True) for n in members] for key, members in later.items()}
    started_later, token = _gather_start(list(groups.values()), "gather_start_rest")
    started = dict(zip(later, started_later))
    token, w["w_in"], m["w_in"], v["w_in"] = lax.optimization_barrier((token, w["w_in"], m["w_in"], v["w_in"]))
    w2, m2, v2 = ({n: a[n].reshape(a[n].shape[-2], a[n].shape[-1]) for n in BIG} for a in (w, m, v))
    early = (token, w2["w_in"], m2["w_in"], v2["w_in"])
    own_first, passed_first, token = _gather_pass(first, started_first, early, "gather_pass_in")
    land_in, land_conv = _gather_wait(first, passed_first, token, "gather_wait_in")
    wg = {"in": land_in, "bconv": b_conv,
          "conv": jnp.transpose(land_conv[:, :w_conv.shape[1], :], (1, 0, 2)).reshape(w_conv.shape[1], -1)}
    small = {n: w[n] for n in ("g_attn", "b_forget", "g_q_fox", "g_k_fox", "g_q_dil", "g_k_dil", "g_ffn")}
    small = {n: (a[0] if a.ndim == 3 else a) for n, a in small.items()}
    in_flight = {}

    def rest_pass(key, after):
        own, passed, tok = _gather_pass(groups[key], started[key], (after,), "gather_pass_" + key)
        in_flight[key] = (own, passed)
        return tok

    def rest_wait(key, after):
        own, passed = in_flight.pop(key)
        lands = _gather_wait(groups[key], passed, after, "gather_wait_" + key)
        return dict(zip(later[key], lands))

    reducer = _Reducer(jnp.stack([chip, ci.astype(jnp.int32)]))
    g_out, d_out, m_out, v_out = {}, {}, {}, {}
    reduced = {}

    def first_element(arrays):
        return jnp.stack([a[(0,) * a.ndim] for a in arrays])

    def update_big(n, deps):
        g2, dl, mn, vn = _adamw(w2[n], reduced[BIG[n]], m2[n], v2[n], name="adamw_" + n, deps=deps, emit_grad=True)
        g_out[n], d_out[n], m_out[n], v_out[n] = (a.reshape(w[n].shape) for a in (g2, dl, mn, vn))

    def update_down(tok):
        (reduced["down"],) = reducer.finish("down", tok)
        update_big("w_down", (tok,))
        return v_out["w_down"]

    small_grads, grad_x = _layer_grads(x[0], loss_target[0], small, wg, rest_pass, rest_wait, reducer, update_down)

    for key, members in (("up", ("up",)), ("mix", ("out", "brf", "brd"))):
        reduced.update(zip(members, reducer.finish(key, grad_x)))
    others = ("w_up", "w_out", "w_br_fox", "w_br_dil")
    for n in others:
        update_big(n, (grad_x,))

    flat = jnp.concatenate([small_grads[n].reshape(-1) for n in SMALL_ORDER])
    rows = _round_up(flat.shape[0], 8 * LANES) // LANES
    pack = jnp.pad(flat, (0, rows * LANES - flat.shape[0])).reshape(rows, LANES)
    packs = _gather_packs(pack, deps=(first_element([v_out[n] for n in others]),))
    total = _sum_devices(packs).reshape(-1)
    red, at = {}, 0
    for n in SMALL_ORDER:
        size = small_grads[n].size
        red[n] = total[at:at + size].reshape(small_grads[n].shape)
        at += size
    loss = red["loss"].reshape(())
    c2 = w_conv.shape[2]
    red["w_conv"] = lax.dynamic_slice_in_dim(red["w_conv"], chip * c2, c2, axis=1)

    tok = reducer.to_core("in", packs)
    smalls = [n for n in WEIGHT_ORDER if n not in BIG]
    for n in smalls:
        shape = w[n].shape
        r2 = (shape[-2], shape[-1]) if n not in ("g_attn", "b_forget", "g_ffn", "b_conv") else (1, shape[-1])
        g2 = red[n].reshape(r2)
        dl, mn, vn = _adamw(w[n].reshape(r2), g2, m[n].reshape(r2), v[n].reshape(r2), name="adamw_" + n, deps=(tok,))
        g_out[n], d_out[n], m_out[n], v_out[n] = (a.reshape(shape) for a in (g2, dl, mn, vn))
    (reduced["in"],) = reducer.finish("in", first_element([v_out[n] for n in smalls]))
    update_big("w_in", (tok,))

    return (loss, grad_x[None], *[g_out[n] for n in WEIGHT_ORDER], *[d_out[n] for n in WEIGHT_ORDER],
            *[m_out[n] for n in WEIGHT_ORDER], *[v_out[n] for n in WEIGHT_ORDER])
```

```python
import math

import jax
import jax.numpy as jnp
import numpy as np
from jax import lax
from jax.experimental import pallas as pl
from jax.experimental.pallas import tpu as pltpu

F32 = jnp.float32
BF16 = jnp.bfloat16
HEAD_DIM = 128
N_HEADS = 8
EPS = 1e-6
NEG = -1e30
LOG2E = math.log2(math.e)
N_CHIPS = 4
N_DEV = 8
LANES = 128
VMEM_LIMIT_BYTES = 56 * 1024 * 1024
DIL_PATTERNS = ((128, 1), (512, 4), (2048, 16))
ATTN_TILE = 512
MM_TILE = 1024
ADAM_LR, ADAM_B1, ADAM_B2, ADAM_EPS, ADAM_WD, ADAM_STEP = 0.001, 0.9, 0.999, 1e-08, 0.01, 10
MESH = pl.DeviceIdType.MESH


def _params(*sem):
    return pltpu.CompilerParams(dimension_semantics=sem, vmem_limit_bytes=VMEM_LIMIT_BYTES)


def _round_up(n, m):
    return -(-n // m) * m


def _pick(dim, prefs):
    for p in prefs:
        if dim % p == 0:
            return p
    raise ValueError(f"no tile for {dim} in {prefs}")


def _logical_shape(arr, kind):
    if kind is None:
        return arr.shape
    s, r, c = arr.shape
    return (r, s * c) if kind == "col" else (s * r, c)


def _spec(shape, kind, br, bc, fi, fj):
    if kind is None:
        return pl.BlockSpec((br, bc), lambda *g: (fi(*g), fj(*g)))
    _, r, c = shape
    if kind == "col":
        nb = c // bc
        assert nb * bc == c, (shape, bc)
        return pl.BlockSpec((None, br, bc), lambda *g: (fj(*g) // nb, fi(*g), fj(*g) % nb))
    nb = r // br
    assert nb * br == r, (shape, br)
    return pl.BlockSpec((None, br, bc), lambda *g: (fi(*g) // nb, fi(*g) % nb, fj(*g)))


def _mm(a, b, *, mode, tm, tn, tk, name, a_kind=None, b_kind=None, out_kind=None,
        out_dtype=F32, res=None, deps=(), loss_target=None):
    pair_a, pair_b = isinstance(a, tuple), isinstance(b, tuple)
    if pair_a or pair_b:
        return _mm_pair(a, b, mode=mode, tm=tm, tn=tn, tk=tk, name=name, b_kind=b_kind, out_kind=out_kind,
                        out_dtype=out_dtype, deps=deps)
    la, lb = _logical_shape(a, a_kind), _logical_shape(b, b_kind)
    if mode == "nn":
        (m, k), (k2, n) = la, lb
    elif mode == "nt":
        (m, k), (n, k2) = la, lb
    else:
        (k, m), (k2, n) = la, lb
    assert k == k2, (name, la, lb)
    assert m % tm == 0 and n % tn == 0 and k % tk == 0, (name, m, n, k, tm, tn, tk)
    nk = k // tk
    im = lambda i, j, l: i
    jn = lambda i, j, l: j
    lk = lambda i, j, l: l
    if mode == "tn":
        a_spec = _spec(a.shape, a_kind, tk, tm, lk, im)
        dims = (((0,), (0,)), ((), ()))
    else:
        a_spec = _spec(a.shape, a_kind, tm, tk, im, lk)
        dims = (((1,), (1,)), ((), ())) if mode == "nt" else (((1,), (0,)), ((), ()))
    if mode == "nt":
        b_spec = _spec(b.shape, b_kind, tn, tk, jn, lk)
    else:
        b_spec = _spec(b.shape, b_kind, tk, tn, lk, jn)
    if out_kind is None:
        oshape = (m, n)
    elif out_kind == "col":
        oshape = (N_CHIPS, m, n // N_CHIPS)
    else:
        oshape = (N_CHIPS, m // N_CHIPS, n)
    o_spec = _spec(oshape, out_kind, tm, tn, im, jn)
    tile = pl.BlockSpec((tm, tn), lambda i, j, l: (i, j))
    in_specs = [a_spec, b_spec]
    args = [a, b]
    for extra in (res, loss_target):
        if extra is not None:
            in_specs.append(tile)
            args.append(extra)
    in_specs += [pl.BlockSpec(memory_space=pl.ANY)] * len(deps)
    args += list(deps)
    if loss_target is None:
        out_specs, out_shape = [o_spec], [jax.ShapeDtypeStruct(oshape, out_dtype)]
    else:
        assert out_kind is None and res is not None
        out_specs = [tile, tile, pl.BlockSpec((8, LANES), lambda i, j, l: (0, 0))]
        out_shape = [jax.ShapeDtypeStruct(oshape, F32), jax.ShapeDtypeStruct(oshape, BF16),
                     jax.ShapeDtypeStruct((8, LANES), F32)]
    n_in, n_out = len(args), len(out_specs)

    def finish(out, refs, first):
        res_ref = refs[2] if res is not None else None
        outs = refs[n_in:n_in + n_out]
        if res_ref is not None:
            out = out + res_ref[...]
        if loss_target is None:
            outs[0][...] = out.astype(outs[0].dtype)
            return

        @pl.when(first)
        def _():
            outs[2][...] = jnp.zeros_like(outs[2])

        err = out - refs[3][...]
        dy = err * (1.0 / n)
        outs[0][...] = dy
        outs[1][...] = dy.astype(BF16)
        outs[2][...] += 0.5 * jnp.sum(jnp.sum(err * err, axis=-1, keepdims=True) * (1.0 / n), axis=0, keepdims=True)

    def first_tile():
        return (pl.program_id(0) == 0) & (pl.program_id(1) == 0)

    def body_whole_k(*refs):
        finish(lax.dot_general(refs[0][...], refs[1][...], dims, preferred_element_type=F32), refs, first_tile())

    def body(*refs):
        acc_ref = refs[-1]
        step = pl.program_id(2)
        first = first_tile()

        @pl.when(step == 0)
        def _():
            acc_ref[...] = jnp.zeros_like(acc_ref)

        acc_ref[...] += lax.dot_general(refs[0][...], refs[1][...], dims, preferred_element_type=F32)

        @pl.when(step == nk - 1)
        def _():
            finish(acc_ref[...], refs, first)

    outs = pl.pallas_call(
        body_whole_k if nk == 1 else body, name=name, grid=(m // tm, n // tn, nk),
        in_specs=in_specs, out_specs=out_specs, out_shape=out_shape,
        scratch_shapes=[] if nk == 1 else [pltpu.VMEM((tm, tn), F32)],
        compiler_params=_params(*(["arbitrary"] * 3 if loss_target is not None else ["parallel", "parallel", "arbitrary"])),
    )(*args)
    return outs[0] if loss_target is None else outs


def _mm_pair(a, b, *, mode, tm, tn, tk, name, b_kind, out_kind, out_dtype, deps):
    anyspec = [pl.BlockSpec(memory_space=pl.ANY)] * len(deps)
    if mode == "tn":
        assert isinstance(b, tuple) and out_kind == "col" and a.shape[0] == tk
        k, m = a.shape
        n0 = b[0].shape[1]
        n, nb0 = 2 * n0, n0 // tn
        oshape = (N_CHIPS, m, n // N_CHIPS)

        def body(a_ref, b0_ref, b1_ref, *rest):
            o_ref = rest[-1]
            for first, b_ref in ((True, b0_ref), (False, b1_ref)):
                @pl.when((pl.program_id(1) < nb0) == first)
                def _():
                    o_ref[...] = lax.dot_general(a_ref[...], b_ref[...], (((0,), (0,)), ((), ())),
                                                 preferred_element_type=F32).astype(o_ref.dtype)

        return pl.pallas_call(
            body, name=name, grid=(m // tm, n // tn),
            in_specs=[pl.BlockSpec((tk, tm), lambda i, j: (0, i)),
                      pl.BlockSpec((tk, tn), lambda i, j: (0, jnp.minimum(j, nb0 - 1))),
                      pl.BlockSpec((tk, tn), lambda i, j: (0, jnp.maximum(j - nb0, 0)))] + anyspec,
            out_specs=_spec(oshape, "col", tm, tn, lambda i, j: i, lambda i, j: j),
            out_shape=jax.ShapeDtypeStruct(oshape, out_dtype), compiler_params=_params("parallel", "arbitrary"),
        )(a, *b, *deps)
    assert mode == "nt" and isinstance(a, tuple) and out_kind is None
    m, k0 = a[0].shape
    n = _logical_shape(b, b_kind)[0]
    nk0 = k0 // tk
    nk = 2 * nk0

    def body(a0_ref, a1_ref, b_ref, *rest):
        o_ref, acc_ref = rest[-2], rest[-1]
        step = pl.program_id(2)

        @pl.when(step == 0)
        def _():
            acc_ref[...] = jnp.zeros_like(acc_ref)

        for first, a_ref in ((True, a0_ref), (False, a1_ref)):
            @pl.when((step < nk0) == first)
            def _():
                acc_ref[...] += lax.dot_general(a_ref[...], b_ref[...], (((1,), (1,)), ((), ())), preferred_element_type=F32)

        @pl.when(step == nk - 1)
        def _():
            o_ref[...] = acc_ref[...].astype(o_ref.dtype)

    return pl.pallas_call(
        body, name=name, grid=(m // tm, n // tn, nk),
        in_specs=[pl.BlockSpec((tm, tk), lambda i, j, l: (i, jnp.minimum(l, nk0 - 1))),
                  pl.BlockSpec((tm, tk), lambda i, j, l: (i, jnp.maximum(l - nk0, 0))),
                  _spec(b.shape, b_kind, tn, tk, lambda i, j, l: j, lambda i, j, l: l)] + anyspec,
        out_specs=pl.BlockSpec((tm, tn), lambda i, j, l: (i, j)),
        out_shape=jax.ShapeDtypeStruct((m, n), out_dtype), scratch_shapes=[pltpu.VMEM((tm, tn), F32)],
        compiler_params=_params("parallel", "parallel", "arbitrary"),
    )(*a, b, *deps)


def _pieces(splits, cs, cp):
    out, g0 = [], 0
    for width in splits:
        g1, runs = g0 + width, []
        for j in range(N_CHIPS):
            a, b = max(g0, cs * j), min(g1, cs * (j + 1))
            if a < b:
                runs.append((j * cp + a - cs * j, a - g0, b - a))
        out.append(runs)
        g0 = g1
    return out


def _head_norm(xv, gv):
    r = lax.rsqrt(jnp.mean(xv * xv, axis=-1, keepdims=True) + EPS)
    return (xv * r) * gv


def _head_norm_bwd(dyv, xv, gv):
    r = lax.rsqrt(jnp.mean(xv * xv, axis=-1, keepdims=True) + EPS)
    xr = xv * r
    gdy = dyv * gv
    return r * (gdy - xr * jnp.mean(gdy * xr, axis=-1, keepdims=True)), jnp.sum(dyv * xr, axis=0, keepdims=True)


def _proj_split(proj_p, splits, cs, dtypes, gains, tm=128):
    s, wp = proj_p.shape
    pieces = _pieces(splits, cs, wp // N_CHIPS)
    normed = sorted(gains)
    nseg = len(splits)

    def body(p_ref, *refs):
        g_refs, o_refs, n_refs = refs[:len(normed)], refs[len(normed):len(normed) + nseg], refs[len(normed) + nseg:]
        for o_ref, runs in zip(o_refs, pieces):
            for src, dst, n in runs:
                o_ref[:, dst:dst + n] = p_ref[:, src:src + n].astype(o_ref.dtype)
        for g_ref, n_ref, i in zip(g_refs, n_refs, normed):
            for c0 in range(0, splits[i], HEAD_DIM):
                cols = slice(c0, c0 + HEAD_DIM)
                n_ref[:, cols] = _head_norm(o_refs[i][:, cols], g_ref[:, cols]).astype(n_ref.dtype)

    return pl.pallas_call(
        body, name="proj_split", grid=(s // tm,),
        in_specs=[pl.BlockSpec((tm, wp), lambda i: (i, 0))] + [pl.BlockSpec((1, splits[i]), lambda i: (0, 0)) for i in normed],
        out_specs=[pl.BlockSpec((tm, w), lambda i: (i, 0)) for w in splits]
        + [pl.BlockSpec((tm, splits[i]), lambda i: (i, 0)) for i in normed],
        out_shape=[jax.ShapeDtypeStruct((s, w), dt) for w, dt in zip(splits, dtypes)]
        + [jax.ShapeDtypeStruct((s, splits[i]), BF16) for i in normed],
        compiler_params=_params("parallel"),
    )(proj_p, *[gains[i] for i in normed])


def _dproj_merge(parts, splits, cs, cp, norms, tm=128):
    s = parts[0].shape[0]
    wp = N_CHIPS * cp
    pieces = _pieces(splits, cs, cp)
    normed = sorted(norms)
    nseg, nn = len(splits), len(normed)

    def body(*refs):
        p_refs, x_refs, g_refs = refs[:nseg], refs[nseg:nseg + nn], refs[nseg + nn:nseg + 2 * nn]
        o_ref, dg_refs = refs[nseg + 2 * nn], refs[nseg + 2 * nn + 1:nseg + 3 * nn + 1]
        stage, tmp = refs[-2], refs[-1]

        @pl.when(pl.program_id(0) == 0)
        def _():
            for dg_ref in dg_refs:
                dg_ref[...] = jnp.zeros_like(dg_ref)

        for j in range(N_CHIPS):
            stage[:, j * cp + cs:(j + 1) * cp] = jnp.zeros((tm, cp - cs), F32)
        for i, (p_ref, runs) in enumerate(zip(p_refs, pieces)):
            src_ref = p_ref
            if i in norms:
                k = normed.index(i)
                for c0 in range(0, splits[i], HEAD_DIM):
                    cols = slice(c0, c0 + HEAD_DIM)
                    dx, dg = _head_norm_bwd(p_ref[:, cols].astype(F32), x_refs[k][:, cols], g_refs[k][:, cols])
                    tmp[:, cols] = dx
                    dg_refs[k][:, cols] += dg
                src_ref = tmp
            for dst, src, n in runs:
                stage[:, dst:dst + n] = src_ref[:, src:src + n].astype(F32)
        o_ref[...] = stage[...].astype(o_ref.dtype)

    wmax = max(splits[i] for i in normed)
    row = lambda w: pl.BlockSpec((tm, w), lambda i: (i, 0))
    vec = lambda w: pl.BlockSpec((1, w), lambda i: (0, 0))
    outs = pl.pallas_call(
        body, name="dproj_merge", grid=(s // tm,),
        in_specs=[row(w) for w in splits] + [row(splits[i]) for i in normed] + [vec(splits[i]) for i in normed],
        out_specs=[row(wp)] + [vec(splits[i]) for i in normed],
        out_shape=[jax.ShapeDtypeStruct((s, wp), BF16)] + [jax.ShapeDtypeStruct((1, splits[i]), F32) for i in normed],
        scratch_shapes=[pltpu.VMEM((tm, wp), F32), pltpu.VMEM((tm, wmax), F32)],
        compiler_params=_params("arbitrary"),
    )(*parts, *[norms[i][0] for i in normed], *[norms[i][1] for i in normed])
    return outs[0], dict(zip(normed, outs[1:]))


def _norm_fwd(x, g, *, group, name, tm=256):
    s, w = x.shape
    ng = w // group

    def body(x_ref, g_ref, o_ref):
        for i in range(ng):
            cols = slice(i * group, (i + 1) * group)
            xv = x_ref[:, cols]
            r = lax.rsqrt(jnp.mean(xv * xv, axis=-1, keepdims=True) + EPS)
            o_ref[:, cols] = ((xv * r) * g_ref[:, cols]).astype(o_ref.dtype)

    return pl.pallas_call(
        body, name=name, grid=(s // tm,),
        in_specs=[pl.BlockSpec((tm, w), lambda i: (i, 0)), pl.BlockSpec((1, w), lambda i: (0, 0))],
        out_specs=pl.BlockSpec((tm, w), lambda i: (i, 0)),
        out_shape=jax.ShapeDtypeStruct((s, w), BF16),
        compiler_params=_params("parallel"),
    )(x, g)


def _norm_bwd(dy, x, g, *, group, name, res=None, out_dtypes=(BF16,), tm=256, deps=()):
    s, w = x.shape
    ng = w // group
    n_in = 4 if res is not None else 3

    def body(*refs):
        dy_ref, x_ref, g_ref = refs[:3]
        res_ref = refs[3] if res is not None else None
        outs = refs[n_in + len(deps):]
        dx_refs, dg_ref = outs[:-1], outs[-1]

        @pl.when(pl.program_id(0) == 0)
        def _():
            dg_ref[...] = jnp.zeros_like(dg_ref)

        for i in range(ng):
            cols = slice(i * group, (i + 1) * group)
            xv = x_ref[:, cols]
            dyv = dy_ref[:, cols].astype(F32)
            r = lax.rsqrt(jnp.mean(xv * xv, axis=-1, keepdims=True) + EPS)
            xr = xv * r
            dg_ref[:, cols] += jnp.sum(dyv * xr, axis=0, keepdims=True)
            gdy = dyv * g_ref[:, cols]
            dx = r * (gdy - xr * jnp.mean(gdy * xr, axis=-1, keepdims=True))
            if res_ref is not None:
                dx = dx + res_ref[:, cols]
            for dx_ref in dx_refs:
                dx_ref[:, cols] = dx.astype(dx_ref.dtype)

    row = pl.BlockSpec((tm, w), lambda i: (i, 0))
    vec = pl.BlockSpec((1, w), lambda i: (0, 0))
    in_specs = [row, row, vec] + ([row] if res is not None else []) + [pl.BlockSpec(memory_space=pl.ANY)] * len(deps)
    args = [dy, x, g] + ([res] if res is not None else []) + list(deps)
    out_specs = [row] * len(out_dtypes) + [vec]
    out_shape = [jax.ShapeDtypeStruct((s, w), dt) for dt in out_dtypes] + [jax.ShapeDtypeStruct((1, w), F32)]
    return pl.pallas_call(
        body, name=name, grid=(s // tm,), in_specs=in_specs, out_specs=out_specs,
        out_shape=out_shape, compiler_params=_params("arbitrary"),
    )(*args)


def _split3(v):
    p1 = v.astype(BF16)
    r1 = v - p1.astype(F32)
    p2 = r1.astype(BF16)
    p3 = (r1 - p2.astype(F32)).astype(BF16)
    return p1, p2, p3


def _tri_sum(v, reverse, tcol=512):
    h, s = v.shape
    tcol = min(tcol, s)
    parts = _split3(v)
    outs = []
    for j in range(s // tcol):
        src = lax.broadcasted_iota(jnp.int32, (s, tcol), 0)
        dst = lax.broadcasted_iota(jnp.int32, (s, tcol), 1) + j * tcol
        keep = (src >= dst) if reverse else (src <= dst)
        tri = jnp.where(keep, 1.0, 0.0).astype(BF16)
        acc = jnp.zeros((h, tcol), F32)
        for p in parts:
            acc = acc + jnp.dot(p, tri, preferred_element_type=F32)
        outs.append(acc)
    return outs


def _forget_fwd(fa_t, b):
    h, s = fa_t.shape
    tcol = min(512, s)

    def body(f_ref, b_ref, c_ref):
        z = f_ref[...] + b_ref[...]
        logf = jnp.minimum(z, 0.0) - jnp.log(1.0 + jnp.exp(-jnp.abs(z)))
        for j, blk in enumerate(_tri_sum(logf, reverse=False, tcol=tcol)):
            c_ref[:, j * tcol:(j + 1) * tcol] = blk

    return pl.pallas_call(
        body, name="forget_fwd", out_shape=jax.ShapeDtypeStruct((h, s), F32),
        compiler_params=_params(),
    )(fa_t, b)


def _forget_bwd(dacol, fa_t, b):
    h, s = fa_t.shape
    tcol = min(512, s)

    def body(d_ref, f_ref, b_ref, dfa_ref, db_ref):
        z = f_ref[...] + b_ref[...]
        dc = -d_ref[...]
        total = jnp.zeros((h, 1), F32)
        for j, blk in enumerate(_tri_sum(dc, reverse=True, tcol=tcol)):
            cols = slice(j * tcol, (j + 1) * tcol)
            dfa = blk * (1.0 - jax.nn.sigmoid(z[:, cols]))
            dfa_ref[:, cols] = dfa
            total = total + jnp.sum(dfa, axis=-1, keepdims=True)
        db_ref[...] = total

    return pl.pallas_call(
        body, name="forget_bwd",
        out_shape=[jax.ShapeDtypeStruct((h, s), F32), jax.ShapeDtypeStruct((h, 1), F32)],
        compiler_params=_params(),
    )(dacol, fa_t, b)


def _distance_bias(s, tile, dilated):
    nb = s // tile
    b = lax.broadcasted_iota(jnp.int32, (nb, tile, tile), 0)
    dist = b * tile + lax.broadcasted_iota(jnp.int32, (nb, tile, tile), 1) - lax.broadcasted_iota(jnp.int32, (nb, tile, tile), 2)
    if not dilated:
        return jnp.where(dist >= 0, 0.0, NEG).astype(F32)
    mult = jnp.zeros(dist.shape, jnp.int32)
    for window, dil in DIL_PATTERNS:
        mult = mult + ((dist >= 0) & (dist <= window) & ((dist & (dil - 1)) == 0)).astype(jnp.int32)
    logm = jnp.where(mult == 3, math.log2(3.0), jnp.where(mult == 2, 1.0, 0.0))
    return jnp.where(mult > 0, logm, NEG).astype(F32)


def _logits(q, k, arow, acol, bias):
    s = lax.dot_general(q, k, (((1,), (1,)), ((), ())), preferred_element_type=F32)
    return s * (LOG2E / math.sqrt(HEAD_DIM)) + arow - acol + bias


def _attn_fwd(q, k, v, arow, acol, *, dilated, name, tq=ATTN_TILE, tk=ATTN_TILE):
    two_term = not dilated
    s, w = q.shape
    nh = w // HEAD_DIM
    assert tq == tk
    tq = tk = min(tq, s)
    nq, nk = s // tq, s // tk

    pairs = [(i, j) for i in range(nq) for j in range(i + 1)]
    q_of, k_of = (jnp.asarray(t, jnp.int32) for t in zip(*pairs))

    def body(qo_ref, ko_ref, q_ref, k_ref, v_ref, ar_ref, ac_ref, b_ref, o_ref, of_ref, lse_ref, m_ref, l_ref, acc_ref):
        t = pl.program_id(1)
        qi, ki = qo_ref[t], ko_ref[t]

        @pl.when(ki == 0)
        def _():
            m_ref[...] = jnp.full_like(m_ref, NEG)
            l_ref[...] = jnp.zeros_like(l_ref)
            acc_ref[...] = jnp.zeros_like(acc_ref)

        sc = _logits(q_ref[...], k_ref[...], ar_ref[...], ac_ref[...], b_ref[...])
        m_new = jnp.maximum(m_ref[...], jnp.max(sc, axis=-1, keepdims=True))
        alpha = jnp.exp2(m_ref[...] - m_new)
        p = jnp.exp2(sc - m_new)
        l_ref[...] = alpha * l_ref[...] + jnp.sum(p, axis=-1, keepdims=True)
        p_hi = p.astype(BF16)
        vv = v_ref[...]
        pv = jnp.dot(p_hi, vv, preferred_element_type=F32)
        if two_term:
            pv = pv + jnp.dot((p - p_hi.astype(F32)).astype(BF16), vv, preferred_element_type=F32)
        acc_ref[...] = alpha * acc_ref[...] + pv
        m_ref[...] = m_new

        @pl.when(ki == qi)
        def _():
            out = acc_ref[...] / l_ref[...]
            o_ref[...] = out.astype(o_ref.dtype)
            of_ref[...] = out
            lse_ref[...] = m_ref[...] + jnp.log2(l_ref[...])

    qs = pl.BlockSpec((tq, HEAD_DIM), lambda h, t, qo, ko: (qo[t], h))
    kv = pl.BlockSpec((tk, HEAD_DIM), lambda h, t, qo, ko: (ko[t], h))
    rowv = pl.BlockSpec((None, tq, 1), lambda h, t, qo, ko: (h, qo[t], 0))
    return pl.pallas_call(
        body, name=name,
        grid_spec=pltpu.PrefetchScalarGridSpec(
            num_scalar_prefetch=2, grid=(nh, len(pairs)),
            in_specs=[qs, kv, kv, rowv,
                      pl.BlockSpec((None, 1, tk), lambda h, t, qo, ko: (h, 0, ko[t])),
                      pl.BlockSpec((None, tq, tk), lambda h, t, qo, ko: (qo[t] - ko[t], 0, 0))],
            out_specs=[qs, qs, rowv],
            scratch_shapes=[pltpu.VMEM((tq, 1), F32), pltpu.VMEM((tq, 1), F32), pltpu.VMEM((tq, HEAD_DIM), F32)]),
        out_shape=[jax.ShapeDtypeStruct((s, w), BF16), jax.ShapeDtypeStruct((s, w), F32),
                   jax.ShapeDtypeStruct((nh, s, 1), F32)],
        compiler_params=_params("parallel", "arbitrary"),
    )(q_of, k_of, q, k, v, arow * LOG2E, acol * LOG2E, _distance_bias(s, tq, dilated))


def _attn_bwd(q, k, v, o, do, lse, arow, acol, *, dilated, name, tq=ATTN_TILE, tk=ATTN_TILE):
    s, w = q.shape
    nh = w // HEAD_DIM
    assert tq == tk
    tq = tk = min(tq, s)
    nq, nk = s // tq, s // tk
    scale = 1.0 / math.sqrt(HEAD_DIM)

    pairs = [(i, j) for j in range(nk) for i in range(j, nq)]
    q_of, k_of = (jnp.asarray(t, jnp.int32) for t in zip(*pairs))

    def body(qo_ref, ko_ref, q_ref, k_ref, v_ref, o_ref, do_ref, lse_ref, ar_ref, ac_ref, b_ref,
             dq_ref, dk_ref, dv_ref, dac_ref, dk_acc, dv_acc, dac_acc):
        t = pl.program_id(1)
        qi, ki = qo_ref[t], ko_ref[t]

        @pl.when(t == 0)
        def _():
            dq_ref[...] = jnp.zeros_like(dq_ref)

        @pl.when(qi == ki)
        def _():
            dk_acc[...] = jnp.zeros_like(dk_acc)
            dv_acc[...] = jnp.zeros_like(dv_acc)
            dac_acc[...] = jnp.zeros_like(dac_acc)

        qv, kvv, dov = q_ref[...], k_ref[...], do_ref[...]
        sc = _logits(qv, kvv, ar_ref[...], ac_ref[...], b_ref[...])
        p = jnp.exp2(sc - lse_ref[...])
        dp = lax.dot_general(dov, v_ref[...], (((1,), (1,)), ((), ())), preferred_element_type=F32)
        delta = jnp.sum(dov.astype(F32) * o_ref[...].astype(F32), axis=-1, keepdims=True)
        ds = p * (dp - delta)
        dsb = ds.astype(BF16)
        dv_acc[...] += lax.dot_general(p.astype(BF16), dov, (((0,), (0,)), ((), ())), preferred_element_type=F32)
        dk_acc[...] += lax.dot_general(dsb, qv, (((0,), (0,)), ((), ())), preferred_element_type=F32)
        rows = pl.ds(pl.multiple_of(qi * tq, tq), tq)
        dq_ref[rows, :] += jnp.dot(dsb, kvv, preferred_element_type=F32) * scale
        dac_acc[...] += jnp.sum(ds, axis=0, keepdims=True)

        @pl.when(qi == nq - 1)
        def _():
            dk_ref[...] = dk_acc[...] * scale
            dv_ref[...] = dv_acc[...]
            dac_ref[...] = dac_acc[...]

    qs = pl.BlockSpec((tq, HEAD_DIM), lambda h, t, qo, ko: (qo[t], h))
    ks = pl.BlockSpec((tk, HEAD_DIM), lambda h, t, qo, ko: (ko[t], h))
    rowv = pl.BlockSpec((None, tq, 1), lambda h, t, qo, ko: (h, qo[t], 0))
    colv = pl.BlockSpec((None, 1, tk), lambda h, t, qo, ko: (h, 0, ko[t]))
    return pl.pallas_call(
        body, name=name,
        grid_spec=pltpu.PrefetchScalarGridSpec(
            num_scalar_prefetch=2, grid=(nh, len(pairs)),
            in_specs=[qs, ks, ks, qs, qs, rowv, rowv, colv,
                      pl.BlockSpec((None, tq, tk), lambda h, t, qo, ko: (qo[t] - ko[t], 0, 0))],
            out_specs=[pl.BlockSpec((s, HEAD_DIM), lambda h, t, qo, ko: (0, h)), ks, ks, colv],
            scratch_shapes=[pltpu.VMEM((tk, HEAD_DIM), F32), pltpu.VMEM((tk, HEAD_DIM), F32), pltpu.VMEM((1, tk), F32)]),
        out_shape=[jax.ShapeDtypeStruct((s, w), F32), jax.ShapeDtypeStruct((s, w), F32),
                   jax.ShapeDtypeStruct((s, w), F32), jax.ShapeDtypeStruct((nh, 1, s), F32)],
        compiler_params=_params("arbitrary", "arbitrary"),
    )(q_of, k_of, q, k, v, o, do, lse, arow * LOG2E, acol * LOG2E, _distance_bias(s, tq, dilated))


def _gate_fwd(ga, gb, pa, pb, tm=256):
    s, d = ga.shape

    def body(ga_ref, gb_ref, pa_ref, pb_ref, o_ref):
        o_ref[...] = (jax.nn.sigmoid(ga_ref[...]) * pa_ref[...]
                      + jax.nn.sigmoid(gb_ref[...]) * pb_ref[...]).astype(o_ref.dtype)

    row = pl.BlockSpec((tm, d), lambda i: (i, 0))
    return pl.pallas_call(
        body, name="gate_fwd", grid=(s // tm,), in_specs=[row] * 4, out_specs=row,
        out_shape=jax.ShapeDtypeStruct((s, d), BF16), compiler_params=_params("parallel"),
    )(ga, gb, pa, pb)


def _gate_bwd(dm, ga, gb, pa, pb, tm=256):
    s, d = ga.shape

    def body(dm_ref, ga_ref, gb_ref, pa_ref, pb_ref, dpa_ref, dpb_ref, dga_ref, dgb_ref):
        dmv = dm_ref[...]
        for g_ref, p_ref, dp_ref, dg_ref in ((ga_ref, pa_ref, dpa_ref, dga_ref), (gb_ref, pb_ref, dpb_ref, dgb_ref)):
            sg = jax.nn.sigmoid(g_ref[...])
            dp_ref[...] = (dmv * sg).astype(BF16)
            dg_ref[...] = (dmv * p_ref[...] * (sg * (1.0 - sg))).astype(BF16)

    row = pl.BlockSpec((tm, d), lambda i: (i, 0))
    return pl.pallas_call(
        body, name="gate_bwd", grid=(s // tm,), in_specs=[row] * 5, out_specs=[row] * 4,
        out_shape=[jax.ShapeDtypeStruct((s, d), BF16)] * 4, compiler_params=_params("parallel"),
    )(dm, ga, gb, pa, pb)


def _shift_down(u, k):
    row = lax.broadcasted_iota(jnp.int32, u.shape, 0)
    return jnp.where(row >= k, pltpu.roll(u, k, 0), 0.0)


def _shift_up(u, k):
    n = u.shape[0]
    row = lax.broadcasted_iota(jnp.int32, u.shape, 0)
    return jnp.where(row < n - k, pltpu.roll(u, n - k, 0), 0.0)


def _conv3(u, wc, b):
    return wc[0:1, :] * _shift_down(u, 2) + wc[1:2, :] * _shift_down(u, 1) + wc[2:3, :] * u + b


def _conv_glu_fwd(u, wc, b, tn=256):
    s, f2 = u.shape
    f = f2 // 2
    nb = f // tn

    def body(ug_ref, uv_ref, wg_ref, wv_ref, bg_ref, bv_ref, o_ref):
        cg = _conv3(ug_ref[...], wg_ref[...], bg_ref[...])
        cv = _conv3(uv_ref[...], wv_ref[...], bv_ref[...])
        o_ref[...] = (cg * jax.nn.sigmoid(cg) * cv).astype(o_ref.dtype)

    def cols(rows, off):
        return pl.BlockSpec((rows, tn), lambda j: (0, j + off))

    return pl.pallas_call(
        body, name="conv_glu_fwd", grid=(nb,),
        in_specs=[cols(s, 0), cols(s, nb), cols(3, 0), cols(3, nb), cols(1, 0), cols(1, nb)],
        out_specs=cols(s, 0), out_shape=jax.ShapeDtypeStruct((s, f), BF16),
        compiler_params=_params("parallel"),
    )(u, u, wc, wc, b, b)


def _conv_glu_bwd(u, da, wc, b, tn=256):
    s, f2 = u.shape
    f = f2 // 2
    nb = f // tn

    def body(ug_ref, uv_ref, da_ref, wg_ref, wv_ref, bg_ref, bv_ref, dug_ref, duv_ref, sg_ref, sv_ref):
        ug, uv, wg, wv = ug_ref[...], uv_ref[...], wg_ref[...], wv_ref[...]
        cg = _conv3(ug, wg, bg_ref[...])
        cv = _conv3(uv, wv, bv_ref[...])
        sig = jax.nn.sigmoid(cg)
        dav = da_ref[...]
        dcv = dav * (cg * sig)
        dcg = dav * cv * (sig * (1.0 + cg * (1.0 - sig)))
        for dc, uu, w, du_ref, st_ref in ((dcg, ug, wg, dug_ref, sg_ref), (dcv, uv, wv, duv_ref, sv_ref)):
            du = w[2:3, :] * dc + w[1:2, :] * _shift_up(dc, 1) + w[0:1, :] * _shift_up(dc, 2)
            du_ref[...] = du.astype(BF16)
            st_ref[...] = jnp.zeros_like(st_ref)
            st_ref[0:1, :] = jnp.sum(dc * _shift_down(uu, 2), axis=0, keepdims=True)
            st_ref[1:2, :] = jnp.sum(dc * _shift_down(uu, 1), axis=0, keepdims=True)
            st_ref[2:3, :] = jnp.sum(dc * uu, axis=0, keepdims=True)
            st_ref[3:4, :] = jnp.sum(dc, axis=0, keepdims=True)

    def cols(rows, off):
        return pl.BlockSpec((rows, tn), lambda j: (0, j + off))

    return pl.pallas_call(
        body, name="conv_glu_bwd", grid=(nb,),
        in_specs=[cols(s, 0), cols(s, nb), cols(s, 0), cols(3, 0), cols(3, nb), cols(1, 0), cols(1, nb)],
        out_specs=[cols(s, 0), cols(s, 0), cols(8, 0), cols(8, 0)],
        out_shape=[jax.ShapeDtypeStruct((s, f), BF16), jax.ShapeDtypeStruct((s, f), BF16),
                   jax.ShapeDtypeStruct((8, f), F32), jax.ShapeDtypeStruct((8, f), F32)],
        compiler_params=_params("parallel"),
    )(u, u, da, wc, wc, b, b)


ROW_TILES = (256, 128, 64, 32, 16, 8)
BLOCK_BYTES = 2 << 20


def _add_halves(g, r1, place):
    ns, r, c = g.shape
    rh = r // 2
    tr = _pick(rh, ROW_TILES)
    g4 = g.reshape(ns, 2, rh, c)

    def body(p_ref, g_ref, r_ref, o_ref):
        o_ref[...] = (g_ref[...].astype(F32) + r_ref[...].astype(F32)).astype(o_ref.dtype)

    def slab(s, pr):
        return s + (s >= pr[0]).astype(jnp.int32)

    return pl.pallas_call(
        body, name="add_halves",
        grid_spec=pltpu.PrefetchScalarGridSpec(
            num_scalar_prefetch=1, grid=(ns - 1, rh // tr),
            in_specs=[pl.BlockSpec((None, None, tr, c), lambda s, i, pr: (slab(s, pr), pr[1], i, 0)),
                      pl.BlockSpec((None, tr, c), lambda s, i, pr: (slab(s, pr), i, 0))],
            out_specs=pl.BlockSpec((None, tr, c), lambda s, i, pr: (slab(s, pr), i, 0))),
        out_shape=jax.ShapeDtypeStruct((ns, rh, c), BF16),
        compiler_params=_params("parallel", "parallel"),
    )(place, g4, r1)


def _sum_chips(g, r1, recv, place):
    ns, r, c = g.shape
    rh = r // 2
    tr = _pick(rh, ROW_TILES)
    g4 = g.reshape(ns, 2, rh, c)

    def body(p_ref, g_ref, r_ref, t0_ref, t1_ref, t2_ref, o_ref):
        own = g_ref[...].astype(F32) + r_ref[...].astype(F32)
        o_ref[...] = ((own + t0_ref[...].astype(F32)) + t1_ref[...].astype(F32)) + t2_ref[...].astype(F32)

    def peer(k):
        return pl.BlockSpec((None, tr, c), lambda i, pr: (k, i, 0))

    return pl.pallas_call(
        body, name="sum_chips",
        grid_spec=pltpu.PrefetchScalarGridSpec(
            num_scalar_prefetch=1, grid=(rh // tr,),
            in_specs=[pl.BlockSpec((None, None, tr, c), lambda i, pr: (pr[0], pr[1], i, 0)),
                      pl.BlockSpec((None, tr, c), lambda i, pr: (pr[0], i, 0)), peer(0), peer(1), peer(2)],
            out_specs=pl.BlockSpec((tr, c), lambda i, pr: (pr[1] * (rh // tr) + i, 0))),
        out_shape=jax.ShapeDtypeStruct((r, c), F32),
        compiler_params=_params("parallel"),
    )(place, g4, r1, recv, recv, recv)


def _sum_devices(packs):
    n, r, c = packs.shape

    def body(p_ref, o_ref):
        acc = p_ref[0]
        for d in range(1, n):
            acc = acc + p_ref[d]
        o_ref[...] = acc

    return pl.pallas_call(
        body, name="sum_devices", out_shape=jax.ShapeDtypeStruct((r, c), F32), compiler_params=_params(),
    )(packs)


def _adamw_update(wv, gv, mv, vv):
    c1 = 1.0 - ADAM_B1 ** ADAM_STEP
    c2 = 1.0 - ADAM_B2 ** ADAM_STEP
    mn = ADAM_B1 * mv + (1.0 - ADAM_B1) * gv
    vn = ADAM_B2 * vv + (1.0 - ADAM_B2) * (gv * gv)
    m_hat = mn / c1
    v_hat = vn / c2
    return -ADAM_LR * (m_hat / (jnp.sqrt(v_hat) + ADAM_EPS) + ADAM_WD * wv), mn, vn


def _adamw(w, g, m, v, name, deps=(), emit_grad=False):
    r, c = w.shape
    tr = _pick(r, [t for t in ROW_TILES if t * c * 4 <= BLOCK_BYTES]) if r >= 8 else r
    n_out = 4 if emit_grad else 3

    def body(w_ref, g_ref, m_ref, v_ref, *rest):
        outs = rest[-n_out:]
        gv = g_ref[:, :c]
        if emit_grad:
            outs[0][...] = gv
        outs[-3][...], outs[-2][...], outs[-1][...] = _adamw_update(w_ref[...], gv, m_ref[...], v_ref[...])

    blk = pl.BlockSpec((tr, c), lambda i: (i, 0))
    g_blk = pl.BlockSpec((tr, g.shape[1]), lambda i: (i, 0))
    return pl.pallas_call(
        body, name=name, grid=(r // tr,), in_specs=[blk, g_blk, blk, blk] + [ANY] * len(deps), out_specs=[blk] * n_out,
        out_shape=[jax.ShapeDtypeStruct((r, c), F32)] * n_out, compiler_params=_params("parallel"),
    )(w, g, m, v, *deps)


ANY = pl.BlockSpec(memory_space=pl.ANY)


def _place():
    x, y, c = lax.axis_index("x"), lax.axis_index("y"), lax.axis_index("c")
    chips = [(1 - x, y), (x, 1 - y), (1 - x, 1 - y)]
    return x, y, c, chips


def _remote(src, dst, send_sem, recv_sem, to):
    return pltpu.make_async_remote_copy(src_ref=src, dst_ref=dst, send_sem=send_sem, recv_sem=recv_sem,
                                        device_id=to, device_id_type=MESH)


HBM = pl.BlockSpec(memory_space=pltpu.HBM)
SEM = pl.BlockSpec(memory_space=pltpu.SEMAPHORE)
EFFECT = pltpu.SideEffectType.DATAFLOW_SIDE_EFFECTING


def _in_hbm(a):
    return pltpu.with_memory_space_constraint(a, pltpu.HBM)


def _half(ref_rows, who):
    return pl.ds(who * (ref_rows // 2), ref_rows // 2)


def _gather_start(groups, name):
    items = [it for g in groups for it in g]
    n = len(items)
    sizes = [len(g) for g in groups]

    def body(*refs):
        srcs, lands = refs[:n], refs[n:2 * n]
        sems = refs[2 * n:2 * n + 2 * len(groups)]
        token = refs[-1]
        x, y, c, chips = _place()
        j = 2 * x + y
        at = 0
        for gi, g in enumerate(groups):
            send, recv = sems[2 * gi], sems[2 * gi + 1]
            for i, (shard, split) in enumerate(g):
                src, land = srcs[at], lands[at]
                at += 1
                rows = _half(shard.shape[0], c) if split else slice(None)
                for k, chip in enumerate(chips):
                    _remote(src.at[rows], land.at[j, rows], send.at[4 * i + k], recv.at[4 * i + k], (*chip, c)).start()
                _remote(src, land.at[j], send.at[4 * i + 3], recv.at[4 * i + 3], (x, y, 1 - c)).start()
        token[...] = jnp.zeros_like(token)

    sem_shapes = []
    for sz in sizes:
        sem_shapes += [pltpu.SemaphoreType.DMA((4 * sz,)), pltpu.SemaphoreType.DMA((4 * sz,))]
    out_shape = (sem_shapes + [pltpu.HBM(sh.shape, sh.dtype) for sh, _ in items]
                 + [pltpu.HBM((N_CHIPS,) + sh.shape, sh.dtype) for sh, _ in items]
                 + [jax.ShapeDtypeStruct((8, LANES), F32)])
    ns = len(sem_shapes)
    outs = pl.pallas_call(
        body, name=name, in_specs=[HBM] * (2 * n),
        out_specs=[SEM] * ns + [HBM] * (2 * n) + [pl.BlockSpec(memory_space=pltpu.VMEM)],
        out_shape=out_shape, input_output_aliases={i: ns + i for i in range(2 * n)},
        compiler_params=pltpu.CompilerParams(has_side_effects=EFFECT),
    )(*[_in_hbm(sh) for sh, _ in items], *[_in_hbm(lax.empty((N_CHIPS,) + sh.shape, sh.dtype)) for sh, _ in items])
    sems, shards, lands, token = outs[:ns], outs[ns:ns + n], outs[ns + n:ns + 2 * n], outs[-1]
    res, at = [], 0
    for gi, sz in enumerate(sizes):
        res.append((shards[at:at + sz], lands[at:at + sz], sems[2 * gi], sems[2 * gi + 1]))
        at += sz
    return res, token


def _gather_pass(group, started, after, name):
    shards, lands, send, recv = started
    n = len(group)
    split_ix = [i for i, (_, split) in enumerate(group) if split]

    def body(*refs):
        lnds, send1, recv1 = refs[n:2 * n], refs[2 * n], refs[2 * n + 1]
        outs = refs[2 * n + 2 + len(after):]
        send2, recv2, token = outs[2 * n], outs[2 * n + 1], outs[2 * n + 2]
        x, y, c, chips = _place()
        sib = (x, y, 1 - c)
        for i, (shard, split) in enumerate(group):
            rows = _half(shard.shape[0], c) if split else slice(None)
            for k, (cx, cy) in enumerate(chips):
                landed = lnds[i].at[2 * cx + cy, rows]
                cp = _remote(landed, landed, send1.at[4 * i + k], recv1.at[4 * i + k], sib)
                cp.wait_send()
                cp.wait_recv()
            own = lnds[i].at[2 * x + y]
            cp = _remote(own, own, send1.at[4 * i + 3], recv1.at[4 * i + 3], sib)
            cp.wait_send()
            cp.wait_recv()
        for i2, i in enumerate(split_ix):
            rows = _half(group[i][0].shape[0], c)
            for k, (cx, cy) in enumerate(chips):
                landed = lnds[i].at[2 * cx + cy, rows]
                _remote(landed, landed, send2.at[3 * i2 + k], recv2.at[3 * i2 + k], sib).start()
        token[...] = jnp.zeros_like(token)

    n2 = len(split_ix)
    out_shape = ([pltpu.HBM(a.shape, a.dtype) for a in (*shards, *lands)]
                 + [pltpu.SemaphoreType.DMA((3 * n2,)), pltpu.SemaphoreType.DMA((3 * n2,)), jax.ShapeDtypeStruct((8, LANES), F32)])
    outs = pl.pallas_call(
        body, name=name, in_specs=[HBM] * (2 * n) + [SEM, SEM] + [ANY] * len(after),
        out_specs=[HBM] * (2 * n) + [SEM, SEM, pl.BlockSpec(memory_space=pltpu.VMEM)],
        out_shape=out_shape, input_output_aliases={i: i for i in range(2 * n)},
        compiler_params=pltpu.CompilerParams(has_side_effects=EFFECT),
    )(*shards, *lands, send, recv, *after)
    return outs[:n], (outs[n:2 * n], outs[2 * n], outs[2 * n + 1]), outs[2 * n + 2]


def _gather_wait(group, passed, after, name):
    lands, send2, recv2 = passed
    n = len(group)
    split_ix = [i for i, (_, split) in enumerate(group) if split]

    def body(*refs):
        lnds, s2, r2 = refs[:n], refs[n], refs[n + 1]
        x, y, c, chips = _place()
        sib = (x, y, 1 - c)
        for i2, i in enumerate(split_ix):
            rows = _half(group[i][0].shape[0], 1 - c)
            for k, (cx, cy) in enumerate(chips):
                landed = lnds[i].at[2 * cx + cy, rows]
                cp = _remote(landed, landed, s2.at[3 * i2 + k], r2.at[3 * i2 + k], sib)
                cp.wait_send()
                cp.wait_recv()

    return pl.pallas_call(
        body, name=name, in_specs=[HBM] * n + [SEM, SEM, ANY], out_specs=[HBM] * n,
        out_shape=[pltpu.HBM(a.shape, a.dtype) for a in lands], input_output_aliases={i: i for i in range(n)},
        compiler_params=pltpu.CompilerParams(has_side_effects=EFFECT),
    )(*lands, send2, recv2, after)


def _xfer_start(name, srcs, land_shapes, n_copies, copies, after):
    n, nl = len(srcs), len(land_shapes)

    def body(*refs):
        src_refs, land_refs = refs[:n], refs[n:n + nl]
        send, recv, token = refs[n + nl + 1], refs[n + nl + 2], refs[-1]
        for cp in copies(src_refs, land_refs, send, recv):
            cp.start()
        token[...] = jnp.zeros_like(token)

    lands = [_in_hbm(lax.empty(shape, dtype)) for shape, dtype in land_shapes]
    out_shape = ([pltpu.SemaphoreType.DMA((n_copies,)), pltpu.SemaphoreType.DMA((n_copies,))]
                 + [pltpu.HBM(a.shape, a.dtype) for a in (*srcs, *lands)] + [jax.ShapeDtypeStruct((8, LANES), F32)])
    outs = pl.pallas_call(
        body, name=name, in_specs=[HBM] * (n + nl) + [ANY],
        out_specs=[SEM, SEM] + [HBM] * (n + nl) + [pl.BlockSpec(memory_space=pltpu.VMEM)],
        out_shape=out_shape, input_output_aliases={i: 2 + i for i in range(n + nl)},
        compiler_params=pltpu.CompilerParams(has_side_effects=EFFECT),
    )(*[_in_hbm(a) for a in srcs], *lands, after)
    return (outs[2:2 + n], outs[2 + n:2 + n + nl], outs[0], outs[1]), outs[-1]


def _xfer_wait(name, started, copies, after):
    srcs, lands, send, recv = started
    n, nl = len(srcs), len(lands)

    def body(*refs):
        src_refs, land_refs, s_ref, r_ref = refs[:n], refs[n:n + nl], refs[n + nl], refs[n + nl + 1]
        for cp in copies(src_refs, land_refs, s_ref, r_ref):
            cp.wait_send()
            cp.wait_recv()

    outs = pl.pallas_call(
        body, name=name, in_specs=[HBM] * (n + nl) + [SEM, SEM, ANY], out_specs=[HBM] * (n + nl),
        out_shape=[pltpu.HBM(a.shape, a.dtype) for a in (*srcs, *lands)],
        input_output_aliases={i: i for i in range(n + nl)},
        compiler_params=pltpu.CompilerParams(has_side_effects=EFFECT),
    )(*srcs, *lands, send, recv, after)
    return outs[:n], outs[n:]


def _swap_copies(srcs, lands, send, recv):
    x, y, c, _ = _place()
    return [_remote(src.at[:, _half(src.shape[1], 1 - c)], land, send.at[i], recv.at[i], (x, y, 1 - c))
            for i, (src, land) in enumerate(zip(srcs, lands))]


def _scatter_copies(srcs, lands, send, recv):
    x, y, c, chips = _place()
    return [_remote(src.at[2 * cx + cy], land.at[k], send.at[3 * i + k], recv.at[3 * i + k], (cx, cy, c))
            for i, (src, land) in enumerate(zip(srcs, lands)) for k, (cx, cy) in enumerate(chips)]


def _join_copies(srcs, lands, send, recv):
    x, y, c, _ = _place()
    return [_remote(src.at[_half(src.shape[0], c)], src.at[_half(src.shape[0], c)], send.at[i], recv.at[i], (x, y, 1 - c))
            for i, src in enumerate(srcs)]


def _corner(a):
    return a[(slice(0, 1),) * a.ndim]


class _Reducer:
    def __init__(self, place):
        self.place = place
        self.state = {}

    def swap(self, key, grads, after):
        shapes = [((g.shape[0], g.shape[1] // 2, g.shape[2]), g.dtype) for g in grads]
        self.state[key], token = _xfer_start("swap_start_" + key, grads, shapes, len(grads), _swap_copies, _corner(after))
        return token

    def to_chips(self, key, after):
        grads, from_sibling = _xfer_wait("swap_wait_" + key, self.state[key], _swap_copies, after)
        sums = [_add_halves(g, r, self.place) for g, r in zip(grads, from_sibling)]
        shapes = [((3,) + s.shape[1:], s.dtype) for s in sums]
        started, token = _xfer_start("scatter_start_" + key, sums, shapes, 3 * len(sums), _scatter_copies, _corner(sums[-1]))
        self.state[key] = (grads, from_sibling, started)
        return token

    def to_core(self, key, after):
        grads, from_sibling, started = self.state[key]
        _, from_chips = _xfer_wait("scatter_wait_" + key, started, _scatter_copies, after)
        shards = [_sum_chips(g, r, rc, self.place) for g, r, rc in zip(grads, from_sibling, from_chips)]
        self.state[key], token = _xfer_start("join_start_" + key, shards, [], len(shards), _join_copies, _corner(shards[-1]))
        return token

    def finish(self, key, after):
        return _xfer_wait("join_wait_" + key, self.state.pop(key), _join_copies, after)[0]


def _gather_packs(pack, deps=()):
    def body(p_ref, *rest):
        o_ref, lsem, ssem, rsem = rest[-4:]
        x, y, c, _ = _place()
        me = 4 * x + 2 * y + c
        local = pltpu.make_async_copy(p_ref, o_ref.at[me], lsem)
        local.start()
        cps = []
        for k in range(1, N_DEV):
            fx, fy, fc = (k >> 2) & 1, (k >> 1) & 1, k & 1
            to = (x ^ fx, y ^ fy, c ^ fc)
            cps.append(_remote(p_ref, o_ref.at[me], ssem.at[k - 1], rsem.at[k - 1], to))
        for cp in cps:
            cp.start()
        for k in range(1, N_DEV):
            fx, fy, fc = (k >> 2) & 1, (k >> 1) & 1, k & 1
            src = o_ref.at[4 * (x ^ fx) + 2 * (y ^ fy) + (c ^ fc)]
            _remote(src, src, ssem.at[k - 1], rsem.at[k - 1], (x, y, c)).wait_recv()
        for cp in cps:
            cp.wait_send()
        local.wait()

    return pl.pallas_call(
        body, name="gather_packs", in_specs=[ANY] * (1 + len(deps)), out_specs=ANY,
        out_shape=jax.ShapeDtypeStruct((N_DEV,) + pack.shape, pack.dtype),
        scratch_shapes=[pltpu.SemaphoreType.DMA, pltpu.SemaphoreType.DMA((N_DEV - 1,)), pltpu.SemaphoreType.DMA((N_DEV - 1,))],
    )(pack, *deps)


LANE_TILES = (512, 896, 1408, 704, 384, 256, 128)


def _layer_grads(x, target, small, wg, rest_pass, rest_wait, red, filler):
    s, d = x.shape
    f = wg["conv"].shape[1] // 2
    w_att = N_HEADS * HEAD_DIM
    in_splits = (w_att, w_att, w_att, N_HEADS, w_att, w_att, w_att, d, d)
    in_cols = sum(in_splits)
    cs = in_cols // N_CHIPS
    cp = wg["in"].shape[2]
    tm = min(s, MM_TILE)
    tm_wide = min(s, MM_TILE // 2)
    t_in = cp
    t_up = 2 * f // N_CHIPS
    t_d = _pick(d, LANE_TILES)
    t_d2 = min(d, MM_TILE)
    t_dq = _pick(d // N_CHIPS, LANE_TILES)
    t_fq = _pick(f // N_CHIPS, LANE_TILES)

    h1 = _norm_fwd(x, small["g_attn"], group=d, name="rms1_fwd")
    proj_p = _mm(h1, wg["in"], mode="nn", b_kind="col", tm=tm_wide, tn=t_in, tk=d, name="mm_in")
    gains = {n: small[n].reshape(1, w_att) for n in ("g_q_fox", "g_k_fox", "g_q_dil", "g_k_dil")}
    qa, ka, va_b, fa, qb, kb, vb_b, ga, gb, qa_n, ka_n, qb_n, kb_n = _proj_split(
        proj_p, in_splits, cs, (F32, F32, BF16, F32, F32, F32, BF16, F32, F32),
        {0: gains["g_q_fox"], 1: gains["g_k_fox"], 4: gains["g_q_dil"], 5: gains["g_k_dil"]})
    fa_t = fa.T
    b_f = small["b_forget"].reshape(N_HEADS, 1)
    c_f = _forget_fwd(fa_t, b_f)
    slopes = jnp.asarray(2.0 ** (-8.0 * np.arange(1, N_HEADS + 1) / N_HEADS), dtype=F32)
    a_d = -(slopes[:, None] * jnp.arange(s, dtype=F32)[None, :])
    rows_f, cols_f = c_f[:, :, None], c_f[:, None, :]
    rows_d, cols_d = a_d[:, :, None], a_d[:, None, :]
    o_a, o_a32, lse_a = _attn_fwd(qa_n, ka_n, va_b, rows_f, cols_f, dilated=False, name="attn_fox_fwd")
    token = rest_pass("mid", o_a)
    rows_d = rows_d + token[0, 0]
    o_b, o_b32, lse_b = _attn_fwd(qb_n, kb_n, vb_b, rows_d, cols_d, dilated=True, name="attn_dil_fwd")
    wg = dict(wg, **rest_wait("mid", o_b))
    token = rest_pass("late", o_b)
    pa = _mm(o_a, wg["brf"], mode="nn", b_kind="col", tm=tm, tn=t_dq, tk=w_att, name="mm_brf", deps=(token,))
    pb = _mm(o_b, wg["brd"], mode="nn", b_kind="col", tm=tm, tn=t_dq, tk=w_att, name="mm_brd")
    merged = _gate_fwd(ga, gb, pa, pb)
    x1 = _mm(merged, wg["out"], mode="nn", b_kind="row", res=x, tm=tm, tn=t_d, tk=t_dq, name="mm_out")
    wg = dict(wg, **rest_wait("late", x1))
    h2 = _norm_fwd(x1, small["g_ffn"], group=d, name="rms2_fwd")
    u = _mm(h2, wg["up"], mode="nn", b_kind="col", tm=tm_wide, tn=t_up, tk=d, name="mm_up")
    act = _conv_glu_fwd(u, wg["conv"], wg["bconv"])
    dy_f, dy_b, loss_blk = _mm(act, wg["down"], mode="nn", b_kind="row", res=x1, loss_target=target,
                               tm=tm, tn=t_d2, tk=t_fq, name="mm_down")

    d_act = _mm(dy_b, wg["down"], mode="nt", b_kind="row", tm=tm, tn=t_fq, tk=d, name="mm_down_dx")
    g_down = _mm(act, dy_b, mode="tn", out_dtype=BF16, out_kind="row", tm=t_fq, tn=t_d2, tk=s, name="mm_down_dw")
    tok = red.swap("down", [g_down], g_down)
    du_g, du_v, st_g, st_v = _conv_glu_bwd(u, d_act, wg["conv"] + tok[0, 0], wg["bconv"])
    tok = red.to_chips("down", du_g)
    du = (du_g, du_v)
    g_up = _mm(h2, du, mode="tn", out_dtype=BF16, out_kind="col", tm=t_d2, tn=t_up // 2, tk=s, name="mm_up_dw", deps=(tok,))
    tok = red.swap("up", [g_up], g_up)
    dh2 = _mm(du, wg["up"], mode="nt", b_kind="col", tm=tm, tn=t_d2, tk=t_up, name="mm_up_dx", deps=(tok,))
    tok = red.to_core("down", dh2)
    tok2 = red.to_chips("up", dh2)
    dx1_b, dx1_f, dg_ffn = _norm_bwd(dh2, x1, small["g_ffn"], group=d, res=dy_f, out_dtypes=(BF16, F32), name="rms2_bwd")
    d_merged = _mm(dx1_b, wg["out"], mode="nt", b_kind="row", tm=tm, tn=t_dq, tk=d, name="mm_out_dx", deps=(tok, tok2))
    g_out = _mm(merged, dx1_b, mode="tn", out_dtype=BF16, out_kind="row", tm=t_dq, tn=t_d2, tk=s, name="mm_out_dw")
    dpa, dpb, dga, dgb = _gate_bwd(d_merged, ga, gb, pa, pb)
    do_a = _mm(dpa, wg["brf"], mode="nt", b_kind="col", out_dtype=BF16, tm=s, tn=w_att, tk=t_dq, name="mm_brf_dx")
    do_b = _mm(dpb, wg["brd"], mode="nt", b_kind="col", out_dtype=BF16, tm=s, tn=w_att, tk=t_dq, name="mm_brd_dx")
    g_brf = _mm(o_a, dpa, mode="tn", out_dtype=BF16, out_kind="col", tm=w_att, tn=t_dq, tk=s, name="mm_brf_dw")
    g_brd = _mm(o_b, dpb, mode="tn", out_dtype=BF16, out_kind="col", tm=w_att, tn=t_dq, tk=s, name="mm_brd_dw")
    tok = red.swap("mix", [g_out, g_brf, g_brd], g_brd)
    dqa_n, dka_n, dva, dac_a = _attn_bwd(qa_n, ka_n, va_b, o_a32, do_a, lse_a, rows_f + tok[0, 0], cols_f, dilated=False, name="attn_fox_bwd")
    tok = red.to_core("up", dqa_n)
    tok2 = red.to_chips("mix", dqa_n)
    dqb_n, dkb_n, dvb, _ = _attn_bwd(qb_n, kb_n, vb_b, o_b32, do_b, lse_b, rows_d + (tok[0, 0] + tok2[0, 0]), cols_d, dilated=True, name="attn_dil_bwd")
    tok = red.to_core("mix", dqb_n)
    dfa_t, db_f = _forget_bwd(dac_a[:, 0, :], fa_t, b_f)
    dproj_p, dgains = _dproj_merge(
        [dqa_n, dka_n, dva, dfa_t.T, dqb_n, dkb_n, dvb, dga, dgb], in_splits, cs, cp,
        {0: (qa, gains["g_q_fox"]), 1: (ka, gains["g_k_fox"]), 4: (qb, gains["g_q_dil"]), 5: (kb, gains["g_k_dil"])})
    dg_qf, dg_kf, dg_qd, dg_kd = dgains[0], dgains[1], dgains[4], dgains[5]
    g_in = _mm(h1, dproj_p, mode="tn", out_dtype=BF16, out_kind="col", tm=t_d2, tn=t_in, tk=s, name="mm_in_dw", deps=(tok,))
    tok = red.swap("in", [g_in], g_in)
    tok = red.to_chips("in", filler(tok))
    dh1 = _mm(dproj_p, wg["in"], mode="nt", b_kind="col", tm=tm, tn=t_d2, tk=t_in, name="mm_in_dx", deps=(tok,))
    grad_x, dg_attn = _norm_bwd(dh1, x, small["g_attn"], group=d, res=dx1_f, out_dtypes=(F32,), name="rms1_bwd")

    small_grads = {
        "g_attn": dg_attn, "b_forget": db_f.reshape(1, N_HEADS),
        "g_q_fox": dg_qf, "g_k_fox": dg_kf, "g_q_dil": dg_qd, "g_k_dil": dg_kd, "g_ffn": dg_ffn,
        "w_conv": jnp.concatenate([st_g[0:3], st_v[0:3]], axis=1),
        "b_conv": jnp.concatenate([st_g[3:4], st_v[3:4]], axis=1),
        "loss": loss_blk[0:1, 0:1],
    }
    return small_grads, grad_x


SMALL_ORDER = ("g_attn", "b_forget", "g_q_fox", "g_k_fox", "g_q_dil", "g_k_dil", "g_ffn", "w_conv", "b_conv", "loss")
WEIGHT_ORDER = ("g_attn", "w_in", "b_forget", "g_q_fox", "g_k_fox", "g_q_dil", "g_k_dil", "w_br_fox", "w_br_dil",
                "w_out", "g_ffn", "w_up", "w_conv", "b_conv", "w_down")
BIG = {"w_in": "in", "w_br_fox": "brf", "w_br_dil": "brd", "w_out": "out", "w_up": "up", "w_down": "down"}


def kernel(x, g_attn, w_in, b_forget, g_q_fox, g_k_fox, g_q_dil, g_k_dil, w_br_fox, w_br_dil, w_out, g_ffn, w_up, w_conv, b_conv, w_down, loss_target, m_g_attn, m_w_in, m_b_forget, m_g_q_fox, m_g_k_fox, m_g_q_dil, m_g_k_dil, m_w_br_fox, m_w_br_dil, m_w_out, m_g_ffn, m_w_up, m_w_conv, m_b_conv, m_w_down, v_g_attn, v_w_in, v_b_forget, v_g_q_fox, v_g_k_fox, v_g_q_dil, v_g_k_dil, v_w_br_fox, v_w_br_dil, v_w_out, v_g_ffn, v_w_up, v_w_conv, v_b_conv, v_w_down):
    w = dict(g_attn=g_attn, w_in=w_in, b_forget=b_forget, g_q_fox=g_q_fox, g_k_fox=g_k_fox, g_q_dil=g_q_dil,
             g_k_dil=g_k_dil, w_br_fox=w_br_fox, w_br_dil=w_br_dil, w_out=w_out, g_ffn=g_ffn, w_up=w_up,
             w_conv=w_conv, b_conv=b_conv, w_down=w_down)
    m = dict(g_attn=m_g_attn, w_in=m_w_in, b_forget=m_b_forget, g_q_fox=m_g_q_fox, g_k_fox=m_g_k_fox,
             g_q_dil=m_g_q_dil, g_k_dil=m_g_k_dil, w_br_fox=m_w_br_fox, w_br_dil=m_w_br_dil, w_out=m_w_out,
             g_ffn=m_g_ffn, w_up=m_w_up, w_conv=m_w_conv, b_conv=m_b_conv, w_down=m_w_down)
    v = dict(g_attn=v_g_attn, w_in=v_w_in, b_forget=v_b_forget, g_q_fox=v_g_q_fox, g_k_fox=v_g_k_fox,
             g_q_dil=v_g_q_dil, g_k_dil=v_g_k_dil, w_br_fox=v_w_br_fox, w_br_dil=v_w_br_dil, w_out=v_w_out,
             g_ffn=v_g_ffn, w_up=v_w_up, w_conv=v_w_conv, b_conv=v_b_conv, w_down=v_w_down)
    xi, yi, ci = lax.axis_index("x"), lax.axis_index("y"), lax.axis_index("c")
    chip = (2 * xi + yi).astype(jnp.int32)

    cs = w_in.shape[2]
    cp = _round_up(cs, LANES)
    conv_pad = jnp.pad(w_conv[0], ((0, 8 - w_conv.shape[1]), (0, 0)))
    first = [(jnp.pad(w_in[0].astype(BF16), ((0, 0), (0, cp - cs))), True), (conv_pad, False)]
    (started_first,), token = _gather_start([first], "gather_start_in")
    one = 1.0 + token[0, 0]
    shards = {n: (a[0] * one).astype(BF16) for n, a in
              (("brf", w_br_fox), ("brd", w_br_dil), ("out", w_out), ("up", w_up), ("down", w_down))}
    later = {"mid": ("brf", "brd", "out"), "late": ("up", "down")}
    groups = {key: [(shards[n], True) for n in members] for key, members in later.items()}
    started_later, token = _gather_start(list(groups.values()), "gather_start_rest")
    started = dict(zip(later, started_later))
    token, w["w_in"], m["w_in"], v["w_in"] = lax.optimization_barrier((token, w["w_in"], m["w_in"], v["w_in"]))
    w2, m2, v2 = ({n: a[n].reshape(a[n].shape[-2], a[n].shape[-1]) for n in BIG} for a in (w, m, v))
    early = (token, w2["w_in"], m2["w_in"], v2["w_in"])
    own_first, passed_first, token = _gather_pass(first, started_first, early, "gather_pass_in")
    land_in, land_conv = _gather_wait(first, passed_first, token, "gather_wait_in")
    wg = {"in": land_in, "bconv": b_conv,
          "conv": jnp.transpose(land_conv[:, :w_conv.shape[1], :], (1, 0, 2)).reshape(w_conv.shape[1], -1)}
    small = {n: w[n] for n in ("g_attn", "b_forget", "g_q_fox", "g_k_fox", "g_q_dil", "g_k_dil", "g_ffn")}
    small = {n: (a[0] if a.ndim == 3 else a) for n, a in small.items()}
    in_flight = {}

    def rest_pass(key, after):
        own, passed, tok = _gather_pass(groups[key], started[key], (after,), "gather_pass_" + key)
        in_flight[key] = (own, passed)
        return tok

    def rest_wait(key, after):
        own, passed = in_flight.pop(key)
        lands = _gather_wait(groups[key], passed, after, "gather_wait_" + key)
        return dict(zip(later[key], lands))

    reducer = _Reducer(jnp.stack([chip, ci.astype(jnp.int32)]))
    g_out, d_out, m_out, v_out = {}, {}, {}, {}
    reduced = {}

    def first_element(arrays):
        return jnp.stack([a[(0,) * a.ndim] for a in arrays])

    def update_big(n, deps):
        g2, dl, mn, vn = _adamw(w2[n], reduced[BIG[n]], m2[n], v2[n], name="adamw_" + n, deps=deps, emit_grad=True)
        g_out[n], d_out[n], m_out[n], v_out[n] = (a.reshape(w[n].shape) for a in (g2, dl, mn, vn))

    def update_down(tok):
        (reduced["down"],) = reducer.finish("down", tok)
        update_big("w_down", (tok,))
        return v_out["w_down"]

    small_grads, grad_x = _layer_grads(x[0], loss_target[0], small, wg, rest_pass, rest_wait, reducer, update_down)

    for key, members in (("up", ("up",)), ("mix", ("out", "brf", "brd"))):
        reduced.update(zip(members, reducer.finish(key, grad_x)))
    others = ("w_up", "w_out", "w_br_fox", "w_br_dil")
    for n in others:
        update_big(n, (grad_x,))

    flat = jnp.concatenate([small_grads[n].reshape(-1) for n in SMALL_ORDER])
    rows = _round_up(flat.shape[0], 8 * LANES) // LANES
    pack = jnp.pad(flat, (0, rows * LANES - flat.shape[0])).reshape(rows, LANES)
    packs = _gather_packs(pack, deps=(first_element([v_out[n] for n in others]),))
    total = _sum_devices(packs).reshape(-1)
    red, at = {}, 0
    for n in SMALL_ORDER:
        size = small_grads[n].size
        red[n] = total[at:at + size].reshape(small_grads[n].shape)
        at += size
    loss = red["loss"].reshape(())
    c2 = w_conv.shape[2]
    red["w_conv"] = lax.dynamic_slice_in_dim(red["w_conv"], chip * c2, c2, axis=1)

    tok = reducer.to_core("in", packs)
    smalls = [n for n in WEIGHT_ORDER if n not in BIG]
    for n in smalls:
        shape = w[n].shape
        r2 = (shape[-2], shape[-1]) if n not in ("g_attn", "b_forget", "g_ffn", "b_conv") else (1, shape[-1])
        g2 = red[n].reshape(r2)
        dl, mn, vn = _adamw(w[n].reshape(r2), g2, m[n].reshape(r2), v[n].reshape(r2), name="adamw_" + n, deps=(tok,))
        g_out[n], d_out[n], m_out[n], v_out[n] = (a.reshape(shape) for a in (g2, dl, mn, vn))
    (reduced["in"],) = reducer.finish("in", first_element([v_out[n] for n in smalls]))
    update_big("w_in", (tok,))

    return (loss, grad_x[None], *[g_out[n] for n in WEIGHT_ORDER], *[d_out[n] for n in WEIGHT_ORDER],
            *[m_out[n] for n in WEIGHT_ORDER], *[v_out[n] for n in WEIGHT_ORDER])
```

```python
import math

import jax
import jax.numpy as jnp
import numpy as np
from jax import lax
from jax.experimental import pallas as pl
from jax.experimental.pallas import tpu as pltpu

F32 = jnp.float32
BF16 = jnp.bfloat16
HEAD_DIM = 128
N_HEADS = 8
EPS = 1e-6
NEG = -1e30
LOG2E = math.log2(math.e)
N_CHIPS = 4
N_DEV = 8
LANES = 128
VMEM_LIMIT_BYTES = 56 * 1024 * 1024
DIL_PATTERNS = ((128, 1), (512, 4), (2048, 16))
ATTN_TILE = 512
MM_TILE = 1024
ADAM_LR, ADAM_B1, ADAM_B2, ADAM_EPS, ADAM_WD, ADAM_STEP = 0.001, 0.9, 0.999, 1e-08, 0.01, 10
MESH = pl.DeviceIdType.MESH


def _params(*sem):
    return pltpu.CompilerParams(dimension_semantics=sem, vmem_limit_bytes=VMEM_LIMIT_BYTES)


def _round_up(n, m):
    return -(-n // m) * m


def _pick(dim, prefs):
    for p in prefs:
        if dim % p == 0:
            return p
    raise ValueError(f"no tile for {dim} in {prefs}")


def _logical_shape(arr, kind):
    if kind is None:
        return arr.shape
    s, r, c = arr.shape
    return (r, s * c) if kind == "col" else (s * r, c)


def _spec(shape, kind, br, bc, fi, fj):
    if kind is None:
        return pl.BlockSpec((br, bc), lambda *g: (fi(*g), fj(*g)))
    _, r, c = shape
    if kind == "col":
        nb = c // bc
        assert nb * bc == c, (shape, bc)
        return pl.BlockSpec((None, br, bc), lambda *g: (fj(*g) // nb, fi(*g), fj(*g) % nb))
    nb = r // br
    assert nb * br == r, (shape, br)
    return pl.BlockSpec((None, br, bc), lambda *g: (fi(*g) // nb, fi(*g) % nb, fj(*g)))


def _mm(a, b, *, mode, tm, tn, tk, name, a_kind=None, b_kind=None, out_kind=None,
        out_dtype=F32, res=None, deps=(), loss_target=None, gate=None):
    pair_a, pair_b = isinstance(a, tuple), isinstance(b, tuple)
    if pair_a or pair_b:
        return _mm_pair(a, b, mode=mode, tm=tm, tn=tn, tk=tk, name=name, b_kind=b_kind, out_kind=out_kind,
                        out_dtype=out_dtype, deps=deps)
    la, lb = _logical_shape(a, a_kind), _logical_shape(b, b_kind)
    if mode == "nn":
        (m, k), (k2, n) = la, lb
    elif mode == "nt":
        (m, k), (n, k2) = la, lb
    else:
        (k, m), (k2, n) = la, lb
    assert k == k2, (name, la, lb)
    assert m % tm == 0 and n % tn == 0 and k % tk == 0, (name, m, n, k, tm, tn, tk)
    nk = k // tk
    im = lambda i, j, l: i
    jn = lambda i, j, l: j
    lk = lambda i, j, l: l
    if mode == "tn":
        a_spec = _spec(a.shape, a_kind, tk, tm, lk, im)
        dims = (((0,), (0,)), ((), ()))
    else:
        a_spec = _spec(a.shape, a_kind, tm, tk, im, lk)
        dims = (((1,), (1,)), ((), ())) if mode == "nt" else (((1,), (0,)), ((), ()))
    if mode == "nt":
        b_spec = _spec(b.shape, b_kind, tn, tk, jn, lk)
    else:
        b_spec = _spec(b.shape, b_kind, tk, tn, lk, jn)
    if out_kind is None:
        oshape = (m, n)
    elif out_kind == "col":
        oshape = (N_CHIPS, m, n // N_CHIPS)
    else:
        oshape = (N_CHIPS, m // N_CHIPS, n)
    o_spec = _spec(oshape, out_kind, tm, tn, im, jn)
    tile = pl.BlockSpec((tm, tn), lambda i, j, l: (i, j))
    in_specs = [a_spec, b_spec]
    args = [a, b]
    for extra in (res, loss_target, *(gate or ())):
        if extra is not None:
            in_specs.append(tile)
            args.append(extra)
    in_specs += [pl.BlockSpec(memory_space=pl.ANY)] * len(deps)
    args += list(deps)
    if gate is not None:
        assert out_kind is None and res is None and loss_target is None
        out_specs, out_shape = [tile, tile], [jax.ShapeDtypeStruct(oshape, F32), jax.ShapeDtypeStruct(oshape, BF16)]
    elif loss_target is None:
        out_specs, out_shape = [o_spec], [jax.ShapeDtypeStruct(oshape, out_dtype)]
    else:
        assert out_kind is None and res is not None
        out_specs = [tile, tile, pl.BlockSpec((8, LANES), lambda i, j, l: (0, 0))]
        out_shape = [jax.ShapeDtypeStruct(oshape, F32), jax.ShapeDtypeStruct(oshape, BF16),
                     jax.ShapeDtypeStruct((8, LANES), F32)]
    n_in, n_out = len(args), len(out_specs)

    def finish(out, refs, first):
        res_ref = refs[2] if res is not None else None
        outs = refs[n_in:n_in + n_out]
        if res_ref is not None:
            out = out + res_ref[...]
        if gate is not None:
            ga_ref, gb_ref, pa_ref = refs[2:5]
            outs[0][...] = out
            outs[1][...] = (jax.nn.sigmoid(ga_ref[...]) * pa_ref[...] + jax.nn.sigmoid(gb_ref[...]) * out).astype(BF16)
            return
        if loss_target is None:
            outs[0][...] = out.astype(outs[0].dtype)
            return

        @pl.when(first)
        def _():
            outs[2][...] = jnp.zeros_like(outs[2])

        err = out - refs[3][...]
        dy = err * (1.0 / n)
        outs[0][...] = dy
        outs[1][...] = dy.astype(BF16)
        outs[2][...] += 0.5 * jnp.sum(jnp.sum(err * err, axis=-1, keepdims=True) * (1.0 / n), axis=0, keepdims=True)

    def first_tile():
        return (pl.program_id(0) == 0) & (pl.program_id(1) == 0)

    def body_whole_k(*refs):
        finish(lax.dot_general(refs[0][...], refs[1][...], dims, preferred_element_type=F32), refs, first_tile())

    def body(*refs):
        acc_ref = refs[-1]
        step = pl.program_id(2)
        first = first_tile()

        @pl.when(step == 0)
        def _():
            acc_ref[...] = jnp.zeros_like(acc_ref)

        acc_ref[...] += lax.dot_general(refs[0][...], refs[1][...], dims, preferred_element_type=F32)

        @pl.when(step == nk - 1)
        def _():
            finish(acc_ref[...], refs, first)

    outs = pl.pallas_call(
        body_whole_k if nk == 1 else body, name=name, grid=(m // tm, n // tn, nk),
        in_specs=in_specs, out_specs=out_specs, out_shape=out_shape,
        scratch_shapes=[] if nk == 1 else [pltpu.VMEM((tm, tn), F32)],
        compiler_params=_params(*(["arbitrary"] * 3 if loss_target is not None else ["parallel", "parallel", "arbitrary"])),
    )(*args)
    return outs[0] if loss_target is None and gate is None else outs


def _mm_pair(a, b, *, mode, tm, tn, tk, name, b_kind, out_kind, out_dtype, deps):
    anyspec = [pl.BlockSpec(memory_space=pl.ANY)] * len(deps)
    if mode == "tn":
        assert isinstance(b, tuple) and out_kind == "col" and a.shape[0] == tk
        k, m = a.shape
        n0 = b[0].shape[1]
        n, nb0 = 2 * n0, n0 // tn
        oshape = (N_CHIPS, m, n // N_CHIPS)

        def body(a_ref, b0_ref, b1_ref, *rest):
            o_ref = rest[-1]
            for first, b_ref in ((True, b0_ref), (False, b1_ref)):
                @pl.when((pl.program_id(1) < nb0) == first)
                def _():
                    o_ref[...] = lax.dot_general(a_ref[...], b_ref[...], (((0,), (0,)), ((), ())),
                                                 preferred_element_type=F32).astype(o_ref.dtype)

        return pl.pallas_call(
            body, name=name, grid=(m // tm, n // tn),
            in_specs=[pl.BlockSpec((tk, tm), lambda i, j: (0, i)),
                      pl.BlockSpec((tk, tn), lambda i, j: (0, jnp.minimum(j, nb0 - 1))),
                      pl.BlockSpec((tk, tn), lambda i, j: (0, jnp.maximum(j - nb0, 0)))] + anyspec,
            out_specs=_spec(oshape, "col", tm, tn, lambda i, j: i, lambda i, j: j),
            out_shape=jax.ShapeDtypeStruct(oshape, out_dtype), compiler_params=_params("parallel", "arbitrary"),
        )(a, *b, *deps)
    assert mode == "nt" and isinstance(a, tuple) and out_kind is None
    m, k0 = a[0].shape
    n = _logical_shape(b, b_kind)[0]
    nk0 = k0 // tk
    nk = 2 * nk0

    def body(a0_ref, a1_ref, b_ref, *rest):
        o_ref, acc_ref = rest[-2], rest[-1]
        step = pl.program_id(2)

        @pl.when(step == 0)
        def _():
            acc_ref[...] = jnp.zeros_like(acc_ref)

        for first, a_ref in ((True, a0_ref), (False, a1_ref)):
            @pl.when((step < nk0) == first)
            def _():
                acc_ref[...] += lax.dot_general(a_ref[...], b_ref[...], (((1,), (1,)), ((), ())), preferred_element_type=F32)

        @pl.when(step == nk - 1)
        def _():
            o_ref[...] = acc_ref[...].astype(o_ref.dtype)

    return pl.pallas_call(
        body, name=name, grid=(m // tm, n // tn, nk),
        in_specs=[pl.BlockSpec((tm, tk), lambda i, j, l: (i, jnp.minimum(l, nk0 - 1))),
                  pl.BlockSpec((tm, tk), lambda i, j, l: (i, jnp.maximum(l - nk0, 0))),
                  _spec(b.shape, b_kind, tn, tk, lambda i, j, l: j, lambda i, j, l: l)] + anyspec,
        out_specs=pl.BlockSpec((tm, tn), lambda i, j, l: (i, j)),
        out_shape=jax.ShapeDtypeStruct((m, n), out_dtype), scratch_shapes=[pltpu.VMEM((tm, tn), F32)],
        compiler_params=_params("parallel", "parallel", "arbitrary"),
    )(*a, b, *deps)


def _pieces(splits, cs, cp):
    out, g0 = [], 0
    for width in splits:
        g1, runs = g0 + width, []
        for j in range(N_CHIPS):
            a, b = max(g0, cs * j), min(g1, cs * (j + 1))
            if a < b:
                runs.append((j * cp + a - cs * j, a - g0, b - a))
        out.append(runs)
        g0 = g1
    return out


def _head_norm(xv, gv):
    r = lax.rsqrt(jnp.mean(xv * xv, axis=-1, keepdims=True) + EPS)
    return (xv * r) * gv


def _head_norm_bwd(dyv, xv, gv):
    r = lax.rsqrt(jnp.mean(xv * xv, axis=-1, keepdims=True) + EPS)
    xr = xv * r
    gdy = dyv * gv
    return r * (gdy - xr * jnp.mean(gdy * xr, axis=-1, keepdims=True)), jnp.sum(dyv * xr, axis=0, keepdims=True)


def _proj_split(proj_p, splits, cs, dtypes, gains, tm=128):
    s, wp = proj_p.shape
    pieces = _pieces(splits, cs, wp // N_CHIPS)
    normed = sorted(gains)
    nseg = len(splits)

    def body(p_ref, *refs):
        g_refs, o_refs, n_refs = refs[:len(normed)], refs[len(normed):len(normed) + nseg], refs[len(normed) + nseg:]
        for o_ref, runs in zip(o_refs, pieces):
            for src, dst, n in runs:
                o_ref[:, dst:dst + n] = p_ref[:, src:src + n].astype(o_ref.dtype)
        for g_ref, n_ref, i in zip(g_refs, n_refs, normed):
            for c0 in range(0, splits[i], HEAD_DIM):
                cols = slice(c0, c0 + HEAD_DIM)
                n_ref[:, cols] = _head_norm(o_refs[i][:, cols], g_ref[:, cols]).astype(n_ref.dtype)

    return pl.pallas_call(
        body, name="proj_split", grid=(s // tm,),
        in_specs=[pl.BlockSpec((tm, wp), lambda i: (i, 0))] + [pl.BlockSpec((1, splits[i]), lambda i: (0, 0)) for i in normed],
        out_specs=[pl.BlockSpec((tm, w), lambda i: (i, 0)) for w in splits]
        + [pl.BlockSpec((tm, splits[i]), lambda i: (i, 0)) for i in normed],
        out_shape=[jax.ShapeDtypeStruct((s, w), dt) for w, dt in zip(splits, dtypes)]
        + [jax.ShapeDtypeStruct((s, splits[i]), BF16) for i in normed],
        compiler_params=_params("parallel"),
    )(proj_p, *[gains[i] for i in normed])


def _dproj_merge(parts, splits, cs, cp, norms, tm=128):
    s = parts[0].shape[0]
    wp = N_CHIPS * cp
    pieces = _pieces(splits, cs, cp)
    normed = sorted(norms)
    nseg, nn = len(splits), len(normed)

    def body(*refs):
        p_refs, x_refs, g_refs = refs[:nseg], refs[nseg:nseg + nn], refs[nseg + nn:nseg + 2 * nn]
        o_ref, dg_refs = refs[nseg + 2 * nn], refs[nseg + 2 * nn + 1:nseg + 3 * nn + 1]
        stage, tmp = refs[-2], refs[-1]

        @pl.when(pl.program_id(0) == 0)
        def _():
            for dg_ref in dg_refs:
                dg_ref[...] = jnp.zeros_like(dg_ref)

        for j in range(N_CHIPS):
            stage[:, j * cp + cs:(j + 1) * cp] = jnp.zeros((tm, cp - cs), F32)
        for i, (p_ref, runs) in enumerate(zip(p_refs, pieces)):
            src_ref = p_ref
            if i in norms:
                k = normed.index(i)
                for c0 in range(0, splits[i], HEAD_DIM):
                    cols = slice(c0, c0 + HEAD_DIM)
                    dx, dg = _head_norm_bwd(p_ref[:, cols].astype(F32), x_refs[k][:, cols], g_refs[k][:, cols])
                    tmp[:, cols] = dx
                    dg_refs[k][:, cols] += dg
                src_ref = tmp
            for dst, src, n in runs:
                stage[:, dst:dst + n] = src_ref[:, src:src + n].astype(F32)
        o_ref[...] = stage[...].astype(o_ref.dtype)

    wmax = max(splits[i] for i in normed)
    row = lambda w: pl.BlockSpec((tm, w), lambda i: (i, 0))
    vec = lambda w: pl.BlockSpec((1, w), lambda i: (0, 0))
    outs = pl.pallas_call(
        body, name="dproj_merge", grid=(s // tm,),
        in_specs=[row(w) for w in splits] + [row(splits[i]) for i in normed] + [vec(splits[i]) for i in normed],
        out_specs=[row(wp)] + [vec(splits[i]) for i in normed],
        out_shape=[jax.ShapeDtypeStruct((s, wp), BF16)] + [jax.ShapeDtypeStruct((1, splits[i]), F32) for i in normed],
        scratch_shapes=[pltpu.VMEM((tm, wp), F32), pltpu.VMEM((tm, wmax), F32)],
        compiler_params=_params("arbitrary"),
    )(*parts, *[norms[i][0] for i in normed], *[norms[i][1] for i in normed])
    return outs[0], dict(zip(normed, outs[1:]))


def _norm_fwd(x, g, *, group, name, tm=256):
    s, w = x.shape
    ng = w // group

    def body(x_ref, g_ref, o_ref):
        for i in range(ng):
            cols = slice(i * group, (i + 1) * group)
            xv = x_ref[:, cols]
            r = lax.rsqrt(jnp.mean(xv * xv, axis=-1, keepdims=True) + EPS)
            o_ref[:, cols] = ((xv * r) * g_ref[:, cols]).astype(o_ref.dtype)

    return pl.pallas_call(
        body, name=name, grid=(s // tm,),
        in_specs=[pl.BlockSpec((tm, w), lambda i: (i, 0)), pl.BlockSpec((1, w), lambda i: (0, 0))],
        out_specs=pl.BlockSpec((tm, w), lambda i: (i, 0)),
        out_shape=jax.ShapeDtypeStruct((s, w), BF16),
        compiler_params=_params("parallel"),
    )(x, g)


def _norm_bwd(dy, x, g, *, group, name, res=None, out_dtypes=(BF16,), tm=256, deps=()):
    s, w = x.shape
    ng = w // group
    n_in = 4 if res is not None else 3

    def body(*refs):
        dy_ref, x_ref, g_ref = refs[:3]
        res_ref = refs[3] if res is not None else None
        outs = refs[n_in + len(deps):]
        dx_refs, dg_ref = outs[:-1], outs[-1]

        @pl.when(pl.program_id(0) == 0)
        def _():
            dg_ref[...] = jnp.zeros_like(dg_ref)

        for i in range(ng):
            cols = slice(i * group, (i + 1) * group)
            xv = x_ref[:, cols]
            dyv = dy_ref[:, cols].astype(F32)
            r = lax.rsqrt(jnp.mean(xv * xv, axis=-1, keepdims=True) + EPS)
            xr = xv * r
            dg_ref[:, cols] += jnp.sum(dyv * xr, axis=0, keepdims=True)
            gdy = dyv * g_ref[:, cols]
            dx = r * (gdy - xr * jnp.mean(gdy * xr, axis=-1, keepdims=True))
            if res_ref is not None:
                dx = dx + res_ref[:, cols]
            for dx_ref in dx_refs:
                dx_ref[:, cols] = dx.astype(dx_ref.dtype)

    row = pl.BlockSpec((tm, w), lambda i: (i, 0))
    vec = pl.BlockSpec((1, w), lambda i: (0, 0))
    in_specs = [row, row, vec] + ([row] if res is not None else []) + [pl.BlockSpec(memory_space=pl.ANY)] * len(deps)
    args = [dy, x, g] + ([res] if res is not None else []) + list(deps)
    out_specs = [row] * len(out_dtypes) + [vec]
    out_shape = [jax.ShapeDtypeStruct((s, w), dt) for dt in out_dtypes] + [jax.ShapeDtypeStruct((1, w), F32)]
    return pl.pallas_call(
        body, name=name, grid=(s // tm,), in_specs=in_specs, out_specs=out_specs,
        out_shape=out_shape, compiler_params=_params("arbitrary"),
    )(*args)


def _split3(v):
    p1 = v.astype(BF16)
    r1 = v - p1.astype(F32)
    p2 = r1.astype(BF16)
    p3 = (r1 - p2.astype(F32)).astype(BF16)
    return p1, p2, p3


def _tri_sum(v, reverse, tcol=512):
    h, s = v.shape
    tcol = min(tcol, s)
    parts = _split3(v)
    outs = []
    for j in range(s // tcol):
        src = lax.broadcasted_iota(jnp.int32, (s, tcol), 0)
        dst = lax.broadcasted_iota(jnp.int32, (s, tcol), 1) + j * tcol
        keep = (src >= dst) if reverse else (src <= dst)
        tri = jnp.where(keep, 1.0, 0.0).astype(BF16)
        acc = jnp.zeros((h, tcol), F32)
        for p in parts:
            acc = acc + jnp.dot(p, tri, preferred_element_type=F32)
        outs.append(acc)
    return outs


def _forget_fwd(fa_t, b):
    h, s = fa_t.shape
    tcol = min(512, s)

    def body(f_ref, b_ref, c_ref):
        z = f_ref[...] + b_ref[...]
        logf = jnp.minimum(z, 0.0) - jnp.log(1.0 + jnp.exp(-jnp.abs(z)))
        for j, blk in enumerate(_tri_sum(logf, reverse=False, tcol=tcol)):
            c_ref[:, j * tcol:(j + 1) * tcol] = blk

    return pl.pallas_call(
        body, name="forget_fwd", out_shape=jax.ShapeDtypeStruct((h, s), F32),
        compiler_params=_params(),
    )(fa_t, b)


def _forget_bwd(dacol, fa_t, b):
    h, s = fa_t.shape
    tcol = min(512, s)

    def body(d_ref, f_ref, b_ref, dfa_ref, db_ref):
        z = f_ref[...] + b_ref[...]
        dc = -d_ref[...]
        total = jnp.zeros((h, 1), F32)
        for j, blk in enumerate(_tri_sum(dc, reverse=True, tcol=tcol)):
            cols = slice(j * tcol, (j + 1) * tcol)
            dfa = blk * (1.0 - jax.nn.sigmoid(z[:, cols]))
            dfa_ref[:, cols] = dfa
            total = total + jnp.sum(dfa, axis=-1, keepdims=True)
        db_ref[...] = total

    return pl.pallas_call(
        body, name="forget_bwd",
        out_shape=[jax.ShapeDtypeStruct((h, s), F32), jax.ShapeDtypeStruct((h, 1), F32)],
        compiler_params=_params(),
    )(dacol, fa_t, b)


def _distance_bias(s, tile, dilated):
    nb = s // tile
    b = lax.broadcasted_iota(jnp.int32, (nb, tile, tile), 0)
    dist = b * tile + lax.broadcasted_iota(jnp.int32, (nb, tile, tile), 1) - lax.broadcasted_iota(jnp.int32, (nb, tile, tile), 2)
    if not dilated:
        return jnp.where(dist >= 0, 0.0, NEG).astype(F32)
    mult = jnp.zeros(dist.shape, jnp.int32)
    for window, dil in DIL_PATTERNS:
        mult = mult + ((dist >= 0) & (dist <= window) & ((dist & (dil - 1)) == 0)).astype(jnp.int32)
    logm = jnp.where(mult == 3, math.log2(3.0), jnp.where(mult == 2, 1.0, 0.0))
    return jnp.where(mult > 0, logm, NEG).astype(F32)


def _logits(q, k, arow, acol, bias):
    s = lax.dot_general(q, k, (((1,), (1,)), ((), ())), preferred_element_type=F32)
    return s * (LOG2E / math.sqrt(HEAD_DIM)) + arow - acol + bias


def _attn_fwd(q, k, v, arow, acol, *, dilated, name, tq=ATTN_TILE, tk=ATTN_TILE):
    two_term = not dilated
    s, w = q.shape
    nh = w // HEAD_DIM
    assert tq == tk
    tq = tk = min(tq, s)
    nq, nk = s // tq, s // tk

    pairs = [(i, j) for i in range(nq) for j in range(i + 1)]
    q_of, k_of = (jnp.asarray(t, jnp.int32) for t in zip(*pairs))

    def body(qo_ref, ko_ref, q_ref, k_ref, v_ref, ar_ref, ac_ref, b_ref, o_ref, of_ref, lse_ref, m_ref, l_ref, acc_ref):
        t = pl.program_id(1)
        qi, ki = qo_ref[t], ko_ref[t]

        @pl.when(ki == 0)
        def _():
            m_ref[...] = jnp.full_like(m_ref, NEG)
            l_ref[...] = jnp.zeros_like(l_ref)
            acc_ref[...] = jnp.zeros_like(acc_ref)

        sc = _logits(q_ref[...], k_ref[...], ar_ref[...], ac_ref[...], b_ref[...])
        m_new = jnp.maximum(m_ref[...], jnp.max(sc, axis=-1, keepdims=True))
        alpha = jnp.exp2(m_ref[...] - m_new)
        p = jnp.exp2(sc - m_new)
        l_ref[...] = alpha * l_ref[...] + jnp.sum(p, axis=-1, keepdims=True)
        p_hi = p.astype(BF16)
        vv = v_ref[...]
        pv = jnp.dot(p_hi, vv, preferred_element_type=F32)
        if two_term:
            pv = pv + jnp.dot((p - p_hi.astype(F32)).astype(BF16), vv, preferred_element_type=F32)
        acc_ref[...] = alpha * acc_ref[...] + pv
        m_ref[...] = m_new

        @pl.when(ki == qi)
        def _():
            out = acc_ref[...] / l_ref[...]
            o_ref[...] = out.astype(o_ref.dtype)
            of_ref[...] = out
            lse_ref[...] = m_ref[...] + jnp.log2(l_ref[...])

    qs = pl.BlockSpec((tq, HEAD_DIM), lambda h, t, qo, ko: (qo[t], h))
    kv = pl.BlockSpec((tk, HEAD_DIM), lambda h, t, qo, ko: (ko[t], h))
    rowv = pl.BlockSpec((None, tq, 1), lambda h, t, qo, ko: (h, qo[t], 0))
    return pl.pallas_call(
        body, name=name,
        grid_spec=pltpu.PrefetchScalarGridSpec(
            num_scalar_prefetch=2, grid=(nh, len(pairs)),
            in_specs=[qs, kv, kv, rowv,
                      pl.BlockSpec((None, 1, tk), lambda h, t, qo, ko: (h, 0, ko[t])),
                      pl.BlockSpec((None, tq, tk), lambda h, t, qo, ko: (qo[t] - ko[t], 0, 0))],
            out_specs=[qs, qs, rowv],
            scratch_shapes=[pltpu.VMEM((tq, 1), F32), pltpu.VMEM((tq, 1), F32), pltpu.VMEM((tq, HEAD_DIM), F32)]),
        out_shape=[jax.ShapeDtypeStruct((s, w), BF16), jax.ShapeDtypeStruct((s, w), F32),
                   jax.ShapeDtypeStruct((nh, s, 1), F32)],
        compiler_params=_params("parallel", "arbitrary"),
    )(q_of, k_of, q, k, v, arow * LOG2E, acol * LOG2E, _distance_bias(s, tq, dilated))


def _attn_bwd(q, k, v, o, do, lse, arow, acol, *, dilated, name, tq=ATTN_TILE, tk=ATTN_TILE):
    s, w = q.shape
    nh = w // HEAD_DIM
    assert tq == tk
    tq = tk = min(tq, s)
    nq, nk = s // tq, s // tk
    scale = 1.0 / math.sqrt(HEAD_DIM)

    pairs = [(i, j) for j in range(nk) for i in range(j, nq)]
    q_of, k_of = (jnp.asarray(t, jnp.int32) for t in zip(*pairs))

    def body(qo_ref, ko_ref, q_ref, k_ref, v_ref, o_ref, do_ref, lse_ref, ar_ref, ac_ref, b_ref,
             dq_ref, dk_ref, dv_ref, dac_ref, dk_acc, dv_acc, dac_acc):
        t = pl.program_id(1)
        qi, ki = qo_ref[t], ko_ref[t]

        @pl.when(t == 0)
        def _():
            dq_ref[...] = jnp.zeros_like(dq_ref)

        @pl.when(qi == ki)
        def _():
            dk_acc[...] = jnp.zeros_like(dk_acc)
            dv_acc[...] = jnp.zeros_like(dv_acc)
            dac_acc[...] = jnp.zeros_like(dac_acc)

        qv, kvv, dov = q_ref[...], k_ref[...], do_ref[...]
        sc = _logits(qv, kvv, ar_ref[...], ac_ref[...], b_ref[...])
        p = jnp.exp2(sc - lse_ref[...])
        dp = lax.dot_general(dov, v_ref[...], (((1,), (1,)), ((), ())), preferred_element_type=F32)
        delta = jnp.sum(dov.astype(F32) * o_ref[...].astype(F32), axis=-1, keepdims=True)
        ds = p * (dp - delta)
        dsb = ds.astype(BF16)
        dv_acc[...] += lax.dot_general(p.astype(BF16), dov, (((0,), (0,)), ((), ())), preferred_element_type=F32)
        dk_acc[...] += lax.dot_general(dsb, qv, (((0,), (0,)), ((), ())), preferred_element_type=F32)
        rows = pl.ds(pl.multiple_of(qi * tq, tq), tq)
        dq_ref[rows, :] += jnp.dot(dsb, kvv, preferred_element_type=F32) * scale
        dac_acc[...] += jnp.sum(ds, axis=0, keepdims=True)

        @pl.when(qi == nq - 1)
        def _():
            dk_ref[...] = dk_acc[...] * scale
            dv_ref[...] = dv_acc[...]
            dac_ref[...] = dac_acc[...]

    qs = pl.BlockSpec((tq, HEAD_DIM), lambda h, t, qo, ko: (qo[t], h))
    ks = pl.BlockSpec((tk, HEAD_DIM), lambda h, t, qo, ko: (ko[t], h))
    rowv = pl.BlockSpec((None, tq, 1), lambda h, t, qo, ko: (h, qo[t], 0))
    colv = pl.BlockSpec((None, 1, tk), lambda h, t, qo, ko: (h, 0, ko[t]))
    return pl.pallas_call(
        body, name=name,
        grid_spec=pltpu.PrefetchScalarGridSpec(
            num_scalar_prefetch=2, grid=(nh, len(pairs)),
            in_specs=[qs, ks, ks, qs, qs, rowv, rowv, colv,
                      pl.BlockSpec((None, tq, tk), lambda h, t, qo, ko: (qo[t] - ko[t], 0, 0))],
            out_specs=[pl.BlockSpec((s, HEAD_DIM), lambda h, t, qo, ko: (0, h)), ks, ks, colv],
            scratch_shapes=[pltpu.VMEM((tk, HEAD_DIM), F32), pltpu.VMEM((tk, HEAD_DIM), F32), pltpu.VMEM((1, tk), F32)]),
        out_shape=[jax.ShapeDtypeStruct((s, w), F32), jax.ShapeDtypeStruct((s, w), F32),
                   jax.ShapeDtypeStruct((s, w), F32), jax.ShapeDtypeStruct((nh, 1, s), F32)],
        compiler_params=_params("arbitrary", "arbitrary"),
    )(q_of, k_of, q, k, v, o, do, lse, arow * LOG2E, acol * LOG2E, _distance_bias(s, tq, dilated))


def _gate_fwd(ga, gb, pa, pb, tm=256):
    s, d = ga.shape

    def body(ga_ref, gb_ref, pa_ref, pb_ref, o_ref):
        o_ref[...] = (jax.nn.sigmoid(ga_ref[...]) * pa_ref[...]
                      + jax.nn.sigmoid(gb_ref[...]) * pb_ref[...]).astype(o_ref.dtype)

    row = pl.BlockSpec((tm, d), lambda i: (i, 0))
    return pl.pallas_call(
        body, name="gate_fwd", grid=(s // tm,), in_specs=[row] * 4, out_specs=row,
        out_shape=jax.ShapeDtypeStruct((s, d), BF16), compiler_params=_params("parallel"),
    )(ga, gb, pa, pb)


def _gate_bwd(dm, ga, gb, pa, pb, tm=256):
    s, d = ga.shape

    def body(dm_ref, ga_ref, gb_ref, pa_ref, pb_ref, dpa_ref, dpb_ref, dga_ref, dgb_ref):
        dmv = dm_ref[...]
        for g_ref, p_ref, dp_ref, dg_ref in ((ga_ref, pa_ref, dpa_ref, dga_ref), (gb_ref, pb_ref, dpb_ref, dgb_ref)):
            sg = jax.nn.sigmoid(g_ref[...])
            dp_ref[...] = (dmv * sg).astype(BF16)
            dg_ref[...] = (dmv * p_ref[...] * (sg * (1.0 - sg))).astype(BF16)

    row = pl.BlockSpec((tm, d), lambda i: (i, 0))
    return pl.pallas_call(
        body, name="gate_bwd", grid=(s // tm,), in_specs=[row] * 5, out_specs=[row] * 4,
        out_shape=[jax.ShapeDtypeStruct((s, d), BF16)] * 4, compiler_params=_params("parallel"),
    )(dm, ga, gb, pa, pb)


def _shift_down(u, k):
    row = lax.broadcasted_iota(jnp.int32, u.shape, 0)
    return jnp.where(row >= k, pltpu.roll(u, k, 0), 0.0)


def _shift_up(u, k):
    n = u.shape[0]
    row = lax.broadcasted_iota(jnp.int32, u.shape, 0)
    return jnp.where(row < n - k, pltpu.roll(u, n - k, 0), 0.0)


def _conv3(u, wc, b):
    return wc[0:1, :] * _shift_down(u, 2) + wc[1:2, :] * _shift_down(u, 1) + wc[2:3, :] * u + b


def _conv_glu_fwd(u, wc, b, tn=256):
    s, f2 = u.shape
    f = f2 // 2
    nb = f // tn

    def body(ug_ref, uv_ref, wg_ref, wv_ref, bg_ref, bv_ref, o_ref):
        cg = _conv3(ug_ref[...], wg_ref[...], bg_ref[...])
        cv = _conv3(uv_ref[...], wv_ref[...], bv_ref[...])
        o_ref[...] = (cg * jax.nn.sigmoid(cg) * cv).astype(o_ref.dtype)

    def cols(rows, off):
        return pl.BlockSpec((rows, tn), lambda j: (0, j + off))

    return pl.pallas_call(
        body, name="conv_glu_fwd", grid=(nb,),
        in_specs=[cols(s, 0), cols(s, nb), cols(3, 0), cols(3, nb), cols(1, 0), cols(1, nb)],
        out_specs=cols(s, 0), out_shape=jax.ShapeDtypeStruct((s, f), BF16),
        compiler_params=_params("parallel"),
    )(u, u, wc, wc, b, b)


def _conv_glu_bwd(u, da, wc, b, tn=256):
    s, f2 = u.shape
    f = f2 // 2
    nb = f // tn

    def body(ug_ref, uv_ref, da_ref, wg_ref, wv_ref, bg_ref, bv_ref, dug_ref, duv_ref, sg_ref, sv_ref):
        ug, uv, wg, wv = ug_ref[...], uv_ref[...], wg_ref[...], wv_ref[...]
        cg = _conv3(ug, wg, bg_ref[...])
        cv = _conv3(uv, wv, bv_ref[...])
        sig = jax.nn.sigmoid(cg)
        dav = da_ref[...]
        dcv = dav * (cg * sig)
        dcg = dav * cv * (sig * (1.0 + cg * (1.0 - sig)))
        for dc, uu, w, du_ref, st_ref in ((dcg, ug, wg, dug_ref, sg_ref), (dcv, uv, wv, duv_ref, sv_ref)):
            du = w[2:3, :] * dc + w[1:2, :] * _shift_up(dc, 1) + w[0:1, :] * _shift_up(dc, 2)
            du_ref[...] = du.astype(BF16)
            st_ref[...] = jnp.zeros_like(st_ref)
            st_ref[0:1, :] = jnp.sum(dc * _shift_down(uu, 2), axis=0, keepdims=True)
            st_ref[1:2, :] = jnp.sum(dc * _shift_down(uu, 1), axis=0, keepdims=True)
            st_ref[2:3, :] = jnp.sum(dc * uu, axis=0, keepdims=True)
            st_ref[3:4, :] = jnp.sum(dc, axis=0, keepdims=True)

    def cols(rows, off):
        return pl.BlockSpec((rows, tn), lambda j: (0, j + off))

    return pl.pallas_call(
        body, name="conv_glu_bwd", grid=(nb,),
        in_specs=[cols(s, 0), cols(s, nb), cols(s, 0), cols(3, 0), cols(3, nb), cols(1, 0), cols(1, nb)],
        out_specs=[cols(s, 0), cols(s, 0), cols(8, 0), cols(8, 0)],
        out_shape=[jax.ShapeDtypeStruct((s, f), BF16), jax.ShapeDtypeStruct((s, f), BF16),
                   jax.ShapeDtypeStruct((8, f), F32), jax.ShapeDtypeStruct((8, f), F32)],
        compiler_params=_params("parallel"),
    )(u, u, da, wc, wc, b, b)


ROW_TILES = (256, 128, 64, 32, 16, 8)
BLOCK_BYTES = 2 << 20


def _add_halves(g, r1, place):
    ns, r, c = g.shape
    rh = r // 2
    tr = _pick(rh, ROW_TILES)
    g4 = g.reshape(ns, 2, rh, c)

    def body(p_ref, g_ref, r_ref, o_ref):
        o_ref[...] = (g_ref[...].astype(F32) + r_ref[...].astype(F32)).astype(o_ref.dtype)

    def slab(s, pr):
        return s + (s >= pr[0]).astype(jnp.int32)

    return pl.pallas_call(
        body, name="add_halves",
        grid_spec=pltpu.PrefetchScalarGridSpec(
            num_scalar_prefetch=1, grid=(ns - 1, rh // tr),
            in_specs=[pl.BlockSpec((None, None, tr, c), lambda s, i, pr: (slab(s, pr), pr[1], i, 0)),
                      pl.BlockSpec((None, tr, c), lambda s, i, pr: (slab(s, pr), i, 0))],
            out_specs=pl.BlockSpec((None, tr, c), lambda s, i, pr: (slab(s, pr), i, 0))),
        out_shape=jax.ShapeDtypeStruct((ns, rh, c), BF16),
        compiler_params=_params("parallel", "parallel"),
    )(place, g4, r1)


def _sum_chips(g, r1, recv, place):
    ns, r, c = g.shape
    rh = r // 2
    tr = _pick(rh, ROW_TILES)
    g4 = g.reshape(ns, 2, rh, c)

    def body(p_ref, g_ref, r_ref, t0_ref, t1_ref, t2_ref, o_ref):
        own = g_ref[...].astype(F32) + r_ref[...].astype(F32)
        o_ref[...] = ((own + t0_ref[...].astype(F32)) + t1_ref[...].astype(F32)) + t2_ref[...].astype(F32)

    def peer(k):
        return pl.BlockSpec((None, tr, c), lambda i, pr: (k, i, 0))

    return pl.pallas_call(
        body, name="sum_chips",
        grid_spec=pltpu.PrefetchScalarGridSpec(
            num_scalar_prefetch=1, grid=(rh // tr,),
            in_specs=[pl.BlockSpec((None, None, tr, c), lambda i, pr: (pr[0], pr[1], i, 0)),
                      pl.BlockSpec((None, tr, c), lambda i, pr: (pr[0], i, 0)), peer(0), peer(1), peer(2)],
            out_specs=pl.BlockSpec((tr, c), lambda i, pr: (pr[1] * (rh // tr) + i, 0))),
        out_shape=jax.ShapeDtypeStruct((r, c), F32),
        compiler_params=_params("parallel"),
    )(place, g4, r1, recv, recv, recv)


def _sum_devices(packs):
    n, r, c = packs.shape

    def body(p_ref, o_ref):
        acc = p_ref[0]
        for d in range(1, n):
            acc = acc + p_ref[d]
        o_ref[...] = acc

    return pl.pallas_call(
        body, name="sum_devices", out_shape=jax.ShapeDtypeStruct((r, c), F32), compiler_params=_params(),
    )(packs)


def _adamw_update(wv, gv, mv, vv):
    c1 = 1.0 - ADAM_B1 ** ADAM_STEP
    c2 = 1.0 - ADAM_B2 ** ADAM_STEP
    mn = ADAM_B1 * mv + (1.0 - ADAM_B1) * gv
    vn = ADAM_B2 * vv + (1.0 - ADAM_B2) * (gv * gv)
    m_hat = mn / c1
    v_hat = vn / c2
    return -ADAM_LR * (m_hat / (jnp.sqrt(v_hat) + ADAM_EPS) + ADAM_WD * wv), mn, vn


def _adamw(w, g, m, v, name, deps=(), emit_grad=False):
    r, c = w.shape
    tr = _pick(r, [t for t in ROW_TILES if t * c * 4 <= BLOCK_BYTES]) if r >= 8 else r
    n_out = 4 if emit_grad else 3

    def body(w_ref, g_ref, m_ref, v_ref, *rest):
        outs = rest[-n_out:]
        gv = g_ref[:, :c]
        if emit_grad:
            outs[0][...] = gv
        outs[-3][...], outs[-2][...], outs[-1][...] = _adamw_update(w_ref[...], gv, m_ref[...], v_ref[...])

    blk = pl.BlockSpec((tr, c), lambda i: (i, 0))
    g_blk = pl.BlockSpec((tr, g.shape[1]), lambda i: (i, 0))
    return pl.pallas_call(
        body, name=name, grid=(r // tr,), in_specs=[blk, g_blk, blk, blk] + [ANY] * len(deps), out_specs=[blk] * n_out,
        out_shape=[jax.ShapeDtypeStruct((r, c), F32)] * n_out, compiler_params=_params("parallel"),
    )(w, g, m, v, *deps)


ANY = pl.BlockSpec(memory_space=pl.ANY)


def _place():
    x, y, c = lax.axis_index("x"), lax.axis_index("y"), lax.axis_index("c")
    chips = [(1 - x, y), (x, 1 - y), (1 - x, 1 - y)]
    return x, y, c, chips


def _remote(src, dst, send_sem, recv_sem, to):
    return pltpu.make_async_remote_copy(src_ref=src, dst_ref=dst, send_sem=send_sem, recv_sem=recv_sem,
                                        device_id=to, device_id_type=MESH)


HBM = pl.BlockSpec(memory_space=pltpu.HBM)
SEM = pl.BlockSpec(memory_space=pltpu.SEMAPHORE)
EFFECT = pltpu.SideEffectType.DATAFLOW_SIDE_EFFECTING


def _in_hbm(a):
    return pltpu.with_memory_space_constraint(a, pltpu.HBM)


def _half(ref_rows, who):
    return pl.ds(who * (ref_rows // 2), ref_rows // 2)


def _gather_start(groups, name):
    items = [it for g in groups for it in g]
    n = len(items)
    sizes = [len(g) for g in groups]

    def body(*refs):
        srcs, lands = refs[:n], refs[n:2 * n]
        sems = refs[2 * n:2 * n + 2 * len(groups)]
        token = refs[-1]
        x, y, c, chips = _place()
        j = 2 * x + y
        at = 0
        for gi, g in enumerate(groups):
            send, recv = sems[2 * gi], sems[2 * gi + 1]
            for i, (shard, split) in enumerate(g):
                src, land = srcs[at], lands[at]
                at += 1
                rows = _half(shard.shape[0], c) if split else slice(None)
                for k, chip in enumerate(chips):
                    _remote(src.at[rows], land.at[j, rows], send.at[4 * i + k], recv.at[4 * i + k], (*chip, c)).start()
                _remote(src, land.at[j], send.at[4 * i + 3], recv.at[4 * i + 3], (x, y, 1 - c)).start()
        token[...] = jnp.zeros_like(token)

    sem_shapes = []
    for sz in sizes:
        sem_shapes += [pltpu.SemaphoreType.DMA((4 * sz,)), pltpu.SemaphoreType.DMA((4 * sz,))]
    out_shape = (sem_shapes + [pltpu.HBM(sh.shape, sh.dtype) for sh, _ in items]
                 + [pltpu.HBM((N_CHIPS,) + sh.shape, sh.dtype) for sh, _ in items]
                 + [jax.ShapeDtypeStruct((8, LANES), F32)])
    ns = len(sem_shapes)
    outs = pl.pallas_call(
        body, name=name, in_specs=[HBM] * (2 * n),
        out_specs=[SEM] * ns + [HBM] * (2 * n) + [pl.BlockSpec(memory_space=pltpu.VMEM)],
        out_shape=out_shape, input_output_aliases={i: ns + i for i in range(2 * n)},
        compiler_params=pltpu.CompilerParams(has_side_effects=EFFECT),
    )(*[_in_hbm(sh) for sh, _ in items], *[_in_hbm(lax.empty((N_CHIPS,) + sh.shape, sh.dtype)) for sh, _ in items])
    sems, shards, lands, token = outs[:ns], outs[ns:ns + n], outs[ns + n:ns + 2 * n], outs[-1]
    res, at = [], 0
    for gi, sz in enumerate(sizes):
        res.append((shards[at:at + sz], lands[at:at + sz], sems[2 * gi], sems[2 * gi + 1]))
        at += sz
    return res, token


def _gather_pass(group, started, after, name):
    shards, lands, send, recv = started
    n = len(group)
    split_ix = [i for i, (_, split) in enumerate(group) if split]

    def body(*refs):
        lnds, send1, recv1 = refs[n:2 * n], refs[2 * n], refs[2 * n + 1]
        outs = refs[2 * n + 2 + len(after):]
        send2, recv2, token = outs[2 * n], outs[2 * n + 1], outs[2 * n + 2]
        x, y, c, chips = _place()
        sib = (x, y, 1 - c)
        for i, (shard, split) in enumerate(group):
            rows = _half(shard.shape[0], c) if split else slice(None)
            for k, (cx, cy) in enumerate(chips):
                landed = lnds[i].at[2 * cx + cy, rows]
                cp = _remote(landed, landed, send1.at[4 * i + k], recv1.at[4 * i + k], sib)
                cp.wait_send()
                cp.wait_recv()
            own = lnds[i].at[2 * x + y]
            cp = _remote(own, own, send1.at[4 * i + 3], recv1.at[4 * i + 3], sib)
            cp.wait_send()
            cp.wait_recv()
        for i2, i in enumerate(split_ix):
            rows = _half(group[i][0].shape[0], c)
            for k, (cx, cy) in enumerate(chips):
                landed = lnds[i].at[2 * cx + cy, rows]
                _remote(landed, landed, send2.at[3 * i2 + k], recv2.at[3 * i2 + k], sib).start()
        token[...] = jnp.zeros_like(token)

    n2 = len(split_ix)
    out_shape = ([pltpu.HBM(a.shape, a.dtype) for a in (*shards, *lands)]
                 + [pltpu.SemaphoreType.DMA((3 * n2,)), pltpu.SemaphoreType.DMA((3 * n2,)), jax.ShapeDtypeStruct((8, LANES), F32)])
    outs = pl.pallas_call(
        body, name=name, in_specs=[HBM] * (2 * n) + [SEM, SEM] + [ANY] * len(after),
        out_specs=[HBM] * (2 * n) + [SEM, SEM, pl.BlockSpec(memory_space=pltpu.VMEM)],
        out_shape=out_shape, input_output_aliases={i: i for i in range(2 * n)},
        compiler_params=pltpu.CompilerParams(has_side_effects=EFFECT),
    )(*shards, *lands, send, recv, *after)
    return outs[:n], (outs[n:2 * n], outs[2 * n], outs[2 * n + 1]), outs[2 * n + 2]


def _gather_wait(group, passed, after, name):
    lands, send2, recv2 = passed
    n = len(group)
    split_ix = [i for i, (_, split) in enumerate(group) if split]

    def body(*refs):
        lnds, s2, r2 = refs[:n], refs[n], refs[n + 1]
        x, y, c, chips = _place()
        sib = (x, y, 1 - c)
        for i2, i in enumerate(split_ix):
            rows = _half(group[i][0].shape[0], 1 - c)
            for k, (cx, cy) in enumerate(chips):
                landed = lnds[i].at[2 * cx + cy, rows]
                cp = _remote(landed, landed, s2.at[3 * i2 + k], r2.at[3 * i2 + k], sib)
                cp.wait_send()
                cp.wait_recv()

    return pl.pallas_call(
        body, name=name, in_specs=[HBM] * n + [SEM, SEM, ANY], out_specs=[HBM] * n,
        out_shape=[pltpu.HBM(a.shape, a.dtype) for a in lands], input_output_aliases={i: i for i in range(n)},
        compiler_params=pltpu.CompilerParams(has_side_effects=EFFECT),
    )(*lands, send2, recv2, after)


def _xfer_start(name, srcs, land_shapes, n_copies, copies, after):
    n, nl = len(srcs), len(land_shapes)

    def body(*refs):
        src_refs, land_refs = refs[:n], refs[n:n + nl]
        send, recv, token = refs[n + nl + 1], refs[n + nl + 2], refs[-1]
        for cp in copies(src_refs, land_refs, send, recv):
            cp.start()
        token[...] = jnp.zeros_like(token)

    lands = [_in_hbm(lax.empty(shape, dtype)) for shape, dtype in land_shapes]
    out_shape = ([pltpu.SemaphoreType.DMA((n_copies,)), pltpu.SemaphoreType.DMA((n_copies,))]
                 + [pltpu.HBM(a.shape, a.dtype) for a in (*srcs, *lands)] + [jax.ShapeDtypeStruct((8, LANES), F32)])
    outs = pl.pallas_call(
        body, name=name, in_specs=[HBM] * (n + nl) + [ANY],
        out_specs=[SEM, SEM] + [HBM] * (n + nl) + [pl.BlockSpec(memory_space=pltpu.VMEM)],
        out_shape=out_shape, input_output_aliases={i: 2 + i for i in range(n + nl)},
        compiler_params=pltpu.CompilerParams(has_side_effects=EFFECT),
    )(*[_in_hbm(a) for a in srcs], *lands, after)
    return (outs[2:2 + n], outs[2 + n:2 + n + nl], outs[0], outs[1]), outs[-1]


def _xfer_wait(name, started, copies, after):
    srcs, lands, send, recv = started
    n, nl = len(srcs), len(lands)

    def body(*refs):
        src_refs, land_refs, s_ref, r_ref = refs[:n], refs[n:n + nl], refs[n + nl], refs[n + nl + 1]
        for cp in copies(src_refs, land_refs, s_ref, r_ref):
            cp.wait_send()
            cp.wait_recv()

    outs = pl.pallas_call(
        body, name=name, in_specs=[HBM] * (n + nl) + [SEM, SEM, ANY], out_specs=[HBM] * (n + nl),
        out_shape=[pltpu.HBM(a.shape, a.dtype) for a in (*srcs, *lands)],
        input_output_aliases={i: i for i in range(n + nl)},
        compiler_params=pltpu.CompilerParams(has_side_effects=EFFECT),
    )(*srcs, *lands, send, recv, after)
    return outs[:n], outs[n:]


def _swap_copies(srcs, lands, send, recv):
    x, y, c, _ = _place()
    return [_remote(src.at[:, _half(src.shape[1], 1 - c)], land, send.at[i], recv.at[i], (x, y, 1 - c))
            for i, (src, land) in enumerate(zip(srcs, lands))]


def _scatter_copies(srcs, lands, send, recv):
    x, y, c, chips = _place()
    return [_remote(src.at[2 * cx + cy], land.at[k], send.at[3 * i + k], recv.at[3 * i + k], (cx, cy, c))
            for i, (src, land) in enumerate(zip(srcs, lands)) for k, (cx, cy) in enumerate(chips)]


def _join_copies(srcs, lands, send, recv):
    x, y, c, _ = _place()
    return [_remote(src.at[_half(src.shape[0], c)], src.at[_half(src.shape[0], c)], send.at[i], recv.at[i], (x, y, 1 - c))
            for i, src in enumerate(srcs)]


def _corner(a):
    return a[(slice(0, 1),) * a.ndim]


class _Reducer:
    def __init__(self, place):
        self.place = place
        self.state = {}

    def swap(self, key, grads, after):
        shapes = [((g.shape[0], g.shape[1] // 2, g.shape[2]), g.dtype) for g in grads]
        self.state[key], token = _xfer_start("swap_start_" + key, grads, shapes, len(grads), _swap_copies, _corner(after))
        return token

    def to_chips(self, key, after):
        grads, from_sibling = _xfer_wait("swap_wait_" + key, self.state[key], _swap_copies, after)
        sums = [_add_halves(g, r, self.place) for g, r in zip(grads, from_sibling)]
        shapes = [((3,) + s.shape[1:], s.dtype) for s in sums]
        started, token = _xfer_start("scatter_start_" + key, sums, shapes, 3 * len(sums), _scatter_copies, _corner(sums[-1]))
        self.state[key] = (grads, from_sibling, started)
        return token

    def to_core(self, key, after):
        grads, from_sibling, started = self.state[key]
        _, from_chips = _xfer_wait("scatter_wait_" + key, started, _scatter_copies, after)
        shards = [_sum_chips(g, r, rc, self.place) for g, r, rc in zip(grads, from_sibling, from_chips)]
        self.state[key], token = _xfer_start("join_start_" + key, shards, [], len(shards), _join_copies, _corner(shards[-1]))
        return token

    def finish(self, key, after):
        return _xfer_wait("join_wait_" + key, self.state.pop(key), _join_copies, after)[0]


def _gather_packs(pack, deps=()):
    def body(p_ref, *rest):
        o_ref, lsem, ssem, rsem = rest[-4:]
        x, y, c, _ = _place()
        me = 4 * x + 2 * y + c
        local = pltpu.make_async_copy(p_ref, o_ref.at[me], lsem)
        local.start()
        cps = []
        for k in range(1, N_DEV):
            fx, fy, fc = (k >> 2) & 1, (k >> 1) & 1, k & 1
            to = (x ^ fx, y ^ fy, c ^ fc)
            cps.append(_remote(p_ref, o_ref.at[me], ssem.at[k - 1], rsem.at[k - 1], to))
        for cp in cps:
            cp.start()
        for k in range(1, N_DEV):
            fx, fy, fc = (k >> 2) & 1, (k >> 1) & 1, k & 1
            src = o_ref.at[4 * (x ^ fx) + 2 * (y ^ fy) + (c ^ fc)]
            _remote(src, src, ssem.at[k - 1], rsem.at[k - 1], (x, y, c)).wait_recv()
        for cp in cps:
            cp.wait_send()
        local.wait()

    return pl.pallas_call(
        body, name="gather_packs", in_specs=[ANY] * (1 + len(deps)), out_specs=ANY,
        out_shape=jax.ShapeDtypeStruct((N_DEV,) + pack.shape, pack.dtype),
        scratch_shapes=[pltpu.SemaphoreType.DMA, pltpu.SemaphoreType.DMA((N_DEV - 1,)), pltpu.SemaphoreType.DMA((N_DEV - 1,))],
    )(pack, *deps)


LANE_TILES = (512, 896, 1408, 704, 384, 256, 128)


def _layer_grads(x, target, small, wg, rest_pass, rest_wait, red, filler):
    s, d = x.shape
    f = wg["conv"].shape[1] // 2
    w_att = N_HEADS * HEAD_DIM
    in_splits = (w_att, w_att, w_att, N_HEADS, w_att, w_att, w_att, d, d)
    in_cols = sum(in_splits)
    cs = in_cols // N_CHIPS
    cp = wg["in"].shape[2]
    tm = min(s, MM_TILE)
    tm_wide = min(s, MM_TILE // 2)
    t_in = cp
    t_up = 2 * f // N_CHIPS
    t_d = _pick(d, LANE_TILES)
    t_d2 = min(d, MM_TILE)
    t_dq = _pick(d // N_CHIPS, LANE_TILES)
    t_fq = _pick(f // N_CHIPS, LANE_TILES)

    h1 = _norm_fwd(x, small["g_attn"], group=d, name="rms1_fwd")
    proj_p = _mm(h1, wg["in"], mode="nn", b_kind="col", tm=tm_wide, tn=t_in, tk=d, name="mm_in")
    gains = {n: small[n].reshape(1, w_att) for n in ("g_q_fox", "g_k_fox", "g_q_dil", "g_k_dil")}
    qa, ka, va_b, fa, qb, kb, vb_b, ga, gb, qa_n, ka_n, qb_n, kb_n = _proj_split(
        proj_p, in_splits, cs, (F32, F32, BF16, F32, F32, F32, BF16, F32, F32),
        {0: gains["g_q_fox"], 1: gains["g_k_fox"], 4: gains["g_q_dil"], 5: gains["g_k_dil"]})
    fa_t = fa.T
    b_f = small["b_forget"].reshape(N_HEADS, 1)
    c_f = _forget_fwd(fa_t, b_f)
    slopes = jnp.asarray(2.0 ** (-8.0 * np.arange(1, N_HEADS + 1) / N_HEADS), dtype=F32)
    a_d = -(slopes[:, None] * jnp.arange(s, dtype=F32)[None, :])
    rows_f, cols_f = c_f[:, :, None], c_f[:, None, :]
    rows_d, cols_d = a_d[:, :, None], a_d[:, None, :]
    o_a, o_a32, lse_a = _attn_fwd(qa_n, ka_n, va_b, rows_f, cols_f, dilated=False, name="attn_fox_fwd")
    token = rest_pass("mid", o_a)
    rows_d = rows_d + token[0, 0]
    o_b, o_b32, lse_b = _attn_fwd(qb_n, kb_n, vb_b, rows_d, cols_d, dilated=True, name="attn_dil_fwd")
    wg = dict(wg, **rest_wait("mid", o_b))
    token = rest_pass("late", o_b)
    pa = _mm(o_a, wg["brf"], mode="nn", b_kind="col", tm=tm, tn=t_dq, tk=w_att, name="mm_brf", deps=(token,))
    pb, merged = _mm(o_b, wg["brd"], mode="nn", b_kind="col", tm=tm, tn=t_dq, tk=w_att, name="mm_brd", gate=(ga, gb, pa))
    x1 = _mm(merged, wg["out"], mode="nn", b_kind="row", res=x, tm=tm, tn=t_d, tk=t_dq, name="mm_out")
    wg = dict(wg, **rest_wait("late", x1))
    h2 = _norm_fwd(x1, small["g_ffn"], group=d, name="rms2_fwd")
    u = _mm(h2, wg["up"], mode="nn", b_kind="col", tm=tm_wide, tn=t_up, tk=d, name="mm_up")
    act = _conv_glu_fwd(u, wg["conv"], wg["bconv"])
    dy_f, dy_b, loss_blk = _mm(act, wg["down"], mode="nn", b_kind="row", res=x1, loss_target=target,
                               tm=tm, tn=t_d2, tk=t_fq, name="mm_down")

    d_act = _mm(dy_b, wg["down"], mode="nt", b_kind="row", tm=tm, tn=t_fq, tk=d, name="mm_down_dx")
    g_down = _mm(act, dy_b, mode="tn", out_dtype=BF16, out_kind="row", tm=t_fq, tn=t_d2, tk=s, name="mm_down_dw")
    tok = red.swap("down", [g_down], g_down)
    du_g, du_v, st_g, st_v = _conv_glu_bwd(u, d_act, wg["conv"] + tok[0, 0], wg["bconv"])
    tok = red.to_chips("down", du_g)
    du = (du_g, du_v)
    g_up = _mm(h2, du, mode="tn", out_dtype=BF16, out_kind="col", tm=t_d2, tn=t_up // 2, tk=s, name="mm_up_dw", deps=(tok,))
    tok = red.swap("up", [g_up], g_up)
    dh2 = _mm(du, wg["up"], mode="nt", b_kind="col", tm=tm, tn=t_d2, tk=t_up, name="mm_up_dx", deps=(tok,))
    tok = red.to_core("down", dh2)
    tok2 = red.to_chips("up", dh2)
    dx1_b, dx1_f, dg_ffn = _norm_bwd(dh2, x1, small["g_ffn"], group=d, res=dy_f, out_dtypes=(BF16, F32), name="rms2_bwd")
    d_merged = _mm(dx1_b, wg["out"], mode="nt", b_kind="row", tm=tm, tn=t_dq, tk=d, name="mm_out_dx", deps=(tok, tok2))
    g_out = _mm(merged, dx1_b, mode="tn", out_dtype=BF16, out_kind="row", tm=t_dq, tn=t_d2, tk=s, name="mm_out_dw")
    dpa, dpb, dga, dgb = _gate_bwd(d_merged, ga, gb, pa, pb)
    do_a = _mm(dpa, wg["brf"], mode="nt", b_kind="col", out_dtype=BF16, tm=s, tn=w_att, tk=t_dq, name="mm_brf_dx")
    do_b = _mm(dpb, wg["brd"], mode="nt", b_kind="col", out_dtype=BF16, tm=s, tn=w_att, tk=t_dq, name="mm_brd_dx")
    g_brf = _mm(o_a, dpa, mode="tn", out_dtype=BF16, out_kind="col", tm=w_att, tn=t_dq, tk=s, name="mm_brf_dw")
    g_brd = _mm(o_b, dpb, mode="tn", out_dtype=BF16, out_kind="col", tm=w_att, tn=t_dq, tk=s, name="mm_brd_dw")
    tok = red.swap("mix", [g_out, g_brf, g_brd], g_brd)
    dqa_n, dka_n, dva, dac_a = _attn_bwd(qa_n, ka_n, va_b, o_a32, do_a, lse_a, rows_f + tok[0, 0], cols_f, dilated=False, name="attn_fox_bwd")
    tok = red.to_core("up", dqa_n)
    tok2 = red.to_chips("mix", dqa_n)
    dqb_n, dkb_n, dvb, _ = _attn_bwd(qb_n, kb_n, vb_b, o_b32, do_b, lse_b, rows_d + (tok[0, 0] + tok2[0, 0]), cols_d, dilated=True, name="attn_dil_bwd")
    tok = red.to_core("mix", dqb_n)
    dfa_t, db_f = _forget_bwd(dac_a[:, 0, :], fa_t, b_f)
    dproj_p, dgains = _dproj_merge(
        [dqa_n, dka_n, dva, dfa_t.T, dqb_n, dkb_n, dvb, dga, dgb], in_splits, cs, cp,
        {0: (qa, gains["g_q_fox"]), 1: (ka, gains["g_k_fox"]), 4: (qb, gains["g_q_dil"]), 5: (kb, gains["g_k_dil"])})
    dg_qf, dg_kf, dg_qd, dg_kd = dgains[0], dgains[1], dgains[4], dgains[5]
    g_in = _mm(h1, dproj_p, mode="tn", out_dtype=BF16, out_kind="col", tm=t_d2, tn=t_in, tk=s, name="mm_in_dw", deps=(tok,))
    tok = red.swap("in", [g_in], g_in)
    tok = red.to_chips("in", filler(tok))
    dh1 = _mm(dproj_p, wg["in"], mode="nt", b_kind="col", tm=tm, tn=t_d2, tk=t_in, name="mm_in_dx", deps=(tok,))
    grad_x, dg_attn = _norm_bwd(dh1, x, small["g_attn"], group=d, res=dx1_f, out_dtypes=(F32,), name="rms1_bwd")

    small_grads = {
        "g_attn": dg_attn, "b_forget": db_f.reshape(1, N_HEADS),
        "g_q_fox": dg_qf, "g_k_fox": dg_kf, "g_q_dil": dg_qd, "g_k_dil": dg_kd, "g_ffn": dg_ffn,
        "w_conv": jnp.concatenate([st_g[0:3], st_v[0:3]], axis=1),
        "b_conv": jnp.concatenate([st_g[3:4], st_v[3:4]], axis=1),
        "loss": loss_blk[0:1, 0:1],
    }
    return small_grads, grad_x


SMALL_ORDER = ("g_attn", "b_forget", "g_q_fox", "g_k_fox", "g_q_dil", "g_k_dil", "g_ffn", "w_conv", "b_conv", "loss")
WEIGHT_ORDER = ("g_attn", "w_in", "b_forget", "g_q_fox", "g_k_fox", "g_q_dil", "g_k_dil", "w_br_fox", "w_br_dil",
                "w_out", "g_ffn", "w_up", "w_conv", "b_conv", "w_down")
BIG = {"w_in": "in", "w_br_fox": "brf", "w_br_dil": "brd", "w_out": "out", "w_up": "up", "w_down": "down"}


def kernel(x, g_attn, w_in, b_forget, g_q_fox, g_k_fox, g_q_dil, g_k_dil, w_br_fox, w_br_dil, w_out, g_ffn, w_up, w_conv, b_conv, w_down, loss_target, m_g_attn, m_w_in, m_b_forget, m_g_q_fox, m_g_k_fox, m_g_q_dil, m_g_k_dil, m_w_br_fox, m_w_br_dil, m_w_out, m_g_ffn, m_w_up, m_w_conv, m_b_conv, m_w_down, v_g_attn, v_w_in, v_b_forget, v_g_q_fox, v_g_k_fox, v_g_q_dil, v_g_k_dil, v_w_br_fox, v_w_br_dil, v_w_out, v_g_ffn, v_w_up, v_w_conv, v_b_conv, v_w_down):
    w = dict(g_attn=g_attn, w_in=w_in, b_forget=b_forget, g_q_fox=g_q_fox, g_k_fox=g_k_fox, g_q_dil=g_q_dil,
             g_k_dil=g_k_dil, w_br_fox=w_br_fox, w_br_dil=w_br_dil, w_out=w_out, g_ffn=g_ffn, w_up=w_up,
             w_conv=w_conv, b_conv=b_conv, w_down=w_down)
    m = dict(g_attn=m_g_attn, w_in=m_w_in, b_forget=m_b_forget, g_q_fox=m_g_q_fox, g_k_fox=m_g_k_fox,
             g_q_dil=m_g_q_dil, g_k_dil=m_g_k_dil, w_br_fox=m_w_br_fox, w_br_dil=m_w_br_dil, w_out=m_w_out,
             g_ffn=m_g_ffn, w_up=m_w_up, w_conv=m_w_conv, b_conv=m_b_conv, w_down=m_w_down)
    v = dict(g_attn=v_g_attn, w_in=v_w_in, b_forget=v_b_forget, g_q_fox=v_g_q_fox, g_k_fox=v_g_k_fox,
             g_q_dil=v_g_q_dil, g_k_dil=v_g_k_dil, w_br_fox=v_w_br_fox, w_br_dil=v_w_br_dil, w_out=v_w_out,
             g_ffn=v_g_ffn, w_up=v_w_up, w_conv=v_w_conv, b_conv=v_b_conv, w_down=v_w_down)
    xi, yi, ci = lax.axis_index("x"), lax.axis_index("y"), lax.axis_index("c")
    chip = (2 * xi + yi).astype(jnp.int32)

    cs = w_in.shape[2]
    cp = _round_up(cs, LANES)
    conv_pad = jnp.pad(w_conv[0], ((0, 8 - w_conv.shape[1]), (0, 0)))
    first = [(jnp.pad(w_in[0].astype(BF16), ((0, 0), (0, cp - cs))), True), (conv_pad, False)]
    (started_first,), token = _gather_start([first], "gather_start_in")
    one = 1.0 + token[0, 0]
    shards = {n: (a[0] * one).astype(BF16) for n, a in
              (("brf", w_br_fox), ("brd", w_br_dil), ("out", w_out), ("up", w_up), ("down", w_down))}
    later = {"mid": ("brf", "brd", "out"), "late": ("up", "down")}
    groups = {key: [(shards[n], True) for n in members] for key, members in later.items()}
    started_later, token = _gather_start(list(groups.values()), "gather_start_rest")
    started = dict(zip(later, started_later))
    token, w["w_in"], m["w_in"], v["w_in"] = lax.optimization_barrier((token, w["w_in"], m["w_in"], v["w_in"]))
    w2, m2, v2 = ({n: a[n].reshape(a[n].shape[-2], a[n].shape[-1]) for n in BIG} for a in (w, m, v))
    early = (token, w2["w_in"], m2["w_in"], v2["w_in"])
    own_first, passed_first, token = _gather_pass(first, started_first, early, "gather_pass_in")
    land_in, land_conv = _gather_wait(first, passed_first, token, "gather_wait_in")
    wg = {"in": land_in, "bconv": b_conv,
          "conv": jnp.transpose(land_conv[:, :w_conv.shape[1], :], (1, 0, 2)).reshape(w_conv.shape[1], -1)}
    small = {n: w[n] for n in ("g_attn", "b_forget", "g_q_fox", "g_k_fox", "g_q_dil", "g_k_dil", "g_ffn")}
    small = {n: (a[0] if a.ndim == 3 else a) for n, a in small.items()}
    in_flight = {}

    def rest_pass(key, after):
        own, passed, tok = _gather_pass(groups[key], started[key], (after,), "gather_pass_" + key)
        in_flight[key] = (own, passed)
        return tok

    def rest_wait(key, after):
        own, passed = in_flight.pop(key)
        lands = _gather_wait(groups[key], passed, after, "gather_wait_" + key)
        return dict(zip(later[key], lands))

    reducer = _Reducer(jnp.stack([chip, ci.astype(jnp.int32)]))
    g_out, d_out, m_out, v_out = {}, {}, {}, {}
    reduced = {}

    def first_element(arrays):
        return jnp.stack([a[(0,) * a.ndim] for a in arrays])

    def update_big(n, deps):
        g2, dl, mn, vn = _adamw(w2[n], reduced[BIG[n]], m2[n], v2[n], name="adamw_" + n, deps=deps, emit_grad=True)
        g_out[n], d_out[n], m_out[n], v_out[n] = (a.reshape(w[n].shape) for a in (g2, dl, mn, vn))

    def update_down(tok):
        (reduced["down"],) = reducer.finish("down", tok)
        update_big("w_down", (tok,))
        return v_out["w_down"]

    small_grads, grad_x = _layer_grads(x[0], loss_target[0], small, wg, rest_pass, rest_wait, reducer, update_down)

    for key, members in (("up", ("up",)), ("mix", ("out", "brf", "brd"))):
        reduced.update(zip(members, reducer.finish(key, grad_x)))
    others = ("w_up", "w_out", "w_br_fox", "w_br_dil")
    for n in others:
        update_big(n, (grad_x,))

    flat = jnp.concatenate([small_grads[n].reshape(-1) for n in SMALL_ORDER])
    rows = _round_up(flat.shape[0], 8 * LANES) // LANES
    pack = jnp.pad(flat, (0, rows * LANES - flat.shape[0])).reshape(rows, LANES)
    packs = _gather_packs(pack, deps=(first_element([v_out[n] for n in others]),))
    total = _sum_devices(packs).reshape(-1)
    red, at = {}, 0
    for n in SMALL_ORDER:
        size = small_grads[n].size
        red[n] = total[at:at + size].reshape(small_grads[n].shape)
        at += size
    loss = red["loss"].reshape(())
    c2 = w_conv.shape[2]
    red["w_conv"] = lax.dynamic_slice_in_dim(red["w_conv"], chip * c2, c2, axis=1)

    tok = reducer.to_core("in", packs)
    smalls = [n for n in WEIGHT_ORDER if n not in BIG]
    for n in smalls:
        shape = w[n].shape
        r2 = (shape[-2], shape[-1]) if n not in ("g_attn", "b_forget", "g_ffn", "b_conv") else (1, shape[-1])
        g2 = red[n].reshape(r2)
        dl, mn, vn = _adamw(w[n].reshape(r2), g2, m[n].reshape(r2), v[n].reshape(r2), name="adamw_" + n, deps=(tok,))
        g_out[n], d_out[n], m_out[n], v_out[n] = (a.reshape(shape) for a in (g2, dl, mn, vn))
    (reduced["in"],) = reducer.finish("in", first_element([v_out[n] for n in smalls]))
    update_big("w_in", (tok,))

    return (loss, grad_x[None], *[g_out[n] for n in WEIGHT_ORDER], *[d_out[n] for n in WEIGHT_ORDER],
            *[m_out[n] for n in WEIGHT_ORDER], *[v_out[n] for n in WEIGHT_ORDER])
```

```python
import math

import jax
import jax.numpy as jnp
import numpy as np
from jax import lax
from jax.experimental import pallas as pl
from jax.experimental.pallas import tpu as pltpu

F32 = jnp.float32
BF16 = jnp.bfloat16
HEAD_DIM = 128
N_HEADS = 8
EPS = 1e-6
NEG = -1e30
LOG2E = math.log2(math.e)
N_CHIPS = 4
N_DEV = 8
LANES = 128
VMEM_LIMIT_BYTES = 56 * 1024 * 1024
DIL_PATTERNS = ((128, 1), (512, 4), (2048, 16))
ATTN_TILE = 512
MM_TILE = 1024
ADAM_LR, ADAM_B1, ADAM_B2, ADAM_EPS, ADAM_WD, ADAM_STEP = 0.001, 0.9, 0.999, 1e-08, 0.01, 10
MESH = pl.DeviceIdType.MESH


def _params(*sem):
    return pltpu.CompilerParams(dimension_semantics=sem, vmem_limit_bytes=VMEM_LIMIT_BYTES)


def _round_up(n, m):
    return -(-n // m) * m


def _pick(dim, prefs):
    for p in prefs:
        if dim % p == 0:
            return p
    raise ValueError(f"no tile for {dim} in {prefs}")


def _logical_shape(arr, kind):
    if kind is None:
        return arr.shape
    s, r, c = arr.shape
    return (r, s * c) if kind == "col" else (s * r, c)


def _spec(shape, kind, br, bc, fi, fj):
    if kind is None:
        return pl.BlockSpec((br, bc), lambda *g: (fi(*g), fj(*g)))
    _, r, c = shape
    if kind == "col":
        nb = c // bc
        assert nb * bc == c, (shape, bc)
        return pl.BlockSpec((None, br, bc), lambda *g: (fj(*g) // nb, fi(*g), fj(*g) % nb))
    nb = r // br
    assert nb * br == r, (shape, br)
    return pl.BlockSpec((None, br, bc), lambda *g: (fi(*g) // nb, fi(*g) % nb, fj(*g)))


def _mm(a, b, *, mode, tm, tn, tk, name, a_kind=None, b_kind=None, out_kind=None,
        out_dtype=F32, res=None, deps=(), loss_target=None, gate=None):
    pair_a, pair_b = isinstance(a, tuple), isinstance(b, tuple)
    if pair_a or pair_b:
        return _mm_pair(a, b, mode=mode, tm=tm, tn=tn, tk=tk, name=name, b_kind=b_kind, out_kind=out_kind,
                        out_dtype=out_dtype, deps=deps)
    la, lb = _logical_shape(a, a_kind), _logical_shape(b, b_kind)
    if mode == "nn":
        (m, k), (k2, n) = la, lb
    elif mode == "nt":
        (m, k), (n, k2) = la, lb
    else:
        (k, m), (k2, n) = la, lb
    assert k == k2, (name, la, lb)
    assert m % tm == 0 and n % tn == 0 and k % tk == 0, (name, m, n, k, tm, tn, tk)
    nk = k // tk
    im = lambda i, j, l: i
    jn = lambda i, j, l: j
    lk = lambda i, j, l: l
    if mode == "tn":
        a_spec = _spec(a.shape, a_kind, tk, tm, lk, im)
        dims = (((0,), (0,)), ((), ()))
    else:
        a_spec = _spec(a.shape, a_kind, tm, tk, im, lk)
        dims = (((1,), (1,)), ((), ())) if mode == "nt" else (((1,), (0,)), ((), ()))
    if mode == "nt":
        b_spec = _spec(b.shape, b_kind, tn, tk, jn, lk)
    else:
        b_spec = _spec(b.shape, b_kind, tk, tn, lk, jn)
    if out_kind is None:
        oshape = (m, n)
    elif out_kind == "col":
        oshape = (N_CHIPS, m, n // N_CHIPS)
    else:
        oshape = (N_CHIPS, m // N_CHIPS, n)
    o_spec = _spec(oshape, out_kind, tm, tn, im, jn)
    tile = pl.BlockSpec((tm, tn), lambda i, j, l: (i, j))
    in_specs = [a_spec, b_spec]
    args = [a, b]
    for extra in (res, loss_target, *(gate or ())):
        if extra is not None:
            in_specs.append(tile)
            args.append(extra)
    in_specs += [pl.BlockSpec(memory_space=pl.ANY)] * len(deps)
    args += list(deps)
    if gate is not None and len(gate) == 4:
        assert out_kind is None and res is None and loss_target is None
        out_specs, out_shape = [tile] * 4, [jax.ShapeDtypeStruct(oshape, BF16)] * 4
    elif gate is not None:
        assert out_kind is None and res is None and loss_target is None
        out_specs, out_shape = [tile, tile], [jax.ShapeDtypeStruct(oshape, F32), jax.ShapeDtypeStruct(oshape, BF16)]
    elif loss_target is None:
        out_specs, out_shape = [o_spec], [jax.ShapeDtypeStruct(oshape, out_dtype)]
    else:
        assert out_kind is None and res is not None
        out_specs = [tile, tile, pl.BlockSpec((8, LANES), lambda i, j, l: (0, 0))]
        out_shape = [jax.ShapeDtypeStruct(oshape, F32), jax.ShapeDtypeStruct(oshape, BF16),
                     jax.ShapeDtypeStruct((8, LANES), F32)]
    n_in, n_out = len(args), len(out_specs)

    def finish(out, refs, first):
        res_ref = refs[2] if res is not None else None
        outs = refs[n_in:n_in + n_out]
        if res_ref is not None:
            out = out + res_ref[...]
        if gate is not None and len(gate) == 4:
            for g_ref, p_ref, dp_ref, dg_ref in ((refs[2], refs[4], outs[0], outs[2]), (refs[3], refs[5], outs[1], outs[3])):
                sg = jax.nn.sigmoid(g_ref[...])
                dp_ref[...] = (out * sg).astype(BF16)
                dg_ref[...] = (out * p_ref[...] * (sg * (1.0 - sg))).astype(BF16)
            return
        if gate is not None:
            ga_ref, gb_ref, pa_ref = refs[2:5]
            outs[0][...] = out
            outs[1][...] = (jax.nn.sigmoid(ga_ref[...]) * pa_ref[...] + jax.nn.sigmoid(gb_ref[...]) * out).astype(BF16)
            return
        if loss_target is None:
            outs[0][...] = out.astype(outs[0].dtype)
            return

        @pl.when(first)
        def _():
            outs[2][...] = jnp.zeros_like(outs[2])

        err = out - refs[3][...]
        dy = err * (1.0 / n)
        outs[0][...] = dy
        outs[1][...] = dy.astype(BF16)
        outs[2][...] += 0.5 * jnp.sum(jnp.sum(err * err, axis=-1, keepdims=True) * (1.0 / n), axis=0, keepdims=True)

    def first_tile():
        return (pl.program_id(0) == 0) & (pl.program_id(1) == 0)

    def body_whole_k(*refs):
        finish(lax.dot_general(refs[0][...], refs[1][...], dims, preferred_element_type=F32), refs, first_tile())

    def body(*refs):
        acc_ref = refs[-1]
        step = pl.program_id(2)
        first = first_tile()

        @pl.when(step == 0)
        def _():
            acc_ref[...] = jnp.zeros_like(acc_ref)

        acc_ref[...] += lax.dot_general(refs[0][...], refs[1][...], dims, preferred_element_type=F32)

        @pl.when(step == nk - 1)
        def _():
            finish(acc_ref[...], refs, first)

    outs = pl.pallas_call(
        body_whole_k if nk == 1 else body, name=name, grid=(m // tm, n // tn, nk),
        in_specs=in_specs, out_specs=out_specs, out_shape=out_shape,
        scratch_shapes=[] if nk == 1 else [pltpu.VMEM((tm, tn), F32)],
        compiler_params=_params(*(["arbitrary"] * 3 if loss_target is not None else ["parallel", "parallel", "arbitrary"])),
    )(*args)
    return outs[0] if loss_target is None and gate is None else outs


def _mm_pair(a, b, *, mode, tm, tn, tk, name, b_kind, out_kind, out_dtype, deps):
    anyspec = [pl.BlockSpec(memory_space=pl.ANY)] * len(deps)
    if mode == "tn":
        assert isinstance(b, tuple) and out_kind == "col" and a.shape[0] == tk
        k, m = a.shape
        n0 = b[0].shape[1]
        n, nb0 = 2 * n0, n0 // tn
        oshape = (N_CHIPS, m, n // N_CHIPS)

        def body(a_ref, b0_ref, b1_ref, *rest):
            o_ref = rest[-1]
            for first, b_ref in ((True, b0_ref), (False, b1_ref)):
                @pl.when((pl.program_id(1) < nb0) == first)
                def _():
                    o_ref[...] = lax.dot_general(a_ref[...], b_ref[...], (((0,), (0,)), ((), ())),
                                                 preferred_element_type=F32).astype(o_ref.dtype)

        return pl.pallas_call(
            body, name=name, grid=(m // tm, n // tn),
            in_specs=[pl.BlockSpec((tk, tm), lambda i, j: (0, i)),
                      pl.BlockSpec((tk, tn), lambda i, j: (0, jnp.minimum(j, nb0 - 1))),
                      pl.BlockSpec((tk, tn), lambda i, j: (0, jnp.maximum(j - nb0, 0)))] + anyspec,
            out_specs=_spec(oshape, "col", tm, tn, lambda i, j: i, lambda i, j: j),
            out_shape=jax.ShapeDtypeStruct(oshape, out_dtype), compiler_params=_params("parallel", "arbitrary"),
        )(a, *b, *deps)
    assert mode == "nt" and isinstance(a, tuple) and out_kind is None
    m, k0 = a[0].shape
    n = _logical_shape(b, b_kind)[0]
    nk0 = k0 // tk
    nk = 2 * nk0

    def body(a0_ref, a1_ref, b_ref, *rest):
        o_ref, acc_ref = rest[-2], rest[-1]
        step = pl.program_id(2)

        @pl.when(step == 0)
        def _():
            acc_ref[...] = jnp.zeros_like(acc_ref)

        for first, a_ref in ((True, a0_ref), (False, a1_ref)):
            @pl.when((step < nk0) == first)
            def _():
                acc_ref[...] += lax.dot_general(a_ref[...], b_ref[...], (((1,), (1,)), ((), ())), preferred_element_type=F32)

        @pl.when(step == nk - 1)
        def _():
            o_ref[...] = acc_ref[...].astype(o_ref.dtype)

    return pl.pallas_call(
        body, name=name, grid=(m // tm, n // tn, nk),
        in_specs=[pl.BlockSpec((tm, tk), lambda i, j, l: (i, jnp.minimum(l, nk0 - 1))),
                  pl.BlockSpec((tm, tk), lambda i, j, l: (i, jnp.maximum(l - nk0, 0))),
                  _spec(b.shape, b_kind, tn, tk, lambda i, j, l: j, lambda i, j, l: l)] + anyspec,
        out_specs=pl.BlockSpec((tm, tn), lambda i, j, l: (i, j)),
        out_shape=jax.ShapeDtypeStruct((m, n), out_dtype), scratch_shapes=[pltpu.VMEM((tm, tn), F32)],
        compiler_params=_params("parallel", "parallel", "arbitrary"),
    )(*a, b, *deps)


def _pieces(splits, cs, cp):
    out, g0 = [], 0
    for width in splits:
        g1, runs = g0 + width, []
        for j in range(N_CHIPS):
            a, b = max(g0, cs * j), min(g1, cs * (j + 1))
            if a < b:
                runs.append((j * cp + a - cs * j, a - g0, b - a))
        out.append(runs)
        g0 = g1
    return out


def _head_norm(xv, gv):
    r = lax.rsqrt(jnp.mean(xv * xv, axis=-1, keepdims=True) + EPS)
    return (xv * r) * gv


def _head_norm_bwd(dyv, xv, gv):
    r = lax.rsqrt(jnp.mean(xv * xv, axis=-1, keepdims=True) + EPS)
    xr = xv * r
    gdy = dyv * gv
    return r * (gdy - xr * jnp.mean(gdy * xr, axis=-1, keepdims=True)), jnp.sum(dyv * xr, axis=0, keepdims=True)


def _proj_split(proj_p, splits, cs, dtypes, gains, tm=128):
    s, wp = proj_p.shape
    pieces = _pieces(splits, cs, wp // N_CHIPS)
    normed = sorted(gains)
    nseg = len(splits)

    def body(p_ref, *refs):
        g_refs, o_refs, n_refs = refs[:len(normed)], refs[len(normed):len(normed) + nseg], refs[len(normed) + nseg:]
        for o_ref, runs in zip(o_refs, pieces):
            for src, dst, n in runs:
                o_ref[:, dst:dst + n] = p_ref[:, src:src + n].astype(o_ref.dtype)
        for g_ref, n_ref, i in zip(g_refs, n_refs, normed):
            for c0 in range(0, splits[i], HEAD_DIM):
                cols = slice(c0, c0 + HEAD_DIM)
                n_ref[:, cols] = _head_norm(o_refs[i][:, cols], g_ref[:, cols]).astype(n_ref.dtype)

    return pl.pallas_call(
        body, name="proj_split", grid=(s // tm,),
        in_specs=[pl.BlockSpec((tm, wp), lambda i: (i, 0))] + [pl.BlockSpec((1, splits[i]), lambda i: (0, 0)) for i in normed],
        out_specs=[pl.BlockSpec((tm, w), lambda i: (i, 0)) for w in splits]
        + [pl.BlockSpec((tm, splits[i]), lambda i: (i, 0)) for i in normed],
        out_shape=[jax.ShapeDtypeStruct((s, w), dt) for w, dt in zip(splits, dtypes)]
        + [jax.ShapeDtypeStruct((s, splits[i]), BF16) for i in normed],
        compiler_params=_params("parallel"),
    )(proj_p, *[gains[i] for i in normed])


def _dproj_merge(parts, splits, cs, cp, norms, tm=128):
    s = parts[0].shape[0]
    wp = N_CHIPS * cp
    pieces = _pieces(splits, cs, cp)
    normed = sorted(norms)
    nseg, nn = len(splits), len(normed)

    def body(*refs):
        p_refs, x_refs, g_refs = refs[:nseg], refs[nseg:nseg + nn], refs[nseg + nn:nseg + 2 * nn]
        o_ref, dg_refs = refs[nseg + 2 * nn], refs[nseg + 2 * nn + 1:nseg + 3 * nn + 1]
        stage, tmp = refs[-2], refs[-1]

        @pl.when(pl.program_id(0) == 0)
        def _():
            for dg_ref in dg_refs:
                dg_ref[...] = jnp.zeros_like(dg_ref)

        for j in range(N_CHIPS):
            stage[:, j * cp + cs:(j + 1) * cp] = jnp.zeros((tm, cp - cs), F32)
        for i, (p_ref, runs) in enumerate(zip(p_refs, pieces)):
            src_ref = p_ref
            if i in norms:
                k = normed.index(i)
                for c0 in range(0, splits[i], HEAD_DIM):
                    cols = slice(c0, c0 + HEAD_DIM)
                    dx, dg = _head_norm_bwd(p_ref[:, cols].astype(F32), x_refs[k][:, cols], g_refs[k][:, cols])
                    tmp[:, cols] = dx
                    dg_refs[k][:, cols] += dg
                src_ref = tmp
            for dst, src, n in runs:
                stage[:, dst:dst + n] = src_ref[:, src:src + n].astype(F32)
        o_ref[...] = stage[...].astype(o_ref.dtype)

    wmax = max(splits[i] for i in normed)
    row = lambda w: pl.BlockSpec((tm, w), lambda i: (i, 0))
    vec = lambda w: pl.BlockSpec((1, w), lambda i: (0, 0))
    outs = pl.pallas_call(
        body, name="dproj_merge", grid=(s // tm,),
        in_specs=[row(w) for w in splits] + [row(splits[i]) for i in normed] + [vec(splits[i]) for i in normed],
        out_specs=[row(wp)] + [vec(splits[i]) for i in normed],
        out_shape=[jax.ShapeDtypeStruct((s, wp), BF16)] + [jax.ShapeDtypeStruct((1, splits[i]), F32) for i in normed],
        scratch_shapes=[pltpu.VMEM((tm, wp), F32), pltpu.VMEM((tm, wmax), F32)],
        compiler_params=_params("arbitrary"),
    )(*parts, *[norms[i][0] for i in normed], *[norms[i][1] for i in normed])
    return outs[0], dict(zip(normed, outs[1:]))


def _norm_fwd(x, g, *, group, name, tm=256):
    s, w = x.shape
    ng = w // group

    def body(x_ref, g_ref, o_ref):
        for i in range(ng):
            cols = slice(i * group, (i + 1) * group)
            xv = x_ref[:, cols]
            r = lax.rsqrt(jnp.mean(xv * xv, axis=-1, keepdims=True) + EPS)
            o_ref[:, cols] = ((xv * r) * g_ref[:, cols]).astype(o_ref.dtype)

    return pl.pallas_call(
        body, name=name, grid=(s // tm,),
        in_specs=[pl.BlockSpec((tm, w), lambda i: (i, 0)), pl.BlockSpec((1, w), lambda i: (0, 0))],
        out_specs=pl.BlockSpec((tm, w), lambda i: (i, 0)),
        out_shape=jax.ShapeDtypeStruct((s, w), BF16),
        compiler_params=_params("parallel"),
    )(x, g)


def _norm_bwd(dy, x, g, *, group, name, res=None, out_dtypes=(BF16,), tm=256, deps=()):
    s, w = x.shape
    ng = w // group
    n_in = 4 if res is not None else 3

    def body(*refs):
        dy_ref, x_ref, g_ref = refs[:3]
        res_ref = refs[3] if res is not None else None
        outs = refs[n_in + len(deps):]
        dx_refs, dg_ref = outs[:-1], outs[-1]

        @pl.when(pl.program_id(0) == 0)
        def _():
            dg_ref[...] = jnp.zeros_like(dg_ref)

        for i in range(ng):
            cols = slice(i * group, (i + 1) * group)
            xv = x_ref[:, cols]
            dyv = dy_ref[:, cols].astype(F32)
            r = lax.rsqrt(jnp.mean(xv * xv, axis=-1, keepdims=True) + EPS)
            xr = xv * r
            dg_ref[:, cols] += jnp.sum(dyv * xr, axis=0, keepdims=True)
            gdy = dyv * g_ref[:, cols]
            dx = r * (gdy - xr * jnp.mean(gdy * xr, axis=-1, keepdims=True))
            if res_ref is not None:
                dx = dx + res_ref[:, cols]
            for dx_ref in dx_refs:
                dx_ref[:, cols] = dx.astype(dx_ref.dtype)

    row = pl.BlockSpec((tm, w), lambda i: (i, 0))
    vec = pl.BlockSpec((1, w), lambda i: (0, 0))
    in_specs = [row, row, vec] + ([row] if res is not None else []) + [pl.BlockSpec(memory_space=pl.ANY)] * len(deps)
    args = [dy, x, g] + ([res] if res is not None else []) + list(deps)
    out_specs = [row] * len(out_dtypes) + [vec]
    out_shape = [jax.ShapeDtypeStruct((s, w), dt) for dt in out_dtypes] + [jax.ShapeDtypeStruct((1, w), F32)]
    return pl.pallas_call(
        body, name=name, grid=(s // tm,), in_specs=in_specs, out_specs=out_specs,
        out_shape=out_shape, compiler_params=_params("arbitrary"),
    )(*args)


def _split3(v):
    p1 = v.astype(BF16)
    r1 = v - p1.astype(F32)
    p2 = r1.astype(BF16)
    p3 = (r1 - p2.astype(F32)).astype(BF16)
    return p1, p2, p3


def _tri_sum(v, reverse, tcol=512):
    h, s = v.shape
    tcol = min(tcol, s)
    parts = _split3(v)
    outs = []
    for j in range(s // tcol):
        src = lax.broadcasted_iota(jnp.int32, (s, tcol), 0)
        dst = lax.broadcasted_iota(jnp.int32, (s, tcol), 1) + j * tcol
        keep = (src >= dst) if reverse else (src <= dst)
        tri = jnp.where(keep, 1.0, 0.0).astype(BF16)
        acc = jnp.zeros((h, tcol), F32)
        for p in parts:
            acc = acc + jnp.dot(p, tri, preferred_element_type=F32)
        outs.append(acc)
    return outs


def _forget_fwd(fa_t, b):
    h, s = fa_t.shape
    tcol = min(512, s)

    def body(f_ref, b_ref, c_ref):
        z = f_ref[...] + b_ref[...]
        logf = jnp.minimum(z, 0.0) - jnp.log(1.0 + jnp.exp(-jnp.abs(z)))
        for j, blk in enumerate(_tri_sum(logf, reverse=False, tcol=tcol)):
            c_ref[:, j * tcol:(j + 1) * tcol] = blk

    return pl.pallas_call(
        body, name="forget_fwd", out_shape=jax.ShapeDtypeStruct((h, s), F32),
        compiler_params=_params(),
    )(fa_t, b)


def _forget_bwd(dacol, fa_t, b):
    h, s = fa_t.shape
    tcol = min(512, s)

    def body(d_ref, f_ref, b_ref, dfa_ref, db_ref):
        z = f_ref[...] + b_ref[...]
        dc = -d_ref[...]
        total = jnp.zeros((h, 1), F32)
        for j, blk in enumerate(_tri_sum(dc, reverse=True, tcol=tcol)):
            cols = slice(j * tcol, (j + 1) * tcol)
            dfa = blk * (1.0 - jax.nn.sigmoid(z[:, cols]))
            dfa_ref[:, cols] = dfa
            total = total + jnp.sum(dfa, axis=-1, keepdims=True)
        db_ref[...] = total

    return pl.pallas_call(
        body, name="forget_bwd",
        out_shape=[jax.ShapeDtypeStruct((h, s), F32), jax.ShapeDtypeStruct((h, 1), F32)],
        compiler_params=_params(),
    )(dacol, fa_t, b)


def _distance_bias(s, tile, dilated):
    nb = s // tile
    b = lax.broadcasted_iota(jnp.int32, (nb, tile, tile), 0)
    dist = b * tile + lax.broadcasted_iota(jnp.int32, (nb, tile, tile), 1) - lax.broadcasted_iota(jnp.int32, (nb, tile, tile), 2)
    if not dilated:
        return jnp.where(dist >= 0, 0.0, NEG).astype(F32)
    mult = jnp.zeros(dist.shape, jnp.int32)
    for window, dil in DIL_PATTERNS:
        mult = mult + ((dist >= 0) & (dist <= window) & ((dist & (dil - 1)) == 0)).astype(jnp.int32)
    logm = jnp.where(mult == 3, math.log2(3.0), jnp.where(mult == 2, 1.0, 0.0))
    return jnp.where(mult > 0, logm, NEG).astype(F32)


def _logits(q, k, arow, acol, bias):
    s = lax.dot_general(q, k, (((1,), (1,)), ((), ())), preferred_element_type=F32)
    return s * (LOG2E / math.sqrt(HEAD_DIM)) + arow - acol + bias


def _attn_fwd(q, k, v, arow, acol, *, dilated, name, tq=ATTN_TILE, tk=ATTN_TILE):
    two_term = not dilated
    s, w = q.shape
    nh = w // HEAD_DIM
    assert tq == tk
    tq = tk = min(tq, s)
    nq, nk = s // tq, s // tk

    pairs = [(i, j) for i in range(nq) for j in range(i + 1)]
    q_of, k_of = (jnp.asarray(t, jnp.int32) for t in zip(*pairs))

    def body(qo_ref, ko_ref, q_ref, k_ref, v_ref, ar_ref, ac_ref, b_ref, o_ref, of_ref, lse_ref, m_ref, l_ref, acc_ref):
        t = pl.program_id(1)
        qi, ki = qo_ref[t], ko_ref[t]

        @pl.when(ki == 0)
        def _():
            m_ref[...] = jnp.full_like(m_ref, NEG)
            l_ref[...] = jnp.zeros_like(l_ref)
            acc_ref[...] = jnp.zeros_like(acc_ref)

        sc = _logits(q_ref[...], k_ref[...], ar_ref[...], ac_ref[...], b_ref[...])
        m_new = jnp.maximum(m_ref[...], jnp.max(sc, axis=-1, keepdims=True))
        alpha = jnp.exp2(m_ref[...] - m_new)
        p = jnp.exp2(sc - m_new)
        l_ref[...] = alpha * l_ref[...] + jnp.sum(p, axis=-1, keepdims=True)
        p_hi = p.astype(BF16)
        vv = v_ref[...]
        pv = jnp.dot(p_hi, vv, preferred_element_type=F32)
        if two_term:
            pv = pv + jnp.dot((p - p_hi.astype(F32)).astype(BF16), vv, preferred_element_type=F32)
        acc_ref[...] = alpha * acc_ref[...] + pv
        m_ref[...] = m_new

        @pl.when(ki == qi)
        def _():
            out = acc_ref[...] / l_ref[...]
            o_ref[...] = out.astype(o_ref.dtype)
            of_ref[...] = out
            lse_ref[...] = m_ref[...] + jnp.log2(l_ref[...])

    qs = pl.BlockSpec((tq, HEAD_DIM), lambda h, t, qo, ko: (qo[t], h))
    kv = pl.BlockSpec((tk, HEAD_DIM), lambda h, t, qo, ko: (ko[t], h))
    rowv = pl.BlockSpec((None, tq, 1), lambda h, t, qo, ko: (h, qo[t], 0))
    return pl.pallas_call(
        body, name=name,
        grid_spec=pltpu.PrefetchScalarGridSpec(
            num_scalar_prefetch=2, grid=(nh, len(pairs)),
            in_specs=[qs, kv, kv, rowv,
                      pl.BlockSpec((None, 1, tk), lambda h, t, qo, ko: (h, 0, ko[t])),
                      pl.BlockSpec((None, tq, tk), lambda h, t, qo, ko: (qo[t] - ko[t], 0, 0))],
            out_specs=[qs, qs, rowv],
            scratch_shapes=[pltpu.VMEM((tq, 1), F32), pltpu.VMEM((tq, 1), F32), pltpu.VMEM((tq, HEAD_DIM), F32)]),
        out_shape=[jax.ShapeDtypeStruct((s, w), BF16), jax.ShapeDtypeStruct((s, w), F32),
                   jax.ShapeDtypeStruct((nh, s, 1), F32)],
        compiler_params=_params("parallel", "arbitrary"),
    )(q_of, k_of, q, k, v, arow * LOG2E, acol * LOG2E, _distance_bias(s, tq, dilated))


def _attn_bwd(q, k, v, o, do, lse, arow, acol, *, dilated, name, tq=ATTN_TILE, tk=ATTN_TILE):
    s, w = q.shape
    nh = w // HEAD_DIM
    assert tq == tk
    tq = tk = min(tq, s)
    nq, nk = s // tq, s // tk
    scale = 1.0 / math.sqrt(HEAD_DIM)

    pairs = [(i, j) for j in range(nk) for i in range(j, nq)]
    q_of, k_of = (jnp.asarray(t, jnp.int32) for t in zip(*pairs))

    def body(qo_ref, ko_ref, q_ref, k_ref, v_ref, o_ref, do_ref, lse_ref, ar_ref, ac_ref, b_ref,
             dq_ref, dk_ref, dv_ref, dac_ref, dk_acc, dv_acc, dac_acc):
        t = pl.program_id(1)
        qi, ki = qo_ref[t], ko_ref[t]

        @pl.when(t == 0)
        def _():
            dq_ref[...] = jnp.zeros_like(dq_ref)

        @pl.when(qi == ki)
        def _():
            dk_acc[...] = jnp.zeros_like(dk_acc)
            dv_acc[...] = jnp.zeros_like(dv_acc)
            dac_acc[...] = jnp.zeros_like(dac_acc)

        qv, kvv, dov = q_ref[...], k_ref[...], do_ref[...]
        sc = _logits(qv, kvv, ar_ref[...], ac_ref[...], b_ref[...])
        p = jnp.exp2(sc - lse_ref[...])
        dp = lax.dot_general(dov, v_ref[...], (((1,), (1,)), ((), ())), preferred_element_type=F32)
        delta = jnp.sum(dov.astype(F32) * o_ref[...].astype(F32), axis=-1, keepdims=True)
        ds = p * (dp - delta)
        dsb = ds.astype(BF16)
        dv_acc[...] += lax.dot_general(p.astype(BF16), dov, (((0,), (0,)), ((), ())), preferred_element_type=F32)
        dk_acc[...] += lax.dot_general(dsb, qv, (((0,), (0,)), ((), ())), preferred_element_type=F32)
        rows = pl.ds(pl.multiple_of(qi * tq, tq), tq)
        dq_ref[rows, :] += jnp.dot(dsb, kvv, preferred_element_type=F32) * scale
        dac_acc[...] += jnp.sum(ds, axis=0, keepdims=True)

        @pl.when(qi == nq - 1)
        def _():
            dk_ref[...] = dk_acc[...] * scale
            dv_ref[...] = dv_acc[...]
            dac_ref[...] = dac_acc[...]

    qs = pl.BlockSpec((tq, HEAD_DIM), lambda h, t, qo, ko: (qo[t], h))
    ks = pl.BlockSpec((tk, HEAD_DIM), lambda h, t, qo, ko: (ko[t], h))
    rowv = pl.BlockSpec((None, tq, 1), lambda h, t, qo, ko: (h, qo[t], 0))
    colv = pl.BlockSpec((None, 1, tk), lambda h, t, qo, ko: (h, 0, ko[t]))
    return pl.pallas_call(
        body, name=name,
        grid_spec=pltpu.PrefetchScalarGridSpec(
            num_scalar_prefetch=2, grid=(nh, len(pairs)),
            in_specs=[qs, ks, ks, qs, qs, rowv, rowv, colv,
                      pl.BlockSpec((None, tq, tk), lambda h, t, qo, ko: (qo[t] - ko[t], 0, 0))],
            out_specs=[pl.BlockSpec((s, HEAD_DIM), lambda h, t, qo, ko: (0, h)), ks, ks, colv],
            scratch_shapes=[pltpu.VMEM((tk, HEAD_DIM), F32), pltpu.VMEM((tk, HEAD_DIM), F32), pltpu.VMEM((1, tk), F32)]),
        out_shape=[jax.ShapeDtypeStruct((s, w), F32), jax.ShapeDtypeStruct((s, w), F32),
                   jax.ShapeDtypeStruct((s, w), F32), jax.ShapeDtypeStruct((nh, 1, s), F32)],
        compiler_params=_params("arbitrary", "arbitrary"),
    )(q_of, k_of, q, k, v, o, do, lse, arow * LOG2E, acol * LOG2E, _distance_bias(s, tq, dilated))


def _gate_fwd(ga, gb, pa, pb, tm=256):
    s, d = ga.shape

    def body(ga_ref, gb_ref, pa_ref, pb_ref, o_ref):
        o_ref[...] = (jax.nn.sigmoid(ga_ref[...]) * pa_ref[...]
                      + jax.nn.sigmoid(gb_ref[...]) * pb_ref[...]).astype(o_ref.dtype)

    row = pl.BlockSpec((tm, d), lambda i: (i, 0))
    return pl.pallas_call(
        body, name="gate_fwd", grid=(s // tm,), in_specs=[row] * 4, out_specs=row,
        out_shape=jax.ShapeDtypeStruct((s, d), BF16), compiler_params=_params("parallel"),
    )(ga, gb, pa, pb)


def _gate_bwd(dm, ga, gb, pa, pb, tm=256):
    s, d = ga.shape

    def body(dm_ref, ga_ref, gb_ref, pa_ref, pb_ref, dpa_ref, dpb_ref, dga_ref, dgb_ref):
        dmv = dm_ref[...]
        for g_ref, p_ref, dp_ref, dg_ref in ((ga_ref, pa_ref, dpa_ref, dga_ref), (gb_ref, pb_ref, dpb_ref, dgb_ref)):
            sg = jax.nn.sigmoid(g_ref[...])
            dp_ref[...] = (dmv * sg).astype(BF16)
            dg_ref[...] = (dmv * p_ref[...] * (sg * (1.0 - sg))).astype(BF16)

    row = pl.BlockSpec((tm, d), lambda i: (i, 0))
    return pl.pallas_call(
        body, name="gate_bwd", grid=(s // tm,), in_specs=[row] * 5, out_specs=[row] * 4,
        out_shape=[jax.ShapeDtypeStruct((s, d), BF16)] * 4, compiler_params=_params("parallel"),
    )(dm, ga, gb, pa, pb)


def _shift_down(u, k):
    row = lax.broadcasted_iota(jnp.int32, u.shape, 0)
    return jnp.where(row >= k, pltpu.roll(u, k, 0), 0.0)


def _shift_up(u, k):
    n = u.shape[0]
    row = lax.broadcasted_iota(jnp.int32, u.shape, 0)
    return jnp.where(row < n - k, pltpu.roll(u, n - k, 0), 0.0)


def _conv3(u, wc, b):
    return wc[0:1, :] * _shift_down(u, 2) + wc[1:2, :] * _shift_down(u, 1) + wc[2:3, :] * u + b


def _conv_glu_fwd(u, wc, b, tn=256):
    s, f2 = u.shape
    f = f2 // 2
    nb = f // tn

    def body(ug_ref, uv_ref, wg_ref, wv_ref, bg_ref, bv_ref, o_ref):
        cg = _conv3(ug_ref[...], wg_ref[...], bg_ref[...])
        cv = _conv3(uv_ref[...], wv_ref[...], bv_ref[...])
        o_ref[...] = (cg * jax.nn.sigmoid(cg) * cv).astype(o_ref.dtype)

    def cols(rows, off):
        return pl.BlockSpec((rows, tn), lambda j: (0, j + off))

    return pl.pallas_call(
        body, name="conv_glu_fwd", grid=(nb,),
        in_specs=[cols(s, 0), cols(s, nb), cols(3, 0), cols(3, nb), cols(1, 0), cols(1, nb)],
        out_specs=cols(s, 0), out_shape=jax.ShapeDtypeStruct((s, f), BF16),
        compiler_params=_params("parallel"),
    )(u, u, wc, wc, b, b)


def _conv_glu_bwd(u, da, wc, b, tn=256):
    s, f2 = u.shape
    f = f2 // 2
    nb = f // tn

    def body(ug_ref, uv_ref, da_ref, wg_ref, wv_ref, bg_ref, bv_ref, dug_ref, duv_ref, sg_ref, sv_ref):
        ug, uv, wg, wv = ug_ref[...], uv_ref[...], wg_ref[...], wv_ref[...]
        cg = _conv3(ug, wg, bg_ref[...])
        cv = _conv3(uv, wv, bv_ref[...])
        sig = jax.nn.sigmoid(cg)
        dav = da_ref[...]
        dcv = dav * (cg * sig)
        dcg = dav * cv * (sig * (1.0 + cg * (1.0 - sig)))
        for dc, uu, w, du_ref, st_ref in ((dcg, ug, wg, dug_ref, sg_ref), (dcv, uv, wv, duv_ref, sv_ref)):
            du = w[2:3, :] * dc + w[1:2, :] * _shift_up(dc, 1) + w[0:1, :] * _shift_up(dc, 2)
            du_ref[...] = du.astype(BF16)
            st_ref[...] = jnp.zeros_like(st_ref)
            st_ref[0:1, :] = jnp.sum(dc * _shift_down(uu, 2), axis=0, keepdims=True)
            st_ref[1:2, :] = jnp.sum(dc * _shift_down(uu, 1), axis=0, keepdims=True)
            st_ref[2:3, :] = jnp.sum(dc * uu, axis=0, keepdims=True)
            st_ref[3:4, :] = jnp.sum(dc, axis=0, keepdims=True)

    def cols(rows, off):
        return pl.BlockSpec((rows, tn), lambda j: (0, j + off))

    return pl.pallas_call(
        body, name="conv_glu_bwd", grid=(nb,),
        in_specs=[cols(s, 0), cols(s, nb), cols(s, 0), cols(3, 0), cols(3, nb), cols(1, 0), cols(1, nb)],
        out_specs=[cols(s, 0), cols(s, 0), cols(8, 0), cols(8, 0)],
        out_shape=[jax.ShapeDtypeStruct((s, f), BF16), jax.ShapeDtypeStruct((s, f), BF16),
                   jax.ShapeDtypeStruct((8, f), F32), jax.ShapeDtypeStruct((8, f), F32)],
        compiler_params=_params("parallel"),
    )(u, u, da, wc, wc, b, b)


ROW_TILES = (256, 128, 64, 32, 16, 8)
BLOCK_BYTES = 2 << 20


def _add_halves(g, r1, place):
    ns, r, c = g.shape
    rh = r // 2
    tr = _pick(rh, ROW_TILES)
    g4 = g.reshape(ns, 2, rh, c)

    def body(p_ref, g_ref, r_ref, o_ref):
        o_ref[...] = (g_ref[...].astype(F32) + r_ref[...].astype(F32)).astype(o_ref.dtype)

    def slab(s, pr):
        return s + (s >= pr[0]).astype(jnp.int32)

    return pl.pallas_call(
        body, name="add_halves",
        grid_spec=pltpu.PrefetchScalarGridSpec(
            num_scalar_prefetch=1, grid=(ns - 1, rh // tr),
            in_specs=[pl.BlockSpec((None, None, tr, c), lambda s, i, pr: (slab(s, pr), pr[1], i, 0)),
                      pl.BlockSpec((None, tr, c), lambda s, i, pr: (slab(s, pr), i, 0))],
            out_specs=pl.BlockSpec((None, tr, c), lambda s, i, pr: (slab(s, pr), i, 0))),
        out_shape=jax.ShapeDtypeStruct((ns, rh, c), BF16),
        compiler_params=_params("parallel", "parallel"),
    )(place, g4, r1)


def _sum_chips(g, r1, recv, place):
    ns, r, c = g.shape
    rh = r // 2
    tr = _pick(rh, ROW_TILES)
    g4 = g.reshape(ns, 2, rh, c)

    def body(p_ref, g_ref, r_ref, t0_ref, t1_ref, t2_ref, o_ref):
        own = g_ref[...].astype(F32) + r_ref[...].astype(F32)
        o_ref[...] = ((own + t0_ref[...].astype(F32)) + t1_ref[...].astype(F32)) + t2_ref[...].astype(F32)

    def peer(k):
        return pl.BlockSpec((None, tr, c), lambda i, pr: (k, i, 0))

    return pl.pallas_call(
        body, name="sum_chips",
        grid_spec=pltpu.PrefetchScalarGridSpec(
            num_scalar_prefetch=1, grid=(rh // tr,),
            in_specs=[pl.BlockSpec((None, None, tr, c), lambda i, pr: (pr[0], pr[1], i, 0)),
                      pl.BlockSpec((None, tr, c), lambda i, pr: (pr[0], i, 0)), peer(0), peer(1), peer(2)],
            out_specs=pl.BlockSpec((tr, c), lambda i, pr: (pr[1] * (rh // tr) + i, 0))),
        out_shape=jax.ShapeDtypeStruct((r, c), F32),
        compiler_params=_params("parallel"),
    )(place, g4, r1, recv, recv, recv)


def _sum_devices(packs):
    n, r, c = packs.shape

    def body(p_ref, o_ref):
        acc = p_ref[0]
        for d in range(1, n):
            acc = acc + p_ref[d]
        o_ref[...] = acc

    return pl.pallas_call(
        body, name="sum_devices", out_shape=jax.ShapeDtypeStruct((r, c), F32), compiler_params=_params(),
    )(packs)


def _adamw_update(wv, gv, mv, vv):
    c1 = 1.0 - ADAM_B1 ** ADAM_STEP
    c2 = 1.0 - ADAM_B2 ** ADAM_STEP
    mn = ADAM_B1 * mv + (1.0 - ADAM_B1) * gv
    vn = ADAM_B2 * vv + (1.0 - ADAM_B2) * (gv * gv)
    m_hat = mn / c1
    v_hat = vn / c2
    return -ADAM_LR * (m_hat / (jnp.sqrt(v_hat) + ADAM_EPS) + ADAM_WD * wv), mn, vn


def _adamw(w, g, m, v, name, deps=(), emit_grad=False):
    r, c = w.shape
    tr = _pick(r, [t for t in ROW_TILES if t * c * 4 <= BLOCK_BYTES]) if r >= 8 else r
    n_out = 4 if emit_grad else 3

    def body(w_ref, g_ref, m_ref, v_ref, *rest):
        outs = rest[-n_out:]
        gv = g_ref[:, :c]
        if emit_grad:
            outs[0][...] = gv
        outs[-3][...], outs[-2][...], outs[-1][...] = _adamw_update(w_ref[...], gv, m_ref[...], v_ref[...])

    blk = pl.BlockSpec((tr, c), lambda i: (i, 0))
    g_blk = pl.BlockSpec((tr, g.shape[1]), lambda i: (i, 0))
    return pl.pallas_call(
        body, name=name, grid=(r // tr,), in_specs=[blk, g_blk, blk, blk] + [ANY] * len(deps), out_specs=[blk] * n_out,
        out_shape=[jax.ShapeDtypeStruct((r, c), F32)] * n_out, compiler_params=_params("parallel"),
    )(w, g, m, v, *deps)


ANY = pl.BlockSpec(memory_space=pl.ANY)


def _place():
    x, y, c = lax.axis_index("x"), lax.axis_index("y"), lax.axis_index("c")
    chips = [(1 - x, y), (x, 1 - y), (1 - x, 1 - y)]
    return x, y, c, chips


def _remote(src, dst, send_sem, recv_sem, to):
    return pltpu.make_async_remote_copy(src_ref=src, dst_ref=dst, send_sem=send_sem, recv_sem=recv_sem,
                                        device_id=to, device_id_type=MESH)


HBM = pl.BlockSpec(memory_space=pltpu.HBM)
SEM = pl.BlockSpec(memory_space=pltpu.SEMAPHORE)
EFFECT = pltpu.SideEffectType.DATAFLOW_SIDE_EFFECTING


def _in_hbm(a):
    return pltpu.with_memory_space_constraint(a, pltpu.HBM)


def _half(ref_rows, who):
    return pl.ds(who * (ref_rows // 2), ref_rows // 2)


def _gather_start(groups, name):
    items = [it for g in groups for it in g]
    n = len(items)
    sizes = [len(g) for g in groups]

    def body(*refs):
        srcs, lands = refs[:n], refs[n:2 * n]
        sems = refs[2 * n:2 * n + 2 * len(groups)]
        token = refs[-1]
        x, y, c, chips = _place()
        j = 2 * x + y
        at = 0
        for gi, g in enumerate(groups):
            send, recv = sems[2 * gi], sems[2 * gi + 1]
            for i, (shard, split) in enumerate(g):
                src, land = srcs[at], lands[at]
                at += 1
                rows = _half(shard.shape[0], c) if split else slice(None)
                for k, chip in enumerate(chips):
                    _remote(src.at[rows], land.at[j, rows], send.at[4 * i + k], recv.at[4 * i + k], (*chip, c)).start()
                _remote(src, land.at[j], send.at[4 * i + 3], recv.at[4 * i + 3], (x, y, 1 - c)).start()
        token[...] = jnp.zeros_like(token)

    sem_shapes = []
    for sz in sizes:
        sem_shapes += [pltpu.SemaphoreType.DMA((4 * sz,)), pltpu.SemaphoreType.DMA((4 * sz,))]
    out_shape = (sem_shapes + [pltpu.HBM(sh.shape, sh.dtype) for sh, _ in items]
                 + [pltpu.HBM((N_CHIPS,) + sh.shape, sh.dtype) for sh, _ in items]
                 + [jax.ShapeDtypeStruct((8, LANES), F32)])
    ns = len(sem_shapes)
    outs = pl.pallas_call(
        body, name=name, in_specs=[HBM] * (2 * n),
        out_specs=[SEM] * ns + [HBM] * (2 * n) + [pl.BlockSpec(memory_space=pltpu.VMEM)],
        out_shape=out_shape, input_output_aliases={i: ns + i for i in range(2 * n)},
        compiler_params=pltpu.CompilerParams(has_side_effects=EFFECT),
    )(*[_in_hbm(sh) for sh, _ in items], *[_in_hbm(lax.empty((N_CHIPS,) + sh.shape, sh.dtype)) for sh, _ in items])
    sems, shards, lands, token = outs[:ns], outs[ns:ns + n], outs[ns + n:ns + 2 * n], outs[-1]
    res, at = [], 0
    for gi, sz in enumerate(sizes):
        res.append((shards[at:at + sz], lands[at:at + sz], sems[2 * gi], sems[2 * gi + 1]))
        at += sz
    return res, token


def _gather_pass(group, started, after, name):
    shards, lands, send, recv = started
    n = len(group)
    split_ix = [i for i, (_, split) in enumerate(group) if split]

    def body(*refs):
        lnds, send1, recv1 = refs[n:2 * n], refs[2 * n], refs[2 * n + 1]
        outs = refs[2 * n + 2 + len(after):]
        send2, recv2, token = outs[2 * n], outs[2 * n + 1], outs[2 * n + 2]
        x, y, c, chips = _place()
        sib = (x, y, 1 - c)
        for i, (shard, split) in enumerate(group):
            rows = _half(shard.shape[0], c) if split else slice(None)
            for k, (cx, cy) in enumerate(chips):
                landed = lnds[i].at[2 * cx + cy, rows]
                cp = _remote(landed, landed, send1.at[4 * i + k], recv1.at[4 * i + k], sib)
                cp.wait_send()
                cp.wait_recv()
            own = lnds[i].at[2 * x + y]
            cp = _remote(own, own, send1.at[4 * i + 3], recv1.at[4 * i + 3], sib)
            cp.wait_send()
            cp.wait_recv()
        for i2, i in enumerate(split_ix):
            rows = _half(group[i][0].shape[0], c)
            for k, (cx, cy) in enumerate(chips):
                landed = lnds[i].at[2 * cx + cy, rows]
                _remote(landed, landed, send2.at[3 * i2 + k], recv2.at[3 * i2 + k], sib).start()
        token[...] = jnp.zeros_like(token)

    n2 = len(split_ix)
    out_shape = ([pltpu.HBM(a.shape, a.dtype) for a in (*shards, *lands)]
                 + [pltpu.SemaphoreType.DMA((3 * n2,)), pltpu.SemaphoreType.DMA((3 * n2,)), jax.ShapeDtypeStruct((8, LANES), F32)])
    outs = pl.pallas_call(
        body, name=name, in_specs=[HBM] * (2 * n) + [SEM, SEM] + [ANY] * len(after),
        out_specs=[HBM] * (2 * n) + [SEM, SEM, pl.BlockSpec(memory_space=pltpu.VMEM)],
        out_shape=out_shape, input_output_aliases={i: i for i in range(2 * n)},
        compiler_params=pltpu.CompilerParams(has_side_effects=EFFECT),
    )(*shards, *lands, send, recv, *after)
    return outs[:n], (outs[n:2 * n], outs[2 * n], outs[2 * n + 1]), outs[2 * n + 2]


def _gather_wait(group, passed, after, name):
    lands, send2, recv2 = passed
    n = len(group)
    split_ix = [i for i, (_, split) in enumerate(group) if split]

    def body(*refs):
        lnds, s2, r2 = refs[:n], refs[n], refs[n + 1]
        x, y, c, chips = _place()
        sib = (x, y, 1 - c)
        for i2, i in enumerate(split_ix):
            rows = _half(group[i][0].shape[0], 1 - c)
            for k, (cx, cy) in enumerate(chips):
                landed = lnds[i].at[2 * cx + cy, rows]
                cp = _remote(landed, landed, s2.at[3 * i2 + k], r2.at[3 * i2 + k], sib)
                cp.wait_send()
                cp.wait_recv()

    return pl.pallas_call(
        body, name=name, in_specs=[HBM] * n + [SEM, SEM, ANY], out_specs=[HBM] * n,
        out_shape=[pltpu.HBM(a.shape, a.dtype) for a in lands], input_output_aliases={i: i for i in range(n)},
        compiler_params=pltpu.CompilerParams(has_side_effects=EFFECT),
    )(*lands, send2, recv2, after)


def _xfer_start(name, srcs, land_shapes, n_copies, copies, after):
    n, nl = len(srcs), len(land_shapes)

    def body(*refs):
        src_refs, land_refs = refs[:n], refs[n:n + nl]
        send, recv, token = refs[n + nl + 1], refs[n + nl + 2], refs[-1]
        for cp in copies(src_refs, land_refs, send, recv):
            cp.start()
        token[...] = jnp.zeros_like(token)

    lands = [_in_hbm(lax.empty(shape, dtype)) for shape, dtype in land_shapes]
    out_shape = ([pltpu.SemaphoreType.DMA((n_copies,)), pltpu.SemaphoreType.DMA((n_copies,))]
                 + [pltpu.HBM(a.shape, a.dtype) for a in (*srcs, *lands)] + [jax.ShapeDtypeStruct((8, LANES), F32)])
    outs = pl.pallas_call(
        body, name=name, in_specs=[HBM] * (n + nl) + [ANY],
        out_specs=[SEM, SEM] + [HBM] * (n + nl) + [pl.BlockSpec(memory_space=pltpu.VMEM)],
        out_shape=out_shape, input_output_aliases={i: 2 + i for i in range(n + nl)},
        compiler_params=pltpu.CompilerParams(has_side_effects=EFFECT),
    )(*[_in_hbm(a) for a in srcs], *lands, after)
    return (outs[2:2 + n], outs[2 + n:2 + n + nl], outs[0], outs[1]), outs[-1]


def _xfer_wait(name, started, copies, after):
    srcs, lands, send, recv = started
    n, nl = len(srcs), len(lands)

    def body(*refs):
        src_refs, land_refs, s_ref, r_ref = refs[:n], refs[n:n + nl], refs[n + nl], refs[n + nl + 1]
        for cp in copies(src_refs, land_refs, s_ref, r_ref):
            cp.wait_send()
            cp.wait_recv()

    outs = pl.pallas_call(
        body, name=name, in_specs=[HBM] * (n + nl) + [SEM, SEM, ANY], out_specs=[HBM] * (n + nl),
        out_shape=[pltpu.HBM(a.shape, a.dtype) for a in (*srcs, *lands)],
        input_output_aliases={i: i for i in range(n + nl)},
        compiler_params=pltpu.CompilerParams(has_side_effects=EFFECT),
    )(*srcs, *lands, send, recv, after)
    return outs[:n], outs[n:]


def _swap_copies(srcs, lands, send, recv):
    x, y, c, _ = _place()
    return [_remote(src.at[:, _half(src.shape[1], 1 - c)], land, send.at[i], recv.at[i], (x, y, 1 - c))
            for i, (src, land) in enumerate(zip(srcs, lands))]


def _scatter_copies(srcs, lands, send, recv):
    x, y, c, chips = _place()
    return [_remote(src.at[2 * cx + cy], land.at[k], send.at[3 * i + k], recv.at[3 * i + k], (cx, cy, c))
            for i, (src, land) in enumerate(zip(srcs, lands)) for k, (cx, cy) in enumerate(chips)]


def _join_copies(srcs, lands, send, recv):
    x, y, c, _ = _place()
    return [_remote(src.at[_half(src.shape[0], c)], src.at[_half(src.shape[0], c)], send.at[i], recv.at[i], (x, y, 1 - c))
            for i, src in enumerate(srcs)]


def _corner(a):
    return a[(slice(0, 1),) * a.ndim]


class _Reducer:
    def __init__(self, place):
        self.place = place
        self.state = {}

    def swap(self, key, grads, after):
        shapes = [((g.shape[0], g.shape[1] // 2, g.shape[2]), g.dtype) for g in grads]
        self.state[key], token = _xfer_start("swap_start_" + key, grads, shapes, len(grads), _swap_copies, _corner(after))
        return token

    def to_chips(self, key, after):
        grads, from_sibling = _xfer_wait("swap_wait_" + key, self.state[key], _swap_copies, after)
        sums = [_add_halves(g, r, self.place) for g, r in zip(grads, from_sibling)]
        shapes = [((3,) + s.shape[1:], s.dtype) for s in sums]
        started, token = _xfer_start("scatter_start_" + key, sums, shapes, 3 * len(sums), _scatter_copies, _corner(sums[-1]))
        self.state[key] = (grads, from_sibling, started)
        return token

    def to_core(self, key, after):
        grads, from_sibling, started = self.state[key]
        _, from_chips = _xfer_wait("scatter_wait_" + key, started, _scatter_copies, after)
        shards = [_sum_chips(g, r, rc, self.place) for g, r, rc in zip(grads, from_sibling, from_chips)]
        self.state[key], token = _xfer_start("join_start_" + key, shards, [], len(shards), _join_copies, _corner(shards[-1]))
        return token

    def finish(self, key, after):
        return _xfer_wait("join_wait_" + key, self.state.pop(key), _join_copies, after)[0]


def _gather_packs(pack, deps=()):
    def body(p_ref, *rest):
        o_ref, lsem, ssem, rsem = rest[-4:]
        x, y, c, _ = _place()
        me = 4 * x + 2 * y + c
        local = pltpu.make_async_copy(p_ref, o_ref.at[me], lsem)
        local.start()
        cps = []
        for k in range(1, N_DEV):
            fx, fy, fc = (k >> 2) & 1, (k >> 1) & 1, k & 1
            to = (x ^ fx, y ^ fy, c ^ fc)
            cps.append(_remote(p_ref, o_ref.at[me], ssem.at[k - 1], rsem.at[k - 1], to))
        for cp in cps:
            cp.start()
        for k in range(1, N_DEV):
            fx, fy, fc = (k >> 2) & 1, (k >> 1) & 1, k & 1
            src = o_ref.at[4 * (x ^ fx) + 2 * (y ^ fy) + (c ^ fc)]
            _remote(src, src, ssem.at[k - 1], rsem.at[k - 1], (x, y, c)).wait_recv()
        for cp in cps:
            cp.wait_send()
        local.wait()

    return pl.pallas_call(
        body, name="gather_packs", in_specs=[ANY] * (1 + len(deps)), out_specs=ANY,
        out_shape=jax.ShapeDtypeStruct((N_DEV,) + pack.shape, pack.dtype),
        scratch_shapes=[pltpu.SemaphoreType.DMA, pltpu.SemaphoreType.DMA((N_DEV - 1,)), pltpu.SemaphoreType.DMA((N_DEV - 1,))],
    )(pack, *deps)


LANE_TILES = (512, 896, 1408, 704, 384, 256, 128)


def _layer_grads(x, target, small, wg, rest_pass, rest_wait, red, filler):
    s, d = x.shape
    f = wg["conv"].shape[1] // 2
    w_att = N_HEADS * HEAD_DIM
    in_splits = (w_att, w_att, w_att, N_HEADS, w_att, w_att, w_att, d, d)
    in_cols = sum(in_splits)
    cs = in_cols // N_CHIPS
    cp = wg["in"].shape[2]
    tm = min(s, MM_TILE)
    tm_wide = min(s, MM_TILE // 2)
    t_in = cp
    t_up = 2 * f // N_CHIPS
    t_d = _pick(d, LANE_TILES)
    t_d2 = min(d, MM_TILE)
    t_dq = _pick(d // N_CHIPS, LANE_TILES)
    t_fq = _pick(f // N_CHIPS, LANE_TILES)

    h1 = _norm_fwd(x, small["g_attn"], group=d, name="rms1_fwd")
    proj_p = _mm(h1, wg["in"], mode="nn", b_kind="col", tm=tm_wide, tn=t_in, tk=d, name="mm_in")
    gains = {n: small[n].reshape(1, w_att) for n in ("g_q_fox", "g_k_fox", "g_q_dil", "g_k_dil")}
    qa, ka, va_b, fa, qb, kb, vb_b, ga, gb, qa_n, ka_n, qb_n, kb_n = _proj_split(
        proj_p, in_splits, cs, (F32, F32, BF16, F32, F32, F32, BF16, F32, F32),
        {0: gains["g_q_fox"], 1: gains["g_k_fox"], 4: gains["g_q_dil"], 5: gains["g_k_dil"]})
    fa_t = fa.T
    b_f = small["b_forget"].reshape(N_HEADS, 1)
    c_f = _forget_fwd(fa_t, b_f)
    slopes = jnp.asarray(2.0 ** (-8.0 * np.arange(1, N_HEADS + 1) / N_HEADS), dtype=F32)
    a_d = -(slopes[:, None] * jnp.arange(s, dtype=F32)[None, :])
    rows_f, cols_f = c_f[:, :, None], c_f[:, None, :]
    rows_d, cols_d = a_d[:, :, None], a_d[:, None, :]
    o_a, o_a32, lse_a = _attn_fwd(qa_n, ka_n, va_b, rows_f, cols_f, dilated=False, name="attn_fox_fwd")
    token = rest_pass("mid", o_a)
    rows_d = rows_d + token[0, 0]
    o_b, o_b32, lse_b = _attn_fwd(qb_n, kb_n, vb_b, rows_d, cols_d, dilated=True, name="attn_dil_fwd")
    wg = dict(wg, **rest_wait("mid", o_b))
    token = rest_pass("late", o_b)
    pa = _mm(o_a, wg["brf"], mode="nn", b_kind="col", tm=tm, tn=t_dq, tk=w_att, name="mm_brf", deps=(token,))
    pb, merged = _mm(o_b, wg["brd"], mode="nn", b_kind="col", tm=tm, tn=t_dq, tk=w_att, name="mm_brd", gate=(ga, gb, pa))
    x1 = _mm(merged, wg["out"], mode="nn", b_kind="row", res=x, tm=tm, tn=t_d, tk=t_dq, name="mm_out")
    wg = dict(wg, **rest_wait("late", x1))
    h2 = _norm_fwd(x1, small["g_ffn"], group=d, name="rms2_fwd")
    u = _mm(h2, wg["up"], mode="nn", b_kind="col", tm=tm_wide, tn=t_up, tk=d, name="mm_up")
    act = _conv_glu_fwd(u, wg["conv"], wg["bconv"])
    dy_f, dy_b, loss_blk = _mm(act, wg["down"], mode="nn", b_kind="row", res=x1, loss_target=target,
                               tm=tm, tn=t_d2, tk=t_fq, name="mm_down")

    d_act = _mm(dy_b, wg["down"], mode="nt", b_kind="row", tm=tm, tn=t_fq, tk=d, name="mm_down_dx")
    g_down = _mm(act, dy_b, mode="tn", out_dtype=BF16, out_kind="row", tm=t_fq, tn=t_d2, tk=s, name="mm_down_dw")
    tok = red.swap("down", [g_down], g_down)
    du_g, du_v, st_g, st_v = _conv_glu_bwd(u, d_act, wg["conv"] + tok[0, 0], wg["bconv"])
    tok = red.to_chips("down", du_g)
    du = (du_g, du_v)
    g_up = _mm(h2, du, mode="tn", out_dtype=BF16, out_kind="col", tm=t_d2, tn=t_up // 2, tk=s, name="mm_up_dw", deps=(tok,))
    tok = red.swap("up", [g_up], g_up)
    dh2 = _mm(du, wg["up"], mode="nt", b_kind="col", tm=tm, tn=t_d2, tk=t_up, name="mm_up_dx", deps=(tok,))
    tok = red.to_core("down", dh2)
    tok2 = red.to_chips("up", dh2)
    dx1_b, dx1_f, dg_ffn = _norm_bwd(dh2, x1, small["g_ffn"], group=d, res=dy_f, out_dtypes=(BF16, F32), name="rms2_bwd")
    dpa, dpb, dga, dgb = _mm(dx1_b, wg["out"], mode="nt", b_kind="row", tm=tm, tn=t_dq, tk=d, name="mm_out_dx",
                             deps=(tok, tok2), gate=(ga, gb, pa, pb))
    g_out = _mm(merged, dx1_b, mode="tn", out_dtype=BF16, out_kind="row", tm=t_dq, tn=t_d2, tk=s, name="mm_out_dw")
    do_a = _mm(dpa, wg["brf"], mode="nt", b_kind="col", out_dtype=BF16, tm=s, tn=w_att, tk=t_dq, name="mm_brf_dx")
    do_b = _mm(dpb, wg["brd"], mode="nt", b_kind="col", out_dtype=BF16, tm=s, tn=w_att, tk=t_dq, name="mm_brd_dx")
    g_brf = _mm(o_a, dpa, mode="tn", out_dtype=BF16, out_kind="col", tm=w_att, tn=t_dq, tk=s, name="mm_brf_dw")
    g_brd = _mm(o_b, dpb, mode="tn", out_dtype=BF16, out_kind="col", tm=w_att, tn=t_dq, tk=s, name="mm_brd_dw")
    tok = red.swap("mix", [g_out, g_brf, g_brd], g_brd)
    dqa_n, dka_n, dva, dac_a = _attn_bwd(qa_n, ka_n, va_b, o_a32, do_a, lse_a, rows_f + tok[0, 0], cols_f, dilated=False, name="attn_fox_bwd")
    tok = red.to_core("up", dqa_n)
    tok2 = red.to_chips("mix", dqa_n)
    dqb_n, dkb_n, dvb, _ = _attn_bwd(qb_n, kb_n, vb_b, o_b32, do_b, lse_b, rows_d + (tok[0, 0] + tok2[0, 0]), cols_d, dilated=True, name="attn_dil_bwd")
    tok = red.to_core("mix", dqb_n)
    dfa_t, db_f = _forget_bwd(dac_a[:, 0, :], fa_t, b_f)
    dproj_p, dgains = _dproj_merge(
        [dqa_n, dka_n, dva, dfa_t.T, dqb_n, dkb_n, dvb, dga, dgb], in_splits, cs, cp,
        {0: (qa, gains["g_q_fox"]), 1: (ka, gains["g_k_fox"]), 4: (qb, gains["g_q_dil"]), 5: (kb, gains["g_k_dil"])})
    dg_qf, dg_kf, dg_qd, dg_kd = dgains[0], dgains[1], dgains[4], dgains[5]
    g_in = _mm(h1, dproj_p, mode="tn", out_dtype=BF16, out_kind="col", tm=t_d2, tn=t_in, tk=s, name="mm_in_dw", deps=(tok,))
    tok = red.swap("in", [g_in], g_in)
    tok = red.to_chips("in", filler(tok))
    dh1 = _mm(dproj_p, wg["in"], mode="nt", b_kind="col", tm=tm, tn=t_d2, tk=t_in, name="mm_in_dx", deps=(tok,))
    grad_x, dg_attn = _norm_bwd(dh1, x, small["g_attn"], group=d, res=dx1_f, out_dtypes=(F32,), name="rms1_bwd")

    small_grads = {
        "g_attn": dg_attn, "b_forget": db_f.reshape(1, N_HEADS),
        "g_q_fox": dg_qf, "g_k_fox": dg_kf, "g_q_dil": dg_qd, "g_k_dil": dg_kd, "g_ffn": dg_ffn,
        "w_conv": jnp.concatenate([st_g[0:3], st_v[0:3]], axis=1),
        "b_conv": jnp.concatenate([st_g[3:4], st_v[3:4]], axis=1),
        "loss": loss_blk[0:1, 0:1],
    }
    return small_grads, grad_x


SMALL_ORDER = ("g_attn", "b_forget", "g_q_fox", "g_k_fox", "g_q_dil", "g_k_dil", "g_ffn", "w_conv", "b_conv", "loss")
WEIGHT_ORDER = ("g_attn", "w_in", "b_forget", "g_q_fox", "g_k_fox", "g_q_dil", "g_k_dil", "w_br_fox", "w_br_dil",
                "w_out", "g_ffn", "w_up", "w_conv", "b_conv", "w_down")
BIG = {"w_in": "in", "w_br_fox": "brf", "w_br_dil": "brd", "w_out": "out", "w_up": "up", "w_down": "down"}


def kernel(x, g_attn, w_in, b_forget, g_q_fox, g_k_fox, g_q_dil, g_k_dil, w_br_fox, w_br_dil, w_out, g_ffn, w_up, w_conv, b_conv, w_down, loss_target, m_g_attn, m_w_in, m_b_forget, m_g_q_fox, m_g_k_fox, m_g_q_dil, m_g_k_dil, m_w_br_fox, m_w_br_dil, m_w_out, m_g_ffn, m_w_up, m_w_conv, m_b_conv, m_w_down, v_g_attn, v_w_in, v_b_forget, v_g_q_fox, v_g_k_fox, v_g_q_dil, v_g_k_dil, v_w_br_fox, v_w_br_dil, v_w_out, v_g_ffn, v_w_up, v_w_conv, v_b_conv, v_w_down):
    w = dict(g_attn=g_attn, w_in=w_in, b_forget=b_forget, g_q_fox=g_q_fox, g_k_fox=g_k_fox, g_q_dil=g_q_dil,
             g_k_dil=g_k_dil, w_br_fox=w_br_fox, w_br_dil=w_br_dil, w_out=w_out, g_ffn=g_ffn, w_up=w_up,
             w_conv=w_conv, b_conv=b_conv, w_down=w_down)
    m = dict(g_attn=m_g_attn, w_in=m_w_in, b_forget=m_b_forget, g_q_fox=m_g_q_fox, g_k_fox=m_g_k_fox,
             g_q_dil=m_g_q_dil, g_k_dil=m_g_k_dil, w_br_fox=m_w_br_fox, w_br_dil=m_w_br_dil, w_out=m_w_out,
             g_ffn=m_g_ffn, w_up=m_w_up, w_conv=m_w_conv, b_conv=m_b_conv, w_down=m_w_down)
    v = dict(g_attn=v_g_attn, w_in=v_w_in, b_forget=v_b_forget, g_q_fox=v_g_q_fox, g_k_fox=v_g_k_fox,
             g_q_dil=v_g_q_dil, g_k_dil=v_g_k_dil, w_br_fox=v_w_br_fox, w_br_dil=v_w_br_dil, w_out=v_w_out,
             g_ffn=v_g_ffn, w_up=v_w_up, w_conv=v_w_conv, b_conv=v_b_conv, w_down=v_w_down)
    xi, yi, ci = lax.axis_index("x"), lax.axis_index("y"), lax.axis_index("c")
    chip = (2 * xi + yi).astype(jnp.int32)

    cs = w_in.shape[2]
    cp = _round_up(cs, LANES)
    conv_pad = jnp.pad(w_conv[0], ((0, 8 - w_conv.shape[1]), (0, 0)))
    first = [(jnp.pad(w_in[0].astype(BF16), ((0, 0), (0, cp - cs))), True), (conv_pad, False)]
    (started_first,), token = _gather_start([first], "gather_start_in")
    one = 1.0 + token[0, 0]
    shards = {n: (a[0] * one).astype(BF16) for n, a in
              (("brf", w_br_fox), ("brd", w_br_dil), ("out", w_out), ("up", w_up), ("down", w_down))}
    later = {"mid": ("brf", "brd", "out"), "late": ("up", "down")}
    groups = {key: [(shards[n], True) for n in members] for key, members in later.items()}
    started_later, token = _gather_start(list(groups.values()), "gather_start_rest")
    started = dict(zip(later, started_later))
    token, w["w_in"], m["w_in"], v["w_in"] = lax.optimization_barrier((token, w["w_in"], m["w_in"], v["w_in"]))
    w2, m2, v2 = ({n: a[n].reshape(a[n].shape[-2], a[n].shape[-1]) for n in BIG} for a in (w, m, v))
    early = (token, w2["w_in"], m2["w_in"], v2["w_in"])
    own_first, passed_first, token = _gather_pass(first, started_first, early, "gather_pass_in")
    land_in, land_conv = _gather_wait(first, passed_first, token, "gather_wait_in")
    wg = {"in": land_in, "bconv": b_conv,
          "conv": jnp.transpose(land_conv[:, :w_conv.shape[1], :], (1, 0, 2)).reshape(w_conv.shape[1], -1)}
    small = {n: w[n] for n in ("g_attn", "b_forget", "g_q_fox", "g_k_fox", "g_q_dil", "g_k_dil", "g_ffn")}
    small = {n: (a[0] if a.ndim == 3 else a) for n, a in small.items()}
    in_flight = {}

    def rest_pass(key, after):
        own, passed, tok = _gather_pass(groups[key], started[key], (after,), "gather_pass_" + key)
        in_flight[key] = (own, passed)
        return tok

    def rest_wait(key, after):
        own, passed = in_flight.pop(key)
        lands = _gather_wait(groups[key], passed, after, "gather_wait_" + key)
        return dict(zip(later[key], lands))

    reducer = _Reducer(jnp.stack([chip, ci.astype(jnp.int32)]))
    g_out, d_out, m_out, v_out = {}, {}, {}, {}
    reduced = {}

    def first_element(arrays):
        return jnp.stack([a[(0,) * a.ndim] for a in arrays])

    def update_big(n, deps):
        g2, dl, mn, vn = _adamw(w2[n], reduced[BIG[n]], m2[n], v2[n], name="adamw_" + n, deps=deps, emit_grad=True)
        g_out[n], d_out[n], m_out[n], v_out[n] = (a.reshape(w[n].shape) for a in (g2, dl, mn, vn))

    def update_down(tok):
        (reduced["down"],) = reducer.finish("down", tok)
        update_big("w_down", (tok,))
        return v_out["w_down"]

    small_grads, grad_x = _layer_grads(x[0], loss_target[0], small, wg, rest_pass, rest_wait, reducer, update_down)

    for key, members in (("up", ("up",)), ("mix", ("out", "brf", "brd"))):
        reduced.update(zip(members, reducer.finish(key, grad_x)))
    others = ("w_up", "w_out", "w_br_fox", "w_br_dil")
    for n in others:
        update_big(n, (grad_x,))

    flat = jnp.concatenate([small_grads[n].reshape(-1) for n in SMALL_ORDER])
    rows = _round_up(flat.shape[0], 8 * LANES) // LANES
    pack = jnp.pad(flat, (0, rows * LANES - flat.shape[0])).reshape(rows, LANES)
    packs = _gather_packs(pack, deps=(first_element([v_out[n] for n in others]),))
    total = _sum_devices(packs).reshape(-1)
    red, at = {}, 0
    for n in SMALL_ORDER:
        size = small_grads[n].size
        red[n] = total[at:at + size].reshape(small_grads[n].shape)
        at += size
    loss = red["loss"].reshape(())
    c2 = w_conv.shape[2]
    red["w_conv"] = lax.dynamic_slice_in_dim(red["w_conv"], chip * c2, c2, axis=1)

    tok = reducer.to_core("in", packs)
    smalls = [n for n in WEIGHT_ORDER if n not in BIG]
    for n in smalls:
        shape = w[n].shape
        r2 = (shape[-2], shape[-1]) if n not in ("g_attn", "b_forget", "g_ffn", "b_conv") else (1, shape[-1])
        g2 = red[n].reshape(r2)
        dl, mn, vn = _adamw(w[n].reshape(r2), g2, m[n].reshape(r2), v[n].reshape(r2), name="adamw_" + n, deps=(tok,))
        g_out[n], d_out[n], m_out[n], v_out[n] = (a.reshape(shape) for a in (g2, dl, mn, vn))
    (reduced["in"],) = reducer.finish("in", first_element([v_out[n] for n in smalls]))
    update_big("w_in", (tok,))

    return (loss, grad_x[None], *[g_out[n] for n in WEIGHT_ORDER], *[d_out[n] for n in WEIGHT_ORDER],
            *[m_out[n] for n in WEIGHT_ORDER], *[v_out[n] for n in WEIGHT_ORDER])
```
